```python
import jax, jax.numpy as jnp
from jax import lax
import numpy as np

D_MODEL = 1024
BATCH = 8
SEQ = 4096
DEPTH = 4

N_META = 16
EPS = 1e-6
N_BRANCH = 4
POOL_WINDOWS = (2, 4, 8, 16)
POOL_GROUP = 64
POOL_W = POOL_GROUP * 4
MLA_HEADS = 8
QK_NOPE = 64
QK_ROPE = 32
V_DIM = 64
Q_RANK = 256
KV_RANK = 128
ROPE_THETA = 10000.0
MLA_W = MLA_HEADS * V_DIM
Q_BLOCK = 128
CONF_W = 256
CONF_K = 31
SC_W = 256
SC_K = 3

IN_SPLITS = (POOL_W, POOL_W,
             Q_RANK, KV_RANK, QK_ROPE, MLA_W,
             2 * CONF_W, CONF_W,
             3 * SC_W, SC_W,
             N_BRANCH * D_MODEL)
IN_W = sum(IN_SPLITS)

kernel_name = "hybrid_parallel_gated_mixers"


def rms_norm(x, g):
    xf = x.astype(jnp.float32)
    y = xf * lax.rsqrt(jnp.mean(xf * xf, axis=-1, keepdims=True) + EPS)
    return (y * g.astype(jnp.float32)).astype(x.dtype)


def layer_norm(x, g, b):
    xf = x.astype(jnp.float32)
    mu = jnp.mean(xf, axis=-1, keepdims=True)
    var = jnp.mean(jnp.square(xf - mu), axis=-1, keepdims=True)
    y = (xf - mu) * lax.rsqrt(var + EPS)
    return (y * g.astype(jnp.float32) + b.astype(jnp.float32)).astype(x.dtype)


def split_cols(z):
    idx = [int(i) for i in np.cumsum(IN_SPLITS)[:-1]]
    return jnp.split(z, idx, axis=-1)


def causal_dwconv(u, w):
    width, c = w.shape
    up = jnp.pad(u, ((0, 0), (width - 1, 0), (0, 0)))
    return lax.conv_general_dilated(up, w[:, None, :].astype(u.dtype), window_strides=(1,),
                                    padding='VALID', dimension_numbers=('NWC', 'WIO', 'NWC'),
                                    feature_group_count=c)


def rope_tables(n_pos, dim, dtype):
    inv = 1.0 / (ROPE_THETA ** (jnp.arange(0, dim, 2, dtype=jnp.float32) / dim))
    ang = jnp.arange(n_pos, dtype=jnp.float32)[:, None] * inv[None, :]
    return jnp.cos(ang).astype(dtype), jnp.sin(ang).astype(dtype)


def apply_rope(t, cos, sin):
    t1, t2 = jnp.split(t, 2, axis=-1)
    return jnp.concatenate([t1 * cos - t2 * sin, t1 * sin + t2 * cos], axis=-1)


def pool_mixer(v, w_group, scale):
    b_, l_, _ = v.shape
    vf = v.astype(jnp.float32)
    groups = jnp.split(vf, len(POOL_WINDOWS), axis=-1)
    pos = jnp.arange(l_)
    outs = []
    for g, w in zip(groups, POOL_WINDOWS):
        cs = jnp.cumsum(g, axis=1)
        lag = jnp.pad(cs, ((0, 0), (w, 0), (0, 0)))[:, :l_]
        cnt = jnp.minimum(pos + 1, w).astype(jnp.float32)[None, :, None]
        outs.append((cs - lag) / cnt - g)
    p = jnp.stack(outs, axis=2).astype(v.dtype)
    y = jnp.einsum('blgc,gcd->blgd', p, w_group).reshape(b_, l_, POOL_W)
    return y * scale


def mla_attention(c_q, c_kv, k_rope, q_norm_g, w_uq, kv_norm_g, w_ukv, cos, sin):
    b_, l_, _ = c_q.shape
    q = (rms_norm(c_q, q_norm_g) @ w_uq).reshape(b_, l_, MLA_HEADS, QK_NOPE + QK_ROPE)
    q_nope, q_rope = jnp.split(q, [QK_NOPE], axis=-1)
    q_rope = apply_rope(q_rope, cos[:, None, :], sin[:, None, :])
    kv = (rms_norm(c_kv, kv_norm_g) @ w_ukv).reshape(b_, l_, MLA_HEADS, QK_NOPE + V_DIM)
    k_nope, v = jnp.split(kv, [QK_NOPE], axis=-1)
    k_rope = apply_rope(k_rope, cos, sin)
    k = jnp.concatenate([k_nope, jnp.broadcast_to(k_rope[:, :, None, :], (b_, l_, MLA_HEADS, QK_ROPE))], axis=-1)
    qf = jnp.concatenate([q_nope, q_rope], axis=-1) * ((QK_NOPE + QK_ROPE) ** -0.5)
    n_blk = -(-l_ // Q_BLOCK)
    lp = n_blk * Q_BLOCK
    pad = ((0, 0), (0, lp - l_), (0, 0), (0, 0))
    qf, k, v = jnp.pad(qf, pad), jnp.pad(k, pad), jnp.pad(v, pad)
    k_pos = jnp.arange(lp)
    q_blocks = qf.reshape(b_, n_blk, Q_BLOCK, MLA_HEADS, QK_NOPE + QK_ROPE).transpose(1, 0, 2, 3, 4)

    def attend(args):
        qb, i = args
        s = jnp.einsum('bqhd,bkhd->bhqk', qb, k).astype(jnp.float32)
        q_pos = i * Q_BLOCK + jnp.arange(Q_BLOCK)
        s = jnp.where(k_pos[None, :] <= q_pos[:, None], s, -jnp.inf)
        p = jax.nn.softmax(s, axis=-1).astype(v.dtype)
        return jnp.einsum('bhqk,bkhd->bqhd', p, v)

    o = lax.map(attend, (q_blocks, jnp.arange(n_blk)))
    return o.transpose(1, 0, 2, 3, 4).reshape(b_, lp, MLA_W)[:, :l_]


def conformer_conv(u, w_dw, b_dw, ln_g, ln_b):
    a, gate = jnp.split(u, 2, axis=-1)
    z = a * jax.nn.sigmoid(gate)
    z = causal_dwconv(z, w_dw) + b_dw
    z = layer_norm(z, ln_g, ln_b)
    return jax.nn.silu(z)


def short_conv(bcx, w_dw):
    bg, cg, xv = jnp.split(bcx, 3, axis=-1)
    return bg * causal_dwconv(cg * xv, w_dw)


def _fwd_setup_inputs(seed: int = 0) -> dict:
    key = jax.random.key(seed)
    ks = jax.random.split(key, 24)
    f32 = jnp.float32

    def nrm(k, shape, fan_in):
        return jax.random.normal(k, shape, f32) * (fan_in ** -0.5)

    def gain(k, shape):
        return 1.0 + 0.05 * jax.random.normal(k, shape, f32)

    def bias(k, shape):
        return 0.02 * jax.random.normal(k, shape, f32)

    return {
        "x": jax.random.normal(ks[0], (BATCH, SEQ, D_MODEL), f32),
        "meta_tokens": jax.random.normal(ks[1], (N_META, D_MODEL), f32),
        "pre_norm_g": gain(ks[2], (DEPTH, D_MODEL)),
        "w_in": nrm(ks[3], (DEPTH, D_MODEL, IN_W), D_MODEL),
        "gate_bias": bias(ks[4], (DEPTH, N_BRANCH * D_MODEL)),
        "pool_w": nrm(ks[5], (DEPTH, 4, POOL_GROUP, POOL_GROUP), POOL_GROUP),
        "pool_scale": gain(ks[6], (DEPTH, POOL_W)),
        "w_out_pool": nrm(ks[7], (DEPTH, POOL_W, D_MODEL), POOL_W),
        "q_norm_g": gain(ks[8], (DEPTH, Q_RANK)),
        "w_uq": nrm(ks[9], (DEPTH, Q_RANK, MLA_HEADS * (QK_NOPE + QK_ROPE)), Q_RANK),
        "kv_norm_g": gain(ks[10], (DEPTH, KV_RANK)),
        "w_ukv": nrm(ks[11], (DEPTH, KV_RANK, MLA_HEADS * (QK_NOPE + V_DIM)), KV_RANK),
        "w_out_mla": nrm(ks[12], (DEPTH, MLA_W, D_MODEL), MLA_W),
        "conf_dw_w": nrm(ks[13], (DEPTH, CONF_K, CONF_W), CONF_K),
        "conf_dw_b": bias(ks[14], (DEPTH, CONF_W)),
        "conf_ln_g": gain(ks[15], (DEPTH, CONF_W)),
        "conf_ln_b": bias(ks[16], (DEPTH, CONF_W)),
        "w_out_conf": nrm(ks[17], (DEPTH, CONF_W, D_MODEL), CONF_W),
        "sc_dw_w": nrm(ks[18], (DEPTH, SC_K, SC_W), SC_K),
        "w_out_sc": nrm(ks[19], (DEPTH, SC_W, D_MODEL), SC_W),
        "w_o": nrm(ks[20], (DEPTH, D_MODEL, D_MODEL), D_MODEL),
        "post_norm_g": gain(ks[21], (DEPTH, D_MODEL)),
    }


def _fwd_reference(x, meta_tokens, pre_norm_g, w_in, gate_bias, pool_w, pool_scale, w_out_pool,
              q_norm_g, w_uq, kv_norm_g, w_ukv, w_out_mla, conf_dw_w, conf_dw_b, conf_ln_g,
              conf_ln_b, w_out_conf, sc_dw_w, w_out_sc, w_o, post_norm_g):
    b_ = x.shape[0]
    meta = jnp.broadcast_to(meta_tokens[None].astype(x.dtype), (b_, N_META, D_MODEL))
    h_res = jnp.concatenate([meta, x], axis=1)
    l_ = h_res.shape[1]
    cos, sin = rope_tables(l_, QK_ROPE, x.dtype)

    for i in range(DEPTH):
        h = rms_norm(h_res, pre_norm_g[i])
        z = h @ w_in[i]
        (pv, pg, cq, ckv, kr, mg, cu, cg, sbcx, sg, gl) = split_cols(z)

        y_a = (pool_mixer(pv, pool_w[i], pool_scale[i]) * jax.nn.silu(pg)) @ w_out_pool[i]
        y_b = (mla_attention(cq, ckv, kr, q_norm_g[i], w_uq[i], kv_norm_g[i], w_ukv[i], cos, sin)
               * jax.nn.silu(mg)) @ w_out_mla[i]
        y_c = (conformer_conv(cu, conf_dw_w[i], conf_dw_b[i], conf_ln_g[i], conf_ln_b[i])
               * jax.nn.silu(cg)) @ w_out_conf[i]
        y_d = (short_conv(sbcx, sc_dw_w[i]) * jax.nn.silu(sg)) @ w_out_sc[i]

        gts = jax.nn.sigmoid(gl + gate_bias[i]).reshape(b_, l_, N_BRANCH, D_MODEL)
        m = (gts[:, :, 0] * y_a + gts[:, :, 1] * y_b + gts[:, :, 2] * y_c + gts[:, :, 3] * y_d)
        h_res = h_res + rms_norm(m @ w_o[i], post_norm_g[i])

    return h_res[:, N_META:]


import jax as _jax
import jax.numpy as _jnp

TWIN_FORMAT = 'train_step'
FWD_PARAMS = ['x', 'meta_tokens', 'pre_norm_g', 'w_in', 'gate_bias', 'pool_w', 'pool_scale', 'w_out_pool', 'q_norm_g', 'w_uq', 'kv_norm_g', 'w_ukv', 'w_out_mla', 'conf_dw_w', 'conf_dw_b', 'conf_ln_g', 'conf_ln_b', 'w_out_conf', 'sc_dw_w', 'w_out_sc', 'w_o', 'post_norm_g']
TWIN_WEIGHTS = ['meta_tokens', 'pre_norm_g', 'w_in', 'gate_bias', 'pool_w', 'pool_scale', 'w_out_pool', 'q_norm_g', 'w_uq', 'kv_norm_g', 'w_ukv', 'w_out_mla', 'conf_dw_w', 'conf_dw_b', 'conf_ln_g', 'conf_ln_b', 'w_out_conf', 'sc_dw_w', 'w_out_sc', 'w_o', 'post_norm_g']
TWIN_DIFF_INPUT = 'x'
TWIN_INPUTS = ['x', 'meta_tokens', 'pre_norm_g', 'w_in', 'gate_bias', 'pool_w', 'pool_scale', 'w_out_pool', 'q_norm_g', 'w_uq', 'kv_norm_g', 'w_ukv', 'w_out_mla', 'conf_dw_w', 'conf_dw_b', 'conf_ln_g', 'conf_ln_b', 'w_out_conf', 'sc_dw_w', 'w_out_sc', 'w_o', 'post_norm_g', 'loss_target', 'm_meta_tokens', 'm_pre_norm_g', 'm_w_in', 'm_gate_bias', 'm_pool_w', 'm_pool_scale', 'm_w_out_pool', 'm_q_norm_g', 'm_w_uq', 'm_kv_norm_g', 'm_w_ukv', 'm_w_out_mla', 'm_conf_dw_w', 'm_conf_dw_b', 'm_conf_ln_g', 'm_conf_ln_b', 'm_w_out_conf', 'm_sc_dw_w', 'm_w_out_sc', 'm_w_o', 'm_post_norm_g', 'v_meta_tokens', 'v_pre_norm_g', 'v_w_in', 'v_gate_bias', 'v_pool_w', 'v_pool_scale', 'v_w_out_pool', 'v_q_norm_g', 'v_w_uq', 'v_kv_norm_g', 'v_w_ukv', 'v_w_out_mla', 'v_conf_dw_w', 'v_conf_dw_b', 'v_conf_ln_g', 'v_conf_ln_b', 'v_w_out_conf', 'v_sc_dw_w', 'v_w_out_sc', 'v_w_o', 'v_post_norm_g']
TWIN_OUTPUTS = ['loss', 'grad_x', 'grad_meta_tokens', 'grad_pre_norm_g', 'grad_w_in', 'grad_gate_bias', 'grad_pool_w', 'grad_pool_scale', 'grad_w_out_pool', 'grad_q_norm_g', 'grad_w_uq', 'grad_kv_norm_g', 'grad_w_ukv', 'grad_w_out_mla', 'grad_conf_dw_w', 'grad_conf_dw_b', 'grad_conf_ln_g', 'grad_conf_ln_b', 'grad_w_out_conf', 'grad_sc_dw_w', 'grad_w_out_sc', 'grad_w_o', 'grad_post_norm_g', 'delta_meta_tokens', 'delta_pre_norm_g', 'delta_w_in', 'delta_gate_bias', 'delta_pool_w', 'delta_pool_scale', 'delta_w_out_pool', 'delta_q_norm_g', 'delta_w_uq', 'delta_kv_norm_g', 'delta_w_ukv', 'delta_w_out_mla', 'delta_conf_dw_w', 'delta_conf_dw_b', 'delta_conf_ln_g', 'delta_conf_ln_b', 'delta_w_out_conf', 'delta_sc_dw_w', 'delta_w_out_sc', 'delta_w_o', 'delta_post_norm_g', 'new_m_meta_tokens', 'new_m_pre_norm_g', 'new_m_w_in', 'new_m_gate_bias', 'new_m_pool_w', 'new_m_pool_scale', 'new_m_w_out_pool', 'new_m_q_norm_g', 'new_m_w_uq', 'new_m_kv_norm_g', 'new_m_w_ukv', 'new_m_w_out_mla', 'new_m_conf_dw_w', 'new_m_conf_dw_b', 'new_m_conf_ln_g', 'new_m_conf_ln_b', 'new_m_w_out_conf', 'new_m_sc_dw_w', 'new_m_w_out_sc', 'new_m_w_o', 'new_m_post_norm_g', 'new_v_meta_tokens', 'new_v_pre_norm_g', 'new_v_w_in', 'new_v_gate_bias', 'new_v_pool_w', 'new_v_pool_scale', 'new_v_w_out_pool', 'new_v_q_norm_g', 'new_v_w_uq', 'new_v_kv_norm_g', 'new_v_w_ukv', 'new_v_w_out_mla', 'new_v_conf_dw_w', 'new_v_conf_dw_b', 'new_v_conf_ln_g', 'new_v_conf_ln_b', 'new_v_w_out_conf', 'new_v_sc_dw_w', 'new_v_w_out_sc', 'new_v_w_o', 'new_v_post_norm_g']
TWIN_LEAF_KINDS = {'loss': 'loss', 'grad_x': 'grad_x', 'grad_meta_tokens': 'grad_w', 'grad_pre_norm_g': 'grad_w', 'grad_w_in': 'grad_w', 'grad_gate_bias': 'grad_w', 'grad_pool_w': 'grad_w', 'grad_pool_scale': 'grad_w', 'grad_w_out_pool': 'grad_w', 'grad_q_norm_g': 'grad_w', 'grad_w_uq': 'grad_w', 'grad_kv_norm_g': 'grad_w', 'grad_w_ukv': 'grad_w', 'grad_w_out_mla': 'grad_w', 'grad_conf_dw_w': 'grad_w', 'grad_conf_dw_b': 'grad_w', 'grad_conf_ln_g': 'grad_w', 'grad_conf_ln_b': 'grad_w', 'grad_w_out_conf': 'grad_w', 'grad_sc_dw_w': 'grad_w', 'grad_w_out_sc': 'grad_w', 'grad_w_o': 'grad_w', 'grad_post_norm_g': 'grad_w', 'delta_meta_tokens': 'delta_w', 'delta_pre_norm_g': 'delta_w', 'delta_w_in': 'delta_w', 'delta_gate_bias': 'delta_w', 'delta_pool_w': 'delta_w', 'delta_pool_scale': 'delta_w', 'delta_w_out_pool': 'delta_w', 'delta_q_norm_g': 'delta_w', 'delta_w_uq': 'delta_w', 'delta_kv_norm_g': 'delta_w', 'delta_w_ukv': 'delta_w', 'delta_w_out_mla': 'delta_w', 'delta_conf_dw_w': 'delta_w', 'delta_conf_dw_b': 'delta_w', 'delta_conf_ln_g': 'delta_w', 'delta_conf_ln_b': 'delta_w', 'delta_w_out_conf': 'delta_w', 'delta_sc_dw_w': 'delta_w', 'delta_w_out_sc': 'delta_w', 'delta_w_o': 'delta_w', 'delta_post_norm_g': 'delta_w', 'new_m_meta_tokens': 'new_m', 'new_m_pre_norm_g': 'new_m', 'new_m_w_in': 'new_m', 'new_m_gate_bias': 'new_m', 'new_m_pool_w': 'new_m', 'new_m_pool_scale': 'new_m', 'new_m_w_out_pool': 'new_m', 'new_m_q_norm_g': 'new_m', 'new_m_w_uq': 'new_m', 'new_m_kv_norm_g': 'new_m', 'new_m_w_ukv': 'new_m', 'new_m_w_out_mla': 'new_m', 'new_m_conf_dw_w': 'new_m', 'new_m_conf_dw_b': 'new_m', 'new_m_conf_ln_g': 'new_m', 'new_m_conf_ln_b': 'new_m', 'new_m_w_out_conf': 'new_m', 'new_m_sc_dw_w': 'new_m', 'new_m_w_out_sc': 'new_m', 'new_m_w_o': 'new_m', 'new_m_post_norm_g': 'new_m', 'new_v_meta_tokens': 'new_v', 'new_v_pre_norm_g': 'new_v', 'new_v_w_in': 'new_v', 'new_v_gate_bias': 'new_v', 'new_v_pool_w': 'new_v', 'new_v_pool_scale': 'new_v', 'new_v_w_out_pool': 'new_v', 'new_v_q_norm_g': 'new_v', 'new_v_w_uq': 'new_v', 'new_v_kv_norm_g': 'new_v', 'new_v_w_ukv': 'new_v', 'new_v_w_out_mla': 'new_v', 'new_v_conf_dw_w': 'new_v', 'new_v_conf_dw_b': 'new_v', 'new_v_conf_ln_g': 'new_v', 'new_v_conf_ln_b': 'new_v', 'new_v_w_out_conf': 'new_v', 'new_v_sc_dw_w': 'new_v', 'new_v_w_out_sc': 'new_v', 'new_v_w_o': 'new_v', 'new_v_post_norm_g': 'new_v'}


def _forward(args):
    return _fwd_reference(*[args[k] for k in FWD_PARAMS])


def _output_shape():
    out = _jax.eval_shape(lambda: _forward(_fwd_setup_inputs(0)))
    return out.shape, out.dtype

N_MICROBATCH = 1
ADAM_LR = 0.001
ADAM_B1 = 0.9
ADAM_B2 = 0.999
ADAM_EPS = 1e-08
ADAM_WD = 0.01
ADAM_STEP = 10
PER_EXAMPLE_BATCH_AXIS = {'x': 0, 'loss_target': 0}
SHARED_INPUTS = []
_WEIGHT_DTYPES = {'meta_tokens': _jnp.float32, 'pre_norm_g': _jnp.float32, 'w_in': _jnp.float32, 'gate_bias': _jnp.float32, 'pool_w': _jnp.float32, 'pool_scale': _jnp.float32, 'w_out_pool': _jnp.float32, 'q_norm_g': _jnp.float32, 'w_uq': _jnp.float32, 'kv_norm_g': _jnp.float32, 'w_ukv': _jnp.float32, 'w_out_mla': _jnp.float32, 'conf_dw_w': _jnp.float32, 'conf_dw_b': _jnp.float32, 'conf_ln_g': _jnp.float32, 'conf_ln_b': _jnp.float32, 'w_out_conf': _jnp.float32, 'sc_dw_w': _jnp.float32, 'w_out_sc': _jnp.float32, 'w_o': _jnp.float32, 'post_norm_g': _jnp.float32}
MOMENT_SCALE = {'meta_tokens': 8.830403e-02, 'pre_norm_g': 1.499461e+00, 'w_in': 5.506707e-01, 'gate_bias': 1.652718e-01, 'pool_w': 1.083670e+00, 'pool_scale': 1.114274e+00, 'w_out_pool': 5.438766e-01, 'q_norm_g': 2.014198e-01, 'w_uq': 1.155926e-01, 'kv_norm_g': 4.195412e-01, 'w_ukv': 1.394296e-01, 'w_out_mla': 1.105036e-01, 'conf_dw_w': 7.153837e-01, 'conf_dw_b': 1.935349e+00, 'conf_ln_g': 9.756043e-01, 'conf_ln_b': 1.049958e+00, 'w_out_conf': 3.768724e-01, 'sc_dw_w': 1.106790e+00, 'w_out_sc': 5.367885e-01, 'w_o': 8.595305e-01, 'post_norm_g': 3.185873e+01}


def _to_microbatches(a, axis):
    t = _jnp.moveaxis(a, axis, 0)
    t = t.reshape((N_MICROBATCH, t.shape[0] // N_MICROBATCH) + t.shape[1:])
    return _jnp.moveaxis(t, 1, axis + 1)


def setup_inputs(seed: int = 0) -> dict:
    inp = _fwd_setup_inputs(seed)
    key = _jax.random.fold_in(_jax.random.key(seed), 7919)
    shape, _ = _output_shape()
    out = dict(inp)
    out["loss_target"] = _jax.random.normal(_jax.random.fold_in(key, 0), shape, _jnp.float32)
    for i, name in enumerate(TWIN_WEIGHTS):
        w = inp[name].astype(_jnp.float32)
        if MOMENT_SCALE is None:
            s = _jnp.sqrt(_jnp.mean(_jnp.square(w)) + 1e-30)
        else:
            s = MOMENT_SCALE[name]
        km, kv = _jax.random.split(_jax.random.fold_in(key, i + 1))
        out[name] = w
        out["m_" + name] = s * _jax.random.normal(km, w.shape, _jnp.float32)
        out["v_" + name] = (s * s) * _jax.random.uniform(kv, w.shape, _jnp.float32, 0.5, 1.5)
    if N_MICROBATCH > 1:
        for name, axis in PER_EXAMPLE_BATCH_AXIS.items():
            out[name] = _to_microbatches(out[name], axis)
    return {'x': out['x'], 'meta_tokens': out['meta_tokens'], 'pre_norm_g': out['pre_norm_g'], 'w_in': out['w_in'], 'gate_bias': out['gate_bias'], 'pool_w': out['pool_w'], 'pool_scale': out['pool_scale'], 'w_out_pool': out['w_out_pool'], 'q_norm_g': out['q_norm_g'], 'w_uq': out['w_uq'], 'kv_norm_g': out['kv_norm_g'], 'w_ukv': out['w_ukv'], 'w_out_mla': out['w_out_mla'], 'conf_dw_w': out['conf_dw_w'], 'conf_dw_b': out['conf_dw_b'], 'conf_ln_g': out['conf_ln_g'], 'conf_ln_b': out['conf_ln_b'], 'w_out_conf': out['w_out_conf'], 'sc_dw_w': out['sc_dw_w'], 'w_out_sc': out['w_out_sc'], 'w_o': out['w_o'], 'post_norm_g': out['post_norm_g'], 'loss_target': out['loss_target'], 'm_meta_tokens': out['m_meta_tokens'], 'm_pre_norm_g': out['m_pre_norm_g'], 'm_w_in': out['m_w_in'], 'm_gate_bias': out['m_gate_bias'], 'm_pool_w': out['m_pool_w'], 'm_pool_scale': out['m_pool_scale'], 'm_w_out_pool': out['m_w_out_pool'], 'm_q_norm_g': out['m_q_norm_g'], 'm_w_uq': out['m_w_uq'], 'm_kv_norm_g': out['m_kv_norm_g'], 'm_w_ukv': out['m_w_ukv'], 'm_w_out_mla': out['m_w_out_mla'], 'm_conf_dw_w': out['m_conf_dw_w'], 'm_conf_dw_b': out['m_conf_dw_b'], 'm_conf_ln_g': out['m_conf_ln_g'], 'm_conf_ln_b': out['m_conf_ln_b'], 'm_w_out_conf': out['m_w_out_conf'], 'm_sc_dw_w': out['m_sc_dw_w'], 'm_w_out_sc': out['m_w_out_sc'], 'm_w_o': out['m_w_o'], 'm_post_norm_g': out['m_post_norm_g'], 'v_meta_tokens': out['v_meta_tokens'], 'v_pre_norm_g': out['v_pre_norm_g'], 'v_w_in': out['v_w_in'], 'v_gate_bias': out['v_gate_bias'], 'v_pool_w': out['v_pool_w'], 'v_pool_scale': out['v_pool_scale'], 'v_w_out_pool': out['v_w_out_pool'], 'v_q_norm_g': out['v_q_norm_g'], 'v_w_uq': out['v_w_uq'], 'v_kv_norm_g': out['v_kv_norm_g'], 'v_w_ukv': out['v_w_ukv'], 'v_w_out_mla': out['v_w_out_mla'], 'v_conf_dw_w': out['v_conf_dw_w'], 'v_conf_dw_b': out['v_conf_dw_b'], 'v_conf_ln_g': out['v_conf_ln_g'], 'v_conf_ln_b': out['v_conf_ln_b'], 'v_w_out_conf': out['v_w_out_conf'], 'v_sc_dw_w': out['v_sc_dw_w'], 'v_w_out_sc': out['v_w_out_sc'], 'v_w_o': out['v_w_o'], 'v_post_norm_g': out['v_post_norm_g']}


def _loss(weights, diff, rest, loss_target):
    with _jax.named_scope("forward"):
        args = {**rest, TWIN_DIFF_INPUT: diff, **{k: w.astype(_WEIGHT_DTYPES[k]) for k, w in weights.items()}}
        y = _forward(args)
    with _jax.named_scope("loss_head"):
        err = _jnp.square(y.astype(_jnp.float32) - loss_target)
        return 0.5 * _jnp.sum(_jnp.mean(err, axis=-1)) if err.ndim else 0.5 * err


def _adamw(w, g, m, v):
    m = ADAM_B1 * m + (1.0 - ADAM_B1) * g
    v = ADAM_B2 * v + (1.0 - ADAM_B2) * _jnp.square(g)
    m_hat = m / (1.0 - ADAM_B1 ** ADAM_STEP)
    v_hat = v / (1.0 - ADAM_B2 ** ADAM_STEP)
    delta = -ADAM_LR * (m_hat / (_jnp.sqrt(v_hat) + ADAM_EPS) + ADAM_WD * w)
    return delta, m, v


def reference(x, meta_tokens, pre_norm_g, w_in, gate_bias, pool_w, pool_scale, w_out_pool, q_norm_g, w_uq, kv_norm_g, w_ukv, w_out_mla, conf_dw_w, conf_dw_b, conf_ln_g, conf_ln_b, w_out_conf, sc_dw_w, w_out_sc, w_o, post_norm_g, loss_target, m_meta_tokens, m_pre_norm_g, m_w_in, m_gate_bias, m_pool_w, m_pool_scale, m_w_out_pool, m_q_norm_g, m_w_uq, m_kv_norm_g, m_w_ukv, m_w_out_mla, m_conf_dw_w, m_conf_dw_b, m_conf_ln_g, m_conf_ln_b, m_w_out_conf, m_sc_dw_w, m_w_out_sc, m_w_o, m_post_norm_g, v_meta_tokens, v_pre_norm_g, v_w_in, v_gate_bias, v_pool_w, v_pool_scale, v_w_out_pool, v_q_norm_g, v_w_uq, v_kv_norm_g, v_w_ukv, v_w_out_mla, v_conf_dw_w, v_conf_dw_b, v_conf_ln_g, v_conf_ln_b, v_w_out_conf, v_sc_dw_w, v_w_out_sc, v_w_o, v_post_norm_g):
    given = dict(x=x, meta_tokens=meta_tokens, pre_norm_g=pre_norm_g, w_in=w_in, gate_bias=gate_bias, pool_w=pool_w, pool_scale=pool_scale, w_out_pool=w_out_pool, q_norm_g=q_norm_g, w_uq=w_uq, kv_norm_g=kv_norm_g, w_ukv=w_ukv, w_out_mla=w_out_mla, conf_dw_w=conf_dw_w, conf_dw_b=conf_dw_b, conf_ln_g=conf_ln_g, conf_ln_b=conf_ln_b, w_out_conf=w_out_conf, sc_dw_w=sc_dw_w, w_out_sc=w_out_sc, w_o=w_o, post_norm_g=post_norm_g, loss_target=loss_target, m_meta_tokens=m_meta_tokens, m_pre_norm_g=m_pre_norm_g, m_w_in=m_w_in, m_gate_bias=m_gate_bias, m_pool_w=m_pool_w, m_pool_scale=m_pool_scale, m_w_out_pool=m_w_out_pool, m_q_norm_g=m_q_norm_g, m_w_uq=m_w_uq, m_kv_norm_g=m_kv_norm_g, m_w_ukv=m_w_ukv, m_w_out_mla=m_w_out_mla, m_conf_dw_w=m_conf_dw_w, m_conf_dw_b=m_conf_dw_b, m_conf_ln_g=m_conf_ln_g, m_conf_ln_b=m_conf_ln_b, m_w_out_conf=m_w_out_conf, m_sc_dw_w=m_sc_dw_w, m_w_out_sc=m_w_out_sc, m_w_o=m_w_o, m_post_norm_g=m_post_norm_g, v_meta_tokens=v_meta_tokens, v_pre_norm_g=v_pre_norm_g, v_w_in=v_w_in, v_gate_bias=v_gate_bias, v_pool_w=v_pool_w, v_pool_scale=v_pool_scale, v_w_out_pool=v_w_out_pool, v_q_norm_g=v_q_norm_g, v_w_uq=v_w_uq, v_kv_norm_g=v_kv_norm_g, v_w_ukv=v_w_ukv, v_w_out_mla=v_w_out_mla, v_conf_dw_w=v_conf_dw_w, v_conf_dw_b=v_conf_dw_b, v_conf_ln_g=v_conf_ln_g, v_conf_ln_b=v_conf_ln_b, v_w_out_conf=v_w_out_conf, v_sc_dw_w=v_sc_dw_w, v_w_out_sc=v_w_out_sc, v_w_o=v_w_o, v_post_norm_g=v_post_norm_g)
    weights = {n: given[n] for n in TWIN_WEIGHTS}
    shared = {n: given[n] for n in SHARED_INPUTS}
    per_example = {n: given[n] for n in ['x']}
    grad_fn = _jax.value_and_grad(_loss, argnums=(0, 1))

    def one_microbatch(ex, loss_target):
        ex = dict(ex)
        diff = ex.pop(TWIN_DIFF_INPUT)
        return grad_fn(weights, diff, {**shared, **ex}, loss_target)

    if N_MICROBATCH == 1:
        loss, (grad_w, grad_x) = one_microbatch(per_example, given["loss_target"])
    else:
        def body(carry, xs):
            loss_sum, grad_sum = carry
            l_k, (gw_k, gx_k) = one_microbatch(xs[0], xs[1])
            with _jax.named_scope("update"):
                return (loss_sum + l_k, _jax.tree.map(_jnp.add, grad_sum, gw_k)), gx_k

        init = (_jnp.zeros((), _jnp.float32), _jax.tree.map(_jnp.zeros_like, weights))
        (loss, grad_w), grad_x = _jax.lax.scan(body, init, (per_example, given["loss_target"]))
    with _jax.named_scope("update"):
        delta_w, new_m, new_v = {}, {}, {}
        for n in TWIN_WEIGHTS:
            delta_w[n], new_m[n], new_v[n] = _adamw(weights[n], grad_w[n], given["m_" + n], given["v_" + n])
    return (loss, grad_x, *[grad_w[n] for n in TWIN_WEIGHTS], *[delta_w[n] for n in TWIN_WEIGHTS],
            *[new_m[n] for n in TWIN_WEIGHTS], *[new_v[n] for n in TWIN_WEIGHTS])
```

```python
import functools
import math

import jax
import jax.numpy as jnp
from jax import lax
from jax.experimental import pallas as pl
from jax.experimental.pallas import tpu as pltpu

F32 = jnp.float32
BF16 = jnp.bfloat16

D_MODEL = 1024
DEPTH = 4
N_META = 16
EPS = 1e-6
HEADS = 8
QK_NOPE = 64
QK_ROPE = 32
V_DIM = 64
HEAD_PAD = 128
ROPE_THETA = 10000.0
Q_SCALE = (QK_NOPE + QK_ROPE) ** -0.5
CONF_K = 31
SC_K = 3
IN_W = 7328
N_CHIPS = 4

ZB = 3328
ZG = 4096
PV, PG, CQ, CKV, KR, MG, CA, CGT, CG, BG, C2, XV, SG = (0, 256, 512, 768, 896, 1024, 1536, 1792, 2048, 2304, 2560, 2816, 3072)

ROW_TILE = 384
HALO = 32
LANES = 128
COMM_ROWS = 2048
VMEM_LIMIT = 56 * 1024 * 1024

ADAM_LR = 0.001
ADAM_B1 = 0.9
ADAM_B2 = 0.999
ADAM_EPS = 1e-08
ADAM_WD = 0.01
ADAM_STEP = 10

MESH = pl.DeviceIdType.MESH

SHARDED = (
    ("w_in", (DEPTH, D_MODEL, IN_W // N_CHIPS), 2),
    ("w_out_pool", (DEPTH, 256, 256), 2),
    ("w_uq", (DEPTH, 256, 192), 2),
    ("w_ukv", (DEPTH, 128, 256), 2),
    ("w_out_mla", (DEPTH, 512, 256), 2),
    ("w_out_conf", (DEPTH, 256, 256), 2),
    ("w_out_sc", (DEPTH, 256, 256), 2),
    ("w_o", (DEPTH, 256, D_MODEL), 1),
)
SHARDED_SMALL = (
    ("meta_tokens", (N_META, 256), 1),
    ("conf_dw_w", (DEPTH, CONF_K, 64), 2),
    ("sc_dw_w", (DEPTH, SC_K, 64), 2),
)
REPLICATED = (
    ("pre_norm_g", (DEPTH, D_MODEL)),
    ("gate_bias", (DEPTH, 4 * D_MODEL)),
    ("pool_w", (DEPTH, 4, 64, 64)),
    ("pool_scale", (DEPTH, 256)),
    ("q_norm_g", (DEPTH, 256)),
    ("kv_norm_g", (DEPTH, 128)),
    ("conf_dw_b", (DEPTH, 256)),
    ("conf_ln_g", (DEPTH, 256)),
    ("conf_ln_b", (DEPTH, 256)),
    ("post_norm_g", (DEPTH, D_MODEL)),
)
WEIGHT_ORDER = ("meta_tokens", "pre_norm_g", "w_in", "gate_bias", "pool_w", "pool_scale", "w_out_pool", "q_norm_g", "w_uq",
                "kv_norm_g", "w_ukv", "w_out_mla", "conf_dw_w", "conf_dw_b", "conf_ln_g", "conf_ln_b", "w_out_conf", "sc_dw_w",
                "w_out_sc", "w_o", "post_norm_g")


def _dot(a, b):
    return lax.dot_general(a, b, (((1,), (0,)), ((), ())), preferred_element_type=F32)


def _dot_nt(a, b):
    return lax.dot_general(a, b, (((1,), (1,)), ((), ())), preferred_element_type=F32)


def _dot_tn(a, b):
    return lax.dot_general(a, b, (((0,), (0,)), ((), ())), preferred_element_type=F32)


def _sigmoid(x):
    return jax.nn.sigmoid(x)


def _silu(x):
    return x * _sigmoid(x)


def _silu_grad(x):
    s = _sigmoid(x)
    return s * (1.0 + x * (1.0 - s))


def _rms(x, g):
    return x * lax.rsqrt(jnp.mean(x * x, axis=-1, keepdims=True) + EPS) * g


def _sh(x, d):
    return x if d == 0 else pltpu.roll(x, d, 0)


def _ash(x, d):
    return x if d == 0 else pltpu.roll(x, x.shape[0] - d, 0)


def _lanes8(t):
    return jnp.concatenate([t] * HEADS, axis=1)


def _pool_window_sums(v, shift):
    a2 = v + shift(v, 1)
    a4 = a2 + shift(a2, 2)
    a8 = a4 + shift(a4, 4)
    a16 = a8 + shift(a8, 8)
    lane = lax.broadcasted_iota(jnp.int32, v.shape, 1)
    return jnp.where(lane < 64, a2, jnp.where(lane < 128, a4, jnp.where(lane < 192, a8, a16)))


def _pool_counts(first_row, rows):
    pos = first_row + lax.broadcasted_iota(jnp.int32, (rows, 256), 0)
    lane = lax.broadcasted_iota(jnp.int32, (rows, 256), 1)
    width = jnp.where(lane < 64, 2, jnp.where(lane < 128, 4, jnp.where(lane < 192, 8, 16)))
    return jnp.maximum(jnp.minimum(pos + 1, width), 1).astype(F32)


def _params(sem=None):
    return pltpu.CompilerParams(dimension_semantics=sem, vmem_limit_bytes=VMEM_LIMIT)


def _tile_specs(t, n_halo_blocks):
    per = t // HALO

    def cur(c, cb=0):
        return pl.BlockSpec((t, c), lambda i: (i, cb))

    def prev(c, cb=0):
        return pl.BlockSpec((HALO, c), lambda i: (jnp.maximum(i * per - 1, 0), cb))

    def nxt(c, cb=0):
        return pl.BlockSpec((HALO, c), lambda i: (jnp.minimum((i + 1) * per, n_halo_blocks - 1), cb))

    def full(shape):
        return pl.BlockSpec(shape, lambda i: (0,) * len(shape))

    return cur, prev, nxt, full


def _big_tile(rows):
    return rows // 3 if rows % (3 * LANES) == 0 else ROW_TILE


def matmul(a, b, mode, out_dtype, tm, tn, tk, name):
    if mode == "nn":
        (m, k), n = a.shape, b.shape[1]
        a_spec = pl.BlockSpec((tm, tk), lambda i, j, kk: (i, kk))
        b_spec = pl.BlockSpec((tk, tn), lambda i, j, kk: (kk, j))
        dot = _dot
    elif mode == "nt":
        (m, k), n = a.shape, b.shape[0]
        a_spec = pl.BlockSpec((tm, tk), lambda i, j, kk: (i, kk))
        b_spec = pl.BlockSpec((tn, tk), lambda i, j, kk: (j, kk))
        dot = _dot_nt
    else:
        (k, m), n = a.shape, b.shape[1]
        a_spec = pl.BlockSpec((tk, tm), lambda i, j, kk: (kk, i))
        b_spec = pl.BlockSpec((tk, tn), lambda i, j, kk: (kk, j))
        dot = _dot_tn
    assert m % tm == 0 and n % tn == 0 and k % tk == 0, (a.shape, b.shape, tm, tn, tk)
    nk = k // tk

    def body(a_ref, b_ref, o_ref, acc_ref):
        kk = pl.program_id(2)

        @pl.when(kk == 0)
        def _():
            acc_ref[...] = jnp.zeros_like(acc_ref)

        acc_ref[...] += dot(a_ref[...], b_ref[...])

        @pl.when(kk == nk - 1)
        def _():
            o_ref[...] = acc_ref[...].astype(out_dtype)

    return pl.pallas_call(
        body, name=name, grid=(m // tm, n // tn, nk), in_specs=[a_spec, b_spec],
        out_specs=pl.BlockSpec((tm, tn), lambda i, j, kk: (i, j)), out_shape=jax.ShapeDtypeStruct((m, n), out_dtype),
        scratch_shapes=[pltpu.VMEM((tm, tn), F32)], compiler_params=_params(("parallel", "parallel", "arbitrary")),
    )(a, b)


def prenorm_project(hres, g, w):
    rows, d = hres.shape
    n = w.shape[1]
    tm, tn = _big_tile(rows), n // 2

    def body(x_ref, g_ref, w_ref, z_ref, hb_ref):
        @pl.when(pl.program_id(1) == 0)
        def _():
            hb_ref[...] = _rms(x_ref[...], g_ref[...]).astype(BF16)

        z_ref[...] = _dot(hb_ref[...], w_ref[...]).astype(BF16)

    return pl.pallas_call(
        body, name="prenorm_project", grid=(rows // tm, n // tn),
        in_specs=[pl.BlockSpec((tm, d), lambda i, j: (i, 0)), pl.BlockSpec((1, d), lambda i, j: (0, 0)),
                  pl.BlockSpec((d, tn), lambda i, j: (0, j))],
        out_specs=[pl.BlockSpec((tm, tn), lambda i, j: (i, j)), pl.BlockSpec((tm, d), lambda i, j: (i, 0))],
        out_shape=[jax.ShapeDtypeStruct((rows, n), BF16), jax.ShapeDtypeStruct((rows, d), BF16)],
        compiler_params=_params(("parallel", "arbitrary")),
    )(hres, g, w)


def _rope(q, c, s1, s2, width):
    return q * c + pltpu.roll(q, width - 16, 1) * s1 + pltpu.roll(q, 16, 1) * s2


def _rope_transposed(dq, c, s1, s2, width):
    return dq * c + pltpu.roll(dq * s1, 16, 1) + pltpu.roll(dq * s2, width - 16, 1)


def _conf_conv(g1, w_ref):
    acc = jnp.zeros_like(g1)
    for k in range(CONF_K):
        acc = acc + w_ref[k:k + 1, :] * _sh(g1, CONF_K - 1 - k)
    return acc


def _conf_tail(c, cg, lg, lb):
    mu = jnp.mean(c, axis=-1, keepdims=True)
    xc = c - mu
    var = jnp.mean(xc * xc, axis=-1, keepdims=True)
    n = xc * lax.rsqrt(var + EPS) * lg + lb
    return _silu(n) * _silu(cg)


def branches_fwd(z_br, rope, pwbd, pscale, gq, wuq, gkv, wukv, conf_w, conf_vec, sc_w):
    rows = z_br.shape[0]
    t = ROW_TILE
    cur, prev, _, full = _tile_specs(t, rows // HALO)

    def body(zc_ref, zp_ref, rope_ref, pw_ref, ps_ref, gq_ref, wuq_ref, gkv_ref, wukv_ref, cw_ref, cv_ref, sw_ref,
             ua_ref, uc_ref, ud_ref, q_ref, k_ref, v_ref):
        i = pl.program_id(0)
        zp = jnp.where(i == 0, jnp.zeros(zp_ref.shape, zp_ref.dtype), zp_ref[...])

        def ext(lo, w=256):
            return jnp.concatenate([zp[:, lo:lo + w], zc_ref[:, lo:lo + w]], axis=0).astype(F32)

        def col(lo, w=256):
            return zc_ref[:, lo:lo + w].astype(F32)

        v = ext(PV)
        p = (_pool_window_sums(v, _sh) / _pool_counts(i * t - HALO, t + HALO) - v)[HALO:]
        ya = _dot(p.astype(BF16), pw_ref[...]) * ps_ref[...]
        ua_ref[...] = (ya * _silu(col(PG))).astype(BF16)

        g1 = ext(CA) * _sigmoid(ext(CGT))
        c = _conf_conv(g1, cw_ref)[HALO:] + cv_ref[0:1, :]
        uc_ref[...] = _conf_tail(c, col(CG), cv_ref[1:2, :], cv_ref[2:3, :]).astype(BF16)

        e = ext(C2) * ext(XV)
        f = jnp.zeros_like(e)
        for k in range(SC_K):
            f = f + sw_ref[k:k + 1, :] * _sh(e, SC_K - 1 - k)
        ud_ref[...] = (col(BG) * f[HALO:] * _silu(col(SG))).astype(BF16)

        cth, s1, s2 = rope_ref[:, 0:128], rope_ref[:, 128:256], rope_ref[:, 256:384]
        qn = _rms(col(CQ), gq_ref[...]).astype(BF16)
        q = _dot(qn, wuq_ref[...])
        w8 = HEADS * HEAD_PAD
        q_ref[...] = (_rope(q, _lanes8(cth), _lanes8(s1), _lanes8(s2), w8) * Q_SCALE).astype(BF16)
        kvn = _rms(col(CKV, 128), gkv_ref[...]).astype(BF16)
        kv = _dot(kvn, wukv_ref[...])
        kr = _rope(col(KR, 128), cth, s1, s2, HEAD_PAD)
        k_ref[...] = (kv[:, :w8] + _lanes8(kr)).astype(BF16)
        v_ref[...] = kv[:, w8:].astype(BF16)

    outs = [jax.ShapeDtypeStruct((rows, 256), BF16)] * 3 + [jax.ShapeDtypeStruct((rows, 1024), BF16)] * 2 + [
        jax.ShapeDtypeStruct((rows, 512), BF16)]
    return pl.pallas_call(
        body, name="branches_fwd", grid=(rows // t,),
        in_specs=[cur(ZB), prev(ZB), cur(384), full((256, 256)), full((1, 256)), full((1, 256)), full((256, 1024)),
                  full((1, 128)), full((128, 1536)), full((32, 256)), full((8, 256)), full((8, 256))],
        out_specs=[cur(256), cur(256), cur(256), cur(1024), cur(1024), cur(512)], out_shape=outs,
        compiler_params=_params(("parallel",)),
    )(z_br, z_br, rope, pwbd, pscale, gq, wuq, gkv, wukv, conf_w, conf_vec, sc_w)


def _head_lane_mask(h):
    lane = lax.broadcasted_iota(jnp.int32, (1, 2 * V_DIM), 1)
    return (lane >= V_DIM * h) & (lane < V_DIM * (h + 1))


def attention_fwd(q, k, v):
    rows = q.shape[0]
    tq = ROW_TILE
    nq = rows // tq

    def body(q_ref, k_ref, v_ref, o_ref, lse_ref):
        i = pl.program_id(1)
        row = lax.broadcasted_iota(jnp.int32, (tq, tq), 0)
        colm = lax.broadcasted_iota(jnp.int32, (tq, tq), 1)
        out = jnp.zeros((tq, 2 * V_DIM), F32)
        for h in range(2):
            qh = q_ref[:, HEAD_PAD * h:HEAD_PAD * (h + 1)]
            hm = _head_lane_mask(h)

            def step(j, carry, diagonal):
                m, l, acc = carry
                r0 = pl.multiple_of(j * tq, tq)
                kh = k_ref[pl.ds(r0, tq), HEAD_PAD * h:HEAD_PAD * (h + 1)]
                vh = jnp.where(hm, v_ref[pl.ds(r0, tq), :], jnp.zeros((), BF16))
                s = _dot_nt(qh, kh)
                if diagonal:
                    s = jnp.where(colm <= row, s, -1e30)
                m2 = jnp.maximum(m, jnp.max(s, axis=-1, keepdims=True))
                alpha = jnp.exp(m - m2)
                pr = jnp.exp(s - m2)
                return m2, alpha * l + jnp.sum(pr, axis=-1, keepdims=True), alpha * acc + _dot(pr.astype(BF16), vh)

            init = (jnp.full((tq, 1), -1e30, F32), jnp.zeros((tq, 1), F32), jnp.zeros((tq, 2 * V_DIM), F32))
            carry = lax.fori_loop(0, i, lambda j, cr: step(j, cr, False), init)
            m, l, acc = step(i, carry, True)
            out = out + acc / l
            lse_ref[h] = jnp.broadcast_to(m + jnp.log(l), (tq, LANES))
        o_ref[...] = out.astype(BF16)

    return pl.pallas_call(
        body, name="attention_fwd", grid=(HEADS // 2, nq),
        in_specs=[pl.BlockSpec((tq, 2 * HEAD_PAD), lambda p, i: (i, p)), pl.BlockSpec((rows, 2 * HEAD_PAD), lambda p, i: (0, p)),
                  pl.BlockSpec((rows, 2 * V_DIM), lambda p, i: (0, p))],
        out_specs=[pl.BlockSpec((tq, 2 * V_DIM), lambda p, i: (i, p)), pl.BlockSpec((2, tq, LANES), lambda p, i: (p, i, 0))],
        out_shape=[jax.ShapeDtypeStruct((rows, HEADS * V_DIM), BF16), jax.ShapeDtypeStruct((HEADS, rows, LANES), F32)],
        compiler_params=_params(("parallel", "parallel")),
    )(q, k, v)


def merge_fwd(ua, o_att, uc, ud, z_br, z_gl, bias, woa, wob, woc, wod, wo, gpost, hres):
    rows = hres.shape[0]
    t = ROW_TILE
    cur, _, _, full = _tile_specs(t, rows // HALO)
    d = D_MODEL

    def body(ua_ref, ob_ref, uc_ref, ud_ref, mg_ref, gl_ref, b_ref, woa_ref, wob_ref, woc_ref, wod_ref, wo_ref, gp_ref, h_ref,
             ub_ref, mb_ref, o_ref, hn_ref):
        ub = (ob_ref[...].astype(F32) * _silu(mg_ref[...].astype(F32))).astype(BF16)
        ub_ref[...] = ub
        m = jnp.zeros((t, d), F32)
        for idx, (u, w_ref) in enumerate(((ua_ref[...], woa_ref), (ub, wob_ref), (uc_ref[...], woc_ref), (ud_ref[...], wod_ref))):
            gate = _sigmoid(gl_ref[:, d * idx:d * (idx + 1)].astype(F32) + b_ref[:, d * idx:d * (idx + 1)])
            m = m + gate * _dot(u, w_ref[...])
        mb = m.astype(BF16)
        mb_ref[...] = mb
        o = _dot(mb, wo_ref[...])
        o_ref[...] = o
        hn_ref[...] = h_ref[...] + _rms(o, gp_ref[...])

    return pl.pallas_call(
        body, name="merge_fwd", grid=(rows // t,),
        in_specs=[cur(256), cur(512), cur(256), cur(256), cur(512, MG // 512), cur(ZG), full((1, ZG)), full((256, d)), full((512, d)),
                  full((256, d)), full((256, d)), full((d, d)), full((1, d)), cur(d)],
        out_specs=[cur(512), cur(d), cur(d), cur(d)],
        out_shape=[jax.ShapeDtypeStruct((rows, 512), BF16), jax.ShapeDtypeStruct((rows, d), BF16), jax.ShapeDtypeStruct((rows, d), F32),
                   jax.ShapeDtypeStruct((rows, d), F32)],
        compiler_params=_params(("parallel",)),
    )(ua, o_att, uc, ud, z_br, z_gl, bias, woa, wob, woc, wod, wo, gpost, hres)


def loss_head(hres, target, n_tokens):
    rows, d = hres.shape
    t = ROW_TILE
    cur, _, _, full = _tile_specs(t, rows // HALO)
    n_steps = rows // t

    def body(h_ref, t_ref, dh_ref, tot_ref, acc_ref):
        i = pl.program_id(0)

        @pl.when(i == 0)
        def _():
            acc_ref[...] = jnp.zeros_like(acc_ref)

        r = i * t + lax.broadcasted_iota(jnp.int32, (t, 1), 0)
        diff = jnp.where((r >= N_META) & (r < N_META + n_tokens), h_ref[...] - t_ref[...], 0.0)
        dh_ref[...] = diff * (1.0 / d)
        acc_ref[...] += jnp.sum(diff * diff, axis=0, keepdims=True)

        @pl.when(i == n_steps - 1)
        def _():
            tot_ref[...] = jnp.broadcast_to(jnp.sum(acc_ref[...], axis=1, keepdims=True), (1, LANES))

    return pl.pallas_call(
        body, name="loss_head", grid=(n_steps,), in_specs=[cur(d), cur(d)], out_specs=[cur(d), full((1, LANES))],
        out_shape=[jax.ShapeDtypeStruct((rows, d), F32), jax.ShapeDtypeStruct((1, LANES), F32)],
        scratch_shapes=[pltpu.VMEM((1, d), F32)], compiler_params=_params(("arbitrary",)),
    )(hres, target)


def _accumulate(i, ref, value):
    @pl.when(i == 0)
    def _():
        ref[...] = value

    @pl.when(i > 0)
    def _():
        ref[...] += value


def postnorm_bwd(dh, o, mb, wo, gpost):
    rows, d = dh.shape
    t = ROW_TILE
    cur, _, _, full = _tile_specs(t, rows // HALO)

    def body(dh_ref, o_ref, mb_ref, wo_ref, gp_ref, dm_ref, dwo_ref, dgp_ref):
        i = pl.program_id(0)
        _, vjp = jax.vjp(_rms, o_ref[...], gp_ref[...])
        do, dg = vjp(dh_ref[...])
        dob = do.astype(BF16)
        dm_ref[...] = _dot_nt(dob, wo_ref[...])
        _accumulate(i, dwo_ref, _dot_tn(mb_ref[...], dob))
        _accumulate(i, dgp_ref, dg)

    return pl.pallas_call(
        body, name="postnorm_bwd", grid=(rows // t,), in_specs=[cur(d), cur(d), cur(d), full((d, d)), full((1, d))],
        out_specs=[cur(d), full((d, d)), full((1, d))],
        out_shape=[jax.ShapeDtypeStruct((rows, d), F32), jax.ShapeDtypeStruct((d, d), F32), jax.ShapeDtypeStruct((1, d), F32)],
        compiler_params=_params(("arbitrary",)),
    )(dh, o, mb, wo, gpost)


def merge_bwd(dm, ua, ub, uc, ud, z_gl, bias, woa, wob, woc, wod):
    rows, d = dm.shape
    t = ROW_TILE
    cur, _, _, full = _tile_specs(t, rows // HALO)
    widths = (256, 512, 256, 256)

    def body(dm_ref, ua_ref, ub_ref, uc_ref, ud_ref, gl_ref, b_ref, woa_ref, wob_ref, woc_ref, wod_ref,
             dua_ref, dub_ref, duc_ref, dud_ref, dgl_ref, dwa_ref, dwb_ref, dwc_ref, dwd_ref, db_ref):
        i = pl.program_id(0)
        dm = dm_ref[...]
        groups = ((ua_ref, woa_ref, dua_ref, dwa_ref), (ub_ref, wob_ref, dub_ref, dwb_ref), (uc_ref, woc_ref, duc_ref, dwc_ref),
                  (ud_ref, wod_ref, dud_ref, dwd_ref))
        for idx, (u_ref, w_ref, du_ref, dw_ref) in enumerate(groups):
            cols = slice(d * idx, d * (idx + 1))
            u = u_ref[...]
            gate = _sigmoid(gl_ref[:, cols].astype(F32) + b_ref[:, cols])
            dgl = dm * _dot(u, w_ref[...]) * gate * (1.0 - gate)
            dgl_ref[:, cols] = dgl.astype(BF16)
            _accumulate(i, db_ref.at[:, cols], jnp.sum(dgl, axis=0, keepdims=True))
            dyb = (dm * gate).astype(BF16)
            du_ref[...] = _dot_nt(dyb, w_ref[...])
            _accumulate(i, dw_ref, _dot_tn(u, dyb))

    return pl.pallas_call(
        body, name="merge_bwd", grid=(rows // t,),
        in_specs=[cur(d), cur(256), cur(512), cur(256), cur(256), cur(ZG), full((1, ZG))] + [full((w, d)) for w in widths],
        out_specs=[cur(256), cur(512), cur(256), cur(256), cur(ZG)] + [full((w, d)) for w in widths] + [full((1, ZG))],
        out_shape=[jax.ShapeDtypeStruct((rows, w), F32) for w in widths] + [jax.ShapeDtypeStruct((rows, ZG), BF16)] + [
            jax.ShapeDtypeStruct((w, d), F32) for w in widths] + [jax.ShapeDtypeStruct((1, ZG), F32)],
        compiler_params=_params(("arbitrary",)),
    )(dm, ua, ub, uc, ud, z_gl, bias, woa, wob, woc, wod)


def pool_bwd(z_br, dua, pwbd, pscale):
    rows = z_br.shape[0]
    t = ROW_TILE
    n_steps = rows // t
    cur, prev, nxt, full = _tile_specs(t, rows // HALO)

    def body(zc_ref, zp_ref, zn_ref, dc_ref, dn_ref, pw_ref, ps_ref, dz_ref, dpw_ref, dps_ref):
        i = pl.program_id(0)
        zp = jnp.where(i == 0, jnp.zeros(zp_ref.shape, zp_ref.dtype), zp_ref[...])
        zn = jnp.where(i == n_steps - 1, jnp.zeros(zn_ref.shape, zn_ref.dtype), zn_ref[...])
        dun = jnp.where(i == n_steps - 1, jnp.zeros(dn_ref.shape, dn_ref.dtype), dn_ref[...])

        def ext(lo):
            return jnp.concatenate([zp[:, lo:lo + 256], zc_ref[:, lo:lo + 256], zn[:, lo:lo + 256]], axis=0).astype(F32)

        n_ext = t + 2 * HALO
        v, pg = ext(PV), ext(PG)
        cnt = _pool_counts(i * t - HALO, n_ext)
        p = (_pool_window_sums(v, _sh) / cnt - v)[HALO:HALO + t]
        du = jnp.concatenate([jnp.zeros((HALO, 256), F32), dc_ref[...], dun], axis=0)
        dya = du * _silu(pg)
        dypb = (dya * ps_ref[...]).astype(BF16)
        dp = _dot_nt(dypb, pw_ref[...])
        dv = (_pool_window_sums(dp / cnt, _ash) - dp)[HALO:HALO + t]
        pb = p.astype(BF16)
        pw = _dot(pb, pw_ref[...])
        duc, pgc = dc_ref[...], pg[HALO:HALO + t]
        dpg = duc * pw * ps_ref[...] * _silu_grad(pgc)
        dz_ref[...] = jnp.concatenate([dv, dpg], axis=1).astype(BF16)
        _accumulate(i, dpw_ref, _dot_tn(pb, dypb[HALO:HALO + t]))
        _accumulate(i, dps_ref, jnp.sum(dya[HALO:HALO + t] * pw, axis=0, keepdims=True))

    return pl.pallas_call(
        body, name="pool_bwd", grid=(n_steps,),
        in_specs=[cur(ZB), prev(ZB), nxt(ZB), cur(256), nxt(256), full((256, 256)), full((1, 256))],
        out_specs=[cur(512), full((256, 256)), full((1, 256))],
        out_shape=[jax.ShapeDtypeStruct((rows, 512), BF16), jax.ShapeDtypeStruct((256, 256), F32), jax.ShapeDtypeStruct((1, 256), F32)],
        compiler_params=_params(("arbitrary",)),
    )(z_br, z_br, z_br, dua, dua, pwbd, pscale)


def shortconv_bwd(z_br, dud, sc_w):
    rows = z_br.shape[0]
    t = ROW_TILE
    n_steps = rows // t
    cur, prev, nxt, full = _tile_specs(t, rows // HALO)

    def body(zc_ref, zp_ref, zn_ref, dc_ref, dn_ref, sw_ref, dz_ref, dw_ref):
        i = pl.program_id(0)
        zp = jnp.where(i == 0, jnp.zeros(zp_ref.shape, zp_ref.dtype), zp_ref[...])
        zn = jnp.where(i == n_steps - 1, jnp.zeros(zn_ref.shape, zn_ref.dtype), zn_ref[...])
        dun = jnp.where(i == n_steps - 1, jnp.zeros(dn_ref.shape, dn_ref.dtype), dn_ref[...])

        def ext(lo):
            return jnp.concatenate([zp[:, lo:lo + 256], zc_ref[:, lo:lo + 256], zn[:, lo:lo + 256]], axis=0).astype(F32)

        mid = slice(HALO, HALO + t)
        bg, c2, xv, sg = ext(BG), ext(C2), ext(XV), ext(SG)
        du = jnp.concatenate([jnp.zeros((HALO, 256), F32), dc_ref[...], dun], axis=0)
        e = c2 * xv
        shifted = [_sh(e, SC_K - 1 - k) for k in range(SC_K)]
        f = sum(sw_ref[k:k + 1, :] * shifted[k] for k in range(SC_K))
        gate = _silu(sg)
        df = du * gate * bg
        de = sum(sw_ref[k:k + 1, :] * _ash(df, SC_K - 1 - k) for k in range(SC_K))
        dbg = du * gate * f
        dsg = du * bg * f * _silu_grad(sg)
        dz_ref[...] = jnp.concatenate([dbg[mid], (de * xv)[mid], (de * c2)[mid], dsg[mid]], axis=1).astype(BF16)
        dw = jnp.concatenate([jnp.sum((df * shifted[k])[mid], axis=0, keepdims=True) for k in range(SC_K)] + [
            jnp.zeros((8 - SC_K, 256), F32)], axis=0)
        _accumulate(i, dw_ref, dw)

    return pl.pallas_call(
        body, name="shortconv_bwd", grid=(n_steps,), in_specs=[cur(ZB), prev(ZB), nxt(ZB), cur(256), nxt(256), full((8, 256))],
        out_specs=[cur(1024), full((8, 256))],
        out_shape=[jax.ShapeDtypeStruct((rows, 1024), BF16), jax.ShapeDtypeStruct((8, 256), F32)],
        compiler_params=_params(("arbitrary",)),
    )(z_br, z_br, z_br, dud, dud, sc_w)


def conformer_bwd_tail(z_br, duc, conf_w, conf_vec):
    rows = z_br.shape[0]
    t = ROW_TILE
    cur, prev, _, full = _tile_specs(t, rows // HALO)

    def body(zc_ref, zp_ref, du_ref, cw_ref, cv_ref, dc_ref, dcg_ref, dv_ref):
        i = pl.program_id(0)
        zp = jnp.where(i == 0, jnp.zeros(zp_ref.shape, zp_ref.dtype), zp_ref[...])

        def ext(lo):
            return jnp.concatenate([zp[:, lo:lo + 256], zc_ref[:, lo:lo + 256]], axis=0).astype(F32)

        g1 = ext(CA) * _sigmoid(ext(CGT))
        c = _conf_conv(g1, cw_ref)[HALO:] + cv_ref[0:1, :]
        _, vjp = jax.vjp(_conf_tail, c, zc_ref[:, CG:CG + 256].astype(F32), cv_ref[1:2, :], cv_ref[2:3, :])
        dc, dcg, dlg, dlb = vjp(du_ref[...])
        dc_ref[...] = dc
        dcg_ref[...] = dcg.astype(BF16)
        dvec = jnp.concatenate([dlg, dlb, jnp.sum(dc, axis=0, keepdims=True), jnp.zeros((5, 256), F32)], axis=0)
        _accumulate(i, dv_ref, dvec)

    return pl.pallas_call(
        body, name="conformer_bwd_tail", grid=(rows // t,), in_specs=[cur(ZB), prev(ZB), cur(256), full((32, 256)), full((8, 256))],
        out_specs=[cur(256), cur(256), full((8, 256))],
        out_shape=[jax.ShapeDtypeStruct((rows, 256), F32), jax.ShapeDtypeStruct((rows, 256), BF16), jax.ShapeDtypeStruct((8, 256), F32)],
        compiler_params=_params(("arbitrary",)),
    )(z_br, z_br, duc, conf_w, conf_vec)


def conformer_bwd_conv(z_br, dc, conf_w):
    rows = z_br.shape[0]
    t = ROW_TILE
    n_steps = rows // t
    cur, prev, nxt, full = _tile_specs(t, rows // HALO)

    def body(zc_ref, zp_ref, dc_ref, dn_ref, cw_ref, dz_ref, dw_ref):
        i = pl.program_id(0)
        zp = jnp.where(i == 0, jnp.zeros(zp_ref.shape, zp_ref.dtype), zp_ref[...])
        dcn = jnp.where(i == n_steps - 1, jnp.zeros(dn_ref.shape, dn_ref.dtype), dn_ref[...])

        def ext(lo):
            return jnp.concatenate([zp[:, lo:lo + 256], zc_ref[:, lo:lo + 256]], axis=0).astype(F32)

        a, gt = ext(CA), ext(CGT)
        sg = _sigmoid(gt)
        g1 = a * sg
        dc = dc_ref[...]
        dce = jnp.concatenate([dc, dcn], axis=0)
        dg1 = jnp.zeros_like(dce)
        dws = []
        for k in range(CONF_K):
            dg1 = dg1 + cw_ref[k:k + 1, :] * _ash(dce, CONF_K - 1 - k)
            dws.append(jnp.sum(dc * _sh(g1, CONF_K - 1 - k)[HALO:], axis=0, keepdims=True))
        dg1 = dg1[:t]
        ac, sc = a[HALO:], sg[HALO:]
        dz_ref[...] = jnp.concatenate([dg1 * sc, dg1 * ac * sc * (1.0 - sc)], axis=1).astype(BF16)
        _accumulate(i, dw_ref, jnp.concatenate(dws + [jnp.zeros((32 - CONF_K, 256), F32)], axis=0))

    return pl.pallas_call(
        body, name="conformer_bwd_conv", grid=(n_steps,), in_specs=[cur(ZB), prev(ZB), cur(256), nxt(256), full((32, 256))],
        out_specs=[cur(512), full((32, 256))],
        out_shape=[jax.ShapeDtypeStruct((rows, 512), BF16), jax.ShapeDtypeStruct((32, 256), F32)],
        compiler_params=_params(("arbitrary",)),
    )(z_br, z_br, dc, dc, conf_w)


def attention_bwd_prep(dub, o_att, z_br):
    rows = dub.shape[0]
    t = ROW_TILE
    cur, _, _, _ = _tile_specs(t, rows // HALO)

    def body(du_ref, o_ref, mg_ref, do_ref, dmg_ref, delta_ref):
        du, o, mg = du_ref[...], o_ref[...].astype(F32), mg_ref[...].astype(F32)
        do = du * _silu(mg)
        do_ref[...] = do.astype(BF16)
        dmg_ref[...] = (du * o * _silu_grad(mg)).astype(BF16)
        prod = do * o
        lane = lax.broadcasted_iota(jnp.int32, (1, HEADS * V_DIM), 1)
        for h in range(HEADS):
            part = jnp.where((lane >= V_DIM * h) & (lane < V_DIM * (h + 1)), prod, 0.0)
            delta_ref[h] = jnp.broadcast_to(jnp.sum(part, axis=-1, keepdims=True), (t, LANES))

    return pl.pallas_call(
        body, name="attention_bwd_prep", grid=(rows // t,), in_specs=[cur(512), cur(512), cur(512, MG // 512)],
        out_specs=[cur(512), cur(512), pl.BlockSpec((HEADS, t, LANES), lambda i: (0, i, 0))],
        out_shape=[jax.ShapeDtypeStruct((rows, 512), BF16), jax.ShapeDtypeStruct((rows, 512), BF16),
                   jax.ShapeDtypeStruct((HEADS, rows, LANES), F32)],
        compiler_params=_params(("parallel",)),
    )(dub, o_att, z_br)


def attention_bwd(q, k, v, do, lse, delta):
    rows = q.shape[0]
    tq = ROW_TILE
    nq = rows // tq

    def body(q_ref, k_ref, v_ref, do_ref, lse_ref, dl_ref, dq_ref, dk_ref, dv_ref):
        j = pl.program_id(1)

        @pl.when(j == 0)
        def _():
            dq_ref[...] = jnp.zeros_like(dq_ref)

        row = lax.broadcasted_iota(jnp.int32, (tq, tq), 0)
        colm = lax.broadcasted_iota(jnp.int32, (tq, tq), 1)
        dv_tot = jnp.zeros((tq, 2 * V_DIM), F32)
        for h in range(2):
            lanes = slice(HEAD_PAD * h, HEAD_PAD * (h + 1))
            hm = _head_lane_mask(h)
            kh = k_ref[:, lanes]
            vh = jnp.where(hm, v_ref[...], jnp.zeros((), BF16))

            def step(i, carry):
                dk, dv = carry
                r0 = pl.multiple_of(i * tq, tq)
                qi = q_ref[pl.ds(r0, tq), lanes]
                doi = jnp.where(hm, do_ref[pl.ds(r0, tq), :], jnp.zeros((), BF16))
                s = jnp.where((colm + j * tq) <= (row + i * tq), _dot_nt(qi, kh), -1e30)
                pr = jnp.exp(s - lse_ref[h, pl.ds(r0, tq), :][:, 0:1])
                dv = dv + _dot_tn(pr.astype(BF16), doi)
                dp = _dot_nt(doi, vh)
                ds = (pr * (dp - dl_ref[h, pl.ds(r0, tq), :][:, 0:1])).astype(BF16)
                dk = dk + _dot_tn(ds, qi)
                dq_ref[pl.ds(r0, tq), lanes] += _dot(ds, kh)
                return dk, dv

            dk, dv = lax.fori_loop(j, nq, step, (jnp.zeros((tq, HEAD_PAD), F32), jnp.zeros((tq, 2 * V_DIM), F32)))
            dk_ref[:, lanes] = dk
            dv_tot = dv_tot + dv
        dv_ref[...] = dv_tot

    return pl.pallas_call(
        body, name="attention_bwd", grid=(HEADS // 2, nq),
        in_specs=[pl.BlockSpec((rows, 2 * HEAD_PAD), lambda p, j: (0, p)), pl.BlockSpec((tq, 2 * HEAD_PAD), lambda p, j: (j, p)),
                  pl.BlockSpec((tq, 2 * V_DIM), lambda p, j: (j, p)), pl.BlockSpec((rows, 2 * V_DIM), lambda p, j: (0, p)),
                  pl.BlockSpec((2, rows, LANES), lambda p, j: (p, 0, 0)), pl.BlockSpec((2, rows, LANES), lambda p, j: (p, 0, 0))],
        out_specs=[pl.BlockSpec((rows, 2 * HEAD_PAD), lambda p, j: (0, p)), pl.BlockSpec((tq, 2 * HEAD_PAD), lambda p, j: (j, p)),
                   pl.BlockSpec((tq, 2 * V_DIM), lambda p, j: (j, p))],
        out_shape=[jax.ShapeDtypeStruct((rows, HEADS * HEAD_PAD), F32), jax.ShapeDtypeStruct((rows, HEADS * HEAD_PAD), F32),
                   jax.ShapeDtypeStruct((rows, HEADS * V_DIM), F32)],
        compiler_params=_params(("parallel", "arbitrary")),
    )(q, k, v, do, lse, delta)


def mla_prep_bwd(dq, dk, dv, z_br, rope, gq, wuq, gkv, wukv):
    rows = dq.shape[0]
    t = ROW_TILE
    cur, _, _, full = _tile_specs(t, rows // HALO)
    w8 = HEADS * HEAD_PAD

    def body(dq_ref, dk_ref, dv_ref, z_ref, rope_ref, gq_ref, wuq_ref, gkv_ref, wukv_ref, dz_ref, dwuq_ref, dwukv_ref, dgq_ref, dgkv_ref):
        i = pl.program_id(0)
        cth, s1, s2 = rope_ref[:, 0:128], rope_ref[:, 128:256], rope_ref[:, 256:384]
        dqb = _rope_transposed(dq_ref[...] * Q_SCALE, _lanes8(cth), _lanes8(s1), _lanes8(s2), w8).astype(BF16)
        cq = z_ref[:, 0:256].astype(F32)
        qn, vjp_q = jax.vjp(_rms, cq, gq_ref[...])
        _accumulate(i, dwuq_ref, _dot_tn(qn.astype(BF16), dqb))
        dcq, dgq = vjp_q(_dot_nt(dqb, wuq_ref[...]))
        _accumulate(i, dgq_ref, dgq)

        dk = dk_ref[...]
        dkr = sum(dk[:, HEAD_PAD * h:HEAD_PAD * (h + 1)] for h in range(HEADS))
        dkr = _rope_transposed(dkr, cth, s1, s2, HEAD_PAD)
        lane = lax.broadcasted_iota(jnp.int32, (1, HEAD_PAD), 1)
        dkr = jnp.where((lane >= QK_NOPE) & (lane < QK_NOPE + QK_ROPE), dkr, 0.0)
        dkvb = jnp.concatenate([dk, dv_ref[...]], axis=1).astype(BF16)
        ckv = z_ref[:, 256:384].astype(F32)
        kvn, vjp_kv = jax.vjp(_rms, ckv, gkv_ref[...])
        _accumulate(i, dwukv_ref, _dot_tn(kvn.astype(BF16), dkvb))
        dckv, dgkv = vjp_kv(_dot_nt(dkvb, wukv_ref[...]))
        _accumulate(i, dgkv_ref, dgkv)
        dz_ref[...] = jnp.concatenate([dcq, dckv, dkr], axis=1).astype(BF16)

    return pl.pallas_call(
        body, name="mla_prep_bwd", grid=(rows // t,),
        in_specs=[cur(w8), cur(w8), cur(512), cur(512, CQ // 512), cur(384), full((1, 256)), full((256, w8)), full((1, 128)),
                  full((128, w8 + 512))],
        out_specs=[cur(512), full((256, w8)), full((128, w8 + 512)), full((1, 256)), full((1, 128))],
        out_shape=[jax.ShapeDtypeStruct((rows, 512), BF16), jax.ShapeDtypeStruct((256, w8), F32), jax.ShapeDtypeStruct((128, w8 + 512), F32),
                   jax.ShapeDtypeStruct((1, 256), F32), jax.ShapeDtypeStruct((1, 128), F32)],
        compiler_params=_params(("arbitrary",)),
    )(dq, dk, dv, z_br, rope, gq, wuq, gkv, wukv)


def prenorm_bwd(dz_br, w_br, dh_gl, hres, gpre, dh_next):
    rows, d = hres.shape
    t = ROW_TILE
    cur, _, _, full = _tile_specs(t, rows // HALO)

    def body(dz_ref, w_ref, dp_ref, x_ref, g_ref, dn_ref, dx_ref, dg_ref):
        i = pl.program_id(0)
        dh = _dot_nt(dz_ref[...], w_ref[...]) + dp_ref[...]
        _, vjp = jax.vjp(_rms, x_ref[...], g_ref[...])
        dx, dg = vjp(dh)
        dx_ref[...] = dx + dn_ref[...]
        _accumulate(i, dg_ref, dg)

    return pl.pallas_call(
        body, name="prenorm_bwd", grid=(rows // t,), in_specs=[cur(ZB), full((d, ZB)), cur(d), cur(d), full((1, d)), cur(d)],
        out_specs=[cur(d), full((1, d))], out_shape=[jax.ShapeDtypeStruct((rows, d), F32), jax.ShapeDtypeStruct((1, d), F32)],
        compiler_params=_params(("arbitrary",)),
    )(dz_br, w_br, dh_gl, hres, gpre, dh_next)


def _mesh_position():
    return lax.axis_index("x"), lax.axis_index("y"), lax.axis_index("c")


def chip_exchange(src, gather, name):
    block = src.shape if gather else src.shape[1:]

    def body(src_ref, dst_ref, send_sems, recv_sems, local_sem):
        x, y, c = _mesh_position()
        me = 2 * x + y
        peers = ((1 - x, y), (x, 1 - y), (1 - x, 1 - y))

        def part(k):
            return src_ref if gather else src_ref.at[k]

        def copy(j, slot):
            px, py = peers[j]
            return pltpu.make_async_remote_copy(src_ref=part(2 * px + py), dst_ref=dst_ref.at[slot], send_sem=send_sems.at[j],
                                                recv_sem=recv_sems.at[j], device_id=(px, py, c), device_id_type=MESH)

        local = pltpu.make_async_copy(part(me), dst_ref.at[me], local_sem)
        local.start()
        sends = [copy(j, me) for j in range(3)]
        for cp in sends:
            cp.start()
        for j, (px, py) in enumerate(peers):
            copy(j, 2 * px + py).wait_recv()
        for cp in sends:
            cp.wait_send()
        local.wait()

    return pl.pallas_call(
        body, name=name, in_specs=[pl.BlockSpec(memory_space=pl.ANY)], out_specs=pl.BlockSpec(memory_space=pl.ANY),
        out_shape=jax.ShapeDtypeStruct((N_CHIPS,) + tuple(block), src.dtype),
        scratch_shapes=[pltpu.SemaphoreType.DMA((3,)), pltpu.SemaphoreType.DMA((3,)), pltpu.SemaphoreType.DMA(())],
    )(src)


def sibling_swap(src, name):
    def body(src_ref, dst_ref, send_sem, recv_sem):
        x, y, c = _mesh_position()
        cp = pltpu.make_async_remote_copy(src_ref=src_ref, dst_ref=dst_ref, send_sem=send_sem, recv_sem=recv_sem,
                                          device_id=(x, y, 1 - c), device_id_type=MESH)
        cp.start()
        cp.wait()

    return pl.pallas_call(
        body, name=name, in_specs=[pl.BlockSpec(memory_space=pl.ANY)], out_specs=pl.BlockSpec(memory_space=pl.ANY),
        out_shape=jax.ShapeDtypeStruct(src.shape, src.dtype),
        scratch_shapes=[pltpu.SemaphoreType.DMA(()), pltpu.SemaphoreType.DMA(())],
    )(src)


def _comm_block(rows):
    return 1024 if rows % 1024 == 0 else rows


def sum_slots(buf, name):
    n, r, c = buf.shape
    rb = _comm_block(r)

    def body(b_ref, o_ref):
        acc = b_ref[0].astype(F32)
        for s in range(1, n):
            acc = acc + b_ref[s].astype(F32)
        o_ref[...] = acc

    return pl.pallas_call(
        body, name=name, grid=(r // rb,), in_specs=[pl.BlockSpec((n, rb, c), lambda i: (0, i, 0))],
        out_specs=pl.BlockSpec((rb, c), lambda i: (i, 0)), out_shape=jax.ShapeDtypeStruct((r, c), F32),
        compiler_params=_params(("parallel",)),
    )(buf)


def add_pair(a, b, out_dtype, name):
    shape = a.shape
    a2, b2 = a.reshape(-1, shape[-1]), b.reshape(-1, shape[-1])
    r, c = a2.shape
    rb = _comm_block(r)

    def body(a_ref, b_ref, o_ref):
        o_ref[...] = (a_ref[...].astype(F32) + b_ref[...].astype(F32)).astype(out_dtype)

    out = pl.pallas_call(
        body, name=name, grid=(r // rb,), in_specs=[pl.BlockSpec((rb, c), lambda i: (i, 0))] * 2,
        out_specs=pl.BlockSpec((rb, c), lambda i: (i, 0)), out_shape=jax.ShapeDtypeStruct((r, c), out_dtype),
        compiler_params=_params(("parallel",)),
    )(a2, b2)
    return out.reshape(shape)


def adamw(w, g, m, v):
    shape = w.shape
    cols = shape[-1]
    rows = math.prod(shape[:-1])
    rb = rows if rows * cols <= 256 * 1024 else 256
    assert rows % rb == 0, shape

    def body(w_ref, g_ref, m_ref, v_ref, d_ref, nm_ref, nv_ref):
        g_ = g_ref[...]
        nm = ADAM_B1 * m_ref[...] + (1.0 - ADAM_B1) * g_
        nv = ADAM_B2 * v_ref[...] + (1.0 - ADAM_B2) * (g_ * g_)
        m_hat = nm / (1.0 - ADAM_B1 ** ADAM_STEP)
        v_hat = nv / (1.0 - ADAM_B2 ** ADAM_STEP)
        d_ref[...] = -ADAM_LR * (m_hat / (jnp.sqrt(v_hat) + ADAM_EPS) + ADAM_WD * w_ref[...])
        nm_ref[...] = nm
        nv_ref[...] = nv

    spec = pl.BlockSpec((rb, cols), lambda i: (i, 0))
    outs = pl.pallas_call(
        body, name="adamw", grid=(rows // rb,), in_specs=[spec] * 4, out_specs=[spec] * 3,
        out_shape=[jax.ShapeDtypeStruct((rows, cols), F32)] * 3, compiler_params=_params(("parallel",)),
    )(*(a.reshape(rows, cols) for a in (w, g, m, v)))
    return tuple(o.reshape(shape) for o in outs)


def _pack(arrays, dtype, row_multiple):
    flat = jnp.concatenate([a.astype(dtype).reshape(-1) for a in arrays])
    per = LANES * row_multiple
    total = -(-flat.shape[0] // per) * per
    return jnp.pad(flat, (0, total - flat.shape[0])).reshape(total // LANES, LANES)


def _unpack(buf, shapes):
    flat = buf.reshape(-1)
    out, off = [], 0
    for s in shapes:
        n = math.prod(s)
        out.append(flat[off:off + n].reshape(s))
        off += n
    return out


def _place_halves(mine, other, c, axis):
    half = mine.shape[axis]
    shape = list(mine.shape)
    shape[axis] = 2 * half
    start_mine = [0] * len(shape)
    start_other = [0] * len(shape)
    start_mine[axis] = c * half
    start_other[axis] = (1 - c) * half
    out = lax.dynamic_update_slice(jnp.zeros(shape, mine.dtype), mine, start_mine)
    return lax.dynamic_update_slice(out, other, start_other)


def _take_half(buf, which, axis):
    half = buf.shape[axis] // 2
    return lax.dynamic_slice_in_dim(buf, which * half, half, axis)


def _branch_columns(w):
    pad = lambda n: jnp.zeros(w.shape[:-1] + (n,), w.dtype)
    return jnp.concatenate([w[..., 0:896], pad(64), w[..., 896:928], pad(32), w[..., 928:3232]], axis=-1)


def _branch_columns_inverse(dw_br, dw_gl):
    return jnp.concatenate([dw_br[..., 0:896], dw_br[..., 960:992], dw_br[..., 1024:ZB], dw_gl], axis=-1)


def _uq_layout(w):
    r = w.reshape(w.shape[0], HEADS, QK_NOPE + QK_ROPE)
    return jnp.pad(r, ((0, 0), (0, 0), (0, HEAD_PAD - QK_NOPE - QK_ROPE))).reshape(w.shape[0], HEADS * HEAD_PAD)


def _uq_layout_inverse(dw):
    return dw.reshape(dw.shape[0], HEADS, HEAD_PAD)[:, :, :QK_NOPE + QK_ROPE].reshape(dw.shape[0], HEADS * (QK_NOPE + QK_ROPE))


def _ukv_layout(w):
    r = w.reshape(w.shape[0], HEADS, QK_NOPE + V_DIM)
    kp = jnp.pad(r[:, :, :QK_NOPE], ((0, 0), (0, 0), (0, HEAD_PAD - QK_NOPE))).reshape(w.shape[0], HEADS * HEAD_PAD)
    return jnp.concatenate([kp, r[:, :, QK_NOPE:].reshape(w.shape[0], HEADS * V_DIM)], axis=1)


def _ukv_layout_inverse(dw):
    n = dw.shape[0]
    dk = dw[:, :HEADS * HEAD_PAD].reshape(n, HEADS, HEAD_PAD)[:, :, :QK_NOPE]
    dv = dw[:, HEADS * HEAD_PAD:].reshape(n, HEADS, V_DIM)
    return jnp.concatenate([dk, dv], axis=2).reshape(n, HEADS * (QK_NOPE + V_DIM))


def _block_diag(pw):
    out = jnp.zeros((256, 256), pw.dtype)
    for g in range(4):
        out = lax.dynamic_update_slice(out, pw[g], (64 * g, 64 * g))
    return out


def _block_diag_inverse(d):
    return jnp.stack([d[64 * g:64 * (g + 1), 64 * g:64 * (g + 1)] for g in range(4)])


def _pad_rows(a, n):
    return jnp.pad(a, ((0, n - a.shape[0]), (0, 0)))


def _rope_tables(rows):
    inv = 1.0 / (ROPE_THETA ** (jnp.arange(0, QK_ROPE, 2, dtype=F32) / QK_ROPE))
    ang = jnp.arange(rows, dtype=F32)[:, None] * inv[None, :]
    cos, sin = jnp.cos(ang), jnp.sin(ang)
    one, zero = jnp.ones((rows, 1), F32), jnp.zeros((rows, 1), F32)
    rep = lambda a, n: jnp.broadcast_to(a, (rows, n))
    c = jnp.concatenate([rep(one, 64), cos, cos, rep(one, 32)], axis=1)
    s1 = jnp.concatenate([rep(zero, 64), -sin, rep(zero, 48)], axis=1)
    s2 = jnp.concatenate([rep(zero, 80), sin, rep(zero, 32)], axis=1)
    return jnp.concatenate([c, s1, s2], axis=1)


def gather_weights(shards, c):
    big = _pack([shards[n] for n, _, _ in SHARDED], BF16, COMM_ROWS)
    mine = chip_exchange(_take_half(big, c, 0), True, "gather_weights_ici")
    other = sibling_swap(mine, "gather_weights_d2d")
    full = _place_halves(mine, other, c, 1)
    small = chip_exchange(_pack([shards[n] for n, _, _ in SHARDED_SMALL], F32, 8), True, "gather_small_ici")
    out = {}
    for buf, table in ((full, SHARDED), (small, SHARDED_SMALL)):
        per_chip = [_unpack(buf[k], [s for _, s, _ in table]) for k in range(N_CHIPS)]
        for idx, (name, _, axis) in enumerate(table):
            out[name] = jnp.concatenate([per_chip[k][idx] for k in range(N_CHIPS)], axis=axis)
    return out


def reduce_sharded(grads, c):
    tables = SHARDED + SHARDED_SMALL
    per_chip = []
    for k in range(N_CHIPS):
        parts = []
        for name, shape, axis in tables:
            parts.append(lax.slice_in_dim(grads[name], k * shape[axis], (k + 1) * shape[axis], axis=axis))
        per_chip.append(_pack(parts, BF16, COMM_ROWS))
    p = jnp.stack(per_chip)
    theirs = sibling_swap(_take_half(p, 1 - c, 1), "reduce_grads_d2d")
    chip_sum = add_pair(_take_half(p, c, 1), theirs, BF16, "reduce_grads_pair")
    landed = chip_exchange(chip_sum, False, "reduce_grads_ici")
    mine = sum_slots(landed, "reduce_grads_sum")
    other = sibling_swap(mine, "reduce_grads_share")
    total = _place_halves(mine, other, c, 0)
    return dict(zip([n for n, _, _ in tables], _unpack(total, [s for _, s, _ in tables])))


def reduce_replicated(grads):
    buf = _pack([grads[n] for n, _ in REPLICATED], F32, 768)
    chip_sum = add_pair(buf, sibling_swap(buf, "reduce_small_d2d"), F32, "reduce_small_pair")
    total = sum_slots(chip_exchange(chip_sum, True, "reduce_small_ici"), "reduce_small_sum")
    return dict(zip([n for n, _ in REPLICATED], _unpack(total, [s for _, s in REPLICATED])))


def _layer_weights(w, i):
    w_in = w["w_in"][i]
    conf_w = jnp.pad(w["conf_dw_w"][i].astype(F32), ((0, 32 - CONF_K), (0, 0)))
    sc_w = jnp.pad(w["sc_dw_w"][i].astype(F32), ((0, 8 - SC_K), (0, 0)))
    conf_vec = jnp.concatenate([w["conf_dw_b"][i][None], w["conf_ln_g"][i][None], w["conf_ln_b"][i][None], jnp.zeros((5, 256), F32)])
    return dict(
        w_br=_branch_columns(w_in[:, :IN_W - ZG]).astype(BF16), w_gl=w_in[:, IN_W - ZG:].astype(BF16),
        gpre=w["pre_norm_g"][i][None], bias=w["gate_bias"][i][None], pwbd=_block_diag(w["pool_w"][i]).astype(BF16),
        pscale=w["pool_scale"][i][None], gq=w["q_norm_g"][i][None], wuq=_uq_layout(w["w_uq"][i]).astype(BF16),
        gkv=w["kv_norm_g"][i][None], wukv=_ukv_layout(w["w_ukv"][i]).astype(BF16), conf_w=conf_w, conf_vec=conf_vec, sc_w=sc_w,
        woa=w["w_out_pool"][i].astype(BF16), wob=w["w_out_mla"][i].astype(BF16), woc=w["w_out_conf"][i].astype(BF16),
        wod=w["w_out_sc"][i].astype(BF16), wo=w["w_o"][i].astype(BF16), gpost=w["post_norm_g"][i][None])


def local_step(x, target, w):
    seq = x.shape[0]
    length = N_META + seq
    rows = -(-length // ROW_TILE) * ROW_TILE
    bt = _big_tile(rows)
    hres = _pad_rows(jnp.concatenate([w["meta_tokens"].astype(F32), x], axis=0), rows)
    tgt = jnp.pad(target, ((N_META, rows - length), (0, 0)))
    rope = _rope_tables(rows)

    saved = []
    for i in range(DEPTH):
        lw = _layer_weights(w, i)
        z_br, hb = prenorm_project(hres, lw["gpre"], lw["w_br"])
        z_gl = matmul(hb, lw["w_gl"], "nn", BF16, bt, 1024, D_MODEL, "project_gates")
        ua, uc, ud, q, k, v = branches_fwd(z_br, rope, lw["pwbd"], lw["pscale"], lw["gq"], lw["wuq"], lw["gkv"], lw["wukv"],
                                           lw["conf_w"], lw["conf_vec"], lw["sc_w"])
        o_att, lse = attention_fwd(q, k, v)
        ub, mb, o, hnew = merge_fwd(ua, o_att, uc, ud, z_br, z_gl, lw["bias"], lw["woa"], lw["wob"], lw["woc"], lw["wod"], lw["wo"],
                                    lw["gpost"], hres)
        saved.append(dict(lw=lw, hres=hres, hb=hb, z_br=z_br, z_gl=z_gl, ua=ua, ub=ub, uc=uc, ud=ud, q=q, k=k, v=v, o_att=o_att,
                          lse=lse, mb=mb, o=o))
        hres = hnew

    dh, total = loss_head(hres, tgt, seq)

    g = {n: [None] * DEPTH for n in WEIGHT_ORDER if n != "meta_tokens"}
    for i in reversed(range(DEPTH)):
        s = saved[i]
        lw = s["lw"]
        dm, dwo, dgpost = postnorm_bwd(dh, s["o"], s["mb"], lw["wo"], lw["gpost"])
        dua, dub, duc, dud, dz_gl, dwa, dwb, dwc, dwd, dbias = merge_bwd(dm, s["ua"], s["ub"], s["uc"], s["ud"], s["z_gl"], lw["bias"],
                                                                     lw["woa"], lw["wob"], lw["woc"], lw["wod"])
        dz_pool, dpw, dps = pool_bwd(s["z_br"], dua, lw["pwbd"], lw["pscale"])
        dz_sc, dsw = shortconv_bwd(s["z_br"], dud, lw["sc_w"])
        dc, dcg, dcvec = conformer_bwd_tail(s["z_br"], duc, lw["conf_w"], lw["conf_vec"])
        dz_conf, dcw = conformer_bwd_conv(s["z_br"], dc, lw["conf_w"])
        do, dmg, delta = attention_bwd_prep(dub, s["o_att"], s["z_br"])
        dq, dk, dv = attention_bwd(s["q"], s["k"], s["v"], do, s["lse"], delta)
        dz_mla, dwuq, dwukv, dgq, dgkv = mla_prep_bwd(dq, dk, dv, s["z_br"], rope, lw["gq"], lw["wuq"], lw["gkv"], lw["wukv"])
        dz_br = jnp.concatenate([dz_pool, dz_mla, dmg, dz_conf, dcg, dz_sc], axis=1)
        dw_br = matmul(s["hb"], dz_br, "tn", F32, D_MODEL, ZB // 2, ROW_TILE, "grad_w_branch")
        dw_gl = matmul(s["hb"], dz_gl, "tn", F32, D_MODEL, 1024, ROW_TILE, "grad_w_gates")
        dh_gl = matmul(dz_gl, lw["w_gl"], "nt", F32, bt, D_MODEL, 1024, "grad_h_gates")
        dh, dgpre = prenorm_bwd(dz_br, lw["w_br"], dh_gl, s["hres"], lw["gpre"], dh)

        g["pre_norm_g"][i] = dgpre[0]
        g["w_in"][i] = _branch_columns_inverse(dw_br, dw_gl)
        g["gate_bias"][i] = dbias[0]
        g["pool_w"][i] = _block_diag_inverse(dpw)
        g["pool_scale"][i] = dps[0]
        g["w_out_pool"][i] = dwa
        g["q_norm_g"][i] = dgq[0]
        g["w_uq"][i] = _uq_layout_inverse(dwuq)
        g["kv_norm_g"][i] = dgkv[0]
        g["w_ukv"][i] = _ukv_layout_inverse(dwukv)
        g["w_out_mla"][i] = dwb
        g["conf_dw_w"][i] = dcw[:CONF_K]
        g["conf_dw_b"][i] = dcvec[2]
        g["conf_ln_g"][i] = dcvec[0]
        g["conf_ln_b"][i] = dcvec[1]
        g["w_out_conf"][i] = dwc
        g["sc_dw_w"][i] = dsw[:SC_K]
        g["w_out_sc"][i] = dwd
        g["w_o"][i] = dwo
        g["post_norm_g"][i] = dgpost[0]

    grads = {n: jnp.stack(parts) for n, parts in g.items()}
    grads["meta_tokens"] = dh[:N_META]
    return total[0, 0], dh[N_META:length], grads


def kernel(x, meta_tokens, pre_norm_g, w_in, gate_bias, pool_w, pool_scale, w_out_pool, q_norm_g, w_uq, kv_norm_g, w_ukv, w_out_mla, conf_dw_w, conf_dw_b, conf_ln_g, conf_ln_b, w_out_conf, sc_dw_w, w_out_sc, w_o, post_norm_g, loss_target, m_meta_tokens, m_pre_norm_g, m_w_in, m_gate_bias, m_pool_w, m_pool_scale, m_w_out_pool, m_q_norm_g, m_w_uq, m_kv_norm_g, m_w_ukv, m_w_out_mla, m_conf_dw_w, m_conf_dw_b, m_conf_ln_g, m_conf_ln_b, m_w_out_conf, m_sc_dw_w, m_w_out_sc, m_w_o, m_post_norm_g, v_meta_tokens, v_pre_norm_g, v_w_in, v_gate_bias, v_pool_w, v_pool_scale, v_w_out_pool, v_q_norm_g, v_w_uq, v_kv_norm_g, v_w_ukv, v_w_out_mla, v_conf_dw_w, v_conf_dw_b, v_conf_ln_g, v_conf_ln_b, v_w_out_conf, v_sc_dw_w, v_w_out_sc, v_w_o, v_post_norm_g):
    args = locals()
    weights = {n: args[n] for n in WEIGHT_ORDER}
    c = lax.axis_index("c")

    full = dict(weights)
    full.update(gather_weights(weights, c))
    total, dx, grads = local_step(x[0], loss_target[0], full)
    loss = lax.psum(total * (0.5 / D_MODEL), ("x", "y", "c"))

    reduced = reduce_sharded(grads, c)
    reduced.update(reduce_replicated(grads))

    deltas, new_m, new_v = [], [], []
    for n in WEIGHT_ORDER:
        d, nm, nv = adamw(weights[n], reduced[n], args["m_" + n], args["v_" + n])
        deltas.append(d)
        new_m.append(nm)
        new_v.append(nv)
    return (loss, dx[None], *[reduced[n] for n in WEIGHT_ORDER], *deltas, *new_m, *new_v)
```

```python
import functools
import math

import jax
import jax.numpy as jnp
from jax import lax
from jax.experimental import pallas as pl
from jax.experimental.pallas import tpu as pltpu

F32 = jnp.float32
BF16 = jnp.bfloat16

D_MODEL = 1024
DEPTH = 4
N_META = 16
EPS = 1e-6
HEADS = 8
QK_NOPE = 64
QK_ROPE = 32
V_DIM = 64
HEAD_PAD = 128
ROPE_THETA = 10000.0
Q_SCALE = (QK_NOPE + QK_ROPE) ** -0.5
CONF_K = 31
SC_K = 3
IN_W = 7328
N_CHIPS = 4

ZB = 3328
ZG = 4096
BG, C2, XV, SG, PV, PG, CQ, CKV, KR, MG, CA, CGT, CG = (0, 256, 512, 768, 1024, 1280, 1536, 1792, 1920, 2048, 2560, 2816, 3072)

ROW_TILE = 384
HALO = 32
LANES = 128
VMEM_LIMIT = 56 * 1024 * 1024

ADAM_LR = 0.001
ADAM_B1 = 0.9
ADAM_B2 = 0.999
ADAM_EPS = 1e-08
ADAM_WD = 0.01
ADAM_STEP = 10

MESH = pl.DeviceIdType.MESH
ANY = pl.BlockSpec(memory_space=pl.ANY)

MISC = (
    ("w_out_pool", 256), ("w_ukv", 128), ("w_out_mla", 512), ("w_out_conf", 256), ("w_out_sc", 256), ("w_o", 1024), ("w_uq", 256))
SHARDED_SMALL = (
    ("meta_tokens", (N_META, 256), 1),
    ("conf_dw_w", (DEPTH, CONF_K, 64), 2),
    ("sc_dw_w", (DEPTH, SC_K, 64), 2),
)
REPLICATED = (
    ("pre_norm_g", (DEPTH, D_MODEL)),
    ("gate_bias", (DEPTH, 4 * D_MODEL)),
    ("pool_w", (DEPTH, 4, 64, 64)),
    ("pool_scale", (DEPTH, 256)),
    ("q_norm_g", (DEPTH, 256)),
    ("kv_norm_g", (DEPTH, 128)),
    ("conf_dw_b", (DEPTH, 256)),
    ("conf_ln_g", (DEPTH, 256)),
    ("conf_ln_b", (DEPTH, 256)),
    ("post_norm_g", (DEPTH, D_MODEL)),
)
WEIGHT_ORDER = ("meta_tokens", "pre_norm_g", "w_in", "gate_bias", "pool_w", "pool_scale", "w_out_pool", "q_norm_g", "w_uq",
                "kv_norm_g", "w_ukv", "w_out_mla", "conf_dw_w", "conf_dw_b", "conf_ln_g", "conf_ln_b", "w_out_conf", "sc_dw_w",
                "w_out_sc", "w_o", "post_norm_g")


def _dot(a, b):
    return lax.dot_general(a, b, (((1,), (0,)), ((), ())), preferred_element_type=F32)


def _dot_nt(a, b):
    return lax.dot_general(a, b, (((1,), (1,)), ((), ())), preferred_element_type=F32)


def _dot_tn(a, b):
    return lax.dot_general(a, b, (((0,), (0,)), ((), ())), preferred_element_type=F32)


def _sigmoid(x):
    return jax.nn.sigmoid(x)


def _silu(x):
    return x * _sigmoid(x)


def _silu_grad(x):
    s = _sigmoid(x)
    return s * (1.0 + x * (1.0 - s))


def _rms(x, g):
    return x * lax.rsqrt(jnp.mean(x * x, axis=-1, keepdims=True) + EPS) * g


def _sh(x, d):
    return x if d == 0 else pltpu.roll(x, d, 0)


def _ash(x, d):
    return x if d == 0 else pltpu.roll(x, x.shape[0] - d, 0)


def _lanes8(t):
    return jnp.concatenate([t] * HEADS, axis=1)


def _pool_window_sums(v, shift):
    a2 = v + shift(v, 1)
    a4 = a2 + shift(a2, 2)
    a8 = a4 + shift(a4, 4)
    a16 = a8 + shift(a8, 8)
    lane = lax.broadcasted_iota(jnp.int32, v.shape, 1)
    return jnp.where(lane < 64, a2, jnp.where(lane < 128, a4, jnp.where(lane < 192, a8, a16)))


def _pool_counts(first_row, rows):
    pos = first_row + lax.broadcasted_iota(jnp.int32, (rows, 256), 0)
    lane = lax.broadcasted_iota(jnp.int32, (rows, 256), 1)
    width = jnp.where(lane < 64, 2, jnp.where(lane < 128, 4, jnp.where(lane < 192, 8, 16)))
    return jnp.maximum(jnp.minimum(pos + 1, width), 1).astype(F32)


def _params(sem=None):
    return pltpu.CompilerParams(dimension_semantics=sem, vmem_limit_bytes=VMEM_LIMIT)


def _tile_specs(t, n_halo_blocks):
    per = t // HALO

    def cur(c, cb=0):
        return pl.BlockSpec((t, c), lambda i: (i, cb))

    def prev(c, cb=0):
        return pl.BlockSpec((HALO, c), lambda i: (jnp.maximum(i * per - 1, 0), cb))

    def nxt(c, cb=0):
        return pl.BlockSpec((HALO, c), lambda i: (jnp.minimum((i + 1) * per, n_halo_blocks - 1), cb))

    def full(shape):
        return pl.BlockSpec(shape, lambda i: (0,) * len(shape))

    return cur, prev, nxt, full


def _big_tile(rows):
    return rows // 3 if rows % (3 * LANES) == 0 else ROW_TILE


def matmul(a, b, mode, out_dtype, tm, tn, tk, name):
    if mode == "nn":
        (m, k), n = a.shape, b.shape[1]
        a_spec = pl.BlockSpec((tm, tk), lambda i, j, kk: (i, kk))
        b_spec = pl.BlockSpec((tk, tn), lambda i, j, kk: (kk, j))
        dot = _dot
    elif mode == "nt":
        (m, k), n = a.shape, b.shape[0]
        a_spec = pl.BlockSpec((tm, tk), lambda i, j, kk: (i, kk))
        b_spec = pl.BlockSpec((tn, tk), lambda i, j, kk: (j, kk))
        dot = _dot_nt
    else:
        (k, m), n = a.shape, b.shape[1]
        a_spec = pl.BlockSpec((tk, tm), lambda i, j, kk: (kk, i))
        b_spec = pl.BlockSpec((tk, tn), lambda i, j, kk: (kk, j))
        dot = _dot_tn
    assert m % tm == 0 and n % tn == 0 and k % tk == 0, (a.shape, b.shape, tm, tn, tk)
    nk = k // tk

    def body(a_ref, b_ref, o_ref, acc_ref):
        kk = pl.program_id(2)

        @pl.when(kk == 0)
        def _():
            acc_ref[...] = jnp.zeros_like(acc_ref)

        acc_ref[...] += dot(a_ref[...], b_ref[...])

        @pl.when(kk == nk - 1)
        def _():
            o_ref[...] = acc_ref[...].astype(out_dtype)

    return pl.pallas_call(
        body, name=name, grid=(m // tm, n // tn, nk), in_specs=[a_spec, b_spec],
        out_specs=pl.BlockSpec((tm, tn), lambda i, j, kk: (i, j)), out_shape=jax.ShapeDtypeStruct((m, n), out_dtype),
        scratch_shapes=[pltpu.VMEM((tm, tn), F32)], compiler_params=_params(("parallel", "parallel", "arbitrary")),
    )(a, b)


def prenorm_project(hres, g, w):
    rows, d = hres.shape
    n = w.shape[1]
    tm, tn = _big_tile(rows), n // 2

    def body(x_ref, g_ref, w_ref, z_ref, hb_ref):
        @pl.when(pl.program_id(1) == 0)
        def _():
            hb_ref[...] = _rms(x_ref[...], g_ref[...]).astype(BF16)

        z_ref[...] = _dot(hb_ref[...], w_ref[...]).astype(BF16)

    return pl.pallas_call(
        body, name="prenorm_project", grid=(rows // tm, n // tn),
        in_specs=[pl.BlockSpec((tm, d), lambda i, j: (i, 0)), pl.BlockSpec((1, d), lambda i, j: (0, 0)),
                  pl.BlockSpec((d, tn), lambda i, j: (0, j))],
        out_specs=[pl.BlockSpec((tm, tn), lambda i, j: (i, j)), pl.BlockSpec((tm, d), lambda i, j: (i, 0))],
        out_shape=[jax.ShapeDtypeStruct((rows, n), BF16), jax.ShapeDtypeStruct((rows, d), BF16)],
        compiler_params=_params(("parallel", "arbitrary")),
    )(hres, g, w)


def _rope(q, c, s1, s2, width):
    return q * c + pltpu.roll(q, width - 16, 1) * s1 + pltpu.roll(q, 16, 1) * s2


def _rope_transposed(dq, c, s1, s2, width):
    return dq * c + pltpu.roll(dq * s1, 16, 1) + pltpu.roll(dq * s2, width - 16, 1)


def _conf_conv(g1, w_ref):
    acc = jnp.zeros_like(g1)
    for k in range(CONF_K):
        acc = acc + w_ref[k:k + 1, :] * _sh(g1, CONF_K - 1 - k)
    return acc


def _conf_tail(c, cg, lg, lb):
    mu = jnp.mean(c, axis=-1, keepdims=True)
    xc = c - mu
    var = jnp.mean(xc * xc, axis=-1, keepdims=True)
    n = xc * lax.rsqrt(var + EPS) * lg + lb
    return _silu(n) * _silu(cg)


def branches_fwd(z_br, rope, pwbd, pscale, gq, wuq, gkv, wukv, conf_w, conf_vec, sc_w):
    rows = z_br.shape[0]
    t = ROW_TILE
    cur, prev, _, full = _tile_specs(t, rows // HALO)

    def body(zc_ref, zp_ref, rope_ref, pw_ref, ps_ref, gq_ref, wuq_ref, gkv_ref, wukv_ref, cw_ref, cv_ref, sw_ref,
             ua_ref, uc_ref, ud_ref, q_ref, k_ref, v_ref):
        i = pl.program_id(0)
        zp = jnp.where(i == 0, jnp.zeros(zp_ref.shape, zp_ref.dtype), zp_ref[...])

        def ext(lo, w=256):
            return jnp.concatenate([zp[:, lo:lo + w], zc_ref[:, lo:lo + w]], axis=0).astype(F32)

        def col(lo, w=256):
            return zc_ref[:, lo:lo + w].astype(F32)

        v = ext(PV)
        p = (_pool_window_sums(v, _sh) / _pool_counts(i * t - HALO, t + HALO) - v)[HALO:]
        ya = _dot(p.astype(BF16), pw_ref[...]) * ps_ref[...]
        ua_ref[...] = (ya * _silu(col(PG))).astype(BF16)

        g1 = ext(CA) * _sigmoid(ext(CGT))
        c = _conf_conv(g1, cw_ref)[HALO:] + cv_ref[0:1, :]
        uc_ref[...] = _conf_tail(c, col(CG), cv_ref[1:2, :], cv_ref[2:3, :]).astype(BF16)

        e = ext(C2) * ext(XV)
        f = jnp.zeros_like(e)
        for k in range(SC_K):
            f = f + sw_ref[k:k + 1, :] * _sh(e, SC_K - 1 - k)
        ud_ref[...] = (col(BG) * f[HALO:] * _silu(col(SG))).astype(BF16)

        cth, s1, s2 = rope_ref[:, 0:128], rope_ref[:, 128:256], rope_ref[:, 256:384]
        qn = _rms(col(CQ), gq_ref[...]).astype(BF16)
        q = _dot(qn, wuq_ref[...])
        w8 = HEADS * HEAD_PAD
        q_ref[...] = (_rope(q, _lanes8(cth), _lanes8(s1), _lanes8(s2), w8) * Q_SCALE).astype(BF16)
        kvn = _rms(col(CKV, 128), gkv_ref[...]).astype(BF16)
        kv = _dot(kvn, wukv_ref[...])
        kr = _rope(col(KR, 128), cth, s1, s2, HEAD_PAD)
        k_ref[...] = (kv[:, :w8] + _lanes8(kr)).astype(BF16)
        v_ref[...] = kv[:, w8:].astype(BF16)

    outs = [jax.ShapeDtypeStruct((rows, 256), BF16)] * 3 + [jax.ShapeDtypeStruct((rows, 1024), BF16)] * 2 + [
        jax.ShapeDtypeStruct((rows, 512), BF16)]
    return pl.pallas_call(
        body, name="branches_fwd", grid=(rows // t,),
        in_specs=[cur(ZB), prev(ZB), cur(384), full((256, 256)), full((1, 256)), full((1, 256)), full((256, 1024)),
                  full((1, 128)), full((128, 1536)), full((32, 256)), full((8, 256)), full((8, 256))],
        out_specs=[cur(256), cur(256), cur(256), cur(1024), cur(1024), cur(512)], out_shape=outs,
        compiler_params=_params(("parallel",)),
    )(z_br, z_br, rope, pwbd, pscale, gq, wuq, gkv, wukv, conf_w, conf_vec, sc_w)


def _head_lane_mask(h):
    lane = lax.broadcasted_iota(jnp.int32, (1, 2 * V_DIM), 1)
    return (lane >= V_DIM * h) & (lane < V_DIM * (h + 1))


def attention_fwd(q, k, v):
    rows = q.shape[0]
    tq = ROW_TILE
    nq = rows // tq

    def body(q_ref, k_ref, v_ref, o_ref, lse_ref):
        i = pl.program_id(1)
        row = lax.broadcasted_iota(jnp.int32, (tq, tq), 0)
        colm = lax.broadcasted_iota(jnp.int32, (tq, tq), 1)

        def head_step(h, j, carry, diagonal):
            m, l, acc = carry
            r0 = pl.multiple_of(j * tq, tq)
            kh = k_ref[pl.ds(r0, tq), HEAD_PAD * h:HEAD_PAD * (h + 1)]
            vh = jnp.where(_head_lane_mask(h), v_ref[pl.ds(r0, tq), :], jnp.zeros((), BF16))
            s = _dot_nt(q_ref[:, HEAD_PAD * h:HEAD_PAD * (h + 1)], kh)
            if diagonal:
                s = jnp.where(colm <= row, s, -1e30)
            m2 = jnp.maximum(m, jnp.max(s, axis=-1, keepdims=True))
            alpha = jnp.exp(m - m2)
            pr = jnp.exp(s - m2)
            return m2, alpha * l + jnp.sum(pr, axis=-1, keepdims=True), alpha * acc + _dot(pr.astype(BF16), vh)

        def step(j, carry, diagonal):
            return tuple(head_step(h, j, carry[h], diagonal) for h in range(2))

        init = (jnp.full((tq, 1), -1e30, F32), jnp.zeros((tq, 1), F32), jnp.zeros((tq, 2 * V_DIM), F32))
        carry = lax.fori_loop(0, i, lambda j, cr: step(j, cr, False), (init, init))
        out = jnp.zeros((tq, 2 * V_DIM), F32)
        for h, (m, l, acc) in enumerate(step(i, carry, True)):
            out = out + acc / l
            lse_ref[h] = jnp.broadcast_to(m + jnp.log(l), (tq, LANES))
        o_ref[...] = out.astype(BF16)

    return pl.pallas_call(
        body, name="attention_fwd", grid=(HEADS // 2, nq),
        in_specs=[pl.BlockSpec((tq, 2 * HEAD_PAD), lambda p, i: (i, p)), pl.BlockSpec((rows, 2 * HEAD_PAD), lambda p, i: (0, p)),
                  pl.BlockSpec((rows, 2 * V_DIM), lambda p, i: (0, p))],
        out_specs=[pl.BlockSpec((tq, 2 * V_DIM), lambda p, i: (i, p)), pl.BlockSpec((2, tq, LANES), lambda p, i: (p, i, 0))],
        out_shape=[jax.ShapeDtypeStruct((rows, HEADS * V_DIM), BF16), jax.ShapeDtypeStruct((HEADS, rows, LANES), F32)],
        compiler_params=_params(("parallel", "parallel")),
    )(q, k, v)


def merge_fwd(ua, o_att, uc, ud, z_br, z_gl, bias, woa, wob, woc, wod, wo, gpost, hres):
    rows = hres.shape[0]
    t = ROW_TILE
    cur, _, _, full = _tile_specs(t, rows // HALO)
    d = D_MODEL

    def body(ua_ref, ob_ref, uc_ref, ud_ref, mg_ref, gl_ref, b_ref, woa_ref, wob_ref, woc_ref, wod_ref, wo_ref, gp_ref, h_ref,
             ub_ref, mb_ref, o_ref, hn_ref):
        ub = (ob_ref[...].astype(F32) * _silu(mg_ref[...].astype(F32))).astype(BF16)
        ub_ref[...] = ub
        m = jnp.zeros((t, d), F32)
        for idx, (u, w_ref) in enumerate(((ua_ref[...], woa_ref), (ub, wob_ref), (uc_ref[...], woc_ref), (ud_ref[...], wod_ref))):
            gate = _sigmoid(gl_ref[:, d * idx:d * (idx + 1)].astype(F32) + b_ref[:, d * idx:d * (idx + 1)])
            m = m + gate * _dot(u, w_ref[...])
        mb = m.astype(BF16)
        mb_ref[...] = mb
        o = _dot(mb, wo_ref[...])
        o_ref[...] = o
        hn_ref[...] = h_ref[...] + _rms(o, gp_ref[...])

    return pl.pallas_call(
        body, name="merge_fwd", grid=(rows // t,),
        in_specs=[cur(256), cur(512), cur(256), cur(256), cur(512, MG // 512), cur(ZG), full((1, ZG)), full((256, d)), full((512, d)),
                  full((256, d)), full((256, d)), full((d, d)), full((1, d)), cur(d)],
        out_specs=[cur(512), cur(d), cur(d), cur(d)],
        out_shape=[jax.ShapeDtypeStruct((rows, 512), BF16), jax.ShapeDtypeStruct((rows, d), BF16), jax.ShapeDtypeStruct((rows, d), F32),
                   jax.ShapeDtypeStruct((rows, d), F32)],
        compiler_params=_params(("parallel",)),
    )(ua, o_att, uc, ud, z_br, z_gl, bias, woa, wob, woc, wod, wo, gpost, hres)


def loss_head(hres, target, n_tokens):
    rows, d = hres.shape
    t = ROW_TILE
    cur, _, _, full = _tile_specs(t, rows // HALO)
    n_steps = rows // t

    def body(h_ref, t_ref, dh_ref, tot_ref, acc_ref):
        i = pl.program_id(0)

        @pl.when(i == 0)
        def _():
            acc_ref[...] = jnp.zeros_like(acc_ref)

        r = i * t + lax.broadcasted_iota(jnp.int32, (t, 1), 0)
        diff = jnp.where((r >= N_META) & (r < N_META + n_tokens), h_ref[...] - t_ref[...], 0.0)
        dh_ref[...] = diff * (1.0 / d)
        acc_ref[...] += jnp.sum(diff * diff, axis=0, keepdims=True)

        @pl.when(i == n_steps - 1)
        def _():
            tot_ref[...] = jnp.broadcast_to(jnp.sum(acc_ref[...], axis=1, keepdims=True), (1, LANES))

    return pl.pallas_call(
        body, name="loss_head", grid=(n_steps,), in_specs=[cur(d), cur(d)], out_specs=[cur(d), full((1, LANES))],
        out_shape=[jax.ShapeDtypeStruct((rows, d), F32), jax.ShapeDtypeStruct((1, LANES), F32)],
        scratch_shapes=[pltpu.VMEM((1, d), F32)], compiler_params=_params(("arbitrary",)),
    )(hres, target)


def _accumulate(i, ref, value):
    @pl.when(i == 0)
    def _():
        ref[...] = value

    @pl.when(i > 0)
    def _():
        ref[...] += value


def postnorm_bwd(dh, o, mb, wo, gpost):
    rows, d = dh.shape
    t = ROW_TILE
    cur, _, _, full = _tile_specs(t, rows // HALO)

    def body(dh_ref, o_ref, mb_ref, wo_ref, gp_ref, dm_ref, dwo_ref, dgp_ref):
        i = pl.program_id(0)
        _, vjp = jax.vjp(_rms, o_ref[...], gp_ref[...])
        do, dg = vjp(dh_ref[...])
        dob = do.astype(BF16)
        dm_ref[...] = _dot_nt(dob, wo_ref[...])
        _accumulate(i, dwo_ref, _dot_tn(mb_ref[...], dob))
        _accumulate(i, dgp_ref, dg)

    return pl.pallas_call(
        body, name="postnorm_bwd", grid=(rows // t,), in_specs=[cur(d), cur(d), cur(d), full((d, d)), full((1, d))],
        out_specs=[cur(d), full((d, d)), full((1, d))],
        out_shape=[jax.ShapeDtypeStruct((rows, d), F32), jax.ShapeDtypeStruct((d, d), F32), jax.ShapeDtypeStruct((1, d), F32)],
        compiler_params=_params(("arbitrary",)),
    )(dh, o, mb, wo, gpost)


def merge_bwd(dm, ua, ub, uc, ud, z_gl, bias, woa, wob, woc, wod):
    rows, d = dm.shape
    t = ROW_TILE
    cur, _, _, full = _tile_specs(t, rows // HALO)
    widths = (256, 512, 256, 256)

    def body(dm_ref, ua_ref, ub_ref, uc_ref, ud_ref, gl_ref, b_ref, woa_ref, wob_ref, woc_ref, wod_ref,
             dua_ref, dub_ref, duc_ref, dud_ref, dgl_ref, dwa_ref, dwb_ref, dwc_ref, dwd_ref, db_ref):
        i = pl.program_id(0)
        dm = dm_ref[...]
        groups = ((ua_ref, woa_ref, dua_ref, dwa_ref), (ub_ref, wob_ref, dub_ref, dwb_ref), (uc_ref, woc_ref, duc_ref, dwc_ref),
                  (ud_ref, wod_ref, dud_ref, dwd_ref))
        for idx, (u_ref, w_ref, du_ref, dw_ref) in enumerate(groups):
            cols = slice(d * idx, d * (idx + 1))
            u = u_ref[...]
            gate = _sigmoid(gl_ref[:, cols].astype(F32) + b_ref[:, cols])
            dgl = dm * _dot(u, w_ref[...]) * gate * (1.0 - gate)
            dgl_ref[:, cols] = dgl.astype(BF16)
            _accumulate(i, db_ref.at[:, cols], jnp.sum(dgl, axis=0, keepdims=True))
            dyb = (dm * gate).astype(BF16)
            du_ref[...] = _dot_nt(dyb, w_ref[...])
            _accumulate(i, dw_ref, _dot_tn(u, dyb))

    return pl.pallas_call(
        body, name="merge_bwd", grid=(rows // t,),
        in_specs=[cur(d), cur(256), cur(512), cur(256), cur(256), cur(ZG), full((1, ZG))] + [full((w, d)) for w in widths],
        out_specs=[cur(256), cur(512), cur(256), cur(256), cur(ZG)] + [full((w, d)) for w in widths] + [full((1, ZG))],
        out_shape=[jax.ShapeDtypeStruct((rows, w), F32) for w in widths] + [jax.ShapeDtypeStruct((rows, ZG), BF16)] + [
            jax.ShapeDtypeStruct((w, d), F32) for w in widths] + [jax.ShapeDtypeStruct((1, ZG), F32)],
        compiler_params=_params(("arbitrary",)),
    )(dm, ua, ub, uc, ud, z_gl, bias, woa, wob, woc, wod)


def pool_bwd(z_br, dua, pwbd, pscale, dz_buf):
    rows = z_br.shape[0]
    t = ROW_TILE
    n_steps = rows // t
    cur, prev, nxt, full = _tile_specs(t, rows // HALO)

    def body(zc_ref, zp_ref, zn_ref, dc_ref, dn_ref, pw_ref, ps_ref, _, dz_ref, dpw_ref, dps_ref):
        i = pl.program_id(0)
        zp = jnp.where(i == 0, jnp.zeros(zp_ref.shape, zp_ref.dtype), zp_ref[...])
        zn = jnp.where(i == n_steps - 1, jnp.zeros(zn_ref.shape, zn_ref.dtype), zn_ref[...])
        dun = jnp.where(i == n_steps - 1, jnp.zeros(dn_ref.shape, dn_ref.dtype), dn_ref[...])

        def ext(lo):
            return jnp.concatenate([zp[:, lo:lo + 256], zc_ref[:, lo:lo + 256], zn[:, lo:lo + 256]], axis=0).astype(F32)

        n_ext = t + 2 * HALO
        v, pg = ext(PV), ext(PG)
        cnt = _pool_counts(i * t - HALO, n_ext)
        p = (_pool_window_sums(v, _sh) / cnt - v)[HALO:HALO + t]
        du = jnp.concatenate([jnp.zeros((HALO, 256), F32), dc_ref[...], dun], axis=0)
        dya = du * _silu(pg)
        dypb = (dya * ps_ref[...]).astype(BF16)
        dp = _dot_nt(dypb, pw_ref[...])
        dv = (_pool_window_sums(dp / cnt, _ash) - dp)[HALO:HALO + t]
        pb = p.astype(BF16)
        pw = _dot(pb, pw_ref[...])
        duc, pgc = dc_ref[...], pg[HALO:HALO + t]
        dpg = duc * pw * ps_ref[...] * _silu_grad(pgc)
        dz_ref[...] = jnp.concatenate([dv, dpg], axis=1).astype(BF16)
        _accumulate(i, dpw_ref, _dot_tn(pb, dypb[HALO:HALO + t]))
        _accumulate(i, dps_ref, jnp.sum(dya[HALO:HALO + t] * pw, axis=0, keepdims=True))

    return pl.pallas_call(
        body, name="pool_bwd", grid=(n_steps,),
        in_specs=[cur(ZB), prev(ZB), nxt(ZB), cur(256), nxt(256), full((256, 256)), full((1, 256)), ANY],
        out_specs=[cur(512, PV // 512), full((256, 256)), full((1, 256))],
        out_shape=[jax.ShapeDtypeStruct((rows, ZB), BF16), jax.ShapeDtypeStruct((256, 256), F32), jax.ShapeDtypeStruct((1, 256), F32)],
        input_output_aliases={7: 0}, compiler_params=_params(("arbitrary",)),
    )(z_br, z_br, z_br, dua, dua, pwbd, pscale, dz_buf)


def shortconv_bwd(z_br, dud, sc_w, dz_buf):
    rows = z_br.shape[0]
    t = ROW_TILE
    n_steps = rows // t
    cur, prev, nxt, full = _tile_specs(t, rows // HALO)

    def body(zc_ref, zp_ref, zn_ref, dc_ref, dn_ref, sw_ref, _, dz_ref, dw_ref):
        i = pl.program_id(0)
        zp = jnp.where(i == 0, jnp.zeros(zp_ref.shape, zp_ref.dtype), zp_ref[...])
        zn = jnp.where(i == n_steps - 1, jnp.zeros(zn_ref.shape, zn_ref.dtype), zn_ref[...])
        dun = jnp.where(i == n_steps - 1, jnp.zeros(dn_ref.shape, dn_ref.dtype), dn_ref[...])

        def ext(lo):
            return jnp.concatenate([zp[:, lo:lo + 256], zc_ref[:, lo:lo + 256], zn[:, lo:lo + 256]], axis=0).astype(F32)

        mid = slice(HALO, HALO + t)
        bg, c2, xv, sg = ext(BG), ext(C2), ext(XV), ext(SG)
        du = jnp.concatenate([jnp.zeros((HALO, 256), F32), dc_ref[...], dun], axis=0)
        e = c2 * xv
        shifted = [_sh(e, SC_K - 1 - k) for k in range(SC_K)]
        f = sum(sw_ref[k:k + 1, :] * shifted[k] for k in range(SC_K))
        gate = _silu(sg)
        df = du * gate * bg
        de = sum(sw_ref[k:k + 1, :] * _ash(df, SC_K - 1 - k) for k in range(SC_K))
        dbg = du * gate * f
        dsg = du * bg * f * _silu_grad(sg)
        dz_ref[...] = jnp.concatenate([dbg[mid], (de * xv)[mid], (de * c2)[mid], dsg[mid]], axis=1).astype(BF16)
        dw = jnp.concatenate([jnp.sum((df * shifted[k])[mid], axis=0, keepdims=True) for k in range(SC_K)] + [
            jnp.zeros((8 - SC_K, 256), F32)], axis=0)
        _accumulate(i, dw_ref, dw)

    return pl.pallas_call(
        body, name="shortconv_bwd", grid=(n_steps,), in_specs=[cur(ZB), prev(ZB), nxt(ZB), cur(256), nxt(256), full((8, 256)), ANY],
        out_specs=[cur(1024, BG // 1024), full((8, 256))],
        out_shape=[jax.ShapeDtypeStruct((rows, ZB), BF16), jax.ShapeDtypeStruct((8, 256), F32)],
        input_output_aliases={6: 0}, compiler_params=_params(("arbitrary",)),
    )(z_br, z_br, z_br, dud, dud, sc_w, dz_buf)


def conformer_bwd_tail(z_br, duc, conf_w, conf_vec, dz_buf):
    rows = z_br.shape[0]
    t = ROW_TILE
    cur, prev, _, full = _tile_specs(t, rows // HALO)

    def body(zc_ref, zp_ref, du_ref, cw_ref, cv_ref, _, dc_ref, dcg_ref, dv_ref):
        i = pl.program_id(0)
        zp = jnp.where(i == 0, jnp.zeros(zp_ref.shape, zp_ref.dtype), zp_ref[...])

        def ext(lo):
            return jnp.concatenate([zp[:, lo:lo + 256], zc_ref[:, lo:lo + 256]], axis=0).astype(F32)

        g1 = ext(CA) * _sigmoid(ext(CGT))
        c = _conf_conv(g1, cw_ref)[HALO:] + cv_ref[0:1, :]
        _, vjp = jax.vjp(_conf_tail, c, zc_ref[:, CG:CG + 256].astype(F32), cv_ref[1:2, :], cv_ref[2:3, :])
        dc, dcg, dlg, dlb = vjp(du_ref[...])
        dc_ref[...] = dc
        dcg_ref[...] = dcg.astype(BF16)
        dvec = jnp.concatenate([dlg, dlb, jnp.sum(dc, axis=0, keepdims=True), jnp.zeros((5, 256), F32)], axis=0)
        _accumulate(i, dv_ref, dvec)

    return pl.pallas_call(
        body, name="conformer_bwd_tail", grid=(rows // t,), in_specs=[cur(ZB), prev(ZB), cur(256), full((32, 256)), full((8, 256)), ANY],
        out_specs=[cur(256), cur(256, CG // 256), full((8, 256))],
        out_shape=[jax.ShapeDtypeStruct((rows, 256), F32), jax.ShapeDtypeStruct((rows, ZB), BF16), jax.ShapeDtypeStruct((8, 256), F32)],
        input_output_aliases={5: 1}, compiler_params=_params(("arbitrary",)),
    )(z_br, z_br, duc, conf_w, conf_vec, dz_buf)


def conformer_bwd_conv(z_br, dc, conf_w, dz_buf):
    rows = z_br.shape[0]
    t = ROW_TILE
    n_steps = rows // t
    cur, prev, nxt, full = _tile_specs(t, rows // HALO)

    def body(zc_ref, zp_ref, dc_ref, dn_ref, cw_ref, _, dz_ref, dw_ref):
        i = pl.program_id(0)
        zp = jnp.where(i == 0, jnp.zeros(zp_ref.shape, zp_ref.dtype), zp_ref[...])
        dcn = jnp.where(i == n_steps - 1, jnp.zeros(dn_ref.shape, dn_ref.dtype), dn_ref[...])

        def ext(lo):
            return jnp.concatenate([zp[:, lo:lo + 256], zc_ref[:, lo:lo + 256]], axis=0).astype(F32)

        a, gt = ext(CA), ext(CGT)
        sg = _sigmoid(gt)
        g1 = a * sg
        dc = dc_ref[...]
        dce = jnp.concatenate([dc, dcn], axis=0)
        dg1 = jnp.zeros_like(dce)
        dws = []
        for k in range(CONF_K):
            dg1 = dg1 + cw_ref[k:k + 1, :] * _ash(dce, CONF_K - 1 - k)
            dws.append(jnp.sum(dc * _sh(g1, CONF_K - 1 - k)[HALO:], axis=0, keepdims=True))
        dg1 = dg1[:t]
        ac, sc = a[HALO:], sg[HALO:]
        dz_ref[...] = jnp.concatenate([dg1 * sc, dg1 * ac * sc * (1.0 - sc)], axis=1).astype(BF16)
        _accumulate(i, dw_ref, jnp.concatenate(dws + [jnp.zeros((32 - CONF_K, 256), F32)], axis=0))

    return pl.pallas_call(
        body, name="conformer_bwd_conv", grid=(n_steps,), in_specs=[cur(ZB), prev(ZB), cur(256), nxt(256), full((32, 256)), ANY],
        out_specs=[cur(512, CA // 512), full((32, 256))],
        out_shape=[jax.ShapeDtypeStruct((rows, ZB), BF16), jax.ShapeDtypeStruct((32, 256), F32)],
        input_output_aliases={5: 0}, compiler_params=_params(("arbitrary",)),
    )(z_br, z_br, dc, dc, conf_w, dz_buf)


def attention_bwd_prep(dub, o_att, z_br, dz_buf):
    rows = dub.shape[0]
    t = ROW_TILE
    cur, _, _, _ = _tile_specs(t, rows // HALO)

    def body(du_ref, o_ref, mg_ref, _, do_ref, dmg_ref, delta_ref):
        du, o, mg = du_ref[...], o_ref[...].astype(F32), mg_ref[...].astype(F32)
        do = du * _silu(mg)
        do_ref[...] = do.astype(BF16)
        dmg_ref[...] = (du * o * _silu_grad(mg)).astype(BF16)
        prod = do * o
        lane = lax.broadcasted_iota(jnp.int32, (1, HEADS * V_DIM), 1)
        for h in range(HEADS):
            part = jnp.where((lane >= V_DIM * h) & (lane < V_DIM * (h + 1)), prod, 0.0)
            delta_ref[h] = jnp.broadcast_to(jnp.sum(part, axis=-1, keepdims=True), (t, LANES))

    return pl.pallas_call(
        body, name="attention_bwd_prep", grid=(rows // t,), in_specs=[cur(512), cur(512), cur(512, MG // 512), ANY],
        out_specs=[cur(512), cur(512, MG // 512), pl.BlockSpec((HEADS, t, LANES), lambda i: (0, i, 0))],
        out_shape=[jax.ShapeDtypeStruct((rows, 512), BF16), jax.ShapeDtypeStruct((rows, ZB), BF16),
                   jax.ShapeDtypeStruct((HEADS, rows, LANES), F32)],
        input_output_aliases={3: 1}, compiler_params=_params(("parallel",)),
    )(dub, o_att, z_br, dz_buf)


def attention_bwd(q, k, v, do, lse, delta):
    rows = q.shape[0]
    tq = ROW_TILE
    nq = rows // tq

    def body(q_ref, k_ref, v_ref, do_ref, lse_ref, dl_ref, dq_ref, dk_ref, dv_ref):
        j = pl.program_id(1)

        @pl.when(j == 0)
        def _():
            dq_ref[...] = jnp.zeros_like(dq_ref)

        row = lax.broadcasted_iota(jnp.int32, (tq, tq), 0)
        colm = lax.broadcasted_iota(jnp.int32, (tq, tq), 1)

        def head_step(h, i, dk, dv, diagonal):
            lanes = slice(HEAD_PAD * h, HEAD_PAD * (h + 1))
            hm = _head_lane_mask(h)
            kh = k_ref[:, lanes]
            vh = jnp.where(hm, v_ref[...], jnp.zeros((), BF16))
            r0 = pl.multiple_of(i * tq, tq)
            qi = q_ref[pl.ds(r0, tq), lanes]
            doi = jnp.where(hm, do_ref[pl.ds(r0, tq), :], jnp.zeros((), BF16))
            s = _dot_nt(qi, kh)
            if diagonal:
                s = jnp.where(colm <= row, s, -1e30)
            pr = jnp.exp(s - lse_ref[h, pl.ds(r0, tq), :][:, 0:1])
            dv = dv + _dot_tn(pr.astype(BF16), doi)
            dp = _dot_nt(doi, vh)
            ds = (pr * (dp - dl_ref[h, pl.ds(r0, tq), :][:, 0:1])).astype(BF16)
            dq_ref[pl.ds(r0, tq), lanes] += _dot(ds, kh)
            return dk + _dot_tn(ds, qi), dv

        def step(i, carry, diagonal):
            dk0, dk1, dv = carry
            dk0, dv = head_step(0, i, dk0, dv, diagonal)
            dk1, dv = head_step(1, i, dk1, dv, diagonal)
            return dk0, dk1, dv

        zero = jnp.zeros((tq, HEAD_PAD), F32)
        carry = step(j, (zero, zero, jnp.zeros((tq, 2 * V_DIM), F32)), True)
        dk0, dk1, dv = lax.fori_loop(j + 1, nq, lambda i, cr: step(i, cr, False), carry)
        dk_ref[:, 0:HEAD_PAD] = dk0
        dk_ref[:, HEAD_PAD:2 * HEAD_PAD] = dk1
        dv_ref[...] = dv

    return pl.pallas_call(
        body, name="attention_bwd", grid=(HEADS // 2, nq),
        in_specs=[pl.BlockSpec((rows, 2 * HEAD_PAD), lambda p, j: (0, p)), pl.BlockSpec((tq, 2 * HEAD_PAD), lambda p, j: (j, p)),
                  pl.BlockSpec((tq, 2 * V_DIM), lambda p, j: (j, p)), pl.BlockSpec((rows, 2 * V_DIM), lambda p, j: (0, p)),
                  pl.BlockSpec((2, rows, LANES), lambda p, j: (p, 0, 0)), pl.BlockSpec((2, rows, LANES), lambda p, j: (p, 0, 0))],
        out_specs=[pl.BlockSpec((rows, 2 * HEAD_PAD), lambda p, j: (0, p)), pl.BlockSpec((tq, 2 * HEAD_PAD), lambda p, j: (j, p)),
                   pl.BlockSpec((tq, 2 * V_DIM), lambda p, j: (j, p))],
        out_shape=[jax.ShapeDtypeStruct((rows, HEADS * HEAD_PAD), F32), jax.ShapeDtypeStruct((rows, HEADS * HEAD_PAD), F32),
                   jax.ShapeDtypeStruct((rows, HEADS * V_DIM), F32)],
        compiler_params=_params(("parallel", "arbitrary")),
    )(q, k, v, do, lse, delta)


def mla_prep_bwd(dq, dk, dv, z_br, rope, gq, wuq, gkv, wukv, dz_buf):
    rows = dq.shape[0]
    t = ROW_TILE
    cur, _, _, full = _tile_specs(t, rows // HALO)
    w8 = HEADS * HEAD_PAD

    def body(dq_ref, dk_ref, dv_ref, z_ref, rope_ref, gq_ref, wuq_ref, gkv_ref, wukv_ref, _, dz_ref, dwuq_ref, dwukv_ref, dgq_ref, dgkv_ref):
        i = pl.program_id(0)
        cth, s1, s2 = rope_ref[:, 0:128], rope_ref[:, 128:256], rope_ref[:, 256:384]
        dqb = _rope_transposed(dq_ref[...] * Q_SCALE, _lanes8(cth), _lanes8(s1), _lanes8(s2), w8).astype(BF16)
        cq = z_ref[:, 0:256].astype(F32)
        qn, vjp_q = jax.vjp(_rms, cq, gq_ref[...])
        _accumulate(i, dwuq_ref, _dot_tn(qn.astype(BF16), dqb))
        dcq, dgq = vjp_q(_dot_nt(dqb, wuq_ref[...]))
        _accumulate(i, dgq_ref, dgq)

        dk = dk_ref[...]
        dkr = sum(dk[:, HEAD_PAD * h:HEAD_PAD * (h + 1)] for h in range(HEADS))
        dkr = _rope_transposed(dkr, cth, s1, s2, HEAD_PAD)
        lane = lax.broadcasted_iota(jnp.int32, (1, HEAD_PAD), 1)
        dkr = jnp.where((lane >= QK_NOPE) & (lane < QK_NOPE + QK_ROPE), dkr, 0.0)
        dkvb = jnp.concatenate([dk, dv_ref[...]], axis=1).astype(BF16)
        ckv = z_ref[:, 256:384].astype(F32)
        kvn, vjp_kv = jax.vjp(_rms, ckv, gkv_ref[...])
        _accumulate(i, dwukv_ref, _dot_tn(kvn.astype(BF16), dkvb))
        dckv, dgkv = vjp_kv(_dot_nt(dkvb, wukv_ref[...]))
        _accumulate(i, dgkv_ref, dgkv)
        dz_ref[...] = jnp.concatenate([dcq, dckv, dkr], axis=1).astype(BF16)

    return pl.pallas_call(
        body, name="mla_prep_bwd", grid=(rows // t,),
        in_specs=[cur(w8), cur(w8), cur(512), cur(512, CQ // 512), cur(384), full((1, 256)), full((256, w8)), full((1, 128)),
                  full((128, w8 + 512)), ANY],
        out_specs=[cur(512, CQ // 512), full((256, w8)), full((128, w8 + 512)), full((1, 256)), full((1, 128))],
        out_shape=[jax.ShapeDtypeStruct((rows, ZB), BF16), jax.ShapeDtypeStruct((256, w8), F32), jax.ShapeDtypeStruct((128, w8 + 512), F32),
                   jax.ShapeDtypeStruct((1, 256), F32), jax.ShapeDtypeStruct((1, 128), F32)],
        input_output_aliases={9: 0}, compiler_params=_params(("arbitrary",)),
    )(dq, dk, dv, z_br, rope, gq, wuq, gkv, wukv, dz_buf)


def prenorm_bwd(dz_br, w_br, dh_gl, hres, gpre, dh_next):
    rows, d = hres.shape
    t = ROW_TILE
    cur, _, _, full = _tile_specs(t, rows // HALO)

    def body(dz_ref, w_ref, dp_ref, x_ref, g_ref, dn_ref, dx_ref, dg_ref):
        i = pl.program_id(0)
        dh = _dot_nt(dz_ref[...], w_ref[...]) + dp_ref[...]
        _, vjp = jax.vjp(_rms, x_ref[...], g_ref[...])
        dx, dg = vjp(dh)
        dx_ref[...] = dx + dn_ref[...]
        _accumulate(i, dg_ref, dg)

    return pl.pallas_call(
        body, name="prenorm_bwd", grid=(rows // t,), in_specs=[cur(ZB), full((d, ZB)), cur(d), cur(d), full((1, d)), cur(d)],
        out_specs=[cur(d), full((1, d))], out_shape=[jax.ShapeDtypeStruct((rows, d), F32), jax.ShapeDtypeStruct((1, d), F32)],
        compiler_params=_params(("arbitrary",)),
    )(dz_br, w_br, dh_gl, hres, gpre, dh_next)


def _mesh_position():
    return lax.axis_index("x"), lax.axis_index("y"), lax.axis_index("c")


def chip_exchange(src, gather, name):
    block = src.shape if gather else src.shape[1:]

    def body(src_ref, dst_ref, send_sems, recv_sems, local_sem):
        x, y, c = _mesh_position()
        me = 2 * x + y
        peers = ((1 - x, y), (x, 1 - y), (1 - x, 1 - y))

        def part(k):
            return src_ref if gather else src_ref.at[k]

        def copy(j, slot):
            px, py = peers[j]
            return pltpu.make_async_remote_copy(src_ref=part(2 * px + py), dst_ref=dst_ref.at[slot], send_sem=send_sems.at[j],
                                                recv_sem=recv_sems.at[j], device_id=(px, py, c), device_id_type=MESH)

        local = pltpu.make_async_copy(part(me), dst_ref.at[me], local_sem)
        local.start()
        sends = [copy(j, me) for j in range(3)]
        for cp in sends:
            cp.start()
        for j, (px, py) in enumerate(peers):
            copy(j, 2 * px + py).wait_recv()
        for cp in sends:
            cp.wait_send()
        local.wait()

    return pl.pallas_call(
        body, name=name, in_specs=[pl.BlockSpec(memory_space=pl.ANY)], out_specs=pl.BlockSpec(memory_space=pl.ANY),
        out_shape=jax.ShapeDtypeStruct((N_CHIPS,) + tuple(block), src.dtype),
        scratch_shapes=[pltpu.SemaphoreType.DMA((3,)), pltpu.SemaphoreType.DMA((3,)), pltpu.SemaphoreType.DMA(())],
    )(src)


def sibling_swap(src, name):
    def body(src_ref, dst_ref, send_sem, recv_sem):
        x, y, c = _mesh_position()
        cp = pltpu.make_async_remote_copy(src_ref=src_ref, dst_ref=dst_ref, send_sem=send_sem, recv_sem=recv_sem,
                                          device_id=(x, y, 1 - c), device_id_type=MESH)
        cp.start()
        cp.wait()

    return pl.pallas_call(
        body, name=name, in_specs=[pl.BlockSpec(memory_space=pl.ANY)], out_specs=pl.BlockSpec(memory_space=pl.ANY),
        out_shape=jax.ShapeDtypeStruct(src.shape, src.dtype),
        scratch_shapes=[pltpu.SemaphoreType.DMA(()), pltpu.SemaphoreType.DMA(())],
    )(src)


def _comm_call(body, name, n_in, out_shapes, n_sems):
    return pl.pallas_call(
        body, name=name, in_specs=[ANY] * n_in, out_specs=[ANY] * len(out_shapes), out_shape=out_shapes,
        scratch_shapes=[pltpu.SemaphoreType.DMA((n,)) for n in n_sems])


def _layer_halves(c):
    return pl.ds((DEPTH // 2) * c, DEPTH // 2), pl.ds((DEPTH // 2) * (1 - c), DEPTH // 2)


def gather_layers(srcs, name):
    n = len(srcs)

    def body(*refs):
        src, dst = refs[:n], refs[n:2 * n]
        ici_send, ici_recv, d2d_send, d2d_recv, local_sems = refs[2 * n:]
        x, y, c = _mesh_position()
        me = 2 * x + y
        peers = ((1 - x, y), (x, 1 - y), (1 - x, 1 - y))
        mine, other = _layer_halves(c)

        def fetch(a, j, slot):
            px, py = peers[j]
            return pltpu.make_async_remote_copy(src_ref=src[a].at[mine], dst_ref=dst[a].at[mine, slot], send_sem=ici_send.at[3 * a + j],
                                                recv_sem=ici_recv.at[3 * a + j], device_id=(px, py, c), device_id_type=MESH)

        def forward(a, j, half):
            px, py = peers[j]
            part = dst[a].at[half, 2 * px + py]
            return pltpu.make_async_remote_copy(src_ref=part, dst_ref=part, send_sem=d2d_send.at[3 * a + j],
                                                recv_sem=d2d_recv.at[3 * a + j], device_id=(x, y, 1 - c), device_id_type=MESH)

        local = [pltpu.make_async_copy(src[a], dst[a].at[:, me], local_sems.at[a]) for a in range(n)]
        sends = [fetch(a, j, me) for a in range(n) for j in range(3)]
        for cp in local + sends:
            cp.start()
        passed = []
        for j, (px, py) in enumerate(peers):
            for a in range(n):
                fetch(a, j, 2 * px + py).wait_recv()
                passed.append(forward(a, j, mine))
                passed[-1].start()
        for j in range(3):
            for a in range(n):
                forward(a, j, other).wait_recv()
        for cp in sends + passed:
            cp.wait_send()
        for cp in local:
            cp.wait()

    outs = [jax.ShapeDtypeStruct((DEPTH, N_CHIPS) + s.shape[1:], s.dtype) for s in srcs]
    return _comm_call(body, name, n, outs, (3 * n, 3 * n, 3 * n, 3 * n, n))(*srcs)


def swap_layer_halves(ps, name):
    n = len(ps)

    def body(*refs):
        src, dst = refs[:n], refs[n:2 * n]
        send_sems, recv_sems = refs[2 * n:]
        x, y, c = _mesh_position()
        _, other = _layer_halves(c)
        copies = [pltpu.make_async_remote_copy(src_ref=src[a].at[other], dst_ref=dst[a], send_sem=send_sems.at[a], recv_sem=recv_sems.at[a],
                                               device_id=(x, y, 1 - c), device_id_type=MESH) for a in range(n)]
        for cp in copies:
            cp.start()
        for cp in copies:
            cp.wait()

    outs = [jax.ShapeDtypeStruct((DEPTH // 2,) + p.shape[1:], p.dtype) for p in ps]
    return _comm_call(body, name, n, outs, (n, n))(*ps)


def exchange_chip_sums(ss, name):
    n = len(ss)

    def body(*refs):
        src, dst = refs[:n], refs[n:2 * n]
        send_sems, recv_sems, local_sems = refs[2 * n:]
        x, y, c = _mesh_position()
        me = 2 * x + y
        peers = ((1 - x, y), (x, 1 - y), (1 - x, 1 - y))

        def copy(a, j, slot):
            px, py = peers[j]
            return pltpu.make_async_remote_copy(src_ref=src[a].at[:, 2 * px + py], dst_ref=dst[a].at[:, slot], send_sem=send_sems.at[3 * a + j],
                                                recv_sem=recv_sems.at[3 * a + j], device_id=(px, py, c), device_id_type=MESH)

        local = [pltpu.make_async_copy(src[a].at[:, me], dst[a].at[:, me], local_sems.at[a]) for a in range(n)]
        sends = [copy(a, j, me) for a in range(n) for j in range(3)]
        for cp in local + sends:
            cp.start()
        for j, (px, py) in enumerate(peers):
            for a in range(n):
                copy(a, j, 2 * px + py).wait_recv()
        for cp in sends:
            cp.wait_send()
        for cp in local:
            cp.wait()

    outs = [jax.ShapeDtypeStruct(s.shape, s.dtype) for s in ss]
    return _comm_call(body, name, n, outs, (3 * n, 3 * n, n))(*ss)


def share_layer_halves(gs, name):
    n = len(gs)

    def body(*refs):
        src, dst = refs[:n], refs[n:2 * n]
        send_sems, recv_sems, local_sems = refs[2 * n:]
        x, y, c = _mesh_position()
        mine, other = _layer_halves(c)
        local = [pltpu.make_async_copy(src[a], dst[a].at[mine], local_sems.at[a]) for a in range(n)]
        sends = [pltpu.make_async_remote_copy(src_ref=src[a], dst_ref=dst[a].at[mine], send_sem=send_sems.at[a], recv_sem=recv_sems.at[a],
                                              device_id=(x, y, 1 - c), device_id_type=MESH) for a in range(n)]
        for cp in local + sends:
            cp.start()
        for a in range(n):
            pltpu.make_async_remote_copy(src_ref=src[a], dst_ref=dst[a].at[other], send_sem=send_sems.at[a], recv_sem=recv_sems.at[a],
                                         device_id=(x, y, 1 - c), device_id_type=MESH).wait_recv()
        for cp in sends:
            cp.wait_send()
        for cp in local:
            cp.wait()

    outs = [jax.ShapeDtypeStruct((DEPTH,) + g.shape[1:], g.dtype) for g in gs]
    return _comm_call(body, name, n, outs, (n, n, n))(*gs)


def _row_block(rows, cols, itemsize):
    best = 16
    for rb in range(16, rows + 1, 16):
        if rows % rb == 0 and rb * cols * itemsize <= 2 * 1024 * 1024:
            best = rb
    assert rows % best == 0, (rows, cols)
    return best


def add_sibling_half(p, r, c, name):
    cols = p.shape[-1]
    p2, r2 = p.reshape(-1, cols), r.reshape(-1, cols)
    rows = r2.shape[0]
    rb = _row_block(rows, cols, 2)
    steps = rows // rb

    def body(c_ref, p_ref, r_ref, o_ref):
        o_ref[...] = (p_ref[...].astype(F32) + r_ref[...].astype(F32)).astype(BF16)

    out = pl.pallas_call(
        body, name=name, out_shape=jax.ShapeDtypeStruct((rows, cols), BF16),
        grid_spec=pltpu.PrefetchScalarGridSpec(
            num_scalar_prefetch=1, grid=(steps,), in_specs=[pl.BlockSpec((rb, cols), lambda i, c_ref: (c_ref[0] * steps + i, 0)),
                                                            pl.BlockSpec((rb, cols), lambda i, c_ref: (i, 0))],
            out_specs=pl.BlockSpec((rb, cols), lambda i, c_ref: (i, 0))),
        compiler_params=_params(("parallel",)),
    )(jnp.reshape(c, (1,)).astype(jnp.int32), p2, r2)
    return out.reshape(r.shape)


def sum_chip_slots(l, name):
    layers, n, rows, cols = l.shape
    rb = _row_block(rows, cols, 4)

    def body(l_ref, o_ref):
        acc = l_ref[0, 0].astype(F32)
        for s in range(1, n):
            acc = acc + l_ref[0, s].astype(F32)
        o_ref[0] = acc

    return pl.pallas_call(
        body, name=name, grid=(layers, rows // rb), in_specs=[pl.BlockSpec((1, n, rb, cols), lambda a, i: (a, 0, i, 0))],
        out_specs=pl.BlockSpec((1, rb, cols), lambda a, i: (a, i, 0)), out_shape=jax.ShapeDtypeStruct((layers, rows, cols), F32),
        compiler_params=_params(("parallel", "parallel")),
    )(l)


def _comm_block(rows):
    return 1024 if rows % 1024 == 0 else rows


def sum_slots(buf, name):
    n, r, c = buf.shape
    rb = _comm_block(r)

    def body(b_ref, o_ref):
        acc = b_ref[0].astype(F32)
        for s in range(1, n):
            acc = acc + b_ref[s].astype(F32)
        o_ref[...] = acc

    return pl.pallas_call(
        body, name=name, grid=(r // rb,), in_specs=[pl.BlockSpec((n, rb, c), lambda i: (0, i, 0))],
        out_specs=pl.BlockSpec((rb, c), lambda i: (i, 0)), out_shape=jax.ShapeDtypeStruct((r, c), F32),
        compiler_params=_params(("parallel",)),
    )(buf)


def add_pair(a, b, out_dtype, name):
    shape = a.shape
    a2, b2 = a.reshape(-1, shape[-1]), b.reshape(-1, shape[-1])
    r, c = a2.shape
    rb = _comm_block(r)

    def body(a_ref, b_ref, o_ref):
        o_ref[...] = (a_ref[...].astype(F32) + b_ref[...].astype(F32)).astype(out_dtype)

    out = pl.pallas_call(
        body, name=name, grid=(r // rb,), in_specs=[pl.BlockSpec((rb, c), lambda i: (i, 0))] * 2,
        out_specs=pl.BlockSpec((rb, c), lambda i: (i, 0)), out_shape=jax.ShapeDtypeStruct((r, c), out_dtype),
        compiler_params=_params(("parallel",)),
    )(a2, b2)
    return out.reshape(shape)


def adamw(w, g, m, v):
    shape = w.shape
    cols = shape[-1]
    rows = math.prod(shape[:-1])
    rb = rows if rows * cols <= 256 * 1024 else 256
    assert rows % rb == 0, shape

    def body(w_ref, g_ref, m_ref, v_ref, d_ref, nm_ref, nv_ref):
        g_ = g_ref[...]
        nm = ADAM_B1 * m_ref[...] + (1.0 - ADAM_B1) * g_
        nv = ADAM_B2 * v_ref[...] + (1.0 - ADAM_B2) * (g_ * g_)
        m_hat = nm / (1.0 - ADAM_B1 ** ADAM_STEP)
        v_hat = nv / (1.0 - ADAM_B2 ** ADAM_STEP)
        d_ref[...] = -ADAM_LR * (m_hat / (jnp.sqrt(v_hat) + ADAM_EPS) + ADAM_WD * w_ref[...])
        nm_ref[...] = nm
        nv_ref[...] = nv

    spec = pl.BlockSpec((rb, cols), lambda i: (i, 0))
    outs = pl.pallas_call(
        body, name="adamw", grid=(rows // rb,), in_specs=[spec] * 4, out_specs=[spec] * 3,
        out_shape=[jax.ShapeDtypeStruct((rows, cols), F32)] * 3, compiler_params=_params(("parallel",)),
    )(*(a.reshape(rows, cols) for a in (w, g, m, v)))
    return tuple(o.reshape(shape) for o in outs)


def _pack(arrays, dtype, row_multiple):
    flat = jnp.concatenate([a.astype(dtype).reshape(-1) for a in arrays])
    per = LANES * row_multiple
    total = -(-flat.shape[0] // per) * per
    return jnp.pad(flat, (0, total - flat.shape[0])).reshape(total // LANES, LANES)


def _unpack(buf, shapes):
    flat = buf.reshape(-1)
    out, off = [], 0
    for s in shapes:
        n = math.prod(s)
        out.append(flat[off:off + n].reshape(s))
        off += n
    return out


def _branch_columns(w):
    pad = lambda n: jnp.zeros(w.shape[:-1] + (n,), w.dtype)
    return jnp.concatenate([w[..., 2208:3232], w[..., 0:896], pad(64), w[..., 896:928], pad(32), w[..., 928:2208]], axis=-1)


def _branch_columns_inverse(dw_br, dw_gl):
    return jnp.concatenate([dw_br[..., 1024:1920], dw_br[..., 1984:2016], dw_br[..., 2048:ZB], dw_br[..., 0:1024], dw_gl], axis=-1)


def _uq_layout(w):
    r = w.reshape(w.shape[0], HEADS, QK_NOPE + QK_ROPE)
    return jnp.pad(r, ((0, 0), (0, 0), (0, HEAD_PAD - QK_NOPE - QK_ROPE))).reshape(w.shape[0], HEADS * HEAD_PAD)


def _uq_layout_inverse(dw):
    return dw.reshape(dw.shape[0], HEADS, HEAD_PAD)[:, :, :QK_NOPE + QK_ROPE].reshape(dw.shape[0], HEADS * (QK_NOPE + QK_ROPE))


def _ukv_layout(w):
    r = w.reshape(w.shape[0], HEADS, QK_NOPE + V_DIM)
    kp = jnp.pad(r[:, :, :QK_NOPE], ((0, 0), (0, 0), (0, HEAD_PAD - QK_NOPE))).reshape(w.shape[0], HEADS * HEAD_PAD)
    return jnp.concatenate([kp, r[:, :, QK_NOPE:].reshape(w.shape[0], HEADS * V_DIM)], axis=1)


def _ukv_layout_inverse(dw):
    n = dw.shape[0]
    dk = dw[:, :HEADS * HEAD_PAD].reshape(n, HEADS, HEAD_PAD)[:, :, :QK_NOPE]
    dv = dw[:, HEADS * HEAD_PAD:].reshape(n, HEADS, V_DIM)
    return jnp.concatenate([dk, dv], axis=2).reshape(n, HEADS * (QK_NOPE + V_DIM))


def _block_diag(pw):
    out = jnp.zeros((256, 256), pw.dtype)
    for g in range(4):
        out = lax.dynamic_update_slice(out, pw[g], (64 * g, 64 * g))
    return out


def _block_diag_inverse(d):
    return jnp.stack([d[64 * g:64 * (g + 1), 64 * g:64 * (g + 1)] for g in range(4)])


def _pad_rows(a, n):
    return jnp.pad(a, ((0, n - a.shape[0]), (0, 0)))


def _rope_tables(rows):
    inv = 1.0 / (ROPE_THETA ** (jnp.arange(0, QK_ROPE, 2, dtype=F32) / QK_ROPE))
    ang = jnp.arange(rows, dtype=F32)[:, None] * inv[None, :]
    cos, sin = jnp.cos(ang), jnp.sin(ang)
    one, zero = jnp.ones((rows, 1), F32), jnp.zeros((rows, 1), F32)
    rep = lambda a, n: jnp.broadcast_to(a, (rows, n))
    c = jnp.concatenate([rep(one, 64), cos, cos, rep(one, 32)], axis=1)
    s1 = jnp.concatenate([rep(zero, 64), -sin, rep(zero, 48)], axis=1)
    s2 = jnp.concatenate([rep(zero, 80), sin, rep(zero, 32)], axis=1)
    return jnp.concatenate([c, s1, s2], axis=1)


def _misc_block(parts):
    out = []
    for name, rows in MISC:
        a = parts[name]
        if name == "w_o":
            a = a.reshape(a.shape[:-2] + (rows, 256))
        elif name == "w_uq":
            a = jnp.pad(a, [(0, 0)] * (a.ndim - 1) + [(0, 256 - a.shape[-1])])
        out.append(a)
    return jnp.concatenate(out, axis=-2)


def _misc_unblock(block):
    out, off = {}, 0
    for name, rows in MISC:
        a = block[..., off:off + rows, :]
        off += rows
        if name == "w_o":
            a = a.reshape(a.shape[:-2] + (256, D_MODEL))
        elif name == "w_uq":
            a = a[..., :192]
        out[name] = a
    return out


def _to_chip_blocks(name, a):
    if name == "w_o":
        return a.reshape(a.shape[:-2] + (N_CHIPS, a.shape[-2] // N_CHIPS, a.shape[-1]))
    return jnp.swapaxes(a.reshape(a.shape[:-1] + (N_CHIPS, a.shape[-1] // N_CHIPS)), -3, -2)


def _from_chip_blocks(name, b):
    if name == "w_o":
        return b.reshape(b.shape[:-3] + (N_CHIPS * b.shape[-2], b.shape[-1]))
    s = jnp.swapaxes(b, -3, -2)
    return s.reshape(s.shape[:-2] + (N_CHIPS * s.shape[-1],))


def gather_weights(shards):
    misc = _misc_block({n: shards[n] for n, _ in MISC}).astype(BF16)
    g_in, g_misc = gather_layers([shards["w_in"].astype(BF16), misc], "gather_weights")
    out = {"w_in": _from_chip_blocks("w_in", g_in)}
    for name, blocks in _misc_unblock(g_misc).items():
        out[name] = _from_chip_blocks(name, blocks)
    small = chip_exchange(_pack([shards[n] for n, _, _ in SHARDED_SMALL], F32, 8), True, "gather_small_ici")
    per_chip = [_unpack(small[k], [s for _, s, _ in SHARDED_SMALL]) for k in range(N_CHIPS)]
    for idx, (name, _, axis) in enumerate(SHARDED_SMALL):
        out[name] = jnp.concatenate([per_chip[k][idx] for k in range(N_CHIPS)], axis=axis)
    return out


def reduce_large(grads, c):
    ps = [_to_chip_blocks("w_in", grads["w_in"]).astype(BF16),
          _misc_block({n: _to_chip_blocks(n, grads[n]) for n, _ in MISC}).astype(BF16)]
    rs = swap_layer_halves(ps, "reduce_grads_d2d")
    ss = [add_sibling_half(p, r, c, "reduce_grads_pair_%d" % a) for a, (p, r) in enumerate(zip(ps, rs))]
    ls = exchange_chip_sums(ss, "reduce_grads_ici")
    gs = [sum_chip_slots(l, "reduce_grads_sum_%d" % a) for a, l in enumerate(ls)]
    g_in, g_misc = share_layer_halves(gs, "reduce_grads_share")
    out = {"w_in": g_in}
    out.update(_misc_unblock(g_misc))
    return out


def reduce_small(grads, chip):
    names = [n for n, _ in REPLICATED] + [n for n, _, _ in SHARDED_SMALL]
    buf = _pack([grads[n] for n in names], F32, 8)
    chip_sum = add_pair(buf, sibling_swap(buf, "reduce_small_d2d"), F32, "reduce_small_pair")
    total = sum_slots(chip_exchange(chip_sum, True, "reduce_small_ici"), "reduce_small_sum")
    out = dict(zip(names, _unpack(total, [grads[n].shape for n in names])))
    for name, shape, axis in SHARDED_SMALL:
        out[name] = lax.dynamic_slice_in_dim(out[name], chip * shape[axis], shape[axis], axis)
    return out


def _layer_weights(w, i):
    w_in = w["w_in"][i]
    conf_w = jnp.pad(w["conf_dw_w"][i].astype(F32), ((0, 32 - CONF_K), (0, 0)))
    sc_w = jnp.pad(w["sc_dw_w"][i].astype(F32), ((0, 8 - SC_K), (0, 0)))
    conf_vec = jnp.concatenate([w["conf_dw_b"][i][None], w["conf_ln_g"][i][None], w["conf_ln_b"][i][None], jnp.zeros((5, 256), F32)])
    return dict(
        w_br=_branch_columns(w_in[:, :IN_W - ZG]).astype(BF16), w_gl=w_in[:, IN_W - ZG:].astype(BF16),
        gpre=w["pre_norm_g"][i][None], bias=w["gate_bias"][i][None], pwbd=_block_diag(w["pool_w"][i]).astype(BF16),
        pscale=w["pool_scale"][i][None], gq=w["q_norm_g"][i][None], wuq=_uq_layout(w["w_uq"][i]).astype(BF16),
        gkv=w["kv_norm_g"][i][None], wukv=_ukv_layout(w["w_ukv"][i]).astype(BF16), conf_w=conf_w, conf_vec=conf_vec, sc_w=sc_w,
        woa=w["w_out_pool"][i].astype(BF16), wob=w["w_out_mla"][i].astype(BF16), woc=w["w_out_conf"][i].astype(BF16),
        wod=w["w_out_sc"][i].astype(BF16), wo=w["w_o"][i].astype(BF16), gpost=w["post_norm_g"][i][None])


def local_step(x, target, w):
    seq = x.shape[0]
    length = N_META + seq
    rows = -(-length // ROW_TILE) * ROW_TILE
    bt = _big_tile(rows)
    hres = _pad_rows(jnp.concatenate([w["meta_tokens"].astype(F32), x], axis=0), rows)
    tgt = jnp.pad(target, ((N_META, rows - length), (0, 0)))
    rope = _rope_tables(rows)

    saved = []
    for i in range(DEPTH):
        lw = _layer_weights(w, i)
        z_br, hb = prenorm_project(hres, lw["gpre"], lw["w_br"])
        z_gl = matmul(hb, lw["w_gl"], "nn", BF16, bt, 1024, D_MODEL, "project_gates")
        ua, uc, ud, q, k, v = branches_fwd(z_br, rope, lw["pwbd"], lw["pscale"], lw["gq"], lw["wuq"], lw["gkv"], lw["wukv"],
                                           lw["conf_w"], lw["conf_vec"], lw["sc_w"])
        o_att, lse = attention_fwd(q, k, v)
        ub, mb, o, hnew = merge_fwd(ua, o_att, uc, ud, z_br, z_gl, lw["bias"], lw["woa"], lw["wob"], lw["woc"], lw["wod"], lw["wo"],
                                    lw["gpost"], hres)
        saved.append(dict(lw=lw, hres=hres, hb=hb, z_br=z_br, z_gl=z_gl, ua=ua, ub=ub, uc=uc, ud=ud, q=q, k=k, v=v, o_att=o_att,
                          lse=lse, mb=mb, o=o))
        hres = hnew

    dh, total = loss_head(hres, tgt, seq)

    g = {n: [None] * DEPTH for n in WEIGHT_ORDER if n != "meta_tokens"}
    for i in reversed(range(DEPTH)):
        s = saved[i]
        lw = s["lw"]
        dm, dwo, dgpost = postnorm_bwd(dh, s["o"], s["mb"], lw["wo"], lw["gpost"])
        dua, dub, duc, dud, dz_gl, dwa, dwb, dwc, dwd, dbias = merge_bwd(dm, s["ua"], s["ub"], s["uc"], s["ud"], s["z_gl"], lw["bias"],
                                                                     lw["woa"], lw["wob"], lw["woc"], lw["wod"])
        dz_br = lax.empty((rows, ZB), BF16)
        dz_br, dpw, dps = pool_bwd(s["z_br"], dua, lw["pwbd"], lw["pscale"], dz_br)
        dz_br, dsw = shortconv_bwd(s["z_br"], dud, lw["sc_w"], dz_br)
        dc, dz_br, dcvec = conformer_bwd_tail(s["z_br"], duc, lw["conf_w"], lw["conf_vec"], dz_br)
        dz_br, dcw = conformer_bwd_conv(s["z_br"], dc, lw["conf_w"], dz_br)
        do, dz_br, delta = attention_bwd_prep(dub, s["o_att"], s["z_br"], dz_br)
        dq, dk, dv = attention_bwd(s["q"], s["k"], s["v"], do, s["lse"], delta)
        dz_br, dwuq, dwukv, dgq, dgkv = mla_prep_bwd(dq, dk, dv, s["z_br"], rope, lw["gq"], lw["wuq"], lw["gkv"], lw["wukv"], dz_br)
        dw_br = matmul(s["hb"], dz_br, "tn", F32, D_MODEL, ZB // 2, ROW_TILE, "grad_w_branch")
        dw_gl = matmul(s["hb"], dz_gl, "tn", F32, D_MODEL, 1024, ROW_TILE, "grad_w_gates")
        dh_gl = matmul(dz_gl, lw["w_gl"], "nt", F32, bt, D_MODEL, 1024, "grad_h_gates")
        dh, dgpre = prenorm_bwd(dz_br, lw["w_br"], dh_gl, s["hres"], lw["gpre"], dh)

        g["pre_norm_g"][i] = dgpre[0]
        g["w_in"][i] = _branch_columns_inverse(dw_br, dw_gl)
        g["gate_bias"][i] = dbias[0]
        g["pool_w"][i] = _block_diag_inverse(dpw)
        g["pool_scale"][i] = dps[0]
        g["w_out_pool"][i] = dwa
        g["q_norm_g"][i] = dgq[0]
        g["w_uq"][i] = _uq_layout_inverse(dwuq)
        g["kv_norm_g"][i] = dgkv[0]
        g["w_ukv"][i] = _ukv_layout_inverse(dwukv)
        g["w_out_mla"][i] = dwb
        g["conf_dw_w"][i] = dcw[:CONF_K]
        g["conf_dw_b"][i] = dcvec[2]
        g["conf_ln_g"][i] = dcvec[0]
        g["conf_ln_b"][i] = dcvec[1]
        g["w_out_conf"][i] = dwc
        g["sc_dw_w"][i] = dsw[:SC_K]
        g["w_out_sc"][i] = dwd
        g["w_o"][i] = dwo
        g["post_norm_g"][i] = dgpost[0]

    grads = {n: jnp.stack(parts) for n, parts in g.items()}
    grads["meta_tokens"] = dh[:N_META]
    return total[0, 0], dh[N_META:length], grads


def kernel(x, meta_tokens, pre_norm_g, w_in, gate_bias, pool_w, pool_scale, w_out_pool, q_norm_g, w_uq, kv_norm_g, w_ukv, w_out_mla, conf_dw_w, conf_dw_b, conf_ln_g, conf_ln_b, w_out_conf, sc_dw_w, w_out_sc, w_o, post_norm_g, loss_target, m_meta_tokens, m_pre_norm_g, m_w_in, m_gate_bias, m_pool_w, m_pool_scale, m_w_out_pool, m_q_norm_g, m_w_uq, m_kv_norm_g, m_w_ukv, m_w_out_mla, m_conf_dw_w, m_conf_dw_b, m_conf_ln_g, m_conf_ln_b, m_w_out_conf, m_sc_dw_w, m_w_out_sc, m_w_o, m_post_norm_g, v_meta_tokens, v_pre_norm_g, v_w_in, v_gate_bias, v_pool_w, v_pool_scale, v_w_out_pool, v_q_norm_g, v_w_uq, v_kv_norm_g, v_w_ukv, v_w_out_mla, v_conf_dw_w, v_conf_dw_b, v_conf_ln_g, v_conf_ln_b, v_w_out_conf, v_sc_dw_w, v_w_out_sc, v_w_o, v_post_norm_g):
    args = locals()
    weights = {n: args[n] for n in WEIGHT_ORDER}
    c = lax.axis_index("c")
    chip = 2 * lax.axis_index("x") + lax.axis_index("y")

    full = dict(weights)
    full.update(gather_weights(weights))
    total, dx, grads = local_step(x[0], loss_target[0], full)
    loss = lax.psum(total * (0.5 / D_MODEL), ("x", "y", "c"))

    reduced = reduce_large(grads, c)
    reduced.update(reduce_small(grads, chip))

    deltas, new_m, new_v = [], [], []
    for n in WEIGHT_ORDER:
        d, nm, nv = adamw(weights[n], reduced[n], args["m_" + n], args["v_" + n])
        deltas.append(d)
        new_m.append(nm)
        new_v.append(nv)
    return (loss, dx[None], *[reduced[n] for n in WEIGHT_ORDER], *deltas, *new_m, *new_v)
```

```python
import functools
import math

import jax
import jax.numpy as jnp
from jax import lax
from jax.experimental import pallas as pl
from jax.experimental.pallas import tpu as pltpu

F32 = jnp.float32
BF16 = jnp.bfloat16

D_MODEL = 1024
DEPTH = 4
N_META = 16
EPS = 1e-6
HEADS = 8
QK_NOPE = 64
QK_ROPE = 32
V_DIM = 64
HEAD_PAD = 128
ROPE_THETA = 10000.0
Q_SCALE = (QK_NOPE + QK_ROPE) ** -0.5
CONF_K = 31
SC_K = 3
IN_W = 7328
N_CHIPS = 4

ZB = 3328
ZG = 4096
BG, C2, XV, SG, PV, PG, CQ, CKV, KR, MG, CA, CGT, CG = (0, 256, 512, 768, 1024, 1280, 1536, 1792, 1920, 2048, 2560, 2816, 3072)

ROW_TILE = 384
HALO = 32
LANES = 128
VMEM_LIMIT = 56 * 1024 * 1024

ADAM_LR = 0.001
ADAM_B1 = 0.9
ADAM_B2 = 0.999
ADAM_EPS = 1e-08
ADAM_WD = 0.01
ADAM_STEP = 10

MESH = pl.DeviceIdType.MESH
ANY = pl.BlockSpec(memory_space=pl.ANY)

MISC = (
    ("w_out_pool", 256), ("w_ukv", 128), ("w_out_mla", 512), ("w_out_conf", 256), ("w_out_sc", 256), ("w_o", 1024), ("w_uq", 256))
SHARDED_SMALL = (
    ("meta_tokens", (N_META, 256), 1),
    ("conf_dw_w", (DEPTH, CONF_K, 64), 2),
    ("sc_dw_w", (DEPTH, SC_K, 64), 2),
)
REPLICATED = (
    ("pre_norm_g", (DEPTH, D_MODEL)),
    ("gate_bias", (DEPTH, 4 * D_MODEL)),
    ("pool_w", (DEPTH, 4, 64, 64)),
    ("pool_scale", (DEPTH, 256)),
    ("q_norm_g", (DEPTH, 256)),
    ("kv_norm_g", (DEPTH, 128)),
    ("conf_dw_b", (DEPTH, 256)),
    ("conf_ln_g", (DEPTH, 256)),
    ("conf_ln_b", (DEPTH, 256)),
    ("post_norm_g", (DEPTH, D_MODEL)),
)
WEIGHT_ORDER = ("meta_tokens", "pre_norm_g", "w_in", "gate_bias", "pool_w", "pool_scale", "w_out_pool", "q_norm_g", "w_uq",
                "kv_norm_g", "w_ukv", "w_out_mla", "conf_dw_w", "conf_dw_b", "conf_ln_g", "conf_ln_b", "w_out_conf", "sc_dw_w",
                "w_out_sc", "w_o", "post_norm_g")


def _dot(a, b):
    return lax.dot_general(a, b, (((1,), (0,)), ((), ())), preferred_element_type=F32)


def _dot_nt(a, b):
    return lax.dot_general(a, b, (((1,), (1,)), ((), ())), preferred_element_type=F32)


def _dot_tn(a, b):
    return lax.dot_general(a, b, (((0,), (0,)), ((), ())), preferred_element_type=F32)


def _sigmoid(x):
    return jax.nn.sigmoid(x)


def _silu(x):
    return x * _sigmoid(x)


def _silu_grad(x):
    s = _sigmoid(x)
    return s * (1.0 + x * (1.0 - s))


def _rms(x, g):
    return x * lax.rsqrt(jnp.mean(x * x, axis=-1, keepdims=True) + EPS) * g


def _sh(x, d):
    return x if d == 0 else pltpu.roll(x, d, 0)


def _ash(x, d):
    return x if d == 0 else pltpu.roll(x, x.shape[0] - d, 0)


def _lanes8(t):
    return jnp.concatenate([t] * HEADS, axis=1)


def _pool_window_sums(v, shift):
    a2 = v + shift(v, 1)
    a4 = a2 + shift(a2, 2)
    a8 = a4 + shift(a4, 4)
    a16 = a8 + shift(a8, 8)
    lane = lax.broadcasted_iota(jnp.int32, v.shape, 1)
    return jnp.where(lane < 64, a2, jnp.where(lane < 128, a4, jnp.where(lane < 192, a8, a16)))


def _pool_counts(first_row, rows):
    pos = first_row + lax.broadcasted_iota(jnp.int32, (rows, 256), 0)
    lane = lax.broadcasted_iota(jnp.int32, (rows, 256), 1)
    width = jnp.where(lane < 64, 2, jnp.where(lane < 128, 4, jnp.where(lane < 192, 8, 16)))
    return jnp.maximum(jnp.minimum(pos + 1, width), 1).astype(F32)


def _params(sem=None):
    return pltpu.CompilerParams(dimension_semantics=sem, vmem_limit_bytes=VMEM_LIMIT)


def _tile_specs(t, n_halo_blocks):
    per = t // HALO

    def cur(c, cb=0):
        return pl.BlockSpec((t, c), lambda i: (i, cb))

    def prev(c, cb=0):
        return pl.BlockSpec((HALO, c), lambda i: (jnp.maximum(i * per - 1, 0), cb))

    def nxt(c, cb=0):
        return pl.BlockSpec((HALO, c), lambda i: (jnp.minimum((i + 1) * per, n_halo_blocks - 1), cb))

    def full(shape):
        return pl.BlockSpec(shape, lambda i: (0,) * len(shape))

    return cur, prev, nxt, full


def _big_tile(rows):
    return rows // 3 if rows % (3 * LANES) == 0 else ROW_TILE


def matmul(a, b, mode, out_dtype, tm, tn, tk, name):
    if mode == "nn":
        (m, k), n = a.shape, b.shape[1]
        a_spec = pl.BlockSpec((tm, tk), lambda i, j, kk: (i, kk))
        b_spec = pl.BlockSpec((tk, tn), lambda i, j, kk: (kk, j))
        dot = _dot
    elif mode == "nt":
        (m, k), n = a.shape, b.shape[0]
        a_spec = pl.BlockSpec((tm, tk), lambda i, j, kk: (i, kk))
        b_spec = pl.BlockSpec((tn, tk), lambda i, j, kk: (j, kk))
        dot = _dot_nt
    else:
        (k, m), n = a.shape, b.shape[1]
        a_spec = pl.BlockSpec((tk, tm), lambda i, j, kk: (kk, i))
        b_spec = pl.BlockSpec((tk, tn), lambda i, j, kk: (kk, j))
        dot = _dot_tn
    assert m % tm == 0 and n % tn == 0 and k % tk == 0, (a.shape, b.shape, tm, tn, tk)
    nk = k // tk

    def body(a_ref, b_ref, o_ref, acc_ref):
        kk = pl.program_id(2)

        @pl.when(kk == 0)
        def _():
            acc_ref[...] = jnp.zeros_like(acc_ref)

        acc_ref[...] += dot(a_ref[...], b_ref[...])

        @pl.when(kk == nk - 1)
        def _():
            o_ref[...] = acc_ref[...].astype(out_dtype)

    return pl.pallas_call(
        body, name=name, grid=(m // tm, n // tn, nk), in_specs=[a_spec, b_spec],
        out_specs=pl.BlockSpec((tm, tn), lambda i, j, kk: (i, j)), out_shape=jax.ShapeDtypeStruct((m, n), out_dtype),
        scratch_shapes=[pltpu.VMEM((tm, tn), F32)], compiler_params=_params(("parallel", "parallel", "arbitrary")),
    )(a, b)


def prenorm_project(hres, g, w):
    rows, d = hres.shape
    n = w.shape[1]
    tm, tn = _big_tile(rows), n // 2

    def body(x_ref, g_ref, w_ref, z_ref, hb_ref):
        @pl.when(pl.program_id(1) == 0)
        def _():
            hb_ref[...] = _rms(x_ref[...], g_ref[...]).astype(BF16)

        z_ref[...] = _dot(hb_ref[...], w_ref[...]).astype(BF16)

    return pl.pallas_call(
        body, name="prenorm_project", grid=(rows // tm, n // tn),
        in_specs=[pl.BlockSpec((tm, d), lambda i, j: (i, 0)), pl.BlockSpec((1, d), lambda i, j: (0, 0)),
                  pl.BlockSpec((d, tn), lambda i, j: (0, j))],
        out_specs=[pl.BlockSpec((tm, tn), lambda i, j: (i, j)), pl.BlockSpec((tm, d), lambda i, j: (i, 0))],
        out_shape=[jax.ShapeDtypeStruct((rows, n), BF16), jax.ShapeDtypeStruct((rows, d), BF16)],
        compiler_params=_params(("parallel", "arbitrary")),
    )(hres, g, w)


def _rope(q, c, s1, s2, width):
    return q * c + pltpu.roll(q, width - 16, 1) * s1 + pltpu.roll(q, 16, 1) * s2


def _rope_transposed(dq, c, s1, s2, width):
    return dq * c + pltpu.roll(dq * s1, 16, 1) + pltpu.roll(dq * s2, width - 16, 1)


def _conf_conv(g1, w_ref):
    acc = jnp.zeros_like(g1)
    for k in range(CONF_K):
        acc = acc + w_ref[k:k + 1, :] * _sh(g1, CONF_K - 1 - k)
    return acc


def _conf_tail(c, cg, lg, lb):
    mu = jnp.mean(c, axis=-1, keepdims=True)
    xc = c - mu
    var = jnp.mean(xc * xc, axis=-1, keepdims=True)
    n = xc * lax.rsqrt(var + EPS) * lg + lb
    return _silu(n) * _silu(cg)


def branches_fwd(z_br, rope, pwbd, pscale, gq, wuq, gkv, wukv, conf_w, conf_vec, sc_w):
    rows = z_br.shape[0]
    t = ROW_TILE
    cur, prev, _, full = _tile_specs(t, rows // HALO)

    def body(zc_ref, zp_ref, rope_ref, pw_ref, ps_ref, gq_ref, wuq_ref, gkv_ref, wukv_ref, cw_ref, cv_ref, sw_ref,
             ua_ref, uc_ref, ud_ref, q_ref, k_ref, v_ref):
        i = pl.program_id(0)
        zp = jnp.where(i == 0, jnp.zeros(zp_ref.shape, zp_ref.dtype), zp_ref[...])

        def ext(lo, w=256):
            return jnp.concatenate([zp[:, lo:lo + w], zc_ref[:, lo:lo + w]], axis=0).astype(F32)

        def col(lo, w=256):
            return zc_ref[:, lo:lo + w].astype(F32)

        v = ext(PV)
        p = (_pool_window_sums(v, _sh) / _pool_counts(i * t - HALO, t + HALO) - v)[HALO:]
        ya = _dot(p.astype(BF16), pw_ref[...]) * ps_ref[...]
        ua_ref[...] = (ya * _silu(col(PG))).astype(BF16)

        g1 = ext(CA) * _sigmoid(ext(CGT))
        c = _conf_conv(g1, cw_ref)[HALO:] + cv_ref[0:1, :]
        uc_ref[...] = _conf_tail(c, col(CG), cv_ref[1:2, :], cv_ref[2:3, :]).astype(BF16)

        e = ext(C2) * ext(XV)
        f = jnp.zeros_like(e)
        for k in range(SC_K):
            f = f + sw_ref[k:k + 1, :] * _sh(e, SC_K - 1 - k)
        ud_ref[...] = (col(BG) * f[HALO:] * _silu(col(SG))).astype(BF16)

        cth, s1, s2 = rope_ref[:, 0:128], rope_ref[:, 128:256], rope_ref[:, 256:384]
        qn = _rms(col(CQ), gq_ref[...]).astype(BF16)
        q = _dot(qn, wuq_ref[...])
        w8 = HEADS * HEAD_PAD
        q_ref[...] = (_rope(q, _lanes8(cth), _lanes8(s1), _lanes8(s2), w8) * Q_SCALE).astype(BF16)
        kvn = _rms(col(CKV, 128), gkv_ref[...]).astype(BF16)
        kv = _dot(kvn, wukv_ref[...])
        kr = _rope(col(KR, 128), cth, s1, s2, HEAD_PAD)
        k_ref[...] = (kv[:, :w8] + _lanes8(kr)).astype(BF16)
        v_ref[...] = kv[:, w8:].astype(BF16)

    outs = [jax.ShapeDtypeStruct((rows, 256), BF16)] * 3 + [jax.ShapeDtypeStruct((rows, 1024), BF16)] * 2 + [
        jax.ShapeDtypeStruct((rows, 512), BF16)]
    return pl.pallas_call(
        body, name="branches_fwd", grid=(rows // t,),
        in_specs=[cur(ZB), prev(ZB), cur(384), full((256, 256)), full((1, 256)), full((1, 256)), full((256, 1024)),
                  full((1, 128)), full((128, 1536)), full((32, 256)), full((8, 256)), full((8, 256))],
        out_specs=[cur(256), cur(256), cur(256), cur(1024), cur(1024), cur(512)], out_shape=outs,
        compiler_params=_params(("parallel",)),
    )(z_br, z_br, rope, pwbd, pscale, gq, wuq, gkv, wukv, conf_w, conf_vec, sc_w)


def _head_lane_mask(h):
    lane = lax.broadcasted_iota(jnp.int32, (1, 2 * V_DIM), 1)
    return (lane >= V_DIM * h) & (lane < V_DIM * (h + 1))


def attention_fwd(q, k, v):
    rows = q.shape[0]
    tq = ROW_TILE
    nq = rows // tq

    def body(q_ref, k_ref, v_ref, o_ref, lse_ref):
        i = pl.program_id(1)
        row = lax.broadcasted_iota(jnp.int32, (tq, tq), 0)
        colm = lax.broadcasted_iota(jnp.int32, (tq, tq), 1)

        def head_step(h, j, carry, diagonal):
            m, l, acc = carry
            r0 = pl.multiple_of(j * tq, tq)
            kh = k_ref[pl.ds(r0, tq), HEAD_PAD * h:HEAD_PAD * (h + 1)]
            vh = jnp.where(_head_lane_mask(h), v_ref[pl.ds(r0, tq), :], jnp.zeros((), BF16))
            s = _dot_nt(q_ref[:, HEAD_PAD * h:HEAD_PAD * (h + 1)], kh)
            if diagonal:
                s = jnp.where(colm <= row, s, -1e30)
            m2 = jnp.maximum(m, jnp.max(s, axis=-1, keepdims=True))
            alpha = jnp.exp(m - m2)
            pr = jnp.exp(s - m2)
            return m2, alpha * l + jnp.sum(pr, axis=-1, keepdims=True), alpha * acc + _dot(pr.astype(BF16), vh)

        def step(j, carry, diagonal):
            return tuple(head_step(h, j, carry[h], diagonal) for h in range(2))

        init = (jnp.full((tq, 1), -1e30, F32), jnp.zeros((tq, 1), F32), jnp.zeros((tq, 2 * V_DIM), F32))
        carry = lax.fori_loop(0, i, lambda j, cr: step(j, cr, False), (init, init))
        out = jnp.zeros((tq, 2 * V_DIM), F32)
        for h, (m, l, acc) in enumerate(step(i, carry, True)):
            out = out + acc / l
            lse_ref[h] = jnp.broadcast_to(m + jnp.log(l), (tq, LANES))
        o_ref[...] = out.astype(BF16)

    return pl.pallas_call(
        body, name="attention_fwd", grid=(HEADS // 2, nq),
        in_specs=[pl.BlockSpec((tq, 2 * HEAD_PAD), lambda p, i: (i, p)), pl.BlockSpec((rows, 2 * HEAD_PAD), lambda p, i: (0, p)),
                  pl.BlockSpec((rows, 2 * V_DIM), lambda p, i: (0, p))],
        out_specs=[pl.BlockSpec((tq, 2 * V_DIM), lambda p, i: (i, p)), pl.BlockSpec((2, tq, LANES), lambda p, i: (p, i, 0))],
        out_shape=[jax.ShapeDtypeStruct((rows, HEADS * V_DIM), BF16), jax.ShapeDtypeStruct((HEADS, rows, LANES), F32)],
        compiler_params=_params(("parallel", "parallel")),
    )(q, k, v)


def merge_fwd(ua, o_att, uc, ud, z_br, z_gl, bias, woa, wob, woc, wod, wo, gpost, hres):
    rows = hres.shape[0]
    t = ROW_TILE
    cur, _, _, full = _tile_specs(t, rows // HALO)
    d = D_MODEL

    def body(ua_ref, ob_ref, uc_ref, ud_ref, mg_ref, gl_ref, b_ref, woa_ref, wob_ref, woc_ref, wod_ref, wo_ref, gp_ref, h_ref,
             ub_ref, mb_ref, o_ref, hn_ref):
        ub = (ob_ref[...].astype(F32) * _silu(mg_ref[...].astype(F32))).astype(BF16)
        ub_ref[...] = ub
        m = jnp.zeros((t, d), F32)
        for idx, (u, w_ref) in enumerate(((ua_ref[...], woa_ref), (ub, wob_ref), (uc_ref[...], woc_ref), (ud_ref[...], wod_ref))):
            gate = _sigmoid(gl_ref[:, d * idx:d * (idx + 1)].astype(F32) + b_ref[:, d * idx:d * (idx + 1)])
            m = m + gate * _dot(u, w_ref[...])
        mb = m.astype(BF16)
        mb_ref[...] = mb
        o = _dot(mb, wo_ref[...])
        o_ref[...] = o
        hn_ref[...] = h_ref[...] + _rms(o, gp_ref[...])

    return pl.pallas_call(
        body, name="merge_fwd", grid=(rows // t,),
        in_specs=[cur(256), cur(512), cur(256), cur(256), cur(512, MG // 512), cur(ZG), full((1, ZG)), full((256, d)), full((512, d)),
                  full((256, d)), full((256, d)), full((d, d)), full((1, d)), cur(d)],
        out_specs=[cur(512), cur(d), cur(d), cur(d)],
        out_shape=[jax.ShapeDtypeStruct((rows, 512), BF16), jax.ShapeDtypeStruct((rows, d), BF16), jax.ShapeDtypeStruct((rows, d), F32),
                   jax.ShapeDtypeStruct((rows, d), F32)],
        compiler_params=_params(("parallel",)),
    )(ua, o_att, uc, ud, z_br, z_gl, bias, woa, wob, woc, wod, wo, gpost, hres)


def loss_head(hres, target, n_tokens):
    rows, d = hres.shape
    t = ROW_TILE
    cur, _, _, full = _tile_specs(t, rows // HALO)
    n_steps = rows // t

    def body(h_ref, t_ref, dh_ref, tot_ref, acc_ref):
        i = pl.program_id(0)

        @pl.when(i == 0)
        def _():
            acc_ref[...] = jnp.zeros_like(acc_ref)

        r = i * t + lax.broadcasted_iota(jnp.int32, (t, 1), 0)
        diff = jnp.where((r >= N_META) & (r < N_META + n_tokens), h_ref[...] - t_ref[...], 0.0)
        dh_ref[...] = diff * (1.0 / d)
        acc_ref[...] += jnp.sum(diff * diff, axis=0, keepdims=True)

        @pl.when(i == n_steps - 1)
        def _():
            tot_ref[...] = jnp.broadcast_to(jnp.sum(acc_ref[...], axis=1, keepdims=True), (1, LANES))

    return pl.pallas_call(
        body, name="loss_head", grid=(n_steps,), in_specs=[cur(d), cur(d)], out_specs=[cur(d), full((1, LANES))],
        out_shape=[jax.ShapeDtypeStruct((rows, d), F32), jax.ShapeDtypeStruct((1, LANES), F32)],
        scratch_shapes=[pltpu.VMEM((1, d), F32)], compiler_params=_params(("arbitrary",)),
    )(hres, target)


def _accumulate(i, ref, value):
    @pl.when(i == 0)
    def _():
        ref[...] = value

    @pl.when(i > 0)
    def _():
        ref[...] += value


def postnorm_bwd(dh, o, mb, wo, gpost):
    rows, d = dh.shape
    t = ROW_TILE
    cur, _, _, full = _tile_specs(t, rows // HALO)

    def body(dh_ref, o_ref, mb_ref, wo_ref, gp_ref, dm_ref, dwo_ref, dgp_ref):
        i = pl.program_id(0)
        _, vjp = jax.vjp(_rms, o_ref[...], gp_ref[...])
        do, dg = vjp(dh_ref[...])
        dob = do.astype(BF16)
        dm_ref[...] = _dot_nt(dob, wo_ref[...])
        _accumulate(i, dwo_ref, _dot_tn(mb_ref[...], dob))
        _accumulate(i, dgp_ref, dg)

    return pl.pallas_call(
        body, name="postnorm_bwd", grid=(rows // t,), in_specs=[cur(d), cur(d), cur(d), full((d, d)), full((1, d))],
        out_specs=[cur(d), full((d, d)), full((1, d))],
        out_shape=[jax.ShapeDtypeStruct((rows, d), F32), jax.ShapeDtypeStruct((d, d), F32), jax.ShapeDtypeStruct((1, d), F32)],
        compiler_params=_params(("arbitrary",)),
    )(dh, o, mb, wo, gpost)


def merge_bwd(dm, ua, ub, uc, ud, z_gl, bias, woa, wob, woc, wod):
    rows, d = dm.shape
    t = ROW_TILE
    cur, _, _, full = _tile_specs(t, rows // HALO)
    widths = (256, 512, 256, 256)

    def body(dm_ref, ua_ref, ub_ref, uc_ref, ud_ref, gl_ref, b_ref, woa_ref, wob_ref, woc_ref, wod_ref,
             dua_ref, dub_ref, duc_ref, dud_ref, dgl_ref, dwa_ref, dwb_ref, dwc_ref, dwd_ref, db_ref):
        i = pl.program_id(0)
        dm = dm_ref[...]
        groups = ((ua_ref, woa_ref, dua_ref, dwa_ref), (ub_ref, wob_ref, dub_ref, dwb_ref), (uc_ref, woc_ref, duc_ref, dwc_ref),
                  (ud_ref, wod_ref, dud_ref, dwd_ref))
        for idx, (u_ref, w_ref, du_ref, dw_ref) in enumerate(groups):
            cols = slice(d * idx, d * (idx + 1))
            u = u_ref[...]
            gate = _sigmoid(gl_ref[:, cols].astype(F32) + b_ref[:, cols])
            dgl = dm * _dot(u, w_ref[...]) * gate * (1.0 - gate)
            dgl_ref[:, cols] = dgl.astype(BF16)
            _accumulate(i, db_ref.at[:, cols], jnp.sum(dgl, axis=0, keepdims=True))
            dyb = (dm * gate).astype(BF16)
            du_ref[...] = _dot_nt(dyb, w_ref[...])
            _accumulate(i, dw_ref, _dot_tn(u, dyb))

    return pl.pallas_call(
        body, name="merge_bwd", grid=(rows // t,),
        in_specs=[cur(d), cur(256), cur(512), cur(256), cur(256), cur(ZG), full((1, ZG))] + [full((w, d)) for w in widths],
        out_specs=[cur(256), cur(512), cur(256), cur(256), cur(ZG)] + [full((w, d)) for w in widths] + [full((1, ZG))],
        out_shape=[jax.ShapeDtypeStruct((rows, w), F32) for w in widths] + [jax.ShapeDtypeStruct((rows, ZG), BF16)] + [
            jax.ShapeDtypeStruct((w, d), F32) for w in widths] + [jax.ShapeDtypeStruct((1, ZG), F32)],
        compiler_params=_params(("arbitrary",)),
    )(dm, ua, ub, uc, ud, z_gl, bias, woa, wob, woc, wod)


def pool_bwd(z_br, dua, pwbd, pscale, dz_buf):
    rows = z_br.shape[0]
    t = ROW_TILE
    n_steps = rows // t
    cur, prev, nxt, full = _tile_specs(t, rows // HALO)

    def body(zc_ref, zp_ref, zn_ref, dc_ref, dn_ref, pw_ref, ps_ref, _, dz_ref, dpw_ref, dps_ref):
        i = pl.program_id(0)
        zp = jnp.where(i == 0, jnp.zeros(zp_ref.shape, zp_ref.dtype), zp_ref[...])
        zn = jnp.where(i == n_steps - 1, jnp.zeros(zn_ref.shape, zn_ref.dtype), zn_ref[...])
        dun = jnp.where(i == n_steps - 1, jnp.zeros(dn_ref.shape, dn_ref.dtype), dn_ref[...])

        def ext(lo):
            return jnp.concatenate([zp[:, lo:lo + 256], zc_ref[:, lo:lo + 256], zn[:, lo:lo + 256]], axis=0).astype(F32)

        n_ext = t + 2 * HALO
        v, pg = ext(PV), ext(PG)
        cnt = _pool_counts(i * t - HALO, n_ext)
        p = (_pool_window_sums(v, _sh) / cnt - v)[HALO:HALO + t]
        du = jnp.concatenate([jnp.zeros((HALO, 256), F32), dc_ref[...], dun], axis=0)
        dya = du * _silu(pg)
        dypb = (dya * ps_ref[...]).astype(BF16)
        dp = _dot_nt(dypb, pw_ref[...])
        dv = (_pool_window_sums(dp / cnt, _ash) - dp)[HALO:HALO + t]
        pb = p.astype(BF16)
        pw = _dot(pb, pw_ref[...])
        duc, pgc = dc_ref[...], pg[HALO:HALO + t]
        dpg = duc * pw * ps_ref[...] * _silu_grad(pgc)
        dz_ref[...] = jnp.concatenate([dv, dpg], axis=1).astype(BF16)
        _accumulate(i, dpw_ref, _dot_tn(pb, dypb[HALO:HALO + t]))
        _accumulate(i, dps_ref, jnp.sum(dya[HALO:HALO + t] * pw, axis=0, keepdims=True))

    return pl.pallas_call(
        body, name="pool_bwd", grid=(n_steps,),
        in_specs=[cur(ZB), prev(ZB), nxt(ZB), cur(256), nxt(256), full((256, 256)), full((1, 256)), ANY],
        out_specs=[cur(512, PV // 512), full((256, 256)), full((1, 256))],
        out_shape=[jax.ShapeDtypeStruct((rows, ZB), BF16), jax.ShapeDtypeStruct((256, 256), F32), jax.ShapeDtypeStruct((1, 256), F32)],
        input_output_aliases={7: 0}, compiler_params=_params(("arbitrary",)),
    )(z_br, z_br, z_br, dua, dua, pwbd, pscale, dz_buf)


def shortconv_bwd(z_br, dud, sc_w, dz_buf):
    rows = z_br.shape[0]
    t = ROW_TILE
    n_steps = rows // t
    cur, prev, nxt, full = _tile_specs(t, rows // HALO)

    def body(zc_ref, zp_ref, zn_ref, dc_ref, dn_ref, sw_ref, _, dz_ref, dw_ref):
        i = pl.program_id(0)
        zp = jnp.where(i == 0, jnp.zeros(zp_ref.shape, zp_ref.dtype), zp_ref[...])
        zn = jnp.where(i == n_steps - 1, jnp.zeros(zn_ref.shape, zn_ref.dtype), zn_ref[...])
        dun = jnp.where(i == n_steps - 1, jnp.zeros(dn_ref.shape, dn_ref.dtype), dn_ref[...])

        def ext(lo):
            return jnp.concatenate([zp[:, lo:lo + 256], zc_ref[:, lo:lo + 256], zn[:, lo:lo + 256]], axis=0).astype(F32)

        mid = slice(HALO, HALO + t)
        bg, c2, xv, sg = ext(BG), ext(C2), ext(XV), ext(SG)
        du = jnp.concatenate([jnp.zeros((HALO, 256), F32), dc_ref[...], dun], axis=0)
        e = c2 * xv
        shifted = [_sh(e, SC_K - 1 - k) for k in range(SC_K)]
        f = sum(sw_ref[k:k + 1, :] * shifted[k] for k in range(SC_K))
        gate = _silu(sg)
        df = du * gate * bg
        de = sum(sw_ref[k:k + 1, :] * _ash(df, SC_K - 1 - k) for k in range(SC_K))
        dbg = du * gate * f
        dsg = du * bg * f * _silu_grad(sg)
        dz_ref[...] = jnp.concatenate([dbg[mid], (de * xv)[mid], (de * c2)[mid], dsg[mid]], axis=1).astype(BF16)
        dw = jnp.concatenate([jnp.sum((df * shifted[k])[mid], axis=0, keepdims=True) for k in range(SC_K)] + [
            jnp.zeros((8 - SC_K, 256), F32)], axis=0)
        _accumulate(i, dw_ref, dw)

    return pl.pallas_call(
        body, name="shortconv_bwd", grid=(n_steps,), in_specs=[cur(ZB), prev(ZB), nxt(ZB), cur(256), nxt(256), full((8, 256)), ANY],
        out_specs=[cur(1024, BG // 1024), full((8, 256))],
        out_shape=[jax.ShapeDtypeStruct((rows, ZB), BF16), jax.ShapeDtypeStruct((8, 256), F32)],
        input_output_aliases={6: 0}, compiler_params=_params(("arbitrary",)),
    )(z_br, z_br, z_br, dud, dud, sc_w, dz_buf)


def conformer_bwd_tail(z_br, duc, conf_w, conf_vec, dz_buf):
    rows = z_br.shape[0]
    t = ROW_TILE
    cur, prev, _, full = _tile_specs(t, rows // HALO)

    def body(zc_ref, zp_ref, du_ref, cw_ref, cv_ref, _, dc_ref, dcg_ref, dv_ref):
        i = pl.program_id(0)
        zp = jnp.where(i == 0, jnp.zeros(zp_ref.shape, zp_ref.dtype), zp_ref[...])

        def ext(lo):
            return jnp.concatenate([zp[:, lo:lo + 256], zc_ref[:, lo:lo + 256]], axis=0).astype(F32)

        g1 = ext(CA) * _sigmoid(ext(CGT))
        c = _conf_conv(g1, cw_ref)[HALO:] + cv_ref[0:1, :]
        _, vjp = jax.vjp(_conf_tail, c, zc_ref[:, CG:CG + 256].astype(F32), cv_ref[1:2, :], cv_ref[2:3, :])
        dc, dcg, dlg, dlb = vjp(du_ref[...])
        dc_ref[...] = dc
        dcg_ref[...] = dcg.astype(BF16)
        dvec = jnp.concatenate([dlg, dlb, jnp.sum(dc, axis=0, keepdims=True), jnp.zeros((5, 256), F32)], axis=0)
        _accumulate(i, dv_ref, dvec)

    return pl.pallas_call(
        body, name="conformer_bwd_tail", grid=(rows // t,), in_specs=[cur(ZB), prev(ZB), cur(256), full((32, 256)), full((8, 256)), ANY],
        out_specs=[cur(256), cur(256, CG // 256), full((8, 256))],
        out_shape=[jax.ShapeDtypeStruct((rows, 256), F32), jax.ShapeDtypeStruct((rows, ZB), BF16), jax.ShapeDtypeStruct((8, 256), F32)],
        input_output_aliases={5: 1}, compiler_params=_params(("arbitrary",)),
    )(z_br, z_br, duc, conf_w, conf_vec, dz_buf)


def conformer_bwd_conv(z_br, dc, conf_w, dz_buf):
    rows = z_br.shape[0]
    t = ROW_TILE
    n_steps = rows // t
    cur, prev, nxt, full = _tile_specs(t, rows // HALO)

    def body(zc_ref, zp_ref, dc_ref, dn_ref, cw_ref, _, dz_ref, dw_ref):
        i = pl.program_id(0)
        zp = jnp.where(i == 0, jnp.zeros(zp_ref.shape, zp_ref.dtype), zp_ref[...])
        dcn = jnp.where(i == n_steps - 1, jnp.zeros(dn_ref.shape, dn_ref.dtype), dn_ref[...])

        def ext(lo):
            return jnp.concatenate([zp[:, lo:lo + 256], zc_ref[:, lo:lo + 256]], axis=0).astype(F32)

        a, gt = ext(CA), ext(CGT)
        sg = _sigmoid(gt)
        g1 = a * sg
        dc = dc_ref[...]
        dce = jnp.concatenate([dc, dcn], axis=0)
        dg1 = jnp.zeros_like(dce)
        dws = []
        for k in range(CONF_K):
            dg1 = dg1 + cw_ref[k:k + 1, :] * _ash(dce, CONF_K - 1 - k)
            dws.append(jnp.sum(dc * _sh(g1, CONF_K - 1 - k)[HALO:], axis=0, keepdims=True))
        dg1 = dg1[:t]
        ac, sc = a[HALO:], sg[HALO:]
        dz_ref[...] = jnp.concatenate([dg1 * sc, dg1 * ac * sc * (1.0 - sc)], axis=1).astype(BF16)
        _accumulate(i, dw_ref, jnp.concatenate(dws + [jnp.zeros((32 - CONF_K, 256), F32)], axis=0))

    return pl.pallas_call(
        body, name="conformer_bwd_conv", grid=(n_steps,), in_specs=[cur(ZB), prev(ZB), cur(256), nxt(256), full((32, 256)), ANY],
        out_specs=[cur(512, CA // 512), full((32, 256))],
        out_shape=[jax.ShapeDtypeStruct((rows, ZB), BF16), jax.ShapeDtypeStruct((32, 256), F32)],
        input_output_aliases={5: 0}, compiler_params=_params(("arbitrary",)),
    )(z_br, z_br, dc, dc, conf_w, dz_buf)


def attention_bwd_prep(dub, o_att, z_br, dz_buf):
    rows = dub.shape[0]
    t = ROW_TILE
    cur, _, _, _ = _tile_specs(t, rows // HALO)

    def body(du_ref, o_ref, mg_ref, _, do_ref, dmg_ref, delta_ref):
        du, o, mg = du_ref[...], o_ref[...].astype(F32), mg_ref[...].astype(F32)
        do = du * _silu(mg)
        do_ref[...] = do.astype(BF16)
        dmg_ref[...] = (du * o * _silu_grad(mg)).astype(BF16)
        prod = do * o
        lane = lax.broadcasted_iota(jnp.int32, (1, HEADS * V_DIM), 1)
        for h in range(HEADS):
            part = jnp.where((lane >= V_DIM * h) & (lane < V_DIM * (h + 1)), prod, 0.0)
            delta_ref[h] = jnp.broadcast_to(jnp.sum(part, axis=-1, keepdims=True), (t, LANES))

    return pl.pallas_call(
        body, name="attention_bwd_prep", grid=(rows // t,), in_specs=[cur(512), cur(512), cur(512, MG // 512), ANY],
        out_specs=[cur(512), cur(512, MG // 512), pl.BlockSpec((HEADS, t, LANES), lambda i: (0, i, 0))],
        out_shape=[jax.ShapeDtypeStruct((rows, 512), BF16), jax.ShapeDtypeStruct((rows, ZB), BF16),
                   jax.ShapeDtypeStruct((HEADS, rows, LANES), F32)],
        input_output_aliases={3: 1}, compiler_params=_params(("parallel",)),
    )(dub, o_att, z_br, dz_buf)


def attention_bwd(q, k, v, do, lse, delta):
    rows = q.shape[0]
    tq = ROW_TILE
    nq = rows // tq

    def body(q_ref, k_ref, v_ref, do_ref, lse_ref, dl_ref, dq_ref, dk_ref, dv_ref):
        j = pl.program_id(1)

        @pl.when(j == 0)
        def _():
            dq_ref[...] = jnp.zeros_like(dq_ref)

        row = lax.broadcasted_iota(jnp.int32, (tq, tq), 0)
        colm = lax.broadcasted_iota(jnp.int32, (tq, tq), 1)

        def head_step(h, i, dk, dv, diagonal):
            lanes = slice(HEAD_PAD * h, HEAD_PAD * (h + 1))
            hm = _head_lane_mask(h)
            kh = k_ref[:, lanes]
            vh = jnp.where(hm, v_ref[...], jnp.zeros((), BF16))
            r0 = pl.multiple_of(i * tq, tq)
            qi = q_ref[pl.ds(r0, tq), lanes]
            doi = jnp.where(hm, do_ref[pl.ds(r0, tq), :], jnp.zeros((), BF16))
            s = _dot_nt(qi, kh)
            if diagonal:
                s = jnp.where(colm <= row, s, -1e30)
            pr = jnp.exp(s - lse_ref[h, pl.ds(r0, tq), :][:, 0:1])
            dv = dv + _dot_tn(pr.astype(BF16), doi)
            dp = _dot_nt(doi, vh)
            ds = (pr * (dp - dl_ref[h, pl.ds(r0, tq), :][:, 0:1])).astype(BF16)
            dq_ref[pl.ds(r0, tq), lanes] += _dot(ds, kh)
            return dk + _dot_tn(ds, qi), dv

        def step(i, carry, diagonal):
            dk0, dk1, dv = carry
            dk0, dv = head_step(0, i, dk0, dv, diagonal)
            dk1, dv = head_step(1, i, dk1, dv, diagonal)
            return dk0, dk1, dv

        zero = jnp.zeros((tq, HEAD_PAD), F32)
        carry = step(j, (zero, zero, jnp.zeros((tq, 2 * V_DIM), F32)), True)
        dk0, dk1, dv = lax.fori_loop(j + 1, nq, lambda i, cr: step(i, cr, False), carry)
        dk_ref[:, 0:HEAD_PAD] = dk0
        dk_ref[:, HEAD_PAD:2 * HEAD_PAD] = dk1
        dv_ref[...] = dv

    return pl.pallas_call(
        body, name="attention_bwd", grid=(HEADS // 2, nq),
        in_specs=[pl.BlockSpec((rows, 2 * HEAD_PAD), lambda p, j: (0, p)), pl.BlockSpec((tq, 2 * HEAD_PAD), lambda p, j: (j, p)),
                  pl.BlockSpec((tq, 2 * V_DIM), lambda p, j: (j, p)), pl.BlockSpec((rows, 2 * V_DIM), lambda p, j: (0, p)),
                  pl.BlockSpec((2, rows, LANES), lambda p, j: (p, 0, 0)), pl.BlockSpec((2, rows, LANES), lambda p, j: (p, 0, 0))],
        out_specs=[pl.BlockSpec((rows, 2 * HEAD_PAD), lambda p, j: (0, p)), pl.BlockSpec((tq, 2 * HEAD_PAD), lambda p, j: (j, p)),
                   pl.BlockSpec((tq, 2 * V_DIM), lambda p, j: (j, p))],
        out_shape=[jax.ShapeDtypeStruct((rows, HEADS * HEAD_PAD), F32), jax.ShapeDtypeStruct((rows, HEADS * HEAD_PAD), F32),
                   jax.ShapeDtypeStruct((rows, HEADS * V_DIM), F32)],
        compiler_params=_params(("parallel", "arbitrary")),
    )(q, k, v, do, lse, delta)


def mla_prep_bwd(dq, dk, dv, z_br, rope, gq, wuq, gkv, wukv, dz_buf):
    rows = dq.shape[0]
    t = ROW_TILE
    cur, _, _, full = _tile_specs(t, rows // HALO)
    w8 = HEADS * HEAD_PAD

    def body(dq_ref, dk_ref, dv_ref, z_ref, rope_ref, gq_ref, wuq_ref, gkv_ref, wukv_ref, _, dz_ref, dwuq_ref, dwukv_ref, dgq_ref, dgkv_ref):
        i = pl.program_id(0)
        cth, s1, s2 = rope_ref[:, 0:128], rope_ref[:, 128:256], rope_ref[:, 256:384]
        dqb = _rope_transposed(dq_ref[...] * Q_SCALE, _lanes8(cth), _lanes8(s1), _lanes8(s2), w8).astype(BF16)
        cq = z_ref[:, 0:256].astype(F32)
        qn, vjp_q = jax.vjp(_rms, cq, gq_ref[...])
        _accumulate(i, dwuq_ref, _dot_tn(qn.astype(BF16), dqb))
        dcq, dgq = vjp_q(_dot_nt(dqb, wuq_ref[...]))
        _accumulate(i, dgq_ref, dgq)

        dk = dk_ref[...]
        dkr = sum(dk[:, HEAD_PAD * h:HEAD_PAD * (h + 1)] for h in range(HEADS))
        dkr = _rope_transposed(dkr, cth, s1, s2, HEAD_PAD)
        lane = lax.broadcasted_iota(jnp.int32, (1, HEAD_PAD), 1)
        dkr = jnp.where((lane >= QK_NOPE) & (lane < QK_NOPE + QK_ROPE), dkr, 0.0)
        dkvb = jnp.concatenate([dk, dv_ref[...]], axis=1).astype(BF16)
        ckv = z_ref[:, 256:384].astype(F32)
        kvn, vjp_kv = jax.vjp(_rms, ckv, gkv_ref[...])
        _accumulate(i, dwukv_ref, _dot_tn(kvn.astype(BF16), dkvb))
        dckv, dgkv = vjp_kv(_dot_nt(dkvb, wukv_ref[...]))
        _accumulate(i, dgkv_ref, dgkv)
        dz_ref[...] = jnp.concatenate([dcq, dckv, dkr], axis=1).astype(BF16)

    return pl.pallas_call(
        body, name="mla_prep_bwd", grid=(rows // t,),
        in_specs=[cur(w8), cur(w8), cur(512), cur(512, CQ // 512), cur(384), full((1, 256)), full((256, w8)), full((1, 128)),
                  full((128, w8 + 512)), ANY],
        out_specs=[cur(512, CQ // 512), full((256, w8)), full((128, w8 + 512)), full((1, 256)), full((1, 128))],
        out_shape=[jax.ShapeDtypeStruct((rows, ZB), BF16), jax.ShapeDtypeStruct((256, w8), F32), jax.ShapeDtypeStruct((128, w8 + 512), F32),
                   jax.ShapeDtypeStruct((1, 256), F32), jax.ShapeDtypeStruct((1, 128), F32)],
        input_output_aliases={9: 0}, compiler_params=_params(("arbitrary",)),
    )(dq, dk, dv, z_br, rope, gq, wuq, gkv, wukv, dz_buf)


def prenorm_bwd(dz_br, w_br, dh_gl, hres, gpre, dh_next):
    rows, d = hres.shape
    t = ROW_TILE
    cur, _, _, full = _tile_specs(t, rows // HALO)

    def body(dz_ref, w_ref, dp_ref, x_ref, g_ref, dn_ref, dx_ref, dg_ref):
        i = pl.program_id(0)
        dh = _dot_nt(dz_ref[...], w_ref[...]) + dp_ref[...]
        _, vjp = jax.vjp(_rms, x_ref[...], g_ref[...])
        dx, dg = vjp(dh)
        dx_ref[...] = dx + dn_ref[...]
        _accumulate(i, dg_ref, dg)

    return pl.pallas_call(
        body, name="prenorm_bwd", grid=(rows // t,), in_specs=[cur(ZB), full((d, ZB)), cur(d), cur(d), full((1, d)), cur(d)],
        out_specs=[cur(d), full((1, d))], out_shape=[jax.ShapeDtypeStruct((rows, d), F32), jax.ShapeDtypeStruct((1, d), F32)],
        compiler_params=_params(("arbitrary",)),
    )(dz_br, w_br, dh_gl, hres, gpre, dh_next)


def _mesh_position():
    return lax.axis_index("x"), lax.axis_index("y"), lax.axis_index("c")


def chip_exchange(src, gather, name):
    block = src.shape if gather else src.shape[1:]

    def body(src_ref, dst_ref, send_sems, recv_sems, local_sem):
        x, y, c = _mesh_position()
        me = 2 * x + y
        peers = ((1 - x, y), (x, 1 - y), (1 - x, 1 - y))

        def part(k):
            return src_ref if gather else src_ref.at[k]

        def copy(j, slot):
            px, py = peers[j]
            return pltpu.make_async_remote_copy(src_ref=part(2 * px + py), dst_ref=dst_ref.at[slot], send_sem=send_sems.at[j],
                                                recv_sem=recv_sems.at[j], device_id=(px, py, c), device_id_type=MESH)

        local = pltpu.make_async_copy(part(me), dst_ref.at[me], local_sem)
        local.start()
        sends = [copy(j, me) for j in range(3)]
        for cp in sends:
            cp.start()
        for j, (px, py) in enumerate(peers):
            copy(j, 2 * px + py).wait_recv()
        for cp in sends:
            cp.wait_send()
        local.wait()

    return pl.pallas_call(
        body, name=name, in_specs=[pl.BlockSpec(memory_space=pl.ANY)], out_specs=pl.BlockSpec(memory_space=pl.ANY),
        out_shape=jax.ShapeDtypeStruct((N_CHIPS,) + tuple(block), src.dtype),
        scratch_shapes=[pltpu.SemaphoreType.DMA((3,)), pltpu.SemaphoreType.DMA((3,)), pltpu.SemaphoreType.DMA(())],
    )(src)


def sibling_swap(src, name):
    def body(src_ref, dst_ref, send_sem, recv_sem):
        x, y, c = _mesh_position()
        cp = pltpu.make_async_remote_copy(src_ref=src_ref, dst_ref=dst_ref, send_sem=send_sem, recv_sem=recv_sem,
                                          device_id=(x, y, 1 - c), device_id_type=MESH)
        cp.start()
        cp.wait()

    return pl.pallas_call(
        body, name=name, in_specs=[pl.BlockSpec(memory_space=pl.ANY)], out_specs=pl.BlockSpec(memory_space=pl.ANY),
        out_shape=jax.ShapeDtypeStruct(src.shape, src.dtype),
        scratch_shapes=[pltpu.SemaphoreType.DMA(()), pltpu.SemaphoreType.DMA(())],
    )(src)


def _comm_call(body, name, n_in, out_shapes, n_sems):
    return pl.pallas_call(
        body, name=name, in_specs=[ANY] * n_in, out_specs=[ANY] * len(out_shapes), out_shape=out_shapes,
        scratch_shapes=[pltpu.SemaphoreType.DMA((n,)) for n in n_sems])


def _layer_halves(c):
    return pl.ds((DEPTH // 2) * c, DEPTH // 2), pl.ds((DEPTH // 2) * (1 - c), DEPTH // 2)


def gather_layers(srcs, name):
    n = len(srcs)

    def body(*refs):
        src, dst = refs[:n], refs[n:2 * n]
        ici_send, ici_recv, d2d_send, d2d_recv = refs[2 * n:]
        x, y, c = _mesh_position()
        me = 2 * x + y
        peers = ((1 - x, y), (x, 1 - y), (1 - x, 1 - y))
        mine, other = _layer_halves(c)

        def fetch(a, j, slot):
            px, py = peers[j]
            return pltpu.make_async_remote_copy(src_ref=src[a].at[mine], dst_ref=dst[a].at[mine, slot], send_sem=ici_send.at[3 * a + j],
                                                recv_sem=ici_recv.at[3 * a + j], device_id=(px, py, c), device_id_type=MESH)

        def forward(a, j, half):
            px, py = peers[j]
            part = dst[a].at[half, 2 * px + py]
            return pltpu.make_async_remote_copy(src_ref=part, dst_ref=part, send_sem=d2d_send.at[3 * a + j],
                                                recv_sem=d2d_recv.at[3 * a + j], device_id=(x, y, 1 - c), device_id_type=MESH)

        sends = [fetch(a, j, me) for a in range(n) for j in range(3)]
        for cp in sends:
            cp.start()
        passed = []
        for j, (px, py) in enumerate(peers):
            for a in range(n):
                fetch(a, j, 2 * px + py).wait_recv()
                passed.append(forward(a, j, mine))
                passed[-1].start()
        for j in range(3):
            for a in range(n):
                forward(a, j, other).wait_recv()
        for cp in sends + passed:
            cp.wait_send()

    outs = [jax.ShapeDtypeStruct((DEPTH, N_CHIPS) + s.shape[1:], s.dtype) for s in srcs]
    gathered = _comm_call(body, name, n, outs, (3 * n, 3 * n, 3 * n, 3 * n))(*srcs)
    me = 2 * lax.axis_index("x") + lax.axis_index("y")
    return [lax.dynamic_update_slice(g, s[:, None], (0, me, 0, 0)) for g, s in zip(gathered, srcs)]


def swap_layer_halves(ps, name):
    n = len(ps)

    def body(*refs):
        src, dst = refs[:n], refs[n:2 * n]
        send_sems, recv_sems = refs[2 * n:]
        x, y, c = _mesh_position()
        _, other = _layer_halves(c)
        copies = [pltpu.make_async_remote_copy(src_ref=src[a].at[other], dst_ref=dst[a], send_sem=send_sems.at[a], recv_sem=recv_sems.at[a],
                                               device_id=(x, y, 1 - c), device_id_type=MESH) for a in range(n)]
        for cp in copies:
            cp.start()
        for cp in copies:
            cp.wait()

    outs = [jax.ShapeDtypeStruct((DEPTH // 2,) + p.shape[1:], p.dtype) for p in ps]
    return _comm_call(body, name, n, outs, (n, n))(*ps)


def exchange_chip_sums(ss, name):
    n = len(ss)

    def body(*refs):
        src, dst = refs[:n], refs[n:2 * n]
        send_sems, recv_sems, local_sems = refs[2 * n:]
        x, y, c = _mesh_position()
        me = 2 * x + y
        peers = ((1 - x, y), (x, 1 - y), (1 - x, 1 - y))

        def copy(a, j, slot):
            px, py = peers[j]
            return pltpu.make_async_remote_copy(src_ref=src[a].at[:, 2 * px + py], dst_ref=dst[a].at[:, slot], send_sem=send_sems.at[3 * a + j],
                                                recv_sem=recv_sems.at[3 * a + j], device_id=(px, py, c), device_id_type=MESH)

        local = [pltpu.make_async_copy(src[a].at[:, me], dst[a].at[:, me], local_sems.at[a]) for a in range(n)]
        sends = [copy(a, j, me) for a in range(n) for j in range(3)]
        for cp in local + sends:
            cp.start()
        for j, (px, py) in enumerate(peers):
            for a in range(n):
                copy(a, j, 2 * px + py).wait_recv()
        for cp in sends:
            cp.wait_send()
        for cp in local:
            cp.wait()

    outs = [jax.ShapeDtypeStruct(s.shape, s.dtype) for s in ss]
    return _comm_call(body, name, n, outs, (3 * n, 3 * n, n))(*ss)


def share_layer_halves(gs, name):
    n = len(gs)

    def body(*refs):
        dst = refs[n:2 * n]
        send_sems, recv_sems = refs[2 * n:]
        x, y, c = _mesh_position()
        mine, other = _layer_halves(c)

        def copy(a, half):
            return pltpu.make_async_remote_copy(src_ref=dst[a].at[half], dst_ref=dst[a].at[half], send_sem=send_sems.at[a],
                                                recv_sem=recv_sems.at[a], device_id=(x, y, 1 - c), device_id_type=MESH)

        sends = [copy(a, mine) for a in range(n)]
        for cp in sends:
            cp.start()
        for a in range(n):
            copy(a, other).wait_recv()
        for cp in sends:
            cp.wait_send()

    return pl.pallas_call(
        body, name=name, in_specs=[ANY] * n, out_specs=[ANY] * n, out_shape=[jax.ShapeDtypeStruct(g.shape, g.dtype) for g in gs],
        input_output_aliases={a: a for a in range(n)}, scratch_shapes=[pltpu.SemaphoreType.DMA((n,)), pltpu.SemaphoreType.DMA((n,))],
    )(*gs)


def _row_block(rows, cols, itemsize):
    best = 16
    for rb in range(16, rows + 1, 16):
        if rows % rb == 0 and rb * cols * itemsize <= 2 * 1024 * 1024:
            best = rb
    assert rows % best == 0, (rows, cols)
    return best


def add_sibling_half(p, r, c, name):
    cols = p.shape[-1]
    p2, r2 = p.reshape(-1, cols), r.reshape(-1, cols)
    rows = r2.shape[0]
    rb = _row_block(rows, cols, 2)
    steps = rows // rb

    def body(c_ref, p_ref, r_ref, o_ref):
        o_ref[...] = (p_ref[...].astype(F32) + r_ref[...].astype(F32)).astype(BF16)

    out = pl.pallas_call(
        body, name=name, out_shape=jax.ShapeDtypeStruct((rows, cols), BF16),
        grid_spec=pltpu.PrefetchScalarGridSpec(
            num_scalar_prefetch=1, grid=(steps,), in_specs=[pl.BlockSpec((rb, cols), lambda i, c_ref: (c_ref[0] * steps + i, 0)),
                                                            pl.BlockSpec((rb, cols), lambda i, c_ref: (i, 0))],
            out_specs=pl.BlockSpec((rb, cols), lambda i, c_ref: (i, 0))),
        compiler_params=_params(("parallel",)),
    )(jnp.reshape(c, (1,)).astype(jnp.int32), p2, r2)
    return out.reshape(r.shape)


def sum_chip_slots(l, c, name):
    layers, n, rows, cols = l.shape
    rb = _row_block(rows, cols, 4)

    def body(c_ref, l_ref, o_ref):
        acc = l_ref[0, 0].astype(F32)
        for s in range(1, n):
            acc = acc + l_ref[0, s].astype(F32)
        o_ref[0] = acc

    return pl.pallas_call(
        body, name=name, out_shape=jax.ShapeDtypeStruct((DEPTH, rows, cols), F32),
        grid_spec=pltpu.PrefetchScalarGridSpec(
            num_scalar_prefetch=1, grid=(layers, rows // rb),
            in_specs=[pl.BlockSpec((1, n, rb, cols), lambda a, i, c_ref: (a, 0, i, 0))],
            out_specs=pl.BlockSpec((1, rb, cols), lambda a, i, c_ref: (c_ref[0] * layers + a, i, 0))),
        compiler_params=_params(("parallel", "parallel")),
    )(jnp.reshape(c, (1,)).astype(jnp.int32), l)


def _comm_block(rows):
    return 1024 if rows % 1024 == 0 else rows


def sum_slots(buf, name):
    n, r, c = buf.shape
    rb = _comm_block(r)

    def body(b_ref, o_ref):
        acc = b_ref[0].astype(F32)
        for s in range(1, n):
            acc = acc + b_ref[s].astype(F32)
        o_ref[...] = acc

    return pl.pallas_call(
        body, name=name, grid=(r // rb,), in_specs=[pl.BlockSpec((n, rb, c), lambda i: (0, i, 0))],
        out_specs=pl.BlockSpec((rb, c), lambda i: (i, 0)), out_shape=jax.ShapeDtypeStruct((r, c), F32),
        compiler_params=_params(("parallel",)),
    )(buf)


def add_pair(a, b, out_dtype, name):
    shape = a.shape
    a2, b2 = a.reshape(-1, shape[-1]), b.reshape(-1, shape[-1])
    r, c = a2.shape
    rb = _comm_block(r)

    def body(a_ref, b_ref, o_ref):
        o_ref[...] = (a_ref[...].astype(F32) + b_ref[...].astype(F32)).astype(out_dtype)

    out = pl.pallas_call(
        body, name=name, grid=(r // rb,), in_specs=[pl.BlockSpec((rb, c), lambda i: (i, 0))] * 2,
        out_specs=pl.BlockSpec((rb, c), lambda i: (i, 0)), out_shape=jax.ShapeDtypeStruct((r, c), out_dtype),
        compiler_params=_params(("parallel",)),
    )(a2, b2)
    return out.reshape(shape)


def adamw(w, g, m, v):
    shape = w.shape
    cols = shape[-1]
    rows = math.prod(shape[:-1])
    rb = rows if rows * cols <= 256 * 1024 else 256
    assert rows % rb == 0, shape

    def body(w_ref, g_ref, m_ref, v_ref, d_ref, nm_ref, nv_ref):
        g_ = g_ref[...]
        nm = ADAM_B1 * m_ref[...] + (1.0 - ADAM_B1) * g_
        nv = ADAM_B2 * v_ref[...] + (1.0 - ADAM_B2) * (g_ * g_)
        m_hat = nm / (1.0 - ADAM_B1 ** ADAM_STEP)
        v_hat = nv / (1.0 - ADAM_B2 ** ADAM_STEP)
        d_ref[...] = -ADAM_LR * (m_hat / (jnp.sqrt(v_hat) + ADAM_EPS) + ADAM_WD * w_ref[...])
        nm_ref[...] = nm
        nv_ref[...] = nv

    spec = pl.BlockSpec((rb, cols), lambda i: (i, 0))
    outs = pl.pallas_call(
        body, name="adamw", grid=(rows // rb,), in_specs=[spec] * 4, out_specs=[spec] * 3,
        out_shape=[jax.ShapeDtypeStruct((rows, cols), F32)] * 3, compiler_params=_params(("parallel",)),
    )(*(a.reshape(rows, cols) for a in (w, g, m, v)))
    return tuple(o.reshape(shape) for o in outs)


def _pack(arrays, dtype, row_multiple):
    flat = jnp.concatenate([a.astype(dtype).reshape(-1) for a in arrays])
    per = LANES * row_multiple
    total = -(-flat.shape[0] // per) * per
    return jnp.pad(flat, (0, total - flat.shape[0])).reshape(total // LANES, LANES)


def _unpack(buf, shapes):
    flat = buf.reshape(-1)
    out, off = [], 0
    for s in shapes:
        n = math.prod(s)
        out.append(flat[off:off + n].reshape(s))
        off += n
    return out


def _branch_columns(w):
    pad = lambda n: jnp.zeros(w.shape[:-1] + (n,), w.dtype)
    return jnp.concatenate([w[..., 2208:3232], w[..., 0:896], pad(64), w[..., 896:928], pad(32), w[..., 928:2208]], axis=-1)


def _branch_columns_inverse(dw_br, dw_gl):
    return jnp.concatenate([dw_br[..., 1024:1920], dw_br[..., 1984:2016], dw_br[..., 2048:ZB], dw_br[..., 0:1024], dw_gl], axis=-1)


def _uq_layout(w):
    r = w.reshape(w.shape[0], HEADS, QK_NOPE + QK_ROPE)
    return jnp.pad(r, ((0, 0), (0, 0), (0, HEAD_PAD - QK_NOPE - QK_ROPE))).reshape(w.shape[0], HEADS * HEAD_PAD)


def _uq_layout_inverse(dw):
    return dw.reshape(dw.shape[0], HEADS, HEAD_PAD)[:, :, :QK_NOPE + QK_ROPE].reshape(dw.shape[0], HEADS * (QK_NOPE + QK_ROPE))


def _ukv_layout(w):
    r = w.reshape(w.shape[0], HEADS, QK_NOPE + V_DIM)
    kp = jnp.pad(r[:, :, :QK_NOPE], ((0, 0), (0, 0), (0, HEAD_PAD - QK_NOPE))).reshape(w.shape[0], HEADS * HEAD_PAD)
    return jnp.concatenate([kp, r[:, :, QK_NOPE:].reshape(w.shape[0], HEADS * V_DIM)], axis=1)


def _ukv_layout_inverse(dw):
    n = dw.shape[0]
    dk = dw[:, :HEADS * HEAD_PAD].reshape(n, HEADS, HEAD_PAD)[:, :, :QK_NOPE]
    dv = dw[:, HEADS * HEAD_PAD:].reshape(n, HEADS, V_DIM)
    return jnp.concatenate([dk, dv], axis=2).reshape(n, HEADS * (QK_NOPE + V_DIM))


def _block_diag(pw):
    out = jnp.zeros((256, 256), pw.dtype)
    for g in range(4):
        out = lax.dynamic_update_slice(out, pw[g], (64 * g, 64 * g))
    return out


def _block_diag_inverse(d):
    return jnp.stack([d[64 * g:64 * (g + 1), 64 * g:64 * (g + 1)] for g in range(4)])


def _pad_rows(a, n):
    return jnp.pad(a, ((0, n - a.shape[0]), (0, 0)))


def _rope_tables(rows):
    inv = 1.0 / (ROPE_THETA ** (jnp.arange(0, QK_ROPE, 2, dtype=F32) / QK_ROPE))
    ang = jnp.arange(rows, dtype=F32)[:, None] * inv[None, :]
    cos, sin = jnp.cos(ang), jnp.sin(ang)
    one, zero = jnp.ones((rows, 1), F32), jnp.zeros((rows, 1), F32)
    rep = lambda a, n: jnp.broadcast_to(a, (rows, n))
    c = jnp.concatenate([rep(one, 64), cos, cos, rep(one, 32)], axis=1)
    s1 = jnp.concatenate([rep(zero, 64), -sin, rep(zero, 48)], axis=1)
    s2 = jnp.concatenate([rep(zero, 80), sin, rep(zero, 32)], axis=1)
    return jnp.concatenate([c, s1, s2], axis=1)


def _misc_block(parts):
    out = []
    for name, rows in MISC:
        a = parts[name]
        if name == "w_o":
            a = a.reshape(a.shape[:-2] + (rows, 256))
        elif name == "w_uq":
            a = jnp.pad(a, [(0, 0)] * (a.ndim - 1) + [(0, 256 - a.shape[-1])])
        out.append(a)
    return jnp.concatenate(out, axis=-2)


def _misc_unblock(block):
    out, off = {}, 0
    for name, rows in MISC:
        a = block[..., off:off + rows, :]
        off += rows
        if name == "w_o":
            a = a.reshape(a.shape[:-2] + (256, D_MODEL))
        elif name == "w_uq":
            a = a[..., :192]
        out[name] = a
    return out


def _to_chip_blocks(name, a):
    if name == "w_o":
        return a.reshape(a.shape[:-2] + (N_CHIPS, a.shape[-2] // N_CHIPS, a.shape[-1]))
    return jnp.swapaxes(a.reshape(a.shape[:-1] + (N_CHIPS, a.shape[-1] // N_CHIPS)), -3, -2)


def _from_chip_blocks(name, b):
    if name == "w_o":
        return b.reshape(b.shape[:-3] + (N_CHIPS * b.shape[-2], b.shape[-1]))
    s = jnp.swapaxes(b, -3, -2)
    return s.reshape(s.shape[:-2] + (N_CHIPS * s.shape[-1],))


def gather_weights(shards):
    misc = _misc_block({n: shards[n] for n, _ in MISC}).astype(BF16)
    g_in, g_misc = gather_layers([shards["w_in"].astype(BF16), misc], "gather_weights")
    out = {"w_in": _from_chip_blocks("w_in", g_in)}
    for name, blocks in _misc_unblock(g_misc).items():
        out[name] = _from_chip_blocks(name, blocks)
    small = chip_exchange(_pack([shards[n] for n, _, _ in SHARDED_SMALL], F32, 8), True, "gather_small_ici")
    per_chip = [_unpack(small[k], [s for _, s, _ in SHARDED_SMALL]) for k in range(N_CHIPS)]
    for idx, (name, _, axis) in enumerate(SHARDED_SMALL):
        out[name] = jnp.concatenate([per_chip[k][idx] for k in range(N_CHIPS)], axis=axis)
    return out


def reduce_large(grads, c):
    ps = [_to_chip_blocks("w_in", grads["w_in"]).astype(BF16),
          _misc_block({n: _to_chip_blocks(n, grads[n]) for n, _ in MISC}).astype(BF16)]
    rs = swap_layer_halves(ps, "reduce_grads_d2d")
    ss = [add_sibling_half(p, r, c, "reduce_grads_pair_%d" % a) for a, (p, r) in enumerate(zip(ps, rs))]
    ls = exchange_chip_sums(ss, "reduce_grads_ici")
    gs = [sum_chip_slots(l, c, "reduce_grads_sum_%d" % a) for a, l in enumerate(ls)]
    g_in, g_misc = share_layer_halves(gs, "reduce_grads_share")
    out = {"w_in": g_in}
    out.update(_misc_unblock(g_misc))
    return out


def reduce_small(grads, chip):
    names = [n for n, _ in REPLICATED] + [n for n, _, _ in SHARDED_SMALL]
    buf = _pack([grads[n] for n in names], F32, 8)
    chip_sum = add_pair(buf, sibling_swap(buf, "reduce_small_d2d"), F32, "reduce_small_pair")
    total = sum_slots(chip_exchange(chip_sum, True, "reduce_small_ici"), "reduce_small_sum")
    out = dict(zip(names, _unpack(total, [grads[n].shape for n in names])))
    for name, shape, axis in SHARDED_SMALL:
        out[name] = lax.dynamic_slice_in_dim(out[name], chip * shape[axis], shape[axis], axis)
    return out


def _layer_weights(w, i):
    w_in = w["w_in"][i]
    conf_w = jnp.pad(w["conf_dw_w"][i].astype(F32), ((0, 32 - CONF_K), (0, 0)))
    sc_w = jnp.pad(w["sc_dw_w"][i].astype(F32), ((0, 8 - SC_K), (0, 0)))
    conf_vec = jnp.concatenate([w["conf_dw_b"][i][None], w["conf_ln_g"][i][None], w["conf_ln_b"][i][None], jnp.zeros((5, 256), F32)])
    return dict(
        w_br=_branch_columns(w_in[:, :IN_W - ZG]).astype(BF16), w_gl=w_in[:, IN_W - ZG:].astype(BF16),
        gpre=w["pre_norm_g"][i][None], bias=w["gate_bias"][i][None], pwbd=_block_diag(w["pool_w"][i]).astype(BF16),
        pscale=w["pool_scale"][i][None], gq=w["q_norm_g"][i][None], wuq=_uq_layout(w["w_uq"][i]).astype(BF16),
        gkv=w["kv_norm_g"][i][None], wukv=_ukv_layout(w["w_ukv"][i]).astype(BF16), conf_w=conf_w, conf_vec=conf_vec, sc_w=sc_w,
        woa=w["w_out_pool"][i].astype(BF16), wob=w["w_out_mla"][i].astype(BF16), woc=w["w_out_conf"][i].astype(BF16),
        wod=w["w_out_sc"][i].astype(BF16), wo=w["w_o"][i].astype(BF16), gpost=w["post_norm_g"][i][None])


def local_step(x, target, w):
    seq = x.shape[0]
    length = N_META + seq
    rows = -(-length // ROW_TILE) * ROW_TILE
    bt = _big_tile(rows)
    hres = _pad_rows(jnp.concatenate([w["meta_tokens"].astype(F32), x], axis=0), rows)
    tgt = jnp.pad(target, ((N_META, rows - length), (0, 0)))
    rope = _rope_tables(rows)

    saved = []
    for i in range(DEPTH):
        lw = _layer_weights(w, i)
        z_br, hb = prenorm_project(hres, lw["gpre"], lw["w_br"])
        z_gl = matmul(hb, lw["w_gl"], "nn", BF16, bt, 1024, D_MODEL, "project_gates")
        ua, uc, ud, q, k, v = branches_fwd(z_br, rope, lw["pwbd"], lw["pscale"], lw["gq"], lw["wuq"], lw["gkv"], lw["wukv"],
                                           lw["conf_w"], lw["conf_vec"], lw["sc_w"])
        o_att, lse = attention_fwd(q, k, v)
        ub, mb, o, hnew = merge_fwd(ua, o_att, uc, ud, z_br, z_gl, lw["bias"], lw["woa"], lw["wob"], lw["woc"], lw["wod"], lw["wo"],
                                    lw["gpost"], hres)
        saved.append(dict(lw=lw, hres=hres, hb=hb, z_br=z_br, z_gl=z_gl, ua=ua, ub=ub, uc=uc, ud=ud, q=q, k=k, v=v, o_att=o_att,
                          lse=lse, mb=mb, o=o))
        hres = hnew

    dh, total = loss_head(hres, tgt, seq)

    g = {n: [None] * DEPTH for n in WEIGHT_ORDER if n != "meta_tokens"}
    for i in reversed(range(DEPTH)):
        s = saved[i]
        lw = s["lw"]
        dm, dwo, dgpost = postnorm_bwd(dh, s["o"], s["mb"], lw["wo"], lw["gpost"])
        dua, dub, duc, dud, dz_gl, dwa, dwb, dwc, dwd, dbias = merge_bwd(dm, s["ua"], s["ub"], s["uc"], s["ud"], s["z_gl"], lw["bias"],
                                                                     lw["woa"], lw["wob"], lw["woc"], lw["wod"])
        dz_br = lax.empty((rows, ZB), BF16)
        dz_br, dpw, dps = pool_bwd(s["z_br"], dua, lw["pwbd"], lw["pscale"], dz_br)
        dz_br, dsw = shortconv_bwd(s["z_br"], dud, lw["sc_w"], dz_br)
        dc, dz_br, dcvec = conformer_bwd_tail(s["z_br"], duc, lw["conf_w"], lw["conf_vec"], dz_br)
        dz_br, dcw = conformer_bwd_conv(s["z_br"], dc, lw["conf_w"], dz_br)
        do, dz_br, delta = attention_bwd_prep(dub, s["o_att"], s["z_br"], dz_br)
        dq, dk, dv = attention_bwd(s["q"], s["k"], s["v"], do, s["lse"], delta)
        dz_br, dwuq, dwukv, dgq, dgkv = mla_prep_bwd(dq, dk, dv, s["z_br"], rope, lw["gq"], lw["wuq"], lw["gkv"], lw["wukv"], dz_br)
        dw_br = matmul(s["hb"], dz_br, "tn", F32, D_MODEL, ZB // 2, ROW_TILE, "grad_w_branch")
        dw_gl = matmul(s["hb"], dz_gl, "tn", F32, D_MODEL, 1024, ROW_TILE, "grad_w_gates")
        dh_gl = matmul(dz_gl, lw["w_gl"], "nt", F32, bt, D_MODEL, 1024, "grad_h_gates")
        dh, dgpre = prenorm_bwd(dz_br, lw["w_br"], dh_gl, s["hres"], lw["gpre"], dh)

        g["pre_norm_g"][i] = dgpre[0]
        g["w_in"][i] = _branch_columns_inverse(dw_br, dw_gl)
        g["gate_bias"][i] = dbias[0]
        g["pool_w"][i] = _block_diag_inverse(dpw)
        g["pool_scale"][i] = dps[0]
        g["w_out_pool"][i] = dwa
        g["q_norm_g"][i] = dgq[0]
        g["w_uq"][i] = _uq_layout_inverse(dwuq)
        g["kv_norm_g"][i] = dgkv[0]
        g["w_ukv"][i] = _ukv_layout_inverse(dwukv)
        g["w_out_mla"][i] = dwb
        g["conf_dw_w"][i] = dcw[:CONF_K]
        g["conf_dw_b"][i] = dcvec[2]
        g["conf_ln_g"][i] = dcvec[0]
        g["conf_ln_b"][i] = dcvec[1]
        g["w_out_conf"][i] = dwc
        g["sc_dw_w"][i] = dsw[:SC_K]
        g["w_out_sc"][i] = dwd
        g["w_o"][i] = dwo
        g["post_norm_g"][i] = dgpost[0]

    grads = {n: jnp.stack(parts) for n, parts in g.items()}
    grads["meta_tokens"] = dh[:N_META]
    return total[0, 0], dh[N_META:length], grads


def kernel(x, meta_tokens, pre_norm_g, w_in, gate_bias, pool_w, pool_scale, w_out_pool, q_norm_g, w_uq, kv_norm_g, w_ukv, w_out_mla, conf_dw_w, conf_dw_b, conf_ln_g, conf_ln_b, w_out_conf, sc_dw_w, w_out_sc, w_o, post_norm_g, loss_target, m_meta_tokens, m_pre_norm_g, m_w_in, m_gate_bias, m_pool_w, m_pool_scale, m_w_out_pool, m_q_norm_g, m_w_uq, m_kv_norm_g, m_w_ukv, m_w_out_mla, m_conf_dw_w, m_conf_dw_b, m_conf_ln_g, m_conf_ln_b, m_w_out_conf, m_sc_dw_w, m_w_out_sc, m_w_o, m_post_norm_g, v_meta_tokens, v_pre_norm_g, v_w_in, v_gate_bias, v_pool_w, v_pool_scale, v_w_out_pool, v_q_norm_g, v_w_uq, v_kv_norm_g, v_w_ukv, v_w_out_mla, v_conf_dw_w, v_conf_dw_b, v_conf_ln_g, v_conf_ln_b, v_w_out_conf, v_sc_dw_w, v_w_out_sc, v_w_o, v_post_norm_g):
    args = locals()
    weights = {n: args[n] for n in WEIGHT_ORDER}
    c = lax.axis_index("c")
    chip = 2 * lax.axis_index("x") + lax.axis_index("y")

    full = dict(weights)
    full.update(gather_weights(weights))
    total, dx, grads = local_step(x[0], loss_target[0], full)
    loss = lax.psum(total * (0.5 / D_MODEL), ("x", "y", "c"))

    reduced = reduce_large(grads, c)
    reduced.update(reduce_small(grads, chip))

    deltas, new_m, new_v = [], [], []
    for n in WEIGHT_ORDER:
        d, nm, nv = adamw(weights[n], reduced[n], args["m_" + n], args["v_" + n])
        deltas.append(d)
        new_m.append(nm)
        new_v.append(nv)
    return (loss, dx[None], *[reduced[n] for n in WEIGHT_ORDER], *deltas, *new_m, *new_v)
```

```python
import functools
import math

import jax
import jax.numpy as jnp
from jax import lax
from jax.experimental import pallas as pl
from jax.experimental.pallas import tpu as pltpu

F32 = jnp.float32
BF16 = jnp.bfloat16

D_MODEL = 1024
DEPTH = 4
N_META = 16
EPS = 1e-6
HEADS = 8
QK_NOPE = 64
QK_ROPE = 32
V_DIM = 64
HEAD_PAD = 128
ROPE_THETA = 10000.0
Q_SCALE = (QK_NOPE + QK_ROPE) ** -0.5
CONF_K = 31
SC_K = 3
IN_W = 7328
N_CHIPS = 4

ZB = 3328
ZG = 4096
BG, C2, XV, SG, PV, PG, CQ, CKV, KR, MG, CA, CGT, CG = (0, 256, 512, 768, 1024, 1280, 1536, 1792, 1920, 2048, 2560, 2816, 3072)

ROW_TILE = 384
HALO = 32
LANES = 128
VMEM_LIMIT = 56 * 1024 * 1024

ADAM_LR = 0.001
ADAM_B1 = 0.9
ADAM_B2 = 0.999
ADAM_EPS = 1e-08
ADAM_WD = 0.01
ADAM_STEP = 10

MESH = pl.DeviceIdType.MESH
ANY = pl.BlockSpec(memory_space=pl.ANY)

MISC = (
    ("w_out_pool", 256), ("w_ukv", 128), ("w_out_mla", 512), ("w_out_conf", 256), ("w_out_sc", 256), ("w_o", 1024), ("w_uq", 256))
SHARDED_SMALL = (
    ("meta_tokens", (N_META, 256), 1),
    ("conf_dw_w", (DEPTH, CONF_K, 64), 2),
    ("sc_dw_w", (DEPTH, SC_K, 64), 2),
)
REPLICATED = (
    ("pre_norm_g", (DEPTH, D_MODEL)),
    ("gate_bias", (DEPTH, 4 * D_MODEL)),
    ("pool_w", (DEPTH, 4, 64, 64)),
    ("pool_scale", (DEPTH, 256)),
    ("q_norm_g", (DEPTH, 256)),
    ("kv_norm_g", (DEPTH, 128)),
    ("conf_dw_b", (DEPTH, 256)),
    ("conf_ln_g", (DEPTH, 256)),
    ("conf_ln_b", (DEPTH, 256)),
    ("post_norm_g", (DEPTH, D_MODEL)),
)
WEIGHT_ORDER = ("meta_tokens", "pre_norm_g", "w_in", "gate_bias", "pool_w", "pool_scale", "w_out_pool", "q_norm_g", "w_uq",
                "kv_norm_g", "w_ukv", "w_out_mla", "conf_dw_w", "conf_dw_b", "conf_ln_g", "conf_ln_b", "w_out_conf", "sc_dw_w",
                "w_out_sc", "w_o", "post_norm_g")


def _dot(a, b):
    return lax.dot_general(a, b, (((1,), (0,)), ((), ())), preferred_element_type=F32)


def _dot_nt(a, b):
    return lax.dot_general(a, b, (((1,), (1,)), ((), ())), preferred_element_type=F32)


def _dot_tn(a, b):
    return lax.dot_general(a, b, (((0,), (0,)), ((), ())), preferred_element_type=F32)


def _sigmoid(x):
    return jax.nn.sigmoid(x)


def _silu(x):
    return x * _sigmoid(x)


def _silu_grad(x):
    s = _sigmoid(x)
    return s * (1.0 + x * (1.0 - s))


def _rms(x, g):
    return x * lax.rsqrt(jnp.mean(x * x, axis=-1, keepdims=True) + EPS) * g


def _sh(x, d):
    return x if d == 0 else pltpu.roll(x, d, 0)


def _ash(x, d):
    return x if d == 0 else pltpu.roll(x, x.shape[0] - d, 0)


def _lanes8(t):
    return jnp.concatenate([t] * HEADS, axis=1)


def _pool_window_sums(v, shift):
    a2 = v + shift(v, 1)
    a4 = a2 + shift(a2, 2)
    a8 = a4 + shift(a4, 4)
    a16 = a8 + shift(a8, 8)
    lane = lax.broadcasted_iota(jnp.int32, v.shape, 1)
    return jnp.where(lane < 64, a2, jnp.where(lane < 128, a4, jnp.where(lane < 192, a8, a16)))


def _pool_counts(first_row, rows):
    pos = first_row + lax.broadcasted_iota(jnp.int32, (rows, 256), 0)
    lane = lax.broadcasted_iota(jnp.int32, (rows, 256), 1)
    width = jnp.where(lane < 64, 2, jnp.where(lane < 128, 4, jnp.where(lane < 192, 8, 16)))
    return jnp.maximum(jnp.minimum(pos + 1, width), 1).astype(F32)


def _params(sem=None):
    return pltpu.CompilerParams(dimension_semantics=sem, vmem_limit_bytes=VMEM_LIMIT)


def _tile_specs(t, n_halo_blocks):
    per = t // HALO

    def cur(c, cb=0):
        return pl.BlockSpec((t, c), lambda i: (i, cb))

    def prev(c, cb=0):
        return pl.BlockSpec((HALO, c), lambda i: (jnp.maximum(i * per - 1, 0), cb))

    def nxt(c, cb=0):
        return pl.BlockSpec((HALO, c), lambda i: (jnp.minimum((i + 1) * per, n_halo_blocks - 1), cb))

    def full(shape):
        return pl.BlockSpec(shape, lambda i: (0,) * len(shape))

    return cur, prev, nxt, full


def _big_tile(rows):
    return rows // 3 if rows % (3 * LANES) == 0 else ROW_TILE


def matmul(a, b, mode, out_dtype, tm, tn, tk, name):
    if mode == "nn":
        (m, k), n = a.shape, b.shape[1]
        a_spec = pl.BlockSpec((tm, tk), lambda i, j, kk: (i, kk))
        b_spec = pl.BlockSpec((tk, tn), lambda i, j, kk: (kk, j))
        dot = _dot
    elif mode == "nt":
        (m, k), n = a.shape, b.shape[0]
        a_spec = pl.BlockSpec((tm, tk), lambda i, j, kk: (i, kk))
        b_spec = pl.BlockSpec((tn, tk), lambda i, j, kk: (j, kk))
        dot = _dot_nt
    else:
        (k, m), n = a.shape, b.shape[1]
        a_spec = pl.BlockSpec((tk, tm), lambda i, j, kk: (kk, i))
        b_spec = pl.BlockSpec((tk, tn), lambda i, j, kk: (kk, j))
        dot = _dot_tn
    assert m % tm == 0 and n % tn == 0 and k % tk == 0, (a.shape, b.shape, tm, tn, tk)
    nk = k // tk

    def body(a_ref, b_ref, o_ref, acc_ref):
        kk = pl.program_id(2)

        @pl.when(kk == 0)
        def _():
            acc_ref[...] = jnp.zeros_like(acc_ref)

        acc_ref[...] += dot(a_ref[...], b_ref[...])

        @pl.when(kk == nk - 1)
        def _():
            o_ref[...] = acc_ref[...].astype(out_dtype)

    return pl.pallas_call(
        body, name=name, grid=(m // tm, n // tn, nk), in_specs=[a_spec, b_spec],
        out_specs=pl.BlockSpec((tm, tn), lambda i, j, kk: (i, j)), out_shape=jax.ShapeDtypeStruct((m, n), out_dtype),
        scratch_shapes=[pltpu.VMEM((tm, tn), F32)], compiler_params=_params(("parallel", "parallel", "arbitrary")),
    )(a, b)


def prenorm_project(hres, g, w):
    rows, d = hres.shape
    n = w.shape[1]
    tm, tn = _big_tile(rows), n // 2

    def body(x_ref, g_ref, w_ref, z_ref, hb_ref):
        @pl.when(pl.program_id(1) == 0)
        def _():
            hb_ref[...] = _rms(x_ref[...], g_ref[...]).astype(BF16)

        z_ref[...] = _dot(hb_ref[...], w_ref[...]).astype(BF16)

    return pl.pallas_call(
        body, name="prenorm_project", grid=(rows // tm, n // tn),
        in_specs=[pl.BlockSpec((tm, d), lambda i, j: (i, 0)), pl.BlockSpec((1, d), lambda i, j: (0, 0)),
                  pl.BlockSpec((d, tn), lambda i, j: (0, j))],
        out_specs=[pl.BlockSpec((tm, tn), lambda i, j: (i, j)), pl.BlockSpec((tm, d), lambda i, j: (i, 0))],
        out_shape=[jax.ShapeDtypeStruct((rows, n), BF16), jax.ShapeDtypeStruct((rows, d), BF16)],
        compiler_params=_params(("parallel", "arbitrary")),
    )(hres, g, w)


def _rope(q, c, s1, s2, width):
    return q * c + pltpu.roll(q, width - 16, 1) * s1 + pltpu.roll(q, 16, 1) * s2


def _rope_transposed(dq, c, s1, s2, width):
    return dq * c + pltpu.roll(dq * s1, 16, 1) + pltpu.roll(dq * s2, width - 16, 1)


def _conf_conv(g1, w_ref):
    acc = jnp.zeros_like(g1)
    for k in range(CONF_K):
        acc = acc + w_ref[k:k + 1, :] * _sh(g1, CONF_K - 1 - k)
    return acc


def _conf_tail(c, cg, lg, lb):
    mu = jnp.mean(c, axis=-1, keepdims=True)
    xc = c - mu
    var = jnp.mean(xc * xc, axis=-1, keepdims=True)
    n = xc * lax.rsqrt(var + EPS) * lg + lb
    return _silu(n) * _silu(cg)


def branches_fwd(z_br, rope, pwbd, pscale, gq, wuq, gkv, wukv, conf_w, conf_vec, sc_w):
    rows = z_br.shape[0]
    t = ROW_TILE
    cur, prev, _, full = _tile_specs(t, rows // HALO)

    def body(zc_ref, zp_ref, rope_ref, pw_ref, ps_ref, gq_ref, wuq_ref, gkv_ref, wukv_ref, cw_ref, cv_ref, sw_ref,
             ua_ref, uc_ref, ud_ref, q_ref, k_ref, v_ref):
        i = pl.program_id(0)
        zp = jnp.where(i == 0, jnp.zeros(zp_ref.shape, zp_ref.dtype), zp_ref[...])

        def ext(lo, w=256):
            return jnp.concatenate([zp[:, lo:lo + w], zc_ref[:, lo:lo + w]], axis=0).astype(F32)

        def col(lo, w=256):
            return zc_ref[:, lo:lo + w].astype(F32)

        v = ext(PV)
        p = (_pool_window_sums(v, _sh) / _pool_counts(i * t - HALO, t + HALO) - v)[HALO:]
        ya = _dot(p.astype(BF16), pw_ref[...]) * ps_ref[...]
        ua_ref[...] = (ya * _silu(col(PG))).astype(BF16)

        g1 = ext(CA) * _sigmoid(ext(CGT))
        c = _conf_conv(g1, cw_ref)[HALO:] + cv_ref[0:1, :]
        uc_ref[...] = _conf_tail(c, col(CG), cv_ref[1:2, :], cv_ref[2:3, :]).astype(BF16)

        e = ext(C2) * ext(XV)
        f = jnp.zeros_like(e)
        for k in range(SC_K):
            f = f + sw_ref[k:k + 1, :] * _sh(e, SC_K - 1 - k)
        ud_ref[...] = (col(BG) * f[HALO:] * _silu(col(SG))).astype(BF16)

        cth, s1, s2 = rope_ref[:, 0:128], rope_ref[:, 128:256], rope_ref[:, 256:384]
        qn = _rms(col(CQ), gq_ref[...]).astype(BF16)
        q = _dot(qn, wuq_ref[...])
        w8 = HEADS * HEAD_PAD
        q_ref[...] = (_rope(q, _lanes8(cth), _lanes8(s1), _lanes8(s2), w8) * Q_SCALE).astype(BF16)
        kvn = _rms(col(CKV, 128), gkv_ref[...]).astype(BF16)
        kv = _dot(kvn, wukv_ref[...])
        kr = _rope(col(KR, 128), cth, s1, s2, HEAD_PAD)
        k_ref[...] = (kv[:, :w8] + _lanes8(kr)).astype(BF16)
        v_ref[...] = kv[:, w8:].astype(BF16)

    outs = [jax.ShapeDtypeStruct((rows, 256), BF16)] * 3 + [jax.ShapeDtypeStruct((rows, 1024), BF16)] * 2 + [
        jax.ShapeDtypeStruct((rows, 512), BF16)]
    return pl.pallas_call(
        body, name="branches_fwd", grid=(rows // t,),
        in_specs=[cur(ZB), prev(ZB), cur(384), full((256, 256)), full((1, 256)), full((1, 256)), full((256, 1024)),
                  full((1, 128)), full((128, 1536)), full((32, 256)), full((8, 256)), full((8, 256))],
        out_specs=[cur(256), cur(256), cur(256), cur(1024), cur(1024), cur(512)], out_shape=outs,
        compiler_params=_params(("parallel",)),
    )(z_br, z_br, rope, pwbd, pscale, gq, wuq, gkv, wukv, conf_w, conf_vec, sc_w)


def _head_lane_mask(h):
    lane = lax.broadcasted_iota(jnp.int32, (1, 2 * V_DIM), 1)
    return (lane >= V_DIM * h) & (lane < V_DIM * (h + 1))


def attention_fwd(q, k, v):
    rows = q.shape[0]
    tq = ROW_TILE
    nq = rows // tq

    def body(q_ref, k_ref, v_ref, o_ref, lse_ref):
        i = pl.program_id(1)

        def head_step(h, tile, n_tiles, carry, masked):
            m, l, acc = carry
            width = n_tiles * tq
            r0 = pl.multiple_of(tile * tq, tq)
            kh = k_ref[pl.ds(r0, width), HEAD_PAD * h:HEAD_PAD * (h + 1)]
            vh = jnp.where(_head_lane_mask(h), v_ref[pl.ds(r0, width), :], jnp.zeros((), BF16))
            s = _dot_nt(q_ref[:, HEAD_PAD * h:HEAD_PAD * (h + 1)], kh)
            if masked:
                row = lax.broadcasted_iota(jnp.int32, (tq, width), 0)
                colm = lax.broadcasted_iota(jnp.int32, (tq, width), 1)
                s = jnp.where(colm <= row + (width - tq), s, -1e30)
            m2 = jnp.maximum(m, jnp.max(s, axis=-1, keepdims=True))
            alpha = jnp.exp(m - m2)
            pr = jnp.exp(s - m2)
            return m2, alpha * l + jnp.sum(pr, axis=-1, keepdims=True), alpha * acc + _dot(pr.astype(BF16), vh)

        def step(tile, n_tiles, carry, masked):
            return tuple(head_step(h, tile, n_tiles, carry[h], masked) for h in range(2))

        init = (jnp.full((tq, 1), -1e30, F32), jnp.zeros((tq, 1), F32), jnp.zeros((tq, 2 * V_DIM), F32))
        carry = lax.fori_loop(0, i // 2, lambda t, cr: step(2 * t, 2, cr, False), (init, init))
        carry = lax.cond(i % 2 == 1, lambda cr: step(i - 1, 2, cr, True), lambda cr: step(i, 1, cr, True), carry)
        out = jnp.zeros((tq, 2 * V_DIM), F32)
        for h, (m, l, acc) in enumerate(carry):
            out = out + acc / l
            lse_ref[h] = jnp.broadcast_to(m + jnp.log(l), (tq, LANES))
        o_ref[...] = out.astype(BF16)

    return pl.pallas_call(
        body, name="attention_fwd", grid=(HEADS // 2, nq),
        in_specs=[pl.BlockSpec((tq, 2 * HEAD_PAD), lambda p, i: (i, p)), pl.BlockSpec((rows, 2 * HEAD_PAD), lambda p, i: (0, p)),
                  pl.BlockSpec((rows, 2 * V_DIM), lambda p, i: (0, p))],
        out_specs=[pl.BlockSpec((tq, 2 * V_DIM), lambda p, i: (i, p)), pl.BlockSpec((2, tq, LANES), lambda p, i: (p, i, 0))],
        out_shape=[jax.ShapeDtypeStruct((rows, HEADS * V_DIM), BF16), jax.ShapeDtypeStruct((HEADS, rows, LANES), F32)],
        compiler_params=_params(("parallel", "parallel")),
    )(q, k, v)


def merge_fwd(ua, o_att, uc, ud, z_br, z_gl, bias, woa, wob, woc, wod, wo, gpost, hres):
    rows = hres.shape[0]
    t = ROW_TILE
    cur, _, _, full = _tile_specs(t, rows // HALO)
    d = D_MODEL

    def body(ua_ref, ob_ref, uc_ref, ud_ref, mg_ref, gl_ref, b_ref, woa_ref, wob_ref, woc_ref, wod_ref, wo_ref, gp_ref, h_ref,
             ub_ref, mb_ref, o_ref, hn_ref):
        ub = (ob_ref[...].astype(F32) * _silu(mg_ref[...].astype(F32))).astype(BF16)
        ub_ref[...] = ub
        m = jnp.zeros((t, d), F32)
        for idx, (u, w_ref) in enumerate(((ua_ref[...], woa_ref), (ub, wob_ref), (uc_ref[...], woc_ref), (ud_ref[...], wod_ref))):
            gate = _sigmoid(gl_ref[:, d * idx:d * (idx + 1)].astype(F32) + b_ref[:, d * idx:d * (idx + 1)])
            m = m + gate * _dot(u, w_ref[...])
        mb = m.astype(BF16)
        mb_ref[...] = mb
        o = _dot(mb, wo_ref[...])
        o_ref[...] = o
        hn_ref[...] = h_ref[...] + _rms(o, gp_ref[...])

    return pl.pallas_call(
        body, name="merge_fwd", grid=(rows // t,),
        in_specs=[cur(256), cur(512), cur(256), cur(256), cur(512, MG // 512), cur(ZG), full((1, ZG)), full((256, d)), full((512, d)),
                  full((256, d)), full((256, d)), full((d, d)), full((1, d)), cur(d)],
        out_specs=[cur(512), cur(d), cur(d), cur(d)],
        out_shape=[jax.ShapeDtypeStruct((rows, 512), BF16), jax.ShapeDtypeStruct((rows, d), BF16), jax.ShapeDtypeStruct((rows, d), F32),
                   jax.ShapeDtypeStruct((rows, d), F32)],
        compiler_params=_params(("parallel",)),
    )(ua, o_att, uc, ud, z_br, z_gl, bias, woa, wob, woc, wod, wo, gpost, hres)


def loss_head(hres, target, n_tokens):
    rows, d = hres.shape
    t = ROW_TILE
    cur, _, _, full = _tile_specs(t, rows // HALO)
    n_steps = rows // t

    def body(h_ref, t_ref, dh_ref, tot_ref, acc_ref):
        i = pl.program_id(0)

        @pl.when(i == 0)
        def _():
            acc_ref[...] = jnp.zeros_like(acc_ref)

        r = i * t + lax.broadcasted_iota(jnp.int32, (t, 1), 0)
        diff = jnp.where((r >= N_META) & (r < N_META + n_tokens), h_ref[...] - t_ref[...], 0.0)
        dh_ref[...] = diff * (1.0 / d)
        acc_ref[...] += jnp.sum(diff * diff, axis=0, keepdims=True)

        @pl.when(i == n_steps - 1)
        def _():
            tot_ref[...] = jnp.broadcast_to(jnp.sum(acc_ref[...], axis=1, keepdims=True), (1, LANES))

    return pl.pallas_call(
        body, name="loss_head", grid=(n_steps,), in_specs=[cur(d), cur(d)], out_specs=[cur(d), full((1, LANES))],
        out_shape=[jax.ShapeDtypeStruct((rows, d), F32), jax.ShapeDtypeStruct((1, LANES), F32)],
        scratch_shapes=[pltpu.VMEM((1, d), F32)], compiler_params=_params(("arbitrary",)),
    )(hres, target)


def _accumulate(i, ref, value):
    @pl.when(i == 0)
    def _():
        ref[...] = value

    @pl.when(i > 0)
    def _():
        ref[...] += value


def postnorm_bwd(dh, o, mb, wo, gpost):
    rows, d = dh.shape
    t = ROW_TILE
    cur, _, _, full = _tile_specs(t, rows // HALO)

    def body(dh_ref, o_ref, mb_ref, wo_ref, gp_ref, dm_ref, dwo_ref, dgp_ref):
        i = pl.program_id(0)
        _, vjp = jax.vjp(_rms, o_ref[...], gp_ref[...])
        do, dg = vjp(dh_ref[...])
        dob = do.astype(BF16)
        dm_ref[...] = _dot_nt(dob, wo_ref[...])
        _accumulate(i, dwo_ref, _dot_tn(mb_ref[...], dob))
        _accumulate(i, dgp_ref, dg)

    return pl.pallas_call(
        body, name="postnorm_bwd", grid=(rows // t,), in_specs=[cur(d), cur(d), cur(d), full((d, d)), full((1, d))],
        out_specs=[cur(d), full((d, d)), full((1, d))],
        out_shape=[jax.ShapeDtypeStruct((rows, d), F32), jax.ShapeDtypeStruct((d, d), F32), jax.ShapeDtypeStruct((1, d), F32)],
        compiler_params=_params(("arbitrary",)),
    )(dh, o, mb, wo, gpost)


def merge_bwd(dm, ua, ub, uc, ud, z_gl, bias, woa, wob, woc, wod):
    rows, d = dm.shape
    t = ROW_TILE
    cur, _, _, full = _tile_specs(t, rows // HALO)
    widths = (256, 512, 256, 256)

    def body(dm_ref, ua_ref, ub_ref, uc_ref, ud_ref, gl_ref, b_ref, woa_ref, wob_ref, woc_ref, wod_ref,
             dua_ref, dub_ref, duc_ref, dud_ref, dgl_ref, dwa_ref, dwb_ref, dwc_ref, dwd_ref, db_ref):
        i = pl.program_id(0)
        dm = dm_ref[...]
        groups = ((ua_ref, woa_ref, dua_ref, dwa_ref), (ub_ref, wob_ref, dub_ref, dwb_ref), (uc_ref, woc_ref, duc_ref, dwc_ref),
                  (ud_ref, wod_ref, dud_ref, dwd_ref))
        for idx, (u_ref, w_ref, du_ref, dw_ref) in enumerate(groups):
            cols = slice(d * idx, d * (idx + 1))
            u = u_ref[...]
            gate = _sigmoid(gl_ref[:, cols].astype(F32) + b_ref[:, cols])
            dgl = dm * _dot(u, w_ref[...]) * gate * (1.0 - gate)
            dgl_ref[:, cols] = dgl.astype(BF16)
            _accumulate(i, db_ref.at[:, cols], jnp.sum(dgl, axis=0, keepdims=True))
            dyb = (dm * gate).astype(BF16)
            du_ref[...] = _dot_nt(dyb, w_ref[...])
            _accumulate(i, dw_ref, _dot_tn(u, dyb))

    return pl.pallas_call(
        body, name="merge_bwd", grid=(rows // t,),
        in_specs=[cur(d), cur(256), cur(512), cur(256), cur(256), cur(ZG), full((1, ZG))] + [full((w, d)) for w in widths],
        out_specs=[cur(256), cur(512), cur(256), cur(256), cur(ZG)] + [full((w, d)) for w in widths] + [full((1, ZG))],
        out_shape=[jax.ShapeDtypeStruct((rows, w), F32) for w in widths] + [jax.ShapeDtypeStruct((rows, ZG), BF16)] + [
            jax.ShapeDtypeStruct((w, d), F32) for w in widths] + [jax.ShapeDtypeStruct((1, ZG), F32)],
        compiler_params=_params(("arbitrary",)),
    )(dm, ua, ub, uc, ud, z_gl, bias, woa, wob, woc, wod)


def pool_bwd(z_br, dua, pwbd, pscale, dz_buf):
    rows = z_br.shape[0]
    t = ROW_TILE
    n_steps = rows // t
    cur, prev, nxt, full = _tile_specs(t, rows // HALO)

    def body(zc_ref, zp_ref, zn_ref, dc_ref, dn_ref, pw_ref, ps_ref, _, dz_ref, dpw_ref, dps_ref):
        i = pl.program_id(0)
        zp = jnp.where(i == 0, jnp.zeros(zp_ref.shape, zp_ref.dtype), zp_ref[...])
        zn = jnp.where(i == n_steps - 1, jnp.zeros(zn_ref.shape, zn_ref.dtype), zn_ref[...])
        dun = jnp.where(i == n_steps - 1, jnp.zeros(dn_ref.shape, dn_ref.dtype), dn_ref[...])

        def ext(lo):
            return jnp.concatenate([zp[:, lo:lo + 256], zc_ref[:, lo:lo + 256], zn[:, lo:lo + 256]], axis=0).astype(F32)

        n_ext = t + 2 * HALO
        v, pg = ext(PV), ext(PG)
        cnt = _pool_counts(i * t - HALO, n_ext)
        p = (_pool_window_sums(v, _sh) / cnt - v)[HALO:HALO + t]
        du = jnp.concatenate([jnp.zeros((HALO, 256), F32), dc_ref[...], dun], axis=0)
        dya = du * _silu(pg)
        dypb = (dya * ps_ref[...]).astype(BF16)
        dp = _dot_nt(dypb, pw_ref[...])
        dv = (_pool_window_sums(dp / cnt, _ash) - dp)[HALO:HALO + t]
        pb = p.astype(BF16)
        pw = _dot(pb, pw_ref[...])
        duc, pgc = dc_ref[...], pg[HALO:HALO + t]
        dpg = duc * pw * ps_ref[...] * _silu_grad(pgc)
        dz_ref[...] = jnp.concatenate([dv, dpg], axis=1).astype(BF16)
        _accumulate(i, dpw_ref, _dot_tn(pb, dypb[HALO:HALO + t]))
        _accumulate(i, dps_ref, jnp.sum(dya[HALO:HALO + t] * pw, axis=0, keepdims=True))

    return pl.pallas_call(
        body, name="pool_bwd", grid=(n_steps,),
        in_specs=[cur(ZB), prev(ZB), nxt(ZB), cur(256), nxt(256), full((256, 256)), full((1, 256)), ANY],
        out_specs=[cur(512, PV // 512), full((256, 256)), full((1, 256))],
        out_shape=[jax.ShapeDtypeStruct((rows, ZB), BF16), jax.ShapeDtypeStruct((256, 256), F32), jax.ShapeDtypeStruct((1, 256), F32)],
        input_output_aliases={7: 0}, compiler_params=_params(("arbitrary",)),
    )(z_br, z_br, z_br, dua, dua, pwbd, pscale, dz_buf)


def shortconv_bwd(z_br, dud, sc_w, dz_buf):
    rows = z_br.shape[0]
    t = ROW_TILE
    n_steps = rows // t
    cur, prev, nxt, full = _tile_specs(t, rows // HALO)

    def body(zc_ref, zp_ref, zn_ref, dc_ref, dn_ref, sw_ref, _, dz_ref, dw_ref):
        i = pl.program_id(0)
        zp = jnp.where(i == 0, jnp.zeros(zp_ref.shape, zp_ref.dtype), zp_ref[...])
        zn = jnp.where(i == n_steps - 1, jnp.zeros(zn_ref.shape, zn_ref.dtype), zn_ref[...])
        dun = jnp.where(i == n_steps - 1, jnp.zeros(dn_ref.shape, dn_ref.dtype), dn_ref[...])

        def ext(lo):
            return jnp.concatenate([zp[:, lo:lo + 256], zc_ref[:, lo:lo + 256], zn[:, lo:lo + 256]], axis=0).astype(F32)

        mid = slice(HALO, HALO + t)
        bg, c2, xv, sg = ext(BG), ext(C2), ext(XV), ext(SG)
        du = jnp.concatenate([jnp.zeros((HALO, 256), F32), dc_ref[...], dun], axis=0)
        e = c2 * xv
        shifted = [_sh(e, SC_K - 1 - k) for k in range(SC_K)]
        f = sum(sw_ref[k:k + 1, :] * shifted[k] for k in range(SC_K))
        gate = _silu(sg)
        df = du * gate * bg
        de = sum(sw_ref[k:k + 1, :] * _ash(df, SC_K - 1 - k) for k in range(SC_K))
        dbg = du * gate * f
        dsg = du * bg * f * _silu_grad(sg)
        dz_ref[...] = jnp.concatenate([dbg[mid], (de * xv)[mid], (de * c2)[mid], dsg[mid]], axis=1).astype(BF16)
        dw = jnp.concatenate([jnp.sum((df * shifted[k])[mid], axis=0, keepdims=True) for k in range(SC_K)] + [
            jnp.zeros((8 - SC_K, 256), F32)], axis=0)
        _accumulate(i, dw_ref, dw)

    return pl.pallas_call(
        body, name="shortconv_bwd", grid=(n_steps,), in_specs=[cur(ZB), prev(ZB), nxt(ZB), cur(256), nxt(256), full((8, 256)), ANY],
        out_specs=[cur(1024, BG // 1024), full((8, 256))],
        out_shape=[jax.ShapeDtypeStruct((rows, ZB), BF16), jax.ShapeDtypeStruct((8, 256), F32)],
        input_output_aliases={6: 0}, compiler_params=_params(("arbitrary",)),
    )(z_br, z_br, z_br, dud, dud, sc_w, dz_buf)


def conformer_bwd_tail(z_br, duc, conf_w, conf_vec, dz_buf):
    rows = z_br.shape[0]
    t = ROW_TILE
    cur, prev, _, full = _tile_specs(t, rows // HALO)

    def body(zc_ref, zp_ref, du_ref, cw_ref, cv_ref, _, dc_ref, dcg_ref, dv_ref):
        i = pl.program_id(0)
        zp = jnp.where(i == 0, jnp.zeros(zp_ref.shape, zp_ref.dtype), zp_ref[...])

        def ext(lo):
            return jnp.concatenate([zp[:, lo:lo + 256], zc_ref[:, lo:lo + 256]], axis=0).astype(F32)

        g1 = ext(CA) * _sigmoid(ext(CGT))
        c = _conf_conv(g1, cw_ref)[HALO:] + cv_ref[0:1, :]
        _, vjp = jax.vjp(_conf_tail, c, zc_ref[:, CG:CG + 256].astype(F32), cv_ref[1:2, :], cv_ref[2:3, :])
        dc, dcg, dlg, dlb = vjp(du_ref[...])
        dc_ref[...] = dc
        dcg_ref[...] = dcg.astype(BF16)
        dvec = jnp.concatenate([dlg, dlb, jnp.sum(dc, axis=0, keepdims=True), jnp.zeros((5, 256), F32)], axis=0)
        _accumulate(i, dv_ref, dvec)

    return pl.pallas_call(
        body, name="conformer_bwd_tail", grid=(rows // t,), in_specs=[cur(ZB), prev(ZB), cur(256), full((32, 256)), full((8, 256)), ANY],
        out_specs=[cur(256), cur(256, CG // 256), full((8, 256))],
        out_shape=[jax.ShapeDtypeStruct((rows, 256), F32), jax.ShapeDtypeStruct((rows, ZB), BF16), jax.ShapeDtypeStruct((8, 256), F32)],
        input_output_aliases={5: 1}, compiler_params=_params(("arbitrary",)),
    )(z_br, z_br, duc, conf_w, conf_vec, dz_buf)


def conformer_bwd_conv(z_br, dc, conf_w, dz_buf):
    rows = z_br.shape[0]
    t = ROW_TILE
    n_steps = rows // t
    cur, prev, nxt, full = _tile_specs(t, rows // HALO)

    def body(zc_ref, zp_ref, dc_ref, dn_ref, cw_ref, _, dz_ref, dw_ref):
        i = pl.program_id(0)
        zp = jnp.where(i == 0, jnp.zeros(zp_ref.shape, zp_ref.dtype), zp_ref[...])
        dcn = jnp.where(i == n_steps - 1, jnp.zeros(dn_ref.shape, dn_ref.dtype), dn_ref[...])

        def ext(lo):
            return jnp.concatenate([zp[:, lo:lo + 256], zc_ref[:, lo:lo + 256]], axis=0).astype(F32)

        a, gt = ext(CA), ext(CGT)
        sg = _sigmoid(gt)
        g1 = a * sg
        dc = dc_ref[...]
        dce = jnp.concatenate([dc, dcn], axis=0)
        dg1 = jnp.zeros_like(dce)
        dws = []
        for k in range(CONF_K):
            dg1 = dg1 + cw_ref[k:k + 1, :] * _ash(dce, CONF_K - 1 - k)
            dws.append(jnp.sum(dc * _sh(g1, CONF_K - 1 - k)[HALO:], axis=0, keepdims=True))
        dg1 = dg1[:t]
        ac, sc = a[HALO:], sg[HALO:]
        dz_ref[...] = jnp.concatenate([dg1 * sc, dg1 * ac * sc * (1.0 - sc)], axis=1).astype(BF16)
        _accumulate(i, dw_ref, jnp.concatenate(dws + [jnp.zeros((32 - CONF_K, 256), F32)], axis=0))

    return pl.pallas_call(
        body, name="conformer_bwd_conv", grid=(n_steps,), in_specs=[cur(ZB), prev(ZB), cur(256), nxt(256), full((32, 256)), ANY],
        out_specs=[cur(512, CA // 512), full((32, 256))],
        out_shape=[jax.ShapeDtypeStruct((rows, ZB), BF16), jax.ShapeDtypeStruct((32, 256), F32)],
        input_output_aliases={5: 0}, compiler_params=_params(("arbitrary",)),
    )(z_br, z_br, dc, dc, conf_w, dz_buf)


def attention_bwd_prep(dub, o_att, z_br, dz_buf):
    rows = dub.shape[0]
    t = ROW_TILE
    cur, _, _, _ = _tile_specs(t, rows // HALO)

    def body(du_ref, o_ref, mg_ref, _, do_ref, dmg_ref, delta_ref):
        du, o, mg = du_ref[...], o_ref[...].astype(F32), mg_ref[...].astype(F32)
        do = du * _silu(mg)
        do_ref[...] = do.astype(BF16)
        dmg_ref[...] = (du * o * _silu_grad(mg)).astype(BF16)
        prod = do * o
        lane = lax.broadcasted_iota(jnp.int32, (1, HEADS * V_DIM), 1)
        for h in range(HEADS):
            part = jnp.where((lane >= V_DIM * h) & (lane < V_DIM * (h + 1)), prod, 0.0)
            delta_ref[h] = jnp.broadcast_to(jnp.sum(part, axis=-1, keepdims=True), (t, LANES))

    return pl.pallas_call(
        body, name="attention_bwd_prep", grid=(rows // t,), in_specs=[cur(512), cur(512), cur(512, MG // 512), ANY],
        out_specs=[cur(512), cur(512, MG // 512), pl.BlockSpec((HEADS, t, LANES), lambda i: (0, i, 0))],
        out_shape=[jax.ShapeDtypeStruct((rows, 512), BF16), jax.ShapeDtypeStruct((rows, ZB), BF16),
                   jax.ShapeDtypeStruct((HEADS, rows, LANES), F32)],
        input_output_aliases={3: 1}, compiler_params=_params(("parallel",)),
    )(dub, o_att, z_br, dz_buf)


def attention_bwd(q, k, v, do, lse, delta):
    rows = q.shape[0]
    tq = ROW_TILE
    nq = rows // tq

    def body(q_ref, k_ref, v_ref, do_ref, lse_ref, dl_ref, dq_ref, dk_ref, dv_ref):
        j = pl.program_id(1)

        @pl.when(j == 0)
        def _():
            dq_ref[...] = jnp.zeros_like(dq_ref)


        def head_step(h, tile, n_tiles, dk, dv, diagonal):
            lanes = slice(HEAD_PAD * h, HEAD_PAD * (h + 1))
            hm = _head_lane_mask(h)
            kh = k_ref[:, lanes]
            vh = jnp.where(hm, v_ref[...], jnp.zeros((), BF16))
            r0, width = pl.multiple_of(tile * tq, tq), n_tiles * tq
            qi = q_ref[pl.ds(r0, width), lanes]
            doi = jnp.where(hm, do_ref[pl.ds(r0, width), :], jnp.zeros((), BF16))
            s = _dot_nt(qi, kh)
            if diagonal:
                s = jnp.where(lax.broadcasted_iota(jnp.int32, (tq, tq), 1) <= lax.broadcasted_iota(jnp.int32, (tq, tq), 0), s, -1e30)
            pr = jnp.exp(s - lse_ref[h, pl.ds(r0, width), :][:, 0:1])
            dv = dv + _dot_tn(pr.astype(BF16), doi)
            dp = _dot_nt(doi, vh)
            ds = (pr * (dp - dl_ref[h, pl.ds(r0, width), :][:, 0:1])).astype(BF16)
            dq_ref[pl.ds(r0, width), lanes] += _dot(ds, kh)
            return dk + _dot_tn(ds, qi), dv

        def step(tile, n_tiles, carry, diagonal):
            dk0, dk1, dv = carry
            dk0, dv = head_step(0, tile, n_tiles, dk0, dv, diagonal)
            dk1, dv = head_step(1, tile, n_tiles, dk1, dv, diagonal)
            return dk0, dk1, dv

        zero = jnp.zeros((tq, HEAD_PAD), F32)
        carry = step(j, 1, (zero, zero, jnp.zeros((tq, 2 * V_DIM), F32)), True)
        odd = (nq - 1 - j) % 2
        carry = lax.cond(odd == 1, lambda cr: step(j + 1, 1, cr, False), lambda cr: cr, carry)
        dk0, dk1, dv = lax.fori_loop(0, (nq - 1 - j) // 2, lambda t, cr: step(j + 1 + odd + 2 * t, 2, cr, False), carry)
        dk_ref[:, 0:HEAD_PAD] = dk0
        dk_ref[:, HEAD_PAD:2 * HEAD_PAD] = dk1
        dv_ref[...] = dv

    return pl.pallas_call(
        body, name="attention_bwd", grid=(HEADS // 2, nq),
        in_specs=[pl.BlockSpec((rows, 2 * HEAD_PAD), lambda p, j: (0, p)), pl.BlockSpec((tq, 2 * HEAD_PAD), lambda p, j: (j, p)),
                  pl.BlockSpec((tq, 2 * V_DIM), lambda p, j: (j, p)), pl.BlockSpec((rows, 2 * V_DIM), lambda p, j: (0, p)),
                  pl.BlockSpec((2, rows, LANES), lambda p, j: (p, 0, 0)), pl.BlockSpec((2, rows, LANES), lambda p, j: (p, 0, 0))],
        out_specs=[pl.BlockSpec((rows, 2 * HEAD_PAD), lambda p, j: (0, p)), pl.BlockSpec((tq, 2 * HEAD_PAD), lambda p, j: (j, p)),
                   pl.BlockSpec((tq, 2 * V_DIM), lambda p, j: (j, p))],
        out_shape=[jax.ShapeDtypeStruct((rows, HEADS * HEAD_PAD), F32), jax.ShapeDtypeStruct((rows, HEADS * HEAD_PAD), F32),
                   jax.ShapeDtypeStruct((rows, HEADS * V_DIM), F32)],
        compiler_params=_params(("parallel", "arbitrary")),
    )(q, k, v, do, lse, delta)


def mla_prep_bwd(dq, dk, dv, z_br, rope, gq, wuq, gkv, wukv, dz_buf):
    rows = dq.shape[0]
    t = ROW_TILE
    cur, _, _, full = _tile_specs(t, rows // HALO)
    w8 = HEADS * HEAD_PAD

    def body(dq_ref, dk_ref, dv_ref, z_ref, rope_ref, gq_ref, wuq_ref, gkv_ref, wukv_ref, _, dz_ref, dwuq_ref, dwukv_ref, dgq_ref, dgkv_ref):
        i = pl.program_id(0)
        cth, s1, s2 = rope_ref[:, 0:128], rope_ref[:, 128:256], rope_ref[:, 256:384]
        dqb = _rope_transposed(dq_ref[...] * Q_SCALE, _lanes8(cth), _lanes8(s1), _lanes8(s2), w8).astype(BF16)
        cq = z_ref[:, 0:256].astype(F32)
        qn, vjp_q = jax.vjp(_rms, cq, gq_ref[...])
        _accumulate(i, dwuq_ref, _dot_tn(qn.astype(BF16), dqb))
        dcq, dgq = vjp_q(_dot_nt(dqb, wuq_ref[...]))
        _accumulate(i, dgq_ref, dgq)

        dk = dk_ref[...]
        dkr = sum(dk[:, HEAD_PAD * h:HEAD_PAD * (h + 1)] for h in range(HEADS))
        dkr = _rope_transposed(dkr, cth, s1, s2, HEAD_PAD)
        lane = lax.broadcasted_iota(jnp.int32, (1, HEAD_PAD), 1)
        dkr = jnp.where((lane >= QK_NOPE) & (lane < QK_NOPE + QK_ROPE), dkr, 0.0)
        dkvb = jnp.concatenate([dk, dv_ref[...]], axis=1).astype(BF16)
        ckv = z_ref[:, 256:384].astype(F32)
        kvn, vjp_kv = jax.vjp(_rms, ckv, gkv_ref[...])
        _accumulate(i, dwukv_ref, _dot_tn(kvn.astype(BF16), dkvb))
        dckv, dgkv = vjp_kv(_dot_nt(dkvb, wukv_ref[...]))
        _accumulate(i, dgkv_ref, dgkv)
        dz_ref[...] = jnp.concatenate([dcq, dckv, dkr], axis=1).astype(BF16)

    return pl.pallas_call(
        body, name="mla_prep_bwd", grid=(rows // t,),
        in_specs=[cur(w8), cur(w8), cur(512), cur(512, CQ // 512), cur(384), full((1, 256)), full((256, w8)), full((1, 128)),
                  full((128, w8 + 512)), ANY],
        out_specs=[cur(512, CQ // 512), full((256, w8)), full((128, w8 + 512)), full((1, 256)), full((1, 128))],
        out_shape=[jax.ShapeDtypeStruct((rows, ZB), BF16), jax.ShapeDtypeStruct((256, w8), F32), jax.ShapeDtypeStruct((128, w8 + 512), F32),
                   jax.ShapeDtypeStruct((1, 256), F32), jax.ShapeDtypeStruct((1, 128), F32)],
        input_output_aliases={9: 0}, compiler_params=_params(("arbitrary",)),
    )(dq, dk, dv, z_br, rope, gq, wuq, gkv, wukv, dz_buf)


def prenorm_bwd(dz_br, w_br, dh_gl, hres, gpre, dh_next):
    rows, d = hres.shape
    t = ROW_TILE
    cur, _, _, full = _tile_specs(t, rows // HALO)

    def body(dz_ref, w_ref, dp_ref, x_ref, g_ref, dn_ref, dx_ref, dg_ref):
        i = pl.program_id(0)
        dh = _dot_nt(dz_ref[...], w_ref[...]) + dp_ref[...]
        _, vjp = jax.vjp(_rms, x_ref[...], g_ref[...])
        dx, dg = vjp(dh)
        dx_ref[...] = dx + dn_ref[...]
        _accumulate(i, dg_ref, dg)

    return pl.pallas_call(
        body, name="prenorm_bwd", grid=(rows // t,), in_specs=[cur(ZB), full((d, ZB)), cur(d), cur(d), full((1, d)), cur(d)],
        out_specs=[cur(d), full((1, d))], out_shape=[jax.ShapeDtypeStruct((rows, d), F32), jax.ShapeDtypeStruct((1, d), F32)],
        compiler_params=_params(("arbitrary",)),
    )(dz_br, w_br, dh_gl, hres, gpre, dh_next)


def _mesh_position():
    return lax.axis_index("x"), lax.axis_index("y"), lax.axis_index("c")


def chip_exchange(src, gather, name):
    block = src.shape if gather else src.shape[1:]

    def body(src_ref, dst_ref, send_sems, recv_sems, local_sem):
        x, y, c = _mesh_position()
        me = 2 * x + y
        peers = ((1 - x, y), (x, 1 - y), (1 - x, 1 - y))

        def part(k):
            return src_ref if gather else src_ref.at[k]

        def copy(j, slot):
            px, py = peers[j]
            return pltpu.make_async_remote_copy(src_ref=part(2 * px + py), dst_ref=dst_ref.at[slot], send_sem=send_sems.at[j],
                                                recv_sem=recv_sems.at[j], device_id=(px, py, c), device_id_type=MESH)

        local = pltpu.make_async_copy(part(me), dst_ref.at[me], local_sem)
        local.start()
        sends = [copy(j, me) for j in range(3)]
        for cp in sends:
            cp.start()
        for j, (px, py) in enumerate(peers):
            copy(j, 2 * px + py).wait_recv()
        for cp in sends:
            cp.wait_send()
        local.wait()

    return pl.pallas_call(
        body, name=name, in_specs=[pl.BlockSpec(memory_space=pl.ANY)], out_specs=pl.BlockSpec(memory_space=pl.ANY),
        out_shape=jax.ShapeDtypeStruct((N_CHIPS,) + tuple(block), src.dtype),
        scratch_shapes=[pltpu.SemaphoreType.DMA((3,)), pltpu.SemaphoreType.DMA((3,)), pltpu.SemaphoreType.DMA(())],
    )(src)


def sibling_swap(src, name):
    def body(src_ref, dst_ref, send_sem, recv_sem):
        x, y, c = _mesh_position()
        cp = pltpu.make_async_remote_copy(src_ref=src_ref, dst_ref=dst_ref, send_sem=send_sem, recv_sem=recv_sem,
                                          device_id=(x, y, 1 - c), device_id_type=MESH)
        cp.start()
        cp.wait()

    return pl.pallas_call(
        body, name=name, in_specs=[pl.BlockSpec(memory_space=pl.ANY)], out_specs=pl.BlockSpec(memory_space=pl.ANY),
        out_shape=jax.ShapeDtypeStruct(src.shape, src.dtype),
        scratch_shapes=[pltpu.SemaphoreType.DMA(()), pltpu.SemaphoreType.DMA(())],
    )(src)


def _comm_call(body, name, n_in, out_shapes, n_sems):
    return pl.pallas_call(
        body, name=name, in_specs=[ANY] * n_in, out_specs=[ANY] * len(out_shapes), out_shape=out_shapes,
        scratch_shapes=[pltpu.SemaphoreType.DMA((n,)) for n in n_sems])


def _layer_halves(c):
    return pl.ds((DEPTH // 2) * c, DEPTH // 2), pl.ds((DEPTH // 2) * (1 - c), DEPTH // 2)


def gather_layers(srcs, name):
    n = len(srcs)

    def body(*refs):
        src, dst = refs[:n], refs[n:2 * n]
        ici_send, ici_recv, d2d_send, d2d_recv = refs[2 * n:]
        x, y, c = _mesh_position()
        me = 2 * x + y
        peers = ((1 - x, y), (x, 1 - y), (1 - x, 1 - y))
        mine, other = _layer_halves(c)

        def fetch(a, j, slot):
            px, py = peers[j]
            return pltpu.make_async_remote_copy(src_ref=src[a].at[mine], dst_ref=dst[a].at[mine, slot], send_sem=ici_send.at[3 * a + j],
                                                recv_sem=ici_recv.at[3 * a + j], device_id=(px, py, c), device_id_type=MESH)

        def forward(a, j, half):
            px, py = peers[j]
            part = dst[a].at[half, 2 * px + py]
            return pltpu.make_async_remote_copy(src_ref=part, dst_ref=part, send_sem=d2d_send.at[3 * a + j],
                                                recv_sem=d2d_recv.at[3 * a + j], device_id=(x, y, 1 - c), device_id_type=MESH)

        sends = [fetch(a, j, me) for a in range(n) for j in range(3)]
        for cp in sends:
            cp.start()
        passed = []
        for j, (px, py) in enumerate(peers):
            for a in range(n):
                fetch(a, j, 2 * px + py).wait_recv()
                passed.append(forward(a, j, mine))
                passed[-1].start()
        for j in range(3):
            for a in range(n):
                forward(a, j, other).wait_recv()
        for cp in sends + passed:
            cp.wait_send()

    outs = [jax.ShapeDtypeStruct((DEPTH, N_CHIPS) + s.shape[1:], s.dtype) for s in srcs]
    gathered = _comm_call(body, name, n, outs, (3 * n, 3 * n, 3 * n, 3 * n))(*srcs)
    me = 2 * lax.axis_index("x") + lax.axis_index("y")
    return [lax.dynamic_update_slice(g, s[:, None], (0, me, 0, 0)) for g, s in zip(gathered, srcs)]


def swap_layer_halves(ps, name):
    n = len(ps)

    def body(*refs):
        src, dst = refs[:n], refs[n:2 * n]
        send_sems, recv_sems = refs[2 * n:]
        x, y, c = _mesh_position()
        _, other = _layer_halves(c)
        copies = [pltpu.make_async_remote_copy(src_ref=src[a].at[other], dst_ref=dst[a], send_sem=send_sems.at[a], recv_sem=recv_sems.at[a],
                                               device_id=(x, y, 1 - c), device_id_type=MESH) for a in range(n)]
        for cp in copies:
            cp.start()
        for cp in copies:
            cp.wait()

    outs = [jax.ShapeDtypeStruct((DEPTH // 2,) + p.shape[1:], p.dtype) for p in ps]
    return _comm_call(body, name, n, outs, (n, n))(*ps)


def exchange_chip_sums(ss, name):
    n = len(ss)

    def body(*refs):
        src, dst = refs[:n], refs[n:2 * n]
        send_sems, recv_sems, local_sems = refs[2 * n:]
        x, y, c = _mesh_position()
        me = 2 * x + y
        peers = ((1 - x, y), (x, 1 - y), (1 - x, 1 - y))

        def copy(a, j, slot):
            px, py = peers[j]
            return pltpu.make_async_remote_copy(src_ref=src[a].at[:, 2 * px + py], dst_ref=dst[a].at[:, slot], send_sem=send_sems.at[3 * a + j],
                                                recv_sem=recv_sems.at[3 * a + j], device_id=(px, py, c), device_id_type=MESH)

        local = [pltpu.make_async_copy(src[a].at[:, me], dst[a].at[:, me], local_sems.at[a]) for a in range(n)]
        sends = [copy(a, j, me) for a in range(n) for j in range(3)]
        for cp in local + sends:
            cp.start()
        for j, (px, py) in enumerate(peers):
            for a in range(n):
                copy(a, j, 2 * px + py).wait_recv()
        for cp in sends:
            cp.wait_send()
        for cp in local:
            cp.wait()

    outs = [jax.ShapeDtypeStruct(s.shape, s.dtype) for s in ss]
    return _comm_call(body, name, n, outs, (3 * n, 3 * n, n))(*ss)


def share_layer_halves(gs, name):
    n = len(gs)

    def body(*refs):
        dst = refs[n:2 * n]
        send_sems, recv_sems = refs[2 * n:]
        x, y, c = _mesh_position()
        mine, other = _layer_halves(c)

        def copy(a, half):
            return pltpu.make_async_remote_copy(src_ref=dst[a].at[half], dst_ref=dst[a].at[half], send_sem=send_sems.at[a],
                                                recv_sem=recv_sems.at[a], device_id=(x, y, 1 - c), device_id_type=MESH)

        sends = [copy(a, mine) for a in range(n)]
        for cp in sends:
            cp.start()
        for a in range(n):
            copy(a, other).wait_recv()
        for cp in sends:
            cp.wait_send()

    return pl.pallas_call(
        body, name=name, in_specs=[ANY] * n, out_specs=[ANY] * n, out_shape=[jax.ShapeDtypeStruct(g.shape, g.dtype) for g in gs],
        input_output_aliases={a: a for a in range(n)}, scratch_shapes=[pltpu.SemaphoreType.DMA((n,)), pltpu.SemaphoreType.DMA((n,))],
    )(*gs)


def _row_block(rows, cols, itemsize):
    best = 16
    for rb in range(16, rows + 1, 16):
        if rows % rb == 0 and rb * cols * itemsize <= 2 * 1024 * 1024:
            best = rb
    assert rows % best == 0, (rows, cols)
    return best


def add_sibling_half(p, r, c, name):
    cols = p.shape[-1]
    p2, r2 = p.reshape(-1, cols), r.reshape(-1, cols)
    rows = r2.shape[0]
    rb = _row_block(rows, cols, 2)
    steps = rows // rb

    def body(c_ref, p_ref, r_ref, o_ref):
        o_ref[...] = (p_ref[...].astype(F32) + r_ref[...].astype(F32)).astype(BF16)

    out = pl.pallas_call(
        body, name=name, out_shape=jax.ShapeDtypeStruct((rows, cols), BF16),
        grid_spec=pltpu.PrefetchScalarGridSpec(
            num_scalar_prefetch=1, grid=(steps,), in_specs=[pl.BlockSpec((rb, cols), lambda i, c_ref: (c_ref[0] * steps + i, 0)),
                                                            pl.BlockSpec((rb, cols), lambda i, c_ref: (i, 0))],
            out_specs=pl.BlockSpec((rb, cols), lambda i, c_ref: (i, 0))),
        compiler_params=_params(("parallel",)),
    )(jnp.reshape(c, (1,)).astype(jnp.int32), p2, r2)
    return out.reshape(r.shape)


def sum_chip_slots(l, c, name):
    layers, n, rows, cols = l.shape
    rb = _row_block(rows, cols, 4)

    def body(c_ref, l_ref, o_ref):
        acc = l_ref[0, 0].astype(F32)
        for s in range(1, n):
            acc = acc + l_ref[0, s].astype(F32)
        o_ref[0] = acc

    return pl.pallas_call(
        body, name=name, out_shape=jax.ShapeDtypeStruct((DEPTH, rows, cols), F32),
        grid_spec=pltpu.PrefetchScalarGridSpec(
            num_scalar_prefetch=1, grid=(layers, rows // rb),
            in_specs=[pl.BlockSpec((1, n, rb, cols), lambda a, i, c_ref: (a, 0, i, 0))],
            out_specs=pl.BlockSpec((1, rb, cols), lambda a, i, c_ref: (c_ref[0] * layers + a, i, 0))),
        compiler_params=_params(("parallel", "parallel")),
    )(jnp.reshape(c, (1,)).astype(jnp.int32), l)


def _comm_block(rows):
    return 1024 if rows % 1024 == 0 else rows


def sum_slots(buf, name):
    n, r, c = buf.shape
    rb = _comm_block(r)

    def body(b_ref, o_ref):
        acc = b_ref[0].astype(F32)
        for s in range(1, n):
            acc = acc + b_ref[s].astype(F32)
        o_ref[...] = acc

    return pl.pallas_call(
        body, name=name, grid=(r // rb,), in_specs=[pl.BlockSpec((n, rb, c), lambda i: (0, i, 0))],
        out_specs=pl.BlockSpec((rb, c), lambda i: (i, 0)), out_shape=jax.ShapeDtypeStruct((r, c), F32),
        compiler_params=_params(("parallel",)),
    )(buf)


def add_pair(a, b, out_dtype, name):
    shape = a.shape
    a2, b2 = a.reshape(-1, shape[-1]), b.reshape(-1, shape[-1])
    r, c = a2.shape
    rb = _comm_block(r)

    def body(a_ref, b_ref, o_ref):
        o_ref[...] = (a_ref[...].astype(F32) + b_ref[...].astype(F32)).astype(out_dtype)

    out = pl.pallas_call(
        body, name=name, grid=(r // rb,), in_specs=[pl.BlockSpec((rb, c), lambda i: (i, 0))] * 2,
        out_specs=pl.BlockSpec((rb, c), lambda i: (i, 0)), out_shape=jax.ShapeDtypeStruct((r, c), out_dtype),
        compiler_params=_params(("parallel",)),
    )(a2, b2)
    return out.reshape(shape)


def adamw(w, g, m, v):
    shape = w.shape
    cols = shape[-1]
    rows = math.prod(shape[:-1])
    rb = rows if rows * cols <= 256 * 1024 else 256
    assert rows % rb == 0, shape

    def body(w_ref, g_ref, m_ref, v_ref, d_ref, nm_ref, nv_ref):
        g_ = g_ref[...]
        nm = ADAM_B1 * m_ref[...] + (1.0 - ADAM_B1) * g_
        nv = ADAM_B2 * v_ref[...] + (1.0 - ADAM_B2) * (g_ * g_)
        m_hat = nm / (1.0 - ADAM_B1 ** ADAM_STEP)
        v_hat = nv / (1.0 - ADAM_B2 ** ADAM_STEP)
        d_ref[...] = -ADAM_LR * (m_hat / (jnp.sqrt(v_hat) + ADAM_EPS) + ADAM_WD * w_ref[...])
        nm_ref[...] = nm
        nv_ref[...] = nv

    spec = pl.BlockSpec((rb, cols), lambda i: (i, 0))
    outs = pl.pallas_call(
        body, name="adamw", grid=(rows // rb,), in_specs=[spec] * 4, out_specs=[spec] * 3,
        out_shape=[jax.ShapeDtypeStruct((rows, cols), F32)] * 3, compiler_params=_params(("parallel",)),
    )(*(a.reshape(rows, cols) for a in (w, g, m, v)))
    return tuple(o.reshape(shape) for o in outs)


def _pack(arrays, dtype, row_multiple):
    flat = jnp.concatenate([a.astype(dtype).reshape(-1) for a in arrays])
    per = LANES * row_multiple
    total = -(-flat.shape[0] // per) * per
    return jnp.pad(flat, (0, total - flat.shape[0])).reshape(total // LANES, LANES)


def _unpack(buf, shapes):
    flat = buf.reshape(-1)
    out, off = [], 0
    for s in shapes:
        n = math.prod(s)
        out.append(flat[off:off + n].reshape(s))
        off += n
    return out


def _input_weights(blocks):
    c0, c1, c2, c3 = blocks[0], blocks[1], blocks[2], blocks[3]
    pad = lambda n: jnp.zeros((blocks.shape[1], n), blocks.dtype)
    w_br = jnp.concatenate([c1[:, 376:1400], c0[:, 0:896], pad(64), c0[:, 896:928], pad(32), c0[:, 928:], c1[:, 0:376]], axis=1)
    return w_br, jnp.concatenate([c1[:, 1400:], c2, c3], axis=1)


def _input_weights_inverse(dw_br, dw_gl):
    c0 = jnp.concatenate([dw_br[:, 1024:1920], dw_br[:, 1984:2016], dw_br[:, 2048:2952]], axis=1)
    c1 = jnp.concatenate([dw_br[:, 2952:ZB], dw_br[:, 0:1024], dw_gl[:, 0:432]], axis=1)
    return jnp.stack([c0, c1, dw_gl[:, 432:2264], dw_gl[:, 2264:]])


def _uq_layout(w):
    r = w.reshape(w.shape[0], HEADS, QK_NOPE + QK_ROPE)
    return jnp.pad(r, ((0, 0), (0, 0), (0, HEAD_PAD - QK_NOPE - QK_ROPE))).reshape(w.shape[0], HEADS * HEAD_PAD)


def _uq_layout_inverse(dw):
    return dw.reshape(dw.shape[0], HEADS, HEAD_PAD)[:, :, :QK_NOPE + QK_ROPE].reshape(dw.shape[0], HEADS * (QK_NOPE + QK_ROPE))


def _ukv_layout(w):
    r = w.reshape(w.shape[0], HEADS, QK_NOPE + V_DIM)
    kp = jnp.pad(r[:, :, :QK_NOPE], ((0, 0), (0, 0), (0, HEAD_PAD - QK_NOPE))).reshape(w.shape[0], HEADS * HEAD_PAD)
    return jnp.concatenate([kp, r[:, :, QK_NOPE:].reshape(w.shape[0], HEADS * V_DIM)], axis=1)


def _ukv_layout_inverse(dw):
    n = dw.shape[0]
    dk = dw[:, :HEADS * HEAD_PAD].reshape(n, HEADS, HEAD_PAD)[:, :, :QK_NOPE]
    dv = dw[:, HEADS * HEAD_PAD:].reshape(n, HEADS, V_DIM)
    return jnp.concatenate([dk, dv], axis=2).reshape(n, HEADS * (QK_NOPE + V_DIM))


def _block_diag(pw):
    out = jnp.zeros((256, 256), pw.dtype)
    for g in range(4):
        out = lax.dynamic_update_slice(out, pw[g], (64 * g, 64 * g))
    return out


def _block_diag_inverse(d):
    return jnp.stack([d[64 * g:64 * (g + 1), 64 * g:64 * (g + 1)] for g in range(4)])


def _pad_rows(a, n):
    return jnp.pad(a, ((0, n - a.shape[0]), (0, 0)))


def _rope_tables(rows):
    inv = 1.0 / (ROPE_THETA ** (jnp.arange(0, QK_ROPE, 2, dtype=F32) / QK_ROPE))
    ang = jnp.arange(rows, dtype=F32)[:, None] * inv[None, :]
    cos, sin = jnp.cos(ang), jnp.sin(ang)
    one, zero = jnp.ones((rows, 1), F32), jnp.zeros((rows, 1), F32)
    rep = lambda a, n: jnp.broadcast_to(a, (rows, n))
    c = jnp.concatenate([rep(one, 64), cos, cos, rep(one, 32)], axis=1)
    s1 = jnp.concatenate([rep(zero, 64), -sin, rep(zero, 48)], axis=1)
    s2 = jnp.concatenate([rep(zero, 80), sin, rep(zero, 32)], axis=1)
    return jnp.concatenate([c, s1, s2], axis=1)


def _misc_block(parts):
    out = []
    for name, rows in MISC:
        a = parts[name]
        if name == "w_o":
            a = a.reshape(a.shape[:-2] + (rows, 256))
        elif name == "w_uq":
            a = jnp.pad(a, [(0, 0)] * (a.ndim - 1) + [(0, 256 - a.shape[-1])])
        out.append(a)
    return jnp.concatenate(out, axis=-2)


def _misc_unblock(block):
    out, off = {}, 0
    for name, rows in MISC:
        a = block[..., off:off + rows, :]
        off += rows
        if name == "w_o":
            a = a.reshape(a.shape[:-2] + (256, D_MODEL))
        elif name == "w_uq":
            a = a[..., :192]
        out[name] = a
    return out


def _to_chip_blocks(name, a):
    if name == "w_o":
        return a.reshape(a.shape[:-2] + (N_CHIPS, a.shape[-2] // N_CHIPS, a.shape[-1]))
    return jnp.swapaxes(a.reshape(a.shape[:-1] + (N_CHIPS, a.shape[-1] // N_CHIPS)), -3, -2)


def _from_chip_blocks(name, b):
    if name == "w_o":
        return b.reshape(b.shape[:-3] + (N_CHIPS * b.shape[-2], b.shape[-1]))
    s = jnp.swapaxes(b, -3, -2)
    return s.reshape(s.shape[:-2] + (N_CHIPS * s.shape[-1],))


def gather_weights(shards):
    misc = _misc_block({n: shards[n] for n, _ in MISC}).astype(BF16)
    g_in, g_misc = gather_layers([shards["w_in"].astype(BF16), misc], "gather_weights")
    out = {"w_in": g_in}
    for name, blocks in _misc_unblock(g_misc).items():
        out[name] = _from_chip_blocks(name, blocks)
    small = chip_exchange(_pack([shards[n] for n, _, _ in SHARDED_SMALL], F32, 8), True, "gather_small_ici")
    per_chip = [_unpack(small[k], [s for _, s, _ in SHARDED_SMALL]) for k in range(N_CHIPS)]
    for idx, (name, _, axis) in enumerate(SHARDED_SMALL):
        out[name] = jnp.concatenate([per_chip[k][idx] for k in range(N_CHIPS)], axis=axis)
    return out


def reduce_large(grads, c):
    ps = [grads["w_in"].astype(BF16), _misc_block({n: _to_chip_blocks(n, grads[n]) for n, _ in MISC}).astype(BF16)]
    rs = swap_layer_halves(ps, "reduce_grads_d2d")
    ss = [add_sibling_half(p, r, c, "reduce_grads_pair_%d" % a) for a, (p, r) in enumerate(zip(ps, rs))]
    ls = exchange_chip_sums(ss, "reduce_grads_ici")
    gs = [sum_chip_slots(l, c, "reduce_grads_sum_%d" % a) for a, l in enumerate(ls)]
    g_in, g_misc = share_layer_halves(gs, "reduce_grads_share")
    out = {"w_in": g_in}
    out.update(_misc_unblock(g_misc))
    return out


def reduce_small(grads, chip):
    names = [n for n, _ in REPLICATED] + [n for n, _, _ in SHARDED_SMALL]
    buf = _pack([grads[n] for n in names], F32, 8)
    chip_sum = add_pair(buf, sibling_swap(buf, "reduce_small_d2d"), F32, "reduce_small_pair")
    total = sum_slots(chip_exchange(chip_sum, True, "reduce_small_ici"), "reduce_small_sum")
    out = dict(zip(names, _unpack(total, [grads[n].shape for n in names])))
    for name, shape, axis in SHARDED_SMALL:
        out[name] = lax.dynamic_slice_in_dim(out[name], chip * shape[axis], shape[axis], axis)
    return out


def _layer_weights(w, i):
    w_br, w_gl = _input_weights(w["w_in"][i])
    conf_w = jnp.pad(w["conf_dw_w"][i].astype(F32), ((0, 32 - CONF_K), (0, 0)))
    sc_w = jnp.pad(w["sc_dw_w"][i].astype(F32), ((0, 8 - SC_K), (0, 0)))
    conf_vec = jnp.concatenate([w["conf_dw_b"][i][None], w["conf_ln_g"][i][None], w["conf_ln_b"][i][None], jnp.zeros((5, 256), F32)])
    return dict(
        w_br=w_br.astype(BF16), w_gl=w_gl.astype(BF16),
        gpre=w["pre_norm_g"][i][None], bias=w["gate_bias"][i][None], pwbd=_block_diag(w["pool_w"][i]).astype(BF16),
        pscale=w["pool_scale"][i][None], gq=w["q_norm_g"][i][None], wuq=_uq_layout(w["w_uq"][i]).astype(BF16),
        gkv=w["kv_norm_g"][i][None], wukv=_ukv_layout(w["w_ukv"][i]).astype(BF16), conf_w=conf_w, conf_vec=conf_vec, sc_w=sc_w,
        woa=w["w_out_pool"][i].astype(BF16), wob=w["w_out_mla"][i].astype(BF16), woc=w["w_out_conf"][i].astype(BF16),
        wod=w["w_out_sc"][i].astype(BF16), wo=w["w_o"][i].astype(BF16), gpost=w["post_norm_g"][i][None])


def local_step(x, target, w):
    seq = x.shape[0]
    length = N_META + seq
    rows = -(-length // ROW_TILE) * ROW_TILE
    bt = _big_tile(rows)
    hres = _pad_rows(jnp.concatenate([w["meta_tokens"].astype(F32), x], axis=0), rows)
    tgt = jnp.pad(target, ((N_META, rows - length), (0, 0)))
    rope = _rope_tables(rows)

    saved = []
    for i in range(DEPTH):
        lw = _layer_weights(w, i)
        z_br, hb = prenorm_project(hres, lw["gpre"], lw["w_br"])
        z_gl = matmul(hb, lw["w_gl"], "nn", BF16, bt, 1024, D_MODEL, "project_gates")
        ua, uc, ud, q, k, v = branches_fwd(z_br, rope, lw["pwbd"], lw["pscale"], lw["gq"], lw["wuq"], lw["gkv"], lw["wukv"],
                                           lw["conf_w"], lw["conf_vec"], lw["sc_w"])
        o_att, lse = attention_fwd(q, k, v)
        ub, mb, o, hnew = merge_fwd(ua, o_att, uc, ud, z_br, z_gl, lw["bias"], lw["woa"], lw["wob"], lw["woc"], lw["wod"], lw["wo"],
                                    lw["gpost"], hres)
        saved.append(dict(lw=lw, hres=hres, hb=hb, z_br=z_br, z_gl=z_gl, ua=ua, ub=ub, uc=uc, ud=ud, q=q, k=k, v=v, o_att=o_att,
                          lse=lse, mb=mb, o=o))
        hres = hnew

    dh, total = loss_head(hres, tgt, seq)

    g = {n: [None] * DEPTH for n in WEIGHT_ORDER if n != "meta_tokens"}
    for i in reversed(range(DEPTH)):
        s = saved[i]
        lw = s["lw"]
        dm, dwo, dgpost = postnorm_bwd(dh, s["o"], s["mb"], lw["wo"], lw["gpost"])
        dua, dub, duc, dud, dz_gl, dwa, dwb, dwc, dwd, dbias = merge_bwd(dm, s["ua"], s["ub"], s["uc"], s["ud"], s["z_gl"], lw["bias"],
                                                                     lw["woa"], lw["wob"], lw["woc"], lw["wod"])
        dz_br = lax.empty((rows, ZB), BF16)
        dz_br, dpw, dps = pool_bwd(s["z_br"], dua, lw["pwbd"], lw["pscale"], dz_br)
        dz_br, dsw = shortconv_bwd(s["z_br"], dud, lw["sc_w"], dz_br)
        dc, dz_br, dcvec = conformer_bwd_tail(s["z_br"], duc, lw["conf_w"], lw["conf_vec"], dz_br)
        dz_br, dcw = conformer_bwd_conv(s["z_br"], dc, lw["conf_w"], dz_br)
        do, dz_br, delta = attention_bwd_prep(dub, s["o_att"], s["z_br"], dz_br)
        dq, dk, dv = attention_bwd(s["q"], s["k"], s["v"], do, s["lse"], delta)
        dz_br, dwuq, dwukv, dgq, dgkv = mla_prep_bwd(dq, dk, dv, s["z_br"], rope, lw["gq"], lw["wuq"], lw["gkv"], lw["wukv"], dz_br)
        dw_br = matmul(s["hb"], dz_br, "tn", F32, D_MODEL, ZB // 2, ROW_TILE, "grad_w_branch")
        dw_gl = matmul(s["hb"], dz_gl, "tn", F32, D_MODEL, 1024, ROW_TILE, "grad_w_gates")
        dh_gl = matmul(dz_gl, lw["w_gl"], "nt", F32, bt, D_MODEL, 1024, "grad_h_gates")
        dh, dgpre = prenorm_bwd(dz_br, lw["w_br"], dh_gl, s["hres"], lw["gpre"], dh)

        g["pre_norm_g"][i] = dgpre[0]
        g["w_in"][i] = _input_weights_inverse(dw_br, dw_gl)
        g["gate_bias"][i] = dbias[0]
        g["pool_w"][i] = _block_diag_inverse(dpw)
        g["pool_scale"][i] = dps[0]
        g["w_out_pool"][i] = dwa
        g["q_norm_g"][i] = dgq[0]
        g["w_uq"][i] = _uq_layout_inverse(dwuq)
        g["kv_norm_g"][i] = dgkv[0]
        g["w_ukv"][i] = _ukv_layout_inverse(dwukv)
        g["w_out_mla"][i] = dwb
        g["conf_dw_w"][i] = dcw[:CONF_K]
        g["conf_dw_b"][i] = dcvec[2]
        g["conf_ln_g"][i] = dcvec[0]
        g["conf_ln_b"][i] = dcvec[1]
        g["w_out_conf"][i] = dwc
        g["sc_dw_w"][i] = dsw[:SC_K]
        g["w_out_sc"][i] = dwd
        g["w_o"][i] = dwo
        g["post_norm_g"][i] = dgpost[0]

    grads = {n: jnp.stack(parts) for n, parts in g.items()}
    grads["meta_tokens"] = dh[:N_META]
    return total[0, 0], dh[N_META:length], grads


def kernel(x, meta_tokens, pre_norm_g, w_in, gate_bias, pool_w, pool_scale, w_out_pool, q_norm_g, w_uq, kv_norm_g, w_ukv, w_out_mla, conf_dw_w, conf_dw_b, conf_ln_g, conf_ln_b, w_out_conf, sc_dw_w, w_out_sc, w_o, post_norm_g, loss_target, m_meta_tokens, m_pre_norm_g, m_w_in, m_gate_bias, m_pool_w, m_pool_scale, m_w_out_pool, m_q_norm_g, m_w_uq, m_kv_norm_g, m_w_ukv, m_w_out_mla, m_conf_dw_w, m_conf_dw_b, m_conf_ln_g, m_conf_ln_b, m_w_out_conf, m_sc_dw_w, m_w_out_sc, m_w_o, m_post_norm_g, v_meta_tokens, v_pre_norm_g, v_w_in, v_gate_bias, v_pool_w, v_pool_scale, v_w_out_pool, v_q_norm_g, v_w_uq, v_kv_norm_g, v_w_ukv, v_w_out_mla, v_conf_dw_w, v_conf_dw_b, v_conf_ln_g, v_conf_ln_b, v_w_out_conf, v_sc_dw_w, v_w_out_sc, v_w_o, v_post_norm_g):
    args = locals()
    weights = {n: args[n] for n in WEIGHT_ORDER}
    c = lax.axis_index("c")
    chip = 2 * lax.axis_index("x") + lax.axis_index("y")

    full = dict(weights)
    full.update(gather_weights(weights))
    total, dx, grads = local_step(x[0], loss_target[0], full)
    loss = lax.psum(total * (0.5 / D_MODEL), ("x", "y", "c"))

    reduced = reduce_large(grads, c)
    reduced.update(reduce_small(grads, chip))

    deltas, new_m, new_v = [], [], []
    for n in WEIGHT_ORDER:
        d, nm, nv = adamw(weights[n], reduced[n], args["m_" + n], args["v_" + n])
        deltas.append(d)
        new_m.append(nm)
        new_v.append(nv)
    return (loss, dx[None], *[reduced[n] for n in WEIGHT_ORDER], *deltas, *new_m, *new_v)
```

```python
import functools
import math

import jax
import jax.numpy as jnp
from jax import lax
from jax.experimental import pallas as pl
from jax.experimental.pallas import tpu as pltpu

F32 = jnp.float32
BF16 = jnp.bfloat16

D_MODEL = 1024
DEPTH = 4
N_META = 16
EPS = 1e-6
HEADS = 8
QK_NOPE = 64
QK_ROPE = 32
V_DIM = 64
HEAD_PAD = 128
ROPE_THETA = 10000.0
Q_SCALE = (QK_NOPE + QK_ROPE) ** -0.5
CONF_K = 31
SC_K = 3
IN_W = 7328
N_CHIPS = 4

ZB = 3328
ZG = 4096
BG, C2, XV, SG, PV, PG, CQ, CKV, KR, MG, CA, CGT, CG = (0, 256, 512, 768, 1024, 1280, 1536, 1792, 1920, 2048, 2560, 2816, 3072)

ROW_TILE = 384
HALO = 32
LANES = 128
VMEM_LIMIT = 56 * 1024 * 1024

ADAM_LR = 0.001
ADAM_B1 = 0.9
ADAM_B2 = 0.999
ADAM_EPS = 1e-08
ADAM_WD = 0.01
ADAM_STEP = 10

MESH = pl.DeviceIdType.MESH
ANY = pl.BlockSpec(memory_space=pl.ANY)

MISC = (
    ("w_out_pool", 256), ("w_ukv", 128), ("w_out_mla", 512), ("w_out_conf", 256), ("w_out_sc", 256), ("w_o", 1024), ("w_uq", 256))
SHARDED_SMALL = (
    ("meta_tokens", (N_META, 256), 1),
    ("conf_dw_w", (DEPTH, CONF_K, 64), 2),
    ("sc_dw_w", (DEPTH, SC_K, 64), 2),
)
REPLICATED = (
    ("pre_norm_g", (DEPTH, D_MODEL)),
    ("gate_bias", (DEPTH, 4 * D_MODEL)),
    ("pool_w", (DEPTH, 4, 64, 64)),
    ("pool_scale", (DEPTH, 256)),
    ("q_norm_g", (DEPTH, 256)),
    ("kv_norm_g", (DEPTH, 128)),
    ("conf_dw_b", (DEPTH, 256)),
    ("conf_ln_g", (DEPTH, 256)),
    ("conf_ln_b", (DEPTH, 256)),
    ("post_norm_g", (DEPTH, D_MODEL)),
)
WEIGHT_ORDER = ("meta_tokens", "pre_norm_g", "w_in", "gate_bias", "pool_w", "pool_scale", "w_out_pool", "q_norm_g", "w_uq",
                "kv_norm_g", "w_ukv", "w_out_mla", "conf_dw_w", "conf_dw_b", "conf_ln_g", "conf_ln_b", "w_out_conf", "sc_dw_w",
                "w_out_sc", "w_o", "post_norm_g")


def _dot(a, b):
    return lax.dot_general(a, b, (((1,), (0,)), ((), ())), preferred_element_type=F32)


def _dot_nt(a, b):
    return lax.dot_general(a, b, (((1,), (1,)), ((), ())), preferred_element_type=F32)


def _dot_tn(a, b):
    return lax.dot_general(a, b, (((0,), (0,)), ((), ())), preferred_element_type=F32)


def _sigmoid(x):
    return jax.nn.sigmoid(x)


def _silu(x):
    return x * _sigmoid(x)


def _silu_grad(x):
    s = _sigmoid(x)
    return s * (1.0 + x * (1.0 - s))


def _rms(x, g):
    return x * lax.rsqrt(jnp.mean(x * x, axis=-1, keepdims=True) + EPS) * g


def _sh(x, d):
    return x if d == 0 else pltpu.roll(x, d, 0)


def _ash(x, d):
    return x if d == 0 else pltpu.roll(x, x.shape[0] - d, 0)


def _lanes8(t):
    return jnp.concatenate([t] * HEADS, axis=1)


def _pool_window_sums(v, shift):
    a2 = v + shift(v, 1)
    a4 = a2 + shift(a2, 2)
    a8 = a4 + shift(a4, 4)
    a16 = a8 + shift(a8, 8)
    lane = lax.broadcasted_iota(jnp.int32, v.shape, 1)
    return jnp.where(lane < 64, a2, jnp.where(lane < 128, a4, jnp.where(lane < 192, a8, a16)))


def _pool_counts(first_row, rows):
    pos = first_row + lax.broadcasted_iota(jnp.int32, (rows, 256), 0)
    lane = lax.broadcasted_iota(jnp.int32, (rows, 256), 1)
    width = jnp.where(lane < 64, 2, jnp.where(lane < 128, 4, jnp.where(lane < 192, 8, 16)))
    return jnp.maximum(jnp.minimum(pos + 1, width), 1).astype(F32)


def _params(sem=None):
    return pltpu.CompilerParams(dimension_semantics=sem, vmem_limit_bytes=VMEM_LIMIT)


def _tile_specs(t, n_halo_blocks, li=0):
    per = t // HALO

    def layer(shape):
        return pl.BlockSpec((None,) + tuple(shape), lambda i: (li,) + (0,) * len(shape))

    def cur(c, cb=0):
        return pl.BlockSpec((t, c), lambda i: (i, cb))

    def prev(c, cb=0):
        return pl.BlockSpec((HALO, c), lambda i: (jnp.maximum(i * per - 1, 0), cb))

    def nxt(c, cb=0):
        return pl.BlockSpec((HALO, c), lambda i: (jnp.minimum((i + 1) * per, n_halo_blocks - 1), cb))

    def full(shape):
        return pl.BlockSpec(shape, lambda i: (0,) * len(shape))

    return cur, prev, nxt, full, layer


def _big_tile(rows):
    return rows // 3 if rows % (3 * LANES) == 0 else ROW_TILE


def matmul(a, b, mode, out_dtype, tm, tn, tk, name, b_layer=None):
    bs = b.shape if b_layer is None else b.shape[1:]
    lead = () if b_layer is None else (None,)
    pick = (lambda *ix: ix) if b_layer is None else (lambda *ix: (b_layer,) + ix)
    if mode == "nn":
        (m, k), n = a.shape, bs[1]
        a_spec = pl.BlockSpec((tm, tk), lambda i, j, kk: (i, kk))
        b_spec = pl.BlockSpec(lead + (tk, tn), lambda i, j, kk: pick(kk, j))
        dot = _dot
    elif mode == "nt":
        (m, k), n = a.shape, bs[0]
        a_spec = pl.BlockSpec((tm, tk), lambda i, j, kk: (i, kk))
        b_spec = pl.BlockSpec(lead + (tn, tk), lambda i, j, kk: pick(j, kk))
        dot = _dot_nt
    else:
        (k, m), n = a.shape, bs[1]
        a_spec = pl.BlockSpec((tk, tm), lambda i, j, kk: (kk, i))
        b_spec = pl.BlockSpec(lead + (tk, tn), lambda i, j, kk: pick(kk, j))
        dot = _dot_tn
    assert m % tm == 0 and n % tn == 0 and k % tk == 0, (a.shape, bs, tm, tn, tk)
    nk = k // tk

    def body(a_ref, b_ref, o_ref, acc_ref):
        kk = pl.program_id(2)

        @pl.when(kk == 0)
        def _():
            acc_ref[...] = jnp.zeros_like(acc_ref)

        acc_ref[...] += dot(a_ref[...], b_ref[...])

        @pl.when(kk == nk - 1)
        def _():
            o_ref[...] = acc_ref[...].astype(out_dtype)

    return pl.pallas_call(
        body, name=name, grid=(m // tm, n // tn, nk), in_specs=[a_spec, b_spec],
        out_specs=pl.BlockSpec((tm, tn), lambda i, j, kk: (i, j)), out_shape=jax.ShapeDtypeStruct((m, n), out_dtype),
        scratch_shapes=[pltpu.VMEM((tm, tn), F32)], compiler_params=_params(("parallel", "parallel", "arbitrary")),
    )(a, b)


def prenorm_project(hres, g, w, li):
    rows, d = hres.shape
    n = w.shape[2]
    tm, tn = _big_tile(rows), n // 2

    def body(x_ref, g_ref, w_ref, z_ref, hb_ref):
        @pl.when(pl.program_id(1) == 0)
        def _():
            hb_ref[...] = _rms(x_ref[...], g_ref[...]).astype(BF16)

        z_ref[...] = _dot(hb_ref[...], w_ref[...]).astype(BF16)

    return pl.pallas_call(
        body, name="prenorm_project", grid=(rows // tm, n // tn),
        in_specs=[pl.BlockSpec((tm, d), lambda i, j: (i, 0)), pl.BlockSpec((None, 1, d), lambda i, j: (li, 0, 0)),
                  pl.BlockSpec((None, d, tn), lambda i, j: (li, 0, j))],
        out_specs=[pl.BlockSpec((tm, tn), lambda i, j: (i, j)), pl.BlockSpec((tm, d), lambda i, j: (i, 0))],
        out_shape=[jax.ShapeDtypeStruct((rows, n), BF16), jax.ShapeDtypeStruct((rows, d), BF16)],
        compiler_params=_params(("parallel", "arbitrary")),
    )(hres, g, w)


def _rope(q, c, s1, s2, width):
    return q * c + pltpu.roll(q, width - 16, 1) * s1 + pltpu.roll(q, 16, 1) * s2


def _rope_transposed(dq, c, s1, s2, width):
    return dq * c + pltpu.roll(dq * s1, 16, 1) + pltpu.roll(dq * s2, width - 16, 1)


def _conf_conv(g1, w_ref):
    acc = jnp.zeros_like(g1)
    for k in range(CONF_K):
        acc = acc + w_ref[k:k + 1, :] * _sh(g1, CONF_K - 1 - k)
    return acc


def _conf_tail(c, cg, lg, lb):
    mu = jnp.mean(c, axis=-1, keepdims=True)
    xc = c - mu
    var = jnp.mean(xc * xc, axis=-1, keepdims=True)
    n = xc * lax.rsqrt(var + EPS) * lg + lb
    return _silu(n) * _silu(cg)


def branches_fwd(z_br, rope, pwbd, pscale, gq, wuq, gkv, wukv, conf_w, conf_vec, sc_w, li):
    rows = z_br.shape[0]
    t = ROW_TILE
    cur, prev, _, _, layer = _tile_specs(t, rows // HALO, li)

    def body(zc_ref, zp_ref, rope_ref, pw_ref, ps_ref, gq_ref, wuq_ref, gkv_ref, wukv_ref, cw_ref, cv_ref, sw_ref,
             ua_ref, uc_ref, ud_ref, q_ref, k_ref, v_ref):
        i = pl.program_id(0)
        zp = jnp.where(i == 0, jnp.zeros(zp_ref.shape, zp_ref.dtype), zp_ref[...])

        def ext(lo, w=256):
            return jnp.concatenate([zp[:, lo:lo + w], zc_ref[:, lo:lo + w]], axis=0).astype(F32)

        def col(lo, w=256):
            return zc_ref[:, lo:lo + w].astype(F32)

        v = ext(PV)
        p = (_pool_window_sums(v, _sh) / _pool_counts(i * t - HALO, t + HALO) - v)[HALO:]
        ya = _dot(p.astype(BF16), pw_ref[...]) * ps_ref[...]
        ua_ref[...] = (ya * _silu(col(PG))).astype(BF16)

        g1 = ext(CA) * _sigmoid(ext(CGT))
        c = _conf_conv(g1, cw_ref)[HALO:] + cv_ref[0:1, :]
        uc_ref[...] = _conf_tail(c, col(CG), cv_ref[1:2, :], cv_ref[2:3, :]).astype(BF16)

        e = ext(C2) * ext(XV)
        f = jnp.zeros_like(e)
        for k in range(SC_K):
            f = f + sw_ref[k:k + 1, :] * _sh(e, SC_K - 1 - k)
        ud_ref[...] = (col(BG) * f[HALO:] * _silu(col(SG))).astype(BF16)

        cth, s1, s2 = rope_ref[:, 0:128], rope_ref[:, 128:256], rope_ref[:, 256:384]
        qn = _rms(col(CQ), gq_ref[...]).astype(BF16)
        q = _dot(qn, wuq_ref[...])
        w8 = HEADS * HEAD_PAD
        q_ref[...] = (_rope(q, _lanes8(cth), _lanes8(s1), _lanes8(s2), w8) * Q_SCALE).astype(BF16)
        kvn = _rms(col(CKV, 128), gkv_ref[...]).astype(BF16)
        kv = _dot(kvn, wukv_ref[...])
        kr = _rope(col(KR, 128), cth, s1, s2, HEAD_PAD)
        k_ref[...] = (kv[:, :w8] + _lanes8(kr)).astype(BF16)
        v_ref[...] = kv[:, w8:].astype(BF16)

    outs = [jax.ShapeDtypeStruct((rows, 256), BF16)] * 3 + [jax.ShapeDtypeStruct((rows, 1024), BF16)] * 2 + [
        jax.ShapeDtypeStruct((rows, 512), BF16)]
    return pl.pallas_call(
        body, name="branches_fwd", grid=(rows // t,),
        in_specs=[cur(ZB), prev(ZB), cur(384), layer((256, 256)), layer((1, 256)), layer((1, 256)), layer((256, 1024)),
                  layer((1, 128)), layer((128, 1536)), layer((32, 256)), layer((8, 256)), layer((8, 256))],
        out_specs=[cur(256), cur(256), cur(256), cur(1024), cur(1024), cur(512)], out_shape=outs,
        compiler_params=_params(("parallel",)),
    )(z_br, z_br, rope, pwbd, pscale, gq, wuq, gkv, wukv, conf_w, conf_vec, sc_w)


def _head_lane_mask(h):
    lane = lax.broadcasted_iota(jnp.int32, (1, 2 * V_DIM), 1)
    return (lane >= V_DIM * h) & (lane < V_DIM * (h + 1))


def attention_fwd(q, k, v):
    rows = q.shape[0]
    tq = ROW_TILE
    nq = rows // tq

    def body(q_ref, k_ref, v_ref, o_ref, lse_ref):
        i = pl.program_id(1)

        def head_step(h, tile, n_tiles, carry, masked):
            m, l, acc = carry
            width = n_tiles * tq
            r0 = pl.multiple_of(tile * tq, tq)
            kh = k_ref[pl.ds(r0, width), HEAD_PAD * h:HEAD_PAD * (h + 1)]
            vh = jnp.where(_head_lane_mask(h), v_ref[pl.ds(r0, width), :], jnp.zeros((), BF16))
            s = _dot_nt(q_ref[:, HEAD_PAD * h:HEAD_PAD * (h + 1)], kh)
            if masked:
                row = lax.broadcasted_iota(jnp.int32, (tq, width), 0)
                colm = lax.broadcasted_iota(jnp.int32, (tq, width), 1)
                s = jnp.where(colm <= row + (width - tq), s, -1e30)
            m2 = jnp.maximum(m, jnp.max(s, axis=-1, keepdims=True))
            alpha = jnp.exp(m - m2)
            pr = jnp.exp(s - m2)
            return m2, alpha * l + jnp.sum(pr, axis=-1, keepdims=True), alpha * acc + _dot(pr.astype(BF16), vh)

        def step(tile, n_tiles, carry, masked):
            return tuple(head_step(h, tile, n_tiles, carry[h], masked) for h in range(2))

        init = (jnp.full((tq, 1), -1e30, F32), jnp.zeros((tq, 1), F32), jnp.zeros((tq, 2 * V_DIM), F32))
        carry = lax.fori_loop(0, i // 2, lambda t, cr: step(2 * t, 2, cr, False), (init, init))
        carry = lax.cond(i % 2 == 1, lambda cr: step(i - 1, 2, cr, True), lambda cr: step(i, 1, cr, True), carry)
        out = jnp.zeros((tq, 2 * V_DIM), F32)
        for h, (m, l, acc) in enumerate(carry):
            out = out + acc / l
            lse_ref[h] = jnp.broadcast_to(m + jnp.log(l), (tq, LANES))
        o_ref[...] = out.astype(BF16)

    return pl.pallas_call(
        body, name="attention_fwd", grid=(HEADS // 2, nq),
        in_specs=[pl.BlockSpec((tq, 2 * HEAD_PAD), lambda p, i: (i, p)), pl.BlockSpec((rows, 2 * HEAD_PAD), lambda p, i: (0, p)),
                  pl.BlockSpec((rows, 2 * V_DIM), lambda p, i: (0, p))],
        out_specs=[pl.BlockSpec((tq, 2 * V_DIM), lambda p, i: (i, p)), pl.BlockSpec((2, tq, LANES), lambda p, i: (p, i, 0))],
        out_shape=[jax.ShapeDtypeStruct((rows, HEADS * V_DIM), BF16), jax.ShapeDtypeStruct((HEADS, rows, LANES), F32)],
        compiler_params=_params(("parallel", "parallel")),
    )(q, k, v)


def merge_fwd(ua, o_att, uc, ud, z_br, z_gl, bias, woa, wob, woc, wod, wo, gpost, hres, li):
    rows = hres.shape[0]
    t = ROW_TILE
    cur, _, _, _, layer = _tile_specs(t, rows // HALO, li)
    d = D_MODEL

    def body(ua_ref, ob_ref, uc_ref, ud_ref, mg_ref, gl_ref, b_ref, woa_ref, wob_ref, woc_ref, wod_ref, wo_ref, gp_ref, h_ref,
             ub_ref, mb_ref, o_ref, hn_ref):
        ub = (ob_ref[...].astype(F32) * _silu(mg_ref[...].astype(F32))).astype(BF16)
        ub_ref[...] = ub
        m = jnp.zeros((t, d), F32)
        for idx, (u, w_ref) in enumerate(((ua_ref[...], woa_ref), (ub, wob_ref), (uc_ref[...], woc_ref), (ud_ref[...], wod_ref))):
            gate = _sigmoid(gl_ref[:, d * idx:d * (idx + 1)].astype(F32) + b_ref[:, d * idx:d * (idx + 1)])
            m = m + gate * _dot(u, w_ref[...])
        mb = m.astype(BF16)
        mb_ref[...] = mb
        o = _dot(mb, wo_ref[...])
        o_ref[...] = o
        hn_ref[...] = h_ref[...] + _rms(o, gp_ref[...])

    return pl.pallas_call(
        body, name="merge_fwd", grid=(rows // t,),
        in_specs=[cur(256), cur(512), cur(256), cur(256), cur(512, MG // 512), cur(ZG), layer((1, ZG)), layer((256, d)), layer((512, d)),
                  layer((256, d)), layer((256, d)), layer((d, d)), layer((1, d)), cur(d)],
        out_specs=[cur(512), cur(d), cur(d), cur(d)],
        out_shape=[jax.ShapeDtypeStruct((rows, 512), BF16), jax.ShapeDtypeStruct((rows, d), BF16), jax.ShapeDtypeStruct((rows, d), F32),
                   jax.ShapeDtypeStruct((rows, d), F32)],
        compiler_params=_params(("parallel",)),
    )(ua, o_att, uc, ud, z_br, z_gl, bias, woa, wob, woc, wod, wo, gpost, hres)


def loss_head(hres, target, n_tokens):
    rows, d = hres.shape
    t = ROW_TILE
    cur, _, _, full, _ = _tile_specs(t, rows // HALO)
    n_steps = rows // t

    def body(h_ref, t_ref, dh_ref, tot_ref, acc_ref):
        i = pl.program_id(0)

        @pl.when(i == 0)
        def _():
            acc_ref[...] = jnp.zeros_like(acc_ref)

        r = i * t + lax.broadcasted_iota(jnp.int32, (t, 1), 0)
        diff = jnp.where((r >= N_META) & (r < N_META + n_tokens), h_ref[...] - t_ref[...], 0.0)
        dh_ref[...] = diff * (1.0 / d)
        acc_ref[...] += jnp.sum(diff * diff, axis=0, keepdims=True)

        @pl.when(i == n_steps - 1)
        def _():
            tot_ref[...] = jnp.broadcast_to(jnp.sum(acc_ref[...], axis=1, keepdims=True), (1, LANES))

    return pl.pallas_call(
        body, name="loss_head", grid=(n_steps,), in_specs=[cur(d), cur(d)], out_specs=[cur(d), full((1, LANES))],
        out_shape=[jax.ShapeDtypeStruct((rows, d), F32), jax.ShapeDtypeStruct((1, LANES), F32)],
        scratch_shapes=[pltpu.VMEM((1, d), F32)], compiler_params=_params(("arbitrary",)),
    )(hres, target)


def _accumulate(i, ref, value):
    @pl.when(i == 0)
    def _():
        ref[...] = value

    @pl.when(i > 0)
    def _():
        ref[...] += value


def postnorm_bwd(dh, o, mb, wo, gpost, li):
    rows, d = dh.shape
    t = ROW_TILE
    cur, _, _, full, layer = _tile_specs(t, rows // HALO, li)

    def body(dh_ref, o_ref, mb_ref, wo_ref, gp_ref, dm_ref, dwo_ref, dgp_ref):
        i = pl.program_id(0)
        _, vjp = jax.vjp(_rms, o_ref[...], gp_ref[...])
        do, dg = vjp(dh_ref[...])
        dob = do.astype(BF16)
        dm_ref[...] = _dot_nt(dob, wo_ref[...])
        _accumulate(i, dwo_ref, _dot_tn(mb_ref[...], dob))
        _accumulate(i, dgp_ref, dg)

    return pl.pallas_call(
        body, name="postnorm_bwd", grid=(rows // t,), in_specs=[cur(d), cur(d), cur(d), layer((d, d)), layer((1, d))],
        out_specs=[cur(d), full((d, d)), full((1, d))],
        out_shape=[jax.ShapeDtypeStruct((rows, d), F32), jax.ShapeDtypeStruct((d, d), F32), jax.ShapeDtypeStruct((1, d), F32)],
        compiler_params=_params(("arbitrary",)),
    )(dh, o, mb, wo, gpost)


def merge_bwd(dm, ua, ub, uc, ud, z_gl, bias, woa, wob, woc, wod, li):
    rows, d = dm.shape
    t = ROW_TILE
    cur, _, _, full, layer = _tile_specs(t, rows // HALO, li)
    widths = (256, 512, 256, 256)

    def body(dm_ref, ua_ref, ub_ref, uc_ref, ud_ref, gl_ref, b_ref, woa_ref, wob_ref, woc_ref, wod_ref,
             dua_ref, dub_ref, duc_ref, dud_ref, dgl_ref, dwa_ref, dwb_ref, dwc_ref, dwd_ref, db_ref):
        i = pl.program_id(0)
        dm = dm_ref[...]
        groups = ((ua_ref, woa_ref, dua_ref, dwa_ref), (ub_ref, wob_ref, dub_ref, dwb_ref), (uc_ref, woc_ref, duc_ref, dwc_ref),
                  (ud_ref, wod_ref, dud_ref, dwd_ref))
        for idx, (u_ref, w_ref, du_ref, dw_ref) in enumerate(groups):
            cols = slice(d * idx, d * (idx + 1))
            u = u_ref[...]
            gate = _sigmoid(gl_ref[:, cols].astype(F32) + b_ref[:, cols])
            dgl = dm * _dot(u, w_ref[...]) * gate * (1.0 - gate)
            dgl_ref[:, cols] = dgl.astype(BF16)
            _accumulate(i, db_ref.at[:, cols], jnp.sum(dgl, axis=0, keepdims=True))
            dyb = (dm * gate).astype(BF16)
            du_ref[...] = _dot_nt(dyb, w_ref[...])
            _accumulate(i, dw_ref, _dot_tn(u, dyb))

    return pl.pallas_call(
        body, name="merge_bwd", grid=(rows // t,),
        in_specs=[cur(d), cur(256), cur(512), cur(256), cur(256), cur(ZG), layer((1, ZG))] + [layer((w, d)) for w in widths],
        out_specs=[cur(256), cur(512), cur(256), cur(256), cur(ZG)] + [full((w, d)) for w in widths] + [full((1, ZG))],
        out_shape=[jax.ShapeDtypeStruct((rows, w), F32) for w in widths] + [jax.ShapeDtypeStruct((rows, ZG), BF16)] + [
            jax.ShapeDtypeStruct((w, d), F32) for w in widths] + [jax.ShapeDtypeStruct((1, ZG), F32)],
        compiler_params=_params(("arbitrary",)),
    )(dm, ua, ub, uc, ud, z_gl, bias, woa, wob, woc, wod)


def pool_bwd(z_br, dua, pwbd, pscale, dz_buf, li):
    rows = z_br.shape[0]
    t = ROW_TILE
    n_steps = rows // t
    cur, prev, nxt, full, layer = _tile_specs(t, rows // HALO, li)

    def body(zc_ref, zp_ref, zn_ref, dc_ref, dn_ref, pw_ref, ps_ref, _, dz_ref, dpw_ref, dps_ref):
        i = pl.program_id(0)
        zp = jnp.where(i == 0, jnp.zeros(zp_ref.shape, zp_ref.dtype), zp_ref[...])
        zn = jnp.where(i == n_steps - 1, jnp.zeros(zn_ref.shape, zn_ref.dtype), zn_ref[...])
        dun = jnp.where(i == n_steps - 1, jnp.zeros(dn_ref.shape, dn_ref.dtype), dn_ref[...])

        def ext(lo):
            return jnp.concatenate([zp[:, lo:lo + 256], zc_ref[:, lo:lo + 256], zn[:, lo:lo + 256]], axis=0).astype(F32)

        n_ext = t + 2 * HALO
        v, pg = ext(PV), ext(PG)
        cnt = _pool_counts(i * t - HALO, n_ext)
        p = (_pool_window_sums(v, _sh) / cnt - v)[HALO:HALO + t]
        du = jnp.concatenate([jnp.zeros((HALO, 256), F32), dc_ref[...], dun], axis=0)
        dya = du * _silu(pg)
        dypb = (dya * ps_ref[...]).astype(BF16)
        dp = _dot_nt(dypb, pw_ref[...])
        dv = (_pool_window_sums(dp / cnt, _ash) - dp)[HALO:HALO + t]
        pb = p.astype(BF16)
        pw = _dot(pb, pw_ref[...])
        duc, pgc = dc_ref[...], pg[HALO:HALO + t]
        dpg = duc * pw * ps_ref[...] * _silu_grad(pgc)
        dz_ref[...] = jnp.concatenate([dv, dpg], axis=1).astype(BF16)
        _accumulate(i, dpw_ref, _dot_tn(pb, dypb[HALO:HALO + t]))
        _accumulate(i, dps_ref, jnp.sum(dya[HALO:HALO + t] * pw, axis=0, keepdims=True))

    return pl.pallas_call(
        body, name="pool_bwd", grid=(n_steps,),
        in_specs=[cur(ZB), prev(ZB), nxt(ZB), cur(256), nxt(256), layer((256, 256)), layer((1, 256)), ANY],
        out_specs=[cur(512, PV // 512), full((256, 256)), full((1, 256))],
        out_shape=[jax.ShapeDtypeStruct((rows, ZB), BF16), jax.ShapeDtypeStruct((256, 256), F32), jax.ShapeDtypeStruct((1, 256), F32)],
        input_output_aliases={7: 0}, compiler_params=_params(("arbitrary",)),
    )(z_br, z_br, z_br, dua, dua, pwbd, pscale, dz_buf)


def shortconv_bwd(z_br, dud, sc_w, dz_buf, li):
    rows = z_br.shape[0]
    t = ROW_TILE
    n_steps = rows // t
    cur, prev, nxt, full, layer = _tile_specs(t, rows // HALO, li)

    def body(zc_ref, zp_ref, zn_ref, dc_ref, dn_ref, sw_ref, _, dz_ref, dw_ref):
        i = pl.program_id(0)
        zp = jnp.where(i == 0, jnp.zeros(zp_ref.shape, zp_ref.dtype), zp_ref[...])
        zn = jnp.where(i == n_steps - 1, jnp.zeros(zn_ref.shape, zn_ref.dtype), zn_ref[...])
        dun = jnp.where(i == n_steps - 1, jnp.zeros(dn_ref.shape, dn_ref.dtype), dn_ref[...])

        def ext(lo):
            return jnp.concatenate([zp[:, lo:lo + 256], zc_ref[:, lo:lo + 256], zn[:, lo:lo + 256]], axis=0).astype(F32)

        mid = slice(HALO, HALO + t)
        bg, c2, xv, sg = ext(BG), ext(C2), ext(XV), ext(SG)
        du = jnp.concatenate([jnp.zeros((HALO, 256), F32), dc_ref[...], dun], axis=0)
        e = c2 * xv
        shifted = [_sh(e, SC_K - 1 - k) for k in range(SC_K)]
        f = sum(sw_ref[k:k + 1, :] * shifted[k] for k in range(SC_K))
        gate = _silu(sg)
        df = du * gate * bg
        de = sum(sw_ref[k:k + 1, :] * _ash(df, SC_K - 1 - k) for k in range(SC_K))
        dbg = du * gate * f
        dsg = du * bg * f * _silu_grad(sg)
        dz_ref[...] = jnp.concatenate([dbg[mid], (de * xv)[mid], (de * c2)[mid], dsg[mid]], axis=1).astype(BF16)
        dw = jnp.concatenate([jnp.sum((df * shifted[k])[mid], axis=0, keepdims=True) for k in range(SC_K)] + [
            jnp.zeros((8 - SC_K, 256), F32)], axis=0)
        _accumulate(i, dw_ref, dw)

    return pl.pallas_call(
        body, name="shortconv_bwd", grid=(n_steps,), in_specs=[cur(ZB), prev(ZB), nxt(ZB), cur(256), nxt(256), layer((8, 256)), ANY],
        out_specs=[cur(1024, BG // 1024), full((8, 256))],
        out_shape=[jax.ShapeDtypeStruct((rows, ZB), BF16), jax.ShapeDtypeStruct((8, 256), F32)],
        input_output_aliases={6: 0}, compiler_params=_params(("arbitrary",)),
    )(z_br, z_br, z_br, dud, dud, sc_w, dz_buf)


def conformer_bwd_tail(z_br, duc, conf_w, conf_vec, dz_buf, li):
    rows = z_br.shape[0]
    t = ROW_TILE
    cur, prev, _, full, layer = _tile_specs(t, rows // HALO, li)

    def body(zc_ref, zp_ref, du_ref, cw_ref, cv_ref, _, dc_ref, dcg_ref, dv_ref):
        i = pl.program_id(0)
        zp = jnp.where(i == 0, jnp.zeros(zp_ref.shape, zp_ref.dtype), zp_ref[...])

        def ext(lo):
            return jnp.concatenate([zp[:, lo:lo + 256], zc_ref[:, lo:lo + 256]], axis=0).astype(F32)

        g1 = ext(CA) * _sigmoid(ext(CGT))
        c = _conf_conv(g1, cw_ref)[HALO:] + cv_ref[0:1, :]
        _, vjp = jax.vjp(_conf_tail, c, zc_ref[:, CG:CG + 256].astype(F32), cv_ref[1:2, :], cv_ref[2:3, :])
        dc, dcg, dlg, dlb = vjp(du_ref[...])
        dc_ref[...] = dc
        dcg_ref[...] = dcg.astype(BF16)
        dvec = jnp.concatenate([dlg, dlb, jnp.sum(dc, axis=0, keepdims=True), jnp.zeros((5, 256), F32)], axis=0)
        _accumulate(i, dv_ref, dvec)

    return pl.pallas_call(
        body, name="conformer_bwd_tail", grid=(rows // t,), in_specs=[cur(ZB), prev(ZB), cur(256), layer((32, 256)), layer((8, 256)), ANY],
        out_specs=[cur(256), cur(256, CG // 256), full((8, 256))],
        out_shape=[jax.ShapeDtypeStruct((rows, 256), F32), jax.ShapeDtypeStruct((rows, ZB), BF16), jax.ShapeDtypeStruct((8, 256), F32)],
        input_output_aliases={5: 1}, compiler_params=_params(("arbitrary",)),
    )(z_br, z_br, duc, conf_w, conf_vec, dz_buf)


def conformer_bwd_conv(z_br, dc, conf_w, dz_buf, li):
    rows = z_br.shape[0]
    t = ROW_TILE
    n_steps = rows // t
    cur, prev, nxt, full, layer = _tile_specs(t, rows // HALO, li)

    def body(zc_ref, zp_ref, dc_ref, dn_ref, cw_ref, _, dz_ref, dw_ref):
        i = pl.program_id(0)
        zp = jnp.where(i == 0, jnp.zeros(zp_ref.shape, zp_ref.dtype), zp_ref[...])
        dcn = jnp.where(i == n_steps - 1, jnp.zeros(dn_ref.shape, dn_ref.dtype), dn_ref[...])

        def ext(lo):
            return jnp.concatenate([zp[:, lo:lo + 256], zc_ref[:, lo:lo + 256]], axis=0).astype(F32)

        a, gt = ext(CA), ext(CGT)
        sg = _sigmoid(gt)
        g1 = a * sg
        dc = dc_ref[...]
        dce = jnp.concatenate([dc, dcn], axis=0)
        dg1 = jnp.zeros_like(dce)
        dws = []
        for k in range(CONF_K):
            dg1 = dg1 + cw_ref[k:k + 1, :] * _ash(dce, CONF_K - 1 - k)
            dws.append(jnp.sum(dc * _sh(g1, CONF_K - 1 - k)[HALO:], axis=0, keepdims=True))
        dg1 = dg1[:t]
        ac, sc = a[HALO:], sg[HALO:]
        dz_ref[...] = jnp.concatenate([dg1 * sc, dg1 * ac * sc * (1.0 - sc)], axis=1).astype(BF16)
        _accumulate(i, dw_ref, jnp.concatenate(dws + [jnp.zeros((32 - CONF_K, 256), F32)], axis=0))

    return pl.pallas_call(
        body, name="conformer_bwd_conv", grid=(n_steps,), in_specs=[cur(ZB), prev(ZB), cur(256), nxt(256), layer((32, 256)), ANY],
        out_specs=[cur(512, CA // 512), full((32, 256))],
        out_shape=[jax.ShapeDtypeStruct((rows, ZB), BF16), jax.ShapeDtypeStruct((32, 256), F32)],
        input_output_aliases={5: 0}, compiler_params=_params(("arbitrary",)),
    )(z_br, z_br, dc, dc, conf_w, dz_buf)


def attention_bwd_prep(dub, o_att, z_br, dz_buf):
    rows = dub.shape[0]
    t = ROW_TILE
    cur, _, _, _, _ = _tile_specs(t, rows // HALO)

    def body(du_ref, o_ref, mg_ref, _, do_ref, dmg_ref, delta_ref):
        du, o, mg = du_ref[...], o_ref[...].astype(F32), mg_ref[...].astype(F32)
        do = du * _silu(mg)
        do_ref[...] = do.astype(BF16)
        dmg_ref[...] = (du * o * _silu_grad(mg)).astype(BF16)
        prod = do * o
        lane = lax.broadcasted_iota(jnp.int32, (1, HEADS * V_DIM), 1)
        for h in range(HEADS):
            part = jnp.where((lane >= V_DIM * h) & (lane < V_DIM * (h + 1)), prod, 0.0)
            delta_ref[h] = jnp.broadcast_to(jnp.sum(part, axis=-1, keepdims=True), (t, LANES))

    return pl.pallas_call(
        body, name="attention_bwd_prep", grid=(rows // t,), in_specs=[cur(512), cur(512), cur(512, MG // 512), ANY],
        out_specs=[cur(512), cur(512, MG // 512), pl.BlockSpec((HEADS, t, LANES), lambda i: (0, i, 0))],
        out_shape=[jax.ShapeDtypeStruct((rows, 512), BF16), jax.ShapeDtypeStruct((rows, ZB), BF16),
                   jax.ShapeDtypeStruct((HEADS, rows, LANES), F32)],
        input_output_aliases={3: 1}, compiler_params=_params(("parallel",)),
    )(dub, o_att, z_br, dz_buf)


def attention_bwd(q, k, v, do, lse, delta):
    rows = q.shape[0]
    tq = ROW_TILE
    nq = rows // tq

    def body(q_ref, k_ref, v_ref, do_ref, lse_ref, dl_ref, dq_ref, dk_ref, dv_ref):
        j = pl.program_id(1)

        @pl.when(j == 0)
        def _():
            dq_ref[...] = jnp.zeros_like(dq_ref)


        def head_step(h, tile, n_tiles, dk, dv, diagonal):
            lanes = slice(HEAD_PAD * h, HEAD_PAD * (h + 1))
            hm = _head_lane_mask(h)
            kh = k_ref[:, lanes]
            vh = jnp.where(hm, v_ref[...], jnp.zeros((), BF16))
            r0, width = pl.multiple_of(tile * tq, tq), n_tiles * tq
            qi = q_ref[pl.ds(r0, width), lanes]
            doi = jnp.where(hm, do_ref[pl.ds(r0, width), :], jnp.zeros((), BF16))
            s = _dot_nt(qi, kh)
            if diagonal:
                s = jnp.where(lax.broadcasted_iota(jnp.int32, (tq, tq), 1) <= lax.broadcasted_iota(jnp.int32, (tq, tq), 0), s, -1e30)
            pr = jnp.exp(s - lse_ref[h, pl.ds(r0, width), :][:, 0:1])
            dv = dv + _dot_tn(pr.astype(BF16), doi)
            dp = _dot_nt(doi, vh)
            ds = (pr * (dp - dl_ref[h, pl.ds(r0, width), :][:, 0:1])).astype(BF16)
            dq_ref[pl.ds(r0, width), lanes] += _dot(ds, kh)
            return dk + _dot_tn(ds, qi), dv

        def step(tile, n_tiles, carry, diagonal):
            dk0, dk1, dv = carry
            dk0, dv = head_step(0, tile, n_tiles, dk0, dv, diagonal)
            dk1, dv = head_step(1, tile, n_tiles, dk1, dv, diagonal)
            return dk0, dk1, dv

        zero = jnp.zeros((tq, HEAD_PAD), F32)
        carry = step(j, 1, (zero, zero, jnp.zeros((tq, 2 * V_DIM), F32)), True)
        odd = (nq - 1 - j) % 2
        carry = lax.cond(odd == 1, lambda cr: step(j + 1, 1, cr, False), lambda cr: cr, carry)
        dk0, dk1, dv = lax.fori_loop(0, (nq - 1 - j) // 2, lambda t, cr: step(j + 1 + odd + 2 * t, 2, cr, False), carry)
        dk_ref[:, 0:HEAD_PAD] = dk0
        dk_ref[:, HEAD_PAD:2 * HEAD_PAD] = dk1
        dv_ref[...] = dv

    return pl.pallas_call(
        body, name="attention_bwd", grid=(HEADS // 2, nq),
        in_specs=[pl.BlockSpec((rows, 2 * HEAD_PAD), lambda p, j: (0, p)), pl.BlockSpec((tq, 2 * HEAD_PAD), lambda p, j: (j, p)),
                  pl.BlockSpec((tq, 2 * V_DIM), lambda p, j: (j, p)), pl.BlockSpec((rows, 2 * V_DIM), lambda p, j: (0, p)),
                  pl.BlockSpec((2, rows, LANES), lambda p, j: (p, 0, 0)), pl.BlockSpec((2, rows, LANES), lambda p, j: (p, 0, 0))],
        out_specs=[pl.BlockSpec((rows, 2 * HEAD_PAD), lambda p, j: (0, p)), pl.BlockSpec((tq, 2 * HEAD_PAD), lambda p, j: (j, p)),
                   pl.BlockSpec((tq, 2 * V_DIM), lambda p, j: (j, p))],
        out_shape=[jax.ShapeDtypeStruct((rows, HEADS * HEAD_PAD), F32), jax.ShapeDtypeStruct((rows, HEADS * HEAD_PAD), F32),
                   jax.ShapeDtypeStruct((rows, HEADS * V_DIM), F32)],
        compiler_params=_params(("parallel", "arbitrary")),
    )(q, k, v, do, lse, delta)


def mla_prep_bwd(dq, dk, dv, z_br, rope, gq, wuq, gkv, wukv, dz_buf, li):
    rows = dq.shape[0]
    t = ROW_TILE
    cur, _, _, full, layer = _tile_specs(t, rows // HALO, li)
    w8 = HEADS * HEAD_PAD

    def body(dq_ref, dk_ref, dv_ref, z_ref, rope_ref, gq_ref, wuq_ref, gkv_ref, wukv_ref, _, dz_ref, dwuq_ref, dwukv_ref, dgq_ref, dgkv_ref):
        i = pl.program_id(0)
        cth, s1, s2 = rope_ref[:, 0:128], rope_ref[:, 128:256], rope_ref[:, 256:384]
        dqb = _rope_transposed(dq_ref[...] * Q_SCALE, _lanes8(cth), _lanes8(s1), _lanes8(s2), w8).astype(BF16)
        cq = z_ref[:, 0:256].astype(F32)
        qn, vjp_q = jax.vjp(_rms, cq, gq_ref[...])
        _accumulate(i, dwuq_ref, _dot_tn(qn.astype(BF16), dqb))
        dcq, dgq = vjp_q(_dot_nt(dqb, wuq_ref[...]))
        _accumulate(i, dgq_ref, dgq)

        dk = dk_ref[...]
        dkr = sum(dk[:, HEAD_PAD * h:HEAD_PAD * (h + 1)] for h in range(HEADS))
        dkr = _rope_transposed(dkr, cth, s1, s2, HEAD_PAD)
        lane = lax.broadcasted_iota(jnp.int32, (1, HEAD_PAD), 1)
        dkr = jnp.where((lane >= QK_NOPE) & (lane < QK_NOPE + QK_ROPE), dkr, 0.0)
        dkvb = jnp.concatenate([dk, dv_ref[...]], axis=1).astype(BF16)
        ckv = z_ref[:, 256:384].astype(F32)
        kvn, vjp_kv = jax.vjp(_rms, ckv, gkv_ref[...])
        _accumulate(i, dwukv_ref, _dot_tn(kvn.astype(BF16), dkvb))
        dckv, dgkv = vjp_kv(_dot_nt(dkvb, wukv_ref[...]))
        _accumulate(i, dgkv_ref, dgkv)
        dz_ref[...] = jnp.concatenate([dcq, dckv, dkr], axis=1).astype(BF16)

    return pl.pallas_call(
        body, name="mla_prep_bwd", grid=(rows // t,),
        in_specs=[cur(w8), cur(w8), cur(512), cur(512, CQ // 512), cur(384), layer((1, 256)), layer((256, w8)), layer((1, 128)),
                  layer((128, w8 + 512)), ANY],
        out_specs=[cur(512, CQ // 512), full((256, w8)), full((128, w8 + 512)), full((1, 256)), full((1, 128))],
        out_shape=[jax.ShapeDtypeStruct((rows, ZB), BF16), jax.ShapeDtypeStruct((256, w8), F32), jax.ShapeDtypeStruct((128, w8 + 512), F32),
                   jax.ShapeDtypeStruct((1, 256), F32), jax.ShapeDtypeStruct((1, 128), F32)],
        input_output_aliases={9: 0}, compiler_params=_params(("arbitrary",)),
    )(dq, dk, dv, z_br, rope, gq, wuq, gkv, wukv, dz_buf)


def prenorm_bwd(dz_br, w_br, dh_gl, hres, gpre, dh_next, li):
    rows, d = hres.shape
    t = ROW_TILE
    cur, _, _, full, layer = _tile_specs(t, rows // HALO, li)

    def body(dz_ref, w_ref, dp_ref, x_ref, g_ref, dn_ref, dx_ref, dg_ref):
        i = pl.program_id(0)
        dh = _dot_nt(dz_ref[...], w_ref[...]) + dp_ref[...]
        _, vjp = jax.vjp(_rms, x_ref[...], g_ref[...])
        dx, dg = vjp(dh)
        dx_ref[...] = dx + dn_ref[...]
        _accumulate(i, dg_ref, dg)

    return pl.pallas_call(
        body, name="prenorm_bwd", grid=(rows // t,), in_specs=[cur(ZB), layer((d, ZB)), cur(d), cur(d), layer((1, d)), cur(d)],
        out_specs=[cur(d), full((1, d))], out_shape=[jax.ShapeDtypeStruct((rows, d), F32), jax.ShapeDtypeStruct((1, d), F32)],
        compiler_params=_params(("arbitrary",)),
    )(dz_br, w_br, dh_gl, hres, gpre, dh_next)


def _mesh_position():
    return lax.axis_index("x"), lax.axis_index("y"), lax.axis_index("c")


def chip_exchange(src, gather, name):
    block = src.shape if gather else src.shape[1:]

    def body(src_ref, dst_ref, send_sems, recv_sems, local_sem):
        x, y, c = _mesh_position()
        me = 2 * x + y
        peers = ((1 - x, y), (x, 1 - y), (1 - x, 1 - y))

        def part(k):
            return src_ref if gather else src_ref.at[k]

        def copy(j, slot):
            px, py = peers[j]
            return pltpu.make_async_remote_copy(src_ref=part(2 * px + py), dst_ref=dst_ref.at[slot], send_sem=send_sems.at[j],
                                                recv_sem=recv_sems.at[j], device_id=(px, py, c), device_id_type=MESH)

        local = pltpu.make_async_copy(part(me), dst_ref.at[me], local_sem)
        local.start()
        sends = [copy(j, me) for j in range(3)]
        for cp in sends:
            cp.start()
        for j, (px, py) in enumerate(peers):
            copy(j, 2 * px + py).wait_recv()
        for cp in sends:
            cp.wait_send()
        local.wait()

    return pl.pallas_call(
        body, name=name, in_specs=[pl.BlockSpec(memory_space=pl.ANY)], out_specs=pl.BlockSpec(memory_space=pl.ANY),
        out_shape=jax.ShapeDtypeStruct((N_CHIPS,) + tuple(block), src.dtype),
        scratch_shapes=[pltpu.SemaphoreType.DMA((3,)), pltpu.SemaphoreType.DMA((3,)), pltpu.SemaphoreType.DMA(())],
    )(src)


def sibling_swap(src, name):
    def body(src_ref, dst_ref, send_sem, recv_sem):
        x, y, c = _mesh_position()
        cp = pltpu.make_async_remote_copy(src_ref=src_ref, dst_ref=dst_ref, send_sem=send_sem, recv_sem=recv_sem,
                                          device_id=(x, y, 1 - c), device_id_type=MESH)
        cp.start()
        cp.wait()

    return pl.pallas_call(
        body, name=name, in_specs=[pl.BlockSpec(memory_space=pl.ANY)], out_specs=pl.BlockSpec(memory_space=pl.ANY),
        out_shape=jax.ShapeDtypeStruct(src.shape, src.dtype),
        scratch_shapes=[pltpu.SemaphoreType.DMA(()), pltpu.SemaphoreType.DMA(())],
    )(src)


def _comm_call(body, name, n_in, out_shapes, n_sems):
    return pl.pallas_call(
        body, name=name, in_specs=[ANY] * n_in, out_specs=[ANY] * len(out_shapes), out_shape=out_shapes,
        scratch_shapes=[pltpu.SemaphoreType.DMA((n,)) for n in n_sems])


def _layer_halves(c):
    return pl.ds((DEPTH // 2) * c, DEPTH // 2), pl.ds((DEPTH // 2) * (1 - c), DEPTH // 2)


def gather_layers(srcs, name):
    n = len(srcs)

    def body(*refs):
        src, dst = refs[:n], refs[n:2 * n]
        ici_send, ici_recv, d2d_send, d2d_recv = refs[2 * n:]
        x, y, c = _mesh_position()
        me = 2 * x + y
        peers = ((1 - x, y), (x, 1 - y), (1 - x, 1 - y))
        mine, other = _layer_halves(c)

        def fetch(a, j, slot):
            px, py = peers[j]
            return pltpu.make_async_remote_copy(src_ref=src[a].at[mine], dst_ref=dst[a].at[mine, slot], send_sem=ici_send.at[3 * a + j],
                                                recv_sem=ici_recv.at[3 * a + j], device_id=(px, py, c), device_id_type=MESH)

        def forward(a, j, half):
            px, py = peers[j]
            part = dst[a].at[half, 2 * px + py]
            return pltpu.make_async_remote_copy(src_ref=part, dst_ref=part, send_sem=d2d_send.at[3 * a + j],
                                                recv_sem=d2d_recv.at[3 * a + j], device_id=(x, y, 1 - c), device_id_type=MESH)

        sends = [fetch(a, j, me) for a in range(n) for j in range(3)]
        for cp in sends:
            cp.start()
        passed = []
        for j, (px, py) in enumerate(peers):
            for a in range(n):
                fetch(a, j, 2 * px + py).wait_recv()
                passed.append(forward(a, j, mine))
                passed[-1].start()
        for j in range(3):
            for a in range(n):
                forward(a, j, other).wait_recv()
        for cp in sends + passed:
            cp.wait_send()

    outs = [jax.ShapeDtypeStruct((DEPTH, N_CHIPS) + s.shape[1:], s.dtype) for s in srcs]
    gathered = _comm_call(body, name, n, outs, (3 * n, 3 * n, 3 * n, 3 * n))(*srcs)
    me = 2 * lax.axis_index("x") + lax.axis_index("y")
    return [lax.dynamic_update_slice(g, s[:, None], (0, me, 0, 0)) for g, s in zip(gathered, srcs)]


def swap_layer_halves(ps, name):
    n = len(ps)

    def body(*refs):
        src, dst = refs[:n], refs[n:2 * n]
        send_sems, recv_sems = refs[2 * n:]
        x, y, c = _mesh_position()
        _, other = _layer_halves(c)
        copies = [pltpu.make_async_remote_copy(src_ref=src[a].at[other], dst_ref=dst[a], send_sem=send_sems.at[a], recv_sem=recv_sems.at[a],
                                               device_id=(x, y, 1 - c), device_id_type=MESH) for a in range(n)]
        for cp in copies:
            cp.start()
        for cp in copies:
            cp.wait()

    outs = [jax.ShapeDtypeStruct((DEPTH // 2,) + p.shape[1:], p.dtype) for p in ps]
    return _comm_call(body, name, n, outs, (n, n))(*ps)


def exchange_chip_sums(ss, name):
    n = len(ss)

    def body(*refs):
        src, dst = refs[:n], refs[n:2 * n]
        send_sems, recv_sems, local_sems = refs[2 * n:]
        x, y, c = _mesh_position()
        me = 2 * x + y
        peers = ((1 - x, y), (x, 1 - y), (1 - x, 1 - y))

        def copy(a, j, slot):
            px, py = peers[j]
            return pltpu.make_async_remote_copy(src_ref=src[a].at[:, 2 * px + py], dst_ref=dst[a].at[:, slot], send_sem=send_sems.at[3 * a + j],
                                                recv_sem=recv_sems.at[3 * a + j], device_id=(px, py, c), device_id_type=MESH)

        local = [pltpu.make_async_copy(src[a].at[:, me], dst[a].at[:, me], local_sems.at[a]) for a in range(n)]
        sends = [copy(a, j, me) for a in range(n) for j in range(3)]
        for cp in local + sends:
            cp.start()
        for j, (px, py) in enumerate(peers):
            for a in range(n):
                copy(a, j, 2 * px + py).wait_recv()
        for cp in sends:
            cp.wait_send()
        for cp in local:
            cp.wait()

    outs = [jax.ShapeDtypeStruct(s.shape, s.dtype) for s in ss]
    return _comm_call(body, name, n, outs, (3 * n, 3 * n, n))(*ss)


def share_layer_halves(gs, name):
    n = len(gs)

    def body(*refs):
        dst = refs[n:2 * n]
        send_sems, recv_sems = refs[2 * n:]
        x, y, c = _mesh_position()
        mine, other = _layer_halves(c)

        def copy(a, half):
            return pltpu.make_async_remote_copy(src_ref=dst[a].at[half], dst_ref=dst[a].at[half], send_sem=send_sems.at[a],
                                                recv_sem=recv_sems.at[a], device_id=(x, y, 1 - c), device_id_type=MESH)

        sends = [copy(a, mine) for a in range(n)]
        for cp in sends:
            cp.start()
        for a in range(n):
            copy(a, other).wait_recv()
        for cp in sends:
            cp.wait_send()

    return pl.pallas_call(
        body, name=name, in_specs=[ANY] * n, out_specs=[ANY] * n, out_shape=[jax.ShapeDtypeStruct(g.shape, g.dtype) for g in gs],
        input_output_aliases={a: a for a in range(n)}, scratch_shapes=[pltpu.SemaphoreType.DMA((n,)), pltpu.SemaphoreType.DMA((n,))],
    )(*gs)


def _row_block(rows, cols, itemsize):
    best = 16
    for rb in range(16, rows + 1, 16):
        if rows % rb == 0 and rb * cols * itemsize <= 2 * 1024 * 1024:
            best = rb
    assert rows % best == 0, (rows, cols)
    return best


def add_sibling_half(p, r, c, name):
    cols = p.shape[-1]
    p2, r2 = p.reshape(-1, cols), r.reshape(-1, cols)
    rows = r2.shape[0]
    rb = _row_block(rows, cols, 2)
    steps = rows // rb

    def body(c_ref, p_ref, r_ref, o_ref):
        o_ref[...] = (p_ref[...].astype(F32) + r_ref[...].astype(F32)).astype(BF16)

    out = pl.pallas_call(
        body, name=name, out_shape=jax.ShapeDtypeStruct((rows, cols), BF16),
        grid_spec=pltpu.PrefetchScalarGridSpec(
            num_scalar_prefetch=1, grid=(steps,), in_specs=[pl.BlockSpec((rb, cols), lambda i, c_ref: (c_ref[0] * steps + i, 0)),
                                                            pl.BlockSpec((rb, cols), lambda i, c_ref: (i, 0))],
            out_specs=pl.BlockSpec((rb, cols), lambda i, c_ref: (i, 0))),
        compiler_params=_params(("parallel",)),
    )(jnp.reshape(c, (1,)).astype(jnp.int32), p2, r2)
    return out.reshape(r.shape)


def sum_chip_slots(l, c, name):
    layers, n, rows, cols = l.shape
    rb = _row_block(rows, cols, 4)

    def body(c_ref, l_ref, o_ref):
        acc = l_ref[0, 0].astype(F32)
        for s in range(1, n):
            acc = acc + l_ref[0, s].astype(F32)
        o_ref[0] = acc

    return pl.pallas_call(
        body, name=name, out_shape=jax.ShapeDtypeStruct((DEPTH, rows, cols), F32),
        grid_spec=pltpu.PrefetchScalarGridSpec(
            num_scalar_prefetch=1, grid=(layers, rows // rb),
            in_specs=[pl.BlockSpec((1, n, rb, cols), lambda a, i, c_ref: (a, 0, i, 0))],
            out_specs=pl.BlockSpec((1, rb, cols), lambda a, i, c_ref: (c_ref[0] * layers + a, i, 0))),
        compiler_params=_params(("parallel", "parallel")),
    )(jnp.reshape(c, (1,)).astype(jnp.int32), l)


def _comm_block(rows):
    return 1024 if rows % 1024 == 0 else rows


def sum_slots(buf, name):
    n, r, c = buf.shape
    rb = _comm_block(r)

    def body(b_ref, o_ref):
        acc = b_ref[0].astype(F32)
        for s in range(1, n):
            acc = acc + b_ref[s].astype(F32)
        o_ref[...] = acc

    return pl.pallas_call(
        body, name=name, grid=(r // rb,), in_specs=[pl.BlockSpec((n, rb, c), lambda i: (0, i, 0))],
        out_specs=pl.BlockSpec((rb, c), lambda i: (i, 0)), out_shape=jax.ShapeDtypeStruct((r, c), F32),
        compiler_params=_params(("parallel",)),
    )(buf)


def add_pair(a, b, out_dtype, name):
    shape = a.shape
    a2, b2 = a.reshape(-1, shape[-1]), b.reshape(-1, shape[-1])
    r, c = a2.shape
    rb = _comm_block(r)

    def body(a_ref, b_ref, o_ref):
        o_ref[...] = (a_ref[...].astype(F32) + b_ref[...].astype(F32)).astype(out_dtype)

    out = pl.pallas_call(
        body, name=name, grid=(r // rb,), in_specs=[pl.BlockSpec((rb, c), lambda i: (i, 0))] * 2,
        out_specs=pl.BlockSpec((rb, c), lambda i: (i, 0)), out_shape=jax.ShapeDtypeStruct((r, c), out_dtype),
        compiler_params=_params(("parallel",)),
    )(a2, b2)
    return out.reshape(shape)


def adamw(w, g, m, v):
    shape = w.shape
    cols = shape[-1]
    rows = math.prod(shape[:-1])
    rb = rows if rows * cols <= 256 * 1024 else 256
    assert rows % rb == 0, shape

    def body(w_ref, g_ref, m_ref, v_ref, d_ref, nm_ref, nv_ref):
        g_ = g_ref[...]
        nm = ADAM_B1 * m_ref[...] + (1.0 - ADAM_B1) * g_
        nv = ADAM_B2 * v_ref[...] + (1.0 - ADAM_B2) * (g_ * g_)
        m_hat = nm / (1.0 - ADAM_B1 ** ADAM_STEP)
        v_hat = nv / (1.0 - ADAM_B2 ** ADAM_STEP)
        d_ref[...] = -ADAM_LR * (m_hat / (jnp.sqrt(v_hat) + ADAM_EPS) + ADAM_WD * w_ref[...])
        nm_ref[...] = nm
        nv_ref[...] = nv

    spec = pl.BlockSpec((rb, cols), lambda i: (i, 0))
    outs = pl.pallas_call(
        body, name="adamw", grid=(rows // rb,), in_specs=[spec] * 4, out_specs=[spec] * 3,
        out_shape=[jax.ShapeDtypeStruct((rows, cols), F32)] * 3, compiler_params=_params(("parallel",)),
    )(*(a.reshape(rows, cols) for a in (w, g, m, v)))
    return tuple(o.reshape(shape) for o in outs)


def _pack(arrays, dtype, row_multiple):
    flat = jnp.concatenate([a.astype(dtype).reshape(-1) for a in arrays])
    per = LANES * row_multiple
    total = -(-flat.shape[0] // per) * per
    return jnp.pad(flat, (0, total - flat.shape[0])).reshape(total // LANES, LANES)


def _unpack(buf, shapes):
    flat = buf.reshape(-1)
    out, off = [], 0
    for s in shapes:
        n = math.prod(s)
        out.append(flat[off:off + n].reshape(s))
        off += n
    return out


def _input_weights(blocks):
    c0, c1, c2, c3 = (blocks[..., k, :, :] for k in range(N_CHIPS))
    pad = lambda n: jnp.zeros(c0.shape[:-1] + (n,), blocks.dtype)
    w_br = jnp.concatenate([c1[..., 376:1400], c0[..., 0:896], pad(64), c0[..., 896:928], pad(32), c0[..., 928:], c1[..., 0:376]], axis=-1)
    return w_br, jnp.concatenate([c1[..., 1400:], c2, c3], axis=-1)


def _input_weights_inverse(dw_br, dw_gl):
    c0 = jnp.concatenate([dw_br[..., 1024:1920], dw_br[..., 1984:2016], dw_br[..., 2048:2952]], axis=-1)
    c1 = jnp.concatenate([dw_br[..., 2952:ZB], dw_br[..., 0:1024], dw_gl[..., 0:432]], axis=-1)
    return jnp.stack([c0, c1, dw_gl[..., 432:2264], dw_gl[..., 2264:]], axis=-3)


def _uq_layout(w):
    r = w.reshape(w.shape[:-1] + (HEADS, QK_NOPE + QK_ROPE))
    r = jnp.pad(r, [(0, 0)] * (r.ndim - 1) + [(0, HEAD_PAD - QK_NOPE - QK_ROPE)])
    return r.reshape(w.shape[:-1] + (HEADS * HEAD_PAD,))


def _uq_layout_inverse(dw):
    r = dw.reshape(dw.shape[:-1] + (HEADS, HEAD_PAD))[..., :QK_NOPE + QK_ROPE]
    return r.reshape(dw.shape[:-1] + (HEADS * (QK_NOPE + QK_ROPE),))


def _ukv_layout(w):
    r = w.reshape(w.shape[:-1] + (HEADS, QK_NOPE + V_DIM))
    kp = jnp.pad(r[..., :QK_NOPE], [(0, 0)] * (r.ndim - 1) + [(0, HEAD_PAD - QK_NOPE)]).reshape(w.shape[:-1] + (HEADS * HEAD_PAD,))
    return jnp.concatenate([kp, r[..., QK_NOPE:].reshape(w.shape[:-1] + (HEADS * V_DIM,))], axis=-1)


def _ukv_layout_inverse(dw):
    lead = dw.shape[:-1]
    dk = dw[..., :HEADS * HEAD_PAD].reshape(lead + (HEADS, HEAD_PAD))[..., :QK_NOPE]
    dv = dw[..., HEADS * HEAD_PAD:].reshape(lead + (HEADS, V_DIM))
    return jnp.concatenate([dk, dv], axis=-1).reshape(lead + (HEADS * (QK_NOPE + V_DIM),))


def _block_diag(pw):
    zeros = lambda n: jnp.zeros(pw.shape[:-3] + (64, n), pw.dtype)
    rows = [jnp.concatenate([zeros(64 * g), pw[..., g, :, :], zeros(64 * (3 - g))], axis=-1) for g in range(4)]
    return jnp.concatenate(rows, axis=-2)


def _block_diag_inverse(d):
    return jnp.stack([d[..., 64 * g:64 * (g + 1), 64 * g:64 * (g + 1)] for g in range(4)], axis=-3)


def _pad_rows(a, n):
    return jnp.pad(a, ((0, n - a.shape[0]), (0, 0)))


def _rope_tables(rows):
    inv = 1.0 / (ROPE_THETA ** (jnp.arange(0, QK_ROPE, 2, dtype=F32) / QK_ROPE))
    ang = jnp.arange(rows, dtype=F32)[:, None] * inv[None, :]
    cos, sin = jnp.cos(ang), jnp.sin(ang)
    one, zero = jnp.ones((rows, 1), F32), jnp.zeros((rows, 1), F32)
    rep = lambda a, n: jnp.broadcast_to(a, (rows, n))
    c = jnp.concatenate([rep(one, 64), cos, cos, rep(one, 32)], axis=1)
    s1 = jnp.concatenate([rep(zero, 64), -sin, rep(zero, 48)], axis=1)
    s2 = jnp.concatenate([rep(zero, 80), sin, rep(zero, 32)], axis=1)
    return jnp.concatenate([c, s1, s2], axis=1)


def _misc_block(parts):
    out = []
    for name, rows in MISC:
        a = parts[name]
        if name == "w_o":
            a = a.reshape(a.shape[:-2] + (rows, 256))
        elif name == "w_uq":
            a = jnp.pad(a, [(0, 0)] * (a.ndim - 1) + [(0, 256 - a.shape[-1])])
        out.append(a)
    return jnp.concatenate(out, axis=-2)


def _misc_unblock(block):
    out, off = {}, 0
    for name, rows in MISC:
        a = block[..., off:off + rows, :]
        off += rows
        if name == "w_o":
            a = a.reshape(a.shape[:-2] + (256, D_MODEL))
        elif name == "w_uq":
            a = a[..., :192]
        out[name] = a
    return out


def _to_chip_blocks(name, a):
    if name == "w_o":
        return a.reshape(a.shape[:-2] + (N_CHIPS, a.shape[-2] // N_CHIPS, a.shape[-1]))
    return jnp.swapaxes(a.reshape(a.shape[:-1] + (N_CHIPS, a.shape[-1] // N_CHIPS)), -3, -2)


def _from_chip_blocks(name, b):
    if name == "w_o":
        return b.reshape(b.shape[:-3] + (N_CHIPS * b.shape[-2], b.shape[-1]))
    s = jnp.swapaxes(b, -3, -2)
    return s.reshape(s.shape[:-2] + (N_CHIPS * s.shape[-1],))


def gather_weights(shards):
    misc = _misc_block({n: shards[n] for n, _ in MISC}).astype(BF16)
    g_in, g_misc = gather_layers([shards["w_in"].astype(BF16), misc], "gather_weights")
    out = {"w_in": g_in}
    for name, blocks in _misc_unblock(g_misc).items():
        out[name] = _from_chip_blocks(name, blocks)
    small = chip_exchange(_pack([shards[n] for n, _, _ in SHARDED_SMALL], F32, 8), True, "gather_small_ici")
    per_chip = [_unpack(small[k], [s for _, s, _ in SHARDED_SMALL]) for k in range(N_CHIPS)]
    for idx, (name, _, axis) in enumerate(SHARDED_SMALL):
        out[name] = jnp.concatenate([per_chip[k][idx] for k in range(N_CHIPS)], axis=axis)
    return out


def reduce_large(grads, c):
    ps = [grads["w_in"].astype(BF16), _misc_block({n: _to_chip_blocks(n, grads[n]) for n, _ in MISC}).astype(BF16)]
    rs = swap_layer_halves(ps, "reduce_grads_d2d")
    ss = [add_sibling_half(p, r, c, "reduce_grads_pair_%d" % a) for a, (p, r) in enumerate(zip(ps, rs))]
    ls = exchange_chip_sums(ss, "reduce_grads_ici")
    gs = [sum_chip_slots(l, c, "reduce_grads_sum_%d" % a) for a, l in enumerate(ls)]
    g_in, g_misc = share_layer_halves(gs, "reduce_grads_share")
    out = {"w_in": g_in}
    out.update(_misc_unblock(g_misc))
    return out


def reduce_small(grads, chip):
    names = [n for n, _ in REPLICATED] + [n for n, _, _ in SHARDED_SMALL]
    buf = _pack([grads[n] for n in names], F32, 8)
    chip_sum = add_pair(buf, sibling_swap(buf, "reduce_small_d2d"), F32, "reduce_small_pair")
    total = sum_slots(chip_exchange(chip_sum, True, "reduce_small_ici"), "reduce_small_sum")
    out = dict(zip(names, _unpack(total, [grads[n].shape for n in names])))
    for name, shape, axis in SHARDED_SMALL:
        out[name] = lax.dynamic_slice_in_dim(out[name], chip * shape[axis], shape[axis], axis)
    return out


def _prepare_weights(w):
    w_br, w_gl = _input_weights(w["w_in"])
    row = lambda a: a[:, None, :]
    conf_vec = jnp.concatenate([row(w["conf_dw_b"]), row(w["conf_ln_g"]), row(w["conf_ln_b"]), jnp.zeros((DEPTH, 5, 256), F32)], axis=1)
    return dict(
        w_br=w_br.astype(BF16), w_gl=w_gl.astype(BF16), gpre=row(w["pre_norm_g"]), bias=row(w["gate_bias"]),
        pwbd=_block_diag(w["pool_w"]).astype(BF16), pscale=row(w["pool_scale"]), gq=row(w["q_norm_g"]),
        wuq=_uq_layout(w["w_uq"]).astype(BF16), gkv=row(w["kv_norm_g"]), wukv=_ukv_layout(w["w_ukv"]).astype(BF16),
        conf_w=jnp.pad(w["conf_dw_w"].astype(F32), ((0, 0), (0, 32 - CONF_K), (0, 0))), conf_vec=conf_vec,
        sc_w=jnp.pad(w["sc_dw_w"].astype(F32), ((0, 0), (0, 8 - SC_K), (0, 0))),
        woa=w["w_out_pool"].astype(BF16), wob=w["w_out_mla"].astype(BF16), woc=w["w_out_conf"].astype(BF16),
        wod=w["w_out_sc"].astype(BF16), wo=w["w_o"].astype(BF16), gpost=row(w["post_norm_g"]))


def local_step(x, target, w):
    seq = x.shape[0]
    length = N_META + seq
    rows = -(-length // ROW_TILE) * ROW_TILE
    bt = _big_tile(rows)
    hres = _pad_rows(jnp.concatenate([w["meta_tokens"].astype(F32), x], axis=0), rows)
    tgt = jnp.pad(target, ((N_META, rows - length), (0, 0)))
    rope = _rope_tables(rows)
    lw = _prepare_weights(w)

    saved = []
    for i in range(DEPTH):
        z_br, hb = prenorm_project(hres, lw["gpre"], lw["w_br"], i)
        z_gl = matmul(hb, lw["w_gl"], "nn", BF16, bt, 1024, D_MODEL, "project_gates", b_layer=i)
        ua, uc, ud, q, k, v = branches_fwd(z_br, rope, lw["pwbd"], lw["pscale"], lw["gq"], lw["wuq"], lw["gkv"], lw["wukv"],
                                           lw["conf_w"], lw["conf_vec"], lw["sc_w"], i)
        o_att, lse = attention_fwd(q, k, v)
        ub, mb, o, hnew = merge_fwd(ua, o_att, uc, ud, z_br, z_gl, lw["bias"], lw["woa"], lw["wob"], lw["woc"], lw["wod"], lw["wo"],
                                    lw["gpost"], hres, i)
        saved.append(dict(hres=hres, hb=hb, z_br=z_br, z_gl=z_gl, ua=ua, ub=ub, uc=uc, ud=ud, q=q, k=k, v=v, o_att=o_att, lse=lse,
                          mb=mb, o=o))
        hres = hnew

    dh, total = loss_head(hres, tgt, seq)

    names = ("gpre", "w_br", "w_gl", "bias", "pwbd", "pscale", "woa", "gq", "wuq", "gkv", "wukv", "wob", "conf_w", "conf_vec", "woc",
             "sc_w", "wod", "wo", "gpost")
    g = {n: [None] * DEPTH for n in names}
    for i in reversed(range(DEPTH)):
        s = saved[i]
        dm, g["wo"][i], g["gpost"][i] = postnorm_bwd(dh, s["o"], s["mb"], lw["wo"], lw["gpost"], i)
        dua, dub, duc, dud, dz_gl, g["woa"][i], g["wob"][i], g["woc"][i], g["wod"][i], g["bias"][i] = merge_bwd(
            dm, s["ua"], s["ub"], s["uc"], s["ud"], s["z_gl"], lw["bias"], lw["woa"], lw["wob"], lw["woc"], lw["wod"], i)
        dz_br = lax.empty((rows, ZB), BF16)
        dz_br, g["pwbd"][i], g["pscale"][i] = pool_bwd(s["z_br"], dua, lw["pwbd"], lw["pscale"], dz_br, i)
        dz_br, g["sc_w"][i] = shortconv_bwd(s["z_br"], dud, lw["sc_w"], dz_br, i)
        dc, dz_br, g["conf_vec"][i] = conformer_bwd_tail(s["z_br"], duc, lw["conf_w"], lw["conf_vec"], dz_br, i)
        dz_br, g["conf_w"][i] = conformer_bwd_conv(s["z_br"], dc, lw["conf_w"], dz_br, i)
        do, dz_br, delta = attention_bwd_prep(dub, s["o_att"], s["z_br"], dz_br)
        dq, dk, dv = attention_bwd(s["q"], s["k"], s["v"], do, s["lse"], delta)
        dz_br, g["wuq"][i], g["wukv"][i], g["gq"][i], g["gkv"][i] = mla_prep_bwd(dq, dk, dv, s["z_br"], rope, lw["gq"], lw["wuq"],
                                                                              lw["gkv"], lw["wukv"], dz_br, i)
        g["w_br"][i] = matmul(s["hb"], dz_br, "tn", F32, D_MODEL, ZB // 2, bt, "grad_w_branch")
        g["w_gl"][i] = matmul(s["hb"], dz_gl, "tn", F32, D_MODEL, 1024, bt, "grad_w_gates")
        dh_gl = matmul(dz_gl, lw["w_gl"], "nt", F32, bt, D_MODEL, 1024, "grad_h_gates", b_layer=i)
        dh, g["gpre"][i] = prenorm_bwd(dz_br, lw["w_br"], dh_gl, s["hres"], lw["gpre"], dh, i)

    dw_in = jnp.stack([_input_weights_inverse(br, gl).astype(BF16) for br, gl in zip(g.pop("w_br"), g.pop("w_gl"))])
    g = {n: jnp.stack(parts) for n, parts in g.items()}
    grads = dict(
        meta_tokens=dh[:N_META], pre_norm_g=g["gpre"][:, 0], w_in=dw_in, gate_bias=g["bias"][:, 0],
        pool_w=_block_diag_inverse(g["pwbd"]), pool_scale=g["pscale"][:, 0], w_out_pool=g["woa"], q_norm_g=g["gq"][:, 0],
        w_uq=_uq_layout_inverse(g["wuq"]), kv_norm_g=g["gkv"][:, 0], w_ukv=_ukv_layout_inverse(g["wukv"]), w_out_mla=g["wob"],
        conf_dw_w=g["conf_w"][:, :CONF_K], conf_dw_b=g["conf_vec"][:, 2], conf_ln_g=g["conf_vec"][:, 0], conf_ln_b=g["conf_vec"][:, 1],
        w_out_conf=g["woc"], sc_dw_w=g["sc_w"][:, :SC_K], w_out_sc=g["wod"], w_o=g["wo"], post_norm_g=g["gpost"][:, 0])
    return total[0, 0], dh[N_META:length], grads


def kernel(x, meta_tokens, pre_norm_g, w_in, gate_bias, pool_w, pool_scale, w_out_pool, q_norm_g, w_uq, kv_norm_g, w_ukv, w_out_mla, conf_dw_w, conf_dw_b, conf_ln_g, conf_ln_b, w_out_conf, sc_dw_w, w_out_sc, w_o, post_norm_g, loss_target, m_meta_tokens, m_pre_norm_g, m_w_in, m_gate_bias, m_pool_w, m_pool_scale, m_w_out_pool, m_q_norm_g, m_w_uq, m_kv_norm_g, m_w_ukv, m_w_out_mla, m_conf_dw_w, m_conf_dw_b, m_conf_ln_g, m_conf_ln_b, m_w_out_conf, m_sc_dw_w, m_w_out_sc, m_w_o, m_post_norm_g, v_meta_tokens, v_pre_norm_g, v_w_in, v_gate_bias, v_pool_w, v_pool_scale, v_w_out_pool, v_q_norm_g, v_w_uq, v_kv_norm_g, v_w_ukv, v_w_out_mla, v_conf_dw_w, v_conf_dw_b, v_conf_ln_g, v_conf_ln_b, v_w_out_conf, v_sc_dw_w, v_w_out_sc, v_w_o, v_post_norm_g):
    args = locals()
    weights = {n: args[n] for n in WEIGHT_ORDER}
    c = lax.axis_index("c")
    chip = 2 * lax.axis_index("x") + lax.axis_index("y")

    full = dict(weights)
    full.update(gather_weights(weights))
    total, dx, grads = local_step(x[0], loss_target[0], full)
    loss = lax.psum(total * (0.5 / D_MODEL), ("x", "y", "c"))

    reduced = reduce_large(grads, c)
    reduced.update(reduce_small(grads, chip))

    deltas, new_m, new_v = [], [], []
    for n in WEIGHT_ORDER:
        d, nm, nv = adamw(weights[n], reduced[n], args["m_" + n], args["v_" + n])
        deltas.append(d)
        new_m.append(nm)
        new_v.append(nv)
    return (loss, dx[None], *[reduced[n] for n in WEIGHT_ORDER], *deltas, *new_m, *new_v)
```

```python
import functools
import math

import jax
import jax.numpy as jnp
from jax import lax
from jax.experimental import pallas as pl
from jax.experimental.pallas import tpu as pltpu

F32 = jnp.float32
BF16 = jnp.bfloat16

D_MODEL = 1024
DEPTH = 4
N_META = 16
EPS = 1e-6
HEADS = 8
QK_NOPE = 64
QK_ROPE = 32
V_DIM = 64
HEAD_PAD = 128
ROPE_THETA = 10000.0
Q_SCALE = (QK_NOPE + QK_ROPE) ** -0.5
CONF_K = 31
SC_K = 3
IN_W = 7328
N_CHIPS = 4

ZB = 3328
ZG = 4096
BG, C2, XV, SG, PV, PG, CQ, CKV, KR, MG, CA, CGT, CG = (0, 256, 512, 768, 1024, 1280, 1536, 1792, 1920, 2048, 2560, 2816, 3072)

ROW_TILE = 384
HALO = 32
LANES = 128
VMEM_LIMIT = 56 * 1024 * 1024

ADAM_LR = 0.001
ADAM_B1 = 0.9
ADAM_B2 = 0.999
ADAM_EPS = 1e-08
ADAM_WD = 0.01
ADAM_STEP = 10

MESH = pl.DeviceIdType.MESH
ANY = pl.BlockSpec(memory_space=pl.ANY)

MISC = (
    ("w_out_pool", 256), ("w_ukv", 128), ("w_out_mla", 512), ("w_out_conf", 256), ("w_out_sc", 256), ("w_o", 1024), ("w_uq", 256))
SHARDED_SMALL = (
    ("meta_tokens", (N_META, 256), 1),
    ("conf_dw_w", (DEPTH, CONF_K, 64), 2),
    ("sc_dw_w", (DEPTH, SC_K, 64), 2),
)
REPLICATED = (
    ("pre_norm_g", (DEPTH, D_MODEL)),
    ("gate_bias", (DEPTH, 4 * D_MODEL)),
    ("pool_w", (DEPTH, 4, 64, 64)),
    ("pool_scale", (DEPTH, 256)),
    ("q_norm_g", (DEPTH, 256)),
    ("kv_norm_g", (DEPTH, 128)),
    ("conf_dw_b", (DEPTH, 256)),
    ("conf_ln_g", (DEPTH, 256)),
    ("conf_ln_b", (DEPTH, 256)),
    ("post_norm_g", (DEPTH, D_MODEL)),
)
WEIGHT_ORDER = ("meta_tokens", "pre_norm_g", "w_in", "gate_bias", "pool_w", "pool_scale", "w_out_pool", "q_norm_g", "w_uq",
                "kv_norm_g", "w_ukv", "w_out_mla", "conf_dw_w", "conf_dw_b", "conf_ln_g", "conf_ln_b", "w_out_conf", "sc_dw_w",
                "w_out_sc", "w_o", "post_norm_g")


def _dot(a, b):
    return lax.dot_general(a, b, (((1,), (0,)), ((), ())), preferred_element_type=F32)


def _dot_nt(a, b):
    return lax.dot_general(a, b, (((1,), (1,)), ((), ())), preferred_element_type=F32)


def _dot_tn(a, b):
    return lax.dot_general(a, b, (((0,), (0,)), ((), ())), preferred_element_type=F32)


def _sigmoid(x):
    return jax.nn.sigmoid(x)


def _silu(x):
    return x * _sigmoid(x)


def _silu_grad(x):
    s = _sigmoid(x)
    return s * (1.0 + x * (1.0 - s))


def _rms(x, g):
    return x * lax.rsqrt(jnp.mean(x * x, axis=-1, keepdims=True) + EPS) * g


def _sh(x, d):
    return x if d == 0 else pltpu.roll(x, d, 0)


def _ash(x, d):
    return x if d == 0 else pltpu.roll(x, x.shape[0] - d, 0)


def _lanes8(t):
    return jnp.concatenate([t] * HEADS, axis=1)


def _pool_window_sums(v, shift):
    a2 = v + shift(v, 1)
    a4 = a2 + shift(a2, 2)
    a8 = a4 + shift(a4, 4)
    a16 = a8 + shift(a8, 8)
    lane = lax.broadcasted_iota(jnp.int32, v.shape, 1)
    return jnp.where(lane < 64, a2, jnp.where(lane < 128, a4, jnp.where(lane < 192, a8, a16)))


def _pool_counts(first_row, rows):
    pos = first_row + lax.broadcasted_iota(jnp.int32, (rows, 256), 0)
    lane = lax.broadcasted_iota(jnp.int32, (rows, 256), 1)
    width = jnp.where(lane < 64, 2, jnp.where(lane < 128, 4, jnp.where(lane < 192, 8, 16)))
    return jnp.maximum(jnp.minimum(pos + 1, width), 1).astype(F32)


def _params(sem=None):
    return pltpu.CompilerParams(dimension_semantics=sem, vmem_limit_bytes=VMEM_LIMIT)


def _tile_specs(t, n_halo_blocks, li=0):
    per = t // HALO

    def layer(shape, idx=li):
        return pl.BlockSpec((None,) + tuple(shape), lambda i: (idx,) + (0,) * len(shape))

    def cur(c, cb=0):
        return pl.BlockSpec((t, c), lambda i: (i, cb))

    def prev(c, cb=0):
        return pl.BlockSpec((HALO, c), lambda i: (jnp.maximum(i * per - 1, 0), cb))

    def nxt(c, cb=0):
        return pl.BlockSpec((HALO, c), lambda i: (jnp.minimum((i + 1) * per, n_halo_blocks - 1), cb))

    def full(shape):
        return pl.BlockSpec(shape, lambda i: (0,) * len(shape))

    return cur, prev, nxt, full, layer


def _big_tile(rows):
    return rows // 3 if rows % (3 * LANES) == 0 else ROW_TILE


def matmul(a, b, mode, out_dtype, tm, tn, tk, name, b_layer=None):
    bs = b.shape if b_layer is None else b.shape[1:]
    lead = () if b_layer is None else (None,)
    pick = (lambda *ix: ix) if b_layer is None else (lambda *ix: (b_layer,) + ix)
    if mode == "nn":
        (m, k), n = a.shape, bs[1]
        a_spec = pl.BlockSpec((tm, tk), lambda i, j, kk: (i, kk))
        b_spec = pl.BlockSpec(lead + (tk, tn), lambda i, j, kk: pick(kk, j))
        dot = _dot
    elif mode == "nt":
        (m, k), n = a.shape, bs[0]
        a_spec = pl.BlockSpec((tm, tk), lambda i, j, kk: (i, kk))
        b_spec = pl.BlockSpec(lead + (tn, tk), lambda i, j, kk: pick(j, kk))
        dot = _dot_nt
    else:
        (k, m), n = a.shape, bs[1]
        a_spec = pl.BlockSpec((tk, tm), lambda i, j, kk: (kk, i))
        b_spec = pl.BlockSpec(lead + (tk, tn), lambda i, j, kk: pick(kk, j))
        dot = _dot_tn
    assert m % tm == 0 and n % tn == 0 and k % tk == 0, (a.shape, bs, tm, tn, tk)
    nk = k // tk

    def body(a_ref, b_ref, o_ref, acc_ref):
        kk = pl.program_id(2)

        @pl.when(kk == 0)
        def _():
            acc_ref[...] = jnp.zeros_like(acc_ref)

        acc_ref[...] += dot(a_ref[...], b_ref[...])

        @pl.when(kk == nk - 1)
        def _():
            o_ref[...] = acc_ref[...].astype(out_dtype)

    return pl.pallas_call(
        body, name=name, grid=(m // tm, n // tn, nk), in_specs=[a_spec, b_spec],
        out_specs=pl.BlockSpec((tm, tn), lambda i, j, kk: (i, j)), out_shape=jax.ShapeDtypeStruct((m, n), out_dtype),
        scratch_shapes=[pltpu.VMEM((tm, tn), F32)], compiler_params=_params(("parallel", "parallel", "arbitrary")),
    )(a, b)


def prenorm_project(hres, g, w, li):
    rows, d = hres.shape
    n = w.shape[2]
    tm, tn = _big_tile(rows), n // 2

    def body(x_ref, g_ref, w_ref, z_ref, hb_ref):
        @pl.when(pl.program_id(1) == 0)
        def _():
            hb_ref[...] = _rms(x_ref[...], g_ref[...]).astype(BF16)

        z_ref[...] = _dot(hb_ref[...], w_ref[...]).astype(BF16)

    return pl.pallas_call(
        body, name="prenorm_project", grid=(rows // tm, n // tn),
        in_specs=[pl.BlockSpec((tm, d), lambda i, j: (i, 0)), pl.BlockSpec((None, 1, d), lambda i, j: (li, 0, 0)),
                  pl.BlockSpec((None, d, tn), lambda i, j: (0, 0, j))],
        out_specs=[pl.BlockSpec((tm, tn), lambda i, j: (i, j)), pl.BlockSpec((tm, d), lambda i, j: (i, 0))],
        out_shape=[jax.ShapeDtypeStruct((rows, n), BF16), jax.ShapeDtypeStruct((rows, d), BF16)],
        compiler_params=_params(("parallel", "arbitrary")),
    )(hres, g, w)


def _rope(q, c, s1, s2, width):
    return q * c + pltpu.roll(q, width - 16, 1) * s1 + pltpu.roll(q, 16, 1) * s2


def _rope_transposed(dq, c, s1, s2, width):
    return dq * c + pltpu.roll(dq * s1, 16, 1) + pltpu.roll(dq * s2, width - 16, 1)


def _conf_conv(g1, w_ref):
    acc = jnp.zeros_like(g1)
    for k in range(CONF_K):
        acc = acc + w_ref[k:k + 1, :] * _sh(g1, CONF_K - 1 - k)
    return acc


def _conf_tail(c, cg, lg, lb):
    mu = jnp.mean(c, axis=-1, keepdims=True)
    xc = c - mu
    var = jnp.mean(xc * xc, axis=-1, keepdims=True)
    n = xc * lax.rsqrt(var + EPS) * lg + lb
    return _silu(n) * _silu(cg)


def branches_fwd(z_br, rope, pwbd, pscale, gq, wuq, gkv, wukv, conf_w, conf_vec, sc_w, li):
    rows = z_br.shape[0]
    t = ROW_TILE
    cur, prev, _, _, layer = _tile_specs(t, rows // HALO, li)

    def body(zc_ref, zp_ref, rope_ref, pw_ref, ps_ref, gq_ref, wuq_ref, gkv_ref, wukv_ref, cw_ref, cv_ref, sw_ref,
             ua_ref, uc_ref, ud_ref, q_ref, k_ref, v_ref):
        i = pl.program_id(0)
        zp = jnp.where(i == 0, jnp.zeros(zp_ref.shape, zp_ref.dtype), zp_ref[...])

        def ext(lo, w=256):
            return jnp.concatenate([zp[:, lo:lo + w], zc_ref[:, lo:lo + w]], axis=0).astype(F32)

        def col(lo, w=256):
            return zc_ref[:, lo:lo + w].astype(F32)

        v = ext(PV)
        p = (_pool_window_sums(v, _sh) / _pool_counts(i * t - HALO, t + HALO) - v)[HALO:]
        ya = _dot(p.astype(BF16), pw_ref[...]) * ps_ref[...]
        ua_ref[...] = (ya * _silu(col(PG))).astype(BF16)

        g1 = ext(CA) * _sigmoid(ext(CGT))
        c = _conf_conv(g1, cw_ref)[HALO:] + cv_ref[0:1, :]
        uc_ref[...] = _conf_tail(c, col(CG), cv_ref[1:2, :], cv_ref[2:3, :]).astype(BF16)

        e = ext(C2) * ext(XV)
        f = jnp.zeros_like(e)
        for k in range(SC_K):
            f = f + sw_ref[k:k + 1, :] * _sh(e, SC_K - 1 - k)
        ud_ref[...] = (col(BG) * f[HALO:] * _silu(col(SG))).astype(BF16)

        cth, s1, s2 = rope_ref[:, 0:128], rope_ref[:, 128:256], rope_ref[:, 256:384]
        qn = _rms(col(CQ), gq_ref[...]).astype(BF16)
        q = _dot(qn, wuq_ref[...])
        w8 = HEADS * HEAD_PAD
        q_ref[...] = (_rope(q, _lanes8(cth), _lanes8(s1), _lanes8(s2), w8) * Q_SCALE).astype(BF16)
        kvn = _rms(col(CKV, 128), gkv_ref[...]).astype(BF16)
        kv = _dot(kvn, wukv_ref[...])
        kr = _rope(col(KR, 128), cth, s1, s2, HEAD_PAD)
        k_ref[...] = (kv[:, :w8] + _lanes8(kr)).astype(BF16)
        v_ref[...] = kv[:, w8:].astype(BF16)

    outs = [jax.ShapeDtypeStruct((rows, 256), BF16)] * 3 + [jax.ShapeDtypeStruct((rows, 1024), BF16)] * 2 + [
        jax.ShapeDtypeStruct((rows, 512), BF16)]
    return pl.pallas_call(
        body, name="branches_fwd", grid=(rows // t,),
        in_specs=[cur(ZB), prev(ZB), cur(384), layer((256, 256)), layer((1, 256)), layer((1, 256)), layer((256, 1024), 0),
                  layer((1, 128)), layer((128, 1536), 0), layer((32, 256)), layer((8, 256)), layer((8, 256))],
        out_specs=[cur(256), cur(256), cur(256), cur(1024), cur(1024), cur(512)], out_shape=outs,
        compiler_params=_params(("parallel",)),
    )(z_br, z_br, rope, pwbd, pscale, gq, wuq, gkv, wukv, conf_w, conf_vec, sc_w)


def _head_lane_mask(h):
    lane = lax.broadcasted_iota(jnp.int32, (1, 2 * V_DIM), 1)
    return (lane >= V_DIM * h) & (lane < V_DIM * (h + 1))


def attention_fwd(q, k, v, gather=None):
    rows = q.shape[0]
    tq = ROW_TILE
    nq = rows // tq
    n = 0 if gather is None else len(gather[0])

    def body(*refs):
        if n:
            start, finish = _gather_ops(refs[3:3 + n], refs[5 + 2 * n:5 + 3 * n], refs[5 + 3 * n:], gather[2], True)
            pl.when((pl.program_id(0) == 0) & (pl.program_id(1) == 0))(start)
        compute(*refs[:3], *refs[3 + 2 * n:5 + 2 * n])
        if n:
            pl.when((pl.program_id(0) == HEADS // 2 - 1) & (pl.program_id(1) == nq - 1))(finish)

    def compute(q_ref, k_ref, v_ref, o_ref, lse_ref):
        i = pl.program_id(1)

        def head_step(h, tile, n_tiles, carry, masked):
            m, l, acc = carry
            width = n_tiles * tq
            r0 = pl.multiple_of(tile * tq, tq)
            kh = k_ref[pl.ds(r0, width), HEAD_PAD * h:HEAD_PAD * (h + 1)]
            vh = jnp.where(_head_lane_mask(h), v_ref[pl.ds(r0, width), :], jnp.zeros((), BF16))
            s = _dot_nt(q_ref[:, HEAD_PAD * h:HEAD_PAD * (h + 1)], kh)
            if masked:
                row = lax.broadcasted_iota(jnp.int32, (tq, width), 0)
                colm = lax.broadcasted_iota(jnp.int32, (tq, width), 1)
                s = jnp.where(colm <= row + (width - tq), s, -1e30)
            m2 = jnp.maximum(m, jnp.max(s, axis=-1, keepdims=True))
            alpha = jnp.exp(m - m2)
            pr = jnp.exp(s - m2)
            return m2, alpha * l + jnp.sum(pr, axis=-1, keepdims=True), alpha * acc + _dot(pr.astype(BF16), vh)

        def step(tile, n_tiles, carry, masked):
            return tuple(head_step(h, tile, n_tiles, carry[h], masked) for h in range(2))

        init = (jnp.full((tq, 1), -1e30, F32), jnp.zeros((tq, 1), F32), jnp.zeros((tq, 2 * V_DIM), F32))
        carry = lax.fori_loop(0, i // 2, lambda t, cr: step(2 * t, 2, cr, False), (init, init))
        carry = lax.cond(i % 2 == 1, lambda cr: step(i - 1, 2, cr, True), lambda cr: step(i, 1, cr, True), carry)
        out = jnp.zeros((tq, 2 * V_DIM), F32)
        for h, (m, l, acc) in enumerate(carry):
            out = out + acc / l
            lse_ref[h] = jnp.broadcast_to(m + jnp.log(l), (tq, LANES))
        o_ref[...] = out.astype(BF16)

    srcs, dsts = ([], []) if gather is None else (list(gather[0]), list(gather[1]))
    outs = pl.pallas_call(
        body, name="attention_fwd" if gather is None else "attention_fwd_gather", grid=(HEADS // 2, nq),
        in_specs=[pl.BlockSpec((tq, 2 * HEAD_PAD), lambda p, i: (i, p)), pl.BlockSpec((rows, 2 * HEAD_PAD), lambda p, i: (0, p)),
                  pl.BlockSpec((rows, 2 * V_DIM), lambda p, i: (0, p))] + [ANY] * (2 * n),
        out_specs=[pl.BlockSpec((tq, 2 * V_DIM), lambda p, i: (i, p)), pl.BlockSpec((2, tq, LANES), lambda p, i: (p, i, 0))] + [ANY] * n,
        out_shape=[jax.ShapeDtypeStruct((rows, HEADS * V_DIM), BF16), jax.ShapeDtypeStruct((HEADS, rows, LANES), F32)] + [
            jax.ShapeDtypeStruct(d.shape, d.dtype) for d in dsts],
        input_output_aliases={3 + n + a: 2 + a for a in range(n)}, scratch_shapes=GATHER_SEMS(n) if n else [],
        compiler_params=_params(("arbitrary", "arbitrary") if n else ("parallel", "parallel")),
    )(q, k, v, *srcs, *dsts)
    return outs[0], outs[1], list(outs[2:])


def merge_fwd(ua, o_att, uc, ud, z_br, z_gl, bias, woa, wob, woc, wod, wo, gpost, hres, li):
    rows = hres.shape[0]
    t = ROW_TILE
    cur, _, _, _, layer = _tile_specs(t, rows // HALO, li)
    d = D_MODEL

    def body(ua_ref, ob_ref, uc_ref, ud_ref, mg_ref, gl_ref, b_ref, woa_ref, wob_ref, woc_ref, wod_ref, wo_ref, gp_ref, h_ref,
             ub_ref, mb_ref, o_ref, hn_ref):
        ub = (ob_ref[...].astype(F32) * _silu(mg_ref[...].astype(F32))).astype(BF16)
        ub_ref[...] = ub
        m = jnp.zeros((t, d), F32)
        for idx, (u, w_ref) in enumerate(((ua_ref[...], woa_ref), (ub, wob_ref), (uc_ref[...], woc_ref), (ud_ref[...], wod_ref))):
            gate = _sigmoid(gl_ref[:, d * idx:d * (idx + 1)].astype(F32) + b_ref[:, d * idx:d * (idx + 1)])
            m = m + gate * _dot(u, w_ref[...])
        mb = m.astype(BF16)
        mb_ref[...] = mb
        o = _dot(mb, wo_ref[...])
        o_ref[...] = o
        hn_ref[...] = h_ref[...] + _rms(o, gp_ref[...])

    return pl.pallas_call(
        body, name="merge_fwd", grid=(rows // t,),
        in_specs=[cur(256), cur(512), cur(256), cur(256), cur(512, MG // 512), cur(ZG), layer((1, ZG)), layer((256, d), 0), layer((512, d), 0),
                  layer((256, d), 0), layer((256, d), 0), layer((d, d), 0), layer((1, d)), cur(d)],
        out_specs=[cur(512), cur(d), cur(d), cur(d)],
        out_shape=[jax.ShapeDtypeStruct((rows, 512), BF16), jax.ShapeDtypeStruct((rows, d), BF16), jax.ShapeDtypeStruct((rows, d), F32),
                   jax.ShapeDtypeStruct((rows, d), F32)],
        compiler_params=_params(("parallel",)),
    )(ua, o_att, uc, ud, z_br, z_gl, bias, woa, wob, woc, wod, wo, gpost, hres)


def loss_head(hres, target, n_tokens):
    rows, d = hres.shape
    t = ROW_TILE
    cur, _, _, full, _ = _tile_specs(t, rows // HALO)
    n_steps = rows // t

    def body(h_ref, t_ref, dh_ref, tot_ref, acc_ref):
        i = pl.program_id(0)

        @pl.when(i == 0)
        def _():
            acc_ref[...] = jnp.zeros_like(acc_ref)

        r = i * t + lax.broadcasted_iota(jnp.int32, (t, 1), 0)
        diff = jnp.where((r >= N_META) & (r < N_META + n_tokens), h_ref[...] - t_ref[...], 0.0)
        dh_ref[...] = diff * (1.0 / d)
        acc_ref[...] += jnp.sum(diff * diff, axis=0, keepdims=True)

        @pl.when(i == n_steps - 1)
        def _():
            tot_ref[...] = jnp.broadcast_to(jnp.sum(acc_ref[...], axis=1, keepdims=True), (1, LANES))

    return pl.pallas_call(
        body, name="loss_head", grid=(n_steps,), in_specs=[cur(d), cur(d)], out_specs=[cur(d), full((1, LANES))],
        out_shape=[jax.ShapeDtypeStruct((rows, d), F32), jax.ShapeDtypeStruct((1, LANES), F32)],
        scratch_shapes=[pltpu.VMEM((1, d), F32)], compiler_params=_params(("arbitrary",)),
    )(hres, target)


def _accumulate(i, ref, value):
    @pl.when(i == 0)
    def _():
        ref[...] = value

    @pl.when(i > 0)
    def _():
        ref[...] += value


def postnorm_bwd(dh, o, mb, wo, gpost, li):
    rows, d = dh.shape
    t = ROW_TILE
    cur, _, _, full, layer = _tile_specs(t, rows // HALO, li)

    def body(dh_ref, o_ref, mb_ref, wo_ref, gp_ref, dm_ref, dwo_ref, dgp_ref):
        i = pl.program_id(0)
        _, vjp = jax.vjp(_rms, o_ref[...], gp_ref[...])
        do, dg = vjp(dh_ref[...])
        dob = do.astype(BF16)
        dm_ref[...] = _dot_nt(dob, wo_ref[...])
        _accumulate(i, dwo_ref, _dot_tn(mb_ref[...], dob))
        _accumulate(i, dgp_ref, dg)

    return pl.pallas_call(
        body, name="postnorm_bwd", grid=(rows // t,), in_specs=[cur(d), cur(d), cur(d), layer((d, d), 0), layer((1, d))],
        out_specs=[cur(d), full((d, d)), full((1, d))],
        out_shape=[jax.ShapeDtypeStruct((rows, d), F32), jax.ShapeDtypeStruct((d, d), F32), jax.ShapeDtypeStruct((1, d), F32)],
        compiler_params=_params(("arbitrary",)),
    )(dh, o, mb, wo, gpost)


def merge_bwd(dm, ua, ub, uc, ud, z_gl, bias, woa, wob, woc, wod, li):
    rows, d = dm.shape
    t = ROW_TILE
    cur, _, _, full, layer = _tile_specs(t, rows // HALO, li)
    widths = (256, 512, 256, 256)

    def body(dm_ref, ua_ref, ub_ref, uc_ref, ud_ref, gl_ref, b_ref, woa_ref, wob_ref, woc_ref, wod_ref,
             dua_ref, dub_ref, duc_ref, dud_ref, dgl_ref, dwa_ref, dwb_ref, dwc_ref, dwd_ref, db_ref):
        i = pl.program_id(0)
        dm = dm_ref[...]
        groups = ((ua_ref, woa_ref, dua_ref, dwa_ref), (ub_ref, wob_ref, dub_ref, dwb_ref), (uc_ref, woc_ref, duc_ref, dwc_ref),
                  (ud_ref, wod_ref, dud_ref, dwd_ref))
        for idx, (u_ref, w_ref, du_ref, dw_ref) in enumerate(groups):
            cols = slice(d * idx, d * (idx + 1))
            u = u_ref[...]
            gate = _sigmoid(gl_ref[:, cols].astype(F32) + b_ref[:, cols])
            dgl = dm * _dot(u, w_ref[...]) * gate * (1.0 - gate)
            dgl_ref[:, cols] = dgl.astype(BF16)
            _accumulate(i, db_ref.at[:, cols], jnp.sum(dgl, axis=0, keepdims=True))
            dyb = (dm * gate).astype(BF16)
            du_ref[...] = _dot_nt(dyb, w_ref[...])
            _accumulate(i, dw_ref, _dot_tn(u, dyb))

    return pl.pallas_call(
        body, name="merge_bwd", grid=(rows // t,),
        in_specs=[cur(d), cur(256), cur(512), cur(256), cur(256), cur(ZG), layer((1, ZG))] + [layer((w, d), 0) for w in widths],
        out_specs=[cur(256), cur(512), cur(256), cur(256), cur(ZG)] + [full((w, d)) for w in widths] + [full((1, ZG))],
        out_shape=[jax.ShapeDtypeStruct((rows, w), F32) for w in widths] + [jax.ShapeDtypeStruct((rows, ZG), BF16)] + [
            jax.ShapeDtypeStruct((w, d), F32) for w in widths] + [jax.ShapeDtypeStruct((1, ZG), F32)],
        compiler_params=_params(("arbitrary",)),
    )(dm, ua, ub, uc, ud, z_gl, bias, woa, wob, woc, wod)


def pool_bwd(z_br, dua, pwbd, pscale, dz_buf, li):
    rows = z_br.shape[0]
    t = ROW_TILE
    n_steps = rows // t
    cur, prev, nxt, full, layer = _tile_specs(t, rows // HALO, li)

    def body(zc_ref, zp_ref, zn_ref, dc_ref, dn_ref, pw_ref, ps_ref, _, dz_ref, dpw_ref, dps_ref):
        i = pl.program_id(0)
        zp = jnp.where(i == 0, jnp.zeros(zp_ref.shape, zp_ref.dtype), zp_ref[...])
        zn = jnp.where(i == n_steps - 1, jnp.zeros(zn_ref.shape, zn_ref.dtype), zn_ref[...])
        dun = jnp.where(i == n_steps - 1, jnp.zeros(dn_ref.shape, dn_ref.dtype), dn_ref[...])

        def ext(lo):
            return jnp.concatenate([zp[:, lo:lo + 256], zc_ref[:, lo:lo + 256], zn[:, lo:lo + 256]], axis=0).astype(F32)

        n_ext = t + 2 * HALO
        v, pg = ext(PV), ext(PG)
        cnt = _pool_counts(i * t - HALO, n_ext)
        p = (_pool_window_sums(v, _sh) / cnt - v)[HALO:HALO + t]
        du = jnp.concatenate([jnp.zeros((HALO, 256), F32), dc_ref[...], dun], axis=0)
        dya = du * _silu(pg)
        dypb = (dya * ps_ref[...]).astype(BF16)
        dp = _dot_nt(dypb, pw_ref[...])
        dv = (_pool_window_sums(dp / cnt, _ash) - dp)[HALO:HALO + t]
        pb = p.astype(BF16)
        pw = _dot(pb, pw_ref[...])
        duc, pgc = dc_ref[...], pg[HALO:HALO + t]
        dpg = duc * pw * ps_ref[...] * _silu_grad(pgc)
        dz_ref[...] = jnp.concatenate([dv, dpg], axis=1).astype(BF16)
        _accumulate(i, dpw_ref, _dot_tn(pb, dypb[HALO:HALO + t]))
        _accumulate(i, dps_ref, jnp.sum(dya[HALO:HALO + t] * pw, axis=0, keepdims=True))

    return pl.pallas_call(
        body, name="pool_bwd", grid=(n_steps,),
        in_specs=[cur(ZB), prev(ZB), nxt(ZB), cur(256), nxt(256), layer((256, 256)), layer((1, 256)), ANY],
        out_specs=[cur(512, PV // 512), full((256, 256)), full((1, 256))],
        out_shape=[jax.ShapeDtypeStruct((rows, ZB), BF16), jax.ShapeDtypeStruct((256, 256), F32), jax.ShapeDtypeStruct((1, 256), F32)],
        input_output_aliases={7: 0}, compiler_params=_params(("arbitrary",)),
    )(z_br, z_br, z_br, dua, dua, pwbd, pscale, dz_buf)


def shortconv_bwd(z_br, dud, sc_w, dz_buf, li):
    rows = z_br.shape[0]
    t = ROW_TILE
    n_steps = rows // t
    cur, prev, nxt, full, layer = _tile_specs(t, rows // HALO, li)

    def body(zc_ref, zp_ref, zn_ref, dc_ref, dn_ref, sw_ref, _, dz_ref, dw_ref):
        i = pl.program_id(0)
        zp = jnp.where(i == 0, jnp.zeros(zp_ref.shape, zp_ref.dtype), zp_ref[...])
        zn = jnp.where(i == n_steps - 1, jnp.zeros(zn_ref.shape, zn_ref.dtype), zn_ref[...])
        dun = jnp.where(i == n_steps - 1, jnp.zeros(dn_ref.shape, dn_ref.dtype), dn_ref[...])

        def ext(lo):
            return jnp.concatenate([zp[:, lo:lo + 256], zc_ref[:, lo:lo + 256], zn[:, lo:lo + 256]], axis=0).astype(F32)

        mid = slice(HALO, HALO + t)
        bg, c2, xv, sg = ext(BG), ext(C2), ext(XV), ext(SG)
        du = jnp.concatenate([jnp.zeros((HALO, 256), F32), dc_ref[...], dun], axis=0)
        e = c2 * xv
        shifted = [_sh(e, SC_K - 1 - k) for k in range(SC_K)]
        f = sum(sw_ref[k:k + 1, :] * shifted[k] for k in range(SC_K))
        gate = _silu(sg)
        df = du * gate * bg
        de = sum(sw_ref[k:k + 1, :] * _ash(df, SC_K - 1 - k) for k in range(SC_K))
        dbg = du * gate * f
        dsg = du * bg * f * _silu_grad(sg)
        dz_ref[...] = jnp.concatenate([dbg[mid], (de * xv)[mid], (de * c2)[mid], dsg[mid]], axis=1).astype(BF16)
        dw = jnp.concatenate([jnp.sum((df * shifted[k])[mid], axis=0, keepdims=True) for k in range(SC_K)] + [
            jnp.zeros((8 - SC_K, 256), F32)], axis=0)
        _accumulate(i, dw_ref, dw)

    return pl.pallas_call(
        body, name="shortconv_bwd", grid=(n_steps,), in_specs=[cur(ZB), prev(ZB), nxt(ZB), cur(256), nxt(256), layer((8, 256)), ANY],
        out_specs=[cur(1024, BG // 1024), full((8, 256))],
        out_shape=[jax.ShapeDtypeStruct((rows, ZB), BF16), jax.ShapeDtypeStruct((8, 256), F32)],
        input_output_aliases={6: 0}, compiler_params=_params(("arbitrary",)),
    )(z_br, z_br, z_br, dud, dud, sc_w, dz_buf)


def conformer_bwd_tail(z_br, duc, conf_w, conf_vec, dz_buf, li):
    rows = z_br.shape[0]
    t = ROW_TILE
    cur, prev, _, full, layer = _tile_specs(t, rows // HALO, li)

    def body(zc_ref, zp_ref, du_ref, cw_ref, cv_ref, _, dc_ref, dcg_ref, dv_ref):
        i = pl.program_id(0)
        zp = jnp.where(i == 0, jnp.zeros(zp_ref.shape, zp_ref.dtype), zp_ref[...])

        def ext(lo):
            return jnp.concatenate([zp[:, lo:lo + 256], zc_ref[:, lo:lo + 256]], axis=0).astype(F32)

        g1 = ext(CA) * _sigmoid(ext(CGT))
        c = _conf_conv(g1, cw_ref)[HALO:] + cv_ref[0:1, :]
        _, vjp = jax.vjp(_conf_tail, c, zc_ref[:, CG:CG + 256].astype(F32), cv_ref[1:2, :], cv_ref[2:3, :])
        dc, dcg, dlg, dlb = vjp(du_ref[...])
        dc_ref[...] = dc
        dcg_ref[...] = dcg.astype(BF16)
        dvec = jnp.concatenate([dlg, dlb, jnp.sum(dc, axis=0, keepdims=True), jnp.zeros((5, 256), F32)], axis=0)
        _accumulate(i, dv_ref, dvec)

    return pl.pallas_call(
        body, name="conformer_bwd_tail", grid=(rows // t,), in_specs=[cur(ZB), prev(ZB), cur(256), layer((32, 256)), layer((8, 256)), ANY],
        out_specs=[cur(256), cur(256, CG // 256), full((8, 256))],
        out_shape=[jax.ShapeDtypeStruct((rows, 256), F32), jax.ShapeDtypeStruct((rows, ZB), BF16), jax.ShapeDtypeStruct((8, 256), F32)],
        input_output_aliases={5: 1}, compiler_params=_params(("arbitrary",)),
    )(z_br, z_br, duc, conf_w, conf_vec, dz_buf)


def conformer_bwd_conv(z_br, dc, conf_w, dz_buf, li):
    rows = z_br.shape[0]
    t = ROW_TILE
    n_steps = rows // t
    cur, prev, nxt, full, layer = _tile_specs(t, rows // HALO, li)

    def body(zc_ref, zp_ref, dc_ref, dn_ref, cw_ref, _, dz_ref, dw_ref):
        i = pl.program_id(0)
        zp = jnp.where(i == 0, jnp.zeros(zp_ref.shape, zp_ref.dtype), zp_ref[...])
        dcn = jnp.where(i == n_steps - 1, jnp.zeros(dn_ref.shape, dn_ref.dtype), dn_ref[...])

        def ext(lo):
            return jnp.concatenate([zp[:, lo:lo + 256], zc_ref[:, lo:lo + 256]], axis=0).astype(F32)

        a, gt = ext(CA), ext(CGT)
        sg = _sigmoid(gt)
        g1 = a * sg
        dc = dc_ref[...]
        dce = jnp.concatenate([dc, dcn], axis=0)
        dg1 = jnp.zeros_like(dce)
        dws = []
        for k in range(CONF_K):
            dg1 = dg1 + cw_ref[k:k + 1, :] * _ash(dce, CONF_K - 1 - k)
            dws.append(jnp.sum(dc * _sh(g1, CONF_K - 1 - k)[HALO:], axis=0, keepdims=True))
        dg1 = dg1[:t]
        ac, sc = a[HALO:], sg[HALO:]
        dz_ref[...] = jnp.concatenate([dg1 * sc, dg1 * ac * sc * (1.0 - sc)], axis=1).astype(BF16)
        _accumulate(i, dw_ref, jnp.concatenate(dws + [jnp.zeros((32 - CONF_K, 256), F32)], axis=0))

    return pl.pallas_call(
        body, name="conformer_bwd_conv", grid=(n_steps,), in_specs=[cur(ZB), prev(ZB), cur(256), nxt(256), layer((32, 256)), ANY],
        out_specs=[cur(512, CA // 512), full((32, 256))],
        out_shape=[jax.ShapeDtypeStruct((rows, ZB), BF16), jax.ShapeDtypeStruct((32, 256), F32)],
        input_output_aliases={5: 0}, compiler_params=_params(("arbitrary",)),
    )(z_br, z_br, dc, dc, conf_w, dz_buf)


def attention_bwd_prep(dub, o_att, z_br, dz_buf):
    rows = dub.shape[0]
    t = ROW_TILE
    cur, _, _, _, _ = _tile_specs(t, rows // HALO)

    def body(du_ref, o_ref, mg_ref, _, do_ref, dmg_ref, delta_ref):
        du, o, mg = du_ref[...], o_ref[...].astype(F32), mg_ref[...].astype(F32)
        do = du * _silu(mg)
        do_ref[...] = do.astype(BF16)
        dmg_ref[...] = (du * o * _silu_grad(mg)).astype(BF16)
        prod = do * o
        lane = lax.broadcasted_iota(jnp.int32, (1, HEADS * V_DIM), 1)
        for h in range(HEADS):
            part = jnp.where((lane >= V_DIM * h) & (lane < V_DIM * (h + 1)), prod, 0.0)
            delta_ref[h] = jnp.broadcast_to(jnp.sum(part, axis=-1, keepdims=True), (t, LANES))

    return pl.pallas_call(
        body, name="attention_bwd_prep", grid=(rows // t,), in_specs=[cur(512), cur(512), cur(512, MG // 512), ANY],
        out_specs=[cur(512), cur(512, MG // 512), pl.BlockSpec((HEADS, t, LANES), lambda i: (0, i, 0))],
        out_shape=[jax.ShapeDtypeStruct((rows, 512), BF16), jax.ShapeDtypeStruct((rows, ZB), BF16),
                   jax.ShapeDtypeStruct((HEADS, rows, LANES), F32)],
        input_output_aliases={3: 1}, compiler_params=_params(("parallel",)),
    )(dub, o_att, z_br, dz_buf)


def attention_bwd(q, k, v, do, lse, delta, exchange=None):
    rows = q.shape[0]
    tq = ROW_TILE
    nq = rows // tq
    n = 0 if exchange is None else len(exchange[0])

    def body(*refs):
        if n:
            start, finish = _exchange_ops(refs[6:6 + n], refs[9 + 2 * n:9 + 3 * n], refs[9 + 3 * n:], exchange[2])
            pl.when((pl.program_id(0) == 0) & (pl.program_id(1) == 0))(start)
        compute(*refs[:6], *refs[6 + 2 * n:9 + 2 * n])
        if n:
            pl.when((pl.program_id(0) == HEADS // 2 - 1) & (pl.program_id(1) == nq - 1))(finish)

    def compute(q_ref, k_ref, v_ref, do_ref, lse_ref, dl_ref, dq_ref, dk_ref, dv_ref):
        j = pl.program_id(1)

        @pl.when(j == 0)
        def _():
            dq_ref[...] = jnp.zeros_like(dq_ref)


        def head_step(h, tile, n_tiles, dk, dv, diagonal):
            lanes = slice(HEAD_PAD * h, HEAD_PAD * (h + 1))
            hm = _head_lane_mask(h)
            kh = k_ref[:, lanes]
            vh = jnp.where(hm, v_ref[...], jnp.zeros((), BF16))
            r0, width = pl.multiple_of(tile * tq, tq), n_tiles * tq
            qi = q_ref[pl.ds(r0, width), lanes]
            doi = jnp.where(hm, do_ref[pl.ds(r0, width), :], jnp.zeros((), BF16))
            s = _dot_nt(qi, kh)
            if diagonal:
                s = jnp.where(lax.broadcasted_iota(jnp.int32, (tq, tq), 1) <= lax.broadcasted_iota(jnp.int32, (tq, tq), 0), s, -1e30)
            pr = jnp.exp(s - lse_ref[h, pl.ds(r0, width), :][:, 0:1])
            dv = dv + _dot_tn(pr.astype(BF16), doi)
            dp = _dot_nt(doi, vh)
            ds = (pr * (dp - dl_ref[h, pl.ds(r0, width), :][:, 0:1])).astype(BF16)
            dq_ref[pl.ds(r0, width), lanes] += _dot(ds, kh)
            return dk + _dot_tn(ds, qi), dv

        def step(tile, n_tiles, carry, diagonal):
            dk0, dk1, dv = carry
            dk0, dv = head_step(0, tile, n_tiles, dk0, dv, diagonal)
            dk1, dv = head_step(1, tile, n_tiles, dk1, dv, diagonal)
            return dk0, dk1, dv

        zero = jnp.zeros((tq, HEAD_PAD), F32)
        carry = step(j, 1, (zero, zero, jnp.zeros((tq, 2 * V_DIM), F32)), True)
        odd = (nq - 1 - j) % 2
        carry = lax.cond(odd == 1, lambda cr: step(j + 1, 1, cr, False), lambda cr: cr, carry)
        dk0, dk1, dv = lax.fori_loop(0, (nq - 1 - j) // 2, lambda t, cr: step(j + 1 + odd + 2 * t, 2, cr, False), carry)
        dk_ref[:, 0:HEAD_PAD] = dk0
        dk_ref[:, HEAD_PAD:2 * HEAD_PAD] = dk1
        dv_ref[...] = dv

    srcs, dsts = ([], []) if exchange is None else (list(exchange[0]), list(exchange[1]))
    outs = pl.pallas_call(
        body, name="attention_bwd" if exchange is None else "attention_bwd_exchange", grid=(HEADS // 2, nq),
        in_specs=[pl.BlockSpec((rows, 2 * HEAD_PAD), lambda p, j: (0, p)), pl.BlockSpec((tq, 2 * HEAD_PAD), lambda p, j: (j, p)),
                  pl.BlockSpec((tq, 2 * V_DIM), lambda p, j: (j, p)), pl.BlockSpec((rows, 2 * V_DIM), lambda p, j: (0, p)),
                  pl.BlockSpec((2, rows, LANES), lambda p, j: (p, 0, 0)), pl.BlockSpec((2, rows, LANES), lambda p, j: (p, 0, 0))] + [
                      ANY] * (2 * n),
        out_specs=[pl.BlockSpec((rows, 2 * HEAD_PAD), lambda p, j: (0, p)), pl.BlockSpec((tq, 2 * HEAD_PAD), lambda p, j: (j, p)),
                   pl.BlockSpec((tq, 2 * V_DIM), lambda p, j: (j, p))] + [ANY] * n,
        out_shape=[jax.ShapeDtypeStruct((rows, HEADS * HEAD_PAD), F32), jax.ShapeDtypeStruct((rows, HEADS * HEAD_PAD), F32),
                   jax.ShapeDtypeStruct((rows, HEADS * V_DIM), F32)] + [jax.ShapeDtypeStruct(d.shape, d.dtype) for d in dsts],
        input_output_aliases={6 + n + a: 3 + a for a in range(n)}, scratch_shapes=EXCHANGE_SEMS(n) if n else [],
        compiler_params=_params(("arbitrary", "arbitrary") if n else ("parallel", "arbitrary")),
    )(q, k, v, do, lse, delta, *srcs, *dsts)
    return outs[0], outs[1], outs[2], list(outs[3:])


def mla_prep_bwd(dq, dk, dv, z_br, rope, gq, wuq, gkv, wukv, dz_buf, li):
    rows = dq.shape[0]
    t = ROW_TILE
    cur, _, _, full, layer = _tile_specs(t, rows // HALO, li)
    w8 = HEADS * HEAD_PAD

    def body(dq_ref, dk_ref, dv_ref, z_ref, rope_ref, gq_ref, wuq_ref, gkv_ref, wukv_ref, _, dz_ref, dwuq_ref, dwukv_ref, dgq_ref, dgkv_ref):
        i = pl.program_id(0)
        cth, s1, s2 = rope_ref[:, 0:128], rope_ref[:, 128:256], rope_ref[:, 256:384]
        dqb = _rope_transposed(dq_ref[...] * Q_SCALE, _lanes8(cth), _lanes8(s1), _lanes8(s2), w8).astype(BF16)
        cq = z_ref[:, 0:256].astype(F32)
        qn, vjp_q = jax.vjp(_rms, cq, gq_ref[...])
        _accumulate(i, dwuq_ref, _dot_tn(qn.astype(BF16), dqb))
        dcq, dgq = vjp_q(_dot_nt(dqb, wuq_ref[...]))
        _accumulate(i, dgq_ref, dgq)

        dk = dk_ref[...]
        dkr = sum(dk[:, HEAD_PAD * h:HEAD_PAD * (h + 1)] for h in range(HEADS))
        dkr = _rope_transposed(dkr, cth, s1, s2, HEAD_PAD)
        lane = lax.broadcasted_iota(jnp.int32, (1, HEAD_PAD), 1)
        dkr = jnp.where((lane >= QK_NOPE) & (lane < QK_NOPE + QK_ROPE), dkr, 0.0)
        dkvb = jnp.concatenate([dk, dv_ref[...]], axis=1).astype(BF16)
        ckv = z_ref[:, 256:384].astype(F32)
        kvn, vjp_kv = jax.vjp(_rms, ckv, gkv_ref[...])
        _accumulate(i, dwukv_ref, _dot_tn(kvn.astype(BF16), dkvb))
        dckv, dgkv = vjp_kv(_dot_nt(dkvb, wukv_ref[...]))
        _accumulate(i, dgkv_ref, dgkv)
        dz_ref[...] = jnp.concatenate([dcq, dckv, dkr], axis=1).astype(BF16)

    return pl.pallas_call(
        body, name="mla_prep_bwd", grid=(rows // t,),
        in_specs=[cur(w8), cur(w8), cur(512), cur(512, CQ // 512), cur(384), layer((1, 256)), layer((256, w8), 0), layer((1, 128)),
                  layer((128, w8 + 512), 0), ANY],
        out_specs=[cur(512, CQ // 512), full((256, w8)), full((128, w8 + 512)), full((1, 256)), full((1, 128))],
        out_shape=[jax.ShapeDtypeStruct((rows, ZB), BF16), jax.ShapeDtypeStruct((256, w8), F32), jax.ShapeDtypeStruct((128, w8 + 512), F32),
                   jax.ShapeDtypeStruct((1, 256), F32), jax.ShapeDtypeStruct((1, 128), F32)],
        input_output_aliases={9: 0}, compiler_params=_params(("arbitrary",)),
    )(dq, dk, dv, z_br, rope, gq, wuq, gkv, wukv, dz_buf)


def prenorm_bwd(dz_br, w_br, dh_gl, hres, gpre, dh_next, li):
    rows, d = hres.shape
    t = ROW_TILE
    cur, _, _, full, layer = _tile_specs(t, rows // HALO, li)

    def body(dz_ref, w_ref, dp_ref, x_ref, g_ref, dn_ref, dx_ref, dg_ref):
        i = pl.program_id(0)
        dh = _dot_nt(dz_ref[...], w_ref[...]) + dp_ref[...]
        _, vjp = jax.vjp(_rms, x_ref[...], g_ref[...])
        dx, dg = vjp(dh)
        dx_ref[...] = dx + dn_ref[...]
        _accumulate(i, dg_ref, dg)

    return pl.pallas_call(
        body, name="prenorm_bwd", grid=(rows // t,), in_specs=[cur(ZB), layer((d, ZB), 0), cur(d), cur(d), layer((1, d)), cur(d)],
        out_specs=[cur(d), full((1, d))], out_shape=[jax.ShapeDtypeStruct((rows, d), F32), jax.ShapeDtypeStruct((1, d), F32)],
        compiler_params=_params(("arbitrary",)),
    )(dz_br, w_br, dh_gl, hres, gpre, dh_next)


def _mesh_position():
    return lax.axis_index("x"), lax.axis_index("y"), lax.axis_index("c")


def chip_exchange(src, gather, name):
    block = src.shape if gather else src.shape[1:]

    def body(src_ref, dst_ref, send_sems, recv_sems, local_sem):
        x, y, c = _mesh_position()
        me = 2 * x + y
        peers = ((1 - x, y), (x, 1 - y), (1 - x, 1 - y))

        def part(k):
            return src_ref if gather else src_ref.at[k]

        def copy(j, slot):
            px, py = peers[j]
            return pltpu.make_async_remote_copy(src_ref=part(2 * px + py), dst_ref=dst_ref.at[slot], send_sem=send_sems.at[j],
                                                recv_sem=recv_sems.at[j], device_id=(px, py, c), device_id_type=MESH)

        local = pltpu.make_async_copy(part(me), dst_ref.at[me], local_sem)
        local.start()
        sends = [copy(j, me) for j in range(3)]
        for cp in sends:
            cp.start()
        for j, (px, py) in enumerate(peers):
            copy(j, 2 * px + py).wait_recv()
        for cp in sends:
            cp.wait_send()
        local.wait()

    return pl.pallas_call(
        body, name=name, in_specs=[pl.BlockSpec(memory_space=pl.ANY)], out_specs=pl.BlockSpec(memory_space=pl.ANY),
        out_shape=jax.ShapeDtypeStruct((N_CHIPS,) + tuple(block), src.dtype),
        scratch_shapes=[pltpu.SemaphoreType.DMA((3,)), pltpu.SemaphoreType.DMA((3,)), pltpu.SemaphoreType.DMA(())],
    )(src)


def sibling_swap(src, name):
    def body(src_ref, dst_ref, send_sem, recv_sem):
        x, y, c = _mesh_position()
        cp = pltpu.make_async_remote_copy(src_ref=src_ref, dst_ref=dst_ref, send_sem=send_sem, recv_sem=recv_sem,
                                          device_id=(x, y, 1 - c), device_id_type=MESH)
        cp.start()
        cp.wait()

    return pl.pallas_call(
        body, name=name, in_specs=[pl.BlockSpec(memory_space=pl.ANY)], out_specs=pl.BlockSpec(memory_space=pl.ANY),
        out_shape=jax.ShapeDtypeStruct(src.shape, src.dtype),
        scratch_shapes=[pltpu.SemaphoreType.DMA(()), pltpu.SemaphoreType.DMA(())],
    )(src)


def _comm_call(body, name, n_in, out_shapes, n_sems):
    return pl.pallas_call(
        body, name=name, in_specs=[ANY] * n_in, out_specs=[ANY] * len(out_shapes), out_shape=out_shapes,
        scratch_shapes=[pltpu.SemaphoreType.DMA((n,)) for n in n_sems])


def _row_halves(c, rows):
    half = rows // 2
    return pl.ds(pl.multiple_of(c * half, 16), half), pl.ds(pl.multiple_of((1 - c) * half, 16), half)


def _peers():
    x, y, c = _mesh_position()
    return x, y, c, 2 * x + y, ((1 - x, y), (x, 1 - y), (1 - x, 1 - y))


def _gather_ops(src, dst, sems, layer, own_copy):
    ici_send, ici_recv, d2d_send, d2d_recv, own_sems = sems
    n = len(src)

    def fetch(a, j, slot):
        x, y, c, _, peers = _peers()
        px, py = peers[j]
        mine, _ = _row_halves(c, src[a].shape[1])
        return pltpu.make_async_remote_copy(src_ref=src[a].at[layer, mine], dst_ref=dst[a].at[layer, slot, mine], send_sem=ici_send.at[3 * a + j],
                                            recv_sem=ici_recv.at[3 * a + j], device_id=(px, py, c), device_id_type=MESH)

    def forward(a, j, sibling_half):
        x, y, c, _, peers = _peers()
        px, py = peers[j]
        part = dst[a].at[layer, 2 * px + py, _row_halves(c, src[a].shape[1])[1 if sibling_half else 0]]
        return pltpu.make_async_remote_copy(src_ref=part, dst_ref=part, send_sem=d2d_send.at[3 * a + j], recv_sem=d2d_recv.at[3 * a + j],
                                            device_id=(x, y, 1 - c), device_id_type=MESH)

    def own(a):
        return pltpu.make_async_copy(src[a].at[layer], dst[a].at[layer, _peers()[3]], own_sems.at[a])

    def start():
        me = _peers()[3]
        for a in range(n):
            if own_copy:
                own(a).start()
            for j in range(3):
                fetch(a, j, me).start()

    def finish():
        peers = _peers()[4]
        for j, (px, py) in enumerate(peers):
            for a in range(n):
                fetch(a, j, 2 * px + py).wait_recv()
                forward(a, j, False).start()
        for j in range(3):
            for a in range(n):
                forward(a, j, True).wait_recv()
        for j in range(3):
            for a in range(n):
                fetch(a, j, 0).wait_send()
                forward(a, j, False).wait_send()
        if own_copy:
            for a in range(n):
                own(a).wait()

    return start, finish


def _exchange_ops(src, dst, sems, layer):
    send_sems, recv_sems, own_sems = sems
    n = len(src)

    def copy(a, j, slot):
        x, y, c, _, peers = _peers()
        px, py = peers[j]
        return pltpu.make_async_remote_copy(src_ref=src[a].at[2 * px + py], dst_ref=dst[a].at[layer, slot], send_sem=send_sems.at[3 * a + j],
                                            recv_sem=recv_sems.at[3 * a + j], device_id=(px, py, c), device_id_type=MESH)

    def own(a):
        me = _peers()[3]
        return pltpu.make_async_copy(src[a].at[me], dst[a].at[layer, me], own_sems.at[a])

    def start():
        me = _peers()[3]
        for a in range(n):
            own(a).start()
            for j in range(3):
                copy(a, j, me).start()

    def finish():
        peers = _peers()[4]
        for j, (px, py) in enumerate(peers):
            for a in range(n):
                copy(a, j, 2 * px + py).wait_recv()
        for j in range(3):
            for a in range(n):
                copy(a, j, 0).wait_send()
        for a in range(n):
            own(a).wait()

    return start, finish


GATHER_SEMS = lambda n: [pltpu.SemaphoreType.DMA((3 * n,))] * 4 + [pltpu.SemaphoreType.DMA((n,))]
EXCHANGE_SEMS = lambda n: [pltpu.SemaphoreType.DMA((3 * n,))] * 2 + [pltpu.SemaphoreType.DMA((n,))]


def gather_layer(srcs, dsts, layer, name):
    n = len(srcs)

    def body(*refs):
        start, finish = _gather_ops(refs[:n], refs[2 * n:3 * n], refs[3 * n:], layer, False)
        start()
        finish()

    return pl.pallas_call(
        body, name=name, in_specs=[ANY] * (2 * n), out_specs=[ANY] * n, out_shape=[jax.ShapeDtypeStruct(d.shape, d.dtype) for d in dsts],
        input_output_aliases={n + a: a for a in range(n)}, scratch_shapes=GATHER_SEMS(n),
    )(*srcs, *dsts)


def exchange_layer(ss, dsts, layer, name):
    n = len(ss)

    def body(*refs):
        start, finish = _exchange_ops(refs[:n], refs[2 * n:3 * n], refs[3 * n:], layer)
        start()
        finish()

    return pl.pallas_call(
        body, name=name, in_specs=[ANY] * (2 * n), out_specs=[ANY] * n, out_shape=[jax.ShapeDtypeStruct(d.shape, d.dtype) for d in dsts],
        input_output_aliases={n + a: a for a in range(n)}, scratch_shapes=EXCHANGE_SEMS(n),
    )(*ss, *dsts)


def swap_row_halves(ps, name):
    n = len(ps)

    def body(*refs):
        src, dst = refs[:n], refs[n:2 * n]
        send_sems, recv_sems = refs[2 * n:]
        x, y, c = _mesh_position()
        copies = [pltpu.make_async_remote_copy(src_ref=src[a].at[:, _row_halves(c, src[a].shape[1])[1]], dst_ref=dst[a], send_sem=send_sems.at[a],
                                               recv_sem=recv_sems.at[a], device_id=(x, y, 1 - c), device_id_type=MESH) for a in range(n)]
        for cp in copies:
            cp.start()
        for cp in copies:
            cp.wait()

    outs = [jax.ShapeDtypeStruct((p.shape[0], p.shape[1] // 2, p.shape[2]), p.dtype) for p in ps]
    return _comm_call(body, name, n, outs, (n, n))(*ps)


def add_row_half(p, r, c, name):
    n, half, cols = r.shape
    rb = _row_block(half, cols, 2)
    steps = half // rb

    def body(c_ref, p_ref, r_ref, o_ref):
        o_ref[...] = (p_ref[...].astype(F32) + r_ref[...].astype(F32)).astype(BF16)

    return pl.pallas_call(
        body, name=name, out_shape=jax.ShapeDtypeStruct(r.shape, BF16),
        grid_spec=pltpu.PrefetchScalarGridSpec(
            num_scalar_prefetch=1, grid=(n, steps),
            in_specs=[pl.BlockSpec((1, rb, cols), lambda k, i, c_ref: (k, c_ref[0] * steps + i, 0)),
                      pl.BlockSpec((1, rb, cols), lambda k, i, c_ref: (k, i, 0))],
            out_specs=pl.BlockSpec((1, rb, cols), lambda k, i, c_ref: (k, i, 0))),
        compiler_params=_params(("parallel", "parallel")),
    )(jnp.reshape(c, (1,)).astype(jnp.int32), p, r)


def sum_row_halves(l, c, name):
    layers, n, half, cols = l.shape
    rb = _row_block(half, cols, 4)
    steps = half // rb

    def body(c_ref, l_ref, o_ref):
        acc = l_ref[0, 0].astype(F32)
        for s in range(1, n):
            acc = acc + l_ref[0, s].astype(F32)
        o_ref[0] = acc

    return pl.pallas_call(
        body, name=name, out_shape=jax.ShapeDtypeStruct((layers, 2 * half, cols), F32),
        grid_spec=pltpu.PrefetchScalarGridSpec(
            num_scalar_prefetch=1, grid=(layers, steps), in_specs=[pl.BlockSpec((1, n, rb, cols), lambda a, i, c_ref: (a, 0, i, 0))],
            out_specs=pl.BlockSpec((1, rb, cols), lambda a, i, c_ref: (a, c_ref[0] * steps + i, 0))),
        compiler_params=_params(("parallel", "parallel")),
    )(jnp.reshape(c, (1,)).astype(jnp.int32), l)


def share_row_halves(gs, name):
    n = len(gs)

    def body(*refs):
        dst = refs[n:2 * n]
        send_sems, recv_sems = refs[2 * n:]
        x, y, c = _mesh_position()

        def copy(a, sibling_half):
            part = dst[a].at[:, _row_halves(c, dst[a].shape[1])[1 if sibling_half else 0]]
            return pltpu.make_async_remote_copy(src_ref=part, dst_ref=part, send_sem=send_sems.at[a], recv_sem=recv_sems.at[a],
                                                device_id=(x, y, 1 - c), device_id_type=MESH)

        for a in range(n):
            copy(a, False).start()
        for a in range(n):
            copy(a, True).wait_recv()
        for a in range(n):
            copy(a, False).wait_send()

    return pl.pallas_call(
        body, name=name, in_specs=[ANY] * n, out_specs=[ANY] * n, out_shape=[jax.ShapeDtypeStruct(g.shape, g.dtype) for g in gs],
        input_output_aliases={a: a for a in range(n)}, scratch_shapes=[pltpu.SemaphoreType.DMA((n,)), pltpu.SemaphoreType.DMA((n,))],
    )(*gs)


def _row_block(rows, cols, itemsize):
    best = 16
    for rb in range(16, rows + 1, 16):
        if rows % rb == 0 and rb * cols * itemsize <= 2 * 1024 * 1024:
            best = rb
    assert rows % best == 0, (rows, cols)
    return best


def _comm_block(rows):
    return 1024 if rows % 1024 == 0 else rows


def sum_slots(buf, name):
    n, r, c = buf.shape
    rb = _comm_block(r)

    def body(b_ref, o_ref):
        acc = b_ref[0].astype(F32)
        for s in range(1, n):
            acc = acc + b_ref[s].astype(F32)
        o_ref[...] = acc

    return pl.pallas_call(
        body, name=name, grid=(r // rb,), in_specs=[pl.BlockSpec((n, rb, c), lambda i: (0, i, 0))],
        out_specs=pl.BlockSpec((rb, c), lambda i: (i, 0)), out_shape=jax.ShapeDtypeStruct((r, c), F32),
        compiler_params=_params(("parallel",)),
    )(buf)


def add_pair(a, b, out_dtype, name):
    shape = a.shape
    a2, b2 = a.reshape(-1, shape[-1]), b.reshape(-1, shape[-1])
    r, c = a2.shape
    rb = _comm_block(r)

    def body(a_ref, b_ref, o_ref):
        o_ref[...] = (a_ref[...].astype(F32) + b_ref[...].astype(F32)).astype(out_dtype)

    out = pl.pallas_call(
        body, name=name, grid=(r // rb,), in_specs=[pl.BlockSpec((rb, c), lambda i: (i, 0))] * 2,
        out_specs=pl.BlockSpec((rb, c), lambda i: (i, 0)), out_shape=jax.ShapeDtypeStruct((r, c), out_dtype),
        compiler_params=_params(("parallel",)),
    )(a2, b2)
    return out.reshape(shape)


def adamw(w, g, m, v):
    shape = w.shape
    cols = shape[-1]
    rows = math.prod(shape[:-1])
    rb = rows if rows * cols <= 256 * 1024 else 256
    assert rows % rb == 0, shape

    def body(w_ref, g_ref, m_ref, v_ref, d_ref, nm_ref, nv_ref):
        g_ = g_ref[...]
        nm = ADAM_B1 * m_ref[...] + (1.0 - ADAM_B1) * g_
        nv = ADAM_B2 * v_ref[...] + (1.0 - ADAM_B2) * (g_ * g_)
        m_hat = nm / (1.0 - ADAM_B1 ** ADAM_STEP)
        v_hat = nv / (1.0 - ADAM_B2 ** ADAM_STEP)
        d_ref[...] = -ADAM_LR * (m_hat / (jnp.sqrt(v_hat) + ADAM_EPS) + ADAM_WD * w_ref[...])
        nm_ref[...] = nm
        nv_ref[...] = nv

    spec = pl.BlockSpec((rb, cols), lambda i: (i, 0))
    outs = pl.pallas_call(
        body, name="adamw", grid=(rows // rb,), in_specs=[spec] * 4, out_specs=[spec] * 3,
        out_shape=[jax.ShapeDtypeStruct((rows, cols), F32)] * 3, compiler_params=_params(("parallel",)),
    )(*(a.reshape(rows, cols) for a in (w, g, m, v)))
    return tuple(o.reshape(shape) for o in outs)


def _pack(arrays, dtype, row_multiple):
    flat = jnp.concatenate([a.astype(dtype).reshape(-1) for a in arrays])
    per = LANES * row_multiple
    total = -(-flat.shape[0] // per) * per
    return jnp.pad(flat, (0, total - flat.shape[0])).reshape(total // LANES, LANES)


def _unpack(buf, shapes):
    flat = buf.reshape(-1)
    out, off = [], 0
    for s in shapes:
        n = math.prod(s)
        out.append(flat[off:off + n].reshape(s))
        off += n
    return out


def _input_weights(blocks):
    c0, c1, c2, c3 = (blocks[..., k, :, :] for k in range(N_CHIPS))
    pad = lambda n: jnp.zeros(c0.shape[:-1] + (n,), blocks.dtype)
    w_br = jnp.concatenate([c1[..., 376:1400], c0[..., 0:896], pad(64), c0[..., 896:928], pad(32), c0[..., 928:], c1[..., 0:376]], axis=-1)
    return w_br, jnp.concatenate([c1[..., 1400:], c2, c3], axis=-1)


def _input_weights_inverse(dw_br, dw_gl):
    c0 = jnp.concatenate([dw_br[..., 1024:1920], dw_br[..., 1984:2016], dw_br[..., 2048:2952]], axis=-1)
    c1 = jnp.concatenate([dw_br[..., 2952:ZB], dw_br[..., 0:1024], dw_gl[..., 0:432]], axis=-1)
    return jnp.stack([c0, c1, dw_gl[..., 432:2264], dw_gl[..., 2264:]], axis=-3)


def _uq_layout(w):
    r = w.reshape(w.shape[:-1] + (HEADS, QK_NOPE + QK_ROPE))
    r = jnp.pad(r, [(0, 0)] * (r.ndim - 1) + [(0, HEAD_PAD - QK_NOPE - QK_ROPE)])
    return r.reshape(w.shape[:-1] + (HEADS * HEAD_PAD,))


def _uq_layout_inverse(dw):
    r = dw.reshape(dw.shape[:-1] + (HEADS, HEAD_PAD))[..., :QK_NOPE + QK_ROPE]
    return r.reshape(dw.shape[:-1] + (HEADS * (QK_NOPE + QK_ROPE),))


def _ukv_layout(w):
    r = w.reshape(w.shape[:-1] + (HEADS, QK_NOPE + V_DIM))
    kp = jnp.pad(r[..., :QK_NOPE], [(0, 0)] * (r.ndim - 1) + [(0, HEAD_PAD - QK_NOPE)]).reshape(w.shape[:-1] + (HEADS * HEAD_PAD,))
    return jnp.concatenate([kp, r[..., QK_NOPE:].reshape(w.shape[:-1] + (HEADS * V_DIM,))], axis=-1)


def _ukv_layout_inverse(dw):
    lead = dw.shape[:-1]
    dk = dw[..., :HEADS * HEAD_PAD].reshape(lead + (HEADS, HEAD_PAD))[..., :QK_NOPE]
    dv = dw[..., HEADS * HEAD_PAD:].reshape(lead + (HEADS, V_DIM))
    return jnp.concatenate([dk, dv], axis=-1).reshape(lead + (HEADS * (QK_NOPE + V_DIM),))


def _block_diag(pw):
    zeros = lambda n: jnp.zeros(pw.shape[:-3] + (64, n), pw.dtype)
    rows = [jnp.concatenate([zeros(64 * g), pw[..., g, :, :], zeros(64 * (3 - g))], axis=-1) for g in range(4)]
    return jnp.concatenate(rows, axis=-2)


def _block_diag_inverse(d):
    return jnp.stack([d[..., 64 * g:64 * (g + 1), 64 * g:64 * (g + 1)] for g in range(4)], axis=-3)


def _pad_rows(a, n):
    return jnp.pad(a, ((0, n - a.shape[0]), (0, 0)))


def _rope_tables(rows):
    inv = 1.0 / (ROPE_THETA ** (jnp.arange(0, QK_ROPE, 2, dtype=F32) / QK_ROPE))
    ang = jnp.arange(rows, dtype=F32)[:, None] * inv[None, :]
    cos, sin = jnp.cos(ang), jnp.sin(ang)
    one, zero = jnp.ones((rows, 1), F32), jnp.zeros((rows, 1), F32)
    rep = lambda a, n: jnp.broadcast_to(a, (rows, n))
    c = jnp.concatenate([rep(one, 64), cos, cos, rep(one, 32)], axis=1)
    s1 = jnp.concatenate([rep(zero, 64), -sin, rep(zero, 48)], axis=1)
    s2 = jnp.concatenate([rep(zero, 80), sin, rep(zero, 32)], axis=1)
    return jnp.concatenate([c, s1, s2], axis=1)


def _misc_block(parts):
    out = []
    for name, rows in MISC:
        a = parts[name]
        if name == "w_o":
            a = a.reshape(a.shape[:-2] + (rows, 256))
        elif name == "w_uq":
            a = jnp.pad(a, [(0, 0)] * (a.ndim - 1) + [(0, 256 - a.shape[-1])])
        out.append(a)
    return jnp.concatenate(out, axis=-2)


def _misc_unblock(block):
    out, off = {}, 0
    for name, rows in MISC:
        a = block[..., off:off + rows, :]
        off += rows
        if name == "w_o":
            a = a.reshape(a.shape[:-2] + (256, D_MODEL))
        elif name == "w_uq":
            a = a[..., :192]
        out[name] = a
    return out


def _to_chip_blocks(name, a):
    if name == "w_o":
        return a.reshape(a.shape[:-2] + (N_CHIPS, a.shape[-2] // N_CHIPS, a.shape[-1]))
    return jnp.swapaxes(a.reshape(a.shape[:-1] + (N_CHIPS, a.shape[-1] // N_CHIPS)), -3, -2)


def _from_chip_blocks(name, b):
    if name == "w_o":
        return b.reshape(b.shape[:-3] + (N_CHIPS * b.shape[-2], b.shape[-1]))
    s = jnp.swapaxes(b, -3, -2)
    return s.reshape(s.shape[:-2] + (N_CHIPS * s.shape[-1],))


LARGE = ("w_in",) + tuple(n for n, _ in MISC)


def gather_small(shards):
    small = chip_exchange(_pack([shards[n] for n, _, _ in SHARDED_SMALL], F32, 8), True, "gather_small_ici")
    per_chip = [_unpack(small[k], [s for _, s, _ in SHARDED_SMALL]) for k in range(N_CHIPS)]
    return {name: jnp.concatenate([per_chip[k][idx] for k in range(N_CHIPS)], axis=axis) for idx, (name, _, axis) in enumerate(SHARDED_SMALL)}


class LocalWeights:
    def __init__(self, full):
        self.full = full
        self.grads = [None] * DEPTH

    def layer(self, i):
        return {n: self.full[n][i] for n in LARGE}

    def gather_with_attention(self, i):
        return None

    def gathered(self, dsts):
        pass

    def exchange_with_attention(self):
        return None

    def exchanged(self, dsts):
        pass

    def put_grads(self, i, grads):
        self.grads[i] = grads

    def reduced(self):
        return {n: jnp.stack([g[n] for g in self.grads]) for n in LARGE}


class MeshWeights:
    def __init__(self, shards, c, chip):
        self.c, self.chip = c, chip
        self.srcs = [shards["w_in"].astype(BF16), _misc_block({n: shards[n] for n, _ in MISC}).astype(BF16)]
        dsts = [lax.empty((DEPTH, N_CHIPS) + s.shape[1:], BF16) for s in self.srcs]
        self.dsts = gather_layer(self.srcs, dsts, 0, "gather_layer")
        self.landed = [lax.empty((DEPTH, N_CHIPS, s.shape[1] // 2, s.shape[2]), BF16) for s in self.srcs]
        self.pending = None

    def layer(self, i):
        blocks = [d[i] for d in self.dsts]
        if i == 0:
            own = (jnp.arange(N_CHIPS) == self.chip)[:, None, None]
            blocks = [jnp.where(own, s[0][None], b) for s, b in zip(self.srcs, blocks)]
        out = {"w_in": blocks[0]}
        for name, b in _misc_unblock(blocks[1]).items():
            out[name] = _from_chip_blocks(name, b)
        return out

    def gather_with_attention(self, i):
        return (self.srcs, self.dsts, i + 1) if i + 1 < DEPTH else None

    def gathered(self, dsts):
        if dsts:
            self.dsts = dsts

    def exchange_with_attention(self):
        return None if self.pending is None else (self.pending[0], self.landed, self.pending[1])

    def exchanged(self, dsts):
        if dsts:
            self.landed, self.pending = dsts, None

    def put_grads(self, i, grads):
        ps = [grads["w_in"].astype(BF16), _misc_block({n: _to_chip_blocks(n, grads[n]) for n, _ in MISC}).astype(BF16)]
        rs = swap_row_halves(ps, "reduce_swap")
        self.pending = ([add_row_half(p, r, self.c, "reduce_pair_%d" % a) for a, (p, r) in enumerate(zip(ps, rs))], i)

    def reduced(self):
        landed = exchange_layer(self.pending[0], self.landed, self.pending[1], "reduce_exchange")
        gs = [sum_row_halves(l, self.c, "reduce_sum_%d" % a) for a, l in enumerate(landed)]
        g_in, g_misc = share_row_halves(gs, "reduce_share")
        out = {"w_in": g_in}
        out.update(_misc_unblock(g_misc))
        return out


def reduce_small(grads, chip):
    names = [n for n, _ in REPLICATED] + [n for n, _, _ in SHARDED_SMALL]
    buf = _pack([grads[n] for n in names], F32, 8)
    chip_sum = add_pair(buf, sibling_swap(buf, "reduce_small_d2d"), F32, "reduce_small_pair")
    total = sum_slots(chip_exchange(chip_sum, True, "reduce_small_ici"), "reduce_small_sum")
    out = dict(zip(names, _unpack(total, [grads[n].shape for n in names])))
    for name, shape, axis in SHARDED_SMALL:
        out[name] = lax.dynamic_slice_in_dim(out[name], chip * shape[axis], shape[axis], axis)
    return out


def _prepare_small(w):
    row = lambda a: a[:, None, :]
    conf_vec = jnp.concatenate([row(w["conf_dw_b"]), row(w["conf_ln_g"]), row(w["conf_ln_b"]), jnp.zeros((DEPTH, 5, 256), F32)], axis=1)
    return dict(
        gpre=row(w["pre_norm_g"]), bias=row(w["gate_bias"]), pwbd=_block_diag(w["pool_w"]).astype(BF16), pscale=row(w["pool_scale"]),
        gq=row(w["q_norm_g"]), gkv=row(w["kv_norm_g"]), conf_w=jnp.pad(w["conf_dw_w"].astype(F32), ((0, 0), (0, 32 - CONF_K), (0, 0))),
        conf_vec=conf_vec, sc_w=jnp.pad(w["sc_dw_w"].astype(F32), ((0, 0), (0, 8 - SC_K), (0, 0))), gpost=row(w["post_norm_g"]))


def _prepare_layer(large):
    w_br, w_gl = _input_weights(large["w_in"])
    one = lambda a: a.astype(BF16)[None]
    return dict(w_br=one(w_br), w_gl=one(w_gl), wuq=one(_uq_layout(large["w_uq"])), wukv=one(_ukv_layout(large["w_ukv"])),
                woa=one(large["w_out_pool"]), wob=one(large["w_out_mla"]), woc=one(large["w_out_conf"]), wod=one(large["w_out_sc"]),
                wo=one(large["w_o"]))


def local_step(x, target, w, large):
    seq = x.shape[0]
    length = N_META + seq
    rows = -(-length // ROW_TILE) * ROW_TILE
    bt = _big_tile(rows)
    hres = _pad_rows(jnp.concatenate([w["meta_tokens"].astype(F32), x], axis=0), rows)
    tgt = jnp.pad(target, ((N_META, rows - length), (0, 0)))
    rope = _rope_tables(rows)
    sw = _prepare_small(w)

    saved = []
    for i in range(DEPTH):
        lw = _prepare_layer(large.layer(i))
        z_br, hb = prenorm_project(hres, sw["gpre"], lw["w_br"], i)
        z_gl = matmul(hb, lw["w_gl"], "nn", BF16, bt, 1024, D_MODEL, "project_gates", b_layer=0)
        ua, uc, ud, q, k, v = branches_fwd(z_br, rope, sw["pwbd"], sw["pscale"], sw["gq"], lw["wuq"], sw["gkv"], lw["wukv"],
                                           sw["conf_w"], sw["conf_vec"], sw["sc_w"], i)
        o_att, lse, dsts = attention_fwd(q, k, v, large.gather_with_attention(i))
        large.gathered(dsts)
        ub, mb, o, hnew = merge_fwd(ua, o_att, uc, ud, z_br, z_gl, sw["bias"], lw["woa"], lw["wob"], lw["woc"], lw["wod"], lw["wo"],
                                    sw["gpost"], hres, i)
        saved.append(dict(lw=lw, hres=hres, hb=hb, z_br=z_br, z_gl=z_gl, ua=ua, ub=ub, uc=uc, ud=ud, q=q, k=k, v=v, o_att=o_att,
                          lse=lse, mb=mb, o=o))
        hres = hnew

    dh, total = loss_head(hres, tgt, seq)

    g = {n: [None] * DEPTH for n in ("gpre", "bias", "pwbd", "pscale", "gq", "gkv", "conf_w", "conf_vec", "sc_w", "gpost")}
    for i in reversed(range(DEPTH)):
        s = saved[i]
        lw = s["lw"]
        dm, dwo, g["gpost"][i] = postnorm_bwd(dh, s["o"], s["mb"], lw["wo"], sw["gpost"], i)
        dua, dub, duc, dud, dz_gl, dwa, dwb, dwc, dwd, g["bias"][i] = merge_bwd(
            dm, s["ua"], s["ub"], s["uc"], s["ud"], s["z_gl"], sw["bias"], lw["woa"], lw["wob"], lw["woc"], lw["wod"], i)
        dz_br = lax.empty((rows, ZB), BF16)
        dz_br, g["pwbd"][i], g["pscale"][i] = pool_bwd(s["z_br"], dua, sw["pwbd"], sw["pscale"], dz_br, i)
        dz_br, g["sc_w"][i] = shortconv_bwd(s["z_br"], dud, sw["sc_w"], dz_br, i)
        dc, dz_br, g["conf_vec"][i] = conformer_bwd_tail(s["z_br"], duc, sw["conf_w"], sw["conf_vec"], dz_br, i)
        dz_br, g["conf_w"][i] = conformer_bwd_conv(s["z_br"], dc, sw["conf_w"], dz_br, i)
        do, dz_br, delta = attention_bwd_prep(dub, s["o_att"], s["z_br"], dz_br)
        dq, dk, dv, dsts = attention_bwd(s["q"], s["k"], s["v"], do, s["lse"], delta, large.exchange_with_attention())
        large.exchanged(dsts)
        dz_br, dwuq, dwukv, g["gq"][i], g["gkv"][i] = mla_prep_bwd(dq, dk, dv, s["z_br"], rope, sw["gq"], lw["wuq"], sw["gkv"],
                                                                 lw["wukv"], dz_br, i)
        dw_br = matmul(s["hb"], dz_br, "tn", F32, D_MODEL, ZB // 2, bt, "grad_w_branch")
        dw_gl = matmul(s["hb"], dz_gl, "tn", F32, D_MODEL, 1024, bt, "grad_w_gates")
        dh_gl = matmul(dz_gl, lw["w_gl"], "nt", F32, bt, D_MODEL, 1024, "grad_h_gates", b_layer=0)
        dh, g["gpre"][i] = prenorm_bwd(dz_br, lw["w_br"], dh_gl, s["hres"], sw["gpre"], dh, i)
        large.put_grads(i, dict(w_in=_input_weights_inverse(dw_br, dw_gl), w_out_pool=dwa, w_uq=_uq_layout_inverse(dwuq),
                                w_ukv=_ukv_layout_inverse(dwukv), w_out_mla=dwb, w_out_conf=dwc, w_out_sc=dwd, w_o=dwo))

    g = {n: jnp.stack(parts) for n, parts in g.items()}
    grads = dict(
        meta_tokens=dh[:N_META], pre_norm_g=g["gpre"][:, 0], gate_bias=g["bias"][:, 0], pool_w=_block_diag_inverse(g["pwbd"]),
        pool_scale=g["pscale"][:, 0], q_norm_g=g["gq"][:, 0], kv_norm_g=g["gkv"][:, 0], conf_dw_w=g["conf_w"][:, :CONF_K],
        conf_dw_b=g["conf_vec"][:, 2], conf_ln_g=g["conf_vec"][:, 0], conf_ln_b=g["conf_vec"][:, 1], sc_dw_w=g["sc_w"][:, :SC_K],
        post_norm_g=g["gpost"][:, 0])
    return total[0, 0], dh[N_META:length], grads


def kernel(x, meta_tokens, pre_norm_g, w_in, gate_bias, pool_w, pool_scale, w_out_pool, q_norm_g, w_uq, kv_norm_g, w_ukv, w_out_mla, conf_dw_w, conf_dw_b, conf_ln_g, conf_ln_b, w_out_conf, sc_dw_w, w_out_sc, w_o, post_norm_g, loss_target, m_meta_tokens, m_pre_norm_g, m_w_in, m_gate_bias, m_pool_w, m_pool_scale, m_w_out_pool, m_q_norm_g, m_w_uq, m_kv_norm_g, m_w_ukv, m_w_out_mla, m_conf_dw_w, m_conf_dw_b, m_conf_ln_g, m_conf_ln_b, m_w_out_conf, m_sc_dw_w, m_w_out_sc, m_w_o, m_post_norm_g, v_meta_tokens, v_pre_norm_g, v_w_in, v_gate_bias, v_pool_w, v_pool_scale, v_w_out_pool, v_q_norm_g, v_w_uq, v_kv_norm_g, v_w_ukv, v_w_out_mla, v_conf_dw_w, v_conf_dw_b, v_conf_ln_g, v_conf_ln_b, v_w_out_conf, v_sc_dw_w, v_w_out_sc, v_w_o, v_post_norm_g):
    args = locals()
    weights = {n: args[n] for n in WEIGHT_ORDER}
    c = lax.axis_index("c")
    chip = 2 * lax.axis_index("x") + lax.axis_index("y")

    small = {n: weights[n] for n, _ in REPLICATED}
    small.update(gather_small(weights))
    large = MeshWeights(weights, c, chip)
    total, dx, grads = local_step(x[0], loss_target[0], small, large)
    loss = lax.psum(total * (0.5 / D_MODEL), ("x", "y", "c"))

    reduced = large.reduced()
    reduced.update(reduce_small(grads, chip))

    deltas, new_m, new_v = [], [], []
    for n in WEIGHT_ORDER:
        d, nm, nv = adamw(weights[n], reduced[n], args["m_" + n], args["v_" + n])
        deltas.append(d)
        new_m.append(nm)
        new_v.append(nv)
    return (loss, dx[None], *[reduced[n] for n in WEIGHT_ORDER], *deltas, *new_m, *new_v)
```

```python
import functools
import math

import jax
import jax.numpy as jnp
from jax import lax
from jax.experimental import pallas as pl
from jax.experimental.pallas import tpu as pltpu

F32 = jnp.float32
BF16 = jnp.bfloat16

D_MODEL = 1024
DEPTH = 4
N_META = 16
EPS = 1e-6
HEADS = 8
QK_NOPE = 64
QK_ROPE = 32
V_DIM = 64
HEAD_PAD = 128
ROPE_THETA = 10000.0
Q_SCALE = (QK_NOPE + QK_ROPE) ** -0.5
CONF_K = 31
SC_K = 3
IN_W = 7328
N_CHIPS = 4

ZB = 3328
ZG = 4096
BG, C2, XV, SG, PV, PG, CQ, CKV, KR, MG, CA, CGT, CG = (0, 256, 512, 768, 1024, 1280, 1536, 1792, 1920, 2048, 2560, 2816, 3072)

ROW_TILE = 384
HALO = 32
LANES = 128
VMEM_LIMIT = 56 * 1024 * 1024

ADAM_LR = 0.001
ADAM_B1 = 0.9
ADAM_B2 = 0.999
ADAM_EPS = 1e-08
ADAM_WD = 0.01
ADAM_STEP = 10

MESH = pl.DeviceIdType.MESH
ANY = pl.BlockSpec(memory_space=pl.ANY)

MISC = (
    ("w_out_pool", 256), ("w_ukv", 128), ("w_out_mla", 512), ("w_out_conf", 256), ("w_out_sc", 256), ("w_o", 1024), ("w_uq", 256))
SHARDED_SMALL = (
    ("meta_tokens", (N_META, 256), 1),
    ("conf_dw_w", (DEPTH, CONF_K, 64), 2),
    ("sc_dw_w", (DEPTH, SC_K, 64), 2),
)
REPLICATED = (
    ("pre_norm_g", (DEPTH, D_MODEL)),
    ("gate_bias", (DEPTH, 4 * D_MODEL)),
    ("pool_w", (DEPTH, 4, 64, 64)),
    ("pool_scale", (DEPTH, 256)),
    ("q_norm_g", (DEPTH, 256)),
    ("kv_norm_g", (DEPTH, 128)),
    ("conf_dw_b", (DEPTH, 256)),
    ("conf_ln_g", (DEPTH, 256)),
    ("conf_ln_b", (DEPTH, 256)),
    ("post_norm_g", (DEPTH, D_MODEL)),
)
WEIGHT_ORDER = ("meta_tokens", "pre_norm_g", "w_in", "gate_bias", "pool_w", "pool_scale", "w_out_pool", "q_norm_g", "w_uq",
                "kv_norm_g", "w_ukv", "w_out_mla", "conf_dw_w", "conf_dw_b", "conf_ln_g", "conf_ln_b", "w_out_conf", "sc_dw_w",
                "w_out_sc", "w_o", "post_norm_g")


def _dot(a, b):
    return lax.dot_general(a, b, (((1,), (0,)), ((), ())), preferred_element_type=F32)


def _dot_nt(a, b):
    return lax.dot_general(a, b, (((1,), (1,)), ((), ())), preferred_element_type=F32)


def _dot_tn(a, b):
    return lax.dot_general(a, b, (((0,), (0,)), ((), ())), preferred_element_type=F32)


def _sigmoid(x):
    return jax.nn.sigmoid(x)


def _silu(x):
    return x * _sigmoid(x)


def _silu_grad(x):
    s = _sigmoid(x)
    return s * (1.0 + x * (1.0 - s))


def _rms(x, g):
    return x * lax.rsqrt(jnp.mean(x * x, axis=-1, keepdims=True) + EPS) * g


def _sh(x, d):
    return x if d == 0 else pltpu.roll(x, d, 0)


def _ash(x, d):
    return x if d == 0 else pltpu.roll(x, x.shape[0] - d, 0)


def _lanes8(t):
    return jnp.concatenate([t] * HEADS, axis=1)


def _pool_window_sums(v, shift):
    a2 = v + shift(v, 1)
    a4 = a2 + shift(a2, 2)
    a8 = a4 + shift(a4, 4)
    a16 = a8 + shift(a8, 8)
    lane = lax.broadcasted_iota(jnp.int32, v.shape, 1)
    return jnp.where(lane < 64, a2, jnp.where(lane < 128, a4, jnp.where(lane < 192, a8, a16)))


def _pool_counts(first_row, rows):
    pos = first_row + lax.broadcasted_iota(jnp.int32, (rows, 256), 0)
    lane = lax.broadcasted_iota(jnp.int32, (rows, 256), 1)
    width = jnp.where(lane < 64, 2, jnp.where(lane < 128, 4, jnp.where(lane < 192, 8, 16)))
    return jnp.maximum(jnp.minimum(pos + 1, width), 1).astype(F32)


def _params(sem=None):
    return pltpu.CompilerParams(dimension_semantics=sem, vmem_limit_bytes=VMEM_LIMIT)


def _tile_specs(t, n_halo_blocks, li=0):
    per = t // HALO

    def layer(shape, idx=li):
        return pl.BlockSpec((None,) + tuple(shape), lambda i: (idx,) + (0,) * len(shape))

    def cur(c, cb=0):
        return pl.BlockSpec((t, c), lambda i: (i, cb))

    def prev(c, cb=0):
        return pl.BlockSpec((HALO, c), lambda i: (jnp.maximum(i * per - 1, 0), cb))

    def nxt(c, cb=0):
        return pl.BlockSpec((HALO, c), lambda i: (jnp.minimum((i + 1) * per, n_halo_blocks - 1), cb))

    def full(shape):
        return pl.BlockSpec(shape, lambda i: (0,) * len(shape))

    return cur, prev, nxt, full, layer


def _big_tile(rows):
    return rows // 3 if rows % (3 * LANES) == 0 else ROW_TILE


def matmul(a, b, mode, out_dtype, tm, tn, tk, name, b_layer=None):
    bs = b.shape if b_layer is None else b.shape[1:]
    lead = () if b_layer is None else (None,)
    pick = (lambda *ix: ix) if b_layer is None else (lambda *ix: (b_layer,) + ix)
    if mode == "nn":
        (m, k), n = a.shape, bs[1]
        a_spec = pl.BlockSpec((tm, tk), lambda i, j, kk: (i, kk))
        b_spec = pl.BlockSpec(lead + (tk, tn), lambda i, j, kk: pick(kk, j))
        dot = _dot
    elif mode == "nt":
        (m, k), n = a.shape, bs[0]
        a_spec = pl.BlockSpec((tm, tk), lambda i, j, kk: (i, kk))
        b_spec = pl.BlockSpec(lead + (tn, tk), lambda i, j, kk: pick(j, kk))
        dot = _dot_nt
    else:
        (k, m), n = a.shape, bs[1]
        a_spec = pl.BlockSpec((tk, tm), lambda i, j, kk: (kk, i))
        b_spec = pl.BlockSpec(lead + (tk, tn), lambda i, j, kk: pick(kk, j))
        dot = _dot_tn
    assert m % tm == 0 and n % tn == 0 and k % tk == 0, (a.shape, bs, tm, tn, tk)
    nk = k // tk

    def body(a_ref, b_ref, o_ref, acc_ref):
        kk = pl.program_id(2)

        @pl.when(kk == 0)
        def _():
            acc_ref[...] = jnp.zeros_like(acc_ref)

        acc_ref[...] += dot(a_ref[...], b_ref[...])

        @pl.when(kk == nk - 1)
        def _():
            o_ref[...] = acc_ref[...].astype(out_dtype)

    return pl.pallas_call(
        body, name=name, grid=(m // tm, n // tn, nk), in_specs=[a_spec, b_spec],
        out_specs=pl.BlockSpec((tm, tn), lambda i, j, kk: (i, j)), out_shape=jax.ShapeDtypeStruct((m, n), out_dtype),
        scratch_shapes=[pltpu.VMEM((tm, tn), F32)], compiler_params=_params(("parallel", "parallel", "arbitrary")),
    )(a, b)


def prenorm_project(hres, g, w, li):
    rows, d = hres.shape
    n = w.shape[2]
    tm, tn = _big_tile(rows), n // 2

    def body(x_ref, g_ref, w_ref, z_ref, hb_ref):
        @pl.when(pl.program_id(1) == 0)
        def _():
            hb_ref[...] = _rms(x_ref[...], g_ref[...]).astype(BF16)

        z_ref[...] = _dot(hb_ref[...], w_ref[...]).astype(BF16)

    return pl.pallas_call(
        body, name="prenorm_project", grid=(rows // tm, n // tn),
        in_specs=[pl.BlockSpec((tm, d), lambda i, j: (i, 0)), pl.BlockSpec((None, 1, d), lambda i, j: (li, 0, 0)),
                  pl.BlockSpec((None, d, tn), lambda i, j: (0, 0, j))],
        out_specs=[pl.BlockSpec((tm, tn), lambda i, j: (i, j)), pl.BlockSpec((tm, d), lambda i, j: (i, 0))],
        out_shape=[jax.ShapeDtypeStruct((rows, n), BF16), jax.ShapeDtypeStruct((rows, d), BF16)],
        compiler_params=_params(("parallel", "arbitrary")),
    )(hres, g, w)


def _rope(q, c, s1, s2, width):
    return q * c + pltpu.roll(q, width - 16, 1) * s1 + pltpu.roll(q, 16, 1) * s2


def _rope_transposed(dq, c, s1, s2, width):
    return dq * c + pltpu.roll(dq * s1, 16, 1) + pltpu.roll(dq * s2, width - 16, 1)


def _conf_conv(g1, w_ref):
    acc = jnp.zeros_like(g1)
    for k in range(CONF_K):
        acc = acc + w_ref[k:k + 1, :] * _sh(g1, CONF_K - 1 - k)
    return acc


def _conf_tail(c, cg, lg, lb):
    mu = jnp.mean(c, axis=-1, keepdims=True)
    xc = c - mu
    var = jnp.mean(xc * xc, axis=-1, keepdims=True)
    n = xc * lax.rsqrt(var + EPS) * lg + lb
    return _silu(n) * _silu(cg)


def branches_fwd(z_br, rope, pwbd, pscale, gq, wuq, gkv, wukv, conf_w, conf_vec, sc_w, li):
    rows = z_br.shape[0]
    t = ROW_TILE
    cur, prev, _, _, layer = _tile_specs(t, rows // HALO, li)

    def body(zc_ref, zp_ref, rope_ref, pw_ref, ps_ref, gq_ref, wuq_ref, gkv_ref, wukv_ref, cw_ref, cv_ref, sw_ref,
             ua_ref, uc_ref, ud_ref, q_ref, k_ref, v_ref):
        i = pl.program_id(0)
        zp = jnp.where(i == 0, jnp.zeros(zp_ref.shape, zp_ref.dtype), zp_ref[...])

        def ext(lo, w=256):
            return jnp.concatenate([zp[:, lo:lo + w], zc_ref[:, lo:lo + w]], axis=0).astype(F32)

        def col(lo, w=256):
            return zc_ref[:, lo:lo + w].astype(F32)

        v = ext(PV)
        p = (_pool_window_sums(v, _sh) / _pool_counts(i * t - HALO, t + HALO) - v)[HALO:]
        ya = _dot(p.astype(BF16), pw_ref[...]) * ps_ref[...]
        ua_ref[...] = (ya * _silu(col(PG))).astype(BF16)

        g1 = ext(CA) * _sigmoid(ext(CGT))
        c = _conf_conv(g1, cw_ref)[HALO:] + cv_ref[0:1, :]
        uc_ref[...] = _conf_tail(c, col(CG), cv_ref[1:2, :], cv_ref[2:3, :]).astype(BF16)

        e = ext(C2) * ext(XV)
        f = jnp.zeros_like(e)
        for k in range(SC_K):
            f = f + sw_ref[k:k + 1, :] * _sh(e, SC_K - 1 - k)
        ud_ref[...] = (col(BG) * f[HALO:] * _silu(col(SG))).astype(BF16)

        cth, s1, s2 = rope_ref[:, 0:128], rope_ref[:, 128:256], rope_ref[:, 256:384]
        qn = _rms(col(CQ), gq_ref[...]).astype(BF16)
        q = _dot(qn, wuq_ref[...])
        w8 = HEADS * HEAD_PAD
        q_ref[...] = (_rope(q, _lanes8(cth), _lanes8(s1), _lanes8(s2), w8) * Q_SCALE).astype(BF16)
        kvn = _rms(col(CKV, 128), gkv_ref[...]).astype(BF16)
        kv = _dot(kvn, wukv_ref[...])
        kr = _rope(col(KR, 128), cth, s1, s2, HEAD_PAD)
        k_ref[...] = (kv[:, :w8] + _lanes8(kr)).astype(BF16)
        v_ref[...] = kv[:, w8:].astype(BF16)

    outs = [jax.ShapeDtypeStruct((rows, 256), BF16)] * 3 + [jax.ShapeDtypeStruct((rows, 1024), BF16)] * 2 + [
        jax.ShapeDtypeStruct((rows, 512), BF16)]
    return pl.pallas_call(
        body, name="branches_fwd", grid=(rows // t,),
        in_specs=[cur(ZB), prev(ZB), cur(384), layer((256, 256)), layer((1, 256)), layer((1, 256)), layer((256, 1024), 0),
                  layer((1, 128)), layer((128, 1536), 0), layer((32, 256)), layer((8, 256)), layer((8, 256))],
        out_specs=[cur(256), cur(256), cur(256), cur(1024), cur(1024), cur(512)], out_shape=outs,
        compiler_params=_params(("parallel",)),
    )(z_br, z_br, rope, pwbd, pscale, gq, wuq, gkv, wukv, conf_w, conf_vec, sc_w)


def _head_lane_mask(h):
    lane = lax.broadcasted_iota(jnp.int32, (1, 2 * V_DIM), 1)
    return (lane >= V_DIM * h) & (lane < V_DIM * (h + 1))


def attention_fwd(q, k, v, gather=None):
    rows = q.shape[0]
    tq = ROW_TILE
    nq = rows // tq
    n = 0 if gather is None else len(gather[0])

    def body(*refs):
        if n:
            start, finish = _gather_ops(refs[3:3 + n], refs[5 + 2 * n:5 + 3 * n], refs[5 + 3 * n:], gather[2], True)
            pl.when((pl.program_id(0) == 0) & (pl.program_id(1) == 0))(start)
        compute(*refs[:3], *refs[3 + 2 * n:5 + 2 * n])
        if n:
            pl.when((pl.program_id(0) == HEADS // 2 - 1) & (pl.program_id(1) == nq - 1))(finish)

    def compute(q_ref, k_ref, v_ref, o_ref, lse_ref):
        i = pl.program_id(1)

        def head_step(h, tile, n_tiles, carry, masked):
            m, l, acc = carry
            width = n_tiles * tq
            r0 = pl.multiple_of(tile * tq, tq)
            kh = k_ref[pl.ds(r0, width), HEAD_PAD * h:HEAD_PAD * (h + 1)]
            vh = jnp.where(_head_lane_mask(h), v_ref[pl.ds(r0, width), :], jnp.zeros((), BF16))
            s = _dot_nt(q_ref[:, HEAD_PAD * h:HEAD_PAD * (h + 1)], kh)
            if masked:
                row = lax.broadcasted_iota(jnp.int32, (tq, width), 0)
                colm = lax.broadcasted_iota(jnp.int32, (tq, width), 1)
                s = jnp.where(colm <= row + (width - tq), s, -1e30)
            m2 = jnp.maximum(m, jnp.max(s, axis=-1, keepdims=True))
            alpha = jnp.exp(m - m2)
            pr = jnp.exp(s - m2)
            return m2, alpha * l + jnp.sum(pr, axis=-1, keepdims=True), alpha * acc + _dot(pr.astype(BF16), vh)

        def step(tile, n_tiles, carry, masked):
            return tuple(head_step(h, tile, n_tiles, carry[h], masked) for h in range(2))

        init = (jnp.full((tq, 1), -1e30, F32), jnp.zeros((tq, 1), F32), jnp.zeros((tq, 2 * V_DIM), F32))
        carry = lax.fori_loop(0, i // 2, lambda t, cr: step(2 * t, 2, cr, False), (init, init))
        carry = lax.cond(i % 2 == 1, lambda cr: step(i - 1, 2, cr, True), lambda cr: step(i, 1, cr, True), carry)
        out = jnp.zeros((tq, 2 * V_DIM), F32)
        for h, (m, l, acc) in enumerate(carry):
            out = out + acc / l
            lse_ref[h] = jnp.broadcast_to(m + jnp.log(l), (tq, LANES))
        o_ref[...] = out.astype(BF16)

    srcs, dsts = ([], []) if gather is None else (list(gather[0]), list(gather[1]))
    outs = pl.pallas_call(
        body, name="attention_fwd" if gather is None else "attention_fwd_gather", grid=(HEADS // 2, nq),
        in_specs=[pl.BlockSpec((tq, 2 * HEAD_PAD), lambda p, i: (i, p)), pl.BlockSpec((rows, 2 * HEAD_PAD), lambda p, i: (0, p)),
                  pl.BlockSpec((rows, 2 * V_DIM), lambda p, i: (0, p))] + [ANY] * (2 * n),
        out_specs=[pl.BlockSpec((tq, 2 * V_DIM), lambda p, i: (i, p)), pl.BlockSpec((2, tq, LANES), lambda p, i: (p, i, 0))] + [ANY] * n,
        out_shape=[jax.ShapeDtypeStruct((rows, HEADS * V_DIM), BF16), jax.ShapeDtypeStruct((HEADS, rows, LANES), F32)] + [
            jax.ShapeDtypeStruct(d.shape, d.dtype) for d in dsts],
        input_output_aliases={3 + n + a: 2 + a for a in range(n)}, scratch_shapes=GATHER_SEMS(n) if n else [],
        compiler_params=_params(("arbitrary", "arbitrary") if n else ("parallel", "parallel")),
    )(q, k, v, *srcs, *dsts)
    return outs[0], outs[1], list(outs[2:])


def merge_fwd(ua, o_att, uc, ud, z_br, z_gl, bias, woa, wob, woc, wod, wo, gpost, hres, li):
    rows = hres.shape[0]
    t = ROW_TILE
    cur, _, _, _, layer = _tile_specs(t, rows // HALO, li)
    d = D_MODEL

    def body(ua_ref, ob_ref, uc_ref, ud_ref, mg_ref, gl_ref, b_ref, woa_ref, wob_ref, woc_ref, wod_ref, wo_ref, gp_ref, h_ref,
             ub_ref, mb_ref, o_ref, hn_ref):
        ub = (ob_ref[...].astype(F32) * _silu(mg_ref[...].astype(F32))).astype(BF16)
        ub_ref[...] = ub
        m = jnp.zeros((t, d), F32)
        for idx, (u, w_ref) in enumerate(((ua_ref[...], woa_ref), (ub, wob_ref), (uc_ref[...], woc_ref), (ud_ref[...], wod_ref))):
            gate = _sigmoid(gl_ref[:, d * idx:d * (idx + 1)].astype(F32) + b_ref[:, d * idx:d * (idx + 1)])
            m = m + gate * _dot(u, w_ref[...])
        mb = m.astype(BF16)
        mb_ref[...] = mb
        o = _dot(mb, wo_ref[...])
        o_ref[...] = o
        hn_ref[...] = h_ref[...] + _rms(o, gp_ref[...])

    return pl.pallas_call(
        body, name="merge_fwd", grid=(rows // t,),
        in_specs=[cur(256), cur(512), cur(256), cur(256), cur(512, MG // 512), cur(ZG), layer((1, ZG)), layer((256, d), 0), layer((512, d), 0),
                  layer((256, d), 0), layer((256, d), 0), layer((d, d), 0), layer((1, d)), cur(d)],
        out_specs=[cur(512), cur(d), cur(d), cur(d)],
        out_shape=[jax.ShapeDtypeStruct((rows, 512), BF16), jax.ShapeDtypeStruct((rows, d), BF16), jax.ShapeDtypeStruct((rows, d), F32),
                   jax.ShapeDtypeStruct((rows, d), F32)],
        compiler_params=_params(("parallel",)),
    )(ua, o_att, uc, ud, z_br, z_gl, bias, woa, wob, woc, wod, wo, gpost, hres)


def loss_head(hres, target, n_tokens):
    rows, d = hres.shape
    t = ROW_TILE
    cur, _, _, full, _ = _tile_specs(t, rows // HALO)
    n_steps = rows // t

    def body(h_ref, t_ref, dh_ref, tot_ref, acc_ref):
        i = pl.program_id(0)

        @pl.when(i == 0)
        def _():
            acc_ref[...] = jnp.zeros_like(acc_ref)

        r = i * t + lax.broadcasted_iota(jnp.int32, (t, 1), 0)
        diff = jnp.where((r >= N_META) & (r < N_META + n_tokens), h_ref[...] - t_ref[...], 0.0)
        dh_ref[...] = diff * (1.0 / d)
        acc_ref[...] += jnp.sum(diff * diff, axis=0, keepdims=True)

        @pl.when(i == n_steps - 1)
        def _():
            tot_ref[...] = jnp.broadcast_to(jnp.sum(acc_ref[...], axis=1, keepdims=True), (1, LANES))

    return pl.pallas_call(
        body, name="loss_head", grid=(n_steps,), in_specs=[cur(d), cur(d)], out_specs=[cur(d), full((1, LANES))],
        out_shape=[jax.ShapeDtypeStruct((rows, d), F32), jax.ShapeDtypeStruct((1, LANES), F32)],
        scratch_shapes=[pltpu.VMEM((1, d), F32)], compiler_params=_params(("arbitrary",)),
    )(hres, target)


def _accumulate(i, ref, value):
    @pl.when(i == 0)
    def _():
        ref[...] = value

    @pl.when(i > 0)
    def _():
        ref[...] += value


def postnorm_bwd(dh, o, mb, wo, gpost, li):
    rows, d = dh.shape
    t = ROW_TILE
    cur, _, _, full, layer = _tile_specs(t, rows // HALO, li)

    def body(dh_ref, o_ref, mb_ref, wo_ref, gp_ref, dm_ref, dwo_ref, dgp_ref):
        i = pl.program_id(0)
        _, vjp = jax.vjp(_rms, o_ref[...], gp_ref[...])
        do, dg = vjp(dh_ref[...])
        dob = do.astype(BF16)
        dm_ref[...] = _dot_nt(dob, wo_ref[...])
        _accumulate(i, dwo_ref, _dot_tn(mb_ref[...], dob))
        _accumulate(i, dgp_ref, dg)

    return pl.pallas_call(
        body, name="postnorm_bwd", grid=(rows // t,), in_specs=[cur(d), cur(d), cur(d), layer((d, d), 0), layer((1, d))],
        out_specs=[cur(d), full((d, d)), full((1, d))],
        out_shape=[jax.ShapeDtypeStruct((rows, d), F32), jax.ShapeDtypeStruct((d, d), F32), jax.ShapeDtypeStruct((1, d), F32)],
        compiler_params=_params(("arbitrary",)),
    )(dh, o, mb, wo, gpost)


def merge_bwd(dm, ua, ub, uc, ud, z_gl, bias, woa, wob, woc, wod, li):
    rows, d = dm.shape
    t = ROW_TILE
    cur, _, _, full, layer = _tile_specs(t, rows // HALO, li)
    widths = (256, 512, 256, 256)

    def body(dm_ref, ua_ref, ub_ref, uc_ref, ud_ref, gl_ref, b_ref, woa_ref, wob_ref, woc_ref, wod_ref,
             dua_ref, dub_ref, duc_ref, dud_ref, dgl_ref, dwa_ref, dwb_ref, dwc_ref, dwd_ref, db_ref):
        i = pl.program_id(0)
        dm = dm_ref[...]
        groups = ((ua_ref, woa_ref, dua_ref, dwa_ref), (ub_ref, wob_ref, dub_ref, dwb_ref), (uc_ref, woc_ref, duc_ref, dwc_ref),
                  (ud_ref, wod_ref, dud_ref, dwd_ref))
        for idx, (u_ref, w_ref, du_ref, dw_ref) in enumerate(groups):
            cols = slice(d * idx, d * (idx + 1))
            u = u_ref[...]
            gate = _sigmoid(gl_ref[:, cols].astype(F32) + b_ref[:, cols])
            dgl = dm * _dot(u, w_ref[...]) * gate * (1.0 - gate)
            dgl_ref[:, cols] = dgl.astype(BF16)
            _accumulate(i, db_ref.at[:, cols], jnp.sum(dgl, axis=0, keepdims=True))
            dyb = (dm * gate).astype(BF16)
            du_ref[...] = _dot_nt(dyb, w_ref[...])
            _accumulate(i, dw_ref, _dot_tn(u, dyb))

    return pl.pallas_call(
        body, name="merge_bwd", grid=(rows // t,),
        in_specs=[cur(d), cur(256), cur(512), cur(256), cur(256), cur(ZG), layer((1, ZG))] + [layer((w, d), 0) for w in widths],
        out_specs=[cur(256), cur(512), cur(256), cur(256), cur(ZG)] + [full((w, d)) for w in widths] + [full((1, ZG))],
        out_shape=[jax.ShapeDtypeStruct((rows, w), F32) for w in widths] + [jax.ShapeDtypeStruct((rows, ZG), BF16)] + [
            jax.ShapeDtypeStruct((w, d), F32) for w in widths] + [jax.ShapeDtypeStruct((1, ZG), F32)],
        compiler_params=_params(("arbitrary",)),
    )(dm, ua, ub, uc, ud, z_gl, bias, woa, wob, woc, wod)


def pool_bwd(z_br, dua, pwbd, pscale, dz_buf, li):
    rows = z_br.shape[0]
    t = ROW_TILE
    n_steps = rows // t
    cur, prev, nxt, full, layer = _tile_specs(t, rows // HALO, li)

    def body(zc_ref, zp_ref, zn_ref, dc_ref, dn_ref, pw_ref, ps_ref, _, dz_ref, dpw_ref, dps_ref):
        i = pl.program_id(0)
        zp = jnp.where(i == 0, jnp.zeros(zp_ref.shape, zp_ref.dtype), zp_ref[...])
        zn = jnp.where(i == n_steps - 1, jnp.zeros(zn_ref.shape, zn_ref.dtype), zn_ref[...])
        dun = jnp.where(i == n_steps - 1, jnp.zeros(dn_ref.shape, dn_ref.dtype), dn_ref[...])

        def ext(lo):
            return jnp.concatenate([zp[:, lo:lo + 256], zc_ref[:, lo:lo + 256], zn[:, lo:lo + 256]], axis=0).astype(F32)

        n_ext = t + 2 * HALO
        v, pg = ext(PV), ext(PG)
        cnt = _pool_counts(i * t - HALO, n_ext)
        p = (_pool_window_sums(v, _sh) / cnt - v)[HALO:HALO + t]
        du = jnp.concatenate([jnp.zeros((HALO, 256), F32), dc_ref[...], dun], axis=0)
        dya = du * _silu(pg)
        dypb = (dya * ps_ref[...]).astype(BF16)
        dp = _dot_nt(dypb, pw_ref[...])
        dv = (_pool_window_sums(dp / cnt, _ash) - dp)[HALO:HALO + t]
        pb = p.astype(BF16)
        pw = _dot(pb, pw_ref[...])
        duc, pgc = dc_ref[...], pg[HALO:HALO + t]
        dpg = duc * pw * ps_ref[...] * _silu_grad(pgc)
        dz_ref[...] = jnp.concatenate([dv, dpg], axis=1).astype(BF16)
        _accumulate(i, dpw_ref, _dot_tn(pb, dypb[HALO:HALO + t]))
        _accumulate(i, dps_ref, jnp.sum(dya[HALO:HALO + t] * pw, axis=0, keepdims=True))

    return pl.pallas_call(
        body, name="pool_bwd", grid=(n_steps,),
        in_specs=[cur(ZB), prev(ZB), nxt(ZB), cur(256), nxt(256), layer((256, 256)), layer((1, 256)), ANY],
        out_specs=[cur(512, PV // 512), full((256, 256)), full((1, 256))],
        out_shape=[jax.ShapeDtypeStruct((rows, ZB), BF16), jax.ShapeDtypeStruct((256, 256), F32), jax.ShapeDtypeStruct((1, 256), F32)],
        input_output_aliases={7: 0}, compiler_params=_params(("arbitrary",)),
    )(z_br, z_br, z_br, dua, dua, pwbd, pscale, dz_buf)


def shortconv_bwd(z_br, dud, sc_w, dz_buf, li):
    rows = z_br.shape[0]
    t = ROW_TILE
    n_steps = rows // t
    cur, prev, nxt, full, layer = _tile_specs(t, rows // HALO, li)

    def body(zc_ref, zp_ref, zn_ref, dc_ref, dn_ref, sw_ref, _, dz_ref, dw_ref):
        i = pl.program_id(0)
        zp = jnp.where(i == 0, jnp.zeros(zp_ref.shape, zp_ref.dtype), zp_ref[...])
        zn = jnp.where(i == n_steps - 1, jnp.zeros(zn_ref.shape, zn_ref.dtype), zn_ref[...])
        dun = jnp.where(i == n_steps - 1, jnp.zeros(dn_ref.shape, dn_ref.dtype), dn_ref[...])

        def ext(lo):
            return jnp.concatenate([zp[:, lo:lo + 256], zc_ref[:, lo:lo + 256], zn[:, lo:lo + 256]], axis=0).astype(F32)

        mid = slice(HALO, HALO + t)
        bg, c2, xv, sg = ext(BG), ext(C2), ext(XV), ext(SG)
        du = jnp.concatenate([jnp.zeros((HALO, 256), F32), dc_ref[...], dun], axis=0)
        e = c2 * xv
        shifted = [_sh(e, SC_K - 1 - k) for k in range(SC_K)]
        f = sum(sw_ref[k:k + 1, :] * shifted[k] for k in range(SC_K))
        gate = _silu(sg)
        df = du * gate * bg
        de = sum(sw_ref[k:k + 1, :] * _ash(df, SC_K - 1 - k) for k in range(SC_K))
        dbg = du * gate * f
        dsg = du * bg * f * _silu_grad(sg)
        dz_ref[...] = jnp.concatenate([dbg[mid], (de * xv)[mid], (de * c2)[mid], dsg[mid]], axis=1).astype(BF16)
        dw = jnp.concatenate([jnp.sum((df * shifted[k])[mid], axis=0, keepdims=True) for k in range(SC_K)] + [
            jnp.zeros((8 - SC_K, 256), F32)], axis=0)
        _accumulate(i, dw_ref, dw)

    return pl.pallas_call(
        body, name="shortconv_bwd", grid=(n_steps,), in_specs=[cur(ZB), prev(ZB), nxt(ZB), cur(256), nxt(256), layer((8, 256)), ANY],
        out_specs=[cur(1024, BG // 1024), full((8, 256))],
        out_shape=[jax.ShapeDtypeStruct((rows, ZB), BF16), jax.ShapeDtypeStruct((8, 256), F32)],
        input_output_aliases={6: 0}, compiler_params=_params(("arbitrary",)),
    )(z_br, z_br, z_br, dud, dud, sc_w, dz_buf)


def conformer_bwd_tail(z_br, duc, conf_w, conf_vec, dz_buf, li):
    rows = z_br.shape[0]
    t = ROW_TILE
    cur, prev, _, full, layer = _tile_specs(t, rows // HALO, li)

    def body(zc_ref, zp_ref, du_ref, cw_ref, cv_ref, _, dc_ref, dcg_ref, dv_ref):
        i = pl.program_id(0)
        zp = jnp.where(i == 0, jnp.zeros(zp_ref.shape, zp_ref.dtype), zp_ref[...])

        def ext(lo):
            return jnp.concatenate([zp[:, lo:lo + 256], zc_ref[:, lo:lo + 256]], axis=0).astype(F32)

        g1 = ext(CA) * _sigmoid(ext(CGT))
        c = _conf_conv(g1, cw_ref)[HALO:] + cv_ref[0:1, :]
        _, vjp = jax.vjp(_conf_tail, c, zc_ref[:, CG:CG + 256].astype(F32), cv_ref[1:2, :], cv_ref[2:3, :])
        dc, dcg, dlg, dlb = vjp(du_ref[...])
        dc_ref[...] = dc
        dcg_ref[...] = dcg.astype(BF16)
        dvec = jnp.concatenate([dlg, dlb, jnp.sum(dc, axis=0, keepdims=True), jnp.zeros((5, 256), F32)], axis=0)
        _accumulate(i, dv_ref, dvec)

    return pl.pallas_call(
        body, name="conformer_bwd_tail", grid=(rows // t,), in_specs=[cur(ZB), prev(ZB), cur(256), layer((32, 256)), layer((8, 256)), ANY],
        out_specs=[cur(256), cur(256, CG // 256), full((8, 256))],
        out_shape=[jax.ShapeDtypeStruct((rows, 256), F32), jax.ShapeDtypeStruct((rows, ZB), BF16), jax.ShapeDtypeStruct((8, 256), F32)],
        input_output_aliases={5: 1}, compiler_params=_params(("arbitrary",)),
    )(z_br, z_br, duc, conf_w, conf_vec, dz_buf)


def conformer_bwd_conv(z_br, dc, conf_w, dz_buf, li):
    rows = z_br.shape[0]
    t = ROW_TILE
    n_steps = rows // t
    cur, prev, nxt, full, layer = _tile_specs(t, rows // HALO, li)

    def body(zc_ref, zp_ref, dc_ref, dn_ref, cw_ref, _, dz_ref, dw_ref):
        i = pl.program_id(0)
        zp = jnp.where(i == 0, jnp.zeros(zp_ref.shape, zp_ref.dtype), zp_ref[...])
        dcn = jnp.where(i == n_steps - 1, jnp.zeros(dn_ref.shape, dn_ref.dtype), dn_ref[...])

        def ext(lo):
            return jnp.concatenate([zp[:, lo:lo + 256], zc_ref[:, lo:lo + 256]], axis=0).astype(F32)

        a, gt = ext(CA), ext(CGT)
        sg = _sigmoid(gt)
        g1 = a * sg
        dc = dc_ref[...]
        dce = jnp.concatenate([dc, dcn], axis=0)
        dg1 = jnp.zeros_like(dce)
        dws = []
        for k in range(CONF_K):
            dg1 = dg1 + cw_ref[k:k + 1, :] * _ash(dce, CONF_K - 1 - k)
            dws.append(jnp.sum(dc * _sh(g1, CONF_K - 1 - k)[HALO:], axis=0, keepdims=True))
        dg1 = dg1[:t]
        ac, sc = a[HALO:], sg[HALO:]
        dz_ref[...] = jnp.concatenate([dg1 * sc, dg1 * ac * sc * (1.0 - sc)], axis=1).astype(BF16)
        _accumulate(i, dw_ref, jnp.concatenate(dws + [jnp.zeros((32 - CONF_K, 256), F32)], axis=0))

    return pl.pallas_call(
        body, name="conformer_bwd_conv", grid=(n_steps,), in_specs=[cur(ZB), prev(ZB), cur(256), nxt(256), layer((32, 256)), ANY],
        out_specs=[cur(512, CA // 512), full((32, 256))],
        out_shape=[jax.ShapeDtypeStruct((rows, ZB), BF16), jax.ShapeDtypeStruct((32, 256), F32)],
        input_output_aliases={5: 0}, compiler_params=_params(("arbitrary",)),
    )(z_br, z_br, dc, dc, conf_w, dz_buf)


def attention_bwd_prep(dub, o_att, z_br, dz_buf):
    rows = dub.shape[0]
    t = ROW_TILE
    cur, _, _, _, _ = _tile_specs(t, rows // HALO)

    def body(du_ref, o_ref, mg_ref, _, do_ref, dmg_ref, delta_ref):
        du, o, mg = du_ref[...], o_ref[...].astype(F32), mg_ref[...].astype(F32)
        do = du * _silu(mg)
        do_ref[...] = do.astype(BF16)
        dmg_ref[...] = (du * o * _silu_grad(mg)).astype(BF16)
        prod = do * o
        lane = lax.broadcasted_iota(jnp.int32, (1, HEADS * V_DIM), 1)
        for h in range(HEADS):
            part = jnp.where((lane >= V_DIM * h) & (lane < V_DIM * (h + 1)), prod, 0.0)
            delta_ref[h] = jnp.broadcast_to(jnp.sum(part, axis=-1, keepdims=True), (t, LANES))

    return pl.pallas_call(
        body, name="attention_bwd_prep", grid=(rows // t,), in_specs=[cur(512), cur(512), cur(512, MG // 512), ANY],
        out_specs=[cur(512), cur(512, MG // 512), pl.BlockSpec((HEADS, t, LANES), lambda i: (0, i, 0))],
        out_shape=[jax.ShapeDtypeStruct((rows, 512), BF16), jax.ShapeDtypeStruct((rows, ZB), BF16),
                   jax.ShapeDtypeStruct((HEADS, rows, LANES), F32)],
        input_output_aliases={3: 1}, compiler_params=_params(("parallel",)),
    )(dub, o_att, z_br, dz_buf)


def attention_bwd(q, k, v, do, lse, delta, exchange=None):
    rows = q.shape[0]
    tq = ROW_TILE
    nq = rows // tq
    n = 0 if exchange is None else len(exchange[0])

    def body(*refs):
        if n:
            start, finish = _exchange_ops(refs[6:6 + n], refs[9 + 2 * n:9 + 3 * n], refs[9 + 3 * n:], exchange[2])
            pl.when((pl.program_id(0) == 0) & (pl.program_id(1) == 0))(start)
        compute(*refs[:6], *refs[6 + 2 * n:9 + 2 * n])
        if n:
            pl.when((pl.program_id(0) == HEADS // 2 - 1) & (pl.program_id(1) == nq - 1))(finish)

    def compute(q_ref, k_ref, v_ref, do_ref, lse_ref, dl_ref, dq_ref, dk_ref, dv_ref):
        j = pl.program_id(1)

        @pl.when(j == 0)
        def _():
            dq_ref[...] = jnp.zeros_like(dq_ref)

        lane_k = lax.broadcasted_iota(jnp.int32, (1, 2 * HEAD_PAD), 1)
        lane_v = lax.broadcasted_iota(jnp.int32, (1, 2 * V_DIM), 1)
        zero = jnp.zeros((), BF16)
        kp, vp = k_ref[...], v_ref[...]
        kbd = jnp.concatenate([jnp.where(lane_k < HEAD_PAD, kp, zero), jnp.where(lane_k >= HEAD_PAD, kp, zero)], axis=0)
        vbd = jnp.concatenate([jnp.where(lane_v < V_DIM, vp, zero), jnp.where(lane_v >= V_DIM, vp, zero)], axis=0)

        def step(tile, n_tiles, carry, diagonal):
            dk, dv = carry
            r0, width = pl.multiple_of(tile * tq, tq), n_tiles * tq
            qi, doi = q_ref[pl.ds(r0, width), :], do_ref[pl.ds(r0, width), :]
            head1 = lax.broadcasted_iota(jnp.int32, (1, 2 * tq), 1) >= tq

            def per_head(ref):
                return jnp.where(head1, ref[1, pl.ds(r0, width), :][:, 0:1], ref[0, pl.ds(r0, width), :][:, 0:1])

            s = _dot_nt(qi, kbd)
            if diagonal:
                colm = lax.broadcasted_iota(jnp.int32, (tq, 2 * tq), 1)
                s = jnp.where(jnp.where(colm >= tq, colm - tq, colm) <= lax.broadcasted_iota(jnp.int32, (tq, 2 * tq), 0), s, -1e30)
            pr = jnp.exp(s - per_head(lse_ref))
            dv = dv + _dot_tn(pr.astype(BF16), doi)
            ds = (pr * (_dot_nt(doi, vbd) - per_head(dl_ref))).astype(BF16)
            dq_ref[pl.ds(r0, width), :] += _dot(ds, kbd)
            return dk + _dot_tn(ds, qi), dv

        carry = step(j, 1, (jnp.zeros((2 * tq, 2 * HEAD_PAD), F32), jnp.zeros((2 * tq, 2 * V_DIM), F32)), True)
        odd = (nq - 1 - j) % 2
        carry = lax.cond(odd == 1, lambda cr: step(j + 1, 1, cr, False), lambda cr: cr, carry)
        dk, dv = lax.fori_loop(0, (nq - 1 - j) // 2, lambda t, cr: step(j + 1 + odd + 2 * t, 2, cr, False), carry)
        dk_ref[:, 0:HEAD_PAD] = dk[0:tq, 0:HEAD_PAD]
        dk_ref[:, HEAD_PAD:2 * HEAD_PAD] = dk[tq:2 * tq, HEAD_PAD:2 * HEAD_PAD]
        dv_ref[...] = jnp.where(lane_v < V_DIM, dv[0:tq], dv[tq:2 * tq])

    srcs, dsts = ([], []) if exchange is None else (list(exchange[0]), list(exchange[1]))
    outs = pl.pallas_call(
        body, name="attention_bwd" if exchange is None else "attention_bwd_exchange", grid=(HEADS // 2, nq),
        in_specs=[pl.BlockSpec((rows, 2 * HEAD_PAD), lambda p, j: (0, p)), pl.BlockSpec((tq, 2 * HEAD_PAD), lambda p, j: (j, p)),
                  pl.BlockSpec((tq, 2 * V_DIM), lambda p, j: (j, p)), pl.BlockSpec((rows, 2 * V_DIM), lambda p, j: (0, p)),
                  pl.BlockSpec((2, rows, LANES), lambda p, j: (p, 0, 0)), pl.BlockSpec((2, rows, LANES), lambda p, j: (p, 0, 0))] + [
                      ANY] * (2 * n),
        out_specs=[pl.BlockSpec((rows, 2 * HEAD_PAD), lambda p, j: (0, p)), pl.BlockSpec((tq, 2 * HEAD_PAD), lambda p, j: (j, p)),
                   pl.BlockSpec((tq, 2 * V_DIM), lambda p, j: (j, p))] + [ANY] * n,
        out_shape=[jax.ShapeDtypeStruct((rows, HEADS * HEAD_PAD), F32), jax.ShapeDtypeStruct((rows, HEADS * HEAD_PAD), F32),
                   jax.ShapeDtypeStruct((rows, HEADS * V_DIM), F32)] + [jax.ShapeDtypeStruct(d.shape, d.dtype) for d in dsts],
        input_output_aliases={6 + n + a: 3 + a for a in range(n)}, scratch_shapes=EXCHANGE_SEMS(n) if n else [],
        compiler_params=_params(("arbitrary", "arbitrary") if n else ("parallel", "arbitrary")),
    )(q, k, v, do, lse, delta, *srcs, *dsts)
    return outs[0], outs[1], outs[2], list(outs[3:])


def mla_prep_bwd(dq, dk, dv, z_br, rope, gq, wuq, gkv, wukv, dz_buf, li):
    rows = dq.shape[0]
    t = ROW_TILE
    cur, _, _, full, layer = _tile_specs(t, rows // HALO, li)
    w8 = HEADS * HEAD_PAD

    def body(dq_ref, dk_ref, dv_ref, z_ref, rope_ref, gq_ref, wuq_ref, gkv_ref, wukv_ref, _, dz_ref, dwuq_ref, dwukv_ref, dgq_ref, dgkv_ref):
        i = pl.program_id(0)
        cth, s1, s2 = rope_ref[:, 0:128], rope_ref[:, 128:256], rope_ref[:, 256:384]
        dqb = _rope_transposed(dq_ref[...] * Q_SCALE, _lanes8(cth), _lanes8(s1), _lanes8(s2), w8).astype(BF16)
        cq = z_ref[:, 0:256].astype(F32)
        qn, vjp_q = jax.vjp(_rms, cq, gq_ref[...])
        _accumulate(i, dwuq_ref, _dot_tn(qn.astype(BF16), dqb))
        dcq, dgq = vjp_q(_dot_nt(dqb, wuq_ref[...]))
        _accumulate(i, dgq_ref, dgq)

        dk = dk_ref[...]
        dkr = sum(dk[:, HEAD_PAD * h:HEAD_PAD * (h + 1)] for h in range(HEADS))
        dkr = _rope_transposed(dkr, cth, s1, s2, HEAD_PAD)
        lane = lax.broadcasted_iota(jnp.int32, (1, HEAD_PAD), 1)
        dkr = jnp.where((lane >= QK_NOPE) & (lane < QK_NOPE + QK_ROPE), dkr, 0.0)
        dkvb = jnp.concatenate([dk, dv_ref[...]], axis=1).astype(BF16)
        ckv = z_ref[:, 256:384].astype(F32)
        kvn, vjp_kv = jax.vjp(_rms, ckv, gkv_ref[...])
        _accumulate(i, dwukv_ref, _dot_tn(kvn.astype(BF16), dkvb))
        dckv, dgkv = vjp_kv(_dot_nt(dkvb, wukv_ref[...]))
        _accumulate(i, dgkv_ref, dgkv)
        dz_ref[...] = jnp.concatenate([dcq, dckv, dkr], axis=1).astype(BF16)

    return pl.pallas_call(
        body, name="mla_prep_bwd", grid=(rows // t,),
        in_specs=[cur(w8), cur(w8), cur(512), cur(512, CQ // 512), cur(384), layer((1, 256)), layer((256, w8), 0), layer((1, 128)),
                  layer((128, w8 + 512), 0), ANY],
        out_specs=[cur(512, CQ // 512), full((256, w8)), full((128, w8 + 512)), full((1, 256)), full((1, 128))],
        out_shape=[jax.ShapeDtypeStruct((rows, ZB), BF16), jax.ShapeDtypeStruct((256, w8), F32), jax.ShapeDtypeStruct((128, w8 + 512), F32),
                   jax.ShapeDtypeStruct((1, 256), F32), jax.ShapeDtypeStruct((1, 128), F32)],
        input_output_aliases={9: 0}, compiler_params=_params(("arbitrary",)),
    )(dq, dk, dv, z_br, rope, gq, wuq, gkv, wukv, dz_buf)


def prenorm_bwd(dz_br, w_br, dh_gl, hres, gpre, dh_next, li):
    rows, d = hres.shape
    t = ROW_TILE
    cur, _, _, full, layer = _tile_specs(t, rows // HALO, li)

    def body(dz_ref, w_ref, dp_ref, x_ref, g_ref, dn_ref, dx_ref, dg_ref):
        i = pl.program_id(0)
        dh = _dot_nt(dz_ref[...], w_ref[...]) + dp_ref[...]
        _, vjp = jax.vjp(_rms, x_ref[...], g_ref[...])
        dx, dg = vjp(dh)
        dx_ref[...] = dx + dn_ref[...]
        _accumulate(i, dg_ref, dg)

    return pl.pallas_call(
        body, name="prenorm_bwd", grid=(rows // t,), in_specs=[cur(ZB), layer((d, ZB), 0), cur(d), cur(d), layer((1, d)), cur(d)],
        out_specs=[cur(d), full((1, d))], out_shape=[jax.ShapeDtypeStruct((rows, d), F32), jax.ShapeDtypeStruct((1, d), F32)],
        compiler_params=_params(("arbitrary",)),
    )(dz_br, w_br, dh_gl, hres, gpre, dh_next)


def _mesh_position():
    return lax.axis_index("x"), lax.axis_index("y"), lax.axis_index("c")


def chip_exchange(src, gather, name):
    block = src.shape if gather else src.shape[1:]

    def body(src_ref, dst_ref, send_sems, recv_sems, local_sem):
        x, y, c = _mesh_position()
        me = 2 * x + y
        peers = ((1 - x, y), (x, 1 - y), (1 - x, 1 - y))

        def part(k):
            return src_ref if gather else src_ref.at[k]

        def copy(j, slot):
            px, py = peers[j]
            return pltpu.make_async_remote_copy(src_ref=part(2 * px + py), dst_ref=dst_ref.at[slot], send_sem=send_sems.at[j],
                                                recv_sem=recv_sems.at[j], device_id=(px, py, c), device_id_type=MESH)

        local = pltpu.make_async_copy(part(me), dst_ref.at[me], local_sem)
        local.start()
        sends = [copy(j, me) for j in range(3)]
        for cp in sends:
            cp.start()
        for j, (px, py) in enumerate(peers):
            copy(j, 2 * px + py).wait_recv()
        for cp in sends:
            cp.wait_send()
        local.wait()

    return pl.pallas_call(
        body, name=name, in_specs=[pl.BlockSpec(memory_space=pl.ANY)], out_specs=pl.BlockSpec(memory_space=pl.ANY),
        out_shape=jax.ShapeDtypeStruct((N_CHIPS,) + tuple(block), src.dtype),
        scratch_shapes=[pltpu.SemaphoreType.DMA((3,)), pltpu.SemaphoreType.DMA((3,)), pltpu.SemaphoreType.DMA(())],
    )(src)


def sibling_swap(src, name):
    def body(src_ref, dst_ref, send_sem, recv_sem):
        x, y, c = _mesh_position()
        cp = pltpu.make_async_remote_copy(src_ref=src_ref, dst_ref=dst_ref, send_sem=send_sem, recv_sem=recv_sem,
                                          device_id=(x, y, 1 - c), device_id_type=MESH)
        cp.start()
        cp.wait()

    return pl.pallas_call(
        body, name=name, in_specs=[pl.BlockSpec(memory_space=pl.ANY)], out_specs=pl.BlockSpec(memory_space=pl.ANY),
        out_shape=jax.ShapeDtypeStruct(src.shape, src.dtype),
        scratch_shapes=[pltpu.SemaphoreType.DMA(()), pltpu.SemaphoreType.DMA(())],
    )(src)


def _comm_call(body, name, n_in, out_shapes, n_sems):
    return pl.pallas_call(
        body, name=name, in_specs=[ANY] * n_in, out_specs=[ANY] * len(out_shapes), out_shape=out_shapes,
        scratch_shapes=[pltpu.SemaphoreType.DMA((n,)) for n in n_sems])


def _row_halves(c, rows):
    half = rows // 2
    return pl.ds(pl.multiple_of(c * half, 16), half), pl.ds(pl.multiple_of((1 - c) * half, 16), half)


def _peers():
    x, y, c = _mesh_position()
    return x, y, c, 2 * x + y, ((1 - x, y), (x, 1 - y), (1 - x, 1 - y))


def _gather_ops(src, dst, sems, layer, own_copy):
    ici_send, ici_recv, d2d_send, d2d_recv, own_sems = sems
    n = len(src)

    def fetch(a, j, slot):
        x, y, c, _, peers = _peers()
        px, py = peers[j]
        mine, _ = _row_halves(c, src[a].shape[1])
        return pltpu.make_async_remote_copy(src_ref=src[a].at[layer, mine], dst_ref=dst[a].at[layer, slot, mine], send_sem=ici_send.at[3 * a + j],
                                            recv_sem=ici_recv.at[3 * a + j], device_id=(px, py, c), device_id_type=MESH)

    def forward(a, j, sibling_half):
        x, y, c, _, peers = _peers()
        px, py = peers[j]
        part = dst[a].at[layer, 2 * px + py, _row_halves(c, src[a].shape[1])[1 if sibling_half else 0]]
        return pltpu.make_async_remote_copy(src_ref=part, dst_ref=part, send_sem=d2d_send.at[3 * a + j], recv_sem=d2d_recv.at[3 * a + j],
                                            device_id=(x, y, 1 - c), device_id_type=MESH)

    def own(a):
        return pltpu.make_async_copy(src[a].at[layer], dst[a].at[layer, _peers()[3]], own_sems.at[a])

    def start():
        me = _peers()[3]
        for a in range(n):
            if own_copy:
                own(a).start()
            for j in range(3):
                fetch(a, j, me).start()

    def finish():
        peers = _peers()[4]
        for j, (px, py) in enumerate(peers):
            for a in range(n):
                fetch(a, j, 2 * px + py).wait_recv()
                forward(a, j, False).start()
        for j in range(3):
            for a in range(n):
                forward(a, j, True).wait_recv()
        for j in range(3):
            for a in range(n):
                fetch(a, j, 0).wait_send()
                forward(a, j, False).wait_send()
        if own_copy:
            for a in range(n):
                own(a).wait()

    return start, finish


def _exchange_ops(src, dst, sems, layer):
    send_sems, recv_sems, own_sems = sems
    n = len(src)

    def copy(a, j, slot):
        x, y, c, _, peers = _peers()
        px, py = peers[j]
        return pltpu.make_async_remote_copy(src_ref=src[a].at[2 * px + py], dst_ref=dst[a].at[layer, slot], send_sem=send_sems.at[3 * a + j],
                                            recv_sem=recv_sems.at[3 * a + j], device_id=(px, py, c), device_id_type=MESH)

    def own(a):
        me = _peers()[3]
        return pltpu.make_async_copy(src[a].at[me], dst[a].at[layer, me], own_sems.at[a])

    def start():
        me = _peers()[3]
        for a in range(n):
            own(a).start()
            for j in range(3):
                copy(a, j, me).start()

    def finish():
        peers = _peers()[4]
        for j, (px, py) in enumerate(peers):
            for a in range(n):
                copy(a, j, 2 * px + py).wait_recv()
        for j in range(3):
            for a in range(n):
                copy(a, j, 0).wait_send()
        for a in range(n):
            own(a).wait()

    return start, finish


GATHER_SEMS = lambda n: [pltpu.SemaphoreType.DMA((3 * n,))] * 4 + [pltpu.SemaphoreType.DMA((n,))]
EXCHANGE_SEMS = lambda n: [pltpu.SemaphoreType.DMA((3 * n,))] * 2 + [pltpu.SemaphoreType.DMA((n,))]


def gather_layer(srcs, dsts, layer, name):
    n = len(srcs)

    def body(*refs):
        start, finish = _gather_ops(refs[:n], refs[2 * n:3 * n], refs[3 * n:], layer, False)
        start()
        finish()

    return pl.pallas_call(
        body, name=name, in_specs=[ANY] * (2 * n), out_specs=[ANY] * n, out_shape=[jax.ShapeDtypeStruct(d.shape, d.dtype) for d in dsts],
        input_output_aliases={n + a: a for a in range(n)}, scratch_shapes=GATHER_SEMS(n),
    )(*srcs, *dsts)


def exchange_layer(ss, dsts, layer, name):
    n = len(ss)

    def body(*refs):
        start, finish = _exchange_ops(refs[:n], refs[2 * n:3 * n], refs[3 * n:], layer)
        start()
        finish()

    return pl.pallas_call(
        body, name=name, in_specs=[ANY] * (2 * n), out_specs=[ANY] * n, out_shape=[jax.ShapeDtypeStruct(d.shape, d.dtype) for d in dsts],
        input_output_aliases={n + a: a for a in range(n)}, scratch_shapes=EXCHANGE_SEMS(n),
    )(*ss, *dsts)


def swap_row_halves(ps, name):
    n = len(ps)

    def body(*refs):
        src, dst = refs[:n], refs[n:2 * n]
        send_sems, recv_sems = refs[2 * n:]
        x, y, c = _mesh_position()
        copies = [pltpu.make_async_remote_copy(src_ref=src[a].at[:, _row_halves(c, src[a].shape[1])[1]], dst_ref=dst[a], send_sem=send_sems.at[a],
                                               recv_sem=recv_sems.at[a], device_id=(x, y, 1 - c), device_id_type=MESH) for a in range(n)]
        for cp in copies:
            cp.start()
        for cp in copies:
            cp.wait()

    outs = [jax.ShapeDtypeStruct((p.shape[0], p.shape[1] // 2, p.shape[2]), p.dtype) for p in ps]
    return _comm_call(body, name, n, outs, (n, n))(*ps)


def add_row_half(p, r, c, name):
    n, half, cols = r.shape
    rb = _row_block(half, cols, 2)
    steps = half // rb

    def body(c_ref, p_ref, r_ref, o_ref):
        o_ref[...] = (p_ref[...].astype(F32) + r_ref[...].astype(F32)).astype(BF16)

    return pl.pallas_call(
        body, name=name, out_shape=jax.ShapeDtypeStruct(r.shape, BF16),
        grid_spec=pltpu.PrefetchScalarGridSpec(
            num_scalar_prefetch=1, grid=(n, steps),
            in_specs=[pl.BlockSpec((1, rb, cols), lambda k, i, c_ref: (k, c_ref[0] * steps + i, 0)),
                      pl.BlockSpec((1, rb, cols), lambda k, i, c_ref: (k, i, 0))],
            out_specs=pl.BlockSpec((1, rb, cols), lambda k, i, c_ref: (k, i, 0))),
        compiler_params=_params(("parallel", "parallel")),
    )(jnp.reshape(c, (1,)).astype(jnp.int32), p, r)


def sum_row_halves(l, c, name):
    layers, n, half, cols = l.shape
    rb = _row_block(half, cols, 4)
    steps = half // rb

    def body(c_ref, l_ref, o_ref):
        acc = l_ref[0, 0].astype(F32)
        for s in range(1, n):
            acc = acc + l_ref[0, s].astype(F32)
        o_ref[0] = acc

    return pl.pallas_call(
        body, name=name, out_shape=jax.ShapeDtypeStruct((layers, 2 * half, cols), F32),
        grid_spec=pltpu.PrefetchScalarGridSpec(
            num_scalar_prefetch=1, grid=(layers, steps), in_specs=[pl.BlockSpec((1, n, rb, cols), lambda a, i, c_ref: (a, 0, i, 0))],
            out_specs=pl.BlockSpec((1, rb, cols), lambda a, i, c_ref: (a, c_ref[0] * steps + i, 0))),
        compiler_params=_params(("parallel", "parallel")),
    )(jnp.reshape(c, (1,)).astype(jnp.int32), l)


def share_row_halves(gs, name):
    n = len(gs)

    def body(*refs):
        dst = refs[n:2 * n]
        send_sems, recv_sems = refs[2 * n:]
        x, y, c = _mesh_position()

        def copy(a, sibling_half):
            part = dst[a].at[:, _row_halves(c, dst[a].shape[1])[1 if sibling_half else 0]]
            return pltpu.make_async_remote_copy(src_ref=part, dst_ref=part, send_sem=send_sems.at[a], recv_sem=recv_sems.at[a],
                                                device_id=(x, y, 1 - c), device_id_type=MESH)

        for a in range(n):
            copy(a, False).start()
        for a in range(n):
            copy(a, True).wait_recv()
        for a in range(n):
            copy(a, False).wait_send()

    return pl.pallas_call(
        body, name=name, in_specs=[ANY] * n, out_specs=[ANY] * n, out_shape=[jax.ShapeDtypeStruct(g.shape, g.dtype) for g in gs],
        input_output_aliases={a: a for a in range(n)}, scratch_shapes=[pltpu.SemaphoreType.DMA((n,)), pltpu.SemaphoreType.DMA((n,))],
    )(*gs)


def _row_block(rows, cols, itemsize):
    best = 16
    for rb in range(16, rows + 1, 16):
        if rows % rb == 0 and rb * cols * itemsize <= 2 * 1024 * 1024:
            best = rb
    assert rows % best == 0, (rows, cols)
    return best


def _comm_block(rows):
    return 1024 if rows % 1024 == 0 else rows


def sum_slots(buf, name):
    n, r, c = buf.shape
    rb = _comm_block(r)

    def body(b_ref, o_ref):
        acc = b_ref[0].astype(F32)
        for s in range(1, n):
            acc = acc + b_ref[s].astype(F32)
        o_ref[...] = acc

    return pl.pallas_call(
        body, name=name, grid=(r // rb,), in_specs=[pl.BlockSpec((n, rb, c), lambda i: (0, i, 0))],
        out_specs=pl.BlockSpec((rb, c), lambda i: (i, 0)), out_shape=jax.ShapeDtypeStruct((r, c), F32),
        compiler_params=_params(("parallel",)),
    )(buf)


def add_pair(a, b, out_dtype, name):
    shape = a.shape
    a2, b2 = a.reshape(-1, shape[-1]), b.reshape(-1, shape[-1])
    r, c = a2.shape
    rb = _comm_block(r)

    def body(a_ref, b_ref, o_ref):
        o_ref[...] = (a_ref[...].astype(F32) + b_ref[...].astype(F32)).astype(out_dtype)

    out = pl.pallas_call(
        body, name=name, grid=(r // rb,), in_specs=[pl.BlockSpec((rb, c), lambda i: (i, 0))] * 2,
        out_specs=pl.BlockSpec((rb, c), lambda i: (i, 0)), out_shape=jax.ShapeDtypeStruct((r, c), out_dtype),
        compiler_params=_params(("parallel",)),
    )(a2, b2)
    return out.reshape(shape)


def adamw(w, g, m, v, echo_grad=False):
    shape = w.shape
    cols = shape[-1]
    rows = math.prod(shape[:-1])
    rb = rows if rows * cols <= 256 * 1024 else 256
    assert rows % rb == 0, shape
    n_out = 4 if echo_grad else 3

    def body(w_ref, g_ref, m_ref, v_ref, d_ref, nm_ref, nv_ref, *echo_ref):
        g_ = g_ref[...]
        for ref in echo_ref:
            ref[...] = g_
        nm = ADAM_B1 * m_ref[...] + (1.0 - ADAM_B1) * g_
        nv = ADAM_B2 * v_ref[...] + (1.0 - ADAM_B2) * (g_ * g_)
        m_hat = nm / (1.0 - ADAM_B1 ** ADAM_STEP)
        v_hat = nv / (1.0 - ADAM_B2 ** ADAM_STEP)
        d_ref[...] = -ADAM_LR * (m_hat / (jnp.sqrt(v_hat) + ADAM_EPS) + ADAM_WD * w_ref[...])
        nm_ref[...] = nm
        nv_ref[...] = nv

    if len(shape) == 3 and rb < rows and shape[1] % rb == 0:
        spec = pl.BlockSpec((1, rb, cols), lambda a, i: (a, i, 0))
        return tuple(pl.pallas_call(
            body, name="adamw", grid=(shape[0], shape[1] // rb), in_specs=[spec] * 4, out_specs=[spec] * n_out,
            out_shape=[jax.ShapeDtypeStruct(shape, F32)] * n_out, compiler_params=_params(("parallel", "parallel")),
        )(w, g, m, v))
    spec = pl.BlockSpec((rb, cols), lambda i: (i, 0))
    outs = pl.pallas_call(
        body, name="adamw", grid=(rows // rb,), in_specs=[spec] * 4, out_specs=[spec] * n_out,
        out_shape=[jax.ShapeDtypeStruct((rows, cols), F32)] * n_out, compiler_params=_params(("parallel",)),
    )(*(a.reshape(rows, cols) for a in (w, g, m, v)))
    return tuple(o.reshape(shape) for o in outs)


def _pack(arrays, dtype, row_multiple):
    flat = jnp.concatenate([a.astype(dtype).reshape(-1) for a in arrays])
    per = LANES * row_multiple
    total = -(-flat.shape[0] // per) * per
    return jnp.pad(flat, (0, total - flat.shape[0])).reshape(total // LANES, LANES)


def _unpack(buf, shapes):
    flat = buf.reshape(-1)
    out, off = [], 0
    for s in shapes:
        n = math.prod(s)
        out.append(flat[off:off + n].reshape(s))
        off += n
    return out


def _input_weights(blocks):
    c0, c1, c2, c3 = (blocks[..., k, :, :] for k in range(N_CHIPS))
    pad = lambda n: jnp.zeros(c0.shape[:-1] + (n,), blocks.dtype)
    w_br = jnp.concatenate([c1[..., 376:1400], c0[..., 0:896], pad(64), c0[..., 896:928], pad(32), c0[..., 928:], c1[..., 0:376]], axis=-1)
    return w_br, jnp.concatenate([c1[..., 1400:], c2, c3], axis=-1)


def _input_weights_inverse(dw_br, dw_gl):
    c0 = jnp.concatenate([dw_br[..., 1024:1920], dw_br[..., 1984:2016], dw_br[..., 2048:2952]], axis=-1)
    c1 = jnp.concatenate([dw_br[..., 2952:ZB], dw_br[..., 0:1024], dw_gl[..., 0:432]], axis=-1)
    return jnp.stack([c0, c1, dw_gl[..., 432:2264], dw_gl[..., 2264:]], axis=-3)


def _uq_layout(w):
    r = w.reshape(w.shape[:-1] + (HEADS, QK_NOPE + QK_ROPE))
    r = jnp.pad(r, [(0, 0)] * (r.ndim - 1) + [(0, HEAD_PAD - QK_NOPE - QK_ROPE)])
    return r.reshape(w.shape[:-1] + (HEADS * HEAD_PAD,))


def _uq_layout_inverse(dw):
    r = dw.reshape(dw.shape[:-1] + (HEADS, HEAD_PAD))[..., :QK_NOPE + QK_ROPE]
    return r.reshape(dw.shape[:-1] + (HEADS * (QK_NOPE + QK_ROPE),))


def _ukv_layout(w):
    r = w.reshape(w.shape[:-1] + (HEADS, QK_NOPE + V_DIM))
    kp = jnp.pad(r[..., :QK_NOPE], [(0, 0)] * (r.ndim - 1) + [(0, HEAD_PAD - QK_NOPE)]).reshape(w.shape[:-1] + (HEADS * HEAD_PAD,))
    return jnp.concatenate([kp, r[..., QK_NOPE:].reshape(w.shape[:-1] + (HEADS * V_DIM,))], axis=-1)


def _ukv_layout_inverse(dw):
    lead = dw.shape[:-1]
    dk = dw[..., :HEADS * HEAD_PAD].reshape(lead + (HEADS, HEAD_PAD))[..., :QK_NOPE]
    dv = dw[..., HEADS * HEAD_PAD:].reshape(lead + (HEADS, V_DIM))
    return jnp.concatenate([dk, dv], axis=-1).reshape(lead + (HEADS * (QK_NOPE + V_DIM),))


def _block_diag(pw):
    zeros = lambda n: jnp.zeros(pw.shape[:-3] + (64, n), pw.dtype)
    rows = [jnp.concatenate([zeros(64 * g), pw[..., g, :, :], zeros(64 * (3 - g))], axis=-1) for g in range(4)]
    return jnp.concatenate(rows, axis=-2)


def _block_diag_inverse(d):
    return jnp.stack([d[..., 64 * g:64 * (g + 1), 64 * g:64 * (g + 1)] for g in range(4)], axis=-3)


def _pad_rows(a, n):
    return jnp.pad(a, ((0, n - a.shape[0]), (0, 0)))


def _rope_tables(rows):
    inv = 1.0 / (ROPE_THETA ** (jnp.arange(0, QK_ROPE, 2, dtype=F32) / QK_ROPE))
    ang = jnp.arange(rows, dtype=F32)[:, None] * inv[None, :]
    cos, sin = jnp.cos(ang), jnp.sin(ang)
    one, zero = jnp.ones((rows, 1), F32), jnp.zeros((rows, 1), F32)
    rep = lambda a, n: jnp.broadcast_to(a, (rows, n))
    c = jnp.concatenate([rep(one, 64), cos, cos, rep(one, 32)], axis=1)
    s1 = jnp.concatenate([rep(zero, 64), -sin, rep(zero, 48)], axis=1)
    s2 = jnp.concatenate([rep(zero, 80), sin, rep(zero, 32)], axis=1)
    return jnp.concatenate([c, s1, s2], axis=1)


def _misc_block(parts):
    out = []
    for name, rows in MISC:
        a = parts[name]
        if name == "w_o":
            a = a.reshape(a.shape[:-2] + (rows, 256))
        elif name == "w_uq":
            a = jnp.pad(a, [(0, 0)] * (a.ndim - 1) + [(0, 256 - a.shape[-1])])
        out.append(a)
    return jnp.concatenate(out, axis=-2)


def _misc_unblock(block):
    out, off = {}, 0
    for name, rows in MISC:
        a = block[..., off:off + rows, :]
        off += rows
        if name == "w_o":
            a = a.reshape(a.shape[:-2] + (256, D_MODEL))
        elif name == "w_uq":
            a = a[..., :192]
        out[name] = a
    return out


def _to_chip_blocks(name, a):
    if name == "w_o":
        return a.reshape(a.shape[:-2] + (N_CHIPS, a.shape[-2] // N_CHIPS, a.shape[-1]))
    return jnp.swapaxes(a.reshape(a.shape[:-1] + (N_CHIPS, a.shape[-1] // N_CHIPS)), -3, -2)


def _from_chip_blocks(name, b):
    if name == "w_o":
        return b.reshape(b.shape[:-3] + (N_CHIPS * b.shape[-2], b.shape[-1]))
    s = jnp.swapaxes(b, -3, -2)
    return s.reshape(s.shape[:-2] + (N_CHIPS * s.shape[-1],))


LARGE = ("w_in",) + tuple(n for n, _ in MISC)


def gather_small(shards):
    small = chip_exchange(_pack([shards[n] for n, _, _ in SHARDED_SMALL], F32, 8), True, "gather_small_ici")
    per_chip = [_unpack(small[k], [s for _, s, _ in SHARDED_SMALL]) for k in range(N_CHIPS)]
    return {name: jnp.concatenate([per_chip[k][idx] for k in range(N_CHIPS)], axis=axis) for idx, (name, _, axis) in enumerate(SHARDED_SMALL)}


class LocalWeights:
    def __init__(self, full):
        self.full = full
        self.grads = [None] * DEPTH

    def layer(self, i):
        return {n: self.full[n][i] for n in LARGE}

    def gather_with_attention(self, i):
        return None

    def gathered(self, dsts):
        pass

    def exchange_with_attention(self):
        return None

    def exchanged(self, dsts):
        pass

    def put_grads(self, i, grads):
        self.grads[i] = grads

    def reduced(self):
        return {n: jnp.stack([g[n] for g in self.grads]) for n in LARGE}


class MeshWeights:
    def __init__(self, shards, c, chip):
        self.c, self.chip = c, chip
        self.srcs = [shards["w_in"].astype(BF16), _misc_block({n: shards[n] for n, _ in MISC}).astype(BF16)]
        dsts = [lax.empty((DEPTH, N_CHIPS) + s.shape[1:], BF16) for s in self.srcs]
        self.dsts = gather_layer(self.srcs, dsts, 0, "gather_layer")
        self.landed = [lax.empty((DEPTH, N_CHIPS, s.shape[1] // 2, s.shape[2]), BF16) for s in self.srcs]
        self.pending = None

    def layer(self, i):
        blocks = [d[i] for d in self.dsts]
        if i == 0:
            own = (jnp.arange(N_CHIPS) == self.chip)[:, None, None]
            blocks = [jnp.where(own, s[0][None], b) for s, b in zip(self.srcs, blocks)]
        out = {"w_in": blocks[0]}
        for name, b in _misc_unblock(blocks[1]).items():
            out[name] = _from_chip_blocks(name, b)
        return out

    def gather_with_attention(self, i):
        return (self.srcs, self.dsts, i + 1) if i + 1 < DEPTH else None

    def gathered(self, dsts):
        if dsts:
            self.dsts = dsts

    def exchange_with_attention(self):
        return None if self.pending is None else (self.pending[0], self.landed, self.pending[1])

    def exchanged(self, dsts):
        if dsts:
            self.landed, self.pending = dsts, None

    def put_grads(self, i, grads):
        ps = [grads["w_in"].astype(BF16), _misc_block({n: _to_chip_blocks(n, grads[n]) for n, _ in MISC}).astype(BF16)]
        rs = swap_row_halves(ps, "reduce_swap")
        self.pending = ([add_row_half(p, r, self.c, "reduce_pair_%d" % a) for a, (p, r) in enumerate(zip(ps, rs))], i)

    def reduced(self):
        landed = exchange_layer(self.pending[0], self.landed, self.pending[1], "reduce_exchange")
        gs = [sum_row_halves(l, self.c, "reduce_sum_%d" % a) for a, l in enumerate(landed)]
        g_in, g_misc = share_row_halves(gs, "reduce_share")
        out = {"w_in": g_in}
        out.update(_misc_unblock(g_misc))
        return out


def reduce_small(grads, chip):
    names = [n for n, _ in REPLICATED] + [n for n, _, _ in SHARDED_SMALL]
    buf = _pack([grads[n] for n in names], F32, 8)
    chip_sum = add_pair(buf, sibling_swap(buf, "reduce_small_d2d"), F32, "reduce_small_pair")
    total = sum_slots(chip_exchange(chip_sum, True, "reduce_small_ici"), "reduce_small_sum")
    out = dict(zip(names, _unpack(total, [grads[n].shape for n in names])))
    for name, shape, axis in SHARDED_SMALL:
        out[name] = lax.dynamic_slice_in_dim(out[name], chip * shape[axis], shape[axis], axis)
    return out


def _prepare_small(w):
    row = lambda a: a[:, None, :]
    conf_vec = jnp.concatenate([row(w["conf_dw_b"]), row(w["conf_ln_g"]), row(w["conf_ln_b"]), jnp.zeros((DEPTH, 5, 256), F32)], axis=1)
    return dict(
        gpre=row(w["pre_norm_g"]), bias=row(w["gate_bias"]), pwbd=_block_diag(w["pool_w"]).astype(BF16), pscale=row(w["pool_scale"]),
        gq=row(w["q_norm_g"]), gkv=row(w["kv_norm_g"]), conf_w=jnp.pad(w["conf_dw_w"].astype(F32), ((0, 0), (0, 32 - CONF_K), (0, 0))),
        conf_vec=conf_vec, sc_w=jnp.pad(w["sc_dw_w"].astype(F32), ((0, 0), (0, 8 - SC_K), (0, 0))), gpost=row(w["post_norm_g"]))


def _prepare_layer(large):
    w_br, w_gl = _input_weights(large["w_in"])
    one = lambda a: a.astype(BF16)[None]
    return dict(w_br=one(w_br), w_gl=one(w_gl), wuq=one(_uq_layout(large["w_uq"])), wukv=one(_ukv_layout(large["w_ukv"])),
                woa=one(large["w_out_pool"]), wob=one(large["w_out_mla"]), woc=one(large["w_out_conf"]), wod=one(large["w_out_sc"]),
                wo=one(large["w_o"]))


def local_step(x, target, w, large):
    seq = x.shape[0]
    length = N_META + seq
    rows = -(-length // ROW_TILE) * ROW_TILE
    bt = _big_tile(rows)
    hres = _pad_rows(jnp.concatenate([w["meta_tokens"].astype(F32), x], axis=0), rows)
    tgt = jnp.pad(target, ((N_META, rows - length), (0, 0)))
    rope = _rope_tables(rows)
    sw = _prepare_small(w)

    saved = []
    for i in range(DEPTH):
        lw = _prepare_layer(large.layer(i))
        z_br, hb = prenorm_project(hres, sw["gpre"], lw["w_br"], i)
        z_gl = matmul(hb, lw["w_gl"], "nn", BF16, bt, 1024, D_MODEL, "project_gates", b_layer=0)
        ua, uc, ud, q, k, v = branches_fwd(z_br, rope, sw["pwbd"], sw["pscale"], sw["gq"], lw["wuq"], sw["gkv"], lw["wukv"],
                                           sw["conf_w"], sw["conf_vec"], sw["sc_w"], i)
        o_att, lse, dsts = attention_fwd(q, k, v, large.gather_with_attention(i))
        large.gathered(dsts)
        ub, mb, o, hnew = merge_fwd(ua, o_att, uc, ud, z_br, z_gl, sw["bias"], lw["woa"], lw["wob"], lw["woc"], lw["wod"], lw["wo"],
                                    sw["gpost"], hres, i)
        saved.append(dict(lw=lw, hres=hres, hb=hb, z_br=z_br, z_gl=z_gl, ua=ua, ub=ub, uc=uc, ud=ud, q=q, k=k, v=v, o_att=o_att,
                          lse=lse, mb=mb, o=o))
        hres = hnew

    dh, total = loss_head(hres, tgt, seq)

    g = {n: [None] * DEPTH for n in ("gpre", "bias", "pwbd", "pscale", "gq", "gkv", "conf_w", "conf_vec", "sc_w", "gpost")}
    for i in reversed(range(DEPTH)):
        s = saved[i]
        lw = s["lw"]
        dm, dwo, g["gpost"][i] = postnorm_bwd(dh, s["o"], s["mb"], lw["wo"], sw["gpost"], i)
        dua, dub, duc, dud, dz_gl, dwa, dwb, dwc, dwd, g["bias"][i] = merge_bwd(
            dm, s["ua"], s["ub"], s["uc"], s["ud"], s["z_gl"], sw["bias"], lw["woa"], lw["wob"], lw["woc"], lw["wod"], i)
        dz_br = lax.empty((rows, ZB), BF16)
        dz_br, g["pwbd"][i], g["pscale"][i] = pool_bwd(s["z_br"], dua, sw["pwbd"], sw["pscale"], dz_br, i)
        dz_br, g["sc_w"][i] = shortconv_bwd(s["z_br"], dud, sw["sc_w"], dz_br, i)
        dc, dz_br, g["conf_vec"][i] = conformer_bwd_tail(s["z_br"], duc, sw["conf_w"], sw["conf_vec"], dz_br, i)
        dz_br, g["conf_w"][i] = conformer_bwd_conv(s["z_br"], dc, sw["conf_w"], dz_br, i)
        do, dz_br, delta = attention_bwd_prep(dub, s["o_att"], s["z_br"], dz_br)
        dq, dk, dv, dsts = attention_bwd(s["q"], s["k"], s["v"], do, s["lse"], delta, large.exchange_with_attention())
        large.exchanged(dsts)
        dz_br, dwuq, dwukv, g["gq"][i], g["gkv"][i] = mla_prep_bwd(dq, dk, dv, s["z_br"], rope, sw["gq"], lw["wuq"], sw["gkv"],
                                                                 lw["wukv"], dz_br, i)
        dw_br = matmul(s["hb"], dz_br, "tn", F32, D_MODEL, ZB // 2, bt, "grad_w_branch")
        dw_gl = matmul(s["hb"], dz_gl, "tn", F32, D_MODEL, 1024, bt, "grad_w_gates")
        dh_gl = matmul(dz_gl, lw["w_gl"], "nt", F32, bt, D_MODEL, 1024, "grad_h_gates", b_layer=0)
        dh, g["gpre"][i] = prenorm_bwd(dz_br, lw["w_br"], dh_gl, s["hres"], sw["gpre"], dh, i)
        large.put_grads(i, dict(w_in=_input_weights_inverse(dw_br, dw_gl), w_out_pool=dwa, w_uq=_uq_layout_inverse(dwuq),
                                w_ukv=_ukv_layout_inverse(dwukv), w_out_mla=dwb, w_out_conf=dwc, w_out_sc=dwd, w_o=dwo))

    g = {n: jnp.stack(parts) for n, parts in g.items()}
    grads = dict(
        meta_tokens=dh[:N_META], pre_norm_g=g["gpre"][:, 0], gate_bias=g["bias"][:, 0], pool_w=_block_diag_inverse(g["pwbd"]),
        pool_scale=g["pscale"][:, 0], q_norm_g=g["gq"][:, 0], kv_norm_g=g["gkv"][:, 0], conf_dw_w=g["conf_w"][:, :CONF_K],
        conf_dw_b=g["conf_vec"][:, 2], conf_ln_g=g["conf_vec"][:, 0], conf_ln_b=g["conf_vec"][:, 1], sc_dw_w=g["sc_w"][:, :SC_K],
        post_norm_g=g["gpost"][:, 0])
    return total[0, 0], dh[N_META:length], grads


def kernel(x, meta_tokens, pre_norm_g, w_in, gate_bias, pool_w, pool_scale, w_out_pool, q_norm_g, w_uq, kv_norm_g, w_ukv, w_out_mla, conf_dw_w, conf_dw_b, conf_ln_g, conf_ln_b, w_out_conf, sc_dw_w, w_out_sc, w_o, post_norm_g, loss_target, m_meta_tokens, m_pre_norm_g, m_w_in, m_gate_bias, m_pool_w, m_pool_scale, m_w_out_pool, m_q_norm_g, m_w_uq, m_kv_norm_g, m_w_ukv, m_w_out_mla, m_conf_dw_w, m_conf_dw_b, m_conf_ln_g, m_conf_ln_b, m_w_out_conf, m_sc_dw_w, m_w_out_sc, m_w_o, m_post_norm_g, v_meta_tokens, v_pre_norm_g, v_w_in, v_gate_bias, v_pool_w, v_pool_scale, v_w_out_pool, v_q_norm_g, v_w_uq, v_kv_norm_g, v_w_ukv, v_w_out_mla, v_conf_dw_w, v_conf_dw_b, v_conf_ln_g, v_conf_ln_b, v_w_out_conf, v_sc_dw_w, v_w_out_sc, v_w_o, v_post_norm_g):
    args = locals()
    weights = {n: args[n] for n in WEIGHT_ORDER}
    c = lax.axis_index("c")
    chip = 2 * lax.axis_index("x") + lax.axis_index("y")

    small = {n: weights[n] for n, _ in REPLICATED}
    small.update(gather_small(weights))
    large = MeshWeights(weights, c, chip)
    total, dx, grads = local_step(x[0], loss_target[0], small, large)
    loss = lax.psum(total * (0.5 / D_MODEL), ("x", "y", "c"))

    reduced = large.reduced()
    reduced.update(reduce_small(grads, chip))

    deltas, new_m, new_v = [], [], []
    for n in WEIGHT_ORDER:
        d, nm, nv, *echo = adamw(weights[n], reduced[n], args["m_" + n], args["v_" + n], echo_grad=(n == "w_in"))
        if echo:
            reduced[n] = echo[0]
        deltas.append(d)
        new_m.append(nm)
        new_v.append(nv)
    return (loss, dx[None], *[reduced[n] for n in WEIGHT_ORDER], *deltas, *new_m, *new_v)
```

```python
import functools
import math

import jax
import jax.numpy as jnp
from jax import lax
from jax.experimental import pallas as pl
from jax.experimental.pallas import tpu as pltpu

F32 = jnp.float32
BF16 = jnp.bfloat16

D_MODEL = 1024
DEPTH = 4
N_META = 16
EPS = 1e-6
HEADS = 8
QK_NOPE = 64
QK_ROPE = 32
V_DIM = 64
HEAD_PAD = 128
ROPE_THETA = 10000.0
Q_SCALE = (QK_NOPE + QK_ROPE) ** -0.5
CONF_K = 31
SC_K = 3
IN_W = 7328
N_CHIPS = 4

ZB = 3328
ZG = 4096
BG, C2, XV, SG, PV, PG, CQ, CKV, KR, MG, CA, CGT, CG = (0, 256, 512, 768, 1024, 1280, 1536, 1792, 1920, 2048, 2560, 2816, 3072)

ROW_TILE = 384
HALO = 32
LANES = 128
VMEM_LIMIT = 56 * 1024 * 1024

ADAM_LR = 0.001
ADAM_B1 = 0.9
ADAM_B2 = 0.999
ADAM_EPS = 1e-08
ADAM_WD = 0.01
ADAM_STEP = 10

MESH = pl.DeviceIdType.MESH
ANY = pl.BlockSpec(memory_space=pl.ANY)

MISC = (
    ("w_out_pool", 256), ("w_ukv", 128), ("w_out_mla", 512), ("w_out_conf", 256), ("w_out_sc", 256), ("w_o", 1024), ("w_uq", 256))
SHARDED_SMALL = (
    ("meta_tokens", (N_META, 256), 1),
    ("conf_dw_w", (DEPTH, CONF_K, 64), 2),
    ("sc_dw_w", (DEPTH, SC_K, 64), 2),
)
REPLICATED = (
    ("pre_norm_g", (DEPTH, D_MODEL)),
    ("gate_bias", (DEPTH, 4 * D_MODEL)),
    ("pool_w", (DEPTH, 4, 64, 64)),
    ("pool_scale", (DEPTH, 256)),
    ("q_norm_g", (DEPTH, 256)),
    ("kv_norm_g", (DEPTH, 128)),
    ("conf_dw_b", (DEPTH, 256)),
    ("conf_ln_g", (DEPTH, 256)),
    ("conf_ln_b", (DEPTH, 256)),
    ("post_norm_g", (DEPTH, D_MODEL)),
)
WEIGHT_ORDER = ("meta_tokens", "pre_norm_g", "w_in", "gate_bias", "pool_w", "pool_scale", "w_out_pool", "q_norm_g", "w_uq",
                "kv_norm_g", "w_ukv", "w_out_mla", "conf_dw_w", "conf_dw_b", "conf_ln_g", "conf_ln_b", "w_out_conf", "sc_dw_w",
                "w_out_sc", "w_o", "post_norm_g")


def _dot(a, b):
    return lax.dot_general(a, b, (((1,), (0,)), ((), ())), preferred_element_type=F32)


def _dot_nt(a, b):
    return lax.dot_general(a, b, (((1,), (1,)), ((), ())), preferred_element_type=F32)


def _dot_tn(a, b):
    return lax.dot_general(a, b, (((0,), (0,)), ((), ())), preferred_element_type=F32)


def _sigmoid(x):
    return jax.nn.sigmoid(x)


def _silu(x):
    return x * _sigmoid(x)


def _silu_grad(x):
    s = _sigmoid(x)
    return s * (1.0 + x * (1.0 - s))


def _rms(x, g):
    return x * lax.rsqrt(jnp.mean(x * x, axis=-1, keepdims=True) + EPS) * g


def _sh(x, d):
    return x if d == 0 else pltpu.roll(x, d, 0)


def _ash(x, d):
    return x if d == 0 else pltpu.roll(x, x.shape[0] - d, 0)


def _lanes8(t):
    return jnp.concatenate([t] * HEADS, axis=1)


def _pool_window_sums(v, shift):
    a2 = v + shift(v, 1)
    a4 = a2 + shift(a2, 2)
    a8 = a4 + shift(a4, 4)
    a16 = a8 + shift(a8, 8)
    lane = lax.broadcasted_iota(jnp.int32, v.shape, 1)
    return jnp.where(lane < 64, a2, jnp.where(lane < 128, a4, jnp.where(lane < 192, a8, a16)))


def _pool_counts(first_row, rows):
    pos = first_row + lax.broadcasted_iota(jnp.int32, (rows, 256), 0)
    lane = lax.broadcasted_iota(jnp.int32, (rows, 256), 1)
    width = jnp.where(lane < 64, 2, jnp.where(lane < 128, 4, jnp.where(lane < 192, 8, 16)))
    return jnp.maximum(jnp.minimum(pos + 1, width), 1).astype(F32)


def _params(sem=None):
    return pltpu.CompilerParams(dimension_semantics=sem, vmem_limit_bytes=VMEM_LIMIT)


def _tile_specs(t, n_halo_blocks, li=0):
    per = t // HALO

    def layer(shape, idx=li):
        return pl.BlockSpec((None,) + tuple(shape), lambda i: (idx,) + (0,) * len(shape))

    def cur(c, cb=0):
        return pl.BlockSpec((t, c), lambda i: (i, cb))

    def prev(c, cb=0):
        return pl.BlockSpec((HALO, c), lambda i: (jnp.maximum(i * per - 1, 0), cb))

    def nxt(c, cb=0):
        return pl.BlockSpec((HALO, c), lambda i: (jnp.minimum((i + 1) * per, n_halo_blocks - 1), cb))

    def full(shape):
        return pl.BlockSpec(shape, lambda i: (0,) * len(shape))

    return cur, prev, nxt, full, layer


def _big_tile(rows):
    return rows // 3 if rows % (3 * LANES) == 0 else ROW_TILE


def matmul(a, b, mode, out_dtype, tm, tn, tk, name, b_layer=None):
    bs = b.shape if b_layer is None else b.shape[1:]
    lead = () if b_layer is None else (None,)
    pick = (lambda *ix: ix) if b_layer is None else (lambda *ix: (b_layer,) + ix)
    if mode == "nn":
        (m, k), n = a.shape, bs[1]
        a_spec = pl.BlockSpec((tm, tk), lambda i, j, kk: (i, kk))
        b_spec = pl.BlockSpec(lead + (tk, tn), lambda i, j, kk: pick(kk, j))
        dot = _dot
    elif mode == "nt":
        (m, k), n = a.shape, bs[0]
        a_spec = pl.BlockSpec((tm, tk), lambda i, j, kk: (i, kk))
        b_spec = pl.BlockSpec(lead + (tn, tk), lambda i, j, kk: pick(j, kk))
        dot = _dot_nt
    else:
        (k, m), n = a.shape, bs[1]
        a_spec = pl.BlockSpec((tk, tm), lambda i, j, kk: (kk, i))
        b_spec = pl.BlockSpec(lead + (tk, tn), lambda i, j, kk: pick(kk, j))
        dot = _dot_tn
    assert m % tm == 0 and n % tn == 0 and k % tk == 0, (a.shape, bs, tm, tn, tk)
    nk = k // tk

    def body(a_ref, b_ref, o_ref, acc_ref):
        kk = pl.program_id(2)

        @pl.when(kk == 0)
        def _():
            acc_ref[...] = jnp.zeros_like(acc_ref)

        acc_ref[...] += dot(a_ref[...], b_ref[...])

        @pl.when(kk == nk - 1)
        def _():
            o_ref[...] = acc_ref[...].astype(out_dtype)

    return pl.pallas_call(
        body, name=name, grid=(m // tm, n // tn, nk), in_specs=[a_spec, b_spec],
        out_specs=pl.BlockSpec((tm, tn), lambda i, j, kk: (i, j)), out_shape=jax.ShapeDtypeStruct((m, n), out_dtype),
        scratch_shapes=[pltpu.VMEM((tm, tn), F32)], compiler_params=_params(("parallel", "parallel", "arbitrary")),
    )(a, b)


def prenorm_project(hres, g, w, li):
    rows, d = hres.shape
    n = w.shape[2]
    tm, tn = _big_tile(rows), n // 2

    def body(x_ref, g_ref, w_ref, z_ref, hb_ref):
        @pl.when(pl.program_id(1) == 0)
        def _():
            hb_ref[...] = _rms(x_ref[...], g_ref[...]).astype(BF16)

        z_ref[...] = _dot(hb_ref[...], w_ref[...]).astype(BF16)

    return pl.pallas_call(
        body, name="prenorm_project", grid=(rows // tm, n // tn),
        in_specs=[pl.BlockSpec((tm, d), lambda i, j: (i, 0)), pl.BlockSpec((None, 1, d), lambda i, j: (li, 0, 0)),
                  pl.BlockSpec((None, d, tn), lambda i, j: (0, 0, j))],
        out_specs=[pl.BlockSpec((tm, tn), lambda i, j: (i, j)), pl.BlockSpec((tm, d), lambda i, j: (i, 0))],
        out_shape=[jax.ShapeDtypeStruct((rows, n), BF16), jax.ShapeDtypeStruct((rows, d), BF16)],
        compiler_params=_params(("parallel", "arbitrary")),
    )(hres, g, w)


def _rope(q, c, s1, s2, width):
    return q * c + pltpu.roll(q, width - 16, 1) * s1 + pltpu.roll(q, 16, 1) * s2


def _rope_transposed(dq, c, s1, s2, width):
    return dq * c + pltpu.roll(dq * s1, 16, 1) + pltpu.roll(dq * s2, width - 16, 1)


def _conf_conv(g1, w_ref):
    acc = jnp.zeros_like(g1)
    for k in range(CONF_K):
        acc = acc + w_ref[k:k + 1, :] * _sh(g1, CONF_K - 1 - k)
    return acc


def _conf_tail(c, cg, lg, lb):
    mu = jnp.mean(c, axis=-1, keepdims=True)
    xc = c - mu
    var = jnp.mean(xc * xc, axis=-1, keepdims=True)
    n = xc * lax.rsqrt(var + EPS) * lg + lb
    return _silu(n) * _silu(cg)


def branches_fwd(z_br, rope, pwbd, pscale, gq, wuq, gkv, wukv, conf_w, conf_vec, sc_w, li):
    rows = z_br.shape[0]
    t = ROW_TILE
    cur, prev, _, _, layer = _tile_specs(t, rows // HALO, li)

    def body(zc_ref, zp_ref, rope_ref, pw_ref, ps_ref, gq_ref, wuq_ref, gkv_ref, wukv_ref, cw_ref, cv_ref, sw_ref,
             ua_ref, uc_ref, ud_ref, q_ref, k_ref, v_ref):
        i = pl.program_id(0)
        zp = jnp.where(i == 0, jnp.zeros(zp_ref.shape, zp_ref.dtype), zp_ref[...])

        def ext(lo, w=256):
            return jnp.concatenate([zp[:, lo:lo + w], zc_ref[:, lo:lo + w]], axis=0).astype(F32)

        def col(lo, w=256):
            return zc_ref[:, lo:lo + w].astype(F32)

        v = ext(PV)
        p = (_pool_window_sums(v, _sh) / _pool_counts(i * t - HALO, t + HALO) - v)[HALO:]
        ya = _dot(p.astype(BF16), pw_ref[...]) * ps_ref[...]
        ua_ref[...] = (ya * _silu(col(PG))).astype(BF16)

        g1 = ext(CA) * _sigmoid(ext(CGT))
        c = _conf_conv(g1, cw_ref)[HALO:] + cv_ref[0:1, :]
        uc_ref[...] = _conf_tail(c, col(CG), cv_ref[1:2, :], cv_ref[2:3, :]).astype(BF16)

        e = ext(C2) * ext(XV)
        f = jnp.zeros_like(e)
        for k in range(SC_K):
            f = f + sw_ref[k:k + 1, :] * _sh(e, SC_K - 1 - k)
        ud_ref[...] = (col(BG) * f[HALO:] * _silu(col(SG))).astype(BF16)

        cth, s1, s2 = rope_ref[:, 0:128], rope_ref[:, 128:256], rope_ref[:, 256:384]
        qn = _rms(col(CQ), gq_ref[...]).astype(BF16)
        q = _dot(qn, wuq_ref[...])
        w8 = HEADS * HEAD_PAD
        q_ref[...] = (_rope(q, _lanes8(cth), _lanes8(s1), _lanes8(s2), w8) * Q_SCALE).astype(BF16)
        kvn = _rms(col(CKV, 128), gkv_ref[...]).astype(BF16)
        kv = _dot(kvn, wukv_ref[...])
        kr = _rope(col(KR, 128), cth, s1, s2, HEAD_PAD)
        k_ref[...] = (kv[:, :w8] + _lanes8(kr)).astype(BF16)
        v_ref[...] = kv[:, w8:].astype(BF16)

    outs = [jax.ShapeDtypeStruct((rows, 256), BF16)] * 3 + [jax.ShapeDtypeStruct((rows, 1024), BF16)] * 2 + [
        jax.ShapeDtypeStruct((rows, 512), BF16)]
    return pl.pallas_call(
        body, name="branches_fwd", grid=(rows // t,),
        in_specs=[cur(ZB), prev(ZB), cur(384), layer((256, 256)), layer((1, 256)), layer((1, 256)), layer((256, 1024), 0),
                  layer((1, 128)), layer((128, 1536), 0), layer((32, 256)), layer((8, 256)), layer((8, 256))],
        out_specs=[cur(256), cur(256), cur(256), cur(1024), cur(1024), cur(512)], out_shape=outs,
        compiler_params=_params(("parallel",)),
    )(z_br, z_br, rope, pwbd, pscale, gq, wuq, gkv, wukv, conf_w, conf_vec, sc_w)


def _head_lane_mask(h):
    lane = lax.broadcasted_iota(jnp.int32, (1, 2 * V_DIM), 1)
    return (lane >= V_DIM * h) & (lane < V_DIM * (h + 1))


def attention_fwd(q, k, v, gather=None):
    rows = q.shape[0]
    tq = ROW_TILE
    nq = rows // tq
    n = 0 if gather is None else len(gather[0])

    def body(*refs):
        if n:
            start, finish = _gather_ops(refs[3:3 + n], refs[5 + 2 * n:5 + 3 * n], refs[5 + 3 * n:], gather[2], True)
            pl.when((pl.program_id(0) == 0) & (pl.program_id(1) == 0))(start)
        compute(*refs[:3], *refs[3 + 2 * n:5 + 2 * n])
        if n:
            pl.when((pl.program_id(0) == HEADS // 2 - 1) & (pl.program_id(1) == nq - 1))(finish)

    def compute(q_ref, k_ref, v_ref, o_ref, lse_ref):
        i = pl.program_id(1)

        def head_step(h, tile, n_tiles, carry, masked):
            m, l, acc = carry
            width = n_tiles * tq
            r0 = pl.multiple_of(tile * tq, tq)
            kh = k_ref[pl.ds(r0, width), HEAD_PAD * h:HEAD_PAD * (h + 1)]
            vh = jnp.where(_head_lane_mask(h), v_ref[pl.ds(r0, width), :], jnp.zeros((), BF16))
            s = _dot_nt(q_ref[:, HEAD_PAD * h:HEAD_PAD * (h + 1)], kh)
            if masked:
                row = lax.broadcasted_iota(jnp.int32, (tq, width), 0)
                colm = lax.broadcasted_iota(jnp.int32, (tq, width), 1)
                s = jnp.where(colm <= row + (width - tq), s, -1e30)
            m2 = jnp.maximum(m, jnp.max(s, axis=-1, keepdims=True))
            alpha = jnp.exp(m - m2)
            pr = jnp.exp(s - m2)
            return m2, alpha * l + jnp.sum(pr, axis=-1, keepdims=True), alpha * acc + _dot(pr.astype(BF16), vh)

        def step(tile, n_tiles, carry, masked):
            return tuple(head_step(h, tile, n_tiles, carry[h], masked) for h in range(2))

        init = (jnp.full((tq, 1), -1e30, F32), jnp.zeros((tq, 1), F32), jnp.zeros((tq, 2 * V_DIM), F32))
        carry = lax.fori_loop(0, i // 2, lambda t, cr: step(2 * t, 2, cr, False), (init, init))
        carry = lax.cond(i % 2 == 1, lambda cr: step(i - 1, 2, cr, True), lambda cr: step(i, 1, cr, True), carry)
        out = jnp.zeros((tq, 2 * V_DIM), F32)
        for h, (m, l, acc) in enumerate(carry):
            out = out + acc / l
            lse_ref[h] = jnp.broadcast_to(m + jnp.log(l), (tq, LANES))
        o_ref[...] = out.astype(BF16)

    srcs, dsts = ([], []) if gather is None else (list(gather[0]), list(gather[1]))
    outs = pl.pallas_call(
        body, name="attention_fwd" if gather is None else "attention_fwd_gather", grid=(HEADS // 2, nq),
        in_specs=[pl.BlockSpec((tq, 2 * HEAD_PAD), lambda p, i: (i, p)), pl.BlockSpec((rows, 2 * HEAD_PAD), lambda p, i: (0, p)),
                  pl.BlockSpec((rows, 2 * V_DIM), lambda p, i: (0, p))] + [ANY] * (2 * n),
        out_specs=[pl.BlockSpec((tq, 2 * V_DIM), lambda p, i: (i, p)), pl.BlockSpec((2, tq, LANES), lambda p, i: (p, i, 0))] + [ANY] * n,
        out_shape=[jax.ShapeDtypeStruct((rows, HEADS * V_DIM), BF16), jax.ShapeDtypeStruct((HEADS, rows, LANES), F32)] + [
            jax.ShapeDtypeStruct(d.shape, d.dtype) for d in dsts],
        input_output_aliases={3 + n + a: 2 + a for a in range(n)}, scratch_shapes=GATHER_SEMS(n) if n else [],
        compiler_params=_params(("arbitrary", "arbitrary") if n else ("parallel", "parallel")),
    )(q, k, v, *srcs, *dsts)
    return outs[0], outs[1], list(outs[2:])


def merge_fwd(ua, o_att, uc, ud, z_br, z_gl, bias, woa, wob, woc, wod, wo, gpost, hres, li):
    rows = hres.shape[0]
    t = ROW_TILE
    cur, _, _, _, layer = _tile_specs(t, rows // HALO, li)
    d = D_MODEL

    def body(ua_ref, ob_ref, uc_ref, ud_ref, mg_ref, gl_ref, b_ref, woa_ref, wob_ref, woc_ref, wod_ref, wo_ref, gp_ref, h_ref,
             ub_ref, mb_ref, o_ref, hn_ref):
        ub = (ob_ref[...].astype(F32) * _silu(mg_ref[...].astype(F32))).astype(BF16)
        ub_ref[...] = ub
        m = jnp.zeros((t, d), F32)
        for idx, (u, w_ref) in enumerate(((ua_ref[...], woa_ref), (ub, wob_ref), (uc_ref[...], woc_ref), (ud_ref[...], wod_ref))):
            gate = _sigmoid(gl_ref[:, d * idx:d * (idx + 1)].astype(F32) + b_ref[:, d * idx:d * (idx + 1)])
            m = m + gate * _dot(u, w_ref[...])
        mb = m.astype(BF16)
        mb_ref[...] = mb
        o = _dot(mb, wo_ref[...])
        o_ref[...] = o
        hn_ref[...] = h_ref[...] + _rms(o, gp_ref[...])

    return pl.pallas_call(
        body, name="merge_fwd", grid=(rows // t,),
        in_specs=[cur(256), cur(512), cur(256), cur(256), cur(512, MG // 512), cur(ZG), layer((1, ZG)), layer((256, d), 0), layer((512, d), 0),
                  layer((256, d), 0), layer((256, d), 0), layer((d, d), 0), layer((1, d)), cur(d)],
        out_specs=[cur(512), cur(d), cur(d), cur(d)],
        out_shape=[jax.ShapeDtypeStruct((rows, 512), BF16), jax.ShapeDtypeStruct((rows, d), BF16), jax.ShapeDtypeStruct((rows, d), F32),
                   jax.ShapeDtypeStruct((rows, d), F32)],
        compiler_params=_params(("parallel",)),
    )(ua, o_att, uc, ud, z_br, z_gl, bias, woa, wob, woc, wod, wo, gpost, hres)


def loss_head(hres, target, n_tokens):
    rows, d = hres.shape
    t = ROW_TILE
    cur, _, _, full, _ = _tile_specs(t, rows // HALO)
    n_steps = rows // t

    def body(h_ref, t_ref, dh_ref, tot_ref, acc_ref):
        i = pl.program_id(0)

        @pl.when(i == 0)
        def _():
            acc_ref[...] = jnp.zeros_like(acc_ref)

        r = i * t + lax.broadcasted_iota(jnp.int32, (t, 1), 0)
        diff = jnp.where((r >= N_META) & (r < N_META + n_tokens), h_ref[...] - t_ref[...], 0.0)
        dh_ref[...] = diff * (1.0 / d)
        acc_ref[...] += jnp.sum(diff * diff, axis=0, keepdims=True)

        @pl.when(i == n_steps - 1)
        def _():
            tot_ref[...] = jnp.broadcast_to(jnp.sum(acc_ref[...], axis=1, keepdims=True), (1, LANES))

    return pl.pallas_call(
        body, name="loss_head", grid=(n_steps,), in_specs=[cur(d), cur(d)], out_specs=[cur(d), full((1, LANES))],
        out_shape=[jax.ShapeDtypeStruct((rows, d), F32), jax.ShapeDtypeStruct((1, LANES), F32)],
        scratch_shapes=[pltpu.VMEM((1, d), F32)], compiler_params=_params(("arbitrary",)),
    )(hres, target)


def _accumulate(i, ref, value):
    @pl.when(i == 0)
    def _():
        ref[...] = value

    @pl.when(i > 0)
    def _():
        ref[...] += value


def postnorm_bwd(dh, o, mb, wo, gpost, li):
    rows, d = dh.shape
    t = ROW_TILE
    cur, _, _, full, layer = _tile_specs(t, rows // HALO, li)

    def body(dh_ref, o_ref, mb_ref, wo_ref, gp_ref, dm_ref, dwo_ref, dgp_ref):
        i = pl.program_id(0)
        _, vjp = jax.vjp(_rms, o_ref[...], gp_ref[...])
        do, dg = vjp(dh_ref[...])
        dob = do.astype(BF16)
        dm_ref[...] = _dot_nt(dob, wo_ref[...])
        _accumulate(i, dwo_ref, _dot_tn(mb_ref[...], dob))
        _accumulate(i, dgp_ref, dg)

    return pl.pallas_call(
        body, name="postnorm_bwd", grid=(rows // t,), in_specs=[cur(d), cur(d), cur(d), layer((d, d), 0), layer((1, d))],
        out_specs=[cur(d), full((d, d)), full((1, d))],
        out_shape=[jax.ShapeDtypeStruct((rows, d), F32), jax.ShapeDtypeStruct((d, d), F32), jax.ShapeDtypeStruct((1, d), F32)],
        compiler_params=_params(("arbitrary",)),
    )(dh, o, mb, wo, gpost)


def merge_bwd(dm, ua, ub, uc, ud, z_gl, bias, woa, wob, woc, wod, li):
    rows, d = dm.shape
    t = ROW_TILE
    cur, _, _, full, layer = _tile_specs(t, rows // HALO, li)
    widths = (256, 512, 256, 256)

    def body(dm_ref, ua_ref, ub_ref, uc_ref, ud_ref, gl_ref, b_ref, woa_ref, wob_ref, woc_ref, wod_ref,
             dua_ref, dub_ref, duc_ref, dud_ref, dgl_ref, dwa_ref, dwb_ref, dwc_ref, dwd_ref, db_ref):
        i = pl.program_id(0)
        dm = dm_ref[...]
        groups = ((ua_ref, woa_ref, dua_ref, dwa_ref), (ub_ref, wob_ref, dub_ref, dwb_ref), (uc_ref, woc_ref, duc_ref, dwc_ref),
                  (ud_ref, wod_ref, dud_ref, dwd_ref))
        for idx, (u_ref, w_ref, du_ref, dw_ref) in enumerate(groups):
            cols = slice(d * idx, d * (idx + 1))
            u = u_ref[...]
            gate = _sigmoid(gl_ref[:, cols].astype(F32) + b_ref[:, cols])
            dgl = dm * _dot(u, w_ref[...]) * gate * (1.0 - gate)
            dgl_ref[:, cols] = dgl.astype(BF16)
            _accumulate(i, db_ref.at[:, cols], jnp.sum(dgl, axis=0, keepdims=True))
            dyb = (dm * gate).astype(BF16)
            du_ref[...] = _dot_nt(dyb, w_ref[...])
            _accumulate(i, dw_ref, _dot_tn(u, dyb))

    return pl.pallas_call(
        body, name="merge_bwd", grid=(rows // t,),
        in_specs=[cur(d), cur(256), cur(512), cur(256), cur(256), cur(ZG), layer((1, ZG))] + [layer((w, d), 0) for w in widths],
        out_specs=[cur(256), cur(512), cur(256), cur(256), cur(ZG)] + [full((w, d)) for w in widths] + [full((1, ZG))],
        out_shape=[jax.ShapeDtypeStruct((rows, w), F32) for w in widths] + [jax.ShapeDtypeStruct((rows, ZG), BF16)] + [
            jax.ShapeDtypeStruct((w, d), F32) for w in widths] + [jax.ShapeDtypeStruct((1, ZG), F32)],
        compiler_params=_params(("arbitrary",)),
    )(dm, ua, ub, uc, ud, z_gl, bias, woa, wob, woc, wod)


def pool_bwd(z_br, dua, pwbd, pscale, dz_buf, li):
    rows = z_br.shape[0]
    t = ROW_TILE
    n_steps = rows // t
    cur, prev, nxt, full, layer = _tile_specs(t, rows // HALO, li)

    def body(zc_ref, zp_ref, zn_ref, dc_ref, dn_ref, pw_ref, ps_ref, _, dz_ref, dpw_ref, dps_ref):
        i = pl.program_id(0)
        zp = jnp.where(i == 0, jnp.zeros(zp_ref.shape, zp_ref.dtype), zp_ref[...])
        zn = jnp.where(i == n_steps - 1, jnp.zeros(zn_ref.shape, zn_ref.dtype), zn_ref[...])
        dun = jnp.where(i == n_steps - 1, jnp.zeros(dn_ref.shape, dn_ref.dtype), dn_ref[...])

        def ext(lo):
            return jnp.concatenate([zp[:, lo:lo + 256], zc_ref[:, lo:lo + 256], zn[:, lo:lo + 256]], axis=0).astype(F32)

        n_ext = t + 2 * HALO
        v, pg = ext(PV), ext(PG)
        cnt = _pool_counts(i * t - HALO, n_ext)
        p = (_pool_window_sums(v, _sh) / cnt - v)[HALO:HALO + t]
        du = jnp.concatenate([jnp.zeros((HALO, 256), F32), dc_ref[...], dun], axis=0)
        dya = du * _silu(pg)
        dypb = (dya * ps_ref[...]).astype(BF16)
        dp = _dot_nt(dypb, pw_ref[...])
        dv = (_pool_window_sums(dp / cnt, _ash) - dp)[HALO:HALO + t]
        pb = p.astype(BF16)
        pw = _dot(pb, pw_ref[...])
        duc, pgc = dc_ref[...], pg[HALO:HALO + t]
        dpg = duc * pw * ps_ref[...] * _silu_grad(pgc)
        dz_ref[...] = jnp.concatenate([dv, dpg], axis=1).astype(BF16)
        _accumulate(i, dpw_ref, _dot_tn(pb, dypb[HALO:HALO + t]))
        _accumulate(i, dps_ref, jnp.sum(dya[HALO:HALO + t] * pw, axis=0, keepdims=True))

    return pl.pallas_call(
        body, name="pool_bwd", grid=(n_steps,),
        in_specs=[cur(ZB), prev(ZB), nxt(ZB), cur(256), nxt(256), layer((256, 256)), layer((1, 256)), ANY],
        out_specs=[cur(512, PV // 512), full((256, 256)), full((1, 256))],
        out_shape=[jax.ShapeDtypeStruct((rows, ZB), BF16), jax.ShapeDtypeStruct((256, 256), F32), jax.ShapeDtypeStruct((1, 256), F32)],
        input_output_aliases={7: 0}, compiler_params=_params(("arbitrary",)),
    )(z_br, z_br, z_br, dua, dua, pwbd, pscale, dz_buf)


def shortconv_bwd(z_br, dud, sc_w, dz_buf, li):
    rows = z_br.shape[0]
    t = ROW_TILE
    n_steps = rows // t
    cur, prev, nxt, full, layer = _tile_specs(t, rows // HALO, li)

    def body(zc_ref, zp_ref, zn_ref, dc_ref, dn_ref, sw_ref, _, dz_ref, dw_ref):
        i = pl.program_id(0)
        zp = jnp.where(i == 0, jnp.zeros(zp_ref.shape, zp_ref.dtype), zp_ref[...])
        zn = jnp.where(i == n_steps - 1, jnp.zeros(zn_ref.shape, zn_ref.dtype), zn_ref[...])
        dun = jnp.where(i == n_steps - 1, jnp.zeros(dn_ref.shape, dn_ref.dtype), dn_ref[...])

        def ext(lo):
            return jnp.concatenate([zp[:, lo:lo + 256], zc_ref[:, lo:lo + 256], zn[:, lo:lo + 256]], axis=0).astype(F32)

        mid = slice(HALO, HALO + t)
        bg, c2, xv, sg = ext(BG), ext(C2), ext(XV), ext(SG)
        du = jnp.concatenate([jnp.zeros((HALO, 256), F32), dc_ref[...], dun], axis=0)
        e = c2 * xv
        shifted = [_sh(e, SC_K - 1 - k) for k in range(SC_K)]
        f = sum(sw_ref[k:k + 1, :] * shifted[k] for k in range(SC_K))
        gate = _silu(sg)
        df = du * gate * bg
        de = sum(sw_ref[k:k + 1, :] * _ash(df, SC_K - 1 - k) for k in range(SC_K))
        dbg = du * gate * f
        dsg = du * bg * f * _silu_grad(sg)
        dz_ref[...] = jnp.concatenate([dbg[mid], (de * xv)[mid], (de * c2)[mid], dsg[mid]], axis=1).astype(BF16)
        dw = jnp.concatenate([jnp.sum((df * shifted[k])[mid], axis=0, keepdims=True) for k in range(SC_K)] + [
            jnp.zeros((8 - SC_K, 256), F32)], axis=0)
        _accumulate(i, dw_ref, dw)

    return pl.pallas_call(
        body, name="shortconv_bwd", grid=(n_steps,), in_specs=[cur(ZB), prev(ZB), nxt(ZB), cur(256), nxt(256), layer((8, 256)), ANY],
        out_specs=[cur(1024, BG // 1024), full((8, 256))],
        out_shape=[jax.ShapeDtypeStruct((rows, ZB), BF16), jax.ShapeDtypeStruct((8, 256), F32)],
        input_output_aliases={6: 0}, compiler_params=_params(("arbitrary",)),
    )(z_br, z_br, z_br, dud, dud, sc_w, dz_buf)


def conformer_bwd_tail(z_br, duc, conf_w, conf_vec, dz_buf, li):
    rows = z_br.shape[0]
    t = ROW_TILE
    cur, prev, _, full, layer = _tile_specs(t, rows // HALO, li)

    def body(zc_ref, zp_ref, du_ref, cw_ref, cv_ref, _, dc_ref, dcg_ref, dv_ref):
        i = pl.program_id(0)
        zp = jnp.where(i == 0, jnp.zeros(zp_ref.shape, zp_ref.dtype), zp_ref[...])

        def ext(lo):
            return jnp.concatenate([zp[:, lo:lo + 256], zc_ref[:, lo:lo + 256]], axis=0).astype(F32)

        g1 = ext(CA) * _sigmoid(ext(CGT))
        c = _conf_conv(g1, cw_ref)[HALO:] + cv_ref[0:1, :]
        _, vjp = jax.vjp(_conf_tail, c, zc_ref[:, CG:CG + 256].astype(F32), cv_ref[1:2, :], cv_ref[2:3, :])
        dc, dcg, dlg, dlb = vjp(du_ref[...])
        dc_ref[...] = dc
        dcg_ref[...] = dcg.astype(BF16)
        dvec = jnp.concatenate([dlg, dlb, jnp.sum(dc, axis=0, keepdims=True), jnp.zeros((5, 256), F32)], axis=0)
        _accumulate(i, dv_ref, dvec)

    return pl.pallas_call(
        body, name="conformer_bwd_tail", grid=(rows // t,), in_specs=[cur(ZB), prev(ZB), cur(256), layer((32, 256)), layer((8, 256)), ANY],
        out_specs=[cur(256), cur(256, CG // 256), full((8, 256))],
        out_shape=[jax.ShapeDtypeStruct((rows, 256), F32), jax.ShapeDtypeStruct((rows, ZB), BF16), jax.ShapeDtypeStruct((8, 256), F32)],
        input_output_aliases={5: 1}, compiler_params=_params(("arbitrary",)),
    )(z_br, z_br, duc, conf_w, conf_vec, dz_buf)


def conformer_bwd_conv(z_br, dc, conf_w, dz_buf, li):
    rows = z_br.shape[0]
    t = ROW_TILE
    n_steps = rows // t
    cur, prev, nxt, full, layer = _tile_specs(t, rows // HALO, li)

    def body(zc_ref, zp_ref, dc_ref, dn_ref, cw_ref, _, dz_ref, dw_ref):
        i = pl.program_id(0)
        zp = jnp.where(i == 0, jnp.zeros(zp_ref.shape, zp_ref.dtype), zp_ref[...])
        dcn = jnp.where(i == n_steps - 1, jnp.zeros(dn_ref.shape, dn_ref.dtype), dn_ref[...])

        def ext(lo):
            return jnp.concatenate([zp[:, lo:lo + 256], zc_ref[:, lo:lo + 256]], axis=0).astype(F32)

        a, gt = ext(CA), ext(CGT)
        sg = _sigmoid(gt)
        g1 = a * sg
        dc = dc_ref[...]
        dce = jnp.concatenate([dc, dcn], axis=0)
        dg1 = jnp.zeros_like(dce)
        dws = []
        for k in range(CONF_K):
            dg1 = dg1 + cw_ref[k:k + 1, :] * _ash(dce, CONF_K - 1 - k)
            dws.append(jnp.sum(dc * _sh(g1, CONF_K - 1 - k)[HALO:], axis=0, keepdims=True))
        dg1 = dg1[:t]
        ac, sc = a[HALO:], sg[HALO:]
        dz_ref[...] = jnp.concatenate([dg1 * sc, dg1 * ac * sc * (1.0 - sc)], axis=1).astype(BF16)
        _accumulate(i, dw_ref, jnp.concatenate(dws + [jnp.zeros((32 - CONF_K, 256), F32)], axis=0))

    return pl.pallas_call(
        body, name="conformer_bwd_conv", grid=(n_steps,), in_specs=[cur(ZB), prev(ZB), cur(256), nxt(256), layer((32, 256)), ANY],
        out_specs=[cur(512, CA // 512), full((32, 256))],
        out_shape=[jax.ShapeDtypeStruct((rows, ZB), BF16), jax.ShapeDtypeStruct((32, 256), F32)],
        input_output_aliases={5: 0}, compiler_params=_params(("arbitrary",)),
    )(z_br, z_br, dc, dc, conf_w, dz_buf)


def attention_bwd_prep(dub, o_att, z_br, dz_buf):
    rows = dub.shape[0]
    t = ROW_TILE
    cur, _, _, _, _ = _tile_specs(t, rows // HALO)

    def body(du_ref, o_ref, mg_ref, _, do_ref, dmg_ref, delta_ref):
        du, o, mg = du_ref[...], o_ref[...].astype(F32), mg_ref[...].astype(F32)
        do = du * _silu(mg)
        do_ref[...] = do.astype(BF16)
        dmg_ref[...] = (du * o * _silu_grad(mg)).astype(BF16)
        prod = do * o
        lane = lax.broadcasted_iota(jnp.int32, (1, HEADS * V_DIM), 1)
        for h in range(HEADS):
            part = jnp.where((lane >= V_DIM * h) & (lane < V_DIM * (h + 1)), prod, 0.0)
            delta_ref[h] = jnp.broadcast_to(jnp.sum(part, axis=-1, keepdims=True), (t, LANES))

    return pl.pallas_call(
        body, name="attention_bwd_prep", grid=(rows // t,), in_specs=[cur(512), cur(512), cur(512, MG // 512), ANY],
        out_specs=[cur(512), cur(512, MG // 512), pl.BlockSpec((HEADS, t, LANES), lambda i: (0, i, 0))],
        out_shape=[jax.ShapeDtypeStruct((rows, 512), BF16), jax.ShapeDtypeStruct((rows, ZB), BF16),
                   jax.ShapeDtypeStruct((HEADS, rows, LANES), F32)],
        input_output_aliases={3: 1}, compiler_params=_params(("parallel",)),
    )(dub, o_att, z_br, dz_buf)


def attention_bwd(q, k, v, do, lse, delta, exchange=None):
    rows = q.shape[0]
    tq = ROW_TILE
    nq = rows // tq
    n = 0 if exchange is None else len(exchange[0])

    def body(*refs):
        if n:
            start, finish = _exchange_ops(refs[6:6 + n], refs[9 + 2 * n:9 + 3 * n], refs[9 + 3 * n:], exchange[2])
            pl.when((pl.program_id(0) == 0) & (pl.program_id(1) == 0))(start)
        compute(*refs[:6], *refs[6 + 2 * n:9 + 2 * n])
        if n:
            pl.when((pl.program_id(0) == HEADS // 2 - 1) & (pl.program_id(1) == nq - 1))(finish)

    def compute(q_ref, k_ref, v_ref, do_ref, lse_ref, dl_ref, dq_ref, dk_ref, dv_ref):
        j = pl.program_id(1)

        @pl.when(j == 0)
        def _():
            dq_ref[...] = jnp.zeros_like(dq_ref)

        def head_step(h, tile, n_tiles, dk, dv, diagonal):
            lanes = slice(HEAD_PAD * h, HEAD_PAD * (h + 1))
            hm = _head_lane_mask(h)
            kh = k_ref[:, lanes]
            vh = jnp.where(hm, v_ref[...], jnp.zeros((), BF16))
            r0, width = pl.multiple_of(tile * tq, tq), n_tiles * tq
            qi = q_ref[pl.ds(r0, width), lanes]
            doi = jnp.where(hm, do_ref[pl.ds(r0, width), :], jnp.zeros((), BF16))
            s = _dot_nt(qi, kh)
            if diagonal:
                s = jnp.where(lax.broadcasted_iota(jnp.int32, (tq, tq), 1) <= lax.broadcasted_iota(jnp.int32, (tq, tq), 0), s, -1e30)
            pr = jnp.exp(s - lse_ref[h, pl.ds(r0, width), :][:, 0:1])
            dv = dv + _dot_tn(pr.astype(BF16), doi)
            dp = _dot_nt(doi, vh)
            ds = (pr * (dp - dl_ref[h, pl.ds(r0, width), :][:, 0:1])).astype(BF16)
            dq_ref[pl.ds(r0, width), lanes] += _dot(ds, kh)
            return dk + _dot_tn(ds, qi), dv

        def step(tile, n_tiles, carry, diagonal):
            dk0, dk1, dv = carry
            dk0, dv = head_step(0, tile, n_tiles, dk0, dv, diagonal)
            dk1, dv = head_step(1, tile, n_tiles, dk1, dv, diagonal)
            return dk0, dk1, dv

        zero = jnp.zeros((tq, HEAD_PAD), F32)
        carry = step(j, 1, (zero, zero, jnp.zeros((tq, 2 * V_DIM), F32)), True)
        odd = (nq - 1 - j) % 2
        carry = lax.cond(odd == 1, lambda cr: step(j + 1, 1, cr, False), lambda cr: cr, carry)
        dk0, dk1, dv = lax.fori_loop(0, (nq - 1 - j) // 2, lambda t, cr: step(j + 1 + odd + 2 * t, 2, cr, False), carry)
        dk_ref[:, 0:HEAD_PAD] = dk0
        dk_ref[:, HEAD_PAD:2 * HEAD_PAD] = dk1
        dv_ref[...] = dv

    srcs, dsts = ([], []) if exchange is None else (list(exchange[0]), list(exchange[1]))
    outs = pl.pallas_call(
        body, name="attention_bwd" if exchange is None else "attention_bwd_exchange", grid=(HEADS // 2, nq),
        in_specs=[pl.BlockSpec((rows, 2 * HEAD_PAD), lambda p, j: (0, p)), pl.BlockSpec((tq, 2 * HEAD_PAD), lambda p, j: (j, p)),
                  pl.BlockSpec((tq, 2 * V_DIM), lambda p, j: (j, p)), pl.BlockSpec((rows, 2 * V_DIM), lambda p, j: (0, p)),
                  pl.BlockSpec((2, rows, LANES), lambda p, j: (p, 0, 0)), pl.BlockSpec((2, rows, LANES), lambda p, j: (p, 0, 0))] + [
                      ANY] * (2 * n),
        out_specs=[pl.BlockSpec((rows, 2 * HEAD_PAD), lambda p, j: (0, p)), pl.BlockSpec((tq, 2 * HEAD_PAD), lambda p, j: (j, p)),
                   pl.BlockSpec((tq, 2 * V_DIM), lambda p, j: (j, p))] + [ANY] * n,
        out_shape=[jax.ShapeDtypeStruct((rows, HEADS * HEAD_PAD), F32), jax.ShapeDtypeStruct((rows, HEADS * HEAD_PAD), F32),
                   jax.ShapeDtypeStruct((rows, HEADS * V_DIM), F32)] + [jax.ShapeDtypeStruct(d.shape, d.dtype) for d in dsts],
        input_output_aliases={6 + n + a: 3 + a for a in range(n)}, scratch_shapes=EXCHANGE_SEMS(n) if n else [],
        compiler_params=_params(("arbitrary", "arbitrary") if n else ("parallel", "arbitrary")),
    )(q, k, v, do, lse, delta, *srcs, *dsts)
    return outs[0], outs[1], outs[2], list(outs[3:])


def mla_prep_bwd(dq, dk, dv, z_br, rope, gq, wuq, gkv, wukv, dz_buf, li):
    rows = dq.shape[0]
    t = ROW_TILE
    cur, _, _, full, layer = _tile_specs(t, rows // HALO, li)
    w8 = HEADS * HEAD_PAD

    def body(dq_ref, dk_ref, dv_ref, z_ref, rope_ref, gq_ref, wuq_ref, gkv_ref, wukv_ref, _, dz_ref, dwuq_ref, dwukv_ref, dgq_ref, dgkv_ref):
        i = pl.program_id(0)
        cth, s1, s2 = rope_ref[:, 0:128], rope_ref[:, 128:256], rope_ref[:, 256:384]
        dqb = _rope_transposed(dq_ref[...] * Q_SCALE, _lanes8(cth), _lanes8(s1), _lanes8(s2), w8).astype(BF16)
        cq = z_ref[:, 0:256].astype(F32)
        qn, vjp_q = jax.vjp(_rms, cq, gq_ref[...])
        _accumulate(i, dwuq_ref, _dot_tn(qn.astype(BF16), dqb))
        dcq, dgq = vjp_q(_dot_nt(dqb, wuq_ref[...]))
        _accumulate(i, dgq_ref, dgq)

        dk = dk_ref[...]
        dkr = sum(dk[:, HEAD_PAD * h:HEAD_PAD * (h + 1)] for h in range(HEADS))
        dkr = _rope_transposed(dkr, cth, s1, s2, HEAD_PAD)
        lane = lax.broadcasted_iota(jnp.int32, (1, HEAD_PAD), 1)
        dkr = jnp.where((lane >= QK_NOPE) & (lane < QK_NOPE + QK_ROPE), dkr, 0.0)
        dkvb = jnp.concatenate([dk, dv_ref[...]], axis=1).astype(BF16)
        ckv = z_ref[:, 256:384].astype(F32)
        kvn, vjp_kv = jax.vjp(_rms, ckv, gkv_ref[...])
        _accumulate(i, dwukv_ref, _dot_tn(kvn.astype(BF16), dkvb))
        dckv, dgkv = vjp_kv(_dot_nt(dkvb, wukv_ref[...]))
        _accumulate(i, dgkv_ref, dgkv)
        dz_ref[...] = jnp.concatenate([dcq, dckv, dkr], axis=1).astype(BF16)

    return pl.pallas_call(
        body, name="mla_prep_bwd", grid=(rows // t,),
        in_specs=[cur(w8), cur(w8), cur(512), cur(512, CQ // 512), cur(384), layer((1, 256)), layer((256, w8), 0), layer((1, 128)),
                  layer((128, w8 + 512), 0), ANY],
        out_specs=[cur(512, CQ // 512), full((256, w8)), full((128, w8 + 512)), full((1, 256)), full((1, 128))],
        out_shape=[jax.ShapeDtypeStruct((rows, ZB), BF16), jax.ShapeDtypeStruct((256, w8), F32), jax.ShapeDtypeStruct((128, w8 + 512), F32),
                   jax.ShapeDtypeStruct((1, 256), F32), jax.ShapeDtypeStruct((1, 128), F32)],
        input_output_aliases={9: 0}, compiler_params=_params(("arbitrary",)),
    )(dq, dk, dv, z_br, rope, gq, wuq, gkv, wukv, dz_buf)


def prenorm_bwd(dz_br, w_br, dh_gl, hres, gpre, dh_next, li):
    rows, d = hres.shape
    t = ROW_TILE
    cur, _, _, full, layer = _tile_specs(t, rows // HALO, li)

    def body(dz_ref, w_ref, dp_ref, x_ref, g_ref, dn_ref, dx_ref, dg_ref):
        i = pl.program_id(0)
        dh = _dot_nt(dz_ref[...], w_ref[...]) + dp_ref[...]
        _, vjp = jax.vjp(_rms, x_ref[...], g_ref[...])
        dx, dg = vjp(dh)
        dx_ref[...] = dx + dn_ref[...]
        _accumulate(i, dg_ref, dg)

    return pl.pallas_call(
        body, name="prenorm_bwd", grid=(rows // t,), in_specs=[cur(ZB), layer((d, ZB), 0), cur(d), cur(d), layer((1, d)), cur(d)],
        out_specs=[cur(d), full((1, d))], out_shape=[jax.ShapeDtypeStruct((rows, d), F32), jax.ShapeDtypeStruct((1, d), F32)],
        compiler_params=_params(("arbitrary",)),
    )(dz_br, w_br, dh_gl, hres, gpre, dh_next)


def _mesh_position():
    return lax.axis_index("x"), lax.axis_index("y"), lax.axis_index("c")


def chip_exchange(src, gather, name):
    block = src.shape if gather else src.shape[1:]

    def body(src_ref, dst_ref, send_sems, recv_sems, local_sem):
        x, y, c = _mesh_position()
        me = 2 * x + y
        peers = ((1 - x, y), (x, 1 - y), (1 - x, 1 - y))

        def part(k):
            return src_ref if gather else src_ref.at[k]

        def copy(j, slot):
            px, py = peers[j]
            return pltpu.make_async_remote_copy(src_ref=part(2 * px + py), dst_ref=dst_ref.at[slot], send_sem=send_sems.at[j],
                                                recv_sem=recv_sems.at[j], device_id=(px, py, c), device_id_type=MESH)

        local = pltpu.make_async_copy(part(me), dst_ref.at[me], local_sem)
        local.start()
        sends = [copy(j, me) for j in range(3)]
        for cp in sends:
            cp.start()
        for j, (px, py) in enumerate(peers):
            copy(j, 2 * px + py).wait_recv()
        for cp in sends:
            cp.wait_send()
        local.wait()

    return pl.pallas_call(
        body, name=name, in_specs=[pl.BlockSpec(memory_space=pl.ANY)], out_specs=pl.BlockSpec(memory_space=pl.ANY),
        out_shape=jax.ShapeDtypeStruct((N_CHIPS,) + tuple(block), src.dtype),
        scratch_shapes=[pltpu.SemaphoreType.DMA((3,)), pltpu.SemaphoreType.DMA((3,)), pltpu.SemaphoreType.DMA(())],
    )(src)


def sibling_swap(src, name):
    def body(src_ref, dst_ref, send_sem, recv_sem):
        x, y, c = _mesh_position()
        cp = pltpu.make_async_remote_copy(src_ref=src_ref, dst_ref=dst_ref, send_sem=send_sem, recv_sem=recv_sem,
                                          device_id=(x, y, 1 - c), device_id_type=MESH)
        cp.start()
        cp.wait()

    return pl.pallas_call(
        body, name=name, in_specs=[pl.BlockSpec(memory_space=pl.ANY)], out_specs=pl.BlockSpec(memory_space=pl.ANY),
        out_shape=jax.ShapeDtypeStruct(src.shape, src.dtype),
        scratch_shapes=[pltpu.SemaphoreType.DMA(()), pltpu.SemaphoreType.DMA(())],
    )(src)


def _comm_call(body, name, n_in, out_shapes, n_sems):
    return pl.pallas_call(
        body, name=name, in_specs=[ANY] * n_in, out_specs=[ANY] * len(out_shapes), out_shape=out_shapes,
        scratch_shapes=[pltpu.SemaphoreType.DMA((n,)) for n in n_sems])


def _row_halves(c, rows):
    half = rows // 2
    return pl.ds(pl.multiple_of(c * half, 16), half), pl.ds(pl.multiple_of((1 - c) * half, 16), half)


def _peers():
    x, y, c = _mesh_position()
    return x, y, c, 2 * x + y, ((1 - x, y), (x, 1 - y), (1 - x, 1 - y))


def _gather_ops(src, dst, sems, layer, own_copy):
    ici_send, ici_recv, d2d_send, d2d_recv, own_sems = sems
    n = len(src)

    def fetch(a, j, slot):
        x, y, c, _, peers = _peers()
        px, py = peers[j]
        mine, _ = _row_halves(c, src[a].shape[1])
        return pltpu.make_async_remote_copy(src_ref=src[a].at[layer, mine], dst_ref=dst[a].at[layer, slot, mine], send_sem=ici_send.at[3 * a + j],
                                            recv_sem=ici_recv.at[3 * a + j], device_id=(px, py, c), device_id_type=MESH)

    def forward(a, j, sibling_half):
        x, y, c, _, peers = _peers()
        px, py = peers[j]
        part = dst[a].at[layer, 2 * px + py, _row_halves(c, src[a].shape[1])[1 if sibling_half else 0]]
        return pltpu.make_async_remote_copy(src_ref=part, dst_ref=part, send_sem=d2d_send.at[3 * a + j], recv_sem=d2d_recv.at[3 * a + j],
                                            device_id=(x, y, 1 - c), device_id_type=MESH)

    def own(a):
        return pltpu.make_async_copy(src[a].at[layer], dst[a].at[layer, _peers()[3]], own_sems.at[a])

    def start():
        me = _peers()[3]
        for a in range(n):
            if own_copy:
                own(a).start()
            for j in range(3):
                fetch(a, j, me).start()

    def finish():
        peers = _peers()[4]
        for j, (px, py) in enumerate(peers):
            for a in range(n):
                fetch(a, j, 2 * px + py).wait_recv()
                forward(a, j, False).start()
        for j in range(3):
            for a in range(n):
                forward(a, j, True).wait_recv()
        for j in range(3):
            for a in range(n):
                fetch(a, j, 0).wait_send()
                forward(a, j, False).wait_send()
        if own_copy:
            for a in range(n):
                own(a).wait()

    return start, finish


def _exchange_ops(src, dst, sems, layer):
    send_sems, recv_sems, own_sems = sems
    n = len(src)

    def copy(a, j, slot):
        x, y, c, _, peers = _peers()
        px, py = peers[j]
        return pltpu.make_async_remote_copy(src_ref=src[a].at[2 * px + py], dst_ref=dst[a].at[layer, slot], send_sem=send_sems.at[3 * a + j],
                                            recv_sem=recv_sems.at[3 * a + j], device_id=(px, py, c), device_id_type=MESH)

    def own(a):
        me = _peers()[3]
        return pltpu.make_async_copy(src[a].at[me], dst[a].at[layer, me], own_sems.at[a])

    def start():
        me = _peers()[3]
        for a in range(n):
            own(a).start()
            for j in range(3):
                copy(a, j, me).start()

    def finish():
        peers = _peers()[4]
        for j, (px, py) in enumerate(peers):
            for a in range(n):
                copy(a, j, 2 * px + py).wait_recv()
        for j in range(3):
            for a in range(n):
                copy(a, j, 0).wait_send()
        for a in range(n):
            own(a).wait()

    return start, finish


GATHER_SEMS = lambda n: [pltpu.SemaphoreType.DMA((3 * n,))] * 4 + [pltpu.SemaphoreType.DMA((n,))]
EXCHANGE_SEMS = lambda n: [pltpu.SemaphoreType.DMA((3 * n,))] * 2 + [pltpu.SemaphoreType.DMA((n,))]


def gather_layer(srcs, dsts, layer, name):
    n = len(srcs)

    def body(*refs):
        start, finish = _gather_ops(refs[:n], refs[2 * n:3 * n], refs[3 * n:], layer, False)
        start()
        finish()

    return pl.pallas_call(
        body, name=name, in_specs=[ANY] * (2 * n), out_specs=[ANY] * n, out_shape=[jax.ShapeDtypeStruct(d.shape, d.dtype) for d in dsts],
        input_output_aliases={n + a: a for a in range(n)}, scratch_shapes=GATHER_SEMS(n),
    )(*srcs, *dsts)


def exchange_layer(ss, dsts, layer, name):
    n = len(ss)

    def body(*refs):
        start, finish = _exchange_ops(refs[:n], refs[2 * n:3 * n], refs[3 * n:], layer)
        start()
        finish()

    return pl.pallas_call(
        body, name=name, in_specs=[ANY] * (2 * n), out_specs=[ANY] * n, out_shape=[jax.ShapeDtypeStruct(d.shape, d.dtype) for d in dsts],
        input_output_aliases={n + a: a for a in range(n)}, scratch_shapes=EXCHANGE_SEMS(n),
    )(*ss, *dsts)


def swap_row_halves(ps, name):
    n = len(ps)

    def body(*refs):
        src, dst = refs[:n], refs[n:2 * n]
        send_sems, recv_sems = refs[2 * n:]
        x, y, c = _mesh_position()
        copies = [pltpu.make_async_remote_copy(src_ref=src[a].at[:, _row_halves(c, src[a].shape[1])[1]], dst_ref=dst[a], send_sem=send_sems.at[a],
                                               recv_sem=recv_sems.at[a], device_id=(x, y, 1 - c), device_id_type=MESH) for a in range(n)]
        for cp in copies:
            cp.start()
        for cp in copies:
            cp.wait()

    outs = [jax.ShapeDtypeStruct((p.shape[0], p.shape[1] // 2, p.shape[2]), p.dtype) for p in ps]
    return _comm_call(body, name, n, outs, (n, n))(*ps)


def add_row_half(p, r, c, name):
    n, half, cols = r.shape
    rb = _row_block(half, cols, 2)
    steps = half // rb

    def body(c_ref, p_ref, r_ref, o_ref):
        o_ref[...] = (p_ref[...].astype(F32) + r_ref[...].astype(F32)).astype(BF16)

    return pl.pallas_call(
        body, name=name, out_shape=jax.ShapeDtypeStruct(r.shape, BF16),
        grid_spec=pltpu.PrefetchScalarGridSpec(
            num_scalar_prefetch=1, grid=(n, steps),
            in_specs=[pl.BlockSpec((1, rb, cols), lambda k, i, c_ref: (k, c_ref[0] * steps + i, 0)),
                      pl.BlockSpec((1, rb, cols), lambda k, i, c_ref: (k, i, 0))],
            out_specs=pl.BlockSpec((1, rb, cols), lambda k, i, c_ref: (k, i, 0))),
        compiler_params=_params(("parallel", "parallel")),
    )(jnp.reshape(c, (1,)).astype(jnp.int32), p, r)


def sum_row_halves(l, c, name):
    layers, n, half, cols = l.shape
    rb = _row_block(half, cols, 4)
    steps = half // rb

    def body(c_ref, l_ref, o_ref):
        acc = l_ref[0, 0].astype(F32)
        for s in range(1, n):
            acc = acc + l_ref[0, s].astype(F32)
        o_ref[0] = acc

    return pl.pallas_call(
        body, name=name, out_shape=jax.ShapeDtypeStruct((layers, 2 * half, cols), F32),
        grid_spec=pltpu.PrefetchScalarGridSpec(
            num_scalar_prefetch=1, grid=(layers, steps), in_specs=[pl.BlockSpec((1, n, rb, cols), lambda a, i, c_ref: (a, 0, i, 0))],
            out_specs=pl.BlockSpec((1, rb, cols), lambda a, i, c_ref: (a, c_ref[0] * steps + i, 0))),
        compiler_params=_params(("parallel", "parallel")),
    )(jnp.reshape(c, (1,)).astype(jnp.int32), l)


def share_row_halves(gs, name):
    n = len(gs)

    def body(*refs):
        dst = refs[n:2 * n]
        send_sems, recv_sems = refs[2 * n:]
        x, y, c = _mesh_position()

        def copy(a, sibling_half):
            part = dst[a].at[:, _row_halves(c, dst[a].shape[1])[1 if sibling_half else 0]]
            return pltpu.make_async_remote_copy(src_ref=part, dst_ref=part, send_sem=send_sems.at[a], recv_sem=recv_sems.at[a],
                                                device_id=(x, y, 1 - c), device_id_type=MESH)

        for a in range(n):
            copy(a, False).start()
        for a in range(n):
            copy(a, True).wait_recv()
        for a in range(n):
            copy(a, False).wait_send()

    return pl.pallas_call(
        body, name=name, in_specs=[ANY] * n, out_specs=[ANY] * n, out_shape=[jax.ShapeDtypeStruct(g.shape, g.dtype) for g in gs],
        input_output_aliases={a: a for a in range(n)}, scratch_shapes=[pltpu.SemaphoreType.DMA((n,)), pltpu.SemaphoreType.DMA((n,))],
    )(*gs)


def _row_block(rows, cols, itemsize):
    best = 16
    for rb in range(16, rows + 1, 16):
        if rows % rb == 0 and rb * cols * itemsize <= 2 * 1024 * 1024:
            best = rb
    assert rows % best == 0, (rows, cols)
    return best


def _comm_block(rows):
    return 1024 if rows % 1024 == 0 else rows


def sum_slots(buf, name):
    n, r, c = buf.shape
    rb = _comm_block(r)

    def body(b_ref, o_ref):
        acc = b_ref[0].astype(F32)
        for s in range(1, n):
            acc = acc + b_ref[s].astype(F32)
        o_ref[...] = acc

    return pl.pallas_call(
        body, name=name, grid=(r // rb,), in_specs=[pl.BlockSpec((n, rb, c), lambda i: (0, i, 0))],
        out_specs=pl.BlockSpec((rb, c), lambda i: (i, 0)), out_shape=jax.ShapeDtypeStruct((r, c), F32),
        compiler_params=_params(("parallel",)),
    )(buf)


def add_pair(a, b, out_dtype, name):
    shape = a.shape
    a2, b2 = a.reshape(-1, shape[-1]), b.reshape(-1, shape[-1])
    r, c = a2.shape
    rb = _comm_block(r)

    def body(a_ref, b_ref, o_ref):
        o_ref[...] = (a_ref[...].astype(F32) + b_ref[...].astype(F32)).astype(out_dtype)

    out = pl.pallas_call(
        body, name=name, grid=(r // rb,), in_specs=[pl.BlockSpec((rb, c), lambda i: (i, 0))] * 2,
        out_specs=pl.BlockSpec((rb, c), lambda i: (i, 0)), out_shape=jax.ShapeDtypeStruct((r, c), out_dtype),
        compiler_params=_params(("parallel",)),
    )(a2, b2)
    return out.reshape(shape)


def adamw(w, g, m, v):
    shape = w.shape
    cols = shape[-1]
    rows = math.prod(shape[:-1])
    if rows * cols <= 256 * 1024:
        rb, cb = rows, cols
    else:
        rb = max(r for r in range(8, 2049, 8) if rows % r == 0)
        cb = cols if rb * cols * 4 <= 2 * 1024 * 1024 else 256
    assert rows % rb == 0 and cols % cb == 0, shape

    def body(w_ref, g_ref, m_ref, v_ref, d_ref, nm_ref, nv_ref):
        g_ = g_ref[...]
        nm = ADAM_B1 * m_ref[...] + (1.0 - ADAM_B1) * g_
        nv = ADAM_B2 * v_ref[...] + (1.0 - ADAM_B2) * (g_ * g_)
        m_hat = nm / (1.0 - ADAM_B1 ** ADAM_STEP)
        v_hat = nv / (1.0 - ADAM_B2 ** ADAM_STEP)
        d_ref[...] = -ADAM_LR * (m_hat / (jnp.sqrt(v_hat) + ADAM_EPS) + ADAM_WD * w_ref[...])
        nm_ref[...] = nm
        nv_ref[...] = nv

    spec = pl.BlockSpec((rb, cb), lambda i, j: (i, j))
    outs = pl.pallas_call(
        body, name="adamw", grid=(rows // rb, cols // cb), in_specs=[spec] * 4, out_specs=[spec] * 3,
        out_shape=[jax.ShapeDtypeStruct((rows, cols), F32)] * 3, compiler_params=_params(("parallel", "parallel")),
    )(*(a.reshape(rows, cols) for a in (w, g, m, v)))
    return tuple(o.reshape(shape) for o in outs)


def _pack(arrays, dtype, row_multiple):
    flat = jnp.concatenate([a.astype(dtype).reshape(-1) for a in arrays])
    per = LANES * row_multiple
    total = -(-flat.shape[0] // per) * per
    return jnp.pad(flat, (0, total - flat.shape[0])).reshape(total // LANES, LANES)


def _unpack(buf, shapes):
    flat = buf.reshape(-1)
    out, off = [], 0
    for s in shapes:
        n = math.prod(s)
        out.append(flat[off:off + n].reshape(s))
        off += n
    return out


def _input_weights(blocks):
    c0, c1, c2, c3 = (blocks[..., k, :, :] for k in range(N_CHIPS))
    pad = lambda n: jnp.zeros(c0.shape[:-1] + (n,), blocks.dtype)
    w_br = jnp.concatenate([c1[..., 376:1400], c0[..., 0:896], pad(64), c0[..., 896:928], pad(32), c0[..., 928:], c1[..., 0:376]], axis=-1)
    return w_br, jnp.concatenate([c1[..., 1400:], c2, c3], axis=-1)


def _input_weights_inverse(dw_br, dw_gl):
    c0 = jnp.concatenate([dw_br[..., 1024:1920], dw_br[..., 1984:2016], dw_br[..., 2048:2952]], axis=-1)
    c1 = jnp.concatenate([dw_br[..., 2952:ZB], dw_br[..., 0:1024], dw_gl[..., 0:432]], axis=-1)
    return jnp.stack([c0, c1, dw_gl[..., 432:2264], dw_gl[..., 2264:]], axis=-3)


def _uq_layout(w):
    r = w.reshape(w.shape[:-1] + (HEADS, QK_NOPE + QK_ROPE))
    r = jnp.pad(r, [(0, 0)] * (r.ndim - 1) + [(0, HEAD_PAD - QK_NOPE - QK_ROPE)])
    return r.reshape(w.shape[:-1] + (HEADS * HEAD_PAD,))


def _uq_layout_inverse(dw):
    r = dw.reshape(dw.shape[:-1] + (HEADS, HEAD_PAD))[..., :QK_NOPE + QK_ROPE]
    return r.reshape(dw.shape[:-1] + (HEADS * (QK_NOPE + QK_ROPE),))


def _ukv_layout(w):
    r = w.reshape(w.shape[:-1] + (HEADS, QK_NOPE + V_DIM))
    kp = jnp.pad(r[..., :QK_NOPE], [(0, 0)] * (r.ndim - 1) + [(0, HEAD_PAD - QK_NOPE)]).reshape(w.shape[:-1] + (HEADS * HEAD_PAD,))
    return jnp.concatenate([kp, r[..., QK_NOPE:].reshape(w.shape[:-1] + (HEADS * V_DIM,))], axis=-1)


def _ukv_layout_inverse(dw):
    lead = dw.shape[:-1]
    dk = dw[..., :HEADS * HEAD_PAD].reshape(lead + (HEADS, HEAD_PAD))[..., :QK_NOPE]
    dv = dw[..., HEADS * HEAD_PAD:].reshape(lead + (HEADS, V_DIM))
    return jnp.concatenate([dk, dv], axis=-1).reshape(lead + (HEADS * (QK_NOPE + V_DIM),))


def _block_diag(pw):
    zeros = lambda n: jnp.zeros(pw.shape[:-3] + (64, n), pw.dtype)
    rows = [jnp.concatenate([zeros(64 * g), pw[..., g, :, :], zeros(64 * (3 - g))], axis=-1) for g in range(4)]
    return jnp.concatenate(rows, axis=-2)


def _block_diag_inverse(d):
    return jnp.stack([d[..., 64 * g:64 * (g + 1), 64 * g:64 * (g + 1)] for g in range(4)], axis=-3)


def _pad_rows(a, n):
    return jnp.pad(a, ((0, n - a.shape[0]), (0, 0)))


def _rope_tables(rows):
    inv = 1.0 / (ROPE_THETA ** (jnp.arange(0, QK_ROPE, 2, dtype=F32) / QK_ROPE))
    ang = jnp.arange(rows, dtype=F32)[:, None] * inv[None, :]
    cos, sin = jnp.cos(ang), jnp.sin(ang)
    one, zero = jnp.ones((rows, 1), F32), jnp.zeros((rows, 1), F32)
    rep = lambda a, n: jnp.broadcast_to(a, (rows, n))
    c = jnp.concatenate([rep(one, 64), cos, cos, rep(one, 32)], axis=1)
    s1 = jnp.concatenate([rep(zero, 64), -sin, rep(zero, 48)], axis=1)
    s2 = jnp.concatenate([rep(zero, 80), sin, rep(zero, 32)], axis=1)
    return jnp.concatenate([c, s1, s2], axis=1)


def _misc_block(parts):
    out = []
    for name, rows in MISC:
        a = parts[name]
        if name == "w_o":
            a = a.reshape(a.shape[:-2] + (rows, 256))
        elif name == "w_uq":
            a = jnp.pad(a, [(0, 0)] * (a.ndim - 1) + [(0, 256 - a.shape[-1])])
        out.append(a)
    return jnp.concatenate(out, axis=-2)


def _misc_unblock(block):
    out, off = {}, 0
    for name, rows in MISC:
        a = block[..., off:off + rows, :]
        off += rows
        if name == "w_o":
            a = a.reshape(a.shape[:-2] + (256, D_MODEL))
        elif name == "w_uq":
            a = a[..., :192]
        out[name] = a
    return out


def _to_chip_blocks(name, a):
    if name == "w_o":
        return a.reshape(a.shape[:-2] + (N_CHIPS, a.shape[-2] // N_CHIPS, a.shape[-1]))
    return jnp.swapaxes(a.reshape(a.shape[:-1] + (N_CHIPS, a.shape[-1] // N_CHIPS)), -3, -2)


def _from_chip_blocks(name, b):
    if name == "w_o":
        return b.reshape(b.shape[:-3] + (N_CHIPS * b.shape[-2], b.shape[-1]))
    s = jnp.swapaxes(b, -3, -2)
    return s.reshape(s.shape[:-2] + (N_CHIPS * s.shape[-1],))


LARGE = ("w_in",) + tuple(n for n, _ in MISC)


def gather_small(shards):
    small = chip_exchange(_pack([shards[n] for n, _, _ in SHARDED_SMALL], F32, 8), True, "gather_small_ici")
    per_chip = [_unpack(small[k], [s for _, s, _ in SHARDED_SMALL]) for k in range(N_CHIPS)]
    return {name: jnp.concatenate([per_chip[k][idx] for k in range(N_CHIPS)], axis=axis) for idx, (name, _, axis) in enumerate(SHARDED_SMALL)}


class LocalWeights:
    def __init__(self, full):
        self.full = full
        self.grads = [None] * DEPTH

    def layer(self, i):
        return {n: self.full[n][i] for n in LARGE}

    def gather_with_attention(self, i):
        return None

    def gathered(self, dsts):
        pass

    def exchange_with_attention(self):
        return None

    def exchanged(self, dsts):
        pass

    def put_grads(self, i, grads):
        self.grads[i] = grads

    def reduced(self):
        return {n: jnp.stack([g[n] for g in self.grads]) for n in LARGE}


class MeshWeights:
    def __init__(self, shards, c, chip):
        self.c, self.chip = c, chip
        self.srcs = [shards["w_in"].astype(BF16), _misc_block({n: shards[n] for n, _ in MISC}).astype(BF16)]
        dsts = [lax.empty((DEPTH, N_CHIPS) + s.shape[1:], BF16) for s in self.srcs]
        self.dsts = gather_layer(self.srcs, dsts, 0, "gather_layer")
        self.landed = [lax.empty((DEPTH, N_CHIPS, s.shape[1] // 2, s.shape[2]), BF16) for s in self.srcs]
        self.pending = None

    def layer(self, i):
        blocks = [d[i] for d in self.dsts]
        if i == 0:
            own = (jnp.arange(N_CHIPS) == self.chip)[:, None, None]
            blocks = [jnp.where(own, s[0][None], b) for s, b in zip(self.srcs, blocks)]
        out = {"w_in": blocks[0]}
        for name, b in _misc_unblock(blocks[1]).items():
            out[name] = _from_chip_blocks(name, b)
        return out

    def gather_with_attention(self, i):
        return (self.srcs, self.dsts, i + 1) if i + 1 < DEPTH else None

    def gathered(self, dsts):
        if dsts:
            self.dsts = dsts

    def exchange_with_attention(self):
        return None if self.pending is None else (self.pending[0], self.landed, self.pending[1])

    def exchanged(self, dsts):
        if dsts:
            self.landed, self.pending = dsts, None

    def put_grads(self, i, grads):
        ps = [grads["w_in"].astype(BF16), _misc_block({n: _to_chip_blocks(n, grads[n]) for n, _ in MISC}).astype(BF16)]
        rs = swap_row_halves(ps, "reduce_swap")
        self.pending = ([add_row_half(p, r, self.c, "reduce_pair_%d" % a) for a, (p, r) in enumerate(zip(ps, rs))], i)

    def reduced(self):
        landed = exchange_layer(self.pending[0], self.landed, self.pending[1], "reduce_exchange")
        gs = [sum_row_halves(l, self.c, "reduce_sum_%d" % a) for a, l in enumerate(landed)]
        g_in, g_misc = share_row_halves(gs, "reduce_share")
        out = {"w_in": g_in}
        out.update(_misc_unblock(g_misc))
        return out


def reduce_small(grads, chip):
    names = [n for n, _ in REPLICATED] + [n for n, _, _ in SHARDED_SMALL]
    buf = _pack([grads[n] for n in names], F32, 8)
    chip_sum = add_pair(buf, sibling_swap(buf, "reduce_small_d2d"), F32, "reduce_small_pair")
    total = sum_slots(chip_exchange(chip_sum, True, "reduce_small_ici"), "reduce_small_sum")
    out = dict(zip(names, _unpack(total, [grads[n].shape for n in names])))
    for name, shape, axis in SHARDED_SMALL:
        out[name] = lax.dynamic_slice_in_dim(out[name], chip * shape[axis], shape[axis], axis)
    return out


def _prepare_small(w):
    row = lambda a: a[:, None, :]
    conf_vec = jnp.concatenate([row(w["conf_dw_b"]), row(w["conf_ln_g"]), row(w["conf_ln_b"]), jnp.zeros((DEPTH, 5, 256), F32)], axis=1)
    return dict(
        gpre=row(w["pre_norm_g"]), bias=row(w["gate_bias"]), pwbd=_block_diag(w["pool_w"]).astype(BF16), pscale=row(w["pool_scale"]),
        gq=row(w["q_norm_g"]), gkv=row(w["kv_norm_g"]), conf_w=jnp.pad(w["conf_dw_w"].astype(F32), ((0, 0), (0, 32 - CONF_K), (0, 0))),
        conf_vec=conf_vec, sc_w=jnp.pad(w["sc_dw_w"].astype(F32), ((0, 0), (0, 8 - SC_K), (0, 0))), gpost=row(w["post_norm_g"]))


def _prepare_layer(large):
    w_br, w_gl = _input_weights(large["w_in"])
    one = lambda a: a.astype(BF16)[None]
    return dict(w_br=one(w_br), w_gl=one(w_gl), wuq=one(_uq_layout(large["w_uq"])), wukv=one(_ukv_layout(large["w_ukv"])),
                woa=one(large["w_out_pool"]), wob=one(large["w_out_mla"]), woc=one(large["w_out_conf"]), wod=one(large["w_out_sc"]),
                wo=one(large["w_o"]))


def local_step(x, target, w, large):
    seq = x.shape[0]
    length = N_META + seq
    rows = -(-length // ROW_TILE) * ROW_TILE
    bt = _big_tile(rows)
    hres = _pad_rows(jnp.concatenate([w["meta_tokens"].astype(F32), x], axis=0), rows)
    tgt = jnp.pad(target, ((N_META, rows - length), (0, 0)))
    rope = _rope_tables(rows)
    sw = _prepare_small(w)

    saved = []
    for i in range(DEPTH):
        lw = _prepare_layer(large.layer(i))
        z_br, hb = prenorm_project(hres, sw["gpre"], lw["w_br"], i)
        z_gl = matmul(hb, lw["w_gl"], "nn", BF16, bt, 1024, D_MODEL, "project_gates", b_layer=0)
        ua, uc, ud, q, k, v = branches_fwd(z_br, rope, sw["pwbd"], sw["pscale"], sw["gq"], lw["wuq"], sw["gkv"], lw["wukv"],
                                           sw["conf_w"], sw["conf_vec"], sw["sc_w"], i)
        o_att, lse, dsts = attention_fwd(q, k, v, large.gather_with_attention(i))
        large.gathered(dsts)
        ub, mb, o, hnew = merge_fwd(ua, o_att, uc, ud, z_br, z_gl, sw["bias"], lw["woa"], lw["wob"], lw["woc"], lw["wod"], lw["wo"],
                                    sw["gpost"], hres, i)
        saved.append(dict(lw=lw, hres=hres, hb=hb, z_br=z_br, z_gl=z_gl, ua=ua, ub=ub, uc=uc, ud=ud, q=q, k=k, v=v, o_att=o_att,
                          lse=lse, mb=mb, o=o))
        hres = hnew

    dh, total = loss_head(hres, tgt, seq)

    g = {n: [None] * DEPTH for n in ("gpre", "bias", "pwbd", "pscale", "gq", "gkv", "conf_w", "conf_vec", "sc_w", "gpost")}
    for i in reversed(range(DEPTH)):
        s = saved[i]
        lw = s["lw"]
        dm, dwo, g["gpost"][i] = postnorm_bwd(dh, s["o"], s["mb"], lw["wo"], sw["gpost"], i)
        dua, dub, duc, dud, dz_gl, dwa, dwb, dwc, dwd, g["bias"][i] = merge_bwd(
            dm, s["ua"], s["ub"], s["uc"], s["ud"], s["z_gl"], sw["bias"], lw["woa"], lw["wob"], lw["woc"], lw["wod"], i)
        dz_br = lax.empty((rows, ZB), BF16)
        dz_br, g["pwbd"][i], g["pscale"][i] = pool_bwd(s["z_br"], dua, sw["pwbd"], sw["pscale"], dz_br, i)
        dz_br, g["sc_w"][i] = shortconv_bwd(s["z_br"], dud, sw["sc_w"], dz_br, i)
        dc, dz_br, g["conf_vec"][i] = conformer_bwd_tail(s["z_br"], duc, sw["conf_w"], sw["conf_vec"], dz_br, i)
        dz_br, g["conf_w"][i] = conformer_bwd_conv(s["z_br"], dc, sw["conf_w"], dz_br, i)
        do, dz_br, delta = attention_bwd_prep(dub, s["o_att"], s["z_br"], dz_br)
        dq, dk, dv, dsts = attention_bwd(s["q"], s["k"], s["v"], do, s["lse"], delta, large.exchange_with_attention())
        large.exchanged(dsts)
        dz_br, dwuq, dwukv, g["gq"][i], g["gkv"][i] = mla_prep_bwd(dq, dk, dv, s["z_br"], rope, sw["gq"], lw["wuq"], sw["gkv"],
                                                                 lw["wukv"], dz_br, i)
        dw_br = matmul(s["hb"], dz_br, "tn", F32, D_MODEL, ZB // 2, bt, "grad_w_branch")
        dw_gl = matmul(s["hb"], dz_gl, "tn", F32, D_MODEL, 1024, bt, "grad_w_gates")
        dh_gl = matmul(dz_gl, lw["w_gl"], "nt", F32, bt, D_MODEL, 1024, "grad_h_gates", b_layer=0)
        dh, g["gpre"][i] = prenorm_bwd(dz_br, lw["w_br"], dh_gl, s["hres"], sw["gpre"], dh, i)
        large.put_grads(i, dict(w_in=_input_weights_inverse(dw_br, dw_gl), w_out_pool=dwa, w_uq=_uq_layout_inverse(dwuq),
                                w_ukv=_ukv_layout_inverse(dwukv), w_out_mla=dwb, w_out_conf=dwc, w_out_sc=dwd, w_o=dwo))

    g = {n: jnp.stack(parts) for n, parts in g.items()}
    grads = dict(
        meta_tokens=dh[:N_META], pre_norm_g=g["gpre"][:, 0], gate_bias=g["bias"][:, 0], pool_w=_block_diag_inverse(g["pwbd"]),
        pool_scale=g["pscale"][:, 0], q_norm_g=g["gq"][:, 0], kv_norm_g=g["gkv"][:, 0], conf_dw_w=g["conf_w"][:, :CONF_K],
        conf_dw_b=g["conf_vec"][:, 2], conf_ln_g=g["conf_vec"][:, 0], conf_ln_b=g["conf_vec"][:, 1], sc_dw_w=g["sc_w"][:, :SC_K],
        post_norm_g=g["gpost"][:, 0])
    return total[0, 0], dh[N_META:length], grads


def kernel(x, meta_tokens, pre_norm_g, w_in, gate_bias, pool_w, pool_scale, w_out_pool, q_norm_g, w_uq, kv_norm_g, w_ukv, w_out_mla, conf_dw_w, conf_dw_b, conf_ln_g, conf_ln_b, w_out_conf, sc_dw_w, w_out_sc, w_o, post_norm_g, loss_target, m_meta_tokens, m_pre_norm_g, m_w_in, m_gate_bias, m_pool_w, m_pool_scale, m_w_out_pool, m_q_norm_g, m_w_uq, m_kv_norm_g, m_w_ukv, m_w_out_mla, m_conf_dw_w, m_conf_dw_b, m_conf_ln_g, m_conf_ln_b, m_w_out_conf, m_sc_dw_w, m_w_out_sc, m_w_o, m_post_norm_g, v_meta_tokens, v_pre_norm_g, v_w_in, v_gate_bias, v_pool_w, v_pool_scale, v_w_out_pool, v_q_norm_g, v_w_uq, v_kv_norm_g, v_w_ukv, v_w_out_mla, v_conf_dw_w, v_conf_dw_b, v_conf_ln_g, v_conf_ln_b, v_w_out_conf, v_sc_dw_w, v_w_out_sc, v_w_o, v_post_norm_g):
    args = locals()
    weights = {n: args[n] for n in WEIGHT_ORDER}
    c = lax.axis_index("c")
    chip = 2 * lax.axis_index("x") + lax.axis_index("y")

    small = {n: weights[n] for n, _ in REPLICATED}
    small.update(gather_small(weights))
    large = MeshWeights(weights, c, chip)
    total, dx, grads = local_step(x[0], loss_target[0], small, large)
    loss = lax.psum(total * (0.5 / D_MODEL), ("x", "y", "c"))

    reduced = large.reduced()
    reduced.update(reduce_small(grads, chip))

    flip = lambda a: jnp.swapaxes(a, 1, 2)
    deltas, new_m, new_v = [], [], []
    for n in WEIGHT_ORDER:
        operands = (weights[n], reduced[n], args["m_" + n], args["v_" + n])
        if n == "w_in":
            operands = (flip(operands[0]), lax.optimization_barrier(flip(operands[1])), flip(operands[2]), flip(operands[3]))
            reduced[n] = flip(operands[1])
        d, nm, nv = adamw(*operands)
        if n == "w_in":
            d, nm, nv = flip(d), flip(nm), flip(nv)
        deltas.append(d)
        new_m.append(nm)
        new_v.append(nv)
    return (loss, dx[None], *[reduced[n] for n in WEIGHT_ORDER], *deltas, *new_m, *new_v)
```

```python
import functools
import math

import jax
import jax.numpy as jnp
from jax import lax
from jax.experimental import pallas as pl
from jax.experimental.pallas import tpu as pltpu

F32 = jnp.float32
BF16 = jnp.bfloat16

D_MODEL = 1024
DEPTH = 4
N_META = 16
EPS = 1e-6
HEADS = 8
QK_NOPE = 64
QK_ROPE = 32
V_DIM = 64
HEAD_PAD = 128
ROPE_THETA = 10000.0
Q_SCALE = (QK_NOPE + QK_ROPE) ** -0.5
CONF_K = 31
SC_K = 3
IN_W = 7328
N_CHIPS = 4

ZB = 3328
ZG = 4096
BG, C2, XV, SG, PV, PG, CQ, CKV, KR, MG, CA, CGT, CG = (0, 256, 512, 768, 1024, 1280, 1536, 1792, 1920, 2048, 2560, 2816, 3072)

KEY_GROUP = 4
ROW_TILE = 384
HALO = 32
LANES = 128
VMEM_LIMIT = 56 * 1024 * 1024

ADAM_LR = 0.001
ADAM_B1 = 0.9
ADAM_B2 = 0.999
ADAM_EPS = 1e-08
ADAM_WD = 0.01
ADAM_STEP = 10

MESH = pl.DeviceIdType.MESH
ANY = pl.BlockSpec(memory_space=pl.ANY)

MISC = (
    ("w_out_pool", 256), ("w_ukv", 128), ("w_out_mla", 512), ("w_out_conf", 256), ("w_out_sc", 256), ("w_o", 1024), ("w_uq", 256))
SHARDED_SMALL = (
    ("meta_tokens", (N_META, 256), 1),
    ("conf_dw_w", (DEPTH, CONF_K, 64), 2),
    ("sc_dw_w", (DEPTH, SC_K, 64), 2),
)
REPLICATED = (
    ("pre_norm_g", (DEPTH, D_MODEL)),
    ("gate_bias", (DEPTH, 4 * D_MODEL)),
    ("pool_w", (DEPTH, 4, 64, 64)),
    ("pool_scale", (DEPTH, 256)),
    ("q_norm_g", (DEPTH, 256)),
    ("kv_norm_g", (DEPTH, 128)),
    ("conf_dw_b", (DEPTH, 256)),
    ("conf_ln_g", (DEPTH, 256)),
    ("conf_ln_b", (DEPTH, 256)),
    ("post_norm_g", (DEPTH, D_MODEL)),
)
WEIGHT_ORDER = ("meta_tokens", "pre_norm_g", "w_in", "gate_bias", "pool_w", "pool_scale", "w_out_pool", "q_norm_g", "w_uq",
                "kv_norm_g", "w_ukv", "w_out_mla", "conf_dw_w", "conf_dw_b", "conf_ln_g", "conf_ln_b", "w_out_conf", "sc_dw_w",
                "w_out_sc", "w_o", "post_norm_g")


def _dot(a, b):
    return lax.dot_general(a, b, (((1,), (0,)), ((), ())), preferred_element_type=F32)


def _dot_nt(a, b):
    return lax.dot_general(a, b, (((1,), (1,)), ((), ())), preferred_element_type=F32)


def _dot_tn(a, b):
    return lax.dot_general(a, b, (((0,), (0,)), ((), ())), preferred_element_type=F32)


def _sigmoid(x):
    return jax.nn.sigmoid(x)


def _silu(x):
    return x * _sigmoid(x)


def _silu_grad(x):
    s = _sigmoid(x)
    return s * (1.0 + x * (1.0 - s))


def _rms(x, g):
    return x * lax.rsqrt(jnp.mean(x * x, axis=-1, keepdims=True) + EPS) * g


def _sh(x, d):
    return x if d == 0 else pltpu.roll(x, d, 0)


def _ash(x, d):
    return x if d == 0 else pltpu.roll(x, x.shape[0] - d, 0)


def _lanes8(t):
    return jnp.concatenate([t] * HEADS, axis=1)


def _pool_window_sums(v, shift):
    a2 = v + shift(v, 1)
    a4 = a2 + shift(a2, 2)
    a8 = a4 + shift(a4, 4)
    a16 = a8 + shift(a8, 8)
    lane = lax.broadcasted_iota(jnp.int32, v.shape, 1)
    return jnp.where(lane < 64, a2, jnp.where(lane < 128, a4, jnp.where(lane < 192, a8, a16)))


def _pool_counts(first_row, rows):
    pos = first_row + lax.broadcasted_iota(jnp.int32, (rows, 256), 0)
    lane = lax.broadcasted_iota(jnp.int32, (rows, 256), 1)
    width = jnp.where(lane < 64, 2, jnp.where(lane < 128, 4, jnp.where(lane < 192, 8, 16)))
    return jnp.maximum(jnp.minimum(pos + 1, width), 1).astype(F32)


def _params(sem=None):
    return pltpu.CompilerParams(dimension_semantics=sem, vmem_limit_bytes=VMEM_LIMIT)


def _tile_specs(t, n_halo_blocks, li=0):
    per = t // HALO

    def layer(shape, idx=li):
        return pl.BlockSpec((None,) + tuple(shape), lambda i: (idx,) + (0,) * len(shape))

    def cur(c, cb=0):
        return pl.BlockSpec((t, c), lambda i: (i, cb))

    def prev(c, cb=0):
        return pl.BlockSpec((HALO, c), lambda i: (jnp.maximum(i * per - 1, 0), cb))

    def nxt(c, cb=0):
        return pl.BlockSpec((HALO, c), lambda i: (jnp.minimum((i + 1) * per, n_halo_blocks - 1), cb))

    def full(shape):
        return pl.BlockSpec(shape, lambda i: (0,) * len(shape))

    return cur, prev, nxt, full, layer


def _big_tile(rows):
    return rows // 3 if rows % (3 * LANES) == 0 else ROW_TILE


def matmul(a, b, mode, out_dtype, tm, tn, tk, name, b_layer=None):
    bs = b.shape if b_layer is None else b.shape[1:]
    lead = () if b_layer is None else (None,)
    pick = (lambda *ix: ix) if b_layer is None else (lambda *ix: (b_layer,) + ix)
    if mode == "nn":
        (m, k), n = a.shape, bs[1]
        a_spec = pl.BlockSpec((tm, tk), lambda i, j, kk: (i, kk))
        b_spec = pl.BlockSpec(lead + (tk, tn), lambda i, j, kk: pick(kk, j))
        dot = _dot
    elif mode == "nt":
        (m, k), n = a.shape, bs[0]
        a_spec = pl.BlockSpec((tm, tk), lambda i, j, kk: (i, kk))
        b_spec = pl.BlockSpec(lead + (tn, tk), lambda i, j, kk: pick(j, kk))
        dot = _dot_nt
    else:
        (k, m), n = a.shape, bs[1]
        a_spec = pl.BlockSpec((tk, tm), lambda i, j, kk: (kk, i))
        b_spec = pl.BlockSpec(lead + (tk, tn), lambda i, j, kk: pick(kk, j))
        dot = _dot_tn
    assert m % tm == 0 and n % tn == 0 and k % tk == 0, (a.shape, bs, tm, tn, tk)
    nk = k // tk

    def body(a_ref, b_ref, o_ref, acc_ref):
        kk = pl.program_id(2)

        @pl.when(kk == 0)
        def _():
            acc_ref[...] = jnp.zeros_like(acc_ref)

        acc_ref[...] += dot(a_ref[...], b_ref[...])

        @pl.when(kk == nk - 1)
        def _():
            o_ref[...] = acc_ref[...].astype(out_dtype)

    return pl.pallas_call(
        body, name=name, grid=(m // tm, n // tn, nk), in_specs=[a_spec, b_spec],
        out_specs=pl.BlockSpec((tm, tn), lambda i, j, kk: (i, j)), out_shape=jax.ShapeDtypeStruct((m, n), out_dtype),
        scratch_shapes=[pltpu.VMEM((tm, tn), F32)], compiler_params=_params(("parallel", "parallel", "arbitrary")),
    )(a, b)


def prenorm_project(hres, g, w, li):
    rows, d = hres.shape
    n = w.shape[2]
    tm, tn = _big_tile(rows), n // 2

    def body(x_ref, g_ref, w_ref, z_ref, hb_ref):
        @pl.when(pl.program_id(1) == 0)
        def _():
            hb_ref[...] = _rms(x_ref[...], g_ref[...]).astype(BF16)

        z_ref[...] = _dot(hb_ref[...], w_ref[...]).astype(BF16)

    return pl.pallas_call(
        body, name="prenorm_project", grid=(rows // tm, n // tn),
        in_specs=[pl.BlockSpec((tm, d), lambda i, j: (i, 0)), pl.BlockSpec((None, 1, d), lambda i, j: (li, 0, 0)),
                  pl.BlockSpec((None, d, tn), lambda i, j: (0, 0, j))],
        out_specs=[pl.BlockSpec((tm, tn), lambda i, j: (i, j)), pl.BlockSpec((tm, d), lambda i, j: (i, 0))],
        out_shape=[jax.ShapeDtypeStruct((rows, n), BF16), jax.ShapeDtypeStruct((rows, d), BF16)],
        compiler_params=_params(("parallel", "arbitrary")),
    )(hres, g, w)


def _rope(q, c, s1, s2, width):
    return q * c + pltpu.roll(q, width - 16, 1) * s1 + pltpu.roll(q, 16, 1) * s2


def _rope_transposed(dq, c, s1, s2, width):
    return dq * c + pltpu.roll(dq * s1, 16, 1) + pltpu.roll(dq * s2, width - 16, 1)


def _conf_conv(g1, w_ref):
    acc = jnp.zeros_like(g1)
    for k in range(CONF_K):
        acc = acc + w_ref[k:k + 1, :] * _sh(g1, CONF_K - 1 - k)
    return acc


def _conf_tail(c, cg, lg, lb):
    mu = jnp.mean(c, axis=-1, keepdims=True)
    xc = c - mu
    var = jnp.mean(xc * xc, axis=-1, keepdims=True)
    n = xc * lax.rsqrt(var + EPS) * lg + lb
    return _silu(n) * _silu(cg)


def branches_fwd(z_br, rope, pwbd, pscale, gq, wuq, gkv, wukv, conf_w, conf_vec, sc_w, li):
    rows = z_br.shape[0]
    t = ROW_TILE
    cur, prev, _, _, layer = _tile_specs(t, rows // HALO, li)

    def body(zc_ref, zp_ref, rope_ref, pw_ref, ps_ref, gq_ref, wuq_ref, gkv_ref, wukv_ref, cw_ref, cv_ref, sw_ref,
             ua_ref, uc_ref, ud_ref, q_ref, k_ref, v_ref):
        i = pl.program_id(0)
        zp = jnp.where(i == 0, jnp.zeros(zp_ref.shape, zp_ref.dtype), zp_ref[...])

        def ext(lo, w=256):
            return jnp.concatenate([zp[:, lo:lo + w], zc_ref[:, lo:lo + w]], axis=0).astype(F32)

        def col(lo, w=256):
            return zc_ref[:, lo:lo + w].astype(F32)

        v = ext(PV)
        p = (_pool_window_sums(v, _sh) / _pool_counts(i * t - HALO, t + HALO) - v)[HALO:]
        ya = _dot(p.astype(BF16), pw_ref[...]) * ps_ref[...]
        ua_ref[...] = (ya * _silu(col(PG))).astype(BF16)

        g1 = ext(CA) * _sigmoid(ext(CGT))
        c = _conf_conv(g1, cw_ref)[HALO:] + cv_ref[0:1, :]
        uc_ref[...] = _conf_tail(c, col(CG), cv_ref[1:2, :], cv_ref[2:3, :]).astype(BF16)

        e = ext(C2) * ext(XV)
        f = jnp.zeros_like(e)
        for k in range(SC_K):
            f = f + sw_ref[k:k + 1, :] * _sh(e, SC_K - 1 - k)
        ud_ref[...] = (col(BG) * f[HALO:] * _silu(col(SG))).astype(BF16)

        cth, s1, s2 = rope_ref[:, 0:128], rope_ref[:, 128:256], rope_ref[:, 256:384]
        qn = _rms(col(CQ), gq_ref[...]).astype(BF16)
        q = _dot(qn, wuq_ref[...])
        w8 = HEADS * HEAD_PAD
        q_ref[...] = (_rope(q, _lanes8(cth), _lanes8(s1), _lanes8(s2), w8) * Q_SCALE).astype(BF16)
        kvn = _rms(col(CKV, 128), gkv_ref[...]).astype(BF16)
        kv = _dot(kvn, wukv_ref[...])
        kr = _rope(col(KR, 128), cth, s1, s2, HEAD_PAD)
        k_ref[...] = (kv[:, :w8] + _lanes8(kr)).astype(BF16)
        v_ref[...] = kv[:, w8:].astype(BF16)

    outs = [jax.ShapeDtypeStruct((rows, 256), BF16)] * 3 + [jax.ShapeDtypeStruct((rows, 1024), BF16)] * 2 + [
        jax.ShapeDtypeStruct((rows, 512), BF16)]
    return pl.pallas_call(
        body, name="branches_fwd", grid=(rows // t,),
        in_specs=[cur(ZB), prev(ZB), cur(384), layer((256, 256)), layer((1, 256)), layer((1, 256)), layer((256, 1024), 0),
                  layer((1, 128)), layer((128, 1536), 0), layer((32, 256)), layer((8, 256)), layer((8, 256))],
        out_specs=[cur(256), cur(256), cur(256), cur(1024), cur(1024), cur(512)], out_shape=outs,
        compiler_params=_params(("parallel",)),
    )(z_br, z_br, rope, pwbd, pscale, gq, wuq, gkv, wukv, conf_w, conf_vec, sc_w)


def _head_lane_mask(h):
    lane = lax.broadcasted_iota(jnp.int32, (1, 2 * V_DIM), 1)
    return (lane >= V_DIM * h) & (lane < V_DIM * (h + 1))


def attention_fwd(q, k, v, gather=None):
    rows = q.shape[0]
    tq = ROW_TILE
    nq = rows // tq
    n = 0 if gather is None else len(gather[0])

    def body(*refs):
        if n:
            start, finish = _gather_ops(refs[3:3 + n], refs[5 + 2 * n:5 + 3 * n], refs[5 + 3 * n:], gather[2], True)
            pl.when((pl.program_id(0) == 0) & (pl.program_id(1) == 0))(start)
        compute(*refs[:3], *refs[3 + 2 * n:5 + 2 * n])
        if n:
            pl.when((pl.program_id(0) == HEADS // 2 - 1) & (pl.program_id(1) == nq - 1))(finish)

    def compute(q_ref, k_ref, v_ref, o_ref, lse_ref):
        i = pl.program_id(1)

        def head_step(h, tile, n_tiles, carry, masked):
            m, l, acc = carry
            width = n_tiles * tq
            r0 = pl.multiple_of(tile * tq, tq)
            kh = k_ref[pl.ds(r0, width), HEAD_PAD * h:HEAD_PAD * (h + 1)]
            vh = jnp.where(_head_lane_mask(h), v_ref[pl.ds(r0, width), :], jnp.zeros((), BF16))
            s = _dot_nt(q_ref[:, HEAD_PAD * h:HEAD_PAD * (h + 1)], kh)
            if masked:
                row = lax.broadcasted_iota(jnp.int32, (tq, width), 0)
                colm = lax.broadcasted_iota(jnp.int32, (tq, width), 1)
                s = jnp.where(colm <= row + (width - tq), s, -1e30)
            m2 = jnp.maximum(m, jnp.max(s, axis=-1, keepdims=True))
            alpha = jnp.exp(m - m2)
            pr = jnp.exp(s - m2)
            return m2, alpha * l + jnp.sum(pr, axis=-1, keepdims=True), alpha * acc + _dot(pr.astype(BF16), vh)

        def step(tile, n_tiles, carry, masked):
            return tuple(head_step(h, tile, n_tiles, carry[h], masked) for h in range(2))

        init = (jnp.full((tq, 1), -1e30, F32), jnp.zeros((tq, 1), F32), jnp.zeros((tq, 2 * V_DIM), F32))
        carry = lax.fori_loop(0, i // KEY_GROUP, lambda t, cr: step(KEY_GROUP * t, KEY_GROUP, cr, False), (init, init))
        rest = i % KEY_GROUP
        carry = lax.switch(rest, [functools.partial(lambda cr, r: step(i - r, r + 1, cr, True), r=r) for r in range(KEY_GROUP)], carry)
        out = jnp.zeros((tq, 2 * V_DIM), F32)
        for h, (m, l, acc) in enumerate(carry):
            out = out + acc / l
            lse_ref[h] = jnp.broadcast_to(m + jnp.log(l), (tq, LANES))
        o_ref[...] = out.astype(BF16)

    srcs, dsts = ([], []) if gather is None else (list(gather[0]), list(gather[1]))
    outs = pl.pallas_call(
        body, name="attention_fwd" if gather is None else "attention_fwd_gather", grid=(HEADS // 2, nq),
        in_specs=[pl.BlockSpec((tq, 2 * HEAD_PAD), lambda p, i: (i, p)), pl.BlockSpec((rows, 2 * HEAD_PAD), lambda p, i: (0, p)),
                  pl.BlockSpec((rows, 2 * V_DIM), lambda p, i: (0, p))] + [ANY] * (2 * n),
        out_specs=[pl.BlockSpec((tq, 2 * V_DIM), lambda p, i: (i, p)), pl.BlockSpec((2, tq, LANES), lambda p, i: (p, i, 0))] + [ANY] * n,
        out_shape=[jax.ShapeDtypeStruct((rows, HEADS * V_DIM), BF16), jax.ShapeDtypeStruct((HEADS, rows, LANES), F32)] + [
            jax.ShapeDtypeStruct(d.shape, d.dtype) for d in dsts],
        input_output_aliases={3 + n + a: 2 + a for a in range(n)}, scratch_shapes=GATHER_SEMS(n) if n else [],
        compiler_params=_params(("arbitrary", "arbitrary") if n else ("parallel", "parallel")),
    )(q, k, v, *srcs, *dsts)
    return outs[0], outs[1], list(outs[2:])


def merge_fwd(ua, o_att, uc, ud, z_br, z_gl, bias, woa, wob, woc, wod, wo, gpost, hres, li):
    rows = hres.shape[0]
    t = ROW_TILE
    cur, _, _, _, layer = _tile_specs(t, rows // HALO, li)
    d = D_MODEL

    def body(ua_ref, ob_ref, uc_ref, ud_ref, mg_ref, gl_ref, b_ref, woa_ref, wob_ref, woc_ref, wod_ref, wo_ref, gp_ref, h_ref,
             ub_ref, mb_ref, o_ref, hn_ref):
        ub = (ob_ref[...].astype(F32) * _silu(mg_ref[...].astype(F32))).astype(BF16)
        ub_ref[...] = ub
        m = jnp.zeros((t, d), F32)
        for idx, (u, w_ref) in enumerate(((ua_ref[...], woa_ref), (ub, wob_ref), (uc_ref[...], woc_ref), (ud_ref[...], wod_ref))):
            gate = _sigmoid(gl_ref[:, d * idx:d * (idx + 1)].astype(F32) + b_ref[:, d * idx:d * (idx + 1)])
            m = m + gate * _dot(u, w_ref[...])
        mb = m.astype(BF16)
        mb_ref[...] = mb
        o = _dot(mb, wo_ref[...])
        o_ref[...] = o
        hn_ref[...] = h_ref[...] + _rms(o, gp_ref[...])

    return pl.pallas_call(
        body, name="merge_fwd", grid=(rows // t,),
        in_specs=[cur(256), cur(512), cur(256), cur(256), cur(512, MG // 512), cur(ZG), layer((1, ZG)), layer((256, d), 0), layer((512, d), 0),
                  layer((256, d), 0), layer((256, d), 0), layer((d, d), 0), layer((1, d)), cur(d)],
        out_specs=[cur(512), cur(d), cur(d), cur(d)],
        out_shape=[jax.ShapeDtypeStruct((rows, 512), BF16), jax.ShapeDtypeStruct((rows, d), BF16), jax.ShapeDtypeStruct((rows, d), F32),
                   jax.ShapeDtypeStruct((rows, d), F32)],
        compiler_params=_params(("parallel",)),
    )(ua, o_att, uc, ud, z_br, z_gl, bias, woa, wob, woc, wod, wo, gpost, hres)


def loss_head(hres, target, n_tokens):
    rows, d = hres.shape
    t = ROW_TILE
    cur, _, _, full, _ = _tile_specs(t, rows // HALO)
    n_steps = rows // t

    def body(h_ref, t_ref, dh_ref, tot_ref, acc_ref):
        i = pl.program_id(0)

        @pl.when(i == 0)
        def _():
            acc_ref[...] = jnp.zeros_like(acc_ref)

        r = i * t + lax.broadcasted_iota(jnp.int32, (t, 1), 0)
        diff = jnp.where((r >= N_META) & (r < N_META + n_tokens), h_ref[...] - t_ref[...], 0.0)
        dh_ref[...] = diff * (1.0 / d)
        acc_ref[...] += jnp.sum(diff * diff, axis=0, keepdims=True)

        @pl.when(i == n_steps - 1)
        def _():
            tot_ref[...] = jnp.broadcast_to(jnp.sum(acc_ref[...], axis=1, keepdims=True), (1, LANES))

    return pl.pallas_call(
        body, name="loss_head", grid=(n_steps,), in_specs=[cur(d), cur(d)], out_specs=[cur(d), full((1, LANES))],
        out_shape=[jax.ShapeDtypeStruct((rows, d), F32), jax.ShapeDtypeStruct((1, LANES), F32)],
        scratch_shapes=[pltpu.VMEM((1, d), F32)], compiler_params=_params(("arbitrary",)),
    )(hres, target)


def _accumulate(i, ref, value):
    @pl.when(i == 0)
    def _():
        ref[...] = value

    @pl.when(i > 0)
    def _():
        ref[...] += value


def postnorm_bwd(dh, o, mb, wo, gpost, li):
    rows, d = dh.shape
    t = ROW_TILE
    cur, _, _, full, layer = _tile_specs(t, rows // HALO, li)

    def body(dh_ref, o_ref, mb_ref, wo_ref, gp_ref, dm_ref, dwo_ref, dgp_ref):
        i = pl.program_id(0)
        _, vjp = jax.vjp(_rms, o_ref[...], gp_ref[...])
        do, dg = vjp(dh_ref[...])
        dob = do.astype(BF16)
        dm_ref[...] = _dot_nt(dob, wo_ref[...])
        _accumulate(i, dwo_ref, _dot_tn(mb_ref[...], dob))
        _accumulate(i, dgp_ref, dg)

    return pl.pallas_call(
        body, name="postnorm_bwd", grid=(rows // t,), in_specs=[cur(d), cur(d), cur(d), layer((d, d), 0), layer((1, d))],
        out_specs=[cur(d), full((d, d)), full((1, d))],
        out_shape=[jax.ShapeDtypeStruct((rows, d), F32), jax.ShapeDtypeStruct((d, d), F32), jax.ShapeDtypeStruct((1, d), F32)],
        compiler_params=_params(("arbitrary",)),
    )(dh, o, mb, wo, gpost)


def merge_bwd(dm, ua, ub, uc, ud, z_gl, bias, woa, wob, woc, wod, li):
    rows, d = dm.shape
    t = ROW_TILE
    cur, _, _, full, layer = _tile_specs(t, rows // HALO, li)
    widths = (256, 512, 256, 256)

    def body(dm_ref, ua_ref, ub_ref, uc_ref, ud_ref, gl_ref, b_ref, woa_ref, wob_ref, woc_ref, wod_ref,
             dua_ref, dub_ref, duc_ref, dud_ref, dgl_ref, dwa_ref, dwb_ref, dwc_ref, dwd_ref, db_ref):
        i = pl.program_id(0)
        dm = dm_ref[...]
        groups = ((ua_ref, woa_ref, dua_ref, dwa_ref), (ub_ref, wob_ref, dub_ref, dwb_ref), (uc_ref, woc_ref, duc_ref, dwc_ref),
                  (ud_ref, wod_ref, dud_ref, dwd_ref))
        for idx, (u_ref, w_ref, du_ref, dw_ref) in enumerate(groups):
            cols = slice(d * idx, d * (idx + 1))
            u = u_ref[...]
            gate = _sigmoid(gl_ref[:, cols].astype(F32) + b_ref[:, cols])
            dgl = dm * _dot(u, w_ref[...]) * gate * (1.0 - gate)
            dgl_ref[:, cols] = dgl.astype(BF16)
            _accumulate(i, db_ref.at[:, cols], jnp.sum(dgl, axis=0, keepdims=True))
            dyb = (dm * gate).astype(BF16)
            du_ref[...] = _dot_nt(dyb, w_ref[...])
            _accumulate(i, dw_ref, _dot_tn(u, dyb))

    return pl.pallas_call(
        body, name="merge_bwd", grid=(rows // t,),
        in_specs=[cur(d), cur(256), cur(512), cur(256), cur(256), cur(ZG), layer((1, ZG))] + [layer((w, d), 0) for w in widths],
        out_specs=[cur(256), cur(512), cur(256), cur(256), cur(ZG)] + [full((w, d)) for w in widths] + [full((1, ZG))],
        out_shape=[jax.ShapeDtypeStruct((rows, w), F32) for w in widths] + [jax.ShapeDtypeStruct((rows, ZG), BF16)] + [
            jax.ShapeDtypeStruct((w, d), F32) for w in widths] + [jax.ShapeDtypeStruct((1, ZG), F32)],
        compiler_params=_params(("arbitrary",)),
    )(dm, ua, ub, uc, ud, z_gl, bias, woa, wob, woc, wod)


def pool_bwd(z_br, dua, pwbd, pscale, dz_buf, li):
    rows = z_br.shape[0]
    t = ROW_TILE
    n_steps = rows // t
    cur, prev, nxt, full, layer = _tile_specs(t, rows // HALO, li)

    def body(zc_ref, zp_ref, zn_ref, dc_ref, dn_ref, pw_ref, ps_ref, _, dz_ref, dpw_ref, dps_ref):
        i = pl.program_id(0)
        zp = jnp.where(i == 0, jnp.zeros(zp_ref.shape, zp_ref.dtype), zp_ref[...])
        zn = jnp.where(i == n_steps - 1, jnp.zeros(zn_ref.shape, zn_ref.dtype), zn_ref[...])
        dun = jnp.where(i == n_steps - 1, jnp.zeros(dn_ref.shape, dn_ref.dtype), dn_ref[...])

        def ext(lo):
            return jnp.concatenate([zp[:, lo:lo + 256], zc_ref[:, lo:lo + 256], zn[:, lo:lo + 256]], axis=0).astype(F32)

        n_ext = t + 2 * HALO
        v, pg = ext(PV), ext(PG)
        cnt = _pool_counts(i * t - HALO, n_ext)
        p = (_pool_window_sums(v, _sh) / cnt - v)[HALO:HALO + t]
        du = jnp.concatenate([jnp.zeros((HALO, 256), F32), dc_ref[...], dun], axis=0)
        dya = du * _silu(pg)
        dypb = (dya * ps_ref[...]).astype(BF16)
        dp = _dot_nt(dypb, pw_ref[...])
        dv = (_pool_window_sums(dp / cnt, _ash) - dp)[HALO:HALO + t]
        pb = p.astype(BF16)
        pw = _dot(pb, pw_ref[...])
        duc, pgc = dc_ref[...], pg[HALO:HALO + t]
        dpg = duc * pw * ps_ref[...] * _silu_grad(pgc)
        dz_ref[...] = jnp.concatenate([dv, dpg], axis=1).astype(BF16)
        _accumulate(i, dpw_ref, _dot_tn(pb, dypb[HALO:HALO + t]))
        _accumulate(i, dps_ref, jnp.sum(dya[HALO:HALO + t] * pw, axis=0, keepdims=True))

    return pl.pallas_call(
        body, name="pool_bwd", grid=(n_steps,),
        in_specs=[cur(ZB), prev(ZB), nxt(ZB), cur(256), nxt(256), layer((256, 256)), layer((1, 256)), ANY],
        out_specs=[cur(512, PV // 512), full((256, 256)), full((1, 256))],
        out_shape=[jax.ShapeDtypeStruct((rows, ZB), BF16), jax.ShapeDtypeStruct((256, 256), F32), jax.ShapeDtypeStruct((1, 256), F32)],
        input_output_aliases={7: 0}, compiler_params=_params(("arbitrary",)),
    )(z_br, z_br, z_br, dua, dua, pwbd, pscale, dz_buf)


def shortconv_bwd(z_br, dud, sc_w, dz_buf, li):
    rows = z_br.shape[0]
    t = ROW_TILE
    n_steps = rows // t
    cur, prev, nxt, full, layer = _tile_specs(t, rows // HALO, li)

    def body(zc_ref, zp_ref, zn_ref, dc_ref, dn_ref, sw_ref, _, dz_ref, dw_ref):
        i = pl.program_id(0)
        zp = jnp.where(i == 0, jnp.zeros(zp_ref.shape, zp_ref.dtype), zp_ref[...])
        zn = jnp.where(i == n_steps - 1, jnp.zeros(zn_ref.shape, zn_ref.dtype), zn_ref[...])
        dun = jnp.where(i == n_steps - 1, jnp.zeros(dn_ref.shape, dn_ref.dtype), dn_ref[...])

        def ext(lo):
            return jnp.concatenate([zp[:, lo:lo + 256], zc_ref[:, lo:lo + 256], zn[:, lo:lo + 256]], axis=0).astype(F32)

        mid = slice(HALO, HALO + t)
        bg, c2, xv, sg = ext(BG), ext(C2), ext(XV), ext(SG)
        du = jnp.concatenate([jnp.zeros((HALO, 256), F32), dc_ref[...], dun], axis=0)
        e = c2 * xv
        shifted = [_sh(e, SC_K - 1 - k) for k in range(SC_K)]
        f = sum(sw_ref[k:k + 1, :] * shifted[k] for k in range(SC_K))
        gate = _silu(sg)
        df = du * gate * bg
        de = sum(sw_ref[k:k + 1, :] * _ash(df, SC_K - 1 - k) for k in range(SC_K))
        dbg = du * gate * f
        dsg = du * bg * f * _silu_grad(sg)
        dz_ref[...] = jnp.concatenate([dbg[mid], (de * xv)[mid], (de * c2)[mid], dsg[mid]], axis=1).astype(BF16)
        dw = jnp.concatenate([jnp.sum((df * shifted[k])[mid], axis=0, keepdims=True) for k in range(SC_K)] + [
            jnp.zeros((8 - SC_K, 256), F32)], axis=0)
        _accumulate(i, dw_ref, dw)

    return pl.pallas_call(
        body, name="shortconv_bwd", grid=(n_steps,), in_specs=[cur(ZB), prev(ZB), nxt(ZB), cur(256), nxt(256), layer((8, 256)), ANY],
        out_specs=[cur(1024, BG // 1024), full((8, 256))],
        out_shape=[jax.ShapeDtypeStruct((rows, ZB), BF16), jax.ShapeDtypeStruct((8, 256), F32)],
        input_output_aliases={6: 0}, compiler_params=_params(("arbitrary",)),
    )(z_br, z_br, z_br, dud, dud, sc_w, dz_buf)


def conformer_bwd_tail(z_br, duc, conf_w, conf_vec, dz_buf, li):
    rows = z_br.shape[0]
    t = ROW_TILE
    cur, prev, _, full, layer = _tile_specs(t, rows // HALO, li)

    def body(zc_ref, zp_ref, du_ref, cw_ref, cv_ref, _, dc_ref, dcg_ref, dv_ref):
        i = pl.program_id(0)
        zp = jnp.where(i == 0, jnp.zeros(zp_ref.shape, zp_ref.dtype), zp_ref[...])

        def ext(lo):
            return jnp.concatenate([zp[:, lo:lo + 256], zc_ref[:, lo:lo + 256]], axis=0).astype(F32)

        g1 = ext(CA) * _sigmoid(ext(CGT))
        c = _conf_conv(g1, cw_ref)[HALO:] + cv_ref[0:1, :]
        _, vjp = jax.vjp(_conf_tail, c, zc_ref[:, CG:CG + 256].astype(F32), cv_ref[1:2, :], cv_ref[2:3, :])
        dc, dcg, dlg, dlb = vjp(du_ref[...])
        dc_ref[...] = dc
        dcg_ref[...] = dcg.astype(BF16)
        dvec = jnp.concatenate([dlg, dlb, jnp.sum(dc, axis=0, keepdims=True), jnp.zeros((5, 256), F32)], axis=0)
        _accumulate(i, dv_ref, dvec)

    return pl.pallas_call(
        body, name="conformer_bwd_tail", grid=(rows // t,), in_specs=[cur(ZB), prev(ZB), cur(256), layer((32, 256)), layer((8, 256)), ANY],
        out_specs=[cur(256), cur(256, CG // 256), full((8, 256))],
        out_shape=[jax.ShapeDtypeStruct((rows, 256), F32), jax.ShapeDtypeStruct((rows, ZB), BF16), jax.ShapeDtypeStruct((8, 256), F32)],
        input_output_aliases={5: 1}, compiler_params=_params(("arbitrary",)),
    )(z_br, z_br, duc, conf_w, conf_vec, dz_buf)


def conformer_bwd_conv(z_br, dc, conf_w, dz_buf, li):
    rows = z_br.shape[0]
    t = ROW_TILE
    n_steps = rows // t
    cur, prev, nxt, full, layer = _tile_specs(t, rows // HALO, li)

    def body(zc_ref, zp_ref, dc_ref, dn_ref, cw_ref, _, dz_ref, dw_ref):
        i = pl.program_id(0)
        zp = jnp.where(i == 0, jnp.zeros(zp_ref.shape, zp_ref.dtype), zp_ref[...])
        dcn = jnp.where(i == n_steps - 1, jnp.zeros(dn_ref.shape, dn_ref.dtype), dn_ref[...])

        def ext(lo):
            return jnp.concatenate([zp[:, lo:lo + 256], zc_ref[:, lo:lo + 256]], axis=0).astype(F32)

        a, gt = ext(CA), ext(CGT)
        sg = _sigmoid(gt)
        g1 = a * sg
        dc = dc_ref[...]
        dce = jnp.concatenate([dc, dcn], axis=0)
        dg1 = jnp.zeros_like(dce)
        dws = []
        for k in range(CONF_K):
            dg1 = dg1 + cw_ref[k:k + 1, :] * _ash(dce, CONF_K - 1 - k)
            dws.append(jnp.sum(dc * _sh(g1, CONF_K - 1 - k)[HALO:], axis=0, keepdims=True))
        dg1 = dg1[:t]
        ac, sc = a[HALO:], sg[HALO:]
        dz_ref[...] = jnp.concatenate([dg1 * sc, dg1 * ac * sc * (1.0 - sc)], axis=1).astype(BF16)
        _accumulate(i, dw_ref, jnp.concatenate(dws + [jnp.zeros((32 - CONF_K, 256), F32)], axis=0))

    return pl.pallas_call(
        body, name="conformer_bwd_conv", grid=(n_steps,), in_specs=[cur(ZB), prev(ZB), cur(256), nxt(256), layer((32, 256)), ANY],
        out_specs=[cur(512, CA // 512), full((32, 256))],
        out_shape=[jax.ShapeDtypeStruct((rows, ZB), BF16), jax.ShapeDtypeStruct((32, 256), F32)],
        input_output_aliases={5: 0}, compiler_params=_params(("arbitrary",)),
    )(z_br, z_br, dc, dc, conf_w, dz_buf)


def attention_bwd_prep(dub, o_att, z_br, dz_buf):
    rows = dub.shape[0]
    t = ROW_TILE
    cur, _, _, _, _ = _tile_specs(t, rows // HALO)

    def body(du_ref, o_ref, mg_ref, _, do_ref, dmg_ref, delta_ref):
        du, o, mg = du_ref[...], o_ref[...].astype(F32), mg_ref[...].astype(F32)
        do = du * _silu(mg)
        do_ref[...] = do.astype(BF16)
        dmg_ref[...] = (du * o * _silu_grad(mg)).astype(BF16)
        prod = do * o
        lane = lax.broadcasted_iota(jnp.int32, (1, HEADS * V_DIM), 1)
        for h in range(HEADS):
            part = jnp.where((lane >= V_DIM * h) & (lane < V_DIM * (h + 1)), prod, 0.0)
            delta_ref[h] = jnp.broadcast_to(jnp.sum(part, axis=-1, keepdims=True), (t, LANES))

    return pl.pallas_call(
        body, name="attention_bwd_prep", grid=(rows // t,), in_specs=[cur(512), cur(512), cur(512, MG // 512), ANY],
        out_specs=[cur(512), cur(512, MG // 512), pl.BlockSpec((HEADS, t, LANES), lambda i: (0, i, 0))],
        out_shape=[jax.ShapeDtypeStruct((rows, 512), BF16), jax.ShapeDtypeStruct((rows, ZB), BF16),
                   jax.ShapeDtypeStruct((HEADS, rows, LANES), F32)],
        input_output_aliases={3: 1}, compiler_params=_params(("parallel",)),
    )(dub, o_att, z_br, dz_buf)


def attention_bwd(q, k, v, do, lse, delta, exchange=None):
    rows = q.shape[0]
    tq = ROW_TILE
    nq = rows // tq
    n = 0 if exchange is None else len(exchange[0])

    def body(*refs):
        if n:
            start, finish = _exchange_ops(refs[6:6 + n], refs[9 + 2 * n:9 + 3 * n], refs[9 + 3 * n:], exchange[2])
            pl.when((pl.program_id(0) == 0) & (pl.program_id(1) == 0))(start)
        compute(*refs[:6], *refs[6 + 2 * n:9 + 2 * n])
        if n:
            pl.when((pl.program_id(0) == HEADS // 2 - 1) & (pl.program_id(1) == nq - 1))(finish)

    def compute(q_ref, k_ref, v_ref, do_ref, lse_ref, dl_ref, dq_ref, dk_ref, dv_ref):
        j = pl.program_id(1)

        @pl.when(j == 0)
        def _():
            dq_ref[...] = jnp.zeros_like(dq_ref)

        def head_step(h, tile, n_tiles, dk, dv, diagonal):
            lanes = slice(HEAD_PAD * h, HEAD_PAD * (h + 1))
            hm = _head_lane_mask(h)
            kh = k_ref[:, lanes]
            vh = jnp.where(hm, v_ref[...], jnp.zeros((), BF16))
            r0, width = pl.multiple_of(tile * tq, tq), n_tiles * tq
            qi = q_ref[pl.ds(r0, width), lanes]
            doi = jnp.where(hm, do_ref[pl.ds(r0, width), :], jnp.zeros((), BF16))
            s = _dot_nt(qi, kh)
            if diagonal:
                s = jnp.where(lax.broadcasted_iota(jnp.int32, (tq, tq), 1) <= lax.broadcasted_iota(jnp.int32, (tq, tq), 0), s, -1e30)
            pr = jnp.exp(s - lse_ref[h, pl.ds(r0, width), :][:, 0:1])
            dv = dv + _dot_tn(pr.astype(BF16), doi)
            dp = _dot_nt(doi, vh)
            ds = (pr * (dp - dl_ref[h, pl.ds(r0, width), :][:, 0:1])).astype(BF16)
            dq_ref[pl.ds(r0, width), lanes] += _dot(ds, kh)
            return dk + _dot_tn(ds, qi), dv

        def step(tile, n_tiles, carry, diagonal):
            dk0, dk1, dv = carry
            dk0, dv = head_step(0, tile, n_tiles, dk0, dv, diagonal)
            dk1, dv = head_step(1, tile, n_tiles, dk1, dv, diagonal)
            return dk0, dk1, dv

        zero = jnp.zeros((tq, HEAD_PAD), F32)
        carry = step(j, 1, (zero, zero, jnp.zeros((tq, 2 * V_DIM), F32)), True)
        odd = (nq - 1 - j) % 2
        carry = lax.cond(odd == 1, lambda cr: step(j + 1, 1, cr, False), lambda cr: cr, carry)
        dk0, dk1, dv = lax.fori_loop(0, (nq - 1 - j) // 2, lambda t, cr: step(j + 1 + odd + 2 * t, 2, cr, False), carry)
        dk_ref[:, 0:HEAD_PAD] = dk0
        dk_ref[:, HEAD_PAD:2 * HEAD_PAD] = dk1
        dv_ref[...] = dv

    srcs, dsts = ([], []) if exchange is None else (list(exchange[0]), list(exchange[1]))
    outs = pl.pallas_call(
        body, name="attention_bwd" if exchange is None else "attention_bwd_exchange", grid=(HEADS // 2, nq),
        in_specs=[pl.BlockSpec((rows, 2 * HEAD_PAD), lambda p, j: (0, p)), pl.BlockSpec((tq, 2 * HEAD_PAD), lambda p, j: (j, p)),
                  pl.BlockSpec((tq, 2 * V_DIM), lambda p, j: (j, p)), pl.BlockSpec((rows, 2 * V_DIM), lambda p, j: (0, p)),
                  pl.BlockSpec((2, rows, LANES), lambda p, j: (p, 0, 0)), pl.BlockSpec((2, rows, LANES), lambda p, j: (p, 0, 0))] + [
                      ANY] * (2 * n),
        out_specs=[pl.BlockSpec((rows, 2 * HEAD_PAD), lambda p, j: (0, p)), pl.BlockSpec((tq, 2 * HEAD_PAD), lambda p, j: (j, p)),
                   pl.BlockSpec((tq, 2 * V_DIM), lambda p, j: (j, p))] + [ANY] * n,
        out_shape=[jax.ShapeDtypeStruct((rows, HEADS * HEAD_PAD), F32), jax.ShapeDtypeStruct((rows, HEADS * HEAD_PAD), F32),
                   jax.ShapeDtypeStruct((rows, HEADS * V_DIM), F32)] + [jax.ShapeDtypeStruct(d.shape, d.dtype) for d in dsts],
        input_output_aliases={6 + n + a: 3 + a for a in range(n)}, scratch_shapes=EXCHANGE_SEMS(n) if n else [],
        compiler_params=_params(("arbitrary", "arbitrary") if n else ("parallel", "arbitrary")),
    )(q, k, v, do, lse, delta, *srcs, *dsts)
    return outs[0], outs[1], outs[2], list(outs[3:])


def mla_prep_bwd(dq, dk, dv, z_br, rope, gq, wuq, gkv, wukv, dz_buf, li):
    rows = dq.shape[0]
    t = ROW_TILE
    cur, _, _, full, layer = _tile_specs(t, rows // HALO, li)
    w8 = HEADS * HEAD_PAD

    def body(dq_ref, dk_ref, dv_ref, z_ref, rope_ref, gq_ref, wuq_ref, gkv_ref, wukv_ref, _, dz_ref, dwuq_ref, dwukv_ref, dgq_ref, dgkv_ref):
        i = pl.program_id(0)
        cth, s1, s2 = rope_ref[:, 0:128], rope_ref[:, 128:256], rope_ref[:, 256:384]
        dqb = _rope_transposed(dq_ref[...] * Q_SCALE, _lanes8(cth), _lanes8(s1), _lanes8(s2), w8).astype(BF16)
        cq = z_ref[:, 0:256].astype(F32)
        qn, vjp_q = jax.vjp(_rms, cq, gq_ref[...])
        _accumulate(i, dwuq_ref, _dot_tn(qn.astype(BF16), dqb))
        dcq, dgq = vjp_q(_dot_nt(dqb, wuq_ref[...]))
        _accumulate(i, dgq_ref, dgq)

        dk = dk_ref[...]
        dkr = sum(dk[:, HEAD_PAD * h:HEAD_PAD * (h + 1)] for h in range(HEADS))
        dkr = _rope_transposed(dkr, cth, s1, s2, HEAD_PAD)
        lane = lax.broadcasted_iota(jnp.int32, (1, HEAD_PAD), 1)
        dkr = jnp.where((lane >= QK_NOPE) & (lane < QK_NOPE + QK_ROPE), dkr, 0.0)
        dkvb = jnp.concatenate([dk, dv_ref[...]], axis=1).astype(BF16)
        ckv = z_ref[:, 256:384].astype(F32)
        kvn, vjp_kv = jax.vjp(_rms, ckv, gkv_ref[...])
        _accumulate(i, dwukv_ref, _dot_tn(kvn.astype(BF16), dkvb))
        dckv, dgkv = vjp_kv(_dot_nt(dkvb, wukv_ref[...]))
        _accumulate(i, dgkv_ref, dgkv)
        dz_ref[...] = jnp.concatenate([dcq, dckv, dkr], axis=1).astype(BF16)

    return pl.pallas_call(
        body, name="mla_prep_bwd", grid=(rows // t,),
        in_specs=[cur(w8), cur(w8), cur(512), cur(512, CQ // 512), cur(384), layer((1, 256)), layer((256, w8), 0), layer((1, 128)),
                  layer((128, w8 + 512), 0), ANY],
        out_specs=[cur(512, CQ // 512), full((256, w8)), full((128, w8 + 512)), full((1, 256)), full((1, 128))],
        out_shape=[jax.ShapeDtypeStruct((rows, ZB), BF16), jax.ShapeDtypeStruct((256, w8), F32), jax.ShapeDtypeStruct((128, w8 + 512), F32),
                   jax.ShapeDtypeStruct((1, 256), F32), jax.ShapeDtypeStruct((1, 128), F32)],
        input_output_aliases={9: 0}, compiler_params=_params(("arbitrary",)),
    )(dq, dk, dv, z_br, rope, gq, wuq, gkv, wukv, dz_buf)


def prenorm_bwd(dz_br, w_br, dh_gl, hres, gpre, dh_next, li):
    rows, d = hres.shape
    t = ROW_TILE
    cur, _, _, full, layer = _tile_specs(t, rows // HALO, li)

    def body(dz_ref, w_ref, dp_ref, x_ref, g_ref, dn_ref, dx_ref, dg_ref):
        i = pl.program_id(0)
        dh = _dot_nt(dz_ref[...], w_ref[...]) + dp_ref[...]
        _, vjp = jax.vjp(_rms, x_ref[...], g_ref[...])
        dx, dg = vjp(dh)
        dx_ref[...] = dx + dn_ref[...]
        _accumulate(i, dg_ref, dg)

    return pl.pallas_call(
        body, name="prenorm_bwd", grid=(rows // t,), in_specs=[cur(ZB), layer((d, ZB), 0), cur(d), cur(d), layer((1, d)), cur(d)],
        out_specs=[cur(d), full((1, d))], out_shape=[jax.ShapeDtypeStruct((rows, d), F32), jax.ShapeDtypeStruct((1, d), F32)],
        compiler_params=_params(("arbitrary",)),
    )(dz_br, w_br, dh_gl, hres, gpre, dh_next)


def _mesh_position():
    return lax.axis_index("x"), lax.axis_index("y"), lax.axis_index("c")


def chip_exchange(src, gather, name):
    block = src.shape if gather else src.shape[1:]

    def body(src_ref, dst_ref, send_sems, recv_sems, local_sem):
        x, y, c = _mesh_position()
        me = 2 * x + y
        peers = ((1 - x, y), (x, 1 - y), (1 - x, 1 - y))

        def part(k):
            return src_ref if gather else src_ref.at[k]

        def copy(j, slot):
            px, py = peers[j]
            return pltpu.make_async_remote_copy(src_ref=part(2 * px + py), dst_ref=dst_ref.at[slot], send_sem=send_sems.at[j],
                                                recv_sem=recv_sems.at[j], device_id=(px, py, c), device_id_type=MESH)

        local = pltpu.make_async_copy(part(me), dst_ref.at[me], local_sem)
        local.start()
        sends = [copy(j, me) for j in range(3)]
        for cp in sends:
            cp.start()
        for j, (px, py) in enumerate(peers):
            copy(j, 2 * px + py).wait_recv()
        for cp in sends:
            cp.wait_send()
        local.wait()

    return pl.pallas_call(
        body, name=name, in_specs=[pl.BlockSpec(memory_space=pl.ANY)], out_specs=pl.BlockSpec(memory_space=pl.ANY),
        out_shape=jax.ShapeDtypeStruct((N_CHIPS,) + tuple(block), src.dtype),
        scratch_shapes=[pltpu.SemaphoreType.DMA((3,)), pltpu.SemaphoreType.DMA((3,)), pltpu.SemaphoreType.DMA(())],
    )(src)


def sibling_swap(src, name):
    def body(src_ref, dst_ref, send_sem, recv_sem):
        x, y, c = _mesh_position()
        cp = pltpu.make_async_remote_copy(src_ref=src_ref, dst_ref=dst_ref, send_sem=send_sem, recv_sem=recv_sem,
                                          device_id=(x, y, 1 - c), device_id_type=MESH)
        cp.start()
        cp.wait()

    return pl.pallas_call(
        body, name=name, in_specs=[pl.BlockSpec(memory_space=pl.ANY)], out_specs=pl.BlockSpec(memory_space=pl.ANY),
        out_shape=jax.ShapeDtypeStruct(src.shape, src.dtype),
        scratch_shapes=[pltpu.SemaphoreType.DMA(()), pltpu.SemaphoreType.DMA(())],
    )(src)


def _comm_call(body, name, n_in, out_shapes, n_sems):
    return pl.pallas_call(
        body, name=name, in_specs=[ANY] * n_in, out_specs=[ANY] * len(out_shapes), out_shape=out_shapes,
        scratch_shapes=[pltpu.SemaphoreType.DMA((n,)) for n in n_sems])


def _row_halves(c, rows):
    half = rows // 2
    return pl.ds(pl.multiple_of(c * half, 16), half), pl.ds(pl.multiple_of((1 - c) * half, 16), half)


def _peers():
    x, y, c = _mesh_position()
    return x, y, c, 2 * x + y, ((1 - x, y), (x, 1 - y), (1 - x, 1 - y))


def _gather_ops(src, dst, sems, layer, own_copy):
    ici_send, ici_recv, d2d_send, d2d_recv, own_sems = sems
    n = len(src)

    def fetch(a, j, slot):
        x, y, c, _, peers = _peers()
        px, py = peers[j]
        mine, _ = _row_halves(c, src[a].shape[1])
        return pltpu.make_async_remote_copy(src_ref=src[a].at[layer, mine], dst_ref=dst[a].at[layer, slot, mine], send_sem=ici_send.at[3 * a + j],
                                            recv_sem=ici_recv.at[3 * a + j], device_id=(px, py, c), device_id_type=MESH)

    def forward(a, j, sibling_half):
        x, y, c, _, peers = _peers()
        px, py = peers[j]
        part = dst[a].at[layer, 2 * px + py, _row_halves(c, src[a].shape[1])[1 if sibling_half else 0]]
        return pltpu.make_async_remote_copy(src_ref=part, dst_ref=part, send_sem=d2d_send.at[3 * a + j], recv_sem=d2d_recv.at[3 * a + j],
                                            device_id=(x, y, 1 - c), device_id_type=MESH)

    def own(a):
        return pltpu.make_async_copy(src[a].at[layer], dst[a].at[layer, _peers()[3]], own_sems.at[a])

    def start():
        me = _peers()[3]
        for a in range(n):
            if own_copy:
                own(a).start()
            for j in range(3):
                fetch(a, j, me).start()

    def finish():
        peers = _peers()[4]
        for j, (px, py) in enumerate(peers):
            for a in range(n):
                fetch(a, j, 2 * px + py).wait_recv()
                forward(a, j, False).start()
        for j in range(3):
            for a in range(n):
                forward(a, j, True).wait_recv()
        for j in range(3):
            for a in range(n):
                fetch(a, j, 0).wait_send()
                forward(a, j, False).wait_send()
        if own_copy:
            for a in range(n):
                own(a).wait()

    return start, finish


def _exchange_ops(src, dst, sems, layer):
    send_sems, recv_sems, own_sems = sems
    n = len(src)

    def copy(a, j, slot):
        x, y, c, _, peers = _peers()
        px, py = peers[j]
        return pltpu.make_async_remote_copy(src_ref=src[a].at[2 * px + py], dst_ref=dst[a].at[layer, slot], send_sem=send_sems.at[3 * a + j],
                                            recv_sem=recv_sems.at[3 * a + j], device_id=(px, py, c), device_id_type=MESH)

    def own(a):
        me = _peers()[3]
        return pltpu.make_async_copy(src[a].at[me], dst[a].at[layer, me], own_sems.at[a])

    def start():
        me = _peers()[3]
        for a in range(n):
            own(a).start()
            for j in range(3):
                copy(a, j, me).start()

    def finish():
        peers = _peers()[4]
        for j, (px, py) in enumerate(peers):
            for a in range(n):
                copy(a, j, 2 * px + py).wait_recv()
        for j in range(3):
            for a in range(n):
                copy(a, j, 0).wait_send()
        for a in range(n):
            own(a).wait()

    return start, finish


GATHER_SEMS = lambda n: [pltpu.SemaphoreType.DMA((3 * n,))] * 4 + [pltpu.SemaphoreType.DMA((n,))]
EXCHANGE_SEMS = lambda n: [pltpu.SemaphoreType.DMA((3 * n,))] * 2 + [pltpu.SemaphoreType.DMA((n,))]


def gather_layer(srcs, dsts, layer, name):
    n = len(srcs)

    def body(*refs):
        start, finish = _gather_ops(refs[:n], refs[2 * n:3 * n], refs[3 * n:], layer, False)
        start()
        finish()

    return pl.pallas_call(
        body, name=name, in_specs=[ANY] * (2 * n), out_specs=[ANY] * n, out_shape=[jax.ShapeDtypeStruct(d.shape, d.dtype) for d in dsts],
        input_output_aliases={n + a: a for a in range(n)}, scratch_shapes=GATHER_SEMS(n),
    )(*srcs, *dsts)


def exchange_layer(ss, dsts, layer, name):
    n = len(ss)

    def body(*refs):
        start, finish = _exchange_ops(refs[:n], refs[2 * n:3 * n], refs[3 * n:], layer)
        start()
        finish()

    return pl.pallas_call(
        body, name=name, in_specs=[ANY] * (2 * n), out_specs=[ANY] * n, out_shape=[jax.ShapeDtypeStruct(d.shape, d.dtype) for d in dsts],
        input_output_aliases={n + a: a for a in range(n)}, scratch_shapes=EXCHANGE_SEMS(n),
    )(*ss, *dsts)


def swap_row_halves(ps, name):
    n = len(ps)

    def body(*refs):
        src, dst = refs[:n], refs[n:2 * n]
        send_sems, recv_sems = refs[2 * n:]
        x, y, c = _mesh_position()
        copies = [pltpu.make_async_remote_copy(src_ref=src[a].at[:, _row_halves(c, src[a].shape[1])[1]], dst_ref=dst[a], send_sem=send_sems.at[a],
                                               recv_sem=recv_sems.at[a], device_id=(x, y, 1 - c), device_id_type=MESH) for a in range(n)]
        for cp in copies:
            cp.start()
        for cp in copies:
            cp.wait()

    outs = [jax.ShapeDtypeStruct((p.shape[0], p.shape[1] // 2, p.shape[2]), p.dtype) for p in ps]
    return _comm_call(body, name, n, outs, (n, n))(*ps)


def add_row_half(p, r, c, name):
    n, half, cols = r.shape
    rb = _row_block(half, cols, 2)
    steps = half // rb

    def body(c_ref, p_ref, r_ref, o_ref):
        o_ref[...] = (p_ref[...].astype(F32) + r_ref[...].astype(F32)).astype(BF16)

    return pl.pallas_call(
        body, name=name, out_shape=jax.ShapeDtypeStruct(r.shape, BF16),
        grid_spec=pltpu.PrefetchScalarGridSpec(
            num_scalar_prefetch=1, grid=(n, steps),
            in_specs=[pl.BlockSpec((1, rb, cols), lambda k, i, c_ref: (k, c_ref[0] * steps + i, 0)),
                      pl.BlockSpec((1, rb, cols), lambda k, i, c_ref: (k, i, 0))],
            out_specs=pl.BlockSpec((1, rb, cols), lambda k, i, c_ref: (k, i, 0))),
        compiler_params=_params(("parallel", "parallel")),
    )(jnp.reshape(c, (1,)).astype(jnp.int32), p, r)


def sum_row_halves(l, c, name):
    layers, n, half, cols = l.shape
    rb = _row_block(half, cols, 4)
    steps = half // rb

    def body(c_ref, l_ref, o_ref):
        acc = l_ref[0, 0].astype(F32)
        for s in range(1, n):
            acc = acc + l_ref[0, s].astype(F32)
        o_ref[0] = acc

    return pl.pallas_call(
        body, name=name, out_shape=jax.ShapeDtypeStruct((layers, 2 * half, cols), F32),
        grid_spec=pltpu.PrefetchScalarGridSpec(
            num_scalar_prefetch=1, grid=(layers, steps), in_specs=[pl.BlockSpec((1, n, rb, cols), lambda a, i, c_ref: (a, 0, i, 0))],
            out_specs=pl.BlockSpec((1, rb, cols), lambda a, i, c_ref: (a, c_ref[0] * steps + i, 0))),
        compiler_params=_params(("parallel", "parallel")),
    )(jnp.reshape(c, (1,)).astype(jnp.int32), l)


def share_row_halves(gs, name):
    n = len(gs)

    def body(*refs):
        dst = refs[n:2 * n]
        send_sems, recv_sems = refs[2 * n:]
        x, y, c = _mesh_position()

        def copy(a, sibling_half):
            part = dst[a].at[:, _row_halves(c, dst[a].shape[1])[1 if sibling_half else 0]]
            return pltpu.make_async_remote_copy(src_ref=part, dst_ref=part, send_sem=send_sems.at[a], recv_sem=recv_sems.at[a],
                                                device_id=(x, y, 1 - c), device_id_type=MESH)

        for a in range(n):
            copy(a, False).start()
        for a in range(n):
            copy(a, True).wait_recv()
        for a in range(n):
            copy(a, False).wait_send()

    return pl.pallas_call(
        body, name=name, in_specs=[ANY] * n, out_specs=[ANY] * n, out_shape=[jax.ShapeDtypeStruct(g.shape, g.dtype) for g in gs],
        input_output_aliases={a: a for a in range(n)}, scratch_shapes=[pltpu.SemaphoreType.DMA((n,)), pltpu.SemaphoreType.DMA((n,))],
    )(*gs)


def _row_block(rows, cols, itemsize):
    best = 16
    for rb in range(16, rows + 1, 16):
        if rows % rb == 0 and rb * cols * itemsize <= 2 * 1024 * 1024:
            best = rb
    assert rows % best == 0, (rows, cols)
    return best


def _comm_block(rows):
    return 1024 if rows % 1024 == 0 else rows


def sum_slots(buf, name):
    n, r, c = buf.shape
    rb = _comm_block(r)

    def body(b_ref, o_ref):
        acc = b_ref[0].astype(F32)
        for s in range(1, n):
            acc = acc + b_ref[s].astype(F32)
        o_ref[...] = acc

    return pl.pallas_call(
        body, name=name, grid=(r // rb,), in_specs=[pl.BlockSpec((n, rb, c), lambda i: (0, i, 0))],
        out_specs=pl.BlockSpec((rb, c), lambda i: (i, 0)), out_shape=jax.ShapeDtypeStruct((r, c), F32),
        compiler_params=_params(("parallel",)),
    )(buf)


def add_pair(a, b, out_dtype, name):
    shape = a.shape
    a2, b2 = a.reshape(-1, shape[-1]), b.reshape(-1, shape[-1])
    r, c = a2.shape
    rb = _comm_block(r)

    def body(a_ref, b_ref, o_ref):
        o_ref[...] = (a_ref[...].astype(F32) + b_ref[...].astype(F32)).astype(out_dtype)

    out = pl.pallas_call(
        body, name=name, grid=(r // rb,), in_specs=[pl.BlockSpec((rb, c), lambda i: (i, 0))] * 2,
        out_specs=pl.BlockSpec((rb, c), lambda i: (i, 0)), out_shape=jax.ShapeDtypeStruct((r, c), out_dtype),
        compiler_params=_params(("parallel",)),
    )(a2, b2)
    return out.reshape(shape)


def adamw(w, g, m, v):
    shape = w.shape
    cols = shape[-1]
    rows = math.prod(shape[:-1])
    if rows * cols <= 256 * 1024:
        rb, cb = rows, cols
    else:
        rb = max(r for r in range(8, 2049, 8) if rows % r == 0)
        cb = cols if rb * cols * 4 <= 2 * 1024 * 1024 else 256
    assert rows % rb == 0 and cols % cb == 0, shape

    def body(w_ref, g_ref, m_ref, v_ref, d_ref, nm_ref, nv_ref):
        g_ = g_ref[...]
        nm = ADAM_B1 * m_ref[...] + (1.0 - ADAM_B1) * g_
        nv = ADAM_B2 * v_ref[...] + (1.0 - ADAM_B2) * (g_ * g_)
        m_hat = nm / (1.0 - ADAM_B1 ** ADAM_STEP)
        v_hat = nv / (1.0 - ADAM_B2 ** ADAM_STEP)
        d_ref[...] = -ADAM_LR * (m_hat / (jnp.sqrt(v_hat) + ADAM_EPS) + ADAM_WD * w_ref[...])
        nm_ref[...] = nm
        nv_ref[...] = nv

    spec = pl.BlockSpec((rb, cb), lambda i, j: (i, j))
    outs = pl.pallas_call(
        body, name="adamw", grid=(rows // rb, cols // cb), in_specs=[spec] * 4, out_specs=[spec] * 3,
        out_shape=[jax.ShapeDtypeStruct((rows, cols), F32)] * 3, compiler_params=_params(("parallel", "parallel")),
    )(*(a.reshape(rows, cols) for a in (w, g, m, v)))
    return tuple(o.reshape(shape) for o in outs)


def _pack(arrays, dtype, row_multiple):
    flat = jnp.concatenate([a.astype(dtype).reshape(-1) for a in arrays])
    per = LANES * row_multiple
    total = -(-flat.shape[0] // per) * per
    return jnp.pad(flat, (0, total - flat.shape[0])).reshape(total // LANES, LANES)


def _unpack(buf, shapes):
    flat = buf.reshape(-1)
    out, off = [], 0
    for s in shapes:
        n = math.prod(s)
        out.append(flat[off:off + n].reshape(s))
        off += n
    return out


def _input_weights(blocks):
    c0, c1, c2, c3 = (blocks[..., k, :, :] for k in range(N_CHIPS))
    pad = lambda n: jnp.zeros(c0.shape[:-1] + (n,), blocks.dtype)
    w_br = jnp.concatenate([c1[..., 376:1400], c0[..., 0:896], pad(64), c0[..., 896:928], pad(32), c0[..., 928:], c1[..., 0:376]], axis=-1)
    return w_br, jnp.concatenate([c1[..., 1400:], c2, c3], axis=-1)


def _input_weights_inverse(dw_br, dw_gl):
    c0 = jnp.concatenate([dw_br[..., 1024:1920], dw_br[..., 1984:2016], dw_br[..., 2048:2952]], axis=-1)
    c1 = jnp.concatenate([dw_br[..., 2952:ZB], dw_br[..., 0:1024], dw_gl[..., 0:432]], axis=-1)
    return jnp.stack([c0, c1, dw_gl[..., 432:2264], dw_gl[..., 2264:]], axis=-3)


def _uq_layout(w):
    r = w.reshape(w.shape[:-1] + (HEADS, QK_NOPE + QK_ROPE))
    r = jnp.pad(r, [(0, 0)] * (r.ndim - 1) + [(0, HEAD_PAD - QK_NOPE - QK_ROPE)])
    return r.reshape(w.shape[:-1] + (HEADS * HEAD_PAD,))


def _uq_layout_inverse(dw):
    r = dw.reshape(dw.shape[:-1] + (HEADS, HEAD_PAD))[..., :QK_NOPE + QK_ROPE]
    return r.reshape(dw.shape[:-1] + (HEADS * (QK_NOPE + QK_ROPE),))


def _ukv_layout(w):
    r = w.reshape(w.shape[:-1] + (HEADS, QK_NOPE + V_DIM))
    kp = jnp.pad(r[..., :QK_NOPE], [(0, 0)] * (r.ndim - 1) + [(0, HEAD_PAD - QK_NOPE)]).reshape(w.shape[:-1] + (HEADS * HEAD_PAD,))
    return jnp.concatenate([kp, r[..., QK_NOPE:].reshape(w.shape[:-1] + (HEADS * V_DIM,))], axis=-1)


def _ukv_layout_inverse(dw):
    lead = dw.shape[:-1]
    dk = dw[..., :HEADS * HEAD_PAD].reshape(lead + (HEADS, HEAD_PAD))[..., :QK_NOPE]
    dv = dw[..., HEADS * HEAD_PAD:].reshape(lead + (HEADS, V_DIM))
    return jnp.concatenate([dk, dv], axis=-1).reshape(lead + (HEADS * (QK_NOPE + V_DIM),))


def _block_diag(pw):
    zeros = lambda n: jnp.zeros(pw.shape[:-3] + (64, n), pw.dtype)
    rows = [jnp.concatenate([zeros(64 * g), pw[..., g, :, :], zeros(64 * (3 - g))], axis=-1) for g in range(4)]
    return jnp.concatenate(rows, axis=-2)


def _block_diag_inverse(d):
    return jnp.stack([d[..., 64 * g:64 * (g + 1), 64 * g:64 * (g + 1)] for g in range(4)], axis=-3)


def _pad_rows(a, n):
    return jnp.pad(a, ((0, n - a.shape[0]), (0, 0)))


def _rope_tables(rows):
    inv = 1.0 / (ROPE_THETA ** (jnp.arange(0, QK_ROPE, 2, dtype=F32) / QK_ROPE))
    ang = jnp.arange(rows, dtype=F32)[:, None] * inv[None, :]
    cos, sin = jnp.cos(ang), jnp.sin(ang)
    one, zero = jnp.ones((rows, 1), F32), jnp.zeros((rows, 1), F32)
    rep = lambda a, n: jnp.broadcast_to(a, (rows, n))
    c = jnp.concatenate([rep(one, 64), cos, cos, rep(one, 32)], axis=1)
    s1 = jnp.concatenate([rep(zero, 64), -sin, rep(zero, 48)], axis=1)
    s2 = jnp.concatenate([rep(zero, 80), sin, rep(zero, 32)], axis=1)
    return jnp.concatenate([c, s1, s2], axis=1)


def _misc_block(parts):
    out = []
    for name, rows in MISC:
        a = parts[name]
        if name == "w_o":
            a = a.reshape(a.shape[:-2] + (rows, 256))
        elif name == "w_uq":
            a = jnp.pad(a, [(0, 0)] * (a.ndim - 1) + [(0, 256 - a.shape[-1])])
        out.append(a)
    return jnp.concatenate(out, axis=-2)


def _misc_unblock(block):
    out, off = {}, 0
    for name, rows in MISC:
        a = block[..., off:off + rows, :]
        off += rows
        if name == "w_o":
            a = a.reshape(a.shape[:-2] + (256, D_MODEL))
        elif name == "w_uq":
            a = a[..., :192]
        out[name] = a
    return out


def _to_chip_blocks(name, a):
    if name == "w_o":
        return a.reshape(a.shape[:-2] + (N_CHIPS, a.shape[-2] // N_CHIPS, a.shape[-1]))
    return jnp.swapaxes(a.reshape(a.shape[:-1] + (N_CHIPS, a.shape[-1] // N_CHIPS)), -3, -2)


def _from_chip_blocks(name, b):
    if name == "w_o":
        return b.reshape(b.shape[:-3] + (N_CHIPS * b.shape[-2], b.shape[-1]))
    s = jnp.swapaxes(b, -3, -2)
    return s.reshape(s.shape[:-2] + (N_CHIPS * s.shape[-1],))


LARGE = ("w_in",) + tuple(n for n, _ in MISC)


def gather_small(shards):
    small = chip_exchange(_pack([shards[n] for n, _, _ in SHARDED_SMALL], F32, 8), True, "gather_small_ici")
    per_chip = [_unpack(small[k], [s for _, s, _ in SHARDED_SMALL]) for k in range(N_CHIPS)]
    return {name: jnp.concatenate([per_chip[k][idx] for k in range(N_CHIPS)], axis=axis) for idx, (name, _, axis) in enumerate(SHARDED_SMALL)}


class LocalWeights:
    def __init__(self, full):
        self.full = full
        self.grads = [None] * DEPTH

    def layer(self, i):
        return {n: self.full[n][i] for n in LARGE}

    def gather_with_attention(self, i):
        return None

    def gathered(self, dsts):
        pass

    def exchange_with_attention(self):
        return None

    def exchanged(self, dsts):
        pass

    def put_grads(self, i, grads):
        self.grads[i] = grads

    def reduced(self):
        return {n: jnp.stack([g[n] for g in self.grads]) for n in LARGE}


class MeshWeights:
    def __init__(self, shards, c, chip):
        self.c, self.chip = c, chip
        self.srcs = [shards["w_in"].astype(BF16), _misc_block({n: shards[n] for n, _ in MISC}).astype(BF16)]
        dsts = [lax.empty((DEPTH, N_CHIPS) + s.shape[1:], BF16) for s in self.srcs]
        self.dsts = gather_layer(self.srcs, dsts, 0, "gather_layer")
        self.landed = [lax.empty((DEPTH, N_CHIPS, s.shape[1] // 2, s.shape[2]), BF16) for s in self.srcs]
        self.pending = None

    def layer(self, i):
        blocks = [d[i] for d in self.dsts]
        if i == 0:
            own = (jnp.arange(N_CHIPS) == self.chip)[:, None, None]
            blocks = [jnp.where(own, s[0][None], b) for s, b in zip(self.srcs, blocks)]
        out = {"w_in": blocks[0]}
        for name, b in _misc_unblock(blocks[1]).items():
            out[name] = _from_chip_blocks(name, b)
        return out

    def gather_with_attention(self, i):
        return (self.srcs, self.dsts, i + 1) if i + 1 < DEPTH else None

    def gathered(self, dsts):
        if dsts:
            self.dsts = dsts

    def exchange_with_attention(self):
        return None if self.pending is None else (self.pending[0], self.landed, self.pending[1])

    def exchanged(self, dsts):
        if dsts:
            self.landed, self.pending = dsts, None

    def put_grads(self, i, grads):
        ps = [grads["w_in"].astype(BF16), _misc_block({n: _to_chip_blocks(n, grads[n]) for n, _ in MISC}).astype(BF16)]
        rs = swap_row_halves(ps, "reduce_swap")
        self.pending = ([add_row_half(p, r, self.c, "reduce_pair_%d" % a) for a, (p, r) in enumerate(zip(ps, rs))], i)

    def reduced(self):
        landed = exchange_layer(self.pending[0], self.landed, self.pending[1], "reduce_exchange")
        gs = [sum_row_halves(l, self.c, "reduce_sum_%d" % a) for a, l in enumerate(landed)]
        g_in, g_misc = share_row_halves(gs, "reduce_share")
        out = {"w_in": g_in}
        out.update(_misc_unblock(g_misc))
        return out


def reduce_small(grads, chip):
    names = [n for n, _ in REPLICATED] + [n for n, _, _ in SHARDED_SMALL]
    buf = _pack([grads[n] for n in names], F32, 8)
    chip_sum = add_pair(buf, sibling_swap(buf, "reduce_small_d2d"), F32, "reduce_small_pair")
    total = sum_slots(chip_exchange(chip_sum, True, "reduce_small_ici"), "reduce_small_sum")
    out = dict(zip(names, _unpack(total, [grads[n].shape for n in names])))
    for name, shape, axis in SHARDED_SMALL:
        out[name] = lax.dynamic_slice_in_dim(out[name], chip * shape[axis], shape[axis], axis)
    return out


def _prepare_small(w):
    row = lambda a: a[:, None, :]
    conf_vec = jnp.concatenate([row(w["conf_dw_b"]), row(w["conf_ln_g"]), row(w["conf_ln_b"]), jnp.zeros((DEPTH, 5, 256), F32)], axis=1)
    return dict(
        gpre=row(w["pre_norm_g"]), bias=row(w["gate_bias"]), pwbd=_block_diag(w["pool_w"]).astype(BF16), pscale=row(w["pool_scale"]),
        gq=row(w["q_norm_g"]), gkv=row(w["kv_norm_g"]), conf_w=jnp.pad(w["conf_dw_w"].astype(F32), ((0, 0), (0, 32 - CONF_K), (0, 0))),
        conf_vec=conf_vec, sc_w=jnp.pad(w["sc_dw_w"].astype(F32), ((0, 0), (0, 8 - SC_K), (0, 0))), gpost=row(w["post_norm_g"]))


def _prepare_layer(large):
    w_br, w_gl = _input_weights(large["w_in"])
    one = lambda a: a.astype(BF16)[None]
    return dict(w_br=one(w_br), w_gl=one(w_gl), wuq=one(_uq_layout(large["w_uq"])), wukv=one(_ukv_layout(large["w_ukv"])),
                woa=one(large["w_out_pool"]), wob=one(large["w_out_mla"]), woc=one(large["w_out_conf"]), wod=one(large["w_out_sc"]),
                wo=one(large["w_o"]))


def local_step(x, target, w, large):
    seq = x.shape[0]
    length = N_META + seq
    rows = -(-length // ROW_TILE) * ROW_TILE
    bt = _big_tile(rows)
    hres = _pad_rows(jnp.concatenate([w["meta_tokens"].astype(F32), x], axis=0), rows)
    tgt = jnp.pad(target, ((N_META, rows - length), (0, 0)))
    rope = _rope_tables(rows)
    sw = _prepare_small(w)

    saved = []
    for i in range(DEPTH):
        lw = _prepare_layer(large.layer(i))
        z_br, hb = prenorm_project(hres, sw["gpre"], lw["w_br"], i)
        z_gl = matmul(hb, lw["w_gl"], "nn", BF16, bt, 1024, D_MODEL, "project_gates", b_layer=0)
        ua, uc, ud, q, k, v = branches_fwd(z_br, rope, sw["pwbd"], sw["pscale"], sw["gq"], lw["wuq"], sw["gkv"], lw["wukv"],
                                           sw["conf_w"], sw["conf_vec"], sw["sc_w"], i)
        o_att, lse, dsts = attention_fwd(q, k, v, large.gather_with_attention(i))
        large.gathered(dsts)
        ub, mb, o, hnew = merge_fwd(ua, o_att, uc, ud, z_br, z_gl, sw["bias"], lw["woa"], lw["wob"], lw["woc"], lw["wod"], lw["wo"],
                                    sw["gpost"], hres, i)
        saved.append(dict(lw=lw, hres=hres, hb=hb, z_br=z_br, z_gl=z_gl, ua=ua, ub=ub, uc=uc, ud=ud, q=q, k=k, v=v, o_att=o_att,
                          lse=lse, mb=mb, o=o))
        hres = hnew

    dh, total = loss_head(hres, tgt, seq)

    g = {n: [None] * DEPTH for n in ("gpre", "bias", "pwbd", "pscale", "gq", "gkv", "conf_w", "conf_vec", "sc_w", "gpost")}
    for i in reversed(range(DEPTH)):
        s = saved[i]
        lw = s["lw"]
        dm, dwo, g["gpost"][i] = postnorm_bwd(dh, s["o"], s["mb"], lw["wo"], sw["gpost"], i)
        dua, dub, duc, dud, dz_gl, dwa, dwb, dwc, dwd, g["bias"][i] = merge_bwd(
            dm, s["ua"], s["ub"], s["uc"], s["ud"], s["z_gl"], sw["bias"], lw["woa"], lw["wob"], lw["woc"], lw["wod"], i)
        dz_br = lax.empty((rows, ZB), BF16)
        dz_br, g["pwbd"][i], g["pscale"][i] = pool_bwd(s["z_br"], dua, sw["pwbd"], sw["pscale"], dz_br, i)
        dz_br, g["sc_w"][i] = shortconv_bwd(s["z_br"], dud, sw["sc_w"], dz_br, i)
        dc, dz_br, g["conf_vec"][i] = conformer_bwd_tail(s["z_br"], duc, sw["conf_w"], sw["conf_vec"], dz_br, i)
        dz_br, g["conf_w"][i] = conformer_bwd_conv(s["z_br"], dc, sw["conf_w"], dz_br, i)
        do, dz_br, delta = attention_bwd_prep(dub, s["o_att"], s["z_br"], dz_br)
        dq, dk, dv, dsts = attention_bwd(s["q"], s["k"], s["v"], do, s["lse"], delta, large.exchange_with_attention())
        large.exchanged(dsts)
        dz_br, dwuq, dwukv, g["gq"][i], g["gkv"][i] = mla_prep_bwd(dq, dk, dv, s["z_br"], rope, sw["gq"], lw["wuq"], sw["gkv"],
                                                                 lw["wukv"], dz_br, i)
        dw_br = matmul(s["hb"], dz_br, "tn", F32, D_MODEL, ZB // 2, bt, "grad_w_branch")
        dw_gl = matmul(s["hb"], dz_gl, "tn", F32, D_MODEL, 1024, bt, "grad_w_gates")
        dh_gl = matmul(dz_gl, lw["w_gl"], "nt", F32, bt, D_MODEL, 1024, "grad_h_gates", b_layer=0)
        dh, g["gpre"][i] = prenorm_bwd(dz_br, lw["w_br"], dh_gl, s["hres"], sw["gpre"], dh, i)
        large.put_grads(i, dict(w_in=_input_weights_inverse(dw_br, dw_gl), w_out_pool=dwa, w_uq=_uq_layout_inverse(dwuq),
                                w_ukv=_ukv_layout_inverse(dwukv), w_out_mla=dwb, w_out_conf=dwc, w_out_sc=dwd, w_o=dwo))

    g = {n: jnp.stack(parts) for n, parts in g.items()}
    grads = dict(
        meta_tokens=dh[:N_META], pre_norm_g=g["gpre"][:, 0], gate_bias=g["bias"][:, 0], pool_w=_block_diag_inverse(g["pwbd"]),
        pool_scale=g["pscale"][:, 0], q_norm_g=g["gq"][:, 0], kv_norm_g=g["gkv"][:, 0], conf_dw_w=g["conf_w"][:, :CONF_K],
        conf_dw_b=g["conf_vec"][:, 2], conf_ln_g=g["conf_vec"][:, 0], conf_ln_b=g["conf_vec"][:, 1], sc_dw_w=g["sc_w"][:, :SC_K],
        post_norm_g=g["gpost"][:, 0])
    return total[0, 0], dh[N_META:length], grads


def kernel(x, meta_tokens, pre_norm_g, w_in, gate_bias, pool_w, pool_scale, w_out_pool, q_norm_g, w_uq, kv_norm_g, w_ukv, w_out_mla, conf_dw_w, conf_dw_b, conf_ln_g, conf_ln_b, w_out_conf, sc_dw_w, w_out_sc, w_o, post_norm_g, loss_target, m_meta_tokens, m_pre_norm_g, m_w_in, m_gate_bias, m_pool_w, m_pool_scale, m_w_out_pool, m_q_norm_g, m_w_uq, m_kv_norm_g, m_w_ukv, m_w_out_mla, m_conf_dw_w, m_conf_dw_b, m_conf_ln_g, m_conf_ln_b, m_w_out_conf, m_sc_dw_w, m_w_out_sc, m_w_o, m_post_norm_g, v_meta_tokens, v_pre_norm_g, v_w_in, v_gate_bias, v_pool_w, v_pool_scale, v_w_out_pool, v_q_norm_g, v_w_uq, v_kv_norm_g, v_w_ukv, v_w_out_mla, v_conf_dw_w, v_conf_dw_b, v_conf_ln_g, v_conf_ln_b, v_w_out_conf, v_sc_dw_w, v_w_out_sc, v_w_o, v_post_norm_g):
    args = locals()
    weights = {n: args[n] for n in WEIGHT_ORDER}
    c = lax.axis_index("c")
    chip = 2 * lax.axis_index("x") + lax.axis_index("y")

    small = {n: weights[n] for n, _ in REPLICATED}
    small.update(gather_small(weights))
    large = MeshWeights(weights, c, chip)
    total, dx, grads = local_step(x[0], loss_target[0], small, large)
    loss = lax.psum(total * (0.5 / D_MODEL), ("x", "y", "c"))

    reduced = large.reduced()
    reduced.update(reduce_small(grads, chip))

    flip = lambda a: jnp.swapaxes(a, 1, 2)
    deltas, new_m, new_v = [], [], []
    for n in WEIGHT_ORDER:
        operands = (weights[n], reduced[n], args["m_" + n], args["v_" + n])
        if n == "w_in":
            operands = (flip(operands[0]), lax.optimization_barrier(flip(operands[1])), flip(operands[2]), flip(operands[3]))
            reduced[n] = flip(operands[1])
        d, nm, nv = adamw(*operands)
        if n == "w_in":
            d, nm, nv = flip(d), flip(nm), flip(nv)
        deltas.append(d)
        new_m.append(nm)
        new_v.append(nv)
    return (loss, dx[None], *[reduced[n] for n in WEIGHT_ORDER], *deltas, *new_m, *new_v)
```

```python
import functools
import math

import jax
import jax.numpy as jnp
from jax import lax
from jax.experimental import pallas as pl
from jax.experimental.pallas import tpu as pltpu

F32 = jnp.float32
BF16 = jnp.bfloat16

D_MODEL = 1024
DEPTH = 4
N_META = 16
EPS = 1e-6
HEADS = 8
QK_NOPE = 64
QK_ROPE = 32
V_DIM = 64
HEAD_PAD = 128
ROPE_THETA = 10000.0
Q_SCALE = (QK_NOPE + QK_ROPE) ** -0.5
CONF_K = 31
SC_K = 3
IN_W = 7328
N_CHIPS = 4

ZB = 3328
ZG = 4096
BG, C2, XV, SG, PV, PG, CQ, CKV, KR, MG, CA, CGT, CG = (0, 256, 512, 768, 1024, 1280, 1536, 1792, 1920, 2048, 2560, 2816, 3072)

KEY_GROUP = 4
ROW_TILE = 384
HALO = 32
LANES = 128
VMEM_LIMIT = 56 * 1024 * 1024

ADAM_LR = 0.001
ADAM_B1 = 0.9
ADAM_B2 = 0.999
ADAM_EPS = 1e-08
ADAM_WD = 0.01
ADAM_STEP = 10

MESH = pl.DeviceIdType.MESH
ANY = pl.BlockSpec(memory_space=pl.ANY)

MISC = ("w_out_mla", "w_out_pool", "w_out_conf", "w_out_sc", "w_uq", "w_ukv", "w_o")
M_MLA, M_POOL, M_CONF, M_SC, M_UQ, M_UKVK, M_UKVV, M_WO, MISC_ROWS = 0, 512, 768, 1024, 1536, 1792, 1920, 2048, 3072
SHARDED_SMALL = (
    ("meta_tokens", (N_META, 256), 1),
    ("conf_dw_w", (DEPTH, CONF_K, 64), 2),
    ("sc_dw_w", (DEPTH, SC_K, 64), 2),
)
REPLICATED = (
    ("pre_norm_g", (DEPTH, D_MODEL)),
    ("gate_bias", (DEPTH, 4 * D_MODEL)),
    ("pool_w", (DEPTH, 4, 64, 64)),
    ("pool_scale", (DEPTH, 256)),
    ("q_norm_g", (DEPTH, 256)),
    ("kv_norm_g", (DEPTH, 128)),
    ("conf_dw_b", (DEPTH, 256)),
    ("conf_ln_g", (DEPTH, 256)),
    ("conf_ln_b", (DEPTH, 256)),
    ("post_norm_g", (DEPTH, D_MODEL)),
)
WEIGHT_ORDER = ("meta_tokens", "pre_norm_g", "w_in", "gate_bias", "pool_w", "pool_scale", "w_out_pool", "q_norm_g", "w_uq",
                "kv_norm_g", "w_ukv", "w_out_mla", "conf_dw_w", "conf_dw_b", "conf_ln_g", "conf_ln_b", "w_out_conf", "sc_dw_w",
                "w_out_sc", "w_o", "post_norm_g")


def _dot(a, b):
    return lax.dot_general(a, b, (((1,), (0,)), ((), ())), preferred_element_type=F32)


def _dot_nt(a, b):
    return lax.dot_general(a, b, (((1,), (1,)), ((), ())), preferred_element_type=F32)


def _dot_tn(a, b):
    return lax.dot_general(a, b, (((0,), (0,)), ((), ())), preferred_element_type=F32)


def _sigmoid(x):
    return jax.nn.sigmoid(x)


def _silu(x):
    return x * _sigmoid(x)


def _silu_grad(x):
    s = _sigmoid(x)
    return s * (1.0 + x * (1.0 - s))


def _rms(x, g):
    return x * lax.rsqrt(jnp.mean(x * x, axis=-1, keepdims=True) + EPS) * g


def _sh(x, d):
    return x if d == 0 else pltpu.roll(x, d, 0)


def _ash(x, d):
    return x if d == 0 else pltpu.roll(x, x.shape[0] - d, 0)


def _lanes8(t):
    return jnp.concatenate([t] * HEADS, axis=1)


def _pool_window_sums(v, shift):
    a2 = v + shift(v, 1)
    a4 = a2 + shift(a2, 2)
    a8 = a4 + shift(a4, 4)
    a16 = a8 + shift(a8, 8)
    lane = lax.broadcasted_iota(jnp.int32, v.shape, 1)
    return jnp.where(lane < 64, a2, jnp.where(lane < 128, a4, jnp.where(lane < 192, a8, a16)))


def _pool_counts(first_row, rows):
    pos = first_row + lax.broadcasted_iota(jnp.int32, (rows, 256), 0)
    lane = lax.broadcasted_iota(jnp.int32, (rows, 256), 1)
    width = jnp.where(lane < 64, 2, jnp.where(lane < 128, 4, jnp.where(lane < 192, 8, 16)))
    return jnp.maximum(jnp.minimum(pos + 1, width), 1).astype(F32)


def _params(sem=None):
    return pltpu.CompilerParams(dimension_semantics=sem, vmem_limit_bytes=VMEM_LIMIT)


def _tile_specs(t, n_halo_blocks, li=0):
    per = t // HALO

    def layer(shape, idx=li):
        return pl.BlockSpec((None,) + tuple(shape), lambda i: (idx,) + (0,) * len(shape))

    def cur(c, cb=0):
        return pl.BlockSpec((t, c), lambda i: (i, cb))

    def prev(c, cb=0):
        return pl.BlockSpec((HALO, c), lambda i: (jnp.maximum(i * per - 1, 0), cb))

    def nxt(c, cb=0):
        return pl.BlockSpec((HALO, c), lambda i: (jnp.minimum((i + 1) * per, n_halo_blocks - 1), cb))

    def full(shape):
        return pl.BlockSpec(shape, lambda i: (0,) * len(shape))

    return cur, prev, nxt, full, layer


def _big_tile(rows):
    return rows // 3 if rows % (3 * LANES) == 0 else ROW_TILE


def matmul(a, b, mode, out_dtype, tm, tn, tk, name, b_layer=None):
    bs = b.shape if b_layer is None else b.shape[1:]
    lead = () if b_layer is None else (None,)
    pick = (lambda *ix: ix) if b_layer is None else (lambda *ix: (b_layer,) + ix)
    if mode == "nn":
        (m, k), n = a.shape, bs[1]
        a_spec = pl.BlockSpec((tm, tk), lambda i, j, kk: (i, kk))
        b_spec = pl.BlockSpec(lead + (tk, tn), lambda i, j, kk: pick(kk, j))
        dot = _dot
    elif mode == "nt":
        (m, k), n = a.shape, bs[0]
        a_spec = pl.BlockSpec((tm, tk), lambda i, j, kk: (i, kk))
        b_spec = pl.BlockSpec(lead + (tn, tk), lambda i, j, kk: pick(j, kk))
        dot = _dot_nt
    else:
        (k, m), n = a.shape, bs[1]
        a_spec = pl.BlockSpec((tk, tm), lambda i, j, kk: (kk, i))
        b_spec = pl.BlockSpec(lead + (tk, tn), lambda i, j, kk: pick(kk, j))
        dot = _dot_tn
    assert m % tm == 0 and n % tn == 0 and k % tk == 0, (a.shape, bs, tm, tn, tk)
    nk = k // tk

    def body(a_ref, b_ref, o_ref, acc_ref):
        kk = pl.program_id(2)

        @pl.when(kk == 0)
        def _():
            acc_ref[...] = jnp.zeros_like(acc_ref)

        acc_ref[...] += dot(a_ref[...], b_ref[...])

        @pl.when(kk == nk - 1)
        def _():
            o_ref[...] = acc_ref[...].astype(out_dtype)

    return pl.pallas_call(
        body, name=name, grid=(m // tm, n // tn, nk), in_specs=[a_spec, b_spec],
        out_specs=pl.BlockSpec((tm, tn), lambda i, j, kk: (i, j)), out_shape=jax.ShapeDtypeStruct((m, n), out_dtype),
        scratch_shapes=[pltpu.VMEM((tm, tn), F32)], compiler_params=_params(("parallel", "parallel", "arbitrary")),
    )(a, b)


def prenorm_project(hres, g, w, li):
    rows, d = hres.shape
    n = w.shape[2]
    tm, tn = _big_tile(rows), n // 2

    def body(x_ref, g_ref, w_ref, z_ref, hb_ref):
        @pl.when(pl.program_id(1) == 0)
        def _():
            hb_ref[...] = _rms(x_ref[...], g_ref[...]).astype(BF16)

        z_ref[...] = _dot(hb_ref[...], w_ref[...]).astype(BF16)

    return pl.pallas_call(
        body, name="prenorm_project", grid=(rows // tm, n // tn),
        in_specs=[pl.BlockSpec((tm, d), lambda i, j: (i, 0)), pl.BlockSpec((None, 1, d), lambda i, j: (li, 0, 0)),
                  pl.BlockSpec((None, d, tn), lambda i, j: (0, 0, j))],
        out_specs=[pl.BlockSpec((tm, tn), lambda i, j: (i, j)), pl.BlockSpec((tm, d), lambda i, j: (i, 0))],
        out_shape=[jax.ShapeDtypeStruct((rows, n), BF16), jax.ShapeDtypeStruct((rows, d), BF16)],
        compiler_params=_params(("parallel", "arbitrary")),
    )(hres, g, w)


def _rope(q, c, s1, s2, width):
    return q * c + pltpu.roll(q, width - 16, 1) * s1 + pltpu.roll(q, 16, 1) * s2


def _rope_transposed(dq, c, s1, s2, width):
    return dq * c + pltpu.roll(dq * s1, 16, 1) + pltpu.roll(dq * s2, width - 16, 1)


def _conf_conv(g1, w_ref):
    acc = jnp.zeros_like(g1)
    for k in range(CONF_K):
        acc = acc + w_ref[k:k + 1, :] * _sh(g1, CONF_K - 1 - k)
    return acc


def _conf_tail(c, cg, lg, lb):
    mu = jnp.mean(c, axis=-1, keepdims=True)
    xc = c - mu
    var = jnp.mean(xc * xc, axis=-1, keepdims=True)
    n = xc * lax.rsqrt(var + EPS) * lg + lb
    return _silu(n) * _silu(cg)


def _misc_spec(row0, rows):
    assert row0 % rows == 0
    return pl.BlockSpec((None, N_CHIPS, rows, 256), lambda i: (0, 0, row0 // rows, 0))


def _chip_columns(x, w_ref, row0, rows, lanes=256):
    return jnp.concatenate([_dot(x, w_ref[k, row0:row0 + rows, 0:lanes]) for k in range(N_CHIPS)], axis=1)


def branches_fwd(z_br, rope, pwbd, pscale, gq, gkv, misc, conf_w, conf_vec, sc_w, li):
    rows = z_br.shape[0]
    t = ROW_TILE
    cur, prev, _, _, layer = _tile_specs(t, rows // HALO, li)

    def body(zc_ref, zp_ref, rope_ref, pw_ref, ps_ref, gq_ref, gkv_ref, up_ref, cw_ref, cv_ref, sw_ref,
             ua_ref, uc_ref, ud_ref, q_ref, k_ref, v_ref):
        i = pl.program_id(0)
        zp = jnp.where(i == 0, jnp.zeros(zp_ref.shape, zp_ref.dtype), zp_ref[...])

        def ext(lo, w=256):
            return jnp.concatenate([zp[:, lo:lo + w], zc_ref[:, lo:lo + w]], axis=0).astype(F32)

        def col(lo, w=256):
            return zc_ref[:, lo:lo + w].astype(F32)

        v = ext(PV)
        p = (_pool_window_sums(v, _sh) / _pool_counts(i * t - HALO, t + HALO) - v)[HALO:]
        ya = _dot(p.astype(BF16), pw_ref[...]) * ps_ref[...]
        ua_ref[...] = (ya * _silu(col(PG))).astype(BF16)

        g1 = ext(CA) * _sigmoid(ext(CGT))
        c = _conf_conv(g1, cw_ref)[HALO:] + cv_ref[0:1, :]
        uc_ref[...] = _conf_tail(c, col(CG), cv_ref[1:2, :], cv_ref[2:3, :]).astype(BF16)

        e = ext(C2) * ext(XV)
        f = jnp.zeros_like(e)
        for k in range(SC_K):
            f = f + sw_ref[k:k + 1, :] * _sh(e, SC_K - 1 - k)
        ud_ref[...] = (col(BG) * f[HALO:] * _silu(col(SG))).astype(BF16)

        cth, s1, s2 = rope_ref[:, 0:128], rope_ref[:, 128:256], rope_ref[:, 256:384]
        qn = _rms(col(CQ), gq_ref[...]).astype(BF16)
        q = _chip_columns(qn, up_ref, 0, 256)
        w8 = HEADS * HEAD_PAD
        q_ref[...] = (_rope(q, _lanes8(cth), _lanes8(s1), _lanes8(s2), w8) * Q_SCALE).astype(BF16)
        kvn = _rms(col(CKV, 128), gkv_ref[...]).astype(BF16)
        kr = _rope(col(KR, 128), cth, s1, s2, HEAD_PAD)
        k_ref[...] = (_chip_columns(kvn, up_ref, M_UKVK - M_UQ, 128) + _lanes8(kr)).astype(BF16)
        v_ref[...] = _chip_columns(kvn, up_ref, M_UKVV - M_UQ, 128, 2 * V_DIM).astype(BF16)

    outs = [jax.ShapeDtypeStruct((rows, 256), BF16)] * 3 + [jax.ShapeDtypeStruct((rows, 1024), BF16)] * 2 + [
        jax.ShapeDtypeStruct((rows, 512), BF16)]
    return pl.pallas_call(
        body, name="branches_fwd", grid=(rows // t,),
        in_specs=[cur(ZB), prev(ZB), cur(384), layer((256, 256)), layer((1, 256)), layer((1, 256)), layer((1, 128)),
                  _misc_spec(M_UQ, M_WO - M_UQ), layer((32, 256)), layer((8, 256)), layer((8, 256))],
        out_specs=[cur(256), cur(256), cur(256), cur(1024), cur(1024), cur(512)], out_shape=outs,
        compiler_params=_params(("parallel",)),
    )(z_br, z_br, rope, pwbd, pscale, gq, gkv, misc, conf_w, conf_vec, sc_w)


def _head_lane_mask(h):
    lane = lax.broadcasted_iota(jnp.int32, (1, 2 * V_DIM), 1)
    return (lane >= V_DIM * h) & (lane < V_DIM * (h + 1))


def attention_fwd(q, k, v, gather=None):
    rows = q.shape[0]
    tq = ROW_TILE
    nq = rows // tq
    n = 0 if gather is None else len(gather[0])

    def body(*refs):
        if n:
            start, finish = _gather_ops(refs[3:3 + n], refs[5 + 2 * n:5 + 3 * n], refs[5 + 3 * n:], gather[2], True)
            pl.when((pl.program_id(0) == 0) & (pl.program_id(1) == 0))(start)
        compute(*refs[:3], *refs[3 + 2 * n:5 + 2 * n])
        if n:
            pl.when((pl.program_id(0) == HEADS // 2 - 1) & (pl.program_id(1) == nq - 1))(finish)

    def compute(q_ref, k_ref, v_ref, o_ref, lse_ref):
        i = pl.program_id(1)

        def head_step(h, tile, n_tiles, carry, masked):
            m, l, acc = carry
            width = n_tiles * tq
            r0 = pl.multiple_of(tile * tq, tq)
            kh = k_ref[pl.ds(r0, width), HEAD_PAD * h:HEAD_PAD * (h + 1)]
            vh = jnp.where(_head_lane_mask(h), v_ref[pl.ds(r0, width), :], jnp.zeros((), BF16))
            s = _dot_nt(q_ref[:, HEAD_PAD * h:HEAD_PAD * (h + 1)], kh)
            if masked:
                row = lax.broadcasted_iota(jnp.int32, (tq, width), 0)
                colm = lax.broadcasted_iota(jnp.int32, (tq, width), 1)
                s = jnp.where(colm <= row + (width - tq), s, -1e30)
            m2 = jnp.maximum(m, jnp.max(s, axis=-1, keepdims=True))
            alpha = jnp.exp(m - m2)
            pr = jnp.exp(s - m2)
            return m2, alpha * l + jnp.sum(pr, axis=-1, keepdims=True), alpha * acc + _dot(pr.astype(BF16), vh)

        def step(tile, n_tiles, carry, masked):
            return tuple(head_step(h, tile, n_tiles, carry[h], masked) for h in range(2))

        init = (jnp.full((tq, 1), -1e30, F32), jnp.zeros((tq, 1), F32), jnp.zeros((tq, 2 * V_DIM), F32))
        group = min(KEY_GROUP, nq)
        carry = lax.fori_loop(0, i // group, lambda t, cr: step(group * t, group, cr, False), (init, init))
        carry = lax.switch(i % group, [functools.partial(lambda cr, r: step(i - r, r + 1, cr, True), r=r) for r in range(group)], carry)
        out = jnp.zeros((tq, 2 * V_DIM), F32)
        for h, (m, l, acc) in enumerate(carry):
            out = out + acc / l
            lse_ref[h] = jnp.broadcast_to(m + jnp.log(l), (tq, LANES))
        o_ref[...] = out.astype(BF16)

    srcs, dsts = ([], []) if gather is None else (list(gather[0]), list(gather[1]))
    outs = pl.pallas_call(
        body, name="attention_fwd" if gather is None else "attention_fwd_gather", grid=(HEADS // 2, nq),
        in_specs=[pl.BlockSpec((tq, 2 * HEAD_PAD), lambda p, i: (i, p)), pl.BlockSpec((rows, 2 * HEAD_PAD), lambda p, i: (0, p)),
                  pl.BlockSpec((rows, 2 * V_DIM), lambda p, i: (0, p))] + [ANY] * (2 * n),
        out_specs=[pl.BlockSpec((tq, 2 * V_DIM), lambda p, i: (i, p)), pl.BlockSpec((2, tq, LANES), lambda p, i: (p, i, 0))] + [ANY] * n,
        out_shape=[jax.ShapeDtypeStruct((rows, HEADS * V_DIM), BF16), jax.ShapeDtypeStruct((HEADS, rows, LANES), F32)] + [
            jax.ShapeDtypeStruct(d.shape, d.dtype) for d in dsts],
        input_output_aliases={3 + n + a: 2 + a for a in range(n)}, scratch_shapes=GATHER_SEMS(n) if n else [],
        compiler_params=_params(("arbitrary", "arbitrary") if n else ("parallel", "parallel")),
    )(q, k, v, *srcs, *dsts)
    return outs[0], outs[1], list(outs[2:])


OUT_PROJECTIONS = ((M_POOL, 256), (M_MLA, 512), (M_CONF, 256), (M_SC, 256))


def _chunks(x, n=N_CHIPS, width=256):
    return [x[:, width * k:width * (k + 1)] for k in range(n)]


def merge_fwd(ua, o_att, uc, ud, z_br, z_gl, bias, misc, gpost, hres, li):
    rows = hres.shape[0]
    t = ROW_TILE
    cur, _, _, _, layer = _tile_specs(t, rows // HALO, li)
    d = D_MODEL

    def body(ua_ref, ob_ref, uc_ref, ud_ref, mg_ref, gl_ref, b_ref, wout_ref, wo_ref, gp_ref, h_ref, ub_ref, mb_ref, o_ref, hn_ref):
        ub = (ob_ref[...].astype(F32) * _silu(mg_ref[...].astype(F32))).astype(BF16)
        ub_ref[...] = ub
        m = jnp.zeros((t, d), F32)
        for idx, (u, (row0, n)) in enumerate(zip((ua_ref[...], ub, uc_ref[...], ud_ref[...]), OUT_PROJECTIONS)):
            gate = _sigmoid(gl_ref[:, d * idx:d * (idx + 1)].astype(F32) + b_ref[:, d * idx:d * (idx + 1)])
            m = m + gate * _chip_columns(u, wout_ref, row0, n)
        mb = m.astype(BF16)
        mb_ref[...] = mb
        o = jnp.concatenate([sum(_dot(mk, wo_ref[k, 256 * j:256 * (j + 1), :]) for k, mk in enumerate(_chunks(mb)))
                             for j in range(N_CHIPS)], axis=1)
        o_ref[...] = o
        hn_ref[...] = h_ref[...] + _rms(o, gp_ref[...])

    return pl.pallas_call(
        body, name="merge_fwd", grid=(rows // t,),
        in_specs=[cur(256), cur(512), cur(256), cur(256), cur(512, MG // 512), cur(ZG), layer((1, ZG)), _misc_spec(0, 1280),
                  _misc_spec(M_WO, D_MODEL), layer((1, d)), cur(d)],
        out_specs=[cur(512), cur(d), cur(d), cur(d)],
        out_shape=[jax.ShapeDtypeStruct((rows, 512), BF16), jax.ShapeDtypeStruct((rows, d), BF16), jax.ShapeDtypeStruct((rows, d), F32),
                   jax.ShapeDtypeStruct((rows, d), F32)],
        compiler_params=_params(("parallel",)),
    )(ua, o_att, uc, ud, z_br, z_gl, bias, misc, misc, gpost, hres)


def loss_head(hres, target, n_tokens):
    rows, d = hres.shape
    t = ROW_TILE
    cur, _, _, full, _ = _tile_specs(t, rows // HALO)
    n_steps = rows // t

    def body(h_ref, t_ref, dh_ref, tot_ref, acc_ref):
        i = pl.program_id(0)

        @pl.when(i == 0)
        def _():
            acc_ref[...] = jnp.zeros_like(acc_ref)

        r = i * t + lax.broadcasted_iota(jnp.int32, (t, 1), 0)
        diff = jnp.where((r >= N_META) & (r < N_META + n_tokens), h_ref[...] - t_ref[...], 0.0)
        dh_ref[...] = diff * (1.0 / d)
        acc_ref[...] += jnp.sum(diff * diff, axis=0, keepdims=True)

        @pl.when(i == n_steps - 1)
        def _():
            tot_ref[...] = jnp.broadcast_to(jnp.sum(acc_ref[...], axis=1, keepdims=True), (1, LANES))

    return pl.pallas_call(
        body, name="loss_head", grid=(n_steps,), in_specs=[cur(d), cur(d)], out_specs=[cur(d), full((1, LANES))],
        out_shape=[jax.ShapeDtypeStruct((rows, d), F32), jax.ShapeDtypeStruct((1, LANES), F32)],
        scratch_shapes=[pltpu.VMEM((1, d), F32)], compiler_params=_params(("arbitrary",)),
    )(hres, target)


def _accumulate(i, ref, value):
    @pl.when(i == 0)
    def _():
        ref[...] = value

    @pl.when(i > 0)
    def _():
        ref[...] += value


def postnorm_bwd(dh, o, mb, misc, gpost, li):
    rows, d = dh.shape
    t = ROW_TILE
    cur, _, _, full, layer = _tile_specs(t, rows // HALO, li)

    def body(dh_ref, o_ref, mb_ref, wo_ref, gp_ref, dm_ref, dwo_ref, dgp_ref):
        i = pl.program_id(0)
        _, vjp = jax.vjp(_rms, o_ref[...], gp_ref[...])
        do, dg = vjp(dh_ref[...])
        do_chunks, m_chunks = _chunks(do.astype(BF16)), _chunks(mb_ref[...])
        dm_ref[...] = jnp.concatenate([sum(_dot_nt(dj, wo_ref[k, 256 * j:256 * (j + 1), :]) for j, dj in enumerate(do_chunks))
                                       for k in range(N_CHIPS)], axis=1)
        for k, mk in enumerate(m_chunks):
            _accumulate(i, dwo_ref.at[k], jnp.concatenate([_dot_tn(mk, dj) for dj in do_chunks], axis=0))
        _accumulate(i, dgp_ref, dg)

    return pl.pallas_call(
        body, name="postnorm_bwd", grid=(rows // t,), in_specs=[cur(d), cur(d), cur(d), _misc_spec(M_WO, d), layer((1, d))],
        out_specs=[cur(d), full((N_CHIPS, d, 256)), full((1, d))],
        out_shape=[jax.ShapeDtypeStruct((rows, d), F32), jax.ShapeDtypeStruct((N_CHIPS, d, 256), F32), jax.ShapeDtypeStruct((1, d), F32)],
        compiler_params=_params(("arbitrary",)),
    )(dh, o, mb, misc, gpost)


def merge_bwd(dm, ua, ub, uc, ud, z_gl, bias, misc, li):
    rows, d = dm.shape
    t = ROW_TILE
    cur, _, _, full, layer = _tile_specs(t, rows // HALO, li)
    widths = (256, 512, 256, 256)

    def body(dm_ref, ua_ref, ub_ref, uc_ref, ud_ref, gl_ref, b_ref, w_ref, dua_ref, dub_ref, duc_ref, dud_ref, dgl_ref, dw_ref, db_ref):
        i = pl.program_id(0)
        dm = dm_ref[...]
        groups = ((ua_ref, dua_ref), (ub_ref, dub_ref), (uc_ref, duc_ref), (ud_ref, dud_ref))
        for idx, ((u_ref, du_ref), (row0, n)) in enumerate(zip(groups, OUT_PROJECTIONS)):
            cols = slice(d * idx, d * (idx + 1))
            u = u_ref[...]
            gate = _sigmoid(gl_ref[:, cols].astype(F32) + b_ref[:, cols])
            dgl = dm * _chip_columns(u, w_ref, row0, n) * gate * (1.0 - gate)
            dgl_ref[:, cols] = dgl.astype(BF16)
            _accumulate(i, db_ref.at[:, cols], jnp.sum(dgl, axis=0, keepdims=True))
            dy_chunks = _chunks((dm * gate).astype(BF16))
            du_ref[...] = sum(_dot_nt(dyk, w_ref[k, row0:row0 + n, :]) for k, dyk in enumerate(dy_chunks))
            for k, dyk in enumerate(dy_chunks):
                _accumulate(i, dw_ref.at[k, row0:row0 + n, :], _dot_tn(u, dyk))

    return pl.pallas_call(
        body, name="merge_bwd", grid=(rows // t,),
        in_specs=[cur(d), cur(256), cur(512), cur(256), cur(256), cur(ZG), layer((1, ZG)), _misc_spec(0, 1280)],
        out_specs=[cur(256), cur(512), cur(256), cur(256), cur(ZG), full((N_CHIPS, 1280, 256)), full((1, ZG))],
        out_shape=[jax.ShapeDtypeStruct((rows, w), F32) for w in widths] + [jax.ShapeDtypeStruct((rows, ZG), BF16),
                                                                            jax.ShapeDtypeStruct((N_CHIPS, 1280, 256), F32),
                                                                            jax.ShapeDtypeStruct((1, ZG), F32)],
        compiler_params=_params(("arbitrary",)),
    )(dm, ua, ub, uc, ud, z_gl, bias, misc)


def pool_bwd(z_br, dua, pwbd, pscale, dz_buf, li):
    rows = z_br.shape[0]
    t = ROW_TILE
    n_steps = rows // t
    cur, prev, nxt, full, layer = _tile_specs(t, rows // HALO, li)

    def body(zc_ref, zp_ref, zn_ref, dc_ref, dn_ref, pw_ref, ps_ref, _, dz_ref, dpw_ref, dps_ref):
        i = pl.program_id(0)
        zp = jnp.where(i == 0, jnp.zeros(zp_ref.shape, zp_ref.dtype), zp_ref[...])
        zn = jnp.where(i == n_steps - 1, jnp.zeros(zn_ref.shape, zn_ref.dtype), zn_ref[...])
        dun = jnp.where(i == n_steps - 1, jnp.zeros(dn_ref.shape, dn_ref.dtype), dn_ref[...])

        def ext(lo):
            return jnp.concatenate([zp[:, lo:lo + 256], zc_ref[:, lo:lo + 256], zn[:, lo:lo + 256]], axis=0).astype(F32)

        n_ext = t + 2 * HALO
        v, pg = ext(PV), ext(PG)
        cnt = _pool_counts(i * t - HALO, n_ext)
        p = (_pool_window_sums(v, _sh) / cnt - v)[HALO:HALO + t]
        du = jnp.concatenate([jnp.zeros((HALO, 256), F32), dc_ref[...], dun], axis=0)
        dya = du * _silu(pg)
        dypb = (dya * ps_ref[...]).astype(BF16)
        dp = _dot_nt(dypb, pw_ref[...])
        dv = (_pool_window_sums(dp / cnt, _ash) - dp)[HALO:HALO + t]
        pb = p.astype(BF16)
        pw = _dot(pb, pw_ref[...])
        duc, pgc = dc_ref[...], pg[HALO:HALO + t]
        dpg = duc * pw * ps_ref[...] * _silu_grad(pgc)
        dz_ref[...] = jnp.concatenate([dv, dpg], axis=1).astype(BF16)
        _accumulate(i, dpw_ref, _dot_tn(pb, dypb[HALO:HALO + t]))
        _accumulate(i, dps_ref, jnp.sum(dya[HALO:HALO + t] * pw, axis=0, keepdims=True))

    return pl.pallas_call(
        body, name="pool_bwd", grid=(n_steps,),
        in_specs=[cur(ZB), prev(ZB), nxt(ZB), cur(256), nxt(256), layer((256, 256)), layer((1, 256)), ANY],
        out_specs=[cur(512, PV // 512), full((256, 256)), full((1, 256))],
        out_shape=[jax.ShapeDtypeStruct((rows, ZB), BF16), jax.ShapeDtypeStruct((256, 256), F32), jax.ShapeDtypeStruct((1, 256), F32)],
        input_output_aliases={7: 0}, compiler_params=_params(("arbitrary",)),
    )(z_br, z_br, z_br, dua, dua, pwbd, pscale, dz_buf)


def shortconv_bwd(z_br, dud, sc_w, dz_buf, li):
    rows = z_br.shape[0]
    t = ROW_TILE
    n_steps = rows // t
    cur, prev, nxt, full, layer = _tile_specs(t, rows // HALO, li)

    def body(zc_ref, zp_ref, zn_ref, dc_ref, dn_ref, sw_ref, _, dz_ref, dw_ref):
        i = pl.program_id(0)
        zp = jnp.where(i == 0, jnp.zeros(zp_ref.shape, zp_ref.dtype), zp_ref[...])
        zn = jnp.where(i == n_steps - 1, jnp.zeros(zn_ref.shape, zn_ref.dtype), zn_ref[...])
        dun = jnp.where(i == n_steps - 1, jnp.zeros(dn_ref.shape, dn_ref.dtype), dn_ref[...])

        def ext(lo):
            return jnp.concatenate([zp[:, lo:lo + 256], zc_ref[:, lo:lo + 256], zn[:, lo:lo + 256]], axis=0).astype(F32)

        mid = slice(HALO, HALO + t)
        bg, c2, xv, sg = ext(BG), ext(C2), ext(XV), ext(SG)
        du = jnp.concatenate([jnp.zeros((HALO, 256), F32), dc_ref[...], dun], axis=0)
        e = c2 * xv
        shifted = [_sh(e, SC_K - 1 - k) for k in range(SC_K)]
        f = sum(sw_ref[k:k + 1, :] * shifted[k] for k in range(SC_K))
        gate = _silu(sg)
        df = du * gate * bg
        de = sum(sw_ref[k:k + 1, :] * _ash(df, SC_K - 1 - k) for k in range(SC_K))
        dbg = du * gate * f
        dsg = du * bg * f * _silu_grad(sg)
        dz_ref[...] = jnp.concatenate([dbg[mid], (de * xv)[mid], (de * c2)[mid], dsg[mid]], axis=1).astype(BF16)
        dw = jnp.concatenate([jnp.sum((df * shifted[k])[mid], axis=0, keepdims=True) for k in range(SC_K)] + [
            jnp.zeros((8 - SC_K, 256), F32)], axis=0)
        _accumulate(i, dw_ref, dw)

    return pl.pallas_call(
        body, name="shortconv_bwd", grid=(n_steps,), in_specs=[cur(ZB), prev(ZB), nxt(ZB), cur(256), nxt(256), layer((8, 256)), ANY],
        out_specs=[cur(1024, BG // 1024), full((8, 256))],
        out_shape=[jax.ShapeDtypeStruct((rows, ZB), BF16), jax.ShapeDtypeStruct((8, 256), F32)],
        input_output_aliases={6: 0}, compiler_params=_params(("arbitrary",)),
    )(z_br, z_br, z_br, dud, dud, sc_w, dz_buf)


def conformer_bwd_tail(z_br, duc, conf_w, conf_vec, dz_buf, li):
    rows = z_br.shape[0]
    t = ROW_TILE
    cur, prev, _, full, layer = _tile_specs(t, rows // HALO, li)

    def body(zc_ref, zp_ref, du_ref, cw_ref, cv_ref, _, dc_ref, dcg_ref, dv_ref):
        i = pl.program_id(0)
        zp = jnp.where(i == 0, jnp.zeros(zp_ref.shape, zp_ref.dtype), zp_ref[...])

        def ext(lo):
            return jnp.concatenate([zp[:, lo:lo + 256], zc_ref[:, lo:lo + 256]], axis=0).astype(F32)

        g1 = ext(CA) * _sigmoid(ext(CGT))
        c = _conf_conv(g1, cw_ref)[HALO:] + cv_ref[0:1, :]
        _, vjp = jax.vjp(_conf_tail, c, zc_ref[:, CG:CG + 256].astype(F32), cv_ref[1:2, :], cv_ref[2:3, :])
        dc, dcg, dlg, dlb = vjp(du_ref[...])
        dc_ref[...] = dc
        dcg_ref[...] = dcg.astype(BF16)
        dvec = jnp.concatenate([dlg, dlb, jnp.sum(dc, axis=0, keepdims=True), jnp.zeros((5, 256), F32)], axis=0)
        _accumulate(i, dv_ref, dvec)

    return pl.pallas_call(
        body, name="conformer_bwd_tail", grid=(rows // t,), in_specs=[cur(ZB), prev(ZB), cur(256), layer((32, 256)), layer((8, 256)), ANY],
        out_specs=[cur(256), cur(256, CG // 256), full((8, 256))],
        out_shape=[jax.ShapeDtypeStruct((rows, 256), F32), jax.ShapeDtypeStruct((rows, ZB), BF16), jax.ShapeDtypeStruct((8, 256), F32)],
        input_output_aliases={5: 1}, compiler_params=_params(("arbitrary",)),
    )(z_br, z_br, duc, conf_w, conf_vec, dz_buf)


def conformer_bwd_conv(z_br, dc, conf_w, dz_buf, li):
    rows = z_br.shape[0]
    t = ROW_TILE
    n_steps = rows // t
    cur, prev, nxt, full, layer = _tile_specs(t, rows // HALO, li)

    def body(zc_ref, zp_ref, dc_ref, dn_ref, cw_ref, _, dz_ref, dw_ref):
        i = pl.program_id(0)
        zp = jnp.where(i == 0, jnp.zeros(zp_ref.shape, zp_ref.dtype), zp_ref[...])
        dcn = jnp.where(i == n_steps - 1, jnp.zeros(dn_ref.shape, dn_ref.dtype), dn_ref[...])

        def ext(lo):
            return jnp.concatenate([zp[:, lo:lo + 256], zc_ref[:, lo:lo + 256]], axis=0).astype(F32)

        a, gt = ext(CA), ext(CGT)
        sg = _sigmoid(gt)
        g1 = a * sg
        dc = dc_ref[...]
        dce = jnp.concatenate([dc, dcn], axis=0)
        dg1 = jnp.zeros_like(dce)
        dws = []
        for k in range(CONF_K):
            dg1 = dg1 + cw_ref[k:k + 1, :] * _ash(dce, CONF_K - 1 - k)
            dws.append(jnp.sum(dc * _sh(g1, CONF_K - 1 - k)[HALO:], axis=0, keepdims=True))
        dg1 = dg1[:t]
        ac, sc = a[HALO:], sg[HALO:]
        dz_ref[...] = jnp.concatenate([dg1 * sc, dg1 * ac * sc * (1.0 - sc)], axis=1).astype(BF16)
        _accumulate(i, dw_ref, jnp.concatenate(dws + [jnp.zeros((32 - CONF_K, 256), F32)], axis=0))

    return pl.pallas_call(
        body, name="conformer_bwd_conv", grid=(n_steps,), in_specs=[cur(ZB), prev(ZB), cur(256), nxt(256), layer((32, 256)), ANY],
        out_specs=[cur(512, CA // 512), full((32, 256))],
        out_shape=[jax.ShapeDtypeStruct((rows, ZB), BF16), jax.ShapeDtypeStruct((32, 256), F32)],
        input_output_aliases={5: 0}, compiler_params=_params(("arbitrary",)),
    )(z_br, z_br, dc, dc, conf_w, dz_buf)


def attention_bwd_prep(dub, o_att, z_br, dz_buf):
    rows = dub.shape[0]
    t = ROW_TILE
    cur, _, _, _, _ = _tile_specs(t, rows // HALO)

    def body(du_ref, o_ref, mg_ref, _, do_ref, dmg_ref, delta_ref):
        du, o, mg = du_ref[...], o_ref[...].astype(F32), mg_ref[...].astype(F32)
        do = du * _silu(mg)
        do_ref[...] = do.astype(BF16)
        dmg_ref[...] = (du * o * _silu_grad(mg)).astype(BF16)
        prod = do * o
        lane = lax.broadcasted_iota(jnp.int32, (1, HEADS * V_DIM), 1)
        for h in range(HEADS):
            part = jnp.where((lane >= V_DIM * h) & (lane < V_DIM * (h + 1)), prod, 0.0)
            delta_ref[h] = jnp.broadcast_to(jnp.sum(part, axis=-1, keepdims=True), (t, LANES))

    return pl.pallas_call(
        body, name="attention_bwd_prep", grid=(rows // t,), in_specs=[cur(512), cur(512), cur(512, MG // 512), ANY],
        out_specs=[cur(512), cur(512, MG // 512), pl.BlockSpec((HEADS, t, LANES), lambda i: (0, i, 0))],
        out_shape=[jax.ShapeDtypeStruct((rows, 512), BF16), jax.ShapeDtypeStruct((rows, ZB), BF16),
                   jax.ShapeDtypeStruct((HEADS, rows, LANES), F32)],
        input_output_aliases={3: 1}, compiler_params=_params(("parallel",)),
    )(dub, o_att, z_br, dz_buf)


def attention_bwd(q, k, v, do, lse, delta, exchange=None):
    rows = q.shape[0]
    tq = ROW_TILE
    nq = rows // tq
    n = 0 if exchange is None else len(exchange[0])

    def body(*refs):
        if n:
            start, finish = _exchange_ops(refs[6:6 + n], refs[9 + 2 * n:9 + 3 * n], refs[9 + 3 * n:], exchange[2])
            pl.when((pl.program_id(0) == 0) & (pl.program_id(1) == 0))(start)
        compute(*refs[:6], *refs[6 + 2 * n:9 + 2 * n])
        if n:
            pl.when((pl.program_id(0) == HEADS // 2 - 1) & (pl.program_id(1) == nq - 1))(finish)

    def compute(q_ref, k_ref, v_ref, do_ref, lse_ref, dl_ref, dq_ref, dk_ref, dv_ref):
        j = pl.program_id(1)

        @pl.when(j == 0)
        def _():
            dq_ref[...] = jnp.zeros_like(dq_ref)

        def head_step(h, tile, n_tiles, dk, dv, diagonal):
            lanes = slice(HEAD_PAD * h, HEAD_PAD * (h + 1))
            hm = _head_lane_mask(h)
            kh = k_ref[:, lanes]
            vh = jnp.where(hm, v_ref[...], jnp.zeros((), BF16))
            r0, width = pl.multiple_of(tile * tq, tq), n_tiles * tq
            qi = q_ref[pl.ds(r0, width), lanes]
            doi = jnp.where(hm, do_ref[pl.ds(r0, width), :], jnp.zeros((), BF16))
            s = _dot_nt(qi, kh)
            if diagonal:
                s = jnp.where(lax.broadcasted_iota(jnp.int32, (tq, tq), 1) <= lax.broadcasted_iota(jnp.int32, (tq, tq), 0), s, -1e30)
            pr = jnp.exp(s - lse_ref[h, pl.ds(r0, width), :][:, 0:1])
            dv = dv + _dot_tn(pr.astype(BF16), doi)
            dp = _dot_nt(doi, vh)
            ds = (pr * (dp - dl_ref[h, pl.ds(r0, width), :][:, 0:1])).astype(BF16)
            dq_ref[pl.ds(r0, width), lanes] += _dot(ds, kh)
            return dk + _dot_tn(ds, qi), dv

        def step(tile, n_tiles, carry, diagonal):
            dk0, dk1, dv = carry
            dk0, dv = head_step(0, tile, n_tiles, dk0, dv, diagonal)
            dk1, dv = head_step(1, tile, n_tiles, dk1, dv, diagonal)
            return dk0, dk1, dv

        zero = jnp.zeros((tq, HEAD_PAD), F32)
        carry = step(j, 1, (zero, zero, jnp.zeros((tq, 2 * V_DIM), F32)), True)
        odd = (nq - 1 - j) % 2
        carry = lax.cond(odd == 1, lambda cr: step(j + 1, 1, cr, False), lambda cr: cr, carry)
        dk0, dk1, dv = lax.fori_loop(0, (nq - 1 - j) // 2, lambda t, cr: step(j + 1 + odd + 2 * t, 2, cr, False), carry)
        dk_ref[:, 0:HEAD_PAD] = dk0
        dk_ref[:, HEAD_PAD:2 * HEAD_PAD] = dk1
        dv_ref[...] = dv

    srcs, dsts = ([], []) if exchange is None else (list(exchange[0]), list(exchange[1]))
    outs = pl.pallas_call(
        body, name="attention_bwd" if exchange is None else "attention_bwd_exchange", grid=(HEADS // 2, nq),
        in_specs=[pl.BlockSpec((rows, 2 * HEAD_PAD), lambda p, j: (0, p)), pl.BlockSpec((tq, 2 * HEAD_PAD), lambda p, j: (j, p)),
                  pl.BlockSpec((tq, 2 * V_DIM), lambda p, j: (j, p)), pl.BlockSpec((rows, 2 * V_DIM), lambda p, j: (0, p)),
                  pl.BlockSpec((2, rows, LANES), lambda p, j: (p, 0, 0)), pl.BlockSpec((2, rows, LANES), lambda p, j: (p, 0, 0))] + [
                      ANY] * (2 * n),
        out_specs=[pl.BlockSpec((rows, 2 * HEAD_PAD), lambda p, j: (0, p)), pl.BlockSpec((tq, 2 * HEAD_PAD), lambda p, j: (j, p)),
                   pl.BlockSpec((tq, 2 * V_DIM), lambda p, j: (j, p))] + [ANY] * n,
        out_shape=[jax.ShapeDtypeStruct((rows, HEADS * HEAD_PAD), F32), jax.ShapeDtypeStruct((rows, HEADS * HEAD_PAD), F32),
                   jax.ShapeDtypeStruct((rows, HEADS * V_DIM), F32)] + [jax.ShapeDtypeStruct(d.shape, d.dtype) for d in dsts],
        input_output_aliases={6 + n + a: 3 + a for a in range(n)}, scratch_shapes=EXCHANGE_SEMS(n) if n else [],
        compiler_params=_params(("arbitrary", "arbitrary") if n else ("parallel", "arbitrary")),
    )(q, k, v, do, lse, delta, *srcs, *dsts)
    return outs[0], outs[1], outs[2], list(outs[3:])


def mla_prep_bwd(dq, dk, dv, z_br, rope, gq, gkv, misc, dz_buf, li):
    rows = dq.shape[0]
    t = ROW_TILE
    cur, _, _, full, layer = _tile_specs(t, rows // HALO, li)
    w8 = HEADS * HEAD_PAD
    uq, keys, values = slice(0, 256), slice(M_UKVK - M_UQ, M_UKVV - M_UQ), slice(M_UKVV - M_UQ, M_WO - M_UQ)

    def body(dq_ref, dk_ref, dv_ref, z_ref, rope_ref, gq_ref, gkv_ref, up_ref, _, dz_ref, dup_ref, dgq_ref, dgkv_ref):
        i = pl.program_id(0)
        cth, s1, s2 = rope_ref[:, 0:128], rope_ref[:, 128:256], rope_ref[:, 256:384]
        dq_chunks = _chunks(_rope_transposed(dq_ref[...] * Q_SCALE, _lanes8(cth), _lanes8(s1), _lanes8(s2), w8).astype(BF16))
        cq = z_ref[:, 0:256].astype(F32)
        qn, vjp_q = jax.vjp(_rms, cq, gq_ref[...])
        dcq, dgq = vjp_q(sum(_dot_nt(dqk, up_ref[k, uq, :]) for k, dqk in enumerate(dq_chunks)))
        _accumulate(i, dgq_ref, dgq)

        dk = dk_ref[...]
        dkr = sum(dk[:, HEAD_PAD * h:HEAD_PAD * (h + 1)] for h in range(HEADS))
        dkr = _rope_transposed(dkr, cth, s1, s2, HEAD_PAD)
        lane = lax.broadcasted_iota(jnp.int32, (1, HEAD_PAD), 1)
        dkr = jnp.where((lane >= QK_NOPE) & (lane < QK_NOPE + QK_ROPE), dkr, 0.0)
        dk_chunks, dv_chunks = _chunks(dk.astype(BF16)), _chunks(dv_ref[...].astype(BF16), width=2 * V_DIM)
        ckv = z_ref[:, 256:384].astype(F32)
        kvn, vjp_kv = jax.vjp(_rms, ckv, gkv_ref[...])
        dckv, dgkv = vjp_kv(sum(_dot_nt(dk_chunks[k], up_ref[k, keys, :]) + _dot_nt(dv_chunks[k], up_ref[k, values, 0:2 * V_DIM])
                                for k in range(N_CHIPS)))
        _accumulate(i, dgkv_ref, dgkv)
        dz_ref[...] = jnp.concatenate([dcq, dckv, dkr], axis=1).astype(BF16)
        qnb, kvnb = qn.astype(BF16), kvn.astype(BF16)
        for k in range(N_CHIPS):
            d_values = jnp.concatenate([_dot_tn(kvnb, dv_chunks[k]), jnp.zeros((128, 256 - 2 * V_DIM), F32)], axis=1)
            _accumulate(i, dup_ref.at[k], jnp.concatenate([_dot_tn(qnb, dq_chunks[k]), _dot_tn(kvnb, dk_chunks[k]), d_values], axis=0))

    return pl.pallas_call(
        body, name="mla_prep_bwd", grid=(rows // t,),
        in_specs=[cur(w8), cur(w8), cur(512), cur(512, CQ // 512), cur(384), layer((1, 256)), layer((1, 128)),
                  _misc_spec(M_UQ, M_WO - M_UQ), ANY],
        out_specs=[cur(512, CQ // 512), full((N_CHIPS, M_WO - M_UQ, 256)), full((1, 256)), full((1, 128))],
        out_shape=[jax.ShapeDtypeStruct((rows, ZB), BF16), jax.ShapeDtypeStruct((N_CHIPS, M_WO - M_UQ, 256), F32),
                   jax.ShapeDtypeStruct((1, 256), F32), jax.ShapeDtypeStruct((1, 128), F32)],
        input_output_aliases={8: 0}, compiler_params=_params(("arbitrary",)),
    )(dq, dk, dv, z_br, rope, gq, gkv, misc, dz_buf)


def prenorm_bwd(dz_br, w_br, dh_gl, hres, gpre, dh_next, li):
    rows, d = hres.shape
    t = ROW_TILE
    cur, _, _, full, layer = _tile_specs(t, rows // HALO, li)

    def body(dz_ref, w_ref, dp_ref, x_ref, g_ref, dn_ref, dx_ref, dg_ref):
        i = pl.program_id(0)
        dh = _dot_nt(dz_ref[...], w_ref[...]) + dp_ref[...]
        _, vjp = jax.vjp(_rms, x_ref[...], g_ref[...])
        dx, dg = vjp(dh)
        dx_ref[...] = dx + dn_ref[...]
        _accumulate(i, dg_ref, dg)

    return pl.pallas_call(
        body, name="prenorm_bwd", grid=(rows // t,), in_specs=[cur(ZB), layer((d, ZB), 0), cur(d), cur(d), layer((1, d)), cur(d)],
        out_specs=[cur(d), full((1, d))], out_shape=[jax.ShapeDtypeStruct((rows, d), F32), jax.ShapeDtypeStruct((1, d), F32)],
        compiler_params=_params(("arbitrary",)),
    )(dz_br, w_br, dh_gl, hres, gpre, dh_next)


def _mesh_position():
    return lax.axis_index("x"), lax.axis_index("y"), lax.axis_index("c")


def chip_exchange(src, gather, name):
    block = src.shape if gather else src.shape[1:]

    def body(src_ref, dst_ref, send_sems, recv_sems, local_sem):
        x, y, c = _mesh_position()
        me = 2 * x + y
        peers = ((1 - x, y), (x, 1 - y), (1 - x, 1 - y))

        def part(k):
            return src_ref if gather else src_ref.at[k]

        def copy(j, slot):
            px, py = peers[j]
            return pltpu.make_async_remote_copy(src_ref=part(2 * px + py), dst_ref=dst_ref.at[slot], send_sem=send_sems.at[j],
                                                recv_sem=recv_sems.at[j], device_id=(px, py, c), device_id_type=MESH)

        local = pltpu.make_async_copy(part(me), dst_ref.at[me], local_sem)
        local.start()
        sends = [copy(j, me) for j in range(3)]
        for cp in sends:
            cp.start()
        for j, (px, py) in enumerate(peers):
            copy(j, 2 * px + py).wait_recv()
        for cp in sends:
            cp.wait_send()
        local.wait()

    return pl.pallas_call(
        body, name=name, in_specs=[pl.BlockSpec(memory_space=pl.ANY)], out_specs=pl.BlockSpec(memory_space=pl.ANY),
        out_shape=jax.ShapeDtypeStruct((N_CHIPS,) + tuple(block), src.dtype),
        scratch_shapes=[pltpu.SemaphoreType.DMA((3,)), pltpu.SemaphoreType.DMA((3,)), pltpu.SemaphoreType.DMA(())],
    )(src)


def sibling_swap(src, name):
    def body(src_ref, dst_ref, send_sem, recv_sem):
        x, y, c = _mesh_position()
        cp = pltpu.make_async_remote_copy(src_ref=src_ref, dst_ref=dst_ref, send_sem=send_sem, recv_sem=recv_sem,
                                          device_id=(x, y, 1 - c), device_id_type=MESH)
        cp.start()
        cp.wait()

    return pl.pallas_call(
        body, name=name, in_specs=[pl.BlockSpec(memory_space=pl.ANY)], out_specs=pl.BlockSpec(memory_space=pl.ANY),
        out_shape=jax.ShapeDtypeStruct(src.shape, src.dtype),
        scratch_shapes=[pltpu.SemaphoreType.DMA(()), pltpu.SemaphoreType.DMA(())],
    )(src)


def _comm_call(body, name, n_in, out_shapes, n_sems):
    return pl.pallas_call(
        body, name=name, in_specs=[ANY] * n_in, out_specs=[ANY] * len(out_shapes), out_shape=out_shapes,
        scratch_shapes=[pltpu.SemaphoreType.DMA((n,)) for n in n_sems])


def _row_halves(c, rows):
    half = rows // 2
    return pl.ds(pl.multiple_of(c * half, 16), half), pl.ds(pl.multiple_of((1 - c) * half, 16), half)


def _peers():
    x, y, c = _mesh_position()
    return x, y, c, 2 * x + y, ((1 - x, y), (x, 1 - y), (1 - x, 1 - y))


def _gather_ops(src, dst, sems, layer, own_copy):
    ici_send, ici_recv, d2d_send, d2d_recv, own_sems = sems
    n = len(src)

    def fetch(a, j, slot):
        x, y, c, _, peers = _peers()
        px, py = peers[j]
        mine, _ = _row_halves(c, src[a].shape[1])
        return pltpu.make_async_remote_copy(src_ref=src[a].at[layer, mine], dst_ref=dst[a].at[layer, slot, mine], send_sem=ici_send.at[3 * a + j],
                                            recv_sem=ici_recv.at[3 * a + j], device_id=(px, py, c), device_id_type=MESH)

    def forward(a, j, sibling_half):
        x, y, c, _, peers = _peers()
        px, py = peers[j]
        part = dst[a].at[layer, 2 * px + py, _row_halves(c, src[a].shape[1])[1 if sibling_half else 0]]
        return pltpu.make_async_remote_copy(src_ref=part, dst_ref=part, send_sem=d2d_send.at[3 * a + j], recv_sem=d2d_recv.at[3 * a + j],
                                            device_id=(x, y, 1 - c), device_id_type=MESH)

    def own(a):
        return pltpu.make_async_copy(src[a].at[layer], dst[a].at[layer, _peers()[3]], own_sems.at[a])

    def start():
        me = _peers()[3]
        for a in range(n):
            if own_copy:
                own(a).start()
            for j in range(3):
                fetch(a, j, me).start()

    def finish():
        peers = _peers()[4]
        for j, (px, py) in enumerate(peers):
            for a in range(n):
                fetch(a, j, 2 * px + py).wait_recv()
                forward(a, j, False).start()
        for j in range(3):
            for a in range(n):
                forward(a, j, True).wait_recv()
        for j in range(3):
            for a in range(n):
                fetch(a, j, 0).wait_send()
                forward(a, j, False).wait_send()
        if own_copy:
            for a in range(n):
                own(a).wait()

    return start, finish


def _exchange_ops(src, dst, sems, layer):
    send_sems, recv_sems, own_sems = sems
    n = len(src)

    def copy(a, j, slot):
        x, y, c, _, peers = _peers()
        px, py = peers[j]
        return pltpu.make_async_remote_copy(src_ref=src[a].at[2 * px + py], dst_ref=dst[a].at[layer, slot], send_sem=send_sems.at[3 * a + j],
                                            recv_sem=recv_sems.at[3 * a + j], device_id=(px, py, c), device_id_type=MESH)

    def own(a):
        me = _peers()[3]
        return pltpu.make_async_copy(src[a].at[me], dst[a].at[layer, me], own_sems.at[a])

    def start():
        me = _peers()[3]
        for a in range(n):
            own(a).start()
            for j in range(3):
                copy(a, j, me).start()

    def finish():
        peers = _peers()[4]
        for j, (px, py) in enumerate(peers):
            for a in range(n):
                copy(a, j, 2 * px + py).wait_recv()
        for j in range(3):
            for a in range(n):
                copy(a, j, 0).wait_send()
        for a in range(n):
            own(a).wait()

    return start, finish


GATHER_SEMS = lambda n: [pltpu.SemaphoreType.DMA((3 * n,))] * 4 + [pltpu.SemaphoreType.DMA((n,))]
EXCHANGE_SEMS = lambda n: [pltpu.SemaphoreType.DMA((3 * n,))] * 2 + [pltpu.SemaphoreType.DMA((n,))]


def gather_layer(srcs, dsts, layer, name):
    n = len(srcs)

    def body(*refs):
        start, finish = _gather_ops(refs[:n], refs[2 * n:3 * n], refs[3 * n:], layer, False)
        start()
        finish()

    return pl.pallas_call(
        body, name=name, in_specs=[ANY] * (2 * n), out_specs=[ANY] * n, out_shape=[jax.ShapeDtypeStruct(d.shape, d.dtype) for d in dsts],
        input_output_aliases={n + a: a for a in range(n)}, scratch_shapes=GATHER_SEMS(n),
    )(*srcs, *dsts)


def exchange_layer(ss, dsts, layer, name):
    n = len(ss)

    def body(*refs):
        start, finish = _exchange_ops(refs[:n], refs[2 * n:3 * n], refs[3 * n:], layer)
        start()
        finish()

    return pl.pallas_call(
        body, name=name, in_specs=[ANY] * (2 * n), out_specs=[ANY] * n, out_shape=[jax.ShapeDtypeStruct(d.shape, d.dtype) for d in dsts],
        input_output_aliases={n + a: a for a in range(n)}, scratch_shapes=EXCHANGE_SEMS(n),
    )(*ss, *dsts)


def swap_row_halves(ps, name):
    n = len(ps)

    def body(*refs):
        src, dst = refs[:n], refs[n:2 * n]
        send_sems, recv_sems = refs[2 * n:]
        x, y, c = _mesh_position()
        copies = [pltpu.make_async_remote_copy(src_ref=src[a].at[:, _row_halves(c, src[a].shape[1])[1]], dst_ref=dst[a], send_sem=send_sems.at[a],
                                               recv_sem=recv_sems.at[a], device_id=(x, y, 1 - c), device_id_type=MESH) for a in range(n)]
        for cp in copies:
            cp.start()
        for cp in copies:
            cp.wait()

    outs = [jax.ShapeDtypeStruct((p.shape[0], p.shape[1] // 2, p.shape[2]), p.dtype) for p in ps]
    return _comm_call(body, name, n, outs, (n, n))(*ps)


def add_row_half(p, r, c, name):
    n, half, cols = r.shape
    rb = _row_block(half, cols, 2)
    steps = half // rb

    def body(c_ref, p_ref, r_ref, o_ref):
        o_ref[...] = (p_ref[...].astype(F32) + r_ref[...].astype(F32)).astype(BF16)

    return pl.pallas_call(
        body, name=name, out_shape=jax.ShapeDtypeStruct(r.shape, BF16),
        grid_spec=pltpu.PrefetchScalarGridSpec(
            num_scalar_prefetch=1, grid=(n, steps),
            in_specs=[pl.BlockSpec((1, rb, cols), lambda k, i, c_ref: (k, c_ref[0] * steps + i, 0)),
                      pl.BlockSpec((1, rb, cols), lambda k, i, c_ref: (k, i, 0))],
            out_specs=pl.BlockSpec((1, rb, cols), lambda k, i, c_ref: (k, i, 0))),
        compiler_params=_params(("parallel", "parallel")),
    )(jnp.reshape(c, (1,)).astype(jnp.int32), p, r)


def sum_row_halves(l, c, name):
    layers, n, half, cols = l.shape
    rb = _row_block(half, cols, 4)
    steps = half // rb

    def body(c_ref, l_ref, o_ref):
        acc = l_ref[0, 0].astype(F32)
        for s in range(1, n):
            acc = acc + l_ref[0, s].astype(F32)
        o_ref[0] = acc

    return pl.pallas_call(
        body, name=name, out_shape=jax.ShapeDtypeStruct((layers, 2 * half, cols), F32),
        grid_spec=pltpu.PrefetchScalarGridSpec(
            num_scalar_prefetch=1, grid=(layers, steps), in_specs=[pl.BlockSpec((1, n, rb, cols), lambda a, i, c_ref: (a, 0, i, 0))],
            out_specs=pl.BlockSpec((1, rb, cols), lambda a, i, c_ref: (a, c_ref[0] * steps + i, 0))),
        compiler_params=_params(("parallel", "parallel")),
    )(jnp.reshape(c, (1,)).astype(jnp.int32), l)


def share_row_halves(gs, name):
    n = len(gs)

    def body(*refs):
        dst = refs[n:2 * n]
        send_sems, recv_sems = refs[2 * n:]
        x, y, c = _mesh_position()

        def copy(a, sibling_half):
            part = dst[a].at[:, _row_halves(c, dst[a].shape[1])[1 if sibling_half else 0]]
            return pltpu.make_async_remote_copy(src_ref=part, dst_ref=part, send_sem=send_sems.at[a], recv_sem=recv_sems.at[a],
                                                device_id=(x, y, 1 - c), device_id_type=MESH)

        for a in range(n):
            copy(a, False).start()
        for a in range(n):
            copy(a, True).wait_recv()
        for a in range(n):
            copy(a, False).wait_send()

    return pl.pallas_call(
        body, name=name, in_specs=[ANY] * n, out_specs=[ANY] * n, out_shape=[jax.ShapeDtypeStruct(g.shape, g.dtype) for g in gs],
        input_output_aliases={a: a for a in range(n)}, scratch_shapes=[pltpu.SemaphoreType.DMA((n,)), pltpu.SemaphoreType.DMA((n,))],
    )(*gs)


def _row_block(rows, cols, itemsize):
    best = 16
    for rb in range(16, rows + 1, 16):
        if rows % rb == 0 and rb * cols * itemsize <= 2 * 1024 * 1024:
            best = rb
    assert rows % best == 0, (rows, cols)
    return best


def _comm_block(rows):
    return 1024 if rows % 1024 == 0 else rows


def sum_slots(buf, name):
    n, r, c = buf.shape
    rb = _comm_block(r)

    def body(b_ref, o_ref):
        acc = b_ref[0].astype(F32)
        for s in range(1, n):
            acc = acc + b_ref[s].astype(F32)
        o_ref[...] = acc

    return pl.pallas_call(
        body, name=name, grid=(r // rb,), in_specs=[pl.BlockSpec((n, rb, c), lambda i: (0, i, 0))],
        out_specs=pl.BlockSpec((rb, c), lambda i: (i, 0)), out_shape=jax.ShapeDtypeStruct((r, c), F32),
        compiler_params=_params(("parallel",)),
    )(buf)


def add_pair(a, b, out_dtype, name):
    shape = a.shape
    a2, b2 = a.reshape(-1, shape[-1]), b.reshape(-1, shape[-1])
    r, c = a2.shape
    rb = _comm_block(r)

    def body(a_ref, b_ref, o_ref):
        o_ref[...] = (a_ref[...].astype(F32) + b_ref[...].astype(F32)).astype(out_dtype)

    out = pl.pallas_call(
        body, name=name, grid=(r // rb,), in_specs=[pl.BlockSpec((rb, c), lambda i: (i, 0))] * 2,
        out_specs=pl.BlockSpec((rb, c), lambda i: (i, 0)), out_shape=jax.ShapeDtypeStruct((r, c), out_dtype),
        compiler_params=_params(("parallel",)),
    )(a2, b2)
    return out.reshape(shape)


def adamw(w, g, m, v):
    shape = w.shape
    cols = shape[-1]
    rows = math.prod(shape[:-1])
    if rows * cols <= 256 * 1024:
        rb, cb = rows, cols
    else:
        rb = max(r for r in range(8, 2049, 8) if rows % r == 0)
        cb = cols if rb * cols * 4 <= 2 * 1024 * 1024 else 256
    assert rows % rb == 0 and cols % cb == 0, shape

    def body(w_ref, g_ref, m_ref, v_ref, d_ref, nm_ref, nv_ref):
        g_ = g_ref[...]
        nm = ADAM_B1 * m_ref[...] + (1.0 - ADAM_B1) * g_
        nv = ADAM_B2 * v_ref[...] + (1.0 - ADAM_B2) * (g_ * g_)
        m_hat = nm / (1.0 - ADAM_B1 ** ADAM_STEP)
        v_hat = nv / (1.0 - ADAM_B2 ** ADAM_STEP)
        d_ref[...] = -ADAM_LR * (m_hat / (jnp.sqrt(v_hat) + ADAM_EPS) + ADAM_WD * w_ref[...])
        nm_ref[...] = nm
        nv_ref[...] = nv

    spec = pl.BlockSpec((rb, cb), lambda i, j: (i, j))
    outs = pl.pallas_call(
        body, name="adamw", grid=(rows // rb, cols // cb), in_specs=[spec] * 4, out_specs=[spec] * 3,
        out_shape=[jax.ShapeDtypeStruct((rows, cols), F32)] * 3, compiler_params=_params(("parallel", "parallel")),
    )(*(a.reshape(rows, cols) for a in (w, g, m, v)))
    return tuple(o.reshape(shape) for o in outs)


def _pack(arrays, dtype, row_multiple):
    flat = jnp.concatenate([a.astype(dtype).reshape(-1) for a in arrays])
    per = LANES * row_multiple
    total = -(-flat.shape[0] // per) * per
    return jnp.pad(flat, (0, total - flat.shape[0])).reshape(total // LANES, LANES)


def _unpack(buf, shapes):
    flat = buf.reshape(-1)
    out, off = [], 0
    for s in shapes:
        n = math.prod(s)
        out.append(flat[off:off + n].reshape(s))
        off += n
    return out


def _input_weights(blocks):
    c0, c1, c2, c3 = (blocks[..., k, :, :] for k in range(N_CHIPS))
    pad = lambda n: jnp.zeros(c0.shape[:-1] + (n,), blocks.dtype)
    w_br = jnp.concatenate([c1[..., 376:1400], c0[..., 0:896], pad(64), c0[..., 896:928], pad(32), c0[..., 928:], c1[..., 0:376]], axis=-1)
    return w_br, jnp.concatenate([c1[..., 1400:], c2, c3], axis=-1)


def _input_weights_inverse(dw_br, dw_gl):
    c0 = jnp.concatenate([dw_br[..., 1024:1920], dw_br[..., 1984:2016], dw_br[..., 2048:2952]], axis=-1)
    c1 = jnp.concatenate([dw_br[..., 2952:ZB], dw_br[..., 0:1024], dw_gl[..., 0:432]], axis=-1)
    return jnp.stack([c0, c1, dw_gl[..., 432:2264], dw_gl[..., 2264:]], axis=-3)


def _block_diag(pw):
    zeros = lambda n: jnp.zeros(pw.shape[:-3] + (64, n), pw.dtype)
    rows = [jnp.concatenate([zeros(64 * g), pw[..., g, :, :], zeros(64 * (3 - g))], axis=-1) for g in range(4)]
    return jnp.concatenate(rows, axis=-2)


def _block_diag_inverse(d):
    return jnp.stack([d[..., 64 * g:64 * (g + 1), 64 * g:64 * (g + 1)] for g in range(4)], axis=-3)


def _pad_rows(a, n):
    return jnp.pad(a, ((0, n - a.shape[0]), (0, 0)))


def _rope_tables(rows):
    inv = 1.0 / (ROPE_THETA ** (jnp.arange(0, QK_ROPE, 2, dtype=F32) / QK_ROPE))
    ang = jnp.arange(rows, dtype=F32)[:, None] * inv[None, :]
    cos, sin = jnp.cos(ang), jnp.sin(ang)
    one, zero = jnp.ones((rows, 1), F32), jnp.zeros((rows, 1), F32)
    rep = lambda a, n: jnp.broadcast_to(a, (rows, n))
    c = jnp.concatenate([rep(one, 64), cos, cos, rep(one, 32)], axis=1)
    s1 = jnp.concatenate([rep(zero, 64), -sin, rep(zero, 48)], axis=1)
    s2 = jnp.concatenate([rep(zero, 80), sin, rep(zero, 32)], axis=1)
    return jnp.concatenate([c, s1, s2], axis=1)


def _misc_block(parts):
    lead = parts["w_uq"].shape[:-2]
    pad_last = lambda a, n: jnp.pad(a, [(0, 0)] * (a.ndim - 1) + [(0, n - a.shape[-1])])
    uq = pad_last(parts["w_uq"].reshape(lead + (256, 2, QK_NOPE + QK_ROPE)), HEAD_PAD).reshape(lead + (256, 256))
    kv = parts["w_ukv"].reshape(lead + (128, 2, QK_NOPE + V_DIM))
    keys = pad_last(kv[..., :QK_NOPE], HEAD_PAD).reshape(lead + (128, 256))
    values = pad_last(kv[..., QK_NOPE:].reshape(lead + (128, 2 * V_DIM)), 256)
    wo = jnp.swapaxes(parts["w_o"].reshape(lead + (256, N_CHIPS, 256)), -3, -2).reshape(lead + (D_MODEL, 256))
    gap = jnp.zeros(lead + (M_UQ - M_SC - 256, 256), uq.dtype)
    return jnp.concatenate([parts["w_out_mla"], parts["w_out_pool"], parts["w_out_conf"], parts["w_out_sc"], gap, uq, keys, values, wo],
                           axis=-2)


def _misc_unblock(block):
    lead = block.shape[:-2]
    rows = lambda lo, n: block[..., lo:lo + n, :]
    uq = rows(M_UQ, 256).reshape(lead + (256, 2, HEAD_PAD))[..., :QK_NOPE + QK_ROPE].reshape(lead + (256, 2 * (QK_NOPE + QK_ROPE)))
    keys = rows(M_UKVK, 128).reshape(lead + (128, 2, HEAD_PAD))[..., :QK_NOPE]
    values = rows(M_UKVV, 128)[..., :2 * V_DIM].reshape(lead + (128, 2, V_DIM))
    wo = jnp.swapaxes(rows(M_WO, D_MODEL).reshape(lead + (N_CHIPS, 256, 256)), -3, -2).reshape(lead + (256, D_MODEL))
    return dict(w_out_mla=rows(M_MLA, 512), w_out_pool=rows(M_POOL, 256), w_out_conf=rows(M_CONF, 256), w_out_sc=rows(M_SC, 256), w_uq=uq,
                w_ukv=jnp.concatenate([keys, values], axis=-1).reshape(lead + (128, 256)), w_o=wo)


def _to_chip_blocks(name, a):
    if name == "w_o":
        return a.reshape(a.shape[:-2] + (N_CHIPS, a.shape[-2] // N_CHIPS, a.shape[-1]))
    return jnp.swapaxes(a.reshape(a.shape[:-1] + (N_CHIPS, a.shape[-1] // N_CHIPS)), -3, -2)


def _from_chip_blocks(name, b):
    if name == "w_o":
        return b.reshape(b.shape[:-3] + (N_CHIPS * b.shape[-2], b.shape[-1]))
    s = jnp.swapaxes(b, -3, -2)
    return s.reshape(s.shape[:-2] + (N_CHIPS * s.shape[-1],))


LARGE = ("w_in",) + MISC


def gather_small(shards):
    small = chip_exchange(_pack([shards[n] for n, _, _ in SHARDED_SMALL], F32, 8), True, "gather_small_ici")
    per_chip = [_unpack(small[k], [s for _, s, _ in SHARDED_SMALL]) for k in range(N_CHIPS)]
    return {name: jnp.concatenate([per_chip[k][idx] for k in range(N_CHIPS)], axis=axis) for idx, (name, _, axis) in enumerate(SHARDED_SMALL)}


class LocalWeights:
    def __init__(self, full):
        self.w_in = _to_chip_blocks("w_in", full["w_in"])
        self.misc = _misc_block({n: _to_chip_blocks(n, full[n]) for n in MISC})
        self.grads = [None] * DEPTH

    def layer(self, i):
        return self.w_in[i], self.misc[i]

    def gather_with_attention(self, i):
        return None

    def gathered(self, dsts):
        pass

    def exchange_with_attention(self):
        return None

    def exchanged(self, dsts):
        pass

    def put_grads(self, i, w_in, misc):
        self.grads[i] = (w_in, misc)

    def reduced(self):
        out = {n: _from_chip_blocks(n, b) for n, b in _misc_unblock(jnp.stack([m for _, m in self.grads])).items()}
        out["w_in"] = _from_chip_blocks("w_in", jnp.stack([w for w, _ in self.grads]))
        return out


class MeshWeights:
    def __init__(self, shards, c, chip):
        self.c, self.chip = c, chip
        self.srcs = [shards["w_in"].astype(BF16), _misc_block({n: shards[n] for n in MISC}).astype(BF16)]
        dsts = [lax.empty((DEPTH, N_CHIPS) + s.shape[1:], BF16) for s in self.srcs]
        self.dsts = gather_layer(self.srcs, dsts, 0, "gather_layer")
        self.landed = [lax.empty((DEPTH, N_CHIPS, s.shape[1] // 2, s.shape[2]), BF16) for s in self.srcs]
        self.pending = None

    def layer(self, i):
        blocks = [d[i] for d in self.dsts]
        if i == 0:
            own = (jnp.arange(N_CHIPS) == self.chip)[:, None, None]
            blocks = [jnp.where(own, s[0][None], b) for s, b in zip(self.srcs, blocks)]
        return blocks[0], blocks[1]

    def gather_with_attention(self, i):
        return (self.srcs, self.dsts, i + 1) if i + 1 < DEPTH else None

    def gathered(self, dsts):
        if dsts:
            self.dsts = dsts

    def exchange_with_attention(self):
        return None if self.pending is None else (self.pending[0], self.landed, self.pending[1])

    def exchanged(self, dsts):
        if dsts:
            self.landed, self.pending = dsts, None

    def put_grads(self, i, w_in, misc):
        ps = [w_in.astype(BF16), misc.astype(BF16)]
        rs = swap_row_halves(ps, "reduce_swap")
        self.pending = ([add_row_half(p, r, self.c, "reduce_pair_%d" % a) for a, (p, r) in enumerate(zip(ps, rs))], i)

    def reduced(self):
        landed = exchange_layer(self.pending[0], self.landed, self.pending[1], "reduce_exchange")
        gs = [sum_row_halves(l, self.c, "reduce_sum_%d" % a) for a, l in enumerate(landed)]
        g_in, g_misc = share_row_halves(gs, "reduce_share")
        out = {"w_in": g_in}
        out.update(_misc_unblock(g_misc))
        return out


def reduce_small(grads, chip):
    names = [n for n, _ in REPLICATED] + [n for n, _, _ in SHARDED_SMALL]
    buf = _pack([grads[n] for n in names], F32, 8)
    chip_sum = add_pair(buf, sibling_swap(buf, "reduce_small_d2d"), F32, "reduce_small_pair")
    total = sum_slots(chip_exchange(chip_sum, True, "reduce_small_ici"), "reduce_small_sum")
    out = dict(zip(names, _unpack(total, [grads[n].shape for n in names])))
    for name, shape, axis in SHARDED_SMALL:
        out[name] = lax.dynamic_slice_in_dim(out[name], chip * shape[axis], shape[axis], axis)
    return out


def _prepare_small(w):
    row = lambda a: a[:, None, :]
    conf_vec = jnp.concatenate([row(w["conf_dw_b"]), row(w["conf_ln_g"]), row(w["conf_ln_b"]), jnp.zeros((DEPTH, 5, 256), F32)], axis=1)
    return dict(
        gpre=row(w["pre_norm_g"]), bias=row(w["gate_bias"]), pwbd=_block_diag(w["pool_w"]).astype(BF16), pscale=row(w["pool_scale"]),
        gq=row(w["q_norm_g"]), gkv=row(w["kv_norm_g"]), conf_w=jnp.pad(w["conf_dw_w"].astype(F32), ((0, 0), (0, 32 - CONF_K), (0, 0))),
        conf_vec=conf_vec, sc_w=jnp.pad(w["sc_dw_w"].astype(F32), ((0, 0), (0, 8 - SC_K), (0, 0))), gpost=row(w["post_norm_g"]))


def _prepare_layer(w_in_blocks, misc_blocks):
    w_br, w_gl = _input_weights(w_in_blocks)
    one = lambda a: a.astype(BF16)[None]
    return dict(w_br=one(w_br), w_gl=one(w_gl), misc=one(misc_blocks))


def local_step(x, target, w, large):
    seq = x.shape[0]
    length = N_META + seq
    rows = -(-length // ROW_TILE) * ROW_TILE
    bt = _big_tile(rows)
    hres = _pad_rows(jnp.concatenate([w["meta_tokens"].astype(F32), x], axis=0), rows)
    tgt = jnp.pad(target, ((N_META, rows - length), (0, 0)))
    rope = _rope_tables(rows)
    sw = _prepare_small(w)

    saved = []
    for i in range(DEPTH):
        lw = _prepare_layer(*large.layer(i))
        z_br, hb = prenorm_project(hres, sw["gpre"], lw["w_br"], i)
        z_gl = matmul(hb, lw["w_gl"], "nn", BF16, bt, 1024, D_MODEL, "project_gates", b_layer=0)
        ua, uc, ud, q, k, v = branches_fwd(z_br, rope, sw["pwbd"], sw["pscale"], sw["gq"], sw["gkv"], lw["misc"], sw["conf_w"],
                                           sw["conf_vec"], sw["sc_w"], i)
        o_att, lse, dsts = attention_fwd(q, k, v, large.gather_with_attention(i))
        large.gathered(dsts)
        ub, mb, o, hnew = merge_fwd(ua, o_att, uc, ud, z_br, z_gl, sw["bias"], lw["misc"], sw["gpost"], hres, i)
        saved.append(dict(lw=lw, hres=hres, hb=hb, z_br=z_br, z_gl=z_gl, ua=ua, ub=ub, uc=uc, ud=ud, q=q, k=k, v=v, o_att=o_att,
                          lse=lse, mb=mb, o=o))
        hres = hnew

    dh, total = loss_head(hres, tgt, seq)

    g = {n: [None] * DEPTH for n in ("gpre", "bias", "pwbd", "pscale", "gq", "gkv", "conf_w", "conf_vec", "sc_w", "gpost")}
    for i in reversed(range(DEPTH)):
        s = saved[i]
        lw = s["lw"]
        dm, dwo, g["gpost"][i] = postnorm_bwd(dh, s["o"], s["mb"], lw["misc"], sw["gpost"], i)
        dua, dub, duc, dud, dz_gl, dwout, g["bias"][i] = merge_bwd(dm, s["ua"], s["ub"], s["uc"], s["ud"], s["z_gl"], sw["bias"],
                                                                 lw["misc"], i)
        dz_br = lax.empty((rows, ZB), BF16)
        dz_br, g["pwbd"][i], g["pscale"][i] = pool_bwd(s["z_br"], dua, sw["pwbd"], sw["pscale"], dz_br, i)
        dz_br, g["sc_w"][i] = shortconv_bwd(s["z_br"], dud, sw["sc_w"], dz_br, i)
        dc, dz_br, g["conf_vec"][i] = conformer_bwd_tail(s["z_br"], duc, sw["conf_w"], sw["conf_vec"], dz_br, i)
        dz_br, g["conf_w"][i] = conformer_bwd_conv(s["z_br"], dc, sw["conf_w"], dz_br, i)
        do, dz_br, delta = attention_bwd_prep(dub, s["o_att"], s["z_br"], dz_br)
        dq, dk, dv, dsts = attention_bwd(s["q"], s["k"], s["v"], do, s["lse"], delta, large.exchange_with_attention())
        large.exchanged(dsts)
        dz_br, dwup, g["gq"][i], g["gkv"][i] = mla_prep_bwd(dq, dk, dv, s["z_br"], rope, sw["gq"], sw["gkv"], lw["misc"], dz_br, i)
        dw_br = matmul(s["hb"], dz_br, "tn", F32, D_MODEL, ZB // 2, bt, "grad_w_branch")
        dw_gl = matmul(s["hb"], dz_gl, "tn", F32, D_MODEL, 1024, bt, "grad_w_gates")
        dh_gl = matmul(dz_gl, lw["w_gl"], "nt", F32, bt, D_MODEL, 1024, "grad_h_gates", b_layer=0)
        dh, g["gpre"][i] = prenorm_bwd(dz_br, lw["w_br"], dh_gl, s["hres"], sw["gpre"], dh, i)
        gap = jnp.zeros((N_CHIPS, M_UQ - M_SC - 256, 256), F32)
        large.put_grads(i, _input_weights_inverse(dw_br, dw_gl), jnp.concatenate([dwout, gap, dwup, dwo], axis=1))

    g = {n: jnp.stack(parts) for n, parts in g.items()}
    grads = dict(
        meta_tokens=dh[:N_META], pre_norm_g=g["gpre"][:, 0], gate_bias=g["bias"][:, 0], pool_w=_block_diag_inverse(g["pwbd"]),
        pool_scale=g["pscale"][:, 0], q_norm_g=g["gq"][:, 0], kv_norm_g=g["gkv"][:, 0], conf_dw_w=g["conf_w"][:, :CONF_K],
        conf_dw_b=g["conf_vec"][:, 2], conf_ln_g=g["conf_vec"][:, 0], conf_ln_b=g["conf_vec"][:, 1], sc_dw_w=g["sc_w"][:, :SC_K],
        post_norm_g=g["gpost"][:, 0])
    return total[0, 0], dh[N_META:length], grads


def kernel(x, meta_tokens, pre_norm_g, w_in, gate_bias, pool_w, pool_scale, w_out_pool, q_norm_g, w_uq, kv_norm_g, w_ukv, w_out_mla, conf_dw_w, conf_dw_b, conf_ln_g, conf_ln_b, w_out_conf, sc_dw_w, w_out_sc, w_o, post_norm_g, loss_target, m_meta_tokens, m_pre_norm_g, m_w_in, m_gate_bias, m_pool_w, m_pool_scale, m_w_out_pool, m_q_norm_g, m_w_uq, m_kv_norm_g, m_w_ukv, m_w_out_mla, m_conf_dw_w, m_conf_dw_b, m_conf_ln_g, m_conf_ln_b, m_w_out_conf, m_sc_dw_w, m_w_out_sc, m_w_o, m_post_norm_g, v_meta_tokens, v_pre_norm_g, v_w_in, v_gate_bias, v_pool_w, v_pool_scale, v_w_out_pool, v_q_norm_g, v_w_uq, v_kv_norm_g, v_w_ukv, v_w_out_mla, v_conf_dw_w, v_conf_dw_b, v_conf_ln_g, v_conf_ln_b, v_w_out_conf, v_sc_dw_w, v_w_out_sc, v_w_o, v_post_norm_g):
    args = locals()
    weights = {n: args[n] for n in WEIGHT_ORDER}
    c = lax.axis_index("c")
    chip = 2 * lax.axis_index("x") + lax.axis_index("y")

    small = {n: weights[n] for n, _ in REPLICATED}
    small.update(gather_small(weights))
    large = MeshWeights(weights, c, chip)
    total, dx, grads = local_step(x[0], loss_target[0], small, large)
    loss = lax.psum(total * (0.5 / D_MODEL), ("x", "y", "c"))

    reduced = large.reduced()
    reduced.update(reduce_small(grads, chip))

    flip = lambda a: jnp.swapaxes(a, 1, 2)
    deltas, new_m, new_v = [], [], []
    for n in WEIGHT_ORDER:
        operands = (weights[n], reduced[n], args["m_" + n], args["v_" + n])
        if n == "w_in":
            operands = (flip(operands[0]), lax.optimization_barrier(flip(operands[1])), flip(operands[2]), flip(operands[3]))
            reduced[n] = flip(operands[1])
        d, nm, nv = adamw(*operands)
        if n == "w_in":
            d, nm, nv = flip(d), flip(nm), flip(nv)
        deltas.append(d)
        new_m.append(nm)
        new_v.append(nv)
    return (loss, dx[None], *[reduced[n] for n in WEIGHT_ORDER], *deltas, *new_m, *new_v)
```

```python
import functools
import math

import jax
import jax.numpy as jnp
from jax import lax
from jax.experimental import pallas as pl
from jax.experimental.pallas import tpu as pltpu

F32 = jnp.float32
BF16 = jnp.bfloat16

D_MODEL = 1024
DEPTH = 4
N_META = 16
EPS = 1e-6
HEADS = 8
QK_NOPE = 64
QK_ROPE = 32
V_DIM = 64
HEAD_PAD = 128
ROPE_THETA = 10000.0
Q_SCALE = (QK_NOPE + QK_ROPE) ** -0.5
CONF_K = 31
SC_K = 3
IN_W = 7328
N_CHIPS = 4

ZB = 3328
ZG = 4096
BG, C2, XV, SG, PV, PG, CQ, CKV, KR, MG, CA, CGT, CG = (0, 256, 512, 768, 1024, 1280, 1536, 1792, 1920, 2048, 2560, 2816, 3072)

KEY_GROUP = 4
ROW_TILE = 384
HALO = 32
LANES = 128
VMEM_LIMIT = 56 * 1024 * 1024

ADAM_LR = 0.001
ADAM_B1 = 0.9
ADAM_B2 = 0.999
ADAM_EPS = 1e-08
ADAM_WD = 0.01
ADAM_STEP = 10

MESH = pl.DeviceIdType.MESH
ANY = pl.BlockSpec(memory_space=pl.ANY)

MISC = ("w_out_mla", "w_out_pool", "w_out_conf", "w_out_sc", "w_uq", "w_ukv", "w_o")
M_MLA, M_POOL, M_CONF, M_SC, M_UQ, M_UKVK, M_UKVV, M_WO, MISC_ROWS = 0, 512, 768, 1024, 1536, 1792, 1920, 2048, 3072
SHARDED_SMALL = (
    ("meta_tokens", (N_META, 256), 1),
    ("conf_dw_w", (DEPTH, CONF_K, 64), 2),
    ("sc_dw_w", (DEPTH, SC_K, 64), 2),
)
REPLICATED = (
    ("pre_norm_g", (DEPTH, D_MODEL)),
    ("gate_bias", (DEPTH, 4 * D_MODEL)),
    ("pool_w", (DEPTH, 4, 64, 64)),
    ("pool_scale", (DEPTH, 256)),
    ("q_norm_g", (DEPTH, 256)),
    ("kv_norm_g", (DEPTH, 128)),
    ("conf_dw_b", (DEPTH, 256)),
    ("conf_ln_g", (DEPTH, 256)),
    ("conf_ln_b", (DEPTH, 256)),
    ("post_norm_g", (DEPTH, D_MODEL)),
)
WEIGHT_ORDER = ("meta_tokens", "pre_norm_g", "w_in", "gate_bias", "pool_w", "pool_scale", "w_out_pool", "q_norm_g", "w_uq",
                "kv_norm_g", "w_ukv", "w_out_mla", "conf_dw_w", "conf_dw_b", "conf_ln_g", "conf_ln_b", "w_out_conf", "sc_dw_w",
                "w_out_sc", "w_o", "post_norm_g")


def _dot(a, b):
    return lax.dot_general(a, b, (((1,), (0,)), ((), ())), preferred_element_type=F32)


def _dot_nt(a, b):
    return lax.dot_general(a, b, (((1,), (1,)), ((), ())), preferred_element_type=F32)


def _dot_tn(a, b):
    return lax.dot_general(a, b, (((0,), (0,)), ((), ())), preferred_element_type=F32)


def _sigmoid(x):
    return jax.nn.sigmoid(x)


def _silu(x):
    return x * _sigmoid(x)


def _silu_grad(x):
    s = _sigmoid(x)
    return s * (1.0 + x * (1.0 - s))


def _rms(x, g):
    return x * lax.rsqrt(jnp.mean(x * x, axis=-1, keepdims=True) + EPS) * g


def _sh(x, d):
    return x if d == 0 else pltpu.roll(x, d, 0)


def _ash(x, d):
    return x if d == 0 else pltpu.roll(x, x.shape[0] - d, 0)


def _lanes8(t):
    return jnp.concatenate([t] * HEADS, axis=1)


def _pool_window_sums(v, shift):
    a2 = v + shift(v, 1)
    a4 = a2 + shift(a2, 2)
    a8 = a4 + shift(a4, 4)
    a16 = a8 + shift(a8, 8)
    lane = lax.broadcasted_iota(jnp.int32, v.shape, 1)
    return jnp.where(lane < 64, a2, jnp.where(lane < 128, a4, jnp.where(lane < 192, a8, a16)))


def _pool_counts(first_row, rows):
    pos = first_row + lax.broadcasted_iota(jnp.int32, (rows, 256), 0)
    lane = lax.broadcasted_iota(jnp.int32, (rows, 256), 1)
    width = jnp.where(lane < 64, 2, jnp.where(lane < 128, 4, jnp.where(lane < 192, 8, 16)))
    return jnp.maximum(jnp.minimum(pos + 1, width), 1).astype(F32)


def _params(sem=None):
    return pltpu.CompilerParams(dimension_semantics=sem, vmem_limit_bytes=VMEM_LIMIT)


def _tile_specs(t, n_halo_blocks, li=0):
    per = t // HALO

    def layer(shape, idx=li):
        return pl.BlockSpec((None,) + tuple(shape), lambda i: (idx,) + (0,) * len(shape))

    def cur(c, cb=0):
        return pl.BlockSpec((t, c), lambda i: (i, cb))

    def prev(c, cb=0):
        return pl.BlockSpec((HALO, c), lambda i: (jnp.maximum(i * per - 1, 0), cb))

    def nxt(c, cb=0):
        return pl.BlockSpec((HALO, c), lambda i: (jnp.minimum((i + 1) * per, n_halo_blocks - 1), cb))

    def full(shape):
        return pl.BlockSpec(shape, lambda i: (0,) * len(shape))

    return cur, prev, nxt, full, layer


def _big_tile(rows):
    return rows // 3 if rows % (3 * LANES) == 0 else ROW_TILE


def matmul(a, b, mode, out_dtype, tm, tn, tk, name, b_layer=None):
    bs = b.shape if b_layer is None else b.shape[1:]
    lead = () if b_layer is None else (None,)
    pick = (lambda *ix: ix) if b_layer is None else (lambda *ix: (b_layer,) + ix)
    if mode == "nn":
        (m, k), n = a.shape, bs[1]
        a_spec = pl.BlockSpec((tm, tk), lambda i, j, kk: (i, kk))
        b_spec = pl.BlockSpec(lead + (tk, tn), lambda i, j, kk: pick(kk, j))
        dot = _dot
    elif mode == "nt":
        (m, k), n = a.shape, bs[0]
        a_spec = pl.BlockSpec((tm, tk), lambda i, j, kk: (i, kk))
        b_spec = pl.BlockSpec(lead + (tn, tk), lambda i, j, kk: pick(j, kk))
        dot = _dot_nt
    else:
        (k, m), n = a.shape, bs[1]
        a_spec = pl.BlockSpec((tk, tm), lambda i, j, kk: (kk, i))
        b_spec = pl.BlockSpec(lead + (tk, tn), lambda i, j, kk: pick(kk, j))
        dot = _dot_tn
    assert m % tm == 0 and n % tn == 0 and k % tk == 0, (a.shape, bs, tm, tn, tk)
    nk = k // tk

    def body(a_ref, b_ref, o_ref, acc_ref):
        kk = pl.program_id(2)

        @pl.when(kk == 0)
        def _():
            acc_ref[...] = jnp.zeros_like(acc_ref)

        acc_ref[...] += dot(a_ref[...], b_ref[...])

        @pl.when(kk == nk - 1)
        def _():
            o_ref[...] = acc_ref[...].astype(out_dtype)

    return pl.pallas_call(
        body, name=name, grid=(m // tm, n // tn, nk), in_specs=[a_spec, b_spec],
        out_specs=pl.BlockSpec((tm, tn), lambda i, j, kk: (i, j)), out_shape=jax.ShapeDtypeStruct((m, n), out_dtype),
        scratch_shapes=[pltpu.VMEM((tm, tn), F32)], compiler_params=_params(("parallel", "parallel", "arbitrary")),
    )(a, b)


def prenorm_project(hres, g, w, li):
    rows, d = hres.shape
    n = w.shape[2]
    tm, tn = _big_tile(rows), n // 2

    def body(x_ref, g_ref, w_ref, z_ref, hb_ref):
        @pl.when(pl.program_id(1) == 0)
        def _():
            hb_ref[...] = _rms(x_ref[...], g_ref[...]).astype(BF16)

        z_ref[...] = _dot(hb_ref[...], w_ref[...]).astype(BF16)

    return pl.pallas_call(
        body, name="prenorm_project", grid=(rows // tm, n // tn),
        in_specs=[pl.BlockSpec((tm, d), lambda i, j: (i, 0)), pl.BlockSpec((None, 1, d), lambda i, j: (li, 0, 0)),
                  pl.BlockSpec((None, d, tn), lambda i, j: (0, 0, j))],
        out_specs=[pl.BlockSpec((tm, tn), lambda i, j: (i, j)), pl.BlockSpec((tm, d), lambda i, j: (i, 0))],
        out_shape=[jax.ShapeDtypeStruct((rows, n), BF16), jax.ShapeDtypeStruct((rows, d), BF16)],
        compiler_params=_params(("parallel", "arbitrary")),
    )(hres, g, w)


def _rope(q, c, s1, s2, width):
    return q * c + pltpu.roll(q, width - 16, 1) * s1 + pltpu.roll(q, 16, 1) * s2


def _rope_transposed(dq, c, s1, s2, width):
    return dq * c + pltpu.roll(dq * s1, 16, 1) + pltpu.roll(dq * s2, width - 16, 1)


def _conf_conv(g1, w_ref):
    acc = jnp.zeros_like(g1)
    for k in range(CONF_K):
        acc = acc + w_ref[k:k + 1, :] * _sh(g1, CONF_K - 1 - k)
    return acc


def _conf_tail(c, cg, lg, lb):
    mu = jnp.mean(c, axis=-1, keepdims=True)
    xc = c - mu
    var = jnp.mean(xc * xc, axis=-1, keepdims=True)
    n = xc * lax.rsqrt(var + EPS) * lg + lb
    return _silu(n) * _silu(cg)


def _misc_spec(row0, rows):
    assert row0 % rows == 0
    return pl.BlockSpec((None, N_CHIPS, rows, 256), lambda i: (0, 0, row0 // rows, 0))


def _chip_columns(x, w_ref, row0, rows, lanes=256):
    return jnp.concatenate([_dot(x, w_ref[k, row0:row0 + rows, 0:lanes]) for k in range(N_CHIPS)], axis=1)


def branches_fwd(z_br, rope, pwbd, pscale, gq, gkv, misc, conf_w, conf_vec, sc_w, li):
    rows = z_br.shape[0]
    t = ROW_TILE
    cur, prev, _, _, layer = _tile_specs(t, rows // HALO, li)

    def body(zc_ref, zp_ref, rope_ref, pw_ref, ps_ref, gq_ref, gkv_ref, up_ref, cw_ref, cv_ref, sw_ref,
             ua_ref, uc_ref, ud_ref, q_ref, k_ref, v_ref):
        i = pl.program_id(0)
        zp = jnp.where(i == 0, jnp.zeros(zp_ref.shape, zp_ref.dtype), zp_ref[...])

        def ext(lo, w=256):
            return jnp.concatenate([zp[:, lo:lo + w], zc_ref[:, lo:lo + w]], axis=0).astype(F32)

        def col(lo, w=256):
            return zc_ref[:, lo:lo + w].astype(F32)

        v = ext(PV)
        p = (_pool_window_sums(v, _sh) / _pool_counts(i * t - HALO, t + HALO) - v)[HALO:]
        ya = _dot(p.astype(BF16), pw_ref[...]) * ps_ref[...]
        ua_ref[...] = (ya * _silu(col(PG))).astype(BF16)

        g1 = ext(CA) * _sigmoid(ext(CGT))
        c = _conf_conv(g1, cw_ref)[HALO:] + cv_ref[0:1, :]
        uc_ref[...] = _conf_tail(c, col(CG), cv_ref[1:2, :], cv_ref[2:3, :]).astype(BF16)

        e = ext(C2) * ext(XV)
        f = jnp.zeros_like(e)
        for k in range(SC_K):
            f = f + sw_ref[k:k + 1, :] * _sh(e, SC_K - 1 - k)
        ud_ref[...] = (col(BG) * f[HALO:] * _silu(col(SG))).astype(BF16)

        cth, s1, s2 = rope_ref[:, 0:128], rope_ref[:, 128:256], rope_ref[:, 256:384]
        qn = _rms(col(CQ), gq_ref[...]).astype(BF16)
        q = _chip_columns(qn, up_ref, 0, 256)
        w8 = HEADS * HEAD_PAD
        q_ref[...] = (_rope(q, _lanes8(cth), _lanes8(s1), _lanes8(s2), w8) * Q_SCALE).astype(BF16)
        kvn = _rms(col(CKV, 128), gkv_ref[...]).astype(BF16)
        kr = _rope(col(KR, 128), cth, s1, s2, HEAD_PAD)
        k_ref[...] = (_chip_columns(kvn, up_ref, M_UKVK - M_UQ, 128) + _lanes8(kr)).astype(BF16)
        v_ref[...] = _chip_columns(kvn, up_ref, M_UKVV - M_UQ, 128, 2 * V_DIM).astype(BF16)

    outs = [jax.ShapeDtypeStruct((rows, 256), BF16)] * 3 + [jax.ShapeDtypeStruct((rows, 1024), BF16)] * 2 + [
        jax.ShapeDtypeStruct((rows, 512), BF16)]
    return pl.pallas_call(
        body, name="branches_fwd", grid=(rows // t,),
        in_specs=[cur(ZB), prev(ZB), cur(384), layer((256, 256)), layer((1, 256)), layer((1, 256)), layer((1, 128)),
                  _misc_spec(M_UQ, M_WO - M_UQ), layer((32, 256)), layer((8, 256)), layer((8, 256))],
        out_specs=[cur(256), cur(256), cur(256), cur(1024), cur(1024), cur(512)], out_shape=outs,
        compiler_params=_params(("parallel",)),
    )(z_br, z_br, rope, pwbd, pscale, gq, gkv, misc, conf_w, conf_vec, sc_w)


def _head_lane_mask(h):
    lane = lax.broadcasted_iota(jnp.int32, (1, 2 * V_DIM), 1)
    return (lane >= V_DIM * h) & (lane < V_DIM * (h + 1))


def attention_fwd(q, k, v, gather=None):
    rows = q.shape[0]
    tq = ROW_TILE
    nq = rows // tq
    n = 0 if gather is None else len(gather[0])

    def body(*refs):
        if n:
            start, finish = _gather_ops(refs[3:3 + n], refs[5 + 2 * n:5 + 3 * n], refs[5 + 3 * n:], gather[2], True)
            pl.when((pl.program_id(0) == 0) & (pl.program_id(1) == 0))(start)
        compute(*refs[:3], *refs[3 + 2 * n:5 + 2 * n])
        if n:
            pl.when((pl.program_id(0) == HEADS // 2 - 1) & (pl.program_id(1) == nq - 1))(finish)

    def compute(q_ref, k_ref, v_ref, o_ref, lse_ref):
        i = pl.program_id(1)

        def head_step(h, tile, n_tiles, carry, masked):
            m, l, acc = carry
            width = n_tiles * tq
            r0 = pl.multiple_of(tile * tq, tq)
            kh = k_ref[pl.ds(r0, width), HEAD_PAD * h:HEAD_PAD * (h + 1)]
            vh = jnp.where(_head_lane_mask(h), v_ref[pl.ds(r0, width), :], jnp.zeros((), BF16))
            s = _dot_nt(q_ref[:, HEAD_PAD * h:HEAD_PAD * (h + 1)], kh)
            if masked:
                row = lax.broadcasted_iota(jnp.int32, (tq, width), 0)
                colm = lax.broadcasted_iota(jnp.int32, (tq, width), 1)
                s = jnp.where(colm <= row + (width - tq), s, -1e30)
            m2 = jnp.maximum(m, jnp.max(s, axis=-1, keepdims=True))
            alpha = jnp.exp(m - m2)
            pr = jnp.exp(s - m2)
            return m2, alpha * l + jnp.sum(pr, axis=-1, keepdims=True), alpha * acc + _dot(pr.astype(BF16), vh)

        def step(tile, n_tiles, carry, masked):
            return tuple(head_step(h, tile, n_tiles, carry[h], masked) for h in range(2))

        init = (jnp.full((tq, 1), -1e30, F32), jnp.zeros((tq, 1), F32), jnp.zeros((tq, 2 * V_DIM), F32))
        group = min(KEY_GROUP, nq)
        carry = lax.fori_loop(0, i // group, lambda t, cr: step(group * t, group, cr, False), (init, init))
        carry = lax.switch(i % group, [functools.partial(lambda cr, r: step(i - r, r + 1, cr, True), r=r) for r in range(group)], carry)
        out = jnp.zeros((tq, 2 * V_DIM), F32)
        for h, (m, l, acc) in enumerate(carry):
            out = out + acc / l
            lse_ref[h] = jnp.broadcast_to(m + jnp.log(l), (tq, LANES))
        o_ref[...] = out.astype(BF16)

    srcs, dsts = ([], []) if gather is None else (list(gather[0]), list(gather[1]))
    outs = pl.pallas_call(
        body, name="attention_fwd" if gather is None else "attention_fwd_gather", grid=(HEADS // 2, nq),
        in_specs=[pl.BlockSpec((tq, 2 * HEAD_PAD), lambda p, i: (i, p)), pl.BlockSpec((rows, 2 * HEAD_PAD), lambda p, i: (0, p)),
                  pl.BlockSpec((rows, 2 * V_DIM), lambda p, i: (0, p))] + [ANY] * (2 * n),
        out_specs=[pl.BlockSpec((tq, 2 * V_DIM), lambda p, i: (i, p)), pl.BlockSpec((2, tq, LANES), lambda p, i: (p, i, 0))] + [ANY] * n,
        out_shape=[jax.ShapeDtypeStruct((rows, HEADS * V_DIM), BF16), jax.ShapeDtypeStruct((HEADS, rows, LANES), F32)] + [
            jax.ShapeDtypeStruct(d.shape, d.dtype) for d in dsts],
        input_output_aliases={3 + n + a: 2 + a for a in range(n)}, scratch_shapes=GATHER_SEMS(n) if n else [],
        compiler_params=_params(("arbitrary", "arbitrary") if n else ("parallel", "parallel")),
    )(q, k, v, *srcs, *dsts)
    return outs[0], outs[1], list(outs[2:])


OUT_PROJECTIONS = ((M_POOL, 256), (M_MLA, 512), (M_CONF, 256), (M_SC, 256))


def _chunks(x, n=N_CHIPS, width=256):
    return [x[:, width * k:width * (k + 1)] for k in range(n)]


def merge_fwd(ua, o_att, uc, ud, z_br, z_gl, bias, misc, gpost, hres, li):
    rows = hres.shape[0]
    t = ROW_TILE
    cur, _, _, _, layer = _tile_specs(t, rows // HALO, li)
    d = D_MODEL

    def body(ua_ref, ob_ref, uc_ref, ud_ref, mg_ref, gl_ref, b_ref, wout_ref, wo_ref, gp_ref, h_ref, ub_ref, mb_ref, o_ref, hn_ref):
        ub = (ob_ref[...].astype(F32) * _silu(mg_ref[...].astype(F32))).astype(BF16)
        ub_ref[...] = ub
        m = jnp.zeros((t, d), F32)
        for idx, (u, (row0, n)) in enumerate(zip((ua_ref[...], ub, uc_ref[...], ud_ref[...]), OUT_PROJECTIONS)):
            gate = _sigmoid(gl_ref[:, d * idx:d * (idx + 1)].astype(F32) + b_ref[:, d * idx:d * (idx + 1)])
            m = m + gate * _chip_columns(u, wout_ref, row0, n)
        mb = m.astype(BF16)
        mb_ref[...] = mb
        o = jnp.concatenate([sum(_dot(mk, wo_ref[k, 256 * j:256 * (j + 1), :]) for k, mk in enumerate(_chunks(mb)))
                             for j in range(N_CHIPS)], axis=1)
        o_ref[...] = o
        hn_ref[...] = h_ref[...] + _rms(o, gp_ref[...])

    return pl.pallas_call(
        body, name="merge_fwd", grid=(rows // t,),
        in_specs=[cur(256), cur(512), cur(256), cur(256), cur(512, MG // 512), cur(ZG), layer((1, ZG)), _misc_spec(0, 1280),
                  _misc_spec(M_WO, D_MODEL), layer((1, d)), cur(d)],
        out_specs=[cur(512), cur(d), cur(d), cur(d)],
        out_shape=[jax.ShapeDtypeStruct((rows, 512), BF16), jax.ShapeDtypeStruct((rows, d), BF16), jax.ShapeDtypeStruct((rows, d), F32),
                   jax.ShapeDtypeStruct((rows, d), F32)],
        compiler_params=_params(("parallel",)),
    )(ua, o_att, uc, ud, z_br, z_gl, bias, misc, misc, gpost, hres)


def loss_head(hres, target, n_tokens):
    rows, d = hres.shape
    t = ROW_TILE
    cur, _, _, full, _ = _tile_specs(t, rows // HALO)
    n_steps = rows // t

    def body(h_ref, t_ref, dh_ref, tot_ref, acc_ref):
        i = pl.program_id(0)

        @pl.when(i == 0)
        def _():
            acc_ref[...] = jnp.zeros_like(acc_ref)

        r = i * t + lax.broadcasted_iota(jnp.int32, (t, 1), 0)
        diff = jnp.where((r >= N_META) & (r < N_META + n_tokens), h_ref[...] - t_ref[...], 0.0)
        dh_ref[...] = diff * (1.0 / d)
        acc_ref[...] += jnp.sum(diff * diff, axis=0, keepdims=True)

        @pl.when(i == n_steps - 1)
        def _():
            tot_ref[...] = jnp.broadcast_to(jnp.sum(acc_ref[...], axis=1, keepdims=True), (1, LANES))

    return pl.pallas_call(
        body, name="loss_head", grid=(n_steps,), in_specs=[cur(d), cur(d)], out_specs=[cur(d), full((1, LANES))],
        out_shape=[jax.ShapeDtypeStruct((rows, d), F32), jax.ShapeDtypeStruct((1, LANES), F32)],
        scratch_shapes=[pltpu.VMEM((1, d), F32)], compiler_params=_params(("arbitrary",)),
    )(hres, target)


def _accumulate(i, ref, value):
    @pl.when(i == 0)
    def _():
        ref[...] = value

    @pl.when(i > 0)
    def _():
        ref[...] += value


def postnorm_bwd(dh, o, mb, misc, gpost, li):
    rows, d = dh.shape
    t = ROW_TILE
    cur, _, _, full, layer = _tile_specs(t, rows // HALO, li)

    def body(dh_ref, o_ref, mb_ref, wo_ref, gp_ref, dm_ref, dwo_ref, dgp_ref):
        i = pl.program_id(0)
        _, vjp = jax.vjp(_rms, o_ref[...], gp_ref[...])
        do, dg = vjp(dh_ref[...])
        dob = do.astype(BF16)
        dm_ref[...] = jnp.concatenate([sum(_dot_nt(dj, wo_ref[k, 256 * j:256 * (j + 1), :]) for j, dj in enumerate(_chunks(dob)))
                                       for k in range(N_CHIPS)], axis=1)
        dwo = _dot_tn(mb_ref[...], dob)
        for k in range(N_CHIPS):
            _accumulate(i, dwo_ref.at[k], jnp.concatenate(_chunks(dwo[256 * k:256 * (k + 1), :]), axis=0))
        _accumulate(i, dgp_ref, dg)

    return pl.pallas_call(
        body, name="postnorm_bwd", grid=(rows // t,), in_specs=[cur(d), cur(d), cur(d), _misc_spec(M_WO, d), layer((1, d))],
        out_specs=[cur(d), full((N_CHIPS, d, 256)), full((1, d))],
        out_shape=[jax.ShapeDtypeStruct((rows, d), F32), jax.ShapeDtypeStruct((N_CHIPS, d, 256), F32), jax.ShapeDtypeStruct((1, d), F32)],
        compiler_params=_params(("arbitrary",)),
    )(dh, o, mb, misc, gpost)


def merge_bwd(dm, ua, ub, uc, ud, z_gl, bias, misc, li):
    rows, d = dm.shape
    t = ROW_TILE
    cur, _, _, full, layer = _tile_specs(t, rows // HALO, li)
    widths = (256, 512, 256, 256)

    def body(dm_ref, ua_ref, ub_ref, uc_ref, ud_ref, gl_ref, b_ref, w_ref, dua_ref, dub_ref, duc_ref, dud_ref, dgl_ref, dw_ref, db_ref):
        i = pl.program_id(0)
        dm = dm_ref[...]
        groups = ((ua_ref, dua_ref), (ub_ref, dub_ref), (uc_ref, duc_ref), (ud_ref, dud_ref))
        for idx, ((u_ref, du_ref), (row0, n)) in enumerate(zip(groups, OUT_PROJECTIONS)):
            cols = slice(d * idx, d * (idx + 1))
            u = u_ref[...]
            gate = _sigmoid(gl_ref[:, cols].astype(F32) + b_ref[:, cols])
            dgl = dm * _chip_columns(u, w_ref, row0, n) * gate * (1.0 - gate)
            dgl_ref[:, cols] = dgl.astype(BF16)
            _accumulate(i, db_ref.at[:, cols], jnp.sum(dgl, axis=0, keepdims=True))
            dyb = (dm * gate).astype(BF16)
            du_ref[...] = sum(_dot_nt(dyk, w_ref[k, row0:row0 + n, :]) for k, dyk in enumerate(_chunks(dyb)))
            for k, dwk in enumerate(_chunks(_dot_tn(u, dyb))):
                _accumulate(i, dw_ref.at[k, row0:row0 + n, :], dwk)

    return pl.pallas_call(
        body, name="merge_bwd", grid=(rows // t,),
        in_specs=[cur(d), cur(256), cur(512), cur(256), cur(256), cur(ZG), layer((1, ZG)), _misc_spec(0, 1280)],
        out_specs=[cur(256), cur(512), cur(256), cur(256), cur(ZG), full((N_CHIPS, 1280, 256)), full((1, ZG))],
        out_shape=[jax.ShapeDtypeStruct((rows, w), F32) for w in widths] + [jax.ShapeDtypeStruct((rows, ZG), BF16),
                                                                            jax.ShapeDtypeStruct((N_CHIPS, 1280, 256), F32),
                                                                            jax.ShapeDtypeStruct((1, ZG), F32)],
        compiler_params=_params(("arbitrary",)),
    )(dm, ua, ub, uc, ud, z_gl, bias, misc)


def pool_bwd(z_br, dua, pwbd, pscale, dz_buf, li):
    rows = z_br.shape[0]
    t = ROW_TILE
    n_steps = rows // t
    cur, prev, nxt, full, layer = _tile_specs(t, rows // HALO, li)

    def body(zc_ref, zp_ref, zn_ref, dc_ref, dn_ref, pw_ref, ps_ref, _, dz_ref, dpw_ref, dps_ref):
        i = pl.program_id(0)
        zp = jnp.where(i == 0, jnp.zeros(zp_ref.shape, zp_ref.dtype), zp_ref[...])
        zn = jnp.where(i == n_steps - 1, jnp.zeros(zn_ref.shape, zn_ref.dtype), zn_ref[...])
        dun = jnp.where(i == n_steps - 1, jnp.zeros(dn_ref.shape, dn_ref.dtype), dn_ref[...])

        def ext(lo):
            return jnp.concatenate([zp[:, lo:lo + 256], zc_ref[:, lo:lo + 256], zn[:, lo:lo + 256]], axis=0).astype(F32)

        n_ext = t + 2 * HALO
        v, pg = ext(PV), ext(PG)
        cnt = _pool_counts(i * t - HALO, n_ext)
        p = (_pool_window_sums(v, _sh) / cnt - v)[HALO:HALO + t]
        du = jnp.concatenate([jnp.zeros((HALO, 256), F32), dc_ref[...], dun], axis=0)
        dya = du * _silu(pg)
        dypb = (dya * ps_ref[...]).astype(BF16)
        dp = _dot_nt(dypb, pw_ref[...])
        dv = (_pool_window_sums(dp / cnt, _ash) - dp)[HALO:HALO + t]
        pb = p.astype(BF16)
        pw = _dot(pb, pw_ref[...])
        duc, pgc = dc_ref[...], pg[HALO:HALO + t]
        dpg = duc * pw * ps_ref[...] * _silu_grad(pgc)
        dz_ref[...] = jnp.concatenate([dv, dpg], axis=1).astype(BF16)
        _accumulate(i, dpw_ref, _dot_tn(pb, dypb[HALO:HALO + t]))
        _accumulate(i, dps_ref, jnp.sum(dya[HALO:HALO + t] * pw, axis=0, keepdims=True))

    return pl.pallas_call(
        body, name="pool_bwd", grid=(n_steps,),
        in_specs=[cur(ZB), prev(ZB), nxt(ZB), cur(256), nxt(256), layer((256, 256)), layer((1, 256)), ANY],
        out_specs=[cur(512, PV // 512), full((256, 256)), full((1, 256))],
        out_shape=[jax.ShapeDtypeStruct((rows, ZB), BF16), jax.ShapeDtypeStruct((256, 256), F32), jax.ShapeDtypeStruct((1, 256), F32)],
        input_output_aliases={7: 0}, compiler_params=_params(("arbitrary",)),
    )(z_br, z_br, z_br, dua, dua, pwbd, pscale, dz_buf)


def shortconv_bwd(z_br, dud, sc_w, dz_buf, li):
    rows = z_br.shape[0]
    t = ROW_TILE
    n_steps = rows // t
    cur, prev, nxt, full, layer = _tile_specs(t, rows // HALO, li)

    def body(zc_ref, zp_ref, zn_ref, dc_ref, dn_ref, sw_ref, _, dz_ref, dw_ref):
        i = pl.program_id(0)
        zp = jnp.where(i == 0, jnp.zeros(zp_ref.shape, zp_ref.dtype), zp_ref[...])
        zn = jnp.where(i == n_steps - 1, jnp.zeros(zn_ref.shape, zn_ref.dtype), zn_ref[...])
        dun = jnp.where(i == n_steps - 1, jnp.zeros(dn_ref.shape, dn_ref.dtype), dn_ref[...])

        def ext(lo):
            return jnp.concatenate([zp[:, lo:lo + 256], zc_ref[:, lo:lo + 256], zn[:, lo:lo + 256]], axis=0).astype(F32)

        mid = slice(HALO, HALO + t)
        bg, c2, xv, sg = ext(BG), ext(C2), ext(XV), ext(SG)
        du = jnp.concatenate([jnp.zeros((HALO, 256), F32), dc_ref[...], dun], axis=0)
        e = c2 * xv
        shifted = [_sh(e, SC_K - 1 - k) for k in range(SC_K)]
        f = sum(sw_ref[k:k + 1, :] * shifted[k] for k in range(SC_K))
        gate = _silu(sg)
        df = du * gate * bg
        de = sum(sw_ref[k:k + 1, :] * _ash(df, SC_K - 1 - k) for k in range(SC_K))
        dbg = du * gate * f
        dsg = du * bg * f * _silu_grad(sg)
        dz_ref[...] = jnp.concatenate([dbg[mid], (de * xv)[mid], (de * c2)[mid], dsg[mid]], axis=1).astype(BF16)
        dw = jnp.concatenate([jnp.sum((df * shifted[k])[mid], axis=0, keepdims=True) for k in range(SC_K)] + [
            jnp.zeros((8 - SC_K, 256), F32)], axis=0)
        _accumulate(i, dw_ref, dw)

    return pl.pallas_call(
        body, name="shortconv_bwd", grid=(n_steps,), in_specs=[cur(ZB), prev(ZB), nxt(ZB), cur(256), nxt(256), layer((8, 256)), ANY],
        out_specs=[cur(1024, BG // 1024), full((8, 256))],
        out_shape=[jax.ShapeDtypeStruct((rows, ZB), BF16), jax.ShapeDtypeStruct((8, 256), F32)],
        input_output_aliases={6: 0}, compiler_params=_params(("arbitrary",)),
    )(z_br, z_br, z_br, dud, dud, sc_w, dz_buf)


def conformer_bwd_tail(z_br, duc, conf_w, conf_vec, dz_buf, li):
    rows = z_br.shape[0]
    t = ROW_TILE
    cur, prev, _, full, layer = _tile_specs(t, rows // HALO, li)

    def body(zc_ref, zp_ref, du_ref, cw_ref, cv_ref, _, dc_ref, dcg_ref, dv_ref):
        i = pl.program_id(0)
        zp = jnp.where(i == 0, jnp.zeros(zp_ref.shape, zp_ref.dtype), zp_ref[...])

        def ext(lo):
            return jnp.concatenate([zp[:, lo:lo + 256], zc_ref[:, lo:lo + 256]], axis=0).astype(F32)

        g1 = ext(CA) * _sigmoid(ext(CGT))
        c = _conf_conv(g1, cw_ref)[HALO:] + cv_ref[0:1, :]
        _, vjp = jax.vjp(_conf_tail, c, zc_ref[:, CG:CG + 256].astype(F32), cv_ref[1:2, :], cv_ref[2:3, :])
        dc, dcg, dlg, dlb = vjp(du_ref[...])
        dc_ref[...] = dc
        dcg_ref[...] = dcg.astype(BF16)
        dvec = jnp.concatenate([dlg, dlb, jnp.sum(dc, axis=0, keepdims=True), jnp.zeros((5, 256), F32)], axis=0)
        _accumulate(i, dv_ref, dvec)

    return pl.pallas_call(
        body, name="conformer_bwd_tail", grid=(rows // t,), in_specs=[cur(ZB), prev(ZB), cur(256), layer((32, 256)), layer((8, 256)), ANY],
        out_specs=[cur(256), cur(256, CG // 256), full((8, 256))],
        out_shape=[jax.ShapeDtypeStruct((rows, 256), F32), jax.ShapeDtypeStruct((rows, ZB), BF16), jax.ShapeDtypeStruct((8, 256), F32)],
        input_output_aliases={5: 1}, compiler_params=_params(("arbitrary",)),
    )(z_br, z_br, duc, conf_w, conf_vec, dz_buf)


def conformer_bwd_conv(z_br, dc, conf_w, dz_buf, li):
    rows = z_br.shape[0]
    t = ROW_TILE
    n_steps = rows // t
    cur, prev, nxt, full, layer = _tile_specs(t, rows // HALO, li)

    def body(zc_ref, zp_ref, dc_ref, dn_ref, cw_ref, _, dz_ref, dw_ref):
        i = pl.program_id(0)
        zp = jnp.where(i == 0, jnp.zeros(zp_ref.shape, zp_ref.dtype), zp_ref[...])
        dcn = jnp.where(i == n_steps - 1, jnp.zeros(dn_ref.shape, dn_ref.dtype), dn_ref[...])

        def ext(lo):
            return jnp.concatenate([zp[:, lo:lo + 256], zc_ref[:, lo:lo + 256]], axis=0).astype(F32)

        a, gt = ext(CA), ext(CGT)
        sg = _sigmoid(gt)
        g1 = a * sg
        dc = dc_ref[...]
        dce = jnp.concatenate([dc, dcn], axis=0)
        dg1 = jnp.zeros_like(dce)
        dws = []
        for k in range(CONF_K):
            dg1 = dg1 + cw_ref[k:k + 1, :] * _ash(dce, CONF_K - 1 - k)
            dws.append(jnp.sum(dc * _sh(g1, CONF_K - 1 - k)[HALO:], axis=0, keepdims=True))
        dg1 = dg1[:t]
        ac, sc = a[HALO:], sg[HALO:]
        dz_ref[...] = jnp.concatenate([dg1 * sc, dg1 * ac * sc * (1.0 - sc)], axis=1).astype(BF16)
        _accumulate(i, dw_ref, jnp.concatenate(dws + [jnp.zeros((32 - CONF_K, 256), F32)], axis=0))

    return pl.pallas_call(
        body, name="conformer_bwd_conv", grid=(n_steps,), in_specs=[cur(ZB), prev(ZB), cur(256), nxt(256), layer((32, 256)), ANY],
        out_specs=[cur(512, CA // 512), full((32, 256))],
        out_shape=[jax.ShapeDtypeStruct((rows, ZB), BF16), jax.ShapeDtypeStruct((32, 256), F32)],
        input_output_aliases={5: 0}, compiler_params=_params(("arbitrary",)),
    )(z_br, z_br, dc, dc, conf_w, dz_buf)


def attention_bwd_prep(dub, o_att, z_br, dz_buf):
    rows = dub.shape[0]
    t = ROW_TILE
    cur, _, _, _, _ = _tile_specs(t, rows // HALO)

    def body(du_ref, o_ref, mg_ref, _, do_ref, dmg_ref, delta_ref):
        du, o, mg = du_ref[...], o_ref[...].astype(F32), mg_ref[...].astype(F32)
        do = du * _silu(mg)
        do_ref[...] = do.astype(BF16)
        dmg_ref[...] = (du * o * _silu_grad(mg)).astype(BF16)
        prod = do * o
        lane = lax.broadcasted_iota(jnp.int32, (1, HEADS * V_DIM), 1)
        for h in range(HEADS):
            part = jnp.where((lane >= V_DIM * h) & (lane < V_DIM * (h + 1)), prod, 0.0)
            delta_ref[h] = jnp.broadcast_to(jnp.sum(part, axis=-1, keepdims=True), (t, LANES))

    return pl.pallas_call(
        body, name="attention_bwd_prep", grid=(rows // t,), in_specs=[cur(512), cur(512), cur(512, MG // 512), ANY],
        out_specs=[cur(512), cur(512, MG // 512), pl.BlockSpec((HEADS, t, LANES), lambda i: (0, i, 0))],
        out_shape=[jax.ShapeDtypeStruct((rows, 512), BF16), jax.ShapeDtypeStruct((rows, ZB), BF16),
                   jax.ShapeDtypeStruct((HEADS, rows, LANES), F32)],
        input_output_aliases={3: 1}, compiler_params=_params(("parallel",)),
    )(dub, o_att, z_br, dz_buf)


def attention_bwd(q, k, v, do, lse, delta, exchange=None):
    rows = q.shape[0]
    tq = ROW_TILE
    nq = rows // tq
    n = 0 if exchange is None else len(exchange[0])

    def body(*refs):
        if n:
            start, finish = _exchange_ops(refs[6:6 + n], refs[9 + 2 * n:9 + 3 * n], refs[9 + 3 * n:], exchange[2])
            pl.when((pl.program_id(0) == 0) & (pl.program_id(1) == 0))(start)
        compute(*refs[:6], *refs[6 + 2 * n:9 + 2 * n])
        if n:
            pl.when((pl.program_id(0) == HEADS // 2 - 1) & (pl.program_id(1) == nq - 1))(finish)

    def compute(q_ref, k_ref, v_ref, do_ref, lse_ref, dl_ref, dq_ref, dk_ref, dv_ref):
        j = pl.program_id(1)

        @pl.when(j == 0)
        def _():
            dq_ref[...] = jnp.zeros_like(dq_ref)

        def head_step(h, tile, n_tiles, dk, dv, diagonal):
            lanes = slice(HEAD_PAD * h, HEAD_PAD * (h + 1))
            hm = _head_lane_mask(h)
            kh = k_ref[:, lanes]
            vh = jnp.where(hm, v_ref[...], jnp.zeros((), BF16))
            r0, width = pl.multiple_of(tile * tq, tq), n_tiles * tq
            qi = q_ref[pl.ds(r0, width), lanes]
            doi = jnp.where(hm, do_ref[pl.ds(r0, width), :], jnp.zeros((), BF16))
            s = _dot_nt(qi, kh)
            if diagonal:
                s = jnp.where(lax.broadcasted_iota(jnp.int32, (tq, tq), 1) <= lax.broadcasted_iota(jnp.int32, (tq, tq), 0), s, -1e30)
            pr = jnp.exp(s - lse_ref[h, pl.ds(r0, width), :][:, 0:1])
            dv = dv + _dot_tn(pr.astype(BF16), doi)
            dp = _dot_nt(doi, vh)
            ds = (pr * (dp - dl_ref[h, pl.ds(r0, width), :][:, 0:1])).astype(BF16)
            dq_ref[pl.ds(r0, width), lanes] += _dot(ds, kh)
            return dk + _dot_tn(ds, qi), dv

        def step(tile, n_tiles, carry, diagonal):
            dk0, dk1, dv = carry
            dk0, dv = head_step(0, tile, n_tiles, dk0, dv, diagonal)
            dk1, dv = head_step(1, tile, n_tiles, dk1, dv, diagonal)
            return dk0, dk1, dv

        zero = jnp.zeros((tq, HEAD_PAD), F32)
        carry = step(j, 1, (zero, zero, jnp.zeros((tq, 2 * V_DIM), F32)), True)
        odd = (nq - 1 - j) % 2
        carry = lax.cond(odd == 1, lambda cr: step(j + 1, 1, cr, False), lambda cr: cr, carry)
        dk0, dk1, dv = lax.fori_loop(0, (nq - 1 - j) // 2, lambda t, cr: step(j + 1 + odd + 2 * t, 2, cr, False), carry)
        dk_ref[:, 0:HEAD_PAD] = dk0
        dk_ref[:, HEAD_PAD:2 * HEAD_PAD] = dk1
        dv_ref[...] = dv

    srcs, dsts = ([], []) if exchange is None else (list(exchange[0]), list(exchange[1]))
    outs = pl.pallas_call(
        body, name="attention_bwd" if exchange is None else "attention_bwd_exchange", grid=(HEADS // 2, nq),
        in_specs=[pl.BlockSpec((rows, 2 * HEAD_PAD), lambda p, j: (0, p)), pl.BlockSpec((tq, 2 * HEAD_PAD), lambda p, j: (j, p)),
                  pl.BlockSpec((tq, 2 * V_DIM), lambda p, j: (j, p)), pl.BlockSpec((rows, 2 * V_DIM), lambda p, j: (0, p)),
                  pl.BlockSpec((2, rows, LANES), lambda p, j: (p, 0, 0)), pl.BlockSpec((2, rows, LANES), lambda p, j: (p, 0, 0))] + [
                      ANY] * (2 * n),
        out_specs=[pl.BlockSpec((rows, 2 * HEAD_PAD), lambda p, j: (0, p)), pl.BlockSpec((tq, 2 * HEAD_PAD), lambda p, j: (j, p)),
                   pl.BlockSpec((tq, 2 * V_DIM), lambda p, j: (j, p))] + [ANY] * n,
        out_shape=[jax.ShapeDtypeStruct((rows, HEADS * HEAD_PAD), F32), jax.ShapeDtypeStruct((rows, HEADS * HEAD_PAD), F32),
                   jax.ShapeDtypeStruct((rows, HEADS * V_DIM), F32)] + [jax.ShapeDtypeStruct(d.shape, d.dtype) for d in dsts],
        input_output_aliases={6 + n + a: 3 + a for a in range(n)}, scratch_shapes=EXCHANGE_SEMS(n) if n else [],
        compiler_params=_params(("arbitrary", "arbitrary") if n else ("parallel", "arbitrary")),
    )(q, k, v, do, lse, delta, *srcs, *dsts)
    return outs[0], outs[1], outs[2], list(outs[3:])


def mla_prep_bwd(dq, dk, dv, z_br, rope, gq, gkv, misc, dz_buf, li):
    rows = dq.shape[0]
    t = ROW_TILE
    cur, _, _, full, layer = _tile_specs(t, rows // HALO, li)
    w8 = HEADS * HEAD_PAD
    uq, keys, values = slice(0, 256), slice(M_UKVK - M_UQ, M_UKVV - M_UQ), slice(M_UKVV - M_UQ, M_WO - M_UQ)

    def body(dq_ref, dk_ref, dv_ref, z_ref, rope_ref, gq_ref, gkv_ref, up_ref, _, dz_ref, dup_ref, dgq_ref, dgkv_ref):
        i = pl.program_id(0)
        cth, s1, s2 = rope_ref[:, 0:128], rope_ref[:, 128:256], rope_ref[:, 256:384]
        dqb = _rope_transposed(dq_ref[...] * Q_SCALE, _lanes8(cth), _lanes8(s1), _lanes8(s2), w8).astype(BF16)
        dq_chunks = _chunks(dqb)
        cq = z_ref[:, 0:256].astype(F32)
        qn, vjp_q = jax.vjp(_rms, cq, gq_ref[...])
        dcq, dgq = vjp_q(sum(_dot_nt(dqk, up_ref[k, uq, :]) for k, dqk in enumerate(dq_chunks)))
        _accumulate(i, dgq_ref, dgq)

        dk = dk_ref[...]
        dkr = sum(dk[:, HEAD_PAD * h:HEAD_PAD * (h + 1)] for h in range(HEADS))
        dkr = _rope_transposed(dkr, cth, s1, s2, HEAD_PAD)
        lane = lax.broadcasted_iota(jnp.int32, (1, HEAD_PAD), 1)
        dkr = jnp.where((lane >= QK_NOPE) & (lane < QK_NOPE + QK_ROPE), dkr, 0.0)
        dkb, dvb = dk.astype(BF16), dv_ref[...].astype(BF16)
        dk_chunks, dv_chunks = _chunks(dkb), _chunks(dvb, width=2 * V_DIM)
        ckv = z_ref[:, 256:384].astype(F32)
        kvn, vjp_kv = jax.vjp(_rms, ckv, gkv_ref[...])
        dckv, dgkv = vjp_kv(sum(_dot_nt(dk_chunks[k], up_ref[k, keys, :]) + _dot_nt(dv_chunks[k], up_ref[k, values, 0:2 * V_DIM])
                                for k in range(N_CHIPS)))
        _accumulate(i, dgkv_ref, dgkv)
        dz_ref[...] = jnp.concatenate([dcq, dckv, dkr], axis=1).astype(BF16)
        qnb, kvnb = qn.astype(BF16), kvn.astype(BF16)
        d_uq, d_keys, d_values = _chunks(_dot_tn(qnb, dqb)), _chunks(_dot_tn(kvnb, dkb)), _chunks(_dot_tn(kvnb, dvb), width=2 * V_DIM)
        for k in range(N_CHIPS):
            padded = jnp.concatenate([d_values[k], jnp.zeros((128, 256 - 2 * V_DIM), F32)], axis=1)
            _accumulate(i, dup_ref.at[k], jnp.concatenate([d_uq[k], d_keys[k], padded], axis=0))

    return pl.pallas_call(
        body, name="mla_prep_bwd", grid=(rows // t,),
        in_specs=[cur(w8), cur(w8), cur(512), cur(512, CQ // 512), cur(384), layer((1, 256)), layer((1, 128)),
                  _misc_spec(M_UQ, M_WO - M_UQ), ANY],
        out_specs=[cur(512, CQ // 512), full((N_CHIPS, M_WO - M_UQ, 256)), full((1, 256)), full((1, 128))],
        out_shape=[jax.ShapeDtypeStruct((rows, ZB), BF16), jax.ShapeDtypeStruct((N_CHIPS, M_WO - M_UQ, 256), F32),
                   jax.ShapeDtypeStruct((1, 256), F32), jax.ShapeDtypeStruct((1, 128), F32)],
        input_output_aliases={8: 0}, compiler_params=_params(("arbitrary",)),
    )(dq, dk, dv, z_br, rope, gq, gkv, misc, dz_buf)


def prenorm_bwd(dz_br, w_br, dh_gl, hres, gpre, dh_next, li):
    rows, d = hres.shape
    t = ROW_TILE
    cur, _, _, full, layer = _tile_specs(t, rows // HALO, li)

    def body(dz_ref, w_ref, dp_ref, x_ref, g_ref, dn_ref, dx_ref, dg_ref):
        i = pl.program_id(0)
        dh = _dot_nt(dz_ref[...], w_ref[...]) + dp_ref[...]
        _, vjp = jax.vjp(_rms, x_ref[...], g_ref[...])
        dx, dg = vjp(dh)
        dx_ref[...] = dx + dn_ref[...]
        _accumulate(i, dg_ref, dg)

    return pl.pallas_call(
        body, name="prenorm_bwd", grid=(rows // t,), in_specs=[cur(ZB), layer((d, ZB), 0), cur(d), cur(d), layer((1, d)), cur(d)],
        out_specs=[cur(d), full((1, d))], out_shape=[jax.ShapeDtypeStruct((rows, d), F32), jax.ShapeDtypeStruct((1, d), F32)],
        compiler_params=_params(("arbitrary",)),
    )(dz_br, w_br, dh_gl, hres, gpre, dh_next)


def _mesh_position():
    return lax.axis_index("x"), lax.axis_index("y"), lax.axis_index("c")


def chip_exchange(src, gather, name):
    block = src.shape if gather else src.shape[1:]

    def body(src_ref, dst_ref, send_sems, recv_sems, local_sem):
        x, y, c = _mesh_position()
        me = 2 * x + y
        peers = ((1 - x, y), (x, 1 - y), (1 - x, 1 - y))

        def part(k):
            return src_ref if gather else src_ref.at[k]

        def copy(j, slot):
            px, py = peers[j]
            return pltpu.make_async_remote_copy(src_ref=part(2 * px + py), dst_ref=dst_ref.at[slot], send_sem=send_sems.at[j],
                                                recv_sem=recv_sems.at[j], device_id=(px, py, c), device_id_type=MESH)

        local = pltpu.make_async_copy(part(me), dst_ref.at[me], local_sem)
        local.start()
        sends = [copy(j, me) for j in range(3)]
        for cp in sends:
            cp.start()
        for j, (px, py) in enumerate(peers):
            copy(j, 2 * px + py).wait_recv()
        for cp in sends:
            cp.wait_send()
        local.wait()

    return pl.pallas_call(
        body, name=name, in_specs=[pl.BlockSpec(memory_space=pl.ANY)], out_specs=pl.BlockSpec(memory_space=pl.ANY),
        out_shape=jax.ShapeDtypeStruct((N_CHIPS,) + tuple(block), src.dtype),
        scratch_shapes=[pltpu.SemaphoreType.DMA((3,)), pltpu.SemaphoreType.DMA((3,)), pltpu.SemaphoreType.DMA(())],
    )(src)


def sibling_swap(src, name):
    def body(src_ref, dst_ref, send_sem, recv_sem):
        x, y, c = _mesh_position()
        cp = pltpu.make_async_remote_copy(src_ref=src_ref, dst_ref=dst_ref, send_sem=send_sem, recv_sem=recv_sem,
                                          device_id=(x, y, 1 - c), device_id_type=MESH)
        cp.start()
        cp.wait()

    return pl.pallas_call(
        body, name=name, in_specs=[pl.BlockSpec(memory_space=pl.ANY)], out_specs=pl.BlockSpec(memory_space=pl.ANY),
        out_shape=jax.ShapeDtypeStruct(src.shape, src.dtype),
        scratch_shapes=[pltpu.SemaphoreType.DMA(()), pltpu.SemaphoreType.DMA(())],
    )(src)


def _comm_call(body, name, n_in, out_shapes, n_sems):
    return pl.pallas_call(
        body, name=name, in_specs=[ANY] * n_in, out_specs=[ANY] * len(out_shapes), out_shape=out_shapes,
        scratch_shapes=[pltpu.SemaphoreType.DMA((n,)) for n in n_sems])


def _row_halves(c, rows):
    half = rows // 2
    return pl.ds(pl.multiple_of(c * half, 16), half), pl.ds(pl.multiple_of((1 - c) * half, 16), half)


def _peers():
    x, y, c = _mesh_position()
    return x, y, c, 2 * x + y, ((1 - x, y), (x, 1 - y), (1 - x, 1 - y))


def _gather_ops(src, dst, sems, layer, own_copy):
    ici_send, ici_recv, d2d_send, d2d_recv, own_sems = sems
    n = len(src)

    def fetch(a, j, slot):
        x, y, c, _, peers = _peers()
        px, py = peers[j]
        mine, _ = _row_halves(c, src[a].shape[1])
        return pltpu.make_async_remote_copy(src_ref=src[a].at[layer, mine], dst_ref=dst[a].at[layer, slot, mine], send_sem=ici_send.at[3 * a + j],
                                            recv_sem=ici_recv.at[3 * a + j], device_id=(px, py, c), device_id_type=MESH)

    def forward(a, j, sibling_half):
        x, y, c, _, peers = _peers()
        px, py = peers[j]
        part = dst[a].at[layer, 2 * px + py, _row_halves(c, src[a].shape[1])[1 if sibling_half else 0]]
        return pltpu.make_async_remote_copy(src_ref=part, dst_ref=part, send_sem=d2d_send.at[3 * a + j], recv_sem=d2d_recv.at[3 * a + j],
                                            device_id=(x, y, 1 - c), device_id_type=MESH)

    def own(a):
        return pltpu.make_async_copy(src[a].at[layer], dst[a].at[layer, _peers()[3]], own_sems.at[a])

    def start():
        me = _peers()[3]
        for a in range(n):
            if own_copy:
                own(a).start()
            for j in range(3):
                fetch(a, j, me).start()

    def finish():
        peers = _peers()[4]
        for j, (px, py) in enumerate(peers):
            for a in range(n):
                fetch(a, j, 2 * px + py).wait_recv()
                forward(a, j, False).start()
        for j in range(3):
            for a in range(n):
                forward(a, j, True).wait_recv()
        for j in range(3):
            for a in range(n):
                fetch(a, j, 0).wait_send()
                forward(a, j, False).wait_send()
        if own_copy:
            for a in range(n):
                own(a).wait()

    return start, finish


def _exchange_ops(src, dst, sems, layer):
    send_sems, recv_sems, own_sems = sems
    n = len(src)

    def copy(a, j, slot):
        x, y, c, _, peers = _peers()
        px, py = peers[j]
        return pltpu.make_async_remote_copy(src_ref=src[a].at[2 * px + py], dst_ref=dst[a].at[layer, slot], send_sem=send_sems.at[3 * a + j],
                                            recv_sem=recv_sems.at[3 * a + j], device_id=(px, py, c), device_id_type=MESH)

    def own(a):
        me = _peers()[3]
        return pltpu.make_async_copy(src[a].at[me], dst[a].at[layer, me], own_sems.at[a])

    def start():
        me = _peers()[3]
        for a in range(n):
            own(a).start()
            for j in range(3):
                copy(a, j, me).start()

    def finish():
        peers = _peers()[4]
        for j, (px, py) in enumerate(peers):
            for a in range(n):
                copy(a, j, 2 * px + py).wait_recv()
        for j in range(3):
            for a in range(n):
                copy(a, j, 0).wait_send()
        for a in range(n):
            own(a).wait()

    return start, finish


GATHER_SEMS = lambda n: [pltpu.SemaphoreType.DMA((3 * n,))] * 4 + [pltpu.SemaphoreType.DMA((n,))]
EXCHANGE_SEMS = lambda n: [pltpu.SemaphoreType.DMA((3 * n,))] * 2 + [pltpu.SemaphoreType.DMA((n,))]


def gather_layer(srcs, dsts, layer, name):
    n = len(srcs)

    def body(*refs):
        start, finish = _gather_ops(refs[:n], refs[2 * n:3 * n], refs[3 * n:], layer, False)
        start()
        finish()

    return pl.pallas_call(
        body, name=name, in_specs=[ANY] * (2 * n), out_specs=[ANY] * n, out_shape=[jax.ShapeDtypeStruct(d.shape, d.dtype) for d in dsts],
        input_output_aliases={n + a: a for a in range(n)}, scratch_shapes=GATHER_SEMS(n),
    )(*srcs, *dsts)


def exchange_layer(ss, dsts, layer, name):
    n = len(ss)

    def body(*refs):
        start, finish = _exchange_ops(refs[:n], refs[2 * n:3 * n], refs[3 * n:], layer)
        start()
        finish()

    return pl.pallas_call(
        body, name=name, in_specs=[ANY] * (2 * n), out_specs=[ANY] * n, out_shape=[jax.ShapeDtypeStruct(d.shape, d.dtype) for d in dsts],
        input_output_aliases={n + a: a for a in range(n)}, scratch_shapes=EXCHANGE_SEMS(n),
    )(*ss, *dsts)


def swap_row_halves(ps, name):
    n = len(ps)

    def body(*refs):
        src, dst = refs[:n], refs[n:2 * n]
        send_sems, recv_sems = refs[2 * n:]
        x, y, c = _mesh_position()
        copies = [pltpu.make_async_remote_copy(src_ref=src[a].at[:, _row_halves(c, src[a].shape[1])[1]], dst_ref=dst[a], send_sem=send_sems.at[a],
                                               recv_sem=recv_sems.at[a], device_id=(x, y, 1 - c), device_id_type=MESH) for a in range(n)]
        for cp in copies:
            cp.start()
        for cp in copies:
            cp.wait()

    outs = [jax.ShapeDtypeStruct((p.shape[0], p.shape[1] // 2, p.shape[2]), p.dtype) for p in ps]
    return _comm_call(body, name, n, outs, (n, n))(*ps)


def add_row_half(p, r, c, name):
    n, half, cols = r.shape
    rb = _row_block(half, cols, 2)
    steps = half // rb

    def body(c_ref, p_ref, r_ref, o_ref):
        o_ref[...] = (p_ref[...].astype(F32) + r_ref[...].astype(F32)).astype(BF16)

    return pl.pallas_call(
        body, name=name, out_shape=jax.ShapeDtypeStruct(r.shape, BF16),
        grid_spec=pltpu.PrefetchScalarGridSpec(
            num_scalar_prefetch=1, grid=(n, steps),
            in_specs=[pl.BlockSpec((1, rb, cols), lambda k, i, c_ref: (k, c_ref[0] * steps + i, 0)),
                      pl.BlockSpec((1, rb, cols), lambda k, i, c_ref: (k, i, 0))],
            out_specs=pl.BlockSpec((1, rb, cols), lambda k, i, c_ref: (k, i, 0))),
        compiler_params=_params(("parallel", "parallel")),
    )(jnp.reshape(c, (1,)).astype(jnp.int32), p, r)


def sum_row_halves(l, c, name):
    layers, n, half, cols = l.shape
    rb = _row_block(half, cols, 4)
    steps = half // rb

    def body(c_ref, l_ref, o_ref):
        acc = l_ref[0, 0].astype(F32)
        for s in range(1, n):
            acc = acc + l_ref[0, s].astype(F32)
        o_ref[0] = acc

    return pl.pallas_call(
        body, name=name, out_shape=jax.ShapeDtypeStruct((layers, 2 * half, cols), F32),
        grid_spec=pltpu.PrefetchScalarGridSpec(
            num_scalar_prefetch=1, grid=(layers, steps), in_specs=[pl.BlockSpec((1, n, rb, cols), lambda a, i, c_ref: (a, 0, i, 0))],
            out_specs=pl.BlockSpec((1, rb, cols), lambda a, i, c_ref: (a, c_ref[0] * steps + i, 0))),
        compiler_params=_params(("parallel", "parallel")),
    )(jnp.reshape(c, (1,)).astype(jnp.int32), l)


def share_row_halves(gs, name):
    n = len(gs)

    def body(*refs):
        dst = refs[n:2 * n]
        send_sems, recv_sems = refs[2 * n:]
        x, y, c = _mesh_position()

        def copy(a, sibling_half):
            part = dst[a].at[:, _row_halves(c, dst[a].shape[1])[1 if sibling_half else 0]]
            return pltpu.make_async_remote_copy(src_ref=part, dst_ref=part, send_sem=send_sems.at[a], recv_sem=recv_sems.at[a],
                                                device_id=(x, y, 1 - c), device_id_type=MESH)

        for a in range(n):
            copy(a, False).start()
        for a in range(n):
            copy(a, True).wait_recv()
        for a in range(n):
            copy(a, False).wait_send()

    return pl.pallas_call(
        body, name=name, in_specs=[ANY] * n, out_specs=[ANY] * n, out_shape=[jax.ShapeDtypeStruct(g.shape, g.dtype) for g in gs],
        input_output_aliases={a: a for a in range(n)}, scratch_shapes=[pltpu.SemaphoreType.DMA((n,)), pltpu.SemaphoreType.DMA((n,))],
    )(*gs)


def _row_block(rows, cols, itemsize):
    best = 16
    for rb in range(16, rows + 1, 16):
        if rows % rb == 0 and rb * cols * itemsize <= 2 * 1024 * 1024:
            best = rb
    assert rows % best == 0, (rows, cols)
    return best


def _comm_block(rows):
    return 1024 if rows % 1024 == 0 else rows


def sum_slots(buf, name):
    n, r, c = buf.shape
    rb = _comm_block(r)

    def body(b_ref, o_ref):
        acc = b_ref[0].astype(F32)
        for s in range(1, n):
            acc = acc + b_ref[s].astype(F32)
        o_ref[...] = acc

    return pl.pallas_call(
        body, name=name, grid=(r // rb,), in_specs=[pl.BlockSpec((n, rb, c), lambda i: (0, i, 0))],
        out_specs=pl.BlockSpec((rb, c), lambda i: (i, 0)), out_shape=jax.ShapeDtypeStruct((r, c), F32),
        compiler_params=_params(("parallel",)),
    )(buf)


def add_pair(a, b, out_dtype, name):
    shape = a.shape
    a2, b2 = a.reshape(-1, shape[-1]), b.reshape(-1, shape[-1])
    r, c = a2.shape
    rb = _comm_block(r)

    def body(a_ref, b_ref, o_ref):
        o_ref[...] = (a_ref[...].astype(F32) + b_ref[...].astype(F32)).astype(out_dtype)

    out = pl.pallas_call(
        body, name=name, grid=(r // rb,), in_specs=[pl.BlockSpec((rb, c), lambda i: (i, 0))] * 2,
        out_specs=pl.BlockSpec((rb, c), lambda i: (i, 0)), out_shape=jax.ShapeDtypeStruct((r, c), out_dtype),
        compiler_params=_params(("parallel",)),
    )(a2, b2)
    return out.reshape(shape)


def adamw(w, g, m, v):
    shape = w.shape
    cols = shape[-1]
    rows = math.prod(shape[:-1])
    if rows * cols <= 256 * 1024:
        rb, cb = rows, cols
    else:
        rb = max(r for r in range(8, 2049, 8) if rows % r == 0)
        cb = cols if rb * cols * 4 <= 2 * 1024 * 1024 else 256
    assert rows % rb == 0 and cols % cb == 0, shape

    def body(w_ref, g_ref, m_ref, v_ref, d_ref, nm_ref, nv_ref):
        g_ = g_ref[...]
        nm = ADAM_B1 * m_ref[...] + (1.0 - ADAM_B1) * g_
        nv = ADAM_B2 * v_ref[...] + (1.0 - ADAM_B2) * (g_ * g_)
        m_hat = nm / (1.0 - ADAM_B1 ** ADAM_STEP)
        v_hat = nv / (1.0 - ADAM_B2 ** ADAM_STEP)
        d_ref[...] = -ADAM_LR * (m_hat / (jnp.sqrt(v_hat) + ADAM_EPS) + ADAM_WD * w_ref[...])
        nm_ref[...] = nm
        nv_ref[...] = nv

    spec = pl.BlockSpec((rb, cb), lambda i, j: (i, j))
    outs = pl.pallas_call(
        body, name="adamw", grid=(rows // rb, cols // cb), in_specs=[spec] * 4, out_specs=[spec] * 3,
        out_shape=[jax.ShapeDtypeStruct((rows, cols), F32)] * 3, compiler_params=_params(("parallel", "parallel")),
    )(*(a.reshape(rows, cols) for a in (w, g, m, v)))
    return tuple(o.reshape(shape) for o in outs)


def _pack(arrays, dtype, row_multiple):
    flat = jnp.concatenate([a.astype(dtype).reshape(-1) for a in arrays])
    per = LANES * row_multiple
    total = -(-flat.shape[0] // per) * per
    return jnp.pad(flat, (0, total - flat.shape[0])).reshape(total // LANES, LANES)


def _unpack(buf, shapes):
    flat = buf.reshape(-1)
    out, off = [], 0
    for s in shapes:
        n = math.prod(s)
        out.append(flat[off:off + n].reshape(s))
        off += n
    return out


def _input_weights(blocks):
    c0, c1, c2, c3 = (blocks[..., k, :, :] for k in range(N_CHIPS))
    pad = lambda n: jnp.zeros(c0.shape[:-1] + (n,), blocks.dtype)
    w_br = jnp.concatenate([c1[..., 376:1400], c0[..., 0:896], pad(64), c0[..., 896:928], pad(32), c0[..., 928:], c1[..., 0:376]], axis=-1)
    return w_br, jnp.concatenate([c1[..., 1400:], c2, c3], axis=-1)


def _input_weights_inverse(dw_br, dw_gl):
    c0 = jnp.concatenate([dw_br[..., 1024:1920], dw_br[..., 1984:2016], dw_br[..., 2048:2952]], axis=-1)
    c1 = jnp.concatenate([dw_br[..., 2952:ZB], dw_br[..., 0:1024], dw_gl[..., 0:432]], axis=-1)
    return jnp.stack([c0, c1, dw_gl[..., 432:2264], dw_gl[..., 2264:]], axis=-3)


def _block_diag(pw):
    zeros = lambda n: jnp.zeros(pw.shape[:-3] + (64, n), pw.dtype)
    rows = [jnp.concatenate([zeros(64 * g), pw[..., g, :, :], zeros(64 * (3 - g))], axis=-1) for g in range(4)]
    return jnp.concatenate(rows, axis=-2)


def _block_diag_inverse(d):
    return jnp.stack([d[..., 64 * g:64 * (g + 1), 64 * g:64 * (g + 1)] for g in range(4)], axis=-3)


def _pad_rows(a, n):
    return jnp.pad(a, ((0, n - a.shape[0]), (0, 0)))


def _rope_tables(rows):
    inv = 1.0 / (ROPE_THETA ** (jnp.arange(0, QK_ROPE, 2, dtype=F32) / QK_ROPE))
    ang = jnp.arange(rows, dtype=F32)[:, None] * inv[None, :]
    cos, sin = jnp.cos(ang), jnp.sin(ang)
    one, zero = jnp.ones((rows, 1), F32), jnp.zeros((rows, 1), F32)
    rep = lambda a, n: jnp.broadcast_to(a, (rows, n))
    c = jnp.concatenate([rep(one, 64), cos, cos, rep(one, 32)], axis=1)
    s1 = jnp.concatenate([rep(zero, 64), -sin, rep(zero, 48)], axis=1)
    s2 = jnp.concatenate([rep(zero, 80), sin, rep(zero, 32)], axis=1)
    return jnp.concatenate([c, s1, s2], axis=1)


def _misc_block(parts):
    lead = parts["w_uq"].shape[:-2]
    pad_last = lambda a, n: jnp.pad(a, [(0, 0)] * (a.ndim - 1) + [(0, n - a.shape[-1])])
    uq = pad_last(parts["w_uq"].reshape(lead + (256, 2, QK_NOPE + QK_ROPE)), HEAD_PAD).reshape(lead + (256, 256))
    kv = parts["w_ukv"].reshape(lead + (128, 2, QK_NOPE + V_DIM))
    keys = pad_last(kv[..., :QK_NOPE], HEAD_PAD).reshape(lead + (128, 256))
    values = pad_last(kv[..., QK_NOPE:].reshape(lead + (128, 2 * V_DIM)), 256)
    wo = jnp.swapaxes(parts["w_o"].reshape(lead + (256, N_CHIPS, 256)), -3, -2).reshape(lead + (D_MODEL, 256))
    gap = jnp.zeros(lead + (M_UQ - M_SC - 256, 256), uq.dtype)
    return jnp.concatenate([parts["w_out_mla"], parts["w_out_pool"], parts["w_out_conf"], parts["w_out_sc"], gap, uq, keys, values, wo],
                           axis=-2)


def _misc_unblock(block):
    lead = block.shape[:-2]
    rows = lambda lo, n: block[..., lo:lo + n, :]
    uq = rows(M_UQ, 256).reshape(lead + (256, 2, HEAD_PAD))[..., :QK_NOPE + QK_ROPE].reshape(lead + (256, 2 * (QK_NOPE + QK_ROPE)))
    keys = rows(M_UKVK, 128).reshape(lead + (128, 2, HEAD_PAD))[..., :QK_NOPE]
    values = rows(M_UKVV, 128)[..., :2 * V_DIM].reshape(lead + (128, 2, V_DIM))
    wo = jnp.swapaxes(rows(M_WO, D_MODEL).reshape(lead + (N_CHIPS, 256, 256)), -3, -2).reshape(lead + (256, D_MODEL))
    return dict(w_out_mla=rows(M_MLA, 512), w_out_pool=rows(M_POOL, 256), w_out_conf=rows(M_CONF, 256), w_out_sc=rows(M_SC, 256), w_uq=uq,
                w_ukv=jnp.concatenate([keys, values], axis=-1).reshape(lead + (128, 256)), w_o=wo)


def _to_chip_blocks(name, a):
    if name == "w_o":
        return a.reshape(a.shape[:-2] + (N_CHIPS, a.shape[-2] // N_CHIPS, a.shape[-1]))
    return jnp.swapaxes(a.reshape(a.shape[:-1] + (N_CHIPS, a.shape[-1] // N_CHIPS)), -3, -2)


def _from_chip_blocks(name, b):
    if name == "w_o":
        return b.reshape(b.shape[:-3] + (N_CHIPS * b.shape[-2], b.shape[-1]))
    s = jnp.swapaxes(b, -3, -2)
    return s.reshape(s.shape[:-2] + (N_CHIPS * s.shape[-1],))


LARGE = ("w_in",) + MISC


def gather_small(shards):
    small = chip_exchange(_pack([shards[n] for n, _, _ in SHARDED_SMALL], F32, 8), True, "gather_small_ici")
    per_chip = [_unpack(small[k], [s for _, s, _ in SHARDED_SMALL]) for k in range(N_CHIPS)]
    return {name: jnp.concatenate([per_chip[k][idx] for k in range(N_CHIPS)], axis=axis) for idx, (name, _, axis) in enumerate(SHARDED_SMALL)}


class LocalWeights:
    def __init__(self, full):
        self.w_in = _to_chip_blocks("w_in", full["w_in"])
        self.misc = _misc_block({n: _to_chip_blocks(n, full[n]) for n in MISC})
        self.grads = [None] * DEPTH

    def layer(self, i):
        return self.w_in[i], self.misc[i]

    def gather_with_attention(self, i):
        return None

    def gathered(self, dsts):
        pass

    def exchange_with_attention(self):
        return None

    def exchanged(self, dsts):
        pass

    def put_grads(self, i, w_in, misc):
        self.grads[i] = (w_in, misc)

    def reduced(self):
        out = {n: _from_chip_blocks(n, b) for n, b in _misc_unblock(jnp.stack([m for _, m in self.grads])).items()}
        out["w_in"] = _from_chip_blocks("w_in", jnp.stack([w for w, _ in self.grads]))
        return out


class MeshWeights:
    def __init__(self, shards, c, chip):
        self.c, self.chip = c, chip
        self.srcs = [shards["w_in"].astype(BF16), _misc_block({n: shards[n] for n in MISC}).astype(BF16)]
        dsts = [lax.empty((DEPTH, N_CHIPS) + s.shape[1:], BF16) for s in self.srcs]
        self.dsts = gather_layer(self.srcs, dsts, 0, "gather_layer")
        self.landed = [lax.empty((DEPTH, N_CHIPS, s.shape[1] // 2, s.shape[2]), BF16) for s in self.srcs]
        self.pending = None

    def layer(self, i):
        blocks = [d[i] for d in self.dsts]
        if i == 0:
            own = (jnp.arange(N_CHIPS) == self.chip)[:, None, None]
            blocks = [jnp.where(own, s[0][None], b) for s, b in zip(self.srcs, blocks)]
        return blocks[0], blocks[1]

    def gather_with_attention(self, i):
        return (self.srcs, self.dsts, i + 1) if i + 1 < DEPTH else None

    def gathered(self, dsts):
        if dsts:
            self.dsts = dsts

    def exchange_with_attention(self):
        return None if self.pending is None else (self.pending[0], self.landed, self.pending[1])

    def exchanged(self, dsts):
        if dsts:
            self.landed, self.pending = dsts, None

    def put_grads(self, i, w_in, misc):
        ps = [w_in.astype(BF16), misc.astype(BF16)]
        rs = swap_row_halves(ps, "reduce_swap")
        self.pending = ([add_row_half(p, r, self.c, "reduce_pair_%d" % a) for a, (p, r) in enumerate(zip(ps, rs))], i)

    def reduced(self):
        landed = exchange_layer(self.pending[0], self.landed, self.pending[1], "reduce_exchange")
        gs = [sum_row_halves(l, self.c, "reduce_sum_%d" % a) for a, l in enumerate(landed)]
        g_in, g_misc = share_row_halves(gs, "reduce_share")
        out = {"w_in": g_in}
        out.update(_misc_unblock(g_misc))
        return out


def reduce_small(grads, chip):
    names = [n for n, _ in REPLICATED] + [n for n, _, _ in SHARDED_SMALL]
    buf = _pack([grads[n] for n in names], F32, 8)
    chip_sum = add_pair(buf, sibling_swap(buf, "reduce_small_d2d"), F32, "reduce_small_pair")
    total = sum_slots(chip_exchange(chip_sum, True, "reduce_small_ici"), "reduce_small_sum")
    out = dict(zip(names, _unpack(total, [grads[n].shape for n in names])))
    for name, shape, axis in SHARDED_SMALL:
        out[name] = lax.dynamic_slice_in_dim(out[name], chip * shape[axis], shape[axis], axis)
    return out


def _prepare_small(w):
    row = lambda a: a[:, None, :]
    conf_vec = jnp.concatenate([row(w["conf_dw_b"]), row(w["conf_ln_g"]), row(w["conf_ln_b"]), jnp.zeros((DEPTH, 5, 256), F32)], axis=1)
    return dict(
        gpre=row(w["pre_norm_g"]), bias=row(w["gate_bias"]), pwbd=_block_diag(w["pool_w"]).astype(BF16), pscale=row(w["pool_scale"]),
        gq=row(w["q_norm_g"]), gkv=row(w["kv_norm_g"]), conf_w=jnp.pad(w["conf_dw_w"].astype(F32), ((0, 0), (0, 32 - CONF_K), (0, 0))),
        conf_vec=conf_vec, sc_w=jnp.pad(w["sc_dw_w"].astype(F32), ((0, 0), (0, 8 - SC_K), (0, 0))), gpost=row(w["post_norm_g"]))


def _prepare_layer(w_in_blocks, misc_blocks):
    w_br, w_gl = _input_weights(w_in_blocks)
    one = lambda a: a.astype(BF16)[None]
    return dict(w_br=one(w_br), w_gl=one(w_gl), misc=one(misc_blocks))


def local_step(x, target, w, large):
    seq = x.shape[0]
    length = N_META + seq
    rows = -(-length // ROW_TILE) * ROW_TILE
    bt = _big_tile(rows)
    hres = _pad_rows(jnp.concatenate([w["meta_tokens"].astype(F32), x], axis=0), rows)
    tgt = jnp.pad(target, ((N_META, rows - length), (0, 0)))
    rope = _rope_tables(rows)
    sw = _prepare_small(w)

    saved = []
    for i in range(DEPTH):
        lw = _prepare_layer(*large.layer(i))
        z_br, hb = prenorm_project(hres, sw["gpre"], lw["w_br"], i)
        z_gl = matmul(hb, lw["w_gl"], "nn", BF16, bt, 1024, D_MODEL, "project_gates", b_layer=0)
        ua, uc, ud, q, k, v = branches_fwd(z_br, rope, sw["pwbd"], sw["pscale"], sw["gq"], sw["gkv"], lw["misc"], sw["conf_w"],
                                           sw["conf_vec"], sw["sc_w"], i)
        o_att, lse, dsts = attention_fwd(q, k, v, large.gather_with_attention(i))
        large.gathered(dsts)
        ub, mb, o, hnew = merge_fwd(ua, o_att, uc, ud, z_br, z_gl, sw["bias"], lw["misc"], sw["gpost"], hres, i)
        saved.append(dict(lw=lw, hres=hres, hb=hb, z_br=z_br, z_gl=z_gl, ua=ua, ub=ub, uc=uc, ud=ud, q=q, k=k, v=v, o_att=o_att,
                          lse=lse, mb=mb, o=o))
        hres = hnew

    dh, total = loss_head(hres, tgt, seq)

    g = {n: [None] * DEPTH for n in ("gpre", "bias", "pwbd", "pscale", "gq", "gkv", "conf_w", "conf_vec", "sc_w", "gpost")}
    for i in reversed(range(DEPTH)):
        s = saved[i]
        lw = s["lw"]
        dm, dwo, g["gpost"][i] = postnorm_bwd(dh, s["o"], s["mb"], lw["misc"], sw["gpost"], i)
        dua, dub, duc, dud, dz_gl, dwout, g["bias"][i] = merge_bwd(dm, s["ua"], s["ub"], s["uc"], s["ud"], s["z_gl"], sw["bias"],
                                                                 lw["misc"], i)
        dz_br = lax.empty((rows, ZB), BF16)
        dz_br, g["pwbd"][i], g["pscale"][i] = pool_bwd(s["z_br"], dua, sw["pwbd"], sw["pscale"], dz_br, i)
        dz_br, g["sc_w"][i] = shortconv_bwd(s["z_br"], dud, sw["sc_w"], dz_br, i)
        dc, dz_br, g["conf_vec"][i] = conformer_bwd_tail(s["z_br"], duc, sw["conf_w"], sw["conf_vec"], dz_br, i)
        dz_br, g["conf_w"][i] = conformer_bwd_conv(s["z_br"], dc, sw["conf_w"], dz_br, i)
        do, dz_br, delta = attention_bwd_prep(dub, s["o_att"], s["z_br"], dz_br)
        dq, dk, dv, dsts = attention_bwd(s["q"], s["k"], s["v"], do, s["lse"], delta, large.exchange_with_attention())
        large.exchanged(dsts)
        dz_br, dwup, g["gq"][i], g["gkv"][i] = mla_prep_bwd(dq, dk, dv, s["z_br"], rope, sw["gq"], sw["gkv"], lw["misc"], dz_br, i)
        dw_br = matmul(s["hb"], dz_br, "tn", F32, D_MODEL, ZB // 2, bt, "grad_w_branch")
        dw_gl = matmul(s["hb"], dz_gl, "tn", F32, D_MODEL, 1024, bt, "grad_w_gates")
        dh_gl = matmul(dz_gl, lw["w_gl"], "nt", F32, bt, D_MODEL, 1024, "grad_h_gates", b_layer=0)
        dh, g["gpre"][i] = prenorm_bwd(dz_br, lw["w_br"], dh_gl, s["hres"], sw["gpre"], dh, i)
        gap = jnp.zeros((N_CHIPS, M_UQ - M_SC - 256, 256), F32)
        large.put_grads(i, _input_weights_inverse(dw_br, dw_gl), jnp.concatenate([dwout, gap, dwup, dwo], axis=1))

    g = {n: jnp.stack(parts) for n, parts in g.items()}
    grads = dict(
        meta_tokens=dh[:N_META], pre_norm_g=g["gpre"][:, 0], gate_bias=g["bias"][:, 0], pool_w=_block_diag_inverse(g["pwbd"]),
        pool_scale=g["pscale"][:, 0], q_norm_g=g["gq"][:, 0], kv_norm_g=g["gkv"][:, 0], conf_dw_w=g["conf_w"][:, :CONF_K],
        conf_dw_b=g["conf_vec"][:, 2], conf_ln_g=g["conf_vec"][:, 0], conf_ln_b=g["conf_vec"][:, 1], sc_dw_w=g["sc_w"][:, :SC_K],
        post_norm_g=g["gpost"][:, 0])
    return total[0, 0], dh[N_META:length], grads


def kernel(x, meta_tokens, pre_norm_g, w_in, gate_bias, pool_w, pool_scale, w_out_pool, q_norm_g, w_uq, kv_norm_g, w_ukv, w_out_mla, conf_dw_w, conf_dw_b, conf_ln_g, conf_ln_b, w_out_conf, sc_dw_w, w_out_sc, w_o, post_norm_g, loss_target, m_meta_tokens, m_pre_norm_g, m_w_in, m_gate_bias, m_pool_w, m_pool_scale, m_w_out_pool, m_q_norm_g, m_w_uq, m_kv_norm_g, m_w_ukv, m_w_out_mla, m_conf_dw_w, m_conf_dw_b, m_conf_ln_g, m_conf_ln_b, m_w_out_conf, m_sc_dw_w, m_w_out_sc, m_w_o, m_post_norm_g, v_meta_tokens, v_pre_norm_g, v_w_in, v_gate_bias, v_pool_w, v_pool_scale, v_w_out_pool, v_q_norm_g, v_w_uq, v_kv_norm_g, v_w_ukv, v_w_out_mla, v_conf_dw_w, v_conf_dw_b, v_conf_ln_g, v_conf_ln_b, v_w_out_conf, v_sc_dw_w, v_w_out_sc, v_w_o, v_post_norm_g):
    args = locals()
    weights = {n: args[n] for n in WEIGHT_ORDER}
    c = lax.axis_index("c")
    chip = 2 * lax.axis_index("x") + lax.axis_index("y")

    small = {n: weights[n] for n, _ in REPLICATED}
    small.update(gather_small(weights))
    large = MeshWeights(weights, c, chip)
    total, dx, grads = local_step(x[0], loss_target[0], small, large)
    loss = lax.psum(total * (0.5 / D_MODEL), ("x", "y", "c"))

    reduced = large.reduced()
    reduced.update(reduce_small(grads, chip))

    flip = lambda a: jnp.swapaxes(a, 1, 2)
    deltas, new_m, new_v = [], [], []
    for n in WEIGHT_ORDER:
        operands = (weights[n], reduced[n], args["m_" + n], args["v_" + n])
        if n == "w_in":
            operands = (flip(operands[0]), lax.optimization_barrier(flip(operands[1])), flip(operands[2]), flip(operands[3]))
            reduced[n] = flip(operands[1])
        d, nm, nv = adamw(*operands)
        if n == "w_in":
            d, nm, nv = flip(d), flip(nm), flip(nv)
        deltas.append(d)
        new_m.append(nm)
        new_v.append(nv)
    return (loss, dx[None], *[reduced[n] for n in WEIGHT_ORDER], *deltas, *new_m, *new_v)
```

```python
import functools
import math

import jax
import jax.numpy as jnp
from jax import lax
from jax.experimental import pallas as pl
from jax.experimental.pallas import tpu as pltpu

F32 = jnp.float32
BF16 = jnp.bfloat16

D_MODEL = 1024
DEPTH = 4
N_META = 16
EPS = 1e-6
HEADS = 8
QK_NOPE = 64
QK_ROPE = 32
V_DIM = 64
HEAD_PAD = 128
ROPE_THETA = 10000.0
Q_SCALE = (QK_NOPE + QK_ROPE) ** -0.5
CONF_K = 31
SC_K = 3
IN_W = 7328
N_CHIPS = 4

ZB = 3328
ZG = 4096
BG, C2, XV, SG, PV, PG, CQ, CKV, KR, MG, CA, CGT, CG = (0, 256, 512, 768, 1024, 1280, 1536, 1792, 1920, 2048, 2560, 2816, 3072)

KEY_GROUP = 4
ROW_TILE = 384
HALO = 32
LANES = 128
VMEM_LIMIT = 56 * 1024 * 1024

ADAM_LR = 0.001
ADAM_B1 = 0.9
ADAM_B2 = 0.999
ADAM_EPS = 1e-08
ADAM_WD = 0.01
ADAM_STEP = 10

MESH = pl.DeviceIdType.MESH
ANY = pl.BlockSpec(memory_space=pl.ANY)

MISC = ("w_out_mla", "w_out_pool", "w_out_conf", "w_out_sc", "w_uq", "w_ukv", "w_o")
M_MLA, M_POOL, M_CONF, M_SC, M_UQ, M_UKVK, M_UKVV, M_WO, MISC_ROWS = 0, 512, 768, 1024, 1536, 1792, 1920, 2048, 3072
SHARDED_SMALL = (
    ("meta_tokens", (N_META, 256), 1),
    ("conf_dw_w", (DEPTH, CONF_K, 64), 2),
    ("sc_dw_w", (DEPTH, SC_K, 64), 2),
)
REPLICATED = (
    ("pre_norm_g", (DEPTH, D_MODEL)),
    ("gate_bias", (DEPTH, 4 * D_MODEL)),
    ("pool_w", (DEPTH, 4, 64, 64)),
    ("pool_scale", (DEPTH, 256)),
    ("q_norm_g", (DEPTH, 256)),
    ("kv_norm_g", (DEPTH, 128)),
    ("conf_dw_b", (DEPTH, 256)),
    ("conf_ln_g", (DEPTH, 256)),
    ("conf_ln_b", (DEPTH, 256)),
    ("post_norm_g", (DEPTH, D_MODEL)),
)
WEIGHT_ORDER = ("meta_tokens", "pre_norm_g", "w_in", "gate_bias", "pool_w", "pool_scale", "w_out_pool", "q_norm_g", "w_uq",
                "kv_norm_g", "w_ukv", "w_out_mla", "conf_dw_w", "conf_dw_b", "conf_ln_g", "conf_ln_b", "w_out_conf", "sc_dw_w",
                "w_out_sc", "w_o", "post_norm_g")


def _dot(a, b):
    return lax.dot_general(a, b, (((1,), (0,)), ((), ())), preferred_element_type=F32)


def _dot_nt(a, b):
    return lax.dot_general(a, b, (((1,), (1,)), ((), ())), preferred_element_type=F32)


def _dot_tn(a, b):
    return lax.dot_general(a, b, (((0,), (0,)), ((), ())), preferred_element_type=F32)


def _sigmoid(x):
    return jax.nn.sigmoid(x)


def _silu(x):
    return x * _sigmoid(x)


def _silu_grad(x):
    s = _sigmoid(x)
    return s * (1.0 + x * (1.0 - s))


def _rms(x, g):
    return x * lax.rsqrt(jnp.mean(x * x, axis=-1, keepdims=True) + EPS) * g


def _sh(x, d):
    return x if d == 0 else pltpu.roll(x, d, 0)


def _ash(x, d):
    return x if d == 0 else pltpu.roll(x, x.shape[0] - d, 0)


def _lanes8(t):
    return jnp.concatenate([t] * HEADS, axis=1)


def _pool_window_sums(v, shift):
    a2 = v + shift(v, 1)
    a4 = a2 + shift(a2, 2)
    a8 = a4 + shift(a4, 4)
    a16 = a8 + shift(a8, 8)
    lane = lax.broadcasted_iota(jnp.int32, v.shape, 1)
    return jnp.where(lane < 64, a2, jnp.where(lane < 128, a4, jnp.where(lane < 192, a8, a16)))


def _pool_counts(first_row, rows):
    pos = first_row + lax.broadcasted_iota(jnp.int32, (rows, 256), 0)
    lane = lax.broadcasted_iota(jnp.int32, (rows, 256), 1)
    width = jnp.where(lane < 64, 2, jnp.where(lane < 128, 4, jnp.where(lane < 192, 8, 16)))
    return jnp.maximum(jnp.minimum(pos + 1, width), 1).astype(F32)


def _params(sem=None):
    return pltpu.CompilerParams(dimension_semantics=sem, vmem_limit_bytes=VMEM_LIMIT)


def _tile_specs(t, n_halo_blocks, li=0):
    per = t // HALO

    def layer(shape, idx=li):
        return pl.BlockSpec((None,) + tuple(shape), lambda i: (idx,) + (0,) * len(shape))

    def cur(c, cb=0):
        return pl.BlockSpec((t, c), lambda i: (i, cb))

    def prev(c, cb=0):
        return pl.BlockSpec((HALO, c), lambda i: (jnp.maximum(i * per - 1, 0), cb))

    def nxt(c, cb=0):
        return pl.BlockSpec((HALO, c), lambda i: (jnp.minimum((i + 1) * per, n_halo_blocks - 1), cb))

    def full(shape):
        return pl.BlockSpec(shape, lambda i: (0,) * len(shape))

    return cur, prev, nxt, full, layer


def _big_tile(rows):
    return rows // 3 if rows % (3 * LANES) == 0 else ROW_TILE


def matmul(a, b, mode, out_dtype, tm, tn, tk, name, b_layer=None):
    bs = b.shape if b_layer is None else b.shape[1:]
    lead = () if b_layer is None else (None,)
    pick = (lambda *ix: ix) if b_layer is None else (lambda *ix: (b_layer,) + ix)
    if mode == "nn":
        (m, k), n = a.shape, bs[1]
        a_spec = pl.BlockSpec((tm, tk), lambda i, j, kk: (i, kk))
        b_spec = pl.BlockSpec(lead + (tk, tn), lambda i, j, kk: pick(kk, j))
        dot = _dot
    elif mode == "nt":
        (m, k), n = a.shape, bs[0]
        a_spec = pl.BlockSpec((tm, tk), lambda i, j, kk: (i, kk))
        b_spec = pl.BlockSpec(lead + (tn, tk), lambda i, j, kk: pick(j, kk))
        dot = _dot_nt
    else:
        (k, m), n = a.shape, bs[1]
        a_spec = pl.BlockSpec((tk, tm), lambda i, j, kk: (kk, i))
        b_spec = pl.BlockSpec(lead + (tk, tn), lambda i, j, kk: pick(kk, j))
        dot = _dot_tn
    assert m % tm == 0 and n % tn == 0 and k % tk == 0, (a.shape, bs, tm, tn, tk)
    nk = k // tk

    def body(a_ref, b_ref, o_ref, acc_ref):
        kk = pl.program_id(2)

        @pl.when(kk == 0)
        def _():
            acc_ref[...] = jnp.zeros_like(acc_ref)

        acc_ref[...] += dot(a_ref[...], b_ref[...])

        @pl.when(kk == nk - 1)
        def _():
            o_ref[...] = acc_ref[...].astype(out_dtype)

    return pl.pallas_call(
        body, name=name, grid=(m // tm, n // tn, nk), in_specs=[a_spec, b_spec],
        out_specs=pl.BlockSpec((tm, tn), lambda i, j, kk: (i, j)), out_shape=jax.ShapeDtypeStruct((m, n), out_dtype),
        scratch_shapes=[pltpu.VMEM((tm, tn), F32)], compiler_params=_params(("parallel", "parallel", "arbitrary")),
    )(a, b)


def prenorm_project(hres, g, w, li):
    rows, d = hres.shape
    n = w.shape[2]
    tm, tn = _big_tile(rows), n // 2

    def body(x_ref, g_ref, w_ref, z_ref, hb_ref):
        @pl.when(pl.program_id(1) == 0)
        def _():
            hb_ref[...] = _rms(x_ref[...], g_ref[...]).astype(BF16)

        z_ref[...] = _dot(hb_ref[...], w_ref[...]).astype(BF16)

    return pl.pallas_call(
        body, name="prenorm_project", grid=(rows // tm, n // tn),
        in_specs=[pl.BlockSpec((tm, d), lambda i, j: (i, 0)), pl.BlockSpec((None, 1, d), lambda i, j: (li, 0, 0)),
                  pl.BlockSpec((None, d, tn), lambda i, j: (0, 0, j))],
        out_specs=[pl.BlockSpec((tm, tn), lambda i, j: (i, j)), pl.BlockSpec((tm, d), lambda i, j: (i, 0))],
        out_shape=[jax.ShapeDtypeStruct((rows, n), BF16), jax.ShapeDtypeStruct((rows, d), BF16)],
        compiler_params=_params(("parallel", "arbitrary")),
    )(hres, g, w)


def _rope(q, c, s1, s2, width):
    return q * c + pltpu.roll(q, width - 16, 1) * s1 + pltpu.roll(q, 16, 1) * s2


def _rope_transposed(dq, c, s1, s2, width):
    return dq * c + pltpu.roll(dq * s1, 16, 1) + pltpu.roll(dq * s2, width - 16, 1)


def _conf_conv(g1, w_ref):
    acc = jnp.zeros_like(g1)
    for k in range(CONF_K):
        acc = acc + w_ref[k:k + 1, :] * _sh(g1, CONF_K - 1 - k)
    return acc


def _conf_tail(c, cg, lg, lb):
    mu = jnp.mean(c, axis=-1, keepdims=True)
    xc = c - mu
    var = jnp.mean(xc * xc, axis=-1, keepdims=True)
    n = xc * lax.rsqrt(var + EPS) * lg + lb
    return _silu(n) * _silu(cg)


def _misc_spec(misc, row0, rows):
    assert row0 % rows == 0
    return pl.BlockSpec((None, N_CHIPS, rows, 256), lambda i: (misc[1], 0, row0 // rows, 0))


def _chip_columns(x, w_ref, row0, rows, lanes=256):
    return jnp.concatenate([_dot(x, w_ref[k, row0:row0 + rows, 0:lanes]) for k in range(N_CHIPS)], axis=1)


def branches_fwd(z_br, rope, pwbd, pscale, gq, gkv, misc, conf_w, conf_vec, sc_w, li):
    rows = z_br.shape[0]
    t = ROW_TILE
    cur, prev, _, _, layer = _tile_specs(t, rows // HALO, li)

    def body(zc_ref, zp_ref, rope_ref, pw_ref, ps_ref, gq_ref, gkv_ref, up_ref, cw_ref, cv_ref, sw_ref,
             ua_ref, uc_ref, ud_ref, q_ref, k_ref, v_ref):
        i = pl.program_id(0)
        zp = jnp.where(i == 0, jnp.zeros(zp_ref.shape, zp_ref.dtype), zp_ref[...])

        def ext(lo, w=256):
            return jnp.concatenate([zp[:, lo:lo + w], zc_ref[:, lo:lo + w]], axis=0).astype(F32)

        def col(lo, w=256):
            return zc_ref[:, lo:lo + w].astype(F32)

        v = ext(PV)
        p = (_pool_window_sums(v, _sh) / _pool_counts(i * t - HALO, t + HALO) - v)[HALO:]
        ya = _dot(p.astype(BF16), pw_ref[...]) * ps_ref[...]
        ua_ref[...] = (ya * _silu(col(PG))).astype(BF16)

        g1 = ext(CA) * _sigmoid(ext(CGT))
        c = _conf_conv(g1, cw_ref)[HALO:] + cv_ref[0:1, :]
        uc_ref[...] = _conf_tail(c, col(CG), cv_ref[1:2, :], cv_ref[2:3, :]).astype(BF16)

        e = ext(C2) * ext(XV)
        f = jnp.zeros_like(e)
        for k in range(SC_K):
            f = f + sw_ref[k:k + 1, :] * _sh(e, SC_K - 1 - k)
        ud_ref[...] = (col(BG) * f[HALO:] * _silu(col(SG))).astype(BF16)

        cth, s1, s2 = rope_ref[:, 0:128], rope_ref[:, 128:256], rope_ref[:, 256:384]
        qn = _rms(col(CQ), gq_ref[...]).astype(BF16)
        q = _chip_columns(qn, up_ref, 0, 256)
        w8 = HEADS * HEAD_PAD
        q_ref[...] = (_rope(q, _lanes8(cth), _lanes8(s1), _lanes8(s2), w8) * Q_SCALE).astype(BF16)
        kvn = _rms(col(CKV, 128), gkv_ref[...]).astype(BF16)
        kr = _rope(col(KR, 128), cth, s1, s2, HEAD_PAD)
        k_ref[...] = (_chip_columns(kvn, up_ref, M_UKVK - M_UQ, 128) + _lanes8(kr)).astype(BF16)
        v_ref[...] = _chip_columns(kvn, up_ref, M_UKVV - M_UQ, 128, 2 * V_DIM).astype(BF16)

    outs = [jax.ShapeDtypeStruct((rows, 256), BF16)] * 3 + [jax.ShapeDtypeStruct((rows, 1024), BF16)] * 2 + [
        jax.ShapeDtypeStruct((rows, 512), BF16)]
    return pl.pallas_call(
        body, name="branches_fwd", grid=(rows // t,),
        in_specs=[cur(ZB), prev(ZB), cur(384), layer((256, 256)), layer((1, 256)), layer((1, 256)), layer((1, 128)),
                  _misc_spec(misc, M_UQ, M_WO - M_UQ), layer((32, 256)), layer((8, 256)), layer((8, 256))],
        out_specs=[cur(256), cur(256), cur(256), cur(1024), cur(1024), cur(512)], out_shape=outs,
        compiler_params=_params(("parallel",)),
    )(z_br, z_br, rope, pwbd, pscale, gq, gkv, misc[0], conf_w, conf_vec, sc_w)


def _head_lane_mask(h):
    lane = lax.broadcasted_iota(jnp.int32, (1, 2 * V_DIM), 1)
    return (lane >= V_DIM * h) & (lane < V_DIM * (h + 1))


def attention_fwd(q, k, v, gather=None):
    rows = q.shape[0]
    tq = ROW_TILE
    nq = rows // tq
    n = 0 if gather is None else len(gather[0])

    def body(*refs):
        if n:
            start, finish = _gather_ops(refs[3:3 + n], refs[5 + 2 * n:5 + 3 * n], refs[5 + 3 * n:], gather[2], True)
            pl.when((pl.program_id(0) == 0) & (pl.program_id(1) == 0))(start)
        compute(*refs[:3], *refs[3 + 2 * n:5 + 2 * n])
        if n:
            pl.when((pl.program_id(0) == HEADS // 2 - 1) & (pl.program_id(1) == nq - 1))(finish)

    def compute(q_ref, k_ref, v_ref, o_ref, lse_ref):
        i = pl.program_id(1)

        def head_step(h, tile, n_tiles, carry, masked):
            m, l, acc = carry
            width = n_tiles * tq
            r0 = pl.multiple_of(tile * tq, tq)
            kh = k_ref[pl.ds(r0, width), HEAD_PAD * h:HEAD_PAD * (h + 1)]
            vh = jnp.where(_head_lane_mask(h), v_ref[pl.ds(r0, width), :], jnp.zeros((), BF16))
            s = _dot_nt(q_ref[:, HEAD_PAD * h:HEAD_PAD * (h + 1)], kh)
            if masked:
                row = lax.broadcasted_iota(jnp.int32, (tq, width), 0)
                colm = lax.broadcasted_iota(jnp.int32, (tq, width), 1)
                s = jnp.where(colm <= row + (width - tq), s, -1e30)
            m2 = jnp.maximum(m, jnp.max(s, axis=-1, keepdims=True))
            alpha = jnp.exp(m - m2)
            pr = jnp.exp(s - m2)
            return m2, alpha * l + jnp.sum(pr, axis=-1, keepdims=True), alpha * acc + _dot(pr.astype(BF16), vh)

        def step(tile, n_tiles, carry, masked):
            return tuple(head_step(h, tile, n_tiles, carry[h], masked) for h in range(2))

        init = (jnp.full((tq, 1), -1e30, F32), jnp.zeros((tq, 1), F32), jnp.zeros((tq, 2 * V_DIM), F32))
        group = min(KEY_GROUP, nq)
        carry = lax.fori_loop(0, i // group, lambda t, cr: step(group * t, group, cr, False), (init, init))
        carry = lax.switch(i % group, [functools.partial(lambda cr, r: step(i - r, r + 1, cr, True), r=r) for r in range(group)], carry)
        out = jnp.zeros((tq, 2 * V_DIM), F32)
        for h, (m, l, acc) in enumerate(carry):
            out = out + acc / l
            lse_ref[h] = jnp.broadcast_to(m + jnp.log(l), (tq, LANES))
        o_ref[...] = out.astype(BF16)

    srcs, dsts = ([], []) if gather is None else (list(gather[0]), list(gather[1]))
    outs = pl.pallas_call(
        body, name="attention_fwd" if gather is None else "attention_fwd_gather", grid=(HEADS // 2, nq),
        in_specs=[pl.BlockSpec((tq, 2 * HEAD_PAD), lambda p, i: (i, p)), pl.BlockSpec((rows, 2 * HEAD_PAD), lambda p, i: (0, p)),
                  pl.BlockSpec((rows, 2 * V_DIM), lambda p, i: (0, p))] + [ANY] * (2 * n),
        out_specs=[pl.BlockSpec((tq, 2 * V_DIM), lambda p, i: (i, p)), pl.BlockSpec((2, tq, LANES), lambda p, i: (p, i, 0))] + [ANY] * n,
        out_shape=[jax.ShapeDtypeStruct((rows, HEADS * V_DIM), BF16), jax.ShapeDtypeStruct((HEADS, rows, LANES), F32)] + [
            jax.ShapeDtypeStruct(d.shape, d.dtype) for d in dsts],
        input_output_aliases={3 + n + a: 2 + a for a in range(n)}, scratch_shapes=GATHER_SEMS(n) if n else [],
        compiler_params=_params(("arbitrary", "arbitrary") if n else ("parallel", "parallel")),
    )(q, k, v, *srcs, *dsts)
    return outs[0], outs[1], list(outs[2:])


OUT_PROJECTIONS = ((M_POOL, 256), (M_MLA, 512), (M_CONF, 256), (M_SC, 256))


def _chunks(x, n=N_CHIPS, width=256):
    return [x[:, width * k:width * (k + 1)] for k in range(n)]


def merge_fwd(ua, o_att, uc, ud, z_br, z_gl, bias, misc, gpost, hres, li):
    rows = hres.shape[0]
    t = ROW_TILE
    cur, _, _, _, layer = _tile_specs(t, rows // HALO, li)
    d = D_MODEL

    def body(ua_ref, ob_ref, uc_ref, ud_ref, mg_ref, gl_ref, b_ref, wout_ref, wo_ref, gp_ref, h_ref, ub_ref, mb_ref, o_ref, hn_ref):
        ub = (ob_ref[...].astype(F32) * _silu(mg_ref[...].astype(F32))).astype(BF16)
        ub_ref[...] = ub
        m = jnp.zeros((t, d), F32)
        for idx, (u, (row0, n)) in enumerate(zip((ua_ref[...], ub, uc_ref[...], ud_ref[...]), OUT_PROJECTIONS)):
            gate = _sigmoid(gl_ref[:, d * idx:d * (idx + 1)].astype(F32) + b_ref[:, d * idx:d * (idx + 1)])
            m = m + gate * _chip_columns(u, wout_ref, row0, n)
        mb = m.astype(BF16)
        mb_ref[...] = mb
        o = jnp.concatenate([sum(_dot(mk, wo_ref[k, 256 * j:256 * (j + 1), :]) for k, mk in enumerate(_chunks(mb)))
                             for j in range(N_CHIPS)], axis=1)
        o_ref[...] = o
        hn_ref[...] = h_ref[...] + _rms(o, gp_ref[...])

    return pl.pallas_call(
        body, name="merge_fwd", grid=(rows // t,),
        in_specs=[cur(256), cur(512), cur(256), cur(256), cur(512, MG // 512), cur(ZG), layer((1, ZG)), _misc_spec(misc, 0, 1280),
                  _misc_spec(misc, M_WO, D_MODEL), layer((1, d)), cur(d)],
        out_specs=[cur(512), cur(d), cur(d), cur(d)],
        out_shape=[jax.ShapeDtypeStruct((rows, 512), BF16), jax.ShapeDtypeStruct((rows, d), BF16), jax.ShapeDtypeStruct((rows, d), F32),
                   jax.ShapeDtypeStruct((rows, d), F32)],
        compiler_params=_params(("parallel",)),
    )(ua, o_att, uc, ud, z_br, z_gl, bias, misc[0], misc[0], gpost, hres)


def loss_head(hres, target, n_tokens):
    rows, d = hres.shape
    t = ROW_TILE
    cur, _, _, full, _ = _tile_specs(t, rows // HALO)
    n_steps = rows // t

    def body(h_ref, t_ref, dh_ref, tot_ref, acc_ref):
        i = pl.program_id(0)

        @pl.when(i == 0)
        def _():
            acc_ref[...] = jnp.zeros_like(acc_ref)

        r = i * t + lax.broadcasted_iota(jnp.int32, (t, 1), 0)
        diff = jnp.where((r >= N_META) & (r < N_META + n_tokens), h_ref[...] - t_ref[...], 0.0)
        dh_ref[...] = diff * (1.0 / d)
        acc_ref[...] += jnp.sum(diff * diff, axis=0, keepdims=True)

        @pl.when(i == n_steps - 1)
        def _():
            tot_ref[...] = jnp.broadcast_to(jnp.sum(acc_ref[...], axis=1, keepdims=True), (1, LANES))

    return pl.pallas_call(
        body, name="loss_head", grid=(n_steps,), in_specs=[cur(d), cur(d)], out_specs=[cur(d), full((1, LANES))],
        out_shape=[jax.ShapeDtypeStruct((rows, d), F32), jax.ShapeDtypeStruct((1, LANES), F32)],
        scratch_shapes=[pltpu.VMEM((1, d), F32)], compiler_params=_params(("arbitrary",)),
    )(hres, target)


def _accumulate(i, ref, value):
    @pl.when(i == 0)
    def _():
        ref[...] = value

    @pl.when(i > 0)
    def _():
        ref[...] += value


def postnorm_bwd(dh, o, mb, misc, gpost, li):
    rows, d = dh.shape
    t = ROW_TILE
    cur, _, _, full, layer = _tile_specs(t, rows // HALO, li)

    def body(dh_ref, o_ref, mb_ref, wo_ref, gp_ref, dm_ref, dwo_ref, dgp_ref):
        i = pl.program_id(0)
        _, vjp = jax.vjp(_rms, o_ref[...], gp_ref[...])
        do, dg = vjp(dh_ref[...])
        dob = do.astype(BF16)
        dm_ref[...] = jnp.concatenate([sum(_dot_nt(dj, wo_ref[k, 256 * j:256 * (j + 1), :]) for j, dj in enumerate(_chunks(dob)))
                                       for k in range(N_CHIPS)], axis=1)
        dwo = _dot_tn(mb_ref[...], dob)
        for k in range(N_CHIPS):
            _accumulate(i, dwo_ref.at[k], jnp.concatenate(_chunks(dwo[256 * k:256 * (k + 1), :]), axis=0))
        _accumulate(i, dgp_ref, dg)

    return pl.pallas_call(
        body, name="postnorm_bwd", grid=(rows // t,), in_specs=[cur(d), cur(d), cur(d), _misc_spec(misc, M_WO, d), layer((1, d))],
        out_specs=[cur(d), full((N_CHIPS, d, 256)), full((1, d))],
        out_shape=[jax.ShapeDtypeStruct((rows, d), F32), jax.ShapeDtypeStruct((N_CHIPS, d, 256), F32), jax.ShapeDtypeStruct((1, d), F32)],
        compiler_params=_params(("arbitrary",)),
    )(dh, o, mb, misc[0], gpost)


def merge_bwd(dm, ua, ub, uc, ud, z_gl, bias, misc, li):
    rows, d = dm.shape
    t = ROW_TILE
    cur, _, _, full, layer = _tile_specs(t, rows // HALO, li)
    widths = (256, 512, 256, 256)

    def body(dm_ref, ua_ref, ub_ref, uc_ref, ud_ref, gl_ref, b_ref, w_ref, dua_ref, dub_ref, duc_ref, dud_ref, dgl_ref, dw_ref, db_ref):
        i = pl.program_id(0)
        dm = dm_ref[...]
        groups = ((ua_ref, dua_ref), (ub_ref, dub_ref), (uc_ref, duc_ref), (ud_ref, dud_ref))
        for idx, ((u_ref, du_ref), (row0, n)) in enumerate(zip(groups, OUT_PROJECTIONS)):
            cols = slice(d * idx, d * (idx + 1))
            u = u_ref[...]
            gate = _sigmoid(gl_ref[:, cols].astype(F32) + b_ref[:, cols])
            dgl = dm * _chip_columns(u, w_ref, row0, n) * gate * (1.0 - gate)
            dgl_ref[:, cols] = dgl.astype(BF16)
            _accumulate(i, db_ref.at[:, cols], jnp.sum(dgl, axis=0, keepdims=True))
            dyb = (dm * gate).astype(BF16)
            du_ref[...] = sum(_dot_nt(dyk, w_ref[k, row0:row0 + n, :]) for k, dyk in enumerate(_chunks(dyb)))
            for k, dwk in enumerate(_chunks(_dot_tn(u, dyb))):
                _accumulate(i, dw_ref.at[k, row0:row0 + n, :], dwk)

    return pl.pallas_call(
        body, name="merge_bwd", grid=(rows // t,),
        in_specs=[cur(d), cur(256), cur(512), cur(256), cur(256), cur(ZG), layer((1, ZG)), _misc_spec(misc, 0, 1280)],
        out_specs=[cur(256), cur(512), cur(256), cur(256), cur(ZG), full((N_CHIPS, 1280, 256)), full((1, ZG))],
        out_shape=[jax.ShapeDtypeStruct((rows, w), F32) for w in widths] + [jax.ShapeDtypeStruct((rows, ZG), BF16),
                                                                            jax.ShapeDtypeStruct((N_CHIPS, 1280, 256), F32),
                                                                            jax.ShapeDtypeStruct((1, ZG), F32)],
        compiler_params=_params(("arbitrary",)),
    )(dm, ua, ub, uc, ud, z_gl, bias, misc[0])


def pool_bwd(z_br, dua, pwbd, pscale, dz_buf, li):
    rows = z_br.shape[0]
    t = ROW_TILE
    n_steps = rows // t
    cur, prev, nxt, full, layer = _tile_specs(t, rows // HALO, li)

    def body(zc_ref, zp_ref, zn_ref, dc_ref, dn_ref, pw_ref, ps_ref, _, dz_ref, dpw_ref, dps_ref):
        i = pl.program_id(0)
        zp = jnp.where(i == 0, jnp.zeros(zp_ref.shape, zp_ref.dtype), zp_ref[...])
        zn = jnp.where(i == n_steps - 1, jnp.zeros(zn_ref.shape, zn_ref.dtype), zn_ref[...])
        dun = jnp.where(i == n_steps - 1, jnp.zeros(dn_ref.shape, dn_ref.dtype), dn_ref[...])

        def ext(lo):
            return jnp.concatenate([zp[:, lo:lo + 256], zc_ref[:, lo:lo + 256], zn[:, lo:lo + 256]], axis=0).astype(F32)

        n_ext = t + 2 * HALO
        v, pg = ext(PV), ext(PG)
        cnt = _pool_counts(i * t - HALO, n_ext)
        p = (_pool_window_sums(v, _sh) / cnt - v)[HALO:HALO + t]
        du = jnp.concatenate([jnp.zeros((HALO, 256), F32), dc_ref[...], dun], axis=0)
        dya = du * _silu(pg)
        dypb = (dya * ps_ref[...]).astype(BF16)
        dp = _dot_nt(dypb, pw_ref[...])
        dv = (_pool_window_sums(dp / cnt, _ash) - dp)[HALO:HALO + t]
        pb = p.astype(BF16)
        pw = _dot(pb, pw_ref[...])
        duc, pgc = dc_ref[...], pg[HALO:HALO + t]
        dpg = duc * pw * ps_ref[...] * _silu_grad(pgc)
        dz_ref[...] = jnp.concatenate([dv, dpg], axis=1).astype(BF16)
        _accumulate(i, dpw_ref, _dot_tn(pb, dypb[HALO:HALO + t]))
        _accumulate(i, dps_ref, jnp.sum(dya[HALO:HALO + t] * pw, axis=0, keepdims=True))

    return pl.pallas_call(
        body, name="pool_bwd", grid=(n_steps,),
        in_specs=[cur(ZB), prev(ZB), nxt(ZB), cur(256), nxt(256), layer((256, 256)), layer((1, 256)), ANY],
        out_specs=[cur(512, PV // 512), full((256, 256)), full((1, 256))],
        out_shape=[jax.ShapeDtypeStruct((rows, ZB), BF16), jax.ShapeDtypeStruct((256, 256), F32), jax.ShapeDtypeStruct((1, 256), F32)],
        input_output_aliases={7: 0}, compiler_params=_params(("arbitrary",)),
    )(z_br, z_br, z_br, dua, dua, pwbd, pscale, dz_buf)


def shortconv_bwd(z_br, dud, sc_w, dz_buf, li):
    rows = z_br.shape[0]
    t = ROW_TILE
    n_steps = rows // t
    cur, prev, nxt, full, layer = _tile_specs(t, rows // HALO, li)

    def body(zc_ref, zp_ref, zn_ref, dc_ref, dn_ref, sw_ref, _, dz_ref, dw_ref):
        i = pl.program_id(0)
        zp = jnp.where(i == 0, jnp.zeros(zp_ref.shape, zp_ref.dtype), zp_ref[...])
        zn = jnp.where(i == n_steps - 1, jnp.zeros(zn_ref.shape, zn_ref.dtype), zn_ref[...])
        dun = jnp.where(i == n_steps - 1, jnp.zeros(dn_ref.shape, dn_ref.dtype), dn_ref[...])

        def ext(lo):
            return jnp.concatenate([zp[:, lo:lo + 256], zc_ref[:, lo:lo + 256], zn[:, lo:lo + 256]], axis=0).astype(F32)

        mid = slice(HALO, HALO + t)
        bg, c2, xv, sg = ext(BG), ext(C2), ext(XV), ext(SG)
        du = jnp.concatenate([jnp.zeros((HALO, 256), F32), dc_ref[...], dun], axis=0)
        e = c2 * xv
        shifted = [_sh(e, SC_K - 1 - k) for k in range(SC_K)]
        f = sum(sw_ref[k:k + 1, :] * shifted[k] for k in range(SC_K))
        gate = _silu(sg)
        df = du * gate * bg
        de = sum(sw_ref[k:k + 1, :] * _ash(df, SC_K - 1 - k) for k in range(SC_K))
        dbg = du * gate * f
        dsg = du * bg * f * _silu_grad(sg)
        dz_ref[...] = jnp.concatenate([dbg[mid], (de * xv)[mid], (de * c2)[mid], dsg[mid]], axis=1).astype(BF16)
        dw = jnp.concatenate([jnp.sum((df * shifted[k])[mid], axis=0, keepdims=True) for k in range(SC_K)] + [
            jnp.zeros((8 - SC_K, 256), F32)], axis=0)
        _accumulate(i, dw_ref, dw)

    return pl.pallas_call(
        body, name="shortconv_bwd", grid=(n_steps,), in_specs=[cur(ZB), prev(ZB), nxt(ZB), cur(256), nxt(256), layer((8, 256)), ANY],
        out_specs=[cur(1024, BG // 1024), full((8, 256))],
        out_shape=[jax.ShapeDtypeStruct((rows, ZB), BF16), jax.ShapeDtypeStruct((8, 256), F32)],
        input_output_aliases={6: 0}, compiler_params=_params(("arbitrary",)),
    )(z_br, z_br, z_br, dud, dud, sc_w, dz_buf)


def conformer_bwd_tail(z_br, duc, conf_w, conf_vec, dz_buf, li):
    rows = z_br.shape[0]
    t = ROW_TILE
    cur, prev, _, full, layer = _tile_specs(t, rows // HALO, li)

    def body(zc_ref, zp_ref, du_ref, cw_ref, cv_ref, _, dc_ref, dcg_ref, dv_ref):
        i = pl.program_id(0)
        zp = jnp.where(i == 0, jnp.zeros(zp_ref.shape, zp_ref.dtype), zp_ref[...])

        def ext(lo):
            return jnp.concatenate([zp[:, lo:lo + 256], zc_ref[:, lo:lo + 256]], axis=0).astype(F32)

        g1 = ext(CA) * _sigmoid(ext(CGT))
        c = _conf_conv(g1, cw_ref)[HALO:] + cv_ref[0:1, :]
        _, vjp = jax.vjp(_conf_tail, c, zc_ref[:, CG:CG + 256].astype(F32), cv_ref[1:2, :], cv_ref[2:3, :])
        dc, dcg, dlg, dlb = vjp(du_ref[...])
        dc_ref[...] = dc
        dcg_ref[...] = dcg.astype(BF16)
        dvec = jnp.concatenate([dlg, dlb, jnp.sum(dc, axis=0, keepdims=True), jnp.zeros((5, 256), F32)], axis=0)
        _accumulate(i, dv_ref, dvec)

    return pl.pallas_call(
        body, name="conformer_bwd_tail", grid=(rows // t,), in_specs=[cur(ZB), prev(ZB), cur(256), layer((32, 256)), layer((8, 256)), ANY],
        out_specs=[cur(256), cur(256, CG // 256), full((8, 256))],
        out_shape=[jax.ShapeDtypeStruct((rows, 256), F32), jax.ShapeDtypeStruct((rows, ZB), BF16), jax.ShapeDtypeStruct((8, 256), F32)],
        input_output_aliases={5: 1}, compiler_params=_params(("arbitrary",)),
    )(z_br, z_br, duc, conf_w, conf_vec, dz_buf)


def conformer_bwd_conv(z_br, dc, conf_w, dz_buf, li):
    rows = z_br.shape[0]
    t = ROW_TILE
    n_steps = rows // t
    cur, prev, nxt, full, layer = _tile_specs(t, rows // HALO, li)

    def body(zc_ref, zp_ref, dc_ref, dn_ref, cw_ref, _, dz_ref, dw_ref):
        i = pl.program_id(0)
        zp = jnp.where(i == 0, jnp.zeros(zp_ref.shape, zp_ref.dtype), zp_ref[...])
        dcn = jnp.where(i == n_steps - 1, jnp.zeros(dn_ref.shape, dn_ref.dtype), dn_ref[...])

        def ext(lo):
            return jnp.concatenate([zp[:, lo:lo + 256], zc_ref[:, lo:lo + 256]], axis=0).astype(F32)

        a, gt = ext(CA), ext(CGT)
        sg = _sigmoid(gt)
        g1 = a * sg
        dc = dc_ref[...]
        dce = jnp.concatenate([dc, dcn], axis=0)
        dg1 = jnp.zeros_like(dce)
        dws = []
        for k in range(CONF_K):
            dg1 = dg1 + cw_ref[k:k + 1, :] * _ash(dce, CONF_K - 1 - k)
            dws.append(jnp.sum(dc * _sh(g1, CONF_K - 1 - k)[HALO:], axis=0, keepdims=True))
        dg1 = dg1[:t]
        ac, sc = a[HALO:], sg[HALO:]
        dz_ref[...] = jnp.concatenate([dg1 * sc, dg1 * ac * sc * (1.0 - sc)], axis=1).astype(BF16)
        _accumulate(i, dw_ref, jnp.concatenate(dws + [jnp.zeros((32 - CONF_K, 256), F32)], axis=0))

    return pl.pallas_call(
        body, name="conformer_bwd_conv", grid=(n_steps,), in_specs=[cur(ZB), prev(ZB), cur(256), nxt(256), layer((32, 256)), ANY],
        out_specs=[cur(512, CA // 512), full((32, 256))],
        out_shape=[jax.ShapeDtypeStruct((rows, ZB), BF16), jax.ShapeDtypeStruct((32, 256), F32)],
        input_output_aliases={5: 0}, compiler_params=_params(("arbitrary",)),
    )(z_br, z_br, dc, dc, conf_w, dz_buf)


def attention_bwd_prep(dub, o_att, z_br, dz_buf):
    rows = dub.shape[0]
    t = ROW_TILE
    cur, _, _, _, _ = _tile_specs(t, rows // HALO)

    def body(du_ref, o_ref, mg_ref, _, do_ref, dmg_ref, delta_ref):
        du, o, mg = du_ref[...], o_ref[...].astype(F32), mg_ref[...].astype(F32)
        do = du * _silu(mg)
        do_ref[...] = do.astype(BF16)
        dmg_ref[...] = (du * o * _silu_grad(mg)).astype(BF16)
        prod = do * o
        lane = lax.broadcasted_iota(jnp.int32, (1, HEADS * V_DIM), 1)
        for h in range(HEADS):
            part = jnp.where((lane >= V_DIM * h) & (lane < V_DIM * (h + 1)), prod, 0.0)
            delta_ref[h] = jnp.broadcast_to(jnp.sum(part, axis=-1, keepdims=True), (t, LANES))

    return pl.pallas_call(
        body, name="attention_bwd_prep", grid=(rows // t,), in_specs=[cur(512), cur(512), cur(512, MG // 512), ANY],
        out_specs=[cur(512), cur(512, MG // 512), pl.BlockSpec((HEADS, t, LANES), lambda i: (0, i, 0))],
        out_shape=[jax.ShapeDtypeStruct((rows, 512), BF16), jax.ShapeDtypeStruct((rows, ZB), BF16),
                   jax.ShapeDtypeStruct((HEADS, rows, LANES), F32)],
        input_output_aliases={3: 1}, compiler_params=_params(("parallel",)),
    )(dub, o_att, z_br, dz_buf)


def attention_bwd(q, k, v, do, lse, delta, exchange=None):
    rows = q.shape[0]
    tq = ROW_TILE
    nq = rows // tq
    n = 0 if exchange is None else len(exchange[0])

    def body(*refs):
        if n:
            start, finish = _exchange_ops(refs[6:6 + n], refs[9 + 2 * n:9 + 3 * n], refs[9 + 3 * n:], exchange[2])
            pl.when((pl.program_id(0) == 0) & (pl.program_id(1) == 0))(start)
        compute(*refs[:6], *refs[6 + 2 * n:9 + 2 * n])
        if n:
            pl.when((pl.program_id(0) == HEADS // 2 - 1) & (pl.program_id(1) == nq - 1))(finish)

    def compute(q_ref, k_ref, v_ref, do_ref, lse_ref, dl_ref, dq_ref, dk_ref, dv_ref):
        j = pl.program_id(1)

        @pl.when(j == 0)
        def _():
            dq_ref[...] = jnp.zeros_like(dq_ref)

        def head_step(h, tile, n_tiles, dk, dv, diagonal):
            lanes = slice(HEAD_PAD * h, HEAD_PAD * (h + 1))
            hm = _head_lane_mask(h)
            kh = k_ref[:, lanes]
            vh = jnp.where(hm, v_ref[...], jnp.zeros((), BF16))
            r0, width = pl.multiple_of(tile * tq, tq), n_tiles * tq
            qi = q_ref[pl.ds(r0, width), lanes]
            doi = jnp.where(hm, do_ref[pl.ds(r0, width), :], jnp.zeros((), BF16))
            s = _dot_nt(qi, kh)
            if diagonal:
                s = jnp.where(lax.broadcasted_iota(jnp.int32, (tq, tq), 1) <= lax.broadcasted_iota(jnp.int32, (tq, tq), 0), s, -1e30)
            pr = jnp.exp(s - lse_ref[h, pl.ds(r0, width), :][:, 0:1])
            dv = dv + _dot_tn(pr.astype(BF16), doi)
            dp = _dot_nt(doi, vh)
            ds = (pr * (dp - dl_ref[h, pl.ds(r0, width), :][:, 0:1])).astype(BF16)
            dq_ref[pl.ds(r0, width), lanes] += _dot(ds, kh)
            return dk + _dot_tn(ds, qi), dv

        def step(tile, n_tiles, carry, diagonal):
            dk0, dk1, dv = carry
            dk0, dv = head_step(0, tile, n_tiles, dk0, dv, diagonal)
            dk1, dv = head_step(1, tile, n_tiles, dk1, dv, diagonal)
            return dk0, dk1, dv

        zero = jnp.zeros((tq, HEAD_PAD), F32)
        carry = step(j, 1, (zero, zero, jnp.zeros((tq, 2 * V_DIM), F32)), True)
        odd = (nq - 1 - j) % 2
        carry = lax.cond(odd == 1, lambda cr: step(j + 1, 1, cr, False), lambda cr: cr, carry)
        dk0, dk1, dv = lax.fori_loop(0, (nq - 1 - j) // 2, lambda t, cr: step(j + 1 + odd + 2 * t, 2, cr, False), carry)
        dk_ref[:, 0:HEAD_PAD] = dk0
        dk_ref[:, HEAD_PAD:2 * HEAD_PAD] = dk1
        dv_ref[...] = dv

    srcs, dsts = ([], []) if exchange is None else (list(exchange[0]), list(exchange[1]))
    outs = pl.pallas_call(
        body, name="attention_bwd" if exchange is None else "attention_bwd_exchange", grid=(HEADS // 2, nq),
        in_specs=[pl.BlockSpec((rows, 2 * HEAD_PAD), lambda p, j: (0, p)), pl.BlockSpec((tq, 2 * HEAD_PAD), lambda p, j: (j, p)),
                  pl.BlockSpec((tq, 2 * V_DIM), lambda p, j: (j, p)), pl.BlockSpec((rows, 2 * V_DIM), lambda p, j: (0, p)),
                  pl.BlockSpec((2, rows, LANES), lambda p, j: (p, 0, 0)), pl.BlockSpec((2, rows, LANES), lambda p, j: (p, 0, 0))] + [
                      ANY] * (2 * n),
        out_specs=[pl.BlockSpec((rows, 2 * HEAD_PAD), lambda p, j: (0, p)), pl.BlockSpec((tq, 2 * HEAD_PAD), lambda p, j: (j, p)),
                   pl.BlockSpec((tq, 2 * V_DIM), lambda p, j: (j, p))] + [ANY] * n,
        out_shape=[jax.ShapeDtypeStruct((rows, HEADS * HEAD_PAD), F32), jax.ShapeDtypeStruct((rows, HEADS * HEAD_PAD), F32),
                   jax.ShapeDtypeStruct((rows, HEADS * V_DIM), F32)] + [jax.ShapeDtypeStruct(d.shape, d.dtype) for d in dsts],
        input_output_aliases={6 + n + a: 3 + a for a in range(n)}, scratch_shapes=EXCHANGE_SEMS(n) if n else [],
        compiler_params=_params(("arbitrary", "arbitrary") if n else ("parallel", "arbitrary")),
    )(q, k, v, do, lse, delta, *srcs, *dsts)
    return outs[0], outs[1], outs[2], list(outs[3:])


def mla_prep_bwd(dq, dk, dv, z_br, rope, gq, gkv, misc, dz_buf, li):
    rows = dq.shape[0]
    t = ROW_TILE
    cur, _, _, full, layer = _tile_specs(t, rows // HALO, li)
    w8 = HEADS * HEAD_PAD
    uq, keys, values = slice(0, 256), slice(M_UKVK - M_UQ, M_UKVV - M_UQ), slice(M_UKVV - M_UQ, M_WO - M_UQ)

    def body(dq_ref, dk_ref, dv_ref, z_ref, rope_ref, gq_ref, gkv_ref, up_ref, _, dz_ref, dup_ref, dgq_ref, dgkv_ref):
        i = pl.program_id(0)
        cth, s1, s2 = rope_ref[:, 0:128], rope_ref[:, 128:256], rope_ref[:, 256:384]
        dqb = _rope_transposed(dq_ref[...] * Q_SCALE, _lanes8(cth), _lanes8(s1), _lanes8(s2), w8).astype(BF16)
        dq_chunks = _chunks(dqb)
        cq = z_ref[:, 0:256].astype(F32)
        qn, vjp_q = jax.vjp(_rms, cq, gq_ref[...])
        dcq, dgq = vjp_q(sum(_dot_nt(dqk, up_ref[k, uq, :]) for k, dqk in enumerate(dq_chunks)))
        _accumulate(i, dgq_ref, dgq)

        dk = dk_ref[...]
        dkr = sum(dk[:, HEAD_PAD * h:HEAD_PAD * (h + 1)] for h in range(HEADS))
        dkr = _rope_transposed(dkr, cth, s1, s2, HEAD_PAD)
        lane = lax.broadcasted_iota(jnp.int32, (1, HEAD_PAD), 1)
        dkr = jnp.where((lane >= QK_NOPE) & (lane < QK_NOPE + QK_ROPE), dkr, 0.0)
        dkb, dvb = dk.astype(BF16), dv_ref[...].astype(BF16)
        dk_chunks, dv_chunks = _chunks(dkb), _chunks(dvb, width=2 * V_DIM)
        ckv = z_ref[:, 256:384].astype(F32)
        kvn, vjp_kv = jax.vjp(_rms, ckv, gkv_ref[...])
        dckv, dgkv = vjp_kv(sum(_dot_nt(dk_chunks[k], up_ref[k, keys, :]) + _dot_nt(dv_chunks[k], up_ref[k, values, 0:2 * V_DIM])
                                for k in range(N_CHIPS)))
        _accumulate(i, dgkv_ref, dgkv)
        dz_ref[...] = jnp.concatenate([dcq, dckv, dkr], axis=1).astype(BF16)
        qnb, kvnb = qn.astype(BF16), kvn.astype(BF16)
        d_uq, d_keys, d_values = _chunks(_dot_tn(qnb, dqb)), _chunks(_dot_tn(kvnb, dkb)), _chunks(_dot_tn(kvnb, dvb), width=2 * V_DIM)
        for k in range(N_CHIPS):
            padded = jnp.concatenate([d_values[k], jnp.zeros((128, 256 - 2 * V_DIM), F32)], axis=1)
            _accumulate(i, dup_ref.at[k], jnp.concatenate([d_uq[k], d_keys[k], padded], axis=0))

    return pl.pallas_call(
        body, name="mla_prep_bwd", grid=(rows // t,),
        in_specs=[cur(w8), cur(w8), cur(512), cur(512, CQ // 512), cur(384), layer((1, 256)), layer((1, 128)),
                  _misc_spec(misc, M_UQ, M_WO - M_UQ), ANY],
        out_specs=[cur(512, CQ // 512), full((N_CHIPS, M_WO - M_UQ, 256)), full((1, 256)), full((1, 128))],
        out_shape=[jax.ShapeDtypeStruct((rows, ZB), BF16), jax.ShapeDtypeStruct((N_CHIPS, M_WO - M_UQ, 256), F32),
                   jax.ShapeDtypeStruct((1, 256), F32), jax.ShapeDtypeStruct((1, 128), F32)],
        input_output_aliases={8: 0}, compiler_params=_params(("arbitrary",)),
    )(dq, dk, dv, z_br, rope, gq, gkv, misc[0], dz_buf)


def prenorm_bwd(dz_br, w_br, dh_gl, hres, gpre, dh_next, li):
    rows, d = hres.shape
    t = ROW_TILE
    cur, _, _, full, layer = _tile_specs(t, rows // HALO, li)

    def body(dz_ref, w_ref, dp_ref, x_ref, g_ref, dn_ref, dx_ref, dg_ref):
        i = pl.program_id(0)
        dh = _dot_nt(dz_ref[...], w_ref[...]) + dp_ref[...]
        _, vjp = jax.vjp(_rms, x_ref[...], g_ref[...])
        dx, dg = vjp(dh)
        dx_ref[...] = dx + dn_ref[...]
        _accumulate(i, dg_ref, dg)

    return pl.pallas_call(
        body, name="prenorm_bwd", grid=(rows // t,), in_specs=[cur(ZB), layer((d, ZB), 0), cur(d), cur(d), layer((1, d)), cur(d)],
        out_specs=[cur(d), full((1, d))], out_shape=[jax.ShapeDtypeStruct((rows, d), F32), jax.ShapeDtypeStruct((1, d), F32)],
        compiler_params=_params(("arbitrary",)),
    )(dz_br, w_br, dh_gl, hres, gpre, dh_next)


def _mesh_position():
    return lax.axis_index("x"), lax.axis_index("y"), lax.axis_index("c")


def chip_exchange(src, gather, name):
    block = src.shape if gather else src.shape[1:]

    def body(src_ref, dst_ref, send_sems, recv_sems, local_sem):
        x, y, c = _mesh_position()
        me = 2 * x + y
        peers = ((1 - x, y), (x, 1 - y), (1 - x, 1 - y))

        def part(k):
            return src_ref if gather else src_ref.at[k]

        def copy(j, slot):
            px, py = peers[j]
            return pltpu.make_async_remote_copy(src_ref=part(2 * px + py), dst_ref=dst_ref.at[slot], send_sem=send_sems.at[j],
                                                recv_sem=recv_sems.at[j], device_id=(px, py, c), device_id_type=MESH)

        local = pltpu.make_async_copy(part(me), dst_ref.at[me], local_sem)
        local.start()
        sends = [copy(j, me) for j in range(3)]
        for cp in sends:
            cp.start()
        for j, (px, py) in enumerate(peers):
            copy(j, 2 * px + py).wait_recv()
        for cp in sends:
            cp.wait_send()
        local.wait()

    return pl.pallas_call(
        body, name=name, in_specs=[pl.BlockSpec(memory_space=pl.ANY)], out_specs=pl.BlockSpec(memory_space=pl.ANY),
        out_shape=jax.ShapeDtypeStruct((N_CHIPS,) + tuple(block), src.dtype),
        scratch_shapes=[pltpu.SemaphoreType.DMA((3,)), pltpu.SemaphoreType.DMA((3,)), pltpu.SemaphoreType.DMA(())],
    )(src)


def sibling_swap(src, name):
    def body(src_ref, dst_ref, send_sem, recv_sem):
        x, y, c = _mesh_position()
        cp = pltpu.make_async_remote_copy(src_ref=src_ref, dst_ref=dst_ref, send_sem=send_sem, recv_sem=recv_sem,
                                          device_id=(x, y, 1 - c), device_id_type=MESH)
        cp.start()
        cp.wait()

    return pl.pallas_call(
        body, name=name, in_specs=[pl.BlockSpec(memory_space=pl.ANY)], out_specs=pl.BlockSpec(memory_space=pl.ANY),
        out_shape=jax.ShapeDtypeStruct(src.shape, src.dtype),
        scratch_shapes=[pltpu.SemaphoreType.DMA(()), pltpu.SemaphoreType.DMA(())],
    )(src)


def _comm_call(body, name, n_in, out_shapes, n_sems):
    return pl.pallas_call(
        body, name=name, in_specs=[ANY] * n_in, out_specs=[ANY] * len(out_shapes), out_shape=out_shapes,
        scratch_shapes=[pltpu.SemaphoreType.DMA((n,)) for n in n_sems])


def _row_halves(c, rows):
    half = rows // 2
    return pl.ds(pl.multiple_of(c * half, 16), half), pl.ds(pl.multiple_of((1 - c) * half, 16), half)


def _peers():
    x, y, c = _mesh_position()
    return x, y, c, 2 * x + y, ((1 - x, y), (x, 1 - y), (1 - x, 1 - y))


def _gather_ops(src, dst, sems, layer, own_copy):
    ici_send, ici_recv, d2d_send, d2d_recv, own_sems = sems
    n = len(src)

    def fetch(a, j, slot):
        x, y, c, _, peers = _peers()
        px, py = peers[j]
        mine, _ = _row_halves(c, src[a].shape[1])
        return pltpu.make_async_remote_copy(src_ref=src[a].at[layer, mine], dst_ref=dst[a].at[layer, slot, mine], send_sem=ici_send.at[3 * a + j],
                                            recv_sem=ici_recv.at[3 * a + j], device_id=(px, py, c), device_id_type=MESH)

    def forward(a, j, sibling_half):
        x, y, c, _, peers = _peers()
        px, py = peers[j]
        part = dst[a].at[layer, 2 * px + py, _row_halves(c, src[a].shape[1])[1 if sibling_half else 0]]
        return pltpu.make_async_remote_copy(src_ref=part, dst_ref=part, send_sem=d2d_send.at[3 * a + j], recv_sem=d2d_recv.at[3 * a + j],
                                            device_id=(x, y, 1 - c), device_id_type=MESH)

    def own(a):
        return pltpu.make_async_copy(src[a].at[layer], dst[a].at[layer, _peers()[3]], own_sems.at[a])

    def start():
        me = _peers()[3]
        for a in range(n):
            if own_copy:
                own(a).start()
            for j in range(3):
                fetch(a, j, me).start()

    def finish():
        peers = _peers()[4]
        for j, (px, py) in enumerate(peers):
            for a in range(n):
                fetch(a, j, 2 * px + py).wait_recv()
                forward(a, j, False).start()
        for j in range(3):
            for a in range(n):
                forward(a, j, True).wait_recv()
        for j in range(3):
            for a in range(n):
                fetch(a, j, 0).wait_send()
                forward(a, j, False).wait_send()
        if own_copy:
            for a in range(n):
                own(a).wait()

    return start, finish


def _exchange_ops(src, dst, sems, layer):
    send_sems, recv_sems, own_sems = sems
    n = len(src)

    def copy(a, j, slot):
        x, y, c, _, peers = _peers()
        px, py = peers[j]
        return pltpu.make_async_remote_copy(src_ref=src[a].at[2 * px + py], dst_ref=dst[a].at[layer, slot], send_sem=send_sems.at[3 * a + j],
                                            recv_sem=recv_sems.at[3 * a + j], device_id=(px, py, c), device_id_type=MESH)

    def own(a):
        me = _peers()[3]
        return pltpu.make_async_copy(src[a].at[me], dst[a].at[layer, me], own_sems.at[a])

    def start():
        me = _peers()[3]
        for a in range(n):
            own(a).start()
            for j in range(3):
                copy(a, j, me).start()

    def finish():
        peers = _peers()[4]
        for j, (px, py) in enumerate(peers):
            for a in range(n):
                copy(a, j, 2 * px + py).wait_recv()
        for j in range(3):
            for a in range(n):
                copy(a, j, 0).wait_send()
        for a in range(n):
            own(a).wait()

    return start, finish


GATHER_SEMS = lambda n: [pltpu.SemaphoreType.DMA((3 * n,))] * 4 + [pltpu.SemaphoreType.DMA((n,))]
EXCHANGE_SEMS = lambda n: [pltpu.SemaphoreType.DMA((3 * n,))] * 2 + [pltpu.SemaphoreType.DMA((n,))]


def gather_layer(srcs, dsts, layer, name):
    n = len(srcs)

    def body(*refs):
        start, finish = _gather_ops(refs[:n], refs[2 * n:3 * n], refs[3 * n:], layer, False)
        start()
        finish()

    return pl.pallas_call(
        body, name=name, in_specs=[ANY] * (2 * n), out_specs=[ANY] * n, out_shape=[jax.ShapeDtypeStruct(d.shape, d.dtype) for d in dsts],
        input_output_aliases={n + a: a for a in range(n)}, scratch_shapes=GATHER_SEMS(n),
    )(*srcs, *dsts)


def exchange_layer(ss, dsts, layer, name):
    n = len(ss)

    def body(*refs):
        start, finish = _exchange_ops(refs[:n], refs[2 * n:3 * n], refs[3 * n:], layer)
        start()
        finish()

    return pl.pallas_call(
        body, name=name, in_specs=[ANY] * (2 * n), out_specs=[ANY] * n, out_shape=[jax.ShapeDtypeStruct(d.shape, d.dtype) for d in dsts],
        input_output_aliases={n + a: a for a in range(n)}, scratch_shapes=EXCHANGE_SEMS(n),
    )(*ss, *dsts)


def swap_row_halves(ps, name):
    n = len(ps)

    def body(*refs):
        src, dst = refs[:n], refs[n:2 * n]
        send_sems, recv_sems = refs[2 * n:]
        x, y, c = _mesh_position()
        copies = [pltpu.make_async_remote_copy(src_ref=src[a].at[:, _row_halves(c, src[a].shape[1])[1]], dst_ref=dst[a], send_sem=send_sems.at[a],
                                               recv_sem=recv_sems.at[a], device_id=(x, y, 1 - c), device_id_type=MESH) for a in range(n)]
        for cp in copies:
            cp.start()
        for cp in copies:
            cp.wait()

    outs = [jax.ShapeDtypeStruct((p.shape[0], p.shape[1] // 2, p.shape[2]), p.dtype) for p in ps]
    return _comm_call(body, name, n, outs, (n, n))(*ps)


def add_row_half(p, r, c, name):
    n, half, cols = r.shape
    rb = _row_block(half, cols, 2)
    steps = half // rb

    def body(c_ref, p_ref, r_ref, o_ref):
        o_ref[...] = (p_ref[...].astype(F32) + r_ref[...].astype(F32)).astype(BF16)

    return pl.pallas_call(
        body, name=name, out_shape=jax.ShapeDtypeStruct(r.shape, BF16),
        grid_spec=pltpu.PrefetchScalarGridSpec(
            num_scalar_prefetch=1, grid=(n, steps),
            in_specs=[pl.BlockSpec((1, rb, cols), lambda k, i, c_ref: (k, c_ref[0] * steps + i, 0)),
                      pl.BlockSpec((1, rb, cols), lambda k, i, c_ref: (k, i, 0))],
            out_specs=pl.BlockSpec((1, rb, cols), lambda k, i, c_ref: (k, i, 0))),
        compiler_params=_params(("parallel", "parallel")),
    )(jnp.reshape(c, (1,)).astype(jnp.int32), p, r)


def sum_row_halves(l, c, name):
    layers, n, half, cols = l.shape
    rb = _row_block(half, cols, 4)
    steps = half // rb

    def body(c_ref, l_ref, o_ref):
        acc = l_ref[0, 0].astype(F32)
        for s in range(1, n):
            acc = acc + l_ref[0, s].astype(F32)
        o_ref[0] = acc

    return pl.pallas_call(
        body, name=name, out_shape=jax.ShapeDtypeStruct((layers, 2 * half, cols), F32),
        grid_spec=pltpu.PrefetchScalarGridSpec(
            num_scalar_prefetch=1, grid=(layers, steps), in_specs=[pl.BlockSpec((1, n, rb, cols), lambda a, i, c_ref: (a, 0, i, 0))],
            out_specs=pl.BlockSpec((1, rb, cols), lambda a, i, c_ref: (a, c_ref[0] * steps + i, 0))),
        compiler_params=_params(("parallel", "parallel")),
    )(jnp.reshape(c, (1,)).astype(jnp.int32), l)


def share_row_halves(gs, name):
    n = len(gs)

    def body(*refs):
        dst = refs[n:2 * n]
        send_sems, recv_sems = refs[2 * n:]
        x, y, c = _mesh_position()

        def copy(a, sibling_half):
            part = dst[a].at[:, _row_halves(c, dst[a].shape[1])[1 if sibling_half else 0]]
            return pltpu.make_async_remote_copy(src_ref=part, dst_ref=part, send_sem=send_sems.at[a], recv_sem=recv_sems.at[a],
                                                device_id=(x, y, 1 - c), device_id_type=MESH)

        for a in range(n):
            copy(a, False).start()
        for a in range(n):
            copy(a, True).wait_recv()
        for a in range(n):
            copy(a, False).wait_send()

    return pl.pallas_call(
        body, name=name, in_specs=[ANY] * n, out_specs=[ANY] * n, out_shape=[jax.ShapeDtypeStruct(g.shape, g.dtype) for g in gs],
        input_output_aliases={a: a for a in range(n)}, scratch_shapes=[pltpu.SemaphoreType.DMA((n,)), pltpu.SemaphoreType.DMA((n,))],
    )(*gs)


def _row_block(rows, cols, itemsize):
    best = 16
    for rb in range(16, rows + 1, 16):
        if rows % rb == 0 and rb * cols * itemsize <= 2 * 1024 * 1024:
            best = rb
    assert rows % best == 0, (rows, cols)
    return best


def _comm_block(rows):
    return 1024 if rows % 1024 == 0 else rows


def sum_slots(buf, name):
    n, r, c = buf.shape
    rb = _comm_block(r)

    def body(b_ref, o_ref):
        acc = b_ref[0].astype(F32)
        for s in range(1, n):
            acc = acc + b_ref[s].astype(F32)
        o_ref[...] = acc

    return pl.pallas_call(
        body, name=name, grid=(r // rb,), in_specs=[pl.BlockSpec((n, rb, c), lambda i: (0, i, 0))],
        out_specs=pl.BlockSpec((rb, c), lambda i: (i, 0)), out_shape=jax.ShapeDtypeStruct((r, c), F32),
        compiler_params=_params(("parallel",)),
    )(buf)


def add_pair(a, b, out_dtype, name):
    shape = a.shape
    a2, b2 = a.reshape(-1, shape[-1]), b.reshape(-1, shape[-1])
    r, c = a2.shape
    rb = _comm_block(r)

    def body(a_ref, b_ref, o_ref):
        o_ref[...] = (a_ref[...].astype(F32) + b_ref[...].astype(F32)).astype(out_dtype)

    out = pl.pallas_call(
        body, name=name, grid=(r // rb,), in_specs=[pl.BlockSpec((rb, c), lambda i: (i, 0))] * 2,
        out_specs=pl.BlockSpec((rb, c), lambda i: (i, 0)), out_shape=jax.ShapeDtypeStruct((r, c), out_dtype),
        compiler_params=_params(("parallel",)),
    )(a2, b2)
    return out.reshape(shape)


def adamw(w, g, m, v):
    shape = w.shape
    cols = shape[-1]
    rows = math.prod(shape[:-1])
    if rows * cols <= 256 * 1024:
        rb, cb = rows, cols
    else:
        rb = max(r for r in range(8, 2049, 8) if rows % r == 0)
        cb = cols if rb * cols * 4 <= 2 * 1024 * 1024 else 256
    assert rows % rb == 0 and cols % cb == 0, shape

    def body(w_ref, g_ref, m_ref, v_ref, d_ref, nm_ref, nv_ref):
        g_ = g_ref[...]
        nm = ADAM_B1 * m_ref[...] + (1.0 - ADAM_B1) * g_
        nv = ADAM_B2 * v_ref[...] + (1.0 - ADAM_B2) * (g_ * g_)
        m_hat = nm / (1.0 - ADAM_B1 ** ADAM_STEP)
        v_hat = nv / (1.0 - ADAM_B2 ** ADAM_STEP)
        d_ref[...] = -ADAM_LR * (m_hat / (jnp.sqrt(v_hat) + ADAM_EPS) + ADAM_WD * w_ref[...])
        nm_ref[...] = nm
        nv_ref[...] = nv

    spec = pl.BlockSpec((rb, cb), lambda i, j: (i, j))
    outs = pl.pallas_call(
        body, name="adamw", grid=(rows // rb, cols // cb), in_specs=[spec] * 4, out_specs=[spec] * 3,
        out_shape=[jax.ShapeDtypeStruct((rows, cols), F32)] * 3, compiler_params=_params(("parallel", "parallel")),
    )(*(a.reshape(rows, cols) for a in (w, g, m, v)))
    return tuple(o.reshape(shape) for o in outs)


def _pack(arrays, dtype, row_multiple):
    flat = jnp.concatenate([a.astype(dtype).reshape(-1) for a in arrays])
    per = LANES * row_multiple
    total = -(-flat.shape[0] // per) * per
    return jnp.pad(flat, (0, total - flat.shape[0])).reshape(total // LANES, LANES)


def _unpack(buf, shapes):
    flat = buf.reshape(-1)
    out, off = [], 0
    for s in shapes:
        n = math.prod(s)
        out.append(flat[off:off + n].reshape(s))
        off += n
    return out


def _input_weights(blocks):
    c0, c1, c2, c3 = (blocks[..., k, :, :] for k in range(N_CHIPS))
    pad = lambda n: jnp.zeros(c0.shape[:-1] + (n,), blocks.dtype)
    w_br = jnp.concatenate([c1[..., 376:1400], c0[..., 0:896], pad(64), c0[..., 896:928], pad(32), c0[..., 928:], c1[..., 0:376]], axis=-1)
    return w_br, jnp.concatenate([c1[..., 1400:], c2, c3], axis=-1)


def _input_weights_inverse(dw_br, dw_gl):
    c0 = jnp.concatenate([dw_br[..., 1024:1920], dw_br[..., 1984:2016], dw_br[..., 2048:2952]], axis=-1)
    c1 = jnp.concatenate([dw_br[..., 2952:ZB], dw_br[..., 0:1024], dw_gl[..., 0:432]], axis=-1)
    return jnp.stack([c0, c1, dw_gl[..., 432:2264], dw_gl[..., 2264:]], axis=-3)


def _block_diag(pw):
    zeros = lambda n: jnp.zeros(pw.shape[:-3] + (64, n), pw.dtype)
    rows = [jnp.concatenate([zeros(64 * g), pw[..., g, :, :], zeros(64 * (3 - g))], axis=-1) for g in range(4)]
    return jnp.concatenate(rows, axis=-2)


def _block_diag_inverse(d):
    return jnp.stack([d[..., 64 * g:64 * (g + 1), 64 * g:64 * (g + 1)] for g in range(4)], axis=-3)


def _pad_rows(a, n):
    return jnp.pad(a, ((0, n - a.shape[0]), (0, 0)))


def _rope_tables(rows):
    inv = 1.0 / (ROPE_THETA ** (jnp.arange(0, QK_ROPE, 2, dtype=F32) / QK_ROPE))
    ang = jnp.arange(rows, dtype=F32)[:, None] * inv[None, :]
    cos, sin = jnp.cos(ang), jnp.sin(ang)
    one, zero = jnp.ones((rows, 1), F32), jnp.zeros((rows, 1), F32)
    rep = lambda a, n: jnp.broadcast_to(a, (rows, n))
    c = jnp.concatenate([rep(one, 64), cos, cos, rep(one, 32)], axis=1)
    s1 = jnp.concatenate([rep(zero, 64), -sin, rep(zero, 48)], axis=1)
    s2 = jnp.concatenate([rep(zero, 80), sin, rep(zero, 32)], axis=1)
    return jnp.concatenate([c, s1, s2], axis=1)


def _misc_block(parts):
    lead = parts["w_uq"].shape[:-2]
    pad_last = lambda a, n: jnp.pad(a, [(0, 0)] * (a.ndim - 1) + [(0, n - a.shape[-1])])
    uq = pad_last(parts["w_uq"].reshape(lead + (256, 2, QK_NOPE + QK_ROPE)), HEAD_PAD).reshape(lead + (256, 256))
    kv = parts["w_ukv"].reshape(lead + (128, 2, QK_NOPE + V_DIM))
    keys = pad_last(kv[..., :QK_NOPE], HEAD_PAD).reshape(lead + (128, 256))
    values = pad_last(kv[..., QK_NOPE:].reshape(lead + (128, 2 * V_DIM)), 256)
    wo = jnp.swapaxes(parts["w_o"].reshape(lead + (256, N_CHIPS, 256)), -3, -2).reshape(lead + (D_MODEL, 256))
    gap = jnp.zeros(lead + (M_UQ - M_SC - 256, 256), uq.dtype)
    return jnp.concatenate([parts["w_out_mla"], parts["w_out_pool"], parts["w_out_conf"], parts["w_out_sc"], gap, uq, keys, values, wo],
                           axis=-2)


def _misc_unblock(block):
    lead = block.shape[:-2]
    rows = lambda lo, n: block[..., lo:lo + n, :]
    uq = rows(M_UQ, 256).reshape(lead + (256, 2, HEAD_PAD))[..., :QK_NOPE + QK_ROPE].reshape(lead + (256, 2 * (QK_NOPE + QK_ROPE)))
    keys = rows(M_UKVK, 128).reshape(lead + (128, 2, HEAD_PAD))[..., :QK_NOPE]
    values = rows(M_UKVV, 128)[..., :2 * V_DIM].reshape(lead + (128, 2, V_DIM))
    wo = jnp.swapaxes(rows(M_WO, D_MODEL).reshape(lead + (N_CHIPS, 256, 256)), -3, -2).reshape(lead + (256, D_MODEL))
    return dict(w_out_mla=rows(M_MLA, 512), w_out_pool=rows(M_POOL, 256), w_out_conf=rows(M_CONF, 256), w_out_sc=rows(M_SC, 256), w_uq=uq,
                w_ukv=jnp.concatenate([keys, values], axis=-1).reshape(lead + (128, 256)), w_o=wo)


def _to_chip_blocks(name, a):
    if name == "w_o":
        return a.reshape(a.shape[:-2] + (N_CHIPS, a.shape[-2] // N_CHIPS, a.shape[-1]))
    return jnp.swapaxes(a.reshape(a.shape[:-1] + (N_CHIPS, a.shape[-1] // N_CHIPS)), -3, -2)


def _from_chip_blocks(name, b):
    if name == "w_o":
        return b.reshape(b.shape[:-3] + (N_CHIPS * b.shape[-2], b.shape[-1]))
    s = jnp.swapaxes(b, -3, -2)
    return s.reshape(s.shape[:-2] + (N_CHIPS * s.shape[-1],))


LARGE = ("w_in",) + MISC


def gather_small(shards):
    small = chip_exchange(_pack([shards[n] for n, _, _ in SHARDED_SMALL], F32, 8), True, "gather_small_ici")
    per_chip = [_unpack(small[k], [s for _, s, _ in SHARDED_SMALL]) for k in range(N_CHIPS)]
    return {name: jnp.concatenate([per_chip[k][idx] for k in range(N_CHIPS)], axis=axis) for idx, (name, _, axis) in enumerate(SHARDED_SMALL)}


class LocalWeights:
    def __init__(self, full):
        self.w_in = _to_chip_blocks("w_in", full["w_in"])
        self.misc = _misc_block({n: _to_chip_blocks(n, full[n]) for n in MISC}).astype(BF16)
        self.grads = [None] * DEPTH

    def layer(self, i):
        return self.w_in[i], (self.misc, i)

    def gather_with_attention(self, i):
        return None

    def gathered(self, dsts):
        pass

    def exchange_with_attention(self):
        return None

    def exchanged(self, dsts):
        pass

    def put_grads(self, i, w_in, misc):
        self.grads[i] = (w_in, misc)

    def reduced(self):
        out = {n: _from_chip_blocks(n, b) for n, b in _misc_unblock(jnp.stack([m for _, m in self.grads])).items()}
        out["w_in"] = _from_chip_blocks("w_in", jnp.stack([w for w, _ in self.grads]))
        return out


class MeshWeights:
    def __init__(self, shards, c, chip):
        self.c, self.chip = c, chip
        self.srcs = [shards["w_in"].astype(BF16), _misc_block({n: shards[n] for n in MISC}).astype(BF16)]
        dsts = [lax.empty((DEPTH, N_CHIPS) + s.shape[1:], BF16) for s in self.srcs]
        self.dsts = gather_layer(self.srcs, dsts, 0, "gather_layer")
        self.landed = [lax.empty((DEPTH, N_CHIPS, s.shape[1] // 2, s.shape[2]), BF16) for s in self.srcs]
        self.pending = None

    def layer(self, i):
        if i > 0:
            return self.dsts[0][i], (self.dsts[1], i)
        own = (jnp.arange(N_CHIPS) == self.chip)[:, None, None]
        w_in, misc = (jnp.where(own, s[0][None], d[0]) for s, d in zip(self.srcs, self.dsts))
        return w_in, (misc[None], 0)

    def gather_with_attention(self, i):
        return (self.srcs, self.dsts, i + 1) if i + 1 < DEPTH else None

    def gathered(self, dsts):
        if dsts:
            self.dsts = dsts

    def exchange_with_attention(self):
        return None if self.pending is None else (self.pending[0], self.landed, self.pending[1])

    def exchanged(self, dsts):
        if dsts:
            self.landed, self.pending = dsts, None

    def put_grads(self, i, w_in, misc):
        ps = [w_in.astype(BF16), misc.astype(BF16)]
        rs = swap_row_halves(ps, "reduce_swap")
        self.pending = ([add_row_half(p, r, self.c, "reduce_pair_%d" % a) for a, (p, r) in enumerate(zip(ps, rs))], i)

    def reduced(self):
        landed = exchange_layer(self.pending[0], self.landed, self.pending[1], "reduce_exchange")
        gs = [sum_row_halves(l, self.c, "reduce_sum_%d" % a) for a, l in enumerate(landed)]
        g_in, g_misc = share_row_halves(gs, "reduce_share")
        out = {"w_in": g_in}
        out.update(_misc_unblock(g_misc))
        return out


def reduce_small(grads, chip):
    names = [n for n, _ in REPLICATED] + [n for n, _, _ in SHARDED_SMALL]
    buf = _pack([grads[n] for n in names], F32, 8)
    chip_sum = add_pair(buf, sibling_swap(buf, "reduce_small_d2d"), F32, "reduce_small_pair")
    total = sum_slots(chip_exchange(chip_sum, True, "reduce_small_ici"), "reduce_small_sum")
    out = dict(zip(names, _unpack(total, [grads[n].shape for n in names])))
    for name, shape, axis in SHARDED_SMALL:
        out[name] = lax.dynamic_slice_in_dim(out[name], chip * shape[axis], shape[axis], axis)
    return out


def _prepare_small(w):
    row = lambda a: a[:, None, :]
    conf_vec = jnp.concatenate([row(w["conf_dw_b"]), row(w["conf_ln_g"]), row(w["conf_ln_b"]), jnp.zeros((DEPTH, 5, 256), F32)], axis=1)
    return dict(
        gpre=row(w["pre_norm_g"]), bias=row(w["gate_bias"]), pwbd=_block_diag(w["pool_w"]).astype(BF16), pscale=row(w["pool_scale"]),
        gq=row(w["q_norm_g"]), gkv=row(w["kv_norm_g"]), conf_w=jnp.pad(w["conf_dw_w"].astype(F32), ((0, 0), (0, 32 - CONF_K), (0, 0))),
        conf_vec=conf_vec, sc_w=jnp.pad(w["sc_dw_w"].astype(F32), ((0, 0), (0, 8 - SC_K), (0, 0))), gpost=row(w["post_norm_g"]))


def _prepare_layer(w_in_blocks, misc):
    w_br, w_gl = _input_weights(w_in_blocks)
    one = lambda a: a.astype(BF16)[None]
    return dict(w_br=one(w_br), w_gl=one(w_gl), misc=misc)


def local_step(x, target, w, large):
    seq = x.shape[0]
    length = N_META + seq
    rows = -(-length // ROW_TILE) * ROW_TILE
    bt = _big_tile(rows)
    hres = _pad_rows(jnp.concatenate([w["meta_tokens"].astype(F32), x], axis=0), rows)
    tgt = jnp.pad(target, ((N_META, rows - length), (0, 0)))
    rope = _rope_tables(rows)
    sw = _prepare_small(w)

    saved = []
    for i in range(DEPTH):
        lw = _prepare_layer(*large.layer(i))
        z_br, hb = prenorm_project(hres, sw["gpre"], lw["w_br"], i)
        z_gl = matmul(hb, lw["w_gl"], "nn", BF16, bt, 1024, D_MODEL, "project_gates", b_layer=0)
        ua, uc, ud, q, k, v = branches_fwd(z_br, rope, sw["pwbd"], sw["pscale"], sw["gq"], sw["gkv"], lw["misc"], sw["conf_w"],
                                           sw["conf_vec"], sw["sc_w"], i)
        o_att, lse, dsts = attention_fwd(q, k, v, large.gather_with_attention(i))
        large.gathered(dsts)
        ub, mb, o, hnew = merge_fwd(ua, o_att, uc, ud, z_br, z_gl, sw["bias"], lw["misc"], sw["gpost"], hres, i)
        saved.append(dict(lw=lw, hres=hres, hb=hb, z_br=z_br, z_gl=z_gl, ua=ua, ub=ub, uc=uc, ud=ud, q=q, k=k, v=v, o_att=o_att,
                          lse=lse, mb=mb, o=o))
        hres = hnew

    dh, total = loss_head(hres, tgt, seq)

    g = {n: [None] * DEPTH for n in ("gpre", "bias", "pwbd", "pscale", "gq", "gkv", "conf_w", "conf_vec", "sc_w", "gpost")}
    for i in reversed(range(DEPTH)):
        s = saved[i]
        lw = s["lw"]
        dm, dwo, g["gpost"][i] = postnorm_bwd(dh, s["o"], s["mb"], lw["misc"], sw["gpost"], i)
        dua, dub, duc, dud, dz_gl, dwout, g["bias"][i] = merge_bwd(dm, s["ua"], s["ub"], s["uc"], s["ud"], s["z_gl"], sw["bias"],
                                                                 lw["misc"], i)
        dz_br = lax.empty((rows, ZB), BF16)
        dz_br, g["pwbd"][i], g["pscale"][i] = pool_bwd(s["z_br"], dua, sw["pwbd"], sw["pscale"], dz_br, i)
        dz_br, g["sc_w"][i] = shortconv_bwd(s["z_br"], dud, sw["sc_w"], dz_br, i)
        dc, dz_br, g["conf_vec"][i] = conformer_bwd_tail(s["z_br"], duc, sw["conf_w"], sw["conf_vec"], dz_br, i)
        dz_br, g["conf_w"][i] = conformer_bwd_conv(s["z_br"], dc, sw["conf_w"], dz_br, i)
        do, dz_br, delta = attention_bwd_prep(dub, s["o_att"], s["z_br"], dz_br)
        dq, dk, dv, dsts = attention_bwd(s["q"], s["k"], s["v"], do, s["lse"], delta, large.exchange_with_attention())
        large.exchanged(dsts)
        dz_br, dwup, g["gq"][i], g["gkv"][i] = mla_prep_bwd(dq, dk, dv, s["z_br"], rope, sw["gq"], sw["gkv"], lw["misc"], dz_br, i)
        dw_br = matmul(s["hb"], dz_br, "tn", BF16, D_MODEL, ZB // 2, bt, "grad_w_branch")
        dw_gl = matmul(s["hb"], dz_gl, "tn", BF16, D_MODEL, 1024, bt, "grad_w_gates")
        dh_gl = matmul(dz_gl, lw["w_gl"], "nt", F32, bt, D_MODEL, 1024, "grad_h_gates", b_layer=0)
        dh, g["gpre"][i] = prenorm_bwd(dz_br, lw["w_br"], dh_gl, s["hres"], sw["gpre"], dh, i)
        gap = jnp.zeros((N_CHIPS, M_UQ - M_SC - 256, 256), F32)
        large.put_grads(i, _input_weights_inverse(dw_br, dw_gl), jnp.concatenate([dwout, gap, dwup, dwo], axis=1))

    g = {n: jnp.stack(parts) for n, parts in g.items()}
    grads = dict(
        meta_tokens=dh[:N_META], pre_norm_g=g["gpre"][:, 0], gate_bias=g["bias"][:, 0], pool_w=_block_diag_inverse(g["pwbd"]),
        pool_scale=g["pscale"][:, 0], q_norm_g=g["gq"][:, 0], kv_norm_g=g["gkv"][:, 0], conf_dw_w=g["conf_w"][:, :CONF_K],
        conf_dw_b=g["conf_vec"][:, 2], conf_ln_g=g["conf_vec"][:, 0], conf_ln_b=g["conf_vec"][:, 1], sc_dw_w=g["sc_w"][:, :SC_K],
        post_norm_g=g["gpost"][:, 0])
    return total[0, 0], dh[N_META:length], grads


def kernel(x, meta_tokens, pre_norm_g, w_in, gate_bias, pool_w, pool_scale, w_out_pool, q_norm_g, w_uq, kv_norm_g, w_ukv, w_out_mla, conf_dw_w, conf_dw_b, conf_ln_g, conf_ln_b, w_out_conf, sc_dw_w, w_out_sc, w_o, post_norm_g, loss_target, m_meta_tokens, m_pre_norm_g, m_w_in, m_gate_bias, m_pool_w, m_pool_scale, m_w_out_pool, m_q_norm_g, m_w_uq, m_kv_norm_g, m_w_ukv, m_w_out_mla, m_conf_dw_w, m_conf_dw_b, m_conf_ln_g, m_conf_ln_b, m_w_out_conf, m_sc_dw_w, m_w_out_sc, m_w_o, m_post_norm_g, v_meta_tokens, v_pre_norm_g, v_w_in, v_gate_bias, v_pool_w, v_pool_scale, v_w_out_pool, v_q_norm_g, v_w_uq, v_kv_norm_g, v_w_ukv, v_w_out_mla, v_conf_dw_w, v_conf_dw_b, v_conf_ln_g, v_conf_ln_b, v_w_out_conf, v_sc_dw_w, v_w_out_sc, v_w_o, v_post_norm_g):
    args = locals()
    weights = {n: args[n] for n in WEIGHT_ORDER}
    c = lax.axis_index("c")
    chip = 2 * lax.axis_index("x") + lax.axis_index("y")

    small = {n: weights[n] for n, _ in REPLICATED}
    small.update(gather_small(weights))
    large = MeshWeights(weights, c, chip)
    total, dx, grads = local_step(x[0], loss_target[0], small, large)
    loss = lax.psum(total * (0.5 / D_MODEL), ("x", "y", "c"))

    reduced = large.reduced()
    reduced.update(reduce_small(grads, chip))

    flip = lambda a: jnp.swapaxes(a, 1, 2)
    deltas, new_m, new_v = [], [], []
    for n in WEIGHT_ORDER:
        operands = (weights[n], reduced[n], args["m_" + n], args["v_" + n])
        if n == "w_in":
            operands = (flip(operands[0]), lax.optimization_barrier(flip(operands[1])), flip(operands[2]), flip(operands[3]))
            reduced[n] = flip(operands[1])
        d, nm, nv = adamw(*operands)
        if n == "w_in":
            d, nm, nv = flip(d), flip(nm), flip(nv)
        deltas.append(d)
        new_m.append(nm)
        new_v.append(nv)
    return (loss, dx[None], *[reduced[n] for n in WEIGHT_ORDER], *deltas, *new_m, *new_v)
```

```python
import functools
import math

import jax
import jax.numpy as jnp
from jax import lax
from jax.experimental import pallas as pl
from jax.experimental.pallas import tpu as pltpu

F32 = jnp.float32
BF16 = jnp.bfloat16

D_MODEL = 1024
DEPTH = 4
N_META = 16
EPS = 1e-6
HEADS = 8
QK_NOPE = 64
QK_ROPE = 32
V_DIM = 64
HEAD_PAD = 128
ROPE_THETA = 10000.0
Q_SCALE = (QK_NOPE + QK_ROPE) ** -0.5
CONF_K = 31
SC_K = 3
IN_W = 7328
N_CHIPS = 4

ZB = 3328
ZG = 4096
BG, C2, XV, SG, PV, PG, CQ, CKV, KR, MG, CA, CGT, CG = (0, 256, 512, 768, 1024, 1280, 1536, 1792, 1920, 2048, 2560, 2816, 3072)

KEY_GROUP = 4
ROW_TILE = 384
HALO = 32
LANES = 128
VMEM_LIMIT = 56 * 1024 * 1024

ADAM_LR = 0.001
ADAM_B1 = 0.9
ADAM_B2 = 0.999
ADAM_EPS = 1e-08
ADAM_WD = 0.01
ADAM_STEP = 10

MESH = pl.DeviceIdType.MESH
ANY = pl.BlockSpec(memory_space=pl.ANY)

MISC = ("w_out_mla", "w_out_pool", "w_out_conf", "w_out_sc", "w_uq", "w_ukv", "w_o")
M_MLA, M_POOL, M_CONF, M_SC, M_UQ, M_UKVK, M_UKVV, M_WO, MISC_ROWS = 0, 512, 768, 1024, 1536, 1792, 1920, 2048, 3072
SHARDED_SMALL = (
    ("meta_tokens", (N_META, 256), 1),
    ("conf_dw_w", (DEPTH, CONF_K, 64), 2),
    ("sc_dw_w", (DEPTH, SC_K, 64), 2),
)
REPLICATED = (
    ("pre_norm_g", (DEPTH, D_MODEL)),
    ("gate_bias", (DEPTH, 4 * D_MODEL)),
    ("pool_w", (DEPTH, 4, 64, 64)),
    ("pool_scale", (DEPTH, 256)),
    ("q_norm_g", (DEPTH, 256)),
    ("kv_norm_g", (DEPTH, 128)),
    ("conf_dw_b", (DEPTH, 256)),
    ("conf_ln_g", (DEPTH, 256)),
    ("conf_ln_b", (DEPTH, 256)),
    ("post_norm_g", (DEPTH, D_MODEL)),
)
WEIGHT_ORDER = ("meta_tokens", "pre_norm_g", "w_in", "gate_bias", "pool_w", "pool_scale", "w_out_pool", "q_norm_g", "w_uq",
                "kv_norm_g", "w_ukv", "w_out_mla", "conf_dw_w", "conf_dw_b", "conf_ln_g", "conf_ln_b", "w_out_conf", "sc_dw_w",
                "w_out_sc", "w_o", "post_norm_g")


def _dot(a, b):
    return lax.dot_general(a, b, (((1,), (0,)), ((), ())), preferred_element_type=F32)


def _dot_nt(a, b):
    return lax.dot_general(a, b, (((1,), (1,)), ((), ())), preferred_element_type=F32)


def _dot_tn(a, b):
    return lax.dot_general(a, b, (((0,), (0,)), ((), ())), preferred_element_type=F32)


def _sigmoid(x):
    return jax.nn.sigmoid(x)


def _silu(x):
    return x * _sigmoid(x)


def _silu_grad(x):
    s = _sigmoid(x)
    return s * (1.0 + x * (1.0 - s))


def _rms(x, g):
    return x * lax.rsqrt(jnp.mean(x * x, axis=-1, keepdims=True) + EPS) * g


def _sh(x, d):
    return x if d == 0 else pltpu.roll(x, d, 0)


def _ash(x, d):
    return x if d == 0 else pltpu.roll(x, x.shape[0] - d, 0)


def _lanes8(t):
    return jnp.concatenate([t] * HEADS, axis=1)


def _pool_window_sums(v, shift):
    a2 = v + shift(v, 1)
    a4 = a2 + shift(a2, 2)
    a8 = a4 + shift(a4, 4)
    a16 = a8 + shift(a8, 8)
    lane = lax.broadcasted_iota(jnp.int32, v.shape, 1)
    return jnp.where(lane < 64, a2, jnp.where(lane < 128, a4, jnp.where(lane < 192, a8, a16)))


def _pool_counts(first_row, rows):
    pos = first_row + lax.broadcasted_iota(jnp.int32, (rows, 256), 0)
    lane = lax.broadcasted_iota(jnp.int32, (rows, 256), 1)
    width = jnp.where(lane < 64, 2, jnp.where(lane < 128, 4, jnp.where(lane < 192, 8, 16)))
    return jnp.maximum(jnp.minimum(pos + 1, width), 1).astype(F32)


def _params(sem=None):
    return pltpu.CompilerParams(dimension_semantics=sem, vmem_limit_bytes=VMEM_LIMIT)


def _tile_specs(t, n_halo_blocks, li=0):
    per = t // HALO

    def layer(shape, idx=li):
        return pl.BlockSpec((None,) + tuple(shape), lambda i: (idx,) + (0,) * len(shape))

    def cur(c, cb=0):
        return pl.BlockSpec((t, c), lambda i: (i, cb))

    def prev(c, cb=0):
        return pl.BlockSpec((HALO, c), lambda i: (jnp.maximum(i * per - 1, 0), cb))

    def nxt(c, cb=0):
        return pl.BlockSpec((HALO, c), lambda i: (jnp.minimum((i + 1) * per, n_halo_blocks - 1), cb))

    def full(shape):
        return pl.BlockSpec(shape, lambda i: (0,) * len(shape))

    return cur, prev, nxt, full, layer


def _big_tile(rows):
    return rows // 3 if rows % (3 * LANES) == 0 else ROW_TILE


def matmul(a, b, mode, out_dtype, tm, tn, tk, name, b_layer=None):
    bs = b.shape if b_layer is None else b.shape[1:]
    lead = () if b_layer is None else (None,)
    pick = (lambda *ix: ix) if b_layer is None else (lambda *ix: (b_layer,) + ix)
    if mode == "nn":
        (m, k), n = a.shape, bs[1]
        a_spec = pl.BlockSpec((tm, tk), lambda i, j, kk: (i, kk))
        b_spec = pl.BlockSpec(lead + (tk, tn), lambda i, j, kk: pick(kk, j))
        dot = _dot
    elif mode == "nt":
        (m, k), n = a.shape, bs[0]
        a_spec = pl.BlockSpec((tm, tk), lambda i, j, kk: (i, kk))
        b_spec = pl.BlockSpec(lead + (tn, tk), lambda i, j, kk: pick(j, kk))
        dot = _dot_nt
    else:
        (k, m), n = a.shape, bs[1]
        a_spec = pl.BlockSpec((tk, tm), lambda i, j, kk: (kk, i))
        b_spec = pl.BlockSpec(lead + (tk, tn), lambda i, j, kk: pick(kk, j))
        dot = _dot_tn
    assert m % tm == 0 and n % tn == 0 and k % tk == 0, (a.shape, bs, tm, tn, tk)
    nk = k // tk

    def body(a_ref, b_ref, o_ref, acc_ref):
        kk = pl.program_id(2)

        @pl.when(kk == 0)
        def _():
            acc_ref[...] = jnp.zeros_like(acc_ref)

        acc_ref[...] += dot(a_ref[...], b_ref[...])

        @pl.when(kk == nk - 1)
        def _():
            o_ref[...] = acc_ref[...].astype(out_dtype)

    return pl.pallas_call(
        body, name=name, grid=(m // tm, n // tn, nk), in_specs=[a_spec, b_spec],
        out_specs=pl.BlockSpec((tm, tn), lambda i, j, kk: (i, j)), out_shape=jax.ShapeDtypeStruct((m, n), out_dtype),
        scratch_shapes=[pltpu.VMEM((tm, tn), F32)], compiler_params=_params(("parallel", "parallel", "arbitrary")),
    )(a, b)


def prenorm_project(hres, g, w, li):
    rows, d = hres.shape
    n = w.shape[2]
    tm, tn = _big_tile(rows), n // 2

    def body(x_ref, g_ref, w_ref, z_ref, hb_ref):
        @pl.when(pl.program_id(1) == 0)
        def _():
            hb_ref[...] = _rms(x_ref[...], g_ref[...]).astype(BF16)

        z_ref[...] = _dot(hb_ref[...], w_ref[...]).astype(BF16)

    return pl.pallas_call(
        body, name="prenorm_project", grid=(rows // tm, n // tn),
        in_specs=[pl.BlockSpec((tm, d), lambda i, j: (i, 0)), pl.BlockSpec((None, 1, d), lambda i, j: (li, 0, 0)),
                  pl.BlockSpec((None, d, tn), lambda i, j: (0, 0, j))],
        out_specs=[pl.BlockSpec((tm, tn), lambda i, j: (i, j)), pl.BlockSpec((tm, d), lambda i, j: (i, 0))],
        out_shape=[jax.ShapeDtypeStruct((rows, n), BF16), jax.ShapeDtypeStruct((rows, d), BF16)],
        compiler_params=_params(("parallel", "arbitrary")),
    )(hres, g, w)


def _rope(q, c, s1, s2, width):
    return q * c + pltpu.roll(q, width - 16, 1) * s1 + pltpu.roll(q, 16, 1) * s2


def _rope_transposed(dq, c, s1, s2, width):
    return dq * c + pltpu.roll(dq * s1, 16, 1) + pltpu.roll(dq * s2, width - 16, 1)


def _conf_conv(g1, w_ref):
    acc = jnp.zeros_like(g1)
    for k in range(CONF_K):
        acc = acc + w_ref[k:k + 1, :] * _sh(g1, CONF_K - 1 - k)
    return acc


def _conf_tail(c, cg, lg, lb):
    mu = jnp.mean(c, axis=-1, keepdims=True)
    xc = c - mu
    var = jnp.mean(xc * xc, axis=-1, keepdims=True)
    n = xc * lax.rsqrt(var + EPS) * lg + lb
    return _silu(n) * _silu(cg)


def _misc_spec(misc, row0, rows):
    assert row0 % rows == 0
    return pl.BlockSpec((None, N_CHIPS, rows, 256), lambda i: (misc[1], 0, row0 // rows, 0))


def _chip_columns(x, w_ref, row0, rows, lanes=256):
    return jnp.concatenate([_dot(x, w_ref[k, row0:row0 + rows, 0:lanes]) for k in range(N_CHIPS)], axis=1)


def branches_fwd(z_br, rope, pwbd, pscale, gq, gkv, misc, conf_w, conf_vec, sc_w, li):
    rows = z_br.shape[0]
    t = ROW_TILE
    cur, prev, _, _, layer = _tile_specs(t, rows // HALO, li)

    def body(zc_ref, zp_ref, rope_ref, pw_ref, ps_ref, gq_ref, gkv_ref, up_ref, cw_ref, cv_ref, sw_ref,
             ua_ref, uc_ref, ud_ref, q_ref, k_ref, v_ref):
        i = pl.program_id(0)
        zp = jnp.where(i == 0, jnp.zeros(zp_ref.shape, zp_ref.dtype), zp_ref[...])

        def ext(lo, w=256):
            return jnp.concatenate([zp[:, lo:lo + w], zc_ref[:, lo:lo + w]], axis=0).astype(F32)

        def col(lo, w=256):
            return zc_ref[:, lo:lo + w].astype(F32)

        v = ext(PV)
        p = (_pool_window_sums(v, _sh) / _pool_counts(i * t - HALO, t + HALO) - v)[HALO:]
        ya = _dot(p.astype(BF16), pw_ref[...]) * ps_ref[...]
        ua_ref[...] = (ya * _silu(col(PG))).astype(BF16)

        g1 = ext(CA) * _sigmoid(ext(CGT))
        c = _conf_conv(g1, cw_ref)[HALO:] + cv_ref[0:1, :]
        uc_ref[...] = _conf_tail(c, col(CG), cv_ref[1:2, :], cv_ref[2:3, :]).astype(BF16)

        e = ext(C2) * ext(XV)
        f = jnp.zeros_like(e)
        for k in range(SC_K):
            f = f + sw_ref[k:k + 1, :] * _sh(e, SC_K - 1 - k)
        ud_ref[...] = (col(BG) * f[HALO:] * _silu(col(SG))).astype(BF16)

        cth, s1, s2 = rope_ref[:, 0:128], rope_ref[:, 128:256], rope_ref[:, 256:384]
        qn = _rms(col(CQ), gq_ref[...]).astype(BF16)
        q = _chip_columns(qn, up_ref, 0, 256)
        w8 = HEADS * HEAD_PAD
        q_ref[...] = (_rope(q, _lanes8(cth), _lanes8(s1), _lanes8(s2), w8) * Q_SCALE).astype(BF16)
        kvn = _rms(col(CKV, 128), gkv_ref[...]).astype(BF16)
        kr = _rope(col(KR, 128), cth, s1, s2, HEAD_PAD)
        k_ref[...] = (_chip_columns(kvn, up_ref, M_UKVK - M_UQ, 128) + _lanes8(kr)).astype(BF16)
        v_ref[...] = _chip_columns(kvn, up_ref, M_UKVV - M_UQ, 128, 2 * V_DIM).astype(BF16)

    outs = [jax.ShapeDtypeStruct((rows, 256), BF16)] * 3 + [jax.ShapeDtypeStruct((rows, 1024), BF16)] * 2 + [
        jax.ShapeDtypeStruct((rows, 512), BF16)]
    return pl.pallas_call(
        body, name="branches_fwd", grid=(rows // t,),
        in_specs=[cur(ZB), prev(ZB), cur(384), layer((256, 256)), layer((1, 256)), layer((1, 256)), layer((1, 128)),
                  _misc_spec(misc, M_UQ, M_WO - M_UQ), layer((32, 256)), layer((8, 256)), layer((8, 256))],
        out_specs=[cur(256), cur(256), cur(256), cur(1024), cur(1024), cur(512)], out_shape=outs,
        compiler_params=_params(("parallel",)),
    )(z_br, z_br, rope, pwbd, pscale, gq, gkv, misc[0], conf_w, conf_vec, sc_w)


def _head_lane_mask(h):
    lane = lax.broadcasted_iota(jnp.int32, (1, 2 * V_DIM), 1)
    return (lane >= V_DIM * h) & (lane < V_DIM * (h + 1))


def attention_fwd(q, k, v, gather=None):
    rows = q.shape[0]
    tq = ROW_TILE
    nq = rows // tq
    n = 0 if gather is None else len(gather[0])

    def body(*refs):
        if n:
            start, finish = _gather_ops(refs[3:3 + n], refs[5 + 2 * n:5 + 3 * n], refs[5 + 3 * n:], gather[2], True)
            pl.when((pl.program_id(0) == 0) & (pl.program_id(1) == 0))(start)
        compute(*refs[:3], *refs[3 + 2 * n:5 + 2 * n])
        if n:
            pl.when((pl.program_id(0) == HEADS // 2 - 1) & (pl.program_id(1) == nq - 1))(finish)

    def compute(q_ref, k_ref, v_ref, o_ref, lse_ref):
        i = pl.program_id(1)

        def head_step(h, tile, n_tiles, carry, masked):
            m, l, acc = carry
            width = n_tiles * tq
            r0 = pl.multiple_of(tile * tq, tq)
            kh = k_ref[pl.ds(r0, width), HEAD_PAD * h:HEAD_PAD * (h + 1)]
            vh = jnp.where(_head_lane_mask(h), v_ref[pl.ds(r0, width), :], jnp.zeros((), BF16))
            s = _dot_nt(q_ref[:, HEAD_PAD * h:HEAD_PAD * (h + 1)], kh)
            if masked:
                row = lax.broadcasted_iota(jnp.int32, (tq, width), 0)
                colm = lax.broadcasted_iota(jnp.int32, (tq, width), 1)
                s = jnp.where(colm <= row + (width - tq), s, -1e30)
            m2 = jnp.maximum(m, jnp.max(s, axis=-1, keepdims=True))
            alpha = jnp.exp(m - m2)
            pr = jnp.exp(s - m2)
            return m2, alpha * l + jnp.sum(pr, axis=-1, keepdims=True), alpha * acc + _dot(pr.astype(BF16), vh)

        def step(tile, n_tiles, carry, masked):
            return tuple(head_step(h, tile, n_tiles, carry[h], masked) for h in range(2))

        init = (jnp.full((tq, 1), -1e30, F32), jnp.zeros((tq, 1), F32), jnp.zeros((tq, 2 * V_DIM), F32))
        group = min(KEY_GROUP, nq)
        carry = lax.fori_loop(0, i // group, lambda t, cr: step(group * t, group, cr, False), (init, init))
        carry = lax.switch(i % group, [functools.partial(lambda cr, r: step(i - r, r + 1, cr, True), r=r) for r in range(group)], carry)
        out = jnp.zeros((tq, 2 * V_DIM), F32)
        for h, (m, l, acc) in enumerate(carry):
            out = out + acc / l
            lse_ref[h] = jnp.broadcast_to(m + jnp.log(l), (tq, LANES))
        o_ref[...] = out.astype(BF16)

    srcs, dsts = ([], []) if gather is None else (list(gather[0]), list(gather[1]))
    outs = pl.pallas_call(
        body, name="attention_fwd" if gather is None else "attention_fwd_gather", grid=(HEADS // 2, nq),
        in_specs=[pl.BlockSpec((tq, 2 * HEAD_PAD), lambda p, i: (i, p)), pl.BlockSpec((rows, 2 * HEAD_PAD), lambda p, i: (0, p)),
                  pl.BlockSpec((rows, 2 * V_DIM), lambda p, i: (0, p))] + [ANY] * (2 * n),
        out_specs=[pl.BlockSpec((tq, 2 * V_DIM), lambda p, i: (i, p)), pl.BlockSpec((2, tq, LANES), lambda p, i: (p, i, 0))] + [ANY] * n,
        out_shape=[jax.ShapeDtypeStruct((rows, HEADS * V_DIM), BF16), jax.ShapeDtypeStruct((HEADS, rows, LANES), F32)] + [
            jax.ShapeDtypeStruct(d.shape, d.dtype) for d in dsts],
        input_output_aliases={3 + n + a: 2 + a for a in range(n)}, scratch_shapes=GATHER_SEMS(n) if n else [],
        compiler_params=_params(("arbitrary", "arbitrary") if n else ("parallel", "parallel")),
    )(q, k, v, *srcs, *dsts)
    return outs[0], outs[1], list(outs[2:])


OUT_PROJECTIONS = ((M_POOL, 256), (M_MLA, 512), (M_CONF, 256), (M_SC, 256))


def _chunks(x, n=N_CHIPS, width=256):
    return [x[:, width * k:width * (k + 1)] for k in range(n)]


def merge_fwd(ua, o_att, uc, ud, z_br, z_gl, bias, misc, gpost, hres, li):
    rows = hres.shape[0]
    t = ROW_TILE
    cur, _, _, _, layer = _tile_specs(t, rows // HALO, li)
    d = D_MODEL

    def body(ua_ref, ob_ref, uc_ref, ud_ref, mg_ref, gl_ref, b_ref, wout_ref, wo_ref, gp_ref, h_ref, ub_ref, mb_ref, o_ref, hn_ref):
        ub = (ob_ref[...].astype(F32) * _silu(mg_ref[...].astype(F32))).astype(BF16)
        ub_ref[...] = ub
        m = jnp.zeros((t, d), F32)
        for idx, (u, (row0, n)) in enumerate(zip((ua_ref[...], ub, uc_ref[...], ud_ref[...]), OUT_PROJECTIONS)):
            gate = _sigmoid(gl_ref[:, d * idx:d * (idx + 1)].astype(F32) + b_ref[:, d * idx:d * (idx + 1)])
            m = m + gate * _chip_columns(u, wout_ref, row0, n)
        mb = m.astype(BF16)
        mb_ref[...] = mb
        o = jnp.concatenate([sum(_dot(mk, wo_ref[k, 256 * j:256 * (j + 1), :]) for k, mk in enumerate(_chunks(mb)))
                             for j in range(N_CHIPS)], axis=1)
        o_ref[...] = o
        hn_ref[...] = h_ref[...] + _rms(o, gp_ref[...])

    return pl.pallas_call(
        body, name="merge_fwd", grid=(rows // t,),
        in_specs=[cur(256), cur(512), cur(256), cur(256), cur(512, MG // 512), cur(ZG), layer((1, ZG)), _misc_spec(misc, 0, 1280),
                  _misc_spec(misc, M_WO, D_MODEL), layer((1, d)), cur(d)],
        out_specs=[cur(512), cur(d), cur(d), cur(d)],
        out_shape=[jax.ShapeDtypeStruct((rows, 512), BF16), jax.ShapeDtypeStruct((rows, d), BF16), jax.ShapeDtypeStruct((rows, d), F32),
                   jax.ShapeDtypeStruct((rows, d), F32)],
        compiler_params=_params(("parallel",)),
    )(ua, o_att, uc, ud, z_br, z_gl, bias, misc[0], misc[0], gpost, hres)


def loss_head(hres, target, n_tokens):
    rows, d = hres.shape
    t = ROW_TILE
    cur, _, _, full, _ = _tile_specs(t, rows // HALO)
    n_steps = rows // t

    def body(h_ref, t_ref, dh_ref, tot_ref, acc_ref):
        i = pl.program_id(0)

        @pl.when(i == 0)
        def _():
            acc_ref[...] = jnp.zeros_like(acc_ref)

        r = i * t + lax.broadcasted_iota(jnp.int32, (t, 1), 0)
        diff = jnp.where((r >= N_META) & (r < N_META + n_tokens), h_ref[...] - t_ref[...], 0.0)
        dh_ref[...] = diff * (1.0 / d)
        acc_ref[...] += jnp.sum(diff * diff, axis=0, keepdims=True)

        @pl.when(i == n_steps - 1)
        def _():
            tot_ref[...] = jnp.broadcast_to(jnp.sum(acc_ref[...], axis=1, keepdims=True), (1, LANES))

    return pl.pallas_call(
        body, name="loss_head", grid=(n_steps,), in_specs=[cur(d), cur(d)], out_specs=[cur(d), full((1, LANES))],
        out_shape=[jax.ShapeDtypeStruct((rows, d), F32), jax.ShapeDtypeStruct((1, LANES), F32)],
        scratch_shapes=[pltpu.VMEM((1, d), F32)], compiler_params=_params(("arbitrary",)),
    )(hres, target)


def _accumulate(i, ref, value):
    @pl.when(i == 0)
    def _():
        ref[...] = value

    @pl.when(i > 0)
    def _():
        ref[...] += value


def postnorm_bwd(dh, o, mb, misc, gpost, li, swap=None):
    rows, d = dh.shape
    t = ROW_TILE
    cur, _, _, full, layer = _tile_specs(t, rows // HALO, li)
    n = 0 if swap is None else len(swap)
    steps = rows // t

    def body(*refs):
        if n:
            start, finish = _swap_ops(refs[5:5 + n], refs[8 + n:8 + 2 * n], refs[8 + 2 * n:])
            pl.when(pl.program_id(0) == 0)(start)
        compute(*refs[:5], *refs[5 + n:8 + n])
        if n:
            pl.when(pl.program_id(0) == steps - 1)(finish)

    def compute(dh_ref, o_ref, mb_ref, wo_ref, gp_ref, dm_ref, dwo_ref, dgp_ref):
        i = pl.program_id(0)
        _, vjp = jax.vjp(_rms, o_ref[...], gp_ref[...])
        do, dg = vjp(dh_ref[...])
        dob = do.astype(BF16)
        dm_ref[...] = jnp.concatenate([sum(_dot_nt(dj, wo_ref[k, 256 * j:256 * (j + 1), :]) for j, dj in enumerate(_chunks(dob)))
                                       for k in range(N_CHIPS)], axis=1)
        dwo = _dot_tn(mb_ref[...], dob)
        for k in range(N_CHIPS):
            _accumulate(i, dwo_ref.at[k], jnp.concatenate(_chunks(dwo[256 * k:256 * (k + 1), :]), axis=0))
        _accumulate(i, dgp_ref, dg)

    sent = [] if swap is None else list(swap)
    outs = pl.pallas_call(
        body, name="postnorm_bwd" if swap is None else "postnorm_bwd_swap", grid=(steps,),
        in_specs=[cur(d), cur(d), cur(d), _misc_spec(misc, M_WO, d), layer((1, d))] + [ANY] * n,
        out_specs=[cur(d), full((N_CHIPS, d, 256)), full((1, d))] + [ANY] * n,
        out_shape=[jax.ShapeDtypeStruct((rows, d), F32), jax.ShapeDtypeStruct((N_CHIPS, d, 256), F32), jax.ShapeDtypeStruct((1, d), F32)] + [
            jax.ShapeDtypeStruct((p.shape[0], p.shape[1] // 2, p.shape[2]), p.dtype) for p in sent],
        scratch_shapes=[pltpu.SemaphoreType.DMA((n,)), pltpu.SemaphoreType.DMA((n,))] if n else [],
        compiler_params=_params(("arbitrary",)),
    )(dh, o, mb, misc[0], gpost, *sent)
    return outs[0], outs[1], outs[2], list(outs[3:])


def merge_bwd(dm, ua, ub, uc, ud, z_gl, bias, misc, li):
    rows, d = dm.shape
    t = ROW_TILE
    cur, _, _, full, layer = _tile_specs(t, rows // HALO, li)
    widths = (256, 512, 256, 256)

    def body(dm_ref, ua_ref, ub_ref, uc_ref, ud_ref, gl_ref, b_ref, w_ref, dua_ref, dub_ref, duc_ref, dud_ref, dgl_ref, dw_ref, db_ref):
        i = pl.program_id(0)
        dm = dm_ref[...]
        groups = ((ua_ref, dua_ref), (ub_ref, dub_ref), (uc_ref, duc_ref), (ud_ref, dud_ref))
        for idx, ((u_ref, du_ref), (row0, n)) in enumerate(zip(groups, OUT_PROJECTIONS)):
            cols = slice(d * idx, d * (idx + 1))
            u = u_ref[...]
            gate = _sigmoid(gl_ref[:, cols].astype(F32) + b_ref[:, cols])
            dgl = dm * _chip_columns(u, w_ref, row0, n) * gate * (1.0 - gate)
            dgl_ref[:, cols] = dgl.astype(BF16)
            _accumulate(i, db_ref.at[:, cols], jnp.sum(dgl, axis=0, keepdims=True))
            dyb = (dm * gate).astype(BF16)
            du_ref[...] = sum(_dot_nt(dyk, w_ref[k, row0:row0 + n, :]) for k, dyk in enumerate(_chunks(dyb)))
            for k, dwk in enumerate(_chunks(_dot_tn(u, dyb))):
                _accumulate(i, dw_ref.at[k, row0:row0 + n, :], dwk)

    return pl.pallas_call(
        body, name="merge_bwd", grid=(rows // t,),
        in_specs=[cur(d), cur(256), cur(512), cur(256), cur(256), cur(ZG), layer((1, ZG)), _misc_spec(misc, 0, 1280)],
        out_specs=[cur(256), cur(512), cur(256), cur(256), cur(ZG), full((N_CHIPS, 1280, 256)), full((1, ZG))],
        out_shape=[jax.ShapeDtypeStruct((rows, w), F32) for w in widths] + [jax.ShapeDtypeStruct((rows, ZG), BF16),
                                                                            jax.ShapeDtypeStruct((N_CHIPS, 1280, 256), F32),
                                                                            jax.ShapeDtypeStruct((1, ZG), F32)],
        compiler_params=_params(("arbitrary",)),
    )(dm, ua, ub, uc, ud, z_gl, bias, misc[0])


def pool_bwd(z_br, dua, pwbd, pscale, dz_buf, li):
    rows = z_br.shape[0]
    t = ROW_TILE
    n_steps = rows // t
    cur, prev, nxt, full, layer = _tile_specs(t, rows // HALO, li)

    def body(zc_ref, zp_ref, zn_ref, dc_ref, dn_ref, pw_ref, ps_ref, _, dz_ref, dpw_ref, dps_ref):
        i = pl.program_id(0)
        zp = jnp.where(i == 0, jnp.zeros(zp_ref.shape, zp_ref.dtype), zp_ref[...])
        zn = jnp.where(i == n_steps - 1, jnp.zeros(zn_ref.shape, zn_ref.dtype), zn_ref[...])
        dun = jnp.where(i == n_steps - 1, jnp.zeros(dn_ref.shape, dn_ref.dtype), dn_ref[...])

        def ext(lo):
            return jnp.concatenate([zp[:, lo:lo + 256], zc_ref[:, lo:lo + 256], zn[:, lo:lo + 256]], axis=0).astype(F32)

        n_ext = t + 2 * HALO
        v, pg = ext(PV), ext(PG)
        cnt = _pool_counts(i * t - HALO, n_ext)
        p = (_pool_window_sums(v, _sh) / cnt - v)[HALO:HALO + t]
        du = jnp.concatenate([jnp.zeros((HALO, 256), F32), dc_ref[...], dun], axis=0)
        dya = du * _silu(pg)
        dypb = (dya * ps_ref[...]).astype(BF16)
        dp = _dot_nt(dypb, pw_ref[...])
        dv = (_pool_window_sums(dp / cnt, _ash) - dp)[HALO:HALO + t]
        pb = p.astype(BF16)
        pw = _dot(pb, pw_ref[...])
        duc, pgc = dc_ref[...], pg[HALO:HALO + t]
        dpg = duc * pw * ps_ref[...] * _silu_grad(pgc)
        dz_ref[...] = jnp.concatenate([dv, dpg], axis=1).astype(BF16)
        _accumulate(i, dpw_ref, _dot_tn(pb, dypb[HALO:HALO + t]))
        _accumulate(i, dps_ref, jnp.sum(dya[HALO:HALO + t] * pw, axis=0, keepdims=True))

    return pl.pallas_call(
        body, name="pool_bwd", grid=(n_steps,),
        in_specs=[cur(ZB), prev(ZB), nxt(ZB), cur(256), nxt(256), layer((256, 256)), layer((1, 256)), ANY],
        out_specs=[cur(512, PV // 512), full((256, 256)), full((1, 256))],
        out_shape=[jax.ShapeDtypeStruct((rows, ZB), BF16), jax.ShapeDtypeStruct((256, 256), F32), jax.ShapeDtypeStruct((1, 256), F32)],
        input_output_aliases={7: 0}, compiler_params=_params(("arbitrary",)),
    )(z_br, z_br, z_br, dua, dua, pwbd, pscale, dz_buf)


def shortconv_bwd(z_br, dud, sc_w, dz_buf, li):
    rows = z_br.shape[0]
    t = ROW_TILE
    n_steps = rows // t
    cur, prev, nxt, full, layer = _tile_specs(t, rows // HALO, li)

    def body(zc_ref, zp_ref, zn_ref, dc_ref, dn_ref, sw_ref, _, dz_ref, dw_ref):
        i = pl.program_id(0)
        zp = jnp.where(i == 0, jnp.zeros(zp_ref.shape, zp_ref.dtype), zp_ref[...])
        zn = jnp.where(i == n_steps - 1, jnp.zeros(zn_ref.shape, zn_ref.dtype), zn_ref[...])
        dun = jnp.where(i == n_steps - 1, jnp.zeros(dn_ref.shape, dn_ref.dtype), dn_ref[...])

        def ext(lo):
            return jnp.concatenate([zp[:, lo:lo + 256], zc_ref[:, lo:lo + 256], zn[:, lo:lo + 256]], axis=0).astype(F32)

        mid = slice(HALO, HALO + t)
        bg, c2, xv, sg = ext(BG), ext(C2), ext(XV), ext(SG)
        du = jnp.concatenate([jnp.zeros((HALO, 256), F32), dc_ref[...], dun], axis=0)
        e = c2 * xv
        shifted = [_sh(e, SC_K - 1 - k) for k in range(SC_K)]
        f = sum(sw_ref[k:k + 1, :] * shifted[k] for k in range(SC_K))
        gate = _silu(sg)
        df = du * gate * bg
        de = sum(sw_ref[k:k + 1, :] * _ash(df, SC_K - 1 - k) for k in range(SC_K))
        dbg = du * gate * f
        dsg = du * bg * f * _silu_grad(sg)
        dz_ref[...] = jnp.concatenate([dbg[mid], (de * xv)[mid], (de * c2)[mid], dsg[mid]], axis=1).astype(BF16)
        dw = jnp.concatenate([jnp.sum((df * shifted[k])[mid], axis=0, keepdims=True) for k in range(SC_K)] + [
            jnp.zeros((8 - SC_K, 256), F32)], axis=0)
        _accumulate(i, dw_ref, dw)

    return pl.pallas_call(
        body, name="shortconv_bwd", grid=(n_steps,), in_specs=[cur(ZB), prev(ZB), nxt(ZB), cur(256), nxt(256), layer((8, 256)), ANY],
        out_specs=[cur(1024, BG // 1024), full((8, 256))],
        out_shape=[jax.ShapeDtypeStruct((rows, ZB), BF16), jax.ShapeDtypeStruct((8, 256), F32)],
        input_output_aliases={6: 0}, compiler_params=_params(("arbitrary",)),
    )(z_br, z_br, z_br, dud, dud, sc_w, dz_buf)


def conformer_bwd_tail(z_br, duc, conf_w, conf_vec, dz_buf, li):
    rows = z_br.shape[0]
    t = ROW_TILE
    cur, prev, _, full, layer = _tile_specs(t, rows // HALO, li)

    def body(zc_ref, zp_ref, du_ref, cw_ref, cv_ref, _, dc_ref, dcg_ref, dv_ref):
        i = pl.program_id(0)
        zp = jnp.where(i == 0, jnp.zeros(zp_ref.shape, zp_ref.dtype), zp_ref[...])

        def ext(lo):
            return jnp.concatenate([zp[:, lo:lo + 256], zc_ref[:, lo:lo + 256]], axis=0).astype(F32)

        g1 = ext(CA) * _sigmoid(ext(CGT))
        c = _conf_conv(g1, cw_ref)[HALO:] + cv_ref[0:1, :]
        _, vjp = jax.vjp(_conf_tail, c, zc_ref[:, CG:CG + 256].astype(F32), cv_ref[1:2, :], cv_ref[2:3, :])
        dc, dcg, dlg, dlb = vjp(du_ref[...])
        dc_ref[...] = dc
        dcg_ref[...] = dcg.astype(BF16)
        dvec = jnp.concatenate([dlg, dlb, jnp.sum(dc, axis=0, keepdims=True), jnp.zeros((5, 256), F32)], axis=0)
        _accumulate(i, dv_ref, dvec)

    return pl.pallas_call(
        body, name="conformer_bwd_tail", grid=(rows // t,), in_specs=[cur(ZB), prev(ZB), cur(256), layer((32, 256)), layer((8, 256)), ANY],
        out_specs=[cur(256), cur(256, CG // 256), full((8, 256))],
        out_shape=[jax.ShapeDtypeStruct((rows, 256), F32), jax.ShapeDtypeStruct((rows, ZB), BF16), jax.ShapeDtypeStruct((8, 256), F32)],
        input_output_aliases={5: 1}, compiler_params=_params(("arbitrary",)),
    )(z_br, z_br, duc, conf_w, conf_vec, dz_buf)


def conformer_bwd_conv(z_br, dc, conf_w, dz_buf, li):
    rows = z_br.shape[0]
    t = ROW_TILE
    n_steps = rows // t
    cur, prev, nxt, full, layer = _tile_specs(t, rows // HALO, li)

    def body(zc_ref, zp_ref, dc_ref, dn_ref, cw_ref, _, dz_ref, dw_ref):
        i = pl.program_id(0)
        zp = jnp.where(i == 0, jnp.zeros(zp_ref.shape, zp_ref.dtype), zp_ref[...])
        dcn = jnp.where(i == n_steps - 1, jnp.zeros(dn_ref.shape, dn_ref.dtype), dn_ref[...])

        def ext(lo):
            return jnp.concatenate([zp[:, lo:lo + 256], zc_ref[:, lo:lo + 256]], axis=0).astype(F32)

        a, gt = ext(CA), ext(CGT)
        sg = _sigmoid(gt)
        g1 = a * sg
        dc = dc_ref[...]
        dce = jnp.concatenate([dc, dcn], axis=0)
        dg1 = jnp.zeros_like(dce)
        dws = []
        for k in range(CONF_K):
            dg1 = dg1 + cw_ref[k:k + 1, :] * _ash(dce, CONF_K - 1 - k)
            dws.append(jnp.sum(dc * _sh(g1, CONF_K - 1 - k)[HALO:], axis=0, keepdims=True))
        dg1 = dg1[:t]
        ac, sc = a[HALO:], sg[HALO:]
        dz_ref[...] = jnp.concatenate([dg1 * sc, dg1 * ac * sc * (1.0 - sc)], axis=1).astype(BF16)
        _accumulate(i, dw_ref, jnp.concatenate(dws + [jnp.zeros((32 - CONF_K, 256), F32)], axis=0))

    return pl.pallas_call(
        body, name="conformer_bwd_conv", grid=(n_steps,), in_specs=[cur(ZB), prev(ZB), cur(256), nxt(256), layer((32, 256)), ANY],
        out_specs=[cur(512, CA // 512), full((32, 256))],
        out_shape=[jax.ShapeDtypeStruct((rows, ZB), BF16), jax.ShapeDtypeStruct((32, 256), F32)],
        input_output_aliases={5: 0}, compiler_params=_params(("arbitrary",)),
    )(z_br, z_br, dc, dc, conf_w, dz_buf)


def attention_bwd_prep(dub, o_att, z_br, dz_buf):
    rows = dub.shape[0]
    t = ROW_TILE
    cur, _, _, _, _ = _tile_specs(t, rows // HALO)

    def body(du_ref, o_ref, mg_ref, _, do_ref, dmg_ref, delta_ref):
        du, o, mg = du_ref[...], o_ref[...].astype(F32), mg_ref[...].astype(F32)
        do = du * _silu(mg)
        do_ref[...] = do.astype(BF16)
        dmg_ref[...] = (du * o * _silu_grad(mg)).astype(BF16)
        prod = do * o
        lane = lax.broadcasted_iota(jnp.int32, (1, HEADS * V_DIM), 1)
        for h in range(HEADS):
            part = jnp.where((lane >= V_DIM * h) & (lane < V_DIM * (h + 1)), prod, 0.0)
            delta_ref[h] = jnp.broadcast_to(jnp.sum(part, axis=-1, keepdims=True), (t, LANES))

    return pl.pallas_call(
        body, name="attention_bwd_prep", grid=(rows // t,), in_specs=[cur(512), cur(512), cur(512, MG // 512), ANY],
        out_specs=[cur(512), cur(512, MG // 512), pl.BlockSpec((HEADS, t, LANES), lambda i: (0, i, 0))],
        out_shape=[jax.ShapeDtypeStruct((rows, 512), BF16), jax.ShapeDtypeStruct((rows, ZB), BF16),
                   jax.ShapeDtypeStruct((HEADS, rows, LANES), F32)],
        input_output_aliases={3: 1}, compiler_params=_params(("parallel",)),
    )(dub, o_att, z_br, dz_buf)


def attention_bwd(q, k, v, do, lse, delta, exchange=None):
    rows = q.shape[0]
    tq = ROW_TILE
    nq = rows // tq
    n = 0 if exchange is None else len(exchange[0])

    def body(*refs):
        if n:
            start, finish = _exchange_ops(refs[6:6 + n], refs[9 + 2 * n:9 + 3 * n], refs[9 + 3 * n:], exchange[2])
            pl.when((pl.program_id(0) == 0) & (pl.program_id(1) == 0))(start)
        compute(*refs[:6], *refs[6 + 2 * n:9 + 2 * n])
        if n:
            pl.when((pl.program_id(0) == HEADS // 2 - 1) & (pl.program_id(1) == nq - 1))(finish)

    def compute(q_ref, k_ref, v_ref, do_ref, lse_ref, dl_ref, dq_ref, dk_ref, dv_ref):
        j = pl.program_id(1)

        @pl.when(j == 0)
        def _():
            dq_ref[...] = jnp.zeros_like(dq_ref)

        def head_step(h, tile, n_tiles, dk, dv, diagonal):
            lanes = slice(HEAD_PAD * h, HEAD_PAD * (h + 1))
            hm = _head_lane_mask(h)
            kh = k_ref[:, lanes]
            vh = jnp.where(hm, v_ref[...], jnp.zeros((), BF16))
            r0, width = pl.multiple_of(tile * tq, tq), n_tiles * tq
            qi = q_ref[pl.ds(r0, width), lanes]
            doi = jnp.where(hm, do_ref[pl.ds(r0, width), :], jnp.zeros((), BF16))
            s = _dot_nt(qi, kh)
            if diagonal:
                s = jnp.where(lax.broadcasted_iota(jnp.int32, (tq, tq), 1) <= lax.broadcasted_iota(jnp.int32, (tq, tq), 0), s, -1e30)
            pr = jnp.exp(s - lse_ref[h, pl.ds(r0, width), :][:, 0:1])
            dv = dv + _dot_tn(pr.astype(BF16), doi)
            dp = _dot_nt(doi, vh)
            ds = (pr * (dp - dl_ref[h, pl.ds(r0, width), :][:, 0:1])).astype(BF16)
            dq_ref[pl.ds(r0, width), lanes] += _dot(ds, kh)
            return dk + _dot_tn(ds, qi), dv

        def step(tile, n_tiles, carry, diagonal):
            dk0, dk1, dv = carry
            dk0, dv = head_step(0, tile, n_tiles, dk0, dv, diagonal)
            dk1, dv = head_step(1, tile, n_tiles, dk1, dv, diagonal)
            return dk0, dk1, dv

        zero = jnp.zeros((tq, HEAD_PAD), F32)
        carry = step(j, 1, (zero, zero, jnp.zeros((tq, 2 * V_DIM), F32)), True)
        odd = (nq - 1 - j) % 2
        carry = lax.cond(odd == 1, lambda cr: step(j + 1, 1, cr, False), lambda cr: cr, carry)
        dk0, dk1, dv = lax.fori_loop(0, (nq - 1 - j) // 2, lambda t, cr: step(j + 1 + odd + 2 * t, 2, cr, False), carry)
        dk_ref[:, 0:HEAD_PAD] = dk0
        dk_ref[:, HEAD_PAD:2 * HEAD_PAD] = dk1
        dv_ref[...] = dv

    srcs, dsts = ([], []) if exchange is None else (list(exchange[0]), list(exchange[1]))
    outs = pl.pallas_call(
        body, name="attention_bwd" if exchange is None else "attention_bwd_exchange", grid=(HEADS // 2, nq),
        in_specs=[pl.BlockSpec((rows, 2 * HEAD_PAD), lambda p, j: (0, p)), pl.BlockSpec((tq, 2 * HEAD_PAD), lambda p, j: (j, p)),
                  pl.BlockSpec((tq, 2 * V_DIM), lambda p, j: (j, p)), pl.BlockSpec((rows, 2 * V_DIM), lambda p, j: (0, p)),
                  pl.BlockSpec((2, rows, LANES), lambda p, j: (p, 0, 0)), pl.BlockSpec((2, rows, LANES), lambda p, j: (p, 0, 0))] + [
                      ANY] * (2 * n),
        out_specs=[pl.BlockSpec((rows, 2 * HEAD_PAD), lambda p, j: (0, p)), pl.BlockSpec((tq, 2 * HEAD_PAD), lambda p, j: (j, p)),
                   pl.BlockSpec((tq, 2 * V_DIM), lambda p, j: (j, p))] + [ANY] * n,
        out_shape=[jax.ShapeDtypeStruct((rows, HEADS * HEAD_PAD), F32), jax.ShapeDtypeStruct((rows, HEADS * HEAD_PAD), F32),
                   jax.ShapeDtypeStruct((rows, HEADS * V_DIM), F32)] + [jax.ShapeDtypeStruct(d.shape, d.dtype) for d in dsts],
        input_output_aliases={6 + n + a: 3 + a for a in range(n)}, scratch_shapes=EXCHANGE_SEMS(n) if n else [],
        compiler_params=_params(("arbitrary", "arbitrary") if n else ("parallel", "arbitrary")),
    )(q, k, v, do, lse, delta, *srcs, *dsts)
    return outs[0], outs[1], outs[2], list(outs[3:])


def mla_prep_bwd(dq, dk, dv, z_br, rope, gq, gkv, misc, dz_buf, li):
    rows = dq.shape[0]
    t = ROW_TILE
    cur, _, _, full, layer = _tile_specs(t, rows // HALO, li)
    w8 = HEADS * HEAD_PAD
    uq, keys, values = slice(0, 256), slice(M_UKVK - M_UQ, M_UKVV - M_UQ), slice(M_UKVV - M_UQ, M_WO - M_UQ)

    def body(dq_ref, dk_ref, dv_ref, z_ref, rope_ref, gq_ref, gkv_ref, up_ref, _, dz_ref, dup_ref, dgq_ref, dgkv_ref):
        i = pl.program_id(0)
        cth, s1, s2 = rope_ref[:, 0:128], rope_ref[:, 128:256], rope_ref[:, 256:384]
        dqb = _rope_transposed(dq_ref[...] * Q_SCALE, _lanes8(cth), _lanes8(s1), _lanes8(s2), w8).astype(BF16)
        dq_chunks = _chunks(dqb)
        cq = z_ref[:, 0:256].astype(F32)
        qn, vjp_q = jax.vjp(_rms, cq, gq_ref[...])
        dcq, dgq = vjp_q(sum(_dot_nt(dqk, up_ref[k, uq, :]) for k, dqk in enumerate(dq_chunks)))
        _accumulate(i, dgq_ref, dgq)

        dk = dk_ref[...]
        dkr = sum(dk[:, HEAD_PAD * h:HEAD_PAD * (h + 1)] for h in range(HEADS))
        dkr = _rope_transposed(dkr, cth, s1, s2, HEAD_PAD)
        lane = lax.broadcasted_iota(jnp.int32, (1, HEAD_PAD), 1)
        dkr = jnp.where((lane >= QK_NOPE) & (lane < QK_NOPE + QK_ROPE), dkr, 0.0)
        dkb, dvb = dk.astype(BF16), dv_ref[...].astype(BF16)
        dk_chunks, dv_chunks = _chunks(dkb), _chunks(dvb, width=2 * V_DIM)
        ckv = z_ref[:, 256:384].astype(F32)
        kvn, vjp_kv = jax.vjp(_rms, ckv, gkv_ref[...])
        dckv, dgkv = vjp_kv(sum(_dot_nt(dk_chunks[k], up_ref[k, keys, :]) + _dot_nt(dv_chunks[k], up_ref[k, values, 0:2 * V_DIM])
                                for k in range(N_CHIPS)))
        _accumulate(i, dgkv_ref, dgkv)
        dz_ref[...] = jnp.concatenate([dcq, dckv, dkr], axis=1).astype(BF16)
        qnb, kvnb = qn.astype(BF16), kvn.astype(BF16)
        d_uq, d_keys, d_values = _chunks(_dot_tn(qnb, dqb)), _chunks(_dot_tn(kvnb, dkb)), _chunks(_dot_tn(kvnb, dvb), width=2 * V_DIM)
        for k in range(N_CHIPS):
            padded = jnp.concatenate([d_values[k], jnp.zeros((128, 256 - 2 * V_DIM), F32)], axis=1)
            _accumulate(i, dup_ref.at[k], jnp.concatenate([d_uq[k], d_keys[k], padded], axis=0))

    return pl.pallas_call(
        body, name="mla_prep_bwd", grid=(rows // t,),
        in_specs=[cur(w8), cur(w8), cur(512), cur(512, CQ // 512), cur(384), layer((1, 256)), layer((1, 128)),
                  _misc_spec(misc, M_UQ, M_WO - M_UQ), ANY],
        out_specs=[cur(512, CQ // 512), full((N_CHIPS, M_WO - M_UQ, 256)), full((1, 256)), full((1, 128))],
        out_shape=[jax.ShapeDtypeStruct((rows, ZB), BF16), jax.ShapeDtypeStruct((N_CHIPS, M_WO - M_UQ, 256), F32),
                   jax.ShapeDtypeStruct((1, 256), F32), jax.ShapeDtypeStruct((1, 128), F32)],
        input_output_aliases={8: 0}, compiler_params=_params(("arbitrary",)),
    )(dq, dk, dv, z_br, rope, gq, gkv, misc[0], dz_buf)


def prenorm_bwd(dz_br, w_br, dh_gl, hres, gpre, dh_next, li):
    rows, d = hres.shape
    t = ROW_TILE
    cur, _, _, full, layer = _tile_specs(t, rows // HALO, li)

    def body(dz_ref, w_ref, dp_ref, x_ref, g_ref, dn_ref, dx_ref, dg_ref):
        i = pl.program_id(0)
        dh = _dot_nt(dz_ref[...], w_ref[...]) + dp_ref[...]
        _, vjp = jax.vjp(_rms, x_ref[...], g_ref[...])
        dx, dg = vjp(dh)
        dx_ref[...] = dx + dn_ref[...]
        _accumulate(i, dg_ref, dg)

    return pl.pallas_call(
        body, name="prenorm_bwd", grid=(rows // t,), in_specs=[cur(ZB), layer((d, ZB), 0), cur(d), cur(d), layer((1, d)), cur(d)],
        out_specs=[cur(d), full((1, d))], out_shape=[jax.ShapeDtypeStruct((rows, d), F32), jax.ShapeDtypeStruct((1, d), F32)],
        compiler_params=_params(("arbitrary",)),
    )(dz_br, w_br, dh_gl, hres, gpre, dh_next)


def _mesh_position():
    return lax.axis_index("x"), lax.axis_index("y"), lax.axis_index("c")


def chip_exchange(src, gather, name):
    block = src.shape if gather else src.shape[1:]

    def body(src_ref, dst_ref, send_sems, recv_sems, local_sem):
        x, y, c = _mesh_position()
        me = 2 * x + y
        peers = ((1 - x, y), (x, 1 - y), (1 - x, 1 - y))

        def part(k):
            return src_ref if gather else src_ref.at[k]

        def copy(j, slot):
            px, py = peers[j]
            return pltpu.make_async_remote_copy(src_ref=part(2 * px + py), dst_ref=dst_ref.at[slot], send_sem=send_sems.at[j],
                                                recv_sem=recv_sems.at[j], device_id=(px, py, c), device_id_type=MESH)

        local = pltpu.make_async_copy(part(me), dst_ref.at[me], local_sem)
        local.start()
        sends = [copy(j, me) for j in range(3)]
        for cp in sends:
            cp.start()
        for j, (px, py) in enumerate(peers):
            copy(j, 2 * px + py).wait_recv()
        for cp in sends:
            cp.wait_send()
        local.wait()

    return pl.pallas_call(
        body, name=name, in_specs=[pl.BlockSpec(memory_space=pl.ANY)], out_specs=pl.BlockSpec(memory_space=pl.ANY),
        out_shape=jax.ShapeDtypeStruct((N_CHIPS,) + tuple(block), src.dtype),
        scratch_shapes=[pltpu.SemaphoreType.DMA((3,)), pltpu.SemaphoreType.DMA((3,)), pltpu.SemaphoreType.DMA(())],
    )(src)


def sibling_swap(src, name):
    def body(src_ref, dst_ref, send_sem, recv_sem):
        x, y, c = _mesh_position()
        cp = pltpu.make_async_remote_copy(src_ref=src_ref, dst_ref=dst_ref, send_sem=send_sem, recv_sem=recv_sem,
                                          device_id=(x, y, 1 - c), device_id_type=MESH)
        cp.start()
        cp.wait()

    return pl.pallas_call(
        body, name=name, in_specs=[pl.BlockSpec(memory_space=pl.ANY)], out_specs=pl.BlockSpec(memory_space=pl.ANY),
        out_shape=jax.ShapeDtypeStruct(src.shape, src.dtype),
        scratch_shapes=[pltpu.SemaphoreType.DMA(()), pltpu.SemaphoreType.DMA(())],
    )(src)


def _comm_call(body, name, n_in, out_shapes, n_sems):
    return pl.pallas_call(
        body, name=name, in_specs=[ANY] * n_in, out_specs=[ANY] * len(out_shapes), out_shape=out_shapes,
        scratch_shapes=[pltpu.SemaphoreType.DMA((n,)) for n in n_sems])


def _row_halves(c, rows):
    half = rows // 2
    return pl.ds(pl.multiple_of(c * half, 16), half), pl.ds(pl.multiple_of((1 - c) * half, 16), half)


def _peers():
    x, y, c = _mesh_position()
    return x, y, c, 2 * x + y, ((1 - x, y), (x, 1 - y), (1 - x, 1 - y))


def _gather_ops(src, dst, sems, layer, own_copy):
    ici_send, ici_recv, d2d_send, d2d_recv, own_sems = sems
    n = len(src)

    def fetch(a, j, slot):
        x, y, c, _, peers = _peers()
        px, py = peers[j]
        mine, _ = _row_halves(c, src[a].shape[1])
        return pltpu.make_async_remote_copy(src_ref=src[a].at[layer, mine], dst_ref=dst[a].at[layer, slot, mine], send_sem=ici_send.at[3 * a + j],
                                            recv_sem=ici_recv.at[3 * a + j], device_id=(px, py, c), device_id_type=MESH)

    def forward(a, j, sibling_half):
        x, y, c, _, peers = _peers()
        px, py = peers[j]
        part = dst[a].at[layer, 2 * px + py, _row_halves(c, src[a].shape[1])[1 if sibling_half else 0]]
        return pltpu.make_async_remote_copy(src_ref=part, dst_ref=part, send_sem=d2d_send.at[3 * a + j], recv_sem=d2d_recv.at[3 * a + j],
                                            device_id=(x, y, 1 - c), device_id_type=MESH)

    def own(a):
        return pltpu.make_async_copy(src[a].at[layer], dst[a].at[layer, _peers()[3]], own_sems.at[a])

    def start():
        me = _peers()[3]
        for a in range(n):
            if own_copy:
                own(a).start()
            for j in range(3):
                fetch(a, j, me).start()

    def finish():
        peers = _peers()[4]
        for j, (px, py) in enumerate(peers):
            for a in range(n):
                fetch(a, j, 2 * px + py).wait_recv()
                forward(a, j, False).start()
        for j in range(3):
            for a in range(n):
                forward(a, j, True).wait_recv()
        for j in range(3):
            for a in range(n):
                fetch(a, j, 0).wait_send()
                forward(a, j, False).wait_send()
        if own_copy:
            for a in range(n):
                own(a).wait()

    return start, finish


def _exchange_ops(src, dst, sems, layer):
    send_sems, recv_sems, own_sems = sems
    n = len(src)

    def copy(a, j, slot):
        x, y, c, _, peers = _peers()
        px, py = peers[j]
        return pltpu.make_async_remote_copy(src_ref=src[a].at[2 * px + py], dst_ref=dst[a].at[layer, slot], send_sem=send_sems.at[3 * a + j],
                                            recv_sem=recv_sems.at[3 * a + j], device_id=(px, py, c), device_id_type=MESH)

    def own(a):
        me = _peers()[3]
        return pltpu.make_async_copy(src[a].at[me], dst[a].at[layer, me], own_sems.at[a])

    def start():
        me = _peers()[3]
        for a in range(n):
            own(a).start()
            for j in range(3):
                copy(a, j, me).start()

    def finish():
        peers = _peers()[4]
        for j, (px, py) in enumerate(peers):
            for a in range(n):
                copy(a, j, 2 * px + py).wait_recv()
        for j in range(3):
            for a in range(n):
                copy(a, j, 0).wait_send()
        for a in range(n):
            own(a).wait()

    return start, finish


GATHER_SEMS = lambda n: [pltpu.SemaphoreType.DMA((3 * n,))] * 4 + [pltpu.SemaphoreType.DMA((n,))]
EXCHANGE_SEMS = lambda n: [pltpu.SemaphoreType.DMA((3 * n,))] * 2 + [pltpu.SemaphoreType.DMA((n,))]


def gather_layer(srcs, dsts, layer, name):
    n = len(srcs)

    def body(*refs):
        start, finish = _gather_ops(refs[:n], refs[2 * n:3 * n], refs[3 * n:], layer, False)
        start()
        finish()

    return pl.pallas_call(
        body, name=name, in_specs=[ANY] * (2 * n), out_specs=[ANY] * n, out_shape=[jax.ShapeDtypeStruct(d.shape, d.dtype) for d in dsts],
        input_output_aliases={n + a: a for a in range(n)}, scratch_shapes=GATHER_SEMS(n),
    )(*srcs, *dsts)


def exchange_layer(ss, dsts, layer, name):
    n = len(ss)

    def body(*refs):
        start, finish = _exchange_ops(refs[:n], refs[2 * n:3 * n], refs[3 * n:], layer)
        start()
        finish()

    return pl.pallas_call(
        body, name=name, in_specs=[ANY] * (2 * n), out_specs=[ANY] * n, out_shape=[jax.ShapeDtypeStruct(d.shape, d.dtype) for d in dsts],
        input_output_aliases={n + a: a for a in range(n)}, scratch_shapes=EXCHANGE_SEMS(n),
    )(*ss, *dsts)


def _swap_ops(src, dst, sems):
    send_sems, recv_sems = sems

    def copy(a):
        x, y, c = _mesh_position()
        return pltpu.make_async_remote_copy(src_ref=src[a].at[:, _row_halves(c, src[a].shape[1])[1]], dst_ref=dst[a], send_sem=send_sems.at[a],
                                            recv_sem=recv_sems.at[a], device_id=(x, y, 1 - c), device_id_type=MESH)

    def start():
        for a in range(len(src)):
            copy(a).start()

    def finish():
        for a in range(len(src)):
            copy(a).wait()

    return start, finish


def swap_row_halves(ps, name):
    n = len(ps)

    def body(*refs):
        start, finish = _swap_ops(refs[:n], refs[n:2 * n], refs[2 * n:])
        start()
        finish()

    outs = [jax.ShapeDtypeStruct((p.shape[0], p.shape[1] // 2, p.shape[2]), p.dtype) for p in ps]
    return _comm_call(body, name, n, outs, (n, n))(*ps)


def add_row_half(p, r, c, name):
    n, half, cols = r.shape
    rb = _row_block(half, cols, 2)
    steps = half // rb

    def body(c_ref, p_ref, r_ref, o_ref):
        o_ref[...] = (p_ref[...].astype(F32) + r_ref[...].astype(F32)).astype(BF16)

    return pl.pallas_call(
        body, name=name, out_shape=jax.ShapeDtypeStruct(r.shape, BF16),
        grid_spec=pltpu.PrefetchScalarGridSpec(
            num_scalar_prefetch=1, grid=(n, steps),
            in_specs=[pl.BlockSpec((1, rb, cols), lambda k, i, c_ref: (k, c_ref[0] * steps + i, 0)),
                      pl.BlockSpec((1, rb, cols), lambda k, i, c_ref: (k, i, 0))],
            out_specs=pl.BlockSpec((1, rb, cols), lambda k, i, c_ref: (k, i, 0))),
        compiler_params=_params(("parallel", "parallel")),
    )(jnp.reshape(c, (1,)).astype(jnp.int32), p, r)


def sum_row_halves(l, c, name):
    layers, n, half, cols = l.shape
    rb = _row_block(half, cols, 4)
    steps = half // rb

    def body(c_ref, l_ref, o_ref):
        acc = l_ref[0, 0].astype(F32)
        for s in range(1, n):
            acc = acc + l_ref[0, s].astype(F32)
        o_ref[0] = acc

    return pl.pallas_call(
        body, name=name, out_shape=jax.ShapeDtypeStruct((layers, 2 * half, cols), F32),
        grid_spec=pltpu.PrefetchScalarGridSpec(
            num_scalar_prefetch=1, grid=(layers, steps), in_specs=[pl.BlockSpec((1, n, rb, cols), lambda a, i, c_ref: (a, 0, i, 0))],
            out_specs=pl.BlockSpec((1, rb, cols), lambda a, i, c_ref: (a, c_ref[0] * steps + i, 0))),
        compiler_params=_params(("parallel", "parallel")),
    )(jnp.reshape(c, (1,)).astype(jnp.int32), l)


def share_row_halves(gs, name):
    n = len(gs)

    def body(*refs):
        dst = refs[n:2 * n]
        send_sems, recv_sems = refs[2 * n:]
        x, y, c = _mesh_position()

        def copy(a, sibling_half):
            part = dst[a].at[:, _row_halves(c, dst[a].shape[1])[1 if sibling_half else 0]]
            return pltpu.make_async_remote_copy(src_ref=part, dst_ref=part, send_sem=send_sems.at[a], recv_sem=recv_sems.at[a],
                                                device_id=(x, y, 1 - c), device_id_type=MESH)

        for a in range(n):
            copy(a, False).start()
        for a in range(n):
            copy(a, True).wait_recv()
        for a in range(n):
            copy(a, False).wait_send()

    return pl.pallas_call(
        body, name=name, in_specs=[ANY] * n, out_specs=[ANY] * n, out_shape=[jax.ShapeDtypeStruct(g.shape, g.dtype) for g in gs],
        input_output_aliases={a: a for a in range(n)}, scratch_shapes=[pltpu.SemaphoreType.DMA((n,)), pltpu.SemaphoreType.DMA((n,))],
    )(*gs)


def _row_block(rows, cols, itemsize):
    best = 16
    for rb in range(16, rows + 1, 16):
        if rows % rb == 0 and rb * cols * itemsize <= 2 * 1024 * 1024:
            best = rb
    assert rows % best == 0, (rows, cols)
    return best


def _comm_block(rows):
    return 1024 if rows % 1024 == 0 else rows


def sum_slots(buf, name):
    n, r, c = buf.shape
    rb = _comm_block(r)

    def body(b_ref, o_ref):
        acc = b_ref[0].astype(F32)
        for s in range(1, n):
            acc = acc + b_ref[s].astype(F32)
        o_ref[...] = acc

    return pl.pallas_call(
        body, name=name, grid=(r // rb,), in_specs=[pl.BlockSpec((n, rb, c), lambda i: (0, i, 0))],
        out_specs=pl.BlockSpec((rb, c), lambda i: (i, 0)), out_shape=jax.ShapeDtypeStruct((r, c), F32),
        compiler_params=_params(("parallel",)),
    )(buf)


def add_pair(a, b, out_dtype, name):
    shape = a.shape
    a2, b2 = a.reshape(-1, shape[-1]), b.reshape(-1, shape[-1])
    r, c = a2.shape
    rb = _comm_block(r)

    def body(a_ref, b_ref, o_ref):
        o_ref[...] = (a_ref[...].astype(F32) + b_ref[...].astype(F32)).astype(out_dtype)

    out = pl.pallas_call(
        body, name=name, grid=(r // rb,), in_specs=[pl.BlockSpec((rb, c), lambda i: (i, 0))] * 2,
        out_specs=pl.BlockSpec((rb, c), lambda i: (i, 0)), out_shape=jax.ShapeDtypeStruct((r, c), out_dtype),
        compiler_params=_params(("parallel",)),
    )(a2, b2)
    return out.reshape(shape)


def adamw(w, g, m, v):
    shape = w.shape
    cols = shape[-1]
    rows = math.prod(shape[:-1])
    if rows * cols <= 256 * 1024:
        rb, cb = rows, cols
    else:
        rb = max(r for r in range(8, 2049, 8) if rows % r == 0)
        cb = cols if rb * cols * 4 <= 2 * 1024 * 1024 else 256
    assert rows % rb == 0 and cols % cb == 0, shape

    def body(w_ref, g_ref, m_ref, v_ref, d_ref, nm_ref, nv_ref):
        g_ = g_ref[...]
        nm = ADAM_B1 * m_ref[...] + (1.0 - ADAM_B1) * g_
        nv = ADAM_B2 * v_ref[...] + (1.0 - ADAM_B2) * (g_ * g_)
        m_hat = nm / (1.0 - ADAM_B1 ** ADAM_STEP)
        v_hat = nv / (1.0 - ADAM_B2 ** ADAM_STEP)
        d_ref[...] = -ADAM_LR * (m_hat / (jnp.sqrt(v_hat) + ADAM_EPS) + ADAM_WD * w_ref[...])
        nm_ref[...] = nm
        nv_ref[...] = nv

    spec = pl.BlockSpec((rb, cb), lambda i, j: (i, j))
    outs = pl.pallas_call(
        body, name="adamw", grid=(rows // rb, cols // cb), in_specs=[spec] * 4, out_specs=[spec] * 3,
        out_shape=[jax.ShapeDtypeStruct((rows, cols), F32)] * 3, compiler_params=_params(("parallel", "parallel")),
    )(*(a.reshape(rows, cols) for a in (w, g, m, v)))
    return tuple(o.reshape(shape) for o in outs)


def _pack(arrays, dtype, row_multiple):
    flat = jnp.concatenate([a.astype(dtype).reshape(-1) for a in arrays])
    per = LANES * row_multiple
    total = -(-flat.shape[0] // per) * per
    return jnp.pad(flat, (0, total - flat.shape[0])).reshape(total // LANES, LANES)


def _unpack(buf, shapes):
    flat = buf.reshape(-1)
    out, off = [], 0
    for s in shapes:
        n = math.prod(s)
        out.append(flat[off:off + n].reshape(s))
        off += n
    return out


def _input_weights(blocks):
    c0, c1, c2, c3 = (blocks[..., k, :, :] for k in range(N_CHIPS))
    pad = lambda n: jnp.zeros(c0.shape[:-1] + (n,), blocks.dtype)
    w_br = jnp.concatenate([c1[..., 376:1400], c0[..., 0:896], pad(64), c0[..., 896:928], pad(32), c0[..., 928:], c1[..., 0:376]], axis=-1)
    return w_br, jnp.concatenate([c1[..., 1400:], c2, c3], axis=-1)


def _input_weights_inverse(dw_br, dw_gl):
    c0 = jnp.concatenate([dw_br[..., 1024:1920], dw_br[..., 1984:2016], dw_br[..., 2048:2952]], axis=-1)
    c1 = jnp.concatenate([dw_br[..., 2952:ZB], dw_br[..., 0:1024], dw_gl[..., 0:432]], axis=-1)
    return jnp.stack([c0, c1, dw_gl[..., 432:2264], dw_gl[..., 2264:]], axis=-3)


def _block_diag(pw):
    zeros = lambda n: jnp.zeros(pw.shape[:-3] + (64, n), pw.dtype)
    rows = [jnp.concatenate([zeros(64 * g), pw[..., g, :, :], zeros(64 * (3 - g))], axis=-1) for g in range(4)]
    return jnp.concatenate(rows, axis=-2)


def _block_diag_inverse(d):
    return jnp.stack([d[..., 64 * g:64 * (g + 1), 64 * g:64 * (g + 1)] for g in range(4)], axis=-3)


def _pad_rows(a, n):
    return jnp.pad(a, ((0, n - a.shape[0]), (0, 0)))


def _rope_tables(rows):
    inv = 1.0 / (ROPE_THETA ** (jnp.arange(0, QK_ROPE, 2, dtype=F32) / QK_ROPE))
    ang = jnp.arange(rows, dtype=F32)[:, None] * inv[None, :]
    cos, sin = jnp.cos(ang), jnp.sin(ang)
    one, zero = jnp.ones((rows, 1), F32), jnp.zeros((rows, 1), F32)
    rep = lambda a, n: jnp.broadcast_to(a, (rows, n))
    c = jnp.concatenate([rep(one, 64), cos, cos, rep(one, 32)], axis=1)
    s1 = jnp.concatenate([rep(zero, 64), -sin, rep(zero, 48)], axis=1)
    s2 = jnp.concatenate([rep(zero, 80), sin, rep(zero, 32)], axis=1)
    return jnp.concatenate([c, s1, s2], axis=1)


def _misc_block(parts):
    lead = parts["w_uq"].shape[:-2]
    pad_last = lambda a, n: jnp.pad(a, [(0, 0)] * (a.ndim - 1) + [(0, n - a.shape[-1])])
    uq = pad_last(parts["w_uq"].reshape(lead + (256, 2, QK_NOPE + QK_ROPE)), HEAD_PAD).reshape(lead + (256, 256))
    kv = parts["w_ukv"].reshape(lead + (128, 2, QK_NOPE + V_DIM))
    keys = pad_last(kv[..., :QK_NOPE], HEAD_PAD).reshape(lead + (128, 256))
    values = pad_last(kv[..., QK_NOPE:].reshape(lead + (128, 2 * V_DIM)), 256)
    wo = jnp.swapaxes(parts["w_o"].reshape(lead + (256, N_CHIPS, 256)), -3, -2).reshape(lead + (D_MODEL, 256))
    gap = jnp.zeros(lead + (M_UQ - M_SC - 256, 256), uq.dtype)
    return jnp.concatenate([parts["w_out_mla"], parts["w_out_pool"], parts["w_out_conf"], parts["w_out_sc"], gap, uq, keys, values, wo],
                           axis=-2)


def _misc_unblock(block):
    lead = block.shape[:-2]
    rows = lambda lo, n: block[..., lo:lo + n, :]
    uq = rows(M_UQ, 256).reshape(lead + (256, 2, HEAD_PAD))[..., :QK_NOPE + QK_ROPE].reshape(lead + (256, 2 * (QK_NOPE + QK_ROPE)))
    keys = rows(M_UKVK, 128).reshape(lead + (128, 2, HEAD_PAD))[..., :QK_NOPE]
    values = rows(M_UKVV, 128)[..., :2 * V_DIM].reshape(lead + (128, 2, V_DIM))
    wo = jnp.swapaxes(rows(M_WO, D_MODEL).reshape(lead + (N_CHIPS, 256, 256)), -3, -2).reshape(lead + (256, D_MODEL))
    return dict(w_out_mla=rows(M_MLA, 512), w_out_pool=rows(M_POOL, 256), w_out_conf=rows(M_CONF, 256), w_out_sc=rows(M_SC, 256), w_uq=uq,
                w_ukv=jnp.concatenate([keys, values], axis=-1).reshape(lead + (128, 256)), w_o=wo)


def _to_chip_blocks(name, a):
    if name == "w_o":
        return a.reshape(a.shape[:-2] + (N_CHIPS, a.shape[-2] // N_CHIPS, a.shape[-1]))
    return jnp.swapaxes(a.reshape(a.shape[:-1] + (N_CHIPS, a.shape[-1] // N_CHIPS)), -3, -2)


def _from_chip_blocks(name, b):
    if name == "w_o":
        return b.reshape(b.shape[:-3] + (N_CHIPS * b.shape[-2], b.shape[-1]))
    s = jnp.swapaxes(b, -3, -2)
    return s.reshape(s.shape[:-2] + (N_CHIPS * s.shape[-1],))


LARGE = ("w_in",) + MISC


def gather_small(shards):
    small = chip_exchange(_pack([shards[n] for n, _, _ in SHARDED_SMALL], F32, 8), True, "gather_small_ici")
    per_chip = [_unpack(small[k], [s for _, s, _ in SHARDED_SMALL]) for k in range(N_CHIPS)]
    return {name: jnp.concatenate([per_chip[k][idx] for k in range(N_CHIPS)], axis=axis) for idx, (name, _, axis) in enumerate(SHARDED_SMALL)}


class LocalWeights:
    def __init__(self, full):
        self.w_in = _to_chip_blocks("w_in", full["w_in"])
        self.misc = _misc_block({n: _to_chip_blocks(n, full[n]) for n in MISC}).astype(BF16)
        self.grads = [None] * DEPTH

    def layer(self, i):
        return self.w_in[i], (self.misc, i)

    def gather_with_attention(self, i):
        return None

    def gathered(self, dsts):
        pass

    def exchange_with_attention(self):
        return None

    def exchanged(self, dsts):
        pass

    def swap_with_postnorm(self):
        return None

    def swapped(self, rs):
        pass

    def put_grads(self, i, w_in, misc):
        self.grads[i] = (w_in, misc)

    def reduced(self):
        out = {n: _from_chip_blocks(n, b) for n, b in _misc_unblock(jnp.stack([m for _, m in self.grads])).items()}
        out["w_in"] = _from_chip_blocks("w_in", jnp.stack([w for w, _ in self.grads]))
        return out


class MeshWeights:
    def __init__(self, shards, c, chip):
        self.c, self.chip = c, chip
        self.srcs = [shards["w_in"].astype(BF16), _misc_block({n: shards[n] for n in MISC}).astype(BF16)]
        dsts = [lax.empty((DEPTH, N_CHIPS) + s.shape[1:], BF16) for s in self.srcs]
        self.dsts = gather_layer(self.srcs, dsts, 0, "gather_layer")
        self.landed = [lax.empty((DEPTH, N_CHIPS, s.shape[1] // 2, s.shape[2]), BF16) for s in self.srcs]
        self.pending = self.to_swap = None

    def layer(self, i):
        if i > 0:
            return self.dsts[0][i], (self.dsts[1], i)
        own = (jnp.arange(N_CHIPS) == self.chip)[:, None, None]
        w_in, misc = (jnp.where(own, s[0][None], d[0]) for s, d in zip(self.srcs, self.dsts))
        return w_in, (misc[None], 0)

    def gather_with_attention(self, i):
        return (self.srcs, self.dsts, i + 1) if i + 1 < DEPTH else None

    def gathered(self, dsts):
        if dsts:
            self.dsts = dsts

    def exchange_with_attention(self):
        return None if self.pending is None else (self.pending[0], self.landed, self.pending[1])

    def exchanged(self, dsts):
        if dsts:
            self.landed, self.pending = dsts, None

    def put_grads(self, i, w_in, misc):
        self.to_swap = ([w_in.astype(BF16), misc.astype(BF16)], i)

    def swap_with_postnorm(self):
        return None if self.to_swap is None else self.to_swap[0]

    def swapped(self, rs):
        if rs:
            ps, i = self.to_swap
            self.pending = ([add_row_half(p, r, self.c, "reduce_pair_%d" % a) for a, (p, r) in enumerate(zip(ps, rs))], i)
            self.to_swap = None

    def reduced(self):
        self.swapped(swap_row_halves(self.to_swap[0], "reduce_swap"))
        landed = exchange_layer(self.pending[0], self.landed, self.pending[1], "reduce_exchange")
        gs = [sum_row_halves(l, self.c, "reduce_sum_%d" % a) for a, l in enumerate(landed)]
        g_in, g_misc = share_row_halves(gs, "reduce_share")
        out = {"w_in": g_in}
        out.update(_misc_unblock(g_misc))
        return out


def reduce_small(grads, chip):
    names = [n for n, _ in REPLICATED] + [n for n, _, _ in SHARDED_SMALL]
    buf = _pack([grads[n] for n in names], F32, 8)
    chip_sum = add_pair(buf, sibling_swap(buf, "reduce_small_d2d"), F32, "reduce_small_pair")
    total = sum_slots(chip_exchange(chip_sum, True, "reduce_small_ici"), "reduce_small_sum")
    out = dict(zip(names, _unpack(total, [grads[n].shape for n in names])))
    for name, shape, axis in SHARDED_SMALL:
        out[name] = lax.dynamic_slice_in_dim(out[name], chip * shape[axis], shape[axis], axis)
    return out


def _prepare_small(w):
    row = lambda a: a[:, None, :]
    conf_vec = jnp.concatenate([row(w["conf_dw_b"]), row(w["conf_ln_g"]), row(w["conf_ln_b"]), jnp.zeros((DEPTH, 5, 256), F32)], axis=1)
    return dict(
        gpre=row(w["pre_norm_g"]), bias=row(w["gate_bias"]), pwbd=_block_diag(w["pool_w"]).astype(BF16), pscale=row(w["pool_scale"]),
        gq=row(w["q_norm_g"]), gkv=row(w["kv_norm_g"]), conf_w=jnp.pad(w["conf_dw_w"].astype(F32), ((0, 0), (0, 32 - CONF_K), (0, 0))),
        conf_vec=conf_vec, sc_w=jnp.pad(w["sc_dw_w"].astype(F32), ((0, 0), (0, 8 - SC_K), (0, 0))), gpost=row(w["post_norm_g"]))


def _prepare_layer(w_in_blocks, misc):
    w_br, w_gl = _input_weights(w_in_blocks)
    one = lambda a: a.astype(BF16)[None]
    return dict(w_br=one(w_br), w_gl=one(w_gl), misc=misc)


def local_step(x, target, w, large):
    seq = x.shape[0]
    length = N_META + seq
    rows = -(-length // ROW_TILE) * ROW_TILE
    bt = _big_tile(rows)
    hres = _pad_rows(jnp.concatenate([w["meta_tokens"].astype(F32), x], axis=0), rows)
    tgt = jnp.pad(target, ((N_META, rows - length), (0, 0)))
    rope = _rope_tables(rows)
    sw = _prepare_small(w)

    saved = []
    for i in range(DEPTH):
        lw = _prepare_layer(*large.layer(i))
        z_br, hb = prenorm_project(hres, sw["gpre"], lw["w_br"], i)
        z_gl = matmul(hb, lw["w_gl"], "nn", BF16, bt, 1024, D_MODEL, "project_gates", b_layer=0)
        ua, uc, ud, q, k, v = branches_fwd(z_br, rope, sw["pwbd"], sw["pscale"], sw["gq"], sw["gkv"], lw["misc"], sw["conf_w"],
                                           sw["conf_vec"], sw["sc_w"], i)
        o_att, lse, dsts = attention_fwd(q, k, v, large.gather_with_attention(i))
        large.gathered(dsts)
        ub, mb, o, hnew = merge_fwd(ua, o_att, uc, ud, z_br, z_gl, sw["bias"], lw["misc"], sw["gpost"], hres, i)
        saved.append(dict(lw=lw, hres=hres, hb=hb, z_br=z_br, z_gl=z_gl, ua=ua, ub=ub, uc=uc, ud=ud, q=q, k=k, v=v, o_att=o_att,
                          lse=lse, mb=mb, o=o))
        hres = hnew

    dh, total = loss_head(hres, tgt, seq)

    g = {n: [None] * DEPTH for n in ("gpre", "bias", "pwbd", "pscale", "gq", "gkv", "conf_w", "conf_vec", "sc_w", "gpost")}
    for i in reversed(range(DEPTH)):
        s = saved[i]
        lw = s["lw"]
        dm, dwo, g["gpost"][i], rs = postnorm_bwd(dh, s["o"], s["mb"], lw["misc"], sw["gpost"], i, large.swap_with_postnorm())
        large.swapped(rs)
        dua, dub, duc, dud, dz_gl, dwout, g["bias"][i] = merge_bwd(dm, s["ua"], s["ub"], s["uc"], s["ud"], s["z_gl"], sw["bias"],
                                                                 lw["misc"], i)
        dz_br = lax.empty((rows, ZB), BF16)
        dz_br, g["pwbd"][i], g["pscale"][i] = pool_bwd(s["z_br"], dua, sw["pwbd"], sw["pscale"], dz_br, i)
        dz_br, g["sc_w"][i] = shortconv_bwd(s["z_br"], dud, sw["sc_w"], dz_br, i)
        dc, dz_br, g["conf_vec"][i] = conformer_bwd_tail(s["z_br"], duc, sw["conf_w"], sw["conf_vec"], dz_br, i)
        dz_br, g["conf_w"][i] = conformer_bwd_conv(s["z_br"], dc, sw["conf_w"], dz_br, i)
        do, dz_br, delta = attention_bwd_prep(dub, s["o_att"], s["z_br"], dz_br)
        dq, dk, dv, dsts = attention_bwd(s["q"], s["k"], s["v"], do, s["lse"], delta, large.exchange_with_attention())
        large.exchanged(dsts)
        dz_br, dwup, g["gq"][i], g["gkv"][i] = mla_prep_bwd(dq, dk, dv, s["z_br"], rope, sw["gq"], sw["gkv"], lw["misc"], dz_br, i)
        dw_br = matmul(s["hb"], dz_br, "tn", BF16, D_MODEL, ZB // 2, bt, "grad_w_branch")
        dw_gl = matmul(s["hb"], dz_gl, "tn", BF16, D_MODEL, 1024, bt, "grad_w_gates")
        dh_gl = matmul(dz_gl, lw["w_gl"], "nt", F32, bt, D_MODEL, 1024, "grad_h_gates", b_layer=0)
        dh, g["gpre"][i] = prenorm_bwd(dz_br, lw["w_br"], dh_gl, s["hres"], sw["gpre"], dh, i)
        gap = jnp.zeros((N_CHIPS, M_UQ - M_SC - 256, 256), F32)
        large.put_grads(i, _input_weights_inverse(dw_br, dw_gl), jnp.concatenate([dwout, gap, dwup, dwo], axis=1))

    g = {n: jnp.stack(parts) for n, parts in g.items()}
    grads = dict(
        meta_tokens=dh[:N_META], pre_norm_g=g["gpre"][:, 0], gate_bias=g["bias"][:, 0], pool_w=_block_diag_inverse(g["pwbd"]),
        pool_scale=g["pscale"][:, 0], q_norm_g=g["gq"][:, 0], kv_norm_g=g["gkv"][:, 0], conf_dw_w=g["conf_w"][:, :CONF_K],
        conf_dw_b=g["conf_vec"][:, 2], conf_ln_g=g["conf_vec"][:, 0], conf_ln_b=g["conf_vec"][:, 1], sc_dw_w=g["sc_w"][:, :SC_K],
        post_norm_g=g["gpost"][:, 0])
    return total[0, 0], dh[N_META:length], grads


def kernel(x, meta_tokens, pre_norm_g, w_in, gate_bias, pool_w, pool_scale, w_out_pool, q_norm_g, w_uq, kv_norm_g, w_ukv, w_out_mla, conf_dw_w, conf_dw_b, conf_ln_g, conf_ln_b, w_out_conf, sc_dw_w, w_out_sc, w_o, post_norm_g, loss_target, m_meta_tokens, m_pre_norm_g, m_w_in, m_gate_bias, m_pool_w, m_pool_scale, m_w_out_pool, m_q_norm_g, m_w_uq, m_kv_norm_g, m_w_ukv, m_w_out_mla, m_conf_dw_w, m_conf_dw_b, m_conf_ln_g, m_conf_ln_b, m_w_out_conf, m_sc_dw_w, m_w_out_sc, m_w_o, m_post_norm_g, v_meta_tokens, v_pre_norm_g, v_w_in, v_gate_bias, v_pool_w, v_pool_scale, v_w_out_pool, v_q_norm_g, v_w_uq, v_kv_norm_g, v_w_ukv, v_w_out_mla, v_conf_dw_w, v_conf_dw_b, v_conf_ln_g, v_conf_ln_b, v_w_out_conf, v_sc_dw_w, v_w_out_sc, v_w_o, v_post_norm_g):
    args = locals()
    weights = {n: args[n] for n in WEIGHT_ORDER}
    c = lax.axis_index("c")
    chip = 2 * lax.axis_index("x") + lax.axis_index("y")

    small = {n: weights[n] for n, _ in REPLICATED}
    small.update(gather_small(weights))
    large = MeshWeights(weights, c, chip)
    total, dx, grads = local_step(x[0], loss_target[0], small, large)
    loss = lax.psum(total * (0.5 / D_MODEL), ("x", "y", "c"))

    reduced = large.reduced()
    reduced.update(reduce_small(grads, chip))

    flip = lambda a: jnp.swapaxes(a, 1, 2)
    deltas, new_m, new_v = [], [], []
    for n in WEIGHT_ORDER:
        operands = (weights[n], reduced[n], args["m_" + n], args["v_" + n])
        if n == "w_in":
            operands = (flip(operands[0]), lax.optimization_barrier(flip(operands[1])), flip(operands[2]), flip(operands[3]))
            reduced[n] = flip(operands[1])
        d, nm, nv = adamw(*operands)
        if n == "w_in":
            d, nm, nv = flip(d), flip(nm), flip(nv)
        deltas.append(d)
        new_m.append(nm)
        new_v.append(nv)
    return (loss, dx[None], *[reduced[n] for n in WEIGHT_ORDER], *deltas, *new_m, *new_v)
```

```python
import functools
import math

import jax
import jax.numpy as jnp
from jax import lax
from jax.experimental import pallas as pl
from jax.experimental.pallas import tpu as pltpu

F32 = jnp.float32
BF16 = jnp.bfloat16

D_MODEL = 1024
DEPTH = 4
N_META = 16
EPS = 1e-6
HEADS = 8
QK_NOPE = 64
QK_ROPE = 32
V_DIM = 64
HEAD_PAD = 128
ROPE_THETA = 10000.0
Q_SCALE = (QK_NOPE + QK_ROPE) ** -0.5
CONF_K = 31
SC_K = 3
IN_W = 7328
N_CHIPS = 4

ZB = 3328
ZG = 4096
BG, C2, XV, SG, PV, PG, CQ, CKV, KR, MG, CA, CGT, CG = (0, 256, 512, 768, 1024, 1280, 1536, 1792, 1920, 2048, 2560, 2816, 3072)

KEY_GROUP = 8
ROW_TILE = 384
HALO = 32
LANES = 128
VMEM_LIMIT = 56 * 1024 * 1024

ADAM_LR = 0.001
ADAM_B1 = 0.9
ADAM_B2 = 0.999
ADAM_EPS = 1e-08
ADAM_WD = 0.01
ADAM_STEP = 10

MESH = pl.DeviceIdType.MESH
ANY = pl.BlockSpec(memory_space=pl.ANY)

MISC = ("w_out_mla", "w_out_pool", "w_out_conf", "w_out_sc", "w_uq", "w_ukv", "w_o")
M_MLA, M_POOL, M_CONF, M_SC, M_UQ, M_UKVK, M_UKVV, M_WO, MISC_ROWS = 0, 512, 768, 1024, 1536, 1792, 1920, 2048, 3072
SHARDED_SMALL = (
    ("meta_tokens", (N_META, 256), 1),
    ("conf_dw_w", (DEPTH, CONF_K, 64), 2),
    ("sc_dw_w", (DEPTH, SC_K, 64), 2),
)
REPLICATED = (
    ("pre_norm_g", (DEPTH, D_MODEL)),
    ("gate_bias", (DEPTH, 4 * D_MODEL)),
    ("pool_w", (DEPTH, 4, 64, 64)),
    ("pool_scale", (DEPTH, 256)),
    ("q_norm_g", (DEPTH, 256)),
    ("kv_norm_g", (DEPTH, 128)),
    ("conf_dw_b", (DEPTH, 256)),
    ("conf_ln_g", (DEPTH, 256)),
    ("conf_ln_b", (DEPTH, 256)),
    ("post_norm_g", (DEPTH, D_MODEL)),
)
WEIGHT_ORDER = ("meta_tokens", "pre_norm_g", "w_in", "gate_bias", "pool_w", "pool_scale", "w_out_pool", "q_norm_g", "w_uq",
                "kv_norm_g", "w_ukv", "w_out_mla", "conf_dw_w", "conf_dw_b", "conf_ln_g", "conf_ln_b", "w_out_conf", "sc_dw_w",
                "w_out_sc", "w_o", "post_norm_g")


def _dot(a, b):
    return lax.dot_general(a, b, (((1,), (0,)), ((), ())), preferred_element_type=F32)


def _dot_nt(a, b):
    return lax.dot_general(a, b, (((1,), (1,)), ((), ())), preferred_element_type=F32)


def _dot_tn(a, b):
    return lax.dot_general(a, b, (((0,), (0,)), ((), ())), preferred_element_type=F32)


def _sigmoid(x):
    return jax.nn.sigmoid(x)


def _silu(x):
    return x * _sigmoid(x)


def _silu_grad(x):
    s = _sigmoid(x)
    return s * (1.0 + x * (1.0 - s))


def _rms(x, g):
    return x * lax.rsqrt(jnp.mean(x * x, axis=-1, keepdims=True) + EPS) * g


def _sh(x, d):
    return x if d == 0 else pltpu.roll(x, d, 0)


def _ash(x, d):
    return x if d == 0 else pltpu.roll(x, x.shape[0] - d, 0)


def _lanes8(t):
    return jnp.concatenate([t] * HEADS, axis=1)


def _pool_window_sums(v, shift):
    a2 = v + shift(v, 1)
    a4 = a2 + shift(a2, 2)
    a8 = a4 + shift(a4, 4)
    a16 = a8 + shift(a8, 8)
    lane = lax.broadcasted_iota(jnp.int32, v.shape, 1)
    return jnp.where(lane < 64, a2, jnp.where(lane < 128, a4, jnp.where(lane < 192, a8, a16)))


def _pool_counts(first_row, rows):
    pos = first_row + lax.broadcasted_iota(jnp.int32, (rows, 256), 0)
    lane = lax.broadcasted_iota(jnp.int32, (rows, 256), 1)
    width = jnp.where(lane < 64, 2, jnp.where(lane < 128, 4, jnp.where(lane < 192, 8, 16)))
    return jnp.maximum(jnp.minimum(pos + 1, width), 1).astype(F32)


def _params(sem=None):
    return pltpu.CompilerParams(dimension_semantics=sem, vmem_limit_bytes=VMEM_LIMIT)


def _tile_specs(t, n_halo_blocks, li=0):
    per = t // HALO

    def layer(shape, idx=li):
        return pl.BlockSpec((None,) + tuple(shape), lambda i: (idx,) + (0,) * len(shape))

    def cur(c, cb=0):
        return pl.BlockSpec((t, c), lambda i: (i, cb))

    def prev(c, cb=0):
        return pl.BlockSpec((HALO, c), lambda i: (jnp.maximum(i * per - 1, 0), cb))

    def nxt(c, cb=0):
        return pl.BlockSpec((HALO, c), lambda i: (jnp.minimum((i + 1) * per, n_halo_blocks - 1), cb))

    def full(shape):
        return pl.BlockSpec(shape, lambda i: (0,) * len(shape))

    return cur, prev, nxt, full, layer


def _big_tile(rows):
    return rows // 3 if rows % (3 * LANES) == 0 else ROW_TILE


def matmul(a, b, mode, out_dtype, tm, tn, tk, name, b_layer=None):
    bs = b.shape if b_layer is None else b.shape[1:]
    lead = () if b_layer is None else (None,)
    pick = (lambda *ix: ix) if b_layer is None else (lambda *ix: (b_layer,) + ix)
    if mode == "nn":
        (m, k), n = a.shape, bs[1]
        a_spec = pl.BlockSpec((tm, tk), lambda i, j, kk: (i, kk))
        b_spec = pl.BlockSpec(lead + (tk, tn), lambda i, j, kk: pick(kk, j))
        dot = _dot
    elif mode == "nt":
        (m, k), n = a.shape, bs[0]
        a_spec = pl.BlockSpec((tm, tk), lambda i, j, kk: (i, kk))
        b_spec = pl.BlockSpec(lead + (tn, tk), lambda i, j, kk: pick(j, kk))
        dot = _dot_nt
    else:
        (k, m), n = a.shape, bs[1]
        a_spec = pl.BlockSpec((tk, tm), lambda i, j, kk: (kk, i))
        b_spec = pl.BlockSpec(lead + (tk, tn), lambda i, j, kk: pick(kk, j))
        dot = _dot_tn
    assert m % tm == 0 and n % tn == 0 and k % tk == 0, (a.shape, bs, tm, tn, tk)
    nk = k // tk

    def body(a_ref, b_ref, o_ref, acc_ref):
        kk = pl.program_id(2)

        @pl.when(kk == 0)
        def _():
            acc_ref[...] = jnp.zeros_like(acc_ref)

        acc_ref[...] += dot(a_ref[...], b_ref[...])

        @pl.when(kk == nk - 1)
        def _():
            o_ref[...] = acc_ref[...].astype(out_dtype)

    return pl.pallas_call(
        body, name=name, grid=(m // tm, n // tn, nk), in_specs=[a_spec, b_spec],
        out_specs=pl.BlockSpec((tm, tn), lambda i, j, kk: (i, j)), out_shape=jax.ShapeDtypeStruct((m, n), out_dtype),
        scratch_shapes=[pltpu.VMEM((tm, tn), F32)], compiler_params=_params(("parallel", "parallel", "arbitrary")),
    )(a, b)


def prenorm_project(hres, g, w, li):
    rows, d = hres.shape
    n = w.shape[2]
    tm, tn = _big_tile(rows), n // 2

    def body(x_ref, g_ref, w_ref, z_ref, hb_ref):
        @pl.when(pl.program_id(1) == 0)
        def _():
            hb_ref[...] = _rms(x_ref[...], g_ref[...]).astype(BF16)

        z_ref[...] = _dot(hb_ref[...], w_ref[...]).astype(BF16)

    return pl.pallas_call(
        body, name="prenorm_project", grid=(rows // tm, n // tn),
        in_specs=[pl.BlockSpec((tm, d), lambda i, j: (i, 0)), pl.BlockSpec((None, 1, d), lambda i, j: (li, 0, 0)),
                  pl.BlockSpec((None, d, tn), lambda i, j: (0, 0, j))],
        out_specs=[pl.BlockSpec((tm, tn), lambda i, j: (i, j)), pl.BlockSpec((tm, d), lambda i, j: (i, 0))],
        out_shape=[jax.ShapeDtypeStruct((rows, n), BF16), jax.ShapeDtypeStruct((rows, d), BF16)],
        compiler_params=_params(("parallel", "arbitrary")),
    )(hres, g, w)


def _rope(q, c, s1, s2, width):
    return q * c + pltpu.roll(q, width - 16, 1) * s1 + pltpu.roll(q, 16, 1) * s2


def _rope_transposed(dq, c, s1, s2, width):
    return dq * c + pltpu.roll(dq * s1, 16, 1) + pltpu.roll(dq * s2, width - 16, 1)


def _conf_conv(g1, w_ref):
    acc = jnp.zeros_like(g1)
    for k in range(CONF_K):
        acc = acc + w_ref[k:k + 1, :] * _sh(g1, CONF_K - 1 - k)
    return acc


def _conf_tail(c, cg, lg, lb):
    mu = jnp.mean(c, axis=-1, keepdims=True)
    xc = c - mu
    var = jnp.mean(xc * xc, axis=-1, keepdims=True)
    n = xc * lax.rsqrt(var + EPS) * lg + lb
    return _silu(n) * _silu(cg)


def _misc_spec(misc, row0, rows):
    assert row0 % rows == 0
    return pl.BlockSpec((None, N_CHIPS, rows, 256), lambda i: (misc[1], 0, row0 // rows, 0))


def _chip_columns(x, w_ref, row0, rows, lanes=256):
    return jnp.concatenate([_dot(x, w_ref[k, row0:row0 + rows, 0:lanes]) for k in range(N_CHIPS)], axis=1)


def branches_fwd(z_br, rope, pwbd, pscale, gq, gkv, misc, conf_w, conf_vec, sc_w, li):
    rows = z_br.shape[0]
    t = ROW_TILE
    cur, prev, _, _, layer = _tile_specs(t, rows // HALO, li)

    def body(zc_ref, zp_ref, rope_ref, pw_ref, ps_ref, gq_ref, gkv_ref, up_ref, cw_ref, cv_ref, sw_ref,
             ua_ref, uc_ref, ud_ref, q_ref, k_ref, v_ref):
        i = pl.program_id(0)
        zp = jnp.where(i == 0, jnp.zeros(zp_ref.shape, zp_ref.dtype), zp_ref[...])

        def ext(lo, w=256):
            return jnp.concatenate([zp[:, lo:lo + w], zc_ref[:, lo:lo + w]], axis=0).astype(F32)

        def col(lo, w=256):
            return zc_ref[:, lo:lo + w].astype(F32)

        v = ext(PV)
        p = (_pool_window_sums(v, _sh) / _pool_counts(i * t - HALO, t + HALO) - v)[HALO:]
        ya = _dot(p.astype(BF16), pw_ref[...]) * ps_ref[...]
        ua_ref[...] = (ya * _silu(col(PG))).astype(BF16)

        g1 = ext(CA) * _sigmoid(ext(CGT))
        c = _conf_conv(g1, cw_ref)[HALO:] + cv_ref[0:1, :]
        uc_ref[...] = _conf_tail(c, col(CG), cv_ref[1:2, :], cv_ref[2:3, :]).astype(BF16)

        e = ext(C2) * ext(XV)
        f = jnp.zeros_like(e)
        for k in range(SC_K):
            f = f + sw_ref[k:k + 1, :] * _sh(e, SC_K - 1 - k)
        ud_ref[...] = (col(BG) * f[HALO:] * _silu(col(SG))).astype(BF16)

        cth, s1, s2 = rope_ref[:, 0:128], rope_ref[:, 128:256], rope_ref[:, 256:384]
        qn = _rms(col(CQ), gq_ref[...]).astype(BF16)
        q = _chip_columns(qn, up_ref, 0, 256)
        w8 = HEADS * HEAD_PAD
        q_ref[...] = (_rope(q, _lanes8(cth), _lanes8(s1), _lanes8(s2), w8) * Q_SCALE).astype(BF16)
        kvn = _rms(col(CKV, 128), gkv_ref[...]).astype(BF16)
        kr = _rope(col(KR, 128), cth, s1, s2, HEAD_PAD)
        k_ref[...] = (_chip_columns(kvn, up_ref, M_UKVK - M_UQ, 128) + _lanes8(kr)).astype(BF16)
        v_ref[...] = _chip_columns(kvn, up_ref, M_UKVV - M_UQ, 128, 2 * V_DIM).astype(BF16)

    outs = [jax.ShapeDtypeStruct((rows, 256), BF16)] * 3 + [jax.ShapeDtypeStruct((rows, 1024), BF16)] * 2 + [
        jax.ShapeDtypeStruct((rows, 512), BF16)]
    return pl.pallas_call(
        body, name="branches_fwd", grid=(rows // t,),
        in_specs=[cur(ZB), prev(ZB), cur(384), layer((256, 256)), layer((1, 256)), layer((1, 256)), layer((1, 128)),
                  _misc_spec(misc, M_UQ, M_WO - M_UQ), layer((32, 256)), layer((8, 256)), layer((8, 256))],
        out_specs=[cur(256), cur(256), cur(256), cur(1024), cur(1024), cur(512)], out_shape=outs,
        compiler_params=_params(("parallel",)),
    )(z_br, z_br, rope, pwbd, pscale, gq, gkv, misc[0], conf_w, conf_vec, sc_w)


def _head_lane_mask(h):
    lane = lax.broadcasted_iota(jnp.int32, (1, 2 * V_DIM), 1)
    return (lane >= V_DIM * h) & (lane < V_DIM * (h + 1))


def attention_fwd(q, k, v, gather=None):
    rows = q.shape[0]
    tq = ROW_TILE
    nq = rows // tq
    n = 0 if gather is None else len(gather[0])

    def body(*refs):
        if n:
            start, finish = _gather_ops(refs[3:3 + n], refs[5 + 2 * n:5 + 3 * n], refs[5 + 3 * n:], gather[2], True)
            pl.when((pl.program_id(0) == 0) & (pl.program_id(1) == 0))(start)
        compute(*refs[:3], *refs[3 + 2 * n:5 + 2 * n])
        if n:
            pl.when((pl.program_id(0) == HEADS // 2 - 1) & (pl.program_id(1) == nq - 1))(finish)

    def compute(q_ref, k_ref, v_ref, o_ref, lse_ref):
        i = pl.program_id(1)

        def head_step(h, tile, n_tiles, carry, masked):
            m, l, acc = carry
            width = n_tiles * tq
            r0 = pl.multiple_of(tile * tq, tq)
            kh = k_ref[pl.ds(r0, width), HEAD_PAD * h:HEAD_PAD * (h + 1)]
            vh = jnp.where(_head_lane_mask(h), v_ref[pl.ds(r0, width), :], jnp.zeros((), BF16))
            s = _dot_nt(q_ref[:, HEAD_PAD * h:HEAD_PAD * (h + 1)], kh)
            if masked:
                row = lax.broadcasted_iota(jnp.int32, (tq, width), 0)
                colm = lax.broadcasted_iota(jnp.int32, (tq, width), 1)
                s = jnp.where(colm <= row + (width - tq), s, -1e30)
            m2 = jnp.maximum(m, jnp.max(s, axis=-1, keepdims=True))
            alpha = jnp.exp(m - m2)
            pr = jnp.exp(s - m2)
            return m2, alpha * l + jnp.sum(pr, axis=-1, keepdims=True), alpha * acc + _dot(pr.astype(BF16), vh)

        def step(tile, n_tiles, carry, masked):
            return tuple(head_step(h, tile, n_tiles, carry[h], masked) for h in range(2))

        init = (jnp.full((tq, 1), -1e30, F32), jnp.zeros((tq, 1), F32), jnp.zeros((tq, 2 * V_DIM), F32))
        group = min(KEY_GROUP, nq)
        carry = lax.fori_loop(0, i // group, lambda t, cr: step(group * t, group, cr, False), (init, init))
        carry = lax.switch(i % group, [functools.partial(lambda cr, r: step(i - r, r + 1, cr, True), r=r) for r in range(group)], carry)
        out = jnp.zeros((tq, 2 * V_DIM), F32)
        for h, (m, l, acc) in enumerate(carry):
            out = out + acc / l
            lse_ref[h] = jnp.broadcast_to(m + jnp.log(l), (tq, LANES))
        o_ref[...] = out.astype(BF16)

    srcs, dsts = ([], []) if gather is None else (list(gather[0]), list(gather[1]))
    outs = pl.pallas_call(
        body, name="attention_fwd" if gather is None else "attention_fwd_gather", grid=(HEADS // 2, nq),
        in_specs=[pl.BlockSpec((tq, 2 * HEAD_PAD), lambda p, i: (i, p)), pl.BlockSpec((rows, 2 * HEAD_PAD), lambda p, i: (0, p)),
                  pl.BlockSpec((rows, 2 * V_DIM), lambda p, i: (0, p))] + [ANY] * (2 * n),
        out_specs=[pl.BlockSpec((tq, 2 * V_DIM), lambda p, i: (i, p)), pl.BlockSpec((2, tq, LANES), lambda p, i: (p, i, 0))] + [ANY] * n,
        out_shape=[jax.ShapeDtypeStruct((rows, HEADS * V_DIM), BF16), jax.ShapeDtypeStruct((HEADS, rows, LANES), F32)] + [
            jax.ShapeDtypeStruct(d.shape, d.dtype) for d in dsts],
        input_output_aliases={3 + n + a: 2 + a for a in range(n)}, scratch_shapes=GATHER_SEMS(n) if n else [],
        compiler_params=_params(("arbitrary", "arbitrary") if n else ("parallel", "parallel")),
    )(q, k, v, *srcs, *dsts)
    return outs[0], outs[1], list(outs[2:])


OUT_PROJECTIONS = ((M_POOL, 256), (M_MLA, 512), (M_CONF, 256), (M_SC, 256))


def _chunks(x, n=N_CHIPS, width=256):
    return [x[:, width * k:width * (k + 1)] for k in range(n)]


def merge_fwd(ua, o_att, uc, ud, z_br, z_gl, bias, misc, gpost, hres, li):
    rows = hres.shape[0]
    t = ROW_TILE
    cur, _, _, _, layer = _tile_specs(t, rows // HALO, li)
    d = D_MODEL

    def body(ua_ref, ob_ref, uc_ref, ud_ref, mg_ref, gl_ref, b_ref, wout_ref, wo_ref, gp_ref, h_ref, ub_ref, mb_ref, o_ref, hn_ref):
        ub = (ob_ref[...].astype(F32) * _silu(mg_ref[...].astype(F32))).astype(BF16)
        ub_ref[...] = ub
        m = jnp.zeros((t, d), F32)
        for idx, (u, (row0, n)) in enumerate(zip((ua_ref[...], ub, uc_ref[...], ud_ref[...]), OUT_PROJECTIONS)):
            gate = _sigmoid(gl_ref[:, d * idx:d * (idx + 1)].astype(F32) + b_ref[:, d * idx:d * (idx + 1)])
            m = m + gate * _chip_columns(u, wout_ref, row0, n)
        mb = m.astype(BF16)
        mb_ref[...] = mb
        o = jnp.concatenate([sum(_dot(mk, wo_ref[k, 256 * j:256 * (j + 1), :]) for k, mk in enumerate(_chunks(mb)))
                             for j in range(N_CHIPS)], axis=1)
        o_ref[...] = o
        hn_ref[...] = h_ref[...] + _rms(o, gp_ref[...])

    return pl.pallas_call(
        body, name="merge_fwd", grid=(rows // t,),
        in_specs=[cur(256), cur(512), cur(256), cur(256), cur(512, MG // 512), cur(ZG), layer((1, ZG)), _misc_spec(misc, 0, 1280),
                  _misc_spec(misc, M_WO, D_MODEL), layer((1, d)), cur(d)],
        out_specs=[cur(512), cur(d), cur(d), cur(d)],
        out_shape=[jax.ShapeDtypeStruct((rows, 512), BF16), jax.ShapeDtypeStruct((rows, d), BF16), jax.ShapeDtypeStruct((rows, d), F32),
                   jax.ShapeDtypeStruct((rows, d), F32)],
        compiler_params=_params(("parallel",)),
    )(ua, o_att, uc, ud, z_br, z_gl, bias, misc[0], misc[0], gpost, hres)


def loss_head(hres, target, n_tokens):
    rows, d = hres.shape
    t = ROW_TILE
    cur, _, _, full, _ = _tile_specs(t, rows // HALO)
    n_steps = rows // t

    def body(h_ref, t_ref, dh_ref, tot_ref, acc_ref):
        i = pl.program_id(0)

        @pl.when(i == 0)
        def _():
            acc_ref[...] = jnp.zeros_like(acc_ref)

        r = i * t + lax.broadcasted_iota(jnp.int32, (t, 1), 0)
        diff = jnp.where((r >= N_META) & (r < N_META + n_tokens), h_ref[...] - t_ref[...], 0.0)
        dh_ref[...] = diff * (1.0 / d)
        acc_ref[...] += jnp.sum(diff * diff, axis=0, keepdims=True)

        @pl.when(i == n_steps - 1)
        def _():
            tot_ref[...] = jnp.broadcast_to(jnp.sum(acc_ref[...], axis=1, keepdims=True), (1, LANES))

    return pl.pallas_call(
        body, name="loss_head", grid=(n_steps,), in_specs=[cur(d), cur(d)], out_specs=[cur(d), full((1, LANES))],
        out_shape=[jax.ShapeDtypeStruct((rows, d), F32), jax.ShapeDtypeStruct((1, LANES), F32)],
        scratch_shapes=[pltpu.VMEM((1, d), F32)], compiler_params=_params(("arbitrary",)),
    )(hres, target)


def _accumulate(i, ref, value):
    @pl.when(i == 0)
    def _():
        ref[...] = value

    @pl.when(i > 0)
    def _():
        ref[...] += value


def postnorm_bwd(dh, o, mb, misc, gpost, li, swap=None):
    rows, d = dh.shape
    t = ROW_TILE
    cur, _, _, full, layer = _tile_specs(t, rows // HALO, li)
    n = 0 if swap is None else len(swap)
    steps = rows // t

    def body(*refs):
        if n:
            start, finish = _swap_ops(refs[5:5 + n], refs[8 + n:8 + 2 * n], refs[8 + 2 * n:])
            pl.when(pl.program_id(0) == 0)(start)
        compute(*refs[:5], *refs[5 + n:8 + n])
        if n:
            pl.when(pl.program_id(0) == steps - 1)(finish)

    def compute(dh_ref, o_ref, mb_ref, wo_ref, gp_ref, dm_ref, dwo_ref, dgp_ref):
        i = pl.program_id(0)
        _, vjp = jax.vjp(_rms, o_ref[...], gp_ref[...])
        do, dg = vjp(dh_ref[...])
        dob = do.astype(BF16)
        dm_ref[...] = jnp.concatenate([sum(_dot_nt(dj, wo_ref[k, 256 * j:256 * (j + 1), :]) for j, dj in enumerate(_chunks(dob)))
                                       for k in range(N_CHIPS)], axis=1)
        dwo = _dot_tn(mb_ref[...], dob)
        for k in range(N_CHIPS):
            _accumulate(i, dwo_ref.at[k], jnp.concatenate(_chunks(dwo[256 * k:256 * (k + 1), :]), axis=0))
        _accumulate(i, dgp_ref, dg)

    sent = [] if swap is None else list(swap)
    outs = pl.pallas_call(
        body, name="postnorm_bwd" if swap is None else "postnorm_bwd_swap", grid=(steps,),
        in_specs=[cur(d), cur(d), cur(d), _misc_spec(misc, M_WO, d), layer((1, d))] + [ANY] * n,
        out_specs=[cur(d), full((N_CHIPS, d, 256)), full((1, d))] + [ANY] * n,
        out_shape=[jax.ShapeDtypeStruct((rows, d), F32), jax.ShapeDtypeStruct((N_CHIPS, d, 256), F32), jax.ShapeDtypeStruct((1, d), F32)] + [
            jax.ShapeDtypeStruct((p.shape[0], p.shape[1] // 2, p.shape[2]), p.dtype) for p in sent],
        scratch_shapes=[pltpu.SemaphoreType.DMA((n,)), pltpu.SemaphoreType.DMA((n,))] if n else [],
        compiler_params=_params(("arbitrary",)),
    )(dh, o, mb, misc[0], gpost, *sent)
    return outs[0], outs[1], outs[2], list(outs[3:])


def merge_bwd(dm, ua, ub, uc, ud, z_gl, bias, misc, li):
    rows, d = dm.shape
    t = ROW_TILE
    cur, _, _, full, layer = _tile_specs(t, rows // HALO, li)
    widths = (256, 512, 256, 256)

    def body(dm_ref, ua_ref, ub_ref, uc_ref, ud_ref, gl_ref, b_ref, w_ref, dua_ref, dub_ref, duc_ref, dud_ref, dgl_ref, dw_ref, db_ref):
        i = pl.program_id(0)
        dm = dm_ref[...]
        groups = ((ua_ref, dua_ref), (ub_ref, dub_ref), (uc_ref, duc_ref), (ud_ref, dud_ref))
        for idx, ((u_ref, du_ref), (row0, n)) in enumerate(zip(groups, OUT_PROJECTIONS)):
            cols = slice(d * idx, d * (idx + 1))
            u = u_ref[...]
            gate = _sigmoid(gl_ref[:, cols].astype(F32) + b_ref[:, cols])
            dgl = dm * _chip_columns(u, w_ref, row0, n) * gate * (1.0 - gate)
            dgl_ref[:, cols] = dgl.astype(BF16)
            _accumulate(i, db_ref.at[:, cols], jnp.sum(dgl, axis=0, keepdims=True))
            dyb = (dm * gate).astype(BF16)
            du_ref[...] = sum(_dot_nt(dyk, w_ref[k, row0:row0 + n, :]) for k, dyk in enumerate(_chunks(dyb)))
            for k, dwk in enumerate(_chunks(_dot_tn(u, dyb))):
                _accumulate(i, dw_ref.at[k, row0:row0 + n, :], dwk)

    return pl.pallas_call(
        body, name="merge_bwd", grid=(rows // t,),
        in_specs=[cur(d), cur(256), cur(512), cur(256), cur(256), cur(ZG), layer((1, ZG)), _misc_spec(misc, 0, 1280)],
        out_specs=[cur(256), cur(512), cur(256), cur(256), cur(ZG), full((N_CHIPS, 1280, 256)), full((1, ZG))],
        out_shape=[jax.ShapeDtypeStruct((rows, w), F32) for w in widths] + [jax.ShapeDtypeStruct((rows, ZG), BF16),
                                                                            jax.ShapeDtypeStruct((N_CHIPS, 1280, 256), F32),
                                                                            jax.ShapeDtypeStruct((1, ZG), F32)],
        compiler_params=_params(("arbitrary",)),
    )(dm, ua, ub, uc, ud, z_gl, bias, misc[0])


def pool_bwd(z_br, dua, pwbd, pscale, dz_buf, li):
    rows = z_br.shape[0]
    t = ROW_TILE
    n_steps = rows // t
    cur, prev, nxt, full, layer = _tile_specs(t, rows // HALO, li)

    def body(zc_ref, zp_ref, zn_ref, dc_ref, dn_ref, pw_ref, ps_ref, _, dz_ref, dpw_ref, dps_ref):
        i = pl.program_id(0)
        zp = jnp.where(i == 0, jnp.zeros(zp_ref.shape, zp_ref.dtype), zp_ref[...])
        zn = jnp.where(i == n_steps - 1, jnp.zeros(zn_ref.shape, zn_ref.dtype), zn_ref[...])
        dun = jnp.where(i == n_steps - 1, jnp.zeros(dn_ref.shape, dn_ref.dtype), dn_ref[...])

        def ext(lo):
            return jnp.concatenate([zp[:, lo:lo + 256], zc_ref[:, lo:lo + 256], zn[:, lo:lo + 256]], axis=0).astype(F32)

        n_ext = t + 2 * HALO
        v, pg = ext(PV), ext(PG)
        cnt = _pool_counts(i * t - HALO, n_ext)
        p = (_pool_window_sums(v, _sh) / cnt - v)[HALO:HALO + t]
        du = jnp.concatenate([jnp.zeros((HALO, 256), F32), dc_ref[...], dun], axis=0)
        dya = du * _silu(pg)
        dypb = (dya * ps_ref[...]).astype(BF16)
        dp = _dot_nt(dypb, pw_ref[...])
        dv = (_pool_window_sums(dp / cnt, _ash) - dp)[HALO:HALO + t]
        pb = p.astype(BF16)
        pw = _dot(pb, pw_ref[...])
        duc, pgc = dc_ref[...], pg[HALO:HALO + t]
        dpg = duc * pw * ps_ref[...] * _silu_grad(pgc)
        dz_ref[...] = jnp.concatenate([dv, dpg], axis=1).astype(BF16)
        _accumulate(i, dpw_ref, _dot_tn(pb, dypb[HALO:HALO + t]))
        _accumulate(i, dps_ref, jnp.sum(dya[HALO:HALO + t] * pw, axis=0, keepdims=True))

    return pl.pallas_call(
        body, name="pool_bwd", grid=(n_steps,),
        in_specs=[cur(ZB), prev(ZB), nxt(ZB), cur(256), nxt(256), layer((256, 256)), layer((1, 256)), ANY],
        out_specs=[cur(512, PV // 512), full((256, 256)), full((1, 256))],
        out_shape=[jax.ShapeDtypeStruct((rows, ZB), BF16), jax.ShapeDtypeStruct((256, 256), F32), jax.ShapeDtypeStruct((1, 256), F32)],
        input_output_aliases={7: 0}, compiler_params=_params(("arbitrary",)),
    )(z_br, z_br, z_br, dua, dua, pwbd, pscale, dz_buf)


def shortconv_bwd(z_br, dud, sc_w, dz_buf, li):
    rows = z_br.shape[0]
    t = ROW_TILE
    n_steps = rows // t
    cur, prev, nxt, full, layer = _tile_specs(t, rows // HALO, li)

    def body(zc_ref, zp_ref, zn_ref, dc_ref, dn_ref, sw_ref, _, dz_ref, dw_ref):
        i = pl.program_id(0)
        zp = jnp.where(i == 0, jnp.zeros(zp_ref.shape, zp_ref.dtype), zp_ref[...])
        zn = jnp.where(i == n_steps - 1, jnp.zeros(zn_ref.shape, zn_ref.dtype), zn_ref[...])
        dun = jnp.where(i == n_steps - 1, jnp.zeros(dn_ref.shape, dn_ref.dtype), dn_ref[...])

        def ext(lo):
            return jnp.concatenate([zp[:, lo:lo + 256], zc_ref[:, lo:lo + 256], zn[:, lo:lo + 256]], axis=0).astype(F32)

        mid = slice(HALO, HALO + t)
        bg, c2, xv, sg = ext(BG), ext(C2), ext(XV), ext(SG)
        du = jnp.concatenate([jnp.zeros((HALO, 256), F32), dc_ref[...], dun], axis=0)
        e = c2 * xv
        shifted = [_sh(e, SC_K - 1 - k) for k in range(SC_K)]
        f = sum(sw_ref[k:k + 1, :] * shifted[k] for k in range(SC_K))
        gate = _silu(sg)
        df = du * gate * bg
        de = sum(sw_ref[k:k + 1, :] * _ash(df, SC_K - 1 - k) for k in range(SC_K))
        dbg = du * gate * f
        dsg = du * bg * f * _silu_grad(sg)
        dz_ref[...] = jnp.concatenate([dbg[mid], (de * xv)[mid], (de * c2)[mid], dsg[mid]], axis=1).astype(BF16)
        dw = jnp.concatenate([jnp.sum((df * shifted[k])[mid], axis=0, keepdims=True) for k in range(SC_K)] + [
            jnp.zeros((8 - SC_K, 256), F32)], axis=0)
        _accumulate(i, dw_ref, dw)

    return pl.pallas_call(
        body, name="shortconv_bwd", grid=(n_steps,), in_specs=[cur(ZB), prev(ZB), nxt(ZB), cur(256), nxt(256), layer((8, 256)), ANY],
        out_specs=[cur(1024, BG // 1024), full((8, 256))],
        out_shape=[jax.ShapeDtypeStruct((rows, ZB), BF16), jax.ShapeDtypeStruct((8, 256), F32)],
        input_output_aliases={6: 0}, compiler_params=_params(("arbitrary",)),
    )(z_br, z_br, z_br, dud, dud, sc_w, dz_buf)


def conformer_bwd_tail(z_br, duc, conf_w, conf_vec, dz_buf, li):
    rows = z_br.shape[0]
    t = ROW_TILE
    cur, prev, _, full, layer = _tile_specs(t, rows // HALO, li)

    def body(zc_ref, zp_ref, du_ref, cw_ref, cv_ref, _, dc_ref, dcg_ref, dv_ref):
        i = pl.program_id(0)
        zp = jnp.where(i == 0, jnp.zeros(zp_ref.shape, zp_ref.dtype), zp_ref[...])

        def ext(lo):
            return jnp.concatenate([zp[:, lo:lo + 256], zc_ref[:, lo:lo + 256]], axis=0).astype(F32)

        g1 = ext(CA) * _sigmoid(ext(CGT))
        c = _conf_conv(g1, cw_ref)[HALO:] + cv_ref[0:1, :]
        _, vjp = jax.vjp(_conf_tail, c, zc_ref[:, CG:CG + 256].astype(F32), cv_ref[1:2, :], cv_ref[2:3, :])
        dc, dcg, dlg, dlb = vjp(du_ref[...])
        dc_ref[...] = dc
        dcg_ref[...] = dcg.astype(BF16)
        dvec = jnp.concatenate([dlg, dlb, jnp.sum(dc, axis=0, keepdims=True), jnp.zeros((5, 256), F32)], axis=0)
        _accumulate(i, dv_ref, dvec)

    return pl.pallas_call(
        body, name="conformer_bwd_tail", grid=(rows // t,), in_specs=[cur(ZB), prev(ZB), cur(256), layer((32, 256)), layer((8, 256)), ANY],
        out_specs=[cur(256), cur(256, CG // 256), full((8, 256))],
        out_shape=[jax.ShapeDtypeStruct((rows, 256), F32), jax.ShapeDtypeStruct((rows, ZB), BF16), jax.ShapeDtypeStruct((8, 256), F32)],
        input_output_aliases={5: 1}, compiler_params=_params(("arbitrary",)),
    )(z_br, z_br, duc, conf_w, conf_vec, dz_buf)


def conformer_bwd_conv(z_br, dc, conf_w, dz_buf, li):
    rows = z_br.shape[0]
    t = ROW_TILE
    n_steps = rows // t
    cur, prev, nxt, full, layer = _tile_specs(t, rows // HALO, li)

    def body(zc_ref, zp_ref, dc_ref, dn_ref, cw_ref, _, dz_ref, dw_ref):
        i = pl.program_id(0)
        zp = jnp.where(i == 0, jnp.zeros(zp_ref.shape, zp_ref.dtype), zp_ref[...])
        dcn = jnp.where(i == n_steps - 1, jnp.zeros(dn_ref.shape, dn_ref.dtype), dn_ref[...])

        def ext(lo):
            return jnp.concatenate([zp[:, lo:lo + 256], zc_ref[:, lo:lo + 256]], axis=0).astype(F32)

        a, gt = ext(CA), ext(CGT)
        sg = _sigmoid(gt)
        g1 = a * sg
        dc = dc_ref[...]
        dce = jnp.concatenate([dc, dcn], axis=0)
        dg1 = jnp.zeros_like(dce)
        dws = []
        for k in range(CONF_K):
            dg1 = dg1 + cw_ref[k:k + 1, :] * _ash(dce, CONF_K - 1 - k)
            dws.append(jnp.sum(dc * _sh(g1, CONF_K - 1 - k)[HALO:], axis=0, keepdims=True))
        dg1 = dg1[:t]
        ac, sc = a[HALO:], sg[HALO:]
        dz_ref[...] = jnp.concatenate([dg1 * sc, dg1 * ac * sc * (1.0 - sc)], axis=1).astype(BF16)
        _accumulate(i, dw_ref, jnp.concatenate(dws + [jnp.zeros((32 - CONF_K, 256), F32)], axis=0))

    return pl.pallas_call(
        body, name="conformer_bwd_conv", grid=(n_steps,), in_specs=[cur(ZB), prev(ZB), cur(256), nxt(256), layer((32, 256)), ANY],
        out_specs=[cur(512, CA // 512), full((32, 256))],
        out_shape=[jax.ShapeDtypeStruct((rows, ZB), BF16), jax.ShapeDtypeStruct((32, 256), F32)],
        input_output_aliases={5: 0}, compiler_params=_params(("arbitrary",)),
    )(z_br, z_br, dc, dc, conf_w, dz_buf)


def attention_bwd_prep(dub, o_att, z_br, dz_buf):
    rows = dub.shape[0]
    t = ROW_TILE
    cur, _, _, _, _ = _tile_specs(t, rows // HALO)

    def body(du_ref, o_ref, mg_ref, _, do_ref, dmg_ref, delta_ref):
        du, o, mg = du_ref[...], o_ref[...].astype(F32), mg_ref[...].astype(F32)
        do = du * _silu(mg)
        do_ref[...] = do.astype(BF16)
        dmg_ref[...] = (du * o * _silu_grad(mg)).astype(BF16)
        prod = do * o
        lane = lax.broadcasted_iota(jnp.int32, (1, HEADS * V_DIM), 1)
        for h in range(HEADS):
            part = jnp.where((lane >= V_DIM * h) & (lane < V_DIM * (h + 1)), prod, 0.0)
            delta_ref[h] = jnp.broadcast_to(jnp.sum(part, axis=-1, keepdims=True), (t, LANES))

    return pl.pallas_call(
        body, name="attention_bwd_prep", grid=(rows // t,), in_specs=[cur(512), cur(512), cur(512, MG // 512), ANY],
        out_specs=[cur(512), cur(512, MG // 512), pl.BlockSpec((HEADS, t, LANES), lambda i: (0, i, 0))],
        out_shape=[jax.ShapeDtypeStruct((rows, 512), BF16), jax.ShapeDtypeStruct((rows, ZB), BF16),
                   jax.ShapeDtypeStruct((HEADS, rows, LANES), F32)],
        input_output_aliases={3: 1}, compiler_params=_params(("parallel",)),
    )(dub, o_att, z_br, dz_buf)


def attention_bwd(q, k, v, do, lse, delta, exchange=None):
    rows = q.shape[0]
    tq = ROW_TILE
    nq = rows // tq
    n = 0 if exchange is None else len(exchange[0])

    def body(*refs):
        if n:
            start, finish = _exchange_ops(refs[6:6 + n], refs[9 + 2 * n:9 + 3 * n], refs[9 + 3 * n:], exchange[2])
            pl.when((pl.program_id(0) == 0) & (pl.program_id(1) == 0))(start)
        compute(*refs[:6], *refs[6 + 2 * n:9 + 2 * n])
        if n:
            pl.when((pl.program_id(0) == HEADS // 2 - 1) & (pl.program_id(1) == nq - 1))(finish)

    def compute(q_ref, k_ref, v_ref, do_ref, lse_ref, dl_ref, dq_ref, dk_ref, dv_ref):
        j = pl.program_id(1)

        @pl.when(j == 0)
        def _():
            dq_ref[...] = jnp.zeros_like(dq_ref)

        def head_step(h, tile, n_tiles, dk, dv, diagonal):
            lanes = slice(HEAD_PAD * h, HEAD_PAD * (h + 1))
            hm = _head_lane_mask(h)
            kh = k_ref[:, lanes]
            vh = jnp.where(hm, v_ref[...], jnp.zeros((), BF16))
            r0, width = pl.multiple_of(tile * tq, tq), n_tiles * tq
            qi = q_ref[pl.ds(r0, width), lanes]
            doi = jnp.where(hm, do_ref[pl.ds(r0, width), :], jnp.zeros((), BF16))
            s = _dot_nt(qi, kh)
            if diagonal:
                s = jnp.where(lax.broadcasted_iota(jnp.int32, (tq, tq), 1) <= lax.broadcasted_iota(jnp.int32, (tq, tq), 0), s, -1e30)
            pr = jnp.exp(s - lse_ref[h, pl.ds(r0, width), :][:, 0:1])
            dv = dv + _dot_tn(pr.astype(BF16), doi)
            dp = _dot_nt(doi, vh)
            ds = (pr * (dp - dl_ref[h, pl.ds(r0, width), :][:, 0:1])).astype(BF16)
            dq_ref[pl.ds(r0, width), lanes] += _dot(ds, kh)
            return dk + _dot_tn(ds, qi), dv

        def step(tile, n_tiles, carry, diagonal):
            dk0, dk1, dv = carry
            dk0, dv = head_step(0, tile, n_tiles, dk0, dv, diagonal)
            dk1, dv = head_step(1, tile, n_tiles, dk1, dv, diagonal)
            return dk0, dk1, dv

        zero = jnp.zeros((tq, HEAD_PAD), F32)
        carry = step(j, 1, (zero, zero, jnp.zeros((tq, 2 * V_DIM), F32)), True)
        odd = (nq - 1 - j) % 2
        carry = lax.cond(odd == 1, lambda cr: step(j + 1, 1, cr, False), lambda cr: cr, carry)
        dk0, dk1, dv = lax.fori_loop(0, (nq - 1 - j) // 2, lambda t, cr: step(j + 1 + odd + 2 * t, 2, cr, False), carry)
        dk_ref[:, 0:HEAD_PAD] = dk0
        dk_ref[:, HEAD_PAD:2 * HEAD_PAD] = dk1
        dv_ref[...] = dv

    srcs, dsts = ([], []) if exchange is None else (list(exchange[0]), list(exchange[1]))
    outs = pl.pallas_call(
        body, name="attention_bwd" if exchange is None else "attention_bwd_exchange", grid=(HEADS // 2, nq),
        in_specs=[pl.BlockSpec((rows, 2 * HEAD_PAD), lambda p, j: (0, p)), pl.BlockSpec((tq, 2 * HEAD_PAD), lambda p, j: (j, p)),
                  pl.BlockSpec((tq, 2 * V_DIM), lambda p, j: (j, p)), pl.BlockSpec((rows, 2 * V_DIM), lambda p, j: (0, p)),
                  pl.BlockSpec((2, rows, LANES), lambda p, j: (p, 0, 0)), pl.BlockSpec((2, rows, LANES), lambda p, j: (p, 0, 0))] + [
                      ANY] * (2 * n),
        out_specs=[pl.BlockSpec((rows, 2 * HEAD_PAD), lambda p, j: (0, p)), pl.BlockSpec((tq, 2 * HEAD_PAD), lambda p, j: (j, p)),
                   pl.BlockSpec((tq, 2 * V_DIM), lambda p, j: (j, p))] + [ANY] * n,
        out_shape=[jax.ShapeDtypeStruct((rows, HEADS * HEAD_PAD), F32), jax.ShapeDtypeStruct((rows, HEADS * HEAD_PAD), F32),
                   jax.ShapeDtypeStruct((rows, HEADS * V_DIM), F32)] + [jax.ShapeDtypeStruct(d.shape, d.dtype) for d in dsts],
        input_output_aliases={6 + n + a: 3 + a for a in range(n)}, scratch_shapes=EXCHANGE_SEMS(n) if n else [],
        compiler_params=_params(("arbitrary", "arbitrary") if n else ("parallel", "arbitrary")),
    )(q, k, v, do, lse, delta, *srcs, *dsts)
    return outs[0], outs[1], outs[2], list(outs[3:])


def mla_prep_bwd(dq, dk, dv, z_br, rope, gq, gkv, misc, dz_buf, li):
    rows = dq.shape[0]
    t = ROW_TILE
    cur, _, _, full, layer = _tile_specs(t, rows // HALO, li)
    w8 = HEADS * HEAD_PAD
    uq, keys, values = slice(0, 256), slice(M_UKVK - M_UQ, M_UKVV - M_UQ), slice(M_UKVV - M_UQ, M_WO - M_UQ)

    def body(dq_ref, dk_ref, dv_ref, z_ref, rope_ref, gq_ref, gkv_ref, up_ref, _, dz_ref, dup_ref, dgq_ref, dgkv_ref):
        i = pl.program_id(0)
        cth, s1, s2 = rope_ref[:, 0:128], rope_ref[:, 128:256], rope_ref[:, 256:384]
        dqb = _rope_transposed(dq_ref[...] * Q_SCALE, _lanes8(cth), _lanes8(s1), _lanes8(s2), w8).astype(BF16)
        dq_chunks = _chunks(dqb)
        cq = z_ref[:, 0:256].astype(F32)
        qn, vjp_q = jax.vjp(_rms, cq, gq_ref[...])
        dcq, dgq = vjp_q(sum(_dot_nt(dqk, up_ref[k, uq, :]) for k, dqk in enumerate(dq_chunks)))
        _accumulate(i, dgq_ref, dgq)

        dk = dk_ref[...]
        dkr = sum(dk[:, HEAD_PAD * h:HEAD_PAD * (h + 1)] for h in range(HEADS))
        dkr = _rope_transposed(dkr, cth, s1, s2, HEAD_PAD)
        lane = lax.broadcasted_iota(jnp.int32, (1, HEAD_PAD), 1)
        dkr = jnp.where((lane >= QK_NOPE) & (lane < QK_NOPE + QK_ROPE), dkr, 0.0)
        dkb, dvb = dk.astype(BF16), dv_ref[...].astype(BF16)
        dk_chunks, dv_chunks = _chunks(dkb), _chunks(dvb, width=2 * V_DIM)
        ckv = z_ref[:, 256:384].astype(F32)
        kvn, vjp_kv = jax.vjp(_rms, ckv, gkv_ref[...])
        dckv, dgkv = vjp_kv(sum(_dot_nt(dk_chunks[k], up_ref[k, keys, :]) + _dot_nt(dv_chunks[k], up_ref[k, values, 0:2 * V_DIM])
                                for k in range(N_CHIPS)))
        _accumulate(i, dgkv_ref, dgkv)
        dz_ref[...] = jnp.concatenate([dcq, dckv, dkr], axis=1).astype(BF16)
        qnb, kvnb = qn.astype(BF16), kvn.astype(BF16)
        d_uq, d_keys, d_values = _chunks(_dot_tn(qnb, dqb)), _chunks(_dot_tn(kvnb, dkb)), _chunks(_dot_tn(kvnb, dvb), width=2 * V_DIM)
        for k in range(N_CHIPS):
            padded = jnp.concatenate([d_values[k], jnp.zeros((128, 256 - 2 * V_DIM), F32)], axis=1)
            _accumulate(i, dup_ref.at[k], jnp.concatenate([d_uq[k], d_keys[k], padded], axis=0))

    return pl.pallas_call(
        body, name="mla_prep_bwd", grid=(rows // t,),
        in_specs=[cur(w8), cur(w8), cur(512), cur(512, CQ // 512), cur(384), layer((1, 256)), layer((1, 128)),
                  _misc_spec(misc, M_UQ, M_WO - M_UQ), ANY],
        out_specs=[cur(512, CQ // 512), full((N_CHIPS, M_WO - M_UQ, 256)), full((1, 256)), full((1, 128))],
        out_shape=[jax.ShapeDtypeStruct((rows, ZB), BF16), jax.ShapeDtypeStruct((N_CHIPS, M_WO - M_UQ, 256), F32),
                   jax.ShapeDtypeStruct((1, 256), F32), jax.ShapeDtypeStruct((1, 128), F32)],
        input_output_aliases={8: 0}, compiler_params=_params(("arbitrary",)),
    )(dq, dk, dv, z_br, rope, gq, gkv, misc[0], dz_buf)


def prenorm_bwd(dz_br, w_br, dh_gl, hres, gpre, dh_next, li):
    rows, d = hres.shape
    t = ROW_TILE
    cur, _, _, full, layer = _tile_specs(t, rows // HALO, li)

    def body(dz_ref, w_ref, dp_ref, x_ref, g_ref, dn_ref, dx_ref, dg_ref):
        i = pl.program_id(0)
        dh = _dot_nt(dz_ref[...], w_ref[...]) + dp_ref[...]
        _, vjp = jax.vjp(_rms, x_ref[...], g_ref[...])
        dx, dg = vjp(dh)
        dx_ref[...] = dx + dn_ref[...]
        _accumulate(i, dg_ref, dg)

    return pl.pallas_call(
        body, name="prenorm_bwd", grid=(rows // t,), in_specs=[cur(ZB), layer((d, ZB), 0), cur(d), cur(d), layer((1, d)), cur(d)],
        out_specs=[cur(d), full((1, d))], out_shape=[jax.ShapeDtypeStruct((rows, d), F32), jax.ShapeDtypeStruct((1, d), F32)],
        compiler_params=_params(("arbitrary",)),
    )(dz_br, w_br, dh_gl, hres, gpre, dh_next)


def _mesh_position():
    return lax.axis_index("x"), lax.axis_index("y"), lax.axis_index("c")


def chip_exchange(src, gather, name):
    block = src.shape if gather else src.shape[1:]

    def body(src_ref, dst_ref, send_sems, recv_sems, local_sem):
        x, y, c = _mesh_position()
        me = 2 * x + y
        peers = ((1 - x, y), (x, 1 - y), (1 - x, 1 - y))

        def part(k):
            return src_ref if gather else src_ref.at[k]

        def copy(j, slot):
            px, py = peers[j]
            return pltpu.make_async_remote_copy(src_ref=part(2 * px + py), dst_ref=dst_ref.at[slot], send_sem=send_sems.at[j],
                                                recv_sem=recv_sems.at[j], device_id=(px, py, c), device_id_type=MESH)

        local = pltpu.make_async_copy(part(me), dst_ref.at[me], local_sem)
        local.start()
        sends = [copy(j, me) for j in range(3)]
        for cp in sends:
            cp.start()
        for j, (px, py) in enumerate(peers):
            copy(j, 2 * px + py).wait_recv()
        for cp in sends:
            cp.wait_send()
        local.wait()

    return pl.pallas_call(
        body, name=name, in_specs=[pl.BlockSpec(memory_space=pl.ANY)], out_specs=pl.BlockSpec(memory_space=pl.ANY),
        out_shape=jax.ShapeDtypeStruct((N_CHIPS,) + tuple(block), src.dtype),
        scratch_shapes=[pltpu.SemaphoreType.DMA((3,)), pltpu.SemaphoreType.DMA((3,)), pltpu.SemaphoreType.DMA(())],
    )(src)


def sibling_swap(src, name):
    def body(src_ref, dst_ref, send_sem, recv_sem):
        x, y, c = _mesh_position()
        cp = pltpu.make_async_remote_copy(src_ref=src_ref, dst_ref=dst_ref, send_sem=send_sem, recv_sem=recv_sem,
                                          device_id=(x, y, 1 - c), device_id_type=MESH)
        cp.start()
        cp.wait()

    return pl.pallas_call(
        body, name=name, in_specs=[pl.BlockSpec(memory_space=pl.ANY)], out_specs=pl.BlockSpec(memory_space=pl.ANY),
        out_shape=jax.ShapeDtypeStruct(src.shape, src.dtype),
        scratch_shapes=[pltpu.SemaphoreType.DMA(()), pltpu.SemaphoreType.DMA(())],
    )(src)


def _comm_call(body, name, n_in, out_shapes, n_sems):
    return pl.pallas_call(
        body, name=name, in_specs=[ANY] * n_in, out_specs=[ANY] * len(out_shapes), out_shape=out_shapes,
        scratch_shapes=[pltpu.SemaphoreType.DMA((n,)) for n in n_sems])


def _row_halves(c, rows):
    half = rows // 2
    return pl.ds(pl.multiple_of(c * half, 16), half), pl.ds(pl.multiple_of((1 - c) * half, 16), half)


def _peers():
    x, y, c = _mesh_position()
    return x, y, c, 2 * x + y, ((1 - x, y), (x, 1 - y), (1 - x, 1 - y))


def _gather_ops(src, dst, sems, layer, own_copy):
    ici_send, ici_recv, d2d_send, d2d_recv, own_sems = sems
    n = len(src)

    def fetch(a, j, slot):
        x, y, c, _, peers = _peers()
        px, py = peers[j]
        mine, _ = _row_halves(c, src[a].shape[1])
        return pltpu.make_async_remote_copy(src_ref=src[a].at[layer, mine], dst_ref=dst[a].at[layer, slot, mine], send_sem=ici_send.at[3 * a + j],
                                            recv_sem=ici_recv.at[3 * a + j], device_id=(px, py, c), device_id_type=MESH)

    def forward(a, j, sibling_half):
        x, y, c, _, peers = _peers()
        px, py = peers[j]
        part = dst[a].at[layer, 2 * px + py, _row_halves(c, src[a].shape[1])[1 if sibling_half else 0]]
        return pltpu.make_async_remote_copy(src_ref=part, dst_ref=part, send_sem=d2d_send.at[3 * a + j], recv_sem=d2d_recv.at[3 * a + j],
                                            device_id=(x, y, 1 - c), device_id_type=MESH)

    def own(a):
        return pltpu.make_async_copy(src[a].at[layer], dst[a].at[layer, _peers()[3]], own_sems.at[a])

    def start():
        me = _peers()[3]
        for a in range(n):
            if own_copy:
                own(a).start()
            for j in range(3):
                fetch(a, j, me).start()

    def finish():
        peers = _peers()[4]
        for j, (px, py) in enumerate(peers):
            for a in range(n):
                fetch(a, j, 2 * px + py).wait_recv()
                forward(a, j, False).start()
        for j in range(3):
            for a in range(n):
                forward(a, j, True).wait_recv()
        for j in range(3):
            for a in range(n):
                fetch(a, j, 0).wait_send()
                forward(a, j, False).wait_send()
        if own_copy:
            for a in range(n):
                own(a).wait()

    return start, finish


def _exchange_ops(src, dst, sems, layer):
    send_sems, recv_sems, own_sems = sems
    n = len(src)

    def copy(a, j, slot):
        x, y, c, _, peers = _peers()
        px, py = peers[j]
        return pltpu.make_async_remote_copy(src_ref=src[a].at[2 * px + py], dst_ref=dst[a].at[layer, slot], send_sem=send_sems.at[3 * a + j],
                                            recv_sem=recv_sems.at[3 * a + j], device_id=(px, py, c), device_id_type=MESH)

    def own(a):
        me = _peers()[3]
        return pltpu.make_async_copy(src[a].at[me], dst[a].at[layer, me], own_sems.at[a])

    def start():
        me = _peers()[3]
        for a in range(n):
            own(a).start()
            for j in range(3):
                copy(a, j, me).start()

    def finish():
        peers = _peers()[4]
        for j, (px, py) in enumerate(peers):
            for a in range(n):
                copy(a, j, 2 * px + py).wait_recv()
        for j in range(3):
            for a in range(n):
                copy(a, j, 0).wait_send()
        for a in range(n):
            own(a).wait()

    return start, finish


GATHER_SEMS = lambda n: [pltpu.SemaphoreType.DMA((3 * n,))] * 4 + [pltpu.SemaphoreType.DMA((n,))]
EXCHANGE_SEMS = lambda n: [pltpu.SemaphoreType.DMA((3 * n,))] * 2 + [pltpu.SemaphoreType.DMA((n,))]


def gather_layer(srcs, dsts, layer, name):
    n = len(srcs)

    def body(*refs):
        start, finish = _gather_ops(refs[:n], refs[2 * n:3 * n], refs[3 * n:], layer, False)
        start()
        finish()

    return pl.pallas_call(
        body, name=name, in_specs=[ANY] * (2 * n), out_specs=[ANY] * n, out_shape=[jax.ShapeDtypeStruct(d.shape, d.dtype) for d in dsts],
        input_output_aliases={n + a: a for a in range(n)}, scratch_shapes=GATHER_SEMS(n),
    )(*srcs, *dsts)


def exchange_layer(ss, dsts, layer, name):
    n = len(ss)

    def body(*refs):
        start, finish = _exchange_ops(refs[:n], refs[2 * n:3 * n], refs[3 * n:], layer)
        start()
        finish()

    return pl.pallas_call(
        body, name=name, in_specs=[ANY] * (2 * n), out_specs=[ANY] * n, out_shape=[jax.ShapeDtypeStruct(d.shape, d.dtype) for d in dsts],
        input_output_aliases={n + a: a for a in range(n)}, scratch_shapes=EXCHANGE_SEMS(n),
    )(*ss, *dsts)


def _swap_ops(src, dst, sems):
    send_sems, recv_sems = sems

    def copy(a):
        x, y, c = _mesh_position()
        return pltpu.make_async_remote_copy(src_ref=src[a].at[:, _row_halves(c, src[a].shape[1])[1]], dst_ref=dst[a], send_sem=send_sems.at[a],
                                            recv_sem=recv_sems.at[a], device_id=(x, y, 1 - c), device_id_type=MESH)

    def start():
        for a in range(len(src)):
            copy(a).start()

    def finish():
        for a in range(len(src)):
            copy(a).wait()

    return start, finish


def swap_row_halves(ps, name):
    n = len(ps)

    def body(*refs):
        start, finish = _swap_ops(refs[:n], refs[n:2 * n], refs[2 * n:])
        start()
        finish()

    outs = [jax.ShapeDtypeStruct((p.shape[0], p.shape[1] // 2, p.shape[2]), p.dtype) for p in ps]
    return _comm_call(body, name, n, outs, (n, n))(*ps)


def add_row_halves(ps, rs, c, name):
    n = len(ps)

    def body(c_ref, *refs):
        for p_ref, r_ref, o_ref in zip(refs[:n], refs[n:2 * n], refs[2 * n:]):
            o_ref[...] = (p_ref[...].astype(F32) + r_ref[...].astype(F32)).astype(BF16)

    mine = [pl.BlockSpec((1,) + r.shape[1:], lambda k, c_ref: (k, c_ref[0], 0)) for r in rs]
    whole = [pl.BlockSpec((1,) + r.shape[1:], lambda k, c_ref: (k, 0, 0)) for r in rs]
    return pl.pallas_call(
        body, name=name, out_shape=[jax.ShapeDtypeStruct(r.shape, BF16) for r in rs],
        grid_spec=pltpu.PrefetchScalarGridSpec(num_scalar_prefetch=1, grid=(N_CHIPS,), in_specs=mine + whole, out_specs=whole),
        compiler_params=_params(("parallel",)),
    )(jnp.reshape(c, (1,)).astype(jnp.int32), *ps, *rs)


def sum_row_halves(l, c, name):
    layers, n, half, cols = l.shape
    rb = _row_block(half, cols, 4)
    steps = half // rb

    def body(c_ref, l_ref, o_ref):
        acc = l_ref[0, 0].astype(F32)
        for s in range(1, n):
            acc = acc + l_ref[0, s].astype(F32)
        o_ref[0] = acc

    return pl.pallas_call(
        body, name=name, out_shape=jax.ShapeDtypeStruct((layers, 2 * half, cols), F32),
        grid_spec=pltpu.PrefetchScalarGridSpec(
            num_scalar_prefetch=1, grid=(layers, steps), in_specs=[pl.BlockSpec((1, n, rb, cols), lambda a, i, c_ref: (a, 0, i, 0))],
            out_specs=pl.BlockSpec((1, rb, cols), lambda a, i, c_ref: (a, c_ref[0] * steps + i, 0))),
        compiler_params=_params(("parallel", "parallel")),
    )(jnp.reshape(c, (1,)).astype(jnp.int32), l)


def share_row_halves(gs, name):
    n = len(gs)

    def body(*refs):
        dst = refs[n:2 * n]
        send_sems, recv_sems = refs[2 * n:]
        x, y, c = _mesh_position()

        def copy(a, sibling_half):
            part = dst[a].at[:, _row_halves(c, dst[a].shape[1])[1 if sibling_half else 0]]
            return pltpu.make_async_remote_copy(src_ref=part, dst_ref=part, send_sem=send_sems.at[a], recv_sem=recv_sems.at[a],
                                                device_id=(x, y, 1 - c), device_id_type=MESH)

        for a in range(n):
            copy(a, False).start()
        for a in range(n):
            copy(a, True).wait_recv()
        for a in range(n):
            copy(a, False).wait_send()

    return pl.pallas_call(
        body, name=name, in_specs=[ANY] * n, out_specs=[ANY] * n, out_shape=[jax.ShapeDtypeStruct(g.shape, g.dtype) for g in gs],
        input_output_aliases={a: a for a in range(n)}, scratch_shapes=[pltpu.SemaphoreType.DMA((n,)), pltpu.SemaphoreType.DMA((n,))],
    )(*gs)


def _row_block(rows, cols, itemsize):
    best = 16
    for rb in range(16, rows + 1, 16):
        if rows % rb == 0 and rb * cols * itemsize <= 2 * 1024 * 1024:
            best = rb
    assert rows % best == 0, (rows, cols)
    return best


def _comm_block(rows):
    return 1024 if rows % 1024 == 0 else rows


def sum_slots(buf, name):
    n, r, c = buf.shape
    rb = _comm_block(r)

    def body(b_ref, o_ref):
        acc = b_ref[0].astype(F32)
        for s in range(1, n):
            acc = acc + b_ref[s].astype(F32)
        o_ref[...] = acc

    return pl.pallas_call(
        body, name=name, grid=(r // rb,), in_specs=[pl.BlockSpec((n, rb, c), lambda i: (0, i, 0))],
        out_specs=pl.BlockSpec((rb, c), lambda i: (i, 0)), out_shape=jax.ShapeDtypeStruct((r, c), F32),
        compiler_params=_params(("parallel",)),
    )(buf)


def add_pair(a, b, out_dtype, name):
    shape = a.shape
    a2, b2 = a.reshape(-1, shape[-1]), b.reshape(-1, shape[-1])
    r, c = a2.shape
    rb = _comm_block(r)

    def body(a_ref, b_ref, o_ref):
        o_ref[...] = (a_ref[...].astype(F32) + b_ref[...].astype(F32)).astype(out_dtype)

    out = pl.pallas_call(
        body, name=name, grid=(r // rb,), in_specs=[pl.BlockSpec((rb, c), lambda i: (i, 0))] * 2,
        out_specs=pl.BlockSpec((rb, c), lambda i: (i, 0)), out_shape=jax.ShapeDtypeStruct((r, c), out_dtype),
        compiler_params=_params(("parallel",)),
    )(a2, b2)
    return out.reshape(shape)


def adamw(w, g, m, v):
    shape = w.shape
    cols = shape[-1]
    rows = math.prod(shape[:-1])
    if rows * cols <= 256 * 1024:
        rb, cb = rows, cols
    else:
        rb = max(r for r in range(8, 2049, 8) if rows % r == 0)
        cb = cols if rb * cols * 4 <= 2 * 1024 * 1024 else 256
    assert rows % rb == 0 and cols % cb == 0, shape

    def body(w_ref, g_ref, m_ref, v_ref, d_ref, nm_ref, nv_ref):
        g_ = g_ref[...]
        nm = ADAM_B1 * m_ref[...] + (1.0 - ADAM_B1) * g_
        nv = ADAM_B2 * v_ref[...] + (1.0 - ADAM_B2) * (g_ * g_)
        m_hat = nm / (1.0 - ADAM_B1 ** ADAM_STEP)
        v_hat = nv / (1.0 - ADAM_B2 ** ADAM_STEP)
        d_ref[...] = -ADAM_LR * (m_hat / (jnp.sqrt(v_hat) + ADAM_EPS) + ADAM_WD * w_ref[...])
        nm_ref[...] = nm
        nv_ref[...] = nv

    spec = pl.BlockSpec((rb, cb), lambda i, j: (i, j))
    outs = pl.pallas_call(
        body, name="adamw", grid=(rows // rb, cols // cb), in_specs=[spec] * 4, out_specs=[spec] * 3,
        out_shape=[jax.ShapeDtypeStruct((rows, cols), F32)] * 3, compiler_params=_params(("parallel", "parallel")),
    )(*(a.reshape(rows, cols) for a in (w, g, m, v)))
    return tuple(o.reshape(shape) for o in outs)


def _pack(arrays, dtype, row_multiple):
    flat = jnp.concatenate([a.astype(dtype).reshape(-1) for a in arrays])
    per = LANES * row_multiple
    total = -(-flat.shape[0] // per) * per
    return jnp.pad(flat, (0, total - flat.shape[0])).reshape(total // LANES, LANES)


def _unpack(buf, shapes):
    flat = buf.reshape(-1)
    out, off = [], 0
    for s in shapes:
        n = math.prod(s)
        out.append(flat[off:off + n].reshape(s))
        off += n
    return out


def _input_weights(blocks):
    c0, c1, c2, c3 = (blocks[..., k, :, :] for k in range(N_CHIPS))
    pad = lambda n: jnp.zeros(c0.shape[:-1] + (n,), blocks.dtype)
    w_br = jnp.concatenate([c1[..., 376:1400], c0[..., 0:896], pad(64), c0[..., 896:928], pad(32), c0[..., 928:], c1[..., 0:376]], axis=-1)
    return w_br, jnp.concatenate([c1[..., 1400:], c2, c3], axis=-1)


def _input_weights_inverse(dw_br, dw_gl):
    c0 = jnp.concatenate([dw_br[..., 1024:1920], dw_br[..., 1984:2016], dw_br[..., 2048:2952]], axis=-1)
    c1 = jnp.concatenate([dw_br[..., 2952:ZB], dw_br[..., 0:1024], dw_gl[..., 0:432]], axis=-1)
    return jnp.stack([c0, c1, dw_gl[..., 432:2264], dw_gl[..., 2264:]], axis=-3)


def _block_diag(pw):
    zeros = lambda n: jnp.zeros(pw.shape[:-3] + (64, n), pw.dtype)
    rows = [jnp.concatenate([zeros(64 * g), pw[..., g, :, :], zeros(64 * (3 - g))], axis=-1) for g in range(4)]
    return jnp.concatenate(rows, axis=-2)


def _block_diag_inverse(d):
    return jnp.stack([d[..., 64 * g:64 * (g + 1), 64 * g:64 * (g + 1)] for g in range(4)], axis=-3)


def _pad_rows(a, n):
    return jnp.pad(a, ((0, n - a.shape[0]), (0, 0)))


def _rope_tables(rows):
    inv = 1.0 / (ROPE_THETA ** (jnp.arange(0, QK_ROPE, 2, dtype=F32) / QK_ROPE))
    ang = jnp.arange(rows, dtype=F32)[:, None] * inv[None, :]
    cos, sin = jnp.cos(ang), jnp.sin(ang)
    one, zero = jnp.ones((rows, 1), F32), jnp.zeros((rows, 1), F32)
    rep = lambda a, n: jnp.broadcast_to(a, (rows, n))
    c = jnp.concatenate([rep(one, 64), cos, cos, rep(one, 32)], axis=1)
    s1 = jnp.concatenate([rep(zero, 64), -sin, rep(zero, 48)], axis=1)
    s2 = jnp.concatenate([rep(zero, 80), sin, rep(zero, 32)], axis=1)
    return jnp.concatenate([c, s1, s2], axis=1)


def _misc_block(parts):
    lead = parts["w_uq"].shape[:-2]
    pad_last = lambda a, n: jnp.pad(a, [(0, 0)] * (a.ndim - 1) + [(0, n - a.shape[-1])])
    uq = pad_last(parts["w_uq"].reshape(lead + (256, 2, QK_NOPE + QK_ROPE)), HEAD_PAD).reshape(lead + (256, 256))
    kv = parts["w_ukv"].reshape(lead + (128, 2, QK_NOPE + V_DIM))
    keys = pad_last(kv[..., :QK_NOPE], HEAD_PAD).reshape(lead + (128, 256))
    values = pad_last(kv[..., QK_NOPE:].reshape(lead + (128, 2 * V_DIM)), 256)
    wo = jnp.swapaxes(parts["w_o"].reshape(lead + (256, N_CHIPS, 256)), -3, -2).reshape(lead + (D_MODEL, 256))
    gap = jnp.zeros(lead + (M_UQ - M_SC - 256, 256), uq.dtype)
    return jnp.concatenate([parts["w_out_mla"], parts["w_out_pool"], parts["w_out_conf"], parts["w_out_sc"], gap, uq, keys, values, wo],
                           axis=-2)


def _misc_unblock(block):
    lead = block.shape[:-2]
    rows = lambda lo, n: block[..., lo:lo + n, :]
    uq = rows(M_UQ, 256).reshape(lead + (256, 2, HEAD_PAD))[..., :QK_NOPE + QK_ROPE].reshape(lead + (256, 2 * (QK_NOPE + QK_ROPE)))
    keys = rows(M_UKVK, 128).reshape(lead + (128, 2, HEAD_PAD))[..., :QK_NOPE]
    values = rows(M_UKVV, 128)[..., :2 * V_DIM].reshape(lead + (128, 2, V_DIM))
    wo = jnp.swapaxes(rows(M_WO, D_MODEL).reshape(lead + (N_CHIPS, 256, 256)), -3, -2).reshape(lead + (256, D_MODEL))
    return dict(w_out_mla=rows(M_MLA, 512), w_out_pool=rows(M_POOL, 256), w_out_conf=rows(M_CONF, 256), w_out_sc=rows(M_SC, 256), w_uq=uq,
                w_ukv=jnp.concatenate([keys, values], axis=-1).reshape(lead + (128, 256)), w_o=wo)


def _to_chip_blocks(name, a):
    if name == "w_o":
        return a.reshape(a.shape[:-2] + (N_CHIPS, a.shape[-2] // N_CHIPS, a.shape[-1]))
    return jnp.swapaxes(a.reshape(a.shape[:-1] + (N_CHIPS, a.shape[-1] // N_CHIPS)), -3, -2)


def _from_chip_blocks(name, b):
    if name == "w_o":
        return b.reshape(b.shape[:-3] + (N_CHIPS * b.shape[-2], b.shape[-1]))
    s = jnp.swapaxes(b, -3, -2)
    return s.reshape(s.shape[:-2] + (N_CHIPS * s.shape[-1],))


LARGE = ("w_in",) + MISC


def gather_small(shards):
    small = chip_exchange(_pack([shards[n] for n, _, _ in SHARDED_SMALL], F32, 8), True, "gather_small_ici")
    per_chip = [_unpack(small[k], [s for _, s, _ in SHARDED_SMALL]) for k in range(N_CHIPS)]
    return {name: jnp.concatenate([per_chip[k][idx] for k in range(N_CHIPS)], axis=axis) for idx, (name, _, axis) in enumerate(SHARDED_SMALL)}


class LocalWeights:
    def __init__(self, full):
        self.w_in = _to_chip_blocks("w_in", full["w_in"])
        self.misc = _misc_block({n: _to_chip_blocks(n, full[n]) for n in MISC}).astype(BF16)
        self.grads = [None] * DEPTH

    def layer(self, i):
        return self.w_in[i], (self.misc, i)

    def gather_with_attention(self, i):
        return None

    def gathered(self, dsts):
        pass

    def exchange_with_attention(self):
        return None

    def exchanged(self, dsts):
        pass

    def swap_with_postnorm(self):
        return None

    def swapped(self, rs):
        pass

    def put_grads(self, i, w_in, misc):
        self.grads[i] = (w_in, misc)

    def reduced(self):
        out = {n: _from_chip_blocks(n, b) for n, b in _misc_unblock(jnp.stack([m for _, m in self.grads])).items()}
        out["w_in"] = _from_chip_blocks("w_in", jnp.stack([w for w, _ in self.grads]))
        return out


class MeshWeights:
    def __init__(self, shards, c, chip):
        self.c, self.chip = c, chip
        self.srcs = [shards["w_in"].astype(BF16), _misc_block({n: shards[n] for n in MISC}).astype(BF16)]
        dsts = [lax.empty((DEPTH, N_CHIPS) + s.shape[1:], BF16) for s in self.srcs]
        self.dsts = gather_layer(self.srcs, dsts, 0, "gather_layer")
        self.landed = [lax.empty((DEPTH, N_CHIPS, s.shape[1] // 2, s.shape[2]), BF16) for s in self.srcs]
        self.pending = self.to_swap = None

    def layer(self, i):
        if i > 0:
            return self.dsts[0][i], (self.dsts[1], i)
        own = (jnp.arange(N_CHIPS) == self.chip)[:, None, None]
        w_in, misc = (jnp.where(own, s[0][None], d[0]) for s, d in zip(self.srcs, self.dsts))
        return w_in, (misc[None], 0)

    def gather_with_attention(self, i):
        return (self.srcs, self.dsts, i + 1) if i + 1 < DEPTH else None

    def gathered(self, dsts):
        if dsts:
            self.dsts = dsts

    def exchange_with_attention(self):
        return None if self.pending is None else (self.pending[0], self.landed, self.pending[1])

    def exchanged(self, dsts):
        if dsts:
            self.landed, self.pending = dsts, None

    def put_grads(self, i, w_in, misc):
        self.to_swap = ([w_in.astype(BF16), misc.astype(BF16)], i)

    def swap_with_postnorm(self):
        return None if self.to_swap is None else self.to_swap[0]

    def swapped(self, rs):
        if rs:
            ps, i = self.to_swap
            self.pending = (add_row_halves(ps, rs, self.c, "reduce_pair"), i)
            self.to_swap = None

    def reduced(self):
        self.swapped(swap_row_halves(self.to_swap[0], "reduce_swap"))
        landed = exchange_layer(self.pending[0], self.landed, self.pending[1], "reduce_exchange")
        gs = [sum_row_halves(l, self.c, "reduce_sum_%d" % a) for a, l in enumerate(landed)]
        g_in, g_misc = share_row_halves(gs, "reduce_share")
        out = {"w_in": g_in}
        out.update(_misc_unblock(g_misc))
        return out


def reduce_small(grads, chip):
    names = [n for n, _ in REPLICATED] + [n for n, _, _ in SHARDED_SMALL]
    buf = _pack([grads[n] for n in names], F32, 8)
    chip_sum = add_pair(buf, sibling_swap(buf, "reduce_small_d2d"), F32, "reduce_small_pair")
    total = sum_slots(chip_exchange(chip_sum, True, "reduce_small_ici"), "reduce_small_sum")
    out = dict(zip(names, _unpack(total, [grads[n].shape for n in names])))
    for name, shape, axis in SHARDED_SMALL:
        out[name] = lax.dynamic_slice_in_dim(out[name], chip * shape[axis], shape[axis], axis)
    return out


def _prepare_small(w):
    row = lambda a: a[:, None, :]
    conf_vec = jnp.concatenate([row(w["conf_dw_b"]), row(w["conf_ln_g"]), row(w["conf_ln_b"]), jnp.zeros((DEPTH, 5, 256), F32)], axis=1)
    return dict(
        gpre=row(w["pre_norm_g"]), bias=row(w["gate_bias"]), pwbd=_block_diag(w["pool_w"]).astype(BF16), pscale=row(w["pool_scale"]),
        gq=row(w["q_norm_g"]), gkv=row(w["kv_norm_g"]), conf_w=jnp.pad(w["conf_dw_w"].astype(F32), ((0, 0), (0, 32 - CONF_K), (0, 0))),
        conf_vec=conf_vec, sc_w=jnp.pad(w["sc_dw_w"].astype(F32), ((0, 0), (0, 8 - SC_K), (0, 0))), gpost=row(w["post_norm_g"]))


def _prepare_layer(w_in_blocks, misc):
    w_br, w_gl = _input_weights(w_in_blocks)
    one = lambda a: a.astype(BF16)[None]
    return dict(w_br=one(w_br), w_gl=one(w_gl), misc=misc)


def local_step(x, target, w, large):
    seq = x.shape[0]
    length = N_META + seq
    rows = -(-length // ROW_TILE) * ROW_TILE
    bt = _big_tile(rows)
    hres = _pad_rows(jnp.concatenate([w["meta_tokens"].astype(F32), x], axis=0), rows)
    tgt = jnp.pad(target, ((N_META, rows - length), (0, 0)))
    rope = _rope_tables(rows)
    sw = _prepare_small(w)

    saved = []
    for i in range(DEPTH):
        lw = _prepare_layer(*large.layer(i))
        z_br, hb = prenorm_project(hres, sw["gpre"], lw["w_br"], i)
        z_gl = matmul(hb, lw["w_gl"], "nn", BF16, bt, 1024, D_MODEL, "project_gates", b_layer=0)
        ua, uc, ud, q, k, v = branches_fwd(z_br, rope, sw["pwbd"], sw["pscale"], sw["gq"], sw["gkv"], lw["misc"], sw["conf_w"],
                                           sw["conf_vec"], sw["sc_w"], i)
        o_att, lse, dsts = attention_fwd(q, k, v, large.gather_with_attention(i))
        large.gathered(dsts)
        ub, mb, o, hnew = merge_fwd(ua, o_att, uc, ud, z_br, z_gl, sw["bias"], lw["misc"], sw["gpost"], hres, i)
        saved.append(dict(lw=lw, hres=hres, hb=hb, z_br=z_br, z_gl=z_gl, ua=ua, ub=ub, uc=uc, ud=ud, q=q, k=k, v=v, o_att=o_att,
                          lse=lse, mb=mb, o=o))
        hres = hnew

    dh, total = loss_head(hres, tgt, seq)

    g = {n: [None] * DEPTH for n in ("gpre", "bias", "pwbd", "pscale", "gq", "gkv", "conf_w", "conf_vec", "sc_w", "gpost")}
    for i in reversed(range(DEPTH)):
        s = saved[i]
        lw = s["lw"]
        dm, dwo, g["gpost"][i], rs = postnorm_bwd(dh, s["o"], s["mb"], lw["misc"], sw["gpost"], i, large.swap_with_postnorm())
        large.swapped(rs)
        dua, dub, duc, dud, dz_gl, dwout, g["bias"][i] = merge_bwd(dm, s["ua"], s["ub"], s["uc"], s["ud"], s["z_gl"], sw["bias"],
                                                                 lw["misc"], i)
        dz_br = lax.empty((rows, ZB), BF16)
        dz_br, g["pwbd"][i], g["pscale"][i] = pool_bwd(s["z_br"], dua, sw["pwbd"], sw["pscale"], dz_br, i)
        dz_br, g["sc_w"][i] = shortconv_bwd(s["z_br"], dud, sw["sc_w"], dz_br, i)
        dc, dz_br, g["conf_vec"][i] = conformer_bwd_tail(s["z_br"], duc, sw["conf_w"], sw["conf_vec"], dz_br, i)
        dz_br, g["conf_w"][i] = conformer_bwd_conv(s["z_br"], dc, sw["conf_w"], dz_br, i)
        do, dz_br, delta = attention_bwd_prep(dub, s["o_att"], s["z_br"], dz_br)
        dq, dk, dv, dsts = attention_bwd(s["q"], s["k"], s["v"], do, s["lse"], delta, large.exchange_with_attention())
        large.exchanged(dsts)
        dz_br, dwup, g["gq"][i], g["gkv"][i] = mla_prep_bwd(dq, dk, dv, s["z_br"], rope, sw["gq"], sw["gkv"], lw["misc"], dz_br, i)
        dw_br = matmul(s["hb"], dz_br, "tn", BF16, D_MODEL, ZB // 2, bt, "grad_w_branch")
        dw_gl = matmul(s["hb"], dz_gl, "tn", BF16, D_MODEL, 1024, bt, "grad_w_gates")
        dh_gl = matmul(dz_gl, lw["w_gl"], "nt", F32, bt, D_MODEL, 1024, "grad_h_gates", b_layer=0)
        dh, g["gpre"][i] = prenorm_bwd(dz_br, lw["w_br"], dh_gl, s["hres"], sw["gpre"], dh, i)
        gap = jnp.zeros((N_CHIPS, M_UQ - M_SC - 256, 256), F32)
        large.put_grads(i, _input_weights_inverse(dw_br, dw_gl), jnp.concatenate([dwout, gap, dwup, dwo], axis=1))

    g = {n: jnp.stack(parts) for n, parts in g.items()}
    grads = dict(
        meta_tokens=dh[:N_META], pre_norm_g=g["gpre"][:, 0], gate_bias=g["bias"][:, 0], pool_w=_block_diag_inverse(g["pwbd"]),
        pool_scale=g["pscale"][:, 0], q_norm_g=g["gq"][:, 0], kv_norm_g=g["gkv"][:, 0], conf_dw_w=g["conf_w"][:, :CONF_K],
        conf_dw_b=g["conf_vec"][:, 2], conf_ln_g=g["conf_vec"][:, 0], conf_ln_b=g["conf_vec"][:, 1], sc_dw_w=g["sc_w"][:, :SC_K],
        post_norm_g=g["gpost"][:, 0])
    return total[0, 0], dh[N_META:length], grads


def kernel(x, meta_tokens, pre_norm_g, w_in, gate_bias, pool_w, pool_scale, w_out_pool, q_norm_g, w_uq, kv_norm_g, w_ukv, w_out_mla, conf_dw_w, conf_dw_b, conf_ln_g, conf_ln_b, w_out_conf, sc_dw_w, w_out_sc, w_o, post_norm_g, loss_target, m_meta_tokens, m_pre_norm_g, m_w_in, m_gate_bias, m_pool_w, m_pool_scale, m_w_out_pool, m_q_norm_g, m_w_uq, m_kv_norm_g, m_w_ukv, m_w_out_mla, m_conf_dw_w, m_conf_dw_b, m_conf_ln_g, m_conf_ln_b, m_w_out_conf, m_sc_dw_w, m_w_out_sc, m_w_o, m_post_norm_g, v_meta_tokens, v_pre_norm_g, v_w_in, v_gate_bias, v_pool_w, v_pool_scale, v_w_out_pool, v_q_norm_g, v_w_uq, v_kv_norm_g, v_w_ukv, v_w_out_mla, v_conf_dw_w, v_conf_dw_b, v_conf_ln_g, v_conf_ln_b, v_w_out_conf, v_sc_dw_w, v_w_out_sc, v_w_o, v_post_norm_g):
    args = locals()
    weights = {n: args[n] for n in WEIGHT_ORDER}
    c = lax.axis_index("c")
    chip = 2 * lax.axis_index("x") + lax.axis_index("y")

    small = {n: weights[n] for n, _ in REPLICATED}
    small.update(gather_small(weights))
    large = MeshWeights(weights, c, chip)
    total, dx, grads = local_step(x[0], loss_target[0], small, large)
    loss = lax.psum(total * (0.5 / D_MODEL), ("x", "y", "c"))

    reduced = large.reduced()
    reduced.update(reduce_small(grads, chip))

    flip = lambda a: jnp.swapaxes(a, 1, 2)
    deltas, new_m, new_v = [], [], []
    for n in WEIGHT_ORDER:
        operands = (weights[n], reduced[n], args["m_" + n], args["v_" + n])
        if n == "w_in":
            operands = (flip(operands[0]), lax.optimization_barrier(flip(operands[1])), flip(operands[2]), flip(operands[3]))
            reduced[n] = flip(operands[1])
        d, nm, nv = adamw(*operands)
        if n == "w_in":
            d, nm, nv = flip(d), flip(nm), flip(nv)
        deltas.append(d)
        new_m.append(nm)
        new_v.append(nv)
    return (loss, dx[None], *[reduced[n] for n in WEIGHT_ORDER], *deltas, *new_m, *new_v)
```

```python
import functools
import math

import jax
import jax.numpy as jnp
from jax import lax
from jax.experimental import pallas as pl
from jax.experimental.pallas import tpu as pltpu

F32 = jnp.float32
BF16 = jnp.bfloat16

D_MODEL = 1024
DEPTH = 4
N_META = 16
EPS = 1e-6
HEADS = 8
QK_NOPE = 64
QK_ROPE = 32
V_DIM = 64
HEAD_PAD = 128
ROPE_THETA = 10000.0
Q_SCALE = (QK_NOPE + QK_ROPE) ** -0.5
CONF_K = 31
SC_K = 3
IN_W = 7328
N_CHIPS = 4

ZB = 3328
ZG = 4096
BG, C2, XV, SG, PV, PG, CQ, CKV, KR, MG, CA, CGT, CG = (0, 256, 512, 768, 1024, 1280, 1536, 1792, 1920, 2048, 2560, 2816, 3072)

KEY_GROUP = 4
ROW_TILE = 384
HALO = 32
LANES = 128
VMEM_LIMIT = 56 * 1024 * 1024

ADAM_LR = 0.001
ADAM_B1 = 0.9
ADAM_B2 = 0.999
ADAM_EPS = 1e-08
ADAM_WD = 0.01
ADAM_STEP = 10

MESH = pl.DeviceIdType.MESH
ANY = pl.BlockSpec(memory_space=pl.ANY)

MISC = ("w_out_mla", "w_out_pool", "w_out_conf", "w_out_sc", "w_uq", "w_ukv", "w_o")
M_MLA, M_POOL, M_CONF, M_SC, M_UQ, M_UKVK, M_UKVV, M_WO, MISC_ROWS = 0, 512, 768, 1024, 1536, 1792, 1920, 2048, 3072
SHARDED_SMALL = (
    ("meta_tokens", (N_META, 256), 1),
    ("conf_dw_w", (DEPTH, CONF_K, 64), 2),
    ("sc_dw_w", (DEPTH, SC_K, 64), 2),
)
REPLICATED = (
    ("pre_norm_g", (DEPTH, D_MODEL)),
    ("gate_bias", (DEPTH, 4 * D_MODEL)),
    ("pool_w", (DEPTH, 4, 64, 64)),
    ("pool_scale", (DEPTH, 256)),
    ("q_norm_g", (DEPTH, 256)),
    ("kv_norm_g", (DEPTH, 128)),
    ("conf_dw_b", (DEPTH, 256)),
    ("conf_ln_g", (DEPTH, 256)),
    ("conf_ln_b", (DEPTH, 256)),
    ("post_norm_g", (DEPTH, D_MODEL)),
)
WEIGHT_ORDER = ("meta_tokens", "pre_norm_g", "w_in", "gate_bias", "pool_w", "pool_scale", "w_out_pool", "q_norm_g", "w_uq",
                "kv_norm_g", "w_ukv", "w_out_mla", "conf_dw_w", "conf_dw_b", "conf_ln_g", "conf_ln_b", "w_out_conf", "sc_dw_w",
                "w_out_sc", "w_o", "post_norm_g")


def _dot(a, b):
    return lax.dot_general(a, b, (((1,), (0,)), ((), ())), preferred_element_type=F32)


def _dot_nt(a, b):
    return lax.dot_general(a, b, (((1,), (1,)), ((), ())), preferred_element_type=F32)


def _dot_tn(a, b):
    return lax.dot_general(a, b, (((0,), (0,)), ((), ())), preferred_element_type=F32)


def _sigmoid(x):
    return jax.nn.sigmoid(x)


def _silu(x):
    return x * _sigmoid(x)


def _silu_grad(x):
    s = _sigmoid(x)
    return s * (1.0 + x * (1.0 - s))


def _rms(x, g):
    return x * lax.rsqrt(jnp.mean(x * x, axis=-1, keepdims=True) + EPS) * g


def _sh(x, d):
    return x if d == 0 else pltpu.roll(x, d, 0)


def _ash(x, d):
    return x if d == 0 else pltpu.roll(x, x.shape[0] - d, 0)


def _lanes8(t):
    return jnp.concatenate([t] * HEADS, axis=1)


def _pool_window_sums(v, shift):
    a2 = v + shift(v, 1)
    a4 = a2 + shift(a2, 2)
    a8 = a4 + shift(a4, 4)
    a16 = a8 + shift(a8, 8)
    lane = lax.broadcasted_iota(jnp.int32, v.shape, 1)
    return jnp.where(lane < 64, a2, jnp.where(lane < 128, a4, jnp.where(lane < 192, a8, a16)))


def _pool_counts(first_row, rows):
    pos = first_row + lax.broadcasted_iota(jnp.int32, (rows, 256), 0)
    lane = lax.broadcasted_iota(jnp.int32, (rows, 256), 1)
    width = jnp.where(lane < 64, 2, jnp.where(lane < 128, 4, jnp.where(lane < 192, 8, 16)))
    return jnp.maximum(jnp.minimum(pos + 1, width), 1).astype(F32)


def _params(sem=None):
    return pltpu.CompilerParams(dimension_semantics=sem, vmem_limit_bytes=VMEM_LIMIT)


def _tile_specs(t, n_halo_blocks, li=0):
    per = t // HALO

    def layer(shape, idx=li):
        return pl.BlockSpec((None,) + tuple(shape), lambda i: (idx,) + (0,) * len(shape))

    def cur(c, cb=0):
        return pl.BlockSpec((t, c), lambda i: (i, cb))

    def prev(c, cb=0):
        return pl.BlockSpec((HALO, c), lambda i: (jnp.maximum(i * per - 1, 0), cb))

    def nxt(c, cb=0):
        return pl.BlockSpec((HALO, c), lambda i: (jnp.minimum((i + 1) * per, n_halo_blocks - 1), cb))

    def full(shape):
        return pl.BlockSpec(shape, lambda i: (0,) * len(shape))

    return cur, prev, nxt, full, layer


def _big_tile(rows):
    return rows // 3 if rows % (3 * LANES) == 0 else ROW_TILE


def matmul(a, b, mode, out_dtype, tm, tn, tk, name, b_layer=None):
    bs = b.shape if b_layer is None else b.shape[1:]
    lead = () if b_layer is None else (None,)
    pick = (lambda *ix: ix) if b_layer is None else (lambda *ix: (b_layer,) + ix)
    if mode == "nn":
        (m, k), n = a.shape, bs[1]
        a_spec = pl.BlockSpec((tm, tk), lambda i, j, kk: (i, kk))
        b_spec = pl.BlockSpec(lead + (tk, tn), lambda i, j, kk: pick(kk, j))
        dot = _dot
    elif mode == "nt":
        (m, k), n = a.shape, bs[0]
        a_spec = pl.BlockSpec((tm, tk), lambda i, j, kk: (i, kk))
        b_spec = pl.BlockSpec(lead + (tn, tk), lambda i, j, kk: pick(j, kk))
        dot = _dot_nt
    else:
        (k, m), n = a.shape, bs[1]
        a_spec = pl.BlockSpec((tk, tm), lambda i, j, kk: (kk, i))
        b_spec = pl.BlockSpec(lead + (tk, tn), lambda i, j, kk: pick(kk, j))
        dot = _dot_tn
    assert m % tm == 0 and n % tn == 0 and k % tk == 0, (a.shape, bs, tm, tn, tk)
    nk = k // tk

    def body(a_ref, b_ref, o_ref, acc_ref):
        kk = pl.program_id(2)

        @pl.when(kk == 0)
        def _():
            acc_ref[...] = jnp.zeros_like(acc_ref)

        acc_ref[...] += dot(a_ref[...], b_ref[...])

        @pl.when(kk == nk - 1)
        def _():
            o_ref[...] = acc_ref[...].astype(out_dtype)

    return pl.pallas_call(
        body, name=name, grid=(m // tm, n // tn, nk), in_specs=[a_spec, b_spec],
        out_specs=pl.BlockSpec((tm, tn), lambda i, j, kk: (i, j)), out_shape=jax.ShapeDtypeStruct((m, n), out_dtype),
        scratch_shapes=[pltpu.VMEM((tm, tn), F32)], compiler_params=_params(("parallel", "parallel", "arbitrary")),
    )(a, b)


def prenorm_project(hres, g, w, li):
    rows, d = hres.shape
    n = w.shape[2]
    tm, tn = _big_tile(rows), n // 2

    def body(x_ref, g_ref, w_ref, z_ref, hb_ref):
        @pl.when(pl.program_id(1) == 0)
        def _():
            hb_ref[...] = _rms(x_ref[...], g_ref[...]).astype(BF16)

        z_ref[...] = _dot(hb_ref[...], w_ref[...]).astype(BF16)

    return pl.pallas_call(
        body, name="prenorm_project", grid=(rows // tm, n // tn),
        in_specs=[pl.BlockSpec((tm, d), lambda i, j: (i, 0)), pl.BlockSpec((None, 1, d), lambda i, j: (li, 0, 0)),
                  pl.BlockSpec((None, d, tn), lambda i, j: (0, 0, j))],
        out_specs=[pl.BlockSpec((tm, tn), lambda i, j: (i, j)), pl.BlockSpec((tm, d), lambda i, j: (i, 0))],
        out_shape=[jax.ShapeDtypeStruct((rows, n), BF16), jax.ShapeDtypeStruct((rows, d), BF16)],
        compiler_params=_params(("parallel", "arbitrary")),
    )(hres, g, w)


def _rope(q, c, s1, s2, width):
    return q * c + pltpu.roll(q, width - 16, 1) * s1 + pltpu.roll(q, 16, 1) * s2


def _rope_transposed(dq, c, s1, s2, width):
    return dq * c + pltpu.roll(dq * s1, 16, 1) + pltpu.roll(dq * s2, width - 16, 1)


def _conf_conv(g1, w_ref):
    acc = jnp.zeros_like(g1)
    for k in range(CONF_K):
        acc = acc + w_ref[k:k + 1, :] * _sh(g1, CONF_K - 1 - k)
    return acc


def _conf_tail(c, cg, lg, lb):
    mu = jnp.mean(c, axis=-1, keepdims=True)
    xc = c - mu
    var = jnp.mean(xc * xc, axis=-1, keepdims=True)
    n = xc * lax.rsqrt(var + EPS) * lg + lb
    return _silu(n) * _silu(cg)


def _misc_spec(misc, row0, rows):
    assert row0 % rows == 0
    return pl.BlockSpec((None, N_CHIPS, rows, 256), lambda i: (misc[1], 0, row0 // rows, 0))


def _chip_columns(x, w_ref, row0, rows, lanes=256):
    return jnp.concatenate([_dot(x, w_ref[k, row0:row0 + rows, 0:lanes]) for k in range(N_CHIPS)], axis=1)


def branches_fwd(z_br, rope, pwbd, pscale, gq, gkv, misc, conf_w, conf_vec, sc_w, li):
    rows = z_br.shape[0]
    t = ROW_TILE
    cur, prev, _, _, layer = _tile_specs(t, rows // HALO, li)

    def body(zc_ref, zp_ref, rope_ref, pw_ref, ps_ref, gq_ref, gkv_ref, up_ref, cw_ref, cv_ref, sw_ref,
             ua_ref, uc_ref, ud_ref, q_ref, k_ref, v_ref):
        i = pl.program_id(0)
        zp = jnp.where(i == 0, jnp.zeros(zp_ref.shape, zp_ref.dtype), zp_ref[...])

        def ext(lo, w=256):
            return jnp.concatenate([zp[:, lo:lo + w], zc_ref[:, lo:lo + w]], axis=0).astype(F32)

        def col(lo, w=256):
            return zc_ref[:, lo:lo + w].astype(F32)

        v = ext(PV)
        p = (_pool_window_sums(v, _sh) / _pool_counts(i * t - HALO, t + HALO) - v)[HALO:]
        ya = _dot(p.astype(BF16), pw_ref[...]) * ps_ref[...]
        ua_ref[...] = (ya * _silu(col(PG))).astype(BF16)

        g1 = ext(CA) * _sigmoid(ext(CGT))
        c = _conf_conv(g1, cw_ref)[HALO:] + cv_ref[0:1, :]
        uc_ref[...] = _conf_tail(c, col(CG), cv_ref[1:2, :], cv_ref[2:3, :]).astype(BF16)

        e = ext(C2) * ext(XV)
        f = jnp.zeros_like(e)
        for k in range(SC_K):
            f = f + sw_ref[k:k + 1, :] * _sh(e, SC_K - 1 - k)
        ud_ref[...] = (col(BG) * f[HALO:] * _silu(col(SG))).astype(BF16)

        cth, s1, s2 = rope_ref[:, 0:128], rope_ref[:, 128:256], rope_ref[:, 256:384]
        qn = _rms(col(CQ), gq_ref[...]).astype(BF16)
        q = _chip_columns(qn, up_ref, 0, 256)
        w8 = HEADS * HEAD_PAD
        q_ref[...] = (_rope(q, _lanes8(cth), _lanes8(s1), _lanes8(s2), w8) * Q_SCALE).astype(BF16)
        kvn = _rms(col(CKV, 128), gkv_ref[...]).astype(BF16)
        kr = _rope(col(KR, 128), cth, s1, s2, HEAD_PAD)
        k_ref[...] = (_chip_columns(kvn, up_ref, M_UKVK - M_UQ, 128) + _lanes8(kr)).astype(BF16)
        v_ref[...] = _chip_columns(kvn, up_ref, M_UKVV - M_UQ, 128, 2 * V_DIM).astype(BF16)

    outs = [jax.ShapeDtypeStruct((rows, 256), BF16)] * 3 + [jax.ShapeDtypeStruct((rows, 1024), BF16)] * 2 + [
        jax.ShapeDtypeStruct((rows, 512), BF16)]
    return pl.pallas_call(
        body, name="branches_fwd", grid=(rows // t,),
        in_specs=[cur(ZB), prev(ZB), cur(384), layer((256, 256)), layer((1, 256)), layer((1, 256)), layer((1, 128)),
                  _misc_spec(misc, M_UQ, M_WO - M_UQ), layer((32, 256)), layer((8, 256)), layer((8, 256))],
        out_specs=[cur(256), cur(256), cur(256), cur(1024), cur(1024), cur(512)], out_shape=outs,
        compiler_params=_params(("parallel",)),
    )(z_br, z_br, rope, pwbd, pscale, gq, gkv, misc[0], conf_w, conf_vec, sc_w)


def _head_lane_mask(h):
    lane = lax.broadcasted_iota(jnp.int32, (1, 2 * V_DIM), 1)
    return (lane >= V_DIM * h) & (lane < V_DIM * (h + 1))


def attention_fwd(q, k, v, gather=None):
    rows = q.shape[0]
    tq = ROW_TILE
    nq = rows // tq
    n = 0 if gather is None else len(gather[0])

    def body(*refs):
        if n:
            start, finish = _gather_ops(refs[3:3 + n], refs[5 + 2 * n:5 + 3 * n], refs[5 + 3 * n:], gather[2], True)
            pl.when((pl.program_id(0) == 0) & (pl.program_id(1) == 0))(start)
        compute(*refs[:3], *refs[3 + 2 * n:5 + 2 * n])
        if n:
            pl.when((pl.program_id(0) == HEADS // 2 - 1) & (pl.program_id(1) == nq - 1))(finish)

    def compute(q_ref, k_ref, v_ref, o_ref, lse_ref):
        i = pl.program_id(1)

        def head_step(h, tile, n_tiles, carry, masked):
            m, l, acc = carry
            width = n_tiles * tq
            r0 = pl.multiple_of(tile * tq, tq)
            kh = k_ref[pl.ds(r0, width), HEAD_PAD * h:HEAD_PAD * (h + 1)]
            vh = jnp.where(_head_lane_mask(h), v_ref[pl.ds(r0, width), :], jnp.zeros((), BF16))
            s = _dot_nt(q_ref[:, HEAD_PAD * h:HEAD_PAD * (h + 1)], kh)
            if masked:
                row = lax.broadcasted_iota(jnp.int32, (tq, width), 0)
                colm = lax.broadcasted_iota(jnp.int32, (tq, width), 1)
                s = jnp.where(colm <= row + (width - tq), s, -1e30)
            m2 = jnp.maximum(m, jnp.max(s, axis=-1, keepdims=True))
            alpha = jnp.exp(m - m2)
            pr = jnp.exp(s - m2)
            return m2, alpha * l + jnp.sum(pr, axis=-1, keepdims=True), alpha * acc + _dot(pr.astype(BF16), vh)

        def step(tile, n_tiles, carry, masked):
            return tuple(head_step(h, tile, n_tiles, carry[h], masked) for h in range(2))

        init = (jnp.full((tq, 1), -1e30, F32), jnp.zeros((tq, 1), F32), jnp.zeros((tq, 2 * V_DIM), F32))
        group = min(KEY_GROUP, nq)
        carry = lax.fori_loop(0, i // group, lambda t, cr: step(group * t, group, cr, False), (init, init))
        carry = lax.switch(i % group, [functools.partial(lambda cr, r: step(i - r, r + 1, cr, True), r=r) for r in range(group)], carry)
        out = jnp.zeros((tq, 2 * V_DIM), F32)
        for h, (m, l, acc) in enumerate(carry):
            out = out + acc / l
            lse_ref[h] = jnp.broadcast_to(m + jnp.log(l), (tq, LANES))
        o_ref[...] = out.astype(BF16)

    srcs, dsts = ([], []) if gather is None else (list(gather[0]), list(gather[1]))
    outs = pl.pallas_call(
        body, name="attention_fwd" if gather is None else "attention_fwd_gather", grid=(HEADS // 2, nq),
        in_specs=[pl.BlockSpec((tq, 2 * HEAD_PAD), lambda p, i: (i, p)), pl.BlockSpec((rows, 2 * HEAD_PAD), lambda p, i: (0, p)),
                  pl.BlockSpec((rows, 2 * V_DIM), lambda p, i: (0, p))] + [ANY] * (2 * n),
        out_specs=[pl.BlockSpec((tq, 2 * V_DIM), lambda p, i: (i, p)), pl.BlockSpec((2, tq, LANES), lambda p, i: (p, i, 0))] + [ANY] * n,
        out_shape=[jax.ShapeDtypeStruct((rows, HEADS * V_DIM), BF16), jax.ShapeDtypeStruct((HEADS, rows, LANES), F32)] + [
            jax.ShapeDtypeStruct(d.shape, d.dtype) for d in dsts],
        input_output_aliases={3 + n + a: 2 + a for a in range(n)}, scratch_shapes=GATHER_SEMS(n) if n else [],
        compiler_params=_params(("arbitrary", "arbitrary") if n else ("parallel", "parallel")),
    )(q, k, v, *srcs, *dsts)
    return outs[0], outs[1], list(outs[2:])


OUT_PROJECTIONS = ((M_POOL, 256), (M_MLA, 512), (M_CONF, 256), (M_SC, 256))


def _chunks(x, n=N_CHIPS, width=256):
    return [x[:, width * k:width * (k + 1)] for k in range(n)]


def merge_fwd(ua, o_att, uc, ud, z_br, z_gl, bias, misc, gpost, hres, li):
    rows = hres.shape[0]
    t = ROW_TILE
    cur, _, _, _, layer = _tile_specs(t, rows // HALO, li)
    d = D_MODEL

    def body(ua_ref, ob_ref, uc_ref, ud_ref, mg_ref, gl_ref, b_ref, wout_ref, wo_ref, gp_ref, h_ref, ub_ref, mb_ref, o_ref, hn_ref):
        ub = (ob_ref[...].astype(F32) * _silu(mg_ref[...].astype(F32))).astype(BF16)
        ub_ref[...] = ub
        m = jnp.zeros((t, d), F32)
        for idx, (u, (row0, n)) in enumerate(zip((ua_ref[...], ub, uc_ref[...], ud_ref[...]), OUT_PROJECTIONS)):
            gate = _sigmoid(gl_ref[:, d * idx:d * (idx + 1)].astype(F32) + b_ref[:, d * idx:d * (idx + 1)])
            m = m + gate * _chip_columns(u, wout_ref, row0, n)
        mb = m.astype(BF16)
        mb_ref[...] = mb
        o = jnp.concatenate([sum(_dot(mk, wo_ref[k, 256 * j:256 * (j + 1), :]) for k, mk in enumerate(_chunks(mb)))
                             for j in range(N_CHIPS)], axis=1)
        o_ref[...] = o
        hn_ref[...] = h_ref[...] + _rms(o, gp_ref[...])

    return pl.pallas_call(
        body, name="merge_fwd", grid=(rows // t,),
        in_specs=[cur(256), cur(512), cur(256), cur(256), cur(512, MG // 512), cur(ZG), layer((1, ZG)), _misc_spec(misc, 0, 1280),
                  _misc_spec(misc, M_WO, D_MODEL), layer((1, d)), cur(d)],
        out_specs=[cur(512), cur(d), cur(d), cur(d)],
        out_shape=[jax.ShapeDtypeStruct((rows, 512), BF16), jax.ShapeDtypeStruct((rows, d), BF16), jax.ShapeDtypeStruct((rows, d), F32),
                   jax.ShapeDtypeStruct((rows, d), F32)],
        compiler_params=_params(("parallel",)),
    )(ua, o_att, uc, ud, z_br, z_gl, bias, misc[0], misc[0], gpost, hres)


def loss_head(hres, target, n_tokens):
    rows, d = hres.shape
    t = ROW_TILE
    cur, _, _, full, _ = _tile_specs(t, rows // HALO)
    n_steps = rows // t

    def body(h_ref, t_ref, dh_ref, tot_ref, acc_ref):
        i = pl.program_id(0)

        @pl.when(i == 0)
        def _():
            acc_ref[...] = jnp.zeros_like(acc_ref)

        r = i * t + lax.broadcasted_iota(jnp.int32, (t, 1), 0)
        diff = jnp.where((r >= N_META) & (r < N_META + n_tokens), h_ref[...] - t_ref[...], 0.0)
        dh_ref[...] = diff * (1.0 / d)
        acc_ref[...] += jnp.sum(diff * diff, axis=0, keepdims=True)

        @pl.when(i == n_steps - 1)
        def _():
            tot_ref[...] = jnp.broadcast_to(jnp.sum(acc_ref[...], axis=1, keepdims=True), (1, LANES))

    return pl.pallas_call(
        body, name="loss_head", grid=(n_steps,), in_specs=[cur(d), cur(d)], out_specs=[cur(d), full((1, LANES))],
        out_shape=[jax.ShapeDtypeStruct((rows, d), F32), jax.ShapeDtypeStruct((1, LANES), F32)],
        scratch_shapes=[pltpu.VMEM((1, d), F32)], compiler_params=_params(("arbitrary",)),
    )(hres, target)


def _accumulate(i, ref, value):
    @pl.when(i == 0)
    def _():
        ref[...] = value

    @pl.when(i > 0)
    def _():
        ref[...] += value


def postnorm_bwd(dh, o, mb, misc, gpost, li, swap=None):
    rows, d = dh.shape
    t = ROW_TILE
    cur, _, _, full, layer = _tile_specs(t, rows // HALO, li)
    n = 0 if swap is None else len(swap)
    steps = rows // t

    def body(*refs):
        if n:
            start, finish = _swap_ops(refs[5:5 + n], refs[8 + n:8 + 2 * n], refs[8 + 2 * n:])
            pl.when(pl.program_id(0) == 0)(start)
        compute(*refs[:5], *refs[5 + n:8 + n])
        if n:
            pl.when(pl.program_id(0) == steps - 1)(finish)

    def compute(dh_ref, o_ref, mb_ref, wo_ref, gp_ref, dm_ref, dwo_ref, dgp_ref):
        i = pl.program_id(0)
        _, vjp = jax.vjp(_rms, o_ref[...], gp_ref[...])
        do, dg = vjp(dh_ref[...])
        dob = do.astype(BF16)
        dm_ref[...] = jnp.concatenate([sum(_dot_nt(dj, wo_ref[k, 256 * j:256 * (j + 1), :]) for j, dj in enumerate(_chunks(dob)))
                                       for k in range(N_CHIPS)], axis=1)
        dwo = _dot_tn(mb_ref[...], dob)
        for k in range(N_CHIPS):
            _accumulate(i, dwo_ref.at[k], jnp.concatenate(_chunks(dwo[256 * k:256 * (k + 1), :]), axis=0))
        _accumulate(i, dgp_ref, dg)

    sent = [] if swap is None else list(swap)
    outs = pl.pallas_call(
        body, name="postnorm_bwd" if swap is None else "postnorm_bwd_swap", grid=(steps,),
        in_specs=[cur(d), cur(d), cur(d), _misc_spec(misc, M_WO, d), layer((1, d))] + [ANY] * n,
        out_specs=[cur(d), full((N_CHIPS, d, 256)), full((1, d))] + [ANY] * n,
        out_shape=[jax.ShapeDtypeStruct((rows, d), F32), jax.ShapeDtypeStruct((N_CHIPS, d, 256), F32), jax.ShapeDtypeStruct((1, d), F32)] + [
            jax.ShapeDtypeStruct((p.shape[0], p.shape[1] // 2, p.shape[2]), p.dtype) for p in sent],
        scratch_shapes=[pltpu.SemaphoreType.DMA((n,)), pltpu.SemaphoreType.DMA((n,))] if n else [],
        compiler_params=_params(("arbitrary",)),
    )(dh, o, mb, misc[0], gpost, *sent)
    return outs[0], outs[1], outs[2], list(outs[3:])


def merge_bwd(dm, ua, ub, uc, ud, z_gl, bias, misc, o_att, z_br, dz_buf, li):
    rows, d = dm.shape
    t = ROW_TILE
    cur, _, _, full, layer = _tile_specs(t, rows // HALO, li)

    def body(dm_ref, ua_ref, ub_ref, uc_ref, ud_ref, gl_ref, b_ref, w_ref, o_ref, mg_ref, _,
             dua_ref, do_ref, duc_ref, dud_ref, dgl_ref, dw_ref, db_ref, dmg_ref, delta_ref):
        i = pl.program_id(0)
        dm = dm_ref[...]
        groups = ((ua_ref, dua_ref), (ub_ref, None), (uc_ref, duc_ref), (ud_ref, dud_ref))
        for idx, ((u_ref, du_ref), (row0, n)) in enumerate(zip(groups, OUT_PROJECTIONS)):
            cols = slice(d * idx, d * (idx + 1))
            u = u_ref[...]
            gate = _sigmoid(gl_ref[:, cols].astype(F32) + b_ref[:, cols])
            dgl = dm * _chip_columns(u, w_ref, row0, n) * gate * (1.0 - gate)
            dgl_ref[:, cols] = dgl.astype(BF16)
            _accumulate(i, db_ref.at[:, cols], jnp.sum(dgl, axis=0, keepdims=True))
            dyb = (dm * gate).astype(BF16)
            du = sum(_dot_nt(dyk, w_ref[k, row0:row0 + n, :]) for k, dyk in enumerate(_chunks(dyb)))
            for k, dwk in enumerate(_chunks(_dot_tn(u, dyb))):
                _accumulate(i, dw_ref.at[k, row0:row0 + n, :], dwk)
            if du_ref is not None:
                du_ref[...] = du
                continue
            o, mg = o_ref[...].astype(F32), mg_ref[...].astype(F32)
            do = du * _silu(mg)
            do_ref[...] = do.astype(BF16)
            dmg_ref[...] = (du * o * _silu_grad(mg)).astype(BF16)
            prod = do * o
            lane = lax.broadcasted_iota(jnp.int32, (1, HEADS * V_DIM), 1)
            for h in range(HEADS):
                part = jnp.where((lane >= V_DIM * h) & (lane < V_DIM * (h + 1)), prod, 0.0)
                delta_ref[h] = jnp.broadcast_to(jnp.sum(part, axis=-1, keepdims=True), (t, LANES))

    return pl.pallas_call(
        body, name="merge_bwd", grid=(rows // t,),
        in_specs=[cur(d), cur(256), cur(512), cur(256), cur(256), cur(ZG), layer((1, ZG)), _misc_spec(misc, 0, 1280), cur(512),
                  cur(512, MG // 512), ANY],
        out_specs=[cur(256), cur(512), cur(256), cur(256), cur(ZG), full((N_CHIPS, 1280, 256)), full((1, ZG)), cur(512, MG // 512),
                   pl.BlockSpec((HEADS, t, LANES), lambda i: (0, i, 0))],
        out_shape=[jax.ShapeDtypeStruct((rows, 256), F32), jax.ShapeDtypeStruct((rows, 512), BF16), jax.ShapeDtypeStruct((rows, 256), F32),
                   jax.ShapeDtypeStruct((rows, 256), F32), jax.ShapeDtypeStruct((rows, ZG), BF16),
                   jax.ShapeDtypeStruct((N_CHIPS, 1280, 256), F32), jax.ShapeDtypeStruct((1, ZG), F32),
                   jax.ShapeDtypeStruct((rows, ZB), BF16), jax.ShapeDtypeStruct((HEADS, rows, LANES), F32)],
        input_output_aliases={10: 7}, compiler_params=_params(("arbitrary",)),
    )(dm, ua, ub, uc, ud, z_gl, bias, misc[0], o_att, z_br, dz_buf)


def pool_bwd(z_br, dua, pwbd, pscale, dz_buf, li):
    rows = z_br.shape[0]
    t = ROW_TILE
    n_steps = rows // t
    cur, prev, nxt, full, layer = _tile_specs(t, rows // HALO, li)

    def body(zc_ref, zp_ref, zn_ref, dc_ref, dn_ref, pw_ref, ps_ref, _, dz_ref, dpw_ref, dps_ref):
        i = pl.program_id(0)
        zp = jnp.where(i == 0, jnp.zeros(zp_ref.shape, zp_ref.dtype), zp_ref[...])
        zn = jnp.where(i == n_steps - 1, jnp.zeros(zn_ref.shape, zn_ref.dtype), zn_ref[...])
        dun = jnp.where(i == n_steps - 1, jnp.zeros(dn_ref.shape, dn_ref.dtype), dn_ref[...])

        def ext(lo):
            return jnp.concatenate([zp[:, lo:lo + 256], zc_ref[:, lo:lo + 256], zn[:, lo:lo + 256]], axis=0).astype(F32)

        n_ext = t + 2 * HALO
        v, pg = ext(PV), ext(PG)
        cnt = _pool_counts(i * t - HALO, n_ext)
        p = (_pool_window_sums(v, _sh) / cnt - v)[HALO:HALO + t]
        du = jnp.concatenate([jnp.zeros((HALO, 256), F32), dc_ref[...], dun], axis=0)
        dya = du * _silu(pg)
        dypb = (dya * ps_ref[...]).astype(BF16)
        dp = _dot_nt(dypb, pw_ref[...])
        dv = (_pool_window_sums(dp / cnt, _ash) - dp)[HALO:HALO + t]
        pb = p.astype(BF16)
        pw = _dot(pb, pw_ref[...])
        duc, pgc = dc_ref[...], pg[HALO:HALO + t]
        dpg = duc * pw * ps_ref[...] * _silu_grad(pgc)
        dz_ref[...] = jnp.concatenate([dv, dpg], axis=1).astype(BF16)
        _accumulate(i, dpw_ref, _dot_tn(pb, dypb[HALO:HALO + t]))
        _accumulate(i, dps_ref, jnp.sum(dya[HALO:HALO + t] * pw, axis=0, keepdims=True))

    return pl.pallas_call(
        body, name="pool_bwd", grid=(n_steps,),
        in_specs=[cur(ZB), prev(ZB), nxt(ZB), cur(256), nxt(256), layer((256, 256)), layer((1, 256)), ANY],
        out_specs=[cur(512, PV // 512), full((256, 256)), full((1, 256))],
        out_shape=[jax.ShapeDtypeStruct((rows, ZB), BF16), jax.ShapeDtypeStruct((256, 256), F32), jax.ShapeDtypeStruct((1, 256), F32)],
        input_output_aliases={7: 0}, compiler_params=_params(("arbitrary",)),
    )(z_br, z_br, z_br, dua, dua, pwbd, pscale, dz_buf)


def shortconv_bwd(z_br, dud, sc_w, dz_buf, li):
    rows = z_br.shape[0]
    t = ROW_TILE
    n_steps = rows // t
    cur, prev, nxt, full, layer = _tile_specs(t, rows // HALO, li)

    def body(zc_ref, zp_ref, zn_ref, dc_ref, dn_ref, sw_ref, _, dz_ref, dw_ref):
        i = pl.program_id(0)
        zp = jnp.where(i == 0, jnp.zeros(zp_ref.shape, zp_ref.dtype), zp_ref[...])
        zn = jnp.where(i == n_steps - 1, jnp.zeros(zn_ref.shape, zn_ref.dtype), zn_ref[...])
        dun = jnp.where(i == n_steps - 1, jnp.zeros(dn_ref.shape, dn_ref.dtype), dn_ref[...])

        def ext(lo):
            return jnp.concatenate([zp[:, lo:lo + 256], zc_ref[:, lo:lo + 256], zn[:, lo:lo + 256]], axis=0).astype(F32)

        mid = slice(HALO, HALO + t)
        bg, c2, xv, sg = ext(BG), ext(C2), ext(XV), ext(SG)
        du = jnp.concatenate([jnp.zeros((HALO, 256), F32), dc_ref[...], dun], axis=0)
        e = c2 * xv
        shifted = [_sh(e, SC_K - 1 - k) for k in range(SC_K)]
        f = sum(sw_ref[k:k + 1, :] * shifted[k] for k in range(SC_K))
        gate = _silu(sg)
        df = du * gate * bg
        de = sum(sw_ref[k:k + 1, :] * _ash(df, SC_K - 1 - k) for k in range(SC_K))
        dbg = du * gate * f
        dsg = du * bg * f * _silu_grad(sg)
        dz_ref[...] = jnp.concatenate([dbg[mid], (de * xv)[mid], (de * c2)[mid], dsg[mid]], axis=1).astype(BF16)
        dw = jnp.concatenate([jnp.sum((df * shifted[k])[mid], axis=0, keepdims=True) for k in range(SC_K)] + [
            jnp.zeros((8 - SC_K, 256), F32)], axis=0)
        _accumulate(i, dw_ref, dw)

    return pl.pallas_call(
        body, name="shortconv_bwd", grid=(n_steps,), in_specs=[cur(ZB), prev(ZB), nxt(ZB), cur(256), nxt(256), layer((8, 256)), ANY],
        out_specs=[cur(1024, BG // 1024), full((8, 256))],
        out_shape=[jax.ShapeDtypeStruct((rows, ZB), BF16), jax.ShapeDtypeStruct((8, 256), F32)],
        input_output_aliases={6: 0}, compiler_params=_params(("arbitrary",)),
    )(z_br, z_br, z_br, dud, dud, sc_w, dz_buf)


def conformer_bwd_tail(z_br, duc, conf_w, conf_vec, dz_buf, li):
    rows = z_br.shape[0]
    t = ROW_TILE
    cur, prev, _, full, layer = _tile_specs(t, rows // HALO, li)

    def body(zc_ref, zp_ref, du_ref, cw_ref, cv_ref, _, dc_ref, dcg_ref, dv_ref):
        i = pl.program_id(0)
        zp = jnp.where(i == 0, jnp.zeros(zp_ref.shape, zp_ref.dtype), zp_ref[...])

        def ext(lo):
            return jnp.concatenate([zp[:, lo:lo + 256], zc_ref[:, lo:lo + 256]], axis=0).astype(F32)

        g1 = ext(CA) * _sigmoid(ext(CGT))
        c = _conf_conv(g1, cw_ref)[HALO:] + cv_ref[0:1, :]
        _, vjp = jax.vjp(_conf_tail, c, zc_ref[:, CG:CG + 256].astype(F32), cv_ref[1:2, :], cv_ref[2:3, :])
        dc, dcg, dlg, dlb = vjp(du_ref[...])
        dc_ref[...] = dc
        dcg_ref[...] = dcg.astype(BF16)
        dvec = jnp.concatenate([dlg, dlb, jnp.sum(dc, axis=0, keepdims=True), jnp.zeros((5, 256), F32)], axis=0)
        _accumulate(i, dv_ref, dvec)

    return pl.pallas_call(
        body, name="conformer_bwd_tail", grid=(rows // t,), in_specs=[cur(ZB), prev(ZB), cur(256), layer((32, 256)), layer((8, 256)), ANY],
        out_specs=[cur(256), cur(256, CG // 256), full((8, 256))],
        out_shape=[jax.ShapeDtypeStruct((rows, 256), F32), jax.ShapeDtypeStruct((rows, ZB), BF16), jax.ShapeDtypeStruct((8, 256), F32)],
        input_output_aliases={5: 1}, compiler_params=_params(("arbitrary",)),
    )(z_br, z_br, duc, conf_w, conf_vec, dz_buf)


def conformer_bwd_conv(z_br, dc, conf_w, dz_buf, li):
    rows = z_br.shape[0]
    t = ROW_TILE
    n_steps = rows // t
    cur, prev, nxt, full, layer = _tile_specs(t, rows // HALO, li)

    def body(zc_ref, zp_ref, dc_ref, dn_ref, cw_ref, _, dz_ref, dw_ref):
        i = pl.program_id(0)
        zp = jnp.where(i == 0, jnp.zeros(zp_ref.shape, zp_ref.dtype), zp_ref[...])
        dcn = jnp.where(i == n_steps - 1, jnp.zeros(dn_ref.shape, dn_ref.dtype), dn_ref[...])

        def ext(lo):
            return jnp.concatenate([zp[:, lo:lo + 256], zc_ref[:, lo:lo + 256]], axis=0).astype(F32)

        a, gt = ext(CA), ext(CGT)
        sg = _sigmoid(gt)
        g1 = a * sg
        dc = dc_ref[...]
        dce = jnp.concatenate([dc, dcn], axis=0)
        dg1 = jnp.zeros_like(dce)
        dws = []
        for k in range(CONF_K):
            dg1 = dg1 + cw_ref[k:k + 1, :] * _ash(dce, CONF_K - 1 - k)
            dws.append(jnp.sum(dc * _sh(g1, CONF_K - 1 - k)[HALO:], axis=0, keepdims=True))
        dg1 = dg1[:t]
        ac, sc = a[HALO:], sg[HALO:]
        dz_ref[...] = jnp.concatenate([dg1 * sc, dg1 * ac * sc * (1.0 - sc)], axis=1).astype(BF16)
        _accumulate(i, dw_ref, jnp.concatenate(dws + [jnp.zeros((32 - CONF_K, 256), F32)], axis=0))

    return pl.pallas_call(
        body, name="conformer_bwd_conv", grid=(n_steps,), in_specs=[cur(ZB), prev(ZB), cur(256), nxt(256), layer((32, 256)), ANY],
        out_specs=[cur(512, CA // 512), full((32, 256))],
        out_shape=[jax.ShapeDtypeStruct((rows, ZB), BF16), jax.ShapeDtypeStruct((32, 256), F32)],
        input_output_aliases={5: 0}, compiler_params=_params(("arbitrary",)),
    )(z_br, z_br, dc, dc, conf_w, dz_buf)


def attention_bwd(q, k, v, do, lse, delta, exchange=None):
    rows = q.shape[0]
    tq = ROW_TILE
    nq = rows // tq
    n = 0 if exchange is None else len(exchange[0])

    def body(*refs):
        if n:
            start, finish = _exchange_ops(refs[6:6 + n], refs[9 + 2 * n:9 + 3 * n], refs[9 + 3 * n:], exchange[2])
            pl.when((pl.program_id(0) == 0) & (pl.program_id(1) == 0))(start)
        compute(*refs[:6], *refs[6 + 2 * n:9 + 2 * n])
        if n:
            pl.when((pl.program_id(0) == HEADS // 2 - 1) & (pl.program_id(1) == nq - 1))(finish)

    def compute(q_ref, k_ref, v_ref, do_ref, lse_ref, dl_ref, dq_ref, dk_ref, dv_ref):
        j = pl.program_id(1)

        @pl.when(j == 0)
        def _():
            dq_ref[...] = jnp.zeros_like(dq_ref)

        def head_step(h, tile, n_tiles, dk, dv, diagonal):
            lanes = slice(HEAD_PAD * h, HEAD_PAD * (h + 1))
            hm = _head_lane_mask(h)
            kh = k_ref[:, lanes]
            vh = jnp.where(hm, v_ref[...], jnp.zeros((), BF16))
            r0, width = pl.multiple_of(tile * tq, tq), n_tiles * tq
            qi = q_ref[pl.ds(r0, width), lanes]
            doi = jnp.where(hm, do_ref[pl.ds(r0, width), :], jnp.zeros((), BF16))
            s = _dot_nt(qi, kh)
            if diagonal:
                s = jnp.where(lax.broadcasted_iota(jnp.int32, (tq, tq), 1) <= lax.broadcasted_iota(jnp.int32, (tq, tq), 0), s, -1e30)
            pr = jnp.exp(s - lse_ref[h, pl.ds(r0, width), :][:, 0:1])
            dv = dv + _dot_tn(pr.astype(BF16), doi)
            dp = _dot_nt(doi, vh)
            ds = (pr * (dp - dl_ref[h, pl.ds(r0, width), :][:, 0:1])).astype(BF16)
            dq_ref[pl.ds(r0, width), lanes] += _dot(ds, kh)
            return dk + _dot_tn(ds, qi), dv

        def step(tile, n_tiles, carry, diagonal):
            dk0, dk1, dv = carry
            dk0, dv = head_step(0, tile, n_tiles, dk0, dv, diagonal)
            dk1, dv = head_step(1, tile, n_tiles, dk1, dv, diagonal)
            return dk0, dk1, dv

        zero = jnp.zeros((tq, HEAD_PAD), F32)
        carry = step(j, 1, (zero, zero, jnp.zeros((tq, 2 * V_DIM), F32)), True)
        odd = (nq - 1 - j) % 2
        carry = lax.cond(odd == 1, lambda cr: step(j + 1, 1, cr, False), lambda cr: cr, carry)
        dk0, dk1, dv = lax.fori_loop(0, (nq - 1 - j) // 2, lambda t, cr: step(j + 1 + odd + 2 * t, 2, cr, False), carry)
        dk_ref[:, 0:HEAD_PAD] = dk0
        dk_ref[:, HEAD_PAD:2 * HEAD_PAD] = dk1
        dv_ref[...] = dv

    srcs, dsts = ([], []) if exchange is None else (list(exchange[0]), list(exchange[1]))
    outs = pl.pallas_call(
        body, name="attention_bwd" if exchange is None else "attention_bwd_exchange", grid=(HEADS // 2, nq),
        in_specs=[pl.BlockSpec((rows, 2 * HEAD_PAD), lambda p, j: (0, p)), pl.BlockSpec((tq, 2 * HEAD_PAD), lambda p, j: (j, p)),
                  pl.BlockSpec((tq, 2 * V_DIM), lambda p, j: (j, p)), pl.BlockSpec((rows, 2 * V_DIM), lambda p, j: (0, p)),
                  pl.BlockSpec((2, rows, LANES), lambda p, j: (p, 0, 0)), pl.BlockSpec((2, rows, LANES), lambda p, j: (p, 0, 0))] + [
                      ANY] * (2 * n),
        out_specs=[pl.BlockSpec((rows, 2 * HEAD_PAD), lambda p, j: (0, p)), pl.BlockSpec((tq, 2 * HEAD_PAD), lambda p, j: (j, p)),
                   pl.BlockSpec((tq, 2 * V_DIM), lambda p, j: (j, p))] + [ANY] * n,
        out_shape=[jax.ShapeDtypeStruct((rows, HEADS * HEAD_PAD), F32), jax.ShapeDtypeStruct((rows, HEADS * HEAD_PAD), F32),
                   jax.ShapeDtypeStruct((rows, HEADS * V_DIM), F32)] + [jax.ShapeDtypeStruct(d.shape, d.dtype) for d in dsts],
        input_output_aliases={6 + n + a: 3 + a for a in range(n)}, scratch_shapes=EXCHANGE_SEMS(n) if n else [],
        compiler_params=_params(("arbitrary", "arbitrary") if n else ("parallel", "arbitrary")),
    )(q, k, v, do, lse, delta, *srcs, *dsts)
    return outs[0], outs[1], outs[2], list(outs[3:])


def mla_prep_bwd(dq, dk, dv, z_br, rope, gq, gkv, misc, dz_buf, li):
    rows = dq.shape[0]
    t = ROW_TILE
    cur, _, _, full, layer = _tile_specs(t, rows // HALO, li)
    w8 = HEADS * HEAD_PAD
    uq, keys, values = slice(0, 256), slice(M_UKVK - M_UQ, M_UKVV - M_UQ), slice(M_UKVV - M_UQ, M_WO - M_UQ)

    def body(dq_ref, dk_ref, dv_ref, z_ref, rope_ref, gq_ref, gkv_ref, up_ref, _, dz_ref, dup_ref, dgq_ref, dgkv_ref):
        i = pl.program_id(0)
        cth, s1, s2 = rope_ref[:, 0:128], rope_ref[:, 128:256], rope_ref[:, 256:384]
        dqb = _rope_transposed(dq_ref[...] * Q_SCALE, _lanes8(cth), _lanes8(s1), _lanes8(s2), w8).astype(BF16)
        dq_chunks = _chunks(dqb)
        cq = z_ref[:, 0:256].astype(F32)
        qn, vjp_q = jax.vjp(_rms, cq, gq_ref[...])
        dcq, dgq = vjp_q(sum(_dot_nt(dqk, up_ref[k, uq, :]) for k, dqk in enumerate(dq_chunks)))
        _accumulate(i, dgq_ref, dgq)

        dk = dk_ref[...]
        dkr = sum(dk[:, HEAD_PAD * h:HEAD_PAD * (h + 1)] for h in range(HEADS))
        dkr = _rope_transposed(dkr, cth, s1, s2, HEAD_PAD)
        lane = lax.broadcasted_iota(jnp.int32, (1, HEAD_PAD), 1)
        dkr = jnp.where((lane >= QK_NOPE) & (lane < QK_NOPE + QK_ROPE), dkr, 0.0)
        dkb, dvb = dk.astype(BF16), dv_ref[...].astype(BF16)
        dk_chunks, dv_chunks = _chunks(dkb), _chunks(dvb, width=2 * V_DIM)
        ckv = z_ref[:, 256:384].astype(F32)
        kvn, vjp_kv = jax.vjp(_rms, ckv, gkv_ref[...])
        dckv, dgkv = vjp_kv(sum(_dot_nt(dk_chunks[k], up_ref[k, keys, :]) + _dot_nt(dv_chunks[k], up_ref[k, values, 0:2 * V_DIM])
                                for k in range(N_CHIPS)))
        _accumulate(i, dgkv_ref, dgkv)
        dz_ref[...] = jnp.concatenate([dcq, dckv, dkr], axis=1).astype(BF16)
        qnb, kvnb = qn.astype(BF16), kvn.astype(BF16)
        d_uq, d_keys, d_values = _chunks(_dot_tn(qnb, dqb)), _chunks(_dot_tn(kvnb, dkb)), _chunks(_dot_tn(kvnb, dvb), width=2 * V_DIM)
        for k in range(N_CHIPS):
            padded = jnp.concatenate([d_values[k], jnp.zeros((128, 256 - 2 * V_DIM), F32)], axis=1)
            _accumulate(i, dup_ref.at[k], jnp.concatenate([d_uq[k], d_keys[k], padded], axis=0))

    return pl.pallas_call(
        body, name="mla_prep_bwd", grid=(rows // t,),
        in_specs=[cur(w8), cur(w8), cur(512), cur(512, CQ // 512), cur(384), layer((1, 256)), layer((1, 128)),
                  _misc_spec(misc, M_UQ, M_WO - M_UQ), ANY],
        out_specs=[cur(512, CQ // 512), full((N_CHIPS, M_WO - M_UQ, 256)), full((1, 256)), full((1, 128))],
        out_shape=[jax.ShapeDtypeStruct((rows, ZB), BF16), jax.ShapeDtypeStruct((N_CHIPS, M_WO - M_UQ, 256), F32),
                   jax.ShapeDtypeStruct((1, 256), F32), jax.ShapeDtypeStruct((1, 128), F32)],
        input_output_aliases={8: 0}, compiler_params=_params(("arbitrary",)),
    )(dq, dk, dv, z_br, rope, gq, gkv, misc[0], dz_buf)


def prenorm_bwd(dz_br, w_br, dh_gl, hres, gpre, dh_next, li):
    rows, d = hres.shape
    t = ROW_TILE
    cur, _, _, full, layer = _tile_specs(t, rows // HALO, li)

    def body(dz_ref, w_ref, dp_ref, x_ref, g_ref, dn_ref, dx_ref, dg_ref):
        i = pl.program_id(0)
        dh = _dot_nt(dz_ref[...], w_ref[...]) + dp_ref[...]
        _, vjp = jax.vjp(_rms, x_ref[...], g_ref[...])
        dx, dg = vjp(dh)
        dx_ref[...] = dx + dn_ref[...]
        _accumulate(i, dg_ref, dg)

    return pl.pallas_call(
        body, name="prenorm_bwd", grid=(rows // t,), in_specs=[cur(ZB), layer((d, ZB), 0), cur(d), cur(d), layer((1, d)), cur(d)],
        out_specs=[cur(d), full((1, d))], out_shape=[jax.ShapeDtypeStruct((rows, d), F32), jax.ShapeDtypeStruct((1, d), F32)],
        compiler_params=_params(("arbitrary",)),
    )(dz_br, w_br, dh_gl, hres, gpre, dh_next)


def _mesh_position():
    return lax.axis_index("x"), lax.axis_index("y"), lax.axis_index("c")


def chip_exchange(src, gather, name):
    block = src.shape if gather else src.shape[1:]

    def body(src_ref, dst_ref, send_sems, recv_sems, local_sem):
        x, y, c = _mesh_position()
        me = 2 * x + y
        peers = ((1 - x, y), (x, 1 - y), (1 - x, 1 - y))

        def part(k):
            return src_ref if gather else src_ref.at[k]

        def copy(j, slot):
            px, py = peers[j]
            return pltpu.make_async_remote_copy(src_ref=part(2 * px + py), dst_ref=dst_ref.at[slot], send_sem=send_sems.at[j],
                                                recv_sem=recv_sems.at[j], device_id=(px, py, c), device_id_type=MESH)

        local = pltpu.make_async_copy(part(me), dst_ref.at[me], local_sem)
        local.start()
        sends = [copy(j, me) for j in range(3)]
        for cp in sends:
            cp.start()
        for j, (px, py) in enumerate(peers):
            copy(j, 2 * px + py).wait_recv()
        for cp in sends:
            cp.wait_send()
        local.wait()

    return pl.pallas_call(
        body, name=name, in_specs=[pl.BlockSpec(memory_space=pl.ANY)], out_specs=pl.BlockSpec(memory_space=pl.ANY),
        out_shape=jax.ShapeDtypeStruct((N_CHIPS,) + tuple(block), src.dtype),
        scratch_shapes=[pltpu.SemaphoreType.DMA((3,)), pltpu.SemaphoreType.DMA((3,)), pltpu.SemaphoreType.DMA(())],
    )(src)


def sibling_swap(src, name):
    def body(src_ref, dst_ref, send_sem, recv_sem):
        x, y, c = _mesh_position()
        cp = pltpu.make_async_remote_copy(src_ref=src_ref, dst_ref=dst_ref, send_sem=send_sem, recv_sem=recv_sem,
                                          device_id=(x, y, 1 - c), device_id_type=MESH)
        cp.start()
        cp.wait()

    return pl.pallas_call(
        body, name=name, in_specs=[pl.BlockSpec(memory_space=pl.ANY)], out_specs=pl.BlockSpec(memory_space=pl.ANY),
        out_shape=jax.ShapeDtypeStruct(src.shape, src.dtype),
        scratch_shapes=[pltpu.SemaphoreType.DMA(()), pltpu.SemaphoreType.DMA(())],
    )(src)


def _comm_call(body, name, n_in, out_shapes, n_sems):
    return pl.pallas_call(
        body, name=name, in_specs=[ANY] * n_in, out_specs=[ANY] * len(out_shapes), out_shape=out_shapes,
        scratch_shapes=[pltpu.SemaphoreType.DMA((n,)) for n in n_sems])


def _row_halves(c, rows):
    half = rows // 2
    return pl.ds(pl.multiple_of(c * half, 16), half), pl.ds(pl.multiple_of((1 - c) * half, 16), half)


def _peers():
    x, y, c = _mesh_position()
    return x, y, c, 2 * x + y, ((1 - x, y), (x, 1 - y), (1 - x, 1 - y))


def _gather_ops(src, dst, sems, layer, own_copy):
    ici_send, ici_recv, d2d_send, d2d_recv, own_sems = sems
    n = len(src)

    def fetch(a, j, slot):
        x, y, c, _, peers = _peers()
        px, py = peers[j]
        mine, _ = _row_halves(c, src[a].shape[1])
        return pltpu.make_async_remote_copy(src_ref=src[a].at[layer, mine], dst_ref=dst[a].at[layer, slot, mine], send_sem=ici_send.at[3 * a + j],
                                            recv_sem=ici_recv.at[3 * a + j], device_id=(px, py, c), device_id_type=MESH)

    def forward(a, j, sibling_half):
        x, y, c, _, peers = _peers()
        px, py = peers[j]
        part = dst[a].at[layer, 2 * px + py, _row_halves(c, src[a].shape[1])[1 if sibling_half else 0]]
        return pltpu.make_async_remote_copy(src_ref=part, dst_ref=part, send_sem=d2d_send.at[3 * a + j], recv_sem=d2d_recv.at[3 * a + j],
                                            device_id=(x, y, 1 - c), device_id_type=MESH)

    def own(a):
        return pltpu.make_async_copy(src[a].at[layer], dst[a].at[layer, _peers()[3]], own_sems.at[a])

    def start():
        me = _peers()[3]
        for a in range(n):
            if own_copy:
                own(a).start()
            for j in range(3):
                fetch(a, j, me).start()

    def finish():
        peers = _peers()[4]
        for j, (px, py) in enumerate(peers):
            for a in range(n):
                fetch(a, j, 2 * px + py).wait_recv()
                forward(a, j, False).start()
        for j in range(3):
            for a in range(n):
                forward(a, j, True).wait_recv()
        for j in range(3):
            for a in range(n):
                fetch(a, j, 0).wait_send()
                forward(a, j, False).wait_send()
        if own_copy:
            for a in range(n):
                own(a).wait()

    return start, finish


def _exchange_ops(src, dst, sems, layer):
    send_sems, recv_sems, own_sems = sems
    n = len(src)

    def copy(a, j, slot):
        x, y, c, _, peers = _peers()
        px, py = peers[j]
        return pltpu.make_async_remote_copy(src_ref=src[a].at[2 * px + py], dst_ref=dst[a].at[layer, slot], send_sem=send_sems.at[3 * a + j],
                                            recv_sem=recv_sems.at[3 * a + j], device_id=(px, py, c), device_id_type=MESH)

    def own(a):
        me = _peers()[3]
        return pltpu.make_async_copy(src[a].at[me], dst[a].at[layer, me], own_sems.at[a])

    def start():
        me = _peers()[3]
        for a in range(n):
            own(a).start()
            for j in range(3):
                copy(a, j, me).start()

    def finish():
        peers = _peers()[4]
        for j, (px, py) in enumerate(peers):
            for a in range(n):
                copy(a, j, 2 * px + py).wait_recv()
        for j in range(3):
            for a in range(n):
                copy(a, j, 0).wait_send()
        for a in range(n):
            own(a).wait()

    return start, finish


GATHER_SEMS = lambda n: [pltpu.SemaphoreType.DMA((3 * n,))] * 4 + [pltpu.SemaphoreType.DMA((n,))]
EXCHANGE_SEMS = lambda n: [pltpu.SemaphoreType.DMA((3 * n,))] * 2 + [pltpu.SemaphoreType.DMA((n,))]


def gather_layer(srcs, dsts, layer, name):
    n = len(srcs)

    def body(*refs):
        start, finish = _gather_ops(refs[:n], refs[2 * n:3 * n], refs[3 * n:], layer, False)
        start()
        finish()

    return pl.pallas_call(
        body, name=name, in_specs=[ANY] * (2 * n), out_specs=[ANY] * n, out_shape=[jax.ShapeDtypeStruct(d.shape, d.dtype) for d in dsts],
        input_output_aliases={n + a: a for a in range(n)}, scratch_shapes=GATHER_SEMS(n),
    )(*srcs, *dsts)


def exchange_layer(ss, dsts, layer, name):
    n = len(ss)

    def body(*refs):
        start, finish = _exchange_ops(refs[:n], refs[2 * n:3 * n], refs[3 * n:], layer)
        start()
        finish()

    return pl.pallas_call(
        body, name=name, in_specs=[ANY] * (2 * n), out_specs=[ANY] * n, out_shape=[jax.ShapeDtypeStruct(d.shape, d.dtype) for d in dsts],
        input_output_aliases={n + a: a for a in range(n)}, scratch_shapes=EXCHANGE_SEMS(n),
    )(*ss, *dsts)


def _swap_ops(src, dst, sems):
    send_sems, recv_sems = sems

    def copy(a):
        x, y, c = _mesh_position()
        return pltpu.make_async_remote_copy(src_ref=src[a].at[:, _row_halves(c, src[a].shape[1])[1]], dst_ref=dst[a], send_sem=send_sems.at[a],
                                            recv_sem=recv_sems.at[a], device_id=(x, y, 1 - c), device_id_type=MESH)

    def start():
        for a in range(len(src)):
            copy(a).start()

    def finish():
        for a in range(len(src)):
            copy(a).wait()

    return start, finish


def swap_row_halves(ps, name):
    n = len(ps)

    def body(*refs):
        start, finish = _swap_ops(refs[:n], refs[n:2 * n], refs[2 * n:])
        start()
        finish()

    outs = [jax.ShapeDtypeStruct((p.shape[0], p.shape[1] // 2, p.shape[2]), p.dtype) for p in ps]
    return _comm_call(body, name, n, outs, (n, n))(*ps)


def add_row_half(p, r, c, name):
    n, half, cols = r.shape
    rb = _row_block(half, cols, 2)
    steps = half // rb

    def body(c_ref, p_ref, r_ref, o_ref):
        o_ref[...] = (p_ref[...].astype(F32) + r_ref[...].astype(F32)).astype(BF16)

    return pl.pallas_call(
        body, name=name, out_shape=jax.ShapeDtypeStruct(r.shape, BF16),
        grid_spec=pltpu.PrefetchScalarGridSpec(
            num_scalar_prefetch=1, grid=(n, steps),
            in_specs=[pl.BlockSpec((1, rb, cols), lambda k, i, c_ref: (k, c_ref[0] * steps + i, 0)),
                      pl.BlockSpec((1, rb, cols), lambda k, i, c_ref: (k, i, 0))],
            out_specs=pl.BlockSpec((1, rb, cols), lambda k, i, c_ref: (k, i, 0))),
        compiler_params=_params(("parallel", "parallel")),
    )(jnp.reshape(c, (1,)).astype(jnp.int32), p, r)


def sum_row_halves(l, c, name):
    layers, n, half, cols = l.shape
    rb = _row_block(half, cols, 4)
    steps = half // rb

    def body(c_ref, l_ref, o_ref):
        acc = l_ref[0, 0].astype(F32)
        for s in range(1, n):
            acc = acc + l_ref[0, s].astype(F32)
        o_ref[0] = acc

    return pl.pallas_call(
        body, name=name, out_shape=jax.ShapeDtypeStruct((layers, 2 * half, cols), F32),
        grid_spec=pltpu.PrefetchScalarGridSpec(
            num_scalar_prefetch=1, grid=(layers, steps), in_specs=[pl.BlockSpec((1, n, rb, cols), lambda a, i, c_ref: (a, 0, i, 0))],
            out_specs=pl.BlockSpec((1, rb, cols), lambda a, i, c_ref: (a, c_ref[0] * steps + i, 0))),
        compiler_params=_params(("parallel", "parallel")),
    )(jnp.reshape(c, (1,)).astype(jnp.int32), l)


def share_row_halves(gs, name):
    n = len(gs)

    def body(*refs):
        dst = refs[n:2 * n]
        send_sems, recv_sems = refs[2 * n:]
        x, y, c = _mesh_position()

        def copy(a, sibling_half):
            part = dst[a].at[:, _row_halves(c, dst[a].shape[1])[1 if sibling_half else 0]]
            return pltpu.make_async_remote_copy(src_ref=part, dst_ref=part, send_sem=send_sems.at[a], recv_sem=recv_sems.at[a],
                                                device_id=(x, y, 1 - c), device_id_type=MESH)

        for a in range(n):
            copy(a, False).start()
        for a in range(n):
            copy(a, True).wait_recv()
        for a in range(n):
            copy(a, False).wait_send()

    return pl.pallas_call(
        body, name=name, in_specs=[ANY] * n, out_specs=[ANY] * n, out_shape=[jax.ShapeDtypeStruct(g.shape, g.dtype) for g in gs],
        input_output_aliases={a: a for a in range(n)}, scratch_shapes=[pltpu.SemaphoreType.DMA((n,)), pltpu.SemaphoreType.DMA((n,))],
    )(*gs)


def _row_block(rows, cols, itemsize):
    best = 16
    for rb in range(16, rows + 1, 16):
        if rows % rb == 0 and rb * cols * itemsize <= 2 * 1024 * 1024:
            best = rb
    assert rows % best == 0, (rows, cols)
    return best


def _comm_block(rows):
    return 1024 if rows % 1024 == 0 else rows


def sum_slots(buf, name):
    n, r, c = buf.shape
    rb = _comm_block(r)

    def body(b_ref, o_ref):
        acc = b_ref[0].astype(F32)
        for s in range(1, n):
            acc = acc + b_ref[s].astype(F32)
        o_ref[...] = acc

    return pl.pallas_call(
        body, name=name, grid=(r // rb,), in_specs=[pl.BlockSpec((n, rb, c), lambda i: (0, i, 0))],
        out_specs=pl.BlockSpec((rb, c), lambda i: (i, 0)), out_shape=jax.ShapeDtypeStruct((r, c), F32),
        compiler_params=_params(("parallel",)),
    )(buf)


def add_pair(a, b, out_dtype, name):
    shape = a.shape
    a2, b2 = a.reshape(-1, shape[-1]), b.reshape(-1, shape[-1])
    r, c = a2.shape
    rb = _comm_block(r)

    def body(a_ref, b_ref, o_ref):
        o_ref[...] = (a_ref[...].astype(F32) + b_ref[...].astype(F32)).astype(out_dtype)

    out = pl.pallas_call(
        body, name=name, grid=(r // rb,), in_specs=[pl.BlockSpec((rb, c), lambda i: (i, 0))] * 2,
        out_specs=pl.BlockSpec((rb, c), lambda i: (i, 0)), out_shape=jax.ShapeDtypeStruct((r, c), out_dtype),
        compiler_params=_params(("parallel",)),
    )(a2, b2)
    return out.reshape(shape)


def adamw(w, g, m, v):
    shape = w.shape
    cols = shape[-1]
    rows = math.prod(shape[:-1])
    if rows * cols <= 256 * 1024:
        rb, cb = rows, cols
    else:
        rb = max(r for r in range(8, 2049, 8) if rows % r == 0)
        cb = cols if rb * cols * 4 <= 2 * 1024 * 1024 else 256
    assert rows % rb == 0 and cols % cb == 0, shape

    def body(w_ref, g_ref, m_ref, v_ref, d_ref, nm_ref, nv_ref):
        g_ = g_ref[...]
        nm = ADAM_B1 * m_ref[...] + (1.0 - ADAM_B1) * g_
        nv = ADAM_B2 * v_ref[...] + (1.0 - ADAM_B2) * (g_ * g_)
        m_hat = nm / (1.0 - ADAM_B1 ** ADAM_STEP)
        v_hat = nv / (1.0 - ADAM_B2 ** ADAM_STEP)
        d_ref[...] = -ADAM_LR * (m_hat / (jnp.sqrt(v_hat) + ADAM_EPS) + ADAM_WD * w_ref[...])
        nm_ref[...] = nm
        nv_ref[...] = nv

    spec = pl.BlockSpec((rb, cb), lambda i, j: (i, j))
    outs = pl.pallas_call(
        body, name="adamw", grid=(rows // rb, cols // cb), in_specs=[spec] * 4, out_specs=[spec] * 3,
        out_shape=[jax.ShapeDtypeStruct((rows, cols), F32)] * 3, compiler_params=_params(("parallel", "parallel")),
    )(*(a.reshape(rows, cols) for a in (w, g, m, v)))
    return tuple(o.reshape(shape) for o in outs)


def _pack(arrays, dtype, row_multiple):
    flat = jnp.concatenate([a.astype(dtype).reshape(-1) for a in arrays])
    per = LANES * row_multiple
    total = -(-flat.shape[0] // per) * per
    return jnp.pad(flat, (0, total - flat.shape[0])).reshape(total // LANES, LANES)


def _unpack(buf, shapes):
    flat = buf.reshape(-1)
    out, off = [], 0
    for s in shapes:
        n = math.prod(s)
        out.append(flat[off:off + n].reshape(s))
        off += n
    return out


def _input_weights(blocks):
    c0, c1, c2, c3 = (blocks[..., k, :, :] for k in range(N_CHIPS))
    pad = lambda n: jnp.zeros(c0.shape[:-1] + (n,), blocks.dtype)
    w_br = jnp.concatenate([c1[..., 376:1400], c0[..., 0:896], pad(64), c0[..., 896:928], pad(32), c0[..., 928:], c1[..., 0:376]], axis=-1)
    return w_br, jnp.concatenate([c1[..., 1400:], c2, c3], axis=-1)


def _input_weights_inverse(dw_br, dw_gl):
    c0 = jnp.concatenate([dw_br[..., 1024:1920], dw_br[..., 1984:2016], dw_br[..., 2048:2952]], axis=-1)
    c1 = jnp.concatenate([dw_br[..., 2952:ZB], dw_br[..., 0:1024], dw_gl[..., 0:432]], axis=-1)
    return jnp.stack([c0, c1, dw_gl[..., 432:2264], dw_gl[..., 2264:]], axis=-3)


def _block_diag(pw):
    zeros = lambda n: jnp.zeros(pw.shape[:-3] + (64, n), pw.dtype)
    rows = [jnp.concatenate([zeros(64 * g), pw[..., g, :, :], zeros(64 * (3 - g))], axis=-1) for g in range(4)]
    return jnp.concatenate(rows, axis=-2)


def _block_diag_inverse(d):
    return jnp.stack([d[..., 64 * g:64 * (g + 1), 64 * g:64 * (g + 1)] for g in range(4)], axis=-3)


def _pad_rows(a, n):
    return jnp.pad(a, ((0, n - a.shape[0]), (0, 0)))


def _rope_tables(rows):
    inv = 1.0 / (ROPE_THETA ** (jnp.arange(0, QK_ROPE, 2, dtype=F32) / QK_ROPE))
    ang = jnp.arange(rows, dtype=F32)[:, None] * inv[None, :]
    cos, sin = jnp.cos(ang), jnp.sin(ang)
    one, zero = jnp.ones((rows, 1), F32), jnp.zeros((rows, 1), F32)
    rep = lambda a, n: jnp.broadcast_to(a, (rows, n))
    c = jnp.concatenate([rep(one, 64), cos, cos, rep(one, 32)], axis=1)
    s1 = jnp.concatenate([rep(zero, 64), -sin, rep(zero, 48)], axis=1)
    s2 = jnp.concatenate([rep(zero, 80), sin, rep(zero, 32)], axis=1)
    return jnp.concatenate([c, s1, s2], axis=1)


def _misc_block(parts):
    lead = parts["w_uq"].shape[:-2]
    pad_last = lambda a, n: jnp.pad(a, [(0, 0)] * (a.ndim - 1) + [(0, n - a.shape[-1])])
    uq = pad_last(parts["w_uq"].reshape(lead + (256, 2, QK_NOPE + QK_ROPE)), HEAD_PAD).reshape(lead + (256, 256))
    kv = parts["w_ukv"].reshape(lead + (128, 2, QK_NOPE + V_DIM))
    keys = pad_last(kv[..., :QK_NOPE], HEAD_PAD).reshape(lead + (128, 256))
    values = pad_last(kv[..., QK_NOPE:].reshape(lead + (128, 2 * V_DIM)), 256)
    wo = jnp.swapaxes(parts["w_o"].reshape(lead + (256, N_CHIPS, 256)), -3, -2).reshape(lead + (D_MODEL, 256))
    gap = jnp.zeros(lead + (M_UQ - M_SC - 256, 256), uq.dtype)
    return jnp.concatenate([parts["w_out_mla"], parts["w_out_pool"], parts["w_out_conf"], parts["w_out_sc"], gap, uq, keys, values, wo],
                           axis=-2)


def _misc_unblock(block):
    lead = block.shape[:-2]
    rows = lambda lo, n: block[..., lo:lo + n, :]
    uq = rows(M_UQ, 256).reshape(lead + (256, 2, HEAD_PAD))[..., :QK_NOPE + QK_ROPE].reshape(lead + (256, 2 * (QK_NOPE + QK_ROPE)))
    keys = rows(M_UKVK, 128).reshape(lead + (128, 2, HEAD_PAD))[..., :QK_NOPE]
    values = rows(M_UKVV, 128)[..., :2 * V_DIM].reshape(lead + (128, 2, V_DIM))
    wo = jnp.swapaxes(rows(M_WO, D_MODEL).reshape(lead + (N_CHIPS, 256, 256)), -3, -2).reshape(lead + (256, D_MODEL))
    return dict(w_out_mla=rows(M_MLA, 512), w_out_pool=rows(M_POOL, 256), w_out_conf=rows(M_CONF, 256), w_out_sc=rows(M_SC, 256), w_uq=uq,
                w_ukv=jnp.concatenate([keys, values], axis=-1).reshape(lead + (128, 256)), w_o=wo)


def _to_chip_blocks(name, a):
    if name == "w_o":
        return a.reshape(a.shape[:-2] + (N_CHIPS, a.shape[-2] // N_CHIPS, a.shape[-1]))
    return jnp.swapaxes(a.reshape(a.shape[:-1] + (N_CHIPS, a.shape[-1] // N_CHIPS)), -3, -2)


def _from_chip_blocks(name, b):
    if name == "w_o":
        return b.reshape(b.shape[:-3] + (N_CHIPS * b.shape[-2], b.shape[-1]))
    s = jnp.swapaxes(b, -3, -2)
    return s.reshape(s.shape[:-2] + (N_CHIPS * s.shape[-1],))


LARGE = ("w_in",) + MISC


def gather_small(shards):
    small = chip_exchange(_pack([shards[n] for n, _, _ in SHARDED_SMALL], F32, 8), True, "gather_small_ici")
    per_chip = [_unpack(small[k], [s for _, s, _ in SHARDED_SMALL]) for k in range(N_CHIPS)]
    return {name: jnp.concatenate([per_chip[k][idx] for k in range(N_CHIPS)], axis=axis) for idx, (name, _, axis) in enumerate(SHARDED_SMALL)}


class LocalWeights:
    def __init__(self, full):
        self.w_in = _to_chip_blocks("w_in", full["w_in"])
        self.misc = _misc_block({n: _to_chip_blocks(n, full[n]) for n in MISC}).astype(BF16)
        self.grads = [None] * DEPTH

    def layer(self, i):
        return self.w_in[i], (self.misc, i)

    def gather_with_attention(self, i):
        return None

    def gathered(self, dsts):
        pass

    def exchange_with_attention(self):
        return None

    def exchanged(self, dsts):
        pass

    def swap_with_postnorm(self):
        return None

    def swapped(self, rs):
        pass

    def put_grads(self, i, w_in, misc):
        self.grads[i] = (w_in, misc)

    def reduced(self):
        out = {n: _from_chip_blocks(n, b) for n, b in _misc_unblock(jnp.stack([m for _, m in self.grads])).items()}
        out["w_in"] = _from_chip_blocks("w_in", jnp.stack([w for w, _ in self.grads]))
        return out


class MeshWeights:
    def __init__(self, shards, c, chip):
        self.c, self.chip = c, chip
        self.srcs = [shards["w_in"].astype(BF16), _misc_block({n: shards[n] for n in MISC}).astype(BF16)]
        dsts = [lax.empty((DEPTH, N_CHIPS) + s.shape[1:], BF16) for s in self.srcs]
        self.dsts = gather_layer(self.srcs, dsts, 0, "gather_layer")
        self.landed = [lax.empty((DEPTH, N_CHIPS, s.shape[1] // 2, s.shape[2]), BF16) for s in self.srcs]
        self.pending = self.to_swap = None

    def layer(self, i):
        if i > 0:
            return self.dsts[0][i], (self.dsts[1], i)
        own = (jnp.arange(N_CHIPS) == self.chip)[:, None, None]
        w_in, misc = (jnp.where(own, s[0][None], d[0]) for s, d in zip(self.srcs, self.dsts))
        return w_in, (misc[None], 0)

    def gather_with_attention(self, i):
        return (self.srcs, self.dsts, i + 1) if i + 1 < DEPTH else None

    def gathered(self, dsts):
        if dsts:
            self.dsts = dsts

    def exchange_with_attention(self):
        return None if self.pending is None else (self.pending[0], self.landed, self.pending[1])

    def exchanged(self, dsts):
        if dsts:
            self.landed, self.pending = dsts, None

    def put_grads(self, i, w_in, misc):
        self.to_swap = ([w_in.astype(BF16), misc.astype(BF16)], i)

    def swap_with_postnorm(self):
        return None if self.to_swap is None else self.to_swap[0]

    def swapped(self, rs):
        if rs:
            ps, i = self.to_swap
            self.pending = ([add_row_half(p, r, self.c, "reduce_pair_%d" % a) for a, (p, r) in enumerate(zip(ps, rs))], i)
            self.to_swap = None

    def reduced(self):
        self.swapped(swap_row_halves(self.to_swap[0], "reduce_swap"))
        landed = exchange_layer(self.pending[0], self.landed, self.pending[1], "reduce_exchange")
        gs = [sum_row_halves(l, self.c, "reduce_sum_%d" % a) for a, l in enumerate(landed)]
        g_in, g_misc = share_row_halves(gs, "reduce_share")
        out = {"w_in": g_in}
        out.update(_misc_unblock(g_misc))
        return out


def reduce_small(grads, chip):
    names = [n for n, _ in REPLICATED] + [n for n, _, _ in SHARDED_SMALL]
    buf = _pack([grads[n] for n in names], F32, 8)
    chip_sum = add_pair(buf, sibling_swap(buf, "reduce_small_d2d"), F32, "reduce_small_pair")
    total = sum_slots(chip_exchange(chip_sum, True, "reduce_small_ici"), "reduce_small_sum")
    out = dict(zip(names, _unpack(total, [grads[n].shape for n in names])))
    for name, shape, axis in SHARDED_SMALL:
        out[name] = lax.dynamic_slice_in_dim(out[name], chip * shape[axis], shape[axis], axis)
    return out


def _prepare_small(w):
    row = lambda a: a[:, None, :]
    conf_vec = jnp.concatenate([row(w["conf_dw_b"]), row(w["conf_ln_g"]), row(w["conf_ln_b"]), jnp.zeros((DEPTH, 5, 256), F32)], axis=1)
    return dict(
        gpre=row(w["pre_norm_g"]), bias=row(w["gate_bias"]), pwbd=_block_diag(w["pool_w"]).astype(BF16), pscale=row(w["pool_scale"]),
        gq=row(w["q_norm_g"]), gkv=row(w["kv_norm_g"]), conf_w=jnp.pad(w["conf_dw_w"].astype(F32), ((0, 0), (0, 32 - CONF_K), (0, 0))),
        conf_vec=conf_vec, sc_w=jnp.pad(w["sc_dw_w"].astype(F32), ((0, 0), (0, 8 - SC_K), (0, 0))), gpost=row(w["post_norm_g"]))


def _prepare_layer(w_in_blocks, misc):
    w_br, w_gl = _input_weights(w_in_blocks)
    one = lambda a: a.astype(BF16)[None]
    return dict(w_br=one(w_br), w_gl=one(w_gl), misc=misc)


def local_step(x, target, w, large):
    seq = x.shape[0]
    length = N_META + seq
    rows = -(-length // ROW_TILE) * ROW_TILE
    bt = _big_tile(rows)
    hres = _pad_rows(jnp.concatenate([w["meta_tokens"].astype(F32), x], axis=0), rows)
    tgt = jnp.pad(target, ((N_META, rows - length), (0, 0)))
    rope = _rope_tables(rows)
    sw = _prepare_small(w)

    saved = []
    for i in range(DEPTH):
        lw = _prepare_layer(*large.layer(i))
        z_br, hb = prenorm_project(hres, sw["gpre"], lw["w_br"], i)
        z_gl = matmul(hb, lw["w_gl"], "nn", BF16, bt, 1024, D_MODEL, "project_gates", b_layer=0)
        ua, uc, ud, q, k, v = branches_fwd(z_br, rope, sw["pwbd"], sw["pscale"], sw["gq"], sw["gkv"], lw["misc"], sw["conf_w"],
                                           sw["conf_vec"], sw["sc_w"], i)
        o_att, lse, dsts = attention_fwd(q, k, v, large.gather_with_attention(i))
        large.gathered(dsts)
        ub, mb, o, hnew = merge_fwd(ua, o_att, uc, ud, z_br, z_gl, sw["bias"], lw["misc"], sw["gpost"], hres, i)
        saved.append(dict(lw=lw, hres=hres, hb=hb, z_br=z_br, z_gl=z_gl, ua=ua, ub=ub, uc=uc, ud=ud, q=q, k=k, v=v, o_att=o_att,
                          lse=lse, mb=mb, o=o))
        hres = hnew

    dh, total = loss_head(hres, tgt, seq)

    g = {n: [None] * DEPTH for n in ("gpre", "bias", "pwbd", "pscale", "gq", "gkv", "conf_w", "conf_vec", "sc_w", "gpost")}
    for i in reversed(range(DEPTH)):
        s = saved[i]
        lw = s["lw"]
        dm, dwo, g["gpost"][i], rs = postnorm_bwd(dh, s["o"], s["mb"], lw["misc"], sw["gpost"], i, large.swap_with_postnorm())
        large.swapped(rs)
        dz_br = lax.empty((rows, ZB), BF16)
        dua, do, duc, dud, dz_gl, dwout, g["bias"][i], dz_br, delta = merge_bwd(
            dm, s["ua"], s["ub"], s["uc"], s["ud"], s["z_gl"], sw["bias"], lw["misc"], s["o_att"], s["z_br"], dz_br, i)
        dz_br, g["pwbd"][i], g["pscale"][i] = pool_bwd(s["z_br"], dua, sw["pwbd"], sw["pscale"], dz_br, i)
        dz_br, g["sc_w"][i] = shortconv_bwd(s["z_br"], dud, sw["sc_w"], dz_br, i)
        dc, dz_br, g["conf_vec"][i] = conformer_bwd_tail(s["z_br"], duc, sw["conf_w"], sw["conf_vec"], dz_br, i)
        dz_br, g["conf_w"][i] = conformer_bwd_conv(s["z_br"], dc, sw["conf_w"], dz_br, i)
        dq, dk, dv, dsts = attention_bwd(s["q"], s["k"], s["v"], do, s["lse"], delta, large.exchange_with_attention())
        large.exchanged(dsts)
        dz_br, dwup, g["gq"][i], g["gkv"][i] = mla_prep_bwd(dq, dk, dv, s["z_br"], rope, sw["gq"], sw["gkv"], lw["misc"], dz_br, i)
        dw_br = matmul(s["hb"], dz_br, "tn", BF16, D_MODEL, ZB // 2, bt, "grad_w_branch")
        dw_gl = matmul(s["hb"], dz_gl, "tn", BF16, D_MODEL, 1024, bt, "grad_w_gates")
        dh_gl = matmul(dz_gl, lw["w_gl"], "nt", F32, bt, D_MODEL, 1024, "grad_h_gates", b_layer=0)
        dh, g["gpre"][i] = prenorm_bwd(dz_br, lw["w_br"], dh_gl, s["hres"], sw["gpre"], dh, i)
        gap = jnp.zeros((N_CHIPS, M_UQ - M_SC - 256, 256), F32)
        large.put_grads(i, _input_weights_inverse(dw_br, dw_gl), jnp.concatenate([dwout, gap, dwup, dwo], axis=1))

    g = {n: jnp.stack(parts) for n, parts in g.items()}
    grads = dict(
        meta_tokens=dh[:N_META], pre_norm_g=g["gpre"][:, 0], gate_bias=g["bias"][:, 0], pool_w=_block_diag_inverse(g["pwbd"]),
        pool_scale=g["pscale"][:, 0], q_norm_g=g["gq"][:, 0], kv_norm_g=g["gkv"][:, 0], conf_dw_w=g["conf_w"][:, :CONF_K],
        conf_dw_b=g["conf_vec"][:, 2], conf_ln_g=g["conf_vec"][:, 0], conf_ln_b=g["conf_vec"][:, 1], sc_dw_w=g["sc_w"][:, :SC_K],
        post_norm_g=g["gpost"][:, 0])
    return total[0, 0], dh[N_META:length], grads


def kernel(x, meta_tokens, pre_norm_g, w_in, gate_bias, pool_w, pool_scale, w_out_pool, q_norm_g, w_uq, kv_norm_g, w_ukv, w_out_mla, conf_dw_w, conf_dw_b, conf_ln_g, conf_ln_b, w_out_conf, sc_dw_w, w_out_sc, w_o, post_norm_g, loss_target, m_meta_tokens, m_pre_norm_g, m_w_in, m_gate_bias, m_pool_w, m_pool_scale, m_w_out_pool, m_q_norm_g, m_w_uq, m_kv_norm_g, m_w_ukv, m_w_out_mla, m_conf_dw_w, m_conf_dw_b, m_conf_ln_g, m_conf_ln_b, m_w_out_conf, m_sc_dw_w, m_w_out_sc, m_w_o, m_post_norm_g, v_meta_tokens, v_pre_norm_g, v_w_in, v_gate_bias, v_pool_w, v_pool_scale, v_w_out_pool, v_q_norm_g, v_w_uq, v_kv_norm_g, v_w_ukv, v_w_out_mla, v_conf_dw_w, v_conf_dw_b, v_conf_ln_g, v_conf_ln_b, v_w_out_conf, v_sc_dw_w, v_w_out_sc, v_w_o, v_post_norm_g):
    args = locals()
    weights = {n: args[n] for n in WEIGHT_ORDER}
    c = lax.axis_index("c")
    chip = 2 * lax.axis_index("x") + lax.axis_index("y")

    small = {n: weights[n] for n, _ in REPLICATED}
    small.update(gather_small(weights))
    large = MeshWeights(weights, c, chip)
    total, dx, grads = local_step(x[0], loss_target[0], small, large)
    loss = lax.psum(total * (0.5 / D_MODEL), ("x", "y", "c"))

    reduced = large.reduced()
    reduced.update(reduce_small(grads, chip))

    flip = lambda a: jnp.swapaxes(a, 1, 2)
    deltas, new_m, new_v = [], [], []
    for n in WEIGHT_ORDER:
        operands = (weights[n], reduced[n], args["m_" + n], args["v_" + n])
        if n == "w_in":
            operands = (flip(operands[0]), lax.optimization_barrier(flip(operands[1])), flip(operands[2]), flip(operands[3]))
            reduced[n] = flip(operands[1])
        d, nm, nv = adamw(*operands)
        if n == "w_in":
            d, nm, nv = flip(d), flip(nm), flip(nv)
        deltas.append(d)
        new_m.append(nm)
        new_v.append(nv)
    return (loss, dx[None], *[reduced[n] for n in WEIGHT_ORDER], *deltas, *new_m, *new_v)
```

```python
import functools
import math

import jax
import jax.numpy as jnp
from jax import lax
from jax.experimental import pallas as pl
from jax.experimental.pallas import tpu as pltpu

F32 = jnp.float32
BF16 = jnp.bfloat16

D_MODEL = 1024
DEPTH = 4
N_META = 16
EPS = 1e-6
HEADS = 8
QK_NOPE = 64
QK_ROPE = 32
V_DIM = 64
HEAD_PAD = 128
ROPE_THETA = 10000.0
Q_SCALE = (QK_NOPE + QK_ROPE) ** -0.5
CONF_K = 31
SC_K = 3
IN_W = 7328
N_CHIPS = 4

ZB = 3328
ZG = 4096
BG, C2, XV, SG, PV, PG, CQ, CKV, KR, MG, CA, CGT, CG = (0, 256, 512, 768, 1024, 1280, 1536, 1792, 1920, 2048, 2560, 2816, 3072)

KEY_GROUP = 4
ROW_TILE = 384
HALO = 32
LANES = 128
VMEM_LIMIT = 56 * 1024 * 1024

ADAM_LR = 0.001
ADAM_B1 = 0.9
ADAM_B2 = 0.999
ADAM_EPS = 1e-08
ADAM_WD = 0.01
ADAM_STEP = 10

MESH = pl.DeviceIdType.MESH
ANY = pl.BlockSpec(memory_space=pl.ANY)

MISC = ("w_out_mla", "w_out_pool", "w_out_conf", "w_out_sc", "w_uq", "w_ukv", "w_o")
M_MLA, M_POOL, M_CONF, M_SC, M_UQ, M_UKVK, M_UKVV, M_WO, MISC_ROWS = 0, 512, 768, 1024, 1536, 1792, 1920, 2048, 3072
SHARDED_SMALL = (
    ("meta_tokens", (N_META, 256), 1),
    ("conf_dw_w", (DEPTH, CONF_K, 64), 2),
    ("sc_dw_w", (DEPTH, SC_K, 64), 2),
)
REPLICATED = (
    ("pre_norm_g", (DEPTH, D_MODEL)),
    ("gate_bias", (DEPTH, 4 * D_MODEL)),
    ("pool_w", (DEPTH, 4, 64, 64)),
    ("pool_scale", (DEPTH, 256)),
    ("q_norm_g", (DEPTH, 256)),
    ("kv_norm_g", (DEPTH, 128)),
    ("conf_dw_b", (DEPTH, 256)),
    ("conf_ln_g", (DEPTH, 256)),
    ("conf_ln_b", (DEPTH, 256)),
    ("post_norm_g", (DEPTH, D_MODEL)),
)
WEIGHT_ORDER = ("meta_tokens", "pre_norm_g", "w_in", "gate_bias", "pool_w", "pool_scale", "w_out_pool", "q_norm_g", "w_uq",
                "kv_norm_g", "w_ukv", "w_out_mla", "conf_dw_w", "conf_dw_b", "conf_ln_g", "conf_ln_b", "w_out_conf", "sc_dw_w",
                "w_out_sc", "w_o", "post_norm_g")


def _dot(a, b):
    return lax.dot_general(a, b, (((1,), (0,)), ((), ())), preferred_element_type=F32)


def _dot_nt(a, b):
    return lax.dot_general(a, b, (((1,), (1,)), ((), ())), preferred_element_type=F32)


def _dot_tn(a, b):
    return lax.dot_general(a, b, (((0,), (0,)), ((), ())), preferred_element_type=F32)


def _sigmoid(x):
    return jax.nn.sigmoid(x)


def _silu(x):
    return x * _sigmoid(x)


def _silu_grad(x):
    s = _sigmoid(x)
    return s * (1.0 + x * (1.0 - s))


def _rms(x, g):
    return x * lax.rsqrt(jnp.mean(x * x, axis=-1, keepdims=True) + EPS) * g


def _sh(x, d):
    return x if d == 0 else pltpu.roll(x, d, 0)


def _ash(x, d):
    return x if d == 0 else pltpu.roll(x, x.shape[0] - d, 0)


def _lanes8(t):
    return jnp.concatenate([t] * HEADS, axis=1)


def _pool_window_sums(v, shift):
    a2 = v + shift(v, 1)
    a4 = a2 + shift(a2, 2)
    a8 = a4 + shift(a4, 4)
    a16 = a8 + shift(a8, 8)
    lane = lax.broadcasted_iota(jnp.int32, v.shape, 1)
    return jnp.where(lane < 64, a2, jnp.where(lane < 128, a4, jnp.where(lane < 192, a8, a16)))


def _pool_counts(first_row, rows):
    pos = first_row + lax.broadcasted_iota(jnp.int32, (rows, 256), 0)
    lane = lax.broadcasted_iota(jnp.int32, (rows, 256), 1)
    width = jnp.where(lane < 64, 2, jnp.where(lane < 128, 4, jnp.where(lane < 192, 8, 16)))
    return jnp.maximum(jnp.minimum(pos + 1, width), 1).astype(F32)


def _params(sem=None):
    return pltpu.CompilerParams(dimension_semantics=sem, vmem_limit_bytes=VMEM_LIMIT)


def _tile_specs(t, n_halo_blocks, li=0):
    per = t // HALO

    def layer(shape, idx=li):
        return pl.BlockSpec((None,) + tuple(shape), lambda i: (idx,) + (0,) * len(shape))

    def cur(c, cb=0):
        return pl.BlockSpec((t, c), lambda i: (i, cb))

    def prev(c, cb=0):
        return pl.BlockSpec((HALO, c), lambda i: (jnp.maximum(i * per - 1, 0), cb))

    def nxt(c, cb=0):
        return pl.BlockSpec((HALO, c), lambda i: (jnp.minimum((i + 1) * per, n_halo_blocks - 1), cb))

    def full(shape):
        return pl.BlockSpec(shape, lambda i: (0,) * len(shape))

    return cur, prev, nxt, full, layer


def _big_tile(rows):
    return rows // 3 if rows % (3 * LANES) == 0 else ROW_TILE


def matmul(a, b, mode, out_dtype, tm, tn, tk, name, b_layer=None):
    bs = b.shape if b_layer is None else b.shape[1:]
    lead = () if b_layer is None else (None,)
    pick = (lambda *ix: ix) if b_layer is None else (lambda *ix: (b_layer,) + ix)
    if mode == "nn":
        (m, k), n = a.shape, bs[1]
        a_spec = pl.BlockSpec((tm, tk), lambda i, j, kk: (i, kk))
        b_spec = pl.BlockSpec(lead + (tk, tn), lambda i, j, kk: pick(kk, j))
        dot = _dot
    elif mode == "nt":
        (m, k), n = a.shape, bs[0]
        a_spec = pl.BlockSpec((tm, tk), lambda i, j, kk: (i, kk))
        b_spec = pl.BlockSpec(lead + (tn, tk), lambda i, j, kk: pick(j, kk))
        dot = _dot_nt
    else:
        (k, m), n = a.shape, bs[1]
        a_spec = pl.BlockSpec((tk, tm), lambda i, j, kk: (kk, i))
        b_spec = pl.BlockSpec(lead + (tk, tn), lambda i, j, kk: pick(kk, j))
        dot = _dot_tn
    assert m % tm == 0 and n % tn == 0 and k % tk == 0, (a.shape, bs, tm, tn, tk)
    nk = k // tk

    def body(a_ref, b_ref, o_ref, acc_ref):
        kk = pl.program_id(2)

        @pl.when(kk == 0)
        def _():
            acc_ref[...] = jnp.zeros_like(acc_ref)

        acc_ref[...] += dot(a_ref[...], b_ref[...])

        @pl.when(kk == nk - 1)
        def _():
            o_ref[...] = acc_ref[...].astype(out_dtype)

    return pl.pallas_call(
        body, name=name, grid=(m // tm, n // tn, nk), in_specs=[a_spec, b_spec],
        out_specs=pl.BlockSpec((tm, tn), lambda i, j, kk: (i, j)), out_shape=jax.ShapeDtypeStruct((m, n), out_dtype),
        scratch_shapes=[pltpu.VMEM((tm, tn), F32)], compiler_params=_params(("parallel", "parallel", "arbitrary")),
    )(a, b)


def prenorm_project(hres, g, w, li):
    rows, d = hres.shape
    n = w.shape[2]
    tm, tn = _big_tile(rows), n // 2

    def body(x_ref, g_ref, w_ref, z_ref, hb_ref):
        @pl.when(pl.program_id(1) == 0)
        def _():
            hb_ref[...] = _rms(x_ref[...], g_ref[...]).astype(BF16)

        z_ref[...] = _dot(hb_ref[...], w_ref[...]).astype(BF16)

    return pl.pallas_call(
        body, name="prenorm_project", grid=(rows // tm, n // tn),
        in_specs=[pl.BlockSpec((tm, d), lambda i, j: (i, 0)), pl.BlockSpec((None, 1, d), lambda i, j: (li, 0, 0)),
                  pl.BlockSpec((None, d, tn), lambda i, j: (0, 0, j))],
        out_specs=[pl.BlockSpec((tm, tn), lambda i, j: (i, j)), pl.BlockSpec((tm, d), lambda i, j: (i, 0))],
        out_shape=[jax.ShapeDtypeStruct((rows, n), BF16), jax.ShapeDtypeStruct((rows, d), BF16)],
        compiler_params=_params(("parallel", "arbitrary")),
    )(hres, g, w)


def _rope(q, c, s1, s2, width):
    return q * c + pltpu.roll(q, width - 16, 1) * s1 + pltpu.roll(q, 16, 1) * s2


def _rope_transposed(dq, c, s1, s2, width):
    return dq * c + pltpu.roll(dq * s1, 16, 1) + pltpu.roll(dq * s2, width - 16, 1)


def _conf_conv(g1, w_ref):
    acc = jnp.zeros_like(g1)
    for k in range(CONF_K):
        acc = acc + w_ref[k:k + 1, :] * _sh(g1, CONF_K - 1 - k)
    return acc


def _conf_tail(c, cg, lg, lb):
    mu = jnp.mean(c, axis=-1, keepdims=True)
    xc = c - mu
    var = jnp.mean(xc * xc, axis=-1, keepdims=True)
    n = xc * lax.rsqrt(var + EPS) * lg + lb
    return _silu(n) * _silu(cg)


def _misc_spec(misc, row0, rows):
    assert row0 % rows == 0
    return pl.BlockSpec((None, N_CHIPS, rows, 256), lambda i: (misc[1], 0, row0 // rows, 0))


def _chip_columns(x, w_ref, row0, rows, lanes=256):
    return jnp.concatenate([_dot(x, w_ref[k, row0:row0 + rows, 0:lanes]) for k in range(N_CHIPS)], axis=1)


def branches_fwd(z_br, rope, pwbd, pscale, gq, gkv, misc, conf_w, conf_vec, sc_w, li):
    rows = z_br.shape[0]
    t = ROW_TILE
    cur, prev, _, _, layer = _tile_specs(t, rows // HALO, li)

    def body(zc_ref, zp_ref, rope_ref, pw_ref, ps_ref, gq_ref, gkv_ref, up_ref, cw_ref, cv_ref, sw_ref,
             ua_ref, uc_ref, ud_ref, q_ref, k_ref, v_ref):
        i = pl.program_id(0)
        zp = jnp.where(i == 0, jnp.zeros(zp_ref.shape, zp_ref.dtype), zp_ref[...])

        def ext(lo, w=256):
            return jnp.concatenate([zp[:, lo:lo + w], zc_ref[:, lo:lo + w]], axis=0).astype(F32)

        def col(lo, w=256):
            return zc_ref[:, lo:lo + w].astype(F32)

        v = ext(PV)
        p = (_pool_window_sums(v, _sh) / _pool_counts(i * t - HALO, t + HALO) - v)[HALO:]
        ya = _dot(p.astype(BF16), pw_ref[...]) * ps_ref[...]
        ua_ref[...] = (ya * _silu(col(PG))).astype(BF16)

        g1 = ext(CA) * _sigmoid(ext(CGT))
        c = _conf_conv(g1, cw_ref)[HALO:] + cv_ref[0:1, :]
        uc_ref[...] = _conf_tail(c, col(CG), cv_ref[1:2, :], cv_ref[2:3, :]).astype(BF16)

        e = ext(C2) * ext(XV)
        f = jnp.zeros_like(e)
        for k in range(SC_K):
            f = f + sw_ref[k:k + 1, :] * _sh(e, SC_K - 1 - k)
        ud_ref[...] = (col(BG) * f[HALO:] * _silu(col(SG))).astype(BF16)

        cth, s1, s2 = rope_ref[:, 0:128], rope_ref[:, 128:256], rope_ref[:, 256:384]
        qn = _rms(col(CQ), gq_ref[...]).astype(BF16)
        q = _chip_columns(qn, up_ref, 0, 256)
        w8 = HEADS * HEAD_PAD
        q_ref[...] = (_rope(q, _lanes8(cth), _lanes8(s1), _lanes8(s2), w8) * Q_SCALE).astype(BF16)
        kvn = _rms(col(CKV, 128), gkv_ref[...]).astype(BF16)
        kr = _rope(col(KR, 128), cth, s1, s2, HEAD_PAD)
        k_ref[...] = (_chip_columns(kvn, up_ref, M_UKVK - M_UQ, 128) + _lanes8(kr)).astype(BF16)
        v_ref[...] = _chip_columns(kvn, up_ref, M_UKVV - M_UQ, 128, 2 * V_DIM).astype(BF16)

    outs = [jax.ShapeDtypeStruct((rows, 256), BF16)] * 3 + [jax.ShapeDtypeStruct((rows, 1024), BF16)] * 2 + [
        jax.ShapeDtypeStruct((rows, 512), BF16)]
    return pl.pallas_call(
        body, name="branches_fwd", grid=(rows // t,),
        in_specs=[cur(ZB), prev(ZB), cur(384), layer((256, 256)), layer((1, 256)), layer((1, 256)), layer((1, 128)),
                  _misc_spec(misc, M_UQ, M_WO - M_UQ), layer((32, 256)), layer((8, 256)), layer((8, 256))],
        out_specs=[cur(256), cur(256), cur(256), cur(1024), cur(1024), cur(512)], out_shape=outs,
        compiler_params=_params(("parallel",)),
    )(z_br, z_br, rope, pwbd, pscale, gq, gkv, misc[0], conf_w, conf_vec, sc_w)


def _head_lane_mask(h):
    lane = lax.broadcasted_iota(jnp.int32, (1, 2 * V_DIM), 1)
    return (lane >= V_DIM * h) & (lane < V_DIM * (h + 1))


def attention_fwd(q, k, v, gather=None):
    rows = q.shape[0]
    tq = ROW_TILE
    nq = rows // tq
    n = 0 if gather is None else len(gather[0])

    def body(*refs):
        if n:
            start, finish = _gather_ops(refs[3:3 + n], refs[5 + 2 * n:5 + 3 * n], refs[5 + 3 * n:], gather[2], True)
            pl.when((pl.program_id(0) == 0) & (pl.program_id(1) == 0))(start)
        compute(*refs[:3], *refs[3 + 2 * n:5 + 2 * n])
        if n:
            pl.when((pl.program_id(0) == HEADS // 2 - 1) & (pl.program_id(1) == nq - 1))(finish)

    def compute(q_ref, k_ref, v_ref, o_ref, lse_ref):
        i = pl.program_id(1)

        def head_step(h, tile, n_tiles, carry, masked):
            m, l, acc = carry
            width = n_tiles * tq
            r0 = pl.multiple_of(tile * tq, tq)
            kh = k_ref[pl.ds(r0, width), HEAD_PAD * h:HEAD_PAD * (h + 1)]
            vh = jnp.where(_head_lane_mask(h), v_ref[pl.ds(r0, width), :], jnp.zeros((), BF16))
            s = _dot_nt(q_ref[:, HEAD_PAD * h:HEAD_PAD * (h + 1)], kh)
            if masked:
                row = lax.broadcasted_iota(jnp.int32, (tq, width), 0)
                colm = lax.broadcasted_iota(jnp.int32, (tq, width), 1)
                s = jnp.where(colm <= row + (width - tq), s, -1e30)
            m2 = jnp.maximum(m, jnp.max(s, axis=-1, keepdims=True))
            alpha = jnp.exp(m - m2)
            pr = jnp.exp(s - m2)
            return m2, alpha * l + jnp.sum(pr, axis=-1, keepdims=True), alpha * acc + _dot(pr.astype(BF16), vh)

        def step(tile, n_tiles, carry, masked):
            return tuple(head_step(h, tile, n_tiles, carry[h], masked) for h in range(2))

        init = (jnp.full((tq, 1), -1e30, F32), jnp.zeros((tq, 1), F32), jnp.zeros((tq, 2 * V_DIM), F32))
        group = min(KEY_GROUP, nq)
        carry = lax.fori_loop(0, i // group, lambda t, cr: step(group * t, group, cr, False), (init, init))
        carry = lax.switch(i % group, [functools.partial(lambda cr, r: step(i - r, r + 1, cr, True), r=r) for r in range(group)], carry)
        out = jnp.zeros((tq, 2 * V_DIM), F32)
        for h, (m, l, acc) in enumerate(carry):
            out = out + acc / l
            lse_ref[h] = jnp.broadcast_to(m + jnp.log(l), (tq, LANES))
        o_ref[...] = out.astype(BF16)

    srcs, dsts = ([], []) if gather is None else (list(gather[0]), list(gather[1]))
    outs = pl.pallas_call(
        body, name="attention_fwd" if gather is None else "attention_fwd_gather", grid=(HEADS // 2, nq),
        in_specs=[pl.BlockSpec((tq, 2 * HEAD_PAD), lambda p, i: (i, p)), pl.BlockSpec((rows, 2 * HEAD_PAD), lambda p, i: (0, p)),
                  pl.BlockSpec((rows, 2 * V_DIM), lambda p, i: (0, p))] + [ANY] * (2 * n),
        out_specs=[pl.BlockSpec((tq, 2 * V_DIM), lambda p, i: (i, p)), pl.BlockSpec((2, tq, LANES), lambda p, i: (p, i, 0))] + [ANY] * n,
        out_shape=[jax.ShapeDtypeStruct((rows, HEADS * V_DIM), BF16), jax.ShapeDtypeStruct((HEADS, rows, LANES), F32)] + [
            jax.ShapeDtypeStruct(d.shape, d.dtype) for d in dsts],
        input_output_aliases={3 + n + a: 2 + a for a in range(n)}, scratch_shapes=GATHER_SEMS(n) if n else [],
        compiler_params=_params(("arbitrary", "arbitrary") if n else ("parallel", "parallel")),
    )(q, k, v, *srcs, *dsts)
    return outs[0], outs[1], list(outs[2:])


OUT_PROJECTIONS = ((M_POOL, 256), (M_MLA, 512), (M_CONF, 256), (M_SC, 256))


def _chunks(x, n=N_CHIPS, width=256):
    return [x[:, width * k:width * (k + 1)] for k in range(n)]


def merge_fwd(ua, o_att, uc, ud, z_br, z_gl, bias, misc, gpost, hres, li):
    rows = hres.shape[0]
    t = ROW_TILE
    cur, _, _, _, layer = _tile_specs(t, rows // HALO, li)
    d = D_MODEL

    def body(ua_ref, ob_ref, uc_ref, ud_ref, mg_ref, gl_ref, b_ref, wout_ref, wo_ref, gp_ref, h_ref, ub_ref, mb_ref, o_ref, hn_ref):
        ub = (ob_ref[...].astype(F32) * _silu(mg_ref[...].astype(F32))).astype(BF16)
        ub_ref[...] = ub
        m = jnp.zeros((t, d), F32)
        for idx, (u, (row0, n)) in enumerate(zip((ua_ref[...], ub, uc_ref[...], ud_ref[...]), OUT_PROJECTIONS)):
            gate = _sigmoid(gl_ref[:, d * idx:d * (idx + 1)].astype(F32) + b_ref[:, d * idx:d * (idx + 1)])
            m = m + gate * _chip_columns(u, wout_ref, row0, n)
        mb = m.astype(BF16)
        mb_ref[...] = mb
        o = jnp.concatenate([sum(_dot(mk, wo_ref[k, 256 * j:256 * (j + 1), :]) for k, mk in enumerate(_chunks(mb)))
                             for j in range(N_CHIPS)], axis=1)
        o_ref[...] = o
        hn_ref[...] = h_ref[...] + _rms(o, gp_ref[...])

    return pl.pallas_call(
        body, name="merge_fwd", grid=(rows // t,),
        in_specs=[cur(256), cur(512), cur(256), cur(256), cur(512, MG // 512), cur(ZG), layer((1, ZG)), _misc_spec(misc, 0, 1280),
                  _misc_spec(misc, M_WO, D_MODEL), layer((1, d)), cur(d)],
        out_specs=[cur(512), cur(d), cur(d), cur(d)],
        out_shape=[jax.ShapeDtypeStruct((rows, 512), BF16), jax.ShapeDtypeStruct((rows, d), BF16), jax.ShapeDtypeStruct((rows, d), F32),
                   jax.ShapeDtypeStruct((rows, d), F32)],
        compiler_params=_params(("parallel",)),
    )(ua, o_att, uc, ud, z_br, z_gl, bias, misc[0], misc[0], gpost, hres)


def loss_head(hres, target, n_tokens):
    rows, d = hres.shape
    t = ROW_TILE
    cur, _, _, full, _ = _tile_specs(t, rows // HALO)
    n_steps = rows // t

    def body(h_ref, t_ref, dh_ref, tot_ref, acc_ref):
        i = pl.program_id(0)

        @pl.when(i == 0)
        def _():
            acc_ref[...] = jnp.zeros_like(acc_ref)

        r = i * t + lax.broadcasted_iota(jnp.int32, (t, 1), 0)
        diff = jnp.where((r >= N_META) & (r < N_META + n_tokens), h_ref[...] - t_ref[...], 0.0)
        dh_ref[...] = diff * (1.0 / d)
        acc_ref[...] += jnp.sum(diff * diff, axis=0, keepdims=True)

        @pl.when(i == n_steps - 1)
        def _():
            tot_ref[...] = jnp.broadcast_to(jnp.sum(acc_ref[...], axis=1, keepdims=True), (1, LANES))

    return pl.pallas_call(
        body, name="loss_head", grid=(n_steps,), in_specs=[cur(d), cur(d)], out_specs=[cur(d), full((1, LANES))],
        out_shape=[jax.ShapeDtypeStruct((rows, d), F32), jax.ShapeDtypeStruct((1, LANES), F32)],
        scratch_shapes=[pltpu.VMEM((1, d), F32)], compiler_params=_params(("arbitrary",)),
    )(hres, target)


def _accumulate(i, ref, value):
    @pl.when(i == 0)
    def _():
        ref[...] = value

    @pl.when(i > 0)
    def _():
        ref[...] += value


def postnorm_bwd(dh, o, mb, misc, gpost, li, swap=None):
    rows, d = dh.shape
    t = ROW_TILE
    cur, _, _, full, layer = _tile_specs(t, rows // HALO, li)
    n = 0 if swap is None else len(swap)
    steps = rows // t

    def body(*refs):
        if n:
            start, finish = _swap_ops(refs[5:5 + n], refs[8 + n:8 + 2 * n], refs[8 + 2 * n:])
            pl.when(pl.program_id(0) == 0)(start)
        compute(*refs[:5], *refs[5 + n:8 + n])
        if n:
            pl.when(pl.program_id(0) == steps - 1)(finish)

    def compute(dh_ref, o_ref, mb_ref, wo_ref, gp_ref, dm_ref, dwo_ref, dgp_ref):
        i = pl.program_id(0)
        _, vjp = jax.vjp(_rms, o_ref[...], gp_ref[...])
        do, dg = vjp(dh_ref[...])
        dob = do.astype(BF16)
        dm_ref[...] = jnp.concatenate([sum(_dot_nt(dj, wo_ref[k, 256 * j:256 * (j + 1), :]) for j, dj in enumerate(_chunks(dob)))
                                       for k in range(N_CHIPS)], axis=1)
        dwo = _dot_tn(mb_ref[...], dob)
        for k in range(N_CHIPS):
            _accumulate(i, dwo_ref.at[k], jnp.concatenate(_chunks(dwo[256 * k:256 * (k + 1), :]), axis=0))
        _accumulate(i, dgp_ref, dg)

    sent = [] if swap is None else list(swap)
    outs = pl.pallas_call(
        body, name="postnorm_bwd" if swap is None else "postnorm_bwd_swap", grid=(steps,),
        in_specs=[cur(d), cur(d), cur(d), _misc_spec(misc, M_WO, d), layer((1, d))] + [ANY] * n,
        out_specs=[cur(d), full((N_CHIPS, d, 256)), full((1, d))] + [ANY] * n,
        out_shape=[jax.ShapeDtypeStruct((rows, d), F32), jax.ShapeDtypeStruct((N_CHIPS, d, 256), F32), jax.ShapeDtypeStruct((1, d), F32)] + [
            jax.ShapeDtypeStruct((p.shape[0], p.shape[1] // 2, p.shape[2]), p.dtype) for p in sent],
        scratch_shapes=[pltpu.SemaphoreType.DMA((n,)), pltpu.SemaphoreType.DMA((n,))] if n else [],
        compiler_params=_params(("arbitrary",)),
    )(dh, o, mb, misc[0], gpost, *sent)
    return outs[0], outs[1], outs[2], list(outs[3:])


def merge_bwd(dm, ua, ub, uc, ud, z_gl, bias, misc, o_att, z_br, dz_buf, li):
    rows, d = dm.shape
    t = ROW_TILE
    cur, _, _, full, layer = _tile_specs(t, rows // HALO, li)

    def body(dm_ref, ua_ref, ub_ref, uc_ref, ud_ref, gl_ref, b_ref, w_ref, o_ref, mg_ref, _,
             dua_ref, do_ref, duc_ref, dud_ref, dgl_ref, dw_ref, db_ref, dmg_ref, delta_ref):
        i = pl.program_id(0)
        dm = dm_ref[...]
        groups = ((ua_ref, dua_ref), (ub_ref, None), (uc_ref, duc_ref), (ud_ref, dud_ref))
        for idx, ((u_ref, du_ref), (row0, n)) in enumerate(zip(groups, OUT_PROJECTIONS)):
            cols = slice(d * idx, d * (idx + 1))
            u = u_ref[...]
            gate = _sigmoid(gl_ref[:, cols].astype(F32) + b_ref[:, cols])
            dgl = dm * _chip_columns(u, w_ref, row0, n) * gate * (1.0 - gate)
            dgl_ref[:, cols] = dgl.astype(BF16)
            _accumulate(i, db_ref.at[:, cols], jnp.sum(dgl, axis=0, keepdims=True))
            dyb = (dm * gate).astype(BF16)
            du = sum(_dot_nt(dyk, w_ref[k, row0:row0 + n, :]) for k, dyk in enumerate(_chunks(dyb)))
            for k, dwk in enumerate(_chunks(_dot_tn(u, dyb))):
                _accumulate(i, dw_ref.at[k, row0:row0 + n, :], dwk)
            if du_ref is not None:
                du_ref[...] = du
                continue
            o, mg = o_ref[...].astype(F32), mg_ref[...].astype(F32)
            do = du * _silu(mg)
            do_ref[...] = do.astype(BF16)
            dmg_ref[...] = (du * o * _silu_grad(mg)).astype(BF16)
            prod = do * o
            lane = lax.broadcasted_iota(jnp.int32, (1, HEADS * V_DIM), 1)
            for h in range(HEADS):
                part = jnp.where((lane >= V_DIM * h) & (lane < V_DIM * (h + 1)), prod, 0.0)
                delta_ref[h] = jnp.broadcast_to(jnp.sum(part, axis=-1, keepdims=True), (t, LANES))

    return pl.pallas_call(
        body, name="merge_bwd", grid=(rows // t,),
        in_specs=[cur(d), cur(256), cur(512), cur(256), cur(256), cur(ZG), layer((1, ZG)), _misc_spec(misc, 0, 1280), cur(512),
                  cur(512, MG // 512), ANY],
        out_specs=[cur(256), cur(512), cur(256), cur(256), cur(ZG), full((N_CHIPS, 1280, 256)), full((1, ZG)), cur(512, MG // 512),
                   pl.BlockSpec((HEADS, t, LANES), lambda i: (0, i, 0))],
        out_shape=[jax.ShapeDtypeStruct((rows, 256), F32), jax.ShapeDtypeStruct((rows, 512), BF16), jax.ShapeDtypeStruct((rows, 256), F32),
                   jax.ShapeDtypeStruct((rows, 256), F32), jax.ShapeDtypeStruct((rows, ZG), BF16),
                   jax.ShapeDtypeStruct((N_CHIPS, 1280, 256), F32), jax.ShapeDtypeStruct((1, ZG), F32),
                   jax.ShapeDtypeStruct((rows, ZB), BF16), jax.ShapeDtypeStruct((HEADS, rows, LANES), F32)],
        input_output_aliases={10: 7}, compiler_params=_params(("arbitrary",)),
    )(dm, ua, ub, uc, ud, z_gl, bias, misc[0], o_att, z_br, dz_buf)


def pool_shortconv_bwd(z_br, dua, dud, pwbd, pscale, sc_w, dz_buf, li):
    rows = z_br.shape[0]
    t = ROW_TILE
    n_steps = rows // t
    cur, prev, nxt, full, layer = _tile_specs(t, rows // HALO, li)

    def body(zc_ref, zp_ref, zn_ref, dac_ref, dan_ref, ddc_ref, ddn_ref, pw_ref, ps_ref, sw_ref, _, dz_ref, dpw_ref, dps_ref, dw_ref):
        i = pl.program_id(0)
        last = i == n_steps - 1
        zp = jnp.where(i == 0, jnp.zeros(zp_ref.shape, zp_ref.dtype), zp_ref[...])
        zn = jnp.where(last, jnp.zeros(zn_ref.shape, zn_ref.dtype), zn_ref[...])

        def ext(lo):
            return jnp.concatenate([zp[:, lo:lo + 256], zc_ref[:, lo:lo + 256], zn[:, lo:lo + 256]], axis=0).astype(F32)

        def ext_grad(c_ref, n_ref):
            return jnp.concatenate([jnp.zeros((HALO, 256), F32), c_ref[...], jnp.where(last, jnp.zeros(n_ref.shape, F32), n_ref[...])], axis=0)

        mid = slice(HALO, HALO + t)

        bg, c2, xv, sg = ext(BG), ext(C2), ext(XV), ext(SG)
        du = ext_grad(ddc_ref, ddn_ref)
        e = c2 * xv
        shifted = [_sh(e, SC_K - 1 - k) for k in range(SC_K)]
        f = sum(sw_ref[k:k + 1, :] * shifted[k] for k in range(SC_K))
        gate = _silu(sg)
        df = du * gate * bg
        de = sum(sw_ref[k:k + 1, :] * _ash(df, SC_K - 1 - k) for k in range(SC_K))
        d_sc = [(du * gate * f)[mid], (de * xv)[mid], (de * c2)[mid], (du * bg * f * _silu_grad(sg))[mid]]
        dw = jnp.concatenate([jnp.sum((df * shifted[k])[mid], axis=0, keepdims=True) for k in range(SC_K)] + [
            jnp.zeros((8 - SC_K, 256), F32)], axis=0)
        _accumulate(i, dw_ref, dw)

        v, pg = ext(PV), ext(PG)
        cnt = _pool_counts(i * t - HALO, t + 2 * HALO)
        p = (_pool_window_sums(v, _sh) / cnt - v)[mid]
        dya = ext_grad(dac_ref, dan_ref) * _silu(pg)
        dypb = (dya * ps_ref[...]).astype(BF16)
        dp = _dot_nt(dypb, pw_ref[...])
        dv = (_pool_window_sums(dp / cnt, _ash) - dp)[mid]
        pb = p.astype(BF16)
        pw = _dot(pb, pw_ref[...])
        dpg = dac_ref[...] * pw * ps_ref[...] * _silu_grad(pg[mid])
        _accumulate(i, dpw_ref, _dot_tn(pb, dypb[mid]))
        _accumulate(i, dps_ref, jnp.sum(dya[mid] * pw, axis=0, keepdims=True))

        dz_ref[...] = jnp.concatenate(d_sc + [dv, dpg], axis=1).astype(BF16)

    return pl.pallas_call(
        body, name="pool_shortconv_bwd", grid=(n_steps,),
        in_specs=[cur(ZB), prev(ZB), nxt(ZB), cur(256), nxt(256), cur(256), nxt(256), layer((256, 256)), layer((1, 256)), layer((8, 256)),
                  ANY],
        out_specs=[cur(1536, BG // 1536), full((256, 256)), full((1, 256)), full((8, 256))],
        out_shape=[jax.ShapeDtypeStruct((rows, ZB), BF16), jax.ShapeDtypeStruct((256, 256), F32), jax.ShapeDtypeStruct((1, 256), F32),
                   jax.ShapeDtypeStruct((8, 256), F32)],
        input_output_aliases={10: 0}, compiler_params=_params(("arbitrary",)),
    )(z_br, z_br, z_br, dua, dua, dud, dud, pwbd, pscale, sc_w, dz_buf)


def conformer_bwd_tail(z_br, duc, conf_w, conf_vec, dz_buf, li):
    rows = z_br.shape[0]
    t = ROW_TILE
    cur, prev, _, full, layer = _tile_specs(t, rows // HALO, li)

    def body(zc_ref, zp_ref, du_ref, cw_ref, cv_ref, _, dc_ref, dcg_ref, dv_ref):
        i = pl.program_id(0)
        zp = jnp.where(i == 0, jnp.zeros(zp_ref.shape, zp_ref.dtype), zp_ref[...])

        def ext(lo):
            return jnp.concatenate([zp[:, lo:lo + 256], zc_ref[:, lo:lo + 256]], axis=0).astype(F32)

        g1 = ext(CA) * _sigmoid(ext(CGT))
        c = _conf_conv(g1, cw_ref)[HALO:] + cv_ref[0:1, :]
        _, vjp = jax.vjp(_conf_tail, c, zc_ref[:, CG:CG + 256].astype(F32), cv_ref[1:2, :], cv_ref[2:3, :])
        dc, dcg, dlg, dlb = vjp(du_ref[...])
        dc_ref[...] = dc
        dcg_ref[...] = dcg.astype(BF16)
        dvec = jnp.concatenate([dlg, dlb, jnp.sum(dc, axis=0, keepdims=True), jnp.zeros((5, 256), F32)], axis=0)
        _accumulate(i, dv_ref, dvec)

    return pl.pallas_call(
        body, name="conformer_bwd_tail", grid=(rows // t,), in_specs=[cur(ZB), prev(ZB), cur(256), layer((32, 256)), layer((8, 256)), ANY],
        out_specs=[cur(256), cur(256, CG // 256), full((8, 256))],
        out_shape=[jax.ShapeDtypeStruct((rows, 256), F32), jax.ShapeDtypeStruct((rows, ZB), BF16), jax.ShapeDtypeStruct((8, 256), F32)],
        input_output_aliases={5: 1}, compiler_params=_params(("arbitrary",)),
    )(z_br, z_br, duc, conf_w, conf_vec, dz_buf)


def conformer_bwd_conv(z_br, dc, conf_w, dz_buf, li):
    rows = z_br.shape[0]
    t = ROW_TILE
    n_steps = rows // t
    cur, prev, nxt, full, layer = _tile_specs(t, rows // HALO, li)

    def body(zc_ref, zp_ref, dc_ref, dn_ref, cw_ref, _, dz_ref, dw_ref):
        i = pl.program_id(0)
        zp = jnp.where(i == 0, jnp.zeros(zp_ref.shape, zp_ref.dtype), zp_ref[...])
        dcn = jnp.where(i == n_steps - 1, jnp.zeros(dn_ref.shape, dn_ref.dtype), dn_ref[...])

        def ext(lo):
            return jnp.concatenate([zp[:, lo:lo + 256], zc_ref[:, lo:lo + 256]], axis=0).astype(F32)

        a, gt = ext(CA), ext(CGT)
        sg = _sigmoid(gt)
        g1 = a * sg
        dc = dc_ref[...]
        dce = jnp.concatenate([dc, dcn], axis=0)
        dg1 = jnp.zeros_like(dce)
        dws = []
        for k in range(CONF_K):
            dg1 = dg1 + cw_ref[k:k + 1, :] * _ash(dce, CONF_K - 1 - k)
            dws.append(jnp.sum(dc * _sh(g1, CONF_K - 1 - k)[HALO:], axis=0, keepdims=True))
        dg1 = dg1[:t]
        ac, sc = a[HALO:], sg[HALO:]
        dz_ref[...] = jnp.concatenate([dg1 * sc, dg1 * ac * sc * (1.0 - sc)], axis=1).astype(BF16)
        _accumulate(i, dw_ref, jnp.concatenate(dws + [jnp.zeros((32 - CONF_K, 256), F32)], axis=0))

    return pl.pallas_call(
        body, name="conformer_bwd_conv", grid=(n_steps,), in_specs=[cur(ZB), prev(ZB), cur(256), nxt(256), layer((32, 256)), ANY],
        out_specs=[cur(512, CA // 512), full((32, 256))],
        out_shape=[jax.ShapeDtypeStruct((rows, ZB), BF16), jax.ShapeDtypeStruct((32, 256), F32)],
        input_output_aliases={5: 0}, compiler_params=_params(("arbitrary",)),
    )(z_br, z_br, dc, dc, conf_w, dz_buf)


def attention_bwd(q, k, v, do, lse, delta, exchange=None):
    rows = q.shape[0]
    tq = ROW_TILE
    nq = rows // tq
    n = 0 if exchange is None else len(exchange[0])

    def body(*refs):
        if n:
            start, finish = _exchange_ops(refs[6:6 + n], refs[9 + 2 * n:9 + 3 * n], refs[9 + 3 * n:], exchange[2])
            pl.when((pl.program_id(0) == 0) & (pl.program_id(1) == 0))(start)
        compute(*refs[:6], *refs[6 + 2 * n:9 + 2 * n])
        if n:
            pl.when((pl.program_id(0) == HEADS // 2 - 1) & (pl.program_id(1) == nq - 1))(finish)

    def compute(q_ref, k_ref, v_ref, do_ref, lse_ref, dl_ref, dq_ref, dk_ref, dv_ref):
        j = pl.program_id(1)

        @pl.when(j == 0)
        def _():
            dq_ref[...] = jnp.zeros_like(dq_ref)

        def head_step(h, tile, n_tiles, dk, dv, diagonal):
            lanes = slice(HEAD_PAD * h, HEAD_PAD * (h + 1))
            hm = _head_lane_mask(h)
            kh = k_ref[:, lanes]
            vh = jnp.where(hm, v_ref[...], jnp.zeros((), BF16))
            r0, width = pl.multiple_of(tile * tq, tq), n_tiles * tq
            qi = q_ref[pl.ds(r0, width), lanes]
            doi = jnp.where(hm, do_ref[pl.ds(r0, width), :], jnp.zeros((), BF16))
            s = _dot_nt(qi, kh)
            if diagonal:
                s = jnp.where(lax.broadcasted_iota(jnp.int32, (tq, tq), 1) <= lax.broadcasted_iota(jnp.int32, (tq, tq), 0), s, -1e30)
            pr = jnp.exp(s - lse_ref[h, pl.ds(r0, width), :][:, 0:1])
            dv = dv + _dot_tn(pr.astype(BF16), doi)
            dp = _dot_nt(doi, vh)
            ds = (pr * (dp - dl_ref[h, pl.ds(r0, width), :][:, 0:1])).astype(BF16)
            dq_ref[pl.ds(r0, width), lanes] += _dot(ds, kh)
            return dk + _dot_tn(ds, qi), dv

        def step(tile, n_tiles, carry, diagonal):
            dk0, dk1, dv = carry
            dk0, dv = head_step(0, tile, n_tiles, dk0, dv, diagonal)
            dk1, dv = head_step(1, tile, n_tiles, dk1, dv, diagonal)
            return dk0, dk1, dv

        zero = jnp.zeros((tq, HEAD_PAD), F32)
        carry = step(j, 1, (zero, zero, jnp.zeros((tq, 2 * V_DIM), F32)), True)
        odd = (nq - 1 - j) % 2
        carry = lax.cond(odd == 1, lambda cr: step(j + 1, 1, cr, False), lambda cr: cr, carry)
        dk0, dk1, dv = lax.fori_loop(0, (nq - 1 - j) // 2, lambda t, cr: step(j + 1 + odd + 2 * t, 2, cr, False), carry)
        dk_ref[:, 0:HEAD_PAD] = dk0
        dk_ref[:, HEAD_PAD:2 * HEAD_PAD] = dk1
        dv_ref[...] = dv

    srcs, dsts = ([], []) if exchange is None else (list(exchange[0]), list(exchange[1]))
    outs = pl.pallas_call(
        body, name="attention_bwd" if exchange is None else "attention_bwd_exchange", grid=(HEADS // 2, nq),
        in_specs=[pl.BlockSpec((rows, 2 * HEAD_PAD), lambda p, j: (0, p)), pl.BlockSpec((tq, 2 * HEAD_PAD), lambda p, j: (j, p)),
                  pl.BlockSpec((tq, 2 * V_DIM), lambda p, j: (j, p)), pl.BlockSpec((rows, 2 * V_DIM), lambda p, j: (0, p)),
                  pl.BlockSpec((2, rows, LANES), lambda p, j: (p, 0, 0)), pl.BlockSpec((2, rows, LANES), lambda p, j: (p, 0, 0))] + [
                      ANY] * (2 * n),
        out_specs=[pl.BlockSpec((rows, 2 * HEAD_PAD), lambda p, j: (0, p)), pl.BlockSpec((tq, 2 * HEAD_PAD), lambda p, j: (j, p)),
                   pl.BlockSpec((tq, 2 * V_DIM), lambda p, j: (j, p))] + [ANY] * n,
        out_shape=[jax.ShapeDtypeStruct((rows, HEADS * HEAD_PAD), F32), jax.ShapeDtypeStruct((rows, HEADS * HEAD_PAD), F32),
                   jax.ShapeDtypeStruct((rows, HEADS * V_DIM), F32)] + [jax.ShapeDtypeStruct(d.shape, d.dtype) for d in dsts],
        input_output_aliases={6 + n + a: 3 + a for a in range(n)}, scratch_shapes=EXCHANGE_SEMS(n) if n else [],
        compiler_params=_params(("arbitrary", "arbitrary") if n else ("parallel", "arbitrary")),
    )(q, k, v, do, lse, delta, *srcs, *dsts)
    return outs[0], outs[1], outs[2], list(outs[3:])


def mla_prep_bwd(dq, dk, dv, z_br, rope, gq, gkv, misc, dz_buf, li):
    rows = dq.shape[0]
    t = ROW_TILE
    cur, _, _, full, layer = _tile_specs(t, rows // HALO, li)
    w8 = HEADS * HEAD_PAD
    uq, keys, values = slice(0, 256), slice(M_UKVK - M_UQ, M_UKVV - M_UQ), slice(M_UKVV - M_UQ, M_WO - M_UQ)

    def body(dq_ref, dk_ref, dv_ref, z_ref, rope_ref, gq_ref, gkv_ref, up_ref, _, dz_ref, dup_ref, dgq_ref, dgkv_ref):
        i = pl.program_id(0)
        cth, s1, s2 = rope_ref[:, 0:128], rope_ref[:, 128:256], rope_ref[:, 256:384]
        dqb = _rope_transposed(dq_ref[...] * Q_SCALE, _lanes8(cth), _lanes8(s1), _lanes8(s2), w8).astype(BF16)
        dq_chunks = _chunks(dqb)
        cq = z_ref[:, 0:256].astype(F32)
        qn, vjp_q = jax.vjp(_rms, cq, gq_ref[...])
        dcq, dgq = vjp_q(sum(_dot_nt(dqk, up_ref[k, uq, :]) for k, dqk in enumerate(dq_chunks)))
        _accumulate(i, dgq_ref, dgq)

        dk = dk_ref[...]
        dkr = sum(dk[:, HEAD_PAD * h:HEAD_PAD * (h + 1)] for h in range(HEADS))
        dkr = _rope_transposed(dkr, cth, s1, s2, HEAD_PAD)
        lane = lax.broadcasted_iota(jnp.int32, (1, HEAD_PAD), 1)
        dkr = jnp.where((lane >= QK_NOPE) & (lane < QK_NOPE + QK_ROPE), dkr, 0.0)
        dkb, dvb = dk.astype(BF16), dv_ref[...].astype(BF16)
        dk_chunks, dv_chunks = _chunks(dkb), _chunks(dvb, width=2 * V_DIM)
        ckv = z_ref[:, 256:384].astype(F32)
        kvn, vjp_kv = jax.vjp(_rms, ckv, gkv_ref[...])
        dckv, dgkv = vjp_kv(sum(_dot_nt(dk_chunks[k], up_ref[k, keys, :]) + _dot_nt(dv_chunks[k], up_ref[k, values, 0:2 * V_DIM])
                                for k in range(N_CHIPS)))
        _accumulate(i, dgkv_ref, dgkv)
        dz_ref[...] = jnp.concatenate([dcq, dckv, dkr], axis=1).astype(BF16)
        qnb, kvnb = qn.astype(BF16), kvn.astype(BF16)
        d_uq, d_keys, d_values = _chunks(_dot_tn(qnb, dqb)), _chunks(_dot_tn(kvnb, dkb)), _chunks(_dot_tn(kvnb, dvb), width=2 * V_DIM)
        for k in range(N_CHIPS):
            padded = jnp.concatenate([d_values[k], jnp.zeros((128, 256 - 2 * V_DIM), F32)], axis=1)
            _accumulate(i, dup_ref.at[k], jnp.concatenate([d_uq[k], d_keys[k], padded], axis=0))

    return pl.pallas_call(
        body, name="mla_prep_bwd", grid=(rows // t,),
        in_specs=[cur(w8), cur(w8), cur(512), cur(512, CQ // 512), cur(384), layer((1, 256)), layer((1, 128)),
                  _misc_spec(misc, M_UQ, M_WO - M_UQ), ANY],
        out_specs=[cur(512, CQ // 512), full((N_CHIPS, M_WO - M_UQ, 256)), full((1, 256)), full((1, 128))],
        out_shape=[jax.ShapeDtypeStruct((rows, ZB), BF16), jax.ShapeDtypeStruct((N_CHIPS, M_WO - M_UQ, 256), F32),
                   jax.ShapeDtypeStruct((1, 256), F32), jax.ShapeDtypeStruct((1, 128), F32)],
        input_output_aliases={8: 0}, compiler_params=_params(("arbitrary",)),
    )(dq, dk, dv, z_br, rope, gq, gkv, misc[0], dz_buf)


def prenorm_bwd(dz_br, w_br, dh_gl, hres, gpre, dh_next, li):
    rows, d = hres.shape
    t = ROW_TILE
    cur, _, _, full, layer = _tile_specs(t, rows // HALO, li)

    def body(dz_ref, w_ref, dp_ref, x_ref, g_ref, dn_ref, dx_ref, dg_ref):
        i = pl.program_id(0)
        dh = _dot_nt(dz_ref[...], w_ref[...]) + dp_ref[...]
        _, vjp = jax.vjp(_rms, x_ref[...], g_ref[...])
        dx, dg = vjp(dh)
        dx_ref[...] = dx + dn_ref[...]
        _accumulate(i, dg_ref, dg)

    return pl.pallas_call(
        body, name="prenorm_bwd", grid=(rows // t,), in_specs=[cur(ZB), layer((d, ZB), 0), cur(d), cur(d), layer((1, d)), cur(d)],
        out_specs=[cur(d), full((1, d))], out_shape=[jax.ShapeDtypeStruct((rows, d), F32), jax.ShapeDtypeStruct((1, d), F32)],
        compiler_params=_params(("arbitrary",)),
    )(dz_br, w_br, dh_gl, hres, gpre, dh_next)


def _mesh_position():
    return lax.axis_index("x"), lax.axis_index("y"), lax.axis_index("c")


def chip_exchange(src, gather, name):
    block = src.shape if gather else src.shape[1:]

    def body(src_ref, dst_ref, send_sems, recv_sems, local_sem):
        x, y, c = _mesh_position()
        me = 2 * x + y
        peers = ((1 - x, y), (x, 1 - y), (1 - x, 1 - y))

        def part(k):
            return src_ref if gather else src_ref.at[k]

        def copy(j, slot):
            px, py = peers[j]
            return pltpu.make_async_remote_copy(src_ref=part(2 * px + py), dst_ref=dst_ref.at[slot], send_sem=send_sems.at[j],
                                                recv_sem=recv_sems.at[j], device_id=(px, py, c), device_id_type=MESH)

        local = pltpu.make_async_copy(part(me), dst_ref.at[me], local_sem)
        local.start()
        sends = [copy(j, me) for j in range(3)]
        for cp in sends:
            cp.start()
        for j, (px, py) in enumerate(peers):
            copy(j, 2 * px + py).wait_recv()
        for cp in sends:
            cp.wait_send()
        local.wait()

    return pl.pallas_call(
        body, name=name, in_specs=[pl.BlockSpec(memory_space=pl.ANY)], out_specs=pl.BlockSpec(memory_space=pl.ANY),
        out_shape=jax.ShapeDtypeStruct((N_CHIPS,) + tuple(block), src.dtype),
        scratch_shapes=[pltpu.SemaphoreType.DMA((3,)), pltpu.SemaphoreType.DMA((3,)), pltpu.SemaphoreType.DMA(())],
    )(src)


def sibling_swap(src, name):
    def body(src_ref, dst_ref, send_sem, recv_sem):
        x, y, c = _mesh_position()
        cp = pltpu.make_async_remote_copy(src_ref=src_ref, dst_ref=dst_ref, send_sem=send_sem, recv_sem=recv_sem,
                                          device_id=(x, y, 1 - c), device_id_type=MESH)
        cp.start()
        cp.wait()

    return pl.pallas_call(
        body, name=name, in_specs=[pl.BlockSpec(memory_space=pl.ANY)], out_specs=pl.BlockSpec(memory_space=pl.ANY),
        out_shape=jax.ShapeDtypeStruct(src.shape, src.dtype),
        scratch_shapes=[pltpu.SemaphoreType.DMA(()), pltpu.SemaphoreType.DMA(())],
    )(src)


def _comm_call(body, name, n_in, out_shapes, n_sems):
    return pl.pallas_call(
        body, name=name, in_specs=[ANY] * n_in, out_specs=[ANY] * len(out_shapes), out_shape=out_shapes,
        scratch_shapes=[pltpu.SemaphoreType.DMA((n,)) for n in n_sems])


def _row_halves(c, rows):
    half = rows // 2
    return pl.ds(pl.multiple_of(c * half, 16), half), pl.ds(pl.multiple_of((1 - c) * half, 16), half)


def _peers():
    x, y, c = _mesh_position()
    return x, y, c, 2 * x + y, ((1 - x, y), (x, 1 - y), (1 - x, 1 - y))


def _gather_ops(src, dst, sems, layer, own_copy):
    ici_send, ici_recv, d2d_send, d2d_recv, own_sems = sems
    n = len(src)

    def fetch(a, j, slot):
        x, y, c, _, peers = _peers()
        px, py = peers[j]
        mine, _ = _row_halves(c, src[a].shape[1])
        return pltpu.make_async_remote_copy(src_ref=src[a].at[layer, mine], dst_ref=dst[a].at[layer, slot, mine], send_sem=ici_send.at[3 * a + j],
                                            recv_sem=ici_recv.at[3 * a + j], device_id=(px, py, c), device_id_type=MESH)

    def forward(a, j, sibling_half):
        x, y, c, _, peers = _peers()
        px, py = peers[j]
        part = dst[a].at[layer, 2 * px + py, _row_halves(c, src[a].shape[1])[1 if sibling_half else 0]]
        return pltpu.make_async_remote_copy(src_ref=part, dst_ref=part, send_sem=d2d_send.at[3 * a + j], recv_sem=d2d_recv.at[3 * a + j],
                                            device_id=(x, y, 1 - c), device_id_type=MESH)

    def own(a):
        return pltpu.make_async_copy(src[a].at[layer], dst[a].at[layer, _peers()[3]], own_sems.at[a])

    def start():
        me = _peers()[3]
        for a in range(n):
            if own_copy:
                own(a).start()
            for j in range(3):
                fetch(a, j, me).start()

    def finish():
        peers = _peers()[4]
        for j, (px, py) in enumerate(peers):
            for a in range(n):
                fetch(a, j, 2 * px + py).wait_recv()
                forward(a, j, False).start()
        for j in range(3):
            for a in range(n):
                forward(a, j, True).wait_recv()
        for j in range(3):
            for a in range(n):
                fetch(a, j, 0).wait_send()
                forward(a, j, False).wait_send()
        if own_copy:
            for a in range(n):
                own(a).wait()

    return start, finish


def _exchange_ops(src, dst, sems, layer):
    send_sems, recv_sems, own_sems = sems
    n = len(src)

    def copy(a, j, slot):
        x, y, c, _, peers = _peers()
        px, py = peers[j]
        return pltpu.make_async_remote_copy(src_ref=src[a].at[2 * px + py], dst_ref=dst[a].at[layer, slot], send_sem=send_sems.at[3 * a + j],
                                            recv_sem=recv_sems.at[3 * a + j], device_id=(px, py, c), device_id_type=MESH)

    def own(a):
        me = _peers()[3]
        return pltpu.make_async_copy(src[a].at[me], dst[a].at[layer, me], own_sems.at[a])

    def start():
        me = _peers()[3]
        for a in range(n):
            own(a).start()
            for j in range(3):
                copy(a, j, me).start()

    def finish():
        peers = _peers()[4]
        for j, (px, py) in enumerate(peers):
            for a in range(n):
                copy(a, j, 2 * px + py).wait_recv()
        for j in range(3):
            for a in range(n):
                copy(a, j, 0).wait_send()
        for a in range(n):
            own(a).wait()

    return start, finish


GATHER_SEMS = lambda n: [pltpu.SemaphoreType.DMA((3 * n,))] * 4 + [pltpu.SemaphoreType.DMA((n,))]
EXCHANGE_SEMS = lambda n: [pltpu.SemaphoreType.DMA((3 * n,))] * 2 + [pltpu.SemaphoreType.DMA((n,))]


def gather_layer(srcs, dsts, layer, name):
    n = len(srcs)

    def body(*refs):
        start, finish = _gather_ops(refs[:n], refs[2 * n:3 * n], refs[3 * n:], layer, False)
        start()
        finish()

    return pl.pallas_call(
        body, name=name, in_specs=[ANY] * (2 * n), out_specs=[ANY] * n, out_shape=[jax.ShapeDtypeStruct(d.shape, d.dtype) for d in dsts],
        input_output_aliases={n + a: a for a in range(n)}, scratch_shapes=GATHER_SEMS(n),
    )(*srcs, *dsts)


def exchange_layer(ss, dsts, layer, name):
    n = len(ss)

    def body(*refs):
        start, finish = _exchange_ops(refs[:n], refs[2 * n:3 * n], refs[3 * n:], layer)
        start()
        finish()

    return pl.pallas_call(
        body, name=name, in_specs=[ANY] * (2 * n), out_specs=[ANY] * n, out_shape=[jax.ShapeDtypeStruct(d.shape, d.dtype) for d in dsts],
        input_output_aliases={n + a: a for a in range(n)}, scratch_shapes=EXCHANGE_SEMS(n),
    )(*ss, *dsts)


def _swap_ops(src, dst, sems):
    send_sems, recv_sems = sems

    def copy(a):
        x, y, c = _mesh_position()
        return pltpu.make_async_remote_copy(src_ref=src[a].at[:, _row_halves(c, src[a].shape[1])[1]], dst_ref=dst[a], send_sem=send_sems.at[a],
                                            recv_sem=recv_sems.at[a], device_id=(x, y, 1 - c), device_id_type=MESH)

    def start():
        for a in range(len(src)):
            copy(a).start()

    def finish():
        for a in range(len(src)):
            copy(a).wait()

    return start, finish


def swap_row_halves(ps, name):
    n = len(ps)

    def body(*refs):
        start, finish = _swap_ops(refs[:n], refs[n:2 * n], refs[2 * n:])
        start()
        finish()

    outs = [jax.ShapeDtypeStruct((p.shape[0], p.shape[1] // 2, p.shape[2]), p.dtype) for p in ps]
    return _comm_call(body, name, n, outs, (n, n))(*ps)


def add_row_half(p, r, c, name):
    n, half, cols = r.shape
    rb = _row_block(half, cols, 2)
    steps = half // rb

    def body(c_ref, p_ref, r_ref, o_ref):
        o_ref[...] = (p_ref[...].astype(F32) + r_ref[...].astype(F32)).astype(BF16)

    return pl.pallas_call(
        body, name=name, out_shape=jax.ShapeDtypeStruct(r.shape, BF16),
        grid_spec=pltpu.PrefetchScalarGridSpec(
            num_scalar_prefetch=1, grid=(n, steps),
            in_specs=[pl.BlockSpec((1, rb, cols), lambda k, i, c_ref: (k, c_ref[0] * steps + i, 0)),
                      pl.BlockSpec((1, rb, cols), lambda k, i, c_ref: (k, i, 0))],
            out_specs=pl.BlockSpec((1, rb, cols), lambda k, i, c_ref: (k, i, 0))),
        compiler_params=_params(("parallel", "parallel")),
    )(jnp.reshape(c, (1,)).astype(jnp.int32), p, r)


def sum_row_halves(l, c, name):
    layers, n, half, cols = l.shape
    rb = _row_block(half, cols, 4)
    steps = half // rb

    def body(c_ref, l_ref, o_ref):
        acc = l_ref[0, 0].astype(F32)
        for s in range(1, n):
            acc = acc + l_ref[0, s].astype(F32)
        o_ref[0] = acc

    return pl.pallas_call(
        body, name=name, out_shape=jax.ShapeDtypeStruct((layers, 2 * half, cols), F32),
        grid_spec=pltpu.PrefetchScalarGridSpec(
            num_scalar_prefetch=1, grid=(layers, steps), in_specs=[pl.BlockSpec((1, n, rb, cols), lambda a, i, c_ref: (a, 0, i, 0))],
            out_specs=pl.BlockSpec((1, rb, cols), lambda a, i, c_ref: (a, c_ref[0] * steps + i, 0))),
        compiler_params=_params(("parallel", "parallel")),
    )(jnp.reshape(c, (1,)).astype(jnp.int32), l)


def share_row_halves(gs, name):
    n = len(gs)

    def body(*refs):
        dst = refs[n:2 * n]
        send_sems, recv_sems = refs[2 * n:]
        x, y, c = _mesh_position()

        def copy(a, sibling_half):
            part = dst[a].at[:, _row_halves(c, dst[a].shape[1])[1 if sibling_half else 0]]
            return pltpu.make_async_remote_copy(src_ref=part, dst_ref=part, send_sem=send_sems.at[a], recv_sem=recv_sems.at[a],
                                                device_id=(x, y, 1 - c), device_id_type=MESH)

        for a in range(n):
            copy(a, False).start()
        for a in range(n):
            copy(a, True).wait_recv()
        for a in range(n):
            copy(a, False).wait_send()

    return pl.pallas_call(
        body, name=name, in_specs=[ANY] * n, out_specs=[ANY] * n, out_shape=[jax.ShapeDtypeStruct(g.shape, g.dtype) for g in gs],
        input_output_aliases={a: a for a in range(n)}, scratch_shapes=[pltpu.SemaphoreType.DMA((n,)), pltpu.SemaphoreType.DMA((n,))],
    )(*gs)


def _row_block(rows, cols, itemsize):
    best = 16
    for rb in range(16, rows + 1, 16):
        if rows % rb == 0 and rb * cols * itemsize <= 2 * 1024 * 1024:
            best = rb
    assert rows % best == 0, (rows, cols)
    return best


def _comm_block(rows):
    return 1024 if rows % 1024 == 0 else rows


def sum_slots(buf, name):
    n, r, c = buf.shape
    rb = _comm_block(r)

    def body(b_ref, o_ref):
        acc = b_ref[0].astype(F32)
        for s in range(1, n):
            acc = acc + b_ref[s].astype(F32)
        o_ref[...] = acc

    return pl.pallas_call(
        body, name=name, grid=(r // rb,), in_specs=[pl.BlockSpec((n, rb, c), lambda i: (0, i, 0))],
        out_specs=pl.BlockSpec((rb, c), lambda i: (i, 0)), out_shape=jax.ShapeDtypeStruct((r, c), F32),
        compiler_params=_params(("parallel",)),
    )(buf)


def add_pair(a, b, out_dtype, name):
    shape = a.shape
    a2, b2 = a.reshape(-1, shape[-1]), b.reshape(-1, shape[-1])
    r, c = a2.shape
    rb = _comm_block(r)

    def body(a_ref, b_ref, o_ref):
        o_ref[...] = (a_ref[...].astype(F32) + b_ref[...].astype(F32)).astype(out_dtype)

    out = pl.pallas_call(
        body, name=name, grid=(r // rb,), in_specs=[pl.BlockSpec((rb, c), lambda i: (i, 0))] * 2,
        out_specs=pl.BlockSpec((rb, c), lambda i: (i, 0)), out_shape=jax.ShapeDtypeStruct((r, c), out_dtype),
        compiler_params=_params(("parallel",)),
    )(a2, b2)
    return out.reshape(shape)


def adamw(w, g, m, v):
    shape = w.shape
    cols = shape[-1]
    rows = math.prod(shape[:-1])
    if rows * cols <= 256 * 1024:
        rb, cb = rows, cols
    else:
        rb = max(r for r in range(8, 2049, 8) if rows % r == 0)
        cb = cols if rb * cols * 4 <= 2 * 1024 * 1024 else 256
    assert rows % rb == 0 and cols % cb == 0, shape

    def body(w_ref, g_ref, m_ref, v_ref, d_ref, nm_ref, nv_ref):
        g_ = g_ref[...]
        nm = ADAM_B1 * m_ref[...] + (1.0 - ADAM_B1) * g_
        nv = ADAM_B2 * v_ref[...] + (1.0 - ADAM_B2) * (g_ * g_)
        m_hat = nm / (1.0 - ADAM_B1 ** ADAM_STEP)
        v_hat = nv / (1.0 - ADAM_B2 ** ADAM_STEP)
        d_ref[...] = -ADAM_LR * (m_hat / (jnp.sqrt(v_hat) + ADAM_EPS) + ADAM_WD * w_ref[...])
        nm_ref[...] = nm
        nv_ref[...] = nv

    spec = pl.BlockSpec((rb, cb), lambda i, j: (i, j))
    outs = pl.pallas_call(
        body, name="adamw", grid=(rows // rb, cols // cb), in_specs=[spec] * 4, out_specs=[spec] * 3,
        out_shape=[jax.ShapeDtypeStruct((rows, cols), F32)] * 3, compiler_params=_params(("parallel", "parallel")),
    )(*(a.reshape(rows, cols) for a in (w, g, m, v)))
    return tuple(o.reshape(shape) for o in outs)


def _pack(arrays, dtype, row_multiple):
    flat = jnp.concatenate([a.astype(dtype).reshape(-1) for a in arrays])
    per = LANES * row_multiple
    total = -(-flat.shape[0] // per) * per
    return jnp.pad(flat, (0, total - flat.shape[0])).reshape(total // LANES, LANES)


def _unpack(buf, shapes):
    flat = buf.reshape(-1)
    out, off = [], 0
    for s in shapes:
        n = math.prod(s)
        out.append(flat[off:off + n].reshape(s))
        off += n
    return out


def _input_weights(blocks):
    c0, c1, c2, c3 = (blocks[..., k, :, :] for k in range(N_CHIPS))
    pad = lambda n: jnp.zeros(c0.shape[:-1] + (n,), blocks.dtype)
    w_br = jnp.concatenate([c1[..., 376:1400], c0[..., 0:896], pad(64), c0[..., 896:928], pad(32), c0[..., 928:], c1[..., 0:376]], axis=-1)
    return w_br, jnp.concatenate([c1[..., 1400:], c2, c3], axis=-1)


def _input_weights_inverse(dw_br, dw_gl):
    c0 = jnp.concatenate([dw_br[..., 1024:1920], dw_br[..., 1984:2016], dw_br[..., 2048:2952]], axis=-1)
    c1 = jnp.concatenate([dw_br[..., 2952:ZB], dw_br[..., 0:1024], dw_gl[..., 0:432]], axis=-1)
    return jnp.stack([c0, c1, dw_gl[..., 432:2264], dw_gl[..., 2264:]], axis=-3)


def _block_diag(pw):
    zeros = lambda n: jnp.zeros(pw.shape[:-3] + (64, n), pw.dtype)
    rows = [jnp.concatenate([zeros(64 * g), pw[..., g, :, :], zeros(64 * (3 - g))], axis=-1) for g in range(4)]
    return jnp.concatenate(rows, axis=-2)


def _block_diag_inverse(d):
    return jnp.stack([d[..., 64 * g:64 * (g + 1), 64 * g:64 * (g + 1)] for g in range(4)], axis=-3)


def _pad_rows(a, n):
    return jnp.pad(a, ((0, n - a.shape[0]), (0, 0)))


def _rope_tables(rows):
    inv = 1.0 / (ROPE_THETA ** (jnp.arange(0, QK_ROPE, 2, dtype=F32) / QK_ROPE))
    ang = jnp.arange(rows, dtype=F32)[:, None] * inv[None, :]
    cos, sin = jnp.cos(ang), jnp.sin(ang)
    one, zero = jnp.ones((rows, 1), F32), jnp.zeros((rows, 1), F32)
    rep = lambda a, n: jnp.broadcast_to(a, (rows, n))
    c = jnp.concatenate([rep(one, 64), cos, cos, rep(one, 32)], axis=1)
    s1 = jnp.concatenate([rep(zero, 64), -sin, rep(zero, 48)], axis=1)
    s2 = jnp.concatenate([rep(zero, 80), sin, rep(zero, 32)], axis=1)
    return jnp.concatenate([c, s1, s2], axis=1)


def _misc_block(parts):
    lead = parts["w_uq"].shape[:-2]
    pad_last = lambda a, n: jnp.pad(a, [(0, 0)] * (a.ndim - 1) + [(0, n - a.shape[-1])])
    uq = pad_last(parts["w_uq"].reshape(lead + (256, 2, QK_NOPE + QK_ROPE)), HEAD_PAD).reshape(lead + (256, 256))
    kv = parts["w_ukv"].reshape(lead + (128, 2, QK_NOPE + V_DIM))
    keys = pad_last(kv[..., :QK_NOPE], HEAD_PAD).reshape(lead + (128, 256))
    values = pad_last(kv[..., QK_NOPE:].reshape(lead + (128, 2 * V_DIM)), 256)
    wo = jnp.swapaxes(parts["w_o"].reshape(lead + (256, N_CHIPS, 256)), -3, -2).reshape(lead + (D_MODEL, 256))
    gap = jnp.zeros(lead + (M_UQ - M_SC - 256, 256), uq.dtype)
    return jnp.concatenate([parts["w_out_mla"], parts["w_out_pool"], parts["w_out_conf"], parts["w_out_sc"], gap, uq, keys, values, wo],
                           axis=-2)


def _misc_unblock(block):
    lead = block.shape[:-2]
    rows = lambda lo, n: block[..., lo:lo + n, :]
    uq = rows(M_UQ, 256).reshape(lead + (256, 2, HEAD_PAD))[..., :QK_NOPE + QK_ROPE].reshape(lead + (256, 2 * (QK_NOPE + QK_ROPE)))
    keys = rows(M_UKVK, 128).reshape(lead + (128, 2, HEAD_PAD))[..., :QK_NOPE]
    values = rows(M_UKVV, 128)[..., :2 * V_DIM].reshape(lead + (128, 2, V_DIM))
    wo = jnp.swapaxes(rows(M_WO, D_MODEL).reshape(lead + (N_CHIPS, 256, 256)), -3, -2).reshape(lead + (256, D_MODEL))
    return dict(w_out_mla=rows(M_MLA, 512), w_out_pool=rows(M_POOL, 256), w_out_conf=rows(M_CONF, 256), w_out_sc=rows(M_SC, 256), w_uq=uq,
                w_ukv=jnp.concatenate([keys, values], axis=-1).reshape(lead + (128, 256)), w_o=wo)


def _to_chip_blocks(name, a):
    if name == "w_o":
        return a.reshape(a.shape[:-2] + (N_CHIPS, a.shape[-2] // N_CHIPS, a.shape[-1]))
    return jnp.swapaxes(a.reshape(a.shape[:-1] + (N_CHIPS, a.shape[-1] // N_CHIPS)), -3, -2)


def _from_chip_blocks(name, b):
    if name == "w_o":
        return b.reshape(b.shape[:-3] + (N_CHIPS * b.shape[-2], b.shape[-1]))
    s = jnp.swapaxes(b, -3, -2)
    return s.reshape(s.shape[:-2] + (N_CHIPS * s.shape[-1],))


LARGE = ("w_in",) + MISC


def gather_small(shards):
    small = chip_exchange(_pack([shards[n] for n, _, _ in SHARDED_SMALL], F32, 8), True, "gather_small_ici")
    per_chip = [_unpack(small[k], [s for _, s, _ in SHARDED_SMALL]) for k in range(N_CHIPS)]
    return {name: jnp.concatenate([per_chip[k][idx] for k in range(N_CHIPS)], axis=axis) for idx, (name, _, axis) in enumerate(SHARDED_SMALL)}


class LocalWeights:
    def __init__(self, full):
        self.w_in = _to_chip_blocks("w_in", full["w_in"])
        self.misc = _misc_block({n: _to_chip_blocks(n, full[n]) for n in MISC}).astype(BF16)
        self.grads = [None] * DEPTH

    def layer(self, i):
        return self.w_in[i], (self.misc, i)

    def gather_with_attention(self, i):
        return None

    def gathered(self, dsts):
        pass

    def exchange_with_attention(self):
        return None

    def exchanged(self, dsts):
        pass

    def swap_with_postnorm(self):
        return None

    def swapped(self, rs):
        pass

    def put_grads(self, i, w_in, misc):
        self.grads[i] = (w_in, misc)

    def reduced(self):
        out = {n: _from_chip_blocks(n, b) for n, b in _misc_unblock(jnp.stack([m for _, m in self.grads])).items()}
        out["w_in"] = _from_chip_blocks("w_in", jnp.stack([w for w, _ in self.grads]))
        return out


class MeshWeights:
    def __init__(self, shards, c, chip):
        self.c, self.chip = c, chip
        self.srcs = [shards["w_in"].astype(BF16), _misc_block({n: shards[n] for n in MISC}).astype(BF16)]
        dsts = [lax.empty((DEPTH, N_CHIPS) + s.shape[1:], BF16) for s in self.srcs]
        self.dsts = gather_layer(self.srcs, dsts, 0, "gather_layer")
        self.landed = [lax.empty((DEPTH, N_CHIPS, s.shape[1] // 2, s.shape[2]), BF16) for s in self.srcs]
        self.pending = self.to_swap = None

    def layer(self, i):
        if i > 0:
            return self.dsts[0][i], (self.dsts[1], i)
        own = (jnp.arange(N_CHIPS) == self.chip)[:, None, None]
        w_in, misc = (jnp.where(own, s[0][None], d[0]) for s, d in zip(self.srcs, self.dsts))
        return w_in, (misc[None], 0)

    def gather_with_attention(self, i):
        return (self.srcs, self.dsts, i + 1) if i + 1 < DEPTH else None

    def gathered(self, dsts):
        if dsts:
            self.dsts = dsts

    def exchange_with_attention(self):
        return None if self.pending is None else (self.pending[0], self.landed, self.pending[1])

    def exchanged(self, dsts):
        if dsts:
            self.landed, self.pending = dsts, None

    def put_grads(self, i, w_in, misc):
        self.to_swap = ([w_in.astype(BF16), misc.astype(BF16)], i)

    def swap_with_postnorm(self):
        return None if self.to_swap is None else self.to_swap[0]

    def swapped(self, rs):
        if rs:
            ps, i = self.to_swap
            self.pending = ([add_row_half(p, r, self.c, "reduce_pair_%d" % a) for a, (p, r) in enumerate(zip(ps, rs))], i)
            self.to_swap = None

    def reduced(self):
        self.swapped(swap_row_halves(self.to_swap[0], "reduce_swap"))
        landed = exchange_layer(self.pending[0], self.landed, self.pending[1], "reduce_exchange")
        gs = [sum_row_halves(l, self.c, "reduce_sum_%d" % a) for a, l in enumerate(landed)]
        g_in, g_misc = share_row_halves(gs, "reduce_share")
        out = {"w_in": g_in}
        out.update(_misc_unblock(g_misc))
        return out


def reduce_small(grads, chip):
    names = [n for n, _ in REPLICATED] + [n for n, _, _ in SHARDED_SMALL]
    buf = _pack([grads[n] for n in names], F32, 8)
    chip_sum = add_pair(buf, sibling_swap(buf, "reduce_small_d2d"), F32, "reduce_small_pair")
    total = sum_slots(chip_exchange(chip_sum, True, "reduce_small_ici"), "reduce_small_sum")
    out = dict(zip(names, _unpack(total, [grads[n].shape for n in names])))
    for name, shape, axis in SHARDED_SMALL:
        out[name] = lax.dynamic_slice_in_dim(out[name], chip * shape[axis], shape[axis], axis)
    return out


def _prepare_small(w):
    row = lambda a: a[:, None, :]
    conf_vec = jnp.concatenate([row(w["conf_dw_b"]), row(w["conf_ln_g"]), row(w["conf_ln_b"]), jnp.zeros((DEPTH, 5, 256), F32)], axis=1)
    return dict(
        gpre=row(w["pre_norm_g"]), bias=row(w["gate_bias"]), pwbd=_block_diag(w["pool_w"]).astype(BF16), pscale=row(w["pool_scale"]),
        gq=row(w["q_norm_g"]), gkv=row(w["kv_norm_g"]), conf_w=jnp.pad(w["conf_dw_w"].astype(F32), ((0, 0), (0, 32 - CONF_K), (0, 0))),
        conf_vec=conf_vec, sc_w=jnp.pad(w["sc_dw_w"].astype(F32), ((0, 0), (0, 8 - SC_K), (0, 0))), gpost=row(w["post_norm_g"]))


def _prepare_layer(w_in_blocks, misc):
    w_br, w_gl = _input_weights(w_in_blocks)
    one = lambda a: a.astype(BF16)[None]
    return dict(w_br=one(w_br), w_gl=one(w_gl), misc=misc)


def local_step(x, target, w, large):
    seq = x.shape[0]
    length = N_META + seq
    rows = -(-length // ROW_TILE) * ROW_TILE
    bt = _big_tile(rows)
    hres = _pad_rows(jnp.concatenate([w["meta_tokens"].astype(F32), x], axis=0), rows)
    tgt = jnp.pad(target, ((N_META, rows - length), (0, 0)))
    rope = _rope_tables(rows)
    sw = _prepare_small(w)

    saved = []
    for i in range(DEPTH):
        lw = _prepare_layer(*large.layer(i))
        z_br, hb = prenorm_project(hres, sw["gpre"], lw["w_br"], i)
        z_gl = matmul(hb, lw["w_gl"], "nn", BF16, bt, 1024, D_MODEL, "project_gates", b_layer=0)
        ua, uc, ud, q, k, v = branches_fwd(z_br, rope, sw["pwbd"], sw["pscale"], sw["gq"], sw["gkv"], lw["misc"], sw["conf_w"],
                                           sw["conf_vec"], sw["sc_w"], i)
        o_att, lse, dsts = attention_fwd(q, k, v, large.gather_with_attention(i))
        large.gathered(dsts)
        ub, mb, o, hnew = merge_fwd(ua, o_att, uc, ud, z_br, z_gl, sw["bias"], lw["misc"], sw["gpost"], hres, i)
        saved.append(dict(lw=lw, hres=hres, hb=hb, z_br=z_br, z_gl=z_gl, ua=ua, ub=ub, uc=uc, ud=ud, q=q, k=k, v=v, o_att=o_att,
                          lse=lse, mb=mb, o=o))
        hres = hnew

    dh, total = loss_head(hres, tgt, seq)

    g = {n: [None] * DEPTH for n in ("gpre", "bias", "pwbd", "pscale", "gq", "gkv", "conf_w", "conf_vec", "sc_w", "gpost")}
    for i in reversed(range(DEPTH)):
        s = saved[i]
        lw = s["lw"]
        dm, dwo, g["gpost"][i], rs = postnorm_bwd(dh, s["o"], s["mb"], lw["misc"], sw["gpost"], i, large.swap_with_postnorm())
        large.swapped(rs)
        dz_br = lax.empty((rows, ZB), BF16)
        dua, do, duc, dud, dz_gl, dwout, g["bias"][i], dz_br, delta = merge_bwd(
            dm, s["ua"], s["ub"], s["uc"], s["ud"], s["z_gl"], sw["bias"], lw["misc"], s["o_att"], s["z_br"], dz_br, i)
        dz_br, g["pwbd"][i], g["pscale"][i], g["sc_w"][i] = pool_shortconv_bwd(s["z_br"], dua, dud, sw["pwbd"], sw["pscale"], sw["sc_w"],
                                                                              dz_br, i)
        dc, dz_br, g["conf_vec"][i] = conformer_bwd_tail(s["z_br"], duc, sw["conf_w"], sw["conf_vec"], dz_br, i)
        dz_br, g["conf_w"][i] = conformer_bwd_conv(s["z_br"], dc, sw["conf_w"], dz_br, i)
        dq, dk, dv, dsts = attention_bwd(s["q"], s["k"], s["v"], do, s["lse"], delta, large.exchange_with_attention())
        large.exchanged(dsts)
        dz_br, dwup, g["gq"][i], g["gkv"][i] = mla_prep_bwd(dq, dk, dv, s["z_br"], rope, sw["gq"], sw["gkv"], lw["misc"], dz_br, i)
        dw_br = matmul(s["hb"], dz_br, "tn", BF16, D_MODEL, ZB // 2, bt, "grad_w_branch")
        dw_gl = matmul(s["hb"], dz_gl, "tn", BF16, D_MODEL, 1024, bt, "grad_w_gates")
        dh_gl = matmul(dz_gl, lw["w_gl"], "nt", F32, bt, D_MODEL, 1024, "grad_h_gates", b_layer=0)
        dh, g["gpre"][i] = prenorm_bwd(dz_br, lw["w_br"], dh_gl, s["hres"], sw["gpre"], dh, i)
        gap = jnp.zeros((N_CHIPS, M_UQ - M_SC - 256, 256), F32)
        large.put_grads(i, _input_weights_inverse(dw_br, dw_gl), jnp.concatenate([dwout, gap, dwup, dwo], axis=1))

    g = {n: jnp.stack(parts) for n, parts in g.items()}
    grads = dict(
        meta_tokens=dh[:N_META], pre_norm_g=g["gpre"][:, 0], gate_bias=g["bias"][:, 0], pool_w=_block_diag_inverse(g["pwbd"]),
        pool_scale=g["pscale"][:, 0], q_norm_g=g["gq"][:, 0], kv_norm_g=g["gkv"][:, 0], conf_dw_w=g["conf_w"][:, :CONF_K],
        conf_dw_b=g["conf_vec"][:, 2], conf_ln_g=g["conf_vec"][:, 0], conf_ln_b=g["conf_vec"][:, 1], sc_dw_w=g["sc_w"][:, :SC_K],
        post_norm_g=g["gpost"][:, 0])
    return total[0, 0], dh[N_META:length], grads


def kernel(x, meta_tokens, pre_norm_g, w_in, gate_bias, pool_w, pool_scale, w_out_pool, q_norm_g, w_uq, kv_norm_g, w_ukv, w_out_mla, conf_dw_w, conf_dw_b, conf_ln_g, conf_ln_b, w_out_conf, sc_dw_w, w_out_sc, w_o, post_norm_g, loss_target, m_meta_tokens, m_pre_norm_g, m_w_in, m_gate_bias, m_pool_w, m_pool_scale, m_w_out_pool, m_q_norm_g, m_w_uq, m_kv_norm_g, m_w_ukv, m_w_out_mla, m_conf_dw_w, m_conf_dw_b, m_conf_ln_g, m_conf_ln_b, m_w_out_conf, m_sc_dw_w, m_w_out_sc, m_w_o, m_post_norm_g, v_meta_tokens, v_pre_norm_g, v_w_in, v_gate_bias, v_pool_w, v_pool_scale, v_w_out_pool, v_q_norm_g, v_w_uq, v_kv_norm_g, v_w_ukv, v_w_out_mla, v_conf_dw_w, v_conf_dw_b, v_conf_ln_g, v_conf_ln_b, v_w_out_conf, v_sc_dw_w, v_w_out_sc, v_w_o, v_post_norm_g):
    args = locals()
    weights = {n: args[n] for n in WEIGHT_ORDER}
    c = lax.axis_index("c")
    chip = 2 * lax.axis_index("x") + lax.axis_index("y")

    small = {n: weights[n] for n, _ in REPLICATED}
    small.update(gather_small(weights))
    large = MeshWeights(weights, c, chip)
    total, dx, grads = local_step(x[0], loss_target[0], small, large)
    loss = lax.psum(total * (0.5 / D_MODEL), ("x", "y", "c"))

    reduced = large.reduced()
    reduced.update(reduce_small(grads, chip))

    flip = lambda a: jnp.swapaxes(a, 1, 2)
    deltas, new_m, new_v = [], [], []
    for n in WEIGHT_ORDER:
        operands = (weights[n], reduced[n], args["m_" + n], args["v_" + n])
        if n == "w_in":
            operands = (flip(operands[0]), lax.optimization_barrier(flip(operands[1])), flip(operands[2]), flip(operands[3]))
            reduced[n] = flip(operands[1])
        d, nm, nv = adamw(*operands)
        if n == "w_in":
            d, nm, nv = flip(d), flip(nm), flip(nv)
        deltas.append(d)
        new_m.append(nm)
        new_v.append(nv)
    return (loss, dx[None], *[reduced[n] for n in WEIGHT_ORDER], *deltas, *new_m, *new_v)
```

```python
import functools
import math

import jax
import jax.numpy as jnp
from jax import lax
from jax.experimental import pallas as pl
from jax.experimental.pallas import tpu as pltpu

F32 = jnp.float32
BF16 = jnp.bfloat16

D_MODEL = 1024
DEPTH = 4
N_META = 16
EPS = 1e-6
HEADS = 8
QK_NOPE = 64
QK_ROPE = 32
V_DIM = 64
HEAD_PAD = 128
ROPE_THETA = 10000.0
Q_SCALE = (QK_NOPE + QK_ROPE) ** -0.5
CONF_K = 31
SC_K = 3
IN_W = 7328
N_CHIPS = 4

ZB = 3328
ZG = 4096
BG, C2, XV, SG, PV, PG, CQ, CKV, KR, MG, CA, CGT, CG = (0, 256, 512, 768, 1024, 1280, 1536, 1792, 1920, 2048, 2560, 2816, 3072)

KEY_GROUP = 4
ROW_TILE = 384
HALO = 32
LANES = 128
VMEM_LIMIT = 56 * 1024 * 1024

ADAM_LR = 0.001
ADAM_B1 = 0.9
ADAM_B2 = 0.999
ADAM_EPS = 1e-08
ADAM_WD = 0.01
ADAM_STEP = 10

MESH = pl.DeviceIdType.MESH
ANY = pl.BlockSpec(memory_space=pl.ANY)

MISC = ("w_out_mla", "w_out_pool", "w_out_conf", "w_out_sc", "w_uq", "w_ukv", "w_o")
M_MLA, M_POOL, M_CONF, M_SC, M_UQ, M_UKVK, M_UKVV, M_WO, MISC_ROWS = 0, 512, 768, 1024, 1536, 1792, 1920, 2048, 3072
SHARDED_SMALL = (
    ("meta_tokens", (N_META, 256), 1),
    ("conf_dw_w", (DEPTH, CONF_K, 64), 2),
    ("sc_dw_w", (DEPTH, SC_K, 64), 2),
)
REPLICATED = (
    ("pre_norm_g", (DEPTH, D_MODEL)),
    ("gate_bias", (DEPTH, 4 * D_MODEL)),
    ("pool_w", (DEPTH, 4, 64, 64)),
    ("pool_scale", (DEPTH, 256)),
    ("q_norm_g", (DEPTH, 256)),
    ("kv_norm_g", (DEPTH, 128)),
    ("conf_dw_b", (DEPTH, 256)),
    ("conf_ln_g", (DEPTH, 256)),
    ("conf_ln_b", (DEPTH, 256)),
    ("post_norm_g", (DEPTH, D_MODEL)),
)
WEIGHT_ORDER = ("meta_tokens", "pre_norm_g", "w_in", "gate_bias", "pool_w", "pool_scale", "w_out_pool", "q_norm_g", "w_uq",
                "kv_norm_g", "w_ukv", "w_out_mla", "conf_dw_w", "conf_dw_b", "conf_ln_g", "conf_ln_b", "w_out_conf", "sc_dw_w",
                "w_out_sc", "w_o", "post_norm_g")


def _dot(a, b):
    return lax.dot_general(a, b, (((1,), (0,)), ((), ())), preferred_element_type=F32)


def _dot_nt(a, b):
    return lax.dot_general(a, b, (((1,), (1,)), ((), ())), preferred_element_type=F32)


def _dot_tn(a, b):
    return lax.dot_general(a, b, (((0,), (0,)), ((), ())), preferred_element_type=F32)


def _sigmoid(x):
    return jax.nn.sigmoid(x)


def _silu(x):
    return x * _sigmoid(x)


def _silu_grad(x):
    s = _sigmoid(x)
    return s * (1.0 + x * (1.0 - s))


def _rms(x, g):
    return x * lax.rsqrt(jnp.mean(x * x, axis=-1, keepdims=True) + EPS) * g


def _sh(x, d):
    return x if d == 0 else pltpu.roll(x, d, 0)


def _ash(x, d):
    return x if d == 0 else pltpu.roll(x, x.shape[0] - d, 0)


def _lanes8(t):
    return jnp.concatenate([t] * HEADS, axis=1)


def _pool_window_sums(v, shift):
    a2 = v + shift(v, 1)
    a4 = a2 + shift(a2, 2)
    a8 = a4 + shift(a4, 4)
    a16 = a8 + shift(a8, 8)
    lane = lax.broadcasted_iota(jnp.int32, v.shape, 1)
    return jnp.where(lane < 64, a2, jnp.where(lane < 128, a4, jnp.where(lane < 192, a8, a16)))


def _pool_counts(first_row, rows):
    pos = first_row + lax.broadcasted_iota(jnp.int32, (rows, 256), 0)
    lane = lax.broadcasted_iota(jnp.int32, (rows, 256), 1)
    width = jnp.where(lane < 64, 2, jnp.where(lane < 128, 4, jnp.where(lane < 192, 8, 16)))
    return jnp.maximum(jnp.minimum(pos + 1, width), 1).astype(F32)


def _params(sem=None):
    return pltpu.CompilerParams(dimension_semantics=sem, vmem_limit_bytes=VMEM_LIMIT)


def _tile_specs(t, n_halo_blocks, li=0):
    per = t // HALO

    def layer(shape, idx=li):
        return pl.BlockSpec((None,) + tuple(shape), lambda i: (idx,) + (0,) * len(shape))

    def cur(c, cb=0):
        return pl.BlockSpec((t, c), lambda i: (i, cb))

    def prev(c, cb=0):
        return pl.BlockSpec((HALO, c), lambda i: (jnp.maximum(i * per - 1, 0), cb))

    def nxt(c, cb=0):
        return pl.BlockSpec((HALO, c), lambda i: (jnp.minimum((i + 1) * per, n_halo_blocks - 1), cb))

    def full(shape):
        return pl.BlockSpec(shape, lambda i: (0,) * len(shape))

    return cur, prev, nxt, full, layer


def _big_tile(rows):
    return rows // 3 if rows % (3 * LANES) == 0 else ROW_TILE


def matmul(a, b, mode, out_dtype, tm, tn, tk, name, b_layer=None):
    bs = b.shape if b_layer is None else b.shape[1:]
    lead = () if b_layer is None else (None,)
    pick = (lambda *ix: ix) if b_layer is None else (lambda *ix: (b_layer,) + ix)
    if mode == "nn":
        (m, k), n = a.shape, bs[1]
        a_spec = pl.BlockSpec((tm, tk), lambda i, j, kk: (i, kk))
        b_spec = pl.BlockSpec(lead + (tk, tn), lambda i, j, kk: pick(kk, j))
        dot = _dot
    elif mode == "nt":
        (m, k), n = a.shape, bs[0]
        a_spec = pl.BlockSpec((tm, tk), lambda i, j, kk: (i, kk))
        b_spec = pl.BlockSpec(lead + (tn, tk), lambda i, j, kk: pick(j, kk))
        dot = _dot_nt
    else:
        (k, m), n = a.shape, bs[1]
        a_spec = pl.BlockSpec((tk, tm), lambda i, j, kk: (kk, i))
        b_spec = pl.BlockSpec(lead + (tk, tn), lambda i, j, kk: pick(kk, j))
        dot = _dot_tn
    assert m % tm == 0 and n % tn == 0 and k % tk == 0, (a.shape, bs, tm, tn, tk)
    nk = k // tk

    def body(a_ref, b_ref, o_ref, acc_ref):
        kk = pl.program_id(2)

        @pl.when(kk == 0)
        def _():
            acc_ref[...] = jnp.zeros_like(acc_ref)

        acc_ref[...] += dot(a_ref[...], b_ref[...])

        @pl.when(kk == nk - 1)
        def _():
            o_ref[...] = acc_ref[...].astype(out_dtype)

    return pl.pallas_call(
        body, name=name, grid=(m // tm, n // tn, nk), in_specs=[a_spec, b_spec],
        out_specs=pl.BlockSpec((tm, tn), lambda i, j, kk: (i, j)), out_shape=jax.ShapeDtypeStruct((m, n), out_dtype),
        scratch_shapes=[pltpu.VMEM((tm, tn), F32)], compiler_params=_params(("parallel", "parallel", "arbitrary")),
    )(a, b)


def prenorm_project(hres, g, w, li):
    rows, d = hres.shape
    n = w.shape[2]
    tm, tn = _big_tile(rows), n // 2

    def body(x_ref, g_ref, w_ref, z_ref, hb_ref, hbt_ref):
        @pl.when(pl.program_id(1) == 0)
        def _():
            h = _rms(x_ref[...], g_ref[...])
            hb_ref[...] = h.astype(BF16)
            hbt_ref[...] = h.T.astype(BF16)

        z_ref[...] = _dot(hb_ref[...], w_ref[...]).astype(BF16)

    return pl.pallas_call(
        body, name="prenorm_project", grid=(rows // tm, n // tn),
        in_specs=[pl.BlockSpec((tm, d), lambda i, j: (i, 0)), pl.BlockSpec((None, 1, d), lambda i, j: (li, 0, 0)),
                  pl.BlockSpec((None, d, tn), lambda i, j: (0, 0, j))],
        out_specs=[pl.BlockSpec((tm, tn), lambda i, j: (i, j)), pl.BlockSpec((tm, d), lambda i, j: (i, 0)),
                   pl.BlockSpec((d, tm), lambda i, j: (0, i))],
        out_shape=[jax.ShapeDtypeStruct((rows, n), BF16), jax.ShapeDtypeStruct((rows, d), BF16), jax.ShapeDtypeStruct((d, rows), BF16)],
        compiler_params=_params(("parallel", "arbitrary")),
    )(hres, g, w)


def _rope(q, c, s1, s2, width):
    return q * c + pltpu.roll(q, width - 16, 1) * s1 + pltpu.roll(q, 16, 1) * s2


def _rope_transposed(dq, c, s1, s2, width):
    return dq * c + pltpu.roll(dq * s1, 16, 1) + pltpu.roll(dq * s2, width - 16, 1)


def _conf_conv(g1, w_ref):
    acc = jnp.zeros_like(g1)
    for k in range(CONF_K):
        acc = acc + w_ref[k:k + 1, :] * _sh(g1, CONF_K - 1 - k)
    return acc


def _conf_tail(c, cg, lg, lb):
    mu = jnp.mean(c, axis=-1, keepdims=True)
    xc = c - mu
    var = jnp.mean(xc * xc, axis=-1, keepdims=True)
    n = xc * lax.rsqrt(var + EPS) * lg + lb
    return _silu(n) * _silu(cg)


def _misc_spec(misc, row0, rows):
    assert row0 % rows == 0
    return pl.BlockSpec((None, N_CHIPS, rows, 256), lambda i: (misc[1], 0, row0 // rows, 0))


def _chip_columns(x, w_ref, row0, rows, lanes=256):
    return jnp.concatenate([_dot(x, w_ref[k, row0:row0 + rows, 0:lanes]) for k in range(N_CHIPS)], axis=1)


def branches_fwd(z_br, rope, pwbd, pscale, gq, gkv, misc, conf_w, conf_vec, sc_w, li):
    rows = z_br.shape[0]
    t = ROW_TILE
    cur, prev, _, _, layer = _tile_specs(t, rows // HALO, li)

    def body(zc_ref, zp_ref, rope_ref, pw_ref, ps_ref, gq_ref, gkv_ref, up_ref, cw_ref, cv_ref, sw_ref,
             ua_ref, uc_ref, ud_ref, q_ref, k_ref, v_ref):
        i = pl.program_id(0)
        zp = jnp.where(i == 0, jnp.zeros(zp_ref.shape, zp_ref.dtype), zp_ref[...])

        def ext(lo, w=256):
            return jnp.concatenate([zp[:, lo:lo + w], zc_ref[:, lo:lo + w]], axis=0).astype(F32)

        def col(lo, w=256):
            return zc_ref[:, lo:lo + w].astype(F32)

        v = ext(PV)
        p = (_pool_window_sums(v, _sh) / _pool_counts(i * t - HALO, t + HALO) - v)[HALO:]
        ya = _dot(p.astype(BF16), pw_ref[...]) * ps_ref[...]
        ua_ref[...] = (ya * _silu(col(PG))).astype(BF16)

        g1 = ext(CA) * _sigmoid(ext(CGT))
        c = _conf_conv(g1, cw_ref)[HALO:] + cv_ref[0:1, :]
        uc_ref[...] = _conf_tail(c, col(CG), cv_ref[1:2, :], cv_ref[2:3, :]).astype(BF16)

        e = ext(C2) * ext(XV)
        f = jnp.zeros_like(e)
        for k in range(SC_K):
            f = f + sw_ref[k:k + 1, :] * _sh(e, SC_K - 1 - k)
        ud_ref[...] = (col(BG) * f[HALO:] * _silu(col(SG))).astype(BF16)

        cth, s1, s2 = rope_ref[:, 0:128], rope_ref[:, 128:256], rope_ref[:, 256:384]
        qn = _rms(col(CQ), gq_ref[...]).astype(BF16)
        q = _chip_columns(qn, up_ref, 0, 256)
        w8 = HEADS * HEAD_PAD
        q_ref[...] = (_rope(q, _lanes8(cth), _lanes8(s1), _lanes8(s2), w8) * Q_SCALE).astype(BF16)
        kvn = _rms(col(CKV, 128), gkv_ref[...]).astype(BF16)
        kr = _rope(col(KR, 128), cth, s1, s2, HEAD_PAD)
        k_ref[...] = (_chip_columns(kvn, up_ref, M_UKVK - M_UQ, 128) + _lanes8(kr)).astype(BF16)
        v_ref[...] = _chip_columns(kvn, up_ref, M_UKVV - M_UQ, 128, 2 * V_DIM).astype(BF16)

    outs = [jax.ShapeDtypeStruct((rows, 256), BF16)] * 3 + [jax.ShapeDtypeStruct((rows, 1024), BF16)] * 2 + [
        jax.ShapeDtypeStruct((rows, 512), BF16)]
    return pl.pallas_call(
        body, name="branches_fwd", grid=(rows // t,),
        in_specs=[cur(ZB), prev(ZB), cur(384), layer((256, 256)), layer((1, 256)), layer((1, 256)), layer((1, 128)),
                  _misc_spec(misc, M_UQ, M_WO - M_UQ), layer((32, 256)), layer((8, 256)), layer((8, 256))],
        out_specs=[cur(256), cur(256), cur(256), cur(1024), cur(1024), cur(512)], out_shape=outs,
        compiler_params=_params(("parallel",)),
    )(z_br, z_br, rope, pwbd, pscale, gq, gkv, misc[0], conf_w, conf_vec, sc_w)


def _head_lane_mask(h):
    lane = lax.broadcasted_iota(jnp.int32, (1, 2 * V_DIM), 1)
    return (lane >= V_DIM * h) & (lane < V_DIM * (h + 1))


def attention_fwd(q, k, v, gather=None):
    rows = q.shape[0]
    tq = ROW_TILE
    nq = rows // tq
    n = 0 if gather is None else len(gather[0])

    def body(*refs):
        if n:
            start, finish = _gather_ops(refs[3:3 + n], refs[5 + 2 * n:5 + 3 * n], refs[5 + 3 * n:], gather[2], True)
            pl.when((pl.program_id(0) == 0) & (pl.program_id(1) == 0))(start)
        compute(*refs[:3], *refs[3 + 2 * n:5 + 2 * n])
        if n:
            pl.when((pl.program_id(0) == HEADS // 2 - 1) & (pl.program_id(1) == nq - 1))(finish)

    def compute(q_ref, k_ref, v_ref, o_ref, lse_ref):
        i = pl.program_id(1)

        def head_step(h, tile, n_tiles, carry, masked):
            m, l, acc = carry
            width = n_tiles * tq
            r0 = pl.multiple_of(tile * tq, tq)
            kh = k_ref[pl.ds(r0, width), HEAD_PAD * h:HEAD_PAD * (h + 1)]
            vh = jnp.where(_head_lane_mask(h), v_ref[pl.ds(r0, width), :], jnp.zeros((), BF16))
            s = _dot_nt(q_ref[:, HEAD_PAD * h:HEAD_PAD * (h + 1)], kh)
            if masked:
                row = lax.broadcasted_iota(jnp.int32, (tq, width), 0)
                colm = lax.broadcasted_iota(jnp.int32, (tq, width), 1)
                s = jnp.where(colm <= row + (width - tq), s, -1e30)
            m2 = jnp.maximum(m, jnp.max(s, axis=-1, keepdims=True))
            alpha = jnp.exp(m - m2)
            pr = jnp.exp(s - m2)
            return m2, alpha * l + jnp.sum(pr, axis=-1, keepdims=True), alpha * acc + _dot(pr.astype(BF16), vh)

        def step(tile, n_tiles, carry, masked):
            return tuple(head_step(h, tile, n_tiles, carry[h], masked) for h in range(2))

        init = (jnp.full((tq, 1), -1e30, F32), jnp.zeros((tq, 1), F32), jnp.zeros((tq, 2 * V_DIM), F32))
        group = min(KEY_GROUP, nq)
        carry = lax.fori_loop(0, i // group, lambda t, cr: step(group * t, group, cr, False), (init, init))
        carry = lax.switch(i % group, [functools.partial(lambda cr, r: step(i - r, r + 1, cr, True), r=r) for r in range(group)], carry)
        out = jnp.zeros((tq, 2 * V_DIM), F32)
        for h, (m, l, acc) in enumerate(carry):
            out = out + acc / l
            lse_ref[h] = jnp.broadcast_to(m + jnp.log(l), (tq, LANES))
        o_ref[...] = out.astype(BF16)

    srcs, dsts = ([], []) if gather is None else (list(gather[0]), list(gather[1]))
    outs = pl.pallas_call(
        body, name="attention_fwd" if gather is None else "attention_fwd_gather", grid=(HEADS // 2, nq),
        in_specs=[pl.BlockSpec((tq, 2 * HEAD_PAD), lambda p, i: (i, p)), pl.BlockSpec((rows, 2 * HEAD_PAD), lambda p, i: (0, p)),
                  pl.BlockSpec((rows, 2 * V_DIM), lambda p, i: (0, p))] + [ANY] * (2 * n),
        out_specs=[pl.BlockSpec((tq, 2 * V_DIM), lambda p, i: (i, p)), pl.BlockSpec((2, tq, LANES), lambda p, i: (p, i, 0))] + [ANY] * n,
        out_shape=[jax.ShapeDtypeStruct((rows, HEADS * V_DIM), BF16), jax.ShapeDtypeStruct((HEADS, rows, LANES), F32)] + [
            jax.ShapeDtypeStruct(d.shape, d.dtype) for d in dsts],
        input_output_aliases={3 + n + a: 2 + a for a in range(n)}, scratch_shapes=GATHER_SEMS(n) if n else [],
        compiler_params=_params(("arbitrary", "arbitrary") if n else ("parallel", "parallel")),
    )(q, k, v, *srcs, *dsts)
    return outs[0], outs[1], list(outs[2:])


OUT_PROJECTIONS = ((M_POOL, 256), (M_MLA, 512), (M_CONF, 256), (M_SC, 256))


def _chunks(x, n=N_CHIPS, width=256):
    return [x[:, width * k:width * (k + 1)] for k in range(n)]


def merge_fwd(ua, o_att, uc, ud, z_br, z_gl, bias, misc, gpost, hres, li):
    rows = hres.shape[0]
    t = ROW_TILE
    cur, _, _, _, layer = _tile_specs(t, rows // HALO, li)
    d = D_MODEL

    def body(ua_ref, ob_ref, uc_ref, ud_ref, mg_ref, gl_ref, b_ref, wout_ref, wo_ref, gp_ref, h_ref, ub_ref, mb_ref, o_ref, hn_ref):
        ub = (ob_ref[...].astype(F32) * _silu(mg_ref[...].astype(F32))).astype(BF16)
        ub_ref[...] = ub
        m = jnp.zeros((t, d), F32)
        for idx, (u, (row0, n)) in enumerate(zip((ua_ref[...], ub, uc_ref[...], ud_ref[...]), OUT_PROJECTIONS)):
            gate = _sigmoid(gl_ref[:, d * idx:d * (idx + 1)].astype(F32) + b_ref[:, d * idx:d * (idx + 1)])
            m = m + gate * _chip_columns(u, wout_ref, row0, n)
        mb = m.astype(BF16)
        mb_ref[...] = mb
        o = jnp.concatenate([sum(_dot(mk, wo_ref[k, 256 * j:256 * (j + 1), :]) for k, mk in enumerate(_chunks(mb)))
                             for j in range(N_CHIPS)], axis=1)
        o_ref[...] = o
        hn_ref[...] = h_ref[...] + _rms(o, gp_ref[...])

    return pl.pallas_call(
        body, name="merge_fwd", grid=(rows // t,),
        in_specs=[cur(256), cur(512), cur(256), cur(256), cur(512, MG // 512), cur(ZG), layer((1, ZG)), _misc_spec(misc, 0, 1280),
                  _misc_spec(misc, M_WO, D_MODEL), layer((1, d)), cur(d)],
        out_specs=[cur(512), cur(d), cur(d), cur(d)],
        out_shape=[jax.ShapeDtypeStruct((rows, 512), BF16), jax.ShapeDtypeStruct((rows, d), BF16), jax.ShapeDtypeStruct((rows, d), F32),
                   jax.ShapeDtypeStruct((rows, d), F32)],
        compiler_params=_params(("parallel",)),
    )(ua, o_att, uc, ud, z_br, z_gl, bias, misc[0], misc[0], gpost, hres)


def loss_head(hres, target, n_tokens):
    rows, d = hres.shape
    t = ROW_TILE
    cur, _, _, full, _ = _tile_specs(t, rows // HALO)
    n_steps = rows // t

    def body(h_ref, t_ref, dh_ref, tot_ref, acc_ref):
        i = pl.program_id(0)

        @pl.when(i == 0)
        def _():
            acc_ref[...] = jnp.zeros_like(acc_ref)

        r = i * t + lax.broadcasted_iota(jnp.int32, (t, 1), 0)
        diff = jnp.where((r >= N_META) & (r < N_META + n_tokens), h_ref[...] - t_ref[...], 0.0)
        dh_ref[...] = diff * (1.0 / d)
        acc_ref[...] += jnp.sum(diff * diff, axis=0, keepdims=True)

        @pl.when(i == n_steps - 1)
        def _():
            tot_ref[...] = jnp.broadcast_to(jnp.sum(acc_ref[...], axis=1, keepdims=True), (1, LANES))

    return pl.pallas_call(
        body, name="loss_head", grid=(n_steps,), in_specs=[cur(d), cur(d)], out_specs=[cur(d), full((1, LANES))],
        out_shape=[jax.ShapeDtypeStruct((rows, d), F32), jax.ShapeDtypeStruct((1, LANES), F32)],
        scratch_shapes=[pltpu.VMEM((1, d), F32)], compiler_params=_params(("arbitrary",)),
    )(hres, target)


def _accumulate(i, ref, value):
    @pl.when(i == 0)
    def _():
        ref[...] = value

    @pl.when(i > 0)
    def _():
        ref[...] += value


def postnorm_bwd(dh, o, mb, misc, gpost, li, swap=None):
    rows, d = dh.shape
    t = ROW_TILE
    cur, _, _, full, layer = _tile_specs(t, rows // HALO, li)
    n = 0 if swap is None else len(swap)
    steps = rows // t

    def body(*refs):
        if n:
            start, finish = _swap_ops(refs[5:5 + n], refs[8 + n:8 + 2 * n], refs[8 + 2 * n:])
            pl.when(pl.program_id(0) == 0)(start)
        compute(*refs[:5], *refs[5 + n:8 + n])
        if n:
            pl.when(pl.program_id(0) == steps - 1)(finish)

    def compute(dh_ref, o_ref, mb_ref, wo_ref, gp_ref, dm_ref, dwo_ref, dgp_ref):
        i = pl.program_id(0)
        _, vjp = jax.vjp(_rms, o_ref[...], gp_ref[...])
        do, dg = vjp(dh_ref[...])
        dob = do.astype(BF16)
        dm_ref[...] = jnp.concatenate([sum(_dot_nt(dj, wo_ref[k, 256 * j:256 * (j + 1), :]) for j, dj in enumerate(_chunks(dob)))
                                       for k in range(N_CHIPS)], axis=1)
        dwo = _dot_tn(mb_ref[...], dob)
        for k in range(N_CHIPS):
            _accumulate(i, dwo_ref.at[k], jnp.concatenate(_chunks(dwo[256 * k:256 * (k + 1), :]), axis=0))
        _accumulate(i, dgp_ref, dg)

    sent = [] if swap is None else list(swap)
    outs = pl.pallas_call(
        body, name="postnorm_bwd" if swap is None else "postnorm_bwd_swap", grid=(steps,),
        in_specs=[cur(d), cur(d), cur(d), _misc_spec(misc, M_WO, d), layer((1, d))] + [ANY] * n,
        out_specs=[cur(d), full((N_CHIPS, d, 256)), full((1, d))] + [ANY] * n,
        out_shape=[jax.ShapeDtypeStruct((rows, d), F32), jax.ShapeDtypeStruct((N_CHIPS, d, 256), F32), jax.ShapeDtypeStruct((1, d), F32)] + [
            jax.ShapeDtypeStruct((p.shape[0], p.shape[1] // 2, p.shape[2]), p.dtype) for p in sent],
        scratch_shapes=[pltpu.SemaphoreType.DMA((n,)), pltpu.SemaphoreType.DMA((n,))] if n else [],
        compiler_params=_params(("arbitrary",)),
    )(dh, o, mb, misc[0], gpost, *sent)
    return outs[0], outs[1], outs[2], list(outs[3:])


def merge_bwd(dm, ua, ub, uc, ud, z_gl, bias, misc, o_att, z_br, dz_buf, li):
    rows, d = dm.shape
    t = ROW_TILE
    cur, _, _, full, layer = _tile_specs(t, rows // HALO, li)

    def body(dm_ref, ua_ref, ub_ref, uc_ref, ud_ref, gl_ref, b_ref, w_ref, o_ref, mg_ref, _,
             dua_ref, do_ref, duc_ref, dud_ref, dgl_ref, dw_ref, db_ref, dmg_ref, delta_ref):
        i = pl.program_id(0)
        dm = dm_ref[...]
        groups = ((ua_ref, dua_ref), (ub_ref, None), (uc_ref, duc_ref), (ud_ref, dud_ref))
        for idx, ((u_ref, du_ref), (row0, n)) in enumerate(zip(groups, OUT_PROJECTIONS)):
            cols = slice(d * idx, d * (idx + 1))
            u = u_ref[...]
            gate = _sigmoid(gl_ref[:, cols].astype(F32) + b_ref[:, cols])
            dgl = dm * _chip_columns(u, w_ref, row0, n) * gate * (1.0 - gate)
            dgl_ref[:, cols] = dgl.astype(BF16)
            _accumulate(i, db_ref.at[:, cols], jnp.sum(dgl, axis=0, keepdims=True))
            dyb = (dm * gate).astype(BF16)
            du = sum(_dot_nt(dyk, w_ref[k, row0:row0 + n, :]) for k, dyk in enumerate(_chunks(dyb)))
            for k, dwk in enumerate(_chunks(_dot_tn(u, dyb))):
                _accumulate(i, dw_ref.at[k, row0:row0 + n, :], dwk)
            if du_ref is not None:
                du_ref[...] = du
                continue
            o, mg = o_ref[...].astype(F32), mg_ref[...].astype(F32)
            do = du * _silu(mg)
            do_ref[...] = do.astype(BF16)
            dmg_ref[...] = (du * o * _silu_grad(mg)).astype(BF16)
            prod = do * o
            lane = lax.broadcasted_iota(jnp.int32, (1, HEADS * V_DIM), 1)
            for h in range(HEADS):
                part = jnp.where((lane >= V_DIM * h) & (lane < V_DIM * (h + 1)), prod, 0.0)
                delta_ref[h] = jnp.broadcast_to(jnp.sum(part, axis=-1, keepdims=True), (t, LANES))

    return pl.pallas_call(
        body, name="merge_bwd", grid=(rows // t,),
        in_specs=[cur(d), cur(256), cur(512), cur(256), cur(256), cur(ZG), layer((1, ZG)), _misc_spec(misc, 0, 1280), cur(512),
                  cur(512, MG // 512), ANY],
        out_specs=[cur(256), cur(512), cur(256), cur(256), cur(ZG), full((N_CHIPS, 1280, 256)), full((1, ZG)), cur(512, MG // 512),
                   pl.BlockSpec((HEADS, t, LANES), lambda i: (0, i, 0))],
        out_shape=[jax.ShapeDtypeStruct((rows, 256), F32), jax.ShapeDtypeStruct((rows, 512), BF16), jax.ShapeDtypeStruct((rows, 256), F32),
                   jax.ShapeDtypeStruct((rows, 256), F32), jax.ShapeDtypeStruct((rows, ZG), BF16),
                   jax.ShapeDtypeStruct((N_CHIPS, 1280, 256), F32), jax.ShapeDtypeStruct((1, ZG), F32),
                   jax.ShapeDtypeStruct((rows, ZB), BF16), jax.ShapeDtypeStruct((HEADS, rows, LANES), F32)],
        input_output_aliases={10: 7}, compiler_params=_params(("arbitrary",)),
    )(dm, ua, ub, uc, ud, z_gl, bias, misc[0], o_att, z_br, dz_buf)


def pool_shortconv_bwd(z_br, dua, dud, pwbd, pscale, sc_w, dz_buf, li):
    rows = z_br.shape[0]
    t = ROW_TILE
    n_steps = rows // t
    cur, prev, nxt, full, layer = _tile_specs(t, rows // HALO, li)

    def body(zc_ref, zp_ref, zn_ref, dac_ref, dan_ref, ddc_ref, ddn_ref, pw_ref, ps_ref, sw_ref, _, dz_ref, dpw_ref, dps_ref, dw_ref):
        i = pl.program_id(0)
        last = i == n_steps - 1
        zp = jnp.where(i == 0, jnp.zeros(zp_ref.shape, zp_ref.dtype), zp_ref[...])
        zn = jnp.where(last, jnp.zeros(zn_ref.shape, zn_ref.dtype), zn_ref[...])

        def ext(lo):
            return jnp.concatenate([zp[:, lo:lo + 256], zc_ref[:, lo:lo + 256], zn[:, lo:lo + 256]], axis=0).astype(F32)

        def ext_grad(c_ref, n_ref):
            return jnp.concatenate([jnp.zeros((HALO, 256), F32), c_ref[...], jnp.where(last, jnp.zeros(n_ref.shape, F32), n_ref[...])], axis=0)

        mid = slice(HALO, HALO + t)

        bg, c2, xv, sg = ext(BG), ext(C2), ext(XV), ext(SG)
        du = ext_grad(ddc_ref, ddn_ref)
        e = c2 * xv
        shifted = [_sh(e, SC_K - 1 - k) for k in range(SC_K)]
        f = sum(sw_ref[k:k + 1, :] * shifted[k] for k in range(SC_K))
        gate = _silu(sg)
        df = du * gate * bg
        de = sum(sw_ref[k:k + 1, :] * _ash(df, SC_K - 1 - k) for k in range(SC_K))
        d_sc = [(du * gate * f)[mid], (de * xv)[mid], (de * c2)[mid], (du * bg * f * _silu_grad(sg))[mid]]
        dw = jnp.concatenate([jnp.sum((df * shifted[k])[mid], axis=0, keepdims=True) for k in range(SC_K)] + [
            jnp.zeros((8 - SC_K, 256), F32)], axis=0)
        _accumulate(i, dw_ref, dw)

        v, pg = ext(PV), ext(PG)
        cnt = _pool_counts(i * t - HALO, t + 2 * HALO)
        p = (_pool_window_sums(v, _sh) / cnt - v)[mid]
        dya = ext_grad(dac_ref, dan_ref) * _silu(pg)
        dypb = (dya * ps_ref[...]).astype(BF16)
        dp = _dot_nt(dypb, pw_ref[...])
        dv = (_pool_window_sums(dp / cnt, _ash) - dp)[mid]
        pb = p.astype(BF16)
        pw = _dot(pb, pw_ref[...])
        dpg = dac_ref[...] * pw * ps_ref[...] * _silu_grad(pg[mid])
        _accumulate(i, dpw_ref, _dot_tn(pb, dypb[mid]))
        _accumulate(i, dps_ref, jnp.sum(dya[mid] * pw, axis=0, keepdims=True))

        dz_ref[...] = jnp.concatenate(d_sc + [dv, dpg], axis=1).astype(BF16)

    return pl.pallas_call(
        body, name="pool_shortconv_bwd", grid=(n_steps,),
        in_specs=[cur(ZB), prev(ZB), nxt(ZB), cur(256), nxt(256), cur(256), nxt(256), layer((256, 256)), layer((1, 256)), layer((8, 256)),
                  ANY],
        out_specs=[cur(1536, BG // 1536), full((256, 256)), full((1, 256)), full((8, 256))],
        out_shape=[jax.ShapeDtypeStruct((rows, ZB), BF16), jax.ShapeDtypeStruct((256, 256), F32), jax.ShapeDtypeStruct((1, 256), F32),
                   jax.ShapeDtypeStruct((8, 256), F32)],
        input_output_aliases={10: 0}, compiler_params=_params(("arbitrary",)),
    )(z_br, z_br, z_br, dua, dua, dud, dud, pwbd, pscale, sc_w, dz_buf)


def conformer_bwd_tail(z_br, duc, conf_w, conf_vec, dz_buf, li):
    rows = z_br.shape[0]
    t = ROW_TILE
    cur, prev, _, full, layer = _tile_specs(t, rows // HALO, li)

    def body(zc_ref, zp_ref, du_ref, cw_ref, cv_ref, _, dc_ref, dcg_ref, dv_ref):
        i = pl.program_id(0)
        zp = jnp.where(i == 0, jnp.zeros(zp_ref.shape, zp_ref.dtype), zp_ref[...])

        def ext(lo):
            return jnp.concatenate([zp[:, lo:lo + 256], zc_ref[:, lo:lo + 256]], axis=0).astype(F32)

        g1 = ext(CA) * _sigmoid(ext(CGT))
        c = _conf_conv(g1, cw_ref)[HALO:] + cv_ref[0:1, :]
        _, vjp = jax.vjp(_conf_tail, c, zc_ref[:, CG:CG + 256].astype(F32), cv_ref[1:2, :], cv_ref[2:3, :])
        dc, dcg, dlg, dlb = vjp(du_ref[...])
        dc_ref[...] = dc
        dcg_ref[...] = dcg.astype(BF16)
        dvec = jnp.concatenate([dlg, dlb, jnp.sum(dc, axis=0, keepdims=True), jnp.zeros((5, 256), F32)], axis=0)
        _accumulate(i, dv_ref, dvec)

    return pl.pallas_call(
        body, name="conformer_bwd_tail", grid=(rows // t,), in_specs=[cur(ZB), prev(ZB), cur(256), layer((32, 256)), layer((8, 256)), ANY],
        out_specs=[cur(256), cur(256, CG // 256), full((8, 256))],
        out_shape=[jax.ShapeDtypeStruct((rows, 256), F32), jax.ShapeDtypeStruct((rows, ZB), BF16), jax.ShapeDtypeStruct((8, 256), F32)],
        input_output_aliases={5: 1}, compiler_params=_params(("arbitrary",)),
    )(z_br, z_br, duc, conf_w, conf_vec, dz_buf)


def conformer_bwd_conv(z_br, dc, conf_w, dz_buf, li):
    rows = z_br.shape[0]
    t = ROW_TILE
    n_steps = rows // t
    cur, prev, nxt, full, layer = _tile_specs(t, rows // HALO, li)

    def body(zc_ref, zp_ref, dc_ref, dn_ref, cw_ref, _, dz_ref, dw_ref):
        i = pl.program_id(0)
        zp = jnp.where(i == 0, jnp.zeros(zp_ref.shape, zp_ref.dtype), zp_ref[...])
        dcn = jnp.where(i == n_steps - 1, jnp.zeros(dn_ref.shape, dn_ref.dtype), dn_ref[...])

        def ext(lo):
            return jnp.concatenate([zp[:, lo:lo + 256], zc_ref[:, lo:lo + 256]], axis=0).astype(F32)

        a, gt = ext(CA), ext(CGT)
        sg = _sigmoid(gt)
        g1 = a * sg
        dc = dc_ref[...]
        dce = jnp.concatenate([dc, dcn], axis=0)
        dg1 = jnp.zeros_like(dce)
        dws = []
        for k in range(CONF_K):
            dg1 = dg1 + cw_ref[k:k + 1, :] * _ash(dce, CONF_K - 1 - k)
            dws.append(jnp.sum(dc * _sh(g1, CONF_K - 1 - k)[HALO:], axis=0, keepdims=True))
        dg1 = dg1[:t]
        ac, sc = a[HALO:], sg[HALO:]
        dz_ref[...] = jnp.concatenate([dg1 * sc, dg1 * ac * sc * (1.0 - sc)], axis=1).astype(BF16)
        _accumulate(i, dw_ref, jnp.concatenate(dws + [jnp.zeros((32 - CONF_K, 256), F32)], axis=0))

    return pl.pallas_call(
        body, name="conformer_bwd_conv", grid=(n_steps,), in_specs=[cur(ZB), prev(ZB), cur(256), nxt(256), layer((32, 256)), ANY],
        out_specs=[cur(512, CA // 512), full((32, 256))],
        out_shape=[jax.ShapeDtypeStruct((rows, ZB), BF16), jax.ShapeDtypeStruct((32, 256), F32)],
        input_output_aliases={5: 0}, compiler_params=_params(("arbitrary",)),
    )(z_br, z_br, dc, dc, conf_w, dz_buf)


def attention_bwd(q, k, v, do, lse, delta, exchange=None):
    rows = q.shape[0]
    tq = ROW_TILE
    nq = rows // tq
    n = 0 if exchange is None else len(exchange[0])

    def body(*refs):
        if n:
            start, finish = _exchange_ops(refs[6:6 + n], refs[9 + 2 * n:9 + 3 * n], refs[9 + 3 * n:], exchange[2])
            pl.when((pl.program_id(0) == 0) & (pl.program_id(1) == 0))(start)
        compute(*refs[:6], *refs[6 + 2 * n:9 + 2 * n])
        if n:
            pl.when((pl.program_id(0) == HEADS // 2 - 1) & (pl.program_id(1) == nq - 1))(finish)

    def compute(q_ref, k_ref, v_ref, do_ref, lse_ref, dl_ref, dq_ref, dk_ref, dv_ref):
        j = pl.program_id(1)

        @pl.when(j == 0)
        def _():
            dq_ref[...] = jnp.zeros_like(dq_ref)

        def head_step(h, tile, n_tiles, dk, dv, diagonal):
            lanes = slice(HEAD_PAD * h, HEAD_PAD * (h + 1))
            hm = _head_lane_mask(h)
            kh = k_ref[:, lanes]
            vh = jnp.where(hm, v_ref[...], jnp.zeros((), BF16))
            r0, width = pl.multiple_of(tile * tq, tq), n_tiles * tq
            qi = q_ref[pl.ds(r0, width), lanes]
            doi = jnp.where(hm, do_ref[pl.ds(r0, width), :], jnp.zeros((), BF16))
            s = _dot_nt(qi, kh)
            if diagonal:
                s = jnp.where(lax.broadcasted_iota(jnp.int32, (tq, tq), 1) <= lax.broadcasted_iota(jnp.int32, (tq, tq), 0), s, -1e30)
            pr = jnp.exp(s - lse_ref[h, pl.ds(r0, width), :][:, 0:1])
            dv = dv + _dot_tn(pr.astype(BF16), doi)
            dp = _dot_nt(doi, vh)
            ds = (pr * (dp - dl_ref[h, pl.ds(r0, width), :][:, 0:1])).astype(BF16)
            dq_ref[pl.ds(r0, width), lanes] += _dot(ds, kh)
            return dk + _dot_tn(ds, qi), dv

        def step(tile, n_tiles, carry, diagonal):
            dk0, dk1, dv = carry
            dk0, dv = head_step(0, tile, n_tiles, dk0, dv, diagonal)
            dk1, dv = head_step(1, tile, n_tiles, dk1, dv, diagonal)
            return dk0, dk1, dv

        zero = jnp.zeros((tq, HEAD_PAD), F32)
        carry = step(j, 1, (zero, zero, jnp.zeros((tq, 2 * V_DIM), F32)), True)
        odd = (nq - 1 - j) % 2
        carry = lax.cond(odd == 1, lambda cr: step(j + 1, 1, cr, False), lambda cr: cr, carry)
        dk0, dk1, dv = lax.fori_loop(0, (nq - 1 - j) // 2, lambda t, cr: step(j + 1 + odd + 2 * t, 2, cr, False), carry)
        dk_ref[:, 0:HEAD_PAD] = dk0
        dk_ref[:, HEAD_PAD:2 * HEAD_PAD] = dk1
        dv_ref[...] = dv

    srcs, dsts = ([], []) if exchange is None else (list(exchange[0]), list(exchange[1]))
    outs = pl.pallas_call(
        body, name="attention_bwd" if exchange is None else "attention_bwd_exchange", grid=(HEADS // 2, nq),
        in_specs=[pl.BlockSpec((rows, 2 * HEAD_PAD), lambda p, j: (0, p)), pl.BlockSpec((tq, 2 * HEAD_PAD), lambda p, j: (j, p)),
                  pl.BlockSpec((tq, 2 * V_DIM), lambda p, j: (j, p)), pl.BlockSpec((rows, 2 * V_DIM), lambda p, j: (0, p)),
                  pl.BlockSpec((2, rows, LANES), lambda p, j: (p, 0, 0)), pl.BlockSpec((2, rows, LANES), lambda p, j: (p, 0, 0))] + [
                      ANY] * (2 * n),
        out_specs=[pl.BlockSpec((rows, 2 * HEAD_PAD), lambda p, j: (0, p)), pl.BlockSpec((tq, 2 * HEAD_PAD), lambda p, j: (j, p)),
                   pl.BlockSpec((tq, 2 * V_DIM), lambda p, j: (j, p))] + [ANY] * n,
        out_shape=[jax.ShapeDtypeStruct((rows, HEADS * HEAD_PAD), F32), jax.ShapeDtypeStruct((rows, HEADS * HEAD_PAD), F32),
                   jax.ShapeDtypeStruct((rows, HEADS * V_DIM), F32)] + [jax.ShapeDtypeStruct(d.shape, d.dtype) for d in dsts],
        input_output_aliases={6 + n + a: 3 + a for a in range(n)}, scratch_shapes=EXCHANGE_SEMS(n) if n else [],
        compiler_params=_params(("arbitrary", "arbitrary") if n else ("parallel", "arbitrary")),
    )(q, k, v, do, lse, delta, *srcs, *dsts)
    return outs[0], outs[1], outs[2], list(outs[3:])


def mla_prep_bwd(dq, dk, dv, z_br, rope, gq, gkv, misc, dz_buf, li):
    rows = dq.shape[0]
    t = ROW_TILE
    cur, _, _, full, layer = _tile_specs(t, rows // HALO, li)
    w8 = HEADS * HEAD_PAD
    uq, keys, values = slice(0, 256), slice(M_UKVK - M_UQ, M_UKVV - M_UQ), slice(M_UKVV - M_UQ, M_WO - M_UQ)

    def body(dq_ref, dk_ref, dv_ref, z_ref, rope_ref, gq_ref, gkv_ref, up_ref, _, dz_ref, dup_ref, dgq_ref, dgkv_ref):
        i = pl.program_id(0)
        cth, s1, s2 = rope_ref[:, 0:128], rope_ref[:, 128:256], rope_ref[:, 256:384]
        dqb = _rope_transposed(dq_ref[...] * Q_SCALE, _lanes8(cth), _lanes8(s1), _lanes8(s2), w8).astype(BF16)
        dq_chunks = _chunks(dqb)
        cq = z_ref[:, 0:256].astype(F32)
        qn, vjp_q = jax.vjp(_rms, cq, gq_ref[...])
        dcq, dgq = vjp_q(sum(_dot_nt(dqk, up_ref[k, uq, :]) for k, dqk in enumerate(dq_chunks)))
        _accumulate(i, dgq_ref, dgq)

        dk = dk_ref[...]
        dkr = sum(dk[:, HEAD_PAD * h:HEAD_PAD * (h + 1)] for h in range(HEADS))
        dkr = _rope_transposed(dkr, cth, s1, s2, HEAD_PAD)
        lane = lax.broadcasted_iota(jnp.int32, (1, HEAD_PAD), 1)
        dkr = jnp.where((lane >= QK_NOPE) & (lane < QK_NOPE + QK_ROPE), dkr, 0.0)
        dkb, dvb = dk.astype(BF16), dv_ref[...].astype(BF16)
        dk_chunks, dv_chunks = _chunks(dkb), _chunks(dvb, width=2 * V_DIM)
        ckv = z_ref[:, 256:384].astype(F32)
        kvn, vjp_kv = jax.vjp(_rms, ckv, gkv_ref[...])
        dckv, dgkv = vjp_kv(sum(_dot_nt(dk_chunks[k], up_ref[k, keys, :]) + _dot_nt(dv_chunks[k], up_ref[k, values, 0:2 * V_DIM])
                                for k in range(N_CHIPS)))
        _accumulate(i, dgkv_ref, dgkv)
        dz_ref[...] = jnp.concatenate([dcq, dckv, dkr], axis=1).astype(BF16)
        qnb, kvnb = qn.astype(BF16), kvn.astype(BF16)
        d_uq, d_keys, d_values = _chunks(_dot_tn(qnb, dqb)), _chunks(_dot_tn(kvnb, dkb)), _chunks(_dot_tn(kvnb, dvb), width=2 * V_DIM)
        for k in range(N_CHIPS):
            padded = jnp.concatenate([d_values[k], jnp.zeros((128, 256 - 2 * V_DIM), F32)], axis=1)
            _accumulate(i, dup_ref.at[k], jnp.concatenate([d_uq[k], d_keys[k], padded], axis=0))

    return pl.pallas_call(
        body, name="mla_prep_bwd", grid=(rows // t,),
        in_specs=[cur(w8), cur(w8), cur(512), cur(512, CQ // 512), cur(384), layer((1, 256)), layer((1, 128)),
                  _misc_spec(misc, M_UQ, M_WO - M_UQ), ANY],
        out_specs=[cur(512, CQ // 512), full((N_CHIPS, M_WO - M_UQ, 256)), full((1, 256)), full((1, 128))],
        out_shape=[jax.ShapeDtypeStruct((rows, ZB), BF16), jax.ShapeDtypeStruct((N_CHIPS, M_WO - M_UQ, 256), F32),
                   jax.ShapeDtypeStruct((1, 256), F32), jax.ShapeDtypeStruct((1, 128), F32)],
        input_output_aliases={8: 0}, compiler_params=_params(("arbitrary",)),
    )(dq, dk, dv, z_br, rope, gq, gkv, misc[0], dz_buf)


def prenorm_bwd(dz_br, w_br, dh_gl, hres, gpre, dh_next, li):
    rows, d = hres.shape
    t = ROW_TILE
    cur, _, _, full, layer = _tile_specs(t, rows // HALO, li)

    def body(dz_ref, w_ref, dp_ref, x_ref, g_ref, dn_ref, dx_ref, dg_ref):
        i = pl.program_id(0)
        dh = _dot_nt(dz_ref[...], w_ref[...]) + dp_ref[...]
        _, vjp = jax.vjp(_rms, x_ref[...], g_ref[...])
        dx, dg = vjp(dh)
        dx_ref[...] = dx + dn_ref[...]
        _accumulate(i, dg_ref, dg)

    return pl.pallas_call(
        body, name="prenorm_bwd", grid=(rows // t,), in_specs=[cur(ZB), layer((d, ZB), 0), cur(d), cur(d), layer((1, d)), cur(d)],
        out_specs=[cur(d), full((1, d))], out_shape=[jax.ShapeDtypeStruct((rows, d), F32), jax.ShapeDtypeStruct((1, d), F32)],
        compiler_params=_params(("arbitrary",)),
    )(dz_br, w_br, dh_gl, hres, gpre, dh_next)


def _mesh_position():
    return lax.axis_index("x"), lax.axis_index("y"), lax.axis_index("c")


def chip_exchange(src, gather, name):
    block = src.shape if gather else src.shape[1:]

    def body(src_ref, dst_ref, send_sems, recv_sems, local_sem):
        x, y, c = _mesh_position()
        me = 2 * x + y
        peers = ((1 - x, y), (x, 1 - y), (1 - x, 1 - y))

        def part(k):
            return src_ref if gather else src_ref.at[k]

        def copy(j, slot):
            px, py = peers[j]
            return pltpu.make_async_remote_copy(src_ref=part(2 * px + py), dst_ref=dst_ref.at[slot], send_sem=send_sems.at[j],
                                                recv_sem=recv_sems.at[j], device_id=(px, py, c), device_id_type=MESH)

        local = pltpu.make_async_copy(part(me), dst_ref.at[me], local_sem)
        local.start()
        sends = [copy(j, me) for j in range(3)]
        for cp in sends:
            cp.start()
        for j, (px, py) in enumerate(peers):
            copy(j, 2 * px + py).wait_recv()
        for cp in sends:
            cp.wait_send()
        local.wait()

    return pl.pallas_call(
        body, name=name, in_specs=[pl.BlockSpec(memory_space=pl.ANY)], out_specs=pl.BlockSpec(memory_space=pl.ANY),
        out_shape=jax.ShapeDtypeStruct((N_CHIPS,) + tuple(block), src.dtype),
        scratch_shapes=[pltpu.SemaphoreType.DMA((3,)), pltpu.SemaphoreType.DMA((3,)), pltpu.SemaphoreType.DMA(())],
    )(src)


def sibling_swap(src, name):
    def body(src_ref, dst_ref, send_sem, recv_sem):
        x, y, c = _mesh_position()
        cp = pltpu.make_async_remote_copy(src_ref=src_ref, dst_ref=dst_ref, send_sem=send_sem, recv_sem=recv_sem,
                                          device_id=(x, y, 1 - c), device_id_type=MESH)
        cp.start()
        cp.wait()

    return pl.pallas_call(
        body, name=name, in_specs=[pl.BlockSpec(memory_space=pl.ANY)], out_specs=pl.BlockSpec(memory_space=pl.ANY),
        out_shape=jax.ShapeDtypeStruct(src.shape, src.dtype),
        scratch_shapes=[pltpu.SemaphoreType.DMA(()), pltpu.SemaphoreType.DMA(())],
    )(src)


def _comm_call(body, name, n_in, out_shapes, n_sems):
    return pl.pallas_call(
        body, name=name, in_specs=[ANY] * n_in, out_specs=[ANY] * len(out_shapes), out_shape=out_shapes,
        scratch_shapes=[pltpu.SemaphoreType.DMA((n,)) for n in n_sems])


def _row_halves(c, rows):
    half = rows // 2
    return pl.ds(pl.multiple_of(c * half, 16), half), pl.ds(pl.multiple_of((1 - c) * half, 16), half)


def _peers():
    x, y, c = _mesh_position()
    return x, y, c, 2 * x + y, ((1 - x, y), (x, 1 - y), (1 - x, 1 - y))


def _gather_ops(src, dst, sems, layer, own_copy):
    ici_send, ici_recv, d2d_send, d2d_recv, own_sems = sems
    n = len(src)

    def fetch(a, j, slot):
        x, y, c, _, peers = _peers()
        px, py = peers[j]
        mine, _ = _row_halves(c, src[a].shape[1])
        return pltpu.make_async_remote_copy(src_ref=src[a].at[layer, mine], dst_ref=dst[a].at[layer, slot, mine], send_sem=ici_send.at[3 * a + j],
                                            recv_sem=ici_recv.at[3 * a + j], device_id=(px, py, c), device_id_type=MESH)

    def forward(a, j, sibling_half):
        x, y, c, _, peers = _peers()
        px, py = peers[j]
        part = dst[a].at[layer, 2 * px + py, _row_halves(c, src[a].shape[1])[1 if sibling_half else 0]]
        return pltpu.make_async_remote_copy(src_ref=part, dst_ref=part, send_sem=d2d_send.at[3 * a + j], recv_sem=d2d_recv.at[3 * a + j],
                                            device_id=(x, y, 1 - c), device_id_type=MESH)

    def own(a):
        return pltpu.make_async_copy(src[a].at[layer], dst[a].at[layer, _peers()[3]], own_sems.at[a])

    def start():
        me = _peers()[3]
        for a in range(n):
            if own_copy:
                own(a).start()
            for j in range(3):
                fetch(a, j, me).start()

    def finish():
        peers = _peers()[4]
        for j, (px, py) in enumerate(peers):
            for a in range(n):
                fetch(a, j, 2 * px + py).wait_recv()
                forward(a, j, False).start()
        for j in range(3):
            for a in range(n):
                forward(a, j, True).wait_recv()
        for j in range(3):
            for a in range(n):
                fetch(a, j, 0).wait_send()
                forward(a, j, False).wait_send()
        if own_copy:
            for a in range(n):
                own(a).wait()

    return start, finish


def _exchange_ops(src, dst, sems, layer):
    send_sems, recv_sems, own_sems = sems
    n = len(src)

    def copy(a, j, slot):
        x, y, c, _, peers = _peers()
        px, py = peers[j]
        return pltpu.make_async_remote_copy(src_ref=src[a].at[2 * px + py], dst_ref=dst[a].at[layer, slot], send_sem=send_sems.at[3 * a + j],
                                            recv_sem=recv_sems.at[3 * a + j], device_id=(px, py, c), device_id_type=MESH)

    def own(a):
        me = _peers()[3]
        return pltpu.make_async_copy(src[a].at[me], dst[a].at[layer, me], own_sems.at[a])

    def start():
        me = _peers()[3]
        for a in range(n):
            own(a).start()
            for j in range(3):
                copy(a, j, me).start()

    def finish():
        peers = _peers()[4]
        for j, (px, py) in enumerate(peers):
            for a in range(n):
                copy(a, j, 2 * px + py).wait_recv()
        for j in range(3):
            for a in range(n):
                copy(a, j, 0).wait_send()
        for a in range(n):
            own(a).wait()

    return start, finish


GATHER_SEMS = lambda n: [pltpu.SemaphoreType.DMA((3 * n,))] * 4 + [pltpu.SemaphoreType.DMA((n,))]
EXCHANGE_SEMS = lambda n: [pltpu.SemaphoreType.DMA((3 * n,))] * 2 + [pltpu.SemaphoreType.DMA((n,))]


def gather_layer(srcs, dsts, layer, name):
    n = len(srcs)

    def body(*refs):
        start, finish = _gather_ops(refs[:n], refs[2 * n:3 * n], refs[3 * n:], layer, False)
        start()
        finish()

    return pl.pallas_call(
        body, name=name, in_specs=[ANY] * (2 * n), out_specs=[ANY] * n, out_shape=[jax.ShapeDtypeStruct(d.shape, d.dtype) for d in dsts],
        input_output_aliases={n + a: a for a in range(n)}, scratch_shapes=GATHER_SEMS(n),
    )(*srcs, *dsts)


def exchange_layer(ss, dsts, layer, name):
    n = len(ss)

    def body(*refs):
        start, finish = _exchange_ops(refs[:n], refs[2 * n:3 * n], refs[3 * n:], layer)
        start()
        finish()

    return pl.pallas_call(
        body, name=name, in_specs=[ANY] * (2 * n), out_specs=[ANY] * n, out_shape=[jax.ShapeDtypeStruct(d.shape, d.dtype) for d in dsts],
        input_output_aliases={n + a: a for a in range(n)}, scratch_shapes=EXCHANGE_SEMS(n),
    )(*ss, *dsts)


def _swap_ops(src, dst, sems):
    send_sems, recv_sems = sems

    def copy(a):
        x, y, c = _mesh_position()
        return pltpu.make_async_remote_copy(src_ref=src[a].at[:, _row_halves(c, src[a].shape[1])[1]], dst_ref=dst[a], send_sem=send_sems.at[a],
                                            recv_sem=recv_sems.at[a], device_id=(x, y, 1 - c), device_id_type=MESH)

    def start():
        for a in range(len(src)):
            copy(a).start()

    def finish():
        for a in range(len(src)):
            copy(a).wait()

    return start, finish


def swap_row_halves(ps, name):
    n = len(ps)

    def body(*refs):
        start, finish = _swap_ops(refs[:n], refs[n:2 * n], refs[2 * n:])
        start()
        finish()

    outs = [jax.ShapeDtypeStruct((p.shape[0], p.shape[1] // 2, p.shape[2]), p.dtype) for p in ps]
    return _comm_call(body, name, n, outs, (n, n))(*ps)


def add_row_half(p, r, c, name):
    n, half, cols = r.shape
    rb = _row_block(half, cols, 2)
    steps = half // rb

    def body(c_ref, p_ref, r_ref, o_ref):
        o_ref[...] = (p_ref[...].astype(F32) + r_ref[...].astype(F32)).astype(BF16)

    return pl.pallas_call(
        body, name=name, out_shape=jax.ShapeDtypeStruct(r.shape, BF16),
        grid_spec=pltpu.PrefetchScalarGridSpec(
            num_scalar_prefetch=1, grid=(n, steps),
            in_specs=[pl.BlockSpec((1, rb, cols), lambda k, i, c_ref: (k, c_ref[0] * steps + i, 0)),
                      pl.BlockSpec((1, rb, cols), lambda k, i, c_ref: (k, i, 0))],
            out_specs=pl.BlockSpec((1, rb, cols), lambda k, i, c_ref: (k, i, 0))),
        compiler_params=_params(("parallel", "parallel")),
    )(jnp.reshape(c, (1,)).astype(jnp.int32), p, r)


def sum_row_halves(l, c, name):
    layers, n, half, cols = l.shape
    rb = _row_block(half, cols, 4)
    steps = half // rb

    def body(c_ref, l_ref, o_ref):
        acc = l_ref[0, 0].astype(F32)
        for s in range(1, n):
            acc = acc + l_ref[0, s].astype(F32)
        o_ref[0] = acc

    return pl.pallas_call(
        body, name=name, out_shape=jax.ShapeDtypeStruct((layers, 2 * half, cols), F32),
        grid_spec=pltpu.PrefetchScalarGridSpec(
            num_scalar_prefetch=1, grid=(layers, steps), in_specs=[pl.BlockSpec((1, n, rb, cols), lambda a, i, c_ref: (a, 0, i, 0))],
            out_specs=pl.BlockSpec((1, rb, cols), lambda a, i, c_ref: (a, c_ref[0] * steps + i, 0))),
        compiler_params=_params(("parallel", "parallel")),
    )(jnp.reshape(c, (1,)).astype(jnp.int32), l)


def share_row_halves(gs, name):
    n = len(gs)

    def body(*refs):
        dst = refs[n:2 * n]
        send_sems, recv_sems = refs[2 * n:]
        x, y, c = _mesh_position()

        def copy(a, sibling_half):
            part = dst[a].at[:, _row_halves(c, dst[a].shape[1])[1 if sibling_half else 0]]
            return pltpu.make_async_remote_copy(src_ref=part, dst_ref=part, send_sem=send_sems.at[a], recv_sem=recv_sems.at[a],
                                                device_id=(x, y, 1 - c), device_id_type=MESH)

        for a in range(n):
            copy(a, False).start()
        for a in range(n):
            copy(a, True).wait_recv()
        for a in range(n):
            copy(a, False).wait_send()

    return pl.pallas_call(
        body, name=name, in_specs=[ANY] * n, out_specs=[ANY] * n, out_shape=[jax.ShapeDtypeStruct(g.shape, g.dtype) for g in gs],
        input_output_aliases={a: a for a in range(n)}, scratch_shapes=[pltpu.SemaphoreType.DMA((n,)), pltpu.SemaphoreType.DMA((n,))],
    )(*gs)


def _row_block(rows, cols, itemsize):
    best = 16
    for rb in range(16, rows + 1, 16):
        if rows % rb == 0 and rb * cols * itemsize <= 2 * 1024 * 1024:
            best = rb
    assert rows % best == 0, (rows, cols)
    return best


def _comm_block(rows):
    return 1024 if rows % 1024 == 0 else rows


def sum_slots(buf, name):
    n, r, c = buf.shape
    rb = _comm_block(r)

    def body(b_ref, o_ref):
        acc = b_ref[0].astype(F32)
        for s in range(1, n):
            acc = acc + b_ref[s].astype(F32)
        o_ref[...] = acc

    return pl.pallas_call(
        body, name=name, grid=(r // rb,), in_specs=[pl.BlockSpec((n, rb, c), lambda i: (0, i, 0))],
        out_specs=pl.BlockSpec((rb, c), lambda i: (i, 0)), out_shape=jax.ShapeDtypeStruct((r, c), F32),
        compiler_params=_params(("parallel",)),
    )(buf)


def add_pair(a, b, out_dtype, name):
    shape = a.shape
    a2, b2 = a.reshape(-1, shape[-1]), b.reshape(-1, shape[-1])
    r, c = a2.shape
    rb = _comm_block(r)

    def body(a_ref, b_ref, o_ref):
        o_ref[...] = (a_ref[...].astype(F32) + b_ref[...].astype(F32)).astype(out_dtype)

    out = pl.pallas_call(
        body, name=name, grid=(r // rb,), in_specs=[pl.BlockSpec((rb, c), lambda i: (i, 0))] * 2,
        out_specs=pl.BlockSpec((rb, c), lambda i: (i, 0)), out_shape=jax.ShapeDtypeStruct((r, c), out_dtype),
        compiler_params=_params(("parallel",)),
    )(a2, b2)
    return out.reshape(shape)


def adamw(w, g, m, v):
    shape = w.shape
    cols = shape[-1]
    rows = math.prod(shape[:-1])
    if rows * cols <= 256 * 1024:
        rb, cb = rows, cols
    else:
        rb = max(r for r in range(8, 2049, 8) if rows % r == 0)
        cb = cols if rb * cols * 4 <= 2 * 1024 * 1024 else 256
    assert rows % rb == 0 and cols % cb == 0, shape

    def body(w_ref, g_ref, m_ref, v_ref, d_ref, nm_ref, nv_ref):
        g_ = g_ref[...]
        nm = ADAM_B1 * m_ref[...] + (1.0 - ADAM_B1) * g_
        nv = ADAM_B2 * v_ref[...] + (1.0 - ADAM_B2) * (g_ * g_)
        m_hat = nm / (1.0 - ADAM_B1 ** ADAM_STEP)
        v_hat = nv / (1.0 - ADAM_B2 ** ADAM_STEP)
        d_ref[...] = -ADAM_LR * (m_hat / (jnp.sqrt(v_hat) + ADAM_EPS) + ADAM_WD * w_ref[...])
        nm_ref[...] = nm
        nv_ref[...] = nv

    spec = pl.BlockSpec((rb, cb), lambda i, j: (i, j))
    outs = pl.pallas_call(
        body, name="adamw", grid=(rows // rb, cols // cb), in_specs=[spec] * 4, out_specs=[spec] * 3,
        out_shape=[jax.ShapeDtypeStruct((rows, cols), F32)] * 3, compiler_params=_params(("parallel", "parallel")),
    )(*(a.reshape(rows, cols) for a in (w, g, m, v)))
    return tuple(o.reshape(shape) for o in outs)


def _pack(arrays, dtype, row_multiple):
    flat = jnp.concatenate([a.astype(dtype).reshape(-1) for a in arrays])
    per = LANES * row_multiple
    total = -(-flat.shape[0] // per) * per
    return jnp.pad(flat, (0, total - flat.shape[0])).reshape(total // LANES, LANES)


def _unpack(buf, shapes):
    flat = buf.reshape(-1)
    out, off = [], 0
    for s in shapes:
        n = math.prod(s)
        out.append(flat[off:off + n].reshape(s))
        off += n
    return out


def _input_weights(blocks):
    c0, c1, c2, c3 = (blocks[..., k, :, :] for k in range(N_CHIPS))
    pad = lambda n: jnp.zeros(c0.shape[:-1] + (n,), blocks.dtype)
    w_br = jnp.concatenate([c1[..., 376:1400], c0[..., 0:896], pad(64), c0[..., 896:928], pad(32), c0[..., 928:], c1[..., 0:376]], axis=-1)
    return w_br, jnp.concatenate([c1[..., 1400:], c2, c3], axis=-1)


def _input_weights_inverse(dw_br, dw_gl):
    c0 = jnp.concatenate([dw_br[..., 1024:1920], dw_br[..., 1984:2016], dw_br[..., 2048:2952]], axis=-1)
    c1 = jnp.concatenate([dw_br[..., 2952:ZB], dw_br[..., 0:1024], dw_gl[..., 0:432]], axis=-1)
    return jnp.stack([c0, c1, dw_gl[..., 432:2264], dw_gl[..., 2264:]], axis=-3)


def _block_diag(pw):
    zeros = lambda n: jnp.zeros(pw.shape[:-3] + (64, n), pw.dtype)
    rows = [jnp.concatenate([zeros(64 * g), pw[..., g, :, :], zeros(64 * (3 - g))], axis=-1) for g in range(4)]
    return jnp.concatenate(rows, axis=-2)


def _block_diag_inverse(d):
    return jnp.stack([d[..., 64 * g:64 * (g + 1), 64 * g:64 * (g + 1)] for g in range(4)], axis=-3)


def _pad_rows(a, n):
    return jnp.pad(a, ((0, n - a.shape[0]), (0, 0)))


def _rope_tables(rows):
    inv = 1.0 / (ROPE_THETA ** (jnp.arange(0, QK_ROPE, 2, dtype=F32) / QK_ROPE))
    ang = jnp.arange(rows, dtype=F32)[:, None] * inv[None, :]
    cos, sin = jnp.cos(ang), jnp.sin(ang)
    one, zero = jnp.ones((rows, 1), F32), jnp.zeros((rows, 1), F32)
    rep = lambda a, n: jnp.broadcast_to(a, (rows, n))
    c = jnp.concatenate([rep(one, 64), cos, cos, rep(one, 32)], axis=1)
    s1 = jnp.concatenate([rep(zero, 64), -sin, rep(zero, 48)], axis=1)
    s2 = jnp.concatenate([rep(zero, 80), sin, rep(zero, 32)], axis=1)
    return jnp.concatenate([c, s1, s2], axis=1)


def _misc_block(parts):
    lead = parts["w_uq"].shape[:-2]
    pad_last = lambda a, n: jnp.pad(a, [(0, 0)] * (a.ndim - 1) + [(0, n - a.shape[-1])])
    uq = pad_last(parts["w_uq"].reshape(lead + (256, 2, QK_NOPE + QK_ROPE)), HEAD_PAD).reshape(lead + (256, 256))
    kv = parts["w_ukv"].reshape(lead + (128, 2, QK_NOPE + V_DIM))
    keys = pad_last(kv[..., :QK_NOPE], HEAD_PAD).reshape(lead + (128, 256))
    values = pad_last(kv[..., QK_NOPE:].reshape(lead + (128, 2 * V_DIM)), 256)
    wo = jnp.swapaxes(parts["w_o"].reshape(lead + (256, N_CHIPS, 256)), -3, -2).reshape(lead + (D_MODEL, 256))
    gap = jnp.zeros(lead + (M_UQ - M_SC - 256, 256), uq.dtype)
    return jnp.concatenate([parts["w_out_mla"], parts["w_out_pool"], parts["w_out_conf"], parts["w_out_sc"], gap, uq, keys, values, wo],
                           axis=-2)


def _misc_unblock(block):
    lead = block.shape[:-2]
    rows = lambda lo, n: block[..., lo:lo + n, :]
    uq = rows(M_UQ, 256).reshape(lead + (256, 2, HEAD_PAD))[..., :QK_NOPE + QK_ROPE].reshape(lead + (256, 2 * (QK_NOPE + QK_ROPE)))
    keys = rows(M_UKVK, 128).reshape(lead + (128, 2, HEAD_PAD))[..., :QK_NOPE]
    values = rows(M_UKVV, 128)[..., :2 * V_DIM].reshape(lead + (128, 2, V_DIM))
    wo = jnp.swapaxes(rows(M_WO, D_MODEL).reshape(lead + (N_CHIPS, 256, 256)), -3, -2).reshape(lead + (256, D_MODEL))
    return dict(w_out_mla=rows(M_MLA, 512), w_out_pool=rows(M_POOL, 256), w_out_conf=rows(M_CONF, 256), w_out_sc=rows(M_SC, 256), w_uq=uq,
                w_ukv=jnp.concatenate([keys, values], axis=-1).reshape(lead + (128, 256)), w_o=wo)


def _to_chip_blocks(name, a):
    if name == "w_o":
        return a.reshape(a.shape[:-2] + (N_CHIPS, a.shape[-2] // N_CHIPS, a.shape[-1]))
    return jnp.swapaxes(a.reshape(a.shape[:-1] + (N_CHIPS, a.shape[-1] // N_CHIPS)), -3, -2)


def _from_chip_blocks(name, b):
    if name == "w_o":
        return b.reshape(b.shape[:-3] + (N_CHIPS * b.shape[-2], b.shape[-1]))
    s = jnp.swapaxes(b, -3, -2)
    return s.reshape(s.shape[:-2] + (N_CHIPS * s.shape[-1],))


LARGE = ("w_in",) + MISC


def gather_small(shards):
    small = chip_exchange(_pack([shards[n] for n, _, _ in SHARDED_SMALL], F32, 8), True, "gather_small_ici")
    per_chip = [_unpack(small[k], [s for _, s, _ in SHARDED_SMALL]) for k in range(N_CHIPS)]
    return {name: jnp.concatenate([per_chip[k][idx] for k in range(N_CHIPS)], axis=axis) for idx, (name, _, axis) in enumerate(SHARDED_SMALL)}


class LocalWeights:
    def __init__(self, full):
        self.w_in = _to_chip_blocks("w_in", full["w_in"])
        self.misc = _misc_block({n: _to_chip_blocks(n, full[n]) for n in MISC}).astype(BF16)
        self.grads = [None] * DEPTH

    def layer(self, i):
        return self.w_in[i], (self.misc, i)

    def gather_with_attention(self, i):
        return None

    def gathered(self, dsts):
        pass

    def exchange_with_attention(self):
        return None

    def exchanged(self, dsts):
        pass

    def swap_with_postnorm(self):
        return None

    def swapped(self, rs):
        pass

    def put_grads(self, i, w_in, misc):
        self.grads[i] = (w_in, misc)

    def reduced(self):
        out = {n: _from_chip_blocks(n, b) for n, b in _misc_unblock(jnp.stack([m for _, m in self.grads])).items()}
        out["w_in"] = _from_chip_blocks("w_in", jnp.stack([w for w, _ in self.grads]))
        return out


class MeshWeights:
    def __init__(self, shards, c, chip):
        self.c, self.chip = c, chip
        self.srcs = [shards["w_in"].astype(BF16), _misc_block({n: shards[n] for n in MISC}).astype(BF16)]
        dsts = [lax.empty((DEPTH, N_CHIPS) + s.shape[1:], BF16) for s in self.srcs]
        self.dsts = gather_layer(self.srcs, dsts, 0, "gather_layer")
        self.landed = [lax.empty((DEPTH, N_CHIPS, s.shape[1] // 2, s.shape[2]), BF16) for s in self.srcs]
        self.pending = self.to_swap = None

    def layer(self, i):
        if i > 0:
            return self.dsts[0][i], (self.dsts[1], i)
        own = (jnp.arange(N_CHIPS) == self.chip)[:, None, None]
        w_in, misc = (jnp.where(own, s[0][None], d[0]) for s, d in zip(self.srcs, self.dsts))
        return w_in, (misc[None], 0)

    def gather_with_attention(self, i):
        return (self.srcs, self.dsts, i + 1) if i + 1 < DEPTH else None

    def gathered(self, dsts):
        if dsts:
            self.dsts = dsts

    def exchange_with_attention(self):
        return None if self.pending is None else (self.pending[0], self.landed, self.pending[1])

    def exchanged(self, dsts):
        if dsts:
            self.landed, self.pending = dsts, None

    def put_grads(self, i, w_in, misc):
        self.to_swap = ([w_in.astype(BF16), misc.astype(BF16)], i)

    def swap_with_postnorm(self):
        return None if self.to_swap is None else self.to_swap[0]

    def swapped(self, rs):
        if rs:
            ps, i = self.to_swap
            self.pending = ([add_row_half(p, r, self.c, "reduce_pair_%d" % a) for a, (p, r) in enumerate(zip(ps, rs))], i)
            self.to_swap = None

    def reduced(self):
        self.swapped(swap_row_halves(self.to_swap[0], "reduce_swap"))
        landed = exchange_layer(self.pending[0], self.landed, self.pending[1], "reduce_exchange")
        gs = [sum_row_halves(l, self.c, "reduce_sum_%d" % a) for a, l in enumerate(landed)]
        g_in, g_misc = share_row_halves(gs, "reduce_share")
        out = {"w_in": g_in}
        out.update(_misc_unblock(g_misc))
        return out


def reduce_small(grads, chip):
    names = [n for n, _ in REPLICATED] + [n for n, _, _ in SHARDED_SMALL]
    buf = _pack([grads[n] for n in names], F32, 8)
    chip_sum = add_pair(buf, sibling_swap(buf, "reduce_small_d2d"), F32, "reduce_small_pair")
    total = sum_slots(chip_exchange(chip_sum, True, "reduce_small_ici"), "reduce_small_sum")
    out = dict(zip(names, _unpack(total, [grads[n].shape for n in names])))
    for name, shape, axis in SHARDED_SMALL:
        out[name] = lax.dynamic_slice_in_dim(out[name], chip * shape[axis], shape[axis], axis)
    return out


def _prepare_small(w):
    row = lambda a: a[:, None, :]
    conf_vec = jnp.concatenate([row(w["conf_dw_b"]), row(w["conf_ln_g"]), row(w["conf_ln_b"]), jnp.zeros((DEPTH, 5, 256), F32)], axis=1)
    return dict(
        gpre=row(w["pre_norm_g"]), bias=row(w["gate_bias"]), pwbd=_block_diag(w["pool_w"]).astype(BF16), pscale=row(w["pool_scale"]),
        gq=row(w["q_norm_g"]), gkv=row(w["kv_norm_g"]), conf_w=jnp.pad(w["conf_dw_w"].astype(F32), ((0, 0), (0, 32 - CONF_K), (0, 0))),
        conf_vec=conf_vec, sc_w=jnp.pad(w["sc_dw_w"].astype(F32), ((0, 0), (0, 8 - SC_K), (0, 0))), gpost=row(w["post_norm_g"]))


def _prepare_layer(w_in_blocks, misc):
    w_br, w_gl = _input_weights(w_in_blocks)
    one = lambda a: a.astype(BF16)[None]
    return dict(w_br=one(w_br), w_gl=one(w_gl), misc=misc)


def local_step(x, target, w, large):
    seq = x.shape[0]
    length = N_META + seq
    rows = -(-length // ROW_TILE) * ROW_TILE
    bt = _big_tile(rows)
    hres = _pad_rows(jnp.concatenate([w["meta_tokens"].astype(F32), x], axis=0), rows)
    tgt = jnp.pad(target, ((N_META, rows - length), (0, 0)))
    rope = _rope_tables(rows)
    sw = _prepare_small(w)

    saved = []
    for i in range(DEPTH):
        lw = _prepare_layer(*large.layer(i))
        z_br, hb, hbt = prenorm_project(hres, sw["gpre"], lw["w_br"], i)
        z_gl = matmul(hb, lw["w_gl"], "nn", BF16, bt, 1024, D_MODEL, "project_gates", b_layer=0)
        ua, uc, ud, q, k, v = branches_fwd(z_br, rope, sw["pwbd"], sw["pscale"], sw["gq"], sw["gkv"], lw["misc"], sw["conf_w"],
                                           sw["conf_vec"], sw["sc_w"], i)
        o_att, lse, dsts = attention_fwd(q, k, v, large.gather_with_attention(i))
        large.gathered(dsts)
        ub, mb, o, hnew = merge_fwd(ua, o_att, uc, ud, z_br, z_gl, sw["bias"], lw["misc"], sw["gpost"], hres, i)
        saved.append(dict(lw=lw, hres=hres, hbt=hbt, z_br=z_br, z_gl=z_gl, ua=ua, ub=ub, uc=uc, ud=ud, q=q, k=k, v=v, o_att=o_att,
                          lse=lse, mb=mb, o=o))
        hres = hnew

    dh, total = loss_head(hres, tgt, seq)

    g = {n: [None] * DEPTH for n in ("gpre", "bias", "pwbd", "pscale", "gq", "gkv", "conf_w", "conf_vec", "sc_w", "gpost")}
    for i in reversed(range(DEPTH)):
        s = saved[i]
        lw = s["lw"]
        dm, dwo, g["gpost"][i], rs = postnorm_bwd(dh, s["o"], s["mb"], lw["misc"], sw["gpost"], i, large.swap_with_postnorm())
        large.swapped(rs)
        dz_br = lax.empty((rows, ZB), BF16)
        dua, do, duc, dud, dz_gl, dwout, g["bias"][i], dz_br, delta = merge_bwd(
            dm, s["ua"], s["ub"], s["uc"], s["ud"], s["z_gl"], sw["bias"], lw["misc"], s["o_att"], s["z_br"], dz_br, i)
        dz_br, g["pwbd"][i], g["pscale"][i], g["sc_w"][i] = pool_shortconv_bwd(s["z_br"], dua, dud, sw["pwbd"], sw["pscale"], sw["sc_w"],
                                                                              dz_br, i)
        dc, dz_br, g["conf_vec"][i] = conformer_bwd_tail(s["z_br"], duc, sw["conf_w"], sw["conf_vec"], dz_br, i)
        dz_br, g["conf_w"][i] = conformer_bwd_conv(s["z_br"], dc, sw["conf_w"], dz_br, i)
        dq, dk, dv, dsts = attention_bwd(s["q"], s["k"], s["v"], do, s["lse"], delta, large.exchange_with_attention())
        large.exchanged(dsts)
        dz_br, dwup, g["gq"][i], g["gkv"][i] = mla_prep_bwd(dq, dk, dv, s["z_br"], rope, sw["gq"], sw["gkv"], lw["misc"], dz_br, i)
        dw_br = matmul(s["hbt"], dz_br, "nn", BF16, D_MODEL, ZB // 2, bt, "grad_w_branch")
        dw_gl = matmul(s["hbt"], dz_gl, "nn", BF16, D_MODEL, 1024, bt, "grad_w_gates")
        dh_gl = matmul(dz_gl, lw["w_gl"], "nt", F32, bt, D_MODEL, 1024, "grad_h_gates", b_layer=0)
        dh, g["gpre"][i] = prenorm_bwd(dz_br, lw["w_br"], dh_gl, s["hres"], sw["gpre"], dh, i)
        gap = jnp.zeros((N_CHIPS, M_UQ - M_SC - 256, 256), F32)
        large.put_grads(i, _input_weights_inverse(dw_br, dw_gl), jnp.concatenate([dwout, gap, dwup, dwo], axis=1))

    g = {n: jnp.stack(parts) for n, parts in g.items()}
    grads = dict(
        meta_tokens=dh[:N_META], pre_norm_g=g["gpre"][:, 0], gate_bias=g["bias"][:, 0], pool_w=_block_diag_inverse(g["pwbd"]),
        pool_scale=g["pscale"][:, 0], q_norm_g=g["gq"][:, 0], kv_norm_g=g["gkv"][:, 0], conf_dw_w=g["conf_w"][:, :CONF_K],
        conf_dw_b=g["conf_vec"][:, 2], conf_ln_g=g["conf_vec"][:, 0], conf_ln_b=g["conf_vec"][:, 1], sc_dw_w=g["sc_w"][:, :SC_K],
        post_norm_g=g["gpost"][:, 0])
    return total[0, 0], dh[N_META:length], grads


def kernel(x, meta_tokens, pre_norm_g, w_in, gate_bias, pool_w, pool_scale, w_out_pool, q_norm_g, w_uq, kv_norm_g, w_ukv, w_out_mla, conf_dw_w, conf_dw_b, conf_ln_g, conf_ln_b, w_out_conf, sc_dw_w, w_out_sc, w_o, post_norm_g, loss_target, m_meta_tokens, m_pre_norm_g, m_w_in, m_gate_bias, m_pool_w, m_pool_scale, m_w_out_pool, m_q_norm_g, m_w_uq, m_kv_norm_g, m_w_ukv, m_w_out_mla, m_conf_dw_w, m_conf_dw_b, m_conf_ln_g, m_conf_ln_b, m_w_out_conf, m_sc_dw_w, m_w_out_sc, m_w_o, m_post_norm_g, v_meta_tokens, v_pre_norm_g, v_w_in, v_gate_bias, v_pool_w, v_pool_scale, v_w_out_pool, v_q_norm_g, v_w_uq, v_kv_norm_g, v_w_ukv, v_w_out_mla, v_conf_dw_w, v_conf_dw_b, v_conf_ln_g, v_conf_ln_b, v_w_out_conf, v_sc_dw_w, v_w_out_sc, v_w_o, v_post_norm_g):
    args = locals()
    weights = {n: args[n] for n in WEIGHT_ORDER}
    c = lax.axis_index("c")
    chip = 2 * lax.axis_index("x") + lax.axis_index("y")

    small = {n: weights[n] for n, _ in REPLICATED}
    small.update(gather_small(weights))
    large = MeshWeights(weights, c, chip)
    total, dx, grads = local_step(x[0], loss_target[0], small, large)
    loss = lax.psum(total * (0.5 / D_MODEL), ("x", "y", "c"))

    reduced = large.reduced()
    reduced.update(reduce_small(grads, chip))

    flip = lambda a: jnp.swapaxes(a, 1, 2)
    deltas, new_m, new_v = [], [], []
    for n in WEIGHT_ORDER:
        operands = (weights[n], reduced[n], args["m_" + n], args["v_" + n])
        if n == "w_in":
            operands = (flip(operands[0]), lax.optimization_barrier(flip(operands[1])), flip(operands[2]), flip(operands[3]))
            reduced[n] = flip(operands[1])
        d, nm, nv = adamw(*operands)
        if n == "w_in":
            d, nm, nv = flip(d), flip(nm), flip(nv)
        deltas.append(d)
        new_m.append(nm)
        new_v.append(nv)
    return (loss, dx[None], *[reduced[n] for n in WEIGHT_ORDER], *deltas, *new_m, *new_v)
```

```python
import functools
import math

import jax
import jax.numpy as jnp
from jax import lax
from jax.experimental import pallas as pl
from jax.experimental.pallas import tpu as pltpu

F32 = jnp.float32
BF16 = jnp.bfloat16

D_MODEL = 1024
DEPTH = 4
N_META = 16
EPS = 1e-6
HEADS = 8
QK_NOPE = 64
QK_ROPE = 32
V_DIM = 64
HEAD_PAD = 128
ROPE_THETA = 10000.0
Q_SCALE = (QK_NOPE + QK_ROPE) ** -0.5
CONF_K = 31
SC_K = 3
IN_W = 7328
N_CHIPS = 4

ZB = 3328
ZG = 4096
BG, C2, XV, SG, PV, PG, CQ, CKV, KR, MG, CA, CGT, CG = (0, 256, 512, 768, 1024, 1280, 1536, 1792, 1920, 2048, 2560, 2816, 3072)

KEY_GROUP = 4
ROW_TILE = 384
HALO = 32
LANES = 128
VMEM_LIMIT = 56 * 1024 * 1024

ADAM_LR = 0.001
ADAM_B1 = 0.9
ADAM_B2 = 0.999
ADAM_EPS = 1e-08
ADAM_WD = 0.01
ADAM_STEP = 10

MESH = pl.DeviceIdType.MESH
ANY = pl.BlockSpec(memory_space=pl.ANY)

MISC = ("w_out_mla", "w_out_pool", "w_out_conf", "w_out_sc", "w_uq", "w_ukv", "w_o")
M_MLA, M_POOL, M_CONF, M_SC, M_UQ, M_UKVK, M_UKVV, M_WO, MISC_ROWS = 0, 512, 768, 1024, 1536, 1792, 1920, 2048, 3072
SHARDED_SMALL = (
    ("meta_tokens", (N_META, 256), 1),
    ("conf_dw_w", (DEPTH, CONF_K, 64), 2),
    ("sc_dw_w", (DEPTH, SC_K, 64), 2),
)
REPLICATED = (
    ("pre_norm_g", (DEPTH, D_MODEL)),
    ("gate_bias", (DEPTH, 4 * D_MODEL)),
    ("pool_w", (DEPTH, 4, 64, 64)),
    ("pool_scale", (DEPTH, 256)),
    ("q_norm_g", (DEPTH, 256)),
    ("kv_norm_g", (DEPTH, 128)),
    ("conf_dw_b", (DEPTH, 256)),
    ("conf_ln_g", (DEPTH, 256)),
    ("conf_ln_b", (DEPTH, 256)),
    ("post_norm_g", (DEPTH, D_MODEL)),
)
WEIGHT_ORDER = ("meta_tokens", "pre_norm_g", "w_in", "gate_bias", "pool_w", "pool_scale", "w_out_pool", "q_norm_g", "w_uq",
                "kv_norm_g", "w_ukv", "w_out_mla", "conf_dw_w", "conf_dw_b", "conf_ln_g", "conf_ln_b", "w_out_conf", "sc_dw_w",
                "w_out_sc", "w_o", "post_norm_g")


def _dot(a, b):
    return lax.dot_general(a, b, (((1,), (0,)), ((), ())), preferred_element_type=F32)


def _dot_nt(a, b):
    return lax.dot_general(a, b, (((1,), (1,)), ((), ())), preferred_element_type=F32)


def _dot_tn(a, b):
    return lax.dot_general(a, b, (((0,), (0,)), ((), ())), preferred_element_type=F32)


def _sigmoid(x):
    return jax.nn.sigmoid(x)


def _silu(x):
    return x * _sigmoid(x)


def _silu_grad(x):
    s = _sigmoid(x)
    return s * (1.0 + x * (1.0 - s))


def _rms(x, g):
    return x * lax.rsqrt(jnp.mean(x * x, axis=-1, keepdims=True) + EPS) * g


def _sh(x, d):
    return x if d == 0 else pltpu.roll(x, d, 0)


def _ash(x, d):
    return x if d == 0 else pltpu.roll(x, x.shape[0] - d, 0)


def _lanes8(t):
    return jnp.concatenate([t] * HEADS, axis=1)


def _pool_window_sums(v, shift):
    a2 = v + shift(v, 1)
    a4 = a2 + shift(a2, 2)
    a8 = a4 + shift(a4, 4)
    a16 = a8 + shift(a8, 8)
    lane = lax.broadcasted_iota(jnp.int32, v.shape, 1)
    return jnp.where(lane < 64, a2, jnp.where(lane < 128, a4, jnp.where(lane < 192, a8, a16)))


def _pool_counts(first_row, rows):
    pos = first_row + lax.broadcasted_iota(jnp.int32, (rows, 256), 0)
    lane = lax.broadcasted_iota(jnp.int32, (rows, 256), 1)
    width = jnp.where(lane < 64, 2, jnp.where(lane < 128, 4, jnp.where(lane < 192, 8, 16)))
    return jnp.maximum(jnp.minimum(pos + 1, width), 1).astype(F32)


def _params(sem=None):
    return pltpu.CompilerParams(dimension_semantics=sem, vmem_limit_bytes=VMEM_LIMIT)


def _tile_specs(t, n_halo_blocks, li=0):
    per = t // HALO

    def layer(shape, idx=li):
        return pl.BlockSpec((None,) + tuple(shape), lambda i: (idx,) + (0,) * len(shape))

    def cur(c, cb=0):
        return pl.BlockSpec((t, c), lambda i: (i, cb))

    def prev(c, cb=0):
        return pl.BlockSpec((HALO, c), lambda i: (jnp.maximum(i * per - 1, 0), cb))

    def nxt(c, cb=0):
        return pl.BlockSpec((HALO, c), lambda i: (jnp.minimum((i + 1) * per, n_halo_blocks - 1), cb))

    def full(shape):
        return pl.BlockSpec(shape, lambda i: (0,) * len(shape))

    return cur, prev, nxt, full, layer


def _big_tile(rows):
    return rows // 3 if rows % (3 * LANES) == 0 else ROW_TILE


def matmul(a, b, mode, out_dtype, tm, tn, tk, name, b_layer=None):
    bs = b.shape if b_layer is None else b.shape[1:]
    lead = () if b_layer is None else (None,)
    pick = (lambda *ix: ix) if b_layer is None else (lambda *ix: (b_layer,) + ix)
    if mode == "nn":
        (m, k), n = a.shape, bs[1]
        a_spec = pl.BlockSpec((tm, tk), lambda i, j, kk: (i, kk))
        b_spec = pl.BlockSpec(lead + (tk, tn), lambda i, j, kk: pick(kk, j))
        dot = _dot
    elif mode == "nt":
        (m, k), n = a.shape, bs[0]
        a_spec = pl.BlockSpec((tm, tk), lambda i, j, kk: (i, kk))
        b_spec = pl.BlockSpec(lead + (tn, tk), lambda i, j, kk: pick(j, kk))
        dot = _dot_nt
    else:
        raise ValueError(mode)
    assert m % tm == 0 and n % tn == 0 and k % tk == 0, (a.shape, bs, tm, tn, tk)
    nk = k // tk

    def body(a_ref, b_ref, o_ref, *acc):
        if nk == 1:
            o_ref[...] = dot(a_ref[...], b_ref[...]).astype(out_dtype)
            return
        (acc_ref,) = acc
        kk = pl.program_id(2)

        @pl.when(kk == 0)
        def _():
            acc_ref[...] = dot(a_ref[...], b_ref[...])

        @pl.when((kk > 0) & (kk < nk - 1))
        def _():
            acc_ref[...] += dot(a_ref[...], b_ref[...])

        @pl.when(kk == nk - 1)
        def _():
            o_ref[...] = (acc_ref[...] + dot(a_ref[...], b_ref[...])).astype(out_dtype)

    return pl.pallas_call(
        body, name=name, grid=(m // tm, n // tn, nk), in_specs=[a_spec, b_spec],
        out_specs=pl.BlockSpec((tm, tn), lambda i, j, kk: (i, j)), out_shape=jax.ShapeDtypeStruct((m, n), out_dtype),
        scratch_shapes=[pltpu.VMEM((tm, tn), F32)] if nk > 1 else [], compiler_params=_params(("parallel", "parallel", "arbitrary")),
    )(a, b)


def prenorm_project(hres, g, w, li):
    rows, d = hres.shape
    n = w.shape[2]
    tm, tn = _big_tile(rows), n // 2

    def body(x_ref, g_ref, w_ref, z_ref, hb_ref, hbt_ref):
        @pl.when(pl.program_id(1) == 0)
        def _():
            h = _rms(x_ref[...], g_ref[...])
            hb_ref[...] = h.astype(BF16)
            hbt_ref[...] = h.T.astype(BF16)

        z_ref[...] = _dot(hb_ref[...], w_ref[...]).astype(BF16)

    return pl.pallas_call(
        body, name="prenorm_project", grid=(rows // tm, n // tn),
        in_specs=[pl.BlockSpec((tm, d), lambda i, j: (i, 0)), pl.BlockSpec((None, 1, d), lambda i, j: (li, 0, 0)),
                  pl.BlockSpec((None, d, tn), lambda i, j: (0, 0, j))],
        out_specs=[pl.BlockSpec((tm, tn), lambda i, j: (i, j)), pl.BlockSpec((tm, d), lambda i, j: (i, 0)),
                   pl.BlockSpec((d, tm), lambda i, j: (0, i))],
        out_shape=[jax.ShapeDtypeStruct((rows, n), BF16), jax.ShapeDtypeStruct((rows, d), BF16), jax.ShapeDtypeStruct((d, rows), BF16)],
        compiler_params=_params(("parallel", "arbitrary")),
    )(hres, g, w)


def _rope(q, c, s1, s2, width):
    return q * c + pltpu.roll(q, width - 16, 1) * s1 + pltpu.roll(q, 16, 1) * s2


def _rope_transposed(dq, c, s1, s2, width):
    return dq * c + pltpu.roll(dq * s1, 16, 1) + pltpu.roll(dq * s2, width - 16, 1)


def _conf_conv(g1, w_ref):
    acc = jnp.zeros_like(g1)
    for k in range(CONF_K):
        acc = acc + w_ref[k:k + 1, :] * _sh(g1, CONF_K - 1 - k)
    return acc


def _conf_tail(c, cg, lg, lb):
    mu = jnp.mean(c, axis=-1, keepdims=True)
    xc = c - mu
    var = jnp.mean(xc * xc, axis=-1, keepdims=True)
    n = xc * lax.rsqrt(var + EPS) * lg + lb
    return _silu(n) * _silu(cg)


def _misc_spec(misc, row0, rows):
    assert row0 % rows == 0
    return pl.BlockSpec((None, N_CHIPS, rows, 256), lambda i: (misc[1], 0, row0 // rows, 0))


def _chip_columns(x, w_ref, row0, rows, lanes=256):
    return jnp.concatenate([_dot(x, w_ref[k, row0:row0 + rows, 0:lanes]) for k in range(N_CHIPS)], axis=1)


def branches_fwd(z_br, rope, pwbd, pscale, gq, gkv, misc, conf_w, conf_vec, sc_w, li):
    rows = z_br.shape[0]
    t = ROW_TILE
    cur, prev, _, _, layer = _tile_specs(t, rows // HALO, li)

    def body(zc_ref, zp_ref, rope_ref, pw_ref, ps_ref, gq_ref, gkv_ref, up_ref, cw_ref, cv_ref, sw_ref,
             ua_ref, uc_ref, ud_ref, q_ref, k_ref, v_ref):
        i = pl.program_id(0)
        zp = jnp.where(i == 0, jnp.zeros(zp_ref.shape, zp_ref.dtype), zp_ref[...])

        def ext(lo, w=256):
            return jnp.concatenate([zp[:, lo:lo + w], zc_ref[:, lo:lo + w]], axis=0).astype(F32)

        def col(lo, w=256):
            return zc_ref[:, lo:lo + w].astype(F32)

        v = ext(PV)
        p = (_pool_window_sums(v, _sh) / _pool_counts(i * t - HALO, t + HALO) - v)[HALO:]
        ya = _dot(p.astype(BF16), pw_ref[...]) * ps_ref[...]
        ua_ref[...] = (ya * _silu(col(PG))).astype(BF16)

        g1 = ext(CA) * _sigmoid(ext(CGT))
        c = _conf_conv(g1, cw_ref)[HALO:] + cv_ref[0:1, :]
        uc_ref[...] = _conf_tail(c, col(CG), cv_ref[1:2, :], cv_ref[2:3, :]).astype(BF16)

        e = ext(C2) * ext(XV)
        f = jnp.zeros_like(e)
        for k in range(SC_K):
            f = f + sw_ref[k:k + 1, :] * _sh(e, SC_K - 1 - k)
        ud_ref[...] = (col(BG) * f[HALO:] * _silu(col(SG))).astype(BF16)

        cth, s1, s2 = rope_ref[:, 0:128], rope_ref[:, 128:256], rope_ref[:, 256:384]
        qn = _rms(col(CQ), gq_ref[...]).astype(BF16)
        q = _chip_columns(qn, up_ref, 0, 256)
        w8 = HEADS * HEAD_PAD
        q_ref[...] = (_rope(q, _lanes8(cth), _lanes8(s1), _lanes8(s2), w8) * Q_SCALE).astype(BF16)
        kvn = _rms(col(CKV, 128), gkv_ref[...]).astype(BF16)
        kr = _rope(col(KR, 128), cth, s1, s2, HEAD_PAD)
        k_ref[...] = (_chip_columns(kvn, up_ref, M_UKVK - M_UQ, 128) + _lanes8(kr)).astype(BF16)
        v_ref[...] = _chip_columns(kvn, up_ref, M_UKVV - M_UQ, 128, 2 * V_DIM).astype(BF16)

    outs = [jax.ShapeDtypeStruct((rows, 256), BF16)] * 3 + [jax.ShapeDtypeStruct((rows, 1024), BF16)] * 2 + [
        jax.ShapeDtypeStruct((rows, 512), BF16)]
    return pl.pallas_call(
        body, name="branches_fwd", grid=(rows // t,),
        in_specs=[cur(ZB), prev(ZB), cur(384), layer((256, 256)), layer((1, 256)), layer((1, 256)), layer((1, 128)),
                  _misc_spec(misc, M_UQ, M_WO - M_UQ), layer((32, 256)), layer((8, 256)), layer((8, 256))],
        out_specs=[cur(256), cur(256), cur(256), cur(1024), cur(1024), cur(512)], out_shape=outs,
        compiler_params=_params(("parallel",)),
    )(z_br, z_br, rope, pwbd, pscale, gq, gkv, misc[0], conf_w, conf_vec, sc_w)


def _head_lane_mask(h):
    lane = lax.broadcasted_iota(jnp.int32, (1, 2 * V_DIM), 1)
    return (lane >= V_DIM * h) & (lane < V_DIM * (h + 1))


def attention_fwd(q, k, v, gather=None):
    rows = q.shape[0]
    tq = ROW_TILE
    nq = rows // tq
    n = 0 if gather is None else len(gather[0])

    def body(*refs):
        if n:
            start, finish = _gather_ops(refs[3:3 + n], refs[5 + 2 * n:5 + 3 * n], refs[5 + 3 * n:], gather[2], True)
            pl.when((pl.program_id(0) == 0) & (pl.program_id(1) == 0))(start)
        compute(*refs[:3], *refs[3 + 2 * n:5 + 2 * n])
        if n:
            pl.when((pl.program_id(0) == HEADS // 2 - 1) & (pl.program_id(1) == nq - 1))(finish)

    def compute(q_ref, k_ref, v_ref, o_ref, lse_ref):
        i = pl.program_id(1)

        def head_step(h, tile, n_tiles, carry, masked):
            m, l, acc = carry
            width = n_tiles * tq
            r0 = pl.multiple_of(tile * tq, tq)
            kh = k_ref[pl.ds(r0, width), HEAD_PAD * h:HEAD_PAD * (h + 1)]
            vh = jnp.where(_head_lane_mask(h), v_ref[pl.ds(r0, width), :], jnp.zeros((), BF16))
            s = _dot_nt(q_ref[:, HEAD_PAD * h:HEAD_PAD * (h + 1)], kh)
            if masked:
                row = lax.broadcasted_iota(jnp.int32, (tq, width), 0)
                colm = lax.broadcasted_iota(jnp.int32, (tq, width), 1)
                s = jnp.where(colm <= row + (width - tq), s, -1e30)
            m2 = jnp.maximum(m, jnp.max(s, axis=-1, keepdims=True))
            alpha = jnp.exp(m - m2)
            pr = jnp.exp(s - m2)
            return m2, alpha * l + jnp.sum(pr, axis=-1, keepdims=True), alpha * acc + _dot(pr.astype(BF16), vh)

        def step(tile, n_tiles, carry, masked):
            return tuple(head_step(h, tile, n_tiles, carry[h], masked) for h in range(2))

        init = (jnp.full((tq, 1), -1e30, F32), jnp.zeros((tq, 1), F32), jnp.zeros((tq, 2 * V_DIM), F32))
        group = min(KEY_GROUP, nq)
        carry = lax.fori_loop(0, i // group, lambda t, cr: step(group * t, group, cr, False), (init, init))
        carry = lax.switch(i % group, [functools.partial(lambda cr, r: step(i - r, r + 1, cr, True), r=r) for r in range(group)], carry)
        out = jnp.zeros((tq, 2 * V_DIM), F32)
        for h, (m, l, acc) in enumerate(carry):
            out = out + acc / l
            lse_ref[h] = jnp.broadcast_to(m + jnp.log(l), (tq, LANES))
        o_ref[...] = out.astype(BF16)

    srcs, dsts = ([], []) if gather is None else (list(gather[0]), list(gather[1]))
    outs = pl.pallas_call(
        body, name="attention_fwd" if gather is None else "attention_fwd_gather", grid=(HEADS // 2, nq),
        in_specs=[pl.BlockSpec((tq, 2 * HEAD_PAD), lambda p, i: (i, p)), pl.BlockSpec((rows, 2 * HEAD_PAD), lambda p, i: (0, p)),
                  pl.BlockSpec((rows, 2 * V_DIM), lambda p, i: (0, p))] + [ANY] * (2 * n),
        out_specs=[pl.BlockSpec((tq, 2 * V_DIM), lambda p, i: (i, p)), pl.BlockSpec((2, tq, LANES), lambda p, i: (p, i, 0))] + [ANY] * n,
        out_shape=[jax.ShapeDtypeStruct((rows, HEADS * V_DIM), BF16), jax.ShapeDtypeStruct((HEADS, rows, LANES), F32)] + [
            jax.ShapeDtypeStruct(d.shape, d.dtype) for d in dsts],
        input_output_aliases={3 + n + a: 2 + a for a in range(n)}, scratch_shapes=GATHER_SEMS(n) if n else [],
        compiler_params=_params(("arbitrary", "arbitrary") if n else ("parallel", "parallel")),
    )(q, k, v, *srcs, *dsts)
    return outs[0], outs[1], list(outs[2:])


OUT_PROJECTIONS = ((M_POOL, 256), (M_MLA, 512), (M_CONF, 256), (M_SC, 256))


def _chunks(x, n=N_CHIPS, width=256):
    return [x[:, width * k:width * (k + 1)] for k in range(n)]


def merge_fwd(ua, o_att, uc, ud, z_br, z_gl, bias, misc, gpost, hres, li):
    rows = hres.shape[0]
    t = ROW_TILE
    cur, _, _, _, layer = _tile_specs(t, rows // HALO, li)
    d = D_MODEL

    def body(ua_ref, ob_ref, uc_ref, ud_ref, mg_ref, gl_ref, b_ref, wout_ref, wo_ref, gp_ref, h_ref, ub_ref, mb_ref, o_ref, hn_ref):
        ub = (ob_ref[...].astype(F32) * _silu(mg_ref[...].astype(F32))).astype(BF16)
        ub_ref[...] = ub
        m = jnp.zeros((t, d), F32)
        for idx, (u, (row0, n)) in enumerate(zip((ua_ref[...], ub, uc_ref[...], ud_ref[...]), OUT_PROJECTIONS)):
            gate = _sigmoid(gl_ref[:, d * idx:d * (idx + 1)].astype(F32) + b_ref[:, d * idx:d * (idx + 1)])
            m = m + gate * _chip_columns(u, wout_ref, row0, n)
        mb = m.astype(BF16)
        mb_ref[...] = mb
        o = jnp.concatenate([sum(_dot(mk, wo_ref[k, 256 * j:256 * (j + 1), :]) for k, mk in enumerate(_chunks(mb)))
                             for j in range(N_CHIPS)], axis=1)
        o_ref[...] = o
        hn_ref[...] = h_ref[...] + _rms(o, gp_ref[...])

    return pl.pallas_call(
        body, name="merge_fwd", grid=(rows // t,),
        in_specs=[cur(256), cur(512), cur(256), cur(256), cur(512, MG // 512), cur(ZG), layer((1, ZG)), _misc_spec(misc, 0, 1280),
                  _misc_spec(misc, M_WO, D_MODEL), layer((1, d)), cur(d)],
        out_specs=[cur(512), cur(d), cur(d), cur(d)],
        out_shape=[jax.ShapeDtypeStruct((rows, 512), BF16), jax.ShapeDtypeStruct((rows, d), BF16), jax.ShapeDtypeStruct((rows, d), F32),
                   jax.ShapeDtypeStruct((rows, d), F32)],
        compiler_params=_params(("parallel",)),
    )(ua, o_att, uc, ud, z_br, z_gl, bias, misc[0], misc[0], gpost, hres)


def loss_head(hres, target, n_tokens):
    rows, d = hres.shape
    t = ROW_TILE
    cur, _, _, full, _ = _tile_specs(t, rows // HALO)
    n_steps = rows // t

    def body(h_ref, t_ref, dh_ref, tot_ref, acc_ref):
        i = pl.program_id(0)

        @pl.when(i == 0)
        def _():
            acc_ref[...] = jnp.zeros_like(acc_ref)

        r = i * t + lax.broadcasted_iota(jnp.int32, (t, 1), 0)
        diff = jnp.where((r >= N_META) & (r < N_META + n_tokens), h_ref[...] - t_ref[...], 0.0)
        dh_ref[...] = diff * (1.0 / d)
        acc_ref[...] += jnp.sum(diff * diff, axis=0, keepdims=True)

        @pl.when(i == n_steps - 1)
        def _():
            tot_ref[...] = jnp.broadcast_to(jnp.sum(acc_ref[...], axis=1, keepdims=True), (1, LANES))

    return pl.pallas_call(
        body, name="loss_head", grid=(n_steps,), in_specs=[cur(d), cur(d)], out_specs=[cur(d), full((1, LANES))],
        out_shape=[jax.ShapeDtypeStruct((rows, d), F32), jax.ShapeDtypeStruct((1, LANES), F32)],
        scratch_shapes=[pltpu.VMEM((1, d), F32)], compiler_params=_params(("arbitrary",)),
    )(hres, target)


def _accumulate(i, ref, value):
    @pl.when(i == 0)
    def _():
        ref[...] = value

    @pl.when(i > 0)
    def _():
        ref[...] += value


def postnorm_bwd(dh, o, mb, misc, gpost, li, swap=None):
    rows, d = dh.shape
    t = ROW_TILE
    cur, _, _, full, layer = _tile_specs(t, rows // HALO, li)
    n = 0 if swap is None else len(swap)
    steps = rows // t

    def body(*refs):
        if n:
            start, finish = _swap_ops(refs[5:5 + n], refs[8 + n:8 + 2 * n], refs[8 + 2 * n:])
            pl.when(pl.program_id(0) == 0)(start)
        compute(*refs[:5], *refs[5 + n:8 + n])
        if n:
            pl.when(pl.program_id(0) == steps - 1)(finish)

    def compute(dh_ref, o_ref, mb_ref, wo_ref, gp_ref, dm_ref, dwo_ref, dgp_ref):
        i = pl.program_id(0)
        _, vjp = jax.vjp(_rms, o_ref[...], gp_ref[...])
        do, dg = vjp(dh_ref[...])
        dob = do.astype(BF16)
        dm_ref[...] = jnp.concatenate([sum(_dot_nt(dj, wo_ref[k, 256 * j:256 * (j + 1), :]) for j, dj in enumerate(_chunks(dob)))
                                       for k in range(N_CHIPS)], axis=1)
        dwo = _dot_tn(mb_ref[...], dob)
        for k in range(N_CHIPS):
            _accumulate(i, dwo_ref.at[k], jnp.concatenate(_chunks(dwo[256 * k:256 * (k + 1), :]), axis=0))
        _accumulate(i, dgp_ref, dg)

    sent = [] if swap is None else list(swap)
    outs = pl.pallas_call(
        body, name="postnorm_bwd" if swap is None else "postnorm_bwd_swap", grid=(steps,),
        in_specs=[cur(d), cur(d), cur(d), _misc_spec(misc, M_WO, d), layer((1, d))] + [ANY] * n,
        out_specs=[cur(d), full((N_CHIPS, d, 256)), full((1, d))] + [ANY] * n,
        out_shape=[jax.ShapeDtypeStruct((rows, d), F32), jax.ShapeDtypeStruct((N_CHIPS, d, 256), F32), jax.ShapeDtypeStruct((1, d), F32)] + [
            jax.ShapeDtypeStruct((p.shape[0], p.shape[1] // 2, p.shape[2]), p.dtype) for p in sent],
        scratch_shapes=[pltpu.SemaphoreType.DMA((n,)), pltpu.SemaphoreType.DMA((n,))] if n else [],
        compiler_params=_params(("arbitrary",)),
    )(dh, o, mb, misc[0], gpost, *sent)
    return outs[0], outs[1], outs[2], list(outs[3:])


def merge_bwd(dm, ua, ub, uc, ud, z_gl, bias, misc, o_att, z_br, dz_buf, li):
    rows, d = dm.shape
    t = ROW_TILE
    cur, _, _, full, layer = _tile_specs(t, rows // HALO, li)

    def body(dm_ref, ua_ref, ub_ref, uc_ref, ud_ref, gl_ref, b_ref, w_ref, o_ref, mg_ref, _,
             dua_ref, do_ref, duc_ref, dud_ref, dgl_ref, dw_ref, db_ref, dmg_ref, delta_ref):
        i = pl.program_id(0)
        dm = dm_ref[...]
        groups = ((ua_ref, dua_ref), (ub_ref, None), (uc_ref, duc_ref), (ud_ref, dud_ref))
        for idx, ((u_ref, du_ref), (row0, n)) in enumerate(zip(groups, OUT_PROJECTIONS)):
            cols = slice(d * idx, d * (idx + 1))
            u = u_ref[...]
            gate = _sigmoid(gl_ref[:, cols].astype(F32) + b_ref[:, cols])
            dgl = dm * _chip_columns(u, w_ref, row0, n) * gate * (1.0 - gate)
            dgl_ref[:, cols] = dgl.astype(BF16)
            _accumulate(i, db_ref.at[:, cols], jnp.sum(dgl, axis=0, keepdims=True))
            dyb = (dm * gate).astype(BF16)
            du = sum(_dot_nt(dyk, w_ref[k, row0:row0 + n, :]) for k, dyk in enumerate(_chunks(dyb)))
            for k, dwk in enumerate(_chunks(_dot_tn(u, dyb))):
                _accumulate(i, dw_ref.at[k, row0:row0 + n, :], dwk)
            if du_ref is not None:
                du_ref[...] = du
                continue
            o, mg = o_ref[...].astype(F32), mg_ref[...].astype(F32)
            do = du * _silu(mg)
            do_ref[...] = do.astype(BF16)
            dmg_ref[...] = (du * o * _silu_grad(mg)).astype(BF16)
            prod = do * o
            lane = lax.broadcasted_iota(jnp.int32, (1, HEADS * V_DIM), 1)
            for h in range(HEADS):
                part = jnp.where((lane >= V_DIM * h) & (lane < V_DIM * (h + 1)), prod, 0.0)
                delta_ref[h] = jnp.broadcast_to(jnp.sum(part, axis=-1, keepdims=True), (t, LANES))

    return pl.pallas_call(
        body, name="merge_bwd", grid=(rows // t,),
        in_specs=[cur(d), cur(256), cur(512), cur(256), cur(256), cur(ZG), layer((1, ZG)), _misc_spec(misc, 0, 1280), cur(512),
                  cur(512, MG // 512), ANY],
        out_specs=[cur(256), cur(512), cur(256), cur(256), cur(ZG), full((N_CHIPS, 1280, 256)), full((1, ZG)), cur(512, MG // 512),
                   pl.BlockSpec((HEADS, t, LANES), lambda i: (0, i, 0))],
        out_shape=[jax.ShapeDtypeStruct((rows, 256), F32), jax.ShapeDtypeStruct((rows, 512), BF16), jax.ShapeDtypeStruct((rows, 256), F32),
                   jax.ShapeDtypeStruct((rows, 256), F32), jax.ShapeDtypeStruct((rows, ZG), BF16),
                   jax.ShapeDtypeStruct((N_CHIPS, 1280, 256), F32), jax.ShapeDtypeStruct((1, ZG), F32),
                   jax.ShapeDtypeStruct((rows, ZB), BF16), jax.ShapeDtypeStruct((HEADS, rows, LANES), F32)],
        input_output_aliases={10: 7}, compiler_params=_params(("arbitrary",)),
    )(dm, ua, ub, uc, ud, z_gl, bias, misc[0], o_att, z_br, dz_buf)


def pool_shortconv_bwd(z_br, dua, dud, pwbd, pscale, sc_w, dz_buf, li):
    rows = z_br.shape[0]
    t = ROW_TILE
    n_steps = rows // t
    cur, prev, nxt, full, layer = _tile_specs(t, rows // HALO, li)

    def body(zc_ref, zp_ref, zn_ref, dac_ref, dan_ref, ddc_ref, ddn_ref, pw_ref, ps_ref, sw_ref, _, dz_ref, dpw_ref, dps_ref, dw_ref):
        i = pl.program_id(0)
        last = i == n_steps - 1
        zp = jnp.where(i == 0, jnp.zeros(zp_ref.shape, zp_ref.dtype), zp_ref[...])
        zn = jnp.where(last, jnp.zeros(zn_ref.shape, zn_ref.dtype), zn_ref[...])

        def ext(lo):
            return jnp.concatenate([zp[:, lo:lo + 256], zc_ref[:, lo:lo + 256], zn[:, lo:lo + 256]], axis=0).astype(F32)

        def ext_grad(c_ref, n_ref):
            return jnp.concatenate([jnp.zeros((HALO, 256), F32), c_ref[...], jnp.where(last, jnp.zeros(n_ref.shape, F32), n_ref[...])], axis=0)

        mid = slice(HALO, HALO + t)

        bg, c2, xv, sg = ext(BG), ext(C2), ext(XV), ext(SG)
        du = ext_grad(ddc_ref, ddn_ref)
        e = c2 * xv
        shifted = [_sh(e, SC_K - 1 - k) for k in range(SC_K)]
        f = sum(sw_ref[k:k + 1, :] * shifted[k] for k in range(SC_K))
        gate = _silu(sg)
        df = du * gate * bg
        de = sum(sw_ref[k:k + 1, :] * _ash(df, SC_K - 1 - k) for k in range(SC_K))
        d_sc = [(du * gate * f)[mid], (de * xv)[mid], (de * c2)[mid], (du * bg * f * _silu_grad(sg))[mid]]
        dw = jnp.concatenate([jnp.sum((df * shifted[k])[mid], axis=0, keepdims=True) for k in range(SC_K)] + [
            jnp.zeros((8 - SC_K, 256), F32)], axis=0)
        _accumulate(i, dw_ref, dw)

        v, pg = ext(PV), ext(PG)
        cnt = _pool_counts(i * t - HALO, t + 2 * HALO)
        p = (_pool_window_sums(v, _sh) / cnt - v)[mid]
        dya = ext_grad(dac_ref, dan_ref) * _silu(pg)
        dypb = (dya * ps_ref[...]).astype(BF16)
        dp = _dot_nt(dypb, pw_ref[...])
        dv = (_pool_window_sums(dp / cnt, _ash) - dp)[mid]
        pb = p.astype(BF16)
        pw = _dot(pb, pw_ref[...])
        dpg = dac_ref[...] * pw * ps_ref[...] * _silu_grad(pg[mid])
        _accumulate(i, dpw_ref, _dot_tn(pb, dypb[mid]))
        _accumulate(i, dps_ref, jnp.sum(dya[mid] * pw, axis=0, keepdims=True))

        dz_ref[...] = jnp.concatenate(d_sc + [dv, dpg], axis=1).astype(BF16)

    return pl.pallas_call(
        body, name="pool_shortconv_bwd", grid=(n_steps,),
        in_specs=[cur(ZB), prev(ZB), nxt(ZB), cur(256), nxt(256), cur(256), nxt(256), layer((256, 256)), layer((1, 256)), layer((8, 256)),
                  ANY],
        out_specs=[cur(1536, BG // 1536), full((256, 256)), full((1, 256)), full((8, 256))],
        out_shape=[jax.ShapeDtypeStruct((rows, ZB), BF16), jax.ShapeDtypeStruct((256, 256), F32), jax.ShapeDtypeStruct((1, 256), F32),
                   jax.ShapeDtypeStruct((8, 256), F32)],
        input_output_aliases={10: 0}, compiler_params=_params(("arbitrary",)),
    )(z_br, z_br, z_br, dua, dua, dud, dud, pwbd, pscale, sc_w, dz_buf)


def conformer_bwd_tail(z_br, duc, conf_w, conf_vec, dz_buf, li):
    rows = z_br.shape[0]
    t = ROW_TILE
    cur, prev, _, full, layer = _tile_specs(t, rows // HALO, li)

    def body(zc_ref, zp_ref, du_ref, cw_ref, cv_ref, _, dc_ref, dcg_ref, dv_ref):
        i = pl.program_id(0)
        zp = jnp.where(i == 0, jnp.zeros(zp_ref.shape, zp_ref.dtype), zp_ref[...])

        def ext(lo):
            return jnp.concatenate([zp[:, lo:lo + 256], zc_ref[:, lo:lo + 256]], axis=0).astype(F32)

        g1 = ext(CA) * _sigmoid(ext(CGT))
        c = _conf_conv(g1, cw_ref)[HALO:] + cv_ref[0:1, :]
        _, vjp = jax.vjp(_conf_tail, c, zc_ref[:, CG:CG + 256].astype(F32), cv_ref[1:2, :], cv_ref[2:3, :])
        dc, dcg, dlg, dlb = vjp(du_ref[...])
        dc_ref[...] = dc
        dcg_ref[...] = dcg.astype(BF16)
        dvec = jnp.concatenate([dlg, dlb, jnp.sum(dc, axis=0, keepdims=True), jnp.zeros((5, 256), F32)], axis=0)
        _accumulate(i, dv_ref, dvec)

    return pl.pallas_call(
        body, name="conformer_bwd_tail", grid=(rows // t,), in_specs=[cur(ZB), prev(ZB), cur(256), layer((32, 256)), layer((8, 256)), ANY],
        out_specs=[cur(256), cur(256, CG // 256), full((8, 256))],
        out_shape=[jax.ShapeDtypeStruct((rows, 256), F32), jax.ShapeDtypeStruct((rows, ZB), BF16), jax.ShapeDtypeStruct((8, 256), F32)],
        input_output_aliases={5: 1}, compiler_params=_params(("arbitrary",)),
    )(z_br, z_br, duc, conf_w, conf_vec, dz_buf)


def conformer_bwd_conv(z_br, dc, conf_w, dz_buf, li):
    rows = z_br.shape[0]
    t = ROW_TILE
    n_steps = rows // t
    cur, prev, nxt, full, layer = _tile_specs(t, rows // HALO, li)

    def body(zc_ref, zp_ref, dc_ref, dn_ref, cw_ref, _, dz_ref, dw_ref):
        i = pl.program_id(0)
        zp = jnp.where(i == 0, jnp.zeros(zp_ref.shape, zp_ref.dtype), zp_ref[...])
        dcn = jnp.where(i == n_steps - 1, jnp.zeros(dn_ref.shape, dn_ref.dtype), dn_ref[...])

        def ext(lo):
            return jnp.concatenate([zp[:, lo:lo + 256], zc_ref[:, lo:lo + 256]], axis=0).astype(F32)

        a, gt = ext(CA), ext(CGT)
        sg = _sigmoid(gt)
        g1 = a * sg
        dc = dc_ref[...]
        dce = jnp.concatenate([dc, dcn], axis=0)
        dg1 = jnp.zeros_like(dce)
        dws = []
        for k in range(CONF_K):
            dg1 = dg1 + cw_ref[k:k + 1, :] * _ash(dce, CONF_K - 1 - k)
            dws.append(jnp.sum(dc * _sh(g1, CONF_K - 1 - k)[HALO:], axis=0, keepdims=True))
        dg1 = dg1[:t]
        ac, sc = a[HALO:], sg[HALO:]
        dz_ref[...] = jnp.concatenate([dg1 * sc, dg1 * ac * sc * (1.0 - sc)], axis=1).astype(BF16)
        _accumulate(i, dw_ref, jnp.concatenate(dws + [jnp.zeros((32 - CONF_K, 256), F32)], axis=0))

    return pl.pallas_call(
        body, name="conformer_bwd_conv", grid=(n_steps,), in_specs=[cur(ZB), prev(ZB), cur(256), nxt(256), layer((32, 256)), ANY],
        out_specs=[cur(512, CA // 512), full((32, 256))],
        out_shape=[jax.ShapeDtypeStruct((rows, ZB), BF16), jax.ShapeDtypeStruct((32, 256), F32)],
        input_output_aliases={5: 0}, compiler_params=_params(("arbitrary",)),
    )(z_br, z_br, dc, dc, conf_w, dz_buf)


def attention_bwd(q, k, v, do, lse, delta, exchange=None):
    rows = q.shape[0]
    tq = ROW_TILE
    nq = rows // tq
    n = 0 if exchange is None else len(exchange[0])

    def body(*refs):
        if n:
            start, finish = _exchange_ops(refs[6:6 + n], refs[9 + 2 * n:9 + 3 * n], refs[9 + 3 * n:], exchange[2])
            pl.when((pl.program_id(0) == 0) & (pl.program_id(1) == 0))(start)
        compute(*refs[:6], *refs[6 + 2 * n:9 + 2 * n])
        if n:
            pl.when((pl.program_id(0) == HEADS // 2 - 1) & (pl.program_id(1) == nq - 1))(finish)

    def compute(q_ref, k_ref, v_ref, do_ref, lse_ref, dl_ref, dq_ref, dk_ref, dv_ref):
        j = pl.program_id(1)

        @pl.when(j == 0)
        def _():
            dq_ref[...] = jnp.zeros_like(dq_ref)

        def head_step(h, tile, n_tiles, dk, dv, diagonal):
            lanes = slice(HEAD_PAD * h, HEAD_PAD * (h + 1))
            hm = _head_lane_mask(h)
            kh = k_ref[:, lanes]
            vh = jnp.where(hm, v_ref[...], jnp.zeros((), BF16))
            r0, width = pl.multiple_of(tile * tq, tq), n_tiles * tq
            qi = q_ref[pl.ds(r0, width), lanes]
            doi = jnp.where(hm, do_ref[pl.ds(r0, width), :], jnp.zeros((), BF16))
            s = _dot_nt(qi, kh)
            if diagonal:
                s = jnp.where(lax.broadcasted_iota(jnp.int32, (tq, tq), 1) <= lax.broadcasted_iota(jnp.int32, (tq, tq), 0), s, -1e30)
            pr = jnp.exp(s - lse_ref[h, pl.ds(r0, width), :][:, 0:1])
            dv = dv + _dot_tn(pr.astype(BF16), doi)
            dp = _dot_nt(doi, vh)
            ds = (pr * (dp - dl_ref[h, pl.ds(r0, width), :][:, 0:1])).astype(BF16)
            dq_ref[pl.ds(r0, width), lanes] += _dot(ds, kh)
            return dk + _dot_tn(ds, qi), dv

        def step(tile, n_tiles, carry, diagonal):
            dk0, dk1, dv = carry
            dk0, dv = head_step(0, tile, n_tiles, dk0, dv, diagonal)
            dk1, dv = head_step(1, tile, n_tiles, dk1, dv, diagonal)
            return dk0, dk1, dv

        zero = jnp.zeros((tq, HEAD_PAD), F32)
        carry = step(j, 1, (zero, zero, jnp.zeros((tq, 2 * V_DIM), F32)), True)
        odd = (nq - 1 - j) % 2
        carry = lax.cond(odd == 1, lambda cr: step(j + 1, 1, cr, False), lambda cr: cr, carry)
        dk0, dk1, dv = lax.fori_loop(0, (nq - 1 - j) // 2, lambda t, cr: step(j + 1 + odd + 2 * t, 2, cr, False), carry)
        dk_ref[:, 0:HEAD_PAD] = dk0
        dk_ref[:, HEAD_PAD:2 * HEAD_PAD] = dk1
        dv_ref[...] = dv

    srcs, dsts = ([], []) if exchange is None else (list(exchange[0]), list(exchange[1]))
    outs = pl.pallas_call(
        body, name="attention_bwd" if exchange is None else "attention_bwd_exchange", grid=(HEADS // 2, nq),
        in_specs=[pl.BlockSpec((rows, 2 * HEAD_PAD), lambda p, j: (0, p)), pl.BlockSpec((tq, 2 * HEAD_PAD), lambda p, j: (j, p)),
                  pl.BlockSpec((tq, 2 * V_DIM), lambda p, j: (j, p)), pl.BlockSpec((rows, 2 * V_DIM), lambda p, j: (0, p)),
                  pl.BlockSpec((2, rows, LANES), lambda p, j: (p, 0, 0)), pl.BlockSpec((2, rows, LANES), lambda p, j: (p, 0, 0))] + [
                      ANY] * (2 * n),
        out_specs=[pl.BlockSpec((rows, 2 * HEAD_PAD), lambda p, j: (0, p)), pl.BlockSpec((tq, 2 * HEAD_PAD), lambda p, j: (j, p)),
                   pl.BlockSpec((tq, 2 * V_DIM), lambda p, j: (j, p))] + [ANY] * n,
        out_shape=[jax.ShapeDtypeStruct((rows, HEADS * HEAD_PAD), F32), jax.ShapeDtypeStruct((rows, HEADS * HEAD_PAD), F32),
                   jax.ShapeDtypeStruct((rows, HEADS * V_DIM), F32)] + [jax.ShapeDtypeStruct(d.shape, d.dtype) for d in dsts],
        input_output_aliases={6 + n + a: 3 + a for a in range(n)}, scratch_shapes=EXCHANGE_SEMS(n) if n else [],
        compiler_params=_params(("arbitrary", "arbitrary") if n else ("parallel", "arbitrary")),
    )(q, k, v, do, lse, delta, *srcs, *dsts)
    return outs[0], outs[1], outs[2], list(outs[3:])


def mla_prep_bwd(dq, dk, dv, z_br, rope, gq, gkv, misc, dz_buf, li):
    rows = dq.shape[0]
    t = ROW_TILE
    cur, _, _, full, layer = _tile_specs(t, rows // HALO, li)
    w8 = HEADS * HEAD_PAD
    uq, keys, values = slice(0, 256), slice(M_UKVK - M_UQ, M_UKVV - M_UQ), slice(M_UKVV - M_UQ, M_WO - M_UQ)

    def body(dq_ref, dk_ref, dv_ref, z_ref, rope_ref, gq_ref, gkv_ref, up_ref, _, dz_ref, dup_ref, dgq_ref, dgkv_ref):
        i = pl.program_id(0)
        cth, s1, s2 = rope_ref[:, 0:128], rope_ref[:, 128:256], rope_ref[:, 256:384]
        dqb = _rope_transposed(dq_ref[...] * Q_SCALE, _lanes8(cth), _lanes8(s1), _lanes8(s2), w8).astype(BF16)
        dq_chunks = _chunks(dqb)
        cq = z_ref[:, 0:256].astype(F32)
        qn, vjp_q = jax.vjp(_rms, cq, gq_ref[...])
        dcq, dgq = vjp_q(sum(_dot_nt(dqk, up_ref[k, uq, :]) for k, dqk in enumerate(dq_chunks)))
        _accumulate(i, dgq_ref, dgq)

        dk = dk_ref[...]
        dkr = sum(dk[:, HEAD_PAD * h:HEAD_PAD * (h + 1)] for h in range(HEADS))
        dkr = _rope_transposed(dkr, cth, s1, s2, HEAD_PAD)
        lane = lax.broadcasted_iota(jnp.int32, (1, HEAD_PAD), 1)
        dkr = jnp.where((lane >= QK_NOPE) & (lane < QK_NOPE + QK_ROPE), dkr, 0.0)
        dkb, dvb = dk.astype(BF16), dv_ref[...].astype(BF16)
        dk_chunks, dv_chunks = _chunks(dkb), _chunks(dvb, width=2 * V_DIM)
        ckv = z_ref[:, 256:384].astype(F32)
        kvn, vjp_kv = jax.vjp(_rms, ckv, gkv_ref[...])
        dckv, dgkv = vjp_kv(sum(_dot_nt(dk_chunks[k], up_ref[k, keys, :]) + _dot_nt(dv_chunks[k], up_ref[k, values, 0:2 * V_DIM])
                                for k in range(N_CHIPS)))
        _accumulate(i, dgkv_ref, dgkv)
        dz_ref[...] = jnp.concatenate([dcq, dckv, dkr], axis=1).astype(BF16)
        qnb, kvnb = qn.astype(BF16), kvn.astype(BF16)
        d_uq, d_keys, d_values = _chunks(_dot_tn(qnb, dqb)), _chunks(_dot_tn(kvnb, dkb)), _chunks(_dot_tn(kvnb, dvb), width=2 * V_DIM)
        for k in range(N_CHIPS):
            padded = jnp.concatenate([d_values[k], jnp.zeros((128, 256 - 2 * V_DIM), F32)], axis=1)
            _accumulate(i, dup_ref.at[k], jnp.concatenate([d_uq[k], d_keys[k], padded], axis=0))

    return pl.pallas_call(
        body, name="mla_prep_bwd", grid=(rows // t,),
        in_specs=[cur(w8), cur(w8), cur(512), cur(512, CQ // 512), cur(384), layer((1, 256)), layer((1, 128)),
                  _misc_spec(misc, M_UQ, M_WO - M_UQ), ANY],
        out_specs=[cur(512, CQ // 512), full((N_CHIPS, M_WO - M_UQ, 256)), full((1, 256)), full((1, 128))],
        out_shape=[jax.ShapeDtypeStruct((rows, ZB), BF16), jax.ShapeDtypeStruct((N_CHIPS, M_WO - M_UQ, 256), F32),
                   jax.ShapeDtypeStruct((1, 256), F32), jax.ShapeDtypeStruct((1, 128), F32)],
        input_output_aliases={8: 0}, compiler_params=_params(("arbitrary",)),
    )(dq, dk, dv, z_br, rope, gq, gkv, misc[0], dz_buf)


def prenorm_bwd(dz_br, w_br, dh_gl, hres, gpre, dh_next, li):
    rows, d = hres.shape
    t = ROW_TILE
    cur, _, _, full, layer = _tile_specs(t, rows // HALO, li)

    def body(dz_ref, w_ref, dp_ref, x_ref, g_ref, dn_ref, dx_ref, dg_ref):
        i = pl.program_id(0)
        dh = _dot_nt(dz_ref[...], w_ref[...]) + dp_ref[...]
        _, vjp = jax.vjp(_rms, x_ref[...], g_ref[...])
        dx, dg = vjp(dh)
        dx_ref[...] = dx + dn_ref[...]
        _accumulate(i, dg_ref, dg)

    return pl.pallas_call(
        body, name="prenorm_bwd", grid=(rows // t,), in_specs=[cur(ZB), layer((d, ZB), 0), cur(d), cur(d), layer((1, d)), cur(d)],
        out_specs=[cur(d), full((1, d))], out_shape=[jax.ShapeDtypeStruct((rows, d), F32), jax.ShapeDtypeStruct((1, d), F32)],
        compiler_params=_params(("arbitrary",)),
    )(dz_br, w_br, dh_gl, hres, gpre, dh_next)


def _mesh_position():
    return lax.axis_index("x"), lax.axis_index("y"), lax.axis_index("c")


def chip_exchange(src, gather, name):
    block = src.shape if gather else src.shape[1:]

    def body(src_ref, dst_ref, send_sems, recv_sems, local_sem):
        x, y, c = _mesh_position()
        me = 2 * x + y
        peers = ((1 - x, y), (x, 1 - y), (1 - x, 1 - y))

        def part(k):
            return src_ref if gather else src_ref.at[k]

        def copy(j, slot):
            px, py = peers[j]
            return pltpu.make_async_remote_copy(src_ref=part(2 * px + py), dst_ref=dst_ref.at[slot], send_sem=send_sems.at[j],
                                                recv_sem=recv_sems.at[j], device_id=(px, py, c), device_id_type=MESH)

        local = pltpu.make_async_copy(part(me), dst_ref.at[me], local_sem)
        local.start()
        sends = [copy(j, me) for j in range(3)]
        for cp in sends:
            cp.start()
        for j, (px, py) in enumerate(peers):
            copy(j, 2 * px + py).wait_recv()
        for cp in sends:
            cp.wait_send()
        local.wait()

    return pl.pallas_call(
        body, name=name, in_specs=[pl.BlockSpec(memory_space=pl.ANY)], out_specs=pl.BlockSpec(memory_space=pl.ANY),
        out_shape=jax.ShapeDtypeStruct((N_CHIPS,) + tuple(block), src.dtype),
        scratch_shapes=[pltpu.SemaphoreType.DMA((3,)), pltpu.SemaphoreType.DMA((3,)), pltpu.SemaphoreType.DMA(())],
    )(src)


def sibling_swap(src, name):
    def body(src_ref, dst_ref, send_sem, recv_sem):
        x, y, c = _mesh_position()
        cp = pltpu.make_async_remote_copy(src_ref=src_ref, dst_ref=dst_ref, send_sem=send_sem, recv_sem=recv_sem,
                                          device_id=(x, y, 1 - c), device_id_type=MESH)
        cp.start()
        cp.wait()

    return pl.pallas_call(
        body, name=name, in_specs=[pl.BlockSpec(memory_space=pl.ANY)], out_specs=pl.BlockSpec(memory_space=pl.ANY),
        out_shape=jax.ShapeDtypeStruct(src.shape, src.dtype),
        scratch_shapes=[pltpu.SemaphoreType.DMA(()), pltpu.SemaphoreType.DMA(())],
    )(src)


def _comm_call(body, name, n_in, out_shapes, n_sems):
    return pl.pallas_call(
        body, name=name, in_specs=[ANY] * n_in, out_specs=[ANY] * len(out_shapes), out_shape=out_shapes,
        scratch_shapes=[pltpu.SemaphoreType.DMA((n,)) for n in n_sems])


def _row_halves(c, rows):
    half = rows // 2
    return pl.ds(pl.multiple_of(c * half, 16), half), pl.ds(pl.multiple_of((1 - c) * half, 16), half)


def _peers():
    x, y, c = _mesh_position()
    return x, y, c, 2 * x + y, ((1 - x, y), (x, 1 - y), (1 - x, 1 - y))


def _gather_ops(src, dst, sems, layer, own_copy):
    ici_send, ici_recv, d2d_send, d2d_recv, own_sems = sems
    n = len(src)

    def fetch(a, j, slot):
        x, y, c, _, peers = _peers()
        px, py = peers[j]
        mine, _ = _row_halves(c, src[a].shape[1])
        return pltpu.make_async_remote_copy(src_ref=src[a].at[layer, mine], dst_ref=dst[a].at[layer, slot, mine], send_sem=ici_send.at[3 * a + j],
                                            recv_sem=ici_recv.at[3 * a + j], device_id=(px, py, c), device_id_type=MESH)

    def forward(a, j, sibling_half):
        x, y, c, _, peers = _peers()
        px, py = peers[j]
        part = dst[a].at[layer, 2 * px + py, _row_halves(c, src[a].shape[1])[1 if sibling_half else 0]]
        return pltpu.make_async_remote_copy(src_ref=part, dst_ref=part, send_sem=d2d_send.at[3 * a + j], recv_sem=d2d_recv.at[3 * a + j],
                                            device_id=(x, y, 1 - c), device_id_type=MESH)

    def own(a):
        return pltpu.make_async_copy(src[a].at[layer], dst[a].at[layer, _peers()[3]], own_sems.at[a])

    def start():
        me = _peers()[3]
        for a in range(n):
            if own_copy:
                own(a).start()
            for j in range(3):
                fetch(a, j, me).start()

    def finish():
        peers = _peers()[4]
        for j, (px, py) in enumerate(peers):
            for a in range(n):
                fetch(a, j, 2 * px + py).wait_recv()
                forward(a, j, False).start()
        for j in range(3):
            for a in range(n):
                forward(a, j, True).wait_recv()
        for j in range(3):
            for a in range(n):
                fetch(a, j, 0).wait_send()
                forward(a, j, False).wait_send()
        if own_copy:
            for a in range(n):
                own(a).wait()

    return start, finish


def _exchange_ops(src, dst, sems, layer):
    send_sems, recv_sems, own_sems = sems
    n = len(src)

    def copy(a, j, slot):
        x, y, c, _, peers = _peers()
        px, py = peers[j]
        return pltpu.make_async_remote_copy(src_ref=src[a].at[2 * px + py], dst_ref=dst[a].at[layer, slot], send_sem=send_sems.at[3 * a + j],
                                            recv_sem=recv_sems.at[3 * a + j], device_id=(px, py, c), device_id_type=MESH)

    def own(a):
        me = _peers()[3]
        return pltpu.make_async_copy(src[a].at[me], dst[a].at[layer, me], own_sems.at[a])

    def start():
        me = _peers()[3]
        for a in range(n):
            own(a).start()
            for j in range(3):
                copy(a, j, me).start()

    def finish():
        peers = _peers()[4]
        for j, (px, py) in enumerate(peers):
            for a in range(n):
                copy(a, j, 2 * px + py).wait_recv()
        for j in range(3):
            for a in range(n):
                copy(a, j, 0).wait_send()
        for a in range(n):
            own(a).wait()

    return start, finish


GATHER_SEMS = lambda n: [pltpu.SemaphoreType.DMA((3 * n,))] * 4 + [pltpu.SemaphoreType.DMA((n,))]
EXCHANGE_SEMS = lambda n: [pltpu.SemaphoreType.DMA((3 * n,))] * 2 + [pltpu.SemaphoreType.DMA((n,))]


def gather_layer(srcs, dsts, layer, name):
    n = len(srcs)

    def body(*refs):
        start, finish = _gather_ops(refs[:n], refs[2 * n:3 * n], refs[3 * n:], layer, False)
        start()
        finish()

    return pl.pallas_call(
        body, name=name, in_specs=[ANY] * (2 * n), out_specs=[ANY] * n, out_shape=[jax.ShapeDtypeStruct(d.shape, d.dtype) for d in dsts],
        input_output_aliases={n + a: a for a in range(n)}, scratch_shapes=GATHER_SEMS(n),
    )(*srcs, *dsts)


def exchange_layer(ss, dsts, layer, name):
    n = len(ss)

    def body(*refs):
        start, finish = _exchange_ops(refs[:n], refs[2 * n:3 * n], refs[3 * n:], layer)
        start()
        finish()

    return pl.pallas_call(
        body, name=name, in_specs=[ANY] * (2 * n), out_specs=[ANY] * n, out_shape=[jax.ShapeDtypeStruct(d.shape, d.dtype) for d in dsts],
        input_output_aliases={n + a: a for a in range(n)}, scratch_shapes=EXCHANGE_SEMS(n),
    )(*ss, *dsts)


def _swap_ops(src, dst, sems):
    send_sems, recv_sems = sems

    def copy(a):
        x, y, c = _mesh_position()
        return pltpu.make_async_remote_copy(src_ref=src[a].at[:, _row_halves(c, src[a].shape[1])[1]], dst_ref=dst[a], send_sem=send_sems.at[a],
                                            recv_sem=recv_sems.at[a], device_id=(x, y, 1 - c), device_id_type=MESH)

    def start():
        for a in range(len(src)):
            copy(a).start()

    def finish():
        for a in range(len(src)):
            copy(a).wait()

    return start, finish


def swap_row_halves(ps, name):
    n = len(ps)

    def body(*refs):
        start, finish = _swap_ops(refs[:n], refs[n:2 * n], refs[2 * n:])
        start()
        finish()

    outs = [jax.ShapeDtypeStruct((p.shape[0], p.shape[1] // 2, p.shape[2]), p.dtype) for p in ps]
    return _comm_call(body, name, n, outs, (n, n))(*ps)


def add_row_half(p, r, c, name):
    n, half, cols = r.shape
    rb = _row_block(half, cols, 2)
    steps = half // rb

    def body(c_ref, p_ref, r_ref, o_ref):
        o_ref[...] = (p_ref[...].astype(F32) + r_ref[...].astype(F32)).astype(BF16)

    return pl.pallas_call(
        body, name=name, out_shape=jax.ShapeDtypeStruct(r.shape, BF16),
        grid_spec=pltpu.PrefetchScalarGridSpec(
            num_scalar_prefetch=1, grid=(n, steps),
            in_specs=[pl.BlockSpec((1, rb, cols), lambda k, i, c_ref: (k, c_ref[0] * steps + i, 0)),
                      pl.BlockSpec((1, rb, cols), lambda k, i, c_ref: (k, i, 0))],
            out_specs=pl.BlockSpec((1, rb, cols), lambda k, i, c_ref: (k, i, 0))),
        compiler_params=_params(("parallel", "parallel")),
    )(jnp.reshape(c, (1,)).astype(jnp.int32), p, r)


def sum_row_halves(l, c, name):
    layers, n, half, cols = l.shape
    rb = _row_block(half, cols, 4)
    steps = half // rb

    def body(c_ref, l_ref, o_ref):
        acc = l_ref[0, 0].astype(F32)
        for s in range(1, n):
            acc = acc + l_ref[0, s].astype(F32)
        o_ref[0] = acc

    return pl.pallas_call(
        body, name=name, out_shape=jax.ShapeDtypeStruct((layers, 2 * half, cols), F32),
        grid_spec=pltpu.PrefetchScalarGridSpec(
            num_scalar_prefetch=1, grid=(layers, steps), in_specs=[pl.BlockSpec((1, n, rb, cols), lambda a, i, c_ref: (a, 0, i, 0))],
            out_specs=pl.BlockSpec((1, rb, cols), lambda a, i, c_ref: (a, c_ref[0] * steps + i, 0))),
        compiler_params=_params(("parallel", "parallel")),
    )(jnp.reshape(c, (1,)).astype(jnp.int32), l)


def share_row_halves(gs, name):
    n = len(gs)

    def body(*refs):
        dst = refs[n:2 * n]
        send_sems, recv_sems = refs[2 * n:]
        x, y, c = _mesh_position()

        def copy(a, sibling_half):
            part = dst[a].at[:, _row_halves(c, dst[a].shape[1])[1 if sibling_half else 0]]
            return pltpu.make_async_remote_copy(src_ref=part, dst_ref=part, send_sem=send_sems.at[a], recv_sem=recv_sems.at[a],
                                                device_id=(x, y, 1 - c), device_id_type=MESH)

        for a in range(n):
            copy(a, False).start()
        for a in range(n):
            copy(a, True).wait_recv()
        for a in range(n):
            copy(a, False).wait_send()

    return pl.pallas_call(
        body, name=name, in_specs=[ANY] * n, out_specs=[ANY] * n, out_shape=[jax.ShapeDtypeStruct(g.shape, g.dtype) for g in gs],
        input_output_aliases={a: a for a in range(n)}, scratch_shapes=[pltpu.SemaphoreType.DMA((n,)), pltpu.SemaphoreType.DMA((n,))],
    )(*gs)


def _row_block(rows, cols, itemsize):
    best = 16
    for rb in range(16, rows + 1, 16):
        if rows % rb == 0 and rb * cols * itemsize <= 2 * 1024 * 1024:
            best = rb
    assert rows % best == 0, (rows, cols)
    return best


def _comm_block(rows):
    return 1024 if rows % 1024 == 0 else rows


def sum_slots(buf, name):
    n, r, c = buf.shape
    rb = _comm_block(r)

    def body(b_ref, o_ref):
        acc = b_ref[0].astype(F32)
        for s in range(1, n):
            acc = acc + b_ref[s].astype(F32)
        o_ref[...] = acc

    return pl.pallas_call(
        body, name=name, grid=(r // rb,), in_specs=[pl.BlockSpec((n, rb, c), lambda i: (0, i, 0))],
        out_specs=pl.BlockSpec((rb, c), lambda i: (i, 0)), out_shape=jax.ShapeDtypeStruct((r, c), F32),
        compiler_params=_params(("parallel",)),
    )(buf)


def add_pair(a, b, out_dtype, name):
    shape = a.shape
    a2, b2 = a.reshape(-1, shape[-1]), b.reshape(-1, shape[-1])
    r, c = a2.shape
    rb = _comm_block(r)

    def body(a_ref, b_ref, o_ref):
        o_ref[...] = (a_ref[...].astype(F32) + b_ref[...].astype(F32)).astype(out_dtype)

    out = pl.pallas_call(
        body, name=name, grid=(r // rb,), in_specs=[pl.BlockSpec((rb, c), lambda i: (i, 0))] * 2,
        out_specs=pl.BlockSpec((rb, c), lambda i: (i, 0)), out_shape=jax.ShapeDtypeStruct((r, c), out_dtype),
        compiler_params=_params(("parallel",)),
    )(a2, b2)
    return out.reshape(shape)


def adamw(w, g, m, v):
    shape = w.shape
    cols = shape[-1]
    rows = math.prod(shape[:-1])
    if rows * cols <= 256 * 1024:
        rb, cb = rows, cols
    else:
        rb = max(r for r in range(8, 2049, 8) if rows % r == 0)
        cb = cols if rb * cols * 4 <= 2 * 1024 * 1024 else 256
    assert rows % rb == 0 and cols % cb == 0, shape

    def body(w_ref, g_ref, m_ref, v_ref, d_ref, nm_ref, nv_ref):
        g_ = g_ref[...]
        nm = ADAM_B1 * m_ref[...] + (1.0 - ADAM_B1) * g_
        nv = ADAM_B2 * v_ref[...] + (1.0 - ADAM_B2) * (g_ * g_)
        m_hat = nm / (1.0 - ADAM_B1 ** ADAM_STEP)
        v_hat = nv / (1.0 - ADAM_B2 ** ADAM_STEP)
        d_ref[...] = -ADAM_LR * (m_hat / (jnp.sqrt(v_hat) + ADAM_EPS) + ADAM_WD * w_ref[...])
        nm_ref[...] = nm
        nv_ref[...] = nv

    spec = pl.BlockSpec((rb, cb), lambda i, j: (i, j))
    outs = pl.pallas_call(
        body, name="adamw", grid=(rows // rb, cols // cb), in_specs=[spec] * 4, out_specs=[spec] * 3,
        out_shape=[jax.ShapeDtypeStruct((rows, cols), F32)] * 3, compiler_params=_params(("parallel", "parallel")),
    )(*(a.reshape(rows, cols) for a in (w, g, m, v)))
    return tuple(o.reshape(shape) for o in outs)


def _pack(arrays, dtype, row_multiple):
    flat = jnp.concatenate([a.astype(dtype).reshape(-1) for a in arrays])
    per = LANES * row_multiple
    total = -(-flat.shape[0] // per) * per
    return jnp.pad(flat, (0, total - flat.shape[0])).reshape(total // LANES, LANES)


def _unpack(buf, shapes):
    flat = buf.reshape(-1)
    out, off = [], 0
    for s in shapes:
        n = math.prod(s)
        out.append(flat[off:off + n].reshape(s))
        off += n
    return out


def _input_weights(blocks):
    c0, c1, c2, c3 = (blocks[..., k, :, :] for k in range(N_CHIPS))
    pad = lambda n: jnp.zeros(c0.shape[:-1] + (n,), blocks.dtype)
    w_br = jnp.concatenate([c1[..., 376:1400], c0[..., 0:896], pad(64), c0[..., 896:928], pad(32), c0[..., 928:], c1[..., 0:376]], axis=-1)
    return w_br, jnp.concatenate([c1[..., 1400:], c2, c3], axis=-1)


def _input_weights_inverse(dw_br, dw_gl):
    c0 = jnp.concatenate([dw_br[..., 1024:1920], dw_br[..., 1984:2016], dw_br[..., 2048:2952]], axis=-1)
    c1 = jnp.concatenate([dw_br[..., 2952:ZB], dw_br[..., 0:1024], dw_gl[..., 0:432]], axis=-1)
    return jnp.stack([c0, c1, dw_gl[..., 432:2264], dw_gl[..., 2264:]], axis=-3)


def _block_diag(pw):
    zeros = lambda n: jnp.zeros(pw.shape[:-3] + (64, n), pw.dtype)
    rows = [jnp.concatenate([zeros(64 * g), pw[..., g, :, :], zeros(64 * (3 - g))], axis=-1) for g in range(4)]
    return jnp.concatenate(rows, axis=-2)


def _block_diag_inverse(d):
    return jnp.stack([d[..., 64 * g:64 * (g + 1), 64 * g:64 * (g + 1)] for g in range(4)], axis=-3)


def _pad_rows(a, n):
    return jnp.pad(a, ((0, n - a.shape[0]), (0, 0)))


def _rope_tables(rows):
    inv = 1.0 / (ROPE_THETA ** (jnp.arange(0, QK_ROPE, 2, dtype=F32) / QK_ROPE))
    ang = jnp.arange(rows, dtype=F32)[:, None] * inv[None, :]
    cos, sin = jnp.cos(ang), jnp.sin(ang)
    one, zero = jnp.ones((rows, 1), F32), jnp.zeros((rows, 1), F32)
    rep = lambda a, n: jnp.broadcast_to(a, (rows, n))
    c = jnp.concatenate([rep(one, 64), cos, cos, rep(one, 32)], axis=1)
    s1 = jnp.concatenate([rep(zero, 64), -sin, rep(zero, 48)], axis=1)
    s2 = jnp.concatenate([rep(zero, 80), sin, rep(zero, 32)], axis=1)
    return jnp.concatenate([c, s1, s2], axis=1)


def _misc_block(parts):
    lead = parts["w_uq"].shape[:-2]
    pad_last = lambda a, n: jnp.pad(a, [(0, 0)] * (a.ndim - 1) + [(0, n - a.shape[-1])])
    uq = pad_last(parts["w_uq"].reshape(lead + (256, 2, QK_NOPE + QK_ROPE)), HEAD_PAD).reshape(lead + (256, 256))
    kv = parts["w_ukv"].reshape(lead + (128, 2, QK_NOPE + V_DIM))
    keys = pad_last(kv[..., :QK_NOPE], HEAD_PAD).reshape(lead + (128, 256))
    values = pad_last(kv[..., QK_NOPE:].reshape(lead + (128, 2 * V_DIM)), 256)
    wo = jnp.swapaxes(parts["w_o"].reshape(lead + (256, N_CHIPS, 256)), -3, -2).reshape(lead + (D_MODEL, 256))
    gap = jnp.zeros(lead + (M_UQ - M_SC - 256, 256), uq.dtype)
    return jnp.concatenate([parts["w_out_mla"], parts["w_out_pool"], parts["w_out_conf"], parts["w_out_sc"], gap, uq, keys, values, wo],
                           axis=-2)


def _misc_unblock(block):
    lead = block.shape[:-2]
    rows = lambda lo, n: block[..., lo:lo + n, :]
    uq = rows(M_UQ, 256).reshape(lead + (256, 2, HEAD_PAD))[..., :QK_NOPE + QK_ROPE].reshape(lead + (256, 2 * (QK_NOPE + QK_ROPE)))
    keys = rows(M_UKVK, 128).reshape(lead + (128, 2, HEAD_PAD))[..., :QK_NOPE]
    values = rows(M_UKVV, 128)[..., :2 * V_DIM].reshape(lead + (128, 2, V_DIM))
    wo = jnp.swapaxes(rows(M_WO, D_MODEL).reshape(lead + (N_CHIPS, 256, 256)), -3, -2).reshape(lead + (256, D_MODEL))
    return dict(w_out_mla=rows(M_MLA, 512), w_out_pool=rows(M_POOL, 256), w_out_conf=rows(M_CONF, 256), w_out_sc=rows(M_SC, 256), w_uq=uq,
                w_ukv=jnp.concatenate([keys, values], axis=-1).reshape(lead + (128, 256)), w_o=wo)


def _to_chip_blocks(name, a):
    if name == "w_o":
        return a.reshape(a.shape[:-2] + (N_CHIPS, a.shape[-2] // N_CHIPS, a.shape[-1]))
    return jnp.swapaxes(a.reshape(a.shape[:-1] + (N_CHIPS, a.shape[-1] // N_CHIPS)), -3, -2)


def _from_chip_blocks(name, b):
    if name == "w_o":
        return b.reshape(b.shape[:-3] + (N_CHIPS * b.shape[-2], b.shape[-1]))
    s = jnp.swapaxes(b, -3, -2)
    return s.reshape(s.shape[:-2] + (N_CHIPS * s.shape[-1],))


LARGE = ("w_in",) + MISC


def gather_small(shards):
    small = chip_exchange(_pack([shards[n] for n, _, _ in SHARDED_SMALL], F32, 8), True, "gather_small_ici")
    per_chip = [_unpack(small[k], [s for _, s, _ in SHARDED_SMALL]) for k in range(N_CHIPS)]
    return {name: jnp.concatenate([per_chip[k][idx] for k in range(N_CHIPS)], axis=axis) for idx, (name, _, axis) in enumerate(SHARDED_SMALL)}


class LocalWeights:
    def __init__(self, full):
        self.w_in = _to_chip_blocks("w_in", full["w_in"])
        self.misc = _misc_block({n: _to_chip_blocks(n, full[n]) for n in MISC}).astype(BF16)
        self.grads = [None] * DEPTH

    def layer(self, i):
        return self.w_in[i], (self.misc, i)

    def gather_with_attention(self, i):
        return None

    def gathered(self, dsts):
        pass

    def exchange_with_attention(self):
        return None

    def exchanged(self, dsts):
        pass

    def swap_with_postnorm(self):
        return None

    def swapped(self, rs):
        pass

    def put_grads(self, i, w_in, misc):
        self.grads[i] = (w_in, misc)

    def reduced(self):
        out = {n: _from_chip_blocks(n, b) for n, b in _misc_unblock(jnp.stack([m for _, m in self.grads])).items()}
        out["w_in"] = _from_chip_blocks("w_in", jnp.stack([w for w, _ in self.grads]))
        return out


class MeshWeights:
    def __init__(self, shards, c, chip):
        self.c, self.chip = c, chip
        self.srcs = [shards["w_in"].astype(BF16), _misc_block({n: shards[n] for n in MISC}).astype(BF16)]
        dsts = [lax.empty((DEPTH, N_CHIPS) + s.shape[1:], BF16) for s in self.srcs]
        self.dsts = gather_layer(self.srcs, dsts, 0, "gather_layer")
        self.landed = [lax.empty((DEPTH, N_CHIPS, s.shape[1] // 2, s.shape[2]), BF16) for s in self.srcs]
        self.pending = self.to_swap = None

    def layer(self, i):
        if i > 0:
            return self.dsts[0][i], (self.dsts[1], i)
        own = (jnp.arange(N_CHIPS) == self.chip)[:, None, None]
        w_in, misc = (jnp.where(own, s[0][None], d[0]) for s, d in zip(self.srcs, self.dsts))
        return w_in, (misc[None], 0)

    def gather_with_attention(self, i):
        return (self.srcs, self.dsts, i + 1) if i + 1 < DEPTH else None

    def gathered(self, dsts):
        if dsts:
            self.dsts = dsts

    def exchange_with_attention(self):
        return None if self.pending is None else (self.pending[0], self.landed, self.pending[1])

    def exchanged(self, dsts):
        if dsts:
            self.landed, self.pending = dsts, None

    def put_grads(self, i, w_in, misc):
        self.to_swap = ([w_in.astype(BF16), misc.astype(BF16)], i)

    def swap_with_postnorm(self):
        return None if self.to_swap is None else self.to_swap[0]

    def swapped(self, rs):
        if rs:
            ps, i = self.to_swap
            self.pending = ([add_row_half(p, r, self.c, "reduce_pair_%d" % a) for a, (p, r) in enumerate(zip(ps, rs))], i)
            self.to_swap = None

    def reduced(self):
        self.swapped(swap_row_halves(self.to_swap[0], "reduce_swap"))
        landed = exchange_layer(self.pending[0], self.landed, self.pending[1], "reduce_exchange")
        gs = [sum_row_halves(l, self.c, "reduce_sum_%d" % a) for a, l in enumerate(landed)]
        g_in, g_misc = share_row_halves(gs, "reduce_share")
        out = {"w_in": g_in}
        out.update(_misc_unblock(g_misc))
        return out


def reduce_small(grads, chip):
    names = [n for n, _ in REPLICATED] + [n for n, _, _ in SHARDED_SMALL]
    buf = _pack([grads[n] for n in names], F32, 8)
    chip_sum = add_pair(buf, sibling_swap(buf, "reduce_small_d2d"), F32, "reduce_small_pair")
    total = sum_slots(chip_exchange(chip_sum, True, "reduce_small_ici"), "reduce_small_sum")
    out = dict(zip(names, _unpack(total, [grads[n].shape for n in names])))
    for name, shape, axis in SHARDED_SMALL:
        out[name] = lax.dynamic_slice_in_dim(out[name], chip * shape[axis], shape[axis], axis)
    return out


def _prepare_small(w):
    row = lambda a: a[:, None, :]
    conf_vec = jnp.concatenate([row(w["conf_dw_b"]), row(w["conf_ln_g"]), row(w["conf_ln_b"]), jnp.zeros((DEPTH, 5, 256), F32)], axis=1)
    return dict(
        gpre=row(w["pre_norm_g"]), bias=row(w["gate_bias"]), pwbd=_block_diag(w["pool_w"]).astype(BF16), pscale=row(w["pool_scale"]),
        gq=row(w["q_norm_g"]), gkv=row(w["kv_norm_g"]), conf_w=jnp.pad(w["conf_dw_w"].astype(F32), ((0, 0), (0, 32 - CONF_K), (0, 0))),
        conf_vec=conf_vec, sc_w=jnp.pad(w["sc_dw_w"].astype(F32), ((0, 0), (0, 8 - SC_K), (0, 0))), gpost=row(w["post_norm_g"]))


def _prepare_layer(w_in_blocks, misc):
    w_br, w_gl = _input_weights(w_in_blocks)
    one = lambda a: a.astype(BF16)[None]
    return dict(w_br=one(w_br), w_gl=one(w_gl), misc=misc)


def local_step(x, target, w, large):
    seq = x.shape[0]
    length = N_META + seq
    rows = -(-length // ROW_TILE) * ROW_TILE
    bt = _big_tile(rows)
    hres = _pad_rows(jnp.concatenate([w["meta_tokens"].astype(F32), x], axis=0), rows)
    tgt = jnp.pad(target, ((N_META, rows - length), (0, 0)))
    rope = _rope_tables(rows)
    sw = _prepare_small(w)

    saved = []
    for i in range(DEPTH):
        lw = _prepare_layer(*large.layer(i))
        z_br, hb, hbt = prenorm_project(hres, sw["gpre"], lw["w_br"], i)
        z_gl = matmul(hb, lw["w_gl"], "nn", BF16, bt, 1024, D_MODEL, "project_gates", b_layer=0)
        ua, uc, ud, q, k, v = branches_fwd(z_br, rope, sw["pwbd"], sw["pscale"], sw["gq"], sw["gkv"], lw["misc"], sw["conf_w"],
                                           sw["conf_vec"], sw["sc_w"], i)
        o_att, lse, dsts = attention_fwd(q, k, v, large.gather_with_attention(i))
        large.gathered(dsts)
        ub, mb, o, hnew = merge_fwd(ua, o_att, uc, ud, z_br, z_gl, sw["bias"], lw["misc"], sw["gpost"], hres, i)
        saved.append(dict(lw=lw, hres=hres, hbt=hbt, z_br=z_br, z_gl=z_gl, ua=ua, ub=ub, uc=uc, ud=ud, q=q, k=k, v=v, o_att=o_att,
                          lse=lse, mb=mb, o=o))
        hres = hnew

    dh, total = loss_head(hres, tgt, seq)

    g = {n: [None] * DEPTH for n in ("gpre", "bias", "pwbd", "pscale", "gq", "gkv", "conf_w", "conf_vec", "sc_w", "gpost")}
    for i in reversed(range(DEPTH)):
        s = saved[i]
        lw = s["lw"]
        dm, dwo, g["gpost"][i], rs = postnorm_bwd(dh, s["o"], s["mb"], lw["misc"], sw["gpost"], i, large.swap_with_postnorm())
        large.swapped(rs)
        dz_br = lax.empty((rows, ZB), BF16)
        dua, do, duc, dud, dz_gl, dwout, g["bias"][i], dz_br, delta = merge_bwd(
            dm, s["ua"], s["ub"], s["uc"], s["ud"], s["z_gl"], sw["bias"], lw["misc"], s["o_att"], s["z_br"], dz_br, i)
        dz_br, g["pwbd"][i], g["pscale"][i], g["sc_w"][i] = pool_shortconv_bwd(s["z_br"], dua, dud, sw["pwbd"], sw["pscale"], sw["sc_w"],
                                                                              dz_br, i)
        dc, dz_br, g["conf_vec"][i] = conformer_bwd_tail(s["z_br"], duc, sw["conf_w"], sw["conf_vec"], dz_br, i)
        dz_br, g["conf_w"][i] = conformer_bwd_conv(s["z_br"], dc, sw["conf_w"], dz_br, i)
        dq, dk, dv, dsts = attention_bwd(s["q"], s["k"], s["v"], do, s["lse"], delta, large.exchange_with_attention())
        large.exchanged(dsts)
        dz_br, dwup, g["gq"][i], g["gkv"][i] = mla_prep_bwd(dq, dk, dv, s["z_br"], rope, sw["gq"], sw["gkv"], lw["misc"], dz_br, i)
        dw_br = matmul(s["hbt"], dz_br, "nn", BF16, D_MODEL, ZB // 2, bt, "grad_w_branch")
        dw_gl = matmul(s["hbt"], dz_gl, "nn", BF16, D_MODEL, 1024, bt, "grad_w_gates")
        dh_gl = matmul(dz_gl, lw["w_gl"], "nt", F32, bt, D_MODEL, 1024, "grad_h_gates", b_layer=0)
        dh, g["gpre"][i] = prenorm_bwd(dz_br, lw["w_br"], dh_gl, s["hres"], sw["gpre"], dh, i)
        gap = jnp.zeros((N_CHIPS, M_UQ - M_SC - 256, 256), F32)
        large.put_grads(i, _input_weights_inverse(dw_br, dw_gl), jnp.concatenate([dwout, gap, dwup, dwo], axis=1))

    g = {n: jnp.stack(parts) for n, parts in g.items()}
    grads = dict(
        meta_tokens=dh[:N_META], pre_norm_g=g["gpre"][:, 0], gate_bias=g["bias"][:, 0], pool_w=_block_diag_inverse(g["pwbd"]),
        pool_scale=g["pscale"][:, 0], q_norm_g=g["gq"][:, 0], kv_norm_g=g["gkv"][:, 0], conf_dw_w=g["conf_w"][:, :CONF_K],
        conf_dw_b=g["conf_vec"][:, 2], conf_ln_g=g["conf_vec"][:, 0], conf_ln_b=g["conf_vec"][:, 1], sc_dw_w=g["sc_w"][:, :SC_K],
        post_norm_g=g["gpost"][:, 0])
    return total[0, 0], dh[N_META:length], grads


def kernel(x, meta_tokens, pre_norm_g, w_in, gate_bias, pool_w, pool_scale, w_out_pool, q_norm_g, w_uq, kv_norm_g, w_ukv, w_out_mla, conf_dw_w, conf_dw_b, conf_ln_g, conf_ln_b, w_out_conf, sc_dw_w, w_out_sc, w_o, post_norm_g, loss_target, m_meta_tokens, m_pre_norm_g, m_w_in, m_gate_bias, m_pool_w, m_pool_scale, m_w_out_pool, m_q_norm_g, m_w_uq, m_kv_norm_g, m_w_ukv, m_w_out_mla, m_conf_dw_w, m_conf_dw_b, m_conf_ln_g, m_conf_ln_b, m_w_out_conf, m_sc_dw_w, m_w_out_sc, m_w_o, m_post_norm_g, v_meta_tokens, v_pre_norm_g, v_w_in, v_gate_bias, v_pool_w, v_pool_scale, v_w_out_pool, v_q_norm_g, v_w_uq, v_kv_norm_g, v_w_ukv, v_w_out_mla, v_conf_dw_w, v_conf_dw_b, v_conf_ln_g, v_conf_ln_b, v_w_out_conf, v_sc_dw_w, v_w_out_sc, v_w_o, v_post_norm_g):
    args = locals()
    weights = {n: args[n] for n in WEIGHT_ORDER}
    c = lax.axis_index("c")
    chip = 2 * lax.axis_index("x") + lax.axis_index("y")

    small = {n: weights[n] for n, _ in REPLICATED}
    small.update(gather_small(weights))
    large = MeshWeights(weights, c, chip)
    total, dx, grads = local_step(x[0], loss_target[0], small, large)
    loss = lax.psum(total * (0.5 / D_MODEL), ("x", "y", "c"))

    reduced = large.reduced()
    reduced.update(reduce_small(grads, chip))

    flip = lambda a: jnp.swapaxes(a, 1, 2)
    deltas, new_m, new_v = [], [], []
    for n in WEIGHT_ORDER:
        operands = (weights[n], reduced[n], args["m_" + n], args["v_" + n])
        if n == "w_in":
            operands = (flip(operands[0]), lax.optimization_barrier(flip(operands[1])), flip(operands[2]), flip(operands[3]))
            reduced[n] = flip(operands[1])
        d, nm, nv = adamw(*operands)
        if n == "w_in":
            d, nm, nv = flip(d), flip(nm), flip(nv)
        deltas.append(d)
        new_m.append(nm)
        new_v.append(nv)
    return (loss, dx[None], *[reduced[n] for n in WEIGHT_ORDER], *deltas, *new_m, *new_v)
```

```python
import functools
import math

import jax
import jax.numpy as jnp
from jax import lax
from jax.experimental import pallas as pl
from jax.experimental.pallas import tpu as pltpu

F32 = jnp.float32
BF16 = jnp.bfloat16

D_MODEL = 1024
DEPTH = 4
N_META = 16
EPS = 1e-6
HEADS = 8
QK_NOPE = 64
QK_ROPE = 32
V_DIM = 64
HEAD_PAD = 128
ROPE_THETA = 10000.0
Q_SCALE = (QK_NOPE + QK_ROPE) ** -0.5
CONF_K = 31
SC_K = 3
IN_W = 7328
N_CHIPS = 4

ZB = 3328
ZG = 4096
BG, C2, XV, SG, PV, PG, CQ, CKV, KR, MG, CA, CGT, CG = (0, 256, 512, 768, 1024, 1280, 1536, 1792, 1920, 2048, 2560, 2816, 3072)

KEY_GROUP = 4
ROW_TILE = 384
HALO = 32
LANES = 128
VMEM_LIMIT = 56 * 1024 * 1024

ADAM_LR = 0.001
ADAM_B1 = 0.9
ADAM_B2 = 0.999
ADAM_EPS = 1e-08
ADAM_WD = 0.01
ADAM_STEP = 10

MESH = pl.DeviceIdType.MESH
ANY = pl.BlockSpec(memory_space=pl.ANY)

MISC = ("w_out_mla", "w_out_pool", "w_out_conf", "w_out_sc", "w_uq", "w_ukv", "w_o")
M_MLA, M_POOL, M_CONF, M_SC, M_UQ, M_UKVK, M_UKVV, M_WO, MISC_ROWS = 0, 512, 768, 1024, 1536, 1792, 1920, 2048, 3072
SHARDED_SMALL = (
    ("meta_tokens", (N_META, 256), 1),
    ("conf_dw_w", (DEPTH, CONF_K, 64), 2),
    ("sc_dw_w", (DEPTH, SC_K, 64), 2),
)
REPLICATED = (
    ("pre_norm_g", (DEPTH, D_MODEL)),
    ("gate_bias", (DEPTH, 4 * D_MODEL)),
    ("pool_w", (DEPTH, 4, 64, 64)),
    ("pool_scale", (DEPTH, 256)),
    ("q_norm_g", (DEPTH, 256)),
    ("kv_norm_g", (DEPTH, 128)),
    ("conf_dw_b", (DEPTH, 256)),
    ("conf_ln_g", (DEPTH, 256)),
    ("conf_ln_b", (DEPTH, 256)),
    ("post_norm_g", (DEPTH, D_MODEL)),
)
WEIGHT_ORDER = ("meta_tokens", "pre_norm_g", "w_in", "gate_bias", "pool_w", "pool_scale", "w_out_pool", "q_norm_g", "w_uq",
                "kv_norm_g", "w_ukv", "w_out_mla", "conf_dw_w", "conf_dw_b", "conf_ln_g", "conf_ln_b", "w_out_conf", "sc_dw_w",
                "w_out_sc", "w_o", "post_norm_g")


def _dot(a, b):
    return lax.dot_general(a, b, (((1,), (0,)), ((), ())), preferred_element_type=F32)


def _dot_nt(a, b):
    return lax.dot_general(a, b, (((1,), (1,)), ((), ())), preferred_element_type=F32)


def _dot_tn(a, b):
    return lax.dot_general(a, b, (((0,), (0,)), ((), ())), preferred_element_type=F32)


def _sigmoid(x):
    return jax.nn.sigmoid(x)


def _silu(x):
    return x * _sigmoid(x)


def _silu_grad(x):
    s = _sigmoid(x)
    return s * (1.0 + x * (1.0 - s))


def _rms(x, g):
    return x * lax.rsqrt(jnp.mean(x * x, axis=-1, keepdims=True) + EPS) * g


def _sh(x, d):
    return x if d == 0 else pltpu.roll(x, d, 0)


def _ash(x, d):
    return x if d == 0 else pltpu.roll(x, x.shape[0] - d, 0)


def _lanes8(t):
    return jnp.concatenate([t] * HEADS, axis=1)


def _pool_window_sums(v, shift):
    a2 = v + shift(v, 1)
    a4 = a2 + shift(a2, 2)
    a8 = a4 + shift(a4, 4)
    a16 = a8 + shift(a8, 8)
    lane = lax.broadcasted_iota(jnp.int32, v.shape, 1)
    return jnp.where(lane < 64, a2, jnp.where(lane < 128, a4, jnp.where(lane < 192, a8, a16)))


def _pool_counts(first_row, rows):
    pos = first_row + lax.broadcasted_iota(jnp.int32, (rows, 256), 0)
    lane = lax.broadcasted_iota(jnp.int32, (rows, 256), 1)
    width = jnp.where(lane < 64, 2, jnp.where(lane < 128, 4, jnp.where(lane < 192, 8, 16)))
    return jnp.maximum(jnp.minimum(pos + 1, width), 1).astype(F32)


def _params(sem=None):
    return pltpu.CompilerParams(dimension_semantics=sem, vmem_limit_bytes=VMEM_LIMIT)


def _tile_specs(t, n_halo_blocks, li=0):
    per = t // HALO

    def layer(shape, idx=li):
        return pl.BlockSpec((None,) + tuple(shape), lambda i: (idx,) + (0,) * len(shape))

    def cur(c, cb=0):
        return pl.BlockSpec((t, c), lambda i: (i, cb))

    def prev(c, cb=0):
        return pl.BlockSpec((HALO, c), lambda i: (jnp.maximum(i * per - 1, 0), cb))

    def nxt(c, cb=0):
        return pl.BlockSpec((HALO, c), lambda i: (jnp.minimum((i + 1) * per, n_halo_blocks - 1), cb))

    def full(shape):
        return pl.BlockSpec(shape, lambda i: (0,) * len(shape))

    return cur, prev, nxt, full, layer


def _big_tile(rows):
    return rows // 3 if rows % (3 * LANES) == 0 else ROW_TILE


def matmul(a, b, mode, out_dtype, tm, tn, tk, name, b_layer=None):
    bs = b.shape if b_layer is None else b.shape[1:]
    lead = () if b_layer is None else (None,)
    pick = (lambda *ix: ix) if b_layer is None else (lambda *ix: (b_layer,) + ix)
    if mode == "nn":
        (m, k), n = a.shape, bs[1]
        a_spec = pl.BlockSpec((tm, tk), lambda i, j, kk: (i, kk))
        b_spec = pl.BlockSpec(lead + (tk, tn), lambda i, j, kk: pick(kk, j))
        dot = _dot
    elif mode == "nt":
        (m, k), n = a.shape, bs[0]
        a_spec = pl.BlockSpec((tm, tk), lambda i, j, kk: (i, kk))
        b_spec = pl.BlockSpec(lead + (tn, tk), lambda i, j, kk: pick(j, kk))
        dot = _dot_nt
    else:
        raise ValueError(mode)
    assert m % tm == 0 and n % tn == 0 and k % tk == 0, (a.shape, bs, tm, tn, tk)
    nk = k // tk

    def body(a_ref, b_ref, o_ref, *acc):
        if nk == 1:
            o_ref[...] = dot(a_ref[...], b_ref[...]).astype(out_dtype)
            return
        (acc_ref,) = acc
        kk = pl.program_id(2)

        @pl.when(kk == 0)
        def _():
            acc_ref[...] = dot(a_ref[...], b_ref[...])

        @pl.when((kk > 0) & (kk < nk - 1))
        def _():
            acc_ref[...] += dot(a_ref[...], b_ref[...])

        @pl.when(kk == nk - 1)
        def _():
            o_ref[...] = (acc_ref[...] + dot(a_ref[...], b_ref[...])).astype(out_dtype)

    return pl.pallas_call(
        body, name=name, grid=(m // tm, n // tn, nk), in_specs=[a_spec, b_spec],
        out_specs=pl.BlockSpec((tm, tn), lambda i, j, kk: (i, j)), out_shape=jax.ShapeDtypeStruct((m, n), out_dtype),
        scratch_shapes=[pltpu.VMEM((tm, tn), F32)] if nk > 1 else [], compiler_params=_params(("parallel", "parallel", "arbitrary")),
    )(a, b)


def prenorm_project(hres, g, w, li):
    rows, d = hres.shape
    n = w.shape[2]
    tm, tn = _big_tile(rows), n // 2

    def body(x_ref, g_ref, w_ref, z_ref, hb_ref, hbt_ref):
        @pl.when(pl.program_id(1) == 0)
        def _():
            h = _rms(x_ref[...], g_ref[...])
            hb_ref[...] = h.astype(BF16)
            hbt_ref[...] = h.T.astype(BF16)

        z_ref[...] = _dot(hb_ref[...], w_ref[...]).astype(BF16)

    return pl.pallas_call(
        body, name="prenorm_project", grid=(rows // tm, n // tn),
        in_specs=[pl.BlockSpec((tm, d), lambda i, j: (i, 0)), pl.BlockSpec((None, 1, d), lambda i, j: (li, 0, 0)),
                  pl.BlockSpec((None, d, tn), lambda i, j: (0, 0, j))],
        out_specs=[pl.BlockSpec((tm, tn), lambda i, j: (i, j)), pl.BlockSpec((tm, d), lambda i, j: (i, 0)),
                   pl.BlockSpec((d, tm), lambda i, j: (0, i))],
        out_shape=[jax.ShapeDtypeStruct((rows, n), BF16), jax.ShapeDtypeStruct((rows, d), BF16), jax.ShapeDtypeStruct((d, rows), BF16)],
        compiler_params=_params(("parallel", "arbitrary")),
    )(hres, g, w)


def _rope(q, c, s1, s2, width):
    return q * c + pltpu.roll(q, width - 16, 1) * s1 + pltpu.roll(q, 16, 1) * s2


def _rope_transposed(dq, c, s1, s2, width):
    return dq * c + pltpu.roll(dq * s1, 16, 1) + pltpu.roll(dq * s2, width - 16, 1)


def _conf_conv(g1, w_ref):
    acc = jnp.zeros_like(g1)
    for k in range(CONF_K):
        acc = acc + w_ref[k:k + 1, :] * _sh(g1, CONF_K - 1 - k)
    return acc


def _conf_tail(c, cg, lg, lb):
    mu = jnp.mean(c, axis=-1, keepdims=True)
    xc = c - mu
    var = jnp.mean(xc * xc, axis=-1, keepdims=True)
    n = xc * lax.rsqrt(var + EPS) * lg + lb
    return _silu(n) * _silu(cg)


def _misc_spec(misc, row0, rows):
    assert row0 % rows == 0
    return pl.BlockSpec((None, N_CHIPS, rows, 256), lambda i: (misc[1], 0, row0 // rows, 0))


def _chip_columns(x, w_ref, row0, rows, lanes=256):
    return jnp.concatenate([_dot(x, w_ref[k, row0:row0 + rows, 0:lanes]) for k in range(N_CHIPS)], axis=1)


def branches_fwd(z_br, rope, pwbd, pscale, gq, gkv, misc, conf_w, conf_vec, sc_w, li):
    rows = z_br.shape[0]
    t = ROW_TILE
    cur, prev, _, _, layer = _tile_specs(t, rows // HALO, li)

    def body(zc_ref, zp_ref, rope_ref, pw_ref, ps_ref, gq_ref, gkv_ref, up_ref, cw_ref, cv_ref, sw_ref,
             ua_ref, uc_ref, ud_ref, q_ref, k_ref, v_ref):
        i = pl.program_id(0)
        zp = jnp.where(i == 0, jnp.zeros(zp_ref.shape, zp_ref.dtype), zp_ref[...])

        def ext(lo, w=256):
            return jnp.concatenate([zp[:, lo:lo + w], zc_ref[:, lo:lo + w]], axis=0).astype(F32)

        def col(lo, w=256):
            return zc_ref[:, lo:lo + w].astype(F32)

        v = ext(PV)
        p = (_pool_window_sums(v, _sh) / _pool_counts(i * t - HALO, t + HALO) - v)[HALO:]
        ya = _dot(p.astype(BF16), pw_ref[...]) * ps_ref[...]
        ua_ref[...] = (ya * _silu(col(PG))).astype(BF16)

        g1 = ext(CA) * _sigmoid(ext(CGT))
        c = _conf_conv(g1, cw_ref)[HALO:] + cv_ref[0:1, :]
        uc_ref[...] = _conf_tail(c, col(CG), cv_ref[1:2, :], cv_ref[2:3, :]).astype(BF16)

        e = ext(C2) * ext(XV)
        f = jnp.zeros_like(e)
        for k in range(SC_K):
            f = f + sw_ref[k:k + 1, :] * _sh(e, SC_K - 1 - k)
        ud_ref[...] = (col(BG) * f[HALO:] * _silu(col(SG))).astype(BF16)

        cth, s1, s2 = rope_ref[:, 0:128], rope_ref[:, 128:256], rope_ref[:, 256:384]
        qn = _rms(col(CQ), gq_ref[...]).astype(BF16)
        q = _chip_columns(qn, up_ref, 0, 256)
        w8 = HEADS * HEAD_PAD
        q_ref[...] = (_rope(q, _lanes8(cth), _lanes8(s1), _lanes8(s2), w8) * Q_SCALE).astype(BF16)
        kvn = _rms(col(CKV, 128), gkv_ref[...]).astype(BF16)
        kr = _rope(col(KR, 128), cth, s1, s2, HEAD_PAD)
        k_ref[...] = (_chip_columns(kvn, up_ref, M_UKVK - M_UQ, 128) + _lanes8(kr)).astype(BF16)
        v_ref[...] = _chip_columns(kvn, up_ref, M_UKVV - M_UQ, 128, 2 * V_DIM).astype(BF16)

    outs = [jax.ShapeDtypeStruct((rows, 256), BF16)] * 3 + [jax.ShapeDtypeStruct((rows, 1024), BF16)] * 2 + [
        jax.ShapeDtypeStruct((rows, 512), BF16)]
    return pl.pallas_call(
        body, name="branches_fwd", grid=(rows // t,),
        in_specs=[cur(ZB), prev(ZB), cur(384), layer((256, 256)), layer((1, 256)), layer((1, 256)), layer((1, 128)),
                  _misc_spec(misc, M_UQ, M_WO - M_UQ), layer((32, 256)), layer((8, 256)), layer((8, 256))],
        out_specs=[cur(256), cur(256), cur(256), cur(1024), cur(1024), cur(512)], out_shape=outs,
        compiler_params=_params(("parallel",)),
    )(z_br, z_br, rope, pwbd, pscale, gq, gkv, misc[0], conf_w, conf_vec, sc_w)


def _head_lane_mask(h):
    lane = lax.broadcasted_iota(jnp.int32, (1, 2 * V_DIM), 1)
    return (lane >= V_DIM * h) & (lane < V_DIM * (h + 1))


def attention_fwd(q, k, v, gather=None):
    rows = q.shape[0]
    tq = ROW_TILE
    nq = rows // tq
    n = 0 if gather is None else len(gather[0])

    def body(*refs):
        if n:
            start, finish = _gather_ops(refs[3:3 + n], refs[5 + 2 * n:5 + 3 * n], refs[5 + 3 * n:], gather[2], True)
            pl.when((pl.program_id(0) == 0) & (pl.program_id(1) == 0))(start)
        compute(*refs[:3], *refs[3 + 2 * n:5 + 2 * n])
        if n:
            pl.when((pl.program_id(0) == HEADS // 2 - 1) & (pl.program_id(1) == nq - 1))(finish)

    def compute(q_ref, k_ref, v_ref, o_ref, lse_ref):
        i = pl.program_id(1)

        def head_step(h, tile, n_tiles, carry, masked):
            m, l, acc = carry
            width = n_tiles * tq
            r0 = pl.multiple_of(tile * tq, tq)
            kh = k_ref[pl.ds(r0, width), HEAD_PAD * h:HEAD_PAD * (h + 1)]
            vh = jnp.where(_head_lane_mask(h), v_ref[pl.ds(r0, width), :], jnp.zeros((), BF16))
            s = _dot_nt(q_ref[:, HEAD_PAD * h:HEAD_PAD * (h + 1)], kh)
            if masked:
                row = lax.broadcasted_iota(jnp.int32, (tq, width), 0)
                colm = lax.broadcasted_iota(jnp.int32, (tq, width), 1)
                s = jnp.where(colm <= row + (width - tq), s, -1e30)
            m2 = jnp.maximum(m, jnp.max(s, axis=-1, keepdims=True))
            alpha = jnp.exp(m - m2)
            pr = jnp.exp(s - m2)
            return m2, alpha * l + jnp.sum(pr, axis=-1, keepdims=True), alpha * acc + _dot(pr.astype(BF16), vh)

        def step(tile, n_tiles, carry, masked):
            return tuple(head_step(h, tile, n_tiles, carry[h], masked) for h in range(2))

        init = (jnp.full((tq, 1), -1e30, F32), jnp.zeros((tq, 1), F32), jnp.zeros((tq, 2 * V_DIM), F32))
        group = min(KEY_GROUP, nq)
        carry = lax.fori_loop(0, i // group, lambda t, cr: step(group * t, group, cr, False), (init, init))
        carry = lax.switch(i % group, [functools.partial(lambda cr, r: step(i - r, r + 1, cr, True), r=r) for r in range(group)], carry)
        out = jnp.zeros((tq, 2 * V_DIM), F32)
        for h, (m, l, acc) in enumerate(carry):
            out = out + acc / l
            lse_ref[h] = jnp.broadcast_to(m + jnp.log(l), (tq, LANES))
        o_ref[...] = out.astype(BF16)

    srcs, dsts = ([], []) if gather is None else (list(gather[0]), list(gather[1]))
    outs = pl.pallas_call(
        body, name="attention_fwd" if gather is None else "attention_fwd_gather", grid=(HEADS // 2, nq),
        in_specs=[pl.BlockSpec((tq, 2 * HEAD_PAD), lambda p, i: (i, p)), pl.BlockSpec((rows, 2 * HEAD_PAD), lambda p, i: (0, p)),
                  pl.BlockSpec((rows, 2 * V_DIM), lambda p, i: (0, p))] + [ANY] * (2 * n),
        out_specs=[pl.BlockSpec((tq, 2 * V_DIM), lambda p, i: (i, p)), pl.BlockSpec((2, tq, LANES), lambda p, i: (p, i, 0))] + [ANY] * n,
        out_shape=[jax.ShapeDtypeStruct((rows, HEADS * V_DIM), BF16), jax.ShapeDtypeStruct((HEADS, rows, LANES), F32)] + [
            jax.ShapeDtypeStruct(d.shape, d.dtype) for d in dsts],
        input_output_aliases={3 + n + a: 2 + a for a in range(n)}, scratch_shapes=GATHER_SEMS(n) if n else [],
        compiler_params=_params(("arbitrary", "arbitrary") if n else ("parallel", "parallel")),
    )(q, k, v, *srcs, *dsts)
    return outs[0], outs[1], list(outs[2:])


OUT_PROJECTIONS = ((M_POOL, 256), (M_MLA, 512), (M_CONF, 256), (M_SC, 256))


def _chunks(x, n=N_CHIPS, width=256):
    return [x[:, width * k:width * (k + 1)] for k in range(n)]


def merge_fwd(ua, o_att, uc, ud, z_br, z_gl, bias, misc, gpost, hres, li):
    rows = hres.shape[0]
    t = ROW_TILE
    cur, _, _, _, layer = _tile_specs(t, rows // HALO, li)
    d = D_MODEL

    def body(ua_ref, ob_ref, uc_ref, ud_ref, mg_ref, gl_ref, b_ref, wout_ref, wo_ref, gp_ref, h_ref, ub_ref, mb_ref, o_ref, hn_ref):
        ub = (ob_ref[...].astype(F32) * _silu(mg_ref[...].astype(F32))).astype(BF16)
        ub_ref[...] = ub
        m = jnp.zeros((t, d), F32)
        for idx, (u, (row0, n)) in enumerate(zip((ua_ref[...], ub, uc_ref[...], ud_ref[...]), OUT_PROJECTIONS)):
            gate = _sigmoid(gl_ref[:, d * idx:d * (idx + 1)].astype(F32) + b_ref[:, d * idx:d * (idx + 1)])
            m = m + gate * _chip_columns(u, wout_ref, row0, n)
        mb = m.astype(BF16)
        mb_ref[...] = mb
        o = jnp.concatenate([sum(_dot(mk, wo_ref[k, 256 * j:256 * (j + 1), :]) for k, mk in enumerate(_chunks(mb)))
                             for j in range(N_CHIPS)], axis=1)
        o_ref[...] = o
        hn_ref[...] = h_ref[...] + _rms(o, gp_ref[...])

    return pl.pallas_call(
        body, name="merge_fwd", grid=(rows // t,),
        in_specs=[cur(256), cur(512), cur(256), cur(256), cur(512, MG // 512), cur(ZG), layer((1, ZG)), _misc_spec(misc, 0, 1280),
                  _misc_spec(misc, M_WO, D_MODEL), layer((1, d)), cur(d)],
        out_specs=[cur(512), cur(d), cur(d), cur(d)],
        out_shape=[jax.ShapeDtypeStruct((rows, 512), BF16), jax.ShapeDtypeStruct((rows, d), BF16), jax.ShapeDtypeStruct((rows, d), F32),
                   jax.ShapeDtypeStruct((rows, d), F32)],
        compiler_params=_params(("parallel",)),
    )(ua, o_att, uc, ud, z_br, z_gl, bias, misc[0], misc[0], gpost, hres)


def loss_head(hres, target, n_tokens):
    rows, d = hres.shape
    t = ROW_TILE
    cur, _, _, full, _ = _tile_specs(t, rows // HALO)
    n_steps = rows // t

    def body(h_ref, t_ref, dh_ref, tot_ref, acc_ref):
        i = pl.program_id(0)

        @pl.when(i == 0)
        def _():
            acc_ref[...] = jnp.zeros_like(acc_ref)

        r = i * t + lax.broadcasted_iota(jnp.int32, (t, 1), 0)
        diff = jnp.where((r >= N_META) & (r < N_META + n_tokens), h_ref[...] - t_ref[...], 0.0)
        dh_ref[...] = diff * (1.0 / d)
        acc_ref[...] += jnp.sum(diff * diff, axis=0, keepdims=True)

        @pl.when(i == n_steps - 1)
        def _():
            tot_ref[...] = jnp.broadcast_to(jnp.sum(acc_ref[...], axis=1, keepdims=True), (1, LANES))

    return pl.pallas_call(
        body, name="loss_head", grid=(n_steps,), in_specs=[cur(d), cur(d)], out_specs=[cur(d), full((1, LANES))],
        out_shape=[jax.ShapeDtypeStruct((rows, d), F32), jax.ShapeDtypeStruct((1, LANES), F32)],
        scratch_shapes=[pltpu.VMEM((1, d), F32)], compiler_params=_params(("arbitrary",)),
    )(hres, target)


def _accumulate(i, ref, value):
    @pl.when(i == 0)
    def _():
        ref[...] = value

    @pl.when(i > 0)
    def _():
        ref[...] += value


def postnorm_bwd(dh, o, mb, misc, gpost, li, swap=None):
    rows, d = dh.shape
    t = ROW_TILE
    cur, _, _, full, layer = _tile_specs(t, rows // HALO, li)
    n = 0 if swap is None else len(swap)
    steps = rows // t

    def body(*refs):
        if n:
            start, finish = _swap_ops(refs[5:5 + n], refs[8 + n:8 + 2 * n], refs[8 + 2 * n:])
            pl.when(pl.program_id(0) == 0)(start)
        compute(*refs[:5], *refs[5 + n:8 + n])
        if n:
            pl.when(pl.program_id(0) == steps - 1)(finish)

    def compute(dh_ref, o_ref, mb_ref, wo_ref, gp_ref, dm_ref, dwo_ref, dgp_ref):
        i = pl.program_id(0)
        _, vjp = jax.vjp(_rms, o_ref[...], gp_ref[...])
        do, dg = vjp(dh_ref[...])
        dob = do.astype(BF16)
        dm_ref[...] = jnp.concatenate([sum(_dot_nt(dj, wo_ref[k, 256 * j:256 * (j + 1), :]) for j, dj in enumerate(_chunks(dob)))
                                       for k in range(N_CHIPS)], axis=1)
        dwo = _dot_tn(mb_ref[...], dob)
        for k in range(N_CHIPS):
            _accumulate(i, dwo_ref.at[k], jnp.concatenate(_chunks(dwo[256 * k:256 * (k + 1), :]), axis=0))
        _accumulate(i, dgp_ref, dg)

    sent = [] if swap is None else list(swap)
    outs = pl.pallas_call(
        body, name="postnorm_bwd" if swap is None else "postnorm_bwd_swap", grid=(steps,),
        in_specs=[cur(d), cur(d), cur(d), _misc_spec(misc, M_WO, d), layer((1, d))] + [ANY] * n,
        out_specs=[cur(d), full((N_CHIPS, d, 256)), full((1, d))] + [ANY] * n,
        out_shape=[jax.ShapeDtypeStruct((rows, d), F32), jax.ShapeDtypeStruct((N_CHIPS, d, 256), F32), jax.ShapeDtypeStruct((1, d), F32)] + [
            jax.ShapeDtypeStruct((p.shape[0], p.shape[1] // 2, p.shape[2]), p.dtype) for p in sent],
        scratch_shapes=[pltpu.SemaphoreType.DMA((n,)), pltpu.SemaphoreType.DMA((n,))] if n else [],
        compiler_params=_params(("arbitrary",)),
    )(dh, o, mb, misc[0], gpost, *sent)
    return outs[0], outs[1], outs[2], list(outs[3:])


def merge_bwd(dm, ua, ub, uc, ud, z_gl, bias, misc, o_att, z_br, dz_buf, li):
    rows, d = dm.shape
    t = ROW_TILE
    cur, _, _, full, layer = _tile_specs(t, rows // HALO, li)

    def body(dm_ref, ua_ref, ub_ref, uc_ref, ud_ref, gl_ref, b_ref, w_ref, o_ref, mg_ref, _,
             dua_ref, do_ref, duc_ref, dud_ref, dgl_ref, dw_ref, db_ref, dmg_ref, delta_ref):
        i = pl.program_id(0)
        dm = dm_ref[...]
        groups = ((ua_ref, dua_ref), (ub_ref, None), (uc_ref, duc_ref), (ud_ref, dud_ref))
        for idx, ((u_ref, du_ref), (row0, n)) in enumerate(zip(groups, OUT_PROJECTIONS)):
            cols = slice(d * idx, d * (idx + 1))
            u = u_ref[...]
            gate = _sigmoid(gl_ref[:, cols].astype(F32) + b_ref[:, cols])
            dgl = dm * _chip_columns(u, w_ref, row0, n) * gate * (1.0 - gate)
            dgl_ref[:, cols] = dgl.astype(BF16)
            _accumulate(i, db_ref.at[:, cols], jnp.sum(dgl, axis=0, keepdims=True))
            dyb = (dm * gate).astype(BF16)
            du = sum(_dot_nt(dyk, w_ref[k, row0:row0 + n, :]) for k, dyk in enumerate(_chunks(dyb)))
            for k, dwk in enumerate(_chunks(_dot_tn(u, dyb))):
                _accumulate(i, dw_ref.at[k, row0:row0 + n, :], dwk)
            if du_ref is not None:
                du_ref[...] = du
                continue
            o, mg = o_ref[...].astype(F32), mg_ref[...].astype(F32)
            do = du * _silu(mg)
            do_ref[...] = do.astype(BF16)
            dmg_ref[...] = (du * o * _silu_grad(mg)).astype(BF16)
            prod = do * o
            lane = lax.broadcasted_iota(jnp.int32, (1, HEADS * V_DIM), 1)
            for h in range(HEADS):
                part = jnp.where((lane >= V_DIM * h) & (lane < V_DIM * (h + 1)), prod, 0.0)
                delta_ref[h] = jnp.broadcast_to(jnp.sum(part, axis=-1, keepdims=True), (t, LANES))

    return pl.pallas_call(
        body, name="merge_bwd", grid=(rows // t,),
        in_specs=[cur(d), cur(256), cur(512), cur(256), cur(256), cur(ZG), layer((1, ZG)), _misc_spec(misc, 0, 1280), cur(512),
                  cur(512, MG // 512), ANY],
        out_specs=[cur(256), cur(512), cur(256), cur(256), cur(ZG), full((N_CHIPS, 1280, 256)), full((1, ZG)), cur(512, MG // 512),
                   pl.BlockSpec((HEADS, t, LANES), lambda i: (0, i, 0))],
        out_shape=[jax.ShapeDtypeStruct((rows, 256), F32), jax.ShapeDtypeStruct((rows, 512), BF16), jax.ShapeDtypeStruct((rows, 256), F32),
                   jax.ShapeDtypeStruct((rows, 256), F32), jax.ShapeDtypeStruct((rows, ZG), BF16),
                   jax.ShapeDtypeStruct((N_CHIPS, 1280, 256), F32), jax.ShapeDtypeStruct((1, ZG), F32),
                   jax.ShapeDtypeStruct((rows, ZB), BF16), jax.ShapeDtypeStruct((HEADS, rows, LANES), F32)],
        input_output_aliases={10: 7}, compiler_params=_params(("arbitrary",)),
    )(dm, ua, ub, uc, ud, z_gl, bias, misc[0], o_att, z_br, dz_buf)


def pool_shortconv_bwd(z_br, dua, dud, pwbd, pscale, sc_w, dz_buf, li):
    rows = z_br.shape[0]
    t = ROW_TILE
    n_steps = rows // t
    cur, prev, nxt, full, layer = _tile_specs(t, rows // HALO, li)

    def body(zc_ref, zp_ref, zn_ref, dac_ref, dan_ref, ddc_ref, ddn_ref, pw_ref, ps_ref, sw_ref, _, dz_ref, dpw_ref, dps_ref, dw_ref):
        i = pl.program_id(0)
        last = i == n_steps - 1
        zp = jnp.where(i == 0, jnp.zeros(zp_ref.shape, zp_ref.dtype), zp_ref[...])
        zn = jnp.where(last, jnp.zeros(zn_ref.shape, zn_ref.dtype), zn_ref[...])

        def ext(lo):
            return jnp.concatenate([zp[:, lo:lo + 256], zc_ref[:, lo:lo + 256], zn[:, lo:lo + 256]], axis=0).astype(F32)

        def ext_grad(c_ref, n_ref):
            return jnp.concatenate([jnp.zeros((HALO, 256), F32), c_ref[...], jnp.where(last, jnp.zeros(n_ref.shape, F32), n_ref[...])], axis=0)

        mid = slice(HALO, HALO + t)

        bg, c2, xv, sg = ext(BG), ext(C2), ext(XV), ext(SG)
        du = ext_grad(ddc_ref, ddn_ref)
        e = c2 * xv
        shifted = [_sh(e, SC_K - 1 - k) for k in range(SC_K)]
        f = sum(sw_ref[k:k + 1, :] * shifted[k] for k in range(SC_K))
        gate = _silu(sg)
        df = du * gate * bg
        de = sum(sw_ref[k:k + 1, :] * _ash(df, SC_K - 1 - k) for k in range(SC_K))
        d_sc = [(du * gate * f)[mid], (de * xv)[mid], (de * c2)[mid], (du * bg * f * _silu_grad(sg))[mid]]
        dw = jnp.concatenate([jnp.sum((df * shifted[k])[mid], axis=0, keepdims=True) for k in range(SC_K)] + [
            jnp.zeros((8 - SC_K, 256), F32)], axis=0)
        _accumulate(i, dw_ref, dw)

        v, pg = ext(PV), ext(PG)
        cnt = _pool_counts(i * t - HALO, t + 2 * HALO)
        p = (_pool_window_sums(v, _sh) / cnt - v)[mid]
        dya = ext_grad(dac_ref, dan_ref) * _silu(pg)
        dypb = (dya * ps_ref[...]).astype(BF16)
        dp = _dot_nt(dypb, pw_ref[...])
        dv = (_pool_window_sums(dp / cnt, _ash) - dp)[mid]
        pb = p.astype(BF16)
        pw = _dot(pb, pw_ref[...])
        dpg = dac_ref[...] * pw * ps_ref[...] * _silu_grad(pg[mid])
        _accumulate(i, dpw_ref, _dot_tn(pb, dypb[mid]))
        _accumulate(i, dps_ref, jnp.sum(dya[mid] * pw, axis=0, keepdims=True))

        dz_ref[...] = jnp.concatenate(d_sc + [dv, dpg], axis=1).astype(BF16)

    return pl.pallas_call(
        body, name="pool_shortconv_bwd", grid=(n_steps,),
        in_specs=[cur(ZB), prev(ZB), nxt(ZB), cur(256), nxt(256), cur(256), nxt(256), layer((256, 256)), layer((1, 256)), layer((8, 256)),
                  ANY],
        out_specs=[cur(1536, BG // 1536), full((256, 256)), full((1, 256)), full((8, 256))],
        out_shape=[jax.ShapeDtypeStruct((rows, ZB), BF16), jax.ShapeDtypeStruct((256, 256), F32), jax.ShapeDtypeStruct((1, 256), F32),
                   jax.ShapeDtypeStruct((8, 256), F32)],
        input_output_aliases={10: 0}, compiler_params=_params(("arbitrary",)),
    )(z_br, z_br, z_br, dua, dua, dud, dud, pwbd, pscale, sc_w, dz_buf)


def conformer_bwd_tail(z_br, duc, conf_w, conf_vec, dz_buf, li):
    rows = z_br.shape[0]
    t = ROW_TILE
    cur, prev, _, full, layer = _tile_specs(t, rows // HALO, li)

    def body(zc_ref, zp_ref, du_ref, cw_ref, cv_ref, _, dc_ref, dcg_ref, dv_ref):
        i = pl.program_id(0)
        zp = jnp.where(i == 0, jnp.zeros(zp_ref.shape, zp_ref.dtype), zp_ref[...])

        def ext(lo):
            return jnp.concatenate([zp[:, lo:lo + 256], zc_ref[:, lo:lo + 256]], axis=0).astype(F32)

        g1 = ext(CA) * _sigmoid(ext(CGT))
        c = _conf_conv(g1, cw_ref)[HALO:] + cv_ref[0:1, :]
        _, vjp = jax.vjp(_conf_tail, c, zc_ref[:, CG:CG + 256].astype(F32), cv_ref[1:2, :], cv_ref[2:3, :])
        dc, dcg, dlg, dlb = vjp(du_ref[...])
        dc_ref[...] = dc
        dcg_ref[...] = dcg.astype(BF16)
        dvec = jnp.concatenate([dlg, dlb, jnp.sum(dc, axis=0, keepdims=True), jnp.zeros((5, 256), F32)], axis=0)
        _accumulate(i, dv_ref, dvec)

    return pl.pallas_call(
        body, name="conformer_bwd_tail", grid=(rows // t,), in_specs=[cur(ZB), prev(ZB), cur(256), layer((32, 256)), layer((8, 256)), ANY],
        out_specs=[cur(256), cur(256, CG // 256), full((8, 256))],
        out_shape=[jax.ShapeDtypeStruct((rows, 256), F32), jax.ShapeDtypeStruct((rows, ZB), BF16), jax.ShapeDtypeStruct((8, 256), F32)],
        input_output_aliases={5: 1}, compiler_params=_params(("arbitrary",)),
    )(z_br, z_br, duc, conf_w, conf_vec, dz_buf)


def conformer_bwd_conv(z_br, dc, conf_w, dz_buf, li):
    rows = z_br.shape[0]
    t = ROW_TILE
    n_steps = rows // t
    cur, prev, nxt, full, layer = _tile_specs(t, rows // HALO, li)

    def body(zc_ref, zp_ref, dc_ref, dn_ref, cw_ref, _, dz_ref, dw_ref):
        i = pl.program_id(0)
        zp = jnp.where(i == 0, jnp.zeros(zp_ref.shape, zp_ref.dtype), zp_ref[...])
        dcn = jnp.where(i == n_steps - 1, jnp.zeros(dn_ref.shape, dn_ref.dtype), dn_ref[...])

        def ext(lo):
            return jnp.concatenate([zp[:, lo:lo + 256], zc_ref[:, lo:lo + 256]], axis=0).astype(F32)

        a, gt = ext(CA), ext(CGT)
        sg = _sigmoid(gt)
        g1 = a * sg
        dc = dc_ref[...]
        dce = jnp.concatenate([dc, dcn], axis=0)
        dg1 = jnp.zeros_like(dce)
        dws = []
        for k in range(CONF_K):
            dg1 = dg1 + cw_ref[k:k + 1, :] * _ash(dce, CONF_K - 1 - k)
            dws.append(jnp.sum(dc * _sh(g1, CONF_K - 1 - k)[HALO:], axis=0, keepdims=True))
        dg1 = dg1[:t]
        ac, sc = a[HALO:], sg[HALO:]
        dz_ref[...] = jnp.concatenate([dg1 * sc, dg1 * ac * sc * (1.0 - sc)], axis=1).astype(BF16)
        _accumulate(i, dw_ref, jnp.concatenate(dws + [jnp.zeros((32 - CONF_K, 256), F32)], axis=0))

    return pl.pallas_call(
        body, name="conformer_bwd_conv", grid=(n_steps,), in_specs=[cur(ZB), prev(ZB), cur(256), nxt(256), layer((32, 256)), ANY],
        out_specs=[cur(512, CA // 512), full((32, 256))],
        out_shape=[jax.ShapeDtypeStruct((rows, ZB), BF16), jax.ShapeDtypeStruct((32, 256), F32)],
        input_output_aliases={5: 0}, compiler_params=_params(("arbitrary",)),
    )(z_br, z_br, dc, dc, conf_w, dz_buf)


def attention_bwd(q, k, v, do, lse, delta, exchange=None):
    rows = q.shape[0]
    tq = ROW_TILE
    nq = rows // tq
    n = 0 if exchange is None else len(exchange[0])

    def body(*refs):
        if n:
            start, finish = _exchange_ops(refs[6:6 + n], refs[9 + 2 * n:9 + 3 * n], refs[9 + 3 * n:], exchange[2])
            pl.when((pl.program_id(0) == 0) & (pl.program_id(1) == 0))(start)
        compute(*refs[:6], *refs[6 + 2 * n:9 + 2 * n])
        if n:
            pl.when((pl.program_id(0) == HEADS // 2 - 1) & (pl.program_id(1) == nq - 1))(finish)

    def compute(q_ref, k_ref, v_ref, do_ref, lse_ref, dl_ref, dq_ref, dk_ref, dv_ref):
        j = pl.program_id(1)

        @pl.when(j == 0)
        def _():
            dq_ref[...] = jnp.zeros_like(dq_ref)

        def head_step(h, tile, n_tiles, dk, dv, diagonal):
            lanes = slice(HEAD_PAD * h, HEAD_PAD * (h + 1))
            hm = _head_lane_mask(h)
            kh = k_ref[:, lanes]
            vh = jnp.where(hm, v_ref[...], jnp.zeros((), BF16))
            r0, width = pl.multiple_of(tile * tq, tq), n_tiles * tq
            qi = q_ref[pl.ds(r0, width), lanes]
            doi = jnp.where(hm, do_ref[pl.ds(r0, width), :], jnp.zeros((), BF16))
            s = _dot_nt(qi, kh)
            if diagonal:
                s = jnp.where(lax.broadcasted_iota(jnp.int32, (tq, tq), 1) <= lax.broadcasted_iota(jnp.int32, (tq, tq), 0), s, -1e30)
            pr = jnp.exp(s - lse_ref[h, pl.ds(r0, width), :][:, 0:1])
            dv = dv + _dot_tn(pr.astype(BF16), doi)
            dp = _dot_nt(doi, vh)
            ds = (pr * (dp - dl_ref[h, pl.ds(r0, width), :][:, 0:1])).astype(BF16)
            dq_ref[pl.ds(r0, width), lanes] += _dot(ds, kh)
            return dk + _dot_tn(ds, qi), dv

        def step(tile, n_tiles, carry, diagonal):
            dk0, dk1, dv = carry
            dk0, dv = head_step(0, tile, n_tiles, dk0, dv, diagonal)
            dk1, dv = head_step(1, tile, n_tiles, dk1, dv, diagonal)
            return dk0, dk1, dv

        zero = jnp.zeros((tq, HEAD_PAD), F32)
        carry = step(j, 1, (zero, zero, jnp.zeros((tq, 2 * V_DIM), F32)), True)
        odd = (nq - 1 - j) % 2
        carry = lax.cond(odd == 1, lambda cr: step(j + 1, 1, cr, False), lambda cr: cr, carry)
        dk0, dk1, dv = lax.fori_loop(0, (nq - 1 - j) // 2, lambda t, cr: step(j + 1 + odd + 2 * t, 2, cr, False), carry)
        dk_ref[:, 0:HEAD_PAD] = dk0
        dk_ref[:, HEAD_PAD:2 * HEAD_PAD] = dk1
        dv_ref[...] = dv

    srcs, dsts = ([], []) if exchange is None else (list(exchange[0]), list(exchange[1]))
    outs = pl.pallas_call(
        body, name="attention_bwd" if exchange is None else "attention_bwd_exchange", grid=(HEADS // 2, nq),
        in_specs=[pl.BlockSpec((rows, 2 * HEAD_PAD), lambda p, j: (0, p)), pl.BlockSpec((tq, 2 * HEAD_PAD), lambda p, j: (j, p)),
                  pl.BlockSpec((tq, 2 * V_DIM), lambda p, j: (j, p)), pl.BlockSpec((rows, 2 * V_DIM), lambda p, j: (0, p)),
                  pl.BlockSpec((2, rows, LANES), lambda p, j: (p, 0, 0)), pl.BlockSpec((2, rows, LANES), lambda p, j: (p, 0, 0))] + [
                      ANY] * (2 * n),
        out_specs=[pl.BlockSpec((rows, 2 * HEAD_PAD), lambda p, j: (0, p)), pl.BlockSpec((tq, 2 * HEAD_PAD), lambda p, j: (j, p)),
                   pl.BlockSpec((tq, 2 * V_DIM), lambda p, j: (j, p))] + [ANY] * n,
        out_shape=[jax.ShapeDtypeStruct((rows, HEADS * HEAD_PAD), F32), jax.ShapeDtypeStruct((rows, HEADS * HEAD_PAD), F32),
                   jax.ShapeDtypeStruct((rows, HEADS * V_DIM), F32)] + [jax.ShapeDtypeStruct(d.shape, d.dtype) for d in dsts],
        input_output_aliases={6 + n + a: 3 + a for a in range(n)}, scratch_shapes=EXCHANGE_SEMS(n) if n else [],
        compiler_params=_params(("arbitrary", "arbitrary") if n else ("parallel", "arbitrary")),
    )(q, k, v, do, lse, delta, *srcs, *dsts)
    return outs[0], outs[1], outs[2], list(outs[3:])


def mla_prep_bwd(dq, dk, dv, z_br, rope, gq, gkv, misc, dz_buf, li):
    rows = dq.shape[0]
    t = ROW_TILE
    cur, _, _, full, layer = _tile_specs(t, rows // HALO, li)
    w8 = HEADS * HEAD_PAD
    uq, keys, values = slice(0, 256), slice(M_UKVK - M_UQ, M_UKVV - M_UQ), slice(M_UKVV - M_UQ, M_WO - M_UQ)

    def body(dq_ref, dk_ref, dv_ref, z_ref, rope_ref, gq_ref, gkv_ref, up_ref, _, dz_ref, dup_ref, dgq_ref, dgkv_ref):
        i = pl.program_id(0)
        cth, s1, s2 = rope_ref[:, 0:128], rope_ref[:, 128:256], rope_ref[:, 256:384]
        dqb = _rope_transposed(dq_ref[...] * Q_SCALE, _lanes8(cth), _lanes8(s1), _lanes8(s2), w8).astype(BF16)
        dq_chunks = _chunks(dqb)
        cq = z_ref[:, 0:256].astype(F32)
        qn, vjp_q = jax.vjp(_rms, cq, gq_ref[...])
        dcq, dgq = vjp_q(sum(_dot_nt(dqk, up_ref[k, uq, :]) for k, dqk in enumerate(dq_chunks)))
        _accumulate(i, dgq_ref, dgq)

        dk = dk_ref[...]
        dkr = sum(dk[:, HEAD_PAD * h:HEAD_PAD * (h + 1)] for h in range(HEADS))
        dkr = _rope_transposed(dkr, cth, s1, s2, HEAD_PAD)
        lane = lax.broadcasted_iota(jnp.int32, (1, HEAD_PAD), 1)
        dkr = jnp.where((lane >= QK_NOPE) & (lane < QK_NOPE + QK_ROPE), dkr, 0.0)
        dkb, dvb = dk.astype(BF16), dv_ref[...].astype(BF16)
        dk_chunks, dv_chunks = _chunks(dkb), _chunks(dvb, width=2 * V_DIM)
        ckv = z_ref[:, 256:384].astype(F32)
        kvn, vjp_kv = jax.vjp(_rms, ckv, gkv_ref[...])
        dckv, dgkv = vjp_kv(sum(_dot_nt(dk_chunks[k], up_ref[k, keys, :]) + _dot_nt(dv_chunks[k], up_ref[k, values, 0:2 * V_DIM])
                                for k in range(N_CHIPS)))
        _accumulate(i, dgkv_ref, dgkv)
        dz_ref[...] = jnp.concatenate([dcq, dckv, dkr], axis=1).astype(BF16)
        qnb, kvnb = qn.astype(BF16), kvn.astype(BF16)
        d_uq, d_keys, d_values = _chunks(_dot_tn(qnb, dqb)), _chunks(_dot_tn(kvnb, dkb)), _chunks(_dot_tn(kvnb, dvb), width=2 * V_DIM)
        for k in range(N_CHIPS):
            padded = jnp.concatenate([d_values[k], jnp.zeros((128, 256 - 2 * V_DIM), F32)], axis=1)
            _accumulate(i, dup_ref.at[k], jnp.concatenate([d_uq[k], d_keys[k], padded], axis=0))

    return pl.pallas_call(
        body, name="mla_prep_bwd", grid=(rows // t,),
        in_specs=[cur(w8), cur(w8), cur(512), cur(512, CQ // 512), cur(384), layer((1, 256)), layer((1, 128)),
                  _misc_spec(misc, M_UQ, M_WO - M_UQ), ANY],
        out_specs=[cur(512, CQ // 512), full((N_CHIPS, M_WO - M_UQ, 256)), full((1, 256)), full((1, 128))],
        out_shape=[jax.ShapeDtypeStruct((rows, ZB), BF16), jax.ShapeDtypeStruct((N_CHIPS, M_WO - M_UQ, 256), F32),
                   jax.ShapeDtypeStruct((1, 256), F32), jax.ShapeDtypeStruct((1, 128), F32)],
        input_output_aliases={8: 0}, compiler_params=_params(("arbitrary",)),
    )(dq, dk, dv, z_br, rope, gq, gkv, misc[0], dz_buf)


def prenorm_bwd(dz_br, w_br, dh_gl, hres, gpre, dh_next, li):
    rows, d = hres.shape
    t = ROW_TILE
    cur, _, _, full, layer = _tile_specs(t, rows // HALO, li)

    def body(dz_ref, w_ref, dp_ref, x_ref, g_ref, dn_ref, dx_ref, dg_ref):
        i = pl.program_id(0)
        dh = _dot_nt(dz_ref[...], w_ref[...]) + dp_ref[...]
        _, vjp = jax.vjp(_rms, x_ref[...], g_ref[...])
        dx, dg = vjp(dh)
        dx_ref[...] = dx + dn_ref[...]
        _accumulate(i, dg_ref, dg)

    return pl.pallas_call(
        body, name="prenorm_bwd", grid=(rows // t,), in_specs=[cur(ZB), layer((d, ZB), 0), cur(d), cur(d), layer((1, d)), cur(d)],
        out_specs=[cur(d), full((1, d))], out_shape=[jax.ShapeDtypeStruct((rows, d), F32), jax.ShapeDtypeStruct((1, d), F32)],
        compiler_params=_params(("arbitrary",)),
    )(dz_br, w_br, dh_gl, hres, gpre, dh_next)


def _mesh_position():
    return lax.axis_index("x"), lax.axis_index("y"), lax.axis_index("c")


def chip_exchange(src, gather, name):
    block = src.shape if gather else src.shape[1:]

    def body(src_ref, dst_ref, send_sems, recv_sems, local_sem):
        x, y, c = _mesh_position()
        me = 2 * x + y
        peers = ((1 - x, y), (x, 1 - y), (1 - x, 1 - y))

        def part(k):
            return src_ref if gather else src_ref.at[k]

        def copy(j, slot):
            px, py = peers[j]
            return pltpu.make_async_remote_copy(src_ref=part(2 * px + py), dst_ref=dst_ref.at[slot], send_sem=send_sems.at[j],
                                                recv_sem=recv_sems.at[j], device_id=(px, py, c), device_id_type=MESH)

        local = pltpu.make_async_copy(part(me), dst_ref.at[me], local_sem)
        local.start()
        sends = [copy(j, me) for j in range(3)]
        for cp in sends:
            cp.start()
        for j, (px, py) in enumerate(peers):
            copy(j, 2 * px + py).wait_recv()
        for cp in sends:
            cp.wait_send()
        local.wait()

    return pl.pallas_call(
        body, name=name, in_specs=[pl.BlockSpec(memory_space=pl.ANY)], out_specs=pl.BlockSpec(memory_space=pl.ANY),
        out_shape=jax.ShapeDtypeStruct((N_CHIPS,) + tuple(block), src.dtype),
        scratch_shapes=[pltpu.SemaphoreType.DMA((3,)), pltpu.SemaphoreType.DMA((3,)), pltpu.SemaphoreType.DMA(())],
    )(src)


def sibling_swap(src, name):
    def body(src_ref, dst_ref, send_sem, recv_sem):
        x, y, c = _mesh_position()
        cp = pltpu.make_async_remote_copy(src_ref=src_ref, dst_ref=dst_ref, send_sem=send_sem, recv_sem=recv_sem,
                                          device_id=(x, y, 1 - c), device_id_type=MESH)
        cp.start()
        cp.wait()

    return pl.pallas_call(
        body, name=name, in_specs=[pl.BlockSpec(memory_space=pl.ANY)], out_specs=pl.BlockSpec(memory_space=pl.ANY),
        out_shape=jax.ShapeDtypeStruct(src.shape, src.dtype),
        scratch_shapes=[pltpu.SemaphoreType.DMA(()), pltpu.SemaphoreType.DMA(())],
    )(src)


def _comm_call(body, name, n_in, out_shapes, n_sems):
    return pl.pallas_call(
        body, name=name, in_specs=[ANY] * n_in, out_specs=[ANY] * len(out_shapes), out_shape=out_shapes,
        scratch_shapes=[pltpu.SemaphoreType.DMA((n,)) for n in n_sems])


def _row_halves(c, rows):
    half = rows // 2
    return pl.ds(pl.multiple_of(c * half, 16), half), pl.ds(pl.multiple_of((1 - c) * half, 16), half)


def _peers():
    x, y, c = _mesh_position()
    return x, y, c, 2 * x + y, ((1 - x, y), (x, 1 - y), (1 - x, 1 - y))


def _gather_ops(src, dst, sems, layer, own_copy):
    ici_send, ici_recv, d2d_send, d2d_recv, own_sems = sems
    n = len(src)

    def fetch(a, j, slot):
        x, y, c, _, peers = _peers()
        px, py = peers[j]
        mine, _ = _row_halves(c, src[a].shape[1])
        return pltpu.make_async_remote_copy(src_ref=src[a].at[layer, mine], dst_ref=dst[a].at[layer, slot, mine], send_sem=ici_send.at[3 * a + j],
                                            recv_sem=ici_recv.at[3 * a + j], device_id=(px, py, c), device_id_type=MESH)

    def forward(a, j, sibling_half):
        x, y, c, _, peers = _peers()
        px, py = peers[j]
        part = dst[a].at[layer, 2 * px + py, _row_halves(c, src[a].shape[1])[1 if sibling_half else 0]]
        return pltpu.make_async_remote_copy(src_ref=part, dst_ref=part, send_sem=d2d_send.at[3 * a + j], recv_sem=d2d_recv.at[3 * a + j],
                                            device_id=(x, y, 1 - c), device_id_type=MESH)

    def own(a):
        return pltpu.make_async_copy(src[a].at[layer], dst[a].at[layer, _peers()[3]], own_sems.at[a])

    def start():
        me = _peers()[3]
        for a in range(n):
            if own_copy:
                own(a).start()
            for j in range(3):
                fetch(a, j, me).start()

    def finish():
        peers = _peers()[4]
        for j, (px, py) in enumerate(peers):
            for a in range(n):
                fetch(a, j, 2 * px + py).wait_recv()
                forward(a, j, False).start()
        for j in range(3):
            for a in range(n):
                forward(a, j, True).wait_recv()
        for j in range(3):
            for a in range(n):
                fetch(a, j, 0).wait_send()
                forward(a, j, False).wait_send()
        if own_copy:
            for a in range(n):
                own(a).wait()

    return start, finish


def _exchange_ops(src, dst, sems, layer):
    send_sems, recv_sems, own_sems = sems
    n = len(src)

    def copy(a, j, slot):
        x, y, c, _, peers = _peers()
        px, py = peers[j]
        return pltpu.make_async_remote_copy(src_ref=src[a].at[2 * px + py], dst_ref=dst[a].at[layer, slot], send_sem=send_sems.at[3 * a + j],
                                            recv_sem=recv_sems.at[3 * a + j], device_id=(px, py, c), device_id_type=MESH)

    def own(a):
        me = _peers()[3]
        return pltpu.make_async_copy(src[a].at[me], dst[a].at[layer, me], own_sems.at[a])

    def start():
        me = _peers()[3]
        for a in range(n):
            own(a).start()
            for j in range(3):
                copy(a, j, me).start()

    def finish():
        peers = _peers()[4]
        for j, (px, py) in enumerate(peers):
            for a in range(n):
                copy(a, j, 2 * px + py).wait_recv()
        for j in range(3):
            for a in range(n):
                copy(a, j, 0).wait_send()
        for a in range(n):
            own(a).wait()

    return start, finish


GATHER_SEMS = lambda n: [pltpu.SemaphoreType.DMA((3 * n,))] * 4 + [pltpu.SemaphoreType.DMA((n,))]
EXCHANGE_SEMS = lambda n: [pltpu.SemaphoreType.DMA((3 * n,))] * 2 + [pltpu.SemaphoreType.DMA((n,))]


def gather_layer(srcs, dsts, layer, name):
    n = len(srcs)

    def body(*refs):
        start, finish = _gather_ops(refs[:n], refs[2 * n:3 * n], refs[3 * n:], layer, False)
        start()
        finish()

    return pl.pallas_call(
        body, name=name, in_specs=[ANY] * (2 * n), out_specs=[ANY] * n, out_shape=[jax.ShapeDtypeStruct(d.shape, d.dtype) for d in dsts],
        input_output_aliases={n + a: a for a in range(n)}, scratch_shapes=GATHER_SEMS(n),
    )(*srcs, *dsts)


def exchange_layer(ss, dsts, layer, name):
    n = len(ss)

    def body(*refs):
        start, finish = _exchange_ops(refs[:n], refs[2 * n:3 * n], refs[3 * n:], layer)
        start()
        finish()

    return pl.pallas_call(
        body, name=name, in_specs=[ANY] * (2 * n), out_specs=[ANY] * n, out_shape=[jax.ShapeDtypeStruct(d.shape, d.dtype) for d in dsts],
        input_output_aliases={n + a: a for a in range(n)}, scratch_shapes=EXCHANGE_SEMS(n),
    )(*ss, *dsts)


def _swap_ops(src, dst, sems):
    send_sems, recv_sems = sems

    def copy(a):
        x, y, c = _mesh_position()
        return pltpu.make_async_remote_copy(src_ref=src[a].at[:, _row_halves(c, src[a].shape[1])[1]], dst_ref=dst[a], send_sem=send_sems.at[a],
                                            recv_sem=recv_sems.at[a], device_id=(x, y, 1 - c), device_id_type=MESH)

    def start():
        for a in range(len(src)):
            copy(a).start()

    def finish():
        for a in range(len(src)):
            copy(a).wait()

    return start, finish


def swap_row_halves(ps, name):
    n = len(ps)

    def body(*refs):
        start, finish = _swap_ops(refs[:n], refs[n:2 * n], refs[2 * n:])
        start()
        finish()

    outs = [jax.ShapeDtypeStruct((p.shape[0], p.shape[1] // 2, p.shape[2]), p.dtype) for p in ps]
    return _comm_call(body, name, n, outs, (n, n))(*ps)


def add_row_half(p, r, c, name):
    n, half, cols = r.shape
    rb = _row_block(half, cols, 2)
    steps = half // rb

    def body(c_ref, p_ref, r_ref, o_ref):
        o_ref[...] = (p_ref[...].astype(F32) + r_ref[...].astype(F32)).astype(BF16)

    return pl.pallas_call(
        body, name=name, out_shape=jax.ShapeDtypeStruct(r.shape, BF16),
        grid_spec=pltpu.PrefetchScalarGridSpec(
            num_scalar_prefetch=1, grid=(n, steps),
            in_specs=[pl.BlockSpec((1, rb, cols), lambda k, i, c_ref: (k, c_ref[0] * steps + i, 0)),
                      pl.BlockSpec((1, rb, cols), lambda k, i, c_ref: (k, i, 0))],
            out_specs=pl.BlockSpec((1, rb, cols), lambda k, i, c_ref: (k, i, 0))),
        compiler_params=_params(("parallel", "parallel")),
    )(jnp.reshape(c, (1,)).astype(jnp.int32), p, r)


def sum_row_halves(l, c, name):
    layers, n, half, cols = l.shape
    rb = _row_block(half, cols, 4)
    steps = half // rb

    def body(c_ref, l_ref, o_ref):
        acc = l_ref[0, 0].astype(F32)
        for s in range(1, n):
            acc = acc + l_ref[0, s].astype(F32)
        o_ref[0] = acc

    return pl.pallas_call(
        body, name=name, out_shape=jax.ShapeDtypeStruct((layers, 2 * half, cols), F32),
        grid_spec=pltpu.PrefetchScalarGridSpec(
            num_scalar_prefetch=1, grid=(layers, steps), in_specs=[pl.BlockSpec((1, n, rb, cols), lambda a, i, c_ref: (a, 0, i, 0))],
            out_specs=pl.BlockSpec((1, rb, cols), lambda a, i, c_ref: (a, c_ref[0] * steps + i, 0))),
        compiler_params=_params(("parallel", "parallel")),
    )(jnp.reshape(c, (1,)).astype(jnp.int32), l)


def share_row_halves(gs, name):
    n = len(gs)

    def body(*refs):
        dst = refs[n:2 * n]
        send_sems, recv_sems = refs[2 * n:]
        x, y, c = _mesh_position()

        def copy(a, sibling_half):
            part = dst[a].at[:, _row_halves(c, dst[a].shape[1])[1 if sibling_half else 0]]
            return pltpu.make_async_remote_copy(src_ref=part, dst_ref=part, send_sem=send_sems.at[a], recv_sem=recv_sems.at[a],
                                                device_id=(x, y, 1 - c), device_id_type=MESH)

        for a in range(n):
            copy(a, False).start()
        for a in range(n):
            copy(a, True).wait_recv()
        for a in range(n):
            copy(a, False).wait_send()

    return pl.pallas_call(
        body, name=name, in_specs=[ANY] * n, out_specs=[ANY] * n, out_shape=[jax.ShapeDtypeStruct(g.shape, g.dtype) for g in gs],
        input_output_aliases={a: a for a in range(n)}, scratch_shapes=[pltpu.SemaphoreType.DMA((n,)), pltpu.SemaphoreType.DMA((n,))],
    )(*gs)


def _row_block(rows, cols, itemsize):
    best = 16
    for rb in range(16, rows + 1, 16):
        if rows % rb == 0 and rb * cols * itemsize <= 2 * 1024 * 1024:
            best = rb
    assert rows % best == 0, (rows, cols)
    return best


def _comm_block(rows):
    return 1024 if rows % 1024 == 0 else rows


def sum_slots(buf, name):
    n, r, c = buf.shape
    rb = _comm_block(r)

    def body(b_ref, o_ref):
        acc = b_ref[0].astype(F32)
        for s in range(1, n):
            acc = acc + b_ref[s].astype(F32)
        o_ref[...] = acc

    return pl.pallas_call(
        body, name=name, grid=(r // rb,), in_specs=[pl.BlockSpec((n, rb, c), lambda i: (0, i, 0))],
        out_specs=pl.BlockSpec((rb, c), lambda i: (i, 0)), out_shape=jax.ShapeDtypeStruct((r, c), F32),
        compiler_params=_params(("parallel",)),
    )(buf)


def add_pair(a, b, out_dtype, name):
    shape = a.shape
    a2, b2 = a.reshape(-1, shape[-1]), b.reshape(-1, shape[-1])
    r, c = a2.shape
    rb = _comm_block(r)

    def body(a_ref, b_ref, o_ref):
        o_ref[...] = (a_ref[...].astype(F32) + b_ref[...].astype(F32)).astype(out_dtype)

    out = pl.pallas_call(
        body, name=name, grid=(r // rb,), in_specs=[pl.BlockSpec((rb, c), lambda i: (i, 0))] * 2,
        out_specs=pl.BlockSpec((rb, c), lambda i: (i, 0)), out_shape=jax.ShapeDtypeStruct((r, c), out_dtype),
        compiler_params=_params(("parallel",)),
    )(a2, b2)
    return out.reshape(shape)


def adamw(w, g, m, v):
    shape = w.shape
    cols = shape[-1]
    rows = math.prod(shape[:-1])
    if rows * cols <= 256 * 1024:
        rb, cb = rows, cols
    else:
        rb = max(r for r in range(8, 2049, 8) if rows % r == 0)
        cb = cols if rb * cols * 4 <= 2 * 1024 * 1024 else 256
    assert rows % rb == 0 and cols % cb == 0, shape

    def body(w_ref, g_ref, m_ref, v_ref, d_ref, nm_ref, nv_ref):
        g_ = g_ref[...]
        nm = ADAM_B1 * m_ref[...] + (1.0 - ADAM_B1) * g_
        nv = ADAM_B2 * v_ref[...] + (1.0 - ADAM_B2) * (g_ * g_)
        m_hat = nm / (1.0 - ADAM_B1 ** ADAM_STEP)
        v_hat = nv / (1.0 - ADAM_B2 ** ADAM_STEP)
        d_ref[...] = -ADAM_LR * (m_hat / (jnp.sqrt(v_hat) + ADAM_EPS) + ADAM_WD * w_ref[...])
        nm_ref[...] = nm
        nv_ref[...] = nv

    spec = pl.BlockSpec((rb, cb), lambda i, j: (i, j))
    outs = pl.pallas_call(
        body, name="adamw", grid=(rows // rb, cols // cb), in_specs=[spec] * 4, out_specs=[spec] * 3,
        out_shape=[jax.ShapeDtypeStruct((rows, cols), F32)] * 3, compiler_params=_params(("parallel", "parallel")),
    )(*(a.reshape(rows, cols) for a in (w, g, m, v)))
    return tuple(o.reshape(shape) for o in outs)


def _pack(arrays, dtype, row_multiple):
    flat = jnp.concatenate([a.astype(dtype).reshape(-1) for a in arrays])
    per = LANES * row_multiple
    total = -(-flat.shape[0] // per) * per
    return jnp.pad(flat, (0, total - flat.shape[0])).reshape(total // LANES, LANES)


def _unpack(buf, shapes):
    flat = buf.reshape(-1)
    out, off = [], 0
    for s in shapes:
        n = math.prod(s)
        out.append(flat[off:off + n].reshape(s))
        off += n
    return out


def _input_weights(blocks):
    c0, c1, c2, c3 = (blocks[..., k, :, :] for k in range(N_CHIPS))
    pad = lambda n: jnp.zeros(c0.shape[:-1] + (n,), blocks.dtype)
    w_br = jnp.concatenate([c1[..., 376:1400], c0[..., 0:896], pad(64), c0[..., 896:928], pad(32), c0[..., 928:], c1[..., 0:376]], axis=-1)
    return w_br, jnp.concatenate([c1[..., 1400:], c2, c3], axis=-1)


def _input_weights_inverse(dw_br, dw_gl):
    c0 = jnp.concatenate([dw_br[..., 1024:1920], dw_br[..., 1984:2016], dw_br[..., 2048:2952]], axis=-1)
    c1 = jnp.concatenate([dw_br[..., 2952:ZB], dw_br[..., 0:1024], dw_gl[..., 0:432]], axis=-1)
    return jnp.stack([c0, c1, dw_gl[..., 432:2264], dw_gl[..., 2264:]], axis=-3)


def _block_diag(pw):
    zeros = lambda n: jnp.zeros(pw.shape[:-3] + (64, n), pw.dtype)
    rows = [jnp.concatenate([zeros(64 * g), pw[..., g, :, :], zeros(64 * (3 - g))], axis=-1) for g in range(4)]
    return jnp.concatenate(rows, axis=-2)


def _block_diag_inverse(d):
    return jnp.stack([d[..., 64 * g:64 * (g + 1), 64 * g:64 * (g + 1)] for g in range(4)], axis=-3)


def _pad_rows(a, n):
    return jnp.pad(a, ((0, n - a.shape[0]), (0, 0)))


def _rope_tables(rows):
    inv = 1.0 / (ROPE_THETA ** (jnp.arange(0, QK_ROPE, 2, dtype=F32) / QK_ROPE))
    ang = jnp.arange(rows, dtype=F32)[:, None] * inv[None, :]
    cos, sin = jnp.cos(ang), jnp.sin(ang)
    one, zero = jnp.ones((rows, 1), F32), jnp.zeros((rows, 1), F32)
    rep = lambda a, n: jnp.broadcast_to(a, (rows, n))
    c = jnp.concatenate([rep(one, 64), cos, cos, rep(one, 32)], axis=1)
    s1 = jnp.concatenate([rep(zero, 64), -sin, rep(zero, 48)], axis=1)
    s2 = jnp.concatenate([rep(zero, 80), sin, rep(zero, 32)], axis=1)
    return jnp.concatenate([c, s1, s2], axis=1)


def _misc_block(parts):
    lead = parts["w_uq"].shape[:-2]
    pad_last = lambda a, n: jnp.pad(a, [(0, 0)] * (a.ndim - 1) + [(0, n - a.shape[-1])])
    uq = pad_last(parts["w_uq"].reshape(lead + (256, 2, QK_NOPE + QK_ROPE)), HEAD_PAD).reshape(lead + (256, 256))
    kv = parts["w_ukv"].reshape(lead + (128, 2, QK_NOPE + V_DIM))
    keys = pad_last(kv[..., :QK_NOPE], HEAD_PAD).reshape(lead + (128, 256))
    values = pad_last(kv[..., QK_NOPE:].reshape(lead + (128, 2 * V_DIM)), 256)
    wo = jnp.swapaxes(parts["w_o"].reshape(lead + (256, N_CHIPS, 256)), -3, -2).reshape(lead + (D_MODEL, 256))
    gap = jnp.zeros(lead + (M_UQ - M_SC - 256, 256), uq.dtype)
    return jnp.concatenate([parts["w_out_mla"], parts["w_out_pool"], parts["w_out_conf"], parts["w_out_sc"], gap, uq, keys, values, wo],
                           axis=-2)


def _misc_unblock(block):
    lead = block.shape[:-2]
    rows = lambda lo, n: block[..., lo:lo + n, :]
    uq = rows(M_UQ, 256).reshape(lead + (256, 2, HEAD_PAD))[..., :QK_NOPE + QK_ROPE].reshape(lead + (256, 2 * (QK_NOPE + QK_ROPE)))
    keys = rows(M_UKVK, 128).reshape(lead + (128, 2, HEAD_PAD))[..., :QK_NOPE]
    values = rows(M_UKVV, 128)[..., :2 * V_DIM].reshape(lead + (128, 2, V_DIM))
    wo = jnp.swapaxes(rows(M_WO, D_MODEL).reshape(lead + (N_CHIPS, 256, 256)), -3, -2).reshape(lead + (256, D_MODEL))
    return dict(w_out_mla=rows(M_MLA, 512), w_out_pool=rows(M_POOL, 256), w_out_conf=rows(M_CONF, 256), w_out_sc=rows(M_SC, 256), w_uq=uq,
                w_ukv=jnp.concatenate([keys, values], axis=-1).reshape(lead + (128, 256)), w_o=wo)


def _to_chip_blocks(name, a):
    if name == "w_o":
        return a.reshape(a.shape[:-2] + (N_CHIPS, a.shape[-2] // N_CHIPS, a.shape[-1]))
    return jnp.swapaxes(a.reshape(a.shape[:-1] + (N_CHIPS, a.shape[-1] // N_CHIPS)), -3, -2)


def _from_chip_blocks(name, b):
    if name == "w_o":
        return b.reshape(b.shape[:-3] + (N_CHIPS * b.shape[-2], b.shape[-1]))
    s = jnp.swapaxes(b, -3, -2)
    return s.reshape(s.shape[:-2] + (N_CHIPS * s.shape[-1],))


LARGE = ("w_in",) + MISC


def gather_small(shards):
    small = chip_exchange(_pack([shards[n] for n, _, _ in SHARDED_SMALL], F32, 8), True, "gather_small_ici")
    per_chip = [_unpack(small[k], [s for _, s, _ in SHARDED_SMALL]) for k in range(N_CHIPS)]
    return {name: jnp.concatenate([per_chip[k][idx] for k in range(N_CHIPS)], axis=axis) for idx, (name, _, axis) in enumerate(SHARDED_SMALL)}


class LocalWeights:
    def __init__(self, full):
        self.w_in = _to_chip_blocks("w_in", full["w_in"])
        self.misc = _misc_block({n: _to_chip_blocks(n, full[n]) for n in MISC}).astype(BF16)
        self.grads = [None] * DEPTH

    def layer(self, i):
        return self.w_in[i], (self.misc, i)

    def gather_with_attention(self, i):
        return None

    def gathered(self, dsts):
        pass

    def exchange_with_attention(self):
        return None

    def exchanged(self, dsts):
        pass

    def swap_with_postnorm(self):
        return None

    def swapped(self, rs):
        pass

    def put_grads(self, i, w_in, misc):
        self.grads[i] = (w_in, misc)

    def reduced(self):
        out = {n: _from_chip_blocks(n, b) for n, b in _misc_unblock(jnp.stack([m for _, m in self.grads])).items()}
        out["w_in"] = _from_chip_blocks("w_in", jnp.stack([w for w, _ in self.grads]))
        return out


class MeshWeights:
    def __init__(self, shards, c, chip):
        self.c, self.chip = c, chip
        self.srcs = [shards["w_in"].astype(BF16), _misc_block({n: shards[n] for n in MISC}).astype(BF16)]
        dsts = [lax.empty((DEPTH, N_CHIPS) + s.shape[1:], BF16) for s in self.srcs]
        self.dsts = gather_layer(self.srcs, dsts, 0, "gather_layer")
        self.landed = [lax.empty((DEPTH, N_CHIPS, s.shape[1] // 2, s.shape[2]), BF16) for s in self.srcs]
        self.pending = self.to_swap = None

    def layer(self, i):
        if i > 0:
            return self.dsts[0][i], (self.dsts[1], i)
        own = (jnp.arange(N_CHIPS) == self.chip)[:, None, None]
        w_in, misc = (jnp.where(own, s[0][None], d[0]) for s, d in zip(self.srcs, self.dsts))
        return w_in, (misc[None], 0)

    def gather_with_attention(self, i):
        return (self.srcs, self.dsts, i + 1) if i + 1 < DEPTH else None

    def gathered(self, dsts):
        if dsts:
            self.dsts = dsts

    def exchange_with_attention(self):
        return None if self.pending is None else (self.pending[0], self.landed, self.pending[1])

    def exchanged(self, dsts):
        if dsts:
            self.landed, self.pending = dsts, None

    def put_grads(self, i, w_in, misc):
        self.to_swap = ([w_in.astype(BF16), misc.astype(BF16)], i)

    def swap_with_postnorm(self):
        return None if self.to_swap is None else self.to_swap[0]

    def swapped(self, rs):
        if rs:
            ps, i = self.to_swap
            self.pending = ([add_row_half(p, r, self.c, "reduce_pair_%d" % a) for a, (p, r) in enumerate(zip(ps, rs))], i)
            self.to_swap = None

    def reduced(self):
        self.swapped(swap_row_halves(self.to_swap[0], "reduce_swap"))
        landed = exchange_layer(self.pending[0], self.landed, self.pending[1], "reduce_exchange")
        gs = [sum_row_halves(l, self.c, "reduce_sum_%d" % a) for a, l in enumerate(landed)]
        g_in, g_misc = share_row_halves(gs, "reduce_share")
        out = {"w_in": g_in}
        out.update(_misc_unblock(g_misc))
        return out


def reduce_small(grads, chip):
    names = [n for n, _ in REPLICATED] + [n for n, _, _ in SHARDED_SMALL]
    buf = _pack([grads[n] for n in names], F32, 8)
    chip_sum = add_pair(buf, sibling_swap(buf, "reduce_small_d2d"), F32, "reduce_small_pair")
    total = sum_slots(chip_exchange(chip_sum, True, "reduce_small_ici"), "reduce_small_sum")
    out = dict(zip(names, _unpack(total, [grads[n].shape for n in names])))
    for name, shape, axis in SHARDED_SMALL:
        out[name] = lax.dynamic_slice_in_dim(out[name], chip * shape[axis], shape[axis], axis)
    return out


def _prepare_small(w):
    row = lambda a: a[:, None, :]
    conf_vec = jnp.concatenate([row(w["conf_dw_b"]), row(w["conf_ln_g"]), row(w["conf_ln_b"]), jnp.zeros((DEPTH, 5, 256), F32)], axis=1)
    return dict(
        gpre=row(w["pre_norm_g"]), bias=row(w["gate_bias"]), pwbd=_block_diag(w["pool_w"]).astype(BF16), pscale=row(w["pool_scale"]),
        gq=row(w["q_norm_g"]), gkv=row(w["kv_norm_g"]), conf_w=jnp.pad(w["conf_dw_w"].astype(F32), ((0, 0), (0, 32 - CONF_K), (0, 0))),
        conf_vec=conf_vec, sc_w=jnp.pad(w["sc_dw_w"].astype(F32), ((0, 0), (0, 8 - SC_K), (0, 0))), gpost=row(w["post_norm_g"]))


def _prepare_layer(w_in_blocks, misc):
    w_br, w_gl = _input_weights(w_in_blocks)
    one = lambda a: a.astype(BF16)[None]
    return dict(w_br=one(w_br), w_gl=one(w_gl), misc=misc)


def local_step(x, target, w, large):
    seq = x.shape[0]
    length = N_META + seq
    rows = -(-length // ROW_TILE) * ROW_TILE
    bt = _big_tile(rows)
    hres = _pad_rows(jnp.concatenate([w["meta_tokens"].astype(F32), x], axis=0), rows)
    tgt = jnp.pad(target, ((N_META, rows - length), (0, 0)))
    rope = _rope_tables(rows)
    sw = _prepare_small(w)

    saved = []
    for i in range(DEPTH):
        lw = _prepare_layer(*large.layer(i))
        z_br, hb, hbt = prenorm_project(hres, sw["gpre"], lw["w_br"], i)
        z_gl = matmul(hb, lw["w_gl"], "nn", BF16, bt, 2048, D_MODEL, "project_gates", b_layer=0)
        ua, uc, ud, q, k, v = branches_fwd(z_br, rope, sw["pwbd"], sw["pscale"], sw["gq"], sw["gkv"], lw["misc"], sw["conf_w"],
                                           sw["conf_vec"], sw["sc_w"], i)
        o_att, lse, dsts = attention_fwd(q, k, v, large.gather_with_attention(i))
        large.gathered(dsts)
        ub, mb, o, hnew = merge_fwd(ua, o_att, uc, ud, z_br, z_gl, sw["bias"], lw["misc"], sw["gpost"], hres, i)
        saved.append(dict(lw=lw, hres=hres, hbt=hbt, z_br=z_br, z_gl=z_gl, ua=ua, ub=ub, uc=uc, ud=ud, q=q, k=k, v=v, o_att=o_att,
                          lse=lse, mb=mb, o=o))
        hres = hnew

    dh, total = loss_head(hres, tgt, seq)

    g = {n: [None] * DEPTH for n in ("gpre", "bias", "pwbd", "pscale", "gq", "gkv", "conf_w", "conf_vec", "sc_w", "gpost")}
    for i in reversed(range(DEPTH)):
        s = saved[i]
        lw = s["lw"]
        dm, dwo, g["gpost"][i], rs = postnorm_bwd(dh, s["o"], s["mb"], lw["misc"], sw["gpost"], i, large.swap_with_postnorm())
        large.swapped(rs)
        dz_br = lax.empty((rows, ZB), BF16)
        dua, do, duc, dud, dz_gl, dwout, g["bias"][i], dz_br, delta = merge_bwd(
            dm, s["ua"], s["ub"], s["uc"], s["ud"], s["z_gl"], sw["bias"], lw["misc"], s["o_att"], s["z_br"], dz_br, i)
        dz_br, g["pwbd"][i], g["pscale"][i], g["sc_w"][i] = pool_shortconv_bwd(s["z_br"], dua, dud, sw["pwbd"], sw["pscale"], sw["sc_w"],
                                                                              dz_br, i)
        dc, dz_br, g["conf_vec"][i] = conformer_bwd_tail(s["z_br"], duc, sw["conf_w"], sw["conf_vec"], dz_br, i)
        dz_br, g["conf_w"][i] = conformer_bwd_conv(s["z_br"], dc, sw["conf_w"], dz_br, i)
        dq, dk, dv, dsts = attention_bwd(s["q"], s["k"], s["v"], do, s["lse"], delta, large.exchange_with_attention())
        large.exchanged(dsts)
        dz_br, dwup, g["gq"][i], g["gkv"][i] = mla_prep_bwd(dq, dk, dv, s["z_br"], rope, sw["gq"], sw["gkv"], lw["misc"], dz_br, i)
        dw_br = matmul(s["hbt"], dz_br, "nn", BF16, D_MODEL, ZB // 2, bt, "grad_w_branch")
        dw_gl = matmul(s["hbt"], dz_gl, "nn", BF16, D_MODEL, 1024, bt, "grad_w_gates")
        dh_gl = matmul(dz_gl, lw["w_gl"], "nt", F32, bt, D_MODEL, 2048, "grad_h_gates", b_layer=0)
        dh, g["gpre"][i] = prenorm_bwd(dz_br, lw["w_br"], dh_gl, s["hres"], sw["gpre"], dh, i)
        gap = jnp.zeros((N_CHIPS, M_UQ - M_SC - 256, 256), F32)
        large.put_grads(i, _input_weights_inverse(dw_br, dw_gl), jnp.concatenate([dwout, gap, dwup, dwo], axis=1))

    g = {n: jnp.stack(parts) for n, parts in g.items()}
    grads = dict(
        meta_tokens=dh[:N_META], pre_norm_g=g["gpre"][:, 0], gate_bias=g["bias"][:, 0], pool_w=_block_diag_inverse(g["pwbd"]),
        pool_scale=g["pscale"][:, 0], q_norm_g=g["gq"][:, 0], kv_norm_g=g["gkv"][:, 0], conf_dw_w=g["conf_w"][:, :CONF_K],
        conf_dw_b=g["conf_vec"][:, 2], conf_ln_g=g["conf_vec"][:, 0], conf_ln_b=g["conf_vec"][:, 1], sc_dw_w=g["sc_w"][:, :SC_K],
        post_norm_g=g["gpost"][:, 0])
    return total[0, 0], dh[N_META:length], grads


def kernel(x, meta_tokens, pre_norm_g, w_in, gate_bias, pool_w, pool_scale, w_out_pool, q_norm_g, w_uq, kv_norm_g, w_ukv, w_out_mla, conf_dw_w, conf_dw_b, conf_ln_g, conf_ln_b, w_out_conf, sc_dw_w, w_out_sc, w_o, post_norm_g, loss_target, m_meta_tokens, m_pre_norm_g, m_w_in, m_gate_bias, m_pool_w, m_pool_scale, m_w_out_pool, m_q_norm_g, m_w_uq, m_kv_norm_g, m_w_ukv, m_w_out_mla, m_conf_dw_w, m_conf_dw_b, m_conf_ln_g, m_conf_ln_b, m_w_out_conf, m_sc_dw_w, m_w_out_sc, m_w_o, m_post_norm_g, v_meta_tokens, v_pre_norm_g, v_w_in, v_gate_bias, v_pool_w, v_pool_scale, v_w_out_pool, v_q_norm_g, v_w_uq, v_kv_norm_g, v_w_ukv, v_w_out_mla, v_conf_dw_w, v_conf_dw_b, v_conf_ln_g, v_conf_ln_b, v_w_out_conf, v_sc_dw_w, v_w_out_sc, v_w_o, v_post_norm_g):
    args = locals()
    weights = {n: args[n] for n in WEIGHT_ORDER}
    c = lax.axis_index("c")
    chip = 2 * lax.axis_index("x") + lax.axis_index("y")

    small = {n: weights[n] for n, _ in REPLICATED}
    small.update(gather_small(weights))
    large = MeshWeights(weights, c, chip)
    total, dx, grads = local_step(x[0], loss_target[0], small, large)
    loss = lax.psum(total * (0.5 / D_MODEL), ("x", "y", "c"))

    reduced = large.reduced()
    reduced.update(reduce_small(grads, chip))

    flip = lambda a: jnp.swapaxes(a, 1, 2)
    deltas, new_m, new_v = [], [], []
    for n in WEIGHT_ORDER:
        operands = (weights[n], reduced[n], args["m_" + n], args["v_" + n])
        if n == "w_in":
            operands = (flip(operands[0]), lax.optimization_barrier(flip(operands[1])), flip(operands[2]), flip(operands[3]))
            reduced[n] = flip(operands[1])
        d, nm, nv = adamw(*operands)
        if n == "w_in":
            d, nm, nv = flip(d), flip(nm), flip(nv)
        deltas.append(d)
        new_m.append(nm)
        new_v.append(nv)
    return (loss, dx[None], *[reduced[n] for n in WEIGHT_ORDER], *deltas, *new_m, *new_v)
```

```python
import functools
import math

import jax
import jax.numpy as jnp
from jax import lax
from jax.experimental import pallas as pl
from jax.experimental.pallas import tpu as pltpu

F32 = jnp.float32
BF16 = jnp.bfloat16

D_MODEL = 1024
DEPTH = 4
N_META = 16
EPS = 1e-6
HEADS = 8
QK_NOPE = 64
QK_ROPE = 32
V_DIM = 64
HEAD_PAD = 128
ROPE_THETA = 10000.0
Q_SCALE = (QK_NOPE + QK_ROPE) ** -0.5
CONF_K = 31
SC_K = 3
IN_W = 7328
N_CHIPS = 4

ZB = 3328
ZG = 4096
BG, C2, XV, SG, PV, PG, CQ, CKV, KR, MG, CA, CGT, CG = (0, 256, 512, 768, 1024, 1280, 1536, 1792, 1920, 2048, 2560, 2816, 3072)

KEY_GROUP = 4
ROW_TILE = 384
HALO = 32
LANES = 128
VMEM_LIMIT = 56 * 1024 * 1024

ADAM_LR = 0.001
ADAM_B1 = 0.9
ADAM_B2 = 0.999
ADAM_EPS = 1e-08
ADAM_WD = 0.01
ADAM_STEP = 10

MESH = pl.DeviceIdType.MESH
ANY = pl.BlockSpec(memory_space=pl.ANY)

MISC = ("w_out_mla", "w_out_pool", "w_out_conf", "w_out_sc", "w_uq", "w_ukv", "w_o")
M_MLA, M_POOL, M_CONF, M_SC, M_UQ, M_UKVK, M_UKVV, M_WO, MISC_ROWS = 0, 512, 768, 1024, 1536, 1792, 1920, 2048, 3072
SHARDED_SMALL = (
    ("meta_tokens", (N_META, 256), 1),
    ("conf_dw_w", (DEPTH, CONF_K, 64), 2),
    ("sc_dw_w", (DEPTH, SC_K, 64), 2),
)
REPLICATED = (
    ("pre_norm_g", (DEPTH, D_MODEL)),
    ("gate_bias", (DEPTH, 4 * D_MODEL)),
    ("pool_w", (DEPTH, 4, 64, 64)),
    ("pool_scale", (DEPTH, 256)),
    ("q_norm_g", (DEPTH, 256)),
    ("kv_norm_g", (DEPTH, 128)),
    ("conf_dw_b", (DEPTH, 256)),
    ("conf_ln_g", (DEPTH, 256)),
    ("conf_ln_b", (DEPTH, 256)),
    ("post_norm_g", (DEPTH, D_MODEL)),
)
WEIGHT_ORDER = ("meta_tokens", "pre_norm_g", "w_in", "gate_bias", "pool_w", "pool_scale", "w_out_pool", "q_norm_g", "w_uq",
                "kv_norm_g", "w_ukv", "w_out_mla", "conf_dw_w", "conf_dw_b", "conf_ln_g", "conf_ln_b", "w_out_conf", "sc_dw_w",
                "w_out_sc", "w_o", "post_norm_g")


def _dot(a, b):
    return lax.dot_general(a, b, (((1,), (0,)), ((), ())), preferred_element_type=F32)


def _dot_nt(a, b):
    return lax.dot_general(a, b, (((1,), (1,)), ((), ())), preferred_element_type=F32)


def _dot_tn(a, b):
    return lax.dot_general(a, b, (((0,), (0,)), ((), ())), preferred_element_type=F32)


def _sigmoid(x):
    return jax.nn.sigmoid(x)


def _silu(x):
    return x * _sigmoid(x)


def _silu_grad(x):
    s = _sigmoid(x)
    return s * (1.0 + x * (1.0 - s))


def _rms(x, g):
    return x * lax.rsqrt(jnp.mean(x * x, axis=-1, keepdims=True) + EPS) * g


def _sh(x, d):
    return x if d == 0 else pltpu.roll(x, d, 0)


def _ash(x, d):
    return x if d == 0 else pltpu.roll(x, x.shape[0] - d, 0)


def _lanes8(t):
    return jnp.concatenate([t] * HEADS, axis=1)


def _pool_window_sums(v, shift):
    a2 = v + shift(v, 1)
    a4 = a2 + shift(a2, 2)
    a8 = a4 + shift(a4, 4)
    a16 = a8 + shift(a8, 8)
    lane = lax.broadcasted_iota(jnp.int32, v.shape, 1)
    return jnp.where(lane < 64, a2, jnp.where(lane < 128, a4, jnp.where(lane < 192, a8, a16)))


def _pool_counts(first_row, rows):
    pos = first_row + lax.broadcasted_iota(jnp.int32, (rows, 256), 0)
    lane = lax.broadcasted_iota(jnp.int32, (rows, 256), 1)
    width = jnp.where(lane < 64, 2, jnp.where(lane < 128, 4, jnp.where(lane < 192, 8, 16)))
    return jnp.maximum(jnp.minimum(pos + 1, width), 1).astype(F32)


def _params(sem=None):
    return pltpu.CompilerParams(dimension_semantics=sem, vmem_limit_bytes=VMEM_LIMIT)


def _tile_specs(t, n_halo_blocks, li=0):
    per = t // HALO

    def layer(shape, idx=li):
        return pl.BlockSpec((None,) + tuple(shape), lambda i: (idx,) + (0,) * len(shape))

    def cur(c, cb=0):
        return pl.BlockSpec((t, c), lambda i: (i, cb))

    def prev(c, cb=0):
        return pl.BlockSpec((HALO, c), lambda i: (jnp.maximum(i * per - 1, 0), cb))

    def nxt(c, cb=0):
        return pl.BlockSpec((HALO, c), lambda i: (jnp.minimum((i + 1) * per, n_halo_blocks - 1), cb))

    def full(shape):
        return pl.BlockSpec(shape, lambda i: (0,) * len(shape))

    return cur, prev, nxt, full, layer


def _big_tile(rows):
    return rows // 3 if rows % (3 * LANES) == 0 else ROW_TILE


def matmul(a, b, mode, out_dtype, tm, tn, tk, name, b_layer=None):
    bs = b.shape if b_layer is None else b.shape[1:]
    lead = () if b_layer is None else (None,)
    pick = (lambda *ix: ix) if b_layer is None else (lambda *ix: (b_layer,) + ix)
    if mode == "nn":
        (m, k), n = a.shape, bs[1]
        a_spec = pl.BlockSpec((tm, tk), lambda i, j, kk: (i, kk))
        b_spec = pl.BlockSpec(lead + (tk, tn), lambda i, j, kk: pick(kk, j))
        dot = _dot
    elif mode == "nt":
        (m, k), n = a.shape, bs[0]
        a_spec = pl.BlockSpec((tm, tk), lambda i, j, kk: (i, kk))
        b_spec = pl.BlockSpec(lead + (tn, tk), lambda i, j, kk: pick(j, kk))
        dot = _dot_nt
    else:
        raise ValueError(mode)
    assert m % tm == 0 and n % tn == 0 and k % tk == 0, (a.shape, bs, tm, tn, tk)
    nk = k // tk

    def body(a_ref, b_ref, o_ref, *acc):
        if nk == 1:
            o_ref[...] = dot(a_ref[...], b_ref[...]).astype(out_dtype)
            return
        (acc_ref,) = acc
        kk = pl.program_id(2)

        @pl.when(kk == 0)
        def _():
            acc_ref[...] = dot(a_ref[...], b_ref[...])

        @pl.when((kk > 0) & (kk < nk - 1))
        def _():
            acc_ref[...] += dot(a_ref[...], b_ref[...])

        @pl.when(kk == nk - 1)
        def _():
            o_ref[...] = (acc_ref[...] + dot(a_ref[...], b_ref[...])).astype(out_dtype)

    return pl.pallas_call(
        body, name=name, grid=(m // tm, n // tn, nk), in_specs=[a_spec, b_spec],
        out_specs=pl.BlockSpec((tm, tn), lambda i, j, kk: (i, j)), out_shape=jax.ShapeDtypeStruct((m, n), out_dtype),
        scratch_shapes=[pltpu.VMEM((tm, tn), F32)] if nk > 1 else [], compiler_params=_params(("parallel", "parallel", "arbitrary")),
    )(a, b)


def prenorm_project(hres, g, w, li):
    rows, d = hres.shape
    n = w.shape[2]
    tm, tn = _big_tile(rows), n // 2

    def body(x_ref, g_ref, w_ref, z_ref, hb_ref, hbt_ref):
        @pl.when(pl.program_id(1) == 0)
        def _():
            h = _rms(x_ref[...], g_ref[...])
            hb_ref[...] = h.astype(BF16)
            hbt_ref[...] = h.T.astype(BF16)

        z_ref[...] = _dot(hb_ref[...], w_ref[...]).astype(BF16)

    return pl.pallas_call(
        body, name="prenorm_project", grid=(rows // tm, n // tn),
        in_specs=[pl.BlockSpec((tm, d), lambda i, j: (i, 0)), pl.BlockSpec((None, 1, d), lambda i, j: (li, 0, 0)),
                  pl.BlockSpec((None, d, tn), lambda i, j: (0, 0, j))],
        out_specs=[pl.BlockSpec((tm, tn), lambda i, j: (i, j)), pl.BlockSpec((tm, d), lambda i, j: (i, 0)),
                   pl.BlockSpec((d, tm), lambda i, j: (0, i))],
        out_shape=[jax.ShapeDtypeStruct((rows, n), BF16), jax.ShapeDtypeStruct((rows, d), BF16), jax.ShapeDtypeStruct((d, rows), BF16)],
        compiler_params=_params(("parallel", "arbitrary")),
    )(hres, g, w)


def _rope(q, c, s1, s2, width):
    return q * c + pltpu.roll(q, width - 16, 1) * s1 + pltpu.roll(q, 16, 1) * s2


def _rope_transposed(dq, c, s1, s2, width):
    return dq * c + pltpu.roll(dq * s1, 16, 1) + pltpu.roll(dq * s2, width - 16, 1)


def _conf_conv(g1, w_ref):
    acc = jnp.zeros_like(g1)
    for k in range(CONF_K):
        acc = acc + w_ref[k:k + 1, :] * _sh(g1, CONF_K - 1 - k)
    return acc


def _conf_tail(c, cg, lg, lb):
    mu = jnp.mean(c, axis=-1, keepdims=True)
    xc = c - mu
    var = jnp.mean(xc * xc, axis=-1, keepdims=True)
    n = xc * lax.rsqrt(var + EPS) * lg + lb
    return _silu(n) * _silu(cg)


def _misc_spec(misc, row0, rows):
    assert row0 % rows == 0
    return pl.BlockSpec((None, N_CHIPS, rows, 256), lambda i: (misc[1], 0, row0 // rows, 0))


def _chip_columns(x, w_ref, row0, rows, lanes=256):
    return jnp.concatenate([_dot(x, w_ref[k, row0:row0 + rows, 0:lanes]) for k in range(N_CHIPS)], axis=1)


def branches_fwd(z_br, rope, pwbd, pscale, gq, gkv, misc, conf_w, conf_vec, sc_w, li):
    rows = z_br.shape[0]
    t = ROW_TILE
    cur, prev, _, _, layer = _tile_specs(t, rows // HALO, li)

    def body(zc_ref, zp_ref, rope_ref, pw_ref, ps_ref, gq_ref, gkv_ref, up_ref, cw_ref, cv_ref, sw_ref,
             ua_ref, uc_ref, ud_ref, q_ref, k_ref, v_ref):
        i = pl.program_id(0)
        zp = jnp.where(i == 0, jnp.zeros(zp_ref.shape, zp_ref.dtype), zp_ref[...])

        def ext(lo, w=256):
            return jnp.concatenate([zp[:, lo:lo + w], zc_ref[:, lo:lo + w]], axis=0).astype(F32)

        def col(lo, w=256):
            return zc_ref[:, lo:lo + w].astype(F32)

        v = ext(PV)
        p = (_pool_window_sums(v, _sh) / _pool_counts(i * t - HALO, t + HALO) - v)[HALO:]
        ya = _dot(p.astype(BF16), pw_ref[...]) * ps_ref[...]
        ua_ref[...] = (ya * _silu(col(PG))).astype(BF16)

        g1 = ext(CA) * _sigmoid(ext(CGT))
        c = _conf_conv(g1, cw_ref)[HALO:] + cv_ref[0:1, :]
        uc_ref[...] = _conf_tail(c, col(CG), cv_ref[1:2, :], cv_ref[2:3, :]).astype(BF16)

        e = ext(C2) * ext(XV)
        f = jnp.zeros_like(e)
        for k in range(SC_K):
            f = f + sw_ref[k:k + 1, :] * _sh(e, SC_K - 1 - k)
        ud_ref[...] = (col(BG) * f[HALO:] * _silu(col(SG))).astype(BF16)

        cth, s1, s2 = rope_ref[:, 0:128], rope_ref[:, 128:256], rope_ref[:, 256:384]
        qn = _rms(col(CQ), gq_ref[...]).astype(BF16)
        q = _chip_columns(qn, up_ref, 0, 256)
        w8 = HEADS * HEAD_PAD
        q_ref[...] = (_rope(q, _lanes8(cth), _lanes8(s1), _lanes8(s2), w8) * Q_SCALE).astype(BF16)
        kvn = _rms(col(CKV, 128), gkv_ref[...]).astype(BF16)
        kr = _rope(col(KR, 128), cth, s1, s2, HEAD_PAD)
        k_ref[...] = (_chip_columns(kvn, up_ref, M_UKVK - M_UQ, 128) + _lanes8(kr)).astype(BF16)
        v_ref[...] = _chip_columns(kvn, up_ref, M_UKVV - M_UQ, 128, 2 * V_DIM).astype(BF16)

    outs = [jax.ShapeDtypeStruct((rows, 256), BF16)] * 3 + [jax.ShapeDtypeStruct((rows, 1024), BF16)] * 2 + [
        jax.ShapeDtypeStruct((rows, 512), BF16)]
    return pl.pallas_call(
        body, name="branches_fwd", grid=(rows // t,),
        in_specs=[cur(ZB), prev(ZB), cur(384), layer((256, 256)), layer((1, 256)), layer((1, 256)), layer((1, 128)),
                  _misc_spec(misc, M_UQ, M_WO - M_UQ), layer((32, 256)), layer((8, 256)), layer((8, 256))],
        out_specs=[cur(256), cur(256), cur(256), cur(1024), cur(1024), cur(512)], out_shape=outs,
        compiler_params=_params(("parallel",)),
    )(z_br, z_br, rope, pwbd, pscale, gq, gkv, misc[0], conf_w, conf_vec, sc_w)


def _head_lane_mask(h):
    lane = lax.broadcasted_iota(jnp.int32, (1, 2 * V_DIM), 1)
    return (lane >= V_DIM * h) & (lane < V_DIM * (h + 1))


def attention_fwd(q, k, v, gather=None):
    rows = q.shape[0]
    tq = ROW_TILE
    nq = rows // tq
    n = 0 if gather is None else len(gather[0])

    def body(*refs):
        if n:
            start, finish = _gather_ops(refs[3:3 + n], refs[5 + 2 * n:5 + 3 * n], refs[5 + 3 * n:], gather[2], True)
            pl.when((pl.program_id(0) == 0) & (pl.program_id(1) == 0))(start)
        compute(*refs[:3], *refs[3 + 2 * n:5 + 2 * n])
        if n:
            pl.when((pl.program_id(0) == HEADS // 2 - 1) & (pl.program_id(1) == nq - 1))(finish)

    def compute(q_ref, k_ref, v_ref, o_ref, lse_ref):
        i = pl.program_id(1)

        def head_step(h, tile, n_tiles, carry, masked):
            m, l, acc = carry
            width = n_tiles * tq
            r0 = pl.multiple_of(tile * tq, tq)
            kh = k_ref[pl.ds(r0, width), HEAD_PAD * h:HEAD_PAD * (h + 1)]
            vh = jnp.where(_head_lane_mask(h), v_ref[pl.ds(r0, width), :], jnp.zeros((), BF16))
            s = _dot_nt(q_ref[:, HEAD_PAD * h:HEAD_PAD * (h + 1)], kh)
            if masked:
                row = lax.broadcasted_iota(jnp.int32, (tq, width), 0)
                colm = lax.broadcasted_iota(jnp.int32, (tq, width), 1)
                s = jnp.where(colm <= row + (width - tq), s, -1e30)
            m2 = jnp.maximum(m, jnp.max(s, axis=-1, keepdims=True))
            alpha = jnp.exp(m - m2)
            pr = jnp.exp(s - m2)
            return m2, alpha * l + jnp.sum(pr, axis=-1, keepdims=True), alpha * acc + _dot(pr.astype(BF16), vh)

        def step(tile, n_tiles, carry, masked):
            return tuple(head_step(h, tile, n_tiles, carry[h], masked) for h in range(2))

        init = (jnp.full((tq, 1), -1e30, F32), jnp.zeros((tq, 1), F32), jnp.zeros((tq, 2 * V_DIM), F32))
        group = min(KEY_GROUP, nq)
        carry = lax.fori_loop(0, i // group, lambda t, cr: step(group * t, group, cr, False), (init, init))
        carry = lax.switch(i % group, [functools.partial(lambda cr, r: step(i - r, r + 1, cr, True), r=r) for r in range(group)], carry)
        out = jnp.zeros((tq, 2 * V_DIM), F32)
        for h, (m, l, acc) in enumerate(carry):
            out = out + acc / l
            lse_ref[h] = jnp.broadcast_to(m + jnp.log(l), (tq, LANES))
        o_ref[...] = out.astype(BF16)

    srcs, dsts = ([], []) if gather is None else (list(gather[0]), list(gather[1]))
    outs = pl.pallas_call(
        body, name="attention_fwd" if gather is None else "attention_fwd_gather", grid=(HEADS // 2, nq),
        in_specs=[pl.BlockSpec((tq, 2 * HEAD_PAD), lambda p, i: (i, p)), pl.BlockSpec((rows, 2 * HEAD_PAD), lambda p, i: (0, p)),
                  pl.BlockSpec((rows, 2 * V_DIM), lambda p, i: (0, p))] + [ANY] * (2 * n),
        out_specs=[pl.BlockSpec((tq, 2 * V_DIM), lambda p, i: (i, p)), pl.BlockSpec((2, tq, LANES), lambda p, i: (p, i, 0))] + [ANY] * n,
        out_shape=[jax.ShapeDtypeStruct((rows, HEADS * V_DIM), BF16), jax.ShapeDtypeStruct((HEADS, rows, LANES), F32)] + [
            jax.ShapeDtypeStruct(d.shape, d.dtype) for d in dsts],
        input_output_aliases={3 + n + a: 2 + a for a in range(n)}, scratch_shapes=GATHER_SEMS(n) if n else [],
        compiler_params=_params(("arbitrary", "arbitrary") if n else ("parallel", "parallel")),
    )(q, k, v, *srcs, *dsts)
    return outs[0], outs[1], list(outs[2:])


OUT_PROJECTIONS = ((M_POOL, 256), (M_MLA, 512), (M_CONF, 256), (M_SC, 256))


def _chunks(x, n=N_CHIPS, width=256):
    return [x[:, width * k:width * (k + 1)] for k in range(n)]


def merge_fwd(ua, o_att, uc, ud, z_br, z_gl, bias, misc, gpost, hres, li):
    rows = hres.shape[0]
    t = ROW_TILE
    cur, _, _, _, layer = _tile_specs(t, rows // HALO, li)
    d = D_MODEL

    def body(ua_ref, ob_ref, uc_ref, ud_ref, mg_ref, gl_ref, b_ref, wout_ref, wo_ref, gp_ref, h_ref, ub_ref, mb_ref, o_ref, hn_ref):
        ub = (ob_ref[...].astype(F32) * _silu(mg_ref[...].astype(F32))).astype(BF16)
        ub_ref[...] = ub
        m = jnp.zeros((t, d), F32)
        for idx, (u, (row0, n)) in enumerate(zip((ua_ref[...], ub, uc_ref[...], ud_ref[...]), OUT_PROJECTIONS)):
            gate = _sigmoid(gl_ref[:, d * idx:d * (idx + 1)].astype(F32) + b_ref[:, d * idx:d * (idx + 1)])
            m = m + gate * _chip_columns(u, wout_ref, row0, n)
        mb = m.astype(BF16)
        mb_ref[...] = mb
        o = jnp.concatenate([sum(_dot(mk, wo_ref[k, 256 * j:256 * (j + 1), :]) for k, mk in enumerate(_chunks(mb)))
                             for j in range(N_CHIPS)], axis=1)
        o_ref[...] = o
        hn_ref[...] = h_ref[...] + _rms(o, gp_ref[...])

    return pl.pallas_call(
        body, name="merge_fwd", grid=(rows // t,),
        in_specs=[cur(256), cur(512), cur(256), cur(256), cur(512, MG // 512), cur(ZG), layer((1, ZG)), _misc_spec(misc, 0, 1280),
                  _misc_spec(misc, M_WO, D_MODEL), layer((1, d)), cur(d)],
        out_specs=[cur(512), cur(d), cur(d), cur(d)],
        out_shape=[jax.ShapeDtypeStruct((rows, 512), BF16), jax.ShapeDtypeStruct((rows, d), BF16), jax.ShapeDtypeStruct((rows, d), F32),
                   jax.ShapeDtypeStruct((rows, d), F32)],
        compiler_params=_params(("parallel",)),
    )(ua, o_att, uc, ud, z_br, z_gl, bias, misc[0], misc[0], gpost, hres)


def loss_head(hres, target, n_tokens):
    rows, d = hres.shape
    t = ROW_TILE
    cur, _, _, full, _ = _tile_specs(t, rows // HALO)
    n_steps = rows // t

    def body(h_ref, t_ref, dh_ref, tot_ref, acc_ref):
        i = pl.program_id(0)

        @pl.when(i == 0)
        def _():
            acc_ref[...] = jnp.zeros_like(acc_ref)

        r = i * t + lax.broadcasted_iota(jnp.int32, (t, 1), 0)
        diff = jnp.where((r >= N_META) & (r < N_META + n_tokens), h_ref[...] - t_ref[...], 0.0)
        dh_ref[...] = diff * (1.0 / d)
        acc_ref[...] += jnp.sum(diff * diff, axis=0, keepdims=True)

        @pl.when(i == n_steps - 1)
        def _():
            tot_ref[...] = jnp.broadcast_to(jnp.sum(acc_ref[...], axis=1, keepdims=True), (1, LANES))

    return pl.pallas_call(
        body, name="loss_head", grid=(n_steps,), in_specs=[cur(d), cur(d)], out_specs=[cur(d), full((1, LANES))],
        out_shape=[jax.ShapeDtypeStruct((rows, d), F32), jax.ShapeDtypeStruct((1, LANES), F32)],
        scratch_shapes=[pltpu.VMEM((1, d), F32)], compiler_params=_params(("arbitrary",)),
    )(hres, target)


def _accumulate(i, ref, value):
    @pl.when(i == 0)
    def _():
        ref[...] = value

    @pl.when(i > 0)
    def _():
        ref[...] += value


def postnorm_bwd(dh, o, mb, misc, gpost, li, swap=None):
    rows, d = dh.shape
    t = ROW_TILE
    cur, _, _, full, layer = _tile_specs(t, rows // HALO, li)
    n = 0 if swap is None else len(swap)
    steps = rows // t

    def body(*refs):
        if n:
            start, finish = _swap_ops(refs[5:5 + n], refs[8 + n:8 + 2 * n], refs[8 + 2 * n:])
            pl.when(pl.program_id(0) == 0)(start)
        compute(*refs[:5], *refs[5 + n:8 + n])
        if n:
            pl.when(pl.program_id(0) == steps - 1)(finish)

    def compute(dh_ref, o_ref, mb_ref, wo_ref, gp_ref, dm_ref, dwo_ref, dgp_ref):
        i = pl.program_id(0)
        _, vjp = jax.vjp(_rms, o_ref[...], gp_ref[...])
        do, dg = vjp(dh_ref[...])
        dob = do.astype(BF16)
        dm_ref[...] = jnp.concatenate([sum(_dot_nt(dj, wo_ref[k, 256 * j:256 * (j + 1), :]) for j, dj in enumerate(_chunks(dob)))
                                       for k in range(N_CHIPS)], axis=1)
        dwo = _dot_tn(mb_ref[...], dob)
        for k in range(N_CHIPS):
            _accumulate(i, dwo_ref.at[k], jnp.concatenate(_chunks(dwo[256 * k:256 * (k + 1), :]), axis=0))
        _accumulate(i, dgp_ref, dg)

    sent = [] if swap is None else list(swap)
    outs = pl.pallas_call(
        body, name="postnorm_bwd" if swap is None else "postnorm_bwd_swap", grid=(steps,),
        in_specs=[cur(d), cur(d), cur(d), _misc_spec(misc, M_WO, d), layer((1, d))] + [ANY] * n,
        out_specs=[cur(d), full((N_CHIPS, d, 256)), full((1, d))] + [ANY] * n,
        out_shape=[jax.ShapeDtypeStruct((rows, d), F32), jax.ShapeDtypeStruct((N_CHIPS, d, 256), F32), jax.ShapeDtypeStruct((1, d), F32)] + [
            jax.ShapeDtypeStruct((p.shape[0], p.shape[1] // 2, p.shape[2]), p.dtype) for p in sent],
        scratch_shapes=[pltpu.SemaphoreType.DMA((n,)), pltpu.SemaphoreType.DMA((n,))] if n else [],
        compiler_params=_params(("arbitrary",)),
    )(dh, o, mb, misc[0], gpost, *sent)
    return outs[0], outs[1], outs[2], list(outs[3:])


def merge_bwd(dm, ua, ub, uc, ud, z_gl, bias, misc, o_att, z_br, dz_buf, li):
    rows, d = dm.shape
    t = ROW_TILE
    cur, _, _, full, layer = _tile_specs(t, rows // HALO, li)

    def body(dm_ref, ua_ref, ub_ref, uc_ref, ud_ref, gl_ref, b_ref, w_ref, o_ref, mg_ref, _,
             dua_ref, do_ref, duc_ref, dud_ref, dgl_ref, dw_ref, db_ref, dmg_ref, delta_ref):
        i = pl.program_id(0)
        dm = dm_ref[...]
        groups = ((ua_ref, dua_ref), (ub_ref, None), (uc_ref, duc_ref), (ud_ref, dud_ref))
        for idx, ((u_ref, du_ref), (row0, n)) in enumerate(zip(groups, OUT_PROJECTIONS)):
            cols = slice(d * idx, d * (idx + 1))
            u = u_ref[...]
            gate = _sigmoid(gl_ref[:, cols].astype(F32) + b_ref[:, cols])
            dgl = dm * _chip_columns(u, w_ref, row0, n) * gate * (1.0 - gate)
            dgl_ref[:, cols] = dgl.astype(BF16)
            _accumulate(i, db_ref.at[:, cols], jnp.sum(dgl, axis=0, keepdims=True))
            dyb = (dm * gate).astype(BF16)
            du = sum(_dot_nt(dyk, w_ref[k, row0:row0 + n, :]) for k, dyk in enumerate(_chunks(dyb)))
            for k, dwk in enumerate(_chunks(_dot_tn(u, dyb))):
                _accumulate(i, dw_ref.at[k, row0:row0 + n, :], dwk)
            if du_ref is not None:
                du_ref[...] = du
                continue
            o, mg = o_ref[...].astype(F32), mg_ref[...].astype(F32)
            do = du * _silu(mg)
            do_ref[...] = do.astype(BF16)
            dmg_ref[...] = (du * o * _silu_grad(mg)).astype(BF16)
            prod = do * o
            lane = lax.broadcasted_iota(jnp.int32, (1, HEADS * V_DIM), 1)
            for h in range(HEADS):
                part = jnp.where((lane >= V_DIM * h) & (lane < V_DIM * (h + 1)), prod, 0.0)
                delta_ref[h] = jnp.broadcast_to(jnp.sum(part, axis=-1, keepdims=True), (t, LANES))

    return pl.pallas_call(
        body, name="merge_bwd", grid=(rows // t,),
        in_specs=[cur(d), cur(256), cur(512), cur(256), cur(256), cur(ZG), layer((1, ZG)), _misc_spec(misc, 0, 1280), cur(512),
                  cur(512, MG // 512), ANY],
        out_specs=[cur(256), cur(512), cur(256), cur(256), cur(ZG), full((N_CHIPS, 1280, 256)), full((1, ZG)), cur(512, MG // 512),
                   pl.BlockSpec((HEADS, t, LANES), lambda i: (0, i, 0))],
        out_shape=[jax.ShapeDtypeStruct((rows, 256), F32), jax.ShapeDtypeStruct((rows, 512), BF16), jax.ShapeDtypeStruct((rows, 256), F32),
                   jax.ShapeDtypeStruct((rows, 256), F32), jax.ShapeDtypeStruct((rows, ZG), BF16),
                   jax.ShapeDtypeStruct((N_CHIPS, 1280, 256), F32), jax.ShapeDtypeStruct((1, ZG), F32),
                   jax.ShapeDtypeStruct((rows, ZB), BF16), jax.ShapeDtypeStruct((HEADS, rows, LANES), F32)],
        input_output_aliases={10: 7}, compiler_params=_params(("arbitrary",)),
    )(dm, ua, ub, uc, ud, z_gl, bias, misc[0], o_att, z_br, dz_buf)


def pool_shortconv_bwd(z_br, dua, dud, pwbd, pscale, sc_w, dz_buf, li):
    rows = z_br.shape[0]
    t = ROW_TILE
    n_steps = rows // t
    cur, prev, nxt, full, layer = _tile_specs(t, rows // HALO, li)

    def body(zc_ref, zp_ref, zn_ref, dac_ref, dan_ref, ddc_ref, ddn_ref, pw_ref, ps_ref, sw_ref, _, dz_ref, dpw_ref, dps_ref, dw_ref):
        i = pl.program_id(0)
        last = i == n_steps - 1
        zp = jnp.where(i == 0, jnp.zeros(zp_ref.shape, zp_ref.dtype), zp_ref[...])
        zn = jnp.where(last, jnp.zeros(zn_ref.shape, zn_ref.dtype), zn_ref[...])

        def ext(lo):
            return jnp.concatenate([zp[:, lo:lo + 256], zc_ref[:, lo:lo + 256], zn[:, lo:lo + 256]], axis=0).astype(F32)

        def ext_grad(c_ref, n_ref):
            return jnp.concatenate([jnp.zeros((HALO, 256), F32), c_ref[...], jnp.where(last, jnp.zeros(n_ref.shape, F32), n_ref[...])], axis=0)

        mid = slice(HALO, HALO + t)

        bg, c2, xv, sg = ext(BG), ext(C2), ext(XV), ext(SG)
        du = ext_grad(ddc_ref, ddn_ref)
        e = c2 * xv
        shifted = [_sh(e, SC_K - 1 - k) for k in range(SC_K)]
        f = sum(sw_ref[k:k + 1, :] * shifted[k] for k in range(SC_K))
        gate = _silu(sg)
        df = du * gate * bg
        de = sum(sw_ref[k:k + 1, :] * _ash(df, SC_K - 1 - k) for k in range(SC_K))
        d_sc = [(du * gate * f)[mid], (de * xv)[mid], (de * c2)[mid], (du * bg * f * _silu_grad(sg))[mid]]
        dw = jnp.concatenate([jnp.sum((df * shifted[k])[mid], axis=0, keepdims=True) for k in range(SC_K)] + [
            jnp.zeros((8 - SC_K, 256), F32)], axis=0)
        _accumulate(i, dw_ref, dw)

        v, pg = ext(PV), ext(PG)
        cnt = _pool_counts(i * t - HALO, t + 2 * HALO)
        p = (_pool_window_sums(v, _sh) / cnt - v)[mid]
        dya = ext_grad(dac_ref, dan_ref) * _silu(pg)
        dypb = (dya * ps_ref[...]).astype(BF16)
        dp = _dot_nt(dypb, pw_ref[...])
        dv = (_pool_window_sums(dp / cnt, _ash) - dp)[mid]
        pb = p.astype(BF16)
        pw = _dot(pb, pw_ref[...])
        dpg = dac_ref[...] * pw * ps_ref[...] * _silu_grad(pg[mid])
        _accumulate(i, dpw_ref, _dot_tn(pb, dypb[mid]))
        _accumulate(i, dps_ref, jnp.sum(dya[mid] * pw, axis=0, keepdims=True))

        dz_ref[...] = jnp.concatenate(d_sc + [dv, dpg], axis=1).astype(BF16)

    return pl.pallas_call(
        body, name="pool_shortconv_bwd", grid=(n_steps,),
        in_specs=[cur(ZB), prev(ZB), nxt(ZB), cur(256), nxt(256), cur(256), nxt(256), layer((256, 256)), layer((1, 256)), layer((8, 256)),
                  ANY],
        out_specs=[cur(1536, BG // 1536), full((256, 256)), full((1, 256)), full((8, 256))],
        out_shape=[jax.ShapeDtypeStruct((rows, ZB), BF16), jax.ShapeDtypeStruct((256, 256), F32), jax.ShapeDtypeStruct((1, 256), F32),
                   jax.ShapeDtypeStruct((8, 256), F32)],
        input_output_aliases={10: 0}, compiler_params=_params(("arbitrary",)),
    )(z_br, z_br, z_br, dua, dua, dud, dud, pwbd, pscale, sc_w, dz_buf)


def conformer_bwd_tail(z_br, duc, conf_w, conf_vec, dz_buf, li):
    rows = z_br.shape[0]
    t = ROW_TILE
    cur, prev, _, full, layer = _tile_specs(t, rows // HALO, li)

    def body(zc_ref, zp_ref, du_ref, cw_ref, cv_ref, _, dc_ref, dcg_ref, dv_ref):
        i = pl.program_id(0)
        zp = jnp.where(i == 0, jnp.zeros(zp_ref.shape, zp_ref.dtype), zp_ref[...])

        def ext(lo):
            return jnp.concatenate([zp[:, lo:lo + 256], zc_ref[:, lo:lo + 256]], axis=0).astype(F32)

        g1 = ext(CA) * _sigmoid(ext(CGT))
        c = _conf_conv(g1, cw_ref)[HALO:] + cv_ref[0:1, :]
        _, vjp = jax.vjp(_conf_tail, c, zc_ref[:, CG:CG + 256].astype(F32), cv_ref[1:2, :], cv_ref[2:3, :])
        dc, dcg, dlg, dlb = vjp(du_ref[...])
        dc_ref[...] = dc
        dcg_ref[...] = dcg.astype(BF16)
        dvec = jnp.concatenate([dlg, dlb, jnp.sum(dc, axis=0, keepdims=True), jnp.zeros((5, 256), F32)], axis=0)
        _accumulate(i, dv_ref, dvec)

    return pl.pallas_call(
        body, name="conformer_bwd_tail", grid=(rows // t,), in_specs=[cur(ZB), prev(ZB), cur(256), layer((32, 256)), layer((8, 256)), ANY],
        out_specs=[cur(256), cur(256, CG // 256), full((8, 256))],
        out_shape=[jax.ShapeDtypeStruct((rows, 256), F32), jax.ShapeDtypeStruct((rows, ZB), BF16), jax.ShapeDtypeStruct((8, 256), F32)],
        input_output_aliases={5: 1}, compiler_params=_params(("arbitrary",)),
    )(z_br, z_br, duc, conf_w, conf_vec, dz_buf)


def conformer_bwd_conv(z_br, dc, conf_w, dz_buf, li):
    rows = z_br.shape[0]
    t = ROW_TILE
    n_steps = rows // t
    cur, prev, nxt, full, layer = _tile_specs(t, rows // HALO, li)

    def body(zc_ref, zp_ref, dc_ref, dn_ref, cw_ref, _, dz_ref, dw_ref):
        i = pl.program_id(0)
        zp = jnp.where(i == 0, jnp.zeros(zp_ref.shape, zp_ref.dtype), zp_ref[...])
        dcn = jnp.where(i == n_steps - 1, jnp.zeros(dn_ref.shape, dn_ref.dtype), dn_ref[...])

        def ext(lo):
            return jnp.concatenate([zp[:, lo:lo + 256], zc_ref[:, lo:lo + 256]], axis=0).astype(F32)

        a, gt = ext(CA), ext(CGT)
        sg = _sigmoid(gt)
        g1 = a * sg
        dc = dc_ref[...]
        dce = jnp.concatenate([dc, dcn], axis=0)
        dg1 = jnp.zeros_like(dce)
        dws = []
        for k in range(CONF_K):
            dg1 = dg1 + cw_ref[k:k + 1, :] * _ash(dce, CONF_K - 1 - k)
            dws.append(jnp.sum(dc * _sh(g1, CONF_K - 1 - k)[HALO:], axis=0, keepdims=True))
        dg1 = dg1[:t]
        ac, sc = a[HALO:], sg[HALO:]
        dz_ref[...] = jnp.concatenate([dg1 * sc, dg1 * ac * sc * (1.0 - sc)], axis=1).astype(BF16)
        _accumulate(i, dw_ref, jnp.concatenate(dws + [jnp.zeros((32 - CONF_K, 256), F32)], axis=0))

    return pl.pallas_call(
        body, name="conformer_bwd_conv", grid=(n_steps,), in_specs=[cur(ZB), prev(ZB), cur(256), nxt(256), layer((32, 256)), ANY],
        out_specs=[cur(512, CA // 512), full((32, 256))],
        out_shape=[jax.ShapeDtypeStruct((rows, ZB), BF16), jax.ShapeDtypeStruct((32, 256), F32)],
        input_output_aliases={5: 0}, compiler_params=_params(("arbitrary",)),
    )(z_br, z_br, dc, dc, conf_w, dz_buf)


def attention_bwd(q, k, v, do, lse, delta, exchange=None):
    rows = q.shape[0]
    tq = ROW_TILE
    nq = rows // tq
    n = 0 if exchange is None else len(exchange[0])

    def body(*refs):
        if n:
            start, finish = _exchange_ops(refs[6:6 + n], refs[9 + 2 * n:9 + 3 * n], refs[9 + 3 * n:], exchange[2])
            pl.when((pl.program_id(0) == 0) & (pl.program_id(1) == 0))(start)
        compute(*refs[:6], *refs[6 + 2 * n:9 + 2 * n])
        if n:
            pl.when((pl.program_id(0) == HEADS // 2 - 1) & (pl.program_id(1) == nq - 1))(finish)

    def compute(q_ref, k_ref, v_ref, do_ref, lse_ref, dl_ref, dq_ref, dk_ref, dv_ref):
        j = pl.program_id(1)

        @pl.when(j == 0)
        def _():
            dq_ref[...] = jnp.zeros_like(dq_ref)

        def head_step(h, tile, n_tiles, dk, dv, diagonal):
            lanes = slice(HEAD_PAD * h, HEAD_PAD * (h + 1))
            hm = _head_lane_mask(h)
            kh = k_ref[:, lanes]
            vh = jnp.where(hm, v_ref[...], jnp.zeros((), BF16))
            r0, width = pl.multiple_of(tile * tq, tq), n_tiles * tq
            qi = q_ref[pl.ds(r0, width), lanes]
            doi = jnp.where(hm, do_ref[pl.ds(r0, width), :], jnp.zeros((), BF16))
            s = _dot_nt(qi, kh)
            if diagonal:
                s = jnp.where(lax.broadcasted_iota(jnp.int32, (tq, tq), 1) <= lax.broadcasted_iota(jnp.int32, (tq, tq), 0), s, -1e30)
            pr = jnp.exp(s - lse_ref[h, pl.ds(r0, width), :][:, 0:1])
            dv = dv + _dot_tn(pr.astype(BF16), doi)
            dp = _dot_nt(doi, vh)
            ds = (pr * (dp - dl_ref[h, pl.ds(r0, width), :][:, 0:1])).astype(BF16)
            dq_ref[pl.ds(r0, width), lanes] += _dot(ds, kh)
            return dk + _dot_tn(ds, qi), dv

        def step(tile, n_tiles, carry, diagonal):
            dk0, dk1, dv = carry
            dk0, dv = head_step(0, tile, n_tiles, dk0, dv, diagonal)
            dk1, dv = head_step(1, tile, n_tiles, dk1, dv, diagonal)
            return dk0, dk1, dv

        zero = jnp.zeros((tq, HEAD_PAD), F32)
        carry = step(j, 1, (zero, zero, jnp.zeros((tq, 2 * V_DIM), F32)), True)
        odd = (nq - 1 - j) % 2
        carry = lax.cond(odd == 1, lambda cr: step(j + 1, 1, cr, False), lambda cr: cr, carry)
        dk0, dk1, dv = lax.fori_loop(0, (nq - 1 - j) // 2, lambda t, cr: step(j + 1 + odd + 2 * t, 2, cr, False), carry)
        dk_ref[:, 0:HEAD_PAD] = dk0
        dk_ref[:, HEAD_PAD:2 * HEAD_PAD] = dk1
        dv_ref[...] = dv

    srcs, dsts = ([], []) if exchange is None else (list(exchange[0]), list(exchange[1]))
    outs = pl.pallas_call(
        body, name="attention_bwd" if exchange is None else "attention_bwd_exchange", grid=(HEADS // 2, nq),
        in_specs=[pl.BlockSpec((rows, 2 * HEAD_PAD), lambda p, j: (0, p)), pl.BlockSpec((tq, 2 * HEAD_PAD), lambda p, j: (j, p)),
                  pl.BlockSpec((tq, 2 * V_DIM), lambda p, j: (j, p)), pl.BlockSpec((rows, 2 * V_DIM), lambda p, j: (0, p)),
                  pl.BlockSpec((2, rows, LANES), lambda p, j: (p, 0, 0)), pl.BlockSpec((2, rows, LANES), lambda p, j: (p, 0, 0))] + [
                      ANY] * (2 * n),
        out_specs=[pl.BlockSpec((rows, 2 * HEAD_PAD), lambda p, j: (0, p)), pl.BlockSpec((tq, 2 * HEAD_PAD), lambda p, j: (j, p)),
                   pl.BlockSpec((tq, 2 * V_DIM), lambda p, j: (j, p))] + [ANY] * n,
        out_shape=[jax.ShapeDtypeStruct((rows, HEADS * HEAD_PAD), F32), jax.ShapeDtypeStruct((rows, HEADS * HEAD_PAD), F32),
                   jax.ShapeDtypeStruct((rows, HEADS * V_DIM), F32)] + [jax.ShapeDtypeStruct(d.shape, d.dtype) for d in dsts],
        input_output_aliases={6 + n + a: 3 + a for a in range(n)}, scratch_shapes=EXCHANGE_SEMS(n) if n else [],
        compiler_params=_params(("arbitrary", "arbitrary") if n else ("parallel", "arbitrary")),
    )(q, k, v, do, lse, delta, *srcs, *dsts)
    return outs[0], outs[1], outs[2], list(outs[3:])


def mla_prep_bwd(dq, dk, dv, z_br, rope, gq, gkv, misc, dz_buf, li):
    rows = dq.shape[0]
    t = ROW_TILE
    cur, _, _, full, layer = _tile_specs(t, rows // HALO, li)
    w8 = HEADS * HEAD_PAD
    uq, keys, values = slice(0, 256), slice(M_UKVK - M_UQ, M_UKVV - M_UQ), slice(M_UKVV - M_UQ, M_WO - M_UQ)

    def body(dq_ref, dk_ref, dv_ref, z_ref, rope_ref, gq_ref, gkv_ref, up_ref, _, dz_ref, dup_ref, dgq_ref, dgkv_ref):
        i = pl.program_id(0)
        cth, s1, s2 = rope_ref[:, 0:128], rope_ref[:, 128:256], rope_ref[:, 256:384]
        dqb = _rope_transposed(dq_ref[...] * Q_SCALE, _lanes8(cth), _lanes8(s1), _lanes8(s2), w8).astype(BF16)
        dq_chunks = _chunks(dqb)
        cq = z_ref[:, 0:256].astype(F32)
        qn, vjp_q = jax.vjp(_rms, cq, gq_ref[...])
        dcq, dgq = vjp_q(sum(_dot_nt(dqk, up_ref[k, uq, :]) for k, dqk in enumerate(dq_chunks)))
        _accumulate(i, dgq_ref, dgq)

        dk = dk_ref[...]
        dkr = sum(dk[:, HEAD_PAD * h:HEAD_PAD * (h + 1)] for h in range(HEADS))
        dkr = _rope_transposed(dkr, cth, s1, s2, HEAD_PAD)
        lane = lax.broadcasted_iota(jnp.int32, (1, HEAD_PAD), 1)
        dkr = jnp.where((lane >= QK_NOPE) & (lane < QK_NOPE + QK_ROPE), dkr, 0.0)
        dkb, dvb = dk.astype(BF16), dv_ref[...].astype(BF16)
        dk_chunks, dv_chunks = _chunks(dkb), _chunks(dvb, width=2 * V_DIM)
        ckv = z_ref[:, 256:384].astype(F32)
        kvn, vjp_kv = jax.vjp(_rms, ckv, gkv_ref[...])
        dckv, dgkv = vjp_kv(sum(_dot_nt(dk_chunks[k], up_ref[k, keys, :]) + _dot_nt(dv_chunks[k], up_ref[k, values, 0:2 * V_DIM])
                                for k in range(N_CHIPS)))
        _accumulate(i, dgkv_ref, dgkv)
        dz_ref[...] = jnp.concatenate([dcq, dckv, dkr], axis=1).astype(BF16)
        qnb, kvnb = qn.astype(BF16), kvn.astype(BF16)
        d_uq, d_keys, d_values = _chunks(_dot_tn(qnb, dqb)), _chunks(_dot_tn(kvnb, dkb)), _chunks(_dot_tn(kvnb, dvb), width=2 * V_DIM)
        for k in range(N_CHIPS):
            padded = jnp.concatenate([d_values[k], jnp.zeros((128, 256 - 2 * V_DIM), F32)], axis=1)
            _accumulate(i, dup_ref.at[k], jnp.concatenate([d_uq[k], d_keys[k], padded], axis=0))

    return pl.pallas_call(
        body, name="mla_prep_bwd", grid=(rows // t,),
        in_specs=[cur(w8), cur(w8), cur(512), cur(512, CQ // 512), cur(384), layer((1, 256)), layer((1, 128)),
                  _misc_spec(misc, M_UQ, M_WO - M_UQ), ANY],
        out_specs=[cur(512, CQ // 512), full((N_CHIPS, M_WO - M_UQ, 256)), full((1, 256)), full((1, 128))],
        out_shape=[jax.ShapeDtypeStruct((rows, ZB), BF16), jax.ShapeDtypeStruct((N_CHIPS, M_WO - M_UQ, 256), F32),
                   jax.ShapeDtypeStruct((1, 256), F32), jax.ShapeDtypeStruct((1, 128), F32)],
        input_output_aliases={8: 0}, compiler_params=_params(("arbitrary",)),
    )(dq, dk, dv, z_br, rope, gq, gkv, misc[0], dz_buf)


def prenorm_bwd(dz_br, w_br, dh_gl, hres, gpre, dh_next, li):
    rows, d = hres.shape
    t = ROW_TILE
    cur, _, _, full, layer = _tile_specs(t, rows // HALO, li)

    def body(dz_ref, w_ref, dp_ref, x_ref, g_ref, dn_ref, dx_ref, dg_ref):
        i = pl.program_id(0)
        dh = _dot_nt(dz_ref[...], w_ref[...]) + dp_ref[...]
        _, vjp = jax.vjp(_rms, x_ref[...], g_ref[...])
        dx, dg = vjp(dh)
        dx_ref[...] = dx + dn_ref[...]
        _accumulate(i, dg_ref, dg)

    return pl.pallas_call(
        body, name="prenorm_bwd", grid=(rows // t,), in_specs=[cur(ZB), layer((d, ZB), 0), cur(d), cur(d), layer((1, d)), cur(d)],
        out_specs=[cur(d), full((1, d))], out_shape=[jax.ShapeDtypeStruct((rows, d), F32), jax.ShapeDtypeStruct((1, d), F32)],
        compiler_params=_params(("arbitrary",)),
    )(dz_br, w_br, dh_gl, hres, gpre, dh_next)


def _mesh_position():
    return lax.axis_index("x"), lax.axis_index("y"), lax.axis_index("c")


def chip_exchange(src, gather, name):
    block = src.shape if gather else src.shape[1:]

    def body(src_ref, dst_ref, send_sems, recv_sems, local_sem):
        x, y, c = _mesh_position()
        me = 2 * x + y
        peers = ((1 - x, y), (x, 1 - y), (1 - x, 1 - y))

        def part(k):
            return src_ref if gather else src_ref.at[k]

        def copy(j, slot):
            px, py = peers[j]
            return pltpu.make_async_remote_copy(src_ref=part(2 * px + py), dst_ref=dst_ref.at[slot], send_sem=send_sems.at[j],
                                                recv_sem=recv_sems.at[j], device_id=(px, py, c), device_id_type=MESH)

        local = pltpu.make_async_copy(part(me), dst_ref.at[me], local_sem)
        local.start()
        sends = [copy(j, me) for j in range(3)]
        for cp in sends:
            cp.start()
        for j, (px, py) in enumerate(peers):
            copy(j, 2 * px + py).wait_recv()
        for cp in sends:
            cp.wait_send()
        local.wait()

    return pl.pallas_call(
        body, name=name, in_specs=[pl.BlockSpec(memory_space=pl.ANY)], out_specs=pl.BlockSpec(memory_space=pl.ANY),
        out_shape=jax.ShapeDtypeStruct((N_CHIPS,) + tuple(block), src.dtype),
        scratch_shapes=[pltpu.SemaphoreType.DMA((3,)), pltpu.SemaphoreType.DMA((3,)), pltpu.SemaphoreType.DMA(())],
    )(src)


def sibling_swap(src, name):
    def body(src_ref, dst_ref, send_sem, recv_sem):
        x, y, c = _mesh_position()
        cp = pltpu.make_async_remote_copy(src_ref=src_ref, dst_ref=dst_ref, send_sem=send_sem, recv_sem=recv_sem,
                                          device_id=(x, y, 1 - c), device_id_type=MESH)
        cp.start()
        cp.wait()

    return pl.pallas_call(
        body, name=name, in_specs=[pl.BlockSpec(memory_space=pl.ANY)], out_specs=pl.BlockSpec(memory_space=pl.ANY),
        out_shape=jax.ShapeDtypeStruct(src.shape, src.dtype),
        scratch_shapes=[pltpu.SemaphoreType.DMA(()), pltpu.SemaphoreType.DMA(())],
    )(src)


def _comm_call(body, name, n_in, out_shapes, n_sems):
    return pl.pallas_call(
        body, name=name, in_specs=[ANY] * n_in, out_specs=[ANY] * len(out_shapes), out_shape=out_shapes,
        scratch_shapes=[pltpu.SemaphoreType.DMA((n,)) for n in n_sems])


def _row_halves(c, rows):
    half = rows // 2
    return pl.ds(pl.multiple_of(c * half, 16), half), pl.ds(pl.multiple_of((1 - c) * half, 16), half)


def _peers():
    x, y, c = _mesh_position()
    return x, y, c, 2 * x + y, ((1 - x, y), (x, 1 - y), (1 - x, 1 - y))


def _gather_ops(src, dst, sems, layer, own_copy):
    ici_send, ici_recv, d2d_send, d2d_recv, own_sems = sems
    n = len(src)

    def fetch(a, j, slot):
        x, y, c, _, peers = _peers()
        px, py = peers[j]
        mine, _ = _row_halves(c, src[a].shape[1])
        return pltpu.make_async_remote_copy(src_ref=src[a].at[layer, mine], dst_ref=dst[a].at[layer, slot, mine], send_sem=ici_send.at[3 * a + j],
                                            recv_sem=ici_recv.at[3 * a + j], device_id=(px, py, c), device_id_type=MESH)

    def forward(a, j, sibling_half):
        x, y, c, _, peers = _peers()
        px, py = peers[j]
        part = dst[a].at[layer, 2 * px + py, _row_halves(c, src[a].shape[1])[1 if sibling_half else 0]]
        return pltpu.make_async_remote_copy(src_ref=part, dst_ref=part, send_sem=d2d_send.at[3 * a + j], recv_sem=d2d_recv.at[3 * a + j],
                                            device_id=(x, y, 1 - c), device_id_type=MESH)

    def own(a):
        return pltpu.make_async_copy(src[a].at[layer], dst[a].at[layer, _peers()[3]], own_sems.at[a])

    def start():
        me = _peers()[3]
        for a in range(n):
            if own_copy:
                own(a).start()
            for j in range(3):
                fetch(a, j, me).start()

    def finish():
        peers = _peers()[4]
        for j, (px, py) in enumerate(peers):
            for a in range(n):
                fetch(a, j, 2 * px + py).wait_recv()
                forward(a, j, False).start()
        for j in range(3):
            for a in range(n):
                forward(a, j, True).wait_recv()
        for j in range(3):
            for a in range(n):
                fetch(a, j, 0).wait_send()
                forward(a, j, False).wait_send()
        if own_copy:
            for a in range(n):
                own(a).wait()

    return start, finish


def _exchange_ops(src, dst, sems, layer):
    send_sems, recv_sems, own_sems = sems
    n = len(src)

    def copy(a, j, slot):
        x, y, c, _, peers = _peers()
        px, py = peers[j]
        return pltpu.make_async_remote_copy(src_ref=src[a].at[2 * px + py], dst_ref=dst[a].at[layer, slot], send_sem=send_sems.at[3 * a + j],
                                            recv_sem=recv_sems.at[3 * a + j], device_id=(px, py, c), device_id_type=MESH)

    def own(a):
        me = _peers()[3]
        return pltpu.make_async_copy(src[a].at[me], dst[a].at[layer, me], own_sems.at[a])

    def start():
        me = _peers()[3]
        for a in range(n):
            own(a).start()
            for j in range(3):
                copy(a, j, me).start()

    def finish():
        peers = _peers()[4]
        for j, (px, py) in enumerate(peers):
            for a in range(n):
                copy(a, j, 2 * px + py).wait_recv()
        for j in range(3):
            for a in range(n):
                copy(a, j, 0).wait_send()
        for a in range(n):
            own(a).wait()

    return start, finish


GATHER_SEMS = lambda n: [pltpu.SemaphoreType.DMA((3 * n,))] * 4 + [pltpu.SemaphoreType.DMA((n,))]
EXCHANGE_SEMS = lambda n: [pltpu.SemaphoreType.DMA((3 * n,))] * 2 + [pltpu.SemaphoreType.DMA((n,))]


def gather_layer(srcs, dsts, layer, name):
    n = len(srcs)

    def body(*refs):
        start, finish = _gather_ops(refs[:n], refs[2 * n:3 * n], refs[3 * n:], layer, False)
        start()
        finish()

    return pl.pallas_call(
        body, name=name, in_specs=[ANY] * (2 * n), out_specs=[ANY] * n, out_shape=[jax.ShapeDtypeStruct(d.shape, d.dtype) for d in dsts],
        input_output_aliases={n + a: a for a in range(n)}, scratch_shapes=GATHER_SEMS(n),
    )(*srcs, *dsts)


def exchange_layer(ss, dsts, layer, name):
    n = len(ss)

    def body(*refs):
        start, finish = _exchange_ops(refs[:n], refs[2 * n:3 * n], refs[3 * n:], layer)
        start()
        finish()

    return pl.pallas_call(
        body, name=name, in_specs=[ANY] * (2 * n), out_specs=[ANY] * n, out_shape=[jax.ShapeDtypeStruct(d.shape, d.dtype) for d in dsts],
        input_output_aliases={n + a: a for a in range(n)}, scratch_shapes=EXCHANGE_SEMS(n),
    )(*ss, *dsts)


def _swap_ops(src, dst, sems):
    send_sems, recv_sems = sems

    def copy(a):
        x, y, c = _mesh_position()
        return pltpu.make_async_remote_copy(src_ref=src[a].at[:, _row_halves(c, src[a].shape[1])[1]], dst_ref=dst[a], send_sem=send_sems.at[a],
                                            recv_sem=recv_sems.at[a], device_id=(x, y, 1 - c), device_id_type=MESH)

    def start():
        for a in range(len(src)):
            copy(a).start()

    def finish():
        for a in range(len(src)):
            copy(a).wait()

    return start, finish


def swap_row_halves(ps, name):
    n = len(ps)

    def body(*refs):
        start, finish = _swap_ops(refs[:n], refs[n:2 * n], refs[2 * n:])
        start()
        finish()

    outs = [jax.ShapeDtypeStruct((p.shape[0], p.shape[1] // 2, p.shape[2]), p.dtype) for p in ps]
    return _comm_call(body, name, n, outs, (n, n))(*ps)


def add_row_half(p, r, c, name):
    n, half, cols = r.shape
    rb = _row_block(half, cols, 2)
    steps = half // rb

    def body(c_ref, p_ref, r_ref, o_ref):
        o_ref[...] = (p_ref[...].astype(F32) + r_ref[...].astype(F32)).astype(BF16)

    return pl.pallas_call(
        body, name=name, out_shape=jax.ShapeDtypeStruct(r.shape, BF16),
        grid_spec=pltpu.PrefetchScalarGridSpec(
            num_scalar_prefetch=1, grid=(n, steps),
            in_specs=[pl.BlockSpec((1, rb, cols), lambda k, i, c_ref: (k, c_ref[0] * steps + i, 0)),
                      pl.BlockSpec((1, rb, cols), lambda k, i, c_ref: (k, i, 0))],
            out_specs=pl.BlockSpec((1, rb, cols), lambda k, i, c_ref: (k, i, 0))),
        compiler_params=_params(("parallel", "parallel")),
    )(jnp.reshape(c, (1,)).astype(jnp.int32), p, r)


def sum_row_halves(l, c, name):
    layers, n, half, cols = l.shape
    rb = _row_block(half, cols, 4)
    steps = half // rb

    def body(c_ref, l_ref, o_ref):
        acc = l_ref[0, 0].astype(F32)
        for s in range(1, n):
            acc = acc + l_ref[0, s].astype(F32)
        o_ref[0] = acc

    return pl.pallas_call(
        body, name=name, out_shape=jax.ShapeDtypeStruct((layers, 2 * half, cols), F32),
        grid_spec=pltpu.PrefetchScalarGridSpec(
            num_scalar_prefetch=1, grid=(layers, steps), in_specs=[pl.BlockSpec((1, n, rb, cols), lambda a, i, c_ref: (a, 0, i, 0))],
            out_specs=pl.BlockSpec((1, rb, cols), lambda a, i, c_ref: (a, c_ref[0] * steps + i, 0))),
        compiler_params=_params(("parallel", "parallel")),
    )(jnp.reshape(c, (1,)).astype(jnp.int32), l)


def share_row_halves(gs, name):
    n = len(gs)

    def body(*refs):
        dst = refs[n:2 * n]
        send_sems, recv_sems = refs[2 * n:]
        x, y, c = _mesh_position()

        def copy(a, sibling_half):
            part = dst[a].at[:, _row_halves(c, dst[a].shape[1])[1 if sibling_half else 0]]
            return pltpu.make_async_remote_copy(src_ref=part, dst_ref=part, send_sem=send_sems.at[a], recv_sem=recv_sems.at[a],
                                                device_id=(x, y, 1 - c), device_id_type=MESH)

        for a in range(n):
            copy(a, False).start()
        for a in range(n):
            copy(a, True).wait_recv()
        for a in range(n):
            copy(a, False).wait_send()

    return pl.pallas_call(
        body, name=name, in_specs=[ANY] * n, out_specs=[ANY] * n, out_shape=[jax.ShapeDtypeStruct(g.shape, g.dtype) for g in gs],
        input_output_aliases={a: a for a in range(n)}, scratch_shapes=[pltpu.SemaphoreType.DMA((n,)), pltpu.SemaphoreType.DMA((n,))],
    )(*gs)


def _row_block(rows, cols, itemsize):
    best = 16
    for rb in range(16, rows + 1, 16):
        if rows % rb == 0 and rb * cols * itemsize <= 2 * 1024 * 1024:
            best = rb
    assert rows % best == 0, (rows, cols)
    return best


def _comm_block(rows):
    return 1024 if rows % 1024 == 0 else rows


def sum_slots(buf, name):
    n, r, c = buf.shape
    rb = _comm_block(r)

    def body(b_ref, o_ref):
        acc = b_ref[0].astype(F32)
        for s in range(1, n):
            acc = acc + b_ref[s].astype(F32)
        o_ref[...] = acc

    return pl.pallas_call(
        body, name=name, grid=(r // rb,), in_specs=[pl.BlockSpec((n, rb, c), lambda i: (0, i, 0))],
        out_specs=pl.BlockSpec((rb, c), lambda i: (i, 0)), out_shape=jax.ShapeDtypeStruct((r, c), F32),
        compiler_params=_params(("parallel",)),
    )(buf)


def add_pair(a, b, out_dtype, name):
    shape = a.shape
    a2, b2 = a.reshape(-1, shape[-1]), b.reshape(-1, shape[-1])
    r, c = a2.shape
    rb = _comm_block(r)

    def body(a_ref, b_ref, o_ref):
        o_ref[...] = (a_ref[...].astype(F32) + b_ref[...].astype(F32)).astype(out_dtype)

    out = pl.pallas_call(
        body, name=name, grid=(r // rb,), in_specs=[pl.BlockSpec((rb, c), lambda i: (i, 0))] * 2,
        out_specs=pl.BlockSpec((rb, c), lambda i: (i, 0)), out_shape=jax.ShapeDtypeStruct((r, c), out_dtype),
        compiler_params=_params(("parallel",)),
    )(a2, b2)
    return out.reshape(shape)


def adamw(w, g, m, v):
    shape = w.shape
    cols = shape[-1]
    rows = math.prod(shape[:-1])
    if rows * cols <= 256 * 1024:
        rb, cb = rows, cols
    else:
        rb = max(r for r in range(8, 2049, 8) if rows % r == 0)
        cb = cols if rb * cols * 4 <= 2 * 1024 * 1024 else 256
    assert rows % rb == 0 and cols % cb == 0, shape

    def body(w_ref, g_ref, m_ref, v_ref, d_ref, nm_ref, nv_ref):
        g_ = g_ref[...]
        nm = ADAM_B1 * m_ref[...] + (1.0 - ADAM_B1) * g_
        nv = ADAM_B2 * v_ref[...] + (1.0 - ADAM_B2) * (g_ * g_)
        m_hat = nm / (1.0 - ADAM_B1 ** ADAM_STEP)
        v_hat = nv / (1.0 - ADAM_B2 ** ADAM_STEP)
        d_ref[...] = -ADAM_LR * (m_hat / (jnp.sqrt(v_hat) + ADAM_EPS) + ADAM_WD * w_ref[...])
        nm_ref[...] = nm
        nv_ref[...] = nv

    spec = pl.BlockSpec((rb, cb), lambda i, j: (i, j))
    outs = pl.pallas_call(
        body, name="adamw", grid=(rows // rb, cols // cb), in_specs=[spec] * 4, out_specs=[spec] * 3,
        out_shape=[jax.ShapeDtypeStruct((rows, cols), F32)] * 3, compiler_params=_params(("parallel", "parallel")),
    )(*(a.reshape(rows, cols) for a in (w, g, m, v)))
    return tuple(o.reshape(shape) for o in outs)


def _pack(arrays, dtype, row_multiple):
    flat = jnp.concatenate([a.astype(dtype).reshape(-1) for a in arrays])
    per = LANES * row_multiple
    total = -(-flat.shape[0] // per) * per
    return jnp.pad(flat, (0, total - flat.shape[0])).reshape(total // LANES, LANES)


def _unpack(buf, shapes):
    flat = buf.reshape(-1)
    out, off = [], 0
    for s in shapes:
        n = math.prod(s)
        out.append(flat[off:off + n].reshape(s))
        off += n
    return out


def _input_weights(blocks):
    c0, c1, c2, c3 = (blocks[..., k, :, :] for k in range(N_CHIPS))
    pad = lambda n: jnp.zeros(c0.shape[:-1] + (n,), blocks.dtype)
    w_br = jnp.concatenate([c1[..., 376:1400], c0[..., 0:896], pad(64), c0[..., 896:928], pad(32), c0[..., 928:], c1[..., 0:376]], axis=-1)
    return w_br, jnp.concatenate([c1[..., 1400:], c2, c3], axis=-1)


def _input_weights_inverse(dw_br, dw_gl):
    c0 = jnp.concatenate([dw_br[..., 1024:1920], dw_br[..., 1984:2016], dw_br[..., 2048:2952]], axis=-1)
    c1 = jnp.concatenate([dw_br[..., 2952:ZB], dw_br[..., 0:1024], dw_gl[..., 0:432]], axis=-1)
    return jnp.stack([c0, c1, dw_gl[..., 432:2264], dw_gl[..., 2264:]], axis=-3)


def _block_diag(pw):
    zeros = lambda n: jnp.zeros(pw.shape[:-3] + (64, n), pw.dtype)
    rows = [jnp.concatenate([zeros(64 * g), pw[..., g, :, :], zeros(64 * (3 - g))], axis=-1) for g in range(4)]
    return jnp.concatenate(rows, axis=-2)


def _block_diag_inverse(d):
    return jnp.stack([d[..., 64 * g:64 * (g + 1), 64 * g:64 * (g + 1)] for g in range(4)], axis=-3)


def _pad_rows(a, n):
    return jnp.pad(a, ((0, n - a.shape[0]), (0, 0)))


def _rope_tables(rows):
    inv = 1.0 / (ROPE_THETA ** (jnp.arange(0, QK_ROPE, 2, dtype=F32) / QK_ROPE))
    ang = jnp.arange(rows, dtype=F32)[:, None] * inv[None, :]
    cos, sin = jnp.cos(ang), jnp.sin(ang)
    one, zero = jnp.ones((rows, 1), F32), jnp.zeros((rows, 1), F32)
    rep = lambda a, n: jnp.broadcast_to(a, (rows, n))
    c = jnp.concatenate([rep(one, 64), cos, cos, rep(one, 32)], axis=1)
    s1 = jnp.concatenate([rep(zero, 64), -sin, rep(zero, 48)], axis=1)
    s2 = jnp.concatenate([rep(zero, 80), sin, rep(zero, 32)], axis=1)
    return jnp.concatenate([c, s1, s2], axis=1)


def _misc_block(parts):
    lead = parts["w_uq"].shape[:-2]
    pad_last = lambda a, n: jnp.pad(a, [(0, 0)] * (a.ndim - 1) + [(0, n - a.shape[-1])])
    uq = pad_last(parts["w_uq"].reshape(lead + (256, 2, QK_NOPE + QK_ROPE)), HEAD_PAD).reshape(lead + (256, 256))
    kv = parts["w_ukv"].reshape(lead + (128, 2, QK_NOPE + V_DIM))
    keys = pad_last(kv[..., :QK_NOPE], HEAD_PAD).reshape(lead + (128, 256))
    values = pad_last(kv[..., QK_NOPE:].reshape(lead + (128, 2 * V_DIM)), 256)
    wo = jnp.swapaxes(parts["w_o"].reshape(lead + (256, N_CHIPS, 256)), -3, -2).reshape(lead + (D_MODEL, 256))
    gap = jnp.zeros(lead + (M_UQ - M_SC - 256, 256), uq.dtype)
    return jnp.concatenate([parts["w_out_mla"], parts["w_out_pool"], parts["w_out_conf"], parts["w_out_sc"], gap, uq, keys, values, wo],
                           axis=-2)


def _misc_unblock(block):
    lead = block.shape[:-2]
    rows = lambda lo, n: block[..., lo:lo + n, :]
    uq = rows(M_UQ, 256).reshape(lead + (256, 2, HEAD_PAD))[..., :QK_NOPE + QK_ROPE].reshape(lead + (256, 2 * (QK_NOPE + QK_ROPE)))
    keys = rows(M_UKVK, 128).reshape(lead + (128, 2, HEAD_PAD))[..., :QK_NOPE]
    values = rows(M_UKVV, 128)[..., :2 * V_DIM].reshape(lead + (128, 2, V_DIM))
    wo = jnp.swapaxes(rows(M_WO, D_MODEL).reshape(lead + (N_CHIPS, 256, 256)), -3, -2).reshape(lead + (256, D_MODEL))
    return dict(w_out_mla=rows(M_MLA, 512), w_out_pool=rows(M_POOL, 256), w_out_conf=rows(M_CONF, 256), w_out_sc=rows(M_SC, 256), w_uq=uq,
                w_ukv=jnp.concatenate([keys, values], axis=-1).reshape(lead + (128, 256)), w_o=wo)


def _to_chip_blocks(name, a):
    if name == "w_o":
        return a.reshape(a.shape[:-2] + (N_CHIPS, a.shape[-2] // N_CHIPS, a.shape[-1]))
    return jnp.swapaxes(a.reshape(a.shape[:-1] + (N_CHIPS, a.shape[-1] // N_CHIPS)), -3, -2)


def _from_chip_blocks(name, b):
    if name == "w_o":
        return b.reshape(b.shape[:-3] + (N_CHIPS * b.shape[-2], b.shape[-1]))
    s = jnp.swapaxes(b, -3, -2)
    return s.reshape(s.shape[:-2] + (N_CHIPS * s.shape[-1],))


LARGE = ("w_in",) + MISC


def gather_small(shards):
    small = chip_exchange(_pack([shards[n] for n, _, _ in SHARDED_SMALL], F32, 8), True, "gather_small_ici")
    per_chip = [_unpack(small[k], [s for _, s, _ in SHARDED_SMALL]) for k in range(N_CHIPS)]
    return {name: jnp.concatenate([per_chip[k][idx] for k in range(N_CHIPS)], axis=axis) for idx, (name, _, axis) in enumerate(SHARDED_SMALL)}


class LocalWeights:
    def __init__(self, full):
        self.w_in = _to_chip_blocks("w_in", full["w_in"])
        self.misc = _misc_block({n: _to_chip_blocks(n, full[n]) for n in MISC}).astype(BF16)
        self.grads = [None] * DEPTH

    def layer(self, i):
        return self.w_in[i], (self.misc, i)

    def gather_with_attention(self, i):
        return None

    def gathered(self, dsts):
        pass

    def exchange_with_attention(self):
        return None

    def exchanged(self, dsts):
        pass

    def swap_with_postnorm(self):
        return None

    def swapped(self, rs):
        pass

    def put_grads(self, i, w_in, misc):
        self.grads[i] = (w_in, misc)

    def reduced(self):
        out = {n: _from_chip_blocks(n, b) for n, b in _misc_unblock(jnp.stack([m for _, m in self.grads])).items()}
        out["w_in"] = _from_chip_blocks("w_in", jnp.stack([w for w, _ in self.grads]))
        return out


class MeshWeights:
    def __init__(self, shards, c, chip):
        self.c, self.chip = c, chip
        self.srcs = [shards["w_in"].astype(BF16), _misc_block({n: shards[n] for n in MISC}).astype(BF16)]
        dsts = [lax.empty((DEPTH, N_CHIPS) + s.shape[1:], BF16) for s in self.srcs]
        self.dsts = gather_layer(self.srcs, dsts, 0, "gather_layer")
        self.landed = [lax.empty((DEPTH, N_CHIPS, s.shape[1] // 2, s.shape[2]), BF16) for s in self.srcs]
        self.pending = self.to_swap = None

    def layer(self, i):
        if i > 0:
            return self.dsts[0][i], (self.dsts[1], i)
        own = (jnp.arange(N_CHIPS) == self.chip)[:, None, None]
        w_in, misc = (jnp.where(own, s[0][None], d[0]) for s, d in zip(self.srcs, self.dsts))
        return w_in, (misc[None], 0)

    def gather_with_attention(self, i):
        return (self.srcs, self.dsts, i + 1) if i + 1 < DEPTH else None

    def gathered(self, dsts):
        if dsts:
            self.dsts = dsts

    def exchange_with_attention(self):
        return None if self.pending is None else (self.pending[0], self.landed, self.pending[1])

    def exchanged(self, dsts):
        if dsts:
            self.landed, self.pending = dsts, None

    def put_grads(self, i, w_in, misc):
        self.to_swap = ([w_in.astype(BF16), misc.astype(BF16)], i)

    def swap_with_postnorm(self):
        return None if self.to_swap is None else self.to_swap[0]

    def swapped(self, rs):
        if rs:
            ps, i = self.to_swap
            self.pending = ([add_row_half(p, r, self.c, "reduce_pair_%d" % a) for a, (p, r) in enumerate(zip(ps, rs))], i)
            self.to_swap = None

    def reduced(self):
        self.swapped(swap_row_halves(self.to_swap[0], "reduce_swap"))
        landed = exchange_layer(self.pending[0], self.landed, self.pending[1], "reduce_exchange")
        gs = [sum_row_halves(l, self.c, "reduce_sum_%d" % a) for a, l in enumerate(landed)]
        g_in, g_misc = share_row_halves(gs, "reduce_share")
        out = {"w_in": g_in}
        out.update(_misc_unblock(g_misc))
        return out


def reduce_small(grads, chip):
    names = [n for n, _ in REPLICATED] + [n for n, _, _ in SHARDED_SMALL]
    buf = _pack([grads[n] for n in names], F32, 8)
    chip_sum = add_pair(buf, sibling_swap(buf, "reduce_small_d2d"), F32, "reduce_small_pair")
    total = sum_slots(chip_exchange(chip_sum, True, "reduce_small_ici"), "reduce_small_sum")
    out = dict(zip(names, _unpack(total, [grads[n].shape for n in names])))
    for name, shape, axis in SHARDED_SMALL:
        out[name] = lax.dynamic_slice_in_dim(out[name], chip * shape[axis], shape[axis], axis)
    return out


def _prepare_small(w):
    row = lambda a: a[:, None, :]
    conf_vec = jnp.concatenate([row(w["conf_dw_b"]), row(w["conf_ln_g"]), row(w["conf_ln_b"]), jnp.zeros((DEPTH, 5, 256), F32)], axis=1)
    return dict(
        gpre=row(w["pre_norm_g"]), bias=row(w["gate_bias"]), pwbd=_block_diag(w["pool_w"]).astype(BF16), pscale=row(w["pool_scale"]),
        gq=row(w["q_norm_g"]), gkv=row(w["kv_norm_g"]), conf_w=jnp.pad(w["conf_dw_w"].astype(F32), ((0, 0), (0, 32 - CONF_K), (0, 0))),
        conf_vec=conf_vec, sc_w=jnp.pad(w["sc_dw_w"].astype(F32), ((0, 0), (0, 8 - SC_K), (0, 0))), gpost=row(w["post_norm_g"]))


def _prepare_layer(w_in_blocks, misc):
    w_br, w_gl = _input_weights(w_in_blocks)
    one = lambda a: a.astype(BF16)[None]
    return dict(w_br=one(w_br), w_gl=one(w_gl), misc=misc)


def local_step(x, target, w, large):
    seq = x.shape[0]
    length = N_META + seq
    rows = -(-length // ROW_TILE) * ROW_TILE
    bt = _big_tile(rows)
    hres = _pad_rows(jnp.concatenate([w["meta_tokens"].astype(F32), x], axis=0), rows)
    tgt = jnp.pad(target, ((N_META, rows - length), (0, 0)))
    rope = _rope_tables(rows)
    sw = _prepare_small(w)

    saved = []
    for i in range(DEPTH):
        lw = _prepare_layer(*large.layer(i))
        z_br, hb, hbt = prenorm_project(hres, sw["gpre"], lw["w_br"], i)
        z_gl = matmul(hb, lw["w_gl"], "nn", BF16, bt, 2048, D_MODEL, "project_gates", b_layer=0)
        ua, uc, ud, q, k, v = branches_fwd(z_br, rope, sw["pwbd"], sw["pscale"], sw["gq"], sw["gkv"], lw["misc"], sw["conf_w"],
                                           sw["conf_vec"], sw["sc_w"], i)
        o_att, lse, dsts = attention_fwd(q, k, v, large.gather_with_attention(i))
        large.gathered(dsts)
        ub, mb, o, hnew = merge_fwd(ua, o_att, uc, ud, z_br, z_gl, sw["bias"], lw["misc"], sw["gpost"], hres, i)
        saved.append(dict(lw=lw, hres=hres, hbt=hbt, z_br=z_br, z_gl=z_gl, ua=ua, ub=ub, uc=uc, ud=ud, q=q, k=k, v=v, o_att=o_att,
                          lse=lse, mb=mb, o=o))
        hres = hnew

    dh, total = loss_head(hres, tgt, seq)

    g = {n: [None] * DEPTH for n in ("gpre", "bias", "pwbd", "pscale", "gq", "gkv", "conf_w", "conf_vec", "sc_w", "gpost")}
    for i in reversed(range(DEPTH)):
        s = saved[i]
        lw = s["lw"]
        dm, dwo, g["gpost"][i], rs = postnorm_bwd(dh, s["o"], s["mb"], lw["misc"], sw["gpost"], i, large.swap_with_postnorm())
        large.swapped(rs)
        dz_br = lax.empty((rows, ZB), BF16)
        dua, do, duc, dud, dz_gl, dwout, g["bias"][i], dz_br, delta = merge_bwd(
            dm, s["ua"], s["ub"], s["uc"], s["ud"], s["z_gl"], sw["bias"], lw["misc"], s["o_att"], s["z_br"], dz_br, i)
        dz_br, g["pwbd"][i], g["pscale"][i], g["sc_w"][i] = pool_shortconv_bwd(s["z_br"], dua, dud, sw["pwbd"], sw["pscale"], sw["sc_w"],
                                                                              dz_br, i)
        dc, dz_br, g["conf_vec"][i] = conformer_bwd_tail(s["z_br"], duc, sw["conf_w"], sw["conf_vec"], dz_br, i)
        dz_br, g["conf_w"][i] = conformer_bwd_conv(s["z_br"], dc, sw["conf_w"], dz_br, i)
        dq, dk, dv, dsts = attention_bwd(s["q"], s["k"], s["v"], do, s["lse"], delta, large.exchange_with_attention())
        large.exchanged(dsts)
        dz_br, dwup, g["gq"][i], g["gkv"][i] = mla_prep_bwd(dq, dk, dv, s["z_br"], rope, sw["gq"], sw["gkv"], lw["misc"], dz_br, i)
        dw_br = matmul(s["hbt"], dz_br, "nn", BF16, D_MODEL, ZB // 2, bt, "grad_w_branch")
        dw_gl = matmul(s["hbt"], dz_gl, "nn", BF16, D_MODEL, 2048, bt, "grad_w_gates")
        dh_gl = matmul(dz_gl, lw["w_gl"], "nt", F32, bt, D_MODEL, 2048, "grad_h_gates", b_layer=0)
        dh, g["gpre"][i] = prenorm_bwd(dz_br, lw["w_br"], dh_gl, s["hres"], sw["gpre"], dh, i)
        gap = jnp.zeros((N_CHIPS, M_UQ - M_SC - 256, 256), F32)
        large.put_grads(i, _input_weights_inverse(dw_br, dw_gl), jnp.concatenate([dwout, gap, dwup, dwo], axis=1))

    g = {n: jnp.stack(parts) for n, parts in g.items()}
    grads = dict(
        meta_tokens=dh[:N_META], pre_norm_g=g["gpre"][:, 0], gate_bias=g["bias"][:, 0], pool_w=_block_diag_inverse(g["pwbd"]),
        pool_scale=g["pscale"][:, 0], q_norm_g=g["gq"][:, 0], kv_norm_g=g["gkv"][:, 0], conf_dw_w=g["conf_w"][:, :CONF_K],
        conf_dw_b=g["conf_vec"][:, 2], conf_ln_g=g["conf_vec"][:, 0], conf_ln_b=g["conf_vec"][:, 1], sc_dw_w=g["sc_w"][:, :SC_K],
        post_norm_g=g["gpost"][:, 0])
    return total[0, 0], dh[N_META:length], grads


def kernel(x, meta_tokens, pre_norm_g, w_in, gate_bias, pool_w, pool_scale, w_out_pool, q_norm_g, w_uq, kv_norm_g, w_ukv, w_out_mla, conf_dw_w, conf_dw_b, conf_ln_g, conf_ln_b, w_out_conf, sc_dw_w, w_out_sc, w_o, post_norm_g, loss_target, m_meta_tokens, m_pre_norm_g, m_w_in, m_gate_bias, m_pool_w, m_pool_scale, m_w_out_pool, m_q_norm_g, m_w_uq, m_kv_norm_g, m_w_ukv, m_w_out_mla, m_conf_dw_w, m_conf_dw_b, m_conf_ln_g, m_conf_ln_b, m_w_out_conf, m_sc_dw_w, m_w_out_sc, m_w_o, m_post_norm_g, v_meta_tokens, v_pre_norm_g, v_w_in, v_gate_bias, v_pool_w, v_pool_scale, v_w_out_pool, v_q_norm_g, v_w_uq, v_kv_norm_g, v_w_ukv, v_w_out_mla, v_conf_dw_w, v_conf_dw_b, v_conf_ln_g, v_conf_ln_b, v_w_out_conf, v_sc_dw_w, v_w_out_sc, v_w_o, v_post_norm_g):
    args = locals()
    weights = {n: args[n] for n in WEIGHT_ORDER}
    c = lax.axis_index("c")
    chip = 2 * lax.axis_index("x") + lax.axis_index("y")

    small = {n: weights[n] for n, _ in REPLICATED}
    small.update(gather_small(weights))
    large = MeshWeights(weights, c, chip)
    total, dx, grads = local_step(x[0], loss_target[0], small, large)
    loss = lax.psum(total * (0.5 / D_MODEL), ("x", "y", "c"))

    reduced = large.reduced()
    reduced.update(reduce_small(grads, chip))

    flip = lambda a: jnp.swapaxes(a, 1, 2)
    deltas, new_m, new_v = [], [], []
    for n in WEIGHT_ORDER:
        operands = (weights[n], reduced[n], args["m_" + n], args["v_" + n])
        if n == "w_in":
            operands = (flip(operands[0]), lax.optimization_barrier(flip(operands[1])), flip(operands[2]), flip(operands[3]))
            reduced[n] = flip(operands[1])
        d, nm, nv = adamw(*operands)
        if n == "w_in":
            d, nm, nv = flip(d), flip(nm), flip(nv)
        deltas.append(d)
        new_m.append(nm)
        new_v.append(nv)
    return (loss, dx[None], *[reduced[n] for n in WEIGHT_ORDER], *deltas, *new_m, *new_v)
```

```python
import functools
import math

import jax
import jax.numpy as jnp
from jax import lax
from jax.experimental import pallas as pl
from jax.experimental.pallas import tpu as pltpu

F32 = jnp.float32
BF16 = jnp.bfloat16

D_MODEL = 1024
DEPTH = 4
N_META = 16
EPS = 1e-6
HEADS = 8
QK_NOPE = 64
QK_ROPE = 32
V_DIM = 64
HEAD_PAD = 128
ROPE_THETA = 10000.0
Q_SCALE = (QK_NOPE + QK_ROPE) ** -0.5
CONF_K = 31
SC_K = 3
IN_W = 7328
N_CHIPS = 4

ZB = 3328
ZG = 4096
BG, C2, XV, SG, PV, PG, CQ, CKV, KR, MG, CA, CGT, CG = (0, 256, 512, 768, 1024, 1280, 1536, 1792, 1920, 2048, 2560, 2816, 3072)

KEY_GROUP = 4
ROW_TILE = 384
HALO = 32
LANES = 128
VMEM_LIMIT = 56 * 1024 * 1024

ADAM_LR = 0.001
ADAM_B1 = 0.9
ADAM_B2 = 0.999
ADAM_EPS = 1e-08
ADAM_WD = 0.01
ADAM_STEP = 10

MESH = pl.DeviceIdType.MESH
ANY = pl.BlockSpec(memory_space=pl.ANY)

MISC = ("w_out_mla", "w_out_pool", "w_out_conf", "w_out_sc", "w_uq", "w_ukv", "w_o")
M_MLA, M_POOL, M_CONF, M_SC, M_UQ, M_UKVK, M_UKVV, M_WO, MISC_ROWS = 0, 512, 768, 1024, 1536, 1792, 1920, 2048, 3072
SHARDED_SMALL = (
    ("meta_tokens", (N_META, 256), 1),
    ("conf_dw_w", (DEPTH, CONF_K, 64), 2),
    ("sc_dw_w", (DEPTH, SC_K, 64), 2),
)
REPLICATED = (
    ("pre_norm_g", (DEPTH, D_MODEL)),
    ("gate_bias", (DEPTH, 4 * D_MODEL)),
    ("pool_w", (DEPTH, 4, 64, 64)),
    ("pool_scale", (DEPTH, 256)),
    ("q_norm_g", (DEPTH, 256)),
    ("kv_norm_g", (DEPTH, 128)),
    ("conf_dw_b", (DEPTH, 256)),
    ("conf_ln_g", (DEPTH, 256)),
    ("conf_ln_b", (DEPTH, 256)),
    ("post_norm_g", (DEPTH, D_MODEL)),
)
WEIGHT_ORDER = ("meta_tokens", "pre_norm_g", "w_in", "gate_bias", "pool_w", "pool_scale", "w_out_pool", "q_norm_g", "w_uq",
                "kv_norm_g", "w_ukv", "w_out_mla", "conf_dw_w", "conf_dw_b", "conf_ln_g", "conf_ln_b", "w_out_conf", "sc_dw_w",
                "w_out_sc", "w_o", "post_norm_g")


def _dot(a, b):
    return lax.dot_general(a, b, (((1,), (0,)), ((), ())), preferred_element_type=F32)


def _dot_nt(a, b):
    return lax.dot_general(a, b, (((1,), (1,)), ((), ())), preferred_element_type=F32)


def _dot_tn(a, b):
    return lax.dot_general(a, b, (((0,), (0,)), ((), ())), preferred_element_type=F32)


def _sigmoid(x):
    return jax.nn.sigmoid(x)


def _silu(x):
    return x * _sigmoid(x)


def _silu_grad(x):
    s = _sigmoid(x)
    return s * (1.0 + x * (1.0 - s))


def _rms(x, g):
    return x * lax.rsqrt(jnp.mean(x * x, axis=-1, keepdims=True) + EPS) * g


def _sh(x, d):
    return x if d == 0 else pltpu.roll(x, d, 0)


def _ash(x, d):
    return x if d == 0 else pltpu.roll(x, x.shape[0] - d, 0)


def _lanes8(t):
    return jnp.concatenate([t] * HEADS, axis=1)


def _pool_window_sums(v, shift):
    a2 = v + shift(v, 1)
    a4 = a2 + shift(a2, 2)
    a8 = a4 + shift(a4, 4)
    a16 = a8 + shift(a8, 8)
    lane = lax.broadcasted_iota(jnp.int32, v.shape, 1)
    return jnp.where(lane < 64, a2, jnp.where(lane < 128, a4, jnp.where(lane < 192, a8, a16)))


def _pool_counts(first_row, rows):
    pos = first_row + lax.broadcasted_iota(jnp.int32, (rows, 256), 0)
    lane = lax.broadcasted_iota(jnp.int32, (rows, 256), 1)
    width = jnp.where(lane < 64, 2, jnp.where(lane < 128, 4, jnp.where(lane < 192, 8, 16)))
    return jnp.maximum(jnp.minimum(pos + 1, width), 1).astype(F32)


def _params(sem=None):
    return pltpu.CompilerParams(dimension_semantics=sem, vmem_limit_bytes=VMEM_LIMIT)


def _tile_specs(t, n_halo_blocks, li=0):
    per = t // HALO

    def layer(shape, idx=li):
        return pl.BlockSpec((None,) + tuple(shape), lambda i: (idx,) + (0,) * len(shape))

    def cur(c, cb=0):
        return pl.BlockSpec((t, c), lambda i: (i, cb))

    def prev(c, cb=0):
        return pl.BlockSpec((HALO, c), lambda i: (jnp.maximum(i * per - 1, 0), cb))

    def nxt(c, cb=0):
        return pl.BlockSpec((HALO, c), lambda i: (jnp.minimum((i + 1) * per, n_halo_blocks - 1), cb))

    def full(shape):
        return pl.BlockSpec(shape, lambda i: (0,) * len(shape))

    return cur, prev, nxt, full, layer


def _big_tile(rows):
    return rows // 3 if rows % (3 * LANES) == 0 else ROW_TILE


def matmul(a, b, mode, out_dtype, tm, tn, tk, name, b_layer=None):
    bs = b.shape if b_layer is None else b.shape[1:]
    lead = () if b_layer is None else (None,)
    pick = (lambda *ix: ix) if b_layer is None else (lambda *ix: (b_layer,) + ix)
    if mode == "nn":
        (m, k), n = a.shape, bs[1]
        a_spec = pl.BlockSpec((tm, tk), lambda i, j, kk: (i, kk))
        b_spec = pl.BlockSpec(lead + (tk, tn), lambda i, j, kk: pick(kk, j))
        dot = _dot
    elif mode == "nt":
        (m, k), n = a.shape, bs[0]
        a_spec = pl.BlockSpec((tm, tk), lambda i, j, kk: (i, kk))
        b_spec = pl.BlockSpec(lead + (tn, tk), lambda i, j, kk: pick(j, kk))
        dot = _dot_nt
    else:
        (k, m), n = a.shape, bs[1]
        a_spec = pl.BlockSpec((tk, tm), lambda i, j, kk: (kk, i))
        b_spec = pl.BlockSpec(lead + (tk, tn), lambda i, j, kk: pick(kk, j))
        dot = _dot_tn
    assert m % tm == 0 and n % tn == 0 and k % tk == 0, (a.shape, bs, tm, tn, tk)
    nk = k // tk

    def body(a_ref, b_ref, o_ref, *acc):
        if nk == 1:
            o_ref[...] = dot(a_ref[...], b_ref[...]).astype(out_dtype)
            return
        (acc_ref,) = acc
        kk = pl.program_id(2)

        @pl.when(kk == 0)
        def _():
            acc_ref[...] = dot(a_ref[...], b_ref[...])

        @pl.when((kk > 0) & (kk < nk - 1))
        def _():
            acc_ref[...] += dot(a_ref[...], b_ref[...])

        @pl.when(kk == nk - 1)
        def _():
            o_ref[...] = (acc_ref[...] + dot(a_ref[...], b_ref[...])).astype(out_dtype)

    return pl.pallas_call(
        body, name=name, grid=(m // tm, n // tn, nk), in_specs=[a_spec, b_spec],
        out_specs=pl.BlockSpec((tm, tn), lambda i, j, kk: (i, j)), out_shape=jax.ShapeDtypeStruct((m, n), out_dtype),
        scratch_shapes=[pltpu.VMEM((tm, tn), F32)] if nk > 1 else [], compiler_params=_params(("parallel", "parallel", "arbitrary")),
    )(a, b)


def prenorm_project(hres, g, w, li):
    rows, d = hres.shape
    n = w.shape[2]
    tm, tn = _big_tile(rows), n // 2

    def body(x_ref, g_ref, w_ref, z_ref, hb_ref):
        @pl.when(pl.program_id(1) == 0)
        def _():
            hb_ref[...] = _rms(x_ref[...], g_ref[...]).astype(BF16)

        z_ref[...] = _dot(hb_ref[...], w_ref[...]).astype(BF16)

    return pl.pallas_call(
        body, name="prenorm_project", grid=(rows // tm, n // tn),
        in_specs=[pl.BlockSpec((tm, d), lambda i, j: (i, 0)), pl.BlockSpec((None, 1, d), lambda i, j: (li, 0, 0)),
                  pl.BlockSpec((None, d, tn), lambda i, j: (0, 0, j))],
        out_specs=[pl.BlockSpec((tm, tn), lambda i, j: (i, j)), pl.BlockSpec((tm, d), lambda i, j: (i, 0))],
        out_shape=[jax.ShapeDtypeStruct((rows, n), BF16), jax.ShapeDtypeStruct((rows, d), BF16)],
        compiler_params=_params(("parallel", "arbitrary")),
    )(hres, g, w)


def _rope(q, c, s1, s2, width):
    return q * c + pltpu.roll(q, width - 16, 1) * s1 + pltpu.roll(q, 16, 1) * s2


def _rope_transposed(dq, c, s1, s2, width):
    return dq * c + pltpu.roll(dq * s1, 16, 1) + pltpu.roll(dq * s2, width - 16, 1)


def _conf_conv(g1, w_ref):
    acc = jnp.zeros_like(g1)
    for k in range(CONF_K):
        acc = acc + w_ref[k:k + 1, :] * _sh(g1, CONF_K - 1 - k)
    return acc


def _conf_tail(c, cg, lg, lb):
    mu = jnp.mean(c, axis=-1, keepdims=True)
    xc = c - mu
    var = jnp.mean(xc * xc, axis=-1, keepdims=True)
    n = xc * lax.rsqrt(var + EPS) * lg + lb
    return _silu(n) * _silu(cg)


def _misc_spec(misc, row0, rows):
    assert row0 % rows == 0
    return pl.BlockSpec((None, N_CHIPS, rows, 256), lambda i: (misc[1], 0, row0 // rows, 0))


def _chip_columns(x, w_ref, row0, rows, lanes=256):
    return jnp.concatenate([_dot(x, w_ref[k, row0:row0 + rows, 0:lanes]) for k in range(N_CHIPS)], axis=1)


def branches_fwd(z_br, rope, pwbd, pscale, gq, gkv, misc, conf_w, conf_vec, sc_w, li):
    rows = z_br.shape[0]
    t = ROW_TILE
    cur, prev, _, _, layer = _tile_specs(t, rows // HALO, li)

    def body(zc_ref, zp_ref, rope_ref, pw_ref, ps_ref, gq_ref, gkv_ref, up_ref, cw_ref, cv_ref, sw_ref,
             ua_ref, uc_ref, ud_ref, q_ref, k_ref, v_ref):
        i = pl.program_id(0)
        zp = jnp.where(i == 0, jnp.zeros(zp_ref.shape, zp_ref.dtype), zp_ref[...])

        def ext(lo, w=256):
            return jnp.concatenate([zp[:, lo:lo + w], zc_ref[:, lo:lo + w]], axis=0).astype(F32)

        def col(lo, w=256):
            return zc_ref[:, lo:lo + w].astype(F32)

        v = ext(PV)
        p = (_pool_window_sums(v, _sh) / _pool_counts(i * t - HALO, t + HALO) - v)[HALO:]
        ya = _dot(p.astype(BF16), pw_ref[...]) * ps_ref[...]
        ua_ref[...] = (ya * _silu(col(PG))).astype(BF16)

        g1 = ext(CA) * _sigmoid(ext(CGT))
        c = _conf_conv(g1, cw_ref)[HALO:] + cv_ref[0:1, :]
        uc_ref[...] = _conf_tail(c, col(CG), cv_ref[1:2, :], cv_ref[2:3, :]).astype(BF16)

        e = ext(C2) * ext(XV)
        f = jnp.zeros_like(e)
        for k in range(SC_K):
            f = f + sw_ref[k:k + 1, :] * _sh(e, SC_K - 1 - k)
        ud_ref[...] = (col(BG) * f[HALO:] * _silu(col(SG))).astype(BF16)

        cth, s1, s2 = rope_ref[:, 0:128], rope_ref[:, 128:256], rope_ref[:, 256:384]
        qn = _rms(col(CQ), gq_ref[...]).astype(BF16)
        q = _chip_columns(qn, up_ref, 0, 256)
        w8 = HEADS * HEAD_PAD
        q_ref[...] = (_rope(q, _lanes8(cth), _lanes8(s1), _lanes8(s2), w8) * Q_SCALE).astype(BF16)
        kvn = _rms(col(CKV, 128), gkv_ref[...]).astype(BF16)
        kr = _rope(col(KR, 128), cth, s1, s2, HEAD_PAD)
        k_ref[...] = (_chip_columns(kvn, up_ref, M_UKVK - M_UQ, 128) + _lanes8(kr)).astype(BF16)
        v_ref[...] = _chip_columns(kvn, up_ref, M_UKVV - M_UQ, 128, 2 * V_DIM).astype(BF16)

    outs = [jax.ShapeDtypeStruct((rows, 256), BF16)] * 3 + [jax.ShapeDtypeStruct((rows, 1024), BF16)] * 2 + [
        jax.ShapeDtypeStruct((rows, 512), BF16)]
    return pl.pallas_call(
        body, name="branches_fwd", grid=(rows // t,),
        in_specs=[cur(ZB), prev(ZB), cur(384), layer((256, 256)), layer((1, 256)), layer((1, 256)), layer((1, 128)),
                  _misc_spec(misc, M_UQ, M_WO - M_UQ), layer((32, 256)), layer((8, 256)), layer((8, 256))],
        out_specs=[cur(256), cur(256), cur(256), cur(1024), cur(1024), cur(512)], out_shape=outs,
        compiler_params=_params(("parallel",)),
    )(z_br, z_br, rope, pwbd, pscale, gq, gkv, misc[0], conf_w, conf_vec, sc_w)


def _head_lane_mask(h):
    lane = lax.broadcasted_iota(jnp.int32, (1, 2 * V_DIM), 1)
    return (lane >= V_DIM * h) & (lane < V_DIM * (h + 1))


def attention_fwd(q, k, v, gather=None):
    rows = q.shape[0]
    tq = ROW_TILE
    nq = rows // tq
    n = 0 if gather is None else len(gather[0])

    def body(*refs):
        if n:
            start, finish = _gather_ops(refs[3:3 + n], refs[5 + 2 * n:5 + 3 * n], refs[5 + 3 * n:], gather[2], True)
            pl.when((pl.program_id(0) == 0) & (pl.program_id(1) == 0))(start)
        compute(*refs[:3], *refs[3 + 2 * n:5 + 2 * n])
        if n:
            pl.when((pl.program_id(0) == HEADS // 2 - 1) & (pl.program_id(1) == nq - 1))(finish)

    def compute(q_ref, k_ref, v_ref, o_ref, lse_ref):
        i = pl.program_id(1)

        def head_step(h, tile, n_tiles, carry, masked):
            m, l, acc = carry
            width = n_tiles * tq
            r0 = pl.multiple_of(tile * tq, tq)
            kh = k_ref[pl.ds(r0, width), HEAD_PAD * h:HEAD_PAD * (h + 1)]
            vh = jnp.where(_head_lane_mask(h), v_ref[pl.ds(r0, width), :], jnp.zeros((), BF16))
            s = _dot_nt(q_ref[:, HEAD_PAD * h:HEAD_PAD * (h + 1)], kh)
            if masked:
                row = lax.broadcasted_iota(jnp.int32, (tq, width), 0)
                colm = lax.broadcasted_iota(jnp.int32, (tq, width), 1)
                s = jnp.where(colm <= row + (width - tq), s, -1e30)
            m2 = jnp.maximum(m, jnp.max(s, axis=-1, keepdims=True))
            alpha = jnp.exp(m - m2)
            pr = jnp.exp(s - m2)
            return m2, alpha * l + jnp.sum(pr, axis=-1, keepdims=True), alpha * acc + _dot(pr.astype(BF16), vh)

        def step(tile, n_tiles, carry, masked):
            return tuple(head_step(h, tile, n_tiles, carry[h], masked) for h in range(2))

        init = (jnp.full((tq, 1), -1e30, F32), jnp.zeros((tq, 1), F32), jnp.zeros((tq, 2 * V_DIM), F32))
        group = min(KEY_GROUP, nq)
        carry = lax.fori_loop(0, i // group, lambda t, cr: step(group * t, group, cr, False), (init, init))
        carry = lax.switch(i % group, [functools.partial(lambda cr, r: step(i - r, r + 1, cr, True), r=r) for r in range(group)], carry)
        out = jnp.zeros((tq, 2 * V_DIM), F32)
        for h, (m, l, acc) in enumerate(carry):
            out = out + acc / l
            lse_ref[h] = jnp.broadcast_to(m + jnp.log(l), (tq, LANES))
        o_ref[...] = out.astype(BF16)

    srcs, dsts = ([], []) if gather is None else (list(gather[0]), list(gather[1]))
    outs = pl.pallas_call(
        body, name="attention_fwd" if gather is None else "attention_fwd_gather", grid=(HEADS // 2, nq),
        in_specs=[pl.BlockSpec((tq, 2 * HEAD_PAD), lambda p, i: (i, p)), pl.BlockSpec((rows, 2 * HEAD_PAD), lambda p, i: (0, p)),
                  pl.BlockSpec((rows, 2 * V_DIM), lambda p, i: (0, p))] + [ANY] * (2 * n),
        out_specs=[pl.BlockSpec((tq, 2 * V_DIM), lambda p, i: (i, p)), pl.BlockSpec((2, tq, LANES), lambda p, i: (p, i, 0))] + [ANY] * n,
        out_shape=[jax.ShapeDtypeStruct((rows, HEADS * V_DIM), BF16), jax.ShapeDtypeStruct((HEADS, rows, LANES), F32)] + [
            jax.ShapeDtypeStruct(d.shape, d.dtype) for d in dsts],
        input_output_aliases={3 + n + a: 2 + a for a in range(n)}, scratch_shapes=GATHER_SEMS(n) if n else [],
        compiler_params=_params(("arbitrary", "arbitrary") if n else ("parallel", "parallel")),
    )(q, k, v, *srcs, *dsts)
    return outs[0], outs[1], list(outs[2:])


OUT_PROJECTIONS = ((M_POOL, 256), (M_MLA, 512), (M_CONF, 256), (M_SC, 256))


def _chunks(x, n=N_CHIPS, width=256):
    return [x[:, width * k:width * (k + 1)] for k in range(n)]


def merge_fwd(ua, o_att, uc, ud, z_br, z_gl, bias, misc, gpost, hres, li):
    rows = hres.shape[0]
    t = ROW_TILE
    cur, _, _, _, layer = _tile_specs(t, rows // HALO, li)
    d = D_MODEL

    def body(ua_ref, ob_ref, uc_ref, ud_ref, mg_ref, gl_ref, b_ref, wout_ref, wo_ref, gp_ref, h_ref, ub_ref, mb_ref, o_ref, hn_ref):
        ub = (ob_ref[...].astype(F32) * _silu(mg_ref[...].astype(F32))).astype(BF16)
        ub_ref[...] = ub
        m = jnp.zeros((t, d), F32)
        for idx, (u, (row0, n)) in enumerate(zip((ua_ref[...], ub, uc_ref[...], ud_ref[...]), OUT_PROJECTIONS)):
            gate = _sigmoid(gl_ref[:, d * idx:d * (idx + 1)].astype(F32) + b_ref[:, d * idx:d * (idx + 1)])
            m = m + gate * _chip_columns(u, wout_ref, row0, n)
        mb = m.astype(BF16)
        mb_ref[...] = mb
        o = jnp.concatenate([sum(_dot(mk, wo_ref[k, 256 * j:256 * (j + 1), :]) for k, mk in enumerate(_chunks(mb)))
                             for j in range(N_CHIPS)], axis=1)
        o_ref[...] = o
        hn_ref[...] = h_ref[...] + _rms(o, gp_ref[...])

    return pl.pallas_call(
        body, name="merge_fwd", grid=(rows // t,),
        in_specs=[cur(256), cur(512), cur(256), cur(256), cur(512, MG // 512), cur(ZG), layer((1, ZG)), _misc_spec(misc, 0, 1280),
                  _misc_spec(misc, M_WO, D_MODEL), layer((1, d)), cur(d)],
        out_specs=[cur(512), cur(d), cur(d), cur(d)],
        out_shape=[jax.ShapeDtypeStruct((rows, 512), BF16), jax.ShapeDtypeStruct((rows, d), BF16), jax.ShapeDtypeStruct((rows, d), F32),
                   jax.ShapeDtypeStruct((rows, d), F32)],
        compiler_params=_params(("parallel",)),
    )(ua, o_att, uc, ud, z_br, z_gl, bias, misc[0], misc[0], gpost, hres)


def loss_head(hres, target, n_tokens):
    rows, d = hres.shape
    t = ROW_TILE
    cur, _, _, full, _ = _tile_specs(t, rows // HALO)
    n_steps = rows // t

    def body(h_ref, t_ref, dh_ref, tot_ref, acc_ref):
        i = pl.program_id(0)

        @pl.when(i == 0)
        def _():
            acc_ref[...] = jnp.zeros_like(acc_ref)

        r = i * t + lax.broadcasted_iota(jnp.int32, (t, 1), 0)
        diff = jnp.where((r >= N_META) & (r < N_META + n_tokens), h_ref[...] - t_ref[...], 0.0)
        dh_ref[...] = diff * (1.0 / d)
        acc_ref[...] += jnp.sum(diff * diff, axis=0, keepdims=True)

        @pl.when(i == n_steps - 1)
        def _():
            tot_ref[...] = jnp.broadcast_to(jnp.sum(acc_ref[...], axis=1, keepdims=True), (1, LANES))

    return pl.pallas_call(
        body, name="loss_head", grid=(n_steps,), in_specs=[cur(d), cur(d)], out_specs=[cur(d), full((1, LANES))],
        out_shape=[jax.ShapeDtypeStruct((rows, d), F32), jax.ShapeDtypeStruct((1, LANES), F32)],
        scratch_shapes=[pltpu.VMEM((1, d), F32)], compiler_params=_params(("arbitrary",)),
    )(hres, target)


def _accumulate(i, ref, value):
    @pl.when(i == 0)
    def _():
        ref[...] = value

    @pl.when(i > 0)
    def _():
        ref[...] += value


def postnorm_bwd(dh, o, mb, misc, gpost, li, swap=None):
    rows, d = dh.shape
    t = ROW_TILE
    cur, _, _, full, layer = _tile_specs(t, rows // HALO, li)
    n = 0 if swap is None else len(swap)
    steps = rows // t

    def body(*refs):
        if n:
            start, finish = _swap_ops(refs[5:5 + n], refs[8 + n:8 + 2 * n], refs[8 + 2 * n:])
            pl.when(pl.program_id(0) == 0)(start)
        compute(*refs[:5], *refs[5 + n:8 + n])
        if n:
            pl.when(pl.program_id(0) == steps - 1)(finish)

    def compute(dh_ref, o_ref, mb_ref, wo_ref, gp_ref, dm_ref, dwo_ref, dgp_ref):
        i = pl.program_id(0)
        _, vjp = jax.vjp(_rms, o_ref[...], gp_ref[...])
        do, dg = vjp(dh_ref[...])
        dob = do.astype(BF16)
        dm_ref[...] = jnp.concatenate([sum(_dot_nt(dj, wo_ref[k, 256 * j:256 * (j + 1), :]) for j, dj in enumerate(_chunks(dob)))
                                       for k in range(N_CHIPS)], axis=1)
        dwo = _dot_tn(mb_ref[...], dob)
        for k in range(N_CHIPS):
            _accumulate(i, dwo_ref.at[k], jnp.concatenate(_chunks(dwo[256 * k:256 * (k + 1), :]), axis=0))
        _accumulate(i, dgp_ref, dg)

    sent = [] if swap is None else list(swap)
    outs = pl.pallas_call(
        body, name="postnorm_bwd" if swap is None else "postnorm_bwd_swap", grid=(steps,),
        in_specs=[cur(d), cur(d), cur(d), _misc_spec(misc, M_WO, d), layer((1, d))] + [ANY] * n,
        out_specs=[cur(d), full((N_CHIPS, d, 256)), full((1, d))] + [ANY] * n,
        out_shape=[jax.ShapeDtypeStruct((rows, d), F32), jax.ShapeDtypeStruct((N_CHIPS, d, 256), F32), jax.ShapeDtypeStruct((1, d), F32)] + [
            jax.ShapeDtypeStruct((p.shape[0], p.shape[1] // 2, p.shape[2]), p.dtype) for p in sent],
        scratch_shapes=[pltpu.SemaphoreType.DMA((n,)), pltpu.SemaphoreType.DMA((n,))] if n else [],
        compiler_params=_params(("arbitrary",)),
    )(dh, o, mb, misc[0], gpost, *sent)
    return outs[0], outs[1], outs[2], list(outs[3:])


def merge_bwd(dm, ua, ub, uc, ud, z_gl, bias, misc, o_att, z_br, dz_buf, li):
    rows, d = dm.shape
    t = ROW_TILE
    cur, _, _, full, layer = _tile_specs(t, rows // HALO, li)

    def body(dm_ref, ua_ref, ub_ref, uc_ref, ud_ref, gl_ref, b_ref, w_ref, o_ref, mg_ref, _,
             dua_ref, do_ref, duc_ref, dud_ref, dgl_ref, dw_ref, db_ref, dmg_ref, delta_ref):
        i = pl.program_id(0)
        dm = dm_ref[...]
        groups = ((ua_ref, dua_ref), (ub_ref, None), (uc_ref, duc_ref), (ud_ref, dud_ref))
        for idx, ((u_ref, du_ref), (row0, n)) in enumerate(zip(groups, OUT_PROJECTIONS)):
            cols = slice(d * idx, d * (idx + 1))
            u = u_ref[...]
            gate = _sigmoid(gl_ref[:, cols].astype(F32) + b_ref[:, cols])
            dgl = dm * _chip_columns(u, w_ref, row0, n) * gate * (1.0 - gate)
            dgl_ref[:, cols] = dgl.astype(BF16)
            _accumulate(i, db_ref.at[:, cols], jnp.sum(dgl, axis=0, keepdims=True))
            dyb = (dm * gate).astype(BF16)
            du = sum(_dot_nt(dyk, w_ref[k, row0:row0 + n, :]) for k, dyk in enumerate(_chunks(dyb)))
            for k, dwk in enumerate(_chunks(_dot_tn(u, dyb))):
                _accumulate(i, dw_ref.at[k, row0:row0 + n, :], dwk)
            if du_ref is not None:
                du_ref[...] = du
                continue
            o, mg = o_ref[...].astype(F32), mg_ref[...].astype(F32)
            do = du * _silu(mg)
            do_ref[...] = do.astype(BF16)
            dmg_ref[...] = (du * o * _silu_grad(mg)).astype(BF16)
            prod = do * o
            lane = lax.broadcasted_iota(jnp.int32, (1, HEADS * V_DIM), 1)
            for h in range(HEADS):
                part = jnp.where((lane >= V_DIM * h) & (lane < V_DIM * (h + 1)), prod, 0.0)
                delta_ref[h] = jnp.broadcast_to(jnp.sum(part, axis=-1, keepdims=True), (t, LANES))

    return pl.pallas_call(
        body, name="merge_bwd", grid=(rows // t,),
        in_specs=[cur(d), cur(256), cur(512), cur(256), cur(256), cur(ZG), layer((1, ZG)), _misc_spec(misc, 0, 1280), cur(512),
                  cur(512, MG // 512), ANY],
        out_specs=[cur(256), cur(512), cur(256), cur(256), cur(ZG), full((N_CHIPS, 1280, 256)), full((1, ZG)), cur(512, MG // 512),
                   pl.BlockSpec((HEADS, t, LANES), lambda i: (0, i, 0))],
        out_shape=[jax.ShapeDtypeStruct((rows, 256), F32), jax.ShapeDtypeStruct((rows, 512), BF16), jax.ShapeDtypeStruct((rows, 256), F32),
                   jax.ShapeDtypeStruct((rows, 256), F32), jax.ShapeDtypeStruct((rows, ZG), BF16),
                   jax.ShapeDtypeStruct((N_CHIPS, 1280, 256), F32), jax.ShapeDtypeStruct((1, ZG), F32),
                   jax.ShapeDtypeStruct((rows, ZB), BF16), jax.ShapeDtypeStruct((HEADS, rows, LANES), F32)],
        input_output_aliases={10: 7}, compiler_params=_params(("arbitrary",)),
    )(dm, ua, ub, uc, ud, z_gl, bias, misc[0], o_att, z_br, dz_buf)


def pool_shortconv_bwd(z_br, dua, dud, pwbd, pscale, sc_w, dz_buf, li):
    rows = z_br.shape[0]
    t = ROW_TILE
    n_steps = rows // t
    cur, prev, nxt, full, layer = _tile_specs(t, rows // HALO, li)

    def body(zc_ref, zp_ref, zn_ref, dac_ref, dan_ref, ddc_ref, ddn_ref, pw_ref, ps_ref, sw_ref, _, dz_ref, dpw_ref, dps_ref, dw_ref):
        i = pl.program_id(0)
        last = i == n_steps - 1
        zp = jnp.where(i == 0, jnp.zeros(zp_ref.shape, zp_ref.dtype), zp_ref[...])
        zn = jnp.where(last, jnp.zeros(zn_ref.shape, zn_ref.dtype), zn_ref[...])

        def ext(lo):
            return jnp.concatenate([zp[:, lo:lo + 256], zc_ref[:, lo:lo + 256], zn[:, lo:lo + 256]], axis=0).astype(F32)

        def ext_grad(c_ref, n_ref):
            return jnp.concatenate([jnp.zeros((HALO, 256), F32), c_ref[...], jnp.where(last, jnp.zeros(n_ref.shape, F32), n_ref[...])], axis=0)

        mid = slice(HALO, HALO + t)

        bg, c2, xv, sg = ext(BG), ext(C2), ext(XV), ext(SG)
        du = ext_grad(ddc_ref, ddn_ref)
        e = c2 * xv
        shifted = [_sh(e, SC_K - 1 - k) for k in range(SC_K)]
        f = sum(sw_ref[k:k + 1, :] * shifted[k] for k in range(SC_K))
        gate = _silu(sg)
        df = du * gate * bg
        de = sum(sw_ref[k:k + 1, :] * _ash(df, SC_K - 1 - k) for k in range(SC_K))
        d_sc = [(du * gate * f)[mid], (de * xv)[mid], (de * c2)[mid], (du * bg * f * _silu_grad(sg))[mid]]
        dw = jnp.concatenate([jnp.sum((df * shifted[k])[mid], axis=0, keepdims=True) for k in range(SC_K)] + [
            jnp.zeros((8 - SC_K, 256), F32)], axis=0)
        _accumulate(i, dw_ref, dw)

        v, pg = ext(PV), ext(PG)
        cnt = _pool_counts(i * t - HALO, t + 2 * HALO)
        p = (_pool_window_sums(v, _sh) / cnt - v)[mid]
        dya = ext_grad(dac_ref, dan_ref) * _silu(pg)
        dypb = (dya * ps_ref[...]).astype(BF16)
        dp = _dot_nt(dypb, pw_ref[...])
        dv = (_pool_window_sums(dp / cnt, _ash) - dp)[mid]
        pb = p.astype(BF16)
        pw = _dot(pb, pw_ref[...])
        dpg = dac_ref[...] * pw * ps_ref[...] * _silu_grad(pg[mid])
        _accumulate(i, dpw_ref, _dot_tn(pb, dypb[mid]))
        _accumulate(i, dps_ref, jnp.sum(dya[mid] * pw, axis=0, keepdims=True))

        dz_ref[...] = jnp.concatenate(d_sc + [dv, dpg], axis=1).astype(BF16)

    return pl.pallas_call(
        body, name="pool_shortconv_bwd", grid=(n_steps,),
        in_specs=[cur(ZB), prev(ZB), nxt(ZB), cur(256), nxt(256), cur(256), nxt(256), layer((256, 256)), layer((1, 256)), layer((8, 256)),
                  ANY],
        out_specs=[cur(1536, BG // 1536), full((256, 256)), full((1, 256)), full((8, 256))],
        out_shape=[jax.ShapeDtypeStruct((rows, ZB), BF16), jax.ShapeDtypeStruct((256, 256), F32), jax.ShapeDtypeStruct((1, 256), F32),
                   jax.ShapeDtypeStruct((8, 256), F32)],
        input_output_aliases={10: 0}, compiler_params=_params(("arbitrary",)),
    )(z_br, z_br, z_br, dua, dua, dud, dud, pwbd, pscale, sc_w, dz_buf)


def conformer_bwd_tail(z_br, duc, conf_w, conf_vec, dz_buf, li):
    rows = z_br.shape[0]
    t = ROW_TILE
    cur, prev, _, full, layer = _tile_specs(t, rows // HALO, li)

    def body(zc_ref, zp_ref, du_ref, cw_ref, cv_ref, _, dc_ref, dcg_ref, dv_ref):
        i = pl.program_id(0)
        zp = jnp.where(i == 0, jnp.zeros(zp_ref.shape, zp_ref.dtype), zp_ref[...])

        def ext(lo):
            return jnp.concatenate([zp[:, lo:lo + 256], zc_ref[:, lo:lo + 256]], axis=0).astype(F32)

        g1 = ext(CA) * _sigmoid(ext(CGT))
        c = _conf_conv(g1, cw_ref)[HALO:] + cv_ref[0:1, :]
        _, vjp = jax.vjp(_conf_tail, c, zc_ref[:, CG:CG + 256].astype(F32), cv_ref[1:2, :], cv_ref[2:3, :])
        dc, dcg, dlg, dlb = vjp(du_ref[...])
        dc_ref[...] = dc
        dcg_ref[...] = dcg.astype(BF16)
        dvec = jnp.concatenate([dlg, dlb, jnp.sum(dc, axis=0, keepdims=True), jnp.zeros((5, 256), F32)], axis=0)
        _accumulate(i, dv_ref, dvec)

    return pl.pallas_call(
        body, name="conformer_bwd_tail", grid=(rows // t,), in_specs=[cur(ZB), prev(ZB), cur(256), layer((32, 256)), layer((8, 256)), ANY],
        out_specs=[cur(256), cur(256, CG // 256), full((8, 256))],
        out_shape=[jax.ShapeDtypeStruct((rows, 256), F32), jax.ShapeDtypeStruct((rows, ZB), BF16), jax.ShapeDtypeStruct((8, 256), F32)],
        input_output_aliases={5: 1}, compiler_params=_params(("arbitrary",)),
    )(z_br, z_br, duc, conf_w, conf_vec, dz_buf)


def conformer_bwd_conv(z_br, dc, conf_w, dz_buf, li):
    rows = z_br.shape[0]
    t = ROW_TILE
    n_steps = rows // t
    cur, prev, nxt, full, layer = _tile_specs(t, rows // HALO, li)

    def body(zc_ref, zp_ref, dc_ref, dn_ref, cw_ref, _, dz_ref, dw_ref):
        i = pl.program_id(0)
        zp = jnp.where(i == 0, jnp.zeros(zp_ref.shape, zp_ref.dtype), zp_ref[...])
        dcn = jnp.where(i == n_steps - 1, jnp.zeros(dn_ref.shape, dn_ref.dtype), dn_ref[...])

        def ext(lo):
            return jnp.concatenate([zp[:, lo:lo + 256], zc_ref[:, lo:lo + 256]], axis=0).astype(F32)

        a, gt = ext(CA), ext(CGT)
        sg = _sigmoid(gt)
        g1 = a * sg
        dc = dc_ref[...]
        dce = jnp.concatenate([dc, dcn], axis=0)
        dg1 = jnp.zeros_like(dce)
        dws = []
        for k in range(CONF_K):
            dg1 = dg1 + cw_ref[k:k + 1, :] * _ash(dce, CONF_K - 1 - k)
            dws.append(jnp.sum(dc * _sh(g1, CONF_K - 1 - k)[HALO:], axis=0, keepdims=True))
        dg1 = dg1[:t]
        ac, sc = a[HALO:], sg[HALO:]
        dz_ref[...] = jnp.concatenate([dg1 * sc, dg1 * ac * sc * (1.0 - sc)], axis=1).astype(BF16)
        _accumulate(i, dw_ref, jnp.concatenate(dws + [jnp.zeros((32 - CONF_K, 256), F32)], axis=0))

    return pl.pallas_call(
        body, name="conformer_bwd_conv", grid=(n_steps,), in_specs=[cur(ZB), prev(ZB), cur(256), nxt(256), layer((32, 256)), ANY],
        out_specs=[cur(512, CA // 512), full((32, 256))],
        out_shape=[jax.ShapeDtypeStruct((rows, ZB), BF16), jax.ShapeDtypeStruct((32, 256), F32)],
        input_output_aliases={5: 0}, compiler_params=_params(("arbitrary",)),
    )(z_br, z_br, dc, dc, conf_w, dz_buf)


def attention_bwd(q, k, v, do, lse, delta, exchange=None):
    rows = q.shape[0]
    tq = ROW_TILE
    nq = rows // tq
    n = 0 if exchange is None else len(exchange[0])

    def body(*refs):
        if n:
            start, finish = _exchange_ops(refs[6:6 + n], refs[9 + 2 * n:9 + 3 * n], refs[9 + 3 * n:], exchange[2])
            pl.when((pl.program_id(0) == 0) & (pl.program_id(1) == 0))(start)
        compute(*refs[:6], *refs[6 + 2 * n:9 + 2 * n])
        if n:
            pl.when((pl.program_id(0) == HEADS // 2 - 1) & (pl.program_id(1) == nq - 1))(finish)

    def compute(q_ref, k_ref, v_ref, do_ref, lse_ref, dl_ref, dq_ref, dk_ref, dv_ref):
        j = pl.program_id(1)

        @pl.when(j == 0)
        def _():
            dq_ref[...] = jnp.zeros_like(dq_ref)

        def head_step(h, tile, n_tiles, dk, dv, diagonal):
            lanes = slice(HEAD_PAD * h, HEAD_PAD * (h + 1))
            hm = _head_lane_mask(h)
            kh = k_ref[:, lanes]
            vh = jnp.where(hm, v_ref[...], jnp.zeros((), BF16))
            r0, width = pl.multiple_of(tile * tq, tq), n_tiles * tq
            qi = q_ref[pl.ds(r0, width), lanes]
            doi = jnp.where(hm, do_ref[pl.ds(r0, width), :], jnp.zeros((), BF16))
            s = _dot_nt(qi, kh)
            if diagonal:
                s = jnp.where(lax.broadcasted_iota(jnp.int32, (tq, tq), 1) <= lax.broadcasted_iota(jnp.int32, (tq, tq), 0), s, -1e30)
            pr = jnp.exp(s - lse_ref[h, pl.ds(r0, width), :][:, 0:1])
            dv = dv + _dot_tn(pr.astype(BF16), doi)
            dp = _dot_nt(doi, vh)
            ds = (pr * (dp - dl_ref[h, pl.ds(r0, width), :][:, 0:1])).astype(BF16)
            dq_ref[pl.ds(r0, width), lanes] += _dot(ds, kh)
            return dk + _dot_tn(ds, qi), dv

        def step(tile, n_tiles, carry, diagonal):
            dk0, dk1, dv = carry
            dk0, dv = head_step(0, tile, n_tiles, dk0, dv, diagonal)
            dk1, dv = head_step(1, tile, n_tiles, dk1, dv, diagonal)
            return dk0, dk1, dv

        zero = jnp.zeros((tq, HEAD_PAD), F32)
        carry = step(j, 1, (zero, zero, jnp.zeros((tq, 2 * V_DIM), F32)), True)
        odd = (nq - 1 - j) % 2
        carry = lax.cond(odd == 1, lambda cr: step(j + 1, 1, cr, False), lambda cr: cr, carry)
        dk0, dk1, dv = lax.fori_loop(0, (nq - 1 - j) // 2, lambda t, cr: step(j + 1 + odd + 2 * t, 2, cr, False), carry)
        dk_ref[:, 0:HEAD_PAD] = dk0
        dk_ref[:, HEAD_PAD:2 * HEAD_PAD] = dk1
        dv_ref[...] = dv

    srcs, dsts = ([], []) if exchange is None else (list(exchange[0]), list(exchange[1]))
    outs = pl.pallas_call(
        body, name="attention_bwd" if exchange is None else "attention_bwd_exchange", grid=(HEADS // 2, nq),
        in_specs=[pl.BlockSpec((rows, 2 * HEAD_PAD), lambda p, j: (0, p)), pl.BlockSpec((tq, 2 * HEAD_PAD), lambda p, j: (j, p)),
                  pl.BlockSpec((tq, 2 * V_DIM), lambda p, j: (j, p)), pl.BlockSpec((rows, 2 * V_DIM), lambda p, j: (0, p)),
                  pl.BlockSpec((2, rows, LANES), lambda p, j: (p, 0, 0)), pl.BlockSpec((2, rows, LANES), lambda p, j: (p, 0, 0))] + [
                      ANY] * (2 * n),
        out_specs=[pl.BlockSpec((rows, 2 * HEAD_PAD), lambda p, j: (0, p)), pl.BlockSpec((tq, 2 * HEAD_PAD), lambda p, j: (j, p)),
                   pl.BlockSpec((tq, 2 * V_DIM), lambda p, j: (j, p))] + [ANY] * n,
        out_shape=[jax.ShapeDtypeStruct((rows, HEADS * HEAD_PAD), F32), jax.ShapeDtypeStruct((rows, HEADS * HEAD_PAD), F32),
                   jax.ShapeDtypeStruct((rows, HEADS * V_DIM), F32)] + [jax.ShapeDtypeStruct(d.shape, d.dtype) for d in dsts],
        input_output_aliases={6 + n + a: 3 + a for a in range(n)}, scratch_shapes=EXCHANGE_SEMS(n) if n else [],
        compiler_params=_params(("arbitrary", "arbitrary") if n else ("parallel", "arbitrary")),
    )(q, k, v, do, lse, delta, *srcs, *dsts)
    return outs[0], outs[1], outs[2], list(outs[3:])


def mla_prep_bwd(dq, dk, dv, z_br, rope, gq, gkv, misc, dz_buf, li):
    rows = dq.shape[0]
    t = ROW_TILE
    cur, _, _, full, layer = _tile_specs(t, rows // HALO, li)
    w8 = HEADS * HEAD_PAD
    uq, keys, values = slice(0, 256), slice(M_UKVK - M_UQ, M_UKVV - M_UQ), slice(M_UKVV - M_UQ, M_WO - M_UQ)

    def body(dq_ref, dk_ref, dv_ref, z_ref, rope_ref, gq_ref, gkv_ref, up_ref, _, dz_ref, dup_ref, dgq_ref, dgkv_ref):
        i = pl.program_id(0)
        cth, s1, s2 = rope_ref[:, 0:128], rope_ref[:, 128:256], rope_ref[:, 256:384]
        dqb = _rope_transposed(dq_ref[...] * Q_SCALE, _lanes8(cth), _lanes8(s1), _lanes8(s2), w8).astype(BF16)
        dq_chunks = _chunks(dqb)
        cq = z_ref[:, 0:256].astype(F32)
        qn, vjp_q = jax.vjp(_rms, cq, gq_ref[...])
        dcq, dgq = vjp_q(sum(_dot_nt(dqk, up_ref[k, uq, :]) for k, dqk in enumerate(dq_chunks)))
        _accumulate(i, dgq_ref, dgq)

        dk = dk_ref[...]
        dkr = sum(dk[:, HEAD_PAD * h:HEAD_PAD * (h + 1)] for h in range(HEADS))
        dkr = _rope_transposed(dkr, cth, s1, s2, HEAD_PAD)
        lane = lax.broadcasted_iota(jnp.int32, (1, HEAD_PAD), 1)
        dkr = jnp.where((lane >= QK_NOPE) & (lane < QK_NOPE + QK_ROPE), dkr, 0.0)
        dkb, dvb = dk.astype(BF16), dv_ref[...].astype(BF16)
        dk_chunks, dv_chunks = _chunks(dkb), _chunks(dvb, width=2 * V_DIM)
        ckv = z_ref[:, 256:384].astype(F32)
        kvn, vjp_kv = jax.vjp(_rms, ckv, gkv_ref[...])
        dckv, dgkv = vjp_kv(sum(_dot_nt(dk_chunks[k], up_ref[k, keys, :]) + _dot_nt(dv_chunks[k], up_ref[k, values, 0:2 * V_DIM])
                                for k in range(N_CHIPS)))
        _accumulate(i, dgkv_ref, dgkv)
        dz_ref[...] = jnp.concatenate([dcq, dckv, dkr], axis=1).astype(BF16)
        qnb, kvnb = qn.astype(BF16), kvn.astype(BF16)
        d_uq, d_keys, d_values = _chunks(_dot_tn(qnb, dqb)), _chunks(_dot_tn(kvnb, dkb)), _chunks(_dot_tn(kvnb, dvb), width=2 * V_DIM)
        for k in range(N_CHIPS):
            padded = jnp.concatenate([d_values[k], jnp.zeros((128, 256 - 2 * V_DIM), F32)], axis=1)
            _accumulate(i, dup_ref.at[k], jnp.concatenate([d_uq[k], d_keys[k], padded], axis=0))

    return pl.pallas_call(
        body, name="mla_prep_bwd", grid=(rows // t,),
        in_specs=[cur(w8), cur(w8), cur(512), cur(512, CQ // 512), cur(384), layer((1, 256)), layer((1, 128)),
                  _misc_spec(misc, M_UQ, M_WO - M_UQ), ANY],
        out_specs=[cur(512, CQ // 512), full((N_CHIPS, M_WO - M_UQ, 256)), full((1, 256)), full((1, 128))],
        out_shape=[jax.ShapeDtypeStruct((rows, ZB), BF16), jax.ShapeDtypeStruct((N_CHIPS, M_WO - M_UQ, 256), F32),
                   jax.ShapeDtypeStruct((1, 256), F32), jax.ShapeDtypeStruct((1, 128), F32)],
        input_output_aliases={8: 0}, compiler_params=_params(("arbitrary",)),
    )(dq, dk, dv, z_br, rope, gq, gkv, misc[0], dz_buf)


def prenorm_bwd(dz_br, w_br, dh_gl, hres, gpre, dh_next, li):
    rows, d = hres.shape
    t = ROW_TILE
    cur, _, _, full, layer = _tile_specs(t, rows // HALO, li)

    def body(dz_ref, w_ref, dp_ref, x_ref, g_ref, dn_ref, dx_ref, dg_ref):
        i = pl.program_id(0)
        dh = _dot_nt(dz_ref[...], w_ref[...]) + dp_ref[...]
        _, vjp = jax.vjp(_rms, x_ref[...], g_ref[...])
        dx, dg = vjp(dh)
        dx_ref[...] = dx + dn_ref[...]
        _accumulate(i, dg_ref, dg)

    return pl.pallas_call(
        body, name="prenorm_bwd", grid=(rows // t,), in_specs=[cur(ZB), layer((d, ZB), 0), cur(d), cur(d), layer((1, d)), cur(d)],
        out_specs=[cur(d), full((1, d))], out_shape=[jax.ShapeDtypeStruct((rows, d), F32), jax.ShapeDtypeStruct((1, d), F32)],
        compiler_params=_params(("arbitrary",)),
    )(dz_br, w_br, dh_gl, hres, gpre, dh_next)


def _mesh_position():
    return lax.axis_index("x"), lax.axis_index("y"), lax.axis_index("c")


def chip_exchange(src, gather, name):
    block = src.shape if gather else src.shape[1:]

    def body(src_ref, dst_ref, send_sems, recv_sems, local_sem):
        x, y, c = _mesh_position()
        me = 2 * x + y
        peers = ((1 - x, y), (x, 1 - y), (1 - x, 1 - y))

        def part(k):
            return src_ref if gather else src_ref.at[k]

        def copy(j, slot):
            px, py = peers[j]
            return pltpu.make_async_remote_copy(src_ref=part(2 * px + py), dst_ref=dst_ref.at[slot], send_sem=send_sems.at[j],
                                                recv_sem=recv_sems.at[j], device_id=(px, py, c), device_id_type=MESH)

        local = pltpu.make_async_copy(part(me), dst_ref.at[me], local_sem)
        local.start()
        sends = [copy(j, me) for j in range(3)]
        for cp in sends:
            cp.start()
        for j, (px, py) in enumerate(peers):
            copy(j, 2 * px + py).wait_recv()
        for cp in sends:
            cp.wait_send()
        local.wait()

    return pl.pallas_call(
        body, name=name, in_specs=[pl.BlockSpec(memory_space=pl.ANY)], out_specs=pl.BlockSpec(memory_space=pl.ANY),
        out_shape=jax.ShapeDtypeStruct((N_CHIPS,) + tuple(block), src.dtype),
        scratch_shapes=[pltpu.SemaphoreType.DMA((3,)), pltpu.SemaphoreType.DMA((3,)), pltpu.SemaphoreType.DMA(())],
    )(src)


def sibling_swap(src, name):
    def body(src_ref, dst_ref, send_sem, recv_sem):
        x, y, c = _mesh_position()
        cp = pltpu.make_async_remote_copy(src_ref=src_ref, dst_ref=dst_ref, send_sem=send_sem, recv_sem=recv_sem,
                                          device_id=(x, y, 1 - c), device_id_type=MESH)
        cp.start()
        cp.wait()

    return pl.pallas_call(
        body, name=name, in_specs=[pl.BlockSpec(memory_space=pl.ANY)], out_specs=pl.BlockSpec(memory_space=pl.ANY),
        out_shape=jax.ShapeDtypeStruct(src.shape, src.dtype),
        scratch_shapes=[pltpu.SemaphoreType.DMA(()), pltpu.SemaphoreType.DMA(())],
    )(src)


def _comm_call(body, name, n_in, out_shapes, n_sems):
    return pl.pallas_call(
        body, name=name, in_specs=[ANY] * n_in, out_specs=[ANY] * len(out_shapes), out_shape=out_shapes,
        scratch_shapes=[pltpu.SemaphoreType.DMA((n,)) for n in n_sems])


def _row_halves(c, rows):
    half = rows // 2
    return pl.ds(pl.multiple_of(c * half, 16), half), pl.ds(pl.multiple_of((1 - c) * half, 16), half)


def _peers():
    x, y, c = _mesh_position()
    return x, y, c, 2 * x + y, ((1 - x, y), (x, 1 - y), (1 - x, 1 - y))


def _gather_ops(src, dst, sems, layer, own_copy):
    ici_send, ici_recv, d2d_send, d2d_recv, own_sems = sems
    n = len(src)

    def fetch(a, j, slot):
        x, y, c, _, peers = _peers()
        px, py = peers[j]
        mine, _ = _row_halves(c, src[a].shape[1])
        return pltpu.make_async_remote_copy(src_ref=src[a].at[layer, mine], dst_ref=dst[a].at[layer, slot, mine], send_sem=ici_send.at[3 * a + j],
                                            recv_sem=ici_recv.at[3 * a + j], device_id=(px, py, c), device_id_type=MESH)

    def forward(a, j, sibling_half):
        x, y, c, _, peers = _peers()
        px, py = peers[j]
        part = dst[a].at[layer, 2 * px + py, _row_halves(c, src[a].shape[1])[1 if sibling_half else 0]]
        return pltpu.make_async_remote_copy(src_ref=part, dst_ref=part, send_sem=d2d_send.at[3 * a + j], recv_sem=d2d_recv.at[3 * a + j],
                                            device_id=(x, y, 1 - c), device_id_type=MESH)

    def own(a):
        return pltpu.make_async_copy(src[a].at[layer], dst[a].at[layer, _peers()[3]], own_sems.at[a])

    def start():
        me = _peers()[3]
        for a in range(n):
            if own_copy:
                own(a).start()
            for j in range(3):
                fetch(a, j, me).start()

    def finish():
        peers = _peers()[4]
        for j, (px, py) in enumerate(peers):
            for a in range(n):
                fetch(a, j, 2 * px + py).wait_recv()
                forward(a, j, False).start()
        for j in range(3):
            for a in range(n):
                forward(a, j, True).wait_recv()
        for j in range(3):
            for a in range(n):
                fetch(a, j, 0).wait_send()
                forward(a, j, False).wait_send()
        if own_copy:
            for a in range(n):
                own(a).wait()

    return start, finish


def _exchange_ops(src, dst, sems, layer):
    send_sems, recv_sems, own_sems = sems
    n = len(src)

    def copy(a, j, slot):
        x, y, c, _, peers = _peers()
        px, py = peers[j]
        return pltpu.make_async_remote_copy(src_ref=src[a].at[2 * px + py], dst_ref=dst[a].at[layer, slot], send_sem=send_sems.at[3 * a + j],
                                            recv_sem=recv_sems.at[3 * a + j], device_id=(px, py, c), device_id_type=MESH)

    def own(a):
        me = _peers()[3]
        return pltpu.make_async_copy(src[a].at[me], dst[a].at[layer, me], own_sems.at[a])

    def start():
        me = _peers()[3]
        for a in range(n):
            own(a).start()
            for j in range(3):
                copy(a, j, me).start()

    def finish():
        peers = _peers()[4]
        for j, (px, py) in enumerate(peers):
            for a in range(n):
                copy(a, j, 2 * px + py).wait_recv()
        for j in range(3):
            for a in range(n):
                copy(a, j, 0).wait_send()
        for a in range(n):
            own(a).wait()

    return start, finish


GATHER_SEMS = lambda n: [pltpu.SemaphoreType.DMA((3 * n,))] * 4 + [pltpu.SemaphoreType.DMA((n,))]
EXCHANGE_SEMS = lambda n: [pltpu.SemaphoreType.DMA((3 * n,))] * 2 + [pltpu.SemaphoreType.DMA((n,))]


def gather_layer(srcs, dsts, layer, name):
    n = len(srcs)

    def body(*refs):
        start, finish = _gather_ops(refs[:n], refs[2 * n:3 * n], refs[3 * n:], layer, False)
        start()
        finish()

    return pl.pallas_call(
        body, name=name, in_specs=[ANY] * (2 * n), out_specs=[ANY] * n, out_shape=[jax.ShapeDtypeStruct(d.shape, d.dtype) for d in dsts],
        input_output_aliases={n + a: a for a in range(n)}, scratch_shapes=GATHER_SEMS(n),
    )(*srcs, *dsts)


def exchange_layer(ss, dsts, layer, name):
    n = len(ss)

    def body(*refs):
        start, finish = _exchange_ops(refs[:n], refs[2 * n:3 * n], refs[3 * n:], layer)
        start()
        finish()

    return pl.pallas_call(
        body, name=name, in_specs=[ANY] * (2 * n), out_specs=[ANY] * n, out_shape=[jax.ShapeDtypeStruct(d.shape, d.dtype) for d in dsts],
        input_output_aliases={n + a: a for a in range(n)}, scratch_shapes=EXCHANGE_SEMS(n),
    )(*ss, *dsts)


def _swap_ops(src, dst, sems):
    send_sems, recv_sems = sems

    def copy(a):
        x, y, c = _mesh_position()
        return pltpu.make_async_remote_copy(src_ref=src[a].at[:, _row_halves(c, src[a].shape[1])[1]], dst_ref=dst[a], send_sem=send_sems.at[a],
                                            recv_sem=recv_sems.at[a], device_id=(x, y, 1 - c), device_id_type=MESH)

    def start():
        for a in range(len(src)):
            copy(a).start()

    def finish():
        for a in range(len(src)):
            copy(a).wait()

    return start, finish


def swap_row_halves(ps, name):
    n = len(ps)

    def body(*refs):
        start, finish = _swap_ops(refs[:n], refs[n:2 * n], refs[2 * n:])
        start()
        finish()

    outs = [jax.ShapeDtypeStruct((p.shape[0], p.shape[1] // 2, p.shape[2]), p.dtype) for p in ps]
    return _comm_call(body, name, n, outs, (n, n))(*ps)


def add_row_half(p, r, c, name):
    n, half, cols = r.shape
    rb = _row_block(half, cols, 2)
    steps = half // rb

    def body(c_ref, p_ref, r_ref, o_ref):
        o_ref[...] = (p_ref[...].astype(F32) + r_ref[...].astype(F32)).astype(BF16)

    return pl.pallas_call(
        body, name=name, out_shape=jax.ShapeDtypeStruct(r.shape, BF16),
        grid_spec=pltpu.PrefetchScalarGridSpec(
            num_scalar_prefetch=1, grid=(n, steps),
            in_specs=[pl.BlockSpec((1, rb, cols), lambda k, i, c_ref: (k, c_ref[0] * steps + i, 0)),
                      pl.BlockSpec((1, rb, cols), lambda k, i, c_ref: (k, i, 0))],
            out_specs=pl.BlockSpec((1, rb, cols), lambda k, i, c_ref: (k, i, 0))),
        compiler_params=_params(("parallel", "parallel")),
    )(jnp.reshape(c, (1,)).astype(jnp.int32), p, r)


def sum_row_halves(l, c, name):
    layers, n, half, cols = l.shape
    rb = _row_block(half, cols, 4)
    steps = half // rb

    def body(c_ref, l_ref, o_ref):
        acc = l_ref[0, 0].astype(F32)
        for s in range(1, n):
            acc = acc + l_ref[0, s].astype(F32)
        o_ref[0] = acc

    return pl.pallas_call(
        body, name=name, out_shape=jax.ShapeDtypeStruct((layers, 2 * half, cols), F32),
        grid_spec=pltpu.PrefetchScalarGridSpec(
            num_scalar_prefetch=1, grid=(layers, steps), in_specs=[pl.BlockSpec((1, n, rb, cols), lambda a, i, c_ref: (a, 0, i, 0))],
            out_specs=pl.BlockSpec((1, rb, cols), lambda a, i, c_ref: (a, c_ref[0] * steps + i, 0))),
        compiler_params=_params(("parallel", "parallel")),
    )(jnp.reshape(c, (1,)).astype(jnp.int32), l)


def share_row_halves(gs, name):
    n = len(gs)

    def body(*refs):
        dst = refs[n:2 * n]
        send_sems, recv_sems = refs[2 * n:]
        x, y, c = _mesh_position()

        def copy(a, sibling_half):
            part = dst[a].at[:, _row_halves(c, dst[a].shape[1])[1 if sibling_half else 0]]
            return pltpu.make_async_remote_copy(src_ref=part, dst_ref=part, send_sem=send_sems.at[a], recv_sem=recv_sems.at[a],
                                                device_id=(x, y, 1 - c), device_id_type=MESH)

        for a in range(n):
            copy(a, False).start()
        for a in range(n):
            copy(a, True).wait_recv()
        for a in range(n):
            copy(a, False).wait_send()

    return pl.pallas_call(
        body, name=name, in_specs=[ANY] * n, out_specs=[ANY] * n, out_shape=[jax.ShapeDtypeStruct(g.shape, g.dtype) for g in gs],
        input_output_aliases={a: a for a in range(n)}, scratch_shapes=[pltpu.SemaphoreType.DMA((n,)), pltpu.SemaphoreType.DMA((n,))],
    )(*gs)


def _row_block(rows, cols, itemsize):
    best = 16
    for rb in range(16, rows + 1, 16):
        if rows % rb == 0 and rb * cols * itemsize <= 2 * 1024 * 1024:
            best = rb
    assert rows % best == 0, (rows, cols)
    return best


def _comm_block(rows):
    return 1024 if rows % 1024 == 0 else rows


def sum_slots(buf, name):
    n, r, c = buf.shape
    rb = _comm_block(r)

    def body(b_ref, o_ref):
        acc = b_ref[0].astype(F32)
        for s in range(1, n):
            acc = acc + b_ref[s].astype(F32)
        o_ref[...] = acc

    return pl.pallas_call(
        body, name=name, grid=(r // rb,), in_specs=[pl.BlockSpec((n, rb, c), lambda i: (0, i, 0))],
        out_specs=pl.BlockSpec((rb, c), lambda i: (i, 0)), out_shape=jax.ShapeDtypeStruct((r, c), F32),
        compiler_params=_params(("parallel",)),
    )(buf)


def add_pair(a, b, out_dtype, name):
    shape = a.shape
    a2, b2 = a.reshape(-1, shape[-1]), b.reshape(-1, shape[-1])
    r, c = a2.shape
    rb = _comm_block(r)

    def body(a_ref, b_ref, o_ref):
        o_ref[...] = (a_ref[...].astype(F32) + b_ref[...].astype(F32)).astype(out_dtype)

    out = pl.pallas_call(
        body, name=name, grid=(r // rb,), in_specs=[pl.BlockSpec((rb, c), lambda i: (i, 0))] * 2,
        out_specs=pl.BlockSpec((rb, c), lambda i: (i, 0)), out_shape=jax.ShapeDtypeStruct((r, c), out_dtype),
        compiler_params=_params(("parallel",)),
    )(a2, b2)
    return out.reshape(shape)


def adamw(w, g, m, v):
    shape = w.shape
    cols = shape[-1]
    rows = math.prod(shape[:-1])
    if rows * cols <= 256 * 1024:
        rb, cb = rows, cols
    else:
        rb = max(r for r in range(8, 2049, 8) if rows % r == 0)
        cb = cols if rb * cols * 4 <= 2 * 1024 * 1024 else 256
    assert rows % rb == 0 and cols % cb == 0, shape

    def body(w_ref, g_ref, m_ref, v_ref, d_ref, nm_ref, nv_ref):
        g_ = g_ref[...]
        nm = ADAM_B1 * m_ref[...] + (1.0 - ADAM_B1) * g_
        nv = ADAM_B2 * v_ref[...] + (1.0 - ADAM_B2) * (g_ * g_)
        m_hat = nm / (1.0 - ADAM_B1 ** ADAM_STEP)
        v_hat = nv / (1.0 - ADAM_B2 ** ADAM_STEP)
        d_ref[...] = -ADAM_LR * (m_hat / (jnp.sqrt(v_hat) + ADAM_EPS) + ADAM_WD * w_ref[...])
        nm_ref[...] = nm
        nv_ref[...] = nv

    spec = pl.BlockSpec((rb, cb), lambda i, j: (i, j))
    outs = pl.pallas_call(
        body, name="adamw", grid=(rows // rb, cols // cb), in_specs=[spec] * 4, out_specs=[spec] * 3,
        out_shape=[jax.ShapeDtypeStruct((rows, cols), F32)] * 3, compiler_params=_params(("parallel", "parallel")),
    )(*(a.reshape(rows, cols) for a in (w, g, m, v)))
    return tuple(o.reshape(shape) for o in outs)


def _pack(arrays, dtype, row_multiple):
    flat = jnp.concatenate([a.astype(dtype).reshape(-1) for a in arrays])
    per = LANES * row_multiple
    total = -(-flat.shape[0] // per) * per
    return jnp.pad(flat, (0, total - flat.shape[0])).reshape(total // LANES, LANES)


def _unpack(buf, shapes):
    flat = buf.reshape(-1)
    out, off = [], 0
    for s in shapes:
        n = math.prod(s)
        out.append(flat[off:off + n].reshape(s))
        off += n
    return out


def _input_weights(blocks):
    c0, c1, c2, c3 = (blocks[..., k, :, :] for k in range(N_CHIPS))
    pad = lambda n: jnp.zeros(c0.shape[:-1] + (n,), blocks.dtype)
    w_br = jnp.concatenate([c1[..., 376:1400], c0[..., 0:896], pad(64), c0[..., 896:928], pad(32), c0[..., 928:], c1[..., 0:376]], axis=-1)
    return w_br, jnp.concatenate([c1[..., 1400:], c2, c3], axis=-1)


def _input_weights_inverse(dw_br, dw_gl):
    c0 = jnp.concatenate([dw_br[..., 1024:1920], dw_br[..., 1984:2016], dw_br[..., 2048:2952]], axis=-1)
    c1 = jnp.concatenate([dw_br[..., 2952:ZB], dw_br[..., 0:1024], dw_gl[..., 0:432]], axis=-1)
    return jnp.stack([c0, c1, dw_gl[..., 432:2264], dw_gl[..., 2264:]], axis=-3)


def _block_diag(pw):
    zeros = lambda n: jnp.zeros(pw.shape[:-3] + (64, n), pw.dtype)
    rows = [jnp.concatenate([zeros(64 * g), pw[..., g, :, :], zeros(64 * (3 - g))], axis=-1) for g in range(4)]
    return jnp.concatenate(rows, axis=-2)


def _block_diag_inverse(d):
    return jnp.stack([d[..., 64 * g:64 * (g + 1), 64 * g:64 * (g + 1)] for g in range(4)], axis=-3)


def _pad_rows(a, n):
    return jnp.pad(a, ((0, n - a.shape[0]), (0, 0)))


def _rope_tables(rows):
    inv = 1.0 / (ROPE_THETA ** (jnp.arange(0, QK_ROPE, 2, dtype=F32) / QK_ROPE))
    ang = jnp.arange(rows, dtype=F32)[:, None] * inv[None, :]
    cos, sin = jnp.cos(ang), jnp.sin(ang)
    one, zero = jnp.ones((rows, 1), F32), jnp.zeros((rows, 1), F32)
    rep = lambda a, n: jnp.broadcast_to(a, (rows, n))
    c = jnp.concatenate([rep(one, 64), cos, cos, rep(one, 32)], axis=1)
    s1 = jnp.concatenate([rep(zero, 64), -sin, rep(zero, 48)], axis=1)
    s2 = jnp.concatenate([rep(zero, 80), sin, rep(zero, 32)], axis=1)
    return jnp.concatenate([c, s1, s2], axis=1)


def _misc_block(parts):
    lead = parts["w_uq"].shape[:-2]
    pad_last = lambda a, n: jnp.pad(a, [(0, 0)] * (a.ndim - 1) + [(0, n - a.shape[-1])])
    uq = pad_last(parts["w_uq"].reshape(lead + (256, 2, QK_NOPE + QK_ROPE)), HEAD_PAD).reshape(lead + (256, 256))
    kv = parts["w_ukv"].reshape(lead + (128, 2, QK_NOPE + V_DIM))
    keys = pad_last(kv[..., :QK_NOPE], HEAD_PAD).reshape(lead + (128, 256))
    values = pad_last(kv[..., QK_NOPE:].reshape(lead + (128, 2 * V_DIM)), 256)
    wo = jnp.swapaxes(parts["w_o"].reshape(lead + (256, N_CHIPS, 256)), -3, -2).reshape(lead + (D_MODEL, 256))
    gap = jnp.zeros(lead + (M_UQ - M_SC - 256, 256), uq.dtype)
    return jnp.concatenate([parts["w_out_mla"], parts["w_out_pool"], parts["w_out_conf"], parts["w_out_sc"], gap, uq, keys, values, wo],
                           axis=-2)


def _misc_unblock(block):
    lead = block.shape[:-2]
    rows = lambda lo, n: block[..., lo:lo + n, :]
    uq = rows(M_UQ, 256).reshape(lead + (256, 2, HEAD_PAD))[..., :QK_NOPE + QK_ROPE].reshape(lead + (256, 2 * (QK_NOPE + QK_ROPE)))
    keys = rows(M_UKVK, 128).reshape(lead + (128, 2, HEAD_PAD))[..., :QK_NOPE]
    values = rows(M_UKVV, 128)[..., :2 * V_DIM].reshape(lead + (128, 2, V_DIM))
    wo = jnp.swapaxes(rows(M_WO, D_MODEL).reshape(lead + (N_CHIPS, 256, 256)), -3, -2).reshape(lead + (256, D_MODEL))
    return dict(w_out_mla=rows(M_MLA, 512), w_out_pool=rows(M_POOL, 256), w_out_conf=rows(M_CONF, 256), w_out_sc=rows(M_SC, 256), w_uq=uq,
                w_ukv=jnp.concatenate([keys, values], axis=-1).reshape(lead + (128, 256)), w_o=wo)


def _to_chip_blocks(name, a):
    if name == "w_o":
        return a.reshape(a.shape[:-2] + (N_CHIPS, a.shape[-2] // N_CHIPS, a.shape[-1]))
    return jnp.swapaxes(a.reshape(a.shape[:-1] + (N_CHIPS, a.shape[-1] // N_CHIPS)), -3, -2)


def _from_chip_blocks(name, b):
    if name == "w_o":
        return b.reshape(b.shape[:-3] + (N_CHIPS * b.shape[-2], b.shape[-1]))
    s = jnp.swapaxes(b, -3, -2)
    return s.reshape(s.shape[:-2] + (N_CHIPS * s.shape[-1],))


LARGE = ("w_in",) + MISC


def gather_small(shards):
    small = chip_exchange(_pack([shards[n] for n, _, _ in SHARDED_SMALL], F32, 8), True, "gather_small_ici")
    per_chip = [_unpack(small[k], [s for _, s, _ in SHARDED_SMALL]) for k in range(N_CHIPS)]
    return {name: jnp.concatenate([per_chip[k][idx] for k in range(N_CHIPS)], axis=axis) for idx, (name, _, axis) in enumerate(SHARDED_SMALL)}


class LocalWeights:
    def __init__(self, full):
        self.w_in = _to_chip_blocks("w_in", full["w_in"])
        self.misc = _misc_block({n: _to_chip_blocks(n, full[n]) for n in MISC}).astype(BF16)
        self.grads = [None] * DEPTH

    def layer(self, i):
        return self.w_in[i], (self.misc, i)

    def gather_with_attention(self, i):
        return None

    def gathered(self, dsts):
        pass

    def exchange_with_attention(self):
        return None

    def exchanged(self, dsts):
        pass

    def swap_with_postnorm(self):
        return None

    def swapped(self, rs):
        pass

    def put_grads(self, i, w_in, misc):
        self.grads[i] = (w_in, misc)

    def reduced(self):
        out = {n: _from_chip_blocks(n, b) for n, b in _misc_unblock(jnp.stack([m for _, m in self.grads])).items()}
        out["w_in"] = _from_chip_blocks("w_in", jnp.stack([w for w, _ in self.grads]))
        return out


class MeshWeights:
    def __init__(self, shards, c, chip):
        self.c, self.chip = c, chip
        self.srcs = [shards["w_in"].astype(BF16), _misc_block({n: shards[n] for n in MISC}).astype(BF16)]
        dsts = [lax.empty((DEPTH, N_CHIPS) + s.shape[1:], BF16) for s in self.srcs]
        self.dsts = gather_layer(self.srcs, dsts, 0, "gather_layer")
        self.landed = [lax.empty((DEPTH, N_CHIPS, s.shape[1] // 2, s.shape[2]), BF16) for s in self.srcs]
        self.pending = self.to_swap = None

    def layer(self, i):
        if i > 0:
            return self.dsts[0][i], (self.dsts[1], i)
        own = (jnp.arange(N_CHIPS) == self.chip)[:, None, None]
        w_in, misc = (jnp.where(own, s[0][None], d[0]) for s, d in zip(self.srcs, self.dsts))
        return w_in, (misc[None], 0)

    def gather_with_attention(self, i):
        return (self.srcs, self.dsts, i + 1) if i + 1 < DEPTH else None

    def gathered(self, dsts):
        if dsts:
            self.dsts = dsts

    def exchange_with_attention(self):
        return None if self.pending is None else (self.pending[0], self.landed, self.pending[1])

    def exchanged(self, dsts):
        if dsts:
            self.landed, self.pending = dsts, None

    def put_grads(self, i, w_in, misc):
        self.to_swap = ([w_in.astype(BF16), misc.astype(BF16)], i)

    def swap_with_postnorm(self):
        return None if self.to_swap is None else self.to_swap[0]

    def swapped(self, rs):
        if rs:
            ps, i = self.to_swap
            self.pending = ([add_row_half(p, r, self.c, "reduce_pair_%d" % a) for a, (p, r) in enumerate(zip(ps, rs))], i)
            self.to_swap = None

    def reduced(self):
        self.swapped(swap_row_halves(self.to_swap[0], "reduce_swap"))
        landed = exchange_layer(self.pending[0], self.landed, self.pending[1], "reduce_exchange")
        gs = [sum_row_halves(l, self.c, "reduce_sum_%d" % a) for a, l in enumerate(landed)]
        g_in, g_misc = share_row_halves(gs, "reduce_share")
        out = {"w_in": g_in}
        out.update(_misc_unblock(g_misc))
        return out


def reduce_small(grads, chip):
    names = [n for n, _ in REPLICATED] + [n for n, _, _ in SHARDED_SMALL]
    buf = _pack([grads[n] for n in names], F32, 8)
    chip_sum = add_pair(buf, sibling_swap(buf, "reduce_small_d2d"), F32, "reduce_small_pair")
    total = sum_slots(chip_exchange(chip_sum, True, "reduce_small_ici"), "reduce_small_sum")
    out = dict(zip(names, _unpack(total, [grads[n].shape for n in names])))
    for name, shape, axis in SHARDED_SMALL:
        out[name] = lax.dynamic_slice_in_dim(out[name], chip * shape[axis], shape[axis], axis)
    return out


def _prepare_small(w):
    row = lambda a: a[:, None, :]
    conf_vec = jnp.concatenate([row(w["conf_dw_b"]), row(w["conf_ln_g"]), row(w["conf_ln_b"]), jnp.zeros((DEPTH, 5, 256), F32)], axis=1)
    return dict(
        gpre=row(w["pre_norm_g"]), bias=row(w["gate_bias"]), pwbd=_block_diag(w["pool_w"]).astype(BF16), pscale=row(w["pool_scale"]),
        gq=row(w["q_norm_g"]), gkv=row(w["kv_norm_g"]), conf_w=jnp.pad(w["conf_dw_w"].astype(F32), ((0, 0), (0, 32 - CONF_K), (0, 0))),
        conf_vec=conf_vec, sc_w=jnp.pad(w["sc_dw_w"].astype(F32), ((0, 0), (0, 8 - SC_K), (0, 0))), gpost=row(w["post_norm_g"]))


def _prepare_layer(w_in_blocks, misc):
    w_br, w_gl = _input_weights(w_in_blocks)
    one = lambda a: a.astype(BF16)[None]
    return dict(w_br=one(w_br), w_gl=one(w_gl), misc=misc)


def local_step(x, target, w, large):
    seq = x.shape[0]
    length = N_META + seq
    rows = -(-length // ROW_TILE) * ROW_TILE
    bt = _big_tile(rows)
    hres = _pad_rows(jnp.concatenate([w["meta_tokens"].astype(F32), x], axis=0), rows)
    tgt = jnp.pad(target, ((N_META, rows - length), (0, 0)))
    rope = _rope_tables(rows)
    sw = _prepare_small(w)

    saved = []
    for i in range(DEPTH):
        lw = _prepare_layer(*large.layer(i))
        z_br, hb = prenorm_project(hres, sw["gpre"], lw["w_br"], i)
        z_gl = matmul(hb, lw["w_gl"], "nn", BF16, bt, 2048, D_MODEL, "project_gates", b_layer=0)
        ua, uc, ud, q, k, v = branches_fwd(z_br, rope, sw["pwbd"], sw["pscale"], sw["gq"], sw["gkv"], lw["misc"], sw["conf_w"],
                                           sw["conf_vec"], sw["sc_w"], i)
        o_att, lse, dsts = attention_fwd(q, k, v, large.gather_with_attention(i))
        large.gathered(dsts)
        ub, mb, o, hnew = merge_fwd(ua, o_att, uc, ud, z_br, z_gl, sw["bias"], lw["misc"], sw["gpost"], hres, i)
        saved.append(dict(lw=lw, hres=hres, hb=hb, z_br=z_br, z_gl=z_gl, ua=ua, ub=ub, uc=uc, ud=ud, q=q, k=k, v=v, o_att=o_att,
                          lse=lse, mb=mb, o=o))
        hres = hnew

    dh, total = loss_head(hres, tgt, seq)

    g = {n: [None] * DEPTH for n in ("gpre", "bias", "pwbd", "pscale", "gq", "gkv", "conf_w", "conf_vec", "sc_w", "gpost")}
    for i in reversed(range(DEPTH)):
        s = saved[i]
        lw = s["lw"]
        dm, dwo, g["gpost"][i], rs = postnorm_bwd(dh, s["o"], s["mb"], lw["misc"], sw["gpost"], i, large.swap_with_postnorm())
        large.swapped(rs)
        dz_br = lax.empty((rows, ZB), BF16)
        dua, do, duc, dud, dz_gl, dwout, g["bias"][i], dz_br, delta = merge_bwd(
            dm, s["ua"], s["ub"], s["uc"], s["ud"], s["z_gl"], sw["bias"], lw["misc"], s["o_att"], s["z_br"], dz_br, i)
        dz_br, g["pwbd"][i], g["pscale"][i], g["sc_w"][i] = pool_shortconv_bwd(s["z_br"], dua, dud, sw["pwbd"], sw["pscale"], sw["sc_w"],
                                                                              dz_br, i)
        dc, dz_br, g["conf_vec"][i] = conformer_bwd_tail(s["z_br"], duc, sw["conf_w"], sw["conf_vec"], dz_br, i)
        dz_br, g["conf_w"][i] = conformer_bwd_conv(s["z_br"], dc, sw["conf_w"], dz_br, i)
        dq, dk, dv, dsts = attention_bwd(s["q"], s["k"], s["v"], do, s["lse"], delta, large.exchange_with_attention())
        large.exchanged(dsts)
        dz_br, dwup, g["gq"][i], g["gkv"][i] = mla_prep_bwd(dq, dk, dv, s["z_br"], rope, sw["gq"], sw["gkv"], lw["misc"], dz_br, i)
        dw_br = matmul(s["hb"], dz_br, "tn", BF16, D_MODEL, ZB // 2, bt, "grad_w_branch")
        dw_gl = matmul(s["hb"], dz_gl, "tn", BF16, D_MODEL, 2048, bt, "grad_w_gates")
        dh_gl = matmul(dz_gl, lw["w_gl"], "nt", F32, bt, D_MODEL, 2048, "grad_h_gates", b_layer=0)
        dh, g["gpre"][i] = prenorm_bwd(dz_br, lw["w_br"], dh_gl, s["hres"], sw["gpre"], dh, i)
        gap = jnp.zeros((N_CHIPS, M_UQ - M_SC - 256, 256), F32)
        large.put_grads(i, _input_weights_inverse(dw_br, dw_gl), jnp.concatenate([dwout, gap, dwup, dwo], axis=1))

    g = {n: jnp.stack(parts) for n, parts in g.items()}
    grads = dict(
        meta_tokens=dh[:N_META], pre_norm_g=g["gpre"][:, 0], gate_bias=g["bias"][:, 0], pool_w=_block_diag_inverse(g["pwbd"]),
        pool_scale=g["pscale"][:, 0], q_norm_g=g["gq"][:, 0], kv_norm_g=g["gkv"][:, 0], conf_dw_w=g["conf_w"][:, :CONF_K],
        conf_dw_b=g["conf_vec"][:, 2], conf_ln_g=g["conf_vec"][:, 0], conf_ln_b=g["conf_vec"][:, 1], sc_dw_w=g["sc_w"][:, :SC_K],
        post_norm_g=g["gpost"][:, 0])
    return total[0, 0], dh[N_META:length], grads


def kernel(x, meta_tokens, pre_norm_g, w_in, gate_bias, pool_w, pool_scale, w_out_pool, q_norm_g, w_uq, kv_norm_g, w_ukv, w_out_mla, conf_dw_w, conf_dw_b, conf_ln_g, conf_ln_b, w_out_conf, sc_dw_w, w_out_sc, w_o, post_norm_g, loss_target, m_meta_tokens, m_pre_norm_g, m_w_in, m_gate_bias, m_pool_w, m_pool_scale, m_w_out_pool, m_q_norm_g, m_w_uq, m_kv_norm_g, m_w_ukv, m_w_out_mla, m_conf_dw_w, m_conf_dw_b, m_conf_ln_g, m_conf_ln_b, m_w_out_conf, m_sc_dw_w, m_w_out_sc, m_w_o, m_post_norm_g, v_meta_tokens, v_pre_norm_g, v_w_in, v_gate_bias, v_pool_w, v_pool_scale, v_w_out_pool, v_q_norm_g, v_w_uq, v_kv_norm_g, v_w_ukv, v_w_out_mla, v_conf_dw_w, v_conf_dw_b, v_conf_ln_g, v_conf_ln_b, v_w_out_conf, v_sc_dw_w, v_w_out_sc, v_w_o, v_post_norm_g):
    args = locals()
    weights = {n: args[n] for n in WEIGHT_ORDER}
    c = lax.axis_index("c")
    chip = 2 * lax.axis_index("x") + lax.axis_index("y")

    small = {n: weights[n] for n, _ in REPLICATED}
    small.update(gather_small(weights))
    large = MeshWeights(weights, c, chip)
    total, dx, grads = local_step(x[0], loss_target[0], small, large)
    loss = lax.psum(total * (0.5 / D_MODEL), ("x", "y", "c"))

    reduced = large.reduced()
    reduced.update(reduce_small(grads, chip))

    flip = lambda a: jnp.swapaxes(a, 1, 2)
    deltas, new_m, new_v = [], [], []
    for n in WEIGHT_ORDER:
        operands = (weights[n], reduced[n], args["m_" + n], args["v_" + n])
        if n == "w_in":
            operands = (flip(operands[0]), lax.optimization_barrier(flip(operands[1])), flip(operands[2]), flip(operands[3]))
            reduced[n] = flip(operands[1])
        d, nm, nv = adamw(*operands)
        if n == "w_in":
            d, nm, nv = flip(d), flip(nm), flip(nv)
        deltas.append(d)
        new_m.append(nm)
        new_v.append(nv)
    return (loss, dx[None], *[reduced[n] for n in WEIGHT_ORDER], *deltas, *new_m, *new_v)
```

```python
import functools
import math

import jax
import jax.numpy as jnp
from jax import lax
from jax.experimental import pallas as pl
from jax.experimental.pallas import tpu as pltpu

F32 = jnp.float32
BF16 = jnp.bfloat16

D_MODEL = 1024
DEPTH = 4
N_META = 16
EPS = 1e-6
HEADS = 8
QK_NOPE = 64
QK_ROPE = 32
V_DIM = 64
HEAD_PAD = 128
ROPE_THETA = 10000.0
Q_SCALE = (QK_NOPE + QK_ROPE) ** -0.5
CONF_K = 31
SC_K = 3
IN_W = 7328
N_CHIPS = 4

ZB = 3328
ZG = 4096
BG, C2, XV, SG, PV, PG, CQ, CKV, KR, MG, CA, CGT, CG = (0, 256, 512, 768, 1024, 1280, 1536, 1792, 1920, 2048, 2560, 2816, 3072)

KEY_GROUP = 4
ROW_TILE = 384
HALO = 32
LANES = 128
VMEM_LIMIT = 56 * 1024 * 1024

ADAM_LR = 0.001
ADAM_B1 = 0.9
ADAM_B2 = 0.999
ADAM_EPS = 1e-08
ADAM_WD = 0.01
ADAM_STEP = 10

MESH = pl.DeviceIdType.MESH
ANY = pl.BlockSpec(memory_space=pl.ANY)

MISC = ("w_out_mla", "w_out_pool", "w_out_conf", "w_out_sc", "w_uq", "w_ukv", "w_o")
M_MLA, M_POOL, M_CONF, M_SC, M_UQ, M_UKVK, M_UKVV, M_WO, MISC_ROWS = 0, 512, 768, 1024, 1536, 1792, 1920, 2048, 3072
SHARDED_SMALL = (
    ("meta_tokens", (N_META, 256), 1),
    ("conf_dw_w", (DEPTH, CONF_K, 64), 2),
    ("sc_dw_w", (DEPTH, SC_K, 64), 2),
)
REPLICATED = (
    ("pre_norm_g", (DEPTH, D_MODEL)),
    ("gate_bias", (DEPTH, 4 * D_MODEL)),
    ("pool_w", (DEPTH, 4, 64, 64)),
    ("pool_scale", (DEPTH, 256)),
    ("q_norm_g", (DEPTH, 256)),
    ("kv_norm_g", (DEPTH, 128)),
    ("conf_dw_b", (DEPTH, 256)),
    ("conf_ln_g", (DEPTH, 256)),
    ("conf_ln_b", (DEPTH, 256)),
    ("post_norm_g", (DEPTH, D_MODEL)),
)
WEIGHT_ORDER = ("meta_tokens", "pre_norm_g", "w_in", "gate_bias", "pool_w", "pool_scale", "w_out_pool", "q_norm_g", "w_uq",
                "kv_norm_g", "w_ukv", "w_out_mla", "conf_dw_w", "conf_dw_b", "conf_ln_g", "conf_ln_b", "w_out_conf", "sc_dw_w",
                "w_out_sc", "w_o", "post_norm_g")


def _dot(a, b):
    return lax.dot_general(a, b, (((1,), (0,)), ((), ())), preferred_element_type=F32)


def _dot_nt(a, b):
    return lax.dot_general(a, b, (((1,), (1,)), ((), ())), preferred_element_type=F32)


def _dot_tn(a, b):
    return lax.dot_general(a, b, (((0,), (0,)), ((), ())), preferred_element_type=F32)


def _sigmoid(x):
    return jax.nn.sigmoid(x)


def _silu(x):
    return x * _sigmoid(x)


def _silu_grad(x):
    s = _sigmoid(x)
    return s * (1.0 + x * (1.0 - s))


def _rms(x, g):
    return x * lax.rsqrt(jnp.mean(x * x, axis=-1, keepdims=True) + EPS) * g


def _sh(x, d):
    return x if d == 0 else pltpu.roll(x, d, 0)


def _ash(x, d):
    return x if d == 0 else pltpu.roll(x, x.shape[0] - d, 0)


def _lanes8(t):
    return jnp.concatenate([t] * HEADS, axis=1)


def _pool_window_sums(v, shift):
    a2 = v + shift(v, 1)
    a4 = a2 + shift(a2, 2)
    a8 = a4 + shift(a4, 4)
    a16 = a8 + shift(a8, 8)
    lane = lax.broadcasted_iota(jnp.int32, v.shape, 1)
    return jnp.where(lane < 64, a2, jnp.where(lane < 128, a4, jnp.where(lane < 192, a8, a16)))


def _pool_counts(first_row, rows):
    pos = first_row + lax.broadcasted_iota(jnp.int32, (rows, 256), 0)
    lane = lax.broadcasted_iota(jnp.int32, (rows, 256), 1)
    width = jnp.where(lane < 64, 2, jnp.where(lane < 128, 4, jnp.where(lane < 192, 8, 16)))
    return jnp.maximum(jnp.minimum(pos + 1, width), 1).astype(F32)


def _params(sem=None):
    return pltpu.CompilerParams(dimension_semantics=sem, vmem_limit_bytes=VMEM_LIMIT)


def _tile_specs(t, n_halo_blocks, li=0):
    per = t // HALO

    def layer(shape, idx=li):
        return pl.BlockSpec((None,) + tuple(shape), lambda i: (idx,) + (0,) * len(shape))

    def cur(c, cb=0):
        return pl.BlockSpec((t, c), lambda i: (i, cb))

    def prev(c, cb=0):
        return pl.BlockSpec((HALO, c), lambda i: (jnp.maximum(i * per - 1, 0), cb))

    def nxt(c, cb=0):
        return pl.BlockSpec((HALO, c), lambda i: (jnp.minimum((i + 1) * per, n_halo_blocks - 1), cb))

    def full(shape):
        return pl.BlockSpec(shape, lambda i: (0,) * len(shape))

    return cur, prev, nxt, full, layer


def _big_tile(rows):
    return rows // 3 if rows % (3 * LANES) == 0 else ROW_TILE


def matmul(a, b, mode, out_dtype, tm, tn, tk, name, b_layer=None):
    bs = b.shape if b_layer is None else b.shape[1:]
    lead = () if b_layer is None else (None,)
    pick = (lambda *ix: ix) if b_layer is None else (lambda *ix: (b_layer,) + ix)
    if mode == "nn":
        (m, k), n = a.shape, bs[1]
        a_spec = pl.BlockSpec((tm, tk), lambda i, j, kk: (i, kk))
        b_spec = pl.BlockSpec(lead + (tk, tn), lambda i, j, kk: pick(kk, j))
        dot = _dot
    elif mode == "nt":
        (m, k), n = a.shape, bs[0]
        a_spec = pl.BlockSpec((tm, tk), lambda i, j, kk: (i, kk))
        b_spec = pl.BlockSpec(lead + (tn, tk), lambda i, j, kk: pick(j, kk))
        dot = _dot_nt
    else:
        (k, m), n = a.shape, bs[1]
        a_spec = pl.BlockSpec((tk, tm), lambda i, j, kk: (kk, i))
        b_spec = pl.BlockSpec(lead + (tk, tn), lambda i, j, kk: pick(kk, j))
        dot = _dot_tn
    assert m % tm == 0 and n % tn == 0 and k % tk == 0, (a.shape, bs, tm, tn, tk)
    nk = k // tk

    def body(a_ref, b_ref, o_ref, *acc):
        if nk == 1:
            o_ref[...] = dot(a_ref[...], b_ref[...]).astype(out_dtype)
            return
        (acc_ref,) = acc
        kk = pl.program_id(2)

        @pl.when(kk == 0)
        def _():
            acc_ref[...] = dot(a_ref[...], b_ref[...])

        @pl.when((kk > 0) & (kk < nk - 1))
        def _():
            acc_ref[...] += dot(a_ref[...], b_ref[...])

        @pl.when(kk == nk - 1)
        def _():
            o_ref[...] = (acc_ref[...] + dot(a_ref[...], b_ref[...])).astype(out_dtype)

    return pl.pallas_call(
        body, name=name, grid=(m // tm, n // tn, nk), in_specs=[a_spec, b_spec],
        out_specs=pl.BlockSpec((tm, tn), lambda i, j, kk: (i, j)), out_shape=jax.ShapeDtypeStruct((m, n), out_dtype),
        scratch_shapes=[pltpu.VMEM((tm, tn), F32)] if nk > 1 else [], compiler_params=_params(("parallel", "parallel", "arbitrary")),
    )(a, b)


def prenorm_project(hres, g, w, li):
    rows, d = hres.shape
    n = w.shape[2]
    tm, tn = _big_tile(rows), n // 2

    def body(x_ref, g_ref, w_ref, z_ref, hb_ref):
        @pl.when(pl.program_id(1) == 0)
        def _():
            hb_ref[...] = _rms(x_ref[...], g_ref[...]).astype(BF16)

        z_ref[...] = _dot(hb_ref[...], w_ref[...]).astype(BF16)

    return pl.pallas_call(
        body, name="prenorm_project", grid=(rows // tm, n // tn),
        in_specs=[pl.BlockSpec((tm, d), lambda i, j: (i, 0)), pl.BlockSpec((None, 1, d), lambda i, j: (li, 0, 0)),
                  pl.BlockSpec((None, d, tn), lambda i, j: (0, 0, j))],
        out_specs=[pl.BlockSpec((tm, tn), lambda i, j: (i, j)), pl.BlockSpec((tm, d), lambda i, j: (i, 0))],
        out_shape=[jax.ShapeDtypeStruct((rows, n), BF16), jax.ShapeDtypeStruct((rows, d), BF16)],
        compiler_params=_params(("parallel", "arbitrary")),
    )(hres, g, w)


def _rope(q, c, s1, s2, width):
    return q * c + pltpu.roll(q, width - 16, 1) * s1 + pltpu.roll(q, 16, 1) * s2


def _rope_transposed(dq, c, s1, s2, width):
    return dq * c + pltpu.roll(dq * s1, 16, 1) + pltpu.roll(dq * s2, width - 16, 1)


def _conf_conv(g1, w_ref):
    acc = jnp.zeros_like(g1)
    for k in range(CONF_K):
        acc = acc + w_ref[k:k + 1, :] * _sh(g1, CONF_K - 1 - k)
    return acc


def _conf_tail(c, cg, lg, lb):
    mu = jnp.mean(c, axis=-1, keepdims=True)
    xc = c - mu
    var = jnp.mean(xc * xc, axis=-1, keepdims=True)
    n = xc * lax.rsqrt(var + EPS) * lg + lb
    return _silu(n) * _silu(cg)


def _misc_spec(misc, row0, rows):
    assert row0 % rows == 0
    return pl.BlockSpec((None, N_CHIPS, rows, 256), lambda i: (misc[1], 0, row0 // rows, 0))


def _chip_columns(x, w_ref, row0, rows, lanes=256):
    return jnp.concatenate([_dot(x, w_ref[k, row0:row0 + rows, 0:lanes]) for k in range(N_CHIPS)], axis=1)


def branches_fwd(z_br, rope, pwbd, pscale, gq, gkv, misc, conf_w, conf_vec, sc_w, li):
    rows = z_br.shape[0]
    t = ROW_TILE
    cur, prev, _, _, layer = _tile_specs(t, rows // HALO, li)

    def body(zc_ref, zp_ref, rope_ref, pw_ref, ps_ref, gq_ref, gkv_ref, up_ref, cw_ref, cv_ref, sw_ref,
             ua_ref, uc_ref, ud_ref, q_ref, k_ref, v_ref):
        i = pl.program_id(0)
        zp = jnp.where(i == 0, jnp.zeros(zp_ref.shape, zp_ref.dtype), zp_ref[...])

        def ext(lo, w=256):
            return jnp.concatenate([zp[:, lo:lo + w], zc_ref[:, lo:lo + w]], axis=0).astype(F32)

        def col(lo, w=256):
            return zc_ref[:, lo:lo + w].astype(F32)

        v = ext(PV)
        p = (_pool_window_sums(v, _sh) / _pool_counts(i * t - HALO, t + HALO) - v)[HALO:]
        ya = _dot(p.astype(BF16), pw_ref[...]) * ps_ref[...]
        ua_ref[...] = (ya * _silu(col(PG))).astype(BF16)

        g1 = ext(CA) * _sigmoid(ext(CGT))
        c = _conf_conv(g1, cw_ref)[HALO:] + cv_ref[0:1, :]
        uc_ref[...] = _conf_tail(c, col(CG), cv_ref[1:2, :], cv_ref[2:3, :]).astype(BF16)

        e = ext(C2) * ext(XV)
        f = jnp.zeros_like(e)
        for k in range(SC_K):
            f = f + sw_ref[k:k + 1, :] * _sh(e, SC_K - 1 - k)
        ud_ref[...] = (col(BG) * f[HALO:] * _silu(col(SG))).astype(BF16)

        cth, s1, s2 = rope_ref[:, 0:128], rope_ref[:, 128:256], rope_ref[:, 256:384]
        qn = _rms(col(CQ), gq_ref[...]).astype(BF16)
        q = _chip_columns(qn, up_ref, 0, 256)
        w8 = HEADS * HEAD_PAD
        q_ref[...] = (_rope(q, _lanes8(cth), _lanes8(s1), _lanes8(s2), w8) * Q_SCALE).astype(BF16)
        kvn = _rms(col(CKV, 128), gkv_ref[...]).astype(BF16)
        kr = _rope(col(KR, 128), cth, s1, s2, HEAD_PAD)
        k_ref[...] = (_chip_columns(kvn, up_ref, M_UKVK - M_UQ, 128) + _lanes8(kr)).astype(BF16)
        v_ref[...] = _chip_columns(kvn, up_ref, M_UKVV - M_UQ, 128, 2 * V_DIM).astype(BF16)

    outs = [jax.ShapeDtypeStruct((rows, 256), BF16)] * 3 + [jax.ShapeDtypeStruct((rows, 1024), BF16)] * 2 + [
        jax.ShapeDtypeStruct((rows, 512), BF16)]
    return pl.pallas_call(
        body, name="branches_fwd", grid=(rows // t,),
        in_specs=[cur(ZB), prev(ZB), cur(384), layer((256, 256)), layer((1, 256)), layer((1, 256)), layer((1, 128)),
                  _misc_spec(misc, M_UQ, M_WO - M_UQ), layer((32, 256)), layer((8, 256)), layer((8, 256))],
        out_specs=[cur(256), cur(256), cur(256), cur(1024), cur(1024), cur(512)], out_shape=outs,
        compiler_params=_params(("parallel",)),
    )(z_br, z_br, rope, pwbd, pscale, gq, gkv, misc[0], conf_w, conf_vec, sc_w)


def _head_lane_mask(h):
    lane = lax.broadcasted_iota(jnp.int32, (1, 2 * V_DIM), 1)
    return (lane >= V_DIM * h) & (lane < V_DIM * (h + 1))


def attention_fwd(q, k, v, gather=None):
    rows = q.shape[0]
    tq = ROW_TILE
    nq = rows // tq
    n = 0 if gather is None else len(gather[0])

    def body(*refs):
        if n:
            start, finish = _gather_ops(refs[3:3 + n], refs[5 + 2 * n:5 + 3 * n], refs[5 + 3 * n:], gather[2], True)
            pl.when((pl.program_id(0) == 0) & (pl.program_id(1) == 0))(start)
        compute(*refs[:3], *refs[3 + 2 * n:5 + 2 * n])
        if n:
            pl.when((pl.program_id(0) == HEADS // 2 - 1) & (pl.program_id(1) == nq - 1))(finish)

    def compute(q_ref, k_ref, v_ref, o_ref, lse_ref):
        i = pl.program_id(1)

        def head_step(h, tile, n_tiles, carry, masked):
            m, l, acc = carry
            width = n_tiles * tq
            r0 = pl.multiple_of(tile * tq, tq)
            kh = k_ref[pl.ds(r0, width), HEAD_PAD * h:HEAD_PAD * (h + 1)]
            vh = jnp.where(_head_lane_mask(h), v_ref[pl.ds(r0, width), :], jnp.zeros((), BF16))
            s = _dot_nt(q_ref[:, HEAD_PAD * h:HEAD_PAD * (h + 1)], kh)
            if masked:
                row = lax.broadcasted_iota(jnp.int32, (tq, width), 0)
                colm = lax.broadcasted_iota(jnp.int32, (tq, width), 1)
                s = jnp.where(colm <= row + (width - tq), s, -1e30)
            m2 = jnp.maximum(m, jnp.max(s, axis=-1, keepdims=True))
            alpha = jnp.exp(m - m2)
            pr = jnp.exp(s - m2)
            return m2, alpha * l + jnp.sum(pr, axis=-1, keepdims=True), alpha * acc + _dot(pr.astype(BF16), vh)

        def step(tile, n_tiles, carry, masked):
            return tuple(head_step(h, tile, n_tiles, carry[h], masked) for h in range(2))

        init = (jnp.full((tq, 1), -1e30, F32), jnp.zeros((tq, 1), F32), jnp.zeros((tq, 2 * V_DIM), F32))
        group = min(KEY_GROUP, nq)
        carry = lax.fori_loop(0, i // group, lambda t, cr: step(group * t, group, cr, False), (init, init))
        carry = lax.switch(i % group, [functools.partial(lambda cr, r: step(i - r, r + 1, cr, True), r=r) for r in range(group)], carry)
        out = jnp.zeros((tq, 2 * V_DIM), F32)
        for h, (m, l, acc) in enumerate(carry):
            out = out + acc / l
            lse_ref[h] = jnp.broadcast_to(m + jnp.log(l), (tq, LANES))
        o_ref[...] = out.astype(BF16)

    srcs, dsts = ([], []) if gather is None else (list(gather[0]), list(gather[1]))
    outs = pl.pallas_call(
        body, name="attention_fwd" if gather is None else "attention_fwd_gather", grid=(HEADS // 2, nq),
        in_specs=[pl.BlockSpec((tq, 2 * HEAD_PAD), lambda p, i: (i, p)), pl.BlockSpec((rows, 2 * HEAD_PAD), lambda p, i: (0, p)),
                  pl.BlockSpec((rows, 2 * V_DIM), lambda p, i: (0, p))] + [ANY] * (2 * n),
        out_specs=[pl.BlockSpec((tq, 2 * V_DIM), lambda p, i: (i, p)), pl.BlockSpec((2, tq, LANES), lambda p, i: (p, i, 0))] + [ANY] * n,
        out_shape=[jax.ShapeDtypeStruct((rows, HEADS * V_DIM), BF16), jax.ShapeDtypeStruct((HEADS, rows, LANES), F32)] + [
            jax.ShapeDtypeStruct(d.shape, d.dtype) for d in dsts],
        input_output_aliases={3 + n + a: 2 + a for a in range(n)}, scratch_shapes=GATHER_SEMS(n) if n else [],
        compiler_params=_params(("arbitrary", "arbitrary") if n else ("parallel", "parallel")),
    )(q, k, v, *srcs, *dsts)
    return outs[0], outs[1], list(outs[2:])


OUT_PROJECTIONS = ((M_POOL, 256), (M_MLA, 512), (M_CONF, 256), (M_SC, 256))


def _chunks(x, n=N_CHIPS, width=256):
    return [x[:, width * k:width * (k + 1)] for k in range(n)]


def merge_fwd(ua, o_att, uc, ud, z_br, z_gl, bias, misc, gpost, hres, li):
    rows = hres.shape[0]
    t = ROW_TILE
    cur, _, _, _, layer = _tile_specs(t, rows // HALO, li)
    d = D_MODEL

    def body(ua_ref, ob_ref, uc_ref, ud_ref, mg_ref, gl_ref, b_ref, wout_ref, wo_ref, gp_ref, h_ref, ub_ref, mb_ref, o_ref, hn_ref):
        ub = (ob_ref[...].astype(F32) * _silu(mg_ref[...].astype(F32))).astype(BF16)
        ub_ref[...] = ub
        m = jnp.zeros((t, d), F32)
        for idx, (u, (row0, n)) in enumerate(zip((ua_ref[...], ub, uc_ref[...], ud_ref[...]), OUT_PROJECTIONS)):
            gate = _sigmoid(gl_ref[:, d * idx:d * (idx + 1)].astype(F32) + b_ref[:, d * idx:d * (idx + 1)])
            m = m + gate * _chip_columns(u, wout_ref, row0, n)
        mb = m.astype(BF16)
        mb_ref[...] = mb
        o = jnp.concatenate([sum(_dot(mk, wo_ref[k, 256 * j:256 * (j + 1), :]) for k, mk in enumerate(_chunks(mb)))
                             for j in range(N_CHIPS)], axis=1)
        o_ref[...] = o
        hn_ref[...] = h_ref[...] + _rms(o, gp_ref[...])

    return pl.pallas_call(
        body, name="merge_fwd", grid=(rows // t,),
        in_specs=[cur(256), cur(512), cur(256), cur(256), cur(512, MG // 512), cur(ZG), layer((1, ZG)), _misc_spec(misc, 0, 1280),
                  _misc_spec(misc, M_WO, D_MODEL), layer((1, d)), cur(d)],
        out_specs=[cur(512), cur(d), cur(d), cur(d)],
        out_shape=[jax.ShapeDtypeStruct((rows, 512), BF16), jax.ShapeDtypeStruct((rows, d), BF16), jax.ShapeDtypeStruct((rows, d), F32),
                   jax.ShapeDtypeStruct((rows, d), F32)],
        compiler_params=_params(("parallel",)),
    )(ua, o_att, uc, ud, z_br, z_gl, bias, misc[0], misc[0], gpost, hres)


def loss_head(hres, target, n_tokens):
    rows, d = hres.shape
    t = ROW_TILE
    cur, _, _, full, _ = _tile_specs(t, rows // HALO)
    n_steps = rows // t

    def body(h_ref, t_ref, dh_ref, tot_ref, acc_ref):
        i = pl.program_id(0)

        @pl.when(i == 0)
        def _():
            acc_ref[...] = jnp.zeros_like(acc_ref)

        r = i * t + lax.broadcasted_iota(jnp.int32, (t, 1), 0)
        diff = jnp.where((r >= N_META) & (r < N_META + n_tokens), h_ref[...] - t_ref[...], 0.0)
        dh_ref[...] = diff * (1.0 / d)
        acc_ref[...] += jnp.sum(diff * diff, axis=0, keepdims=True)

        @pl.when(i == n_steps - 1)
        def _():
            tot_ref[...] = jnp.broadcast_to(jnp.sum(acc_ref[...], axis=1, keepdims=True), (1, LANES))

    return pl.pallas_call(
        body, name="loss_head", grid=(n_steps,), in_specs=[cur(d), cur(d)], out_specs=[cur(d), full((1, LANES))],
        out_shape=[jax.ShapeDtypeStruct((rows, d), F32), jax.ShapeDtypeStruct((1, LANES), F32)],
        scratch_shapes=[pltpu.VMEM((1, d), F32)], compiler_params=_params(("arbitrary",)),
    )(hres, target)


def _accumulate(i, ref, value):
    @pl.when(i == 0)
    def _():
        ref[...] = value

    @pl.when(i > 0)
    def _():
        ref[...] += value


def postnorm_bwd(dh, o, mb, misc, gpost, li, swap=None):
    rows, d = dh.shape
    t = ROW_TILE
    cur, _, _, full, layer = _tile_specs(t, rows // HALO, li)
    n = 0 if swap is None else len(swap)
    steps = rows // t

    def body(*refs):
        if n:
            start, finish = _swap_ops(refs[5:5 + n], refs[8 + n:8 + 2 * n], refs[8 + 2 * n:])
            pl.when(pl.program_id(0) == 0)(start)
        compute(*refs[:5], *refs[5 + n:8 + n])
        if n:
            pl.when(pl.program_id(0) == steps - 1)(finish)

    def compute(dh_ref, o_ref, mb_ref, wo_ref, gp_ref, dm_ref, dwo_ref, dgp_ref):
        i = pl.program_id(0)
        _, vjp = jax.vjp(_rms, o_ref[...], gp_ref[...])
        do, dg = vjp(dh_ref[...])
        dob = do.astype(BF16)
        dm_ref[...] = jnp.concatenate([sum(_dot_nt(dj, wo_ref[k, 256 * j:256 * (j + 1), :]) for j, dj in enumerate(_chunks(dob)))
                                       for k in range(N_CHIPS)], axis=1)
        dwo = _dot_tn(mb_ref[...], dob)
        for k in range(N_CHIPS):
            _accumulate(i, dwo_ref.at[k], jnp.concatenate(_chunks(dwo[256 * k:256 * (k + 1), :]), axis=0))
        _accumulate(i, dgp_ref, dg)

    sent = [] if swap is None else list(swap)
    outs = pl.pallas_call(
        body, name="postnorm_bwd" if swap is None else "postnorm_bwd_swap", grid=(steps,),
        in_specs=[cur(d), cur(d), cur(d), _misc_spec(misc, M_WO, d), layer((1, d))] + [ANY] * n,
        out_specs=[cur(d), full((N_CHIPS, d, 256)), full((1, d))] + [ANY] * n,
        out_shape=[jax.ShapeDtypeStruct((rows, d), F32), jax.ShapeDtypeStruct((N_CHIPS, d, 256), F32), jax.ShapeDtypeStruct((1, d), F32)] + [
            jax.ShapeDtypeStruct((p.shape[0], p.shape[1] // 2, p.shape[2]), p.dtype) for p in sent],
        scratch_shapes=[pltpu.SemaphoreType.DMA((n,)), pltpu.SemaphoreType.DMA((n,))] if n else [],
        compiler_params=_params(("arbitrary",)),
    )(dh, o, mb, misc[0], gpost, *sent)
    return outs[0], outs[1], outs[2], list(outs[3:])


def merge_bwd(dm, ua, ub, uc, ud, z_gl, bias, misc, o_att, z_br, dz_buf, li):
    rows, d = dm.shape
    t = ROW_TILE
    cur, _, _, full, layer = _tile_specs(t, rows // HALO, li)

    def body(dm_ref, ua_ref, ub_ref, uc_ref, ud_ref, gl_ref, b_ref, w_ref, o_ref, mg_ref, _,
             dua_ref, do_ref, duc_ref, dud_ref, dgl_ref, dw_ref, db_ref, dmg_ref, delta_ref):
        i = pl.program_id(0)
        dm = dm_ref[...]
        groups = ((ua_ref, dua_ref), (ub_ref, None), (uc_ref, duc_ref), (ud_ref, dud_ref))
        for idx, ((u_ref, du_ref), (row0, n)) in enumerate(zip(groups, OUT_PROJECTIONS)):
            cols = slice(d * idx, d * (idx + 1))
            u = u_ref[...]
            gate = _sigmoid(gl_ref[:, cols].astype(F32) + b_ref[:, cols])
            dgl = dm * _chip_columns(u, w_ref, row0, n) * gate * (1.0 - gate)
            dgl_ref[:, cols] = dgl.astype(BF16)
            _accumulate(i, db_ref.at[:, cols], jnp.sum(dgl, axis=0, keepdims=True))
            dyb = (dm * gate).astype(BF16)
            du = sum(_dot_nt(dyk, w_ref[k, row0:row0 + n, :]) for k, dyk in enumerate(_chunks(dyb)))
            for k, dwk in enumerate(_chunks(_dot_tn(u, dyb))):
                _accumulate(i, dw_ref.at[k, row0:row0 + n, :], dwk)
            if du_ref is not None:
                du_ref[...] = du
                continue
            o, mg = o_ref[...].astype(F32), mg_ref[...].astype(F32)
            do = du * _silu(mg)
            do_ref[...] = do.astype(BF16)
            dmg_ref[...] = (du * o * _silu_grad(mg)).astype(BF16)
            prod = do * o
            lane = lax.broadcasted_iota(jnp.int32, (1, HEADS * V_DIM), 1)
            for h in range(HEADS):
                part = jnp.where((lane >= V_DIM * h) & (lane < V_DIM * (h + 1)), prod, 0.0)
                delta_ref[h] = jnp.broadcast_to(jnp.sum(part, axis=-1, keepdims=True), (t, LANES))

    return pl.pallas_call(
        body, name="merge_bwd", grid=(rows // t,),
        in_specs=[cur(d), cur(256), cur(512), cur(256), cur(256), cur(ZG), layer((1, ZG)), _misc_spec(misc, 0, 1280), cur(512),
                  cur(512, MG // 512), ANY],
        out_specs=[cur(256), cur(512), cur(256), cur(256), cur(ZG), full((N_CHIPS, 1280, 256)), full((1, ZG)), cur(512, MG // 512),
                   pl.BlockSpec((HEADS, t, LANES), lambda i: (0, i, 0))],
        out_shape=[jax.ShapeDtypeStruct((rows, 256), F32), jax.ShapeDtypeStruct((rows, 512), BF16), jax.ShapeDtypeStruct((rows, 256), F32),
                   jax.ShapeDtypeStruct((rows, 256), F32), jax.ShapeDtypeStruct((rows, ZG), BF16),
                   jax.ShapeDtypeStruct((N_CHIPS, 1280, 256), F32), jax.ShapeDtypeStruct((1, ZG), F32),
                   jax.ShapeDtypeStruct((rows, ZB), BF16), jax.ShapeDtypeStruct((HEADS, rows, LANES), F32)],
        input_output_aliases={10: 7}, compiler_params=_params(("arbitrary",)),
    )(dm, ua, ub, uc, ud, z_gl, bias, misc[0], o_att, z_br, dz_buf)


def pool_shortconv_bwd(z_br, dua, dud, pwbd, pscale, sc_w, dz_buf, li):
    rows = z_br.shape[0]
    t = ROW_TILE
    n_steps = rows // t
    cur, prev, nxt, full, layer = _tile_specs(t, rows // HALO, li)

    def body(zc_ref, zp_ref, zn_ref, dac_ref, dan_ref, ddc_ref, ddn_ref, pw_ref, ps_ref, sw_ref, _, dz_ref, dpw_ref, dps_ref, dw_ref):
        i = pl.program_id(0)
        last = i == n_steps - 1
        zp = jnp.where(i == 0, jnp.zeros(zp_ref.shape, zp_ref.dtype), zp_ref[...])
        zn = jnp.where(last, jnp.zeros(zn_ref.shape, zn_ref.dtype), zn_ref[...])

        def ext(lo):
            return jnp.concatenate([zp[:, lo:lo + 256], zc_ref[:, lo:lo + 256], zn[:, lo:lo + 256]], axis=0).astype(F32)

        def ext_grad(c_ref, n_ref):
            return jnp.concatenate([jnp.zeros((HALO, 256), F32), c_ref[...], jnp.where(last, jnp.zeros(n_ref.shape, F32), n_ref[...])], axis=0)

        mid = slice(HALO, HALO + t)

        bg, c2, xv, sg = ext(BG), ext(C2), ext(XV), ext(SG)
        du = ext_grad(ddc_ref, ddn_ref)
        e = c2 * xv
        shifted = [_sh(e, SC_K - 1 - k) for k in range(SC_K)]
        f = sum(sw_ref[k:k + 1, :] * shifted[k] for k in range(SC_K))
        gate = _silu(sg)
        df = du * gate * bg
        de = sum(sw_ref[k:k + 1, :] * _ash(df, SC_K - 1 - k) for k in range(SC_K))
        d_sc = [(du * gate * f)[mid], (de * xv)[mid], (de * c2)[mid], (du * bg * f * _silu_grad(sg))[mid]]
        dw = jnp.concatenate([jnp.sum((df * shifted[k])[mid], axis=0, keepdims=True) for k in range(SC_K)] + [
            jnp.zeros((8 - SC_K, 256), F32)], axis=0)
        _accumulate(i, dw_ref, dw)

        v, pg = ext(PV), ext(PG)
        cnt = _pool_counts(i * t - HALO, t + 2 * HALO)
        p = (_pool_window_sums(v, _sh) / cnt - v)[mid]
        dya = ext_grad(dac_ref, dan_ref) * _silu(pg)
        dypb = (dya * ps_ref[...]).astype(BF16)
        dp = _dot_nt(dypb, pw_ref[...])
        dv = (_pool_window_sums(dp / cnt, _ash) - dp)[mid]
        pb = p.astype(BF16)
        pw = _dot(pb, pw_ref[...])
        dpg = dac_ref[...] * pw * ps_ref[...] * _silu_grad(pg[mid])
        _accumulate(i, dpw_ref, _dot_tn(pb, dypb[mid]))
        _accumulate(i, dps_ref, jnp.sum(dya[mid] * pw, axis=0, keepdims=True))

        dz_ref[...] = jnp.concatenate(d_sc + [dv, dpg], axis=1).astype(BF16)

    return pl.pallas_call(
        body, name="pool_shortconv_bwd", grid=(n_steps,),
        in_specs=[cur(ZB), prev(ZB), nxt(ZB), cur(256), nxt(256), cur(256), nxt(256), layer((256, 256)), layer((1, 256)), layer((8, 256)),
                  ANY],
        out_specs=[cur(1536, BG // 1536), full((256, 256)), full((1, 256)), full((8, 256))],
        out_shape=[jax.ShapeDtypeStruct((rows, ZB), BF16), jax.ShapeDtypeStruct((256, 256), F32), jax.ShapeDtypeStruct((1, 256), F32),
                   jax.ShapeDtypeStruct((8, 256), F32)],
        input_output_aliases={10: 0}, compiler_params=_params(("arbitrary",)),
    )(z_br, z_br, z_br, dua, dua, dud, dud, pwbd, pscale, sc_w, dz_buf)


def conformer_bwd_tail(z_br, duc, conf_w, conf_vec, dz_buf, li):
    rows = z_br.shape[0]
    t = ROW_TILE
    cur, prev, _, full, layer = _tile_specs(t, rows // HALO, li)

    def body(zc_ref, zp_ref, du_ref, cw_ref, cv_ref, _, dc_ref, dcg_ref, dv_ref):
        i = pl.program_id(0)
        zp = jnp.where(i == 0, jnp.zeros(zp_ref.shape, zp_ref.dtype), zp_ref[...])

        def ext(lo):
            return jnp.concatenate([zp[:, lo:lo + 256], zc_ref[:, lo:lo + 256]], axis=0).astype(F32)

        g1 = ext(CA) * _sigmoid(ext(CGT))
        c = _conf_conv(g1, cw_ref)[HALO:] + cv_ref[0:1, :]
        _, vjp = jax.vjp(_conf_tail, c, zc_ref[:, CG:CG + 256].astype(F32), cv_ref[1:2, :], cv_ref[2:3, :])
        dc, dcg, dlg, dlb = vjp(du_ref[...])
        dc_ref[...] = dc
        dcg_ref[...] = dcg.astype(BF16)
        dvec = jnp.concatenate([dlg, dlb, jnp.sum(dc, axis=0, keepdims=True), jnp.zeros((5, 256), F32)], axis=0)
        _accumulate(i, dv_ref, dvec)

    return pl.pallas_call(
        body, name="conformer_bwd_tail", grid=(rows // t,), in_specs=[cur(ZB), prev(ZB), cur(256), layer((32, 256)), layer((8, 256)), ANY],
        out_specs=[cur(256), cur(256, CG // 256), full((8, 256))],
        out_shape=[jax.ShapeDtypeStruct((rows, 256), F32), jax.ShapeDtypeStruct((rows, ZB), BF16), jax.ShapeDtypeStruct((8, 256), F32)],
        input_output_aliases={5: 1}, compiler_params=_params(("arbitrary",)),
    )(z_br, z_br, duc, conf_w, conf_vec, dz_buf)


def conformer_bwd_conv(z_br, dc, conf_w, dz_buf, li):
    rows = z_br.shape[0]
    t = ROW_TILE
    n_steps = rows // t
    cur, prev, nxt, full, layer = _tile_specs(t, rows // HALO, li)

    def body(zc_ref, zp_ref, dc_ref, dn_ref, cw_ref, _, dz_ref, dw_ref):
        i = pl.program_id(0)
        zp = jnp.where(i == 0, jnp.zeros(zp_ref.shape, zp_ref.dtype), zp_ref[...])
        dcn = jnp.where(i == n_steps - 1, jnp.zeros(dn_ref.shape, dn_ref.dtype), dn_ref[...])

        def ext(lo):
            return jnp.concatenate([zp[:, lo:lo + 256], zc_ref[:, lo:lo + 256]], axis=0).astype(F32)

        a, gt = ext(CA), ext(CGT)
        sg = _sigmoid(gt)
        g1 = a * sg
        dc = dc_ref[...]
        dce = jnp.concatenate([dc, dcn], axis=0)
        dg1 = jnp.zeros_like(dce)
        dws = []
        for k in range(CONF_K):
            dg1 = dg1 + cw_ref[k:k + 1, :] * _ash(dce, CONF_K - 1 - k)
            dws.append(jnp.sum(dc * _sh(g1, CONF_K - 1 - k)[HALO:], axis=0, keepdims=True))
        dg1 = dg1[:t]
        ac, sc = a[HALO:], sg[HALO:]
        dz_ref[...] = jnp.concatenate([dg1 * sc, dg1 * ac * sc * (1.0 - sc)], axis=1).astype(BF16)
        _accumulate(i, dw_ref, jnp.concatenate(dws + [jnp.zeros((32 - CONF_K, 256), F32)], axis=0))

    return pl.pallas_call(
        body, name="conformer_bwd_conv", grid=(n_steps,), in_specs=[cur(ZB), prev(ZB), cur(256), nxt(256), layer((32, 256)), ANY],
        out_specs=[cur(512, CA // 512), full((32, 256))],
        out_shape=[jax.ShapeDtypeStruct((rows, ZB), BF16), jax.ShapeDtypeStruct((32, 256), F32)],
        input_output_aliases={5: 0}, compiler_params=_params(("arbitrary",)),
    )(z_br, z_br, dc, dc, conf_w, dz_buf)


def attention_bwd(q, k, v, do, lse, delta, exchange=None):
    rows = q.shape[0]
    tq = ROW_TILE
    nq = rows // tq
    n = 0 if exchange is None else len(exchange[0])

    def body(*refs):
        if n:
            start, finish = _exchange_ops(refs[6:6 + n], refs[9 + 2 * n:9 + 3 * n], refs[9 + 3 * n:], exchange[2])
            pl.when((pl.program_id(0) == 0) & (pl.program_id(1) == 0))(start)
        compute(*refs[:6], *refs[6 + 2 * n:9 + 2 * n])
        if n:
            pl.when((pl.program_id(0) == HEADS // 2 - 1) & (pl.program_id(1) == nq - 1))(finish)

    def compute(q_ref, k_ref, v_ref, do_ref, lse_ref, dl_ref, dq_ref, dk_ref, dv_ref):
        j = pl.program_id(1)

        @pl.when(j == 0)
        def _():
            dq_ref[...] = jnp.zeros_like(dq_ref)

        def head_step(h, tile, n_tiles, dk, dv, diagonal):
            lanes = slice(HEAD_PAD * h, HEAD_PAD * (h + 1))
            hm = _head_lane_mask(h)
            kh = k_ref[:, lanes]
            vh = jnp.where(hm, v_ref[...], jnp.zeros((), BF16))
            r0, width = pl.multiple_of(tile * tq, tq), n_tiles * tq
            qi = q_ref[pl.ds(r0, width), lanes]
            doi = jnp.where(hm, do_ref[pl.ds(r0, width), :], jnp.zeros((), BF16))
            s = _dot_nt(qi, kh)
            if diagonal:
                s = jnp.where(lax.broadcasted_iota(jnp.int32, (tq, tq), 1) <= lax.broadcasted_iota(jnp.int32, (tq, tq), 0), s, -1e30)
            pr = jnp.exp(s - lse_ref[h, pl.ds(r0, width), :][:, 0:1])
            dv = dv + _dot_tn(pr.astype(BF16), doi)
            dp = _dot_nt(doi, vh)
            ds = (pr * (dp - dl_ref[h, pl.ds(r0, width), :][:, 0:1])).astype(BF16)
            dq_ref[pl.ds(r0, width), lanes] += _dot(ds, kh)
            return dk + _dot_tn(ds, qi), dv

        def step(tile, n_tiles, carry, diagonal):
            dk0, dk1, dv = carry
            dk0, dv = head_step(0, tile, n_tiles, dk0, dv, diagonal)
            dk1, dv = head_step(1, tile, n_tiles, dk1, dv, diagonal)
            return dk0, dk1, dv

        zero = jnp.zeros((tq, HEAD_PAD), F32)
        carry = step(j, 1, (zero, zero, jnp.zeros((tq, 2 * V_DIM), F32)), True)
        odd = (nq - 1 - j) % 2
        carry = lax.cond(odd == 1, lambda cr: step(j + 1, 1, cr, False), lambda cr: cr, carry)
        dk0, dk1, dv = lax.fori_loop(0, (nq - 1 - j) // 2, lambda t, cr: step(j + 1 + odd + 2 * t, 2, cr, False), carry)
        dk_ref[:, 0:HEAD_PAD] = dk0
        dk_ref[:, HEAD_PAD:2 * HEAD_PAD] = dk1
        dv_ref[...] = dv

    srcs, dsts = ([], []) if exchange is None else (list(exchange[0]), list(exchange[1]))
    outs = pl.pallas_call(
        body, name="attention_bwd" if exchange is None else "attention_bwd_exchange", grid=(HEADS // 2, nq),
        in_specs=[pl.BlockSpec((rows, 2 * HEAD_PAD), lambda p, j: (0, p)), pl.BlockSpec((tq, 2 * HEAD_PAD), lambda p, j: (j, p)),
                  pl.BlockSpec((tq, 2 * V_DIM), lambda p, j: (j, p)), pl.BlockSpec((rows, 2 * V_DIM), lambda p, j: (0, p)),
                  pl.BlockSpec((2, rows, LANES), lambda p, j: (p, 0, 0)), pl.BlockSpec((2, rows, LANES), lambda p, j: (p, 0, 0))] + [
                      ANY] * (2 * n),
        out_specs=[pl.BlockSpec((rows, 2 * HEAD_PAD), lambda p, j: (0, p)), pl.BlockSpec((tq, 2 * HEAD_PAD), lambda p, j: (j, p)),
                   pl.BlockSpec((tq, 2 * V_DIM), lambda p, j: (j, p))] + [ANY] * n,
        out_shape=[jax.ShapeDtypeStruct((rows, HEADS * HEAD_PAD), F32), jax.ShapeDtypeStruct((rows, HEADS * HEAD_PAD), F32),
                   jax.ShapeDtypeStruct((rows, HEADS * V_DIM), F32)] + [jax.ShapeDtypeStruct(d.shape, d.dtype) for d in dsts],
        input_output_aliases={6 + n + a: 3 + a for a in range(n)}, scratch_shapes=EXCHANGE_SEMS(n) if n else [],
        compiler_params=_params(("arbitrary", "arbitrary") if n else ("parallel", "arbitrary")),
    )(q, k, v, do, lse, delta, *srcs, *dsts)
    return outs[0], outs[1], outs[2], list(outs[3:])


def mla_prep_bwd(dq, dk, dv, z_br, rope, gq, gkv, misc, dz_buf, li):
    rows = dq.shape[0]
    t = ROW_TILE
    cur, _, _, full, layer = _tile_specs(t, rows // HALO, li)
    w8 = HEADS * HEAD_PAD
    uq, keys, values = slice(0, 256), slice(M_UKVK - M_UQ, M_UKVV - M_UQ), slice(M_UKVV - M_UQ, M_WO - M_UQ)

    def body(dq_ref, dk_ref, dv_ref, z_ref, rope_ref, gq_ref, gkv_ref, up_ref, _, dz_ref, dup_ref, dgq_ref, dgkv_ref):
        i = pl.program_id(0)
        cth, s1, s2 = rope_ref[:, 0:128], rope_ref[:, 128:256], rope_ref[:, 256:384]
        dqb = _rope_transposed(dq_ref[...] * Q_SCALE, _lanes8(cth), _lanes8(s1), _lanes8(s2), w8).astype(BF16)
        dq_chunks = _chunks(dqb)
        cq = z_ref[:, 0:256].astype(F32)
        qn, vjp_q = jax.vjp(_rms, cq, gq_ref[...])
        dcq, dgq = vjp_q(sum(_dot_nt(dqk, up_ref[k, uq, :]) for k, dqk in enumerate(dq_chunks)))
        _accumulate(i, dgq_ref, dgq)

        dk = dk_ref[...]
        dkr = sum(dk[:, HEAD_PAD * h:HEAD_PAD * (h + 1)] for h in range(HEADS))
        dkr = _rope_transposed(dkr, cth, s1, s2, HEAD_PAD)
        lane = lax.broadcasted_iota(jnp.int32, (1, HEAD_PAD), 1)
        dkr = jnp.where((lane >= QK_NOPE) & (lane < QK_NOPE + QK_ROPE), dkr, 0.0)
        dkb, dvb = dk.astype(BF16), dv_ref[...].astype(BF16)
        dk_chunks, dv_chunks = _chunks(dkb), _chunks(dvb, width=2 * V_DIM)
        ckv = z_ref[:, 256:384].astype(F32)
        kvn, vjp_kv = jax.vjp(_rms, ckv, gkv_ref[...])
        dckv, dgkv = vjp_kv(sum(_dot_nt(dk_chunks[k], up_ref[k, keys, :]) + _dot_nt(dv_chunks[k], up_ref[k, values, 0:2 * V_DIM])
                                for k in range(N_CHIPS)))
        _accumulate(i, dgkv_ref, dgkv)
        dz_ref[...] = jnp.concatenate([dcq, dckv, dkr], axis=1).astype(BF16)
        qnb, kvnb = qn.astype(BF16), kvn.astype(BF16)
        d_uq, d_keys, d_values = _chunks(_dot_tn(qnb, dqb)), _chunks(_dot_tn(kvnb, dkb)), _chunks(_dot_tn(kvnb, dvb), width=2 * V_DIM)
        for k in range(N_CHIPS):
            padded = jnp.concatenate([d_values[k], jnp.zeros((128, 256 - 2 * V_DIM), F32)], axis=1)
            _accumulate(i, dup_ref.at[k], jnp.concatenate([d_uq[k], d_keys[k], padded], axis=0))

    return pl.pallas_call(
        body, name="mla_prep_bwd", grid=(rows // t,),
        in_specs=[cur(w8), cur(w8), cur(512), cur(512, CQ // 512), cur(384), layer((1, 256)), layer((1, 128)),
                  _misc_spec(misc, M_UQ, M_WO - M_UQ), ANY],
        out_specs=[cur(512, CQ // 512), full((N_CHIPS, M_WO - M_UQ, 256)), full((1, 256)), full((1, 128))],
        out_shape=[jax.ShapeDtypeStruct((rows, ZB), BF16), jax.ShapeDtypeStruct((N_CHIPS, M_WO - M_UQ, 256), F32),
                   jax.ShapeDtypeStruct((1, 256), F32), jax.ShapeDtypeStruct((1, 128), F32)],
        input_output_aliases={8: 0}, compiler_params=_params(("arbitrary",)),
    )(dq, dk, dv, z_br, rope, gq, gkv, misc[0], dz_buf)


def prenorm_bwd(dz_br, w_br, dh_gl, hres, gpre, dh_next, li):
    rows, d = hres.shape
    t = ROW_TILE
    cur, _, _, full, layer = _tile_specs(t, rows // HALO, li)

    def body(dz_ref, w_ref, dp_ref, x_ref, g_ref, dn_ref, dx_ref, dg_ref):
        i = pl.program_id(0)
        dh = _dot_nt(dz_ref[...], w_ref[...]) + dp_ref[...]
        _, vjp = jax.vjp(_rms, x_ref[...], g_ref[...])
        dx, dg = vjp(dh)
        dx_ref[...] = dx + dn_ref[...]
        _accumulate(i, dg_ref, dg)

    return pl.pallas_call(
        body, name="prenorm_bwd", grid=(rows // t,), in_specs=[cur(ZB), layer((d, ZB), 0), cur(d), cur(d), layer((1, d)), cur(d)],
        out_specs=[cur(d), full((1, d))], out_shape=[jax.ShapeDtypeStruct((rows, d), F32), jax.ShapeDtypeStruct((1, d), F32)],
        compiler_params=_params(("arbitrary",)),
    )(dz_br, w_br, dh_gl, hres, gpre, dh_next)


def _mesh_position():
    return lax.axis_index("x"), lax.axis_index("y"), lax.axis_index("c")


def chip_exchange(src, gather, name):
    block = src.shape if gather else src.shape[1:]

    def body(src_ref, dst_ref, send_sems, recv_sems, local_sem):
        x, y, c = _mesh_position()
        me = 2 * x + y
        peers = ((1 - x, y), (x, 1 - y), (1 - x, 1 - y))

        def part(k):
            return src_ref if gather else src_ref.at[k]

        def copy(j, slot):
            px, py = peers[j]
            return pltpu.make_async_remote_copy(src_ref=part(2 * px + py), dst_ref=dst_ref.at[slot], send_sem=send_sems.at[j],
                                                recv_sem=recv_sems.at[j], device_id=(px, py, c), device_id_type=MESH)

        local = pltpu.make_async_copy(part(me), dst_ref.at[me], local_sem)
        local.start()
        sends = [copy(j, me) for j in range(3)]
        for cp in sends:
            cp.start()
        for j, (px, py) in enumerate(peers):
            copy(j, 2 * px + py).wait_recv()
        for cp in sends:
            cp.wait_send()
        local.wait()

    return pl.pallas_call(
        body, name=name, in_specs=[pl.BlockSpec(memory_space=pl.ANY)], out_specs=pl.BlockSpec(memory_space=pl.ANY),
        out_shape=jax.ShapeDtypeStruct((N_CHIPS,) + tuple(block), src.dtype),
        scratch_shapes=[pltpu.SemaphoreType.DMA((3,)), pltpu.SemaphoreType.DMA((3,)), pltpu.SemaphoreType.DMA(())],
    )(src)


def sibling_swap(src, name):
    def body(src_ref, dst_ref, send_sem, recv_sem):
        x, y, c = _mesh_position()
        cp = pltpu.make_async_remote_copy(src_ref=src_ref, dst_ref=dst_ref, send_sem=send_sem, recv_sem=recv_sem,
                                          device_id=(x, y, 1 - c), device_id_type=MESH)
        cp.start()
        cp.wait()

    return pl.pallas_call(
        body, name=name, in_specs=[pl.BlockSpec(memory_space=pl.ANY)], out_specs=pl.BlockSpec(memory_space=pl.ANY),
        out_shape=jax.ShapeDtypeStruct(src.shape, src.dtype),
        scratch_shapes=[pltpu.SemaphoreType.DMA(()), pltpu.SemaphoreType.DMA(())],
    )(src)


def _comm_call(body, name, n_in, out_shapes, n_sems):
    return pl.pallas_call(
        body, name=name, in_specs=[ANY] * n_in, out_specs=[ANY] * len(out_shapes), out_shape=out_shapes,
        scratch_shapes=[pltpu.SemaphoreType.DMA((n,)) for n in n_sems])


def _row_halves(c, rows):
    half = rows // 2
    return pl.ds(pl.multiple_of(c * half, 16), half), pl.ds(pl.multiple_of((1 - c) * half, 16), half)


def _peers():
    x, y, c = _mesh_position()
    return x, y, c, 2 * x + y, ((1 - x, y), (x, 1 - y), (1 - x, 1 - y))


def _gather_ops(src, dst, sems, layer, own_copy):
    ici_send, ici_recv, d2d_send, d2d_recv, own_sems = sems
    n = len(src)

    def fetch(a, j, slot):
        x, y, c, _, peers = _peers()
        px, py = peers[j]
        mine, _ = _row_halves(c, src[a].shape[1])
        return pltpu.make_async_remote_copy(src_ref=src[a].at[layer, mine], dst_ref=dst[a].at[layer, slot, mine], send_sem=ici_send.at[3 * a + j],
                                            recv_sem=ici_recv.at[3 * a + j], device_id=(px, py, c), device_id_type=MESH)

    def forward(a, j, sibling_half):
        x, y, c, _, peers = _peers()
        px, py = peers[j]
        part = dst[a].at[layer, 2 * px + py, _row_halves(c, src[a].shape[1])[1 if sibling_half else 0]]
        return pltpu.make_async_remote_copy(src_ref=part, dst_ref=part, send_sem=d2d_send.at[3 * a + j], recv_sem=d2d_recv.at[3 * a + j],
                                            device_id=(x, y, 1 - c), device_id_type=MESH)

    def own(a):
        return pltpu.make_async_copy(src[a].at[layer], dst[a].at[layer, _peers()[3]], own_sems.at[a])

    def start():
        me = _peers()[3]
        for a in range(n):
            if own_copy:
                own(a).start()
            for j in range(3):
                fetch(a, j, me).start()

    def finish():
        peers = _peers()[4]
        for j, (px, py) in enumerate(peers):
            for a in range(n):
                fetch(a, j, 2 * px + py).wait_recv()
                forward(a, j, False).start()
        for j in range(3):
            for a in range(n):
                forward(a, j, True).wait_recv()
        for j in range(3):
            for a in range(n):
                fetch(a, j, 0).wait_send()
                forward(a, j, False).wait_send()
        if own_copy:
            for a in range(n):
                own(a).wait()

    return start, finish


def _exchange_ops(src, dst, sems, layer):
    send_sems, recv_sems, own_sems = sems
    n = len(src)

    def copy(a, j, slot):
        x, y, c, _, peers = _peers()
        px, py = peers[j]
        return pltpu.make_async_remote_copy(src_ref=src[a].at[2 * px + py], dst_ref=dst[a].at[layer, slot], send_sem=send_sems.at[3 * a + j],
                                            recv_sem=recv_sems.at[3 * a + j], device_id=(px, py, c), device_id_type=MESH)

    def own(a):
        me = _peers()[3]
        return pltpu.make_async_copy(src[a].at[me], dst[a].at[layer, me], own_sems.at[a])

    def start():
        me = _peers()[3]
        for a in range(n):
            own(a).start()
            for j in range(3):
                copy(a, j, me).start()

    def finish():
        peers = _peers()[4]
        for j, (px, py) in enumerate(peers):
            for a in range(n):
                copy(a, j, 2 * px + py).wait_recv()
        for j in range(3):
            for a in range(n):
                copy(a, j, 0).wait_send()
        for a in range(n):
            own(a).wait()

    return start, finish


GATHER_SEMS = lambda n: [pltpu.SemaphoreType.DMA((3 * n,))] * 4 + [pltpu.SemaphoreType.DMA((n,))]
EXCHANGE_SEMS = lambda n: [pltpu.SemaphoreType.DMA((3 * n,))] * 2 + [pltpu.SemaphoreType.DMA((n,))]


def gather_layer(srcs, dsts, layer, name):
    n = len(srcs)

    def body(*refs):
        start, finish = _gather_ops(refs[:n], refs[2 * n:3 * n], refs[3 * n:], layer, False)
        start()
        finish()

    return pl.pallas_call(
        body, name=name, in_specs=[ANY] * (2 * n), out_specs=[ANY] * n, out_shape=[jax.ShapeDtypeStruct(d.shape, d.dtype) for d in dsts],
        input_output_aliases={n + a: a for a in range(n)}, scratch_shapes=GATHER_SEMS(n),
    )(*srcs, *dsts)


def exchange_layer(ss, dsts, layer, name):
    n = len(ss)

    def body(*refs):
        start, finish = _exchange_ops(refs[:n], refs[2 * n:3 * n], refs[3 * n:], layer)
        start()
        finish()

    return pl.pallas_call(
        body, name=name, in_specs=[ANY] * (2 * n), out_specs=[ANY] * n, out_shape=[jax.ShapeDtypeStruct(d.shape, d.dtype) for d in dsts],
        input_output_aliases={n + a: a for a in range(n)}, scratch_shapes=EXCHANGE_SEMS(n),
    )(*ss, *dsts)


def _swap_ops(src, dst, sems):
    send_sems, recv_sems = sems

    def copy(a):
        x, y, c = _mesh_position()
        return pltpu.make_async_remote_copy(src_ref=src[a].at[:, _row_halves(c, src[a].shape[1])[1]], dst_ref=dst[a], send_sem=send_sems.at[a],
                                            recv_sem=recv_sems.at[a], device_id=(x, y, 1 - c), device_id_type=MESH)

    def start():
        for a in range(len(src)):
            copy(a).start()

    def finish():
        for a in range(len(src)):
            copy(a).wait()

    return start, finish


def swap_row_halves(ps, name):
    n = len(ps)

    def body(*refs):
        start, finish = _swap_ops(refs[:n], refs[n:2 * n], refs[2 * n:])
        start()
        finish()

    outs = [jax.ShapeDtypeStruct((p.shape[0], p.shape[1] // 2, p.shape[2]), p.dtype) for p in ps]
    return _comm_call(body, name, n, outs, (n, n))(*ps)


def add_row_halves(ps, rs, c, name):
    n = len(ps)

    def body(c_ref, *refs):
        for p_ref, r_ref, o_ref in zip(refs[:n], refs[n:2 * n], refs[2 * n:]):
            o_ref[...] = (p_ref[...].astype(F32) + r_ref[...].astype(F32)).astype(BF16)

    mine = [pl.BlockSpec((1,) + r.shape[1:], lambda k, c_ref: (k, c_ref[0], 0)) for r in rs]
    whole = [pl.BlockSpec((1,) + r.shape[1:], lambda k, c_ref: (k, 0, 0)) for r in rs]
    return pl.pallas_call(
        body, name=name, out_shape=[jax.ShapeDtypeStruct(r.shape, BF16) for r in rs],
        grid_spec=pltpu.PrefetchScalarGridSpec(num_scalar_prefetch=1, grid=(N_CHIPS,), in_specs=mine + whole, out_specs=whole),
        compiler_params=_params(("parallel",)),
    )(jnp.reshape(c, (1,)).astype(jnp.int32), *ps, *rs)


def sum_row_halves(l, c, name):
    layers, n, half, cols = l.shape
    rb = _row_block(half, cols, 4)
    steps = half // rb

    def body(c_ref, l_ref, o_ref):
        acc = l_ref[0, 0].astype(F32)
        for s in range(1, n):
            acc = acc + l_ref[0, s].astype(F32)
        o_ref[0] = acc

    return pl.pallas_call(
        body, name=name, out_shape=jax.ShapeDtypeStruct((layers, 2 * half, cols), F32),
        grid_spec=pltpu.PrefetchScalarGridSpec(
            num_scalar_prefetch=1, grid=(layers, steps), in_specs=[pl.BlockSpec((1, n, rb, cols), lambda a, i, c_ref: (a, 0, i, 0))],
            out_specs=pl.BlockSpec((1, rb, cols), lambda a, i, c_ref: (a, c_ref[0] * steps + i, 0))),
        compiler_params=_params(("parallel", "parallel")),
    )(jnp.reshape(c, (1,)).astype(jnp.int32), l)


def share_row_halves(gs, name):
    n = len(gs)

    def body(*refs):
        dst = refs[n:2 * n]
        send_sems, recv_sems = refs[2 * n:]
        x, y, c = _mesh_position()

        def copy(a, sibling_half):
            part = dst[a].at[:, _row_halves(c, dst[a].shape[1])[1 if sibling_half else 0]]
            return pltpu.make_async_remote_copy(src_ref=part, dst_ref=part, send_sem=send_sems.at[a], recv_sem=recv_sems.at[a],
                                                device_id=(x, y, 1 - c), device_id_type=MESH)

        for a in range(n):
            copy(a, False).start()
        for a in range(n):
            copy(a, True).wait_recv()
        for a in range(n):
            copy(a, False).wait_send()

    return pl.pallas_call(
        body, name=name, in_specs=[ANY] * n, out_specs=[ANY] * n, out_shape=[jax.ShapeDtypeStruct(g.shape, g.dtype) for g in gs],
        input_output_aliases={a: a for a in range(n)}, scratch_shapes=[pltpu.SemaphoreType.DMA((n,)), pltpu.SemaphoreType.DMA((n,))],
    )(*gs)


def _row_block(rows, cols, itemsize):
    best = 16
    for rb in range(16, rows + 1, 16):
        if rows % rb == 0 and rb * cols * itemsize <= 2 * 1024 * 1024:
            best = rb
    assert rows % best == 0, (rows, cols)
    return best


def _comm_block(rows):
    return 1024 if rows % 1024 == 0 else rows


def sum_slots(buf, name):
    n, r, c = buf.shape
    rb = _comm_block(r)

    def body(b_ref, o_ref):
        acc = b_ref[0].astype(F32)
        for s in range(1, n):
            acc = acc + b_ref[s].astype(F32)
        o_ref[...] = acc

    return pl.pallas_call(
        body, name=name, grid=(r // rb,), in_specs=[pl.BlockSpec((n, rb, c), lambda i: (0, i, 0))],
        out_specs=pl.BlockSpec((rb, c), lambda i: (i, 0)), out_shape=jax.ShapeDtypeStruct((r, c), F32),
        compiler_params=_params(("parallel",)),
    )(buf)


def add_pair(a, b, out_dtype, name):
    shape = a.shape
    a2, b2 = a.reshape(-1, shape[-1]), b.reshape(-1, shape[-1])
    r, c = a2.shape
    rb = _comm_block(r)

    def body(a_ref, b_ref, o_ref):
        o_ref[...] = (a_ref[...].astype(F32) + b_ref[...].astype(F32)).astype(out_dtype)

    out = pl.pallas_call(
        body, name=name, grid=(r // rb,), in_specs=[pl.BlockSpec((rb, c), lambda i: (i, 0))] * 2,
        out_specs=pl.BlockSpec((rb, c), lambda i: (i, 0)), out_shape=jax.ShapeDtypeStruct((r, c), out_dtype),
        compiler_params=_params(("parallel",)),
    )(a2, b2)
    return out.reshape(shape)


def adamw(w, g, m, v):
    shape = w.shape
    cols = shape[-1]
    rows = math.prod(shape[:-1])
    if rows * cols <= 256 * 1024:
        rb, cb = rows, cols
    else:
        rb = max(r for r in range(8, 2049, 8) if rows % r == 0)
        cb = cols if rb * cols * 4 <= 2 * 1024 * 1024 else 256
    assert rows % rb == 0 and cols % cb == 0, shape

    def body(w_ref, g_ref, m_ref, v_ref, d_ref, nm_ref, nv_ref):
        g_ = g_ref[...]
        nm = ADAM_B1 * m_ref[...] + (1.0 - ADAM_B1) * g_
        nv = ADAM_B2 * v_ref[...] + (1.0 - ADAM_B2) * (g_ * g_)
        m_hat = nm / (1.0 - ADAM_B1 ** ADAM_STEP)
        v_hat = nv / (1.0 - ADAM_B2 ** ADAM_STEP)
        d_ref[...] = -ADAM_LR * (m_hat / (jnp.sqrt(v_hat) + ADAM_EPS) + ADAM_WD * w_ref[...])
        nm_ref[...] = nm
        nv_ref[...] = nv

    spec = pl.BlockSpec((rb, cb), lambda i, j: (i, j))
    outs = pl.pallas_call(
        body, name="adamw", grid=(rows // rb, cols // cb), in_specs=[spec] * 4, out_specs=[spec] * 3,
        out_shape=[jax.ShapeDtypeStruct((rows, cols), F32)] * 3, compiler_params=_params(("parallel", "parallel")),
    )(*(a.reshape(rows, cols) for a in (w, g, m, v)))
    return tuple(o.reshape(shape) for o in outs)


def _pack(arrays, dtype, row_multiple):
    flat = jnp.concatenate([a.astype(dtype).reshape(-1) for a in arrays])
    per = LANES * row_multiple
    total = -(-flat.shape[0] // per) * per
    return jnp.pad(flat, (0, total - flat.shape[0])).reshape(total // LANES, LANES)


def _unpack(buf, shapes):
    flat = buf.reshape(-1)
    out, off = [], 0
    for s in shapes:
        n = math.prod(s)
        out.append(flat[off:off + n].reshape(s))
        off += n
    return out


def _input_weights(blocks):
    c0, c1, c2, c3 = (blocks[..., k, :, :] for k in range(N_CHIPS))
    pad = lambda n: jnp.zeros(c0.shape[:-1] + (n,), blocks.dtype)
    w_br = jnp.concatenate([c1[..., 376:1400], c0[..., 0:896], pad(64), c0[..., 896:928], pad(32), c0[..., 928:], c1[..., 0:376]], axis=-1)
    return w_br, jnp.concatenate([c1[..., 1400:], c2, c3], axis=-1)


def _input_weights_inverse(dw_br, dw_gl):
    c0 = jnp.concatenate([dw_br[..., 1024:1920], dw_br[..., 1984:2016], dw_br[..., 2048:2952]], axis=-1)
    c1 = jnp.concatenate([dw_br[..., 2952:ZB], dw_br[..., 0:1024], dw_gl[..., 0:432]], axis=-1)
    return jnp.stack([c0, c1, dw_gl[..., 432:2264], dw_gl[..., 2264:]], axis=-3)


def _block_diag(pw):
    zeros = lambda n: jnp.zeros(pw.shape[:-3] + (64, n), pw.dtype)
    rows = [jnp.concatenate([zeros(64 * g), pw[..., g, :, :], zeros(64 * (3 - g))], axis=-1) for g in range(4)]
    return jnp.concatenate(rows, axis=-2)


def _block_diag_inverse(d):
    return jnp.stack([d[..., 64 * g:64 * (g + 1), 64 * g:64 * (g + 1)] for g in range(4)], axis=-3)


def _pad_rows(a, n):
    return jnp.pad(a, ((0, n - a.shape[0]), (0, 0)))


def _rope_tables(rows):
    inv = 1.0 / (ROPE_THETA ** (jnp.arange(0, QK_ROPE, 2, dtype=F32) / QK_ROPE))
    ang = jnp.arange(rows, dtype=F32)[:, None] * inv[None, :]
    cos, sin = jnp.cos(ang), jnp.sin(ang)
    one, zero = jnp.ones((rows, 1), F32), jnp.zeros((rows, 1), F32)
    rep = lambda a, n: jnp.broadcast_to(a, (rows, n))
    c = jnp.concatenate([rep(one, 64), cos, cos, rep(one, 32)], axis=1)
    s1 = jnp.concatenate([rep(zero, 64), -sin, rep(zero, 48)], axis=1)
    s2 = jnp.concatenate([rep(zero, 80), sin, rep(zero, 32)], axis=1)
    return jnp.concatenate([c, s1, s2], axis=1)


def _misc_block(parts):
    lead = parts["w_uq"].shape[:-2]
    pad_last = lambda a, n: jnp.pad(a, [(0, 0)] * (a.ndim - 1) + [(0, n - a.shape[-1])])
    uq = pad_last(parts["w_uq"].reshape(lead + (256, 2, QK_NOPE + QK_ROPE)), HEAD_PAD).reshape(lead + (256, 256))
    kv = parts["w_ukv"].reshape(lead + (128, 2, QK_NOPE + V_DIM))
    keys = pad_last(kv[..., :QK_NOPE], HEAD_PAD).reshape(lead + (128, 256))
    values = pad_last(kv[..., QK_NOPE:].reshape(lead + (128, 2 * V_DIM)), 256)
    wo = jnp.swapaxes(parts["w_o"].reshape(lead + (256, N_CHIPS, 256)), -3, -2).reshape(lead + (D_MODEL, 256))
    gap = jnp.zeros(lead + (M_UQ - M_SC - 256, 256), uq.dtype)
    return jnp.concatenate([parts["w_out_mla"], parts["w_out_pool"], parts["w_out_conf"], parts["w_out_sc"], gap, uq, keys, values, wo],
                           axis=-2)


def _misc_unblock(block):
    lead = block.shape[:-2]
    rows = lambda lo, n: block[..., lo:lo + n, :]
    uq = rows(M_UQ, 256).reshape(lead + (256, 2, HEAD_PAD))[..., :QK_NOPE + QK_ROPE].reshape(lead + (256, 2 * (QK_NOPE + QK_ROPE)))
    keys = rows(M_UKVK, 128).reshape(lead + (128, 2, HEAD_PAD))[..., :QK_NOPE]
    values = rows(M_UKVV, 128)[..., :2 * V_DIM].reshape(lead + (128, 2, V_DIM))
    wo = jnp.swapaxes(rows(M_WO, D_MODEL).reshape(lead + (N_CHIPS, 256, 256)), -3, -2).reshape(lead + (256, D_MODEL))
    return dict(w_out_mla=rows(M_MLA, 512), w_out_pool=rows(M_POOL, 256), w_out_conf=rows(M_CONF, 256), w_out_sc=rows(M_SC, 256), w_uq=uq,
                w_ukv=jnp.concatenate([keys, values], axis=-1).reshape(lead + (128, 256)), w_o=wo)


def _to_chip_blocks(name, a):
    if name == "w_o":
        return a.reshape(a.shape[:-2] + (N_CHIPS, a.shape[-2] // N_CHIPS, a.shape[-1]))
    return jnp.swapaxes(a.reshape(a.shape[:-1] + (N_CHIPS, a.shape[-1] // N_CHIPS)), -3, -2)


def _from_chip_blocks(name, b):
    if name == "w_o":
        return b.reshape(b.shape[:-3] + (N_CHIPS * b.shape[-2], b.shape[-1]))
    s = jnp.swapaxes(b, -3, -2)
    return s.reshape(s.shape[:-2] + (N_CHIPS * s.shape[-1],))


LARGE = ("w_in",) + MISC


def gather_small(shards):
    small = chip_exchange(_pack([shards[n] for n, _, _ in SHARDED_SMALL], F32, 8), True, "gather_small_ici")
    per_chip = [_unpack(small[k], [s for _, s, _ in SHARDED_SMALL]) for k in range(N_CHIPS)]
    return {name: jnp.concatenate([per_chip[k][idx] for k in range(N_CHIPS)], axis=axis) for idx, (name, _, axis) in enumerate(SHARDED_SMALL)}


class LocalWeights:
    def __init__(self, full):
        self.w_in = _to_chip_blocks("w_in", full["w_in"])
        self.misc = _misc_block({n: _to_chip_blocks(n, full[n]) for n in MISC}).astype(BF16)
        self.grads = [None] * DEPTH

    def layer(self, i):
        return self.w_in[i], (self.misc, i)

    def gather_with_attention(self, i):
        return None

    def gathered(self, dsts):
        pass

    def exchange_with_attention(self):
        return None

    def exchanged(self, dsts):
        pass

    def swap_with_postnorm(self):
        return None

    def swapped(self, rs):
        pass

    def put_grads(self, i, w_in, misc):
        self.grads[i] = (w_in, misc)

    def reduced(self):
        out = {n: _from_chip_blocks(n, b) for n, b in _misc_unblock(jnp.stack([m for _, m in self.grads])).items()}
        out["w_in"] = _from_chip_blocks("w_in", jnp.stack([w for w, _ in self.grads]))
        return out


class MeshWeights:
    def __init__(self, shards, c, chip):
        self.c, self.chip = c, chip
        self.srcs = [shards["w_in"].astype(BF16), _misc_block({n: shards[n] for n in MISC}).astype(BF16)]
        dsts = [lax.empty((DEPTH, N_CHIPS) + s.shape[1:], BF16) for s in self.srcs]
        self.dsts = gather_layer(self.srcs, dsts, 0, "gather_layer")
        self.landed = [lax.empty((DEPTH, N_CHIPS, s.shape[1] // 2, s.shape[2]), BF16) for s in self.srcs]
        self.pending = self.to_swap = None

    def layer(self, i):
        if i > 0:
            return self.dsts[0][i], (self.dsts[1], i)
        own = (jnp.arange(N_CHIPS) == self.chip)[:, None, None]
        w_in, misc = (jnp.where(own, s[0][None], d[0]) for s, d in zip(self.srcs, self.dsts))
        return w_in, (misc[None], 0)

    def gather_with_attention(self, i):
        return (self.srcs, self.dsts, i + 1) if i + 1 < DEPTH else None

    def gathered(self, dsts):
        if dsts:
            self.dsts = dsts

    def exchange_with_attention(self):
        return None if self.pending is None else (self.pending[0], self.landed, self.pending[1])

    def exchanged(self, dsts):
        if dsts:
            self.landed, self.pending = dsts, None

    def put_grads(self, i, w_in, misc):
        self.to_swap = ([w_in.astype(BF16), misc.astype(BF16)], i)

    def swap_with_postnorm(self):
        return None if self.to_swap is None else self.to_swap[0]

    def swapped(self, rs):
        if rs:
            ps, i = self.to_swap
            self.pending = (add_row_halves(ps, rs, self.c, "reduce_pair"), i)
            self.to_swap = None

    def reduced(self):
        self.swapped(swap_row_halves(self.to_swap[0], "reduce_swap"))
        landed = exchange_layer(self.pending[0], self.landed, self.pending[1], "reduce_exchange")
        gs = [sum_row_halves(l, self.c, "reduce_sum_%d" % a) for a, l in enumerate(landed)]
        g_in, g_misc = share_row_halves(gs, "reduce_share")
        out = {"w_in": g_in}
        out.update(_misc_unblock(g_misc))
        return out


def reduce_small(grads, chip):
    names = [n for n, _ in REPLICATED] + [n for n, _, _ in SHARDED_SMALL]
    buf = _pack([grads[n] for n in names], F32, 8)
    chip_sum = add_pair(buf, sibling_swap(buf, "reduce_small_d2d"), F32, "reduce_small_pair")
    total = sum_slots(chip_exchange(chip_sum, True, "reduce_small_ici"), "reduce_small_sum")
    out = dict(zip(names, _unpack(total, [grads[n].shape for n in names])))
    for name, shape, axis in SHARDED_SMALL:
        out[name] = lax.dynamic_slice_in_dim(out[name], chip * shape[axis], shape[axis], axis)
    return out


def _prepare_small(w):
    row = lambda a: a[:, None, :]
    conf_vec = jnp.concatenate([row(w["conf_dw_b"]), row(w["conf_ln_g"]), row(w["conf_ln_b"]), jnp.zeros((DEPTH, 5, 256), F32)], axis=1)
    return dict(
        gpre=row(w["pre_norm_g"]), bias=row(w["gate_bias"]), pwbd=_block_diag(w["pool_w"]).astype(BF16), pscale=row(w["pool_scale"]),
        gq=row(w["q_norm_g"]), gkv=row(w["kv_norm_g"]), conf_w=jnp.pad(w["conf_dw_w"].astype(F32), ((0, 0), (0, 32 - CONF_K), (0, 0))),
        conf_vec=conf_vec, sc_w=jnp.pad(w["sc_dw_w"].astype(F32), ((0, 0), (0, 8 - SC_K), (0, 0))), gpost=row(w["post_norm_g"]))


def _prepare_layer(w_in_blocks, misc):
    w_br, w_gl = _input_weights(w_in_blocks)
    one = lambda a: a.astype(BF16)[None]
    return dict(w_br=one(w_br), w_gl=one(w_gl), misc=misc)


def local_step(x, target, w, large):
    seq = x.shape[0]
    length = N_META + seq
    rows = -(-length // ROW_TILE) * ROW_TILE
    bt = _big_tile(rows)
    hres = _pad_rows(jnp.concatenate([w["meta_tokens"].astype(F32), x], axis=0), rows)
    tgt = jnp.pad(target, ((N_META, rows - length), (0, 0)))
    rope = _rope_tables(rows)
    sw = _prepare_small(w)

    saved = []
    for i in range(DEPTH):
        lw = _prepare_layer(*large.layer(i))
        z_br, hb = prenorm_project(hres, sw["gpre"], lw["w_br"], i)
        z_gl = matmul(hb, lw["w_gl"], "nn", BF16, bt, 2048, D_MODEL, "project_gates", b_layer=0)
        ua, uc, ud, q, k, v = branches_fwd(z_br, rope, sw["pwbd"], sw["pscale"], sw["gq"], sw["gkv"], lw["misc"], sw["conf_w"],
                                           sw["conf_vec"], sw["sc_w"], i)
        o_att, lse, dsts = attention_fwd(q, k, v, large.gather_with_attention(i))
        large.gathered(dsts)
        ub, mb, o, hnew = merge_fwd(ua, o_att, uc, ud, z_br, z_gl, sw["bias"], lw["misc"], sw["gpost"], hres, i)
        saved.append(dict(lw=lw, hres=hres, hb=hb, z_br=z_br, z_gl=z_gl, ua=ua, ub=ub, uc=uc, ud=ud, q=q, k=k, v=v, o_att=o_att,
                          lse=lse, mb=mb, o=o))
        hres = hnew

    dh, total = loss_head(hres, tgt, seq)

    g = {n: [None] * DEPTH for n in ("gpre", "bias", "pwbd", "pscale", "gq", "gkv", "conf_w", "conf_vec", "sc_w", "gpost")}
    for i in reversed(range(DEPTH)):
        s = saved[i]
        lw = s["lw"]
        dm, dwo, g["gpost"][i], rs = postnorm_bwd(dh, s["o"], s["mb"], lw["misc"], sw["gpost"], i, large.swap_with_postnorm())
        large.swapped(rs)
        dz_br = lax.empty((rows, ZB), BF16)
        dua, do, duc, dud, dz_gl, dwout, g["bias"][i], dz_br, delta = merge_bwd(
            dm, s["ua"], s["ub"], s["uc"], s["ud"], s["z_gl"], sw["bias"], lw["misc"], s["o_att"], s["z_br"], dz_br, i)
        dz_br, g["pwbd"][i], g["pscale"][i], g["sc_w"][i] = pool_shortconv_bwd(s["z_br"], dua, dud, sw["pwbd"], sw["pscale"], sw["sc_w"],
                                                                              dz_br, i)
        dc, dz_br, g["conf_vec"][i] = conformer_bwd_tail(s["z_br"], duc, sw["conf_w"], sw["conf_vec"], dz_br, i)
        dz_br, g["conf_w"][i] = conformer_bwd_conv(s["z_br"], dc, sw["conf_w"], dz_br, i)
        dq, dk, dv, dsts = attention_bwd(s["q"], s["k"], s["v"], do, s["lse"], delta, large.exchange_with_attention())
        large.exchanged(dsts)
        dz_br, dwup, g["gq"][i], g["gkv"][i] = mla_prep_bwd(dq, dk, dv, s["z_br"], rope, sw["gq"], sw["gkv"], lw["misc"], dz_br, i)
        dw_br = matmul(s["hb"], dz_br, "tn", BF16, D_MODEL, ZB // 2, bt, "grad_w_branch")
        dw_gl = matmul(s["hb"], dz_gl, "tn", BF16, D_MODEL, 2048, bt, "grad_w_gates")
        dh_gl = matmul(dz_gl, lw["w_gl"], "nt", F32, bt, D_MODEL, 2048, "grad_h_gates", b_layer=0)
        dh, g["gpre"][i] = prenorm_bwd(dz_br, lw["w_br"], dh_gl, s["hres"], sw["gpre"], dh, i)
        gap = jnp.zeros((N_CHIPS, M_UQ - M_SC - 256, 256), F32)
        large.put_grads(i, _input_weights_inverse(dw_br, dw_gl), jnp.concatenate([dwout, gap, dwup, dwo], axis=1))

    g = {n: jnp.stack(parts) for n, parts in g.items()}
    grads = dict(
        meta_tokens=dh[:N_META], pre_norm_g=g["gpre"][:, 0], gate_bias=g["bias"][:, 0], pool_w=_block_diag_inverse(g["pwbd"]),
        pool_scale=g["pscale"][:, 0], q_norm_g=g["gq"][:, 0], kv_norm_g=g["gkv"][:, 0], conf_dw_w=g["conf_w"][:, :CONF_K],
        conf_dw_b=g["conf_vec"][:, 2], conf_ln_g=g["conf_vec"][:, 0], conf_ln_b=g["conf_vec"][:, 1], sc_dw_w=g["sc_w"][:, :SC_K],
        post_norm_g=g["gpost"][:, 0])
    return total[0, 0], dh[N_META:length], grads


def kernel(x, meta_tokens, pre_norm_g, w_in, gate_bias, pool_w, pool_scale, w_out_pool, q_norm_g, w_uq, kv_norm_g, w_ukv, w_out_mla, conf_dw_w, conf_dw_b, conf_ln_g, conf_ln_b, w_out_conf, sc_dw_w, w_out_sc, w_o, post_norm_g, loss_target, m_meta_tokens, m_pre_norm_g, m_w_in, m_gate_bias, m_pool_w, m_pool_scale, m_w_out_pool, m_q_norm_g, m_w_uq, m_kv_norm_g, m_w_ukv, m_w_out_mla, m_conf_dw_w, m_conf_dw_b, m_conf_ln_g, m_conf_ln_b, m_w_out_conf, m_sc_dw_w, m_w_out_sc, m_w_o, m_post_norm_g, v_meta_tokens, v_pre_norm_g, v_w_in, v_gate_bias, v_pool_w, v_pool_scale, v_w_out_pool, v_q_norm_g, v_w_uq, v_kv_norm_g, v_w_ukv, v_w_out_mla, v_conf_dw_w, v_conf_dw_b, v_conf_ln_g, v_conf_ln_b, v_w_out_conf, v_sc_dw_w, v_w_out_sc, v_w_o, v_post_norm_g):
    args = locals()
    weights = {n: args[n] for n in WEIGHT_ORDER}
    c = lax.axis_index("c")
    chip = 2 * lax.axis_index("x") + lax.axis_index("y")

    small = {n: weights[n] for n, _ in REPLICATED}
    small.update(gather_small(weights))
    large = MeshWeights(weights, c, chip)
    total, dx, grads = local_step(x[0], loss_target[0], small, large)
    loss = lax.psum(total * (0.5 / D_MODEL), ("x", "y", "c"))

    reduced = large.reduced()
    reduced.update(reduce_small(grads, chip))

    flip = lambda a: jnp.swapaxes(a, 1, 2)
    deltas, new_m, new_v = [], [], []
    for n in WEIGHT_ORDER:
        operands = (weights[n], reduced[n], args["m_" + n], args["v_" + n])
        if n == "w_in":
            operands = (flip(operands[0]), lax.optimization_barrier(flip(operands[1])), flip(operands[2]), flip(operands[3]))
            reduced[n] = flip(operands[1])
        d, nm, nv = adamw(*operands)
        if n == "w_in":
            d, nm, nv = flip(d), flip(nm), flip(nv)
        deltas.append(d)
        new_m.append(nm)
        new_v.append(nv)
    return (loss, dx[None], *[reduced[n] for n in WEIGHT_ORDER], *deltas, *new_m, *new_v)
```
